```python
import math
import jax, jax.numpy as jnp
from jax import lax
import numpy as np

D_MODEL = 1024
BATCH = 8
SEQ = 2048
DEPTH = 2

D_MIX = D_MODEL
D_GROUP = D_MIX // 4
HEAD_DIM = 64
N_HEADS_GROUP = D_GROUP // HEAD_DIM

RWKV_DECAY_LORA = 32
RWKV_AAA_LORA = 32
RWKV_MV_LORA = 32
RWKV_GATE_LORA = 64
RWKV_GN_EPS = HEAD_DIM * 1e-5
RWKV_SIZES = (D_GROUP, D_GROUP, D_GROUP, RWKV_DECAY_LORA, RWKV_AAA_LORA, RWKV_GATE_LORA)
RWKV_COLS = sum(RWKV_SIZES)

DILATED_BRANCHES = ((128, 1), (512, 4), (2048, 16))
ALIBI_SLOPES = tuple(2.0 ** (-8.0 * (h + 1) / N_HEADS_GROUP) for h in range(N_HEADS_GROUP))
ATTN_COLS = 3 * D_GROUP

SSD_STATE = 128
SSD_GROUPS = 2
SSD_CONV = 4
SSD_CHUNK = 128
SSD_XBC = D_GROUP + 2 * SSD_GROUPS * SSD_STATE
SSD_COLS = D_GROUP + SSD_XBC + N_HEADS_GROUP

HGRN_CHUNK = 16
HGRN_COLS = 4 * D_GROUP

IN_COLS = RWKV_COLS + ATTN_COLS + SSD_COLS + HGRN_COLS

D_FF = 4 * D_MODEL
ALPHA = (2.0 * DEPTH) ** 0.25
BETA = (8.0 * DEPTH) ** -0.25
LN_EPS = 1e-5
RMS_EPS = 1e-5

kernel_name = "hymba_rwkv7_dilated_ssd_hgrn2_deepnorm"


def _split(t, sizes):
    out, o = [], 0
    for s in sizes:
        out.append(t[..., o:o + s])
        o += s
    return out


def _layer_norm(x, w, b):
    x32 = x.astype(jnp.float32)
    mu = jnp.mean(x32, -1, keepdims=True)
    var = jnp.mean(jnp.square(x32 - mu), -1, keepdims=True)
    return ((x32 - mu) * lax.rsqrt(var + LN_EPS) * w + b).astype(x.dtype)


def _rms(t):
    return t * lax.rsqrt(jnp.mean(jnp.square(t), -1, keepdims=True) + RMS_EPS)


def _token_shift_lerp(f, mu):
    prev = jnp.pad(f, ((0, 0), (1, 0), (0, 0)))[:, :-1]
    return f + (prev - f) * mu


def rwkv7_time_mix(feat, w0, w2, a0, a2, g2, k_k, k_a, r_k, lnx_w, lnx_b,
                   v_first, v_feat, v0, v2):
    bsz, slen, _ = feat.shape
    H, N = N_HEADS_GROUP, HEAD_DIM
    r, k, v, fw, fa, fg = _split(feat, RWKV_SIZES)
    w_log = -jax.nn.softplus(-(w0 + jnp.tanh(fw) @ w2)) - 0.5
    decay = jnp.exp(-jnp.exp(w_log))
    a = jax.nn.sigmoid(a0 + fa @ a2)
    g = jax.nn.sigmoid(fg) @ g2
    if v_first is None:
        v_first = v
    else:
        v = v + (v_first - v) * jax.nn.sigmoid(v0 + v_feat @ v2)

    def heads(t):
        return t.reshape(bsz, slen, H, N)

    kk = heads(k * k_k)
    kk = kk / jnp.maximum(jnp.sqrt(jnp.sum(jnp.square(kk), -1, keepdims=True)), 1e-12)
    k = k * (1.0 + (a - 1.0) * k_a)

    def step(state, inp):
        r_t, w_t, k_t, v_t, kk_t, a_t = inp
        sa = jnp.einsum('bhvk,bhk->bhv', state, -kk_t)
        state = (state * w_t[:, :, None, :]
                 + sa[..., None] * (kk_t * a_t)[:, :, None, :]
                 + v_t[..., None] * k_t[:, :, None, :])
        return state, jnp.einsum('bhvk,bhk->bhv', state, r_t)

    xs = tuple(jnp.swapaxes(heads(t), 0, 1) for t in (r, decay, k, v, kk, a))
    _, y = lax.scan(step, jnp.zeros((bsz, H, N, N), jnp.float32), xs)
    y = jnp.swapaxes(y, 0, 1)
    mu = jnp.mean(y, -1, keepdims=True)
    var = jnp.mean(jnp.square(y - mu), -1, keepdims=True)
    y = ((y - mu) * lax.rsqrt(var + RWKV_GN_EPS)).reshape(bsz, slen, D_GROUP) * lnx_w + lnx_b
    bonus = jnp.sum(heads(r) * heads(k) * r_k, -1, keepdims=True) * heads(v)
    return (y + bonus.reshape(bsz, slen, D_GROUP)) * g, v_first


def _dilated_branch(q, k, v, window, dilation):
    bsz, slen, H, Dh = q.shape
    L = slen // dilation
    blk = window // dilation
    nb = -(-L // blk)
    Lp = nb * blk

    def to_sub(t):
        t = t.reshape(bsz, L, dilation, H, Dh).transpose(0, 2, 1, 3, 4)
        t = t.reshape(bsz * dilation, L, H, Dh)
        return jnp.pad(t, ((0, 0), (0, Lp - L), (0, 0), (0, 0)))

    qs, ks, vs = to_sub(q), to_sub(k), to_sub(v)
    qb = qs.reshape(-1, nb, blk, H, Dh)

    def band(t):
        tp = jnp.pad(t, ((0, 0), (blk, 0), (0, 0), (0, 0))).reshape(-1, nb + 1, blk, H, Dh)
        return jnp.concatenate([tp[:, :-1], tp[:, 1:]], axis=2)

    kb, vb = band(ks), band(vs)
    i = jnp.arange(blk)[:, None]
    j = jnp.arange(2 * blk)[None, :]
    dist = blk + i - j
    n = jnp.arange(nb)[:, None, None]
    valid = (dist >= 0) & (dist <= blk) & ((n > 0) | (j >= blk))
    slopes = jnp.asarray(ALIBI_SLOPES, jnp.float32)
    bias = -slopes[:, None, None] * (dist * dilation).astype(jnp.float32)
    s = jnp.einsum('znqhd,znkhd->znhqk', qb, kb) * (Dh ** -0.5) + bias[None, None]
    s = jnp.where(valid[None, :, None], s, -jnp.inf)
    m = jnp.max(s, -1, keepdims=True)
    p = jnp.exp(s - m)
    l = jnp.sum(p, -1, keepdims=True)
    o = jnp.einsum('znhqk,znkhd->znqhd', p, vb) / jnp.transpose(l[..., 0], (0, 1, 3, 2))[..., None]
    lse = jnp.transpose((m + jnp.log(l))[..., 0], (0, 1, 3, 2))

    def from_sub(t):
        t = t.reshape(bsz * dilation, Lp, *t.shape[3:])[:, :L]
        t = t.reshape(bsz, dilation, L, *t.shape[2:])
        t = jnp.swapaxes(t, 1, 2)
        return t.reshape(bsz, slen, *t.shape[3:])

    return from_sub(o), from_sub(lse)


def dilated_attention(q, k, v):
    outs, lses = [], []
    for window, dilation in DILATED_BRANCHES:
        o, lse = _dilated_branch(q, k, v, window, dilation)
        outs.append(o)
        lses.append(lse)
    wts = jax.nn.softmax(jnp.stack(lses), axis=0)
    return jnp.sum(jnp.stack(outs) * wts[..., None], axis=0)


def _causal_depthwise_conv(x, w, b):
    y = lax.conv_general_dilated(x, w[:, None, :], window_strides=(1,),
                                 padding=[(w.shape[0] - 1, 0)],
                                 dimension_numbers=('NWC', 'WIO', 'NWC'),
                                 feature_group_count=x.shape[-1])
    return y + b


def _segsum_exp(a):
    cs = jnp.cumsum(a, -1)
    diff = cs[..., :, None] - cs[..., None, :]
    mask = jnp.tril(jnp.ones((a.shape[-1], a.shape[-1]), bool))
    return jnp.where(mask, jnp.exp(jnp.where(mask, diff, 0.0)), 0.0)


def ssd_mixer(feat, conv_w, conv_b, dt_bias, A_log, D, norm_w):
    bsz, slen, _ = feat.shape
    H, P, G, N = N_HEADS_GROUP, HEAD_DIM, SSD_GROUPS, SSD_STATE
    z, xbc, dt = _split(feat, (D_GROUP, SSD_XBC, H))
    xbc = jax.nn.silu(_causal_depthwise_conv(xbc, conv_w.astype(jnp.float32), conv_b))
    xs, Bm, Cm = _split(xbc, (D_GROUP, G * N, G * N))
    dt = jax.nn.softplus(dt + dt_bias)
    A = -jnp.exp(A_log.astype(jnp.float32))
    Lc = SSD_CHUNK
    nc = slen // Lc
    hpg = H // G
    x = xs.reshape(bsz, nc, Lc, H, P)
    Bh = jnp.repeat(Bm.reshape(bsz, nc, Lc, G, N), hpg, axis=3)
    Ch = jnp.repeat(Cm.reshape(bsz, nc, Lc, G, N), hpg, axis=3)
    dtc = dt.reshape(bsz, nc, Lc, H)
    dA = jnp.transpose(dtc * A, (0, 3, 1, 2))
    cs = jnp.cumsum(dA, -1)
    xdt = x * dtc[..., None]
    scores = jnp.einsum('bclhn,bcshn->bhcls', Ch, Bh) * _segsum_exp(dA)
    y_diag = jnp.einsum('bhcls,bcshp->bclhp', scores, xdt)
    decay_states = jnp.exp(cs[..., -1:] - cs)
    chunk_states = jnp.einsum('bclhn,bhcl,bclhp->cbhpn', Bh, decay_states, xdt)
    chunk_decay = jnp.transpose(jnp.exp(cs[..., -1]), (2, 0, 1))

    def step(s, inp):
        st, dec = inp
        return s * dec[..., None, None] + st, s

    _, prev = lax.scan(step, jnp.zeros((bsz, H, P, N), jnp.float32), (chunk_states, chunk_decay))
    y_off = jnp.einsum('bclhn,cbhpn,bhcl->bclhp', Ch, prev, jnp.exp(cs))
    y = (y_diag + y_off).reshape(bsz, slen, H, P) + x.reshape(bsz, slen, H, P) * D[:, None]
    y = y.reshape(bsz, slen, D_GROUP) * jax.nn.silu(z)
    y = _rms(y.reshape(bsz, slen, G, D_GROUP // G)).reshape(bsz, slen, D_GROUP)
    return y * norm_w


def hgrn2_mixer(feat, lower_bound, norm_w):
    bsz, slen, _ = feat.shape
    H, K, V = N_HEADS_GROUP, HEAD_DIM, HEAD_DIM
    q, f, i, g = _split(feat, (D_GROUP,) * 4)
    forget = lower_bound + (1.0 - lower_bound) * jax.nn.sigmoid(f)
    log_f = jnp.log(forget)
    k = 1.0 - forget
    q = jax.nn.silu(q)
    C = HGRN_CHUNK
    nc = slen // C
    q, k, log_f = (t.reshape(bsz, nc, C, H, K) for t in (q, k, log_f))
    v = i.reshape(bsz, nc, C, H, V)
    b = jnp.cumsum(log_f, axis=2)
    diff = b[:, :, :, None] - b[:, :, None, :]
    causal = jnp.tril(jnp.ones((C, C), bool))[None, None, :, :, None, None]
    dec = jnp.exp(jnp.where(causal, diff, -jnp.inf))
    att = jnp.einsum('bnthk,bnshk,bntshk->bnhts', q, k, dec)
    o_intra = jnp.einsum('bnhts,bnshv->bnthv', att, v)
    kdec = k * jnp.exp(b[:, :, -1:] - b)
    U = jnp.einsum('bnshk,bnshv->nbhkv', kdec, v)
    tot = jnp.transpose(jnp.exp(b[:, :, -1]), (1, 0, 2, 3))

    def step(s, inp):
        u, d = inp
        return s * d[..., None] + u, s

    _, prev = lax.scan(step, jnp.zeros((bsz, H, K, V), jnp.float32), (U, tot))
    o_inter = jnp.einsum('bnthk,nbhkv->bnthv', q * jnp.exp(b), prev)
    o = _rms((o_intra + o_inter).reshape(bsz, slen, H, V)).reshape(bsz, slen, D_GROUP)
    return o * norm_w * jax.nn.silu(g)


def _fwd_setup_inputs(seed: int = 0) -> dict:
    key = jax.random.key(seed)
    ks = iter(jax.random.split(key, 40))
    nrm = lambda shape, scale: jax.random.normal(next(ks), shape, jnp.float32) * scale
    uni = lambda shape, lo, hi: jax.random.uniform(next(ks), shape, jnp.float32, lo, hi)
    L1 = DEPTH - 1
    H = N_HEADS_GROUP
    dt0 = jnp.exp(uni((DEPTH, H), math.log(1e-3), math.log(1e-1)))
    return {
        "x": nrm((BATCH, SEQ, D_MODEL), 1.0),
        "lower_bounds": nrm((DEPTH, D_GROUP), 0.5),
        "w_in": nrm((DEPTH, D_MODEL, IN_COLS), D_MODEL ** -0.5),
        "w_in_vres": nrm((L1, D_MODEL, RWKV_MV_LORA), D_MODEL ** -0.5),
        "mu_shift": uni((DEPTH, RWKV_COLS), 0.0, 1.0),
        "mu_vres": uni((L1, RWKV_MV_LORA), 0.0, 1.0),
        "rwkv_w0": uni((DEPTH, D_GROUP), -6.0, 1.0),
        "rwkv_w2": nrm((DEPTH, RWKV_DECAY_LORA, D_GROUP), RWKV_DECAY_LORA ** -0.5),
        "rwkv_a0": nrm((DEPTH, D_GROUP), 0.1),
        "rwkv_a2": nrm((DEPTH, RWKV_AAA_LORA, D_GROUP), RWKV_AAA_LORA ** -0.5),
        "rwkv_g2": nrm((DEPTH, RWKV_GATE_LORA, D_GROUP), RWKV_GATE_LORA ** -0.5),
        "rwkv_k_k": 0.85 + nrm((DEPTH, D_GROUP), 0.05),
        "rwkv_k_a": 1.0 + nrm((DEPTH, D_GROUP), 0.05),
        "rwkv_r_k": nrm((DEPTH, H, HEAD_DIM), 0.1),
        "rwkv_lnx_w": 1.0 + nrm((DEPTH, D_GROUP), 0.05),
        "rwkv_lnx_b": nrm((DEPTH, D_GROUP), 0.01),
        "rwkv_v0": nrm((L1, D_GROUP), 0.5),
        "rwkv_v2": nrm((L1, RWKV_MV_LORA, D_GROUP), RWKV_MV_LORA ** -0.5),
        "ssd_conv_w": nrm((DEPTH, SSD_CONV, SSD_XBC), SSD_CONV ** -0.5),
        "ssd_conv_b": nrm((DEPTH, SSD_XBC), 0.01),
        "ssd_dt_bias": dt0 + jnp.log(-jnp.expm1(-dt0)),
        "ssd_A_log": jnp.log(uni((DEPTH, H), 1.0, 16.0)),
        "ssd_D": 1.0 + nrm((DEPTH, H), 0.05),
        "ssd_norm_w": 1.0 + nrm((DEPTH, D_GROUP), 0.05),
        "hgrn_norm_w": 1.0 + nrm((DEPTH, D_GROUP), 0.05),
        "w_out": nrm((DEPTH, D_MIX, D_MODEL), BETA * D_MIX ** -0.5),
        "ln1_w": 1.0 + nrm((DEPTH, D_MODEL), 0.05),
        "ln1_b": nrm((DEPTH, D_MODEL), 0.01),
        "w_up": nrm((DEPTH, D_MODEL, D_FF), D_MODEL ** -0.5),
        "w_down": nrm((DEPTH, D_FF, D_MODEL), BETA * D_FF ** -0.5),
        "ln2_w": 1.0 + nrm((DEPTH, D_MODEL), 0.05),
        "ln2_b": nrm((DEPTH, D_MODEL), 0.01),
    }


def _fwd_reference(x, lower_bounds, w_in, w_in_vres, mu_shift, mu_vres, rwkv_w0, rwkv_w2,
              rwkv_a0, rwkv_a2, rwkv_g2, rwkv_k_k, rwkv_k_a, rwkv_r_k, rwkv_lnx_w,
              rwkv_lnx_b, rwkv_v0, rwkv_v2, ssd_conv_w, ssd_conv_b, ssd_dt_bias, ssd_A_log,
              ssd_D, ssd_norm_w, hgrn_norm_w, w_out, ln1_w, ln1_b, w_up, w_down, ln2_w, ln2_b):
    bsz, slen, _ = x.shape
    lb = jax.nn.softmax(lower_bounds.astype(jnp.float32), axis=0)
    lb = jnp.cumsum(lb, axis=0) - lb[0]
    v_first = None
    for l in range(DEPTH):
        if l == 0:
            proj = x @ w_in[l]
        else:
            proj = x @ jnp.concatenate([w_in[l], w_in_vres[l - 1]], axis=1)
        proj = proj.astype(jnp.float32)
        parts = _split(proj, (RWKV_COLS, ATTN_COLS, SSD_COLS, HGRN_COLS))
        f_rwkv, f_attn, f_ssd, f_hgrn = parts
        f_rwkv = _token_shift_lerp(f_rwkv, mu_shift[l])
        if l == 0:
            y_a, v_first = rwkv7_time_mix(f_rwkv, rwkv_w0[l], rwkv_w2[l], rwkv_a0[l], rwkv_a2[l],
                                          rwkv_g2[l], rwkv_k_k[l], rwkv_k_a[l], rwkv_r_k[l],
                                          rwkv_lnx_w[l], rwkv_lnx_b[l], None, None, None, None)
        else:
            f_vres = _token_shift_lerp(proj[..., IN_COLS:], mu_vres[l - 1])
            y_a, v_first = rwkv7_time_mix(f_rwkv, rwkv_w0[l], rwkv_w2[l], rwkv_a0[l], rwkv_a2[l],
                                          rwkv_g2[l], rwkv_k_k[l], rwkv_k_a[l], rwkv_r_k[l],
                                          rwkv_lnx_w[l], rwkv_lnx_b[l], v_first, f_vres,
                                          rwkv_v0[l - 1], rwkv_v2[l - 1])
        q, k, v = (t.reshape(bsz, slen, N_HEADS_GROUP, HEAD_DIM)
                   for t in _split(f_attn, (D_GROUP,) * 3))
        y_b = dilated_attention(q, k, v).reshape(bsz, slen, D_GROUP)
        y_c = ssd_mixer(f_ssd, ssd_conv_w[l], ssd_conv_b[l], ssd_dt_bias[l], ssd_A_log[l],
                        ssd_D[l], ssd_norm_w[l])
        y_d = hgrn2_mixer(f_hgrn, lb[l], hgrn_norm_w[l])
        mix = jnp.concatenate([y_a, y_b, y_c, y_d], axis=-1).astype(x.dtype) @ w_out[l]
        x = _layer_norm(ALPHA * x + mix, ln1_w[l], ln1_b[l])
        h = jnp.square(jax.nn.relu(x @ w_up[l]))
        x = _layer_norm(ALPHA * x + h @ w_down[l], ln2_w[l], ln2_b[l])
    return x


import jax as _jax
import jax.numpy as _jnp

TWIN_FORMAT = 'train_step'
FWD_PARAMS = ['x', 'lower_bounds', 'w_in', 'w_in_vres', 'mu_shift', 'mu_vres', 'rwkv_w0', 'rwkv_w2', 'rwkv_a0', 'rwkv_a2', 'rwkv_g2', 'rwkv_k_k', 'rwkv_k_a', 'rwkv_r_k', 'rwkv_lnx_w', 'rwkv_lnx_b', 'rwkv_v0', 'rwkv_v2', 'ssd_conv_w', 'ssd_conv_b', 'ssd_dt_bias', 'ssd_A_log', 'ssd_D', 'ssd_norm_w', 'hgrn_norm_w', 'w_out', 'ln1_w', 'ln1_b', 'w_up', 'w_down', 'ln2_w', 'ln2_b']
TWIN_WEIGHTS = ['lower_bounds', 'w_in', 'w_in_vres', 'mu_shift', 'mu_vres', 'rwkv_w0', 'rwkv_w2', 'rwkv_a0', 'rwkv_a2', 'rwkv_g2', 'rwkv_k_k', 'rwkv_k_a', 'rwkv_r_k', 'rwkv_lnx_w', 'rwkv_lnx_b', 'rwkv_v0', 'rwkv_v2', 'ssd_conv_w', 'ssd_conv_b', 'ssd_dt_bias', 'ssd_A_log', 'ssd_D', 'ssd_norm_w', 'hgrn_norm_w', 'w_out', 'ln1_w', 'ln1_b', 'w_up', 'w_down', 'ln2_w', 'ln2_b']
TWIN_DIFF_INPUT = 'x'
TWIN_INPUTS = ['x', 'lower_bounds', 'w_in', 'w_in_vres', 'mu_shift', 'mu_vres', 'rwkv_w0', 'rwkv_w2', 'rwkv_a0', 'rwkv_a2', 'rwkv_g2', 'rwkv_k_k', 'rwkv_k_a', 'rwkv_r_k', 'rwkv_lnx_w', 'rwkv_lnx_b', 'rwkv_v0', 'rwkv_v2', 'ssd_conv_w', 'ssd_conv_b', 'ssd_dt_bias', 'ssd_A_log', 'ssd_D', 'ssd_norm_w', 'hgrn_norm_w', 'w_out', 'ln1_w', 'ln1_b', 'w_up', 'w_down', 'ln2_w', 'ln2_b', 'loss_target', 'm_lower_bounds', 'm_w_in', 'm_w_in_vres', 'm_mu_shift', 'm_mu_vres', 'm_rwkv_w0', 'm_rwkv_w2', 'm_rwkv_a0', 'm_rwkv_a2', 'm_rwkv_g2', 'm_rwkv_k_k', 'm_rwkv_k_a', 'm_rwkv_r_k', 'm_rwkv_lnx_w', 'm_rwkv_lnx_b', 'm_rwkv_v0', 'm_rwkv_v2', 'm_ssd_conv_w', 'm_ssd_conv_b', 'm_ssd_dt_bias', 'm_ssd_A_log', 'm_ssd_D', 'm_ssd_norm_w', 'm_hgrn_norm_w', 'm_w_out', 'm_ln1_w', 'm_ln1_b', 'm_w_up', 'm_w_down', 'm_ln2_w', 'm_ln2_b', 'v_lower_bounds', 'v_w_in', 'v_w_in_vres', 'v_mu_shift', 'v_mu_vres', 'v_rwkv_w0', 'v_rwkv_w2', 'v_rwkv_a0', 'v_rwkv_a2', 'v_rwkv_g2', 'v_rwkv_k_k', 'v_rwkv_k_a', 'v_rwkv_r_k', 'v_rwkv_lnx_w', 'v_rwkv_lnx_b', 'v_rwkv_v0', 'v_rwkv_v2', 'v_ssd_conv_w', 'v_ssd_conv_b', 'v_ssd_dt_bias', 'v_ssd_A_log', 'v_ssd_D', 'v_ssd_norm_w', 'v_hgrn_norm_w', 'v_w_out', 'v_ln1_w', 'v_ln1_b', 'v_w_up', 'v_w_down', 'v_ln2_w', 'v_ln2_b']
TWIN_OUTPUTS = ['loss', 'grad_x', 'grad_lower_bounds', 'grad_w_in', 'grad_w_in_vres', 'grad_mu_shift', 'grad_mu_vres', 'grad_rwkv_w0', 'grad_rwkv_w2', 'grad_rwkv_a0', 'grad_rwkv_a2', 'grad_rwkv_g2', 'grad_rwkv_k_k', 'grad_rwkv_k_a', 'grad_rwkv_r_k', 'grad_rwkv_lnx_w', 'grad_rwkv_lnx_b', 'grad_rwkv_v0', 'grad_rwkv_v2', 'grad_ssd_conv_w', 'grad_ssd_conv_b', 'grad_ssd_dt_bias', 'grad_ssd_A_log', 'grad_ssd_D', 'grad_ssd_norm_w', 'grad_hgrn_norm_w', 'grad_w_out', 'grad_ln1_w', 'grad_ln1_b', 'grad_w_up', 'grad_w_down', 'grad_ln2_w', 'grad_ln2_b', 'delta_lower_bounds', 'delta_w_in', 'delta_w_in_vres', 'delta_mu_shift', 'delta_mu_vres', 'delta_rwkv_w0', 'delta_rwkv_w2', 'delta_rwkv_a0', 'delta_rwkv_a2', 'delta_rwkv_g2', 'delta_rwkv_k_k', 'delta_rwkv_k_a', 'delta_rwkv_r_k', 'delta_rwkv_lnx_w', 'delta_rwkv_lnx_b', 'delta_rwkv_v0', 'delta_rwkv_v2', 'delta_ssd_conv_w', 'delta_ssd_conv_b', 'delta_ssd_dt_bias', 'delta_ssd_A_log', 'delta_ssd_D', 'delta_ssd_norm_w', 'delta_hgrn_norm_w', 'delta_w_out', 'delta_ln1_w', 'delta_ln1_b', 'delta_w_up', 'delta_w_down', 'delta_ln2_w', 'delta_ln2_b', 'new_m_lower_bounds', 'new_m_w_in', 'new_m_w_in_vres', 'new_m_mu_shift', 'new_m_mu_vres', 'new_m_rwkv_w0', 'new_m_rwkv_w2', 'new_m_rwkv_a0', 'new_m_rwkv_a2', 'new_m_rwkv_g2', 'new_m_rwkv_k_k', 'new_m_rwkv_k_a', 'new_m_rwkv_r_k', 'new_m_rwkv_lnx_w', 'new_m_rwkv_lnx_b', 'new_m_rwkv_v0', 'new_m_rwkv_v2', 'new_m_ssd_conv_w', 'new_m_ssd_conv_b', 'new_m_ssd_dt_bias', 'new_m_ssd_A_log', 'new_m_ssd_D', 'new_m_ssd_norm_w', 'new_m_hgrn_norm_w', 'new_m_w_out', 'new_m_ln1_w', 'new_m_ln1_b', 'new_m_w_up', 'new_m_w_down', 'new_m_ln2_w', 'new_m_ln2_b', 'new_v_lower_bounds', 'new_v_w_in', 'new_v_w_in_vres', 'new_v_mu_shift', 'new_v_mu_vres', 'new_v_rwkv_w0', 'new_v_rwkv_w2', 'new_v_rwkv_a0', 'new_v_rwkv_a2', 'new_v_rwkv_g2', 'new_v_rwkv_k_k', 'new_v_rwkv_k_a', 'new_v_rwkv_r_k', 'new_v_rwkv_lnx_w', 'new_v_rwkv_lnx_b', 'new_v_rwkv_v0', 'new_v_rwkv_v2', 'new_v_ssd_conv_w', 'new_v_ssd_conv_b', 'new_v_ssd_dt_bias', 'new_v_ssd_A_log', 'new_v_ssd_D', 'new_v_ssd_norm_w', 'new_v_hgrn_norm_w', 'new_v_w_out', 'new_v_ln1_w', 'new_v_ln1_b', 'new_v_w_up', 'new_v_w_down', 'new_v_ln2_w', 'new_v_ln2_b']
TWIN_LEAF_KINDS = {'loss': 'loss', 'grad_x': 'grad_x', 'grad_lower_bounds': 'grad_w', 'grad_w_in': 'grad_w', 'grad_w_in_vres': 'grad_w', 'grad_mu_shift': 'grad_w', 'grad_mu_vres': 'grad_w', 'grad_rwkv_w0': 'grad_w', 'grad_rwkv_w2': 'grad_w', 'grad_rwkv_a0': 'grad_w', 'grad_rwkv_a2': 'grad_w', 'grad_rwkv_g2': 'grad_w', 'grad_rwkv_k_k': 'grad_w', 'grad_rwkv_k_a': 'grad_w', 'grad_rwkv_r_k': 'grad_w', 'grad_rwkv_lnx_w': 'grad_w', 'grad_rwkv_lnx_b': 'grad_w', 'grad_rwkv_v0': 'grad_w', 'grad_rwkv_v2': 'grad_w', 'grad_ssd_conv_w': 'grad_w', 'grad_ssd_conv_b': 'grad_w', 'grad_ssd_dt_bias': 'grad_w', 'grad_ssd_A_log': 'grad_w', 'grad_ssd_D': 'grad_w', 'grad_ssd_norm_w': 'grad_w', 'grad_hgrn_norm_w': 'grad_w', 'grad_w_out': 'grad_w', 'grad_ln1_w': 'grad_w', 'grad_ln1_b': 'grad_w', 'grad_w_up': 'grad_w', 'grad_w_down': 'grad_w', 'grad_ln2_w': 'grad_w', 'grad_ln2_b': 'grad_w', 'delta_lower_bounds': 'delta_w', 'delta_w_in': 'delta_w', 'delta_w_in_vres': 'delta_w', 'delta_mu_shift': 'delta_w', 'delta_mu_vres': 'delta_w', 'delta_rwkv_w0': 'delta_w', 'delta_rwkv_w2': 'delta_w', 'delta_rwkv_a0': 'delta_w', 'delta_rwkv_a2': 'delta_w', 'delta_rwkv_g2': 'delta_w', 'delta_rwkv_k_k': 'delta_w', 'delta_rwkv_k_a': 'delta_w', 'delta_rwkv_r_k': 'delta_w', 'delta_rwkv_lnx_w': 'delta_w', 'delta_rwkv_lnx_b': 'delta_w', 'delta_rwkv_v0': 'delta_w', 'delta_rwkv_v2': 'delta_w', 'delta_ssd_conv_w': 'delta_w', 'delta_ssd_conv_b': 'delta_w', 'delta_ssd_dt_bias': 'delta_w', 'delta_ssd_A_log': 'delta_w', 'delta_ssd_D': 'delta_w', 'delta_ssd_norm_w': 'delta_w', 'delta_hgrn_norm_w': 'delta_w', 'delta_w_out': 'delta_w', 'delta_ln1_w': 'delta_w', 'delta_ln1_b': 'delta_w', 'delta_w_up': 'delta_w', 'delta_w_down': 'delta_w', 'delta_ln2_w': 'delta_w', 'delta_ln2_b': 'delta_w', 'new_m_lower_bounds': 'new_m', 'new_m_w_in': 'new_m', 'new_m_w_in_vres': 'new_m', 'new_m_mu_shift': 'new_m', 'new_m_mu_vres': 'new_m', 'new_m_rwkv_w0': 'new_m', 'new_m_rwkv_w2': 'new_m', 'new_m_rwkv_a0': 'new_m', 'new_m_rwkv_a2': 'new_m', 'new_m_rwkv_g2': 'new_m', 'new_m_rwkv_k_k': 'new_m', 'new_m_rwkv_k_a': 'new_m', 'new_m_rwkv_r_k': 'new_m', 'new_m_rwkv_lnx_w': 'new_m', 'new_m_rwkv_lnx_b': 'new_m', 'new_m_rwkv_v0': 'new_m', 'new_m_rwkv_v2': 'new_m', 'new_m_ssd_conv_w': 'new_m', 'new_m_ssd_conv_b': 'new_m', 'new_m_ssd_dt_bias': 'new_m', 'new_m_ssd_A_log': 'new_m', 'new_m_ssd_D': 'new_m', 'new_m_ssd_norm_w': 'new_m', 'new_m_hgrn_norm_w': 'new_m', 'new_m_w_out': 'new_m', 'new_m_ln1_w': 'new_m', 'new_m_ln1_b': 'new_m', 'new_m_w_up': 'new_m', 'new_m_w_down': 'new_m', 'new_m_ln2_w': 'new_m', 'new_m_ln2_b': 'new_m', 'new_v_lower_bounds': 'new_v', 'new_v_w_in': 'new_v', 'new_v_w_in_vres': 'new_v', 'new_v_mu_shift': 'new_v', 'new_v_mu_vres': 'new_v', 'new_v_rwkv_w0': 'new_v', 'new_v_rwkv_w2': 'new_v', 'new_v_rwkv_a0': 'new_v', 'new_v_rwkv_a2': 'new_v', 'new_v_rwkv_g2': 'new_v', 'new_v_rwkv_k_k': 'new_v', 'new_v_rwkv_k_a': 'new_v', 'new_v_rwkv_r_k': 'new_v', 'new_v_rwkv_lnx_w': 'new_v', 'new_v_rwkv_lnx_b': 'new_v', 'new_v_rwkv_v0': 'new_v', 'new_v_rwkv_v2': 'new_v', 'new_v_ssd_conv_w': 'new_v', 'new_v_ssd_conv_b': 'new_v', 'new_v_ssd_dt_bias': 'new_v', 'new_v_ssd_A_log': 'new_v', 'new_v_ssd_D': 'new_v', 'new_v_ssd_norm_w': 'new_v', 'new_v_hgrn_norm_w': 'new_v', 'new_v_w_out': 'new_v', 'new_v_ln1_w': 'new_v', 'new_v_ln1_b': 'new_v', 'new_v_w_up': 'new_v', 'new_v_w_down': 'new_v', 'new_v_ln2_w': 'new_v', 'new_v_ln2_b': 'new_v'}


def _forward(args):
    return _fwd_reference(*[args[k] for k in FWD_PARAMS])


def _output_shape():
    out = _jax.eval_shape(lambda: _forward(_fwd_setup_inputs(0)))
    return out.shape, out.dtype

N_MICROBATCH = 1
ADAM_LR = 0.001
ADAM_B1 = 0.9
ADAM_B2 = 0.999
ADAM_EPS = 1e-08
ADAM_WD = 0.01
ADAM_STEP = 10
PER_EXAMPLE_BATCH_AXIS = {'x': 0, 'loss_target': 0}
SHARED_INPUTS = []
_WEIGHT_DTYPES = {'lower_bounds': _jnp.float32, 'w_in': _jnp.float32, 'w_in_vres': _jnp.float32, 'mu_shift': _jnp.float32, 'mu_vres': _jnp.float32, 'rwkv_w0': _jnp.float32, 'rwkv_w2': _jnp.float32, 'rwkv_a0': _jnp.float32, 'rwkv_a2': _jnp.float32, 'rwkv_g2': _jnp.float32, 'rwkv_k_k': _jnp.float32, 'rwkv_k_a': _jnp.float32, 'rwkv_r_k': _jnp.float32, 'rwkv_lnx_w': _jnp.float32, 'rwkv_lnx_b': _jnp.float32, 'rwkv_v0': _jnp.float32, 'rwkv_v2': _jnp.float32, 'ssd_conv_w': _jnp.float32, 'ssd_conv_b': _jnp.float32, 'ssd_dt_bias': _jnp.float32, 'ssd_A_log': _jnp.float32, 'ssd_D': _jnp.float32, 'ssd_norm_w': _jnp.float32, 'hgrn_norm_w': _jnp.float32, 'w_out': _jnp.float32, 'ln1_w': _jnp.float32, 'ln1_b': _jnp.float32, 'w_up': _jnp.float32, 'w_down': _jnp.float32, 'ln2_w': _jnp.float32, 'ln2_b': _jnp.float32}
MOMENT_SCALE = {'lower_bounds': 3.765999e-03, 'w_in': 2.526036e-02, 'w_in_vres': 2.103536e-02, 'mu_shift': 3.869558e-02, 'mu_vres': 3.073917e-02, 'rwkv_w0': 1.073566e-02, 'rwkv_w2': 1.540589e-03, 'rwkv_a0': 9.429296e-03, 'rwkv_a2': 8.344347e-03, 'rwkv_g2': 2.519894e-02, 'rwkv_k_k': 3.360369e-02, 'rwkv_k_a': 2.750911e-02, 'rwkv_r_k': 5.083404e-02, 'rwkv_lnx_w': 2.525514e-02, 'rwkv_lnx_b': 5.905310e-02, 'rwkv_v0': 9.451244e-03, 'rwkv_v2': 7.788008e-03, 'ssd_conv_w': 3.084019e-02, 'ssd_conv_b': 4.775271e-02, 'ssd_dt_bias': 8.534491e-02, 'ssd_A_log': 1.080804e-01, 'ssd_D': 2.213661e-01, 'ssd_norm_w': 5.244127e-02, 'hgrn_norm_w': 2.918211e-02, 'w_out': 6.465777e-02, 'ln1_w': 1.172465e+00, 'ln1_b': 3.450164e-01, 'w_up': 3.039841e-02, 'w_down': 1.435945e-01, 'ln2_w': 1.148679e+01, 'ln2_b': 2.607336e+00}


def _to_microbatches(a, axis):
    t = _jnp.moveaxis(a, axis, 0)
    t = t.reshape((N_MICROBATCH, t.shape[0] // N_MICROBATCH) + t.shape[1:])
    return _jnp.moveaxis(t, 1, axis + 1)


def setup_inputs(seed: int = 0) -> dict:
    inp = _fwd_setup_inputs(seed)
    key = _jax.random.fold_in(_jax.random.key(seed), 7919)
    shape, _ = _output_shape()
    out = dict(inp)
    out["loss_target"] = _jax.random.normal(_jax.random.fold_in(key, 0), shape, _jnp.float32)
    for i, name in enumerate(TWIN_WEIGHTS):
        w = inp[name].astype(_jnp.float32)
        if MOMENT_SCALE is None:
            s = _jnp.sqrt(_jnp.mean(_jnp.square(w)) + 1e-30)
        else:
            s = MOMENT_SCALE[name]
        km, kv = _jax.random.split(_jax.random.fold_in(key, i + 1))
        out[name] = w
        out["m_" + name] = s * _jax.random.normal(km, w.shape, _jnp.float32)
        out["v_" + name] = (s * s) * _jax.random.uniform(kv, w.shape, _jnp.float32, 0.5, 1.5)
    if N_MICROBATCH > 1:
        for name, axis in PER_EXAMPLE_BATCH_AXIS.items():
            out[name] = _to_microbatches(out[name], axis)
    return {'x': out['x'], 'lower_bounds': out['lower_bounds'], 'w_in': out['w_in'], 'w_in_vres': out['w_in_vres'], 'mu_shift': out['mu_shift'], 'mu_vres': out['mu_vres'], 'rwkv_w0': out['rwkv_w0'], 'rwkv_w2': out['rwkv_w2'], 'rwkv_a0': out['rwkv_a0'], 'rwkv_a2': out['rwkv_a2'], 'rwkv_g2': out['rwkv_g2'], 'rwkv_k_k': out['rwkv_k_k'], 'rwkv_k_a': out['rwkv_k_a'], 'rwkv_r_k': out['rwkv_r_k'], 'rwkv_lnx_w': out['rwkv_lnx_w'], 'rwkv_lnx_b': out['rwkv_lnx_b'], 'rwkv_v0': out['rwkv_v0'], 'rwkv_v2': out['rwkv_v2'], 'ssd_conv_w': out['ssd_conv_w'], 'ssd_conv_b': out['ssd_conv_b'], 'ssd_dt_bias': out['ssd_dt_bias'], 'ssd_A_log': out['ssd_A_log'], 'ssd_D': out['ssd_D'], 'ssd_norm_w': out['ssd_norm_w'], 'hgrn_norm_w': out['hgrn_norm_w'], 'w_out': out['w_out'], 'ln1_w': out['ln1_w'], 'ln1_b': out['ln1_b'], 'w_up': out['w_up'], 'w_down': out['w_down'], 'ln2_w': out['ln2_w'], 'ln2_b': out['ln2_b'], 'loss_target': out['loss_target'], 'm_lower_bounds': out['m_lower_bounds'], 'm_w_in': out['m_w_in'], 'm_w_in_vres': out['m_w_in_vres'], 'm_mu_shift': out['m_mu_shift'], 'm_mu_vres': out['m_mu_vres'], 'm_rwkv_w0': out['m_rwkv_w0'], 'm_rwkv_w2': out['m_rwkv_w2'], 'm_rwkv_a0': out['m_rwkv_a0'], 'm_rwkv_a2': out['m_rwkv_a2'], 'm_rwkv_g2': out['m_rwkv_g2'], 'm_rwkv_k_k': out['m_rwkv_k_k'], 'm_rwkv_k_a': out['m_rwkv_k_a'], 'm_rwkv_r_k': out['m_rwkv_r_k'], 'm_rwkv_lnx_w': out['m_rwkv_lnx_w'], 'm_rwkv_lnx_b': out['m_rwkv_lnx_b'], 'm_rwkv_v0': out['m_rwkv_v0'], 'm_rwkv_v2': out['m_rwkv_v2'], 'm_ssd_conv_w': out['m_ssd_conv_w'], 'm_ssd_conv_b': out['m_ssd_conv_b'], 'm_ssd_dt_bias': out['m_ssd_dt_bias'], 'm_ssd_A_log': out['m_ssd_A_log'], 'm_ssd_D': out['m_ssd_D'], 'm_ssd_norm_w': out['m_ssd_norm_w'], 'm_hgrn_norm_w': out['m_hgrn_norm_w'], 'm_w_out': out['m_w_out'], 'm_ln1_w': out['m_ln1_w'], 'm_ln1_b': out['m_ln1_b'], 'm_w_up': out['m_w_up'], 'm_w_down': out['m_w_down'], 'm_ln2_w': out['m_ln2_w'], 'm_ln2_b': out['m_ln2_b'], 'v_lower_bounds': out['v_lower_bounds'], 'v_w_in': out['v_w_in'], 'v_w_in_vres': out['v_w_in_vres'], 'v_mu_shift': out['v_mu_shift'], 'v_mu_vres': out['v_mu_vres'], 'v_rwkv_w0': out['v_rwkv_w0'], 'v_rwkv_w2': out['v_rwkv_w2'], 'v_rwkv_a0': out['v_rwkv_a0'], 'v_rwkv_a2': out['v_rwkv_a2'], 'v_rwkv_g2': out['v_rwkv_g2'], 'v_rwkv_k_k': out['v_rwkv_k_k'], 'v_rwkv_k_a': out['v_rwkv_k_a'], 'v_rwkv_r_k': out['v_rwkv_r_k'], 'v_rwkv_lnx_w': out['v_rwkv_lnx_w'], 'v_rwkv_lnx_b': out['v_rwkv_lnx_b'], 'v_rwkv_v0': out['v_rwkv_v0'], 'v_rwkv_v2': out['v_rwkv_v2'], 'v_ssd_conv_w': out['v_ssd_conv_w'], 'v_ssd_conv_b': out['v_ssd_conv_b'], 'v_ssd_dt_bias': out['v_ssd_dt_bias'], 'v_ssd_A_log': out['v_ssd_A_log'], 'v_ssd_D': out['v_ssd_D'], 'v_ssd_norm_w': out['v_ssd_norm_w'], 'v_hgrn_norm_w': out['v_hgrn_norm_w'], 'v_w_out': out['v_w_out'], 'v_ln1_w': out['v_ln1_w'], 'v_ln1_b': out['v_ln1_b'], 'v_w_up': out['v_w_up'], 'v_w_down': out['v_w_down'], 'v_ln2_w': out['v_ln2_w'], 'v_ln2_b': out['v_ln2_b']}


def _loss(weights, diff, rest, loss_target):
    with _jax.named_scope("forward"):
        args = {**rest, TWIN_DIFF_INPUT: diff, **{k: w.astype(_WEIGHT_DTYPES[k]) for k, w in weights.items()}}
        y = _forward(args)
    with _jax.named_scope("loss_head"):
        err = _jnp.square(y.astype(_jnp.float32) - loss_target)
        return 0.5 * _jnp.sum(_jnp.mean(err, axis=-1)) if err.ndim else 0.5 * err


def _adamw(w, g, m, v):
    m = ADAM_B1 * m + (1.0 - ADAM_B1) * g
    v = ADAM_B2 * v + (1.0 - ADAM_B2) * _jnp.square(g)
    m_hat = m / (1.0 - ADAM_B1 ** ADAM_STEP)
    v_hat = v / (1.0 - ADAM_B2 ** ADAM_STEP)
    delta = -ADAM_LR * (m_hat / (_jnp.sqrt(v_hat) + ADAM_EPS) + ADAM_WD * w)
    return delta, m, v


def reference(x, lower_bounds, w_in, w_in_vres, mu_shift, mu_vres, rwkv_w0, rwkv_w2, rwkv_a0, rwkv_a2, rwkv_g2, rwkv_k_k, rwkv_k_a, rwkv_r_k, rwkv_lnx_w, rwkv_lnx_b, rwkv_v0, rwkv_v2, ssd_conv_w, ssd_conv_b, ssd_dt_bias, ssd_A_log, ssd_D, ssd_norm_w, hgrn_norm_w, w_out, ln1_w, ln1_b, w_up, w_down, ln2_w, ln2_b, loss_target, m_lower_bounds, m_w_in, m_w_in_vres, m_mu_shift, m_mu_vres, m_rwkv_w0, m_rwkv_w2, m_rwkv_a0, m_rwkv_a2, m_rwkv_g2, m_rwkv_k_k, m_rwkv_k_a, m_rwkv_r_k, m_rwkv_lnx_w, m_rwkv_lnx_b, m_rwkv_v0, m_rwkv_v2, m_ssd_conv_w, m_ssd_conv_b, m_ssd_dt_bias, m_ssd_A_log, m_ssd_D, m_ssd_norm_w, m_hgrn_norm_w, m_w_out, m_ln1_w, m_ln1_b, m_w_up, m_w_down, m_ln2_w, m_ln2_b, v_lower_bounds, v_w_in, v_w_in_vres, v_mu_shift, v_mu_vres, v_rwkv_w0, v_rwkv_w2, v_rwkv_a0, v_rwkv_a2, v_rwkv_g2, v_rwkv_k_k, v_rwkv_k_a, v_rwkv_r_k, v_rwkv_lnx_w, v_rwkv_lnx_b, v_rwkv_v0, v_rwkv_v2, v_ssd_conv_w, v_ssd_conv_b, v_ssd_dt_bias, v_ssd_A_log, v_ssd_D, v_ssd_norm_w, v_hgrn_norm_w, v_w_out, v_ln1_w, v_ln1_b, v_w_up, v_w_down, v_ln2_w, v_ln2_b):
    given = dict(x=x, lower_bounds=lower_bounds, w_in=w_in, w_in_vres=w_in_vres, mu_shift=mu_shift, mu_vres=mu_vres, rwkv_w0=rwkv_w0, rwkv_w2=rwkv_w2, rwkv_a0=rwkv_a0, rwkv_a2=rwkv_a2, rwkv_g2=rwkv_g2, rwkv_k_k=rwkv_k_k, rwkv_k_a=rwkv_k_a, rwkv_r_k=rwkv_r_k, rwkv_lnx_w=rwkv_lnx_w, rwkv_lnx_b=rwkv_lnx_b, rwkv_v0=rwkv_v0, rwkv_v2=rwkv_v2, ssd_conv_w=ssd_conv_w, ssd_conv_b=ssd_conv_b, ssd_dt_bias=ssd_dt_bias, ssd_A_log=ssd_A_log, ssd_D=ssd_D, ssd_norm_w=ssd_norm_w, hgrn_norm_w=hgrn_norm_w, w_out=w_out, ln1_w=ln1_w, ln1_b=ln1_b, w_up=w_up, w_down=w_down, ln2_w=ln2_w, ln2_b=ln2_b, loss_target=loss_target, m_lower_bounds=m_lower_bounds, m_w_in=m_w_in, m_w_in_vres=m_w_in_vres, m_mu_shift=m_mu_shift, m_mu_vres=m_mu_vres, m_rwkv_w0=m_rwkv_w0, m_rwkv_w2=m_rwkv_w2, m_rwkv_a0=m_rwkv_a0, m_rwkv_a2=m_rwkv_a2, m_rwkv_g2=m_rwkv_g2, m_rwkv_k_k=m_rwkv_k_k, m_rwkv_k_a=m_rwkv_k_a, m_rwkv_r_k=m_rwkv_r_k, m_rwkv_lnx_w=m_rwkv_lnx_w, m_rwkv_lnx_b=m_rwkv_lnx_b, m_rwkv_v0=m_rwkv_v0, m_rwkv_v2=m_rwkv_v2, m_ssd_conv_w=m_ssd_conv_w, m_ssd_conv_b=m_ssd_conv_b, m_ssd_dt_bias=m_ssd_dt_bias, m_ssd_A_log=m_ssd_A_log, m_ssd_D=m_ssd_D, m_ssd_norm_w=m_ssd_norm_w, m_hgrn_norm_w=m_hgrn_norm_w, m_w_out=m_w_out, m_ln1_w=m_ln1_w, m_ln1_b=m_ln1_b, m_w_up=m_w_up, m_w_down=m_w_down, m_ln2_w=m_ln2_w, m_ln2_b=m_ln2_b, v_lower_bounds=v_lower_bounds, v_w_in=v_w_in, v_w_in_vres=v_w_in_vres, v_mu_shift=v_mu_shift, v_mu_vres=v_mu_vres, v_rwkv_w0=v_rwkv_w0, v_rwkv_w2=v_rwkv_w2, v_rwkv_a0=v_rwkv_a0, v_rwkv_a2=v_rwkv_a2, v_rwkv_g2=v_rwkv_g2, v_rwkv_k_k=v_rwkv_k_k, v_rwkv_k_a=v_rwkv_k_a, v_rwkv_r_k=v_rwkv_r_k, v_rwkv_lnx_w=v_rwkv_lnx_w, v_rwkv_lnx_b=v_rwkv_lnx_b, v_rwkv_v0=v_rwkv_v0, v_rwkv_v2=v_rwkv_v2, v_ssd_conv_w=v_ssd_conv_w, v_ssd_conv_b=v_ssd_conv_b, v_ssd_dt_bias=v_ssd_dt_bias, v_ssd_A_log=v_ssd_A_log, v_ssd_D=v_ssd_D, v_ssd_norm_w=v_ssd_norm_w, v_hgrn_norm_w=v_hgrn_norm_w, v_w_out=v_w_out, v_ln1_w=v_ln1_w, v_ln1_b=v_ln1_b, v_w_up=v_w_up, v_w_down=v_w_down, v_ln2_w=v_ln2_w, v_ln2_b=v_ln2_b)
    weights = {n: given[n] for n in TWIN_WEIGHTS}
    shared = {n: given[n] for n in SHARED_INPUTS}
    per_example = {n: given[n] for n in ['x']}
    grad_fn = _jax.value_and_grad(_loss, argnums=(0, 1))

    def one_microbatch(ex, loss_target):
        ex = dict(ex)
        diff = ex.pop(TWIN_DIFF_INPUT)
        return grad_fn(weights, diff, {**shared, **ex}, loss_target)

    if N_MICROBATCH == 1:
        loss, (grad_w, grad_x) = one_microbatch(per_example, given["loss_target"])
    else:
        def body(carry, xs):
            loss_sum, grad_sum = carry
            l_k, (gw_k, gx_k) = one_microbatch(xs[0], xs[1])
            with _jax.named_scope("update"):
                return (loss_sum + l_k, _jax.tree.map(_jnp.add, grad_sum, gw_k)), gx_k

        init = (_jnp.zeros((), _jnp.float32), _jax.tree.map(_jnp.zeros_like, weights))
        (loss, grad_w), grad_x = _jax.lax.scan(body, init, (per_example, given["loss_target"]))
    with _jax.named_scope("update"):
        delta_w, new_m, new_v = {}, {}, {}
        for n in TWIN_WEIGHTS:
            delta_w[n], new_m[n], new_v[n] = _adamw(weights[n], grad_w[n], given["m_" + n], given["v_" + n])
    return (loss, grad_x, *[grad_w[n] for n in TWIN_WEIGHTS], *[delta_w[n] for n in TWIN_WEIGHTS],
            *[new_m[n] for n in TWIN_WEIGHTS], *[new_v[n] for n in TWIN_WEIGHTS])
```

```python
import functools

import jax
import jax.numpy as jnp
from jax import lax
from jax.experimental import pallas as pl
from jax.experimental.pallas import tpu as pltpu

F32 = jnp.float32
BF16 = jnp.bfloat16
HI = lax.Precision.HIGHEST

N_DEV = 8
SEQ = 2048
D_MODEL = 1024
D_FF = 4096
DG = 256
NH = 4
HD = 64
DEPTH = 2
ALPHA = (2.0 * DEPTH) ** 0.25
LN_EPS = 1e-5
RMS_EPS = 1e-5
GN_EPS = HD * 1e-5
IN_COLS = 3716
SSD_N = 128
SSD_CHUNK = 128
HGRN_CHUNK = 16
DILATED = ((128, 1), (512, 4), (2048, 16))

ADAM_LR, ADAM_B1, ADAM_B2, ADAM_EPS, ADAM_WD, ADAM_STEP = 0.001, 0.9, 0.999, 1e-08, 0.01, 10

PW = 4096
C_R, C_K, C_V = 0, 256, 512
C_AQ, C_AK, C_AV = 768, 1024, 1280
C_Z, C_XBC = 1536, 1792
C_HQ, C_HF, C_HI, C_HG = 2560, 2816, 3072, 3328
C_LORA, C_DT, C_VRES = 3584, 3712, 3840

RB = 256
VMEM_LIMIT = 56 * 1024 * 1024
PACK_W = 1024


def _cp(sem=None):
    return pltpu.CompilerParams(dimension_semantics=sem, vmem_limit_bytes=VMEM_LIMIT)


def _sds(shape, dt=F32):
    return jax.ShapeDtypeStruct(tuple(shape), dt)


def _rows(w, cb=0, rb=RB):
    return pl.BlockSpec((rb, w), lambda i: (i, cb))


def _full(shape):
    n = len(shape)
    return pl.BlockSpec(tuple(shape), lambda *_: (0,) * n)


def _sigmoid(x):
    return 1.0 / (1.0 + jnp.exp(-x))


def _silu(x):
    return x * _sigmoid(x)


def _softplus(x):
    return jnp.maximum(x, 0.0) + jnp.log(1.0 + jnp.exp(jnp.where(x > 0, -x, x)))


def _dot(a, b):
    return jnp.dot(a, b, precision=HI, preferred_element_type=F32)


def _dot_tn(a, b):
    return lax.dot_general(a, b, (((0,), (0,)), ((), ())), precision=HI, preferred_element_type=F32)


def _dot_nt(a, b):
    return lax.dot_general(a, b, (((1,), (1,)), ((), ())), precision=HI, preferred_element_type=F32)


def _seg_ones(n, seg):
    i = jnp.arange(n)
    return (i[:, None] // seg == i[None, :] // seg).astype(F32)


def _shift_down(x, s):
    row = lax.broadcasted_iota(jnp.int32, x.shape, 0)
    return jnp.where(row < s, 0.0, pltpu.roll(x, s, 0))


def _shift_up(x, s):
    n = x.shape[0]
    row = lax.broadcasted_iota(jnp.int32, x.shape, 0)
    return jnp.where(row >= n - s, 0.0, pltpu.roll(x, n - s, 0))


@functools.partial(jax.custom_vjp, nondiff_argnums=(1,))
def _tshift(x, s):
    return _shift_down(x, s)


def _tshift_fwd(x, s):
    return _shift_down(x, s), None


def _tshift_bwd(s, _, g):
    return (_shift_up(g, s),)


_tshift.defvjp(_tshift_fwd, _tshift_bwd)


def _map_fwd(name, fn, grid, ins, in_specs, out_shapes, out_specs):
    n_in = len(ins)

    def body(*refs):
        ys = fn(*[r[...] for r in refs[:n_in]])
        for r, y in zip(refs[n_in:], ys):
            r[...] = y

    return pl.pallas_call(body, grid=grid, in_specs=in_specs, out_specs=out_specs, out_shape=out_shapes,
                          name=name, compiler_params=_cp(("parallel",)))(*ins)


def _map_bwd(name, fn, grid, ins, in_specs, cts, ct_specs, want, acc=(), gout=None):
    n_in = len(ins)
    flat_cts = [c for group in cts for c in group]
    flat_specs = [s for group in ct_specs for s in group]
    n_ct = len(flat_cts)
    gout = gout or {}
    out_shapes = [gout[i][0] if i in gout else _sds(ins[i].shape) for i in want]
    out_specs = [gout[i][1] if i in gout else in_specs[i] for i in want]

    def body(*refs):
        xs = [r[...] for r in refs[:n_in]]
        cvals = [r[...] for r in refs[n_in:n_in + n_ct]]
        gouts = refs[n_in + n_ct:]
        cs, p = [], 0
        for group in cts:
            v = cvals[p]
            for q in range(1, len(group)):
                v = v + cvals[p + q]
            cs.append(v)
            p += len(group)

        def f(*wanted):
            full = list(xs)
            for i, w in zip(want, wanted):
                full[i] = w
            return tuple(fn(*full))

        _, vjp = jax.vjp(f, *[xs[i] for i in want])
        gs = vjp(tuple(cs))
        for o, i, g in zip(gouts, want, gs):
            if i in acc:
                @pl.when(pl.program_id(0) == 0)
                def _():
                    o[...] = jnp.zeros_like(o)

                o[...] += g
            else:
                o[...] = g

    sem = ("arbitrary",) if acc else ("parallel",)
    return pl.pallas_call(body, grid=grid, in_specs=list(in_specs) + flat_specs, out_specs=out_specs,
                          out_shape=out_shapes, name=name, compiler_params=_cp(sem))(*ins, *flat_cts)


def _addn(name, *arrs):
    n, c = arrs[0].shape

    def fn(*xs):
        r = xs[0]
        for x in xs[1:]:
            r = r + x
        return (r,)

    return _map_fwd(name, fn, (n // RB,), list(arrs), [_rows(c)] * len(arrs), [_sds((n, c))], [_rows(c)])[0]


def _mm(name, a, b, mode, tm, tn, tk, add=None, add_scale=1.0):
    if mode == "nn":
        (m, k), n = a.shape, b.shape[1]
    elif mode == "nt":
        (m, k), n = a.shape, b.shape[0]
    else:
        (k, m), n = a.shape, b.shape[1]
    nk = k // tk
    dn = {"nn": (((1,), (0,)), ((), ())), "nt": (((1,), (1,)), ((), ())), "tn": (((0,), (0,)), ((), ()))}[mode]

    def body(*refs):
        if add is None:
            a_ref, b_ref, o_ref, acc = refs
        else:
            a_ref, b_ref, add_ref, o_ref, acc = refs
        kk = pl.program_id(2)

        @pl.when(kk == 0)
        def _():
            acc[...] = jnp.zeros_like(acc)

        acc[...] += lax.dot_general(a_ref[...].astype(BF16), b_ref[...].astype(BF16), dn, preferred_element_type=F32)

        @pl.when(kk == nk - 1)
        def _():
            r = acc[...]
            if add is not None:
                r = r + add_scale * add_ref[...]
            o_ref[...] = r

    a_spec = pl.BlockSpec((tk, tm), lambda i, j, q: (q, i)) if mode == "tn" else pl.BlockSpec((tm, tk), lambda i, j, q: (i, q))
    b_spec = pl.BlockSpec((tn, tk), lambda i, j, q: (j, q)) if mode == "nt" else pl.BlockSpec((tk, tn), lambda i, j, q: (q, j))
    o_spec = pl.BlockSpec((tm, tn), lambda i, j, q: (i, j))
    ins, specs = [a, b], [a_spec, b_spec]
    if add is not None:
        ins.append(add)
        specs.append(o_spec)
    return pl.pallas_call(body, grid=(m // tm, n // tn, nk), in_specs=specs, out_specs=o_spec, out_shape=_sds((m, n)),
                          scratch_shapes=[pltpu.VMEM((tm, tn), F32)], name=name,
                          compiler_params=_cp(("parallel", "parallel", "arbitrary")))(*ins)


LERP_BLOCKS = (0, 1, 2, 3, 4, 5, C_LORA // 128, C_VRES // 128)


def _lerp_colmap(j):
    r = jnp.where(j < 6, j, jnp.where(j == 6, C_LORA // 128, C_VRES // 128))
    return (0, r)


def _lerp_fn(f, mu):
    return (f + (_tshift(f, 1) - f) * mu,)


def _lerp_specs():
    return [pl.BlockSpec((SEQ, 128), _lerp_colmap), pl.BlockSpec((1, 128), lambda j: (0, j))]


def lerp_fwd(l, proj, mu):
    return _map_fwd(f"lerp_fwd{l}", _lerp_fn, (8,), [proj, mu], _lerp_specs(), [_sds((SEQ, 1024))],
                    [pl.BlockSpec((SEQ, 128), lambda j: (0, j))])[0]


def lerp_bwd(l, proj, mu, dfl):
    n_in = 2

    def body(f_ref, mu_ref, g_ref, df_ref, dmu_ref):
        _, vjp = jax.vjp(_lerp_fn, f_ref[...], mu_ref[...])
        df, dmu = vjp((g_ref[...],))
        df_ref[...] = df
        dmu_ref[...] = dmu

    cspec = pl.BlockSpec((SEQ, 128), lambda j: (0, j))
    return pl.pallas_call(body, grid=(8,), in_specs=_lerp_specs() + [cspec],
                          out_specs=[cspec, pl.BlockSpec((1, 128), lambda j: (0, j))],
                          out_shape=[_sds((SEQ, 1024)), _sds((1, 1024))], name=f"lerp_bwd{l}",
                          compiler_params=_cp(("parallel",)))(proj, mu, dfl)


def _conv_fn(x, w, b):
    y = x * w[3:4, :] + _tshift(x, 1) * w[2:3, :] + _tshift(x, 2) * w[1:2, :] + _tshift(x, 3) * w[0:1, :] + b
    return (_silu(y),)


def _conv_specs():
    return [pl.BlockSpec((SEQ, 128), lambda j: (0, C_XBC // 128 + j)), pl.BlockSpec((4, 128), lambda j: (0, j)),
            pl.BlockSpec((1, 128), lambda j: (0, j))]


def conv_fwd(l, proj, w, b):
    return _map_fwd(f"conv_fwd{l}", _conv_fn, (6,), [proj, w, b], _conv_specs(), [_sds((SEQ, 768))],
                    [pl.BlockSpec((SEQ, 128), lambda j: (0, j))])[0]


def conv_bwd(l, proj, w, b, dxc):
    def body(x_ref, w_ref, b_ref, g_ref, dx_ref, dw_ref, db_ref):
        _, vjp = jax.vjp(_conv_fn, x_ref[...], w_ref[...], b_ref[...])
        dx, dw, db = vjp((g_ref[...],))
        dx_ref[...] = dx
        dw_ref[...] = dw
        db_ref[...] = db

    cspec = pl.BlockSpec((SEQ, 128), lambda j: (0, j))
    return pl.pallas_call(body, grid=(6,), in_specs=_conv_specs() + [cspec],
                          out_specs=[cspec, pl.BlockSpec((4, 128), lambda j: (0, j)), pl.BlockSpec((1, 128), lambda j: (0, j))],
                          out_shape=[_sds((SEQ, 768)), _sds((4, 768)), _sds((1, 768))], name=f"conv_bwd{l}",
                          compiler_params=_cp(("parallel",)))(proj, w, b, dxc)


def _rwkv_pre_fn(has_vres):
    def fn(fk, fv, flora, *rest):
        if has_vres:
            fvres, vfirst, w0, w2p, a0, a2p, g2p, k_k, k_a, v0, v2p, seg = rest
        else:
            w0, w2p, a0, a2p, g2p, k_k, k_a, seg = rest
        w_log = -_softplus(-(w0 + _dot(jnp.tanh(flora), w2p))) - 0.5
        w = jnp.exp(-jnp.exp(w_log))
        a = _sigmoid(a0 + _dot(flora, a2p))
        g = _dot(_sigmoid(flora), g2p)
        if has_vres:
            v2 = fv + (vfirst - fv) * _sigmoid(v0 + _dot(fvres, v2p))
        else:
            v2 = fv * 1.0
        kk = fk * k_k
        kk = kk / jnp.maximum(jnp.sqrt(_dot(kk * kk, seg)), 1e-12)
        k2 = fk * (1.0 + (a - 1.0) * k_a)
        return w, k2, v2, -kk, kk * a, g

    return fn


def _rwkv_pre_args(fl, vfirst, p, has_vres):
    ins = [fl, fl, fl]
    specs = [_rows(256, 1), _rows(256, 2), _rows(128, 6)]
    if has_vres:
        ins += [fl, vfirst]
        specs += [_rows(128, 7), _rows(256, 2)]
    names = ["w0", "w2p", "a0", "a2p", "g2p", "k_k", "k_a"] + (["v0", "v2p"] if has_vres else []) + ["seg64"]
    for nme in names:
        ins.append(p[nme])
        specs.append(_full(p[nme].shape))
    return ins, specs, names


def rwkv_pre_fwd(l, fl, vfirst, p):
    has_vres = l > 0
    ins, specs, _ = _rwkv_pre_args(fl, vfirst, p, has_vres)
    return _map_fwd(f"rwkv_pre_fwd{l}", _rwkv_pre_fn(has_vres), (SEQ // RB,), ins, specs,
                    [_sds((SEQ, DG))] * 6, [_rows(DG)] * 6)


def rwkv_pre_bwd(l, fl, vfirst, p, cts):
    has_vres = l > 0
    ins, specs, names = _rwkv_pre_args(fl, vfirst, p, has_vres)
    n_row = 5 if has_vres else 3
    want = list(range(n_row)) + [n_row + i for i, nme in enumerate(names) if nme != "seg64"]
    acc = tuple(w for w in want if w >= n_row)
    ct_specs = [[_rows(DG)] * len(g) for g in cts]
    gout = {0: (_sds((SEQ, DG)), _rows(DG)), 1: (_sds((SEQ, DG)), _rows(DG)), 2: (_sds((SEQ, 128)), _rows(128))}
    if has_vres:
        gout[3] = (_sds((SEQ, 128)), _rows(128))
        gout[4] = (_sds((SEQ, DG)), _rows(DG))
    gs = _map_bwd(f"rwkv_pre_bwd{l}", _rwkv_pre_fn(has_vres), (SEQ // RB,), ins, specs, cts, ct_specs, want, acc, gout)
    keys = ["fk", "fv", "flora"] + (["fvres", "vfirst"] if has_vres else []) + [nme for nme in names if nme != "seg64"]
    return dict(zip(keys, gs))


def _rwkv_post_fn(y, fr, k2, v2, g, lnx_w, lnx_b, r_k, seg):
    mu = _dot(y, seg) * (1.0 / HD)
    d = y - mu
    var = _dot(d * d, seg) * (1.0 / HD)
    yn = d * lax.rsqrt(var + GN_EPS) * lnx_w + lnx_b
    bonus = _dot(fr * k2 * r_k, seg) * v2
    return ((yn + bonus) * g,)


def _rwkv_post_args(y, fl, k2, v2, g, p):
    ins = [y, fl, k2, v2, g, p["lnx_w"], p["lnx_b"], p["r_k"], p["seg64"]]
    specs = [_rows(DG), _rows(DG, 0), _rows(DG), _rows(DG), _rows(DG)] + [_full(x.shape) for x in ins[5:]]
    return ins, specs


def rwkv_post_fwd(l, y, fl, k2, v2, g, p):
    ins, specs = _rwkv_post_args(y, fl, k2, v2, g, p)
    return _map_fwd(f"rwkv_post_fwd{l}", _rwkv_post_fn, (SEQ // RB,), ins, specs, [_sds((SEQ, DG))], [_rows(DG)])[0]


def rwkv_post_bwd(l, y, fl, k2, v2, g, p, dya):
    ins, specs = _rwkv_post_args(y, fl, k2, v2, g, p)
    gs = _map_bwd(f"rwkv_post_bwd{l}", _rwkv_post_fn, (SEQ // RB,), ins, specs, [[dya]], [[_rows(DG)]],
                  want=[0, 1, 2, 3, 4, 5, 6, 7], acc=(5, 6, 7), gout={1: (_sds((SEQ, DG)), _rows(DG))})
    return dict(zip(["y", "fr", "k2", "v2", "g", "lnx_w", "lnx_b", "r_k"], gs))


SCAN_TB = 64


def _split3(x):
    hi = x.astype(BF16)
    r1 = x - hi.astype(F32)
    mid = r1.astype(BF16)
    lo = (r1 - mid.astype(F32)).astype(BF16)
    return hi, mid, lo


def _segdot(x, ones_bf16):
    hi, mid, lo = _split3(x)
    d = lambda u: jnp.dot(u, ones_bf16, preferred_element_type=F32)
    return d(hi) + d(mid) + d(lo)


def _coltile8(rows8, dmask, ones_bf16):
    x = (rows8[:, None, :] * dmask[None]).reshape(8 * HD, DG)
    return _segdot(x, ones_bf16).reshape(8, HD, DG)


def _diag_rows(tiles8, dmask):
    return jnp.sum(tiles8 * dmask[None], axis=1)


def rwkv_scan_fwd(l, fl, w, k2, v2, c, b, p):
    nblk = SEQ // SCAN_TB

    def body(r_ref, w_ref, k_ref, v_ref, c_ref, b_ref, ones_ref, dm_ref, y_ref, st_ref, s_sc):
        @pl.when(pl.program_id(0) == 0)
        def _():
            s_sc[...] = jnp.zeros_like(s_sc)

        ones = ones_ref[...]
        dmask = dm_ref[...]

        def group(gi, carry):
            t0 = pl.multiple_of(gi * 8, 8)
            sl = pl.ds(t0, 8)
            r8, w8, k8, c8, b8 = r_ref[sl, :], w_ref[sl, :], k_ref[sl, :], c_ref[sl, :], b_ref[sl, :]
            vt8 = _coltile8(v_ref[sl, :], dmask, ones)
            s = s_sc[...]
            for j in range(8):
                sa = _segdot(s * c8[j:j + 1, :], ones)
                s = s * w8[j:j + 1, :] + sa * b8[j:j + 1, :] + vt8[j] * k8[j:j + 1, :]
                st_ref[t0 + j] = s
            s_sc[...] = s
            s8 = st_ref[sl]
            yt = _segdot((s8 * r8[:, None, :]).reshape(8 * HD, DG), ones).reshape(8, HD, DG)
            y_ref[sl, :] = _diag_rows(yt, dmask)
            return carry

        lax.fori_loop(0, SCAN_TB // 8, group, 0)

    row = pl.BlockSpec((SCAN_TB, DG), lambda i: (i, 0))
    ins = [fl, w, k2, v2, c, b, p["seg64_bf16"], p["dmask"]]
    specs = [row] * 6 + [_full((DG, DG)), _full((HD, DG))]
    return pl.pallas_call(body, grid=(nblk,), in_specs=specs,
                          out_specs=[row, pl.BlockSpec((SCAN_TB, HD, DG), lambda i: (i, 0, 0))],
                          out_shape=[_sds((SEQ, DG)), _sds((SEQ, HD, DG))],
                          scratch_shapes=[pltpu.VMEM((HD, DG), F32)], name=f"rwkv_scan_fwd{l}",
                          compiler_params=_cp(("arbitrary",)))(*ins)


def rwkv_scan_bwd(l, fl, w, k2, v2, c, b, states, dy, p):
    nblk = SEQ // SCAN_TB

    def body(r_ref, w_ref, k_ref, v_ref, c_ref, b_ref, dy_ref, st_ref, sp_ref, ones_ref, dm_ref,
             dr_ref, dw_ref, dk_ref, dv_ref, dc_ref, db_ref, g_sc, prev_sc, d8_sc, dsa8_sc):
        i = pl.program_id(0)

        @pl.when(i == 0)
        def _():
            g_sc[...] = jnp.zeros_like(g_sc)

        ones = ones_ref[...]
        dmask = dm_ref[...]
        first_block = i == nblk - 1

        def group(gr, carry):
            gi = SCAN_TB // 8 - 1 - gr
            t0 = pl.multiple_of(gi * 8, 8)
            sl = pl.ds(t0, 8)
            r8, w8, k8, c8, b8 = r_ref[sl, :], w_ref[sl, :], k_ref[sl, :], c_ref[sl, :], b_ref[sl, :]
            s8 = st_ref[sl]
            @pl.when(gi > 0)
            def _():
                prev_sc[0] = st_ref[t0 - 1]

            @pl.when(gi == 0)
            def _():
                prev_sc[0] = jnp.where(first_block, 0.0, sp_ref[0])

            for j in range(1, 8):
                prev_sc[j] = s8[j - 1]
            sp8 = prev_sc[...]
            vt8 = _coltile8(v_ref[sl, :], dmask, ones)
            dyt8 = _coltile8(dy_ref[sl, :], dmask, ones)
            sa8 = _segdot((sp8 * c8[:, None, :]).reshape(8 * HD, DG), ones).reshape(8, HD, DG)
            g = g_sc[...]
            for j in range(7, -1, -1):
                g = g + dyt8[j] * r8[j:j + 1, :]
                d8_sc[j] = g
                dsa = _segdot(g * b8[j:j + 1, :], ones)
                dsa8_sc[j] = dsa
                g = g * w8[j:j + 1, :] + dsa * c8[j:j + 1, :]
            g_sc[...] = g
            d8 = d8_sc[...]
            dsa8 = dsa8_sc[...]
            dr_ref[sl, :] = jnp.sum(s8 * dyt8, axis=1)
            dk_ref[sl, :] = jnp.sum(d8 * vt8, axis=1)
            dw_ref[sl, :] = jnp.sum(sp8 * d8, axis=1)
            db_ref[sl, :] = jnp.sum(d8 * sa8, axis=1)
            dc_ref[sl, :] = jnp.sum(sp8 * dsa8, axis=1)
            dvt = _segdot((d8 * k8[:, None, :]).reshape(8 * HD, DG), ones).reshape(8, HD, DG)
            dv_ref[sl, :] = _diag_rows(dvt, dmask)
            return carry

        lax.fori_loop(0, SCAN_TB // 8, group, 0)

    row = pl.BlockSpec((SCAN_TB, DG), lambda i: (nblk - 1 - i, 0))
    st_spec = pl.BlockSpec((SCAN_TB, HD, DG), lambda i: (nblk - 1 - i, 0, 0))
    sp_spec = pl.BlockSpec((1, HD, DG), lambda i: (jnp.maximum((nblk - 1 - i) * SCAN_TB - 1, 0), 0, 0))
    ins = [fl, w, k2, v2, c, b, dy, states, states, p["seg64_bf16"], p["dmask"]]
    specs = [row] * 7 + [st_spec, sp_spec, _full((DG, DG)), _full((HD, DG))]
    tile8 = pltpu.VMEM((8, HD, DG), F32)
    return pl.pallas_call(body, grid=(nblk,), in_specs=specs, out_specs=[row] * 6, out_shape=[_sds((SEQ, DG))] * 6,
                          scratch_shapes=[pltpu.VMEM((HD, DG), F32), tile8, tile8, tile8], name=f"rwkv_scan_bwd{l}",
                          compiler_params=_cp(("arbitrary",)))(*ins)


HG_ROWS = 128


def _hgrn_chunk_fn(layer):
    def fn(hq, hf, hi, hg, sprev, lb0, lb1, norm_w, seg, bd, tri, causal, ones16):
        e0 = jnp.exp(lb0 - jnp.maximum(lb0, lb1))
        e1 = jnp.exp(lb1 - jnp.maximum(lb0, lb1))
        sm0, sm1 = e0 / (e0 + e1), e1 / (e0 + e1)
        lb = (sm0 - sm0) if layer == 0 else ((sm0 + sm1) - sm0)
        forget = lb + (1.0 - lb) * _sigmoid(hf)
        logf = jnp.log(forget)
        kk = 1.0 - forget
        q = _silu(hq)
        c = HGRN_CHUNK
        b = _dot(tri, logf)
        bl = jnp.sum(logf, axis=0, keepdims=True)
        diff = (b[:, None, :] - b[None, :, :]).reshape(c * c, DG)
        dec = jnp.exp(jnp.where(causal > 0.5, diff, -1e30))
        qrep = jnp.broadcast_to(q[:, None, :], (c, c, DG)).reshape(c * c, DG)
        ktil = jnp.broadcast_to(kk[None, :, :], (c, c, DG)).reshape(c * c, DG)
        vtil = jnp.broadcast_to(hi[None, :, :], (c, c, DG)).reshape(c * c, DG)
        att = _dot(qrep * ktil * dec, seg)
        o_intra = jnp.sum((att * vtil).reshape(c, c, DG), axis=1)
        kdec = kk * jnp.exp(bl - b)
        u = _dot_tn(kdec, hi) * bd
        tot = jnp.exp(_dot_tn(logf, ones16))
        snext = sprev * tot + u
        o = o_intra + _dot(q * jnp.exp(b), sprev)
        ms = _dot(o * o, seg) * (1.0 / HD)
        y = o * lax.rsqrt(ms + RMS_EPS) * norm_w * _silu(hg)
        return y, snext

    return fn


def _hgrn_consts(p):
    return [p["seg64"], p["seg64"], p["tri16"], p["causal16"], p["ones16"]]


def hgrn_fwd(l, proj, p):
    fn = _hgrn_chunk_fn(l)
    nch = HG_ROWS // HGRN_CHUNK

    def body(hq_ref, hf_ref, hi_ref, hg_ref, lb0_ref, lb1_ref, nw_ref, seg_ref, bd_ref, tri_ref, cau_ref, o16_ref,
             y_ref, st_ref, s_sc):
        @pl.when(pl.program_id(0) == 0)
        def _():
            s_sc[...] = jnp.zeros_like(s_sc)

        consts = (lb0_ref[...], lb1_ref[...], nw_ref[...], seg_ref[...], bd_ref[...], tri_ref[...], cau_ref[...],
                  o16_ref[...])

        def chunk(ci, carry):
            sl = pl.ds(pl.multiple_of(ci * HGRN_CHUNK, HGRN_CHUNK), HGRN_CHUNK)
            sprev = s_sc[...]
            st_ref[ci] = sprev
            y, snext = fn(hq_ref[sl, :], hf_ref[sl, :], hi_ref[sl, :], hg_ref[sl, :], sprev, *consts)
            y_ref[sl, :] = y
            s_sc[...] = snext
            return carry

        lax.fori_loop(0, nch, chunk, 0)

    rows = lambda cb: pl.BlockSpec((HG_ROWS, DG), lambda i: (i, cb))
    ins = [proj, proj, proj, proj, p["lb0"], p["lb1"], p["hgrn_norm_w"]] + _hgrn_consts(p)
    specs = [rows(C_HQ // DG), rows(C_HF // DG), rows(C_HI // DG), rows(C_HG // DG)] + [_full(x.shape) for x in ins[4:]]
    return pl.pallas_call(body, grid=(SEQ // HG_ROWS,), in_specs=specs,
                          out_specs=[rows(0), pl.BlockSpec((nch, DG, DG), lambda i: (i, 0, 0))],
                          out_shape=[_sds((SEQ, DG)), _sds((SEQ // HGRN_CHUNK, DG, DG))],
                          scratch_shapes=[pltpu.VMEM((DG, DG), F32)], name=f"hgrn_fwd{l}",
                          compiler_params=_cp(("arbitrary",)))(*ins)


def hgrn_bwd(l, proj, states, dy, p):
    fn = _hgrn_chunk_fn(l)
    nch = HG_ROWS // HGRN_CHUNK
    nblk = SEQ // HG_ROWS

    def body(hq_ref, hf_ref, hi_ref, hg_ref, st_ref, dy_ref, lb0_ref, lb1_ref, nw_ref, seg_ref, bd_ref, tri_ref,
             cau_ref, o16_ref, dp_ref, dlb0_ref, dlb1_ref, dnw_ref, ds_sc):
        @pl.when(pl.program_id(0) == 0)
        def _():
            ds_sc[...] = jnp.zeros_like(ds_sc)
            dlb0_ref[...] = jnp.zeros_like(dlb0_ref)
            dlb1_ref[...] = jnp.zeros_like(dlb1_ref)
            dnw_ref[...] = jnp.zeros_like(dnw_ref)

        consts = (seg_ref[...], bd_ref[...], tri_ref[...], cau_ref[...], o16_ref[...])

        def chunk(cr, carry):
            ci = nch - 1 - cr
            sl = pl.ds(pl.multiple_of(ci * HGRN_CHUNK, HGRN_CHUNK), HGRN_CHUNK)
            f = lambda hq, hf, hi, hg, sp, b0, b1, nw: fn(hq, hf, hi, hg, sp, b0, b1, nw, *consts)
            _, vjp = jax.vjp(f, hq_ref[sl, :], hf_ref[sl, :], hi_ref[sl, :], hg_ref[sl, :], st_ref[ci],
                             lb0_ref[...], lb1_ref[...], nw_ref[...])
            dhq, dhf, dhi, dhg, dsp, dlb0, dlb1, dnw = vjp((dy_ref[sl, :], ds_sc[...]))
            dp_ref[sl, 0:DG] = dhq
            dp_ref[sl, DG:2 * DG] = dhf
            dp_ref[sl, 2 * DG:3 * DG] = dhi
            dp_ref[sl, 3 * DG:4 * DG] = dhg
            ds_sc[...] = dsp
            dlb0_ref[...] += dlb0
            dlb1_ref[...] += dlb1
            dnw_ref[...] += dnw
            return carry

        lax.fori_loop(0, nch, chunk, 0)

    rows = lambda cb: pl.BlockSpec((HG_ROWS, DG), lambda i: (nblk - 1 - i, cb))
    ins = [proj, proj, proj, proj, states, dy, p["lb0"], p["lb1"], p["hgrn_norm_w"]] + _hgrn_consts(p)
    specs = [rows(C_HQ // DG), rows(C_HF // DG), rows(C_HI // DG), rows(C_HG // DG),
             pl.BlockSpec((nch, DG, DG), lambda i: (nblk - 1 - i, 0, 0)), rows(0)] + [_full(x.shape) for x in ins[6:]]
    return pl.pallas_call(body, grid=(nblk,), in_specs=specs,
                          out_specs=[pl.BlockSpec((HG_ROWS, 4 * DG), lambda i: (nblk - 1 - i, 0)), _full((1, DG)),
                                     _full((1, DG)), _full((1, DG))],
                          out_shape=[_sds((SEQ, 4 * DG)), _sds((1, DG)), _sds((1, DG)), _sds((1, DG))],
                          scratch_shapes=[pltpu.VMEM((DG, DG), F32)], name=f"hgrn_bwd{l}",
                          compiler_params=_cp(("arbitrary",)))(*ins)


def _ssd_chunk_fn(z, xs, bm, cm, dtr, sprev, dt_bias, a_log, d_par, norm_w, e128, tri, trit, seg128, ones128):
    lc = SSD_CHUNK
    dt = _softplus(dtr + dt_bias)
    a = -jnp.exp(a_log)
    da = dt * a * (lax.broadcasted_iota(jnp.int32, (1, 128), 1) < NH).astype(F32)
    cs = _dot(tri, da)
    cst = _dot_tn(da, trit)
    cs_b = _dot(cs, e128)
    dt_b = _dot(dt, e128)
    csl_b = _dot(jnp.sum(da, axis=0, keepdims=True), e128)
    xdt = xs * dt_b
    lane = lax.broadcasted_iota(jnp.int32, (1, DG), 1)
    rowi = lax.broadcasted_iota(jnp.int32, (lc, lc), 0)
    coli = lax.broadcasted_iota(jnp.int32, (lc, lc), 1)
    y = jnp.zeros((lc, DG), F32)
    snew = jnp.zeros((DG, SSD_N), F32)
    d_b = jnp.zeros((1, DG), F32)
    wdec = xdt * jnp.exp(csl_b - cs_b)
    for g in range(2):
        bg = bm[:, g * SSD_N:(g + 1) * SSD_N]
        cg = cm[:, g * SSD_N:(g + 1) * SSD_N]
        gmat = _dot_nt(cg, bg)
        gmask = ((lane // 128) == g).astype(F32)
        snew = snew + _dot_tn(wdec * gmask, bg)
        y = y + _dot_nt(cg, sprev) * gmask * jnp.exp(cs_b)
        for hh in range(2):
            h = 2 * g + hh
            seg = jnp.where(rowi >= coli, cs[:, h:h + 1] - cst[h:h + 1, :], -1e30)
            hmask = ((lane // HD) == h).astype(F32)
            y = y + _dot(gmat * jnp.exp(seg), xdt * hmask)
            d_b = d_b + d_par[:, h:h + 1] * hmask
    cd = jnp.exp(_dot_tn(_dot(da, e128), ones128))
    snext = sprev * cd + snew
    y = y + xs * d_b
    y = y * _silu(z)
    ms = _dot(y * y, seg128) * (1.0 / 128.0)
    return y * lax.rsqrt(ms + RMS_EPS) * norm_w, snext


def ssd_fwd(l, proj, xc, p):
    nc = SEQ // SSD_CHUNK

    def body(z_ref, xs_ref, b_ref, c_ref, dt_ref, dtb_ref, al_ref, d_ref, nw_ref, e_ref, tri_ref, trit_ref, sg_ref,
             on_ref, y_ref, st_ref, s_sc):
        @pl.when(pl.program_id(0) == 0)
        def _():
            s_sc[...] = jnp.zeros_like(s_sc)

        sprev = s_sc[...]
        st_ref[0] = sprev
        y, snext = _ssd_chunk_fn(z_ref[...], xs_ref[...], b_ref[...], c_ref[...], dt_ref[...], sprev, dtb_ref[...],
                                 al_ref[...], d_ref[...], nw_ref[...], e_ref[...], tri_ref[...], trit_ref[...],
                                 sg_ref[...], on_ref[...])
        y_ref[...] = y
        s_sc[...] = snext

    rw = lambda w, cb: pl.BlockSpec((SSD_CHUNK, w), lambda i: (i, cb))
    ins = [proj, xc, xc, xc, proj, p["dt_bias"], p["a_log"], p["ssd_d"], p["ssd_norm_w"], p["e128"], p["tri128"],
           p["tri128t"], p["seg128"], p["ones128"]]
    specs = [rw(DG, C_Z // DG), rw(DG, 0), rw(DG, 1), rw(DG, 2), rw(128, C_DT // 128)] + [_full(x.shape) for x in ins[5:]]
    return pl.pallas_call(body, grid=(nc,), in_specs=specs,
                          out_specs=[rw(DG, 0), pl.BlockSpec((1, DG, SSD_N), lambda i: (i, 0, 0))],
                          out_shape=[_sds((SEQ, DG)), _sds((nc, DG, SSD_N))],
                          scratch_shapes=[pltpu.VMEM((DG, SSD_N), F32)], name=f"ssd_fwd{l}",
                          compiler_params=_cp(("arbitrary",)))(*ins)


def ssd_bwd(l, proj, xc, states, dy, p):
    nc = SEQ // SSD_CHUNK

    def body(z_ref, xs_ref, b_ref, c_ref, dt_ref, st_ref, dy_ref, dtb_ref, al_ref, d_ref, nw_ref, e_ref, tri_ref,
             trit_ref, sg_ref, on_ref, dz_ref, dxc_ref, ddt_ref, ddtb_ref, dal_ref, dd_ref, dnw_ref, ds_sc):
        @pl.when(pl.program_id(0) == 0)
        def _():
            ds_sc[...] = jnp.zeros_like(ds_sc)
            ddtb_ref[...] = jnp.zeros_like(ddtb_ref)
            dal_ref[...] = jnp.zeros_like(dal_ref)
            dd_ref[...] = jnp.zeros_like(dd_ref)
            dnw_ref[...] = jnp.zeros_like(dnw_ref)

        consts = (e_ref[...], tri_ref[...], trit_ref[...], sg_ref[...], on_ref[...])
        f = lambda *a: _ssd_chunk_fn(*a, *consts)
        _, vjp = jax.vjp(f, z_ref[...], xs_ref[...], b_ref[...], c_ref[...], dt_ref[...], st_ref[0], dtb_ref[...],
                         al_ref[...], d_ref[...], nw_ref[...])
        dz, dxs, db, dc, ddt, dsp, ddtb, dal, dd, dnw = vjp((dy_ref[...], ds_sc[...]))
        dz_ref[...] = dz
        dxc_ref[:, 0:DG] = dxs
        dxc_ref[:, DG:2 * DG] = db
        dxc_ref[:, 2 * DG:3 * DG] = dc
        ddt_ref[...] = ddt
        ds_sc[...] = dsp
        ddtb_ref[...] += ddtb
        dal_ref[...] += dal
        dd_ref[...] += dd
        dnw_ref[...] += dnw

    rw = lambda w, cb: pl.BlockSpec((SSD_CHUNK, w), lambda i: (nc - 1 - i, cb))
    ins = [proj, xc, xc, xc, proj, states, dy, p["dt_bias"], p["a_log"], p["ssd_d"], p["ssd_norm_w"], p["e128"],
           p["tri128"], p["tri128t"], p["seg128"], p["ones128"]]
    specs = [rw(DG, C_Z // DG), rw(DG, 0), rw(DG, 1), rw(DG, 2), rw(128, C_DT // 128),
             pl.BlockSpec((1, DG, SSD_N), lambda i: (nc - 1 - i, 0, 0)), rw(DG, 0)] + [_full(x.shape) for x in ins[7:]]
    return pl.pallas_call(body, grid=(nc,), in_specs=specs,
                          out_specs=[rw(DG, 0), rw(3 * DG, 0), rw(128, 0), _full((1, 128)), _full((1, 128)), _full((1, 128)),
                                     _full((1, DG))],
                          out_shape=[_sds((SEQ, DG)), _sds((SEQ, 3 * DG)), _sds((SEQ, 128)), _sds((1, 128)), _sds((1, 128)),
                                     _sds((1, 128)), _sds((1, DG))],
                          scratch_shapes=[pltpu.VMEM((DG, SSD_N), F32)], name=f"ssd_bwd{l}",
                          compiler_params=_cp(("arbitrary",)))(*ins)


ATT_BLK = 128


def _slope(h):
    return jnp.where(h == 0, 0.25, jnp.where(h == 1, 0.0625, jnp.where(h == 2, 0.015625, 0.00390625))).astype(F32)


def _att_scores(qn, kc, kp, h, dil, has_prev):
    i = lax.broadcasted_iota(jnp.int32, (ATT_BLK, ATT_BLK), 0)
    j = lax.broadcasted_iota(jnp.int32, (ATT_BLK, ATT_BLK), 1)
    slope = _slope(h)
    scale = HD ** -0.5
    s_c = _dot_nt(qn, kc) * scale - slope * ((i - j) * dil).astype(F32)
    s_p = _dot_nt(qn, kp) * scale - slope * ((ATT_BLK + i - j) * dil).astype(F32)
    m_c = j <= i
    m_p = jnp.logical_and(j >= i, has_prev)
    return jnp.where(m_c, s_c, -1e30), jnp.where(m_p, s_p, -1e30), m_c, m_p


def attn_branch_fwd(l, bi, qs, ks, vs):
    dil, _, ln, _ = qs.shape
    nb = ln // ATT_BLK

    def body(q_ref, k_ref, v_ref, o_ref, l_ref):
        h = pl.program_id(1)

        def blk(n, carry):
            r0 = pl.multiple_of(n * ATT_BLK, ATT_BLK)
            rp = pl.multiple_of(jnp.maximum(n - 1, 0) * ATT_BLK, ATT_BLK)
            qn = q_ref[0, 0, pl.ds(r0, ATT_BLK), :]
            kc, vc = k_ref[0, 0, pl.ds(r0, ATT_BLK), :], v_ref[0, 0, pl.ds(r0, ATT_BLK), :]
            kp, vp = k_ref[0, 0, pl.ds(rp, ATT_BLK), :], v_ref[0, 0, pl.ds(rp, ATT_BLK), :]
            s_c, s_p, m_c, m_p = _att_scores(qn, kc, kp, h, dil, n > 0)
            m = jnp.maximum(jnp.max(s_c, axis=1, keepdims=True), jnp.max(s_p, axis=1, keepdims=True))
            p_c = jnp.where(m_c, jnp.exp(s_c - m), 0.0)
            p_p = jnp.where(m_p, jnp.exp(s_p - m), 0.0)
            den = jnp.sum(p_c, axis=1, keepdims=True) + jnp.sum(p_p, axis=1, keepdims=True)
            o = (_dot(p_c, vc) + _dot(p_p, vp)) / den
            o_ref[0, 0, pl.ds(r0, ATT_BLK), :] = o
            l_ref[0, 0, pl.ds(r0, ATT_BLK), :] = jnp.broadcast_to(m + jnp.log(den), (ATT_BLK, HD))
            return carry

        lax.fori_loop(0, nb, blk, 0)

    spec = pl.BlockSpec((1, 1, ln, HD), lambda z, h: (z, h, 0, 0))
    return pl.pallas_call(body, grid=(dil, NH), in_specs=[spec] * 3, out_specs=[spec] * 2,
                          out_shape=[_sds(qs.shape)] * 2, name=f"attn_fwd{l}_{bi}",
                          compiler_params=_cp(("parallel", "parallel")))(qs, ks, vs)


def attn_branch_bwd(l, bi, qs, ks, vs, dos, lses, deltas):
    dil, _, ln, _ = qs.shape
    nb = ln // ATT_BLK
    scale = HD ** -0.5

    def body(q_ref, k_ref, v_ref, do_ref, l_ref, dl_ref, dq_ref, dk_ref, dv_ref):
        h = pl.program_id(1)
        dk_ref[...] = jnp.zeros_like(dk_ref)
        dv_ref[...] = jnp.zeros_like(dv_ref)

        def blk(n, carry):
            r0 = pl.multiple_of(n * ATT_BLK, ATT_BLK)
            rp = pl.multiple_of(jnp.maximum(n - 1, 0) * ATT_BLK, ATT_BLK)
            cur, prv = pl.ds(r0, ATT_BLK), pl.ds(rp, ATT_BLK)
            qn, don = q_ref[0, 0, cur, :], do_ref[0, 0, cur, :]
            lse, dlt = l_ref[0, 0, cur, 0:1], dl_ref[0, 0, cur, 0:1]
            kc, vc, kp, vp = k_ref[0, 0, cur, :], v_ref[0, 0, cur, :], k_ref[0, 0, prv, :], v_ref[0, 0, prv, :]
            s_c, s_p, m_c, m_p = _att_scores(qn, kc, kp, h, dil, n > 0)
            p_c = jnp.where(m_c, jnp.exp(s_c - lse), 0.0)
            p_p = jnp.where(m_p, jnp.exp(s_p - lse), 0.0)
            ds_c = p_c * (_dot_nt(don, vc) - dlt)
            ds_p = p_p * (_dot_nt(don, vp) - dlt)
            dq_ref[0, 0, cur, :] = (_dot(ds_c, kc) + _dot(ds_p, kp)) * scale
            dv_ref[0, 0, prv, :] += _dot_tn(p_p, don)
            dk_ref[0, 0, prv, :] += _dot_tn(ds_p, qn) * scale
            dv_ref[0, 0, cur, :] += _dot_tn(p_c, don)
            dk_ref[0, 0, cur, :] += _dot_tn(ds_c, qn) * scale
            return carry

        lax.fori_loop(0, nb, blk, 0)

    spec = pl.BlockSpec((1, 1, ln, HD), lambda z, h: (z, h, 0, 0))
    return pl.pallas_call(body, grid=(dil, NH), in_specs=[spec] * 6, out_specs=[spec] * 3,
                          out_shape=[_sds(qs.shape)] * 3, name=f"attn_bwd{l}_{bi}",
                          compiler_params=_cp(("parallel", "parallel")))(qs, ks, vs, dos, lses, deltas)


def _attn_merge_fn(o1, o2, o3, l1, l2, l3):
    m = jnp.maximum(jnp.maximum(l1, l2), l3)
    w1, w2, w3 = jnp.exp(l1 - m), jnp.exp(l2 - m), jnp.exp(l3 - m)
    den = w1 + w2 + w3
    return (w1 * o1 + w2 * o2 + w3 * o3) / den, m + jnp.log(den)


def attn_merge(l, os_, ls_):
    ins = list(os_) + list(ls_)
    return _map_fwd(f"attn_merge{l}", _attn_merge_fn, (SEQ // RB,), ins, [_rows(DG)] * 6, [_sds((SEQ, DG))] * 2,
                    [_rows(DG)] * 2)


def attn_delta(l, dyb, yb, seg):
    fn = lambda d, y, s: (_dot(d * y, s),)
    return _map_fwd(f"attn_delta{l}", fn, (SEQ // RB,), [dyb, yb, seg], [_rows(DG), _rows(DG), _full((DG, DG))],
                    [_sds((SEQ, DG))], [_rows(DG)])[0]


def _to_sub(t, dil):
    return t.reshape(SEQ // dil, dil, NH, HD).transpose(1, 2, 0, 3)


def _from_sub(t):
    dil, _, ln, _ = t.shape
    return t.transpose(2, 0, 1, 3).reshape(SEQ, DG)


def _ln_fn(x, mix, w, b):
    h = ALPHA * x + mix
    mu = jnp.mean(h, axis=-1, keepdims=True)
    d = h - mu
    var = jnp.mean(d * d, axis=-1, keepdims=True)
    return (d * lax.rsqrt(var + LN_EPS) * w + b,)


def ln_fwd(name, x, mix, w, b):
    specs = [_rows(D_MODEL), _rows(D_MODEL), _full((1, D_MODEL)), _full((1, D_MODEL))]
    return _map_fwd(name, _ln_fn, (SEQ // RB,), [x, mix, w, b], specs, [_sds((SEQ, D_MODEL))], [_rows(D_MODEL)])[0]


def ln_bwd(name, x, mix, w, b, dy):
    specs = [_rows(D_MODEL), _rows(D_MODEL), _full((1, D_MODEL)), _full((1, D_MODEL))]
    return _map_bwd(name, _ln_fn, (SEQ // RB,), [x, mix, w, b], specs, [[dy]], [[_rows(D_MODEL)]], want=[1, 2, 3],
                    acc=(2, 3))


def _relu2_fn(u):
    r = jnp.maximum(u, 0.0)
    return (r * r,)


def relu2_fwd(name, u):
    return _map_fwd(name, _relu2_fn, (SEQ // RB,), [u], [_rows(D_FF)], [_sds((SEQ, D_FF))], [_rows(D_FF)])[0]


def relu2_bwd(name, u, dh):
    fn = lambda uu, g: (g * 2.0 * jnp.maximum(uu, 0.0),)
    return _map_fwd(name, fn, (SEQ // RB,), [u, dh], [_rows(D_FF)] * 2, [_sds((SEQ, D_FF))], [_rows(D_FF)])[0]


def loss_call(y, tgt):
    def fn(yy, tt):
        e = yy - tt
        part = 0.5 * jnp.sum(jnp.sum(e * e, axis=-1, keepdims=True) * (1.0 / D_MODEL), axis=0, keepdims=True)
        return e * (1.0 / D_MODEL), jnp.broadcast_to(part, (8, 128))

    return _map_fwd("loss", fn, (SEQ // RB,), [y, tgt], [_rows(D_MODEL)] * 2,
                    [_sds((SEQ, D_MODEL)), _sds((SEQ // RB * 8, 128))],
                    [_rows(D_MODEL), pl.BlockSpec((8, 128), lambda i: (i, 0))])


def layer_fwd(l, x, vfirst, wts, p):
    sv = {"x": x}
    proj = _mm(f"mm_in{l}", x, wts["w_in"], "nn", 512, 1024, 1024)
    fl = lerp_fwd(l, proj, p["mu"])
    xc = conv_fwd(l, proj, p["conv_w"], p["conv_b"])
    w, k2, v2, c, b, g = rwkv_pre_fwd(l, fl, vfirst, p)
    y_scan, states = rwkv_scan_fwd(l, fl, w, k2, v2, c, b, p)
    ya = rwkv_post_fwd(l, y_scan, fl, k2, v2, g, p)
    q_a, k_a, v_a = proj[:, C_AQ:C_AQ + DG], proj[:, C_AK:C_AK + DG], proj[:, C_AV:C_AV + DG]
    subs, outs, lses = [], [], []
    for bi, (win, dil) in enumerate(DILATED):
        qs, ks, vs = _to_sub(q_a, dil), _to_sub(k_a, dil), _to_sub(v_a, dil)
        o, lse = attn_branch_fwd(l, bi, qs, ks, vs)
        subs.append((qs, ks, vs))
        outs.append(_from_sub(o))
        lses.append(_from_sub(lse))
    yb, lse_all = attn_merge(l, outs, lses)
    yc, ssd_states = ssd_fwd(l, proj, xc, p)
    yd, hg_states = hgrn_fwd(l, proj, p)
    ycat = jnp.concatenate([ya, yb, yc, yd], axis=1)
    mix = _mm(f"mm_out{l}", ycat, wts["w_out"], "nn", 512, 1024, 1024)
    x1 = ln_fwd(f"ln1_fwd{l}", x, mix, p["ln1_w"], p["ln1_b"])
    u = _mm(f"mm_up{l}", x1, wts["w_up"], "nn", 512, 1024, 1024)
    hh = relu2_fwd(f"relu2_fwd{l}", u)
    m2 = _mm(f"mm_down{l}", hh, wts["w_down"], "nn", 512, 1024, 1024)
    x2 = ln_fwd(f"ln2_fwd{l}", x1, m2, p["ln2_w"], p["ln2_b"])
    sv.update(proj=proj, fl=fl, xc=xc, w=w, k2=k2, v2=v2, c=c, b=b, g=g, y_scan=y_scan, states=states, subs=subs,
              yb=yb, lse_all=lse_all, ssd_states=ssd_states, hg_states=hg_states, ycat=ycat, mix=mix, x1=x1, u=u, hh=hh,
              m2=m2, vfirst=vfirst)
    return x2, sv


def layer_bwd(l, dx2, dvfirst_next, sv, wts, p):
    gr = {}
    x, x1, proj, fl = sv["x"], sv["x1"], sv["proj"], sv["fl"]
    dres2, gr["ln2_w"], gr["ln2_b"] = ln_bwd(f"ln2_bwd{l}", x1, sv["m2"], p["ln2_w"], p["ln2_b"], dx2)
    dh = _mm(f"mm_down_dx{l}", dres2, wts["w_down"], "nt", 512, 1024, 1024)
    gr["w_down"] = _mm(f"mm_down_dw{l}", sv["hh"], dres2, "tn", 512, 1024, 512)
    du = relu2_bwd(f"relu2_bwd{l}", sv["u"], dh)
    dx1 = _mm(f"mm_up_dx{l}", du, wts["w_up"], "nt", 512, 1024, 1024, add=dres2, add_scale=ALPHA)
    gr["w_up"] = _mm(f"mm_up_dw{l}", x1, du, "tn", 512, 1024, 512)
    dres1, gr["ln1_w"], gr["ln1_b"] = ln_bwd(f"ln1_bwd{l}", x, sv["mix"], p["ln1_w"], p["ln1_b"], dx1)
    dycat = _mm(f"mm_out_dx{l}", dres1, wts["w_out"], "nt", 512, 1024, 1024)
    gr["w_out"] = _mm(f"mm_out_dw{l}", sv["ycat"], dres1, "tn", 512, 1024, 512)
    dya, dyb, dyc, dyd = (dycat[:, i * DG:(i + 1) * DG] for i in range(4))
    dhg4, gr["lb0"], gr["lb1"], gr["hgrn_norm_w"] = hgrn_bwd(l, proj, sv["hg_states"], dyd, p)
    dz, dxc, ddt, gr["dt_bias"], gr["a_log"], gr["ssd_d"], gr["ssd_norm_w"] = ssd_bwd(l, proj, sv["xc"], sv["ssd_states"], dyc, p)
    dxbc, gr["conv_w"], gr["conv_b"] = conv_bwd(l, proj, p["conv_w"], p["conv_b"], dxc)
    delta = attn_delta(l, dyb, sv["yb"], p["seg64"])
    dqs, dks, dvs = [], [], []
    for bi, (win, dil) in enumerate(DILATED):
        qs, ks, vs = sv["subs"][bi]
        dq, dk, dv = attn_branch_bwd(l, bi, qs, ks, vs, _to_sub(dyb, dil), _to_sub(sv["lse_all"], dil), _to_sub(delta, dil))
        dqs.append(_from_sub(dq))
        dks.append(_from_sub(dk))
        dvs.append(_from_sub(dv))
    dq_a, dk_a, dv_a = _addn(f"attn_dq{l}", *dqs), _addn(f"attn_dk{l}", *dks), _addn(f"attn_dv{l}", *dvs)
    pg = rwkv_post_bwd(l, sv["y_scan"], fl, sv["k2"], sv["v2"], sv["g"], p, dya)
    gr["lnx_w"], gr["lnx_b"], gr["r_k"] = pg["lnx_w"], pg["lnx_b"], pg["r_k"]
    dr, dw, dk, dv, dc, db = rwkv_scan_bwd(l, fl, sv["w"], sv["k2"], sv["v2"], sv["c"], sv["b"], sv["states"], pg["y"], p)
    v2_cts = [dv, pg["v2"]] + ([dvfirst_next] if dvfirst_next is not None else [])
    qg = rwkv_pre_bwd(l, fl, sv["vfirst"], p, [[dw], [dk, pg["k2"]], v2_cts, [dc], [db], [pg["g"]]])
    for nme in ("w0", "w2p", "a0", "a2p", "g2p", "k_k", "k_a", "v0", "v2p"):
        if nme in qg:
            gr[nme] = qg[nme]
    dfr = _addn(f"rwkv_dr{l}", dr, pg["fr"])
    dvres = qg["fvres"] if l > 0 else jnp.zeros((SEQ, 128), F32)
    dfl_out = jnp.concatenate([dfr, qg["fk"], qg["fv"], qg["flora"], dvres], axis=1)
    dfl_in, gr["mu"] = lerp_bwd(l, proj, p["mu"], dfl_out)
    dproj = jnp.concatenate([dfl_in[:, 0:768], dq_a, dk_a, dv_a, dz, dxbc, dhg4, dfl_in[:, 768:896], ddt,
                             dfl_in[:, 896:1024], jnp.zeros((SEQ, 128), F32)], axis=1)
    dx = _mm(f"mm_in_dx{l}", dproj, wts["w_in"], "nt", 512, 1024, 1024, add=dres1, add_scale=ALPHA)
    gr["w_in"] = _mm(f"mm_in_dw{l}", x, dproj, "tn", 512, 1024, 512)
    return dx, (qg["vfirst"] if l > 0 else None), gr


def _w_in_pad(w_in_l, w_vres):
    rows = w_in_l.shape[0]
    z = lambda n: jnp.zeros((rows, n), w_in_l.dtype)
    vres = z(128) if w_vres is None else jnp.concatenate([w_vres, z(96)], axis=1)
    return jnp.concatenate([w_in_l[:, 0:768], w_in_l[:, 896:1664], w_in_l[:, 1664:1920], w_in_l[:, 1920:2688],
                            w_in_l[:, 2692:3716], w_in_l[:, 768:896], w_in_l[:, 2688:2692], z(124), vres, z(128)], axis=1)


def _w_in_unpad(g):
    g_in = jnp.concatenate([g[:, 0:768], g[:, C_LORA:C_LORA + 128], g[:, 768:1536], g[:, C_Z:C_Z + 256],
                            g[:, C_XBC:C_XBC + 768], g[:, C_DT:C_DT + 4], g[:, C_HQ:C_HQ + 1024]], axis=1)
    return g_in, g[:, C_VRES:C_VRES + 32]


def _consts():
    i16 = jnp.arange(HGRN_CHUNK)
    pair = jnp.arange(HGRN_CHUNK * HGRN_CHUNK)
    i128 = jnp.arange(128)
    seg64 = _seg_ones(DG, HD)
    tri128 = (i128[:, None] >= i128[None, :]).astype(F32)
    return dict(
        seg64=seg64, seg64_bf16=seg64.astype(BF16),
        dmask=(jnp.arange(HD)[:, None] == (jnp.arange(DG)[None, :] % HD)).astype(F32),
        tri16=(i16[:, None] >= i16[None, :]).astype(F32),
        causal16=jnp.broadcast_to(((pair // HGRN_CHUNK) >= (pair % HGRN_CHUNK)).astype(F32)[:, None], (256, DG)),
        ones16=jnp.ones((HGRN_CHUNK, DG), F32),
        e128=((i128[:, None] == (jnp.arange(DG)[None, :] // HD)) & (i128[:, None] < NH)).astype(F32),
        tri128=tri128, tri128t=tri128.T, seg128=_seg_ones(DG, 128), ones128=jnp.ones((128, 128), F32))


def _pad_lanes(v, n):
    return jnp.concatenate([v, jnp.zeros((n - v.shape[0],), v.dtype)])[None, :]


def _layer_params(l, raw, consts):
    p = dict(consts)
    row = lambda name: raw[name][l][None, :]
    z = lambda r: jnp.zeros((r, DG), F32)
    mu_vres = raw["mu_vres"][l - 1] if l > 0 else jnp.zeros((32,), F32)
    p["mu"] = jnp.concatenate([raw["mu_shift"][l], mu_vres, jnp.zeros((96,), F32)])[None, :]
    p["conv_w"], p["conv_b"] = raw["ssd_conv_w"][l], row("ssd_conv_b")
    p["w0"], p["a0"], p["k_k"], p["k_a"] = row("rwkv_w0"), row("rwkv_a0"), row("rwkv_k_k"), row("rwkv_k_a")
    p["lnx_w"], p["lnx_b"] = row("rwkv_lnx_w"), row("rwkv_lnx_b")
    p["r_k"] = raw["rwkv_r_k"][l].reshape(1, DG)
    p["w2p"] = jnp.concatenate([raw["rwkv_w2"][l], z(96)], axis=0)
    p["a2p"] = jnp.concatenate([z(32), raw["rwkv_a2"][l], z(64)], axis=0)
    p["g2p"] = jnp.concatenate([z(64), raw["rwkv_g2"][l]], axis=0)
    if l > 0:
        p["v0"] = raw["rwkv_v0"][l - 1][None, :]
        p["v2p"] = jnp.concatenate([raw["rwkv_v2"][l - 1], z(96)], axis=0)
    p["lb0"], p["lb1"] = raw["lower_bounds"][0:1], raw["lower_bounds"][1:2]
    p["hgrn_norm_w"], p["ssd_norm_w"] = row("hgrn_norm_w"), row("ssd_norm_w")
    p["dt_bias"], p["a_log"], p["ssd_d"] = (_pad_lanes(raw[n][l], 128) for n in ("ssd_dt_bias", "ssd_A_log", "ssd_D"))
    for n in ("ln1_w", "ln1_b", "ln2_w", "ln2_b"):
        p[n] = row(n)
    return p


def _natural_grads(g0, g1):
    gs = (g0, g1)
    st = lambda key, f=lambda a: a[0]: jnp.stack([f(g[key]) for g in gs])
    out = {}
    out["lower_bounds"] = jnp.concatenate([g0["lb0"] + g1["lb0"], g0["lb1"] + g1["lb1"]], axis=0)
    out["mu_shift"] = st("mu", lambda a: a[0, :896])
    out["mu_vres"] = g1["mu"][:, 896:928]
    out["rwkv_w0"], out["rwkv_a0"], out["rwkv_k_k"], out["rwkv_k_a"] = st("w0"), st("a0"), st("k_k"), st("k_a")
    out["rwkv_w2"] = st("w2p", lambda a: a[0:32])
    out["rwkv_a2"] = st("a2p", lambda a: a[32:64])
    out["rwkv_g2"] = st("g2p", lambda a: a[64:128])
    out["rwkv_r_k"] = st("r_k", lambda a: a.reshape(NH, HD))
    out["rwkv_lnx_w"], out["rwkv_lnx_b"] = st("lnx_w"), st("lnx_b")
    out["rwkv_v0"] = g1["v0"]
    out["rwkv_v2"] = g1["v2p"][None, 0:32]
    out["ssd_conv_w"] = st("conv_w", lambda a: a)
    out["ssd_conv_b"] = st("conv_b")
    out["ssd_dt_bias"], out["ssd_A_log"], out["ssd_D"] = (st(k, lambda a: a[0, :NH]) for k in ("dt_bias", "a_log", "ssd_d"))
    out["ssd_norm_w"], out["hgrn_norm_w"] = st("ssd_norm_w"), st("hgrn_norm_w")
    for n in ("ln1_w", "ln1_b", "ln2_w", "ln2_b"):
        out[n] = st(n)
    return out


MESH_T = pl.DeviceIdType.MESH
ANY = pl.BlockSpec(memory_space=pl.ANY)


def _dev_index(px, py, pc):
    return 4 * px + 2 * py + pc


def all_gather(arrs):
    n = len(arrs)

    def body(*refs):
        ins, outs = refs[:n], refs[n:2 * n]
        send_sems, recv_sems, local_sems = refs[2 * n:]
        x, y, c = lax.axis_index("x"), lax.axis_index("y"), lax.axis_index("c")
        me, sibling = (x, y, c), (x, y, 1 - c)
        chips = [(1 - x, y), (x, 1 - y), (1 - x, 1 - y)]

        def copy(a, k, block, to, src=None):
            slot = outs[a].at[_dev_index(*block)]
            return pltpu.make_async_remote_copy(src_ref=slot if src is None else src, dst_ref=slot,
                                                send_sem=send_sems.at[a, k], recv_sem=recv_sems.at[a, k],
                                                device_id=to, device_id_type=MESH_T)

        mine = [pltpu.make_async_copy(ins[a], outs[a].at[_dev_index(*me)], local_sems.at[a]) for a in range(n)]
        for cp in mine:
            cp.start()
        first = []
        for a in range(n):
            first.append(copy(a, 0, me, sibling, src=ins[a]))
            first += [copy(a, 1 + j, me, (*chip, c), src=ins[a]) for j, chip in enumerate(chips)]
        for cp in first:
            cp.start()
        passed = []
        for j, chip in enumerate(chips):
            for a in range(n):
                copy(a, 1 + j, (*chip, c), me).wait_recv()
                fwd = copy(a, 4 + j, (*chip, c), sibling)
                fwd.start()
                passed.append(fwd)
        for a in range(n):
            copy(a, 0, sibling, me).wait_recv()
            for j, chip in enumerate(chips):
                copy(a, 4 + j, (*chip, 1 - c), me).wait_recv()
        for cp in first + passed:
            cp.wait_send()
        for cp in mine:
            cp.wait()

    return pl.pallas_call(
        body, in_specs=[ANY] * n, out_specs=[ANY] * n,
        out_shape=[_sds((N_DEV,) + a.shape, a.dtype) for a in arrs],
        scratch_shapes=[pltpu.SemaphoreType.DMA((n, 7)), pltpu.SemaphoreType.DMA((n, 7)), pltpu.SemaphoreType.DMA((n,))],
        name="all_gather")(*arrs)


def grad_exchange(send):
    def body(send_ref, recv_ref, send_sems, recv_sems, local_sem):
        x, y, c = lax.axis_index("x"), lax.axis_index("y"), lax.axis_index("c")
        me = _dev_index(x, y, c)
        mine = pltpu.make_async_copy(send_ref.at[me], recv_ref.at[me], local_sem)
        mine.start()
        rels = [(rx, ry, rc) for rx in (0, 1) for ry in (0, 1) for rc in (0, 1)][1:]
        peers = [(jnp.where(rx, 1 - x, x), jnp.where(ry, 1 - y, y), jnp.where(rc, 1 - c, c)) for rx, ry, rc in rels]

        def copy(k, peer):
            return pltpu.make_async_remote_copy(src_ref=send_ref.at[_dev_index(*peer)], dst_ref=recv_ref.at[me],
                                                send_sem=send_sems.at[k], recv_sem=recv_sems.at[k],
                                                device_id=peer, device_id_type=MESH_T)

        cps = [copy(k, peer) for k, peer in enumerate(peers)]
        for cp in cps:
            cp.start()
        for k, peer in enumerate(peers):
            pltpu.make_async_remote_copy(src_ref=send_ref.at[me], dst_ref=recv_ref.at[_dev_index(*peer)],
                                         send_sem=send_sems.at[k], recv_sem=recv_sems.at[k],
                                         device_id=peer, device_id_type=MESH_T).wait_recv()
        for cp in cps:
            cp.wait_send()
        mine.wait()

    return pl.pallas_call(
        body, in_specs=[ANY], out_specs=ANY, out_shape=_sds(send.shape, send.dtype),
        scratch_shapes=[pltpu.SemaphoreType.DMA((7,)), pltpu.SemaphoreType.DMA((7,)), pltpu.SemaphoreType.DMA],
        name="grad_exchange")(send)


ADAM_ROWS = 256


def adamw(parts, w, m, v):
    r = w.shape[0]
    c1 = 1.0 - ADAM_B1 ** ADAM_STEP
    c2 = 1.0 - ADAM_B2 ** ADAM_STEP

    def body(p_ref, w_ref, m_ref, v_ref, g_ref, d_ref, nm_ref, nv_ref):
        g = p_ref[0]
        for q in range(1, N_DEV):
            g = g + p_ref[q]
        nm = ADAM_B1 * m_ref[...] + (1.0 - ADAM_B1) * g
        nv = ADAM_B2 * v_ref[...] + (1.0 - ADAM_B2) * (g * g)
        g_ref[...] = g
        nm_ref[...] = nm
        nv_ref[...] = nv
        d_ref[...] = -ADAM_LR * ((nm / c1) / (jnp.sqrt(nv / c2) + ADAM_EPS) + ADAM_WD * w_ref[...])

    blk = pl.BlockSpec((ADAM_ROWS, PACK_W), lambda i: (i, 0))
    return pl.pallas_call(body, grid=(r // ADAM_ROWS,),
                          in_specs=[pl.BlockSpec((N_DEV, ADAM_ROWS, PACK_W), lambda i: (0, i, 0)), blk, blk, blk],
                          out_specs=[blk] * 4, out_shape=[_sds((r, PACK_W))] * 4, name="adamw",
                          compiler_params=_cp(("parallel",)))(parts, w, m, v)


BIG_ROWS = 1024 + 256 + 1024 + 1024
SMS_ROWS = 16
REP_ROWS = 24
PACK_ROWS = 3584
SMALL_SHARDED = (("rwkv_w2", (2, 32, 32)), ("rwkv_a2", (2, 32, 32)), ("rwkv_g2", (2, 64, 32)), ("rwkv_v2", (1, 32, 32)),
                 ("ssd_conv_w", (2, 4, 96)))
REPLICATED = (("lower_bounds", (2, 256)), ("mu_shift", (2, 896)), ("mu_vres", (1, 32)), ("rwkv_w0", (2, 256)),
              ("rwkv_a0", (2, 256)), ("rwkv_k_k", (2, 256)), ("rwkv_k_a", (2, 256)), ("rwkv_r_k", (2, 4, 64)),
              ("rwkv_lnx_w", (2, 256)), ("rwkv_lnx_b", (2, 256)), ("rwkv_v0", (1, 256)), ("ssd_conv_b", (2, 768)),
              ("ssd_dt_bias", (2, 4)), ("ssd_A_log", (2, 4)), ("ssd_D", (2, 4)), ("ssd_norm_w", (2, 256)),
              ("hgrn_norm_w", (2, 256)), ("ln1_w", (2, 1024)), ("ln1_b", (2, 1024)), ("ln2_w", (2, 1024)),
              ("ln2_b", (2, 1024)))


def _flat_rows(parts, rows):
    flat = jnp.concatenate([a.reshape(-1) for a in parts])
    return jnp.concatenate([flat, jnp.zeros((rows * PACK_W - flat.shape[0],), flat.dtype)]).reshape(rows, PACK_W)


def _pack_local(d):
    w_in = jnp.stack([_w_in_pad(d["w_in"][0], None), _w_in_pad(d["w_in"][1], d["w_in_vres"][0])])
    return jnp.concatenate([
        w_in.reshape(1024, PACK_W), d["w_out"].reshape(256, PACK_W), d["w_up"].reshape(1024, PACK_W),
        d["w_down"].reshape(1024, PACK_W), _flat_rows([d[n] for n, _ in SMALL_SHARDED], SMS_ROWS),
        _flat_rows([d[n] for n, _ in REPLICATED], REP_ROWS),
        jnp.zeros((PACK_ROWS - BIG_ROWS - SMS_ROWS - REP_ROWS, PACK_W), F32)], axis=0)


def _unflat(rows2d, table):
    flat, out, o = rows2d.reshape(-1), {}, 0
    for name, shape in table:
        n = 1
        for s in shape:
            n *= s
        out[name] = flat[o:o + n].reshape(shape)
        o += n
    return out


def _unpack_local(pk):
    d = {}
    w_in = pk[0:1024].reshape(2, 128, PW)
    g0, _ = _w_in_unpad(w_in[0])
    g1, gv = _w_in_unpad(w_in[1])
    d["w_in"], d["w_in_vres"] = jnp.stack([g0, g1]), gv[None]
    d["w_out"] = pk[1024:1280].reshape(2, 128, 1024)
    d["w_up"] = pk[1280:2304].reshape(2, 1024, 512)
    d["w_down"] = pk[2304:3328].reshape(2, 512, 1024)
    d.update(_unflat(pk[BIG_ROWS:BIG_ROWS + SMS_ROWS], SMALL_SHARDED))
    d.update(_unflat(pk[BIG_ROWS + SMS_ROWS:BIG_ROWS + SMS_ROWS + REP_ROWS], REPLICATED))
    return d


def _gathered_weights(gb, gs):
    w_in = gb[:, 0:1024].reshape(N_DEV, 2, 128, PW)
    w_out = gb[:, 1024:1280].reshape(N_DEV, 2, 128, 1024)
    w_up = gb[:, 1280:2304].reshape(N_DEV, 2, 1024, 512)
    w_down = gb[:, 2304:3328].reshape(N_DEV, 2, 512, 1024)
    wts = [dict(w_in=w_in[:, l].reshape(1024, PW), w_out=w_out[:, l].reshape(1024, 1024),
                w_up=w_up[:, l].transpose(1, 0, 2).reshape(1024, D_FF), w_down=w_down[:, l].reshape(D_FF, 1024))
           for l in range(DEPTH)]
    small, flat, o = {}, gs.reshape(N_DEV, -1), 0
    for name, shape in SMALL_SHARDED:
        n = shape[0] * shape[1] * shape[2]
        blk = flat[:, o:o + n].reshape((N_DEV,) + shape)
        small[name] = blk.transpose(1, 2, 0, 3).reshape(shape[0], shape[1], N_DEV * shape[2])
        o += n
    return wts, small


def _pack_send(big, small_grads):
    w_in = jnp.stack([g["w_in"].reshape(N_DEV, 128, PW) for g in big], axis=1).reshape(N_DEV, 1024, PACK_W)
    w_out = jnp.stack([g["w_out"].reshape(N_DEV, 128, 1024) for g in big], axis=1).reshape(N_DEV, 256, PACK_W)
    w_up = jnp.stack([g["w_up"].reshape(1024, N_DEV, 512).transpose(1, 0, 2) for g in big], axis=1).reshape(N_DEV, 1024, PACK_W)
    w_down = jnp.stack([g["w_down"].reshape(N_DEV, 512, 1024) for g in big], axis=1).reshape(N_DEV, 1024, PACK_W)
    sms = []
    for name, shape in SMALL_SHARDED:
        g = small_grads[name].reshape(shape[0], shape[1], N_DEV, shape[2]).transpose(2, 0, 1, 3)
        sms.append(g.reshape(N_DEV, -1))
    sms = jnp.concatenate(sms, axis=1)
    sms = jnp.concatenate([sms, jnp.zeros((N_DEV, SMS_ROWS * PACK_W - sms.shape[1]), F32)], axis=1).reshape(N_DEV, SMS_ROWS, PACK_W)
    rep = _flat_rows([small_grads[n] for n, _ in REPLICATED], REP_ROWS)
    rep = jnp.broadcast_to(rep[None], (N_DEV, REP_ROWS, PACK_W))
    pad = jnp.zeros((N_DEV, PACK_ROWS - BIG_ROWS - SMS_ROWS - REP_ROWS, PACK_W), F32)
    return jnp.concatenate([w_in, w_out, w_up, w_down, sms, rep, pad], axis=1)


def _local_step(x, tgt, wts, raw):
    consts = _consts()
    ps = [_layer_params(l, raw, consts) for l in range(DEPTH)]
    x1, sv0 = layer_fwd(0, x, None, wts[0], ps[0])
    x2, sv1 = layer_fwd(1, x1, sv0["fl"], wts[1], ps[1])
    dy, lparts = loss_call(x2, tgt)
    loss = jnp.sum(lparts[::8, 0])
    dx1, dvfirst, g1 = layer_bwd(1, dy, None, sv1, wts[1], ps[1])
    dx0, _, g0 = layer_bwd(0, dx1, dvfirst, sv0, wts[0], ps[0])
    big = [{k: g[k] for k in ("w_in", "w_out", "w_up", "w_down")} for g in (g0, g1)]
    return loss, dx0, big, _natural_grads(g0, g1)


WEIGHT_NAMES = ("lower_bounds", "w_in", "w_in_vres", "mu_shift", "mu_vres", "rwkv_w0", "rwkv_w2", "rwkv_a0", "rwkv_a2",
                "rwkv_g2", "rwkv_k_k", "rwkv_k_a", "rwkv_r_k", "rwkv_lnx_w", "rwkv_lnx_b", "rwkv_v0", "rwkv_v2",
                "ssd_conv_w", "ssd_conv_b", "ssd_dt_bias", "ssd_A_log", "ssd_D", "ssd_norm_w", "hgrn_norm_w", "w_out",
                "ln1_w", "ln1_b", "w_up", "w_down", "ln2_w", "ln2_b")


def kernel(x, lower_bounds, w_in, w_in_vres, mu_shift, mu_vres, rwkv_w0, rwkv_w2, rwkv_a0, rwkv_a2, rwkv_g2, rwkv_k_k, rwkv_k_a, rwkv_r_k, rwkv_lnx_w, rwkv_lnx_b, rwkv_v0, rwkv_v2, ssd_conv_w, ssd_conv_b, ssd_dt_bias, ssd_A_log, ssd_D, ssd_norm_w, hgrn_norm_w, w_out, ln1_w, ln1_b, w_up, w_down, ln2_w, ln2_b, loss_target, m_lower_bounds, m_w_in, m_w_in_vres, m_mu_shift, m_mu_vres, m_rwkv_w0, m_rwkv_w2, m_rwkv_a0, m_rwkv_a2, m_rwkv_g2, m_rwkv_k_k, m_rwkv_k_a, m_rwkv_r_k, m_rwkv_lnx_w, m_rwkv_lnx_b, m_rwkv_v0, m_rwkv_v2, m_ssd_conv_w, m_ssd_conv_b, m_ssd_dt_bias, m_ssd_A_log, m_ssd_D, m_ssd_norm_w, m_hgrn_norm_w, m_w_out, m_ln1_w, m_ln1_b, m_w_up, m_w_down, m_ln2_w, m_ln2_b, v_lower_bounds, v_w_in, v_w_in_vres, v_mu_shift, v_mu_vres, v_rwkv_w0, v_rwkv_w2, v_rwkv_a0, v_rwkv_a2, v_rwkv_g2, v_rwkv_k_k, v_rwkv_k_a, v_rwkv_r_k, v_rwkv_lnx_w, v_rwkv_lnx_b, v_rwkv_v0, v_rwkv_v2, v_ssd_conv_w, v_ssd_conv_b, v_ssd_dt_bias, v_ssd_A_log, v_ssd_D, v_ssd_norm_w, v_hgrn_norm_w, v_w_out, v_ln1_w, v_ln1_b, v_w_up, v_w_down, v_ln2_w, v_ln2_b):
    given = dict(locals())
    w = {n: given[n] for n in WEIGHT_NAMES}
    pw = _pack_local(w)
    pm = _pack_local({n: given["m_" + n] for n in WEIGHT_NAMES})
    pv = _pack_local({n: given["v_" + n] for n in WEIGHT_NAMES})
    gb, gs = all_gather([pw[:BIG_ROWS].astype(BF16), pw[BIG_ROWS:BIG_ROWS + SMS_ROWS]])
    wts, small_full = _gathered_weights(gb, gs)
    raw = {n: w[n] for n, _ in REPLICATED}
    raw.update(small_full)
    loss, dx, big, small_grads = _local_step(x[0], loss_target[0], wts, raw)
    recv = grad_exchange(_pack_send(big, small_grads))
    g, delta, new_m, new_v = adamw(recv, pw, pm, pv)
    loss = lax.psum(loss, ("x", "y", "c"))
    outs = [loss, dx[None]]
    for packed in (g, delta, new_m, new_v):
        d = _unpack_local(packed)
        outs += [d[n] for n in WEIGHT_NAMES]
    return tuple(outs)
```

```python
import functools

import jax
import jax.numpy as jnp
from jax import lax
from jax.experimental import pallas as pl
from jax.experimental.pallas import tpu as pltpu

F32 = jnp.float32
BF16 = jnp.bfloat16
HI = lax.Precision.HIGHEST

N_DEV = 8
SEQ = 2048
D_MODEL = 1024
D_FF = 4096
DG = 256
NH = 4
HD = 64
DEPTH = 2
ALPHA = (2.0 * DEPTH) ** 0.25
LN_EPS = 1e-5
RMS_EPS = 1e-5
GN_EPS = HD * 1e-5
IN_COLS = 3716
SSD_N = 128
SSD_CHUNK = 128
HGRN_CHUNK = 16
DILATED = ((128, 1), (512, 4), (2048, 16))

ADAM_LR, ADAM_B1, ADAM_B2, ADAM_EPS, ADAM_WD, ADAM_STEP = 0.001, 0.9, 0.999, 1e-08, 0.01, 10

PW = 4096
C_R, C_K, C_V = 0, 256, 512
C_AQ, C_AK, C_AV = 768, 1024, 1280
C_Z, C_XBC = 1536, 1792
C_HQ, C_HF, C_HI, C_HG = 2560, 2816, 3072, 3328
C_LORA, C_DT, C_VRES = 3584, 3712, 3840

RB = 256
VMEM_LIMIT = 56 * 1024 * 1024
PACK_W = 1024


def _cp(sem=None):
    return pltpu.CompilerParams(dimension_semantics=sem, vmem_limit_bytes=VMEM_LIMIT)


def _sds(shape, dt=F32):
    return jax.ShapeDtypeStruct(tuple(shape), dt)


def _rows(w, cb=0, rb=RB):
    return pl.BlockSpec((rb, w), lambda i: (i, cb))


def _full(shape):
    n = len(shape)
    return pl.BlockSpec(tuple(shape), lambda *_: (0,) * n)


def _sigmoid(x):
    return 1.0 / (1.0 + jnp.exp(-x))


def _silu(x):
    return x * _sigmoid(x)


def _softplus(x):
    return jnp.maximum(x, 0.0) + jnp.log(1.0 + jnp.exp(jnp.where(x > 0, -x, x)))


MID = lax.Precision.HIGH
NN, TN, NT = (((1,), (0,)), ((), ())), (((0,), (0,)), ((), ())), (((1,), (1,)), ((), ()))


def _dot(a, b):
    return lax.dot_general(a, b, NN, precision=MID, preferred_element_type=F32)


def _dot_tn(a, b):
    return lax.dot_general(a, b, TN, precision=MID, preferred_element_type=F32)


def _dot_nt(a, b):
    return lax.dot_general(a, b, NT, precision=MID, preferred_element_type=F32)


def _dotx(a, b):
    return lax.dot_general(a, b, NN, precision=HI, preferred_element_type=F32)


def _dotx_tn(a, b):
    return lax.dot_general(a, b, TN, precision=HI, preferred_element_type=F32)


def _seg_ones(n, seg):
    i = jnp.arange(n)
    return (i[:, None] // seg == i[None, :] // seg).astype(F32)


def _shift_down(x, s):
    row = lax.broadcasted_iota(jnp.int32, x.shape, 0)
    return jnp.where(row < s, 0.0, pltpu.roll(x, s, 0))


def _shift_up(x, s):
    n = x.shape[0]
    row = lax.broadcasted_iota(jnp.int32, x.shape, 0)
    return jnp.where(row >= n - s, 0.0, pltpu.roll(x, n - s, 0))


@functools.partial(jax.custom_vjp, nondiff_argnums=(1,))
def _tshift(x, s):
    return _shift_down(x, s)


def _tshift_fwd(x, s):
    return _shift_down(x, s), None


def _tshift_bwd(s, _, g):
    return (_shift_up(g, s),)


_tshift.defvjp(_tshift_fwd, _tshift_bwd)


def _map_fwd(name, fn, grid, ins, in_specs, out_shapes, out_specs):
    n_in = len(ins)

    def body(*refs):
        ys = fn(*[r[...] for r in refs[:n_in]])
        for r, y in zip(refs[n_in:], ys):
            r[...] = y

    return pl.pallas_call(body, grid=grid, in_specs=in_specs, out_specs=out_specs, out_shape=out_shapes,
                          name=name, compiler_params=_cp(("parallel",)))(*ins)


def _map_bwd(name, fn, grid, ins, in_specs, cts, ct_specs, want, acc=(), gout=None):
    n_in = len(ins)
    flat_cts = [c for group in cts for c in group]
    flat_specs = [s for group in ct_specs for s in group]
    n_ct = len(flat_cts)
    gout = gout or {}
    out_shapes = [gout[i][0] if i in gout else _sds(ins[i].shape) for i in want]
    out_specs = [gout[i][1] if i in gout else in_specs[i] for i in want]

    def body(*refs):
        xs = [r[...] for r in refs[:n_in]]
        cvals = [r[...] for r in refs[n_in:n_in + n_ct]]
        gouts = refs[n_in + n_ct:]
        cs, p = [], 0
        for group in cts:
            v = cvals[p]
            for q in range(1, len(group)):
                v = v + cvals[p + q]
            cs.append(v)
            p += len(group)

        def f(*wanted):
            full = list(xs)
            for i, w in zip(want, wanted):
                full[i] = w
            return tuple(fn(*full))

        _, vjp = jax.vjp(f, *[xs[i] for i in want])
        gs = vjp(tuple(cs))
        for o, i, g in zip(gouts, want, gs):
            if i in acc:
                @pl.when(pl.program_id(0) == 0)
                def _():
                    o[...] = jnp.zeros_like(o)

                o[...] += g
            else:
                o[...] = g

    sem = ("arbitrary",) if acc else ("parallel",)
    return pl.pallas_call(body, grid=grid, in_specs=list(in_specs) + flat_specs, out_specs=out_specs,
                          out_shape=out_shapes, name=name, compiler_params=_cp(sem))(*ins, *flat_cts)


def _addn(name, *arrs):
    n, c = arrs[0].shape

    def fn(*xs):
        r = xs[0]
        for x in xs[1:]:
            r = r + x
        return (r,)

    return _map_fwd(name, fn, (n // RB,), list(arrs), [_rows(c)] * len(arrs), [_sds((n, c))], [_rows(c)])[0]


def _mm(name, a, b, mode, tm, tn, tk, add=None, add_scale=1.0):
    if mode == "nn":
        (m, k), n = a.shape, b.shape[1]
    elif mode == "nt":
        (m, k), n = a.shape, b.shape[0]
    else:
        (k, m), n = a.shape, b.shape[1]
    nk = k // tk
    dn = {"nn": (((1,), (0,)), ((), ())), "nt": (((1,), (1,)), ((), ())), "tn": (((0,), (0,)), ((), ()))}[mode]

    def body(*refs):
        if add is None:
            a_ref, b_ref, o_ref, acc = refs
        else:
            a_ref, b_ref, add_ref, o_ref, acc = refs
        kk = pl.program_id(2)

        @pl.when(kk == 0)
        def _():
            acc[...] = jnp.zeros_like(acc)

        acc[...] += lax.dot_general(a_ref[...].astype(BF16), b_ref[...].astype(BF16), dn, preferred_element_type=F32)

        @pl.when(kk == nk - 1)
        def _():
            r = acc[...]
            if add is not None:
                r = r + add_scale * add_ref[...]
            o_ref[...] = r

    a_spec = pl.BlockSpec((tk, tm), lambda i, j, q: (q, i)) if mode == "tn" else pl.BlockSpec((tm, tk), lambda i, j, q: (i, q))
    b_spec = pl.BlockSpec((tn, tk), lambda i, j, q: (j, q)) if mode == "nt" else pl.BlockSpec((tk, tn), lambda i, j, q: (q, j))
    o_spec = pl.BlockSpec((tm, tn), lambda i, j, q: (i, j))
    ins, specs = [a, b], [a_spec, b_spec]
    if add is not None:
        ins.append(add)
        specs.append(o_spec)
    return pl.pallas_call(body, grid=(m // tm, n // tn, nk), in_specs=specs, out_specs=o_spec, out_shape=_sds((m, n)),
                          scratch_shapes=[pltpu.VMEM((tm, tn), F32)], name=name,
                          compiler_params=_cp(("parallel", "parallel", "arbitrary")))(*ins)


LERP_BLOCKS = (0, 1, 2, 3, 4, 5, C_LORA // 128, C_VRES // 128)


def _lerp_colmap(j):
    r = jnp.where(j < 6, j, jnp.where(j == 6, C_LORA // 128, C_VRES // 128))
    return (0, r)


def _lerp_fn(f, mu):
    return (f + (_tshift(f, 1) - f) * mu,)


def _lerp_specs():
    return [pl.BlockSpec((SEQ, 128), _lerp_colmap), pl.BlockSpec((1, 128), lambda j: (0, j))]


def lerp_fwd(l, proj, mu):
    return _map_fwd(f"lerp_fwd{l}", _lerp_fn, (8,), [proj, mu], _lerp_specs(), [_sds((SEQ, 1024))],
                    [pl.BlockSpec((SEQ, 128), lambda j: (0, j))])[0]


def lerp_bwd(l, proj, mu, dfl):
    n_in = 2

    def body(f_ref, mu_ref, g_ref, df_ref, dmu_ref):
        _, vjp = jax.vjp(_lerp_fn, f_ref[...], mu_ref[...])
        df, dmu = vjp((g_ref[...],))
        df_ref[...] = df
        dmu_ref[...] = dmu

    cspec = pl.BlockSpec((SEQ, 128), lambda j: (0, j))
    return pl.pallas_call(body, grid=(8,), in_specs=_lerp_specs() + [cspec],
                          out_specs=[cspec, pl.BlockSpec((1, 128), lambda j: (0, j))],
                          out_shape=[_sds((SEQ, 1024)), _sds((1, 1024))], name=f"lerp_bwd{l}",
                          compiler_params=_cp(("parallel",)))(proj, mu, dfl)


def _conv_fn(x, w, b):
    y = x * w[3:4, :] + _tshift(x, 1) * w[2:3, :] + _tshift(x, 2) * w[1:2, :] + _tshift(x, 3) * w[0:1, :] + b
    return (_silu(y),)


def _conv_specs():
    return [pl.BlockSpec((SEQ, 128), lambda j: (0, C_XBC // 128 + j)), pl.BlockSpec((4, 128), lambda j: (0, j)),
            pl.BlockSpec((1, 128), lambda j: (0, j))]


def conv_fwd(l, proj, w, b):
    return _map_fwd(f"conv_fwd{l}", _conv_fn, (6,), [proj, w, b], _conv_specs(), [_sds((SEQ, 768))],
                    [pl.BlockSpec((SEQ, 128), lambda j: (0, j))])[0]


def conv_bwd(l, proj, w, b, dxc):
    def body(x_ref, w_ref, b_ref, g_ref, dx_ref, dw_ref, db_ref):
        _, vjp = jax.vjp(_conv_fn, x_ref[...], w_ref[...], b_ref[...])
        dx, dw, db = vjp((g_ref[...],))
        dx_ref[...] = dx
        dw_ref[...] = dw
        db_ref[...] = db

    cspec = pl.BlockSpec((SEQ, 128), lambda j: (0, j))
    return pl.pallas_call(body, grid=(6,), in_specs=_conv_specs() + [cspec],
                          out_specs=[cspec, pl.BlockSpec((4, 128), lambda j: (0, j)), pl.BlockSpec((1, 128), lambda j: (0, j))],
                          out_shape=[_sds((SEQ, 768)), _sds((4, 768)), _sds((1, 768))], name=f"conv_bwd{l}",
                          compiler_params=_cp(("parallel",)))(proj, w, b, dxc)


def _rwkv_pre_fn(has_vres):
    def fn(fk, fv, flora, *rest):
        if has_vres:
            fvres, vfirst, w0, w2p, a0, a2p, g2p, k_k, k_a, v0, v2p, seg = rest
        else:
            w0, w2p, a0, a2p, g2p, k_k, k_a, seg = rest
        w_log = -_softplus(-(w0 + _dot(jnp.tanh(flora), w2p))) - 0.5
        w = jnp.exp(-jnp.exp(w_log))
        a = _sigmoid(a0 + _dot(flora, a2p))
        g = _dot(_sigmoid(flora), g2p)
        if has_vres:
            v2 = fv + (vfirst - fv) * _sigmoid(v0 + _dot(fvres, v2p))
        else:
            v2 = fv * 1.0
        kk = fk * k_k
        kk = kk / jnp.maximum(jnp.sqrt(_dot(kk * kk, seg)), 1e-12)
        k2 = fk * (1.0 + (a - 1.0) * k_a)
        return w, k2, v2, -kk, kk * a, g

    return fn


def _rwkv_pre_args(fl, vfirst, p, has_vres):
    ins = [fl, fl, fl]
    specs = [_rows(256, 1), _rows(256, 2), _rows(128, 6)]
    if has_vres:
        ins += [fl, vfirst]
        specs += [_rows(128, 7), _rows(256, 2)]
    names = ["w0", "w2p", "a0", "a2p", "g2p", "k_k", "k_a"] + (["v0", "v2p"] if has_vres else []) + ["seg64"]
    for nme in names:
        ins.append(p[nme])
        specs.append(_full(p[nme].shape))
    return ins, specs, names


def rwkv_pre_fwd(l, fl, vfirst, p):
    has_vres = l > 0
    ins, specs, _ = _rwkv_pre_args(fl, vfirst, p, has_vres)
    return _map_fwd(f"rwkv_pre_fwd{l}", _rwkv_pre_fn(has_vres), (SEQ // RB,), ins, specs,
                    [_sds((SEQ, DG))] * 6, [_rows(DG)] * 6)


def rwkv_pre_bwd(l, fl, vfirst, p, cts):
    has_vres = l > 0
    ins, specs, names = _rwkv_pre_args(fl, vfirst, p, has_vres)
    n_row = 5 if has_vres else 3
    want = list(range(n_row)) + [n_row + i for i, nme in enumerate(names) if nme != "seg64"]
    acc = tuple(w for w in want if w >= n_row)
    ct_specs = [[_rows(DG)] * len(g) for g in cts]
    gout = {0: (_sds((SEQ, DG)), _rows(DG)), 1: (_sds((SEQ, DG)), _rows(DG)), 2: (_sds((SEQ, 128)), _rows(128))}
    if has_vres:
        gout[3] = (_sds((SEQ, 128)), _rows(128))
        gout[4] = (_sds((SEQ, DG)), _rows(DG))
    gs = _map_bwd(f"rwkv_pre_bwd{l}", _rwkv_pre_fn(has_vres), (SEQ // RB,), ins, specs, cts, ct_specs, want, acc, gout)
    keys = ["fk", "fv", "flora"] + (["fvres", "vfirst"] if has_vres else []) + [nme for nme in names if nme != "seg64"]
    return dict(zip(keys, gs))


def _rwkv_post_fn(y, fr, k2, v2, g, lnx_w, lnx_b, r_k, seg):
    mu = _dot(y, seg) * (1.0 / HD)
    d = y - mu
    var = _dot(d * d, seg) * (1.0 / HD)
    yn = d * lax.rsqrt(var + GN_EPS) * lnx_w + lnx_b
    bonus = _dot(fr * k2 * r_k, seg) * v2
    return ((yn + bonus) * g,)


def _rwkv_post_args(y, fl, k2, v2, g, p):
    ins = [y, fl, k2, v2, g, p["lnx_w"], p["lnx_b"], p["r_k"], p["seg64"]]
    specs = [_rows(DG), _rows(DG, 0), _rows(DG), _rows(DG), _rows(DG)] + [_full(x.shape) for x in ins[5:]]
    return ins, specs


def rwkv_post_fwd(l, y, fl, k2, v2, g, p):
    ins, specs = _rwkv_post_args(y, fl, k2, v2, g, p)
    return _map_fwd(f"rwkv_post_fwd{l}", _rwkv_post_fn, (SEQ // RB,), ins, specs, [_sds((SEQ, DG))], [_rows(DG)])[0]


def rwkv_post_bwd(l, y, fl, k2, v2, g, p, dya):
    ins, specs = _rwkv_post_args(y, fl, k2, v2, g, p)
    gs = _map_bwd(f"rwkv_post_bwd{l}", _rwkv_post_fn, (SEQ // RB,), ins, specs, [[dya]], [[_rows(DG)]],
                  want=[0, 1, 2, 3, 4, 5, 6, 7], acc=(5, 6, 7), gout={1: (_sds((SEQ, DG)), _rows(DG))})
    return dict(zip(["y", "fr", "k2", "v2", "g", "lnx_w", "lnx_b", "r_k"], gs))


SCAN_TB = 64


def _split3(x):
    hi = x.astype(BF16)
    r1 = x - hi.astype(F32)
    mid = r1.astype(BF16)
    lo = (r1 - mid.astype(F32)).astype(BF16)
    return hi, mid, lo


def _segdot(x, ones_bf16):
    hi, mid, lo = _split3(x)
    d = lambda u: jnp.dot(u, ones_bf16, preferred_element_type=F32)
    return d(hi) + d(mid) + d(lo)


def _coltile8(rows8, dmask, ones_bf16):
    x = (rows8[:, None, :] * dmask[None]).reshape(8 * HD, DG)
    return _segdot(x, ones_bf16).reshape(8, HD, DG)


def _diag_rows(tiles8, dmask):
    return jnp.sum(tiles8 * dmask[None], axis=1)


def rwkv_scan_fwd(l, fl, w, k2, v2, c, b, p):
    nblk = SEQ // SCAN_TB

    def body(r_ref, w_ref, k_ref, v_ref, c_ref, b_ref, ones_ref, dm_ref, y_ref, st_ref, s_sc):
        @pl.when(pl.program_id(0) == 0)
        def _():
            s_sc[...] = jnp.zeros_like(s_sc)

        ones = ones_ref[...]
        dmask = dm_ref[...]

        def group(gi, carry):
            t0 = pl.multiple_of(gi * 8, 8)
            sl = pl.ds(t0, 8)
            r8, w8, k8, c8, b8 = r_ref[sl, :], w_ref[sl, :], k_ref[sl, :], c_ref[sl, :], b_ref[sl, :]
            vt8 = _coltile8(v_ref[sl, :], dmask, ones)
            s = s_sc[...]
            for j in range(8):
                sa = _segdot(s * c8[j:j + 1, :], ones)
                s = s * w8[j:j + 1, :] + sa * b8[j:j + 1, :] + vt8[j] * k8[j:j + 1, :]
                st_ref[t0 + j] = s
            s_sc[...] = s
            s8 = st_ref[sl]
            yt = _segdot((s8 * r8[:, None, :]).reshape(8 * HD, DG), ones).reshape(8, HD, DG)
            y_ref[sl, :] = _diag_rows(yt, dmask)
            return carry

        lax.fori_loop(0, SCAN_TB // 8, group, 0)

    row = pl.BlockSpec((SCAN_TB, DG), lambda i: (i, 0))
    ins = [fl, w, k2, v2, c, b, p["seg64_bf16"], p["dmask"]]
    specs = [row] * 6 + [_full((DG, DG)), _full((HD, DG))]
    return pl.pallas_call(body, grid=(nblk,), in_specs=specs,
                          out_specs=[row, pl.BlockSpec((SCAN_TB, HD, DG), lambda i: (i, 0, 0))],
                          out_shape=[_sds((SEQ, DG)), _sds((SEQ, HD, DG))],
                          scratch_shapes=[pltpu.VMEM((HD, DG), F32)], name=f"rwkv_scan_fwd{l}",
                          compiler_params=_cp(("arbitrary",)))(*ins)


def rwkv_scan_bwd(l, fl, w, k2, v2, c, b, states, dy, p):
    nblk = SEQ // SCAN_TB

    def body(r_ref, w_ref, k_ref, v_ref, c_ref, b_ref, dy_ref, st_ref, sp_ref, ones_ref, dm_ref,
             dr_ref, dw_ref, dk_ref, dv_ref, dc_ref, db_ref, g_sc, prev_sc, d8_sc, dsa8_sc):
        i = pl.program_id(0)

        @pl.when(i == 0)
        def _():
            g_sc[...] = jnp.zeros_like(g_sc)

        ones = ones_ref[...]
        dmask = dm_ref[...]
        first_block = i == nblk - 1

        def group(gr, carry):
            gi = SCAN_TB // 8 - 1 - gr
            t0 = pl.multiple_of(gi * 8, 8)
            sl = pl.ds(t0, 8)
            r8, w8, k8, c8, b8 = r_ref[sl, :], w_ref[sl, :], k_ref[sl, :], c_ref[sl, :], b_ref[sl, :]
            s8 = st_ref[sl]
            @pl.when(gi > 0)
            def _():
                prev_sc[0] = st_ref[t0 - 1]

            @pl.when(gi == 0)
            def _():
                prev_sc[0] = jnp.where(first_block, 0.0, sp_ref[0])

            for j in range(1, 8):
                prev_sc[j] = s8[j - 1]
            sp8 = prev_sc[...]
            vt8 = _coltile8(v_ref[sl, :], dmask, ones)
            dyt8 = _coltile8(dy_ref[sl, :], dmask, ones)
            sa8 = _segdot((sp8 * c8[:, None, :]).reshape(8 * HD, DG), ones).reshape(8, HD, DG)
            g = g_sc[...]
            for j in range(7, -1, -1):
                g = g + dyt8[j] * r8[j:j + 1, :]
                d8_sc[j] = g
                dsa = _segdot(g * b8[j:j + 1, :], ones)
                dsa8_sc[j] = dsa
                g = g * w8[j:j + 1, :] + dsa * c8[j:j + 1, :]
            g_sc[...] = g
            d8 = d8_sc[...]
            dsa8 = dsa8_sc[...]
            dr_ref[sl, :] = jnp.sum(s8 * dyt8, axis=1)
            dk_ref[sl, :] = jnp.sum(d8 * vt8, axis=1)
            dw_ref[sl, :] = jnp.sum(sp8 * d8, axis=1)
            db_ref[sl, :] = jnp.sum(d8 * sa8, axis=1)
            dc_ref[sl, :] = jnp.sum(sp8 * dsa8, axis=1)
            dvt = _segdot((d8 * k8[:, None, :]).reshape(8 * HD, DG), ones).reshape(8, HD, DG)
            dv_ref[sl, :] = _diag_rows(dvt, dmask)
            return carry

        lax.fori_loop(0, SCAN_TB // 8, group, 0)

    row = pl.BlockSpec((SCAN_TB, DG), lambda i: (nblk - 1 - i, 0))
    st_spec = pl.BlockSpec((SCAN_TB, HD, DG), lambda i: (nblk - 1 - i, 0, 0))
    sp_spec = pl.BlockSpec((1, HD, DG), lambda i: (jnp.maximum((nblk - 1 - i) * SCAN_TB - 1, 0), 0, 0))
    ins = [fl, w, k2, v2, c, b, dy, states, states, p["seg64_bf16"], p["dmask"]]
    specs = [row] * 7 + [st_spec, sp_spec, _full((DG, DG)), _full((HD, DG))]
    tile8 = pltpu.VMEM((8, HD, DG), F32)
    return pl.pallas_call(body, grid=(nblk,), in_specs=specs, out_specs=[row] * 6, out_shape=[_sds((SEQ, DG))] * 6,
                          scratch_shapes=[pltpu.VMEM((HD, DG), F32), tile8, tile8, tile8], name=f"rwkv_scan_bwd{l}",
                          compiler_params=_cp(("arbitrary",)))(*ins)


HG_ROWS = 128


def _hgrn_chunk_fn(layer):
    def fn(hq, hf, hi, hg, sprev, lb0, lb1, norm_w, seg, bd, tri, causal, ones16):
        e0 = jnp.exp(lb0 - jnp.maximum(lb0, lb1))
        e1 = jnp.exp(lb1 - jnp.maximum(lb0, lb1))
        sm0, sm1 = e0 / (e0 + e1), e1 / (e0 + e1)
        lb = (sm0 - sm0) if layer == 0 else ((sm0 + sm1) - sm0)
        forget = lb + (1.0 - lb) * _sigmoid(hf)
        logf = jnp.log(forget)
        kk = 1.0 - forget
        q = _silu(hq)
        c = HGRN_CHUNK
        b = _dotx(tri, logf)
        bl = jnp.sum(logf, axis=0, keepdims=True)
        diff = (b[:, None, :] - b[None, :, :]).reshape(c * c, DG)
        dec = jnp.exp(jnp.where(causal > 0.5, diff, -1e30))
        qrep = jnp.broadcast_to(q[:, None, :], (c, c, DG)).reshape(c * c, DG)
        ktil = jnp.broadcast_to(kk[None, :, :], (c, c, DG)).reshape(c * c, DG)
        vtil = jnp.broadcast_to(hi[None, :, :], (c, c, DG)).reshape(c * c, DG)
        att = _dot(qrep * ktil * dec, seg)
        o_intra = jnp.sum((att * vtil).reshape(c, c, DG), axis=1)
        kdec = kk * jnp.exp(bl - b)
        u = _dot_tn(kdec, hi) * bd
        tot = jnp.exp(_dotx_tn(logf, ones16))
        snext = sprev * tot + u
        o = o_intra + _dot(q * jnp.exp(b), sprev)
        ms = _dot(o * o, seg) * (1.0 / HD)
        y = o * lax.rsqrt(ms + RMS_EPS) * norm_w * _silu(hg)
        return y, snext

    return fn


def _hgrn_consts(p):
    return [p["seg64"], p["seg64"], p["tri16"], p["causal16"], p["ones16"]]


def hgrn_fwd(l, proj, p):
    fn = _hgrn_chunk_fn(l)
    nch = HG_ROWS // HGRN_CHUNK

    def body(hq_ref, hf_ref, hi_ref, hg_ref, lb0_ref, lb1_ref, nw_ref, seg_ref, bd_ref, tri_ref, cau_ref, o16_ref,
             y_ref, st_ref, s_sc):
        @pl.when(pl.program_id(0) == 0)
        def _():
            s_sc[...] = jnp.zeros_like(s_sc)

        consts = (lb0_ref[...], lb1_ref[...], nw_ref[...], seg_ref[...], bd_ref[...], tri_ref[...], cau_ref[...],
                  o16_ref[...])

        def chunk(ci, carry):
            sl = pl.ds(pl.multiple_of(ci * HGRN_CHUNK, HGRN_CHUNK), HGRN_CHUNK)
            sprev = s_sc[...]
            st_ref[ci] = sprev
            y, snext = fn(hq_ref[sl, :], hf_ref[sl, :], hi_ref[sl, :], hg_ref[sl, :], sprev, *consts)
            y_ref[sl, :] = y
            s_sc[...] = snext
            return carry

        lax.fori_loop(0, nch, chunk, 0)

    rows = lambda cb: pl.BlockSpec((HG_ROWS, DG), lambda i: (i, cb))
    ins = [proj, proj, proj, proj, p["lb0"], p["lb1"], p["hgrn_norm_w"]] + _hgrn_consts(p)
    specs = [rows(C_HQ // DG), rows(C_HF // DG), rows(C_HI // DG), rows(C_HG // DG)] + [_full(x.shape) for x in ins[4:]]
    return pl.pallas_call(body, grid=(SEQ // HG_ROWS,), in_specs=specs,
                          out_specs=[rows(0), pl.BlockSpec((nch, DG, DG), lambda i: (i, 0, 0))],
                          out_shape=[_sds((SEQ, DG)), _sds((SEQ // HGRN_CHUNK, DG, DG))],
                          scratch_shapes=[pltpu.VMEM((DG, DG), F32)], name=f"hgrn_fwd{l}",
                          compiler_params=_cp(("arbitrary",)))(*ins)


def hgrn_bwd(l, proj, states, dy, p):
    fn = _hgrn_chunk_fn(l)
    nch = HG_ROWS // HGRN_CHUNK
    nblk = SEQ // HG_ROWS

    def body(hq_ref, hf_ref, hi_ref, hg_ref, st_ref, dy_ref, lb0_ref, lb1_ref, nw_ref, seg_ref, bd_ref, tri_ref,
             cau_ref, o16_ref, dp_ref, dlb0_ref, dlb1_ref, dnw_ref, ds_sc):
        @pl.when(pl.program_id(0) == 0)
        def _():
            ds_sc[...] = jnp.zeros_like(ds_sc)
            dlb0_ref[...] = jnp.zeros_like(dlb0_ref)
            dlb1_ref[...] = jnp.zeros_like(dlb1_ref)
            dnw_ref[...] = jnp.zeros_like(dnw_ref)

        consts = (seg_ref[...], bd_ref[...], tri_ref[...], cau_ref[...], o16_ref[...])

        def chunk(cr, carry):
            ci = nch - 1 - cr
            sl = pl.ds(pl.multiple_of(ci * HGRN_CHUNK, HGRN_CHUNK), HGRN_CHUNK)
            f = lambda hq, hf, hi, hg, sp, b0, b1, nw: fn(hq, hf, hi, hg, sp, b0, b1, nw, *consts)
            _, vjp = jax.vjp(f, hq_ref[sl, :], hf_ref[sl, :], hi_ref[sl, :], hg_ref[sl, :], st_ref[ci],
                             lb0_ref[...], lb1_ref[...], nw_ref[...])
            dhq, dhf, dhi, dhg, dsp, dlb0, dlb1, dnw = vjp((dy_ref[sl, :], ds_sc[...]))
            dp_ref[sl, 0:DG] = dhq
            dp_ref[sl, DG:2 * DG] = dhf
            dp_ref[sl, 2 * DG:3 * DG] = dhi
            dp_ref[sl, 3 * DG:4 * DG] = dhg
            ds_sc[...] = dsp
            dlb0_ref[...] += dlb0
            dlb1_ref[...] += dlb1
            dnw_ref[...] += dnw
            return carry

        lax.fori_loop(0, nch, chunk, 0)

    rows = lambda cb: pl.BlockSpec((HG_ROWS, DG), lambda i: (nblk - 1 - i, cb))
    ins = [proj, proj, proj, proj, states, dy, p["lb0"], p["lb1"], p["hgrn_norm_w"]] + _hgrn_consts(p)
    specs = [rows(C_HQ // DG), rows(C_HF // DG), rows(C_HI // DG), rows(C_HG // DG),
             pl.BlockSpec((nch, DG, DG), lambda i: (nblk - 1 - i, 0, 0)), rows(0)] + [_full(x.shape) for x in ins[6:]]
    return pl.pallas_call(body, grid=(nblk,), in_specs=specs,
                          out_specs=[pl.BlockSpec((HG_ROWS, 4 * DG), lambda i: (nblk - 1 - i, 0)), _full((1, DG)),
                                     _full((1, DG)), _full((1, DG))],
                          out_shape=[_sds((SEQ, 4 * DG)), _sds((1, DG)), _sds((1, DG)), _sds((1, DG))],
                          scratch_shapes=[pltpu.VMEM((DG, DG), F32)], name=f"hgrn_bwd{l}",
                          compiler_params=_cp(("arbitrary",)))(*ins)


def _ssd_chunk_fn(z, xs, bm, cm, dtr, sprev, dt_bias, a_log, d_par, norm_w, e128, tri, trit, seg128, ones128):
    lc = SSD_CHUNK
    dt = _softplus(dtr + dt_bias)
    a = -jnp.exp(a_log)
    da = dt * a * (lax.broadcasted_iota(jnp.int32, (1, 128), 1) < NH).astype(F32)
    cs = _dotx(tri, da)
    cst = _dotx_tn(da, trit)
    cs_b = _dotx(cs, e128)
    dt_b = _dotx(dt, e128)
    csl_b = _dotx(jnp.sum(da, axis=0, keepdims=True), e128)
    xdt = xs * dt_b
    lane = lax.broadcasted_iota(jnp.int32, (1, DG), 1)
    rowi = lax.broadcasted_iota(jnp.int32, (lc, lc), 0)
    coli = lax.broadcasted_iota(jnp.int32, (lc, lc), 1)
    y = jnp.zeros((lc, DG), F32)
    snew = jnp.zeros((DG, SSD_N), F32)
    d_b = jnp.zeros((1, DG), F32)
    wdec = xdt * jnp.exp(csl_b - cs_b)
    for g in range(2):
        bg = bm[:, g * SSD_N:(g + 1) * SSD_N]
        cg = cm[:, g * SSD_N:(g + 1) * SSD_N]
        gmat = _dot_nt(cg, bg)
        gmask = ((lane // 128) == g).astype(F32)
        snew = snew + _dot_tn(wdec * gmask, bg)
        y = y + _dot_nt(cg, sprev) * gmask * jnp.exp(cs_b)
        for hh in range(2):
            h = 2 * g + hh
            seg = jnp.where(rowi >= coli, cs[:, h:h + 1] - cst[h:h + 1, :], -1e30)
            hmask = ((lane // HD) == h).astype(F32)
            y = y + _dot(gmat * jnp.exp(seg), xdt * hmask)
            d_b = d_b + d_par[:, h:h + 1] * hmask
    cd = jnp.exp(_dotx_tn(_dotx(da, e128), ones128))
    snext = sprev * cd + snew
    y = y + xs * d_b
    y = y * _silu(z)
    ms = _dot(y * y, seg128) * (1.0 / 128.0)
    return y * lax.rsqrt(ms + RMS_EPS) * norm_w, snext


def ssd_fwd(l, proj, xc, p):
    nc = SEQ // SSD_CHUNK

    def body(z_ref, xs_ref, b_ref, c_ref, dt_ref, dtb_ref, al_ref, d_ref, nw_ref, e_ref, tri_ref, trit_ref, sg_ref,
             on_ref, y_ref, st_ref, s_sc):
        @pl.when(pl.program_id(0) == 0)
        def _():
            s_sc[...] = jnp.zeros_like(s_sc)

        sprev = s_sc[...]
        st_ref[0] = sprev
        y, snext = _ssd_chunk_fn(z_ref[...], xs_ref[...], b_ref[...], c_ref[...], dt_ref[...], sprev, dtb_ref[...],
                                 al_ref[...], d_ref[...], nw_ref[...], e_ref[...], tri_ref[...], trit_ref[...],
                                 sg_ref[...], on_ref[...])
        y_ref[...] = y
        s_sc[...] = snext

    rw = lambda w, cb: pl.BlockSpec((SSD_CHUNK, w), lambda i: (i, cb))
    ins = [proj, xc, xc, xc, proj, p["dt_bias"], p["a_log"], p["ssd_d"], p["ssd_norm_w"], p["e128"], p["tri128"],
           p["tri128t"], p["seg128"], p["ones128"]]
    specs = [rw(DG, C_Z // DG), rw(DG, 0), rw(DG, 1), rw(DG, 2), rw(128, C_DT // 128)] + [_full(x.shape) for x in ins[5:]]
    return pl.pallas_call(body, grid=(nc,), in_specs=specs,
                          out_specs=[rw(DG, 0), pl.BlockSpec((1, DG, SSD_N), lambda i: (i, 0, 0))],
                          out_shape=[_sds((SEQ, DG)), _sds((nc, DG, SSD_N))],
                          scratch_shapes=[pltpu.VMEM((DG, SSD_N), F32)], name=f"ssd_fwd{l}",
                          compiler_params=_cp(("arbitrary",)))(*ins)


def ssd_bwd(l, proj, xc, states, dy, p):
    nc = SEQ // SSD_CHUNK

    def body(z_ref, xs_ref, b_ref, c_ref, dt_ref, st_ref, dy_ref, dtb_ref, al_ref, d_ref, nw_ref, e_ref, tri_ref,
             trit_ref, sg_ref, on_ref, dz_ref, dxc_ref, ddt_ref, ddtb_ref, dal_ref, dd_ref, dnw_ref, ds_sc):
        @pl.when(pl.program_id(0) == 0)
        def _():
            ds_sc[...] = jnp.zeros_like(ds_sc)
            ddtb_ref[...] = jnp.zeros_like(ddtb_ref)
            dal_ref[...] = jnp.zeros_like(dal_ref)
            dd_ref[...] = jnp.zeros_like(dd_ref)
            dnw_ref[...] = jnp.zeros_like(dnw_ref)

        consts = (e_ref[...], tri_ref[...], trit_ref[...], sg_ref[...], on_ref[...])
        f = lambda *a: _ssd_chunk_fn(*a, *consts)
        _, vjp = jax.vjp(f, z_ref[...], xs_ref[...], b_ref[...], c_ref[...], dt_ref[...], st_ref[0], dtb_ref[...],
                         al_ref[...], d_ref[...], nw_ref[...])
        dz, dxs, db, dc, ddt, dsp, ddtb, dal, dd, dnw = vjp((dy_ref[...], ds_sc[...]))
        dz_ref[...] = dz
        dxc_ref[:, 0:DG] = dxs
        dxc_ref[:, DG:2 * DG] = db
        dxc_ref[:, 2 * DG:3 * DG] = dc
        ddt_ref[...] = ddt
        ds_sc[...] = dsp
        ddtb_ref[...] += ddtb
        dal_ref[...] += dal
        dd_ref[...] += dd
        dnw_ref[...] += dnw

    rw = lambda w, cb: pl.BlockSpec((SSD_CHUNK, w), lambda i: (nc - 1 - i, cb))
    ins = [proj, xc, xc, xc, proj, states, dy, p["dt_bias"], p["a_log"], p["ssd_d"], p["ssd_norm_w"], p["e128"],
           p["tri128"], p["tri128t"], p["seg128"], p["ones128"]]
    specs = [rw(DG, C_Z // DG), rw(DG, 0), rw(DG, 1), rw(DG, 2), rw(128, C_DT // 128),
             pl.BlockSpec((1, DG, SSD_N), lambda i: (nc - 1 - i, 0, 0)), rw(DG, 0)] + [_full(x.shape) for x in ins[7:]]
    return pl.pallas_call(body, grid=(nc,), in_specs=specs,
                          out_specs=[rw(DG, 0), rw(3 * DG, 0), rw(128, 0), _full((1, 128)), _full((1, 128)), _full((1, 128)),
                                     _full((1, DG))],
                          out_shape=[_sds((SEQ, DG)), _sds((SEQ, 3 * DG)), _sds((SEQ, 128)), _sds((1, 128)), _sds((1, 128)),
                                     _sds((1, 128)), _sds((1, DG))],
                          scratch_shapes=[pltpu.VMEM((DG, SSD_N), F32)], name=f"ssd_bwd{l}",
                          compiler_params=_cp(("arbitrary",)))(*ins)


ATT_BLK = 128


def _slope(h):
    return jnp.where(h == 0, 0.25, jnp.where(h == 1, 0.0625, jnp.where(h == 2, 0.015625, 0.00390625))).astype(F32)


def _att_scores(qn, kc, kp, h, dil, has_prev):
    i = lax.broadcasted_iota(jnp.int32, (ATT_BLK, ATT_BLK), 0)
    j = lax.broadcasted_iota(jnp.int32, (ATT_BLK, ATT_BLK), 1)
    slope = _slope(h)
    scale = HD ** -0.5
    s_c = _dot_nt(qn, kc) * scale - slope * ((i - j) * dil).astype(F32)
    s_p = _dot_nt(qn, kp) * scale - slope * ((ATT_BLK + i - j) * dil).astype(F32)
    m_c = j <= i
    m_p = jnp.logical_and(j >= i, has_prev)
    return jnp.where(m_c, s_c, -1e30), jnp.where(m_p, s_p, -1e30), m_c, m_p


def attn_branch_fwd(l, bi, qs, ks, vs):
    dil, _, ln, _ = qs.shape
    nb = ln // ATT_BLK

    def body(q_ref, k_ref, v_ref, o_ref, l_ref):
        h = pl.program_id(1)

        def blk(n, carry):
            r0 = pl.multiple_of(n * ATT_BLK, ATT_BLK)
            rp = pl.multiple_of(jnp.maximum(n - 1, 0) * ATT_BLK, ATT_BLK)
            qn = q_ref[0, 0, pl.ds(r0, ATT_BLK), :]
            kc, vc = k_ref[0, 0, pl.ds(r0, ATT_BLK), :], v_ref[0, 0, pl.ds(r0, ATT_BLK), :]
            kp, vp = k_ref[0, 0, pl.ds(rp, ATT_BLK), :], v_ref[0, 0, pl.ds(rp, ATT_BLK), :]
            s_c, s_p, m_c, m_p = _att_scores(qn, kc, kp, h, dil, n > 0)
            m = jnp.maximum(jnp.max(s_c, axis=1, keepdims=True), jnp.max(s_p, axis=1, keepdims=True))
            p_c = jnp.where(m_c, jnp.exp(s_c - m), 0.0)
            p_p = jnp.where(m_p, jnp.exp(s_p - m), 0.0)
            den = jnp.sum(p_c, axis=1, keepdims=True) + jnp.sum(p_p, axis=1, keepdims=True)
            o = (_dot(p_c, vc) + _dot(p_p, vp)) / den
            o_ref[0, 0, pl.ds(r0, ATT_BLK), :] = o
            l_ref[0, 0, pl.ds(r0, ATT_BLK), :] = jnp.broadcast_to(m + jnp.log(den), (ATT_BLK, HD))
            return carry

        lax.fori_loop(0, nb, blk, 0)

    spec = pl.BlockSpec((1, 1, ln, HD), lambda z, h: (z, h, 0, 0))
    return pl.pallas_call(body, grid=(dil, NH), in_specs=[spec] * 3, out_specs=[spec] * 2,
                          out_shape=[_sds(qs.shape)] * 2, name=f"attn_fwd{l}_{bi}",
                          compiler_params=_cp(("parallel", "parallel")))(qs, ks, vs)


def attn_branch_bwd(l, bi, qs, ks, vs, dos, lses, deltas):
    dil, _, ln, _ = qs.shape
    nb = ln // ATT_BLK
    scale = HD ** -0.5

    def body(q_ref, k_ref, v_ref, do_ref, l_ref, dl_ref, dq_ref, dk_ref, dv_ref):
        h = pl.program_id(1)
        dk_ref[...] = jnp.zeros_like(dk_ref)
        dv_ref[...] = jnp.zeros_like(dv_ref)

        def blk(n, carry):
            r0 = pl.multiple_of(n * ATT_BLK, ATT_BLK)
            rp = pl.multiple_of(jnp.maximum(n - 1, 0) * ATT_BLK, ATT_BLK)
            cur, prv = pl.ds(r0, ATT_BLK), pl.ds(rp, ATT_BLK)
            qn, don = q_ref[0, 0, cur, :], do_ref[0, 0, cur, :]
            lse, dlt = l_ref[0, 0, cur, 0:1], dl_ref[0, 0, cur, 0:1]
            kc, vc, kp, vp = k_ref[0, 0, cur, :], v_ref[0, 0, cur, :], k_ref[0, 0, prv, :], v_ref[0, 0, prv, :]
            s_c, s_p, m_c, m_p = _att_scores(qn, kc, kp, h, dil, n > 0)
            p_c = jnp.where(m_c, jnp.exp(s_c - lse), 0.0)
            p_p = jnp.where(m_p, jnp.exp(s_p - lse), 0.0)
            ds_c = p_c * (_dot_nt(don, vc) - dlt)
            ds_p = p_p * (_dot_nt(don, vp) - dlt)
            dq_ref[0, 0, cur, :] = (_dot(ds_c, kc) + _dot(ds_p, kp)) * scale
            dv_ref[0, 0, prv, :] += _dot_tn(p_p, don)
            dk_ref[0, 0, prv, :] += _dot_tn(ds_p, qn) * scale
            dv_ref[0, 0, cur, :] += _dot_tn(p_c, don)
            dk_ref[0, 0, cur, :] += _dot_tn(ds_c, qn) * scale
            return carry

        lax.fori_loop(0, nb, blk, 0)

    spec = pl.BlockSpec((1, 1, ln, HD), lambda z, h: (z, h, 0, 0))
    return pl.pallas_call(body, grid=(dil, NH), in_specs=[spec] * 6, out_specs=[spec] * 3,
                          out_shape=[_sds(qs.shape)] * 3, name=f"attn_bwd{l}_{bi}",
                          compiler_params=_cp(("parallel", "parallel")))(qs, ks, vs, dos, lses, deltas)


def _attn_merge_fn(o1, o2, o3, l1, l2, l3):
    m = jnp.maximum(jnp.maximum(l1, l2), l3)
    w1, w2, w3 = jnp.exp(l1 - m), jnp.exp(l2 - m), jnp.exp(l3 - m)
    den = w1 + w2 + w3
    return (w1 * o1 + w2 * o2 + w3 * o3) / den, m + jnp.log(den)


def attn_merge(l, os_, ls_):
    ins = list(os_) + list(ls_)
    return _map_fwd(f"attn_merge{l}", _attn_merge_fn, (SEQ // RB,), ins, [_rows(DG)] * 6, [_sds((SEQ, DG))] * 2,
                    [_rows(DG)] * 2)


def attn_delta(l, dyb, yb, seg):
    fn = lambda d, y, s: (_dot(d * y, s),)
    return _map_fwd(f"attn_delta{l}", fn, (SEQ // RB,), [dyb, yb, seg], [_rows(DG), _rows(DG), _full((DG, DG))],
                    [_sds((SEQ, DG))], [_rows(DG)])[0]


def _to_sub(t, dil):
    return t.reshape(SEQ // dil, dil, NH, HD).transpose(1, 2, 0, 3)


def _from_sub(t):
    dil, _, ln, _ = t.shape
    return t.transpose(2, 0, 1, 3).reshape(SEQ, DG)


def _ln_fn(x, mix, w, b):
    h = ALPHA * x + mix
    mu = jnp.mean(h, axis=-1, keepdims=True)
    d = h - mu
    var = jnp.mean(d * d, axis=-1, keepdims=True)
    return (d * lax.rsqrt(var + LN_EPS) * w + b,)


def ln_fwd(name, x, mix, w, b):
    specs = [_rows(D_MODEL), _rows(D_MODEL), _full((1, D_MODEL)), _full((1, D_MODEL))]
    return _map_fwd(name, _ln_fn, (SEQ // RB,), [x, mix, w, b], specs, [_sds((SEQ, D_MODEL))], [_rows(D_MODEL)])[0]


def ln_bwd(name, x, mix, w, b, dy):
    specs = [_rows(D_MODEL), _rows(D_MODEL), _full((1, D_MODEL)), _full((1, D_MODEL))]
    return _map_bwd(name, _ln_fn, (SEQ // RB,), [x, mix, w, b], specs, [[dy]], [[_rows(D_MODEL)]], want=[1, 2, 3],
                    acc=(2, 3))


def _relu2_fn(u):
    r = jnp.maximum(u, 0.0)
    return (r * r,)


def relu2_fwd(name, u):
    return _map_fwd(name, _relu2_fn, (SEQ // RB,), [u], [_rows(D_FF)], [_sds((SEQ, D_FF))], [_rows(D_FF)])[0]


def relu2_bwd(name, u, dh):
    fn = lambda uu, g: (g * 2.0 * jnp.maximum(uu, 0.0),)
    return _map_fwd(name, fn, (SEQ // RB,), [u, dh], [_rows(D_FF)] * 2, [_sds((SEQ, D_FF))], [_rows(D_FF)])[0]


def loss_call(y, tgt):
    def fn(yy, tt):
        e = yy - tt
        part = 0.5 * jnp.sum(jnp.sum(e * e, axis=-1, keepdims=True) * (1.0 / D_MODEL), axis=0, keepdims=True)
        return e * (1.0 / D_MODEL), jnp.broadcast_to(part, (8, 128))

    return _map_fwd("loss", fn, (SEQ // RB,), [y, tgt], [_rows(D_MODEL)] * 2,
                    [_sds((SEQ, D_MODEL)), _sds((SEQ // RB * 8, 128))],
                    [_rows(D_MODEL), pl.BlockSpec((8, 128), lambda i: (i, 0))])


def layer_fwd(l, x, vfirst, wts, p):
    sv = {"x": x}
    proj = _mm(f"mm_in{l}", x, wts["w_in"], "nn", 512, 1024, 1024)
    fl = lerp_fwd(l, proj, p["mu"])
    xc = conv_fwd(l, proj, p["conv_w"], p["conv_b"])
    w, k2, v2, c, b, g = rwkv_pre_fwd(l, fl, vfirst, p)
    y_scan, states = rwkv_scan_fwd(l, fl, w, k2, v2, c, b, p)
    ya = rwkv_post_fwd(l, y_scan, fl, k2, v2, g, p)
    q_a, k_a, v_a = proj[:, C_AQ:C_AQ + DG], proj[:, C_AK:C_AK + DG], proj[:, C_AV:C_AV + DG]
    subs, outs, lses = [], [], []
    for bi, (win, dil) in enumerate(DILATED):
        qs, ks, vs = _to_sub(q_a, dil), _to_sub(k_a, dil), _to_sub(v_a, dil)
        o, lse = attn_branch_fwd(l, bi, qs, ks, vs)
        subs.append((qs, ks, vs))
        outs.append(_from_sub(o))
        lses.append(_from_sub(lse))
    yb, lse_all = attn_merge(l, outs, lses)
    yc, ssd_states = ssd_fwd(l, proj, xc, p)
    yd, hg_states = hgrn_fwd(l, proj, p)
    ycat = jnp.concatenate([ya, yb, yc, yd], axis=1)
    mix = _mm(f"mm_out{l}", ycat, wts["w_out"], "nn", 512, 1024, 1024)
    x1 = ln_fwd(f"ln1_fwd{l}", x, mix, p["ln1_w"], p["ln1_b"])
    u = _mm(f"mm_up{l}", x1, wts["w_up"], "nn", 512, 1024, 1024)
    hh = relu2_fwd(f"relu2_fwd{l}", u)
    m2 = _mm(f"mm_down{l}", hh, wts["w_down"], "nn", 512, 1024, 1024)
    x2 = ln_fwd(f"ln2_fwd{l}", x1, m2, p["ln2_w"], p["ln2_b"])
    sv.update(proj=proj, fl=fl, xc=xc, w=w, k2=k2, v2=v2, c=c, b=b, g=g, y_scan=y_scan, states=states, subs=subs,
              yb=yb, lse_all=lse_all, ssd_states=ssd_states, hg_states=hg_states, ycat=ycat, mix=mix, x1=x1, u=u, hh=hh,
              m2=m2, vfirst=vfirst)
    return x2, sv


def layer_bwd(l, dx2, dvfirst_next, sv, wts, p):
    gr = {}
    x, x1, proj, fl = sv["x"], sv["x1"], sv["proj"], sv["fl"]
    dres2, gr["ln2_w"], gr["ln2_b"] = ln_bwd(f"ln2_bwd{l}", x1, sv["m2"], p["ln2_w"], p["ln2_b"], dx2)
    dh = _mm(f"mm_down_dx{l}", dres2, wts["w_down"], "nt", 512, 1024, 1024)
    gr["w_down"] = _mm(f"mm_down_dw{l}", sv["hh"], dres2, "tn", 512, 1024, 512)
    du = relu2_bwd(f"relu2_bwd{l}", sv["u"], dh)
    dx1 = _mm(f"mm_up_dx{l}", du, wts["w_up"], "nt", 512, 1024, 1024, add=dres2, add_scale=ALPHA)
    gr["w_up"] = _mm(f"mm_up_dw{l}", x1, du, "tn", 512, 1024, 512)
    dres1, gr["ln1_w"], gr["ln1_b"] = ln_bwd(f"ln1_bwd{l}", x, sv["mix"], p["ln1_w"], p["ln1_b"], dx1)
    dycat = _mm(f"mm_out_dx{l}", dres1, wts["w_out"], "nt", 512, 1024, 1024)
    gr["w_out"] = _mm(f"mm_out_dw{l}", sv["ycat"], dres1, "tn", 512, 1024, 512)
    dya, dyb, dyc, dyd = (dycat[:, i * DG:(i + 1) * DG] for i in range(4))
    dhg4, gr["lb0"], gr["lb1"], gr["hgrn_norm_w"] = hgrn_bwd(l, proj, sv["hg_states"], dyd, p)
    dz, dxc, ddt, gr["dt_bias"], gr["a_log"], gr["ssd_d"], gr["ssd_norm_w"] = ssd_bwd(l, proj, sv["xc"], sv["ssd_states"], dyc, p)
    dxbc, gr["conv_w"], gr["conv_b"] = conv_bwd(l, proj, p["conv_w"], p["conv_b"], dxc)
    delta = attn_delta(l, dyb, sv["yb"], p["seg64"])
    dqs, dks, dvs = [], [], []
    for bi, (win, dil) in enumerate(DILATED):
        qs, ks, vs = sv["subs"][bi]
        dq, dk, dv = attn_branch_bwd(l, bi, qs, ks, vs, _to_sub(dyb, dil), _to_sub(sv["lse_all"], dil), _to_sub(delta, dil))
        dqs.append(_from_sub(dq))
        dks.append(_from_sub(dk))
        dvs.append(_from_sub(dv))
    dq_a, dk_a, dv_a = _addn(f"attn_dq{l}", *dqs), _addn(f"attn_dk{l}", *dks), _addn(f"attn_dv{l}", *dvs)
    pg = rwkv_post_bwd(l, sv["y_scan"], fl, sv["k2"], sv["v2"], sv["g"], p, dya)
    gr["lnx_w"], gr["lnx_b"], gr["r_k"] = pg["lnx_w"], pg["lnx_b"], pg["r_k"]
    dr, dw, dk, dv, dc, db = rwkv_scan_bwd(l, fl, sv["w"], sv["k2"], sv["v2"], sv["c"], sv["b"], sv["states"], pg["y"], p)
    v2_cts = [dv, pg["v2"]] + ([dvfirst_next] if dvfirst_next is not None else [])
    qg = rwkv_pre_bwd(l, fl, sv["vfirst"], p, [[dw], [dk, pg["k2"]], v2_cts, [dc], [db], [pg["g"]]])
    for nme in ("w0", "w2p", "a0", "a2p", "g2p", "k_k", "k_a", "v0", "v2p"):
        if nme in qg:
            gr[nme] = qg[nme]
    dfr = _addn(f"rwkv_dr{l}", dr, pg["fr"])
    dvres = qg["fvres"] if l > 0 else jnp.zeros((SEQ, 128), F32)
    dfl_out = jnp.concatenate([dfr, qg["fk"], qg["fv"], qg["flora"], dvres], axis=1)
    dfl_in, gr["mu"] = lerp_bwd(l, proj, p["mu"], dfl_out)
    dproj = jnp.concatenate([dfl_in[:, 0:768], dq_a, dk_a, dv_a, dz, dxbc, dhg4, dfl_in[:, 768:896], ddt,
                             dfl_in[:, 896:1024], jnp.zeros((SEQ, 128), F32)], axis=1)
    dx = _mm(f"mm_in_dx{l}", dproj, wts["w_in"], "nt", 512, 1024, 1024, add=dres1, add_scale=ALPHA)
    gr["w_in"] = _mm(f"mm_in_dw{l}", x, dproj, "tn", 512, 1024, 512)
    return dx, (qg["vfirst"] if l > 0 else None), gr


def _w_in_pad(w_in_l, w_vres):
    rows = w_in_l.shape[0]
    z = lambda n: jnp.zeros((rows, n), w_in_l.dtype)
    vres = z(128) if w_vres is None else jnp.concatenate([w_vres, z(96)], axis=1)
    return jnp.concatenate([w_in_l[:, 0:768], w_in_l[:, 896:1664], w_in_l[:, 1664:1920], w_in_l[:, 1920:2688],
                            w_in_l[:, 2692:3716], w_in_l[:, 768:896], w_in_l[:, 2688:2692], z(124), vres, z(128)], axis=1)


def _w_in_unpad(g):
    g_in = jnp.concatenate([g[:, 0:768], g[:, C_LORA:C_LORA + 128], g[:, 768:1536], g[:, C_Z:C_Z + 256],
                            g[:, C_XBC:C_XBC + 768], g[:, C_DT:C_DT + 4], g[:, C_HQ:C_HQ + 1024]], axis=1)
    return g_in, g[:, C_VRES:C_VRES + 32]


def _consts():
    i16 = jnp.arange(HGRN_CHUNK)
    pair = jnp.arange(HGRN_CHUNK * HGRN_CHUNK)
    i128 = jnp.arange(128)
    seg64 = _seg_ones(DG, HD)
    tri128 = (i128[:, None] >= i128[None, :]).astype(F32)
    return dict(
        seg64=seg64, seg64_bf16=seg64.astype(BF16),
        dmask=(jnp.arange(HD)[:, None] == (jnp.arange(DG)[None, :] % HD)).astype(F32),
        tri16=(i16[:, None] >= i16[None, :]).astype(F32),
        causal16=jnp.broadcast_to(((pair // HGRN_CHUNK) >= (pair % HGRN_CHUNK)).astype(F32)[:, None], (256, DG)),
        ones16=jnp.ones((HGRN_CHUNK, DG), F32),
        e128=((i128[:, None] == (jnp.arange(DG)[None, :] // HD)) & (i128[:, None] < NH)).astype(F32),
        tri128=tri128, tri128t=tri128.T, seg128=_seg_ones(DG, 128), ones128=jnp.ones((128, 128), F32))


def _pad_lanes(v, n):
    return jnp.concatenate([v, jnp.zeros((n - v.shape[0],), v.dtype)])[None, :]


def _layer_params(l, raw, consts):
    p = dict(consts)
    row = lambda name: raw[name][l][None, :]
    z = lambda r: jnp.zeros((r, DG), F32)
    mu_vres = raw["mu_vres"][l - 1] if l > 0 else jnp.zeros((32,), F32)
    p["mu"] = jnp.concatenate([raw["mu_shift"][l], mu_vres, jnp.zeros((96,), F32)])[None, :]
    p["conv_w"], p["conv_b"] = raw["ssd_conv_w"][l], row("ssd_conv_b")
    p["w0"], p["a0"], p["k_k"], p["k_a"] = row("rwkv_w0"), row("rwkv_a0"), row("rwkv_k_k"), row("rwkv_k_a")
    p["lnx_w"], p["lnx_b"] = row("rwkv_lnx_w"), row("rwkv_lnx_b")
    p["r_k"] = raw["rwkv_r_k"][l].reshape(1, DG)
    p["w2p"] = jnp.concatenate([raw["rwkv_w2"][l], z(96)], axis=0)
    p["a2p"] = jnp.concatenate([z(32), raw["rwkv_a2"][l], z(64)], axis=0)
    p["g2p"] = jnp.concatenate([z(64), raw["rwkv_g2"][l]], axis=0)
    if l > 0:
        p["v0"] = raw["rwkv_v0"][l - 1][None, :]
        p["v2p"] = jnp.concatenate([raw["rwkv_v2"][l - 1], z(96)], axis=0)
    p["lb0"], p["lb1"] = raw["lower_bounds"][0:1], raw["lower_bounds"][1:2]
    p["hgrn_norm_w"], p["ssd_norm_w"] = row("hgrn_norm_w"), row("ssd_norm_w")
    p["dt_bias"], p["a_log"], p["ssd_d"] = (_pad_lanes(raw[n][l], 128) for n in ("ssd_dt_bias", "ssd_A_log", "ssd_D"))
    for n in ("ln1_w", "ln1_b", "ln2_w", "ln2_b"):
        p[n] = row(n)
    return p


def _natural_grads(g0, g1):
    gs = (g0, g1)
    st = lambda key, f=lambda a: a[0]: jnp.stack([f(g[key]) for g in gs])
    out = {}
    out["lower_bounds"] = jnp.concatenate([g0["lb0"] + g1["lb0"], g0["lb1"] + g1["lb1"]], axis=0)
    out["mu_shift"] = st("mu", lambda a: a[0, :896])
    out["mu_vres"] = g1["mu"][:, 896:928]
    out["rwkv_w0"], out["rwkv_a0"], out["rwkv_k_k"], out["rwkv_k_a"] = st("w0"), st("a0"), st("k_k"), st("k_a")
    out["rwkv_w2"] = st("w2p", lambda a: a[0:32])
    out["rwkv_a2"] = st("a2p", lambda a: a[32:64])
    out["rwkv_g2"] = st("g2p", lambda a: a[64:128])
    out["rwkv_r_k"] = st("r_k", lambda a: a.reshape(NH, HD))
    out["rwkv_lnx_w"], out["rwkv_lnx_b"] = st("lnx_w"), st("lnx_b")
    out["rwkv_v0"] = g1["v0"]
    out["rwkv_v2"] = g1["v2p"][None, 0:32]
    out["ssd_conv_w"] = st("conv_w", lambda a: a)
    out["ssd_conv_b"] = st("conv_b")
    out["ssd_dt_bias"], out["ssd_A_log"], out["ssd_D"] = (st(k, lambda a: a[0, :NH]) for k in ("dt_bias", "a_log", "ssd_d"))
    out["ssd_norm_w"], out["hgrn_norm_w"] = st("ssd_norm_w"), st("hgrn_norm_w")
    for n in ("ln1_w", "ln1_b", "ln2_w", "ln2_b"):
        out[n] = st(n)
    return out


MESH_T = pl.DeviceIdType.MESH
ANY = pl.BlockSpec(memory_space=pl.ANY)


def _dev_index(px, py, pc):
    return 4 * px + 2 * py + pc


def all_gather(arrs):
    n = len(arrs)

    def body(*refs):
        ins, outs = refs[:n], refs[n:2 * n]
        send_sems, recv_sems, local_sems = refs[2 * n:]
        x, y, c = lax.axis_index("x"), lax.axis_index("y"), lax.axis_index("c")
        me, sibling = (x, y, c), (x, y, 1 - c)
        chips = [(1 - x, y), (x, 1 - y), (1 - x, 1 - y)]

        def copy(a, k, block, to, src=None):
            slot = outs[a].at[_dev_index(*block)]
            return pltpu.make_async_remote_copy(src_ref=slot if src is None else src, dst_ref=slot,
                                                send_sem=send_sems.at[a, k], recv_sem=recv_sems.at[a, k],
                                                device_id=to, device_id_type=MESH_T)

        mine = [pltpu.make_async_copy(ins[a], outs[a].at[_dev_index(*me)], local_sems.at[a]) for a in range(n)]
        for cp in mine:
            cp.start()
        first = []
        for a in range(n):
            first.append(copy(a, 0, me, sibling, src=ins[a]))
            first += [copy(a, 1 + j, me, (*chip, c), src=ins[a]) for j, chip in enumerate(chips)]
        for cp in first:
            cp.start()
        passed = []
        for j, chip in enumerate(chips):
            for a in range(n):
                copy(a, 1 + j, (*chip, c), me).wait_recv()
                fwd = copy(a, 4 + j, (*chip, c), sibling)
                fwd.start()
                passed.append(fwd)
        for a in range(n):
            copy(a, 0, sibling, me).wait_recv()
            for j, chip in enumerate(chips):
                copy(a, 4 + j, (*chip, 1 - c), me).wait_recv()
        for cp in first + passed:
            cp.wait_send()
        for cp in mine:
            cp.wait()

    return pl.pallas_call(
        body, in_specs=[ANY] * n, out_specs=[ANY] * n,
        out_shape=[_sds((N_DEV,) + a.shape, a.dtype) for a in arrs],
        scratch_shapes=[pltpu.SemaphoreType.DMA((n, 7)), pltpu.SemaphoreType.DMA((n, 7)), pltpu.SemaphoreType.DMA((n,))],
        name="all_gather")(*arrs)


def grad_exchange(send):
    def body(send_ref, recv_ref, send_sems, recv_sems, local_sem):
        x, y, c = lax.axis_index("x"), lax.axis_index("y"), lax.axis_index("c")
        me = _dev_index(x, y, c)
        mine = pltpu.make_async_copy(send_ref.at[me], recv_ref.at[me], local_sem)
        mine.start()
        rels = [(rx, ry, rc) for rx in (0, 1) for ry in (0, 1) for rc in (0, 1)][1:]
        peers = [(jnp.where(rx, 1 - x, x), jnp.where(ry, 1 - y, y), jnp.where(rc, 1 - c, c)) for rx, ry, rc in rels]

        def copy(k, peer):
            return pltpu.make_async_remote_copy(src_ref=send_ref.at[_dev_index(*peer)], dst_ref=recv_ref.at[me],
                                                send_sem=send_sems.at[k], recv_sem=recv_sems.at[k],
                                                device_id=peer, device_id_type=MESH_T)

        cps = [copy(k, peer) for k, peer in enumerate(peers)]
        for cp in cps:
            cp.start()
        for k, peer in enumerate(peers):
            pltpu.make_async_remote_copy(src_ref=send_ref.at[me], dst_ref=recv_ref.at[_dev_index(*peer)],
                                         send_sem=send_sems.at[k], recv_sem=recv_sems.at[k],
                                         device_id=peer, device_id_type=MESH_T).wait_recv()
        for cp in cps:
            cp.wait_send()
        mine.wait()

    return pl.pallas_call(
        body, in_specs=[ANY], out_specs=ANY, out_shape=_sds(send.shape, send.dtype),
        scratch_shapes=[pltpu.SemaphoreType.DMA((7,)), pltpu.SemaphoreType.DMA((7,)), pltpu.SemaphoreType.DMA],
        name="grad_exchange")(send)


ADAM_ROWS = 256


def adamw(parts, w, m, v):
    r = w.shape[0]
    c1 = 1.0 - ADAM_B1 ** ADAM_STEP
    c2 = 1.0 - ADAM_B2 ** ADAM_STEP

    def body(p_ref, w_ref, m_ref, v_ref, g_ref, d_ref, nm_ref, nv_ref):
        g = p_ref[0]
        for q in range(1, N_DEV):
            g = g + p_ref[q]
        nm = ADAM_B1 * m_ref[...] + (1.0 - ADAM_B1) * g
        nv = ADAM_B2 * v_ref[...] + (1.0 - ADAM_B2) * (g * g)
        g_ref[...] = g
        nm_ref[...] = nm
        nv_ref[...] = nv
        d_ref[...] = -ADAM_LR * ((nm / c1) / (jnp.sqrt(nv / c2) + ADAM_EPS) + ADAM_WD * w_ref[...])

    blk = pl.BlockSpec((ADAM_ROWS, PACK_W), lambda i: (i, 0))
    return pl.pallas_call(body, grid=(r // ADAM_ROWS,),
                          in_specs=[pl.BlockSpec((N_DEV, ADAM_ROWS, PACK_W), lambda i: (0, i, 0)), blk, blk, blk],
                          out_specs=[blk] * 4, out_shape=[_sds((r, PACK_W))] * 4, name="adamw",
                          compiler_params=_cp(("parallel",)))(parts, w, m, v)


BIG_ROWS = 1024 + 256 + 1024 + 1024
SMS_ROWS = 16
REP_ROWS = 24
PACK_ROWS = 3584
SMALL_SHARDED = (("rwkv_w2", (2, 32, 32)), ("rwkv_a2", (2, 32, 32)), ("rwkv_g2", (2, 64, 32)), ("rwkv_v2", (1, 32, 32)),
                 ("ssd_conv_w", (2, 4, 96)))
REPLICATED = (("lower_bounds", (2, 256)), ("mu_shift", (2, 896)), ("mu_vres", (1, 32)), ("rwkv_w0", (2, 256)),
              ("rwkv_a0", (2, 256)), ("rwkv_k_k", (2, 256)), ("rwkv_k_a", (2, 256)), ("rwkv_r_k", (2, 4, 64)),
              ("rwkv_lnx_w", (2, 256)), ("rwkv_lnx_b", (2, 256)), ("rwkv_v0", (1, 256)), ("ssd_conv_b", (2, 768)),
              ("ssd_dt_bias", (2, 4)), ("ssd_A_log", (2, 4)), ("ssd_D", (2, 4)), ("ssd_norm_w", (2, 256)),
              ("hgrn_norm_w", (2, 256)), ("ln1_w", (2, 1024)), ("ln1_b", (2, 1024)), ("ln2_w", (2, 1024)),
              ("ln2_b", (2, 1024)))


def _flat_rows(parts, rows):
    flat = jnp.concatenate([a.reshape(-1) for a in parts])
    return jnp.concatenate([flat, jnp.zeros((rows * PACK_W - flat.shape[0],), flat.dtype)]).reshape(rows, PACK_W)


def _pack_local(d):
    w_in = jnp.stack([_w_in_pad(d["w_in"][0], None), _w_in_pad(d["w_in"][1], d["w_in_vres"][0])])
    return jnp.concatenate([
        w_in.reshape(1024, PACK_W), d["w_out"].reshape(256, PACK_W), d["w_up"].reshape(1024, PACK_W),
        d["w_down"].reshape(1024, PACK_W), _flat_rows([d[n] for n, _ in SMALL_SHARDED], SMS_ROWS),
        _flat_rows([d[n] for n, _ in REPLICATED], REP_ROWS),
        jnp.zeros((PACK_ROWS - BIG_ROWS - SMS_ROWS - REP_ROWS, PACK_W), F32)], axis=0)


def _unflat(rows2d, table):
    flat, out, o = rows2d.reshape(-1), {}, 0
    for name, shape in table:
        n = 1
        for s in shape:
            n *= s
        out[name] = flat[o:o + n].reshape(shape)
        o += n
    return out


def _unpack_local(pk):
    d = {}
    w_in = pk[0:1024].reshape(2, 128, PW)
    g0, _ = _w_in_unpad(w_in[0])
    g1, gv = _w_in_unpad(w_in[1])
    d["w_in"], d["w_in_vres"] = jnp.stack([g0, g1]), gv[None]
    d["w_out"] = pk[1024:1280].reshape(2, 128, 1024)
    d["w_up"] = pk[1280:2304].reshape(2, 1024, 512)
    d["w_down"] = pk[2304:3328].reshape(2, 512, 1024)
    d.update(_unflat(pk[BIG_ROWS:BIG_ROWS + SMS_ROWS], SMALL_SHARDED))
    d.update(_unflat(pk[BIG_ROWS + SMS_ROWS:BIG_ROWS + SMS_ROWS + REP_ROWS], REPLICATED))
    return d


def _gathered_weights(gb, gs):
    w_in = gb[:, 0:1024].reshape(N_DEV, 2, 128, PW)
    w_out = gb[:, 1024:1280].reshape(N_DEV, 2, 128, 1024)
    w_up = gb[:, 1280:2304].reshape(N_DEV, 2, 1024, 512)
    w_down = gb[:, 2304:3328].reshape(N_DEV, 2, 512, 1024)
    wts = [dict(w_in=w_in[:, l].reshape(1024, PW), w_out=w_out[:, l].reshape(1024, 1024),
                w_up=w_up[:, l].transpose(1, 0, 2).reshape(1024, D_FF), w_down=w_down[:, l].reshape(D_FF, 1024))
           for l in range(DEPTH)]
    small, flat, o = {}, gs.reshape(N_DEV, -1), 0
    for name, shape in SMALL_SHARDED:
        n = shape[0] * shape[1] * shape[2]
        blk = flat[:, o:o + n].reshape((N_DEV,) + shape)
        small[name] = blk.transpose(1, 2, 0, 3).reshape(shape[0], shape[1], N_DEV * shape[2])
        o += n
    return wts, small


def _pack_send(big, small_grads):
    w_in = jnp.stack([g["w_in"].reshape(N_DEV, 128, PW) for g in big], axis=1).reshape(N_DEV, 1024, PACK_W)
    w_out = jnp.stack([g["w_out"].reshape(N_DEV, 128, 1024) for g in big], axis=1).reshape(N_DEV, 256, PACK_W)
    w_up = jnp.stack([g["w_up"].reshape(1024, N_DEV, 512).transpose(1, 0, 2) for g in big], axis=1).reshape(N_DEV, 1024, PACK_W)
    w_down = jnp.stack([g["w_down"].reshape(N_DEV, 512, 1024) for g in big], axis=1).reshape(N_DEV, 1024, PACK_W)
    sms = []
    for name, shape in SMALL_SHARDED:
        g = small_grads[name].reshape(shape[0], shape[1], N_DEV, shape[2]).transpose(2, 0, 1, 3)
        sms.append(g.reshape(N_DEV, -1))
    sms = jnp.concatenate(sms, axis=1)
    sms = jnp.concatenate([sms, jnp.zeros((N_DEV, SMS_ROWS * PACK_W - sms.shape[1]), F32)], axis=1).reshape(N_DEV, SMS_ROWS, PACK_W)
    rep = _flat_rows([small_grads[n] for n, _ in REPLICATED], REP_ROWS)
    rep = jnp.broadcast_to(rep[None], (N_DEV, REP_ROWS, PACK_W))
    pad = jnp.zeros((N_DEV, PACK_ROWS - BIG_ROWS - SMS_ROWS - REP_ROWS, PACK_W), F32)
    return jnp.concatenate([w_in, w_out, w_up, w_down, sms, rep, pad], axis=1)


def _local_step(x, tgt, wts, raw):
    consts = _consts()
    ps = [_layer_params(l, raw, consts) for l in range(DEPTH)]
    x1, sv0 = layer_fwd(0, x, None, wts[0], ps[0])
    x2, sv1 = layer_fwd(1, x1, sv0["fl"], wts[1], ps[1])
    dy, lparts = loss_call(x2, tgt)
    loss = jnp.sum(lparts[::8, 0])
    dx1, dvfirst, g1 = layer_bwd(1, dy, None, sv1, wts[1], ps[1])
    dx0, _, g0 = layer_bwd(0, dx1, dvfirst, sv0, wts[0], ps[0])
    big = [{k: g[k] for k in ("w_in", "w_out", "w_up", "w_down")} for g in (g0, g1)]
    return loss, dx0, big, _natural_grads(g0, g1)


WEIGHT_NAMES = ("lower_bounds", "w_in", "w_in_vres", "mu_shift", "mu_vres", "rwkv_w0", "rwkv_w2", "rwkv_a0", "rwkv_a2",
                "rwkv_g2", "rwkv_k_k", "rwkv_k_a", "rwkv_r_k", "rwkv_lnx_w", "rwkv_lnx_b", "rwkv_v0", "rwkv_v2",
                "ssd_conv_w", "ssd_conv_b", "ssd_dt_bias", "ssd_A_log", "ssd_D", "ssd_norm_w", "hgrn_norm_w", "w_out",
                "ln1_w", "ln1_b", "w_up", "w_down", "ln2_w", "ln2_b")


def kernel(x, lower_bounds, w_in, w_in_vres, mu_shift, mu_vres, rwkv_w0, rwkv_w2, rwkv_a0, rwkv_a2, rwkv_g2, rwkv_k_k, rwkv_k_a, rwkv_r_k, rwkv_lnx_w, rwkv_lnx_b, rwkv_v0, rwkv_v2, ssd_conv_w, ssd_conv_b, ssd_dt_bias, ssd_A_log, ssd_D, ssd_norm_w, hgrn_norm_w, w_out, ln1_w, ln1_b, w_up, w_down, ln2_w, ln2_b, loss_target, m_lower_bounds, m_w_in, m_w_in_vres, m_mu_shift, m_mu_vres, m_rwkv_w0, m_rwkv_w2, m_rwkv_a0, m_rwkv_a2, m_rwkv_g2, m_rwkv_k_k, m_rwkv_k_a, m_rwkv_r_k, m_rwkv_lnx_w, m_rwkv_lnx_b, m_rwkv_v0, m_rwkv_v2, m_ssd_conv_w, m_ssd_conv_b, m_ssd_dt_bias, m_ssd_A_log, m_ssd_D, m_ssd_norm_w, m_hgrn_norm_w, m_w_out, m_ln1_w, m_ln1_b, m_w_up, m_w_down, m_ln2_w, m_ln2_b, v_lower_bounds, v_w_in, v_w_in_vres, v_mu_shift, v_mu_vres, v_rwkv_w0, v_rwkv_w2, v_rwkv_a0, v_rwkv_a2, v_rwkv_g2, v_rwkv_k_k, v_rwkv_k_a, v_rwkv_r_k, v_rwkv_lnx_w, v_rwkv_lnx_b, v_rwkv_v0, v_rwkv_v2, v_ssd_conv_w, v_ssd_conv_b, v_ssd_dt_bias, v_ssd_A_log, v_ssd_D, v_ssd_norm_w, v_hgrn_norm_w, v_w_out, v_ln1_w, v_ln1_b, v_w_up, v_w_down, v_ln2_w, v_ln2_b):
    given = dict(locals())
    w = {n: given[n] for n in WEIGHT_NAMES}
    pw = _pack_local(w)
    pm = _pack_local({n: given["m_" + n] for n in WEIGHT_NAMES})
    pv = _pack_local({n: given["v_" + n] for n in WEIGHT_NAMES})
    gb, gs = all_gather([pw[:BIG_ROWS].astype(BF16), pw[BIG_ROWS:BIG_ROWS + SMS_ROWS]])
    wts, small_full = _gathered_weights(gb, gs)
    raw = {n: w[n] for n, _ in REPLICATED}
    raw.update(small_full)
    loss, dx, big, small_grads = _local_step(x[0], loss_target[0], wts, raw)
    recv = grad_exchange(_pack_send(big, small_grads))
    g, delta, new_m, new_v = adamw(recv, pw, pm, pv)
    loss = lax.psum(loss, ("x", "y", "c"))
    outs = [loss, dx[None]]
    for packed in (g, delta, new_m, new_v):
        d = _unpack_local(packed)
        outs += [d[n] for n in WEIGHT_NAMES]
    return tuple(outs)
```

```python
import functools

import jax
import jax.numpy as jnp
from jax import lax
from jax.experimental import pallas as pl
from jax.experimental.pallas import tpu as pltpu

F32 = jnp.float32
BF16 = jnp.bfloat16
HI = lax.Precision.HIGHEST

N_DEV = 8
SEQ = 2048
D_MODEL = 1024
D_FF = 4096
DG = 256
NH = 4
HD = 64
DEPTH = 2
ALPHA = (2.0 * DEPTH) ** 0.25
LN_EPS = 1e-5
RMS_EPS = 1e-5
GN_EPS = HD * 1e-5
IN_COLS = 3716
SSD_N = 128
SSD_CHUNK = 128
HGRN_CHUNK = 16
DILATED = ((128, 1), (512, 4), (2048, 16))

ADAM_LR, ADAM_B1, ADAM_B2, ADAM_EPS, ADAM_WD, ADAM_STEP = 0.001, 0.9, 0.999, 1e-08, 0.01, 10

PW = 4096
C_R, C_K, C_V = 0, 256, 512
C_AQ, C_AK, C_AV = 768, 1024, 1280
C_Z, C_XBC = 1536, 1792
C_HQ, C_HF, C_HI, C_HG = 2560, 2816, 3072, 3328
C_LORA, C_DT, C_VRES = 3584, 3712, 3840

RB = 256
VMEM_LIMIT = 56 * 1024 * 1024
PACK_W = 1024


def _cp(sem=None):
    return pltpu.CompilerParams(dimension_semantics=sem, vmem_limit_bytes=VMEM_LIMIT)


def _sds(shape, dt=F32):
    return jax.ShapeDtypeStruct(tuple(shape), dt)


def _rows(w, cb=0, rb=RB):
    return pl.BlockSpec((rb, w), lambda i: (i, cb))


def _full(shape):
    n = len(shape)
    return pl.BlockSpec(tuple(shape), lambda *_: (0,) * n)


def _sigmoid(x):
    return 1.0 / (1.0 + jnp.exp(-x))


def _silu(x):
    return x * _sigmoid(x)


def _softplus(x):
    return jnp.maximum(x, 0.0) + jnp.log(1.0 + jnp.exp(jnp.where(x > 0, -x, x)))


MID = lax.Precision.HIGH
NN, TN, NT = (((1,), (0,)), ((), ())), (((0,), (0,)), ((), ())), (((1,), (1,)), ((), ()))


def _dot(a, b):
    return lax.dot_general(a, b, NN, precision=MID, preferred_element_type=F32)


def _dot_tn(a, b):
    return lax.dot_general(a, b, TN, precision=MID, preferred_element_type=F32)


def _dot_nt(a, b):
    return lax.dot_general(a, b, NT, precision=MID, preferred_element_type=F32)


def _dotx(a, b):
    return lax.dot_general(a, b, NN, precision=HI, preferred_element_type=F32)


def _dotx_tn(a, b):
    return lax.dot_general(a, b, TN, precision=HI, preferred_element_type=F32)


def _seg_ones(n, seg):
    i = jnp.arange(n)
    return (i[:, None] // seg == i[None, :] // seg).astype(F32)


def _shift_down(x, s):
    row = lax.broadcasted_iota(jnp.int32, x.shape, 0)
    return jnp.where(row < s, 0.0, pltpu.roll(x, s, 0))


def _shift_up(x, s):
    n = x.shape[0]
    row = lax.broadcasted_iota(jnp.int32, x.shape, 0)
    return jnp.where(row >= n - s, 0.0, pltpu.roll(x, n - s, 0))


@functools.partial(jax.custom_vjp, nondiff_argnums=(1,))
def _tshift(x, s):
    return _shift_down(x, s)


def _tshift_fwd(x, s):
    return _shift_down(x, s), None


def _tshift_bwd(s, _, g):
    return (_shift_up(g, s),)


_tshift.defvjp(_tshift_fwd, _tshift_bwd)


def _map_fwd(name, fn, grid, ins, in_specs, out_shapes, out_specs):
    n_in = len(ins)

    def body(*refs):
        ys = fn(*[r[...] for r in refs[:n_in]])
        for r, y in zip(refs[n_in:], ys):
            r[...] = y

    return pl.pallas_call(body, grid=grid, in_specs=in_specs, out_specs=out_specs, out_shape=out_shapes,
                          name=name, compiler_params=_cp(("parallel",)))(*ins)


def _map_bwd(name, fn, grid, ins, in_specs, cts, ct_specs, want, acc=(), gout=None):
    n_in = len(ins)
    flat_cts = [c for group in cts for c in group]
    flat_specs = [s for group in ct_specs for s in group]
    n_ct = len(flat_cts)
    gout = gout or {}
    out_shapes = [gout[i][0] if i in gout else _sds(ins[i].shape) for i in want]
    out_specs = [gout[i][1] if i in gout else in_specs[i] for i in want]

    def body(*refs):
        xs = [r[...] for r in refs[:n_in]]
        cvals = [r[...] for r in refs[n_in:n_in + n_ct]]
        gouts = refs[n_in + n_ct:]
        cs, p = [], 0
        for group in cts:
            v = cvals[p]
            for q in range(1, len(group)):
                v = v + cvals[p + q]
            cs.append(v)
            p += len(group)

        def f(*wanted):
            full = list(xs)
            for i, w in zip(want, wanted):
                full[i] = w
            return tuple(fn(*full))

        _, vjp = jax.vjp(f, *[xs[i] for i in want])
        gs = vjp(tuple(cs))
        for o, i, g in zip(gouts, want, gs):
            if i in acc:
                @pl.when(pl.program_id(0) == 0)
                def _():
                    o[...] = jnp.zeros_like(o)

                o[...] += g
            else:
                o[...] = g

    sem = ("arbitrary",) if acc else ("parallel",)
    return pl.pallas_call(body, grid=grid, in_specs=list(in_specs) + flat_specs, out_specs=out_specs,
                          out_shape=out_shapes, name=name, compiler_params=_cp(sem))(*ins, *flat_cts)


def _addn(name, *arrs):
    n, c = arrs[0].shape

    def fn(*xs):
        r = xs[0]
        for x in xs[1:]:
            r = r + x
        return (r,)

    return _map_fwd(name, fn, (n // RB,), list(arrs), [_rows(c)] * len(arrs), [_sds((n, c))], [_rows(c)])[0]


def _mm(name, a, b, mode, tm, tn, tk, add=None, add_scale=1.0):
    if mode == "nn":
        (m, k), n = a.shape, b.shape[1]
    elif mode == "nt":
        (m, k), n = a.shape, b.shape[0]
    else:
        (k, m), n = a.shape, b.shape[1]
    nk = k // tk
    dn = {"nn": (((1,), (0,)), ((), ())), "nt": (((1,), (1,)), ((), ())), "tn": (((0,), (0,)), ((), ()))}[mode]

    def body(*refs):
        if add is None:
            a_ref, b_ref, o_ref, acc = refs
        else:
            a_ref, b_ref, add_ref, o_ref, acc = refs
        kk = pl.program_id(2)

        @pl.when(kk == 0)
        def _():
            acc[...] = jnp.zeros_like(acc)

        acc[...] += lax.dot_general(a_ref[...].astype(BF16), b_ref[...].astype(BF16), dn, preferred_element_type=F32)

        @pl.when(kk == nk - 1)
        def _():
            r = acc[...]
            if add is not None:
                r = r + add_scale * add_ref[...]
            o_ref[...] = r

    a_spec = pl.BlockSpec((tk, tm), lambda i, j, q: (q, i)) if mode == "tn" else pl.BlockSpec((tm, tk), lambda i, j, q: (i, q))
    b_spec = pl.BlockSpec((tn, tk), lambda i, j, q: (j, q)) if mode == "nt" else pl.BlockSpec((tk, tn), lambda i, j, q: (q, j))
    o_spec = pl.BlockSpec((tm, tn), lambda i, j, q: (i, j))
    ins, specs = [a, b], [a_spec, b_spec]
    if add is not None:
        ins.append(add)
        specs.append(o_spec)
    return pl.pallas_call(body, grid=(m // tm, n // tn, nk), in_specs=specs, out_specs=o_spec, out_shape=_sds((m, n)),
                          scratch_shapes=[pltpu.VMEM((tm, tn), F32)], name=name,
                          compiler_params=_cp(("parallel", "parallel", "arbitrary")))(*ins)


LERP_BLOCKS = (0, 1, 2, 3, 4, 5, C_LORA // 128, C_VRES // 128)


def _lerp_colmap(j):
    r = jnp.where(j < 6, j, jnp.where(j == 6, C_LORA // 128, C_VRES // 128))
    return (0, r)


def _lerp_fn(f, mu):
    return (f + (_tshift(f, 1) - f) * mu,)


def _lerp_specs():
    return [pl.BlockSpec((SEQ, 128), _lerp_colmap), pl.BlockSpec((1, 128), lambda j: (0, j))]


def lerp_fwd(l, proj, mu):
    return _map_fwd(f"lerp_fwd{l}", _lerp_fn, (8,), [proj, mu], _lerp_specs(), [_sds((SEQ, 1024))],
                    [pl.BlockSpec((SEQ, 128), lambda j: (0, j))])[0]


def lerp_bwd(l, proj, mu, dfl):
    n_in = 2

    def body(f_ref, mu_ref, g_ref, df_ref, dmu_ref):
        _, vjp = jax.vjp(_lerp_fn, f_ref[...], mu_ref[...])
        df, dmu = vjp((g_ref[...],))
        df_ref[...] = df
        dmu_ref[...] = dmu

    cspec = pl.BlockSpec((SEQ, 128), lambda j: (0, j))
    return pl.pallas_call(body, grid=(8,), in_specs=_lerp_specs() + [cspec],
                          out_specs=[cspec, pl.BlockSpec((1, 128), lambda j: (0, j))],
                          out_shape=[_sds((SEQ, 1024)), _sds((1, 1024))], name=f"lerp_bwd{l}",
                          compiler_params=_cp(("parallel",)))(proj, mu, dfl)


def _conv_fn(x, w, b):
    y = x * w[3:4, :] + _tshift(x, 1) * w[2:3, :] + _tshift(x, 2) * w[1:2, :] + _tshift(x, 3) * w[0:1, :] + b
    return (_silu(y),)


def _conv_specs():
    return [pl.BlockSpec((SEQ, 128), lambda j: (0, C_XBC // 128 + j)), pl.BlockSpec((4, 128), lambda j: (0, j)),
            pl.BlockSpec((1, 128), lambda j: (0, j))]


def conv_fwd(l, proj, w, b):
    return _map_fwd(f"conv_fwd{l}", _conv_fn, (6,), [proj, w, b], _conv_specs(), [_sds((SEQ, 768))],
                    [pl.BlockSpec((SEQ, 128), lambda j: (0, j))])[0]


def conv_bwd(l, proj, w, b, dxc):
    def body(x_ref, w_ref, b_ref, g_ref, dx_ref, dw_ref, db_ref):
        _, vjp = jax.vjp(_conv_fn, x_ref[...], w_ref[...], b_ref[...])
        dx, dw, db = vjp((g_ref[...],))
        dx_ref[...] = dx
        dw_ref[...] = dw
        db_ref[...] = db

    cspec = pl.BlockSpec((SEQ, 128), lambda j: (0, j))
    return pl.pallas_call(body, grid=(6,), in_specs=_conv_specs() + [cspec],
                          out_specs=[cspec, pl.BlockSpec((4, 128), lambda j: (0, j)), pl.BlockSpec((1, 128), lambda j: (0, j))],
                          out_shape=[_sds((SEQ, 768)), _sds((4, 768)), _sds((1, 768))], name=f"conv_bwd{l}",
                          compiler_params=_cp(("parallel",)))(proj, w, b, dxc)


def _rwkv_pre_fn(has_vres):
    def fn(fk, fv, flora, *rest):
        if has_vres:
            fvres, vfirst, w0, w2p, a0, a2p, g2p, k_k, k_a, v0, v2p, seg = rest
        else:
            w0, w2p, a0, a2p, g2p, k_k, k_a, seg = rest
        w_log = -_softplus(-(w0 + _dot(jnp.tanh(flora), w2p))) - 0.5
        w = jnp.exp(-jnp.exp(w_log))
        a = _sigmoid(a0 + _dot(flora, a2p))
        g = _dot(_sigmoid(flora), g2p)
        if has_vres:
            v2 = fv + (vfirst - fv) * _sigmoid(v0 + _dot(fvres, v2p))
        else:
            v2 = fv * 1.0
        kk = fk * k_k
        kk = kk / jnp.maximum(jnp.sqrt(_dot(kk * kk, seg)), 1e-12)
        k2 = fk * (1.0 + (a - 1.0) * k_a)
        return w, k2, v2, -kk, kk * a, g

    return fn


def _rwkv_pre_args(fl, vfirst, p, has_vres):
    ins = [fl, fl, fl]
    specs = [_rows(256, 1), _rows(256, 2), _rows(128, 6)]
    if has_vres:
        ins += [fl, vfirst]
        specs += [_rows(128, 7), _rows(256, 2)]
    names = ["w0", "w2p", "a0", "a2p", "g2p", "k_k", "k_a"] + (["v0", "v2p"] if has_vres else []) + ["seg64"]
    for nme in names:
        ins.append(p[nme])
        specs.append(_full(p[nme].shape))
    return ins, specs, names


def rwkv_pre_fwd(l, fl, vfirst, p):
    has_vres = l > 0
    ins, specs, _ = _rwkv_pre_args(fl, vfirst, p, has_vres)
    return _map_fwd(f"rwkv_pre_fwd{l}", _rwkv_pre_fn(has_vres), (SEQ // RB,), ins, specs,
                    [_sds((SEQ, DG))] * 6, [_rows(DG)] * 6)


def rwkv_pre_bwd(l, fl, vfirst, p, cts):
    has_vres = l > 0
    ins, specs, names = _rwkv_pre_args(fl, vfirst, p, has_vres)
    n_row = 5 if has_vres else 3
    want = list(range(n_row)) + [n_row + i for i, nme in enumerate(names) if nme != "seg64"]
    acc = tuple(w for w in want if w >= n_row)
    ct_specs = [[_rows(DG)] * len(g) for g in cts]
    gout = {0: (_sds((SEQ, DG)), _rows(DG)), 1: (_sds((SEQ, DG)), _rows(DG)), 2: (_sds((SEQ, 128)), _rows(128))}
    if has_vres:
        gout[3] = (_sds((SEQ, 128)), _rows(128))
        gout[4] = (_sds((SEQ, DG)), _rows(DG))
    gs = _map_bwd(f"rwkv_pre_bwd{l}", _rwkv_pre_fn(has_vres), (SEQ // RB,), ins, specs, cts, ct_specs, want, acc, gout)
    keys = ["fk", "fv", "flora"] + (["fvres", "vfirst"] if has_vres else []) + [nme for nme in names if nme != "seg64"]
    return dict(zip(keys, gs))


def _rwkv_post_fn(y, fr, k2, v2, g, lnx_w, lnx_b, r_k, seg):
    mu = _dot(y, seg) * (1.0 / HD)
    d = y - mu
    var = _dot(d * d, seg) * (1.0 / HD)
    yn = d * lax.rsqrt(var + GN_EPS) * lnx_w + lnx_b
    bonus = _dot(fr * k2 * r_k, seg) * v2
    return ((yn + bonus) * g,)


def _rwkv_post_args(y, fl, k2, v2, g, p):
    ins = [y, fl, k2, v2, g, p["lnx_w"], p["lnx_b"], p["r_k"], p["seg64"]]
    specs = [_rows(DG), _rows(DG, 0), _rows(DG), _rows(DG), _rows(DG)] + [_full(x.shape) for x in ins[5:]]
    return ins, specs


def rwkv_post_fwd(l, y, fl, k2, v2, g, p):
    ins, specs = _rwkv_post_args(y, fl, k2, v2, g, p)
    return _map_fwd(f"rwkv_post_fwd{l}", _rwkv_post_fn, (SEQ // RB,), ins, specs, [_sds((SEQ, DG))], [_rows(DG)])[0]


def rwkv_post_bwd(l, y, fl, k2, v2, g, p, dya):
    ins, specs = _rwkv_post_args(y, fl, k2, v2, g, p)
    gs = _map_bwd(f"rwkv_post_bwd{l}", _rwkv_post_fn, (SEQ // RB,), ins, specs, [[dya]], [[_rows(DG)]],
                  want=[0, 1, 2, 3, 4, 5, 6, 7], acc=(5, 6, 7), gout={1: (_sds((SEQ, DG)), _rows(DG))})
    return dict(zip(["y", "fr", "k2", "v2", "g", "lnx_w", "lnx_b", "r_k"], gs))


SCAN_TB = 64


def _coltile8(rows8, dmask, ones_stack, parts):
    pieces, rest = [], rows8
    for q in range(parts):
        piece = rest.astype(BF16).astype(F32)
        if q < parts - 1:
            rest = rest - piece
        pieces.append((piece[:, None, :] * dmask[None]).reshape(8 * HD, DG).astype(BF16))
    x = pieces[0] if parts == 1 else jnp.concatenate(pieces, axis=1)
    return jnp.dot(x, ones_stack, preferred_element_type=F32).reshape(8, HD, DG)


def _coltiles_bf16(rows_list, dmask, ones_bf16):
    x = jnp.concatenate([(r8[:, None, :] * dmask[None]).reshape(8 * HD, DG).astype(BF16) for r8 in rows_list], axis=0)
    t = jnp.dot(x, ones_bf16, preferred_element_type=F32)
    return [t[q * 8 * HD:(q + 1) * 8 * HD].reshape(8, HD, DG) for q in range(len(rows_list))]


def _segrows8(x8, dmask, ones_bf16):
    t = jnp.dot(x8.reshape(8 * HD, DG).astype(BF16), ones_bf16, preferred_element_type=F32).reshape(8, HD, DG)
    return jnp.sum(t * dmask[None], axis=1)


def rwkv_scan_fwd(l, fl, w, k2, v2, c, b, p):
    nblk = SEQ // SCAN_TB

    def body(r_ref, w_ref, k_ref, v_ref, c_ref, b_ref, ones_ref, dm_ref, y_ref, st_ref, s_sc):
        @pl.when(pl.program_id(0) == 0)
        def _():
            s_sc[...] = jnp.zeros_like(s_sc)

        ones3, ones = ones_ref[...], ones_ref[0:DG, :]
        dmask = dm_ref[...]

        def group(gi, carry):
            t0 = pl.multiple_of(gi * 8, 8)
            sl = pl.ds(t0, 8)
            v8 = v_ref[sl, :]
            wt = _coltile8(w_ref[sl, :], dmask, ones3, 3)
            ct, bt, kt, rt = _coltiles_bf16([c_ref[sl, :], b_ref[sl, :], k_ref[sl, :], r_ref[sl, :]], dmask, ones)
            t = s_sc[...]
            for j in range(8):
                sa = jnp.sum(t * ct[j], axis=0, keepdims=True)
                t = t * wt[j] + bt[j] * sa + kt[j] * v8[j:j + 1, :]
                st_ref[t0 + j] = t
            s_sc[...] = t
            y_ref[sl, :] = jnp.sum(st_ref[sl] * rt, axis=1)
            return carry

        lax.fori_loop(0, SCAN_TB // 8, group, 0)

    row = pl.BlockSpec((SCAN_TB, DG), lambda i: (i, 0))
    ins = [fl, w, k2, v2, c, b, p["seg64x3_bf16"], p["dmask"]]
    specs = [row] * 6 + [_full((3 * DG, DG)), _full((HD, DG))]
    return pl.pallas_call(body, grid=(nblk,), in_specs=specs,
                          out_specs=[row, pl.BlockSpec((SCAN_TB, HD, DG), lambda i: (i, 0, 0))],
                          out_shape=[_sds((SEQ, DG)), _sds((SEQ, HD, DG))],
                          scratch_shapes=[pltpu.VMEM((HD, DG), F32)], name=f"rwkv_scan_fwd{l}",
                          compiler_params=_cp(("arbitrary",)))(*ins)


def rwkv_scan_bwd(l, fl, w, k2, v2, c, b, states, dy, p):
    nblk = SEQ // SCAN_TB

    def body(r_ref, w_ref, k_ref, v_ref, c_ref, b_ref, dy_ref, st_ref, sp_ref, ones_ref, dm_ref,
             dr_ref, dw_ref, dk_ref, dv_ref, dc_ref, db_ref, g_sc, prev_sc, d8_sc, dsa_sc):
        i = pl.program_id(0)

        @pl.when(i == 0)
        def _():
            g_sc[...] = jnp.zeros_like(g_sc)

        ones3, ones = ones_ref[...], ones_ref[0:DG, :]
        dmask = dm_ref[...]
        first_block = i == nblk - 1

        def group(gr, carry):
            gi = SCAN_TB // 8 - 1 - gr
            t0 = pl.multiple_of(gi * 8, 8)
            sl = pl.ds(t0, 8)
            v8, dy8 = v_ref[sl, :], dy_ref[sl, :]
            t8 = st_ref[sl]
            @pl.when(gi > 0)
            def _():
                prev_sc[0] = st_ref[t0 - 1]

            @pl.when(gi == 0)
            def _():
                prev_sc[0] = jnp.where(first_block, 0.0, sp_ref[0])

            for j in range(1, 8):
                prev_sc[j] = t8[j - 1]
            tp8 = prev_sc[...]
            wt = _coltile8(w_ref[sl, :], dmask, ones3, 3)
            ct, bt, kt, rt = _coltiles_bf16([c_ref[sl, :], b_ref[sl, :], k_ref[sl, :], r_ref[sl, :]], dmask, ones)
            sa8 = jnp.sum(tp8 * ct, axis=1)
            g = g_sc[...]
            for j in range(7, -1, -1):
                g = g + rt[j] * dy8[j:j + 1, :]
                d8_sc[j] = g
                dsa = jnp.sum(g * bt[j], axis=0, keepdims=True)
                dsa_sc[j:j + 1, :] = dsa
                g = g * wt[j] + ct[j] * dsa
            g_sc[...] = g
            d8 = d8_sc[...]
            dsa8 = dsa_sc[...]
            dv_ref[sl, :] = jnp.sum(d8 * kt, axis=1)
            dr_ref[sl, :] = _segrows8(t8 * dy8[:, None, :], dmask, ones)
            dk_ref[sl, :] = _segrows8(d8 * v8[:, None, :], dmask, ones)
            dw_ref[sl, :] = _segrows8(tp8 * d8, dmask, ones)
            db_ref[sl, :] = _segrows8(d8 * sa8[:, None, :], dmask, ones)
            dc_ref[sl, :] = _segrows8(tp8 * dsa8[:, None, :], dmask, ones)
            return carry

        lax.fori_loop(0, SCAN_TB // 8, group, 0)

    row = pl.BlockSpec((SCAN_TB, DG), lambda i: (nblk - 1 - i, 0))
    st_spec = pl.BlockSpec((SCAN_TB, HD, DG), lambda i: (nblk - 1 - i, 0, 0))
    sp_spec = pl.BlockSpec((1, HD, DG), lambda i: (jnp.maximum((nblk - 1 - i) * SCAN_TB - 1, 0), 0, 0))
    ins = [fl, w, k2, v2, c, b, dy, states, states, p["seg64x3_bf16"], p["dmask"]]
    specs = [row] * 7 + [st_spec, sp_spec, _full((3 * DG, DG)), _full((HD, DG))]
    tile8 = pltpu.VMEM((8, HD, DG), F32)
    return pl.pallas_call(body, grid=(nblk,), in_specs=specs, out_specs=[row] * 6, out_shape=[_sds((SEQ, DG))] * 6,
                          scratch_shapes=[pltpu.VMEM((HD, DG), F32), tile8, tile8, pltpu.VMEM((8, DG), F32)],
                          name=f"rwkv_scan_bwd{l}", compiler_params=_cp(("arbitrary",)))(*ins)


HG_ROWS = 128


def _hgrn_chunk_fn(layer):
    def fn(hq, hf, hi, hg, sprev, lb0, lb1, norm_w, seg, bd, tri, causal, ones16):
        e0 = jnp.exp(lb0 - jnp.maximum(lb0, lb1))
        e1 = jnp.exp(lb1 - jnp.maximum(lb0, lb1))
        sm0, sm1 = e0 / (e0 + e1), e1 / (e0 + e1)
        lb = (sm0 - sm0) if layer == 0 else ((sm0 + sm1) - sm0)
        forget = lb + (1.0 - lb) * _sigmoid(hf)
        logf = jnp.log(forget)
        kk = 1.0 - forget
        q = _silu(hq)
        c = HGRN_CHUNK
        b = _dotx(tri, logf)
        bl = jnp.sum(logf, axis=0, keepdims=True)
        diff = (b[:, None, :] - b[None, :, :]).reshape(c * c, DG)
        dec = jnp.exp(jnp.where(causal > 0.5, diff, -1e30))
        qrep = jnp.broadcast_to(q[:, None, :], (c, c, DG)).reshape(c * c, DG)
        ktil = jnp.broadcast_to(kk[None, :, :], (c, c, DG)).reshape(c * c, DG)
        vtil = jnp.broadcast_to(hi[None, :, :], (c, c, DG)).reshape(c * c, DG)
        att = _dot(qrep * ktil * dec, seg)
        o_intra = jnp.sum((att * vtil).reshape(c, c, DG), axis=1)
        kdec = kk * jnp.exp(bl - b)
        u = _dot_tn(kdec, hi) * bd
        tot = jnp.exp(_dotx_tn(logf, ones16))
        snext = sprev * tot + u
        o = o_intra + _dot(q * jnp.exp(b), sprev)
        ms = _dot(o * o, seg) * (1.0 / HD)
        y = o * lax.rsqrt(ms + RMS_EPS) * norm_w * _silu(hg)
        return y, snext

    return fn


def _hgrn_consts(p):
    return [p["seg64"], p["seg64"], p["tri16"], p["causal16"], p["ones16"]]


def hgrn_fwd(l, proj, p):
    fn = _hgrn_chunk_fn(l)
    nch = HG_ROWS // HGRN_CHUNK

    def body(hq_ref, hf_ref, hi_ref, hg_ref, lb0_ref, lb1_ref, nw_ref, seg_ref, bd_ref, tri_ref, cau_ref, o16_ref,
             y_ref, st_ref, s_sc):
        @pl.when(pl.program_id(0) == 0)
        def _():
            s_sc[...] = jnp.zeros_like(s_sc)

        consts = (lb0_ref[...], lb1_ref[...], nw_ref[...], seg_ref[...], bd_ref[...], tri_ref[...], cau_ref[...],
                  o16_ref[...])

        def chunk(ci, carry):
            sl = pl.ds(pl.multiple_of(ci * HGRN_CHUNK, HGRN_CHUNK), HGRN_CHUNK)
            sprev = s_sc[...]
            st_ref[ci] = sprev
            y, snext = fn(hq_ref[sl, :], hf_ref[sl, :], hi_ref[sl, :], hg_ref[sl, :], sprev, *consts)
            y_ref[sl, :] = y
            s_sc[...] = snext
            return carry

        lax.fori_loop(0, nch, chunk, 0)

    rows = lambda cb: pl.BlockSpec((HG_ROWS, DG), lambda i: (i, cb))
    ins = [proj, proj, proj, proj, p["lb0"], p["lb1"], p["hgrn_norm_w"]] + _hgrn_consts(p)
    specs = [rows(C_HQ // DG), rows(C_HF // DG), rows(C_HI // DG), rows(C_HG // DG)] + [_full(x.shape) for x in ins[4:]]
    return pl.pallas_call(body, grid=(SEQ // HG_ROWS,), in_specs=specs,
                          out_specs=[rows(0), pl.BlockSpec((nch, DG, DG), lambda i: (i, 0, 0))],
                          out_shape=[_sds((SEQ, DG)), _sds((SEQ // HGRN_CHUNK, DG, DG))],
                          scratch_shapes=[pltpu.VMEM((DG, DG), F32)], name=f"hgrn_fwd{l}",
                          compiler_params=_cp(("arbitrary",)))(*ins)


def hgrn_bwd(l, proj, states, dy, p):
    fn = _hgrn_chunk_fn(l)
    nch = HG_ROWS // HGRN_CHUNK
    nblk = SEQ // HG_ROWS

    def body(hq_ref, hf_ref, hi_ref, hg_ref, st_ref, dy_ref, lb0_ref, lb1_ref, nw_ref, seg_ref, bd_ref, tri_ref,
             cau_ref, o16_ref, dp_ref, dlb0_ref, dlb1_ref, dnw_ref, ds_sc):
        @pl.when(pl.program_id(0) == 0)
        def _():
            ds_sc[...] = jnp.zeros_like(ds_sc)
            dlb0_ref[...] = jnp.zeros_like(dlb0_ref)
            dlb1_ref[...] = jnp.zeros_like(dlb1_ref)
            dnw_ref[...] = jnp.zeros_like(dnw_ref)

        consts = (seg_ref[...], bd_ref[...], tri_ref[...], cau_ref[...], o16_ref[...])

        def chunk(cr, carry):
            ci = nch - 1 - cr
            sl = pl.ds(pl.multiple_of(ci * HGRN_CHUNK, HGRN_CHUNK), HGRN_CHUNK)
            f = lambda hq, hf, hi, hg, sp, b0, b1, nw: fn(hq, hf, hi, hg, sp, b0, b1, nw, *consts)
            _, vjp = jax.vjp(f, hq_ref[sl, :], hf_ref[sl, :], hi_ref[sl, :], hg_ref[sl, :], st_ref[ci],
                             lb0_ref[...], lb1_ref[...], nw_ref[...])
            dhq, dhf, dhi, dhg, dsp, dlb0, dlb1, dnw = vjp((dy_ref[sl, :], ds_sc[...]))
            dp_ref[sl, 0:DG] = dhq
            dp_ref[sl, DG:2 * DG] = dhf
            dp_ref[sl, 2 * DG:3 * DG] = dhi
            dp_ref[sl, 3 * DG:4 * DG] = dhg
            ds_sc[...] = dsp
            dlb0_ref[...] += dlb0
            dlb1_ref[...] += dlb1
            dnw_ref[...] += dnw
            return carry

        lax.fori_loop(0, nch, chunk, 0)

    rows = lambda cb: pl.BlockSpec((HG_ROWS, DG), lambda i: (nblk - 1 - i, cb))
    ins = [proj, proj, proj, proj, states, dy, p["lb0"], p["lb1"], p["hgrn_norm_w"]] + _hgrn_consts(p)
    specs = [rows(C_HQ // DG), rows(C_HF // DG), rows(C_HI // DG), rows(C_HG // DG),
             pl.BlockSpec((nch, DG, DG), lambda i: (nblk - 1 - i, 0, 0)), rows(0)] + [_full(x.shape) for x in ins[6:]]
    return pl.pallas_call(body, grid=(nblk,), in_specs=specs,
                          out_specs=[pl.BlockSpec((HG_ROWS, 4 * DG), lambda i: (nblk - 1 - i, 0)), _full((1, DG)),
                                     _full((1, DG)), _full((1, DG))],
                          out_shape=[_sds((SEQ, 4 * DG)), _sds((1, DG)), _sds((1, DG)), _sds((1, DG))],
                          scratch_shapes=[pltpu.VMEM((DG, DG), F32)], name=f"hgrn_bwd{l}",
                          compiler_params=_cp(("arbitrary",)))(*ins)


def _ssd_chunk_fn(z, xs, bm, cm, dtr, sprev, dt_bias, a_log, d_par, norm_w, e128, tri, trit, seg128, ones128):
    lc = SSD_CHUNK
    dt = _softplus(dtr + dt_bias)
    a = -jnp.exp(a_log)
    da = dt * a * (lax.broadcasted_iota(jnp.int32, (1, 128), 1) < NH).astype(F32)
    cs = _dotx(tri, da)
    cst = _dotx_tn(da, trit)
    cs_b = _dotx(cs, e128)
    dt_b = _dotx(dt, e128)
    csl_b = _dotx(jnp.sum(da, axis=0, keepdims=True), e128)
    xdt = xs * dt_b
    lane = lax.broadcasted_iota(jnp.int32, (1, DG), 1)
    rowi = lax.broadcasted_iota(jnp.int32, (lc, lc), 0)
    coli = lax.broadcasted_iota(jnp.int32, (lc, lc), 1)
    y = jnp.zeros((lc, DG), F32)
    snew = jnp.zeros((DG, SSD_N), F32)
    d_b = jnp.zeros((1, DG), F32)
    wdec = xdt * jnp.exp(csl_b - cs_b)
    for g in range(2):
        bg = bm[:, g * SSD_N:(g + 1) * SSD_N]
        cg = cm[:, g * SSD_N:(g + 1) * SSD_N]
        gmat = _dot_nt(cg, bg)
        gmask = ((lane // 128) == g).astype(F32)
        snew = snew + _dot_tn(wdec * gmask, bg)
        y = y + _dot_nt(cg, sprev) * gmask * jnp.exp(cs_b)
        for hh in range(2):
            h = 2 * g + hh
            seg = jnp.where(rowi >= coli, cs[:, h:h + 1] - cst[h:h + 1, :], -1e30)
            hmask = ((lane // HD) == h).astype(F32)
            y = y + _dot(gmat * jnp.exp(seg), xdt * hmask)
            d_b = d_b + d_par[:, h:h + 1] * hmask
    cd = jnp.exp(_dotx_tn(_dotx(da, e128), ones128))
    snext = sprev * cd + snew
    y = y + xs * d_b
    y = y * _silu(z)
    ms = _dot(y * y, seg128) * (1.0 / 128.0)
    return y * lax.rsqrt(ms + RMS_EPS) * norm_w, snext


def ssd_fwd(l, proj, xc, p):
    nc = SEQ // SSD_CHUNK

    def body(z_ref, xs_ref, b_ref, c_ref, dt_ref, dtb_ref, al_ref, d_ref, nw_ref, e_ref, tri_ref, trit_ref, sg_ref,
             on_ref, y_ref, st_ref, s_sc):
        @pl.when(pl.program_id(0) == 0)
        def _():
            s_sc[...] = jnp.zeros_like(s_sc)

        sprev = s_sc[...]
        st_ref[0] = sprev
        y, snext = _ssd_chunk_fn(z_ref[...], xs_ref[...], b_ref[...], c_ref[...], dt_ref[...], sprev, dtb_ref[...],
                                 al_ref[...], d_ref[...], nw_ref[...], e_ref[...], tri_ref[...], trit_ref[...],
                                 sg_ref[...], on_ref[...])
        y_ref[...] = y
        s_sc[...] = snext

    rw = lambda w, cb: pl.BlockSpec((SSD_CHUNK, w), lambda i: (i, cb))
    ins = [proj, xc, xc, xc, proj, p["dt_bias"], p["a_log"], p["ssd_d"], p["ssd_norm_w"], p["e128"], p["tri128"],
           p["tri128t"], p["seg128"], p["ones128"]]
    specs = [rw(DG, C_Z // DG), rw(DG, 0), rw(DG, 1), rw(DG, 2), rw(128, C_DT // 128)] + [_full(x.shape) for x in ins[5:]]
    return pl.pallas_call(body, grid=(nc,), in_specs=specs,
                          out_specs=[rw(DG, 0), pl.BlockSpec((1, DG, SSD_N), lambda i: (i, 0, 0))],
                          out_shape=[_sds((SEQ, DG)), _sds((nc, DG, SSD_N))],
                          scratch_shapes=[pltpu.VMEM((DG, SSD_N), F32)], name=f"ssd_fwd{l}",
                          compiler_params=_cp(("arbitrary",)))(*ins)


def ssd_bwd(l, proj, xc, states, dy, p):
    nc = SEQ // SSD_CHUNK

    def body(z_ref, xs_ref, b_ref, c_ref, dt_ref, st_ref, dy_ref, dtb_ref, al_ref, d_ref, nw_ref, e_ref, tri_ref,
             trit_ref, sg_ref, on_ref, dz_ref, dxc_ref, ddt_ref, ddtb_ref, dal_ref, dd_ref, dnw_ref, ds_sc):
        @pl.when(pl.program_id(0) == 0)
        def _():
            ds_sc[...] = jnp.zeros_like(ds_sc)
            ddtb_ref[...] = jnp.zeros_like(ddtb_ref)
            dal_ref[...] = jnp.zeros_like(dal_ref)
            dd_ref[...] = jnp.zeros_like(dd_ref)
            dnw_ref[...] = jnp.zeros_like(dnw_ref)

        consts = (e_ref[...], tri_ref[...], trit_ref[...], sg_ref[...], on_ref[...])
        f = lambda *a: _ssd_chunk_fn(*a, *consts)
        _, vjp = jax.vjp(f, z_ref[...], xs_ref[...], b_ref[...], c_ref[...], dt_ref[...], st_ref[0], dtb_ref[...],
                         al_ref[...], d_ref[...], nw_ref[...])
        dz, dxs, db, dc, ddt, dsp, ddtb, dal, dd, dnw = vjp((dy_ref[...], ds_sc[...]))
        dz_ref[...] = dz
        dxc_ref[:, 0:DG] = dxs
        dxc_ref[:, DG:2 * DG] = db
        dxc_ref[:, 2 * DG:3 * DG] = dc
        ddt_ref[...] = ddt
        ds_sc[...] = dsp
        ddtb_ref[...] += ddtb
        dal_ref[...] += dal
        dd_ref[...] += dd
        dnw_ref[...] += dnw

    rw = lambda w, cb: pl.BlockSpec((SSD_CHUNK, w), lambda i: (nc - 1 - i, cb))
    ins = [proj, xc, xc, xc, proj, states, dy, p["dt_bias"], p["a_log"], p["ssd_d"], p["ssd_norm_w"], p["e128"],
           p["tri128"], p["tri128t"], p["seg128"], p["ones128"]]
    specs = [rw(DG, C_Z // DG), rw(DG, 0), rw(DG, 1), rw(DG, 2), rw(128, C_DT // 128),
             pl.BlockSpec((1, DG, SSD_N), lambda i: (nc - 1 - i, 0, 0)), rw(DG, 0)] + [_full(x.shape) for x in ins[7:]]
    return pl.pallas_call(body, grid=(nc,), in_specs=specs,
                          out_specs=[rw(DG, 0), rw(3 * DG, 0), rw(128, 0), _full((1, 128)), _full((1, 128)), _full((1, 128)),
                                     _full((1, DG))],
                          out_shape=[_sds((SEQ, DG)), _sds((SEQ, 3 * DG)), _sds((SEQ, 128)), _sds((1, 128)), _sds((1, 128)),
                                     _sds((1, 128)), _sds((1, DG))],
                          scratch_shapes=[pltpu.VMEM((DG, SSD_N), F32)], name=f"ssd_bwd{l}",
                          compiler_params=_cp(("arbitrary",)))(*ins)


ATT_BLK = 128


def _slope(h):
    return jnp.where(h == 0, 0.25, jnp.where(h == 1, 0.0625, jnp.where(h == 2, 0.015625, 0.00390625))).astype(F32)


def _att_scores(qn, kc, kp, h, dil, has_prev):
    i = lax.broadcasted_iota(jnp.int32, (ATT_BLK, ATT_BLK), 0)
    j = lax.broadcasted_iota(jnp.int32, (ATT_BLK, ATT_BLK), 1)
    slope = _slope(h)
    scale = HD ** -0.5
    s_c = _dot_nt(qn, kc) * scale - slope * ((i - j) * dil).astype(F32)
    s_p = _dot_nt(qn, kp) * scale - slope * ((ATT_BLK + i - j) * dil).astype(F32)
    m_c = j <= i
    m_p = jnp.logical_and(j >= i, has_prev)
    return jnp.where(m_c, s_c, -1e30), jnp.where(m_p, s_p, -1e30), m_c, m_p


def attn_branch_fwd(l, bi, qs, ks, vs):
    dil, _, ln, _ = qs.shape
    nb = ln // ATT_BLK

    def body(q_ref, k_ref, v_ref, o_ref, l_ref):
        h = pl.program_id(1)

        def blk(n, carry):
            r0 = pl.multiple_of(n * ATT_BLK, ATT_BLK)
            rp = pl.multiple_of(jnp.maximum(n - 1, 0) * ATT_BLK, ATT_BLK)
            qn = q_ref[0, 0, pl.ds(r0, ATT_BLK), :]
            kc, vc = k_ref[0, 0, pl.ds(r0, ATT_BLK), :], v_ref[0, 0, pl.ds(r0, ATT_BLK), :]
            kp, vp = k_ref[0, 0, pl.ds(rp, ATT_BLK), :], v_ref[0, 0, pl.ds(rp, ATT_BLK), :]
            s_c, s_p, m_c, m_p = _att_scores(qn, kc, kp, h, dil, n > 0)
            m = jnp.maximum(jnp.max(s_c, axis=1, keepdims=True), jnp.max(s_p, axis=1, keepdims=True))
            p_c = jnp.where(m_c, jnp.exp(s_c - m), 0.0)
            p_p = jnp.where(m_p, jnp.exp(s_p - m), 0.0)
            den = jnp.sum(p_c, axis=1, keepdims=True) + jnp.sum(p_p, axis=1, keepdims=True)
            o = (_dot(p_c, vc) + _dot(p_p, vp)) / den
            o_ref[0, 0, pl.ds(r0, ATT_BLK), :] = o
            l_ref[0, 0, pl.ds(r0, ATT_BLK), :] = jnp.broadcast_to(m + jnp.log(den), (ATT_BLK, HD))
            return carry

        lax.fori_loop(0, nb, blk, 0)

    spec = pl.BlockSpec((1, 1, ln, HD), lambda z, h: (z, h, 0, 0))
    return pl.pallas_call(body, grid=(dil, NH), in_specs=[spec] * 3, out_specs=[spec] * 2,
                          out_shape=[_sds(qs.shape)] * 2, name=f"attn_fwd{l}_{bi}",
                          compiler_params=_cp(("parallel", "parallel")))(qs, ks, vs)


def attn_branch_bwd(l, bi, qs, ks, vs, dos, lses, deltas):
    dil, _, ln, _ = qs.shape
    nb = ln // ATT_BLK
    scale = HD ** -0.5

    def body(q_ref, k_ref, v_ref, do_ref, l_ref, dl_ref, dq_ref, dk_ref, dv_ref):
        h = pl.program_id(1)
        dk_ref[...] = jnp.zeros_like(dk_ref)
        dv_ref[...] = jnp.zeros_like(dv_ref)

        def blk(n, carry):
            r0 = pl.multiple_of(n * ATT_BLK, ATT_BLK)
            rp = pl.multiple_of(jnp.maximum(n - 1, 0) * ATT_BLK, ATT_BLK)
            cur, prv = pl.ds(r0, ATT_BLK), pl.ds(rp, ATT_BLK)
            qn, don = q_ref[0, 0, cur, :], do_ref[0, 0, cur, :]
            lse, dlt = l_ref[0, 0, cur, 0:1], dl_ref[0, 0, cur, 0:1]
            kc, vc, kp, vp = k_ref[0, 0, cur, :], v_ref[0, 0, cur, :], k_ref[0, 0, prv, :], v_ref[0, 0, prv, :]
            s_c, s_p, m_c, m_p = _att_scores(qn, kc, kp, h, dil, n > 0)
            p_c = jnp.where(m_c, jnp.exp(s_c - lse), 0.0)
            p_p = jnp.where(m_p, jnp.exp(s_p - lse), 0.0)
            ds_c = p_c * (_dot_nt(don, vc) - dlt)
            ds_p = p_p * (_dot_nt(don, vp) - dlt)
            dq_ref[0, 0, cur, :] = (_dot(ds_c, kc) + _dot(ds_p, kp)) * scale
            dv_ref[0, 0, prv, :] += _dot_tn(p_p, don)
            dk_ref[0, 0, prv, :] += _dot_tn(ds_p, qn) * scale
            dv_ref[0, 0, cur, :] += _dot_tn(p_c, don)
            dk_ref[0, 0, cur, :] += _dot_tn(ds_c, qn) * scale
            return carry

        lax.fori_loop(0, nb, blk, 0)

    spec = pl.BlockSpec((1, 1, ln, HD), lambda z, h: (z, h, 0, 0))
    return pl.pallas_call(body, grid=(dil, NH), in_specs=[spec] * 6, out_specs=[spec] * 3,
                          out_shape=[_sds(qs.shape)] * 3, name=f"attn_bwd{l}_{bi}",
                          compiler_params=_cp(("parallel", "parallel")))(qs, ks, vs, dos, lses, deltas)


def _attn_merge_fn(o1, o2, o3, l1, l2, l3):
    m = jnp.maximum(jnp.maximum(l1, l2), l3)
    w1, w2, w3 = jnp.exp(l1 - m), jnp.exp(l2 - m), jnp.exp(l3 - m)
    den = w1 + w2 + w3
    return (w1 * o1 + w2 * o2 + w3 * o3) / den, m + jnp.log(den)


def attn_merge(l, os_, ls_):
    ins = list(os_) + list(ls_)
    return _map_fwd(f"attn_merge{l}", _attn_merge_fn, (SEQ // RB,), ins, [_rows(DG)] * 6, [_sds((SEQ, DG))] * 2,
                    [_rows(DG)] * 2)


def attn_delta(l, dyb, yb, seg):
    fn = lambda d, y, s: (_dot(d * y, s),)
    return _map_fwd(f"attn_delta{l}", fn, (SEQ // RB,), [dyb, yb, seg], [_rows(DG), _rows(DG), _full((DG, DG))],
                    [_sds((SEQ, DG))], [_rows(DG)])[0]


def _to_sub(t, dil):
    return t.reshape(SEQ // dil, dil, NH, HD).transpose(1, 2, 0, 3)


def _from_sub(t):
    dil, _, ln, _ = t.shape
    return t.transpose(2, 0, 1, 3).reshape(SEQ, DG)


def _ln_fn(x, mix, w, b):
    h = ALPHA * x + mix
    mu = jnp.mean(h, axis=-1, keepdims=True)
    d = h - mu
    var = jnp.mean(d * d, axis=-1, keepdims=True)
    return (d * lax.rsqrt(var + LN_EPS) * w + b,)


def ln_fwd(name, x, mix, w, b):
    specs = [_rows(D_MODEL), _rows(D_MODEL), _full((1, D_MODEL)), _full((1, D_MODEL))]
    return _map_fwd(name, _ln_fn, (SEQ // RB,), [x, mix, w, b], specs, [_sds((SEQ, D_MODEL))], [_rows(D_MODEL)])[0]


def ln_bwd(name, x, mix, w, b, dy):
    specs = [_rows(D_MODEL), _rows(D_MODEL), _full((1, D_MODEL)), _full((1, D_MODEL))]
    return _map_bwd(name, _ln_fn, (SEQ // RB,), [x, mix, w, b], specs, [[dy]], [[_rows(D_MODEL)]], want=[1, 2, 3],
                    acc=(2, 3))


def _relu2_fn(u):
    r = jnp.maximum(u, 0.0)
    return (r * r,)


def relu2_fwd(name, u):
    return _map_fwd(name, _relu2_fn, (SEQ // RB,), [u], [_rows(D_FF)], [_sds((SEQ, D_FF))], [_rows(D_FF)])[0]


def relu2_bwd(name, u, dh):
    fn = lambda uu, g: (g * 2.0 * jnp.maximum(uu, 0.0),)
    return _map_fwd(name, fn, (SEQ // RB,), [u, dh], [_rows(D_FF)] * 2, [_sds((SEQ, D_FF))], [_rows(D_FF)])[0]


def loss_call(y, tgt):
    def fn(yy, tt):
        e = yy - tt
        part = 0.5 * jnp.sum(jnp.sum(e * e, axis=-1, keepdims=True) * (1.0 / D_MODEL), axis=0, keepdims=True)
        return e * (1.0 / D_MODEL), jnp.broadcast_to(part, (8, 128))

    return _map_fwd("loss", fn, (SEQ // RB,), [y, tgt], [_rows(D_MODEL)] * 2,
                    [_sds((SEQ, D_MODEL)), _sds((SEQ // RB * 8, 128))],
                    [_rows(D_MODEL), pl.BlockSpec((8, 128), lambda i: (i, 0))])


def layer_fwd(l, x, vfirst, wts, p):
    sv = {"x": x}
    proj = _mm(f"mm_in{l}", x, wts["w_in"], "nn", 512, 1024, 1024)
    fl = lerp_fwd(l, proj, p["mu"])
    xc = conv_fwd(l, proj, p["conv_w"], p["conv_b"])
    w, k2, v2, c, b, g = rwkv_pre_fwd(l, fl, vfirst, p)
    y_scan, states = rwkv_scan_fwd(l, fl, w, k2, v2, c, b, p)
    ya = rwkv_post_fwd(l, y_scan, fl, k2, v2, g, p)
    q_a, k_a, v_a = proj[:, C_AQ:C_AQ + DG], proj[:, C_AK:C_AK + DG], proj[:, C_AV:C_AV + DG]
    subs, outs, lses = [], [], []
    for bi, (win, dil) in enumerate(DILATED):
        qs, ks, vs = _to_sub(q_a, dil), _to_sub(k_a, dil), _to_sub(v_a, dil)
        o, lse = attn_branch_fwd(l, bi, qs, ks, vs)
        subs.append((qs, ks, vs))
        outs.append(_from_sub(o))
        lses.append(_from_sub(lse))
    yb, lse_all = attn_merge(l, outs, lses)
    yc, ssd_states = ssd_fwd(l, proj, xc, p)
    yd, hg_states = hgrn_fwd(l, proj, p)
    ycat = jnp.concatenate([ya, yb, yc, yd], axis=1)
    mix = _mm(f"mm_out{l}", ycat, wts["w_out"], "nn", 512, 1024, 1024)
    x1 = ln_fwd(f"ln1_fwd{l}", x, mix, p["ln1_w"], p["ln1_b"])
    u = _mm(f"mm_up{l}", x1, wts["w_up"], "nn", 512, 1024, 1024)
    hh = relu2_fwd(f"relu2_fwd{l}", u)
    m2 = _mm(f"mm_down{l}", hh, wts["w_down"], "nn", 512, 1024, 1024)
    x2 = ln_fwd(f"ln2_fwd{l}", x1, m2, p["ln2_w"], p["ln2_b"])
    sv.update(proj=proj, fl=fl, xc=xc, w=w, k2=k2, v2=v2, c=c, b=b, g=g, y_scan=y_scan, states=states, subs=subs,
              yb=yb, lse_all=lse_all, ssd_states=ssd_states, hg_states=hg_states, ycat=ycat, mix=mix, x1=x1, u=u, hh=hh,
              m2=m2, vfirst=vfirst)
    return x2, sv


def layer_bwd(l, dx2, dvfirst_next, sv, wts, p):
    gr = {}
    x, x1, proj, fl = sv["x"], sv["x1"], sv["proj"], sv["fl"]
    dres2, gr["ln2_w"], gr["ln2_b"] = ln_bwd(f"ln2_bwd{l}", x1, sv["m2"], p["ln2_w"], p["ln2_b"], dx2)
    dh = _mm(f"mm_down_dx{l}", dres2, wts["w_down"], "nt", 512, 1024, 1024)
    gr["w_down"] = _mm(f"mm_down_dw{l}", sv["hh"], dres2, "tn", 512, 1024, 512)
    du = relu2_bwd(f"relu2_bwd{l}", sv["u"], dh)
    dx1 = _mm(f"mm_up_dx{l}", du, wts["w_up"], "nt", 512, 1024, 1024, add=dres2, add_scale=ALPHA)
    gr["w_up"] = _mm(f"mm_up_dw{l}", x1, du, "tn", 512, 1024, 512)
    dres1, gr["ln1_w"], gr["ln1_b"] = ln_bwd(f"ln1_bwd{l}", x, sv["mix"], p["ln1_w"], p["ln1_b"], dx1)
    dycat = _mm(f"mm_out_dx{l}", dres1, wts["w_out"], "nt", 512, 1024, 1024)
    gr["w_out"] = _mm(f"mm_out_dw{l}", sv["ycat"], dres1, "tn", 512, 1024, 512)
    dya, dyb, dyc, dyd = (dycat[:, i * DG:(i + 1) * DG] for i in range(4))
    dhg4, gr["lb0"], gr["lb1"], gr["hgrn_norm_w"] = hgrn_bwd(l, proj, sv["hg_states"], dyd, p)
    dz, dxc, ddt, gr["dt_bias"], gr["a_log"], gr["ssd_d"], gr["ssd_norm_w"] = ssd_bwd(l, proj, sv["xc"], sv["ssd_states"], dyc, p)
    dxbc, gr["conv_w"], gr["conv_b"] = conv_bwd(l, proj, p["conv_w"], p["conv_b"], dxc)
    delta = attn_delta(l, dyb, sv["yb"], p["seg64"])
    dqs, dks, dvs = [], [], []
    for bi, (win, dil) in enumerate(DILATED):
        qs, ks, vs = sv["subs"][bi]
        dq, dk, dv = attn_branch_bwd(l, bi, qs, ks, vs, _to_sub(dyb, dil), _to_sub(sv["lse_all"], dil), _to_sub(delta, dil))
        dqs.append(_from_sub(dq))
        dks.append(_from_sub(dk))
        dvs.append(_from_sub(dv))
    dq_a, dk_a, dv_a = _addn(f"attn_dq{l}", *dqs), _addn(f"attn_dk{l}", *dks), _addn(f"attn_dv{l}", *dvs)
    pg = rwkv_post_bwd(l, sv["y_scan"], fl, sv["k2"], sv["v2"], sv["g"], p, dya)
    gr["lnx_w"], gr["lnx_b"], gr["r_k"] = pg["lnx_w"], pg["lnx_b"], pg["r_k"]
    dr, dw, dk, dv, dc, db = rwkv_scan_bwd(l, fl, sv["w"], sv["k2"], sv["v2"], sv["c"], sv["b"], sv["states"], pg["y"], p)
    v2_cts = [dv, pg["v2"]] + ([dvfirst_next] if dvfirst_next is not None else [])
    qg = rwkv_pre_bwd(l, fl, sv["vfirst"], p, [[dw], [dk, pg["k2"]], v2_cts, [dc], [db], [pg["g"]]])
    for nme in ("w0", "w2p", "a0", "a2p", "g2p", "k_k", "k_a", "v0", "v2p"):
        if nme in qg:
            gr[nme] = qg[nme]
    dfr = _addn(f"rwkv_dr{l}", dr, pg["fr"])
    dvres = qg["fvres"] if l > 0 else jnp.zeros((SEQ, 128), F32)
    dfl_out = jnp.concatenate([dfr, qg["fk"], qg["fv"], qg["flora"], dvres], axis=1)
    dfl_in, gr["mu"] = lerp_bwd(l, proj, p["mu"], dfl_out)
    dproj = jnp.concatenate([dfl_in[:, 0:768], dq_a, dk_a, dv_a, dz, dxbc, dhg4, dfl_in[:, 768:896], ddt,
                             dfl_in[:, 896:1024], jnp.zeros((SEQ, 128), F32)], axis=1)
    dx = _mm(f"mm_in_dx{l}", dproj, wts["w_in"], "nt", 512, 1024, 1024, add=dres1, add_scale=ALPHA)
    gr["w_in"] = _mm(f"mm_in_dw{l}", x, dproj, "tn", 512, 1024, 512)
    return dx, (qg["vfirst"] if l > 0 else None), gr


def _w_in_pad(w_in_l, w_vres):
    rows = w_in_l.shape[0]
    z = lambda n: jnp.zeros((rows, n), w_in_l.dtype)
    vres = z(128) if w_vres is None else jnp.concatenate([w_vres, z(96)], axis=1)
    return jnp.concatenate([w_in_l[:, 0:768], w_in_l[:, 896:1664], w_in_l[:, 1664:1920], w_in_l[:, 1920:2688],
                            w_in_l[:, 2692:3716], w_in_l[:, 768:896], w_in_l[:, 2688:2692], z(124), vres, z(128)], axis=1)


def _w_in_unpad(g):
    g_in = jnp.concatenate([g[:, 0:768], g[:, C_LORA:C_LORA + 128], g[:, 768:1536], g[:, C_Z:C_Z + 256],
                            g[:, C_XBC:C_XBC + 768], g[:, C_DT:C_DT + 4], g[:, C_HQ:C_HQ + 1024]], axis=1)
    return g_in, g[:, C_VRES:C_VRES + 32]


def _consts():
    i16 = jnp.arange(HGRN_CHUNK)
    pair = jnp.arange(HGRN_CHUNK * HGRN_CHUNK)
    i128 = jnp.arange(128)
    seg64 = _seg_ones(DG, HD)
    tri128 = (i128[:, None] >= i128[None, :]).astype(F32)
    return dict(
        seg64=seg64, seg64x3_bf16=jnp.concatenate([seg64, seg64, seg64], axis=0).astype(BF16),
        dmask=(jnp.arange(HD)[:, None] == (jnp.arange(DG)[None, :] % HD)).astype(F32),
        tri16=(i16[:, None] >= i16[None, :]).astype(F32),
        causal16=jnp.broadcast_to(((pair // HGRN_CHUNK) >= (pair % HGRN_CHUNK)).astype(F32)[:, None], (256, DG)),
        ones16=jnp.ones((HGRN_CHUNK, DG), F32),
        e128=((i128[:, None] == (jnp.arange(DG)[None, :] // HD)) & (i128[:, None] < NH)).astype(F32),
        tri128=tri128, tri128t=tri128.T, seg128=_seg_ones(DG, 128), ones128=jnp.ones((128, 128), F32))


def _pad_lanes(v, n):
    return jnp.concatenate([v, jnp.zeros((n - v.shape[0],), v.dtype)])[None, :]


def _layer_params(l, raw, consts):
    p = dict(consts)
    row = lambda name: raw[name][l][None, :]
    z = lambda r: jnp.zeros((r, DG), F32)
    mu_vres = raw["mu_vres"][l - 1] if l > 0 else jnp.zeros((32,), F32)
    p["mu"] = jnp.concatenate([raw["mu_shift"][l], mu_vres, jnp.zeros((96,), F32)])[None, :]
    p["conv_w"], p["conv_b"] = raw["ssd_conv_w"][l], row("ssd_conv_b")
    p["w0"], p["a0"], p["k_k"], p["k_a"] = row("rwkv_w0"), row("rwkv_a0"), row("rwkv_k_k"), row("rwkv_k_a")
    p["lnx_w"], p["lnx_b"] = row("rwkv_lnx_w"), row("rwkv_lnx_b")
    p["r_k"] = raw["rwkv_r_k"][l].reshape(1, DG)
    p["w2p"] = jnp.concatenate([raw["rwkv_w2"][l], z(96)], axis=0)
    p["a2p"] = jnp.concatenate([z(32), raw["rwkv_a2"][l], z(64)], axis=0)
    p["g2p"] = jnp.concatenate([z(64), raw["rwkv_g2"][l]], axis=0)
    if l > 0:
        p["v0"] = raw["rwkv_v0"][l - 1][None, :]
        p["v2p"] = jnp.concatenate([raw["rwkv_v2"][l - 1], z(96)], axis=0)
    p["lb0"], p["lb1"] = raw["lower_bounds"][0:1], raw["lower_bounds"][1:2]
    p["hgrn_norm_w"], p["ssd_norm_w"] = row("hgrn_norm_w"), row("ssd_norm_w")
    p["dt_bias"], p["a_log"], p["ssd_d"] = (_pad_lanes(raw[n][l], 128) for n in ("ssd_dt_bias", "ssd_A_log", "ssd_D"))
    for n in ("ln1_w", "ln1_b", "ln2_w", "ln2_b"):
        p[n] = row(n)
    return p


def _natural_grads(g0, g1):
    gs = (g0, g1)
    st = lambda key, f=lambda a: a[0]: jnp.stack([f(g[key]) for g in gs])
    out = {}
    out["lower_bounds"] = jnp.concatenate([g0["lb0"] + g1["lb0"], g0["lb1"] + g1["lb1"]], axis=0)
    out["mu_shift"] = st("mu", lambda a: a[0, :896])
    out["mu_vres"] = g1["mu"][:, 896:928]
    out["rwkv_w0"], out["rwkv_a0"], out["rwkv_k_k"], out["rwkv_k_a"] = st("w0"), st("a0"), st("k_k"), st("k_a")
    out["rwkv_w2"] = st("w2p", lambda a: a[0:32])
    out["rwkv_a2"] = st("a2p", lambda a: a[32:64])
    out["rwkv_g2"] = st("g2p", lambda a: a[64:128])
    out["rwkv_r_k"] = st("r_k", lambda a: a.reshape(NH, HD))
    out["rwkv_lnx_w"], out["rwkv_lnx_b"] = st("lnx_w"), st("lnx_b")
    out["rwkv_v0"] = g1["v0"]
    out["rwkv_v2"] = g1["v2p"][None, 0:32]
    out["ssd_conv_w"] = st("conv_w", lambda a: a)
    out["ssd_conv_b"] = st("conv_b")
    out["ssd_dt_bias"], out["ssd_A_log"], out["ssd_D"] = (st(k, lambda a: a[0, :NH]) for k in ("dt_bias", "a_log", "ssd_d"))
    out["ssd_norm_w"], out["hgrn_norm_w"] = st("ssd_norm_w"), st("hgrn_norm_w")
    for n in ("ln1_w", "ln1_b", "ln2_w", "ln2_b"):
        out[n] = st(n)
    return out


MESH_T = pl.DeviceIdType.MESH
ANY = pl.BlockSpec(memory_space=pl.ANY)


def _dev_index(px, py, pc):
    return 4 * px + 2 * py + pc


def all_gather(arrs):
    n = len(arrs)

    def body(*refs):
        ins, outs = refs[:n], refs[n:2 * n]
        send_sems, recv_sems, local_sems = refs[2 * n:]
        x, y, c = lax.axis_index("x"), lax.axis_index("y"), lax.axis_index("c")
        me, sibling = (x, y, c), (x, y, 1 - c)
        chips = [(1 - x, y), (x, 1 - y), (1 - x, 1 - y)]

        def copy(a, k, block, to, src=None):
            slot = outs[a].at[_dev_index(*block)]
            return pltpu.make_async_remote_copy(src_ref=slot if src is None else src, dst_ref=slot,
                                                send_sem=send_sems.at[a, k], recv_sem=recv_sems.at[a, k],
                                                device_id=to, device_id_type=MESH_T)

        mine = [pltpu.make_async_copy(ins[a], outs[a].at[_dev_index(*me)], local_sems.at[a]) for a in range(n)]
        for cp in mine:
            cp.start()
        first = []
        for a in range(n):
            first.append(copy(a, 0, me, sibling, src=ins[a]))
            first += [copy(a, 1 + j, me, (*chip, c), src=ins[a]) for j, chip in enumerate(chips)]
        for cp in first:
            cp.start()
        passed = []
        for j, chip in enumerate(chips):
            for a in range(n):
                copy(a, 1 + j, (*chip, c), me).wait_recv()
                fwd = copy(a, 4 + j, (*chip, c), sibling)
                fwd.start()
                passed.append(fwd)
        for a in range(n):
            copy(a, 0, sibling, me).wait_recv()
            for j, chip in enumerate(chips):
                copy(a, 4 + j, (*chip, 1 - c), me).wait_recv()
        for cp in first + passed:
            cp.wait_send()
        for cp in mine:
            cp.wait()

    return pl.pallas_call(
        body, in_specs=[ANY] * n, out_specs=[ANY] * n,
        out_shape=[_sds((N_DEV,) + a.shape, a.dtype) for a in arrs],
        scratch_shapes=[pltpu.SemaphoreType.DMA((n, 7)), pltpu.SemaphoreType.DMA((n, 7)), pltpu.SemaphoreType.DMA((n,))],
        name="all_gather")(*arrs)


def grad_exchange(send):
    def body(send_ref, recv_ref, send_sems, recv_sems, local_sem):
        x, y, c = lax.axis_index("x"), lax.axis_index("y"), lax.axis_index("c")
        me = _dev_index(x, y, c)
        mine = pltpu.make_async_copy(send_ref.at[me], recv_ref.at[me], local_sem)
        mine.start()
        rels = [(rx, ry, rc) for rx in (0, 1) for ry in (0, 1) for rc in (0, 1)][1:]
        peers = [(jnp.where(rx, 1 - x, x), jnp.where(ry, 1 - y, y), jnp.where(rc, 1 - c, c)) for rx, ry, rc in rels]

        def copy(k, peer):
            return pltpu.make_async_remote_copy(src_ref=send_ref.at[_dev_index(*peer)], dst_ref=recv_ref.at[me],
                                                send_sem=send_sems.at[k], recv_sem=recv_sems.at[k],
                                                device_id=peer, device_id_type=MESH_T)

        cps = [copy(k, peer) for k, peer in enumerate(peers)]
        for cp in cps:
            cp.start()
        for k, peer in enumerate(peers):
            pltpu.make_async_remote_copy(src_ref=send_ref.at[me], dst_ref=recv_ref.at[_dev_index(*peer)],
                                         send_sem=send_sems.at[k], recv_sem=recv_sems.at[k],
                                         device_id=peer, device_id_type=MESH_T).wait_recv()
        for cp in cps:
            cp.wait_send()
        mine.wait()

    return pl.pallas_call(
        body, in_specs=[ANY], out_specs=ANY, out_shape=_sds(send.shape, send.dtype),
        scratch_shapes=[pltpu.SemaphoreType.DMA((7,)), pltpu.SemaphoreType.DMA((7,)), pltpu.SemaphoreType.DMA],
        name="grad_exchange")(send)


ADAM_ROWS = 256


def adamw(parts, w, m, v):
    r = w.shape[0]
    c1 = 1.0 - ADAM_B1 ** ADAM_STEP
    c2 = 1.0 - ADAM_B2 ** ADAM_STEP

    def body(p_ref, w_ref, m_ref, v_ref, g_ref, d_ref, nm_ref, nv_ref):
        g = p_ref[0]
        for q in range(1, N_DEV):
            g = g + p_ref[q]
        nm = ADAM_B1 * m_ref[...] + (1.0 - ADAM_B1) * g
        nv = ADAM_B2 * v_ref[...] + (1.0 - ADAM_B2) * (g * g)
        g_ref[...] = g
        nm_ref[...] = nm
        nv_ref[...] = nv
        d_ref[...] = -ADAM_LR * ((nm / c1) / (jnp.sqrt(nv / c2) + ADAM_EPS) + ADAM_WD * w_ref[...])

    blk = pl.BlockSpec((ADAM_ROWS, PACK_W), lambda i: (i, 0))
    return pl.pallas_call(body, grid=(r // ADAM_ROWS,),
                          in_specs=[pl.BlockSpec((N_DEV, ADAM_ROWS, PACK_W), lambda i: (0, i, 0)), blk, blk, blk],
                          out_specs=[blk] * 4, out_shape=[_sds((r, PACK_W))] * 4, name="adamw",
                          compiler_params=_cp(("parallel",)))(parts, w, m, v)


BIG_ROWS = 1024 + 256 + 1024 + 1024
SMS_ROWS = 16
REP_ROWS = 24
PACK_ROWS = 3584
SMALL_SHARDED = (("rwkv_w2", (2, 32, 32)), ("rwkv_a2", (2, 32, 32)), ("rwkv_g2", (2, 64, 32)), ("rwkv_v2", (1, 32, 32)),
                 ("ssd_conv_w", (2, 4, 96)))
REPLICATED = (("lower_bounds", (2, 256)), ("mu_shift", (2, 896)), ("mu_vres", (1, 32)), ("rwkv_w0", (2, 256)),
              ("rwkv_a0", (2, 256)), ("rwkv_k_k", (2, 256)), ("rwkv_k_a", (2, 256)), ("rwkv_r_k", (2, 4, 64)),
              ("rwkv_lnx_w", (2, 256)), ("rwkv_lnx_b", (2, 256)), ("rwkv_v0", (1, 256)), ("ssd_conv_b", (2, 768)),
              ("ssd_dt_bias", (2, 4)), ("ssd_A_log", (2, 4)), ("ssd_D", (2, 4)), ("ssd_norm_w", (2, 256)),
              ("hgrn_norm_w", (2, 256)), ("ln1_w", (2, 1024)), ("ln1_b", (2, 1024)), ("ln2_w", (2, 1024)),
              ("ln2_b", (2, 1024)))


def _flat_rows(parts, rows):
    flat = jnp.concatenate([a.reshape(-1) for a in parts])
    return jnp.concatenate([flat, jnp.zeros((rows * PACK_W - flat.shape[0],), flat.dtype)]).reshape(rows, PACK_W)


def _pack_local(d):
    w_in = jnp.stack([_w_in_pad(d["w_in"][0], None), _w_in_pad(d["w_in"][1], d["w_in_vres"][0])])
    return jnp.concatenate([
        w_in.reshape(1024, PACK_W), d["w_out"].reshape(256, PACK_W), d["w_up"].reshape(1024, PACK_W),
        d["w_down"].reshape(1024, PACK_W), _flat_rows([d[n] for n, _ in SMALL_SHARDED], SMS_ROWS),
        _flat_rows([d[n] for n, _ in REPLICATED], REP_ROWS),
        jnp.zeros((PACK_ROWS - BIG_ROWS - SMS_ROWS - REP_ROWS, PACK_W), F32)], axis=0)


def _unflat(rows2d, table):
    flat, out, o = rows2d.reshape(-1), {}, 0
    for name, shape in table:
        n = 1
        for s in shape:
            n *= s
        out[name] = flat[o:o + n].reshape(shape)
        o += n
    return out


def _unpack_local(pk):
    d = {}
    w_in = pk[0:1024].reshape(2, 128, PW)
    g0, _ = _w_in_unpad(w_in[0])
    g1, gv = _w_in_unpad(w_in[1])
    d["w_in"], d["w_in_vres"] = jnp.stack([g0, g1]), gv[None]
    d["w_out"] = pk[1024:1280].reshape(2, 128, 1024)
    d["w_up"] = pk[1280:2304].reshape(2, 1024, 512)
    d["w_down"] = pk[2304:3328].reshape(2, 512, 1024)
    d.update(_unflat(pk[BIG_ROWS:BIG_ROWS + SMS_ROWS], SMALL_SHARDED))
    d.update(_unflat(pk[BIG_ROWS + SMS_ROWS:BIG_ROWS + SMS_ROWS + REP_ROWS], REPLICATED))
    return d


def _gathered_weights(gb, gs):
    w_in = gb[:, 0:1024].reshape(N_DEV, 2, 128, PW)
    w_out = gb[:, 1024:1280].reshape(N_DEV, 2, 128, 1024)
    w_up = gb[:, 1280:2304].reshape(N_DEV, 2, 1024, 512)
    w_down = gb[:, 2304:3328].reshape(N_DEV, 2, 512, 1024)
    wts = [dict(w_in=w_in[:, l].reshape(1024, PW), w_out=w_out[:, l].reshape(1024, 1024),
                w_up=w_up[:, l].transpose(1, 0, 2).reshape(1024, D_FF), w_down=w_down[:, l].reshape(D_FF, 1024))
           for l in range(DEPTH)]
    small, flat, o = {}, gs.reshape(N_DEV, -1), 0
    for name, shape in SMALL_SHARDED:
        n = shape[0] * shape[1] * shape[2]
        blk = flat[:, o:o + n].reshape((N_DEV,) + shape)
        small[name] = blk.transpose(1, 2, 0, 3).reshape(shape[0], shape[1], N_DEV * shape[2])
        o += n
    return wts, small


def _pack_send(big, small_grads):
    w_in = jnp.stack([g["w_in"].reshape(N_DEV, 128, PW) for g in big], axis=1).reshape(N_DEV, 1024, PACK_W)
    w_out = jnp.stack([g["w_out"].reshape(N_DEV, 128, 1024) for g in big], axis=1).reshape(N_DEV, 256, PACK_W)
    w_up = jnp.stack([g["w_up"].reshape(1024, N_DEV, 512).transpose(1, 0, 2) for g in big], axis=1).reshape(N_DEV, 1024, PACK_W)
    w_down = jnp.stack([g["w_down"].reshape(N_DEV, 512, 1024) for g in big], axis=1).reshape(N_DEV, 1024, PACK_W)
    sms = []
    for name, shape in SMALL_SHARDED:
        g = small_grads[name].reshape(shape[0], shape[1], N_DEV, shape[2]).transpose(2, 0, 1, 3)
        sms.append(g.reshape(N_DEV, -1))
    sms = jnp.concatenate(sms, axis=1)
    sms = jnp.concatenate([sms, jnp.zeros((N_DEV, SMS_ROWS * PACK_W - sms.shape[1]), F32)], axis=1).reshape(N_DEV, SMS_ROWS, PACK_W)
    rep = _flat_rows([small_grads[n] for n, _ in REPLICATED], REP_ROWS)
    rep = jnp.broadcast_to(rep[None], (N_DEV, REP_ROWS, PACK_W))
    pad = jnp.zeros((N_DEV, PACK_ROWS - BIG_ROWS - SMS_ROWS - REP_ROWS, PACK_W), F32)
    return jnp.concatenate([w_in, w_out, w_up, w_down, sms, rep, pad], axis=1)


def _local_step(x, tgt, wts, raw):
    consts = _consts()
    ps = [_layer_params(l, raw, consts) for l in range(DEPTH)]
    x1, sv0 = layer_fwd(0, x, None, wts[0], ps[0])
    x2, sv1 = layer_fwd(1, x1, sv0["fl"], wts[1], ps[1])
    dy, lparts = loss_call(x2, tgt)
    loss = jnp.sum(lparts[::8, 0])
    dx1, dvfirst, g1 = layer_bwd(1, dy, None, sv1, wts[1], ps[1])
    dx0, _, g0 = layer_bwd(0, dx1, dvfirst, sv0, wts[0], ps[0])
    big = [{k: g[k] for k in ("w_in", "w_out", "w_up", "w_down")} for g in (g0, g1)]
    return loss, dx0, big, _natural_grads(g0, g1)


WEIGHT_NAMES = ("lower_bounds", "w_in", "w_in_vres", "mu_shift", "mu_vres", "rwkv_w0", "rwkv_w2", "rwkv_a0", "rwkv_a2",
                "rwkv_g2", "rwkv_k_k", "rwkv_k_a", "rwkv_r_k", "rwkv_lnx_w", "rwkv_lnx_b", "rwkv_v0", "rwkv_v2",
                "ssd_conv_w", "ssd_conv_b", "ssd_dt_bias", "ssd_A_log", "ssd_D", "ssd_norm_w", "hgrn_norm_w", "w_out",
                "ln1_w", "ln1_b", "w_up", "w_down", "ln2_w", "ln2_b")


def kernel(x, lower_bounds, w_in, w_in_vres, mu_shift, mu_vres, rwkv_w0, rwkv_w2, rwkv_a0, rwkv_a2, rwkv_g2, rwkv_k_k, rwkv_k_a, rwkv_r_k, rwkv_lnx_w, rwkv_lnx_b, rwkv_v0, rwkv_v2, ssd_conv_w, ssd_conv_b, ssd_dt_bias, ssd_A_log, ssd_D, ssd_norm_w, hgrn_norm_w, w_out, ln1_w, ln1_b, w_up, w_down, ln2_w, ln2_b, loss_target, m_lower_bounds, m_w_in, m_w_in_vres, m_mu_shift, m_mu_vres, m_rwkv_w0, m_rwkv_w2, m_rwkv_a0, m_rwkv_a2, m_rwkv_g2, m_rwkv_k_k, m_rwkv_k_a, m_rwkv_r_k, m_rwkv_lnx_w, m_rwkv_lnx_b, m_rwkv_v0, m_rwkv_v2, m_ssd_conv_w, m_ssd_conv_b, m_ssd_dt_bias, m_ssd_A_log, m_ssd_D, m_ssd_norm_w, m_hgrn_norm_w, m_w_out, m_ln1_w, m_ln1_b, m_w_up, m_w_down, m_ln2_w, m_ln2_b, v_lower_bounds, v_w_in, v_w_in_vres, v_mu_shift, v_mu_vres, v_rwkv_w0, v_rwkv_w2, v_rwkv_a0, v_rwkv_a2, v_rwkv_g2, v_rwkv_k_k, v_rwkv_k_a, v_rwkv_r_k, v_rwkv_lnx_w, v_rwkv_lnx_b, v_rwkv_v0, v_rwkv_v2, v_ssd_conv_w, v_ssd_conv_b, v_ssd_dt_bias, v_ssd_A_log, v_ssd_D, v_ssd_norm_w, v_hgrn_norm_w, v_w_out, v_ln1_w, v_ln1_b, v_w_up, v_w_down, v_ln2_w, v_ln2_b):
    given = dict(locals())
    w = {n: given[n] for n in WEIGHT_NAMES}
    pw = _pack_local(w)
    pm = _pack_local({n: given["m_" + n] for n in WEIGHT_NAMES})
    pv = _pack_local({n: given["v_" + n] for n in WEIGHT_NAMES})
    gb, gs = all_gather([pw[:BIG_ROWS].astype(BF16), pw[BIG_ROWS:BIG_ROWS + SMS_ROWS]])
    wts, small_full = _gathered_weights(gb, gs)
    raw = {n: w[n] for n, _ in REPLICATED}
    raw.update(small_full)
    loss, dx, big, small_grads = _local_step(x[0], loss_target[0], wts, raw)
    recv = grad_exchange(_pack_send(big, small_grads))
    g, delta, new_m, new_v = adamw(recv, pw, pm, pv)
    loss = lax.psum(loss, ("x", "y", "c"))
    outs = [loss, dx[None]]
    for packed in (g, delta, new_m, new_v):
        d = _unpack_local(packed)
        outs += [d[n] for n in WEIGHT_NAMES]
    return tuple(outs)
```

```python
import functools

import jax
import jax.numpy as jnp
from jax import lax
from jax.experimental import pallas as pl
from jax.experimental.pallas import tpu as pltpu

F32 = jnp.float32
BF16 = jnp.bfloat16
HI = lax.Precision.HIGHEST

N_DEV = 8
SEQ = 2048
D_MODEL = 1024
D_FF = 4096
DG = 256
NH = 4
HD = 64
DEPTH = 2
ALPHA = (2.0 * DEPTH) ** 0.25
LN_EPS = 1e-5
RMS_EPS = 1e-5
GN_EPS = HD * 1e-5
IN_COLS = 3716
SSD_N = 128
SSD_CHUNK = 128
HGRN_CHUNK = 16
DILATED = ((128, 1), (512, 4), (2048, 16))

ADAM_LR, ADAM_B1, ADAM_B2, ADAM_EPS, ADAM_WD, ADAM_STEP = 0.001, 0.9, 0.999, 1e-08, 0.01, 10

PW = 4096
C_R, C_K, C_V = 0, 256, 512
C_AQ, C_AK, C_AV = 768, 1024, 1280
C_Z, C_XBC = 1536, 1792
C_HQ, C_HF, C_HI, C_HG = 2560, 2816, 3072, 3328
C_LORA, C_DT, C_VRES = 3584, 3712, 3840

RB = 256
VMEM_LIMIT = 56 * 1024 * 1024
PACK_W = 1024


def _cp(sem=None):
    return pltpu.CompilerParams(dimension_semantics=sem, vmem_limit_bytes=VMEM_LIMIT)


def _sds(shape, dt=F32):
    return jax.ShapeDtypeStruct(tuple(shape), dt)


def _rows(w, cb=0, rb=RB):
    return pl.BlockSpec((rb, w), lambda i: (i, cb))


def _full(shape):
    n = len(shape)
    return pl.BlockSpec(tuple(shape), lambda *_: (0,) * n)


def _sigmoid(x):
    return 1.0 / (1.0 + jnp.exp(-x))


def _silu(x):
    return x * _sigmoid(x)


def _softplus(x):
    return jnp.maximum(x, 0.0) + jnp.log(1.0 + jnp.exp(jnp.where(x > 0, -x, x)))


MID = lax.Precision.HIGH
NN, TN, NT = (((1,), (0,)), ((), ())), (((0,), (0,)), ((), ())), (((1,), (1,)), ((), ()))


def _dot(a, b):
    return lax.dot_general(a, b, NN, precision=MID, preferred_element_type=F32)


def _dot_tn(a, b):
    return lax.dot_general(a, b, TN, precision=MID, preferred_element_type=F32)


def _dot_nt(a, b):
    return lax.dot_general(a, b, NT, precision=MID, preferred_element_type=F32)


def _dotx(a, b):
    return lax.dot_general(a, b, NN, precision=HI, preferred_element_type=F32)


def _dotx_tn(a, b):
    return lax.dot_general(a, b, TN, precision=HI, preferred_element_type=F32)


def _seg_ones(n, seg):
    i = jnp.arange(n)
    return (i[:, None] // seg == i[None, :] // seg).astype(F32)


def _shift_down(x, s):
    row = lax.broadcasted_iota(jnp.int32, x.shape, 0)
    return jnp.where(row < s, 0.0, pltpu.roll(x, s, 0))


def _shift_up(x, s):
    n = x.shape[0]
    row = lax.broadcasted_iota(jnp.int32, x.shape, 0)
    return jnp.where(row >= n - s, 0.0, pltpu.roll(x, n - s, 0))


@functools.partial(jax.custom_vjp, nondiff_argnums=(1,))
def _tshift(x, s):
    return _shift_down(x, s)


def _tshift_fwd(x, s):
    return _shift_down(x, s), None


def _tshift_bwd(s, _, g):
    return (_shift_up(g, s),)


_tshift.defvjp(_tshift_fwd, _tshift_bwd)


def _map_fwd(name, fn, grid, ins, in_specs, out_shapes, out_specs):
    n_in = len(ins)

    def body(*refs):
        ys = fn(*[r[...] for r in refs[:n_in]])
        for r, y in zip(refs[n_in:], ys):
            r[...] = y

    return pl.pallas_call(body, grid=grid, in_specs=in_specs, out_specs=out_specs, out_shape=out_shapes,
                          name=name, compiler_params=_cp(("parallel",)))(*ins)


def _map_bwd(name, fn, grid, ins, in_specs, cts, ct_specs, want, acc=(), gout=None):
    n_in = len(ins)
    flat_cts = [c for group in cts for c in group]
    flat_specs = [s for group in ct_specs for s in group]
    n_ct = len(flat_cts)
    gout = gout or {}
    out_shapes = [gout[i][0] if i in gout else _sds(ins[i].shape) for i in want]
    out_specs = [gout[i][1] if i in gout else in_specs[i] for i in want]

    def body(*refs):
        xs = [r[...] for r in refs[:n_in]]
        cvals = [r[...] for r in refs[n_in:n_in + n_ct]]
        gouts = refs[n_in + n_ct:]
        cs, p = [], 0
        for group in cts:
            v = cvals[p]
            for q in range(1, len(group)):
                v = v + cvals[p + q]
            cs.append(v)
            p += len(group)

        def f(*wanted):
            full = list(xs)
            for i, w in zip(want, wanted):
                full[i] = w
            return tuple(fn(*full))

        _, vjp = jax.vjp(f, *[xs[i] for i in want])
        gs = vjp(tuple(cs))
        for o, i, g in zip(gouts, want, gs):
            if i in acc:
                @pl.when(pl.program_id(0) == 0)
                def _():
                    o[...] = jnp.zeros_like(o)

                o[...] += g
            else:
                o[...] = g

    sem = ("arbitrary",) if acc else ("parallel",)
    return pl.pallas_call(body, grid=grid, in_specs=list(in_specs) + flat_specs, out_specs=out_specs,
                          out_shape=out_shapes, name=name, compiler_params=_cp(sem))(*ins, *flat_cts)


def _addn(name, *arrs):
    n, c = arrs[0].shape

    def fn(*xs):
        r = xs[0]
        for x in xs[1:]:
            r = r + x
        return (r,)

    return _map_fwd(name, fn, (n // RB,), list(arrs), [_rows(c)] * len(arrs), [_sds((n, c))], [_rows(c)])[0]


def _mm(name, a, b, mode, tm, tn, tk, add=None, add_scale=1.0):
    if mode == "nn":
        (m, k), n = a.shape, b.shape[1]
    elif mode == "nt":
        (m, k), n = a.shape, b.shape[0]
    else:
        (k, m), n = a.shape, b.shape[1]
    nk = k // tk
    dn = {"nn": (((1,), (0,)), ((), ())), "nt": (((1,), (1,)), ((), ())), "tn": (((0,), (0,)), ((), ()))}[mode]

    def body(*refs):
        if add is None:
            a_ref, b_ref, o_ref, acc = refs
        else:
            a_ref, b_ref, add_ref, o_ref, acc = refs
        kk = pl.program_id(2)

        @pl.when(kk == 0)
        def _():
            acc[...] = jnp.zeros_like(acc)

        acc[...] += lax.dot_general(a_ref[...].astype(BF16), b_ref[...].astype(BF16), dn, preferred_element_type=F32)

        @pl.when(kk == nk - 1)
        def _():
            r = acc[...]
            if add is not None:
                r = r + add_scale * add_ref[...]
            o_ref[...] = r

    a_spec = pl.BlockSpec((tk, tm), lambda i, j, q: (q, i)) if mode == "tn" else pl.BlockSpec((tm, tk), lambda i, j, q: (i, q))
    b_spec = pl.BlockSpec((tn, tk), lambda i, j, q: (j, q)) if mode == "nt" else pl.BlockSpec((tk, tn), lambda i, j, q: (q, j))
    o_spec = pl.BlockSpec((tm, tn), lambda i, j, q: (i, j))
    ins, specs = [a, b], [a_spec, b_spec]
    if add is not None:
        ins.append(add)
        specs.append(o_spec)
    return pl.pallas_call(body, grid=(m // tm, n // tn, nk), in_specs=specs, out_specs=o_spec, out_shape=_sds((m, n)),
                          scratch_shapes=[pltpu.VMEM((tm, tn), F32)], name=name,
                          compiler_params=_cp(("parallel", "parallel", "arbitrary")))(*ins)


LERP_BLOCKS = (0, 1, 2, 3, 4, 5, C_LORA // 128, C_VRES // 128)


def _lerp_colmap(j):
    r = jnp.where(j < 6, j, jnp.where(j == 6, C_LORA // 128, C_VRES // 128))
    return (0, r)


def _lerp_fn(f, mu):
    return (f + (_tshift(f, 1) - f) * mu,)


def _lerp_specs():
    return [pl.BlockSpec((SEQ, 128), _lerp_colmap), pl.BlockSpec((1, 128), lambda j: (0, j))]


def lerp_fwd(l, proj, mu):
    return _map_fwd(f"lerp_fwd{l}", _lerp_fn, (8,), [proj, mu], _lerp_specs(), [_sds((SEQ, 1024))],
                    [pl.BlockSpec((SEQ, 128), lambda j: (0, j))])[0]


def lerp_bwd(l, proj, mu, dfl):
    n_in = 2

    def body(f_ref, mu_ref, g_ref, df_ref, dmu_ref):
        _, vjp = jax.vjp(_lerp_fn, f_ref[...], mu_ref[...])
        df, dmu = vjp((g_ref[...],))
        df_ref[...] = df
        dmu_ref[...] = dmu

    cspec = pl.BlockSpec((SEQ, 128), lambda j: (0, j))
    return pl.pallas_call(body, grid=(8,), in_specs=_lerp_specs() + [cspec],
                          out_specs=[cspec, pl.BlockSpec((1, 128), lambda j: (0, j))],
                          out_shape=[_sds((SEQ, 1024)), _sds((1, 1024))], name=f"lerp_bwd{l}",
                          compiler_params=_cp(("parallel",)))(proj, mu, dfl)


def _conv_fn(x, w, b):
    y = x * w[3:4, :] + _tshift(x, 1) * w[2:3, :] + _tshift(x, 2) * w[1:2, :] + _tshift(x, 3) * w[0:1, :] + b
    return (_silu(y),)


def _conv_specs():
    return [pl.BlockSpec((SEQ, 128), lambda j: (0, C_XBC // 128 + j)), pl.BlockSpec((4, 128), lambda j: (0, j)),
            pl.BlockSpec((1, 128), lambda j: (0, j))]


def conv_fwd(l, proj, w, b):
    return _map_fwd(f"conv_fwd{l}", _conv_fn, (6,), [proj, w, b], _conv_specs(), [_sds((SEQ, 768))],
                    [pl.BlockSpec((SEQ, 128), lambda j: (0, j))])[0]


def conv_bwd(l, proj, w, b, dxc):
    def body(x_ref, w_ref, b_ref, g_ref, dx_ref, dw_ref, db_ref):
        _, vjp = jax.vjp(_conv_fn, x_ref[...], w_ref[...], b_ref[...])
        dx, dw, db = vjp((g_ref[...],))
        dx_ref[...] = dx
        dw_ref[...] = dw
        db_ref[...] = db

    cspec = pl.BlockSpec((SEQ, 128), lambda j: (0, j))
    return pl.pallas_call(body, grid=(6,), in_specs=_conv_specs() + [cspec],
                          out_specs=[cspec, pl.BlockSpec((4, 128), lambda j: (0, j)), pl.BlockSpec((1, 128), lambda j: (0, j))],
                          out_shape=[_sds((SEQ, 768)), _sds((4, 768)), _sds((1, 768))], name=f"conv_bwd{l}",
                          compiler_params=_cp(("parallel",)))(proj, w, b, dxc)


def _rwkv_pre_fn(has_vres):
    def fn(fk, fv, flora, *rest):
        if has_vres:
            fvres, vfirst, w0, w2p, a0, a2p, g2p, k_k, k_a, v0, v2p, seg = rest
        else:
            w0, w2p, a0, a2p, g2p, k_k, k_a, seg = rest
        w_log = -_softplus(-(w0 + _dot(jnp.tanh(flora), w2p))) - 0.5
        w = jnp.exp(-jnp.exp(w_log))
        a = _sigmoid(a0 + _dot(flora, a2p))
        g = _dot(_sigmoid(flora), g2p)
        if has_vres:
            v2 = fv + (vfirst - fv) * _sigmoid(v0 + _dot(fvres, v2p))
        else:
            v2 = fv * 1.0
        kk = fk * k_k
        kk = kk / jnp.maximum(jnp.sqrt(_dot(kk * kk, seg)), 1e-12)
        k2 = fk * (1.0 + (a - 1.0) * k_a)
        return w, k2, v2, -kk, kk * a, g

    return fn


def _rwkv_pre_args(fl, vfirst, p, has_vres):
    ins = [fl, fl, fl]
    specs = [_rows(256, 1), _rows(256, 2), _rows(128, 6)]
    if has_vres:
        ins += [fl, vfirst]
        specs += [_rows(128, 7), _rows(256, 2)]
    names = ["w0", "w2p", "a0", "a2p", "g2p", "k_k", "k_a"] + (["v0", "v2p"] if has_vres else []) + ["seg64"]
    for nme in names:
        ins.append(p[nme])
        specs.append(_full(p[nme].shape))
    return ins, specs, names


def rwkv_pre_fwd(l, fl, vfirst, p):
    has_vres = l > 0
    ins, specs, _ = _rwkv_pre_args(fl, vfirst, p, has_vres)
    return _map_fwd(f"rwkv_pre_fwd{l}", _rwkv_pre_fn(has_vres), (SEQ // RB,), ins, specs,
                    [_sds((SEQ, DG))] * 6, [_rows(DG)] * 6)


def rwkv_pre_bwd(l, fl, vfirst, p, cts):
    has_vres = l > 0
    ins, specs, names = _rwkv_pre_args(fl, vfirst, p, has_vres)
    n_row = 5 if has_vres else 3
    want = list(range(n_row)) + [n_row + i for i, nme in enumerate(names) if nme != "seg64"]
    acc = tuple(w for w in want if w >= n_row)
    ct_specs = [[_rows(DG)] * len(g) for g in cts]
    gout = {0: (_sds((SEQ, DG)), _rows(DG)), 1: (_sds((SEQ, DG)), _rows(DG)), 2: (_sds((SEQ, 128)), _rows(128))}
    if has_vres:
        gout[3] = (_sds((SEQ, 128)), _rows(128))
        gout[4] = (_sds((SEQ, DG)), _rows(DG))
    gs = _map_bwd(f"rwkv_pre_bwd{l}", _rwkv_pre_fn(has_vres), (SEQ // RB,), ins, specs, cts, ct_specs, want, acc, gout)
    keys = ["fk", "fv", "flora"] + (["fvres", "vfirst"] if has_vres else []) + [nme for nme in names if nme != "seg64"]
    return dict(zip(keys, gs))


def _rwkv_post_fn(y, fr, k2, v2, g, lnx_w, lnx_b, r_k, seg):
    mu = _dot(y, seg) * (1.0 / HD)
    d = y - mu
    var = _dot(d * d, seg) * (1.0 / HD)
    yn = d * lax.rsqrt(var + GN_EPS) * lnx_w + lnx_b
    bonus = _dot(fr * k2 * r_k, seg) * v2
    return ((yn + bonus) * g,)


def _rwkv_post_args(y, fl, k2, v2, g, p):
    ins = [y, fl, k2, v2, g, p["lnx_w"], p["lnx_b"], p["r_k"], p["seg64"]]
    specs = [_rows(DG), _rows(DG, 0), _rows(DG), _rows(DG), _rows(DG)] + [_full(x.shape) for x in ins[5:]]
    return ins, specs


def rwkv_post_fwd(l, y, fl, k2, v2, g, p):
    ins, specs = _rwkv_post_args(y, fl, k2, v2, g, p)
    return _map_fwd(f"rwkv_post_fwd{l}", _rwkv_post_fn, (SEQ // RB,), ins, specs, [_sds((SEQ, DG))], [_rows(DG)])[0]


def rwkv_post_bwd(l, y, fl, k2, v2, g, p, dya):
    ins, specs = _rwkv_post_args(y, fl, k2, v2, g, p)
    gs = _map_bwd(f"rwkv_post_bwd{l}", _rwkv_post_fn, (SEQ // RB,), ins, specs, [[dya]], [[_rows(DG)]],
                  want=[0, 1, 2, 3, 4, 5, 6, 7], acc=(5, 6, 7), gout={1: (_sds((SEQ, DG)), _rows(DG))})
    return dict(zip(["y", "fr", "k2", "v2", "g", "lnx_w", "lnx_b", "r_k"], gs))


SCAN_TB = 64


def _coltile8(rows8, dmask, ones_stack, parts):
    pieces, rest = [], rows8
    for q in range(parts):
        piece = rest.astype(BF16).astype(F32)
        if q < parts - 1:
            rest = rest - piece
        pieces.append((piece[:, None, :] * dmask[None]).reshape(8 * HD, DG).astype(BF16))
    x = pieces[0] if parts == 1 else jnp.concatenate(pieces, axis=1)
    return jnp.dot(x, ones_stack, preferred_element_type=F32).reshape(8, HD, DG)


def _coltiles_bf16(rows_list, dmask, ones_bf16):
    x = jnp.concatenate([(r8[:, None, :] * dmask[None]).reshape(8 * HD, DG).astype(BF16) for r8 in rows_list], axis=0)
    t = jnp.dot(x, ones_bf16, preferred_element_type=F32)
    return [t[q * 8 * HD:(q + 1) * 8 * HD].reshape(8, HD, DG) for q in range(len(rows_list))]


def _segrows8(x8, dmask, ones_bf16):
    t = jnp.dot(x8.reshape(8 * HD, DG).astype(BF16), ones_bf16, preferred_element_type=F32).reshape(8, HD, DG)
    return jnp.sum(t * dmask[None], axis=1)


def rwkv_scan_fwd(l, fl, w, k2, v2, c, b, p):
    nblk = SEQ // SCAN_TB

    def body(r_ref, w_ref, k_ref, v_ref, c_ref, b_ref, ones_ref, dm_ref, y_ref, st_ref, s_sc):
        @pl.when(pl.program_id(0) == 0)
        def _():
            s_sc[...] = jnp.zeros_like(s_sc)

        ones3, ones = ones_ref[...], ones_ref[0:DG, :]
        dmask = dm_ref[...]

        def group(gi, carry):
            t0 = pl.multiple_of(gi * 8, 8)
            sl = pl.ds(t0, 8)
            v8 = v_ref[sl, :]
            wt = _coltile8(w_ref[sl, :], dmask, ones3, 3)
            ct, bt, kt, rt = _coltiles_bf16([c_ref[sl, :], b_ref[sl, :], k_ref[sl, :], r_ref[sl, :]], dmask, ones)
            t = s_sc[...]
            for j in range(8):
                sa = jnp.sum(t * ct[j], axis=0, keepdims=True)
                t = t * wt[j] + bt[j] * sa + kt[j] * v8[j:j + 1, :]
                st_ref[t0 + j] = t
            s_sc[...] = t
            y_ref[sl, :] = jnp.sum(st_ref[sl] * rt, axis=1)
            return carry

        lax.fori_loop(0, SCAN_TB // 8, group, 0)

    row = pl.BlockSpec((SCAN_TB, DG), lambda i: (i, 0))
    ins = [fl, w, k2, v2, c, b, p["seg64x3_bf16"], p["dmask"]]
    specs = [row] * 6 + [_full((3 * DG, DG)), _full((HD, DG))]
    return pl.pallas_call(body, grid=(nblk,), in_specs=specs,
                          out_specs=[row, pl.BlockSpec((SCAN_TB, HD, DG), lambda i: (i, 0, 0))],
                          out_shape=[_sds((SEQ, DG)), _sds((SEQ, HD, DG))],
                          scratch_shapes=[pltpu.VMEM((HD, DG), F32)], name=f"rwkv_scan_fwd{l}",
                          compiler_params=_cp(("arbitrary",)))(*ins)


def rwkv_scan_bwd(l, fl, w, k2, v2, c, b, states, dy, p):
    nblk = SEQ // SCAN_TB

    def body(r_ref, w_ref, k_ref, v_ref, c_ref, b_ref, dy_ref, st_ref, sp_ref, ones_ref, dm_ref,
             dr_ref, dw_ref, dk_ref, dv_ref, dc_ref, db_ref, g_sc, prev_sc, d8_sc, dsa_sc):
        i = pl.program_id(0)

        @pl.when(i == 0)
        def _():
            g_sc[...] = jnp.zeros_like(g_sc)

        ones3, ones = ones_ref[...], ones_ref[0:DG, :]
        dmask = dm_ref[...]
        first_block = i == nblk - 1

        def group(gr, carry):
            gi = SCAN_TB // 8 - 1 - gr
            t0 = pl.multiple_of(gi * 8, 8)
            sl = pl.ds(t0, 8)
            v8, dy8 = v_ref[sl, :], dy_ref[sl, :]
            t8 = st_ref[sl]
            @pl.when(gi > 0)
            def _():
                prev_sc[0] = st_ref[t0 - 1]

            @pl.when(gi == 0)
            def _():
                prev_sc[0] = jnp.where(first_block, 0.0, sp_ref[0])

            for j in range(1, 8):
                prev_sc[j] = t8[j - 1]
            tp8 = prev_sc[...]
            wt = _coltile8(w_ref[sl, :], dmask, ones3, 3)
            ct, bt, kt, rt = _coltiles_bf16([c_ref[sl, :], b_ref[sl, :], k_ref[sl, :], r_ref[sl, :]], dmask, ones)
            sa8 = jnp.sum(tp8 * ct, axis=1)
            g = g_sc[...]
            for j in range(7, -1, -1):
                g = g + rt[j] * dy8[j:j + 1, :]
                d8_sc[j] = g
                dsa = jnp.sum(g * bt[j], axis=0, keepdims=True)
                dsa_sc[j:j + 1, :] = dsa
                g = g * wt[j] + ct[j] * dsa
            g_sc[...] = g
            d8 = d8_sc[...]
            dsa8 = dsa_sc[...]
            dv_ref[sl, :] = jnp.sum(d8 * kt, axis=1)
            dr_ref[sl, :] = _segrows8(t8 * dy8[:, None, :], dmask, ones)
            dk_ref[sl, :] = _segrows8(d8 * v8[:, None, :], dmask, ones)
            dw_ref[sl, :] = _segrows8(tp8 * d8, dmask, ones)
            db_ref[sl, :] = _segrows8(d8 * sa8[:, None, :], dmask, ones)
            dc_ref[sl, :] = _segrows8(tp8 * dsa8[:, None, :], dmask, ones)
            return carry

        lax.fori_loop(0, SCAN_TB // 8, group, 0)

    row = pl.BlockSpec((SCAN_TB, DG), lambda i: (nblk - 1 - i, 0))
    st_spec = pl.BlockSpec((SCAN_TB, HD, DG), lambda i: (nblk - 1 - i, 0, 0))
    sp_spec = pl.BlockSpec((1, HD, DG), lambda i: (jnp.maximum((nblk - 1 - i) * SCAN_TB - 1, 0), 0, 0))
    ins = [fl, w, k2, v2, c, b, dy, states, states, p["seg64x3_bf16"], p["dmask"]]
    specs = [row] * 7 + [st_spec, sp_spec, _full((3 * DG, DG)), _full((HD, DG))]
    tile8 = pltpu.VMEM((8, HD, DG), F32)
    return pl.pallas_call(body, grid=(nblk,), in_specs=specs, out_specs=[row] * 6, out_shape=[_sds((SEQ, DG))] * 6,
                          scratch_shapes=[pltpu.VMEM((HD, DG), F32), tile8, tile8, pltpu.VMEM((8, DG), F32)],
                          name=f"rwkv_scan_bwd{l}", compiler_params=_cp(("arbitrary",)))(*ins)


HG_ROWS = 128


def _hgrn_chunk_fn(layer):
    def fn(hq, hf, hi, hg, sprev, lb0, lb1, norm_w, seg, bd, tri, causal, ones16):
        e0 = jnp.exp(lb0 - jnp.maximum(lb0, lb1))
        e1 = jnp.exp(lb1 - jnp.maximum(lb0, lb1))
        sm0, sm1 = e0 / (e0 + e1), e1 / (e0 + e1)
        lb = (sm0 - sm0) if layer == 0 else ((sm0 + sm1) - sm0)
        forget = lb + (1.0 - lb) * _sigmoid(hf)
        logf = jnp.log(forget)
        kk = 1.0 - forget
        q = _silu(hq)
        c = HGRN_CHUNK
        b = _dotx(tri, logf)
        bl = jnp.sum(logf, axis=0, keepdims=True)
        diff = (b[:, None, :] - b[None, :, :]).reshape(c * c, DG)
        dec = jnp.exp(jnp.where(causal > 0.5, diff, -1e30))
        qrep = jnp.broadcast_to(q[:, None, :], (c, c, DG)).reshape(c * c, DG)
        ktil = jnp.broadcast_to(kk[None, :, :], (c, c, DG)).reshape(c * c, DG)
        vtil = jnp.broadcast_to(hi[None, :, :], (c, c, DG)).reshape(c * c, DG)
        att = _dot(qrep * ktil * dec, seg)
        o_intra = jnp.sum((att * vtil).reshape(c, c, DG), axis=1)
        kdec = kk * jnp.exp(bl - b)
        u = _dot_tn(kdec, hi) * bd
        tot = jnp.exp(_dotx_tn(logf, ones16))
        snext = sprev * tot + u
        o = o_intra + _dot(q * jnp.exp(b), sprev)
        ms = _dot(o * o, seg) * (1.0 / HD)
        y = o * lax.rsqrt(ms + RMS_EPS) * norm_w * _silu(hg)
        return y, snext

    return fn


def _hgrn_consts(p):
    return [p["seg64"], p["seg64"], p["tri16"], p["causal16"], p["ones16"]]


def hgrn_fwd(l, proj, p):
    fn = _hgrn_chunk_fn(l)
    nch = HG_ROWS // HGRN_CHUNK

    def body(hq_ref, hf_ref, hi_ref, hg_ref, lb0_ref, lb1_ref, nw_ref, seg_ref, bd_ref, tri_ref, cau_ref, o16_ref,
             y_ref, st_ref, s_sc):
        @pl.when(pl.program_id(0) == 0)
        def _():
            s_sc[...] = jnp.zeros_like(s_sc)

        consts = (lb0_ref[...], lb1_ref[...], nw_ref[...], seg_ref[...], bd_ref[...], tri_ref[...], cau_ref[...],
                  o16_ref[...])

        def chunk(ci, carry):
            sl = pl.ds(pl.multiple_of(ci * HGRN_CHUNK, HGRN_CHUNK), HGRN_CHUNK)
            sprev = s_sc[...]
            st_ref[ci] = sprev
            y, snext = fn(hq_ref[sl, :], hf_ref[sl, :], hi_ref[sl, :], hg_ref[sl, :], sprev, *consts)
            y_ref[sl, :] = y
            s_sc[...] = snext
            return carry

        lax.fori_loop(0, nch, chunk, 0)

    rows = lambda cb: pl.BlockSpec((HG_ROWS, DG), lambda i: (i, cb))
    ins = [proj, proj, proj, proj, p["lb0"], p["lb1"], p["hgrn_norm_w"]] + _hgrn_consts(p)
    specs = [rows(C_HQ // DG), rows(C_HF // DG), rows(C_HI // DG), rows(C_HG // DG)] + [_full(x.shape) for x in ins[4:]]
    return pl.pallas_call(body, grid=(SEQ // HG_ROWS,), in_specs=specs,
                          out_specs=[rows(0), pl.BlockSpec((nch, DG, DG), lambda i: (i, 0, 0))],
                          out_shape=[_sds((SEQ, DG)), _sds((SEQ // HGRN_CHUNK, DG, DG))],
                          scratch_shapes=[pltpu.VMEM((DG, DG), F32)], name=f"hgrn_fwd{l}",
                          compiler_params=_cp(("arbitrary",)))(*ins)


def hgrn_bwd(l, proj, states, dy, p):
    fn = _hgrn_chunk_fn(l)
    nch = HG_ROWS // HGRN_CHUNK
    nblk = SEQ // HG_ROWS

    def body(hq_ref, hf_ref, hi_ref, hg_ref, st_ref, dy_ref, lb0_ref, lb1_ref, nw_ref, seg_ref, bd_ref, tri_ref,
             cau_ref, o16_ref, dp_ref, dlb0_ref, dlb1_ref, dnw_ref, ds_sc):
        @pl.when(pl.program_id(0) == 0)
        def _():
            ds_sc[...] = jnp.zeros_like(ds_sc)
            dlb0_ref[...] = jnp.zeros_like(dlb0_ref)
            dlb1_ref[...] = jnp.zeros_like(dlb1_ref)
            dnw_ref[...] = jnp.zeros_like(dnw_ref)

        consts = (seg_ref[...], bd_ref[...], tri_ref[...], cau_ref[...], o16_ref[...])

        def chunk(cr, carry):
            ci = nch - 1 - cr
            sl = pl.ds(pl.multiple_of(ci * HGRN_CHUNK, HGRN_CHUNK), HGRN_CHUNK)
            f = lambda hq, hf, hi, hg, sp, b0, b1, nw: fn(hq, hf, hi, hg, sp, b0, b1, nw, *consts)
            _, vjp = jax.vjp(f, hq_ref[sl, :], hf_ref[sl, :], hi_ref[sl, :], hg_ref[sl, :], st_ref[ci],
                             lb0_ref[...], lb1_ref[...], nw_ref[...])
            dhq, dhf, dhi, dhg, dsp, dlb0, dlb1, dnw = vjp((dy_ref[sl, :], ds_sc[...]))
            dp_ref[sl, 0:DG] = dhq
            dp_ref[sl, DG:2 * DG] = dhf
            dp_ref[sl, 2 * DG:3 * DG] = dhi
            dp_ref[sl, 3 * DG:4 * DG] = dhg
            ds_sc[...] = dsp
            dlb0_ref[...] += dlb0
            dlb1_ref[...] += dlb1
            dnw_ref[...] += dnw
            return carry

        lax.fori_loop(0, nch, chunk, 0)

    rows = lambda cb: pl.BlockSpec((HG_ROWS, DG), lambda i: (nblk - 1 - i, cb))
    ins = [proj, proj, proj, proj, states, dy, p["lb0"], p["lb1"], p["hgrn_norm_w"]] + _hgrn_consts(p)
    specs = [rows(C_HQ // DG), rows(C_HF // DG), rows(C_HI // DG), rows(C_HG // DG),
             pl.BlockSpec((nch, DG, DG), lambda i: (nblk - 1 - i, 0, 0)), rows(0)] + [_full(x.shape) for x in ins[6:]]
    return pl.pallas_call(body, grid=(nblk,), in_specs=specs,
                          out_specs=[pl.BlockSpec((HG_ROWS, 4 * DG), lambda i: (nblk - 1 - i, 0)), _full((1, DG)),
                                     _full((1, DG)), _full((1, DG))],
                          out_shape=[_sds((SEQ, 4 * DG)), _sds((1, DG)), _sds((1, DG)), _sds((1, DG))],
                          scratch_shapes=[pltpu.VMEM((DG, DG), F32)], name=f"hgrn_bwd{l}",
                          compiler_params=_cp(("arbitrary",)))(*ins)


def _ssd_chunk_fn(z, xs, bm, cm, dtr, sprev, dt_bias, a_log, d_par, norm_w, e128, tri, trit, seg128, ones128):
    lc = SSD_CHUNK
    dt = _softplus(dtr + dt_bias)
    a = -jnp.exp(a_log)
    da = dt * a * (lax.broadcasted_iota(jnp.int32, (1, 128), 1) < NH).astype(F32)
    cs = _dotx(tri, da)
    cst = _dotx_tn(da, trit)
    cs_b = _dotx(cs, e128)
    dt_b = _dotx(dt, e128)
    csl_b = _dotx(jnp.sum(da, axis=0, keepdims=True), e128)
    xdt = xs * dt_b
    lane = lax.broadcasted_iota(jnp.int32, (1, DG), 1)
    rowi = lax.broadcasted_iota(jnp.int32, (lc, lc), 0)
    coli = lax.broadcasted_iota(jnp.int32, (lc, lc), 1)
    y = jnp.zeros((lc, DG), F32)
    snew = jnp.zeros((DG, SSD_N), F32)
    d_b = jnp.zeros((1, DG), F32)
    wdec = xdt * jnp.exp(csl_b - cs_b)
    for g in range(2):
        bg = bm[:, g * SSD_N:(g + 1) * SSD_N]
        cg = cm[:, g * SSD_N:(g + 1) * SSD_N]
        gmat = _dot_nt(cg, bg)
        gmask = ((lane // 128) == g).astype(F32)
        snew = snew + _dot_tn(wdec * gmask, bg)
        y = y + _dot_nt(cg, sprev) * gmask * jnp.exp(cs_b)
        for hh in range(2):
            h = 2 * g + hh
            seg = jnp.where(rowi >= coli, cs[:, h:h + 1] - cst[h:h + 1, :], -1e30)
            hmask = ((lane // HD) == h).astype(F32)
            y = y + _dot(gmat * jnp.exp(seg), xdt * hmask)
            d_b = d_b + d_par[:, h:h + 1] * hmask
    cd = jnp.exp(_dotx_tn(_dotx(da, e128), ones128))
    snext = sprev * cd + snew
    y = y + xs * d_b
    y = y * _silu(z)
    ms = _dot(y * y, seg128) * (1.0 / 128.0)
    return y * lax.rsqrt(ms + RMS_EPS) * norm_w, snext


def ssd_fwd(l, proj, xc, p):
    nc = SEQ // SSD_CHUNK

    def body(z_ref, xs_ref, b_ref, c_ref, dt_ref, dtb_ref, al_ref, d_ref, nw_ref, e_ref, tri_ref, trit_ref, sg_ref,
             on_ref, y_ref, st_ref, s_sc):
        @pl.when(pl.program_id(0) == 0)
        def _():
            s_sc[...] = jnp.zeros_like(s_sc)

        sprev = s_sc[...]
        st_ref[0] = sprev
        y, snext = _ssd_chunk_fn(z_ref[...], xs_ref[...], b_ref[...], c_ref[...], dt_ref[...], sprev, dtb_ref[...],
                                 al_ref[...], d_ref[...], nw_ref[...], e_ref[...], tri_ref[...], trit_ref[...],
                                 sg_ref[...], on_ref[...])
        y_ref[...] = y
        s_sc[...] = snext

    rw = lambda w, cb: pl.BlockSpec((SSD_CHUNK, w), lambda i: (i, cb))
    ins = [proj, xc, xc, xc, proj, p["dt_bias"], p["a_log"], p["ssd_d"], p["ssd_norm_w"], p["e128"], p["tri128"],
           p["tri128t"], p["seg128"], p["ones128"]]
    specs = [rw(DG, C_Z // DG), rw(DG, 0), rw(DG, 1), rw(DG, 2), rw(128, C_DT // 128)] + [_full(x.shape) for x in ins[5:]]
    return pl.pallas_call(body, grid=(nc,), in_specs=specs,
                          out_specs=[rw(DG, 0), pl.BlockSpec((1, DG, SSD_N), lambda i: (i, 0, 0))],
                          out_shape=[_sds((SEQ, DG)), _sds((nc, DG, SSD_N))],
                          scratch_shapes=[pltpu.VMEM((DG, SSD_N), F32)], name=f"ssd_fwd{l}",
                          compiler_params=_cp(("arbitrary",)))(*ins)


def ssd_bwd(l, proj, xc, states, dy, p):
    nc = SEQ // SSD_CHUNK

    def body(z_ref, xs_ref, b_ref, c_ref, dt_ref, st_ref, dy_ref, dtb_ref, al_ref, d_ref, nw_ref, e_ref, tri_ref,
             trit_ref, sg_ref, on_ref, dz_ref, dxc_ref, ddt_ref, ddtb_ref, dal_ref, dd_ref, dnw_ref, ds_sc):
        @pl.when(pl.program_id(0) == 0)
        def _():
            ds_sc[...] = jnp.zeros_like(ds_sc)
            ddtb_ref[...] = jnp.zeros_like(ddtb_ref)
            dal_ref[...] = jnp.zeros_like(dal_ref)
            dd_ref[...] = jnp.zeros_like(dd_ref)
            dnw_ref[...] = jnp.zeros_like(dnw_ref)

        consts = (e_ref[...], tri_ref[...], trit_ref[...], sg_ref[...], on_ref[...])
        f = lambda *a: _ssd_chunk_fn(*a, *consts)
        _, vjp = jax.vjp(f, z_ref[...], xs_ref[...], b_ref[...], c_ref[...], dt_ref[...], st_ref[0], dtb_ref[...],
                         al_ref[...], d_ref[...], nw_ref[...])
        dz, dxs, db, dc, ddt, dsp, ddtb, dal, dd, dnw = vjp((dy_ref[...], ds_sc[...]))
        dz_ref[...] = dz
        dxc_ref[:, 0:DG] = dxs
        dxc_ref[:, DG:2 * DG] = db
        dxc_ref[:, 2 * DG:3 * DG] = dc
        ddt_ref[...] = ddt
        ds_sc[...] = dsp
        ddtb_ref[...] += ddtb
        dal_ref[...] += dal
        dd_ref[...] += dd
        dnw_ref[...] += dnw

    rw = lambda w, cb: pl.BlockSpec((SSD_CHUNK, w), lambda i: (nc - 1 - i, cb))
    ins = [proj, xc, xc, xc, proj, states, dy, p["dt_bias"], p["a_log"], p["ssd_d"], p["ssd_norm_w"], p["e128"],
           p["tri128"], p["tri128t"], p["seg128"], p["ones128"]]
    specs = [rw(DG, C_Z // DG), rw(DG, 0), rw(DG, 1), rw(DG, 2), rw(128, C_DT // 128),
             pl.BlockSpec((1, DG, SSD_N), lambda i: (nc - 1 - i, 0, 0)), rw(DG, 0)] + [_full(x.shape) for x in ins[7:]]
    return pl.pallas_call(body, grid=(nc,), in_specs=specs,
                          out_specs=[rw(DG, 0), rw(3 * DG, 0), rw(128, 0), _full((1, 128)), _full((1, 128)), _full((1, 128)),
                                     _full((1, DG))],
                          out_shape=[_sds((SEQ, DG)), _sds((SEQ, 3 * DG)), _sds((SEQ, 128)), _sds((1, 128)), _sds((1, 128)),
                                     _sds((1, 128)), _sds((1, DG))],
                          scratch_shapes=[pltpu.VMEM((DG, SSD_N), F32)], name=f"ssd_bwd{l}",
                          compiler_params=_cp(("arbitrary",)))(*ins)


ATT_BLK = 128


def _slope(h):
    return jnp.where(h == 0, 0.25, jnp.where(h == 1, 0.0625, jnp.where(h == 2, 0.015625, 0.00390625))).astype(F32)


def _att_scores(qn, kc, kp, h, dil, has_prev):
    i = lax.broadcasted_iota(jnp.int32, (ATT_BLK, ATT_BLK), 0)
    j = lax.broadcasted_iota(jnp.int32, (ATT_BLK, ATT_BLK), 1)
    slope = _slope(h)
    scale = HD ** -0.5
    s_c = _dot_nt(qn, kc) * scale - slope * ((i - j) * dil).astype(F32)
    s_p = _dot_nt(qn, kp) * scale - slope * ((ATT_BLK + i - j) * dil).astype(F32)
    m_c = j <= i
    m_p = jnp.logical_and(j >= i, has_prev)
    return jnp.where(m_c, s_c, -1e30), jnp.where(m_p, s_p, -1e30), m_c, m_p


def attn_branch_fwd(l, bi, qs, ks, vs):
    dil, _, ln, _ = qs.shape
    nb = ln // ATT_BLK

    def body(q_ref, k_ref, v_ref, o_ref, l_ref):
        h = pl.program_id(1)

        def blk(n, carry):
            r0 = pl.multiple_of(n * ATT_BLK, ATT_BLK)
            rp = pl.multiple_of(jnp.maximum(n - 1, 0) * ATT_BLK, ATT_BLK)
            qn = q_ref[0, 0, pl.ds(r0, ATT_BLK), :]
            kc, vc = k_ref[0, 0, pl.ds(r0, ATT_BLK), :], v_ref[0, 0, pl.ds(r0, ATT_BLK), :]
            kp, vp = k_ref[0, 0, pl.ds(rp, ATT_BLK), :], v_ref[0, 0, pl.ds(rp, ATT_BLK), :]
            s_c, s_p, m_c, m_p = _att_scores(qn, kc, kp, h, dil, n > 0)
            m = jnp.maximum(jnp.max(s_c, axis=1, keepdims=True), jnp.max(s_p, axis=1, keepdims=True))
            p_c = jnp.where(m_c, jnp.exp(s_c - m), 0.0)
            p_p = jnp.where(m_p, jnp.exp(s_p - m), 0.0)
            den = jnp.sum(p_c, axis=1, keepdims=True) + jnp.sum(p_p, axis=1, keepdims=True)
            o = (_dot(p_c, vc) + _dot(p_p, vp)) / den
            o_ref[0, 0, pl.ds(r0, ATT_BLK), :] = o
            l_ref[0, 0, pl.ds(r0, ATT_BLK), :] = jnp.broadcast_to(m + jnp.log(den), (ATT_BLK, HD))
            return carry

        lax.fori_loop(0, nb, blk, 0)

    spec = pl.BlockSpec((1, 1, ln, HD), lambda z, h: (z, h, 0, 0))
    return pl.pallas_call(body, grid=(dil, NH), in_specs=[spec] * 3, out_specs=[spec] * 2,
                          out_shape=[_sds(qs.shape)] * 2, name=f"attn_fwd{l}_{bi}",
                          compiler_params=_cp(("parallel", "parallel")))(qs, ks, vs)


def attn_branch_bwd(l, bi, qs, ks, vs, dos, lses, deltas):
    dil, _, ln, _ = qs.shape
    nb = ln // ATT_BLK
    scale = HD ** -0.5

    def body(q_ref, k_ref, v_ref, do_ref, l_ref, dl_ref, dq_ref, dk_ref, dv_ref):
        h = pl.program_id(1)
        dk_ref[...] = jnp.zeros_like(dk_ref)
        dv_ref[...] = jnp.zeros_like(dv_ref)

        def blk(n, carry):
            r0 = pl.multiple_of(n * ATT_BLK, ATT_BLK)
            rp = pl.multiple_of(jnp.maximum(n - 1, 0) * ATT_BLK, ATT_BLK)
            cur, prv = pl.ds(r0, ATT_BLK), pl.ds(rp, ATT_BLK)
            qn, don = q_ref[0, 0, cur, :], do_ref[0, 0, cur, :]
            lse, dlt = l_ref[0, 0, cur, 0:1], dl_ref[0, 0, cur, 0:1]
            kc, vc, kp, vp = k_ref[0, 0, cur, :], v_ref[0, 0, cur, :], k_ref[0, 0, prv, :], v_ref[0, 0, prv, :]
            s_c, s_p, m_c, m_p = _att_scores(qn, kc, kp, h, dil, n > 0)
            p_c = jnp.where(m_c, jnp.exp(s_c - lse), 0.0)
            p_p = jnp.where(m_p, jnp.exp(s_p - lse), 0.0)
            ds_c = p_c * (_dot_nt(don, vc) - dlt)
            ds_p = p_p * (_dot_nt(don, vp) - dlt)
            dq_ref[0, 0, cur, :] = (_dot(ds_c, kc) + _dot(ds_p, kp)) * scale
            dv_ref[0, 0, prv, :] += _dot_tn(p_p, don)
            dk_ref[0, 0, prv, :] += _dot_tn(ds_p, qn) * scale
            dv_ref[0, 0, cur, :] += _dot_tn(p_c, don)
            dk_ref[0, 0, cur, :] += _dot_tn(ds_c, qn) * scale
            return carry

        lax.fori_loop(0, nb, blk, 0)

    spec = pl.BlockSpec((1, 1, ln, HD), lambda z, h: (z, h, 0, 0))
    return pl.pallas_call(body, grid=(dil, NH), in_specs=[spec] * 6, out_specs=[spec] * 3,
                          out_shape=[_sds(qs.shape)] * 3, name=f"attn_bwd{l}_{bi}",
                          compiler_params=_cp(("parallel", "parallel")))(qs, ks, vs, dos, lses, deltas)


def _attn_merge_fn(o1, o2, o3, l1, l2, l3):
    m = jnp.maximum(jnp.maximum(l1, l2), l3)
    w1, w2, w3 = jnp.exp(l1 - m), jnp.exp(l2 - m), jnp.exp(l3 - m)
    den = w1 + w2 + w3
    return (w1 * o1 + w2 * o2 + w3 * o3) / den, m + jnp.log(den)


def attn_merge(l, os_, ls_):
    ins = list(os_) + list(ls_)
    return _map_fwd(f"attn_merge{l}", _attn_merge_fn, (SEQ // RB,), ins, [_rows(DG)] * 6, [_sds((SEQ, DG))] * 2,
                    [_rows(DG)] * 2)


def attn_delta(l, dyb, yb, seg):
    fn = lambda d, y, s: (_dot(d * y, s),)
    return _map_fwd(f"attn_delta{l}", fn, (SEQ // RB,), [dyb, yb, seg], [_rows(DG), _rows(DG), _full((DG, DG))],
                    [_sds((SEQ, DG))], [_rows(DG)])[0]


def _to_sub(t, dil):
    return t.reshape(SEQ // dil, dil, NH, HD).transpose(1, 2, 0, 3)


def _from_sub(t):
    dil, _, ln, _ = t.shape
    return t.transpose(2, 0, 1, 3).reshape(SEQ, DG)


def _ln_fn(x, mix, w, b):
    h = ALPHA * x + mix
    mu = jnp.mean(h, axis=-1, keepdims=True)
    d = h - mu
    var = jnp.mean(d * d, axis=-1, keepdims=True)
    return (d * lax.rsqrt(var + LN_EPS) * w + b,)


def ln_fwd(name, x, mix, w, b):
    specs = [_rows(D_MODEL), _rows(D_MODEL), _full((1, D_MODEL)), _full((1, D_MODEL))]
    return _map_fwd(name, _ln_fn, (SEQ // RB,), [x, mix, w, b], specs, [_sds((SEQ, D_MODEL))], [_rows(D_MODEL)])[0]


def ln_bwd(name, x, mix, w, b, dy):
    specs = [_rows(D_MODEL), _rows(D_MODEL), _full((1, D_MODEL)), _full((1, D_MODEL))]
    return _map_bwd(name, _ln_fn, (SEQ // RB,), [x, mix, w, b], specs, [[dy]], [[_rows(D_MODEL)]], want=[1, 2, 3],
                    acc=(2, 3))


def _relu2_fn(u):
    r = jnp.maximum(u, 0.0)
    return (r * r,)


def relu2_fwd(name, u):
    return _map_fwd(name, _relu2_fn, (SEQ // RB,), [u], [_rows(D_FF)], [_sds((SEQ, D_FF))], [_rows(D_FF)])[0]


def relu2_bwd(name, u, dh):
    fn = lambda uu, g: (g * 2.0 * jnp.maximum(uu, 0.0),)
    return _map_fwd(name, fn, (SEQ // RB,), [u, dh], [_rows(D_FF)] * 2, [_sds((SEQ, D_FF))], [_rows(D_FF)])[0]


def loss_call(y, tgt):
    def fn(yy, tt):
        e = yy - tt
        part = 0.5 * jnp.sum(jnp.sum(e * e, axis=-1, keepdims=True) * (1.0 / D_MODEL), axis=0, keepdims=True)
        return e * (1.0 / D_MODEL), jnp.broadcast_to(part, (8, 128))

    return _map_fwd("loss", fn, (SEQ // RB,), [y, tgt], [_rows(D_MODEL)] * 2,
                    [_sds((SEQ, D_MODEL)), _sds((SEQ // RB * 8, 128))],
                    [_rows(D_MODEL), pl.BlockSpec((8, 128), lambda i: (i, 0))])


def layer_fwd(l, x, vfirst, wts, p):
    sv = {"x": x}
    proj = _mm(f"mm_in{l}", x, wts["w_in"], "nn", 512, 1024, 1024)
    fl = lerp_fwd(l, proj, p["mu"])
    xc = conv_fwd(l, proj, p["conv_w"], p["conv_b"])
    w, k2, v2, c, b, g = rwkv_pre_fwd(l, fl, vfirst, p)
    y_scan, states = rwkv_scan_fwd(l, fl, w, k2, v2, c, b, p)
    ya = rwkv_post_fwd(l, y_scan, fl, k2, v2, g, p)
    q_a, k_a, v_a = proj[:, C_AQ:C_AQ + DG], proj[:, C_AK:C_AK + DG], proj[:, C_AV:C_AV + DG]
    subs, outs, lses = [], [], []
    for bi, (win, dil) in enumerate(DILATED):
        qs, ks, vs = _to_sub(q_a, dil), _to_sub(k_a, dil), _to_sub(v_a, dil)
        o, lse = attn_branch_fwd(l, bi, qs, ks, vs)
        subs.append((qs, ks, vs))
        outs.append(_from_sub(o))
        lses.append(_from_sub(lse))
    yb, lse_all = attn_merge(l, outs, lses)
    yc, ssd_states = ssd_fwd(l, proj, xc, p)
    yd, hg_states = hgrn_fwd(l, proj, p)
    ycat = jnp.concatenate([ya, yb, yc, yd], axis=1)
    mix = _mm(f"mm_out{l}", ycat, wts["w_out"], "nn", 512, 1024, 1024)
    x1 = ln_fwd(f"ln1_fwd{l}", x, mix, p["ln1_w"], p["ln1_b"])
    u = _mm(f"mm_up{l}", x1, wts["w_up_t"], "nt", 512, 1024, 1024)
    hh = relu2_fwd(f"relu2_fwd{l}", u)
    m2 = _mm(f"mm_down{l}", hh, wts["w_down"], "nn", 512, 1024, 1024)
    x2 = ln_fwd(f"ln2_fwd{l}", x1, m2, p["ln2_w"], p["ln2_b"])
    sv.update(proj=proj, fl=fl, xc=xc, w=w, k2=k2, v2=v2, c=c, b=b, g=g, y_scan=y_scan, states=states, subs=subs,
              yb=yb, lse_all=lse_all, ssd_states=ssd_states, hg_states=hg_states, ycat=ycat, mix=mix, x1=x1, u=u, hh=hh,
              m2=m2, vfirst=vfirst)
    return x2, sv


def layer_bwd(l, dx2, dvfirst_next, sv, wts, p):
    gr = {}
    x, x1, proj, fl = sv["x"], sv["x1"], sv["proj"], sv["fl"]
    dres2, gr["ln2_w"], gr["ln2_b"] = ln_bwd(f"ln2_bwd{l}", x1, sv["m2"], p["ln2_w"], p["ln2_b"], dx2)
    dh = _mm(f"mm_down_dx{l}", dres2, wts["w_down"], "nt", 512, 1024, 1024)
    gr["w_down"] = _mm(f"mm_down_dw{l}", sv["hh"], dres2, "tn", 512, 1024, 512)
    du = relu2_bwd(f"relu2_bwd{l}", sv["u"], dh)
    dx1 = _mm(f"mm_up_dx{l}", du, wts["w_up_t"], "nn", 512, 1024, 1024, add=dres2, add_scale=ALPHA)
    gr["w_up_t"] = _mm(f"mm_up_dw{l}", du, x1, "tn", 512, 1024, 512)
    dres1, gr["ln1_w"], gr["ln1_b"] = ln_bwd(f"ln1_bwd{l}", x, sv["mix"], p["ln1_w"], p["ln1_b"], dx1)
    dycat = _mm(f"mm_out_dx{l}", dres1, wts["w_out"], "nt", 512, 1024, 1024)
    gr["w_out"] = _mm(f"mm_out_dw{l}", sv["ycat"], dres1, "tn", 512, 1024, 512)
    dya, dyb, dyc, dyd = (dycat[:, i * DG:(i + 1) * DG] for i in range(4))
    dhg4, gr["lb0"], gr["lb1"], gr["hgrn_norm_w"] = hgrn_bwd(l, proj, sv["hg_states"], dyd, p)
    dz, dxc, ddt, gr["dt_bias"], gr["a_log"], gr["ssd_d"], gr["ssd_norm_w"] = ssd_bwd(l, proj, sv["xc"], sv["ssd_states"], dyc, p)
    dxbc, gr["conv_w"], gr["conv_b"] = conv_bwd(l, proj, p["conv_w"], p["conv_b"], dxc)
    delta = attn_delta(l, dyb, sv["yb"], p["seg64"])
    dqs, dks, dvs = [], [], []
    for bi, (win, dil) in enumerate(DILATED):
        qs, ks, vs = sv["subs"][bi]
        dq, dk, dv = attn_branch_bwd(l, bi, qs, ks, vs, _to_sub(dyb, dil), _to_sub(sv["lse_all"], dil), _to_sub(delta, dil))
        dqs.append(_from_sub(dq))
        dks.append(_from_sub(dk))
        dvs.append(_from_sub(dv))
    dq_a, dk_a, dv_a = _addn(f"attn_dq{l}", *dqs), _addn(f"attn_dk{l}", *dks), _addn(f"attn_dv{l}", *dvs)
    pg = rwkv_post_bwd(l, sv["y_scan"], fl, sv["k2"], sv["v2"], sv["g"], p, dya)
    gr["lnx_w"], gr["lnx_b"], gr["r_k"] = pg["lnx_w"], pg["lnx_b"], pg["r_k"]
    dr, dw, dk, dv, dc, db = rwkv_scan_bwd(l, fl, sv["w"], sv["k2"], sv["v2"], sv["c"], sv["b"], sv["states"], pg["y"], p)
    v2_cts = [dv, pg["v2"]] + ([dvfirst_next] if dvfirst_next is not None else [])
    qg = rwkv_pre_bwd(l, fl, sv["vfirst"], p, [[dw], [dk, pg["k2"]], v2_cts, [dc], [db], [pg["g"]]])
    for nme in ("w0", "w2p", "a0", "a2p", "g2p", "k_k", "k_a", "v0", "v2p"):
        if nme in qg:
            gr[nme] = qg[nme]
    dfr = _addn(f"rwkv_dr{l}", dr, pg["fr"])
    dvres = qg["fvres"] if l > 0 else jnp.zeros((SEQ, 128), F32)
    dfl_out = jnp.concatenate([dfr, qg["fk"], qg["fv"], qg["flora"], dvres], axis=1)
    dfl_in, gr["mu"] = lerp_bwd(l, proj, p["mu"], dfl_out)
    dproj = jnp.concatenate([dfl_in[:, 0:768], dq_a, dk_a, dv_a, dz, dxbc, dhg4, dfl_in[:, 768:896], ddt,
                             dfl_in[:, 896:1024], jnp.zeros((SEQ, 128), F32)], axis=1)
    dx = _mm(f"mm_in_dx{l}", dproj, wts["w_in"], "nt", 512, 1024, 1024, add=dres1, add_scale=ALPHA)
    gr["w_in"] = _mm(f"mm_in_dw{l}", x, dproj, "tn", 512, 1024, 512)
    return dx, (qg["vfirst"] if l > 0 else None), gr


def _w_in_pad(w_in_l, w_vres):
    rows = w_in_l.shape[0]
    z = lambda n: jnp.zeros((rows, n), w_in_l.dtype)
    vres = z(128) if w_vres is None else jnp.concatenate([w_vres, z(96)], axis=1)
    return jnp.concatenate([w_in_l[:, 0:768], w_in_l[:, 896:1664], w_in_l[:, 1664:1920], w_in_l[:, 1920:2688],
                            w_in_l[:, 2692:3716], w_in_l[:, 768:896], w_in_l[:, 2688:2692], z(124), vres, z(128)], axis=1)


def _w_in_unpad(g):
    g_in = jnp.concatenate([g[:, 0:768], g[:, C_LORA:C_LORA + 128], g[:, 768:1536], g[:, C_Z:C_Z + 256],
                            g[:, C_XBC:C_XBC + 768], g[:, C_DT:C_DT + 4], g[:, C_HQ:C_HQ + 1024]], axis=1)
    return g_in, g[:, C_VRES:C_VRES + 32]


def _consts():
    i16 = jnp.arange(HGRN_CHUNK)
    pair = jnp.arange(HGRN_CHUNK * HGRN_CHUNK)
    i128 = jnp.arange(128)
    seg64 = _seg_ones(DG, HD)
    tri128 = (i128[:, None] >= i128[None, :]).astype(F32)
    return dict(
        seg64=seg64, seg64x3_bf16=jnp.concatenate([seg64, seg64, seg64], axis=0).astype(BF16),
        dmask=(jnp.arange(HD)[:, None] == (jnp.arange(DG)[None, :] % HD)).astype(F32),
        tri16=(i16[:, None] >= i16[None, :]).astype(F32),
        causal16=jnp.broadcast_to(((pair // HGRN_CHUNK) >= (pair % HGRN_CHUNK)).astype(F32)[:, None], (256, DG)),
        ones16=jnp.ones((HGRN_CHUNK, DG), F32),
        e128=((i128[:, None] == (jnp.arange(DG)[None, :] // HD)) & (i128[:, None] < NH)).astype(F32),
        tri128=tri128, tri128t=tri128.T, seg128=_seg_ones(DG, 128), ones128=jnp.ones((128, 128), F32))


def _pad_lanes(v, n):
    return jnp.concatenate([v, jnp.zeros((n - v.shape[0],), v.dtype)])[None, :]


def _layer_params(l, raw, consts):
    p = dict(consts)
    row = lambda name: raw[name][l][None, :]
    z = lambda r: jnp.zeros((r, DG), F32)
    mu_vres = raw["mu_vres"][l - 1] if l > 0 else jnp.zeros((32,), F32)
    p["mu"] = jnp.concatenate([raw["mu_shift"][l], mu_vres, jnp.zeros((96,), F32)])[None, :]
    p["conv_w"], p["conv_b"] = raw["ssd_conv_w"][l], row("ssd_conv_b")
    p["w0"], p["a0"], p["k_k"], p["k_a"] = row("rwkv_w0"), row("rwkv_a0"), row("rwkv_k_k"), row("rwkv_k_a")
    p["lnx_w"], p["lnx_b"] = row("rwkv_lnx_w"), row("rwkv_lnx_b")
    p["r_k"] = raw["rwkv_r_k"][l].reshape(1, DG)
    p["w2p"] = jnp.concatenate([raw["rwkv_w2"][l], z(96)], axis=0)
    p["a2p"] = jnp.concatenate([z(32), raw["rwkv_a2"][l], z(64)], axis=0)
    p["g2p"] = jnp.concatenate([z(64), raw["rwkv_g2"][l]], axis=0)
    if l > 0:
        p["v0"] = raw["rwkv_v0"][l - 1][None, :]
        p["v2p"] = jnp.concatenate([raw["rwkv_v2"][l - 1], z(96)], axis=0)
    p["lb0"], p["lb1"] = raw["lower_bounds"][0:1], raw["lower_bounds"][1:2]
    p["hgrn_norm_w"], p["ssd_norm_w"] = row("hgrn_norm_w"), row("ssd_norm_w")
    p["dt_bias"], p["a_log"], p["ssd_d"] = (_pad_lanes(raw[n][l], 128) for n in ("ssd_dt_bias", "ssd_A_log", "ssd_D"))
    for n in ("ln1_w", "ln1_b", "ln2_w", "ln2_b"):
        p[n] = row(n)
    return p


def _natural_grads(g0, g1):
    gs = (g0, g1)
    st = lambda key, f=lambda a: a[0]: jnp.stack([f(g[key]) for g in gs])
    out = {}
    out["lower_bounds"] = jnp.concatenate([g0["lb0"] + g1["lb0"], g0["lb1"] + g1["lb1"]], axis=0)
    out["mu_shift"] = st("mu", lambda a: a[0, :896])
    out["mu_vres"] = g1["mu"][:, 896:928]
    out["rwkv_w0"], out["rwkv_a0"], out["rwkv_k_k"], out["rwkv_k_a"] = st("w0"), st("a0"), st("k_k"), st("k_a")
    out["rwkv_w2"] = st("w2p", lambda a: a[0:32])
    out["rwkv_a2"] = st("a2p", lambda a: a[32:64])
    out["rwkv_g2"] = st("g2p", lambda a: a[64:128])
    out["rwkv_r_k"] = st("r_k", lambda a: a.reshape(NH, HD))
    out["rwkv_lnx_w"], out["rwkv_lnx_b"] = st("lnx_w"), st("lnx_b")
    out["rwkv_v0"] = g1["v0"]
    out["rwkv_v2"] = g1["v2p"][None, 0:32]
    out["ssd_conv_w"] = st("conv_w", lambda a: a)
    out["ssd_conv_b"] = st("conv_b")
    out["ssd_dt_bias"], out["ssd_A_log"], out["ssd_D"] = (st(k, lambda a: a[0, :NH]) for k in ("dt_bias", "a_log", "ssd_d"))
    out["ssd_norm_w"], out["hgrn_norm_w"] = st("ssd_norm_w"), st("hgrn_norm_w")
    for n in ("ln1_w", "ln1_b", "ln2_w", "ln2_b"):
        out[n] = st(n)
    return out


MESH_T = pl.DeviceIdType.MESH
ANY = pl.BlockSpec(memory_space=pl.ANY)


def _dev_index(px, py, pc):
    return 4 * px + 2 * py + pc


def all_gather(arrs):
    n = len(arrs)

    def body(*refs):
        ins, outs = refs[:n], refs[n:2 * n]
        send_sems, recv_sems, local_sems = refs[2 * n:]
        x, y, c = lax.axis_index("x"), lax.axis_index("y"), lax.axis_index("c")
        me, sibling = (x, y, c), (x, y, 1 - c)
        chips = [(1 - x, y), (x, 1 - y), (1 - x, 1 - y)]

        def copy(a, k, block, to, src=None):
            slot = outs[a].at[_dev_index(*block)]
            return pltpu.make_async_remote_copy(src_ref=slot if src is None else src, dst_ref=slot,
                                                send_sem=send_sems.at[a, k], recv_sem=recv_sems.at[a, k],
                                                device_id=to, device_id_type=MESH_T)

        mine = [pltpu.make_async_copy(ins[a], outs[a].at[_dev_index(*me)], local_sems.at[a]) for a in range(n)]
        for cp in mine:
            cp.start()
        first = []
        for a in range(n):
            first.append(copy(a, 0, me, sibling, src=ins[a]))
            first += [copy(a, 1 + j, me, (*chip, c), src=ins[a]) for j, chip in enumerate(chips)]
        for cp in first:
            cp.start()
        passed = []
        for j, chip in enumerate(chips):
            for a in range(n):
                copy(a, 1 + j, (*chip, c), me).wait_recv()
                fwd = copy(a, 4 + j, (*chip, c), sibling)
                fwd.start()
                passed.append(fwd)
        for a in range(n):
            copy(a, 0, sibling, me).wait_recv()
            for j, chip in enumerate(chips):
                copy(a, 4 + j, (*chip, 1 - c), me).wait_recv()
        for cp in first + passed:
            cp.wait_send()
        for cp in mine:
            cp.wait()

    return pl.pallas_call(
        body, in_specs=[ANY] * n, out_specs=[ANY] * n,
        out_shape=[_sds((N_DEV,) + a.shape, a.dtype) for a in arrs],
        scratch_shapes=[pltpu.SemaphoreType.DMA((n, 7)), pltpu.SemaphoreType.DMA((n, 7)), pltpu.SemaphoreType.DMA((n,))],
        name="all_gather")(*arrs)


def _chips(x, y):
    return [(x, y), (1 - x, y), (x, 1 - y), (1 - x, 1 - y)]


def exchange_siblings(arrs):
    n = len(arrs)

    def body(*refs):
        ins, mine, sib = refs[:n], refs[n:2 * n], refs[2 * n:3 * n]
        send_sems, recv_sems, local_sems = refs[3 * n:]
        x, y, c = lax.axis_index("x"), lax.axis_index("y"), lax.axis_index("c")
        sibling = (x, y, 1 - c)
        locs, sends = [], []
        for a in range(n):
            for k, (cx, cy) in enumerate(_chips(x, y)):
                lc = pltpu.make_async_copy(ins[a].at[_dev_index(cx, cy, c)], mine[a].at[k], local_sems.at[a, k])
                sd = pltpu.make_async_remote_copy(src_ref=ins[a].at[_dev_index(cx, cy, 1 - c)], dst_ref=sib[a].at[k],
                                                  send_sem=send_sems.at[a, k], recv_sem=recv_sems.at[a, k],
                                                  device_id=sibling, device_id_type=MESH_T)
                lc.start()
                sd.start()
                locs.append(lc)
                sends.append(sd)
        for sd in sends:
            sd.wait_recv()
        for sd in sends:
            sd.wait_send()
        for lc in locs:
            lc.wait()

    quarter = [_sds((4,) + a.shape[1:], a.dtype) for a in arrs]
    sem = pltpu.SemaphoreType.DMA((n, 4))
    outs = pl.pallas_call(body, in_specs=[ANY] * n, out_specs=[ANY] * (2 * n), out_shape=quarter + quarter,
                          scratch_shapes=[sem, sem, sem], name="exchange_siblings")(*arrs)
    return outs[:n], outs[n:]


def reduce_pair(name, mine, sib, wire_dtype):
    _, r, c = mine.shape
    rb = min(r, 262144 // c)

    def body(m_ref, s_ref, own_ref, part_ref):
        own_ref[...] = m_ref[0] + s_ref[0]
        for k in range(1, 4):
            part_ref[k - 1] = (m_ref[k] + s_ref[k]).astype(wire_dtype)

    return pl.pallas_call(body, grid=(r // rb,), in_specs=[pl.BlockSpec((4, rb, c), lambda i: (0, i, 0))] * 2,
                          out_specs=[pl.BlockSpec((rb, c), lambda i: (i, 0)), pl.BlockSpec((3, rb, c), lambda i: (0, i, 0))],
                          out_shape=[_sds((r, c)), _sds((3, r, c), wire_dtype)], name=name,
                          compiler_params=_cp(("parallel",)))(mine, sib)


def exchange_chips(parts, rep):
    n = len(parts)

    def body(*refs):
        ins, rep_ref = refs[:n], refs[n]
        recv, rep_all = refs[n + 1:2 * n + 1], refs[2 * n + 1]
        send_sems, recv_sems, rsend_sems, rrecv_sems, local_sem = refs[2 * n + 2:]
        x, y, c = lax.axis_index("x"), lax.axis_index("y"), lax.axis_index("c")
        me = _dev_index(x, y, c)
        mine = pltpu.make_async_copy(rep_ref, rep_all.at[me], local_sem)
        mine.start()
        cps = []
        for a in range(n):
            for k, (cx, cy) in enumerate(_chips(x, y)[1:]):
                cp = pltpu.make_async_remote_copy(src_ref=ins[a].at[k], dst_ref=recv[a].at[k],
                                                  send_sem=send_sems.at[a, k], recv_sem=recv_sems.at[a, k],
                                                  device_id=(cx, cy, c), device_id_type=MESH_T)
                cp.start()
                cps.append(cp)
        rels = [(rx, ry, rc) for rx in (0, 1) for ry in (0, 1) for rc in (0, 1)][1:]
        peers = [(jnp.where(rx, 1 - x, x), jnp.where(ry, 1 - y, y), jnp.where(rc, 1 - c, c)) for rx, ry, rc in rels]
        rcps = []
        for k, peer in enumerate(peers):
            cp = pltpu.make_async_remote_copy(src_ref=rep_ref, dst_ref=rep_all.at[me], send_sem=rsend_sems.at[k],
                                              recv_sem=rrecv_sems.at[k], device_id=peer, device_id_type=MESH_T)
            cp.start()
            rcps.append(cp)
        for k, peer in enumerate(peers):
            pltpu.make_async_remote_copy(src_ref=rep_ref, dst_ref=rep_all.at[_dev_index(*peer)], send_sem=rsend_sems.at[k],
                                         recv_sem=rrecv_sems.at[k], device_id=peer, device_id_type=MESH_T).wait_recv()
        for cp in cps:
            cp.wait_recv()
        for cp in cps + rcps:
            cp.wait_send()
        mine.wait()

    outs = pl.pallas_call(
        body, in_specs=[ANY] * (n + 1), out_specs=[ANY] * (n + 1),
        out_shape=[_sds(a.shape, a.dtype) for a in parts] + [_sds((N_DEV,) + rep.shape, rep.dtype)],
        scratch_shapes=[pltpu.SemaphoreType.DMA((n, 3)), pltpu.SemaphoreType.DMA((n, 3)), pltpu.SemaphoreType.DMA((7,)),
                        pltpu.SemaphoreType.DMA((7,)), pltpu.SemaphoreType.DMA],
        name="exchange_chips")(*parts, rep)
    return outs[:n], outs[n]


def adamw(name, terms, w, m, v):
    r, c = w.shape
    rb = min(r, 262144 // c)
    c1 = 1.0 - ADAM_B1 ** ADAM_STEP
    c2 = 1.0 - ADAM_B2 ** ADAM_STEP
    nt = len(terms)

    def body(*refs):
        w_ref, m_ref, v_ref = refs[nt:nt + 3]
        g_ref, d_ref, nm_ref, nv_ref = refs[nt + 3:]
        g = refs[0][...].astype(F32)
        for t_ref in refs[1:nt]:
            g = g + t_ref[...].astype(F32)
        nm = ADAM_B1 * m_ref[...] + (1.0 - ADAM_B1) * g
        nv = ADAM_B2 * v_ref[...] + (1.0 - ADAM_B2) * (g * g)
        g_ref[...] = g
        nm_ref[...] = nm
        nv_ref[...] = nv
        d_ref[...] = -ADAM_LR * ((nm / c1) / (jnp.sqrt(nv / c2) + ADAM_EPS) + ADAM_WD * w_ref[...])

    blk = pl.BlockSpec((rb, c), lambda i: (i, 0))
    tspecs = [blk if k is None else pl.BlockSpec((None, rb, c), lambda i, k=k: (k, i, 0)) for _, k in terms]
    return pl.pallas_call(body, grid=(r // rb,), in_specs=tspecs + [blk] * 3, out_specs=[blk] * 4,
                          out_shape=[_sds((r, c))] * 4, name=name,
                          compiler_params=_cp(("parallel",)))(*[t for t, _ in terms], w, m, v)


SMS_ROWS = 16
REP_ROWS = 24
N_BIG = 8
SMALL_SHARDED = (("rwkv_w2", (2, 32, 32)), ("rwkv_a2", (2, 32, 32)), ("rwkv_g2", (2, 64, 32)), ("rwkv_v2", (1, 32, 32)),
                 ("ssd_conv_w", (2, 4, 96)))
REPLICATED = (("lower_bounds", (2, 256)), ("mu_shift", (2, 896)), ("mu_vres", (1, 32)), ("rwkv_w0", (2, 256)),
              ("rwkv_a0", (2, 256)), ("rwkv_k_k", (2, 256)), ("rwkv_k_a", (2, 256)), ("rwkv_r_k", (2, 4, 64)),
              ("rwkv_lnx_w", (2, 256)), ("rwkv_lnx_b", (2, 256)), ("rwkv_v0", (1, 256)), ("ssd_conv_b", (2, 768)),
              ("ssd_dt_bias", (2, 4)), ("ssd_A_log", (2, 4)), ("ssd_D", (2, 4)), ("ssd_norm_w", (2, 256)),
              ("hgrn_norm_w", (2, 256)), ("ln1_w", (2, 1024)), ("ln1_b", (2, 1024)), ("ln2_w", (2, 1024)),
              ("ln2_b", (2, 1024)))


def _flat_rows(parts, rows):
    flat = jnp.concatenate([a.reshape(-1) for a in parts])
    return jnp.concatenate([flat, jnp.zeros((rows * PACK_W - flat.shape[0],), flat.dtype)]).reshape(rows, PACK_W)


def _local_arrays(d):
    arrs = [_w_in_pad(d["w_in"][0], None), _w_in_pad(d["w_in"][1], d["w_in_vres"][0]), d["w_out"][0], d["w_out"][1],
            d["w_up"][0].T, d["w_up"][1].T, d["w_down"][0], d["w_down"][1],
            _flat_rows([d[n] for n, _ in SMALL_SHARDED], SMS_ROWS)]
    return arrs, _flat_rows([d[n] for n, _ in REPLICATED], REP_ROWS)


def _unflat(rows2d, table):
    flat, out, o = rows2d.reshape(-1), {}, 0
    for name, shape in table:
        n = 1
        for s in shape:
            n *= s
        out[name] = flat[o:o + n].reshape(shape)
        o += n
    return out


def _from_local_arrays(arrs, rep):
    d = {}
    g0, _ = _w_in_unpad(arrs[0])
    g1, gv = _w_in_unpad(arrs[1])
    d["w_in"], d["w_in_vres"] = jnp.stack([g0, g1]), gv[None]
    d["w_out"] = jnp.stack([arrs[2], arrs[3]])
    d["w_up"] = jnp.stack([arrs[4].T, arrs[5].T])
    d["w_down"] = jnp.stack([arrs[6], arrs[7]])
    d.update(_unflat(arrs[8], SMALL_SHARDED))
    d.update(_unflat(rep, REPLICATED))
    return d


def _gathered_weights(gathered):
    full = [g.reshape(N_DEV * g.shape[1], g.shape[2]) for g in gathered[:N_BIG]]
    wts = [dict(w_in=full[l], w_out=full[2 + l], w_up_t=full[4 + l], w_down=full[6 + l]) for l in range(DEPTH)]
    small, flat, o = {}, gathered[N_BIG].reshape(N_DEV, -1), 0
    for name, shape in SMALL_SHARDED:
        n = shape[0] * shape[1] * shape[2]
        blk = flat[:, o:o + n].reshape((N_DEV,) + shape)
        small[name] = blk.transpose(1, 2, 0, 3).reshape(shape[0], shape[1], N_DEV * shape[2])
        o += n
    return wts, small


def _send_arrays(big, small_grads):
    blocks = lambda g: g.reshape(N_DEV, g.shape[0] // N_DEV, g.shape[1])
    arrs = [blocks(big[l][k]) for k in ("w_in", "w_out", "w_up_t", "w_down") for l in range(DEPTH)]
    sms = []
    for name, shape in SMALL_SHARDED:
        g = small_grads[name].reshape(shape[0], shape[1], N_DEV, shape[2]).transpose(2, 0, 1, 3)
        sms.append(g.reshape(N_DEV, -1))
    sms = jnp.concatenate(sms, axis=1)
    sms = jnp.concatenate([sms, jnp.zeros((N_DEV, SMS_ROWS * PACK_W - sms.shape[1]), F32)], axis=1)
    arrs.append(sms.reshape(N_DEV, SMS_ROWS, PACK_W))
    return arrs, _flat_rows([small_grads[n] for n, _ in REPLICATED], REP_ROWS)


def _local_step(x, tgt, wts, raw):
    consts = _consts()
    ps = [_layer_params(l, raw, consts) for l in range(DEPTH)]
    x1, sv0 = layer_fwd(0, x, None, wts[0], ps[0])
    x2, sv1 = layer_fwd(1, x1, sv0["fl"], wts[1], ps[1])
    dy, lparts = loss_call(x2, tgt)
    loss = jnp.sum(lparts[::8, 0])
    dx1, dvfirst, g1 = layer_bwd(1, dy, None, sv1, wts[1], ps[1])
    dx0, _, g0 = layer_bwd(0, dx1, dvfirst, sv0, wts[0], ps[0])
    big = [{k: g[k] for k in ("w_in", "w_out", "w_up_t", "w_down")} for g in (g0, g1)]
    return loss, dx0, big, _natural_grads(g0, g1)


WEIGHT_NAMES = ("lower_bounds", "w_in", "w_in_vres", "mu_shift", "mu_vres", "rwkv_w0", "rwkv_w2", "rwkv_a0", "rwkv_a2",
                "rwkv_g2", "rwkv_k_k", "rwkv_k_a", "rwkv_r_k", "rwkv_lnx_w", "rwkv_lnx_b", "rwkv_v0", "rwkv_v2",
                "ssd_conv_w", "ssd_conv_b", "ssd_dt_bias", "ssd_A_log", "ssd_D", "ssd_norm_w", "hgrn_norm_w", "w_out",
                "ln1_w", "ln1_b", "w_up", "w_down", "ln2_w", "ln2_b")


def kernel(x, lower_bounds, w_in, w_in_vres, mu_shift, mu_vres, rwkv_w0, rwkv_w2, rwkv_a0, rwkv_a2, rwkv_g2, rwkv_k_k, rwkv_k_a, rwkv_r_k, rwkv_lnx_w, rwkv_lnx_b, rwkv_v0, rwkv_v2, ssd_conv_w, ssd_conv_b, ssd_dt_bias, ssd_A_log, ssd_D, ssd_norm_w, hgrn_norm_w, w_out, ln1_w, ln1_b, w_up, w_down, ln2_w, ln2_b, loss_target, m_lower_bounds, m_w_in, m_w_in_vres, m_mu_shift, m_mu_vres, m_rwkv_w0, m_rwkv_w2, m_rwkv_a0, m_rwkv_a2, m_rwkv_g2, m_rwkv_k_k, m_rwkv_k_a, m_rwkv_r_k, m_rwkv_lnx_w, m_rwkv_lnx_b, m_rwkv_v0, m_rwkv_v2, m_ssd_conv_w, m_ssd_conv_b, m_ssd_dt_bias, m_ssd_A_log, m_ssd_D, m_ssd_norm_w, m_hgrn_norm_w, m_w_out, m_ln1_w, m_ln1_b, m_w_up, m_w_down, m_ln2_w, m_ln2_b, v_lower_bounds, v_w_in, v_w_in_vres, v_mu_shift, v_mu_vres, v_rwkv_w0, v_rwkv_w2, v_rwkv_a0, v_rwkv_a2, v_rwkv_g2, v_rwkv_k_k, v_rwkv_k_a, v_rwkv_r_k, v_rwkv_lnx_w, v_rwkv_lnx_b, v_rwkv_v0, v_rwkv_v2, v_ssd_conv_w, v_ssd_conv_b, v_ssd_dt_bias, v_ssd_A_log, v_ssd_D, v_ssd_norm_w, v_hgrn_norm_w, v_w_out, v_ln1_w, v_ln1_b, v_w_up, v_w_down, v_ln2_w, v_ln2_b):
    given = dict(locals())
    w = {n: given[n] for n in WEIGHT_NAMES}
    w_arrs, w_rep = _local_arrays(w)
    m_arrs, m_rep = _local_arrays({n: given["m_" + n] for n in WEIGHT_NAMES})
    v_arrs, v_rep = _local_arrays({n: given["v_" + n] for n in WEIGHT_NAMES})
    gathered = all_gather([a.astype(BF16) for a in w_arrs[:N_BIG]] + [w_arrs[N_BIG]])
    wts, small_full = _gathered_weights(gathered)
    raw = {n: w[n] for n, _ in REPLICATED}
    raw.update(small_full)
    loss, dx, big, small_grads = _local_step(x[0], loss_target[0], wts, raw)
    send, rep = _send_arrays(big, small_grads)
    mine, sib = exchange_siblings(send)
    own, parts = [], []
    for a in range(N_BIG + 1):
        o, pt = reduce_pair(f"reduce_pair{a}", mine[a], sib[a], BF16 if a < N_BIG else F32)
        own.append(o)
        parts.append(pt)
    recv, rep_all = exchange_chips(parts, rep)
    results = [adamw(f"adamw{a}", [(own[a], None), (recv[a], 0), (recv[a], 1), (recv[a], 2)], w_arrs[a], m_arrs[a], v_arrs[a])
               for a in range(N_BIG + 1)]
    rep_res = adamw("adamw_rep", [(rep_all, q) for q in range(N_DEV)], w_rep, m_rep, v_rep)
    loss = lax.psum(loss, ("x", "y", "c"))
    outs = [loss, dx[None]]
    for q in range(4):
        d = _from_local_arrays([res[q] for res in results], rep_res[q])
        outs += [d[n] for n in WEIGHT_NAMES]
    return tuple(outs)
```

```python
import functools

import jax
import jax.numpy as jnp
from jax import lax
from jax.experimental import pallas as pl
from jax.experimental.pallas import tpu as pltpu

F32 = jnp.float32
BF16 = jnp.bfloat16
HI = lax.Precision.HIGHEST

N_DEV = 8
SEQ = 2048
D_MODEL = 1024
D_FF = 4096
DG = 256
NH = 4
HD = 64
DEPTH = 2
ALPHA = (2.0 * DEPTH) ** 0.25
LN_EPS = 1e-5
RMS_EPS = 1e-5
GN_EPS = HD * 1e-5
IN_COLS = 3716
SSD_N = 128
SSD_CHUNK = 128
HGRN_CHUNK = 16
DILATED = ((128, 1), (512, 4), (2048, 16))

ADAM_LR, ADAM_B1, ADAM_B2, ADAM_EPS, ADAM_WD, ADAM_STEP = 0.001, 0.9, 0.999, 1e-08, 0.01, 10

PW = 4096
C_R, C_K, C_V = 0, 256, 512
C_AQ, C_AK, C_AV = 768, 1024, 1280
C_Z, C_XBC = 1536, 1792
C_HQ, C_HF, C_HI, C_HG = 2560, 2816, 3072, 3328
C_LORA, C_DT, C_VRES = 3584, 3712, 3840

RB = 256
VMEM_LIMIT = 56 * 1024 * 1024
PACK_W = 1024


def _cp(sem=None):
    return pltpu.CompilerParams(dimension_semantics=sem, vmem_limit_bytes=VMEM_LIMIT)


def _sds(shape, dt=F32):
    return jax.ShapeDtypeStruct(tuple(shape), dt)


def _rows(w, cb=0, rb=RB):
    return pl.BlockSpec((rb, w), lambda i: (i, cb))


def _full(shape):
    n = len(shape)
    return pl.BlockSpec(tuple(shape), lambda *_: (0,) * n)


def _sigmoid(x):
    return 1.0 / (1.0 + jnp.exp(-x))


def _silu(x):
    return x * _sigmoid(x)


def _softplus(x):
    return jnp.maximum(x, 0.0) + jnp.log(1.0 + jnp.exp(jnp.where(x > 0, -x, x)))


MID = lax.Precision.HIGH
NN, TN, NT = (((1,), (0,)), ((), ())), (((0,), (0,)), ((), ())), (((1,), (1,)), ((), ()))


def _dot(a, b):
    return lax.dot_general(a, b, NN, precision=MID, preferred_element_type=F32)


def _dot_tn(a, b):
    return lax.dot_general(a, b, TN, precision=MID, preferred_element_type=F32)


def _dot_nt(a, b):
    return lax.dot_general(a, b, NT, precision=MID, preferred_element_type=F32)


def _dotx(a, b):
    return lax.dot_general(a, b, NN, precision=HI, preferred_element_type=F32)


def _dotx_tn(a, b):
    return lax.dot_general(a, b, TN, precision=HI, preferred_element_type=F32)


def _seg_ones(n, seg):
    i = jnp.arange(n)
    return (i[:, None] // seg == i[None, :] // seg).astype(F32)


def _shift_down(x, s):
    row = lax.broadcasted_iota(jnp.int32, x.shape, 0)
    return jnp.where(row < s, 0.0, pltpu.roll(x, s, 0))


def _shift_up(x, s):
    n = x.shape[0]
    row = lax.broadcasted_iota(jnp.int32, x.shape, 0)
    return jnp.where(row >= n - s, 0.0, pltpu.roll(x, n - s, 0))


@functools.partial(jax.custom_vjp, nondiff_argnums=(1,))
def _tshift(x, s):
    return _shift_down(x, s)


def _tshift_fwd(x, s):
    return _shift_down(x, s), None


def _tshift_bwd(s, _, g):
    return (_shift_up(g, s),)


_tshift.defvjp(_tshift_fwd, _tshift_bwd)


def _map_fwd(name, fn, grid, ins, in_specs, out_shapes, out_specs):
    n_in = len(ins)

    def body(*refs):
        ys = fn(*[r[...] for r in refs[:n_in]])
        for r, y in zip(refs[n_in:], ys):
            r[...] = y

    return pl.pallas_call(body, grid=grid, in_specs=in_specs, out_specs=out_specs, out_shape=out_shapes,
                          name=name, compiler_params=_cp(("parallel",)))(*ins)


def _map_bwd(name, fn, grid, ins, in_specs, cts, ct_specs, want, acc=(), gout=None):
    n_in = len(ins)
    flat_cts = [c for group in cts for c in group]
    flat_specs = [s for group in ct_specs for s in group]
    n_ct = len(flat_cts)
    gout = gout or {}
    out_shapes = [gout[i][0] if i in gout else _sds(ins[i].shape) for i in want]
    out_specs = [gout[i][1] if i in gout else in_specs[i] for i in want]

    def body(*refs):
        xs = [r[...] for r in refs[:n_in]]
        cvals = [r[...] for r in refs[n_in:n_in + n_ct]]
        gouts = refs[n_in + n_ct:]
        cs, p = [], 0
        for group in cts:
            v = cvals[p]
            for q in range(1, len(group)):
                v = v + cvals[p + q]
            cs.append(v)
            p += len(group)

        def f(*wanted):
            full = list(xs)
            for i, w in zip(want, wanted):
                full[i] = w
            return tuple(fn(*full))

        _, vjp = jax.vjp(f, *[xs[i] for i in want])
        gs = vjp(tuple(cs))
        for o, i, g in zip(gouts, want, gs):
            if i in acc:
                @pl.when(pl.program_id(0) == 0)
                def _():
                    o[...] = jnp.zeros_like(o)

                o[...] += g
            else:
                o[...] = g

    sem = ("arbitrary",) if acc else ("parallel",)
    return pl.pallas_call(body, grid=grid, in_specs=list(in_specs) + flat_specs, out_specs=out_specs,
                          out_shape=out_shapes, name=name, compiler_params=_cp(sem))(*ins, *flat_cts)


def _addn(name, *arrs):
    n, c = arrs[0].shape

    def fn(*xs):
        r = xs[0]
        for x in xs[1:]:
            r = r + x
        return (r,)

    return _map_fwd(name, fn, (n // RB,), list(arrs), [_rows(c)] * len(arrs), [_sds((n, c))], [_rows(c)])[0]


def _mm(name, a, b, mode, tm, tn, tk, add=None, add_scale=1.0):
    if mode == "nn":
        (m, k), n = a.shape, b.shape[1]
    elif mode == "nt":
        (m, k), n = a.shape, b.shape[0]
    else:
        (k, m), n = a.shape, b.shape[1]
    nk = k // tk
    dn = {"nn": (((1,), (0,)), ((), ())), "nt": (((1,), (1,)), ((), ())), "tn": (((0,), (0,)), ((), ()))}[mode]

    def body(*refs):
        if add is None:
            a_ref, b_ref, o_ref, acc = refs
        else:
            a_ref, b_ref, add_ref, o_ref, acc = refs
        kk = pl.program_id(2)

        @pl.when(kk == 0)
        def _():
            acc[...] = jnp.zeros_like(acc)

        acc[...] += lax.dot_general(a_ref[...].astype(BF16), b_ref[...].astype(BF16), dn, preferred_element_type=F32)

        @pl.when(kk == nk - 1)
        def _():
            r = acc[...]
            if add is not None:
                r = r + add_scale * add_ref[...]
            o_ref[...] = r

    a_spec = pl.BlockSpec((tk, tm), lambda i, j, q: (q, i)) if mode == "tn" else pl.BlockSpec((tm, tk), lambda i, j, q: (i, q))
    b_spec = pl.BlockSpec((tn, tk), lambda i, j, q: (j, q)) if mode == "nt" else pl.BlockSpec((tk, tn), lambda i, j, q: (q, j))
    o_spec = pl.BlockSpec((tm, tn), lambda i, j, q: (i, j))
    ins, specs = [a, b], [a_spec, b_spec]
    if add is not None:
        ins.append(add)
        specs.append(o_spec)
    return pl.pallas_call(body, grid=(m // tm, n // tn, nk), in_specs=specs, out_specs=o_spec, out_shape=_sds((m, n)),
                          scratch_shapes=[pltpu.VMEM((tm, tn), F32)], name=name,
                          compiler_params=_cp(("parallel", "parallel", "arbitrary")))(*ins)


LERP_BLOCKS = (0, 1, 2, 3, 4, 5, C_LORA // 128, C_VRES // 128)


def _lerp_colmap(j):
    r = jnp.where(j < 6, j, jnp.where(j == 6, C_LORA // 128, C_VRES // 128))
    return (0, r)


def _lerp_fn(f, mu):
    return (f + (_tshift(f, 1) - f) * mu,)


def _lerp_specs():
    return [pl.BlockSpec((SEQ, 128), _lerp_colmap), pl.BlockSpec((1, 128), lambda j: (0, j))]


def lerp_fwd(l, proj, mu):
    return _map_fwd(f"lerp_fwd{l}", _lerp_fn, (8,), [proj, mu], _lerp_specs(), [_sds((SEQ, 1024))],
                    [pl.BlockSpec((SEQ, 128), lambda j: (0, j))])[0]


def lerp_bwd(l, proj, mu, dfl):
    n_in = 2

    def body(f_ref, mu_ref, g_ref, df_ref, dmu_ref):
        _, vjp = jax.vjp(_lerp_fn, f_ref[...], mu_ref[...])
        df, dmu = vjp((g_ref[...],))
        df_ref[...] = df
        dmu_ref[...] = dmu

    cspec = pl.BlockSpec((SEQ, 128), lambda j: (0, j))
    return pl.pallas_call(body, grid=(8,), in_specs=_lerp_specs() + [cspec],
                          out_specs=[cspec, pl.BlockSpec((1, 128), lambda j: (0, j))],
                          out_shape=[_sds((SEQ, 1024)), _sds((1, 1024))], name=f"lerp_bwd{l}",
                          compiler_params=_cp(("parallel",)))(proj, mu, dfl)


def _conv_fn(x, w, b):
    y = x * w[3:4, :] + _tshift(x, 1) * w[2:3, :] + _tshift(x, 2) * w[1:2, :] + _tshift(x, 3) * w[0:1, :] + b
    return (_silu(y),)


def _conv_specs():
    return [pl.BlockSpec((SEQ, 128), lambda j: (0, C_XBC // 128 + j)), pl.BlockSpec((4, 128), lambda j: (0, j)),
            pl.BlockSpec((1, 128), lambda j: (0, j))]


def conv_fwd(l, proj, w, b):
    return _map_fwd(f"conv_fwd{l}", _conv_fn, (6,), [proj, w, b], _conv_specs(), [_sds((SEQ, 768))],
                    [pl.BlockSpec((SEQ, 128), lambda j: (0, j))])[0]


def conv_bwd(l, proj, w, b, dxc):
    def body(x_ref, w_ref, b_ref, g_ref, dx_ref, dw_ref, db_ref):
        _, vjp = jax.vjp(_conv_fn, x_ref[...], w_ref[...], b_ref[...])
        dx, dw, db = vjp((g_ref[...],))
        dx_ref[...] = dx
        dw_ref[...] = dw
        db_ref[...] = db

    cspec = pl.BlockSpec((SEQ, 128), lambda j: (0, j))
    return pl.pallas_call(body, grid=(6,), in_specs=_conv_specs() + [cspec],
                          out_specs=[cspec, pl.BlockSpec((4, 128), lambda j: (0, j)), pl.BlockSpec((1, 128), lambda j: (0, j))],
                          out_shape=[_sds((SEQ, 768)), _sds((4, 768)), _sds((1, 768))], name=f"conv_bwd{l}",
                          compiler_params=_cp(("parallel",)))(proj, w, b, dxc)


def _rwkv_pre_fn(has_vres):
    def fn(fk, fv, flora, *rest):
        if has_vres:
            fvres, vfirst, w0, w2p, a0, a2p, g2p, k_k, k_a, v0, v2p, seg = rest
        else:
            w0, w2p, a0, a2p, g2p, k_k, k_a, seg = rest
        w_log = -_softplus(-(w0 + _dot(jnp.tanh(flora), w2p))) - 0.5
        w = jnp.exp(-jnp.exp(w_log))
        a = _sigmoid(a0 + _dot(flora, a2p))
        g = _dot(_sigmoid(flora), g2p)
        if has_vres:
            v2 = fv + (vfirst - fv) * _sigmoid(v0 + _dot(fvres, v2p))
        else:
            v2 = fv * 1.0
        kk = fk * k_k
        kk = kk / jnp.maximum(jnp.sqrt(_dot(kk * kk, seg)), 1e-12)
        k2 = fk * (1.0 + (a - 1.0) * k_a)
        return w, k2, v2, -kk, kk * a, g

    return fn


def _rwkv_pre_args(fl, vfirst, p, has_vres):
    ins = [fl, fl, fl]
    specs = [_rows(256, 1), _rows(256, 2), _rows(128, 6)]
    if has_vres:
        ins += [fl, vfirst]
        specs += [_rows(128, 7), _rows(256, 2)]
    names = ["w0", "w2p", "a0", "a2p", "g2p", "k_k", "k_a"] + (["v0", "v2p"] if has_vres else []) + ["seg64"]
    for nme in names:
        ins.append(p[nme])
        specs.append(_full(p[nme].shape))
    return ins, specs, names


def rwkv_pre_fwd(l, fl, vfirst, p):
    has_vres = l > 0
    ins, specs, _ = _rwkv_pre_args(fl, vfirst, p, has_vres)
    return _map_fwd(f"rwkv_pre_fwd{l}", _rwkv_pre_fn(has_vres), (SEQ // RB,), ins, specs,
                    [_sds((SEQ, DG))] * 6, [_rows(DG)] * 6)


def rwkv_pre_bwd(l, fl, vfirst, p, cts):
    has_vres = l > 0
    ins, specs, names = _rwkv_pre_args(fl, vfirst, p, has_vres)
    n_row = 5 if has_vres else 3
    want = list(range(n_row)) + [n_row + i for i, nme in enumerate(names) if nme != "seg64"]
    acc = tuple(w for w in want if w >= n_row)
    ct_specs = [[_rows(DG)] * len(g) for g in cts]
    gout = {0: (_sds((SEQ, DG)), _rows(DG)), 1: (_sds((SEQ, DG)), _rows(DG)), 2: (_sds((SEQ, 128)), _rows(128))}
    if has_vres:
        gout[3] = (_sds((SEQ, 128)), _rows(128))
        gout[4] = (_sds((SEQ, DG)), _rows(DG))
    gs = _map_bwd(f"rwkv_pre_bwd{l}", _rwkv_pre_fn(has_vres), (SEQ // RB,), ins, specs, cts, ct_specs, want, acc, gout)
    keys = ["fk", "fv", "flora"] + (["fvres", "vfirst"] if has_vres else []) + [nme for nme in names if nme != "seg64"]
    return dict(zip(keys, gs))


def _rwkv_post_fn(y, fr, k2, v2, g, lnx_w, lnx_b, r_k, seg):
    mu = _dot(y, seg) * (1.0 / HD)
    d = y - mu
    var = _dot(d * d, seg) * (1.0 / HD)
    yn = d * lax.rsqrt(var + GN_EPS) * lnx_w + lnx_b
    bonus = _dot(fr * k2 * r_k, seg) * v2
    return ((yn + bonus) * g,)


def _rwkv_post_args(y, fl, k2, v2, g, p):
    ins = [y, fl, k2, v2, g, p["lnx_w"], p["lnx_b"], p["r_k"], p["seg64"]]
    specs = [_rows(DG), _rows(DG, 0), _rows(DG), _rows(DG), _rows(DG)] + [_full(x.shape) for x in ins[5:]]
    return ins, specs


def rwkv_post_fwd(l, y, fl, k2, v2, g, p):
    ins, specs = _rwkv_post_args(y, fl, k2, v2, g, p)
    return _map_fwd(f"rwkv_post_fwd{l}", _rwkv_post_fn, (SEQ // RB,), ins, specs, [_sds((SEQ, DG))], [_rows(DG)])[0]


def rwkv_post_bwd(l, y, fl, k2, v2, g, p, dya):
    ins, specs = _rwkv_post_args(y, fl, k2, v2, g, p)
    gs = _map_bwd(f"rwkv_post_bwd{l}", _rwkv_post_fn, (SEQ // RB,), ins, specs, [[dya]], [[_rows(DG)]],
                  want=[0, 1, 2, 3, 4, 5, 6, 7], acc=(5, 6, 7), gout={1: (_sds((SEQ, DG)), _rows(DG))})
    return dict(zip(["y", "fr", "k2", "v2", "g", "lnx_w", "lnx_b", "r_k"], gs))


SCAN_TB = 64


def _coltile8(rows8, dmask, ones_stack, parts):
    pieces, rest = [], rows8
    for q in range(parts):
        piece = rest.astype(BF16).astype(F32)
        if q < parts - 1:
            rest = rest - piece
        pieces.append((piece[:, None, :] * dmask[None]).reshape(8 * HD, DG).astype(BF16))
    x = pieces[0] if parts == 1 else jnp.concatenate(pieces, axis=1)
    return jnp.dot(x, ones_stack, preferred_element_type=F32).reshape(8, HD, DG)


def _coltiles_bf16(rows_list, dmask, ones_bf16):
    x = jnp.concatenate([(r8[:, None, :] * dmask[None]).reshape(8 * HD, DG).astype(BF16) for r8 in rows_list], axis=0)
    t = jnp.dot(x, ones_bf16, preferred_element_type=F32)
    return [t[q * 8 * HD:(q + 1) * 8 * HD].reshape(8, HD, DG) for q in range(len(rows_list))]


def _segrows8(x8, dmask, ones_bf16):
    t = jnp.dot(x8.reshape(8 * HD, DG).astype(BF16), ones_bf16, preferred_element_type=F32).reshape(8, HD, DG)
    return jnp.sum(t * dmask[None], axis=1)


def rwkv_scan_fwd(l, fl, w, k2, v2, c, b, p):
    nblk = SEQ // SCAN_TB

    def body(r_ref, w_ref, k_ref, v_ref, c_ref, b_ref, ones_ref, dm_ref, y_ref, st_ref, s_sc):
        @pl.when(pl.program_id(0) == 0)
        def _():
            s_sc[...] = jnp.zeros_like(s_sc)

        ones3, ones = ones_ref[...], ones_ref[0:DG, :]
        dmask = dm_ref[...]

        def group(gi, carry):
            t0 = pl.multiple_of(gi * 8, 8)
            sl = pl.ds(t0, 8)
            v8 = v_ref[sl, :]
            wt = _coltile8(w_ref[sl, :], dmask, ones3, 3)
            ct, bt, kt, rt = _coltiles_bf16([c_ref[sl, :], b_ref[sl, :], k_ref[sl, :], r_ref[sl, :]], dmask, ones)
            t = s_sc[...]
            for j in range(8):
                sa = jnp.sum(t * ct[j], axis=0, keepdims=True)
                t = t * wt[j] + bt[j] * sa + kt[j] * v8[j:j + 1, :]
                st_ref[t0 + j] = t
            s_sc[...] = t
            y_ref[sl, :] = jnp.sum(st_ref[sl] * rt, axis=1)
            return carry

        lax.fori_loop(0, SCAN_TB // 8, group, 0)

    row = pl.BlockSpec((SCAN_TB, DG), lambda i: (i, 0))
    ins = [fl, w, k2, v2, c, b, p["seg64x3_bf16"], p["dmask"]]
    specs = [row] * 6 + [_full((3 * DG, DG)), _full((HD, DG))]
    return pl.pallas_call(body, grid=(nblk,), in_specs=specs,
                          out_specs=[row, pl.BlockSpec((SCAN_TB, HD, DG), lambda i: (i, 0, 0))],
                          out_shape=[_sds((SEQ, DG)), _sds((SEQ, HD, DG))],
                          scratch_shapes=[pltpu.VMEM((HD, DG), F32)], name=f"rwkv_scan_fwd{l}",
                          compiler_params=_cp(("arbitrary",)))(*ins)


def rwkv_scan_bwd(l, fl, w, k2, v2, c, b, states, dy, p):
    nblk = SEQ // SCAN_TB

    def body(r_ref, w_ref, k_ref, v_ref, c_ref, b_ref, dy_ref, st_ref, sp_ref, ones_ref, dm_ref,
             dr_ref, dw_ref, dk_ref, dv_ref, dc_ref, db_ref, g_sc, prev_sc, d8_sc, dsa_sc):
        i = pl.program_id(0)

        @pl.when(i == 0)
        def _():
            g_sc[...] = jnp.zeros_like(g_sc)

        ones3, ones = ones_ref[...], ones_ref[0:DG, :]
        dmask = dm_ref[...]
        first_block = i == nblk - 1

        def group(gr, carry):
            gi = SCAN_TB // 8 - 1 - gr
            t0 = pl.multiple_of(gi * 8, 8)
            sl = pl.ds(t0, 8)
            v8, dy8 = v_ref[sl, :], dy_ref[sl, :]
            t8 = st_ref[sl]
            @pl.when(gi > 0)
            def _():
                prev_sc[0] = st_ref[t0 - 1]

            @pl.when(gi == 0)
            def _():
                prev_sc[0] = jnp.where(first_block, 0.0, sp_ref[0])

            for j in range(1, 8):
                prev_sc[j] = t8[j - 1]
            tp8 = prev_sc[...]
            wt = _coltile8(w_ref[sl, :], dmask, ones3, 3)
            ct, bt, kt, rt = _coltiles_bf16([c_ref[sl, :], b_ref[sl, :], k_ref[sl, :], r_ref[sl, :]], dmask, ones)
            sa8 = jnp.sum(tp8 * ct, axis=1)
            g = g_sc[...]
            for j in range(7, -1, -1):
                g = g + rt[j] * dy8[j:j + 1, :]
                d8_sc[j] = g
                dsa = jnp.sum(g * bt[j], axis=0, keepdims=True)
                dsa_sc[j:j + 1, :] = dsa
                g = g * wt[j] + ct[j] * dsa
            g_sc[...] = g
            d8 = d8_sc[...]
            dsa8 = dsa_sc[...]
            dv_ref[sl, :] = jnp.sum(d8 * kt, axis=1)
            dr_ref[sl, :] = _segrows8(t8 * dy8[:, None, :], dmask, ones)
            dk_ref[sl, :] = _segrows8(d8 * v8[:, None, :], dmask, ones)
            dw_ref[sl, :] = _segrows8(tp8 * d8, dmask, ones)
            db_ref[sl, :] = _segrows8(d8 * sa8[:, None, :], dmask, ones)
            dc_ref[sl, :] = _segrows8(tp8 * dsa8[:, None, :], dmask, ones)
            return carry

        lax.fori_loop(0, SCAN_TB // 8, group, 0)

    row = pl.BlockSpec((SCAN_TB, DG), lambda i: (nblk - 1 - i, 0))
    st_spec = pl.BlockSpec((SCAN_TB, HD, DG), lambda i: (nblk - 1 - i, 0, 0))
    sp_spec = pl.BlockSpec((1, HD, DG), lambda i: (jnp.maximum((nblk - 1 - i) * SCAN_TB - 1, 0), 0, 0))
    ins = [fl, w, k2, v2, c, b, dy, states, states, p["seg64x3_bf16"], p["dmask"]]
    specs = [row] * 7 + [st_spec, sp_spec, _full((3 * DG, DG)), _full((HD, DG))]
    tile8 = pltpu.VMEM((8, HD, DG), F32)
    return pl.pallas_call(body, grid=(nblk,), in_specs=specs, out_specs=[row] * 6, out_shape=[_sds((SEQ, DG))] * 6,
                          scratch_shapes=[pltpu.VMEM((HD, DG), F32), tile8, tile8, pltpu.VMEM((8, DG), F32)],
                          name=f"rwkv_scan_bwd{l}", compiler_params=_cp(("arbitrary",)))(*ins)


HG_ROWS = 128


def _hgrn_chunk_fn(layer):
    def fn(hq, hf, hi, hg, sprev, lb0, lb1, norm_w, seg, bd, tri, causal, ones16):
        e0 = jnp.exp(lb0 - jnp.maximum(lb0, lb1))
        e1 = jnp.exp(lb1 - jnp.maximum(lb0, lb1))
        sm0, sm1 = e0 / (e0 + e1), e1 / (e0 + e1)
        lb = (sm0 - sm0) if layer == 0 else ((sm0 + sm1) - sm0)
        forget = lb + (1.0 - lb) * _sigmoid(hf)
        logf = jnp.log(forget)
        kk = 1.0 - forget
        q = _silu(hq)
        c = HGRN_CHUNK
        b = _dotx(tri, logf)
        bl = jnp.sum(logf, axis=0, keepdims=True)
        diff = (b[:, None, :] - b[None, :, :]).reshape(c * c, DG)
        dec = jnp.exp(jnp.where(causal > 0.5, diff, -1e30))
        qrep = jnp.broadcast_to(q[:, None, :], (c, c, DG)).reshape(c * c, DG)
        ktil = jnp.broadcast_to(kk[None, :, :], (c, c, DG)).reshape(c * c, DG)
        vtil = jnp.broadcast_to(hi[None, :, :], (c, c, DG)).reshape(c * c, DG)
        att = _dot(qrep * ktil * dec, seg)
        o_intra = jnp.sum((att * vtil).reshape(c, c, DG), axis=1)
        kdec = kk * jnp.exp(bl - b)
        u = _dot_tn(kdec, hi) * bd
        tot = jnp.exp(_dotx_tn(logf, ones16))
        snext = sprev * tot + u
        o = o_intra + _dot(q * jnp.exp(b), sprev)
        ms = _dot(o * o, seg) * (1.0 / HD)
        y = o * lax.rsqrt(ms + RMS_EPS) * norm_w * _silu(hg)
        return y, snext

    return fn


def _hgrn_consts(p):
    return [p["seg64"], p["seg64"], p["tri16"], p["causal16"], p["ones16"]]


def hgrn_fwd(l, proj, p):
    fn = _hgrn_chunk_fn(l)
    nch = HG_ROWS // HGRN_CHUNK

    def body(hq_ref, hf_ref, hi_ref, hg_ref, lb0_ref, lb1_ref, nw_ref, seg_ref, bd_ref, tri_ref, cau_ref, o16_ref,
             y_ref, st_ref, s_sc):
        @pl.when(pl.program_id(0) == 0)
        def _():
            s_sc[...] = jnp.zeros_like(s_sc)

        consts = (lb0_ref[...], lb1_ref[...], nw_ref[...], seg_ref[...], bd_ref[...], tri_ref[...], cau_ref[...],
                  o16_ref[...])

        def chunk(ci, carry):
            sl = pl.ds(pl.multiple_of(ci * HGRN_CHUNK, HGRN_CHUNK), HGRN_CHUNK)
            sprev = s_sc[...]
            st_ref[ci] = sprev
            y, snext = fn(hq_ref[sl, :], hf_ref[sl, :], hi_ref[sl, :], hg_ref[sl, :], sprev, *consts)
            y_ref[sl, :] = y
            s_sc[...] = snext
            return carry

        lax.fori_loop(0, nch, chunk, 0)

    rows = lambda cb: pl.BlockSpec((HG_ROWS, DG), lambda i: (i, cb))
    ins = [proj, proj, proj, proj, p["lb0"], p["lb1"], p["hgrn_norm_w"]] + _hgrn_consts(p)
    specs = [rows(C_HQ // DG), rows(C_HF // DG), rows(C_HI // DG), rows(C_HG // DG)] + [_full(x.shape) for x in ins[4:]]
    return pl.pallas_call(body, grid=(SEQ // HG_ROWS,), in_specs=specs,
                          out_specs=[rows(0), pl.BlockSpec((nch, DG, DG), lambda i: (i, 0, 0))],
                          out_shape=[_sds((SEQ, DG)), _sds((SEQ // HGRN_CHUNK, DG, DG))],
                          scratch_shapes=[pltpu.VMEM((DG, DG), F32)], name=f"hgrn_fwd{l}",
                          compiler_params=_cp(("arbitrary",)))(*ins)


def hgrn_bwd(l, proj, states, dy, p):
    fn = _hgrn_chunk_fn(l)
    nch = HG_ROWS // HGRN_CHUNK
    nblk = SEQ // HG_ROWS

    def body(hq_ref, hf_ref, hi_ref, hg_ref, st_ref, dy_ref, lb0_ref, lb1_ref, nw_ref, seg_ref, bd_ref, tri_ref,
             cau_ref, o16_ref, dp_ref, dlb0_ref, dlb1_ref, dnw_ref, ds_sc):
        @pl.when(pl.program_id(0) == 0)
        def _():
            ds_sc[...] = jnp.zeros_like(ds_sc)
            dlb0_ref[...] = jnp.zeros_like(dlb0_ref)
            dlb1_ref[...] = jnp.zeros_like(dlb1_ref)
            dnw_ref[...] = jnp.zeros_like(dnw_ref)

        consts = (seg_ref[...], bd_ref[...], tri_ref[...], cau_ref[...], o16_ref[...])

        def chunk(cr, carry):
            ci = nch - 1 - cr
            sl = pl.ds(pl.multiple_of(ci * HGRN_CHUNK, HGRN_CHUNK), HGRN_CHUNK)
            f = lambda hq, hf, hi, hg, sp, b0, b1, nw: fn(hq, hf, hi, hg, sp, b0, b1, nw, *consts)
            _, vjp = jax.vjp(f, hq_ref[sl, :], hf_ref[sl, :], hi_ref[sl, :], hg_ref[sl, :], st_ref[ci],
                             lb0_ref[...], lb1_ref[...], nw_ref[...])
            dhq, dhf, dhi, dhg, dsp, dlb0, dlb1, dnw = vjp((dy_ref[sl, :], ds_sc[...]))
            dp_ref[sl, 0:DG] = dhq
            dp_ref[sl, DG:2 * DG] = dhf
            dp_ref[sl, 2 * DG:3 * DG] = dhi
            dp_ref[sl, 3 * DG:4 * DG] = dhg
            ds_sc[...] = dsp
            dlb0_ref[...] += dlb0
            dlb1_ref[...] += dlb1
            dnw_ref[...] += dnw
            return carry

        lax.fori_loop(0, nch, chunk, 0)

    rows = lambda cb: pl.BlockSpec((HG_ROWS, DG), lambda i: (nblk - 1 - i, cb))
    ins = [proj, proj, proj, proj, states, dy, p["lb0"], p["lb1"], p["hgrn_norm_w"]] + _hgrn_consts(p)
    specs = [rows(C_HQ // DG), rows(C_HF // DG), rows(C_HI // DG), rows(C_HG // DG),
             pl.BlockSpec((nch, DG, DG), lambda i: (nblk - 1 - i, 0, 0)), rows(0)] + [_full(x.shape) for x in ins[6:]]
    return pl.pallas_call(body, grid=(nblk,), in_specs=specs,
                          out_specs=[pl.BlockSpec((HG_ROWS, 4 * DG), lambda i: (nblk - 1 - i, 0)), _full((1, DG)),
                                     _full((1, DG)), _full((1, DG))],
                          out_shape=[_sds((SEQ, 4 * DG)), _sds((1, DG)), _sds((1, DG)), _sds((1, DG))],
                          scratch_shapes=[pltpu.VMEM((DG, DG), F32)], name=f"hgrn_bwd{l}",
                          compiler_params=_cp(("arbitrary",)))(*ins)


def _ssd_chunk_fn(z, xs, bm, cm, dtr, sprev, dt_bias, a_log, d_par, norm_w, e128, tri, trit, seg128, ones128):
    lc = SSD_CHUNK
    dt = _softplus(dtr + dt_bias)
    a = -jnp.exp(a_log)
    da = dt * a * (lax.broadcasted_iota(jnp.int32, (1, 128), 1) < NH).astype(F32)
    cs = _dotx(tri, da)
    cst = _dotx_tn(da, trit)
    cs_b = _dotx(cs, e128)
    dt_b = _dotx(dt, e128)
    csl_b = _dotx(jnp.sum(da, axis=0, keepdims=True), e128)
    xdt = xs * dt_b
    lane = lax.broadcasted_iota(jnp.int32, (1, DG), 1)
    rowi = lax.broadcasted_iota(jnp.int32, (lc, lc), 0)
    coli = lax.broadcasted_iota(jnp.int32, (lc, lc), 1)
    y = jnp.zeros((lc, DG), F32)
    snew = jnp.zeros((DG, SSD_N), F32)
    d_b = jnp.zeros((1, DG), F32)
    wdec = xdt * jnp.exp(csl_b - cs_b)
    for g in range(2):
        bg = bm[:, g * SSD_N:(g + 1) * SSD_N]
        cg = cm[:, g * SSD_N:(g + 1) * SSD_N]
        gmat = _dot_nt(cg, bg)
        gmask = ((lane // 128) == g).astype(F32)
        snew = snew + _dot_tn(wdec * gmask, bg)
        y = y + _dot_nt(cg, sprev) * gmask * jnp.exp(cs_b)
        for hh in range(2):
            h = 2 * g + hh
            seg = jnp.where(rowi >= coli, cs[:, h:h + 1] - cst[h:h + 1, :], -1e30)
            hmask = ((lane // HD) == h).astype(F32)
            y = y + _dot(gmat * jnp.exp(seg), xdt * hmask)
            d_b = d_b + d_par[:, h:h + 1] * hmask
    cd = jnp.exp(_dotx_tn(_dotx(da, e128), ones128))
    snext = sprev * cd + snew
    y = y + xs * d_b
    y = y * _silu(z)
    ms = _dot(y * y, seg128) * (1.0 / 128.0)
    return y * lax.rsqrt(ms + RMS_EPS) * norm_w, snext


def ssd_fwd(l, proj, xc, p):
    nc = SEQ // SSD_CHUNK

    def body(z_ref, xs_ref, b_ref, c_ref, dt_ref, dtb_ref, al_ref, d_ref, nw_ref, e_ref, tri_ref, trit_ref, sg_ref,
             on_ref, y_ref, st_ref, s_sc):
        @pl.when(pl.program_id(0) == 0)
        def _():
            s_sc[...] = jnp.zeros_like(s_sc)

        sprev = s_sc[...]
        st_ref[0] = sprev
        y, snext = _ssd_chunk_fn(z_ref[...], xs_ref[...], b_ref[...], c_ref[...], dt_ref[...], sprev, dtb_ref[...],
                                 al_ref[...], d_ref[...], nw_ref[...], e_ref[...], tri_ref[...], trit_ref[...],
                                 sg_ref[...], on_ref[...])
        y_ref[...] = y
        s_sc[...] = snext

    rw = lambda w, cb: pl.BlockSpec((SSD_CHUNK, w), lambda i: (i, cb))
    ins = [proj, xc, xc, xc, proj, p["dt_bias"], p["a_log"], p["ssd_d"], p["ssd_norm_w"], p["e128"], p["tri128"],
           p["tri128t"], p["seg128"], p["ones128"]]
    specs = [rw(DG, C_Z // DG), rw(DG, 0), rw(DG, 1), rw(DG, 2), rw(128, C_DT // 128)] + [_full(x.shape) for x in ins[5:]]
    return pl.pallas_call(body, grid=(nc,), in_specs=specs,
                          out_specs=[rw(DG, 0), pl.BlockSpec((1, DG, SSD_N), lambda i: (i, 0, 0))],
                          out_shape=[_sds((SEQ, DG)), _sds((nc, DG, SSD_N))],
                          scratch_shapes=[pltpu.VMEM((DG, SSD_N), F32)], name=f"ssd_fwd{l}",
                          compiler_params=_cp(("arbitrary",)))(*ins)


def ssd_bwd(l, proj, xc, states, dy, p):
    nc = SEQ // SSD_CHUNK

    def body(z_ref, xs_ref, b_ref, c_ref, dt_ref, st_ref, dy_ref, dtb_ref, al_ref, d_ref, nw_ref, e_ref, tri_ref,
             trit_ref, sg_ref, on_ref, dz_ref, dxc_ref, ddt_ref, ddtb_ref, dal_ref, dd_ref, dnw_ref, ds_sc):
        @pl.when(pl.program_id(0) == 0)
        def _():
            ds_sc[...] = jnp.zeros_like(ds_sc)
            ddtb_ref[...] = jnp.zeros_like(ddtb_ref)
            dal_ref[...] = jnp.zeros_like(dal_ref)
            dd_ref[...] = jnp.zeros_like(dd_ref)
            dnw_ref[...] = jnp.zeros_like(dnw_ref)

        consts = (e_ref[...], tri_ref[...], trit_ref[...], sg_ref[...], on_ref[...])
        f = lambda *a: _ssd_chunk_fn(*a, *consts)
        _, vjp = jax.vjp(f, z_ref[...], xs_ref[...], b_ref[...], c_ref[...], dt_ref[...], st_ref[0], dtb_ref[...],
                         al_ref[...], d_ref[...], nw_ref[...])
        dz, dxs, db, dc, ddt, dsp, ddtb, dal, dd, dnw = vjp((dy_ref[...], ds_sc[...]))
        dz_ref[...] = dz
        dxc_ref[:, 0:DG] = dxs
        dxc_ref[:, DG:2 * DG] = db
        dxc_ref[:, 2 * DG:3 * DG] = dc
        ddt_ref[...] = ddt
        ds_sc[...] = dsp
        ddtb_ref[...] += ddtb
        dal_ref[...] += dal
        dd_ref[...] += dd
        dnw_ref[...] += dnw

    rw = lambda w, cb: pl.BlockSpec((SSD_CHUNK, w), lambda i: (nc - 1 - i, cb))
    ins = [proj, xc, xc, xc, proj, states, dy, p["dt_bias"], p["a_log"], p["ssd_d"], p["ssd_norm_w"], p["e128"],
           p["tri128"], p["tri128t"], p["seg128"], p["ones128"]]
    specs = [rw(DG, C_Z // DG), rw(DG, 0), rw(DG, 1), rw(DG, 2), rw(128, C_DT // 128),
             pl.BlockSpec((1, DG, SSD_N), lambda i: (nc - 1 - i, 0, 0)), rw(DG, 0)] + [_full(x.shape) for x in ins[7:]]
    return pl.pallas_call(body, grid=(nc,), in_specs=specs,
                          out_specs=[rw(DG, 0), rw(3 * DG, 0), rw(128, 0), _full((1, 128)), _full((1, 128)), _full((1, 128)),
                                     _full((1, DG))],
                          out_shape=[_sds((SEQ, DG)), _sds((SEQ, 3 * DG)), _sds((SEQ, 128)), _sds((1, 128)), _sds((1, 128)),
                                     _sds((1, 128)), _sds((1, DG))],
                          scratch_shapes=[pltpu.VMEM((DG, SSD_N), F32)], name=f"ssd_bwd{l}",
                          compiler_params=_cp(("arbitrary",)))(*ins)


ATT_BLK = 128


def _slope(h):
    return jnp.where(h == 0, 0.25, jnp.where(h == 1, 0.0625, jnp.where(h == 2, 0.015625, 0.00390625))).astype(F32)


def _att_scores(qn, kc, kp, h, dil, has_prev):
    i = lax.broadcasted_iota(jnp.int32, (ATT_BLK, ATT_BLK), 0)
    j = lax.broadcasted_iota(jnp.int32, (ATT_BLK, ATT_BLK), 1)
    slope = _slope(h)
    scale = HD ** -0.5
    s_c = _dot_nt(qn, kc) * scale - slope * ((i - j) * dil).astype(F32)
    s_p = _dot_nt(qn, kp) * scale - slope * ((ATT_BLK + i - j) * dil).astype(F32)
    m_c = j <= i
    m_p = jnp.logical_and(j >= i, has_prev)
    return jnp.where(m_c, s_c, -1e30), jnp.where(m_p, s_p, -1e30), m_c, m_p


def attn_branch_fwd(l, bi, qs, ks, vs):
    dil, _, ln, _ = qs.shape
    nb = ln // ATT_BLK

    def body(q_ref, k_ref, v_ref, o_ref, l_ref):
        h = pl.program_id(1)

        def blk(n, carry):
            r0 = pl.multiple_of(n * ATT_BLK, ATT_BLK)
            rp = pl.multiple_of(jnp.maximum(n - 1, 0) * ATT_BLK, ATT_BLK)
            qn = q_ref[0, 0, pl.ds(r0, ATT_BLK), :]
            kc, vc = k_ref[0, 0, pl.ds(r0, ATT_BLK), :], v_ref[0, 0, pl.ds(r0, ATT_BLK), :]
            kp, vp = k_ref[0, 0, pl.ds(rp, ATT_BLK), :], v_ref[0, 0, pl.ds(rp, ATT_BLK), :]
            s_c, s_p, m_c, m_p = _att_scores(qn, kc, kp, h, dil, n > 0)
            m = jnp.maximum(jnp.max(s_c, axis=1, keepdims=True), jnp.max(s_p, axis=1, keepdims=True))
            p_c = jnp.where(m_c, jnp.exp(s_c - m), 0.0)
            p_p = jnp.where(m_p, jnp.exp(s_p - m), 0.0)
            den = jnp.sum(p_c, axis=1, keepdims=True) + jnp.sum(p_p, axis=1, keepdims=True)
            o = (_dot(p_c, vc) + _dot(p_p, vp)) / den
            o_ref[0, 0, pl.ds(r0, ATT_BLK), :] = o
            l_ref[0, 0, pl.ds(r0, ATT_BLK), :] = jnp.broadcast_to(m + jnp.log(den), (ATT_BLK, HD))
            return carry

        lax.fori_loop(0, nb, blk, 0)

    spec = pl.BlockSpec((1, 1, ln, HD), lambda z, h: (z, h, 0, 0))
    return pl.pallas_call(body, grid=(dil, NH), in_specs=[spec] * 3, out_specs=[spec] * 2,
                          out_shape=[_sds(qs.shape)] * 2, name=f"attn_fwd{l}_{bi}",
                          compiler_params=_cp(("parallel", "parallel")))(qs, ks, vs)


def attn_branch_bwd(l, bi, qs, ks, vs, dos, lses, deltas):
    dil, _, ln, _ = qs.shape
    nb = ln // ATT_BLK
    scale = HD ** -0.5

    def body(q_ref, k_ref, v_ref, do_ref, l_ref, dl_ref, dq_ref, dk_ref, dv_ref):
        h = pl.program_id(1)
        dk_ref[...] = jnp.zeros_like(dk_ref)
        dv_ref[...] = jnp.zeros_like(dv_ref)

        def blk(n, carry):
            r0 = pl.multiple_of(n * ATT_BLK, ATT_BLK)
            rp = pl.multiple_of(jnp.maximum(n - 1, 0) * ATT_BLK, ATT_BLK)
            cur, prv = pl.ds(r0, ATT_BLK), pl.ds(rp, ATT_BLK)
            qn, don = q_ref[0, 0, cur, :], do_ref[0, 0, cur, :]
            lse, dlt = l_ref[0, 0, cur, 0:1], dl_ref[0, 0, cur, 0:1]
            kc, vc, kp, vp = k_ref[0, 0, cur, :], v_ref[0, 0, cur, :], k_ref[0, 0, prv, :], v_ref[0, 0, prv, :]
            s_c, s_p, m_c, m_p = _att_scores(qn, kc, kp, h, dil, n > 0)
            p_c = jnp.where(m_c, jnp.exp(s_c - lse), 0.0)
            p_p = jnp.where(m_p, jnp.exp(s_p - lse), 0.0)
            ds_c = p_c * (_dot_nt(don, vc) - dlt)
            ds_p = p_p * (_dot_nt(don, vp) - dlt)
            dq_ref[0, 0, cur, :] = (_dot(ds_c, kc) + _dot(ds_p, kp)) * scale
            dv_ref[0, 0, prv, :] += _dot_tn(p_p, don)
            dk_ref[0, 0, prv, :] += _dot_tn(ds_p, qn) * scale
            dv_ref[0, 0, cur, :] += _dot_tn(p_c, don)
            dk_ref[0, 0, cur, :] += _dot_tn(ds_c, qn) * scale
            return carry

        lax.fori_loop(0, nb, blk, 0)

    spec = pl.BlockSpec((1, 1, ln, HD), lambda z, h: (z, h, 0, 0))
    return pl.pallas_call(body, grid=(dil, NH), in_specs=[spec] * 6, out_specs=[spec] * 3,
                          out_shape=[_sds(qs.shape)] * 3, name=f"attn_bwd{l}_{bi}",
                          compiler_params=_cp(("parallel", "parallel")))(qs, ks, vs, dos, lses, deltas)


def _attn_merge_fn(o1, o2, o3, l1, l2, l3):
    m = jnp.maximum(jnp.maximum(l1, l2), l3)
    w1, w2, w3 = jnp.exp(l1 - m), jnp.exp(l2 - m), jnp.exp(l3 - m)
    den = w1 + w2 + w3
    return (w1 * o1 + w2 * o2 + w3 * o3) / den, m + jnp.log(den)


def attn_merge(l, os_, ls_):
    ins = list(os_) + list(ls_)
    return _map_fwd(f"attn_merge{l}", _attn_merge_fn, (SEQ // RB,), ins, [_rows(DG)] * 6, [_sds((SEQ, DG))] * 2,
                    [_rows(DG)] * 2)


def attn_delta(l, dyb, yb, seg):
    fn = lambda d, y, s: (_dot(d * y, s),)
    return _map_fwd(f"attn_delta{l}", fn, (SEQ // RB,), [dyb, yb, seg], [_rows(DG), _rows(DG), _full((DG, DG))],
                    [_sds((SEQ, DG))], [_rows(DG)])[0]


def _to_sub(t, dil):
    return t.reshape(SEQ // dil, dil, NH, HD).transpose(1, 2, 0, 3)


def _from_sub(t):
    dil, _, ln, _ = t.shape
    return t.transpose(2, 0, 1, 3).reshape(SEQ, DG)


def _ln_fn(x, mix, w, b):
    h = ALPHA * x + mix
    mu = jnp.mean(h, axis=-1, keepdims=True)
    d = h - mu
    var = jnp.mean(d * d, axis=-1, keepdims=True)
    return (d * lax.rsqrt(var + LN_EPS) * w + b,)


def ln_fwd(name, x, mix, w, b):
    specs = [_rows(D_MODEL), _rows(D_MODEL), _full((1, D_MODEL)), _full((1, D_MODEL))]
    return _map_fwd(name, _ln_fn, (SEQ // RB,), [x, mix, w, b], specs, [_sds((SEQ, D_MODEL))], [_rows(D_MODEL)])[0]


def ln_bwd(name, x, mix, w, b, dy):
    specs = [_rows(D_MODEL), _rows(D_MODEL), _full((1, D_MODEL)), _full((1, D_MODEL))]
    return _map_bwd(name, _ln_fn, (SEQ // RB,), [x, mix, w, b], specs, [[dy]], [[_rows(D_MODEL)]], want=[1, 2, 3],
                    acc=(2, 3))


def _relu2_fn(u):
    r = jnp.maximum(u, 0.0)
    return (r * r,)


def relu2_fwd(name, u):
    return _map_fwd(name, _relu2_fn, (SEQ // RB,), [u], [_rows(D_FF)], [_sds((SEQ, D_FF))], [_rows(D_FF)])[0]


def relu2_bwd(name, u, dh):
    fn = lambda uu, g: (g * 2.0 * jnp.maximum(uu, 0.0),)
    return _map_fwd(name, fn, (SEQ // RB,), [u, dh], [_rows(D_FF)] * 2, [_sds((SEQ, D_FF))], [_rows(D_FF)])[0]


def loss_call(y, tgt):
    def fn(yy, tt):
        e = yy - tt
        part = 0.5 * jnp.sum(jnp.sum(e * e, axis=-1, keepdims=True) * (1.0 / D_MODEL), axis=0, keepdims=True)
        return e * (1.0 / D_MODEL), jnp.broadcast_to(part, (8, 128))

    return _map_fwd("loss", fn, (SEQ // RB,), [y, tgt], [_rows(D_MODEL)] * 2,
                    [_sds((SEQ, D_MODEL)), _sds((SEQ // RB * 8, 128))],
                    [_rows(D_MODEL), pl.BlockSpec((8, 128), lambda i: (i, 0))])


def layer_fwd(l, x, vfirst, wts, p):
    sv = {"x": x}
    proj = _mm(f"mm_in{l}", x, wts["w_in"], "nn", 512, 1024, 1024)
    fl = lerp_fwd(l, proj, p["mu"])
    xc = conv_fwd(l, proj, p["conv_w"], p["conv_b"])
    w, k2, v2, c, b, g = rwkv_pre_fwd(l, fl, vfirst, p)
    y_scan, states = rwkv_scan_fwd(l, fl, w, k2, v2, c, b, p)
    ya = rwkv_post_fwd(l, y_scan, fl, k2, v2, g, p)
    q_a, k_a, v_a = proj[:, C_AQ:C_AQ + DG], proj[:, C_AK:C_AK + DG], proj[:, C_AV:C_AV + DG]
    subs, outs, lses = [], [], []
    for bi, (win, dil) in enumerate(DILATED):
        qs, ks, vs = _to_sub(q_a, dil), _to_sub(k_a, dil), _to_sub(v_a, dil)
        o, lse = attn_branch_fwd(l, bi, qs, ks, vs)
        subs.append((qs, ks, vs))
        outs.append(_from_sub(o))
        lses.append(_from_sub(lse))
    yb, lse_all = attn_merge(l, outs, lses)
    yc, ssd_states = ssd_fwd(l, proj, xc, p)
    yd, hg_states = hgrn_fwd(l, proj, p)
    ycat = jnp.concatenate([ya, yb, yc, yd], axis=1)
    mix = _mm(f"mm_out{l}", ycat, wts["w_out"], "nn", 512, 1024, 1024)
    x1 = ln_fwd(f"ln1_fwd{l}", x, mix, p["ln1_w"], p["ln1_b"])
    u = _mm(f"mm_up{l}", x1, wts["w_up_t"], "nt", 512, 1024, 1024)
    hh = relu2_fwd(f"relu2_fwd{l}", u)
    m2 = _mm(f"mm_down{l}", hh, wts["w_down"], "nn", 512, 1024, 1024)
    x2 = ln_fwd(f"ln2_fwd{l}", x1, m2, p["ln2_w"], p["ln2_b"])
    sv.update(proj=proj, fl=fl, xc=xc, w=w, k2=k2, v2=v2, c=c, b=b, g=g, y_scan=y_scan, states=states, subs=subs,
              yb=yb, lse_all=lse_all, ssd_states=ssd_states, hg_states=hg_states, ycat=ycat, mix=mix, x1=x1, u=u, hh=hh,
              m2=m2, vfirst=vfirst)
    return x2, sv


def layer_bwd(l, dx2, dvfirst_next, sv, wts, p):
    gr = {}
    x, x1, proj, fl = sv["x"], sv["x1"], sv["proj"], sv["fl"]
    dres2, gr["ln2_w"], gr["ln2_b"] = ln_bwd(f"ln2_bwd{l}", x1, sv["m2"], p["ln2_w"], p["ln2_b"], dx2)
    dh = _mm(f"mm_down_dx{l}", dres2, wts["w_down"], "nt", 512, 1024, 1024)
    gr["w_down"] = _mm(f"mm_down_dw{l}", sv["hh"], dres2, "tn", 512, 1024, 512)
    du = relu2_bwd(f"relu2_bwd{l}", sv["u"], dh)
    dx1 = _mm(f"mm_up_dx{l}", du, wts["w_up_t"], "nn", 512, 1024, 1024, add=dres2, add_scale=ALPHA)
    gr["w_up_t"] = _mm(f"mm_up_dw{l}", du, x1, "tn", 512, 1024, 512)
    dres1, gr["ln1_w"], gr["ln1_b"] = ln_bwd(f"ln1_bwd{l}", x, sv["mix"], p["ln1_w"], p["ln1_b"], dx1)
    dycat = _mm(f"mm_out_dx{l}", dres1, wts["w_out"], "nt", 512, 1024, 1024)
    gr["w_out"] = _mm(f"mm_out_dw{l}", sv["ycat"], dres1, "tn", 512, 1024, 512)
    dya, dyb, dyc, dyd = (dycat[:, i * DG:(i + 1) * DG] for i in range(4))
    dhg4, gr["lb0"], gr["lb1"], gr["hgrn_norm_w"] = hgrn_bwd(l, proj, sv["hg_states"], dyd, p)
    dz, dxc, ddt, gr["dt_bias"], gr["a_log"], gr["ssd_d"], gr["ssd_norm_w"] = ssd_bwd(l, proj, sv["xc"], sv["ssd_states"], dyc, p)
    dxbc, gr["conv_w"], gr["conv_b"] = conv_bwd(l, proj, p["conv_w"], p["conv_b"], dxc)
    delta = attn_delta(l, dyb, sv["yb"], p["seg64"])
    dqs, dks, dvs = [], [], []
    for bi, (win, dil) in enumerate(DILATED):
        qs, ks, vs = sv["subs"][bi]
        dq, dk, dv = attn_branch_bwd(l, bi, qs, ks, vs, _to_sub(dyb, dil), _to_sub(sv["lse_all"], dil), _to_sub(delta, dil))
        dqs.append(_from_sub(dq))
        dks.append(_from_sub(dk))
        dvs.append(_from_sub(dv))
    dq_a, dk_a, dv_a = _addn(f"attn_dq{l}", *dqs), _addn(f"attn_dk{l}", *dks), _addn(f"attn_dv{l}", *dvs)
    pg = rwkv_post_bwd(l, sv["y_scan"], fl, sv["k2"], sv["v2"], sv["g"], p, dya)
    gr["lnx_w"], gr["lnx_b"], gr["r_k"] = pg["lnx_w"], pg["lnx_b"], pg["r_k"]
    dr, dw, dk, dv, dc, db = rwkv_scan_bwd(l, fl, sv["w"], sv["k2"], sv["v2"], sv["c"], sv["b"], sv["states"], pg["y"], p)
    v2_cts = [dv, pg["v2"]] + ([dvfirst_next] if dvfirst_next is not None else [])
    qg = rwkv_pre_bwd(l, fl, sv["vfirst"], p, [[dw], [dk, pg["k2"]], v2_cts, [dc], [db], [pg["g"]]])
    for nme in ("w0", "w2p", "a0", "a2p", "g2p", "k_k", "k_a", "v0", "v2p"):
        if nme in qg:
            gr[nme] = qg[nme]
    dfr = _addn(f"rwkv_dr{l}", dr, pg["fr"])
    dvres = qg["fvres"] if l > 0 else jnp.zeros((SEQ, 128), F32)
    dfl_out = jnp.concatenate([dfr, qg["fk"], qg["fv"], qg["flora"], dvres], axis=1)
    dfl_in, gr["mu"] = lerp_bwd(l, proj, p["mu"], dfl_out)
    dproj = jnp.concatenate([dfl_in[:, 0:768], dq_a, dk_a, dv_a, dz, dxbc, dhg4, dfl_in[:, 768:896], ddt,
                             dfl_in[:, 896:1024], jnp.zeros((SEQ, 128), F32)], axis=1)
    dx = _mm(f"mm_in_dx{l}", dproj, wts["w_in"], "nt", 512, 1024, 1024, add=dres1, add_scale=ALPHA)
    gr["w_in"] = _mm(f"mm_in_dw{l}", x, dproj, "tn", 512, 1024, 512)
    return dx, (qg["vfirst"] if l > 0 else None), gr


def _w_in_pad(w_in_l, w_vres):
    rows = w_in_l.shape[0]
    z = lambda n: jnp.zeros((rows, n), w_in_l.dtype)
    vres = z(128) if w_vres is None else jnp.concatenate([w_vres, z(96)], axis=1)
    return jnp.concatenate([w_in_l[:, 0:768], w_in_l[:, 896:1664], w_in_l[:, 1664:1920], w_in_l[:, 1920:2688],
                            w_in_l[:, 2692:3716], w_in_l[:, 768:896], w_in_l[:, 2688:2692], z(124), vres, z(128)], axis=1)


def _w_in_unpad(g):
    g_in = jnp.concatenate([g[:, 0:768], g[:, C_LORA:C_LORA + 128], g[:, 768:1536], g[:, C_Z:C_Z + 256],
                            g[:, C_XBC:C_XBC + 768], g[:, C_DT:C_DT + 4], g[:, C_HQ:C_HQ + 1024]], axis=1)
    return g_in, g[:, C_VRES:C_VRES + 32]


def _consts():
    i16 = jnp.arange(HGRN_CHUNK)
    pair = jnp.arange(HGRN_CHUNK * HGRN_CHUNK)
    i128 = jnp.arange(128)
    seg64 = _seg_ones(DG, HD)
    tri128 = (i128[:, None] >= i128[None, :]).astype(F32)
    return dict(
        seg64=seg64, seg64x3_bf16=jnp.concatenate([seg64, seg64, seg64], axis=0).astype(BF16),
        dmask=(jnp.arange(HD)[:, None] == (jnp.arange(DG)[None, :] % HD)).astype(F32),
        tri16=(i16[:, None] >= i16[None, :]).astype(F32),
        causal16=jnp.broadcast_to(((pair // HGRN_CHUNK) >= (pair % HGRN_CHUNK)).astype(F32)[:, None], (256, DG)),
        ones16=jnp.ones((HGRN_CHUNK, DG), F32),
        e128=((i128[:, None] == (jnp.arange(DG)[None, :] // HD)) & (i128[:, None] < NH)).astype(F32),
        tri128=tri128, tri128t=tri128.T, seg128=_seg_ones(DG, 128), ones128=jnp.ones((128, 128), F32))


def _pad_lanes(v, n):
    return jnp.concatenate([v, jnp.zeros((n - v.shape[0],), v.dtype)])[None, :]


def _layer_params(l, raw, consts):
    p = dict(consts)
    row = lambda name: raw[name][l][None, :]
    z = lambda r: jnp.zeros((r, DG), F32)
    mu_vres = raw["mu_vres"][l - 1] if l > 0 else jnp.zeros((32,), F32)
    p["mu"] = jnp.concatenate([raw["mu_shift"][l], mu_vres, jnp.zeros((96,), F32)])[None, :]
    p["conv_w"], p["conv_b"] = raw["ssd_conv_w"][l], row("ssd_conv_b")
    p["w0"], p["a0"], p["k_k"], p["k_a"] = row("rwkv_w0"), row("rwkv_a0"), row("rwkv_k_k"), row("rwkv_k_a")
    p["lnx_w"], p["lnx_b"] = row("rwkv_lnx_w"), row("rwkv_lnx_b")
    p["r_k"] = raw["rwkv_r_k"][l].reshape(1, DG)
    p["w2p"] = jnp.concatenate([raw["rwkv_w2"][l], z(96)], axis=0)
    p["a2p"] = jnp.concatenate([z(32), raw["rwkv_a2"][l], z(64)], axis=0)
    p["g2p"] = jnp.concatenate([z(64), raw["rwkv_g2"][l]], axis=0)
    if l > 0:
        p["v0"] = raw["rwkv_v0"][l - 1][None, :]
        p["v2p"] = jnp.concatenate([raw["rwkv_v2"][l - 1], z(96)], axis=0)
    p["lb0"], p["lb1"] = raw["lower_bounds"][0:1], raw["lower_bounds"][1:2]
    p["hgrn_norm_w"], p["ssd_norm_w"] = row("hgrn_norm_w"), row("ssd_norm_w")
    p["dt_bias"], p["a_log"], p["ssd_d"] = (_pad_lanes(raw[n][l], 128) for n in ("ssd_dt_bias", "ssd_A_log", "ssd_D"))
    for n in ("ln1_w", "ln1_b", "ln2_w", "ln2_b"):
        p[n] = row(n)
    return p


def _natural_grads(g0, g1):
    gs = (g0, g1)
    st = lambda key, f=lambda a: a[0]: jnp.stack([f(g[key]) for g in gs])
    out = {}
    out["lower_bounds"] = jnp.concatenate([g0["lb0"] + g1["lb0"], g0["lb1"] + g1["lb1"]], axis=0)
    out["mu_shift"] = st("mu", lambda a: a[0, :896])
    out["mu_vres"] = g1["mu"][:, 896:928]
    out["rwkv_w0"], out["rwkv_a0"], out["rwkv_k_k"], out["rwkv_k_a"] = st("w0"), st("a0"), st("k_k"), st("k_a")
    out["rwkv_w2"] = st("w2p", lambda a: a[0:32])
    out["rwkv_a2"] = st("a2p", lambda a: a[32:64])
    out["rwkv_g2"] = st("g2p", lambda a: a[64:128])
    out["rwkv_r_k"] = st("r_k", lambda a: a.reshape(NH, HD))
    out["rwkv_lnx_w"], out["rwkv_lnx_b"] = st("lnx_w"), st("lnx_b")
    out["rwkv_v0"] = g1["v0"]
    out["rwkv_v2"] = g1["v2p"][None, 0:32]
    out["ssd_conv_w"] = st("conv_w", lambda a: a)
    out["ssd_conv_b"] = st("conv_b")
    out["ssd_dt_bias"], out["ssd_A_log"], out["ssd_D"] = (st(k, lambda a: a[0, :NH]) for k in ("dt_bias", "a_log", "ssd_d"))
    out["ssd_norm_w"], out["hgrn_norm_w"] = st("ssd_norm_w"), st("hgrn_norm_w")
    for n in ("ln1_w", "ln1_b", "ln2_w", "ln2_b"):
        out[n] = st(n)
    return out


MESH_T = pl.DeviceIdType.MESH
ANY = pl.BlockSpec(memory_space=pl.ANY)


def _dev_index(px, py, pc):
    return 4 * px + 2 * py + pc


def all_gather(arrs):
    n = len(arrs)

    def body(*refs):
        ins, outs = refs[:n], refs[n:2 * n]
        send_sems, recv_sems, local_sems = refs[2 * n:]
        x, y, c = lax.axis_index("x"), lax.axis_index("y"), lax.axis_index("c")
        me, sibling = (x, y, c), (x, y, 1 - c)
        chips = [(1 - x, y), (x, 1 - y), (1 - x, 1 - y)]

        def copy(a, k, block, to, src=None):
            slot = outs[a].at[_dev_index(*block)]
            return pltpu.make_async_remote_copy(src_ref=slot if src is None else src, dst_ref=slot,
                                                send_sem=send_sems.at[a, k], recv_sem=recv_sems.at[a, k],
                                                device_id=to, device_id_type=MESH_T)

        mine = [pltpu.make_async_copy(ins[a], outs[a].at[_dev_index(*me)], local_sems.at[a]) for a in range(n)]
        for cp in mine:
            cp.start()
        first = []
        for a in range(n):
            first.append(copy(a, 0, me, sibling, src=ins[a]))
            first += [copy(a, 1 + j, me, (*chip, c), src=ins[a]) for j, chip in enumerate(chips)]
        for cp in first:
            cp.start()
        passed = []
        for j, chip in enumerate(chips):
            for a in range(n):
                copy(a, 1 + j, (*chip, c), me).wait_recv()
                fwd = copy(a, 4 + j, (*chip, c), sibling)
                fwd.start()
                passed.append(fwd)
        for a in range(n):
            copy(a, 0, sibling, me).wait_recv()
            for j, chip in enumerate(chips):
                copy(a, 4 + j, (*chip, 1 - c), me).wait_recv()
        for cp in first + passed:
            cp.wait_send()
        for cp in mine:
            cp.wait()

    return pl.pallas_call(
        body, in_specs=[ANY] * n, out_specs=[ANY] * n,
        out_shape=[_sds((N_DEV,) + a.shape, a.dtype) for a in arrs],
        scratch_shapes=[pltpu.SemaphoreType.DMA((n, 7)), pltpu.SemaphoreType.DMA((n, 7)), pltpu.SemaphoreType.DMA((n,))],
        name="all_gather")(*arrs)


def _chips(x, y):
    return [(x, y), (1 - x, y), (x, 1 - y), (1 - x, 1 - y)]


def exchange_siblings(arrs):
    n = len(arrs)

    def body(*refs):
        ins, sib = refs[:n], refs[n:2 * n]
        send_sems, recv_sems = refs[2 * n:]
        x, y, c = lax.axis_index("x"), lax.axis_index("y"), lax.axis_index("c")
        sibling = (x, y, 1 - c)
        sends = []
        for a in range(n):
            for k, (cx, cy) in enumerate(_chips(x, y)):
                sd = pltpu.make_async_remote_copy(src_ref=ins[a].at[_dev_index(cx, cy, 1 - c)], dst_ref=sib[a].at[k],
                                                  send_sem=send_sems.at[a, k], recv_sem=recv_sems.at[a, k],
                                                  device_id=sibling, device_id_type=MESH_T)
                sd.start()
                sends.append(sd)
        for sd in sends:
            sd.wait_recv()
        for sd in sends:
            sd.wait_send()

    sem = pltpu.SemaphoreType.DMA((n, 4))
    return pl.pallas_call(body, in_specs=[ANY] * n, out_specs=[ANY] * n,
                          out_shape=[_sds((4,) + a.shape[1:], a.dtype) for a in arrs],
                          scratch_shapes=[sem, sem], name="exchange_siblings")(*arrs)


def reduce_pair(name, send, slots, sib, wire_dtype):
    _, r, c = send.shape
    rb = min(r, 262144 // c)

    def body(slots_ref, m0, m1, m2, m3, s_ref, own_ref, part_ref):
        own_ref[...] = m0[...] + s_ref[0]
        for k, m_ref in enumerate((m1, m2, m3)):
            part_ref[k] = (m_ref[...] + s_ref[k + 1]).astype(wire_dtype)

    mine = [pl.BlockSpec((None, rb, c), lambda i, s, k=k: (s[k], i, 0)) for k in range(4)]
    grid_spec = pltpu.PrefetchScalarGridSpec(
        num_scalar_prefetch=1, grid=(r // rb,),
        in_specs=mine + [pl.BlockSpec((4, rb, c), lambda i, s: (0, i, 0))],
        out_specs=[pl.BlockSpec((rb, c), lambda i, s: (i, 0)), pl.BlockSpec((3, rb, c), lambda i, s: (0, i, 0))])
    return pl.pallas_call(body, grid_spec=grid_spec, out_shape=[_sds((r, c)), _sds((3, r, c), wire_dtype)], name=name,
                          compiler_params=_cp(("parallel",)))(slots, send, send, send, send, sib)


def exchange_chips(parts, rep):
    n = len(parts)

    def body(*refs):
        ins, rep_ref = refs[:n], refs[n]
        recv, rep_all = refs[n + 1:2 * n + 1], refs[2 * n + 1]
        send_sems, recv_sems, rsend_sems, rrecv_sems, local_sem = refs[2 * n + 2:]
        x, y, c = lax.axis_index("x"), lax.axis_index("y"), lax.axis_index("c")
        me = _dev_index(x, y, c)
        mine = pltpu.make_async_copy(rep_ref, rep_all.at[me], local_sem)
        mine.start()
        cps = []
        for a in range(n):
            for k, (cx, cy) in enumerate(_chips(x, y)[1:]):
                cp = pltpu.make_async_remote_copy(src_ref=ins[a].at[k], dst_ref=recv[a].at[k],
                                                  send_sem=send_sems.at[a, k], recv_sem=recv_sems.at[a, k],
                                                  device_id=(cx, cy, c), device_id_type=MESH_T)
                cp.start()
                cps.append(cp)
        rels = [(rx, ry, rc) for rx in (0, 1) for ry in (0, 1) for rc in (0, 1)][1:]
        peers = [(jnp.where(rx, 1 - x, x), jnp.where(ry, 1 - y, y), jnp.where(rc, 1 - c, c)) for rx, ry, rc in rels]
        rcps = []
        for k, peer in enumerate(peers):
            cp = pltpu.make_async_remote_copy(src_ref=rep_ref, dst_ref=rep_all.at[me], send_sem=rsend_sems.at[k],
                                              recv_sem=rrecv_sems.at[k], device_id=peer, device_id_type=MESH_T)
            cp.start()
            rcps.append(cp)
        for k, peer in enumerate(peers):
            pltpu.make_async_remote_copy(src_ref=rep_ref, dst_ref=rep_all.at[_dev_index(*peer)], send_sem=rsend_sems.at[k],
                                         recv_sem=rrecv_sems.at[k], device_id=peer, device_id_type=MESH_T).wait_recv()
        for cp in cps:
            cp.wait_recv()
        for cp in cps + rcps:
            cp.wait_send()
        mine.wait()

    outs = pl.pallas_call(
        body, in_specs=[ANY] * (n + 1), out_specs=[ANY] * (n + 1),
        out_shape=[_sds(a.shape, a.dtype) for a in parts] + [_sds((N_DEV,) + rep.shape, rep.dtype)],
        scratch_shapes=[pltpu.SemaphoreType.DMA((n, 3)), pltpu.SemaphoreType.DMA((n, 3)), pltpu.SemaphoreType.DMA((7,)),
                        pltpu.SemaphoreType.DMA((7,)), pltpu.SemaphoreType.DMA],
        name="exchange_chips")(*parts, rep)
    return outs[:n], outs[n]


def adamw(name, terms, w, m, v):
    r, c = w.shape
    rb = min(r, 262144 // c)
    c1 = 1.0 - ADAM_B1 ** ADAM_STEP
    c2 = 1.0 - ADAM_B2 ** ADAM_STEP
    nt = len(terms)

    def body(*refs):
        w_ref, m_ref, v_ref = refs[nt:nt + 3]
        g_ref, d_ref, nm_ref, nv_ref = refs[nt + 3:]
        g = refs[0][...].astype(F32)
        for t_ref in refs[1:nt]:
            g = g + t_ref[...].astype(F32)
        nm = ADAM_B1 * m_ref[...] + (1.0 - ADAM_B1) * g
        nv = ADAM_B2 * v_ref[...] + (1.0 - ADAM_B2) * (g * g)
        g_ref[...] = g
        nm_ref[...] = nm
        nv_ref[...] = nv
        d_ref[...] = -ADAM_LR * ((nm / c1) / (jnp.sqrt(nv / c2) + ADAM_EPS) + ADAM_WD * w_ref[...])

    blk = pl.BlockSpec((rb, c), lambda i: (i, 0))
    tspecs = [blk if k is None else pl.BlockSpec((None, rb, c), lambda i, k=k: (k, i, 0)) for _, k in terms]
    return pl.pallas_call(body, grid=(r // rb,), in_specs=tspecs + [blk] * 3, out_specs=[blk] * 4,
                          out_shape=[_sds((r, c))] * 4, name=name,
                          compiler_params=_cp(("parallel",)))(*[t for t, _ in terms], w, m, v)


SMS_ROWS = 16
REP_ROWS = 24
N_BIG = 8
SMALL_SHARDED = (("rwkv_w2", (2, 32, 32)), ("rwkv_a2", (2, 32, 32)), ("rwkv_g2", (2, 64, 32)), ("rwkv_v2", (1, 32, 32)),
                 ("ssd_conv_w", (2, 4, 96)))
REPLICATED = (("lower_bounds", (2, 256)), ("mu_shift", (2, 896)), ("mu_vres", (1, 32)), ("rwkv_w0", (2, 256)),
              ("rwkv_a0", (2, 256)), ("rwkv_k_k", (2, 256)), ("rwkv_k_a", (2, 256)), ("rwkv_r_k", (2, 4, 64)),
              ("rwkv_lnx_w", (2, 256)), ("rwkv_lnx_b", (2, 256)), ("rwkv_v0", (1, 256)), ("ssd_conv_b", (2, 768)),
              ("ssd_dt_bias", (2, 4)), ("ssd_A_log", (2, 4)), ("ssd_D", (2, 4)), ("ssd_norm_w", (2, 256)),
              ("hgrn_norm_w", (2, 256)), ("ln1_w", (2, 1024)), ("ln1_b", (2, 1024)), ("ln2_w", (2, 1024)),
              ("ln2_b", (2, 1024)))


def _flat_rows(parts, rows):
    flat = jnp.concatenate([a.reshape(-1) for a in parts])
    return jnp.concatenate([flat, jnp.zeros((rows * PACK_W - flat.shape[0],), flat.dtype)]).reshape(rows, PACK_W)


def _local_arrays(d):
    arrs = [_w_in_pad(d["w_in"][0], None), _w_in_pad(d["w_in"][1], d["w_in_vres"][0]), d["w_out"][0], d["w_out"][1],
            d["w_up"][0].T, d["w_up"][1].T, d["w_down"][0], d["w_down"][1],
            _flat_rows([d[n] for n, _ in SMALL_SHARDED], SMS_ROWS)]
    return arrs, _flat_rows([d[n] for n, _ in REPLICATED], REP_ROWS)


def _unflat(rows2d, table):
    flat, out, o = rows2d.reshape(-1), {}, 0
    for name, shape in table:
        n = 1
        for s in shape:
            n *= s
        out[name] = flat[o:o + n].reshape(shape)
        o += n
    return out


def _from_local_arrays(arrs, rep):
    d = {}
    g0, _ = _w_in_unpad(arrs[0])
    g1, gv = _w_in_unpad(arrs[1])
    d["w_in"], d["w_in_vres"] = jnp.stack([g0, g1]), gv[None]
    d["w_out"] = jnp.stack([arrs[2], arrs[3]])
    d["w_up"] = jnp.stack([arrs[4].T, arrs[5].T])
    d["w_down"] = jnp.stack([arrs[6], arrs[7]])
    d.update(_unflat(arrs[8], SMALL_SHARDED))
    d.update(_unflat(rep, REPLICATED))
    return d


def _gathered_weights(gathered):
    full = [g.reshape(N_DEV * g.shape[1], g.shape[2]) for g in gathered[:N_BIG]]
    wts = [dict(w_in=full[l], w_out=full[2 + l], w_up_t=full[4 + l], w_down=full[6 + l]) for l in range(DEPTH)]
    small, flat, o = {}, gathered[N_BIG].reshape(N_DEV, -1), 0
    for name, shape in SMALL_SHARDED:
        n = shape[0] * shape[1] * shape[2]
        blk = flat[:, o:o + n].reshape((N_DEV,) + shape)
        small[name] = blk.transpose(1, 2, 0, 3).reshape(shape[0], shape[1], N_DEV * shape[2])
        o += n
    return wts, small


def _send_arrays(big, small_grads):
    blocks = lambda g: g.reshape(N_DEV, g.shape[0] // N_DEV, g.shape[1])
    arrs = [blocks(big[l][k]) for k in ("w_in", "w_out", "w_up_t", "w_down") for l in range(DEPTH)]
    sms = []
    for name, shape in SMALL_SHARDED:
        g = small_grads[name].reshape(shape[0], shape[1], N_DEV, shape[2]).transpose(2, 0, 1, 3)
        sms.append(g.reshape(N_DEV, -1))
    sms = jnp.concatenate(sms, axis=1)
    sms = jnp.concatenate([sms, jnp.zeros((N_DEV, SMS_ROWS * PACK_W - sms.shape[1]), F32)], axis=1)
    arrs.append(sms.reshape(N_DEV, SMS_ROWS, PACK_W))
    return arrs, _flat_rows([small_grads[n] for n, _ in REPLICATED], REP_ROWS)


def _local_step(x, tgt, wts, raw):
    consts = _consts()
    ps = [_layer_params(l, raw, consts) for l in range(DEPTH)]
    x1, sv0 = layer_fwd(0, x, None, wts[0], ps[0])
    x2, sv1 = layer_fwd(1, x1, sv0["fl"], wts[1], ps[1])
    dy, lparts = loss_call(x2, tgt)
    loss = jnp.sum(lparts[::8, 0])
    dx1, dvfirst, g1 = layer_bwd(1, dy, None, sv1, wts[1], ps[1])
    dx0, _, g0 = layer_bwd(0, dx1, dvfirst, sv0, wts[0], ps[0])
    big = [{k: g[k] for k in ("w_in", "w_out", "w_up_t", "w_down")} for g in (g0, g1)]
    return loss, dx0, big, _natural_grads(g0, g1)


WEIGHT_NAMES = ("lower_bounds", "w_in", "w_in_vres", "mu_shift", "mu_vres", "rwkv_w0", "rwkv_w2", "rwkv_a0", "rwkv_a2",
                "rwkv_g2", "rwkv_k_k", "rwkv_k_a", "rwkv_r_k", "rwkv_lnx_w", "rwkv_lnx_b", "rwkv_v0", "rwkv_v2",
                "ssd_conv_w", "ssd_conv_b", "ssd_dt_bias", "ssd_A_log", "ssd_D", "ssd_norm_w", "hgrn_norm_w", "w_out",
                "ln1_w", "ln1_b", "w_up", "w_down", "ln2_w", "ln2_b")


def kernel(x, lower_bounds, w_in, w_in_vres, mu_shift, mu_vres, rwkv_w0, rwkv_w2, rwkv_a0, rwkv_a2, rwkv_g2, rwkv_k_k, rwkv_k_a, rwkv_r_k, rwkv_lnx_w, rwkv_lnx_b, rwkv_v0, rwkv_v2, ssd_conv_w, ssd_conv_b, ssd_dt_bias, ssd_A_log, ssd_D, ssd_norm_w, hgrn_norm_w, w_out, ln1_w, ln1_b, w_up, w_down, ln2_w, ln2_b, loss_target, m_lower_bounds, m_w_in, m_w_in_vres, m_mu_shift, m_mu_vres, m_rwkv_w0, m_rwkv_w2, m_rwkv_a0, m_rwkv_a2, m_rwkv_g2, m_rwkv_k_k, m_rwkv_k_a, m_rwkv_r_k, m_rwkv_lnx_w, m_rwkv_lnx_b, m_rwkv_v0, m_rwkv_v2, m_ssd_conv_w, m_ssd_conv_b, m_ssd_dt_bias, m_ssd_A_log, m_ssd_D, m_ssd_norm_w, m_hgrn_norm_w, m_w_out, m_ln1_w, m_ln1_b, m_w_up, m_w_down, m_ln2_w, m_ln2_b, v_lower_bounds, v_w_in, v_w_in_vres, v_mu_shift, v_mu_vres, v_rwkv_w0, v_rwkv_w2, v_rwkv_a0, v_rwkv_a2, v_rwkv_g2, v_rwkv_k_k, v_rwkv_k_a, v_rwkv_r_k, v_rwkv_lnx_w, v_rwkv_lnx_b, v_rwkv_v0, v_rwkv_v2, v_ssd_conv_w, v_ssd_conv_b, v_ssd_dt_bias, v_ssd_A_log, v_ssd_D, v_ssd_norm_w, v_hgrn_norm_w, v_w_out, v_ln1_w, v_ln1_b, v_w_up, v_w_down, v_ln2_w, v_ln2_b):
    given = dict(locals())
    w = {n: given[n] for n in WEIGHT_NAMES}
    w_arrs, w_rep = _local_arrays(w)
    m_arrs, m_rep = _local_arrays({n: given["m_" + n] for n in WEIGHT_NAMES})
    v_arrs, v_rep = _local_arrays({n: given["v_" + n] for n in WEIGHT_NAMES})
    gathered = all_gather([a.astype(BF16) for a in w_arrs[:N_BIG]] + [w_arrs[N_BIG]])
    wts, small_full = _gathered_weights(gathered)
    raw = {n: w[n] for n, _ in REPLICATED}
    raw.update(small_full)
    loss, dx, big, small_grads = _local_step(x[0], loss_target[0], wts, raw)
    send, rep = _send_arrays(big, small_grads)
    sib = exchange_siblings(send)
    mx, my, mc = lax.axis_index("x"), lax.axis_index("y"), lax.axis_index("c")
    slots = jnp.stack([_dev_index(cx, cy, mc) for cx, cy in _chips(mx, my)]).astype(jnp.int32)
    own, parts = [], []
    for a in range(N_BIG + 1):
        o, pt = reduce_pair(f"reduce_pair{a}", send[a], slots, sib[a], BF16 if a < N_BIG else F32)
        own.append(o)
        parts.append(pt)
    recv, rep_all = exchange_chips(parts, rep)
    results = [adamw(f"adamw{a}", [(own[a], None), (recv[a], 0), (recv[a], 1), (recv[a], 2)], w_arrs[a], m_arrs[a], v_arrs[a])
               for a in range(N_BIG + 1)]
    rep_res = adamw("adamw_rep", [(rep_all, q) for q in range(N_DEV)], w_rep, m_rep, v_rep)
    loss = lax.psum(loss, ("x", "y", "c"))
    outs = [loss, dx[None]]
    for q in range(4):
        d = _from_local_arrays([res[q] for res in results], rep_res[q])
        outs += [d[n] for n in WEIGHT_NAMES]
    return tuple(outs)
```

```python
import functools

import jax
import jax.numpy as jnp
from jax import lax
from jax.experimental import pallas as pl
from jax.experimental.pallas import tpu as pltpu

F32 = jnp.float32
BF16 = jnp.bfloat16
HI = lax.Precision.HIGHEST

N_DEV = 8
SEQ = 2048
D_MODEL = 1024
D_FF = 4096
DG = 256
NH = 4
HD = 64
DEPTH = 2
ALPHA = (2.0 * DEPTH) ** 0.25
LN_EPS = 1e-5
RMS_EPS = 1e-5
GN_EPS = HD * 1e-5
IN_COLS = 3716
SSD_N = 128
SSD_CHUNK = 128
HGRN_CHUNK = 16
DILATED = ((128, 1), (512, 4), (2048, 16))

ADAM_LR, ADAM_B1, ADAM_B2, ADAM_EPS, ADAM_WD, ADAM_STEP = 0.001, 0.9, 0.999, 1e-08, 0.01, 10

PW = 4096
C_R, C_K, C_V = 0, 256, 512
C_AQ, C_AK, C_AV = 768, 1024, 1280
C_Z, C_XBC = 1536, 1792
C_HQ, C_HF, C_HI, C_HG = 2560, 2816, 3072, 3328
C_LORA, C_DT, C_VRES = 3584, 3712, 3840

RB = 256
VMEM_LIMIT = 56 * 1024 * 1024
PACK_W = 1024


def _cp(sem=None):
    return pltpu.CompilerParams(dimension_semantics=sem, vmem_limit_bytes=VMEM_LIMIT)


def _sds(shape, dt=F32):
    return jax.ShapeDtypeStruct(tuple(shape), dt)


def _rows(w, cb=0, rb=RB):
    return pl.BlockSpec((rb, w), lambda i: (i, cb))


def _full(shape):
    n = len(shape)
    return pl.BlockSpec(tuple(shape), lambda *_: (0,) * n)


def _sigmoid(x):
    return 1.0 / (1.0 + jnp.exp(-x))


def _silu(x):
    return x * _sigmoid(x)


def _softplus(x):
    return jnp.maximum(x, 0.0) + jnp.log(1.0 + jnp.exp(jnp.where(x > 0, -x, x)))


MID = lax.Precision.HIGH
NN, TN, NT = (((1,), (0,)), ((), ())), (((0,), (0,)), ((), ())), (((1,), (1,)), ((), ()))


def _dot(a, b):
    return lax.dot_general(a, b, NN, precision=MID, preferred_element_type=F32)


def _dot_tn(a, b):
    return lax.dot_general(a, b, TN, precision=MID, preferred_element_type=F32)


def _dot_nt(a, b):
    return lax.dot_general(a, b, NT, precision=MID, preferred_element_type=F32)


def _dotx(a, b):
    return lax.dot_general(a, b, NN, precision=HI, preferred_element_type=F32)


def _dotx_tn(a, b):
    return lax.dot_general(a, b, TN, precision=HI, preferred_element_type=F32)


def _seg_ones(n, seg):
    i = jnp.arange(n)
    return (i[:, None] // seg == i[None, :] // seg).astype(F32)


def _shift_down(x, s):
    row = lax.broadcasted_iota(jnp.int32, x.shape, 0)
    return jnp.where(row < s, 0.0, pltpu.roll(x, s, 0))


def _shift_up(x, s):
    n = x.shape[0]
    row = lax.broadcasted_iota(jnp.int32, x.shape, 0)
    return jnp.where(row >= n - s, 0.0, pltpu.roll(x, n - s, 0))


@functools.partial(jax.custom_vjp, nondiff_argnums=(1,))
def _tshift(x, s):
    return _shift_down(x, s)


def _tshift_fwd(x, s):
    return _shift_down(x, s), None


def _tshift_bwd(s, _, g):
    return (_shift_up(g, s),)


_tshift.defvjp(_tshift_fwd, _tshift_bwd)


def _map_fwd(name, fn, grid, ins, in_specs, out_shapes, out_specs):
    n_in = len(ins)

    def body(*refs):
        ys = fn(*[r[...] for r in refs[:n_in]])
        for r, y in zip(refs[n_in:], ys):
            r[...] = y

    return pl.pallas_call(body, grid=grid, in_specs=in_specs, out_specs=out_specs, out_shape=out_shapes,
                          name=name, compiler_params=_cp(("parallel",)))(*ins)


def _map_bwd(name, fn, grid, ins, in_specs, cts, ct_specs, want, acc=(), gout=None):
    n_in = len(ins)
    flat_cts = [c for group in cts for c in group]
    flat_specs = [s for group in ct_specs for s in group]
    n_ct = len(flat_cts)
    gout = gout or {}
    out_shapes = [gout[i][0] if i in gout else _sds(ins[i].shape) for i in want]
    out_specs = [gout[i][1] if i in gout else in_specs[i] for i in want]

    def body(*refs):
        xs = [r[...] for r in refs[:n_in]]
        cvals = [r[...] for r in refs[n_in:n_in + n_ct]]
        gouts = refs[n_in + n_ct:]
        cs, p = [], 0
        for group in cts:
            v = cvals[p]
            for q in range(1, len(group)):
                v = v + cvals[p + q]
            cs.append(v)
            p += len(group)

        def f(*wanted):
            full = list(xs)
            for i, w in zip(want, wanted):
                full[i] = w
            return tuple(fn(*full))

        _, vjp = jax.vjp(f, *[xs[i] for i in want])
        gs = vjp(tuple(cs))
        for o, i, g in zip(gouts, want, gs):
            if i in acc:
                @pl.when(pl.program_id(0) == 0)
                def _():
                    o[...] = jnp.zeros_like(o)

                o[...] += g
            else:
                o[...] = g

    sem = ("arbitrary",) if acc else ("parallel",)
    return pl.pallas_call(body, grid=grid, in_specs=list(in_specs) + flat_specs, out_specs=out_specs,
                          out_shape=out_shapes, name=name, compiler_params=_cp(sem))(*ins, *flat_cts)


def _addn(name, *arrs):
    n, c = arrs[0].shape

    def fn(*xs):
        r = xs[0]
        for x in xs[1:]:
            r = r + x
        return (r,)

    return _map_fwd(name, fn, (n // RB,), list(arrs), [_rows(c)] * len(arrs), [_sds((n, c))], [_rows(c)])[0]


def _mm(name, a, b, mode, tm, tn, tk, add=None, add_scale=1.0):
    if mode == "nn":
        (m, k), n = a.shape, b.shape[1]
    elif mode == "nt":
        (m, k), n = a.shape, b.shape[0]
    else:
        (k, m), n = a.shape, b.shape[1]
    nk = k // tk
    dn = {"nn": (((1,), (0,)), ((), ())), "nt": (((1,), (1,)), ((), ())), "tn": (((0,), (0,)), ((), ()))}[mode]

    def body(*refs):
        if add is None:
            a_ref, b_ref, o_ref, acc = refs
        else:
            a_ref, b_ref, add_ref, o_ref, acc = refs
        kk = pl.program_id(2)

        @pl.when(kk == 0)
        def _():
            acc[...] = jnp.zeros_like(acc)

        acc[...] += lax.dot_general(a_ref[...].astype(BF16), b_ref[...].astype(BF16), dn, preferred_element_type=F32)

        @pl.when(kk == nk - 1)
        def _():
            r = acc[...]
            if add is not None:
                r = r + add_scale * add_ref[...]
            o_ref[...] = r

    a_spec = pl.BlockSpec((tk, tm), lambda i, j, q: (q, i)) if mode == "tn" else pl.BlockSpec((tm, tk), lambda i, j, q: (i, q))
    b_spec = pl.BlockSpec((tn, tk), lambda i, j, q: (j, q)) if mode == "nt" else pl.BlockSpec((tk, tn), lambda i, j, q: (q, j))
    o_spec = pl.BlockSpec((tm, tn), lambda i, j, q: (i, j))
    ins, specs = [a, b], [a_spec, b_spec]
    if add is not None:
        ins.append(add)
        specs.append(o_spec)
    return pl.pallas_call(body, grid=(m // tm, n // tn, nk), in_specs=specs, out_specs=o_spec, out_shape=_sds((m, n)),
                          scratch_shapes=[pltpu.VMEM((tm, tn), F32)], name=name,
                          compiler_params=_cp(("parallel", "parallel", "arbitrary")))(*ins)


LERP_BLOCKS = (0, 1, 2, 3, 4, 5, C_LORA // 128, C_VRES // 128)


def _lerp_colmap(j):
    r = jnp.where(j < 6, j, jnp.where(j == 6, C_LORA // 128, C_VRES // 128))
    return (0, r)


def _lerp_fn(f, mu):
    return (f + (_tshift(f, 1) - f) * mu,)


def _lerp_specs():
    return [pl.BlockSpec((SEQ, 128), _lerp_colmap), pl.BlockSpec((1, 128), lambda j: (0, j))]


def lerp_fwd(l, proj, mu):
    return _map_fwd(f"lerp_fwd{l}", _lerp_fn, (8,), [proj, mu], _lerp_specs(), [_sds((SEQ, 1024))],
                    [pl.BlockSpec((SEQ, 128), lambda j: (0, j))])[0]


def lerp_bwd(l, proj, mu, dfl):
    n_in = 2

    def body(f_ref, mu_ref, g_ref, df_ref, dmu_ref):
        _, vjp = jax.vjp(_lerp_fn, f_ref[...], mu_ref[...])
        df, dmu = vjp((g_ref[...],))
        df_ref[...] = df
        dmu_ref[...] = dmu

    cspec = pl.BlockSpec((SEQ, 128), lambda j: (0, j))
    return pl.pallas_call(body, grid=(8,), in_specs=_lerp_specs() + [cspec],
                          out_specs=[cspec, pl.BlockSpec((1, 128), lambda j: (0, j))],
                          out_shape=[_sds((SEQ, 1024)), _sds((1, 1024))], name=f"lerp_bwd{l}",
                          compiler_params=_cp(("parallel",)))(proj, mu, dfl)


def _conv_fn(x, w, b):
    y = x * w[3:4, :] + _tshift(x, 1) * w[2:3, :] + _tshift(x, 2) * w[1:2, :] + _tshift(x, 3) * w[0:1, :] + b
    return (_silu(y),)


def _conv_specs():
    return [pl.BlockSpec((SEQ, 128), lambda j: (0, C_XBC // 128 + j)), pl.BlockSpec((4, 128), lambda j: (0, j)),
            pl.BlockSpec((1, 128), lambda j: (0, j))]


def conv_fwd(l, proj, w, b):
    return _map_fwd(f"conv_fwd{l}", _conv_fn, (6,), [proj, w, b], _conv_specs(), [_sds((SEQ, 768))],
                    [pl.BlockSpec((SEQ, 128), lambda j: (0, j))])[0]


def conv_bwd(l, proj, w, b, dxc):
    def body(x_ref, w_ref, b_ref, g_ref, dx_ref, dw_ref, db_ref):
        _, vjp = jax.vjp(_conv_fn, x_ref[...], w_ref[...], b_ref[...])
        dx, dw, db = vjp((g_ref[...],))
        dx_ref[...] = dx
        dw_ref[...] = dw
        db_ref[...] = db

    cspec = pl.BlockSpec((SEQ, 128), lambda j: (0, j))
    return pl.pallas_call(body, grid=(6,), in_specs=_conv_specs() + [cspec],
                          out_specs=[cspec, pl.BlockSpec((4, 128), lambda j: (0, j)), pl.BlockSpec((1, 128), lambda j: (0, j))],
                          out_shape=[_sds((SEQ, 768)), _sds((4, 768)), _sds((1, 768))], name=f"conv_bwd{l}",
                          compiler_params=_cp(("parallel",)))(proj, w, b, dxc)


def _rwkv_pre_fn(has_vres):
    def fn(fk, fv, flora, *rest):
        if has_vres:
            fvres, vfirst, w0, w2p, a0, a2p, g2p, k_k, k_a, v0, v2p, seg = rest
        else:
            w0, w2p, a0, a2p, g2p, k_k, k_a, seg = rest
        w_log = -_softplus(-(w0 + _dot(jnp.tanh(flora), w2p))) - 0.5
        w = jnp.exp(-jnp.exp(w_log))
        a = _sigmoid(a0 + _dot(flora, a2p))
        g = _dot(_sigmoid(flora), g2p)
        if has_vres:
            v2 = fv + (vfirst - fv) * _sigmoid(v0 + _dot(fvres, v2p))
        else:
            v2 = fv * 1.0
        kk = fk * k_k
        kk = kk / jnp.maximum(jnp.sqrt(_dot(kk * kk, seg)), 1e-12)
        k2 = fk * (1.0 + (a - 1.0) * k_a)
        return w, k2, v2, -kk, kk * a, g

    return fn


def _rwkv_pre_args(fl, vfirst, p, has_vres):
    ins = [fl, fl, fl]
    specs = [_rows(256, 1), _rows(256, 2), _rows(128, 6)]
    if has_vres:
        ins += [fl, vfirst]
        specs += [_rows(128, 7), _rows(256, 2)]
    names = ["w0", "w2p", "a0", "a2p", "g2p", "k_k", "k_a"] + (["v0", "v2p"] if has_vres else []) + ["seg64"]
    for nme in names:
        ins.append(p[nme])
        specs.append(_full(p[nme].shape))
    return ins, specs, names


def rwkv_pre_fwd(l, fl, vfirst, p):
    has_vres = l > 0
    ins, specs, _ = _rwkv_pre_args(fl, vfirst, p, has_vres)
    return _map_fwd(f"rwkv_pre_fwd{l}", _rwkv_pre_fn(has_vres), (SEQ // RB,), ins, specs,
                    [_sds((SEQ, DG))] * 6, [_rows(DG)] * 6)


def rwkv_pre_bwd(l, fl, vfirst, p, cts):
    has_vres = l > 0
    ins, specs, names = _rwkv_pre_args(fl, vfirst, p, has_vres)
    n_row = 5 if has_vres else 3
    want = list(range(n_row)) + [n_row + i for i, nme in enumerate(names) if nme != "seg64"]
    acc = tuple(w for w in want if w >= n_row)
    ct_specs = [[_rows(DG)] * len(g) for g in cts]
    gout = {0: (_sds((SEQ, DG)), _rows(DG)), 1: (_sds((SEQ, DG)), _rows(DG)), 2: (_sds((SEQ, 128)), _rows(128))}
    if has_vres:
        gout[3] = (_sds((SEQ, 128)), _rows(128))
        gout[4] = (_sds((SEQ, DG)), _rows(DG))
    gs = _map_bwd(f"rwkv_pre_bwd{l}", _rwkv_pre_fn(has_vres), (SEQ // RB,), ins, specs, cts, ct_specs, want, acc, gout)
    keys = ["fk", "fv", "flora"] + (["fvres", "vfirst"] if has_vres else []) + [nme for nme in names if nme != "seg64"]
    return dict(zip(keys, gs))


def _rwkv_post_fn(y, fr, k2, v2, g, lnx_w, lnx_b, r_k, seg):
    mu = _dot(y, seg) * (1.0 / HD)
    d = y - mu
    var = _dot(d * d, seg) * (1.0 / HD)
    yn = d * lax.rsqrt(var + GN_EPS) * lnx_w + lnx_b
    bonus = _dot(fr * k2 * r_k, seg) * v2
    return ((yn + bonus) * g,)


def _rwkv_post_args(y, fl, k2, v2, g, p):
    ins = [y, fl, k2, v2, g, p["lnx_w"], p["lnx_b"], p["r_k"], p["seg64"]]
    specs = [_rows(DG), _rows(DG, 0), _rows(DG), _rows(DG), _rows(DG)] + [_full(x.shape) for x in ins[5:]]
    return ins, specs


def rwkv_post_fwd(l, y, fl, k2, v2, g, p):
    ins, specs = _rwkv_post_args(y, fl, k2, v2, g, p)
    return _map_fwd(f"rwkv_post_fwd{l}", _rwkv_post_fn, (SEQ // RB,), ins, specs, [_sds((SEQ, DG))], [_rows(DG)])[0]


def rwkv_post_bwd(l, y, fl, k2, v2, g, p, dya):
    ins, specs = _rwkv_post_args(y, fl, k2, v2, g, p)
    gs = _map_bwd(f"rwkv_post_bwd{l}", _rwkv_post_fn, (SEQ // RB,), ins, specs, [[dya]], [[_rows(DG)]],
                  want=[0, 1, 2, 3, 4, 5, 6, 7], acc=(5, 6, 7), gout={1: (_sds((SEQ, DG)), _rows(DG))})
    return dict(zip(["y", "fr", "k2", "v2", "g", "lnx_w", "lnx_b", "r_k"], gs))


SCAN_TB = 64


def _coltile8(rows8, dmask, ones_stack, parts):
    pieces, rest = [], rows8
    for q in range(parts):
        piece = rest.astype(BF16).astype(F32)
        if q < parts - 1:
            rest = rest - piece
        pieces.append((piece[:, None, :] * dmask[None]).reshape(8 * HD, DG).astype(BF16))
    x = pieces[0] if parts == 1 else jnp.concatenate(pieces, axis=1)
    return jnp.dot(x, ones_stack, preferred_element_type=F32).reshape(8, HD, DG)


def _coltiles_bf16(rows_list, dmask, ones_bf16):
    x = jnp.concatenate([(r8[:, None, :] * dmask[None]).reshape(8 * HD, DG).astype(BF16) for r8 in rows_list], axis=0)
    t = jnp.dot(x, ones_bf16, preferred_element_type=F32)
    return [t[q * 8 * HD:(q + 1) * 8 * HD].reshape(8, HD, DG) for q in range(len(rows_list))]


def _segrows8(x8, dmask, ones_bf16):
    t = jnp.dot(x8.reshape(8 * HD, DG).astype(BF16), ones_bf16, preferred_element_type=F32).reshape(8, HD, DG)
    return jnp.sum(t * dmask[None], axis=1)


def rwkv_scan_fwd(l, fl, w, k2, v2, c, b, p):
    nblk = SEQ // SCAN_TB

    def body(r_ref, w_ref, k_ref, v_ref, c_ref, b_ref, ones_ref, dm_ref, y_ref, st_ref, s_sc):
        @pl.when(pl.program_id(0) == 0)
        def _():
            s_sc[...] = jnp.zeros_like(s_sc)

        ones3, ones = ones_ref[...], ones_ref[0:DG, :]
        dmask = dm_ref[...]

        def group(gi, carry):
            t0 = pl.multiple_of(gi * 8, 8)
            sl = pl.ds(t0, 8)
            v8 = v_ref[sl, :]
            wt = _coltile8(w_ref[sl, :], dmask, ones3, 3)
            ct, bt, kt, rt = _coltiles_bf16([c_ref[sl, :], b_ref[sl, :], k_ref[sl, :], r_ref[sl, :]], dmask, ones)
            t = s_sc[...]
            for j in range(8):
                sa = jnp.sum(t * ct[j], axis=0, keepdims=True)
                t = t * wt[j] + bt[j] * sa + kt[j] * v8[j:j + 1, :]
                st_ref[t0 + j] = t
            s_sc[...] = t
            y_ref[sl, :] = jnp.sum(st_ref[sl] * rt, axis=1)
            return carry

        lax.fori_loop(0, SCAN_TB // 8, group, 0)

    row = pl.BlockSpec((SCAN_TB, DG), lambda i: (i, 0))
    ins = [fl, w, k2, v2, c, b, p["seg64x3_bf16"], p["dmask"]]
    specs = [row] * 6 + [_full((3 * DG, DG)), _full((HD, DG))]
    return pl.pallas_call(body, grid=(nblk,), in_specs=specs,
                          out_specs=[row, pl.BlockSpec((SCAN_TB, HD, DG), lambda i: (i, 0, 0))],
                          out_shape=[_sds((SEQ, DG)), _sds((SEQ, HD, DG))],
                          scratch_shapes=[pltpu.VMEM((HD, DG), F32)], name=f"rwkv_scan_fwd{l}",
                          compiler_params=_cp(("arbitrary",)))(*ins)


def rwkv_scan_bwd(l, fl, w, k2, v2, c, b, states, dy, p):
    nblk = SEQ // SCAN_TB

    def body(r_ref, w_ref, k_ref, v_ref, c_ref, b_ref, dy_ref, st_ref, sp_ref, ones_ref, dm_ref,
             dr_ref, dw_ref, dk_ref, dv_ref, dc_ref, db_ref, g_sc, prev_sc, d8_sc, dsa_sc):
        i = pl.program_id(0)

        @pl.when(i == 0)
        def _():
            g_sc[...] = jnp.zeros_like(g_sc)

        ones3, ones = ones_ref[...], ones_ref[0:DG, :]
        dmask = dm_ref[...]
        first_block = i == nblk - 1

        def group(gr, carry):
            gi = SCAN_TB // 8 - 1 - gr
            t0 = pl.multiple_of(gi * 8, 8)
            sl = pl.ds(t0, 8)
            v8, dy8 = v_ref[sl, :], dy_ref[sl, :]
            t8 = st_ref[sl]
            @pl.when(gi > 0)
            def _():
                prev_sc[0] = st_ref[t0 - 1]

            @pl.when(gi == 0)
            def _():
                prev_sc[0] = jnp.where(first_block, 0.0, sp_ref[0])

            for j in range(1, 8):
                prev_sc[j] = t8[j - 1]
            tp8 = prev_sc[...]
            wt = _coltile8(w_ref[sl, :], dmask, ones3, 3)
            ct, bt, kt, rt = _coltiles_bf16([c_ref[sl, :], b_ref[sl, :], k_ref[sl, :], r_ref[sl, :]], dmask, ones)
            sa8 = jnp.sum(tp8 * ct, axis=1)
            g = g_sc[...]
            for j in range(7, -1, -1):
                g = g + rt[j] * dy8[j:j + 1, :]
                d8_sc[j] = g
                dsa = jnp.sum(g * bt[j], axis=0, keepdims=True)
                dsa_sc[j:j + 1, :] = dsa
                g = g * wt[j] + ct[j] * dsa
            g_sc[...] = g
            d8 = d8_sc[...]
            dsa8 = dsa_sc[...]
            dv_ref[sl, :] = jnp.sum(d8 * kt, axis=1)
            dr_ref[sl, :] = _segrows8(t8 * dy8[:, None, :], dmask, ones)
            dk_ref[sl, :] = _segrows8(d8 * v8[:, None, :], dmask, ones)
            dw_ref[sl, :] = _segrows8(tp8 * d8, dmask, ones)
            db_ref[sl, :] = _segrows8(d8 * sa8[:, None, :], dmask, ones)
            dc_ref[sl, :] = _segrows8(tp8 * dsa8[:, None, :], dmask, ones)
            return carry

        lax.fori_loop(0, SCAN_TB // 8, group, 0)

    row = pl.BlockSpec((SCAN_TB, DG), lambda i: (nblk - 1 - i, 0))
    st_spec = pl.BlockSpec((SCAN_TB, HD, DG), lambda i: (nblk - 1 - i, 0, 0))
    sp_spec = pl.BlockSpec((1, HD, DG), lambda i: (jnp.maximum((nblk - 1 - i) * SCAN_TB - 1, 0), 0, 0))
    ins = [fl, w, k2, v2, c, b, dy, states, states, p["seg64x3_bf16"], p["dmask"]]
    specs = [row] * 7 + [st_spec, sp_spec, _full((3 * DG, DG)), _full((HD, DG))]
    tile8 = pltpu.VMEM((8, HD, DG), F32)
    return pl.pallas_call(body, grid=(nblk,), in_specs=specs, out_specs=[row] * 6, out_shape=[_sds((SEQ, DG))] * 6,
                          scratch_shapes=[pltpu.VMEM((HD, DG), F32), tile8, tile8, pltpu.VMEM((8, DG), F32)],
                          name=f"rwkv_scan_bwd{l}", compiler_params=_cp(("arbitrary",)))(*ins)


HG_ROWS = 128


def _hgrn_chunk_fn(layer):
    def fn(hq, hf, hi, hg, sprev, lb0, lb1, norm_w, seg, bd, tri, causal, ones16):
        e0 = jnp.exp(lb0 - jnp.maximum(lb0, lb1))
        e1 = jnp.exp(lb1 - jnp.maximum(lb0, lb1))
        sm0, sm1 = e0 / (e0 + e1), e1 / (e0 + e1)
        lb = (sm0 - sm0) if layer == 0 else ((sm0 + sm1) - sm0)
        forget = lb + (1.0 - lb) * _sigmoid(hf)
        logf = jnp.log(forget)
        kk = 1.0 - forget
        q = _silu(hq)
        c = HGRN_CHUNK
        b = _dotx(tri, logf)
        bl = jnp.sum(logf, axis=0, keepdims=True)
        diff = (b[:, None, :] - b[None, :, :]).reshape(c * c, DG)
        dec = jnp.exp(jnp.where(causal > 0.5, diff, -1e30))
        qrep = jnp.broadcast_to(q[:, None, :], (c, c, DG)).reshape(c * c, DG)
        ktil = jnp.broadcast_to(kk[None, :, :], (c, c, DG)).reshape(c * c, DG)
        vtil = jnp.broadcast_to(hi[None, :, :], (c, c, DG)).reshape(c * c, DG)
        att = _dot(qrep * ktil * dec, seg)
        o_intra = jnp.sum((att * vtil).reshape(c, c, DG), axis=1)
        kdec = kk * jnp.exp(bl - b)
        u = _dot_tn(hi, kdec) * bd
        snext = sprev * jnp.exp(bl) + u
        o = o_intra + _dot_nt(q * jnp.exp(b), sprev)
        ms = _dot(o * o, seg) * (1.0 / HD)
        y = o * lax.rsqrt(ms + RMS_EPS) * norm_w * _silu(hg)
        return y, snext

    return fn


def _hgrn_consts(p):
    return [p["seg64"], p["seg64"], p["tri16"], p["causal16"], p["ones16"]]


def hgrn_fwd(l, proj, p):
    fn = _hgrn_chunk_fn(l)
    nch = HG_ROWS // HGRN_CHUNK

    def body(hq_ref, hf_ref, hi_ref, hg_ref, lb0_ref, lb1_ref, nw_ref, seg_ref, bd_ref, tri_ref, cau_ref, o16_ref,
             y_ref, st_ref, s_sc):
        @pl.when(pl.program_id(0) == 0)
        def _():
            s_sc[...] = jnp.zeros_like(s_sc)

        consts = (lb0_ref[...], lb1_ref[...], nw_ref[...], seg_ref[...], bd_ref[...], tri_ref[...], cau_ref[...],
                  o16_ref[...])

        def chunk(ci, carry):
            sl = pl.ds(pl.multiple_of(ci * HGRN_CHUNK, HGRN_CHUNK), HGRN_CHUNK)
            sprev = s_sc[...]
            st_ref[ci] = sprev
            y, snext = fn(hq_ref[sl, :], hf_ref[sl, :], hi_ref[sl, :], hg_ref[sl, :], sprev, *consts)
            y_ref[sl, :] = y
            s_sc[...] = snext
            return carry

        lax.fori_loop(0, nch, chunk, 0)

    rows = lambda cb: pl.BlockSpec((HG_ROWS, DG), lambda i: (i, cb))
    ins = [proj, proj, proj, proj, p["lb0"], p["lb1"], p["hgrn_norm_w"]] + _hgrn_consts(p)
    specs = [rows(C_HQ // DG), rows(C_HF // DG), rows(C_HI // DG), rows(C_HG // DG)] + [_full(x.shape) for x in ins[4:]]
    return pl.pallas_call(body, grid=(SEQ // HG_ROWS,), in_specs=specs,
                          out_specs=[rows(0), pl.BlockSpec((nch, DG, DG), lambda i: (i, 0, 0))],
                          out_shape=[_sds((SEQ, DG)), _sds((SEQ // HGRN_CHUNK, DG, DG))],
                          scratch_shapes=[pltpu.VMEM((DG, DG), F32)], name=f"hgrn_fwd{l}",
                          compiler_params=_cp(("arbitrary",)))(*ins)


def hgrn_bwd(l, proj, states, dy, p):
    fn = _hgrn_chunk_fn(l)
    nch = HG_ROWS // HGRN_CHUNK
    nblk = SEQ // HG_ROWS

    def body(hq_ref, hf_ref, hi_ref, hg_ref, st_ref, dy_ref, lb0_ref, lb1_ref, nw_ref, seg_ref, bd_ref, tri_ref,
             cau_ref, o16_ref, dp_ref, dlb0_ref, dlb1_ref, dnw_ref, ds_sc):
        @pl.when(pl.program_id(0) == 0)
        def _():
            ds_sc[...] = jnp.zeros_like(ds_sc)
            dlb0_ref[...] = jnp.zeros_like(dlb0_ref)
            dlb1_ref[...] = jnp.zeros_like(dlb1_ref)
            dnw_ref[...] = jnp.zeros_like(dnw_ref)

        consts = (seg_ref[...], bd_ref[...], tri_ref[...], cau_ref[...], o16_ref[...])

        def chunk(cr, carry):
            ci = nch - 1 - cr
            sl = pl.ds(pl.multiple_of(ci * HGRN_CHUNK, HGRN_CHUNK), HGRN_CHUNK)
            f = lambda hq, hf, hi, hg, sp, b0, b1, nw: fn(hq, hf, hi, hg, sp, b0, b1, nw, *consts)
            _, vjp = jax.vjp(f, hq_ref[sl, :], hf_ref[sl, :], hi_ref[sl, :], hg_ref[sl, :], st_ref[ci],
                             lb0_ref[...], lb1_ref[...], nw_ref[...])
            dhq, dhf, dhi, dhg, dsp, dlb0, dlb1, dnw = vjp((dy_ref[sl, :], ds_sc[...]))
            dp_ref[sl, 0:DG] = dhq
            dp_ref[sl, DG:2 * DG] = dhf
            dp_ref[sl, 2 * DG:3 * DG] = dhi
            dp_ref[sl, 3 * DG:4 * DG] = dhg
            ds_sc[...] = dsp
            dlb0_ref[...] += dlb0
            dlb1_ref[...] += dlb1
            dnw_ref[...] += dnw
            return carry

        lax.fori_loop(0, nch, chunk, 0)

    rows = lambda cb: pl.BlockSpec((HG_ROWS, DG), lambda i: (nblk - 1 - i, cb))
    ins = [proj, proj, proj, proj, states, dy, p["lb0"], p["lb1"], p["hgrn_norm_w"]] + _hgrn_consts(p)
    specs = [rows(C_HQ // DG), rows(C_HF // DG), rows(C_HI // DG), rows(C_HG // DG),
             pl.BlockSpec((nch, DG, DG), lambda i: (nblk - 1 - i, 0, 0)), rows(0)] + [_full(x.shape) for x in ins[6:]]
    return pl.pallas_call(body, grid=(nblk,), in_specs=specs,
                          out_specs=[pl.BlockSpec((HG_ROWS, 4 * DG), lambda i: (nblk - 1 - i, 0)), _full((1, DG)),
                                     _full((1, DG)), _full((1, DG))],
                          out_shape=[_sds((SEQ, 4 * DG)), _sds((1, DG)), _sds((1, DG)), _sds((1, DG))],
                          scratch_shapes=[pltpu.VMEM((DG, DG), F32)], name=f"hgrn_bwd{l}",
                          compiler_params=_cp(("arbitrary",)))(*ins)


def _ssd_chunk_fn(z, xs, bm, cm, dtr, sprev, dt_bias, a_log, d_par, norm_w, e128, tri, trit, seg128, ones128):
    lc = SSD_CHUNK
    dt = _softplus(dtr + dt_bias)
    a = -jnp.exp(a_log)
    da = dt * a * (lax.broadcasted_iota(jnp.int32, (1, 128), 1) < NH).astype(F32)
    cs = _dotx(tri, da)
    cst = _dotx_tn(da, trit)
    cs_b = _dotx(cs, e128)
    dt_b = _dotx(dt, e128)
    csl_b = _dotx(jnp.sum(da, axis=0, keepdims=True), e128)
    xdt = xs * dt_b
    lane = lax.broadcasted_iota(jnp.int32, (1, DG), 1)
    rowi = lax.broadcasted_iota(jnp.int32, (lc, lc), 0)
    coli = lax.broadcasted_iota(jnp.int32, (lc, lc), 1)
    y = jnp.zeros((lc, DG), F32)
    snew = jnp.zeros((DG, SSD_N), F32)
    d_b = jnp.zeros((1, DG), F32)
    wdec = xdt * jnp.exp(csl_b - cs_b)
    for g in range(2):
        bg = bm[:, g * SSD_N:(g + 1) * SSD_N]
        cg = cm[:, g * SSD_N:(g + 1) * SSD_N]
        gmat = _dot_nt(cg, bg)
        gmask = ((lane // 128) == g).astype(F32)
        snew = snew + _dot_tn(wdec * gmask, bg)
        y = y + _dot_nt(cg, sprev) * gmask * jnp.exp(cs_b)
        for hh in range(2):
            h = 2 * g + hh
            seg = jnp.where(rowi >= coli, cs[:, h:h + 1] - cst[h:h + 1, :], -1e30)
            hmask = ((lane // HD) == h).astype(F32)
            y = y + _dot(gmat * jnp.exp(seg), xdt * hmask)
            d_b = d_b + d_par[:, h:h + 1] * hmask
    cd = jnp.exp(_dotx_tn(_dotx(da, e128), ones128))
    snext = sprev * cd + snew
    y = y + xs * d_b
    y = y * _silu(z)
    ms = _dot(y * y, seg128) * (1.0 / 128.0)
    return y * lax.rsqrt(ms + RMS_EPS) * norm_w, snext


def ssd_fwd(l, proj, xc, p):
    nc = SEQ // SSD_CHUNK

    def body(z_ref, xs_ref, b_ref, c_ref, dt_ref, dtb_ref, al_ref, d_ref, nw_ref, e_ref, tri_ref, trit_ref, sg_ref,
             on_ref, y_ref, st_ref, s_sc):
        @pl.when(pl.program_id(0) == 0)
        def _():
            s_sc[...] = jnp.zeros_like(s_sc)

        sprev = s_sc[...]
        st_ref[0] = sprev
        y, snext = _ssd_chunk_fn(z_ref[...], xs_ref[...], b_ref[...], c_ref[...], dt_ref[...], sprev, dtb_ref[...],
                                 al_ref[...], d_ref[...], nw_ref[...], e_ref[...], tri_ref[...], trit_ref[...],
                                 sg_ref[...], on_ref[...])
        y_ref[...] = y
        s_sc[...] = snext

    rw = lambda w, cb: pl.BlockSpec((SSD_CHUNK, w), lambda i: (i, cb))
    ins = [proj, xc, xc, xc, proj, p["dt_bias"], p["a_log"], p["ssd_d"], p["ssd_norm_w"], p["e128"], p["tri128"],
           p["tri128t"], p["seg128"], p["ones128"]]
    specs = [rw(DG, C_Z // DG), rw(DG, 0), rw(DG, 1), rw(DG, 2), rw(128, C_DT // 128)] + [_full(x.shape) for x in ins[5:]]
    return pl.pallas_call(body, grid=(nc,), in_specs=specs,
                          out_specs=[rw(DG, 0), pl.BlockSpec((1, DG, SSD_N), lambda i: (i, 0, 0))],
                          out_shape=[_sds((SEQ, DG)), _sds((nc, DG, SSD_N))],
                          scratch_shapes=[pltpu.VMEM((DG, SSD_N), F32)], name=f"ssd_fwd{l}",
                          compiler_params=_cp(("arbitrary",)))(*ins)


def ssd_bwd(l, proj, xc, states, dy, p):
    nc = SEQ // SSD_CHUNK

    def body(z_ref, xs_ref, b_ref, c_ref, dt_ref, st_ref, dy_ref, dtb_ref, al_ref, d_ref, nw_ref, e_ref, tri_ref,
             trit_ref, sg_ref, on_ref, dz_ref, dxc_ref, ddt_ref, ddtb_ref, dal_ref, dd_ref, dnw_ref, ds_sc):
        @pl.when(pl.program_id(0) == 0)
        def _():
            ds_sc[...] = jnp.zeros_like(ds_sc)
            ddtb_ref[...] = jnp.zeros_like(ddtb_ref)
            dal_ref[...] = jnp.zeros_like(dal_ref)
            dd_ref[...] = jnp.zeros_like(dd_ref)
            dnw_ref[...] = jnp.zeros_like(dnw_ref)

        consts = (e_ref[...], tri_ref[...], trit_ref[...], sg_ref[...], on_ref[...])
        f = lambda *a: _ssd_chunk_fn(*a, *consts)
        _, vjp = jax.vjp(f, z_ref[...], xs_ref[...], b_ref[...], c_ref[...], dt_ref[...], st_ref[0], dtb_ref[...],
                         al_ref[...], d_ref[...], nw_ref[...])
        dz, dxs, db, dc, ddt, dsp, ddtb, dal, dd, dnw = vjp((dy_ref[...], ds_sc[...]))
        dz_ref[...] = dz
        dxc_ref[:, 0:DG] = dxs
        dxc_ref[:, DG:2 * DG] = db
        dxc_ref[:, 2 * DG:3 * DG] = dc
        ddt_ref[...] = ddt
        ds_sc[...] = dsp
        ddtb_ref[...] += ddtb
        dal_ref[...] += dal
        dd_ref[...] += dd
        dnw_ref[...] += dnw

    rw = lambda w, cb: pl.BlockSpec((SSD_CHUNK, w), lambda i: (nc - 1 - i, cb))
    ins = [proj, xc, xc, xc, proj, states, dy, p["dt_bias"], p["a_log"], p["ssd_d"], p["ssd_norm_w"], p["e128"],
           p["tri128"], p["tri128t"], p["seg128"], p["ones128"]]
    specs = [rw(DG, C_Z // DG), rw(DG, 0), rw(DG, 1), rw(DG, 2), rw(128, C_DT // 128),
             pl.BlockSpec((1, DG, SSD_N), lambda i: (nc - 1 - i, 0, 0)), rw(DG, 0)] + [_full(x.shape) for x in ins[7:]]
    return pl.pallas_call(body, grid=(nc,), in_specs=specs,
                          out_specs=[rw(DG, 0), rw(3 * DG, 0), rw(128, 0), _full((1, 128)), _full((1, 128)), _full((1, 128)),
                                     _full((1, DG))],
                          out_shape=[_sds((SEQ, DG)), _sds((SEQ, 3 * DG)), _sds((SEQ, 128)), _sds((1, 128)), _sds((1, 128)),
                                     _sds((1, 128)), _sds((1, DG))],
                          scratch_shapes=[pltpu.VMEM((DG, SSD_N), F32)], name=f"ssd_bwd{l}",
                          compiler_params=_cp(("arbitrary",)))(*ins)


ATT_BLK = 128


def _att_scores(qn, kc, kp, h, dil, has_prev):
    i = lax.broadcasted_iota(jnp.int32, (ATT_BLK, ATT_BLK), 0)
    j = lax.broadcasted_iota(jnp.int32, (ATT_BLK, ATT_BLK), 1)
    slope = 2.0 ** (-8.0 * (h + 1) / NH)
    scale = HD ** -0.5
    s_c = _dot_nt(qn, kc) * scale - slope * ((i - j) * dil).astype(F32)
    s_p = _dot_nt(qn, kp) * scale - slope * ((ATT_BLK + i - j) * dil).astype(F32)
    m_c = j <= i
    m_p = jnp.logical_and(j >= i, has_prev)
    return jnp.where(m_c, s_c, -1e30), jnp.where(m_p, s_p, -1e30), m_c, m_p


def _sub_spec(ln, width, col):
    return pl.BlockSpec((ln, DG), lambda z: (0, z * (width // DG) + col // DG))


def attn_branch_fwd(l, bi, proj, dil):
    ln = SEQ // dil
    nb = ln // ATT_BLK

    def body(q_ref, k_ref, v_ref, o_ref, l_ref):
        def blk(n, carry):
            r0 = pl.multiple_of(n * ATT_BLK, ATT_BLK)
            rp = pl.multiple_of(jnp.maximum(n - 1, 0) * ATT_BLK, ATT_BLK)
            cur, prv = pl.ds(r0, ATT_BLK), pl.ds(rp, ATT_BLK)
            for h in range(NH):
                hs = slice(h * HD, (h + 1) * HD)
                qn, kc, vc, kp, vp = q_ref[cur, hs], k_ref[cur, hs], v_ref[cur, hs], k_ref[prv, hs], v_ref[prv, hs]
                s_c, s_p, m_c, m_p = _att_scores(qn, kc, kp, h, dil, n > 0)
                m = jnp.maximum(jnp.max(s_c, axis=1, keepdims=True), jnp.max(s_p, axis=1, keepdims=True))
                p_c = jnp.where(m_c, jnp.exp(s_c - m), 0.0)
                p_p = jnp.where(m_p, jnp.exp(s_p - m), 0.0)
                den = jnp.sum(p_c, axis=1, keepdims=True) + jnp.sum(p_p, axis=1, keepdims=True)
                o_ref[cur, hs] = (_dot(p_c, vc) + _dot(p_p, vp)) / den
                l_ref[cur, hs] = jnp.broadcast_to(m + jnp.log(den), (ATT_BLK, HD))
            return carry

        lax.fori_loop(0, nb, blk, 0)

    pv = proj.reshape(ln, dil * PW)
    out = pl.BlockSpec((ln, DG), lambda z: (0, z))
    o, lse = pl.pallas_call(body, grid=(dil,), in_specs=[_sub_spec(ln, PW, C_AQ), _sub_spec(ln, PW, C_AK), _sub_spec(ln, PW, C_AV)],
                            out_specs=[out, out], out_shape=[_sds((ln, dil * DG))] * 2, name=f"attn_fwd{l}_{bi}",
                            compiler_params=_cp(("parallel",)))(pv, pv, pv)
    return o.reshape(SEQ, DG), lse.reshape(SEQ, DG)


def attn_branch_bwd(l, bi, proj, dil, dyb, lse_all, delta):
    ln = SEQ // dil
    nb = ln // ATT_BLK
    scale = HD ** -0.5

    def body(q_ref, k_ref, v_ref, do_ref, l_ref, dl_ref, dq_ref, dk_ref, dv_ref):
        dk_ref[...] = jnp.zeros_like(dk_ref)
        dv_ref[...] = jnp.zeros_like(dv_ref)

        def blk(n, carry):
            r0 = pl.multiple_of(n * ATT_BLK, ATT_BLK)
            rp = pl.multiple_of(jnp.maximum(n - 1, 0) * ATT_BLK, ATT_BLK)
            cur, prv = pl.ds(r0, ATT_BLK), pl.ds(rp, ATT_BLK)
            for h in range(NH):
                hs = slice(h * HD, (h + 1) * HD)
                qn, don = q_ref[cur, hs], do_ref[cur, hs]
                lse, dlt = l_ref[cur, h * HD:h * HD + 1], dl_ref[cur, h * HD:h * HD + 1]
                kc, vc, kp, vp = k_ref[cur, hs], v_ref[cur, hs], k_ref[prv, hs], v_ref[prv, hs]
                s_c, s_p, m_c, m_p = _att_scores(qn, kc, kp, h, dil, n > 0)
                p_c = jnp.where(m_c, jnp.exp(s_c - lse), 0.0)
                p_p = jnp.where(m_p, jnp.exp(s_p - lse), 0.0)
                ds_c = p_c * (_dot_nt(don, vc) - dlt)
                ds_p = p_p * (_dot_nt(don, vp) - dlt)
                dq_ref[cur, hs] = (_dot(ds_c, kc) + _dot(ds_p, kp)) * scale
                dv_ref[prv, hs] += _dot_tn(p_p, don)
                dk_ref[prv, hs] += _dot_tn(ds_p, qn) * scale
                dv_ref[cur, hs] += _dot_tn(p_c, don)
                dk_ref[cur, hs] += _dot_tn(ds_c, qn) * scale
            return carry

        lax.fori_loop(0, nb, blk, 0)

    pv = proj.reshape(ln, dil * PW)
    sub = lambda t: t.reshape(ln, dil * DG)
    row = pl.BlockSpec((ln, DG), lambda z: (0, z))
    outs = pl.pallas_call(body, grid=(dil,),
                          in_specs=[_sub_spec(ln, PW, C_AQ), _sub_spec(ln, PW, C_AK), _sub_spec(ln, PW, C_AV), row, row, row],
                          out_specs=[row] * 3, out_shape=[_sds((ln, dil * DG))] * 3, name=f"attn_bwd{l}_{bi}",
                          compiler_params=_cp(("parallel",)))(pv, pv, pv, sub(dyb), sub(lse_all), sub(delta))
    return [t.reshape(SEQ, DG) for t in outs]


def _attn_merge_fn(o1, o2, o3, l1, l2, l3):
    m = jnp.maximum(jnp.maximum(l1, l2), l3)
    w1, w2, w3 = jnp.exp(l1 - m), jnp.exp(l2 - m), jnp.exp(l3 - m)
    den = w1 + w2 + w3
    return (w1 * o1 + w2 * o2 + w3 * o3) / den, m + jnp.log(den)


def attn_merge(l, os_, ls_):
    ins = list(os_) + list(ls_)
    return _map_fwd(f"attn_merge{l}", _attn_merge_fn, (SEQ // RB,), ins, [_rows(DG)] * 6, [_sds((SEQ, DG))] * 2,
                    [_rows(DG)] * 2)


def attn_delta(l, dyb, yb, seg):
    fn = lambda d, y, s: (_dot(d * y, s),)
    return _map_fwd(f"attn_delta{l}", fn, (SEQ // RB,), [dyb, yb, seg], [_rows(DG), _rows(DG), _full((DG, DG))],
                    [_sds((SEQ, DG))], [_rows(DG)])[0]


def _ln_fn(x, mix, w, b):
    h = ALPHA * x + mix
    mu = jnp.mean(h, axis=-1, keepdims=True)
    d = h - mu
    var = jnp.mean(d * d, axis=-1, keepdims=True)
    return (d * lax.rsqrt(var + LN_EPS) * w + b,)


def ln_fwd(name, x, mix, w, b):
    specs = [_rows(D_MODEL), _rows(D_MODEL), _full((1, D_MODEL)), _full((1, D_MODEL))]
    return _map_fwd(name, _ln_fn, (SEQ // RB,), [x, mix, w, b], specs, [_sds((SEQ, D_MODEL))], [_rows(D_MODEL)])[0]


def ln_bwd(name, x, mix, w, b, dy):
    specs = [_rows(D_MODEL), _rows(D_MODEL), _full((1, D_MODEL)), _full((1, D_MODEL))]
    return _map_bwd(name, _ln_fn, (SEQ // RB,), [x, mix, w, b], specs, [[dy]], [[_rows(D_MODEL)]], want=[1, 2, 3],
                    acc=(2, 3))


def _relu2_fn(u):
    r = jnp.maximum(u, 0.0)
    return (r * r,)


def relu2_fwd(name, u):
    return _map_fwd(name, _relu2_fn, (SEQ // RB,), [u], [_rows(D_FF)], [_sds((SEQ, D_FF))], [_rows(D_FF)])[0]


def relu2_bwd(name, u, dh):
    fn = lambda uu, g: (g * 2.0 * jnp.maximum(uu, 0.0),)
    return _map_fwd(name, fn, (SEQ // RB,), [u, dh], [_rows(D_FF)] * 2, [_sds((SEQ, D_FF))], [_rows(D_FF)])[0]


def loss_call(y, tgt):
    def fn(yy, tt):
        e = yy - tt
        part = 0.5 * jnp.sum(jnp.sum(e * e, axis=-1, keepdims=True) * (1.0 / D_MODEL), axis=0, keepdims=True)
        return e * (1.0 / D_MODEL), jnp.broadcast_to(part, (8, 128))

    return _map_fwd("loss", fn, (SEQ // RB,), [y, tgt], [_rows(D_MODEL)] * 2,
                    [_sds((SEQ, D_MODEL)), _sds((SEQ // RB * 8, 128))],
                    [_rows(D_MODEL), pl.BlockSpec((8, 128), lambda i: (i, 0))])


def layer_fwd(l, x, vfirst, wts, p):
    sv = {"x": x}
    proj = _mm(f"mm_in{l}", x, wts["w_in"], "nn", 512, 1024, 1024)
    fl = lerp_fwd(l, proj, p["mu"])
    xc = conv_fwd(l, proj, p["conv_w"], p["conv_b"])
    w, k2, v2, c, b, g = rwkv_pre_fwd(l, fl, vfirst, p)
    y_scan, states = rwkv_scan_fwd(l, fl, w, k2, v2, c, b, p)
    ya = rwkv_post_fwd(l, y_scan, fl, k2, v2, g, p)
    outs, lses = [], []
    for bi, (win, dil) in enumerate(DILATED):
        o, lse = attn_branch_fwd(l, bi, proj, dil)
        outs.append(o)
        lses.append(lse)
    yb, lse_all = attn_merge(l, outs, lses)
    yc, ssd_states = ssd_fwd(l, proj, xc, p)
    yd, hg_states = hgrn_fwd(l, proj, p)
    ycat = jnp.concatenate([ya, yb, yc, yd], axis=1)
    mix = _mm(f"mm_out{l}", ycat, wts["w_out"], "nn", 512, 1024, 1024)
    x1 = ln_fwd(f"ln1_fwd{l}", x, mix, p["ln1_w"], p["ln1_b"])
    u = _mm(f"mm_up{l}", x1, wts["w_up_t"], "nt", 512, 1024, 1024)
    hh = relu2_fwd(f"relu2_fwd{l}", u)
    m2 = _mm(f"mm_down{l}", hh, wts["w_down"], "nn", 512, 1024, 1024)
    x2 = ln_fwd(f"ln2_fwd{l}", x1, m2, p["ln2_w"], p["ln2_b"])
    sv.update(proj=proj, fl=fl, xc=xc, w=w, k2=k2, v2=v2, c=c, b=b, g=g, y_scan=y_scan, states=states,
              yb=yb, lse_all=lse_all, ssd_states=ssd_states, hg_states=hg_states, ycat=ycat, mix=mix, x1=x1, u=u, hh=hh,
              m2=m2, vfirst=vfirst)
    return x2, sv


def layer_bwd(l, dx2, dvfirst_next, sv, wts, p):
    gr = {}
    x, x1, proj, fl = sv["x"], sv["x1"], sv["proj"], sv["fl"]
    dres2, gr["ln2_w"], gr["ln2_b"] = ln_bwd(f"ln2_bwd{l}", x1, sv["m2"], p["ln2_w"], p["ln2_b"], dx2)
    dh = _mm(f"mm_down_dx{l}", dres2, wts["w_down"], "nt", 512, 1024, 1024)
    gr["w_down"] = _mm(f"mm_down_dw{l}", sv["hh"], dres2, "tn", 512, 1024, 512)
    du = relu2_bwd(f"relu2_bwd{l}", sv["u"], dh)
    dx1 = _mm(f"mm_up_dx{l}", du, wts["w_up_t"], "nn", 512, 1024, 1024, add=dres2, add_scale=ALPHA)
    gr["w_up_t"] = _mm(f"mm_up_dw{l}", du, x1, "tn", 512, 1024, 512)
    dres1, gr["ln1_w"], gr["ln1_b"] = ln_bwd(f"ln1_bwd{l}", x, sv["mix"], p["ln1_w"], p["ln1_b"], dx1)
    dycat = _mm(f"mm_out_dx{l}", dres1, wts["w_out"], "nt", 512, 1024, 1024)
    gr["w_out"] = _mm(f"mm_out_dw{l}", sv["ycat"], dres1, "tn", 512, 1024, 512)
    dya, dyb, dyc, dyd = (dycat[:, i * DG:(i + 1) * DG] for i in range(4))
    dhg4, gr["lb0"], gr["lb1"], gr["hgrn_norm_w"] = hgrn_bwd(l, proj, sv["hg_states"], dyd, p)
    dz, dxc, ddt, gr["dt_bias"], gr["a_log"], gr["ssd_d"], gr["ssd_norm_w"] = ssd_bwd(l, proj, sv["xc"], sv["ssd_states"], dyc, p)
    dxbc, gr["conv_w"], gr["conv_b"] = conv_bwd(l, proj, p["conv_w"], p["conv_b"], dxc)
    delta = attn_delta(l, dyb, sv["yb"], p["seg64"])
    dqs, dks, dvs = [], [], []
    for bi, (win, dil) in enumerate(DILATED):
        dq, dk, dv = attn_branch_bwd(l, bi, proj, dil, dyb, sv["lse_all"], delta)
        dqs.append(dq)
        dks.append(dk)
        dvs.append(dv)
    dq_a, dk_a, dv_a = _addn(f"attn_dq{l}", *dqs), _addn(f"attn_dk{l}", *dks), _addn(f"attn_dv{l}", *dvs)
    pg = rwkv_post_bwd(l, sv["y_scan"], fl, sv["k2"], sv["v2"], sv["g"], p, dya)
    gr["lnx_w"], gr["lnx_b"], gr["r_k"] = pg["lnx_w"], pg["lnx_b"], pg["r_k"]
    dr, dw, dk, dv, dc, db = rwkv_scan_bwd(l, fl, sv["w"], sv["k2"], sv["v2"], sv["c"], sv["b"], sv["states"], pg["y"], p)
    v2_cts = [dv, pg["v2"]] + ([dvfirst_next] if dvfirst_next is not None else [])
    qg = rwkv_pre_bwd(l, fl, sv["vfirst"], p, [[dw], [dk, pg["k2"]], v2_cts, [dc], [db], [pg["g"]]])
    for nme in ("w0", "w2p", "a0", "a2p", "g2p", "k_k", "k_a", "v0", "v2p"):
        if nme in qg:
            gr[nme] = qg[nme]
    dfr = _addn(f"rwkv_dr{l}", dr, pg["fr"])
    dvres = qg["fvres"] if l > 0 else jnp.zeros((SEQ, 128), F32)
    dfl_out = jnp.concatenate([dfr, qg["fk"], qg["fv"], qg["flora"], dvres], axis=1)
    dfl_in, gr["mu"] = lerp_bwd(l, proj, p["mu"], dfl_out)
    dproj = jnp.concatenate([dfl_in[:, 0:768], dq_a, dk_a, dv_a, dz, dxbc, dhg4, dfl_in[:, 768:896], ddt,
                             dfl_in[:, 896:1024], jnp.zeros((SEQ, 128), F32)], axis=1)
    dx = _mm(f"mm_in_dx{l}", dproj, wts["w_in"], "nt", 512, 1024, 1024, add=dres1, add_scale=ALPHA)
    gr["w_in"] = _mm(f"mm_in_dw{l}", x, dproj, "tn", 512, 1024, 512)
    return dx, (qg["vfirst"] if l > 0 else None), gr


def _w_in_pad(w_in_l, w_vres):
    rows = w_in_l.shape[0]
    z = lambda n: jnp.zeros((rows, n), w_in_l.dtype)
    vres = z(128) if w_vres is None else jnp.concatenate([w_vres, z(96)], axis=1)
    return jnp.concatenate([w_in_l[:, 0:768], w_in_l[:, 896:1664], w_in_l[:, 1664:1920], w_in_l[:, 1920:2688],
                            w_in_l[:, 2692:3716], w_in_l[:, 768:896], w_in_l[:, 2688:2692], z(124), vres, z(128)], axis=1)


def _w_in_unpad(g):
    g_in = jnp.concatenate([g[:, 0:768], g[:, C_LORA:C_LORA + 128], g[:, 768:1536], g[:, C_Z:C_Z + 256],
                            g[:, C_XBC:C_XBC + 768], g[:, C_DT:C_DT + 4], g[:, C_HQ:C_HQ + 1024]], axis=1)
    return g_in, g[:, C_VRES:C_VRES + 32]


def _consts():
    i16 = jnp.arange(HGRN_CHUNK)
    pair = jnp.arange(HGRN_CHUNK * HGRN_CHUNK)
    i128 = jnp.arange(128)
    seg64 = _seg_ones(DG, HD)
    tri128 = (i128[:, None] >= i128[None, :]).astype(F32)
    return dict(
        seg64=seg64, seg64x3_bf16=jnp.concatenate([seg64, seg64, seg64], axis=0).astype(BF16),
        dmask=(jnp.arange(HD)[:, None] == (jnp.arange(DG)[None, :] % HD)).astype(F32),
        tri16=(i16[:, None] >= i16[None, :]).astype(F32),
        causal16=jnp.broadcast_to(((pair // HGRN_CHUNK) >= (pair % HGRN_CHUNK)).astype(F32)[:, None], (256, DG)),
        ones16=jnp.ones((HGRN_CHUNK, DG), F32),
        e128=((i128[:, None] == (jnp.arange(DG)[None, :] // HD)) & (i128[:, None] < NH)).astype(F32),
        tri128=tri128, tri128t=tri128.T, seg128=_seg_ones(DG, 128), ones128=jnp.ones((128, 128), F32))


def _pad_lanes(v, n):
    return jnp.concatenate([v, jnp.zeros((n - v.shape[0],), v.dtype)])[None, :]


def _layer_params(l, raw, consts):
    p = dict(consts)
    row = lambda name: raw[name][l][None, :]
    z = lambda r: jnp.zeros((r, DG), F32)
    mu_vres = raw["mu_vres"][l - 1] if l > 0 else jnp.zeros((32,), F32)
    p["mu"] = jnp.concatenate([raw["mu_shift"][l], mu_vres, jnp.zeros((96,), F32)])[None, :]
    p["conv_w"], p["conv_b"] = raw["ssd_conv_w"][l], row("ssd_conv_b")
    p["w0"], p["a0"], p["k_k"], p["k_a"] = row("rwkv_w0"), row("rwkv_a0"), row("rwkv_k_k"), row("rwkv_k_a")
    p["lnx_w"], p["lnx_b"] = row("rwkv_lnx_w"), row("rwkv_lnx_b")
    p["r_k"] = raw["rwkv_r_k"][l].reshape(1, DG)
    p["w2p"] = jnp.concatenate([raw["rwkv_w2"][l], z(96)], axis=0)
    p["a2p"] = jnp.concatenate([z(32), raw["rwkv_a2"][l], z(64)], axis=0)
    p["g2p"] = jnp.concatenate([z(64), raw["rwkv_g2"][l]], axis=0)
    if l > 0:
        p["v0"] = raw["rwkv_v0"][l - 1][None, :]
        p["v2p"] = jnp.concatenate([raw["rwkv_v2"][l - 1], z(96)], axis=0)
    p["lb0"], p["lb1"] = raw["lower_bounds"][0:1], raw["lower_bounds"][1:2]
    p["hgrn_norm_w"], p["ssd_norm_w"] = row("hgrn_norm_w"), row("ssd_norm_w")
    p["dt_bias"], p["a_log"], p["ssd_d"] = (_pad_lanes(raw[n][l], 128) for n in ("ssd_dt_bias", "ssd_A_log", "ssd_D"))
    for n in ("ln1_w", "ln1_b", "ln2_w", "ln2_b"):
        p[n] = row(n)
    return p


def _natural_grads(g0, g1):
    gs = (g0, g1)
    st = lambda key, f=lambda a: a[0]: jnp.stack([f(g[key]) for g in gs])
    out = {}
    out["lower_bounds"] = jnp.concatenate([g0["lb0"] + g1["lb0"], g0["lb1"] + g1["lb1"]], axis=0)
    out["mu_shift"] = st("mu", lambda a: a[0, :896])
    out["mu_vres"] = g1["mu"][:, 896:928]
    out["rwkv_w0"], out["rwkv_a0"], out["rwkv_k_k"], out["rwkv_k_a"] = st("w0"), st("a0"), st("k_k"), st("k_a")
    out["rwkv_w2"] = st("w2p", lambda a: a[0:32])
    out["rwkv_a2"] = st("a2p", lambda a: a[32:64])
    out["rwkv_g2"] = st("g2p", lambda a: a[64:128])
    out["rwkv_r_k"] = st("r_k", lambda a: a.reshape(NH, HD))
    out["rwkv_lnx_w"], out["rwkv_lnx_b"] = st("lnx_w"), st("lnx_b")
    out["rwkv_v0"] = g1["v0"]
    out["rwkv_v2"] = g1["v2p"][None, 0:32]
    out["ssd_conv_w"] = st("conv_w", lambda a: a)
    out["ssd_conv_b"] = st("conv_b")
    out["ssd_dt_bias"], out["ssd_A_log"], out["ssd_D"] = (st(k, lambda a: a[0, :NH]) for k in ("dt_bias", "a_log", "ssd_d"))
    out["ssd_norm_w"], out["hgrn_norm_w"] = st("ssd_norm_w"), st("hgrn_norm_w")
    for n in ("ln1_w", "ln1_b", "ln2_w", "ln2_b"):
        out[n] = st(n)
    return out


MESH_T = pl.DeviceIdType.MESH
ANY = pl.BlockSpec(memory_space=pl.ANY)


def _dev_index(px, py, pc):
    return 4 * px + 2 * py + pc


def all_gather(arrs):
    n = len(arrs)

    def body(*refs):
        ins, outs = refs[:n], refs[n:2 * n]
        send_sems, recv_sems, local_sems = refs[2 * n:]
        x, y, c = lax.axis_index("x"), lax.axis_index("y"), lax.axis_index("c")
        me, sibling = (x, y, c), (x, y, 1 - c)
        chips = [(1 - x, y), (x, 1 - y), (1 - x, 1 - y)]

        def copy(a, k, block, to, src=None):
            slot = outs[a].at[_dev_index(*block)]
            return pltpu.make_async_remote_copy(src_ref=slot if src is None else src, dst_ref=slot,
                                                send_sem=send_sems.at[a, k], recv_sem=recv_sems.at[a, k],
                                                device_id=to, device_id_type=MESH_T)

        mine = [pltpu.make_async_copy(ins[a], outs[a].at[_dev_index(*me)], local_sems.at[a]) for a in range(n)]
        for cp in mine:
            cp.start()
        first = []
        for a in range(n):
            first.append(copy(a, 0, me, sibling, src=ins[a]))
            first += [copy(a, 1 + j, me, (*chip, c), src=ins[a]) for j, chip in enumerate(chips)]
        for cp in first:
            cp.start()
        passed = []
        for j, chip in enumerate(chips):
            for a in range(n):
                copy(a, 1 + j, (*chip, c), me).wait_recv()
                fwd = copy(a, 4 + j, (*chip, c), sibling)
                fwd.start()
                passed.append(fwd)
        for a in range(n):
            copy(a, 0, sibling, me).wait_recv()
            for j, chip in enumerate(chips):
                copy(a, 4 + j, (*chip, 1 - c), me).wait_recv()
        for cp in first + passed:
            cp.wait_send()
        for cp in mine:
            cp.wait()

    return pl.pallas_call(
        body, in_specs=[ANY] * n, out_specs=[ANY] * n,
        out_shape=[_sds((N_DEV,) + a.shape, a.dtype) for a in arrs],
        scratch_shapes=[pltpu.SemaphoreType.DMA((n, 7)), pltpu.SemaphoreType.DMA((n, 7)), pltpu.SemaphoreType.DMA((n,))],
        name="all_gather")(*arrs)


def _chips(x, y):
    return [(x, y), (1 - x, y), (x, 1 - y), (1 - x, 1 - y)]


def exchange_siblings(arrs):
    n = len(arrs)

    def body(*refs):
        ins, sib = refs[:n], refs[n:2 * n]
        send_sems, recv_sems = refs[2 * n:]
        x, y, c = lax.axis_index("x"), lax.axis_index("y"), lax.axis_index("c")
        sibling = (x, y, 1 - c)
        sends = []
        for a in range(n):
            for k, (cx, cy) in enumerate(_chips(x, y)):
                sd = pltpu.make_async_remote_copy(src_ref=ins[a].at[_dev_index(cx, cy, 1 - c)], dst_ref=sib[a].at[k],
                                                  send_sem=send_sems.at[a, k], recv_sem=recv_sems.at[a, k],
                                                  device_id=sibling, device_id_type=MESH_T)
                sd.start()
                sends.append(sd)
        for sd in sends:
            sd.wait_recv()
        for sd in sends:
            sd.wait_send()

    sem = pltpu.SemaphoreType.DMA((n, 4))
    return pl.pallas_call(body, in_specs=[ANY] * n, out_specs=[ANY] * n,
                          out_shape=[_sds((4,) + a.shape[1:], a.dtype) for a in arrs],
                          scratch_shapes=[sem, sem], name="exchange_siblings")(*arrs)


def reduce_pair(name, send, slots, sib, wire_dtype):
    _, r, c = send.shape
    rb = min(r, 262144 // c)

    def body(slots_ref, m0, m1, m2, m3, s_ref, own_ref, part_ref):
        own_ref[...] = m0[...] + s_ref[0]
        for k, m_ref in enumerate((m1, m2, m3)):
            part_ref[k] = (m_ref[...] + s_ref[k + 1]).astype(wire_dtype)

    mine = [pl.BlockSpec((None, rb, c), lambda i, s, k=k: (s[k], i, 0)) for k in range(4)]
    grid_spec = pltpu.PrefetchScalarGridSpec(
        num_scalar_prefetch=1, grid=(r // rb,),
        in_specs=mine + [pl.BlockSpec((4, rb, c), lambda i, s: (0, i, 0))],
        out_specs=[pl.BlockSpec((rb, c), lambda i, s: (i, 0)), pl.BlockSpec((3, rb, c), lambda i, s: (0, i, 0))])
    return pl.pallas_call(body, grid_spec=grid_spec, out_shape=[_sds((r, c)), _sds((3, r, c), wire_dtype)], name=name,
                          compiler_params=_cp(("parallel",)))(slots, send, send, send, send, sib)


def exchange_chips(parts, rep):
    n = len(parts)

    def body(*refs):
        ins, rep_ref = refs[:n], refs[n]
        recv, rep_all = refs[n + 1:2 * n + 1], refs[2 * n + 1]
        send_sems, recv_sems, rsend_sems, rrecv_sems, local_sem = refs[2 * n + 2:]
        x, y, c = lax.axis_index("x"), lax.axis_index("y"), lax.axis_index("c")
        me = _dev_index(x, y, c)
        mine = pltpu.make_async_copy(rep_ref, rep_all.at[me], local_sem)
        mine.start()
        cps = []
        for a in range(n):
            for k, (cx, cy) in enumerate(_chips(x, y)[1:]):
                cp = pltpu.make_async_remote_copy(src_ref=ins[a].at[k], dst_ref=recv[a].at[k],
                                                  send_sem=send_sems.at[a, k], recv_sem=recv_sems.at[a, k],
                                                  device_id=(cx, cy, c), device_id_type=MESH_T)
                cp.start()
                cps.append(cp)
        rels = [(rx, ry, rc) for rx in (0, 1) for ry in (0, 1) for rc in (0, 1)][1:]
        peers = [(jnp.where(rx, 1 - x, x), jnp.where(ry, 1 - y, y), jnp.where(rc, 1 - c, c)) for rx, ry, rc in rels]
        rcps = []
        for k, peer in enumerate(peers):
            cp = pltpu.make_async_remote_copy(src_ref=rep_ref, dst_ref=rep_all.at[me], send_sem=rsend_sems.at[k],
                                              recv_sem=rrecv_sems.at[k], device_id=peer, device_id_type=MESH_T)
            cp.start()
            rcps.append(cp)
        for k, peer in enumerate(peers):
            pltpu.make_async_remote_copy(src_ref=rep_ref, dst_ref=rep_all.at[_dev_index(*peer)], send_sem=rsend_sems.at[k],
                                         recv_sem=rrecv_sems.at[k], device_id=peer, device_id_type=MESH_T).wait_recv()
        for cp in cps:
            cp.wait_recv()
        for cp in cps + rcps:
            cp.wait_send()
        mine.wait()

    outs = pl.pallas_call(
        body, in_specs=[ANY] * (n + 1), out_specs=[ANY] * (n + 1),
        out_shape=[_sds(a.shape, a.dtype) for a in parts] + [_sds((N_DEV,) + rep.shape, rep.dtype)],
        scratch_shapes=[pltpu.SemaphoreType.DMA((n, 3)), pltpu.SemaphoreType.DMA((n, 3)), pltpu.SemaphoreType.DMA((7,)),
                        pltpu.SemaphoreType.DMA((7,)), pltpu.SemaphoreType.DMA],
        name="exchange_chips")(*parts, rep)
    return outs[:n], outs[n]


def adamw(name, terms, w, m, v):
    r, c = w.shape
    rb = min(r, 262144 // c)
    c1 = 1.0 - ADAM_B1 ** ADAM_STEP
    c2 = 1.0 - ADAM_B2 ** ADAM_STEP
    nt = len(terms)

    def body(*refs):
        w_ref, m_ref, v_ref = refs[nt:nt + 3]
        g_ref, d_ref, nm_ref, nv_ref = refs[nt + 3:]
        g = refs[0][...].astype(F32)
        for t_ref in refs[1:nt]:
            g = g + t_ref[...].astype(F32)
        nm = ADAM_B1 * m_ref[...] + (1.0 - ADAM_B1) * g
        nv = ADAM_B2 * v_ref[...] + (1.0 - ADAM_B2) * (g * g)
        g_ref[...] = g
        nm_ref[...] = nm
        nv_ref[...] = nv
        d_ref[...] = -ADAM_LR * ((nm / c1) / (jnp.sqrt(nv / c2) + ADAM_EPS) + ADAM_WD * w_ref[...])

    blk = pl.BlockSpec((rb, c), lambda i: (i, 0))
    tspecs = [blk if k is None else pl.BlockSpec((None, rb, c), lambda i, k=k: (k, i, 0)) for _, k in terms]
    return pl.pallas_call(body, grid=(r // rb,), in_specs=tspecs + [blk] * 3, out_specs=[blk] * 4,
                          out_shape=[_sds((r, c))] * 4, name=name,
                          compiler_params=_cp(("parallel",)))(*[t for t, _ in terms], w, m, v)


SMS_ROWS = 16
REP_ROWS = 24
N_BIG = 8
SMALL_SHARDED = (("rwkv_w2", (2, 32, 32)), ("rwkv_a2", (2, 32, 32)), ("rwkv_g2", (2, 64, 32)), ("rwkv_v2", (1, 32, 32)),
                 ("ssd_conv_w", (2, 4, 96)))
REPLICATED = (("lower_bounds", (2, 256)), ("mu_shift", (2, 896)), ("mu_vres", (1, 32)), ("rwkv_w0", (2, 256)),
              ("rwkv_a0", (2, 256)), ("rwkv_k_k", (2, 256)), ("rwkv_k_a", (2, 256)), ("rwkv_r_k", (2, 4, 64)),
              ("rwkv_lnx_w", (2, 256)), ("rwkv_lnx_b", (2, 256)), ("rwkv_v0", (1, 256)), ("ssd_conv_b", (2, 768)),
              ("ssd_dt_bias", (2, 4)), ("ssd_A_log", (2, 4)), ("ssd_D", (2, 4)), ("ssd_norm_w", (2, 256)),
              ("hgrn_norm_w", (2, 256)), ("ln1_w", (2, 1024)), ("ln1_b", (2, 1024)), ("ln2_w", (2, 1024)),
              ("ln2_b", (2, 1024)))


def _flat_rows(parts, rows):
    flat = jnp.concatenate([a.reshape(-1) for a in parts])
    return jnp.concatenate([flat, jnp.zeros((rows * PACK_W - flat.shape[0],), flat.dtype)]).reshape(rows, PACK_W)


def _local_arrays(d):
    arrs = [_w_in_pad(d["w_in"][0], None), _w_in_pad(d["w_in"][1], d["w_in_vres"][0]), d["w_out"][0], d["w_out"][1],
            d["w_up"][0].T, d["w_up"][1].T, d["w_down"][0], d["w_down"][1],
            _flat_rows([d[n] for n, _ in SMALL_SHARDED], SMS_ROWS)]
    return arrs, _flat_rows([d[n] for n, _ in REPLICATED], REP_ROWS)


def _unflat(rows2d, table):
    flat, out, o = rows2d.reshape(-1), {}, 0
    for name, shape in table:
        n = 1
        for s in shape:
            n *= s
        out[name] = flat[o:o + n].reshape(shape)
        o += n
    return out


def _from_local_arrays(arrs, rep):
    d = {}
    g0, _ = _w_in_unpad(arrs[0])
    g1, gv = _w_in_unpad(arrs[1])
    d["w_in"], d["w_in_vres"] = jnp.stack([g0, g1]), gv[None]
    d["w_out"] = jnp.stack([arrs[2], arrs[3]])
    d["w_up"] = jnp.stack([arrs[4].T, arrs[5].T])
    d["w_down"] = jnp.stack([arrs[6], arrs[7]])
    d.update(_unflat(arrs[8], SMALL_SHARDED))
    d.update(_unflat(rep, REPLICATED))
    return d


def _gathered_weights(gathered):
    full = [g.reshape(N_DEV * g.shape[1], g.shape[2]) for g in gathered[:N_BIG]]
    wts = [dict(w_in=full[l], w_out=full[2 + l], w_up_t=full[4 + l], w_down=full[6 + l]) for l in range(DEPTH)]
    small, flat, o = {}, gathered[N_BIG].reshape(N_DEV, -1), 0
    for name, shape in SMALL_SHARDED:
        n = shape[0] * shape[1] * shape[2]
        blk = flat[:, o:o + n].reshape((N_DEV,) + shape)
        small[name] = blk.transpose(1, 2, 0, 3).reshape(shape[0], shape[1], N_DEV * shape[2])
        o += n
    return wts, small


def _send_arrays(big, small_grads):
    blocks = lambda g: g.reshape(N_DEV, g.shape[0] // N_DEV, g.shape[1])
    arrs = [blocks(big[l][k]) for k in ("w_in", "w_out", "w_up_t", "w_down") for l in range(DEPTH)]
    sms = []
    for name, shape in SMALL_SHARDED:
        g = small_grads[name].reshape(shape[0], shape[1], N_DEV, shape[2]).transpose(2, 0, 1, 3)
        sms.append(g.reshape(N_DEV, -1))
    sms = jnp.concatenate(sms, axis=1)
    sms = jnp.concatenate([sms, jnp.zeros((N_DEV, SMS_ROWS * PACK_W - sms.shape[1]), F32)], axis=1)
    arrs.append(sms.reshape(N_DEV, SMS_ROWS, PACK_W))
    return arrs, _flat_rows([small_grads[n] for n, _ in REPLICATED], REP_ROWS)


def _local_step(x, tgt, wts, raw):
    consts = _consts()
    ps = [_layer_params(l, raw, consts) for l in range(DEPTH)]
    x1, sv0 = layer_fwd(0, x, None, wts[0], ps[0])
    x2, sv1 = layer_fwd(1, x1, sv0["fl"], wts[1], ps[1])
    dy, lparts = loss_call(x2, tgt)
    loss = jnp.sum(lparts[::8, 0])
    dx1, dvfirst, g1 = layer_bwd(1, dy, None, sv1, wts[1], ps[1])
    dx0, _, g0 = layer_bwd(0, dx1, dvfirst, sv0, wts[0], ps[0])
    big = [{k: g[k] for k in ("w_in", "w_out", "w_up_t", "w_down")} for g in (g0, g1)]
    return loss, dx0, big, _natural_grads(g0, g1)


WEIGHT_NAMES = ("lower_bounds", "w_in", "w_in_vres", "mu_shift", "mu_vres", "rwkv_w0", "rwkv_w2", "rwkv_a0", "rwkv_a2",
                "rwkv_g2", "rwkv_k_k", "rwkv_k_a", "rwkv_r_k", "rwkv_lnx_w", "rwkv_lnx_b", "rwkv_v0", "rwkv_v2",
                "ssd_conv_w", "ssd_conv_b", "ssd_dt_bias", "ssd_A_log", "ssd_D", "ssd_norm_w", "hgrn_norm_w", "w_out",
                "ln1_w", "ln1_b", "w_up", "w_down", "ln2_w", "ln2_b")


def kernel(x, lower_bounds, w_in, w_in_vres, mu_shift, mu_vres, rwkv_w0, rwkv_w2, rwkv_a0, rwkv_a2, rwkv_g2, rwkv_k_k, rwkv_k_a, rwkv_r_k, rwkv_lnx_w, rwkv_lnx_b, rwkv_v0, rwkv_v2, ssd_conv_w, ssd_conv_b, ssd_dt_bias, ssd_A_log, ssd_D, ssd_norm_w, hgrn_norm_w, w_out, ln1_w, ln1_b, w_up, w_down, ln2_w, ln2_b, loss_target, m_lower_bounds, m_w_in, m_w_in_vres, m_mu_shift, m_mu_vres, m_rwkv_w0, m_rwkv_w2, m_rwkv_a0, m_rwkv_a2, m_rwkv_g2, m_rwkv_k_k, m_rwkv_k_a, m_rwkv_r_k, m_rwkv_lnx_w, m_rwkv_lnx_b, m_rwkv_v0, m_rwkv_v2, m_ssd_conv_w, m_ssd_conv_b, m_ssd_dt_bias, m_ssd_A_log, m_ssd_D, m_ssd_norm_w, m_hgrn_norm_w, m_w_out, m_ln1_w, m_ln1_b, m_w_up, m_w_down, m_ln2_w, m_ln2_b, v_lower_bounds, v_w_in, v_w_in_vres, v_mu_shift, v_mu_vres, v_rwkv_w0, v_rwkv_w2, v_rwkv_a0, v_rwkv_a2, v_rwkv_g2, v_rwkv_k_k, v_rwkv_k_a, v_rwkv_r_k, v_rwkv_lnx_w, v_rwkv_lnx_b, v_rwkv_v0, v_rwkv_v2, v_ssd_conv_w, v_ssd_conv_b, v_ssd_dt_bias, v_ssd_A_log, v_ssd_D, v_ssd_norm_w, v_hgrn_norm_w, v_w_out, v_ln1_w, v_ln1_b, v_w_up, v_w_down, v_ln2_w, v_ln2_b):
    given = dict(locals())
    w = {n: given[n] for n in WEIGHT_NAMES}
    w_arrs, w_rep = _local_arrays(w)
    m_arrs, m_rep = _local_arrays({n: given["m_" + n] for n in WEIGHT_NAMES})
    v_arrs, v_rep = _local_arrays({n: given["v_" + n] for n in WEIGHT_NAMES})
    gathered = all_gather([a.astype(BF16) for a in w_arrs[:N_BIG]] + [w_arrs[N_BIG]])
    wts, small_full = _gathered_weights(gathered)
    raw = {n: w[n] for n, _ in REPLICATED}
    raw.update(small_full)
    loss, dx, big, small_grads = _local_step(x[0], loss_target[0], wts, raw)
    send, rep = _send_arrays(big, small_grads)
    sib = exchange_siblings(send)
    mx, my, mc = lax.axis_index("x"), lax.axis_index("y"), lax.axis_index("c")
    slots = jnp.stack([_dev_index(cx, cy, mc) for cx, cy in _chips(mx, my)]).astype(jnp.int32)
    own, parts = [], []
    for a in range(N_BIG + 1):
        o, pt = reduce_pair(f"reduce_pair{a}", send[a], slots, sib[a], BF16 if a < N_BIG else F32)
        own.append(o)
        parts.append(pt)
    recv, rep_all = exchange_chips(parts, rep)
    results = [adamw(f"adamw{a}", [(own[a], None), (recv[a], 0), (recv[a], 1), (recv[a], 2)], w_arrs[a], m_arrs[a], v_arrs[a])
               for a in range(N_BIG + 1)]
    rep_res = adamw("adamw_rep", [(rep_all, q) for q in range(N_DEV)], w_rep, m_rep, v_rep)
    loss = lax.psum(loss, ("x", "y", "c"))
    outs = [loss, dx[None]]
    for q in range(4):
        d = _from_local_arrays([res[q] for res in results], rep_res[q])
        outs += [d[n] for n in WEIGHT_NAMES]
    return tuple(outs)
```

```python
import functools

import jax
import jax.numpy as jnp
from jax import lax
from jax.experimental import pallas as pl
from jax.experimental.pallas import tpu as pltpu

F32 = jnp.float32
BF16 = jnp.bfloat16
HI = lax.Precision.HIGHEST

N_DEV = 8
SEQ = 2048
D_MODEL = 1024
D_FF = 4096
DG = 256
NH = 4
HD = 64
DEPTH = 2
ALPHA = (2.0 * DEPTH) ** 0.25
LN_EPS = 1e-5
RMS_EPS = 1e-5
GN_EPS = HD * 1e-5
IN_COLS = 3716
SSD_N = 128
SSD_CHUNK = 128
HGRN_CHUNK = 16
DILATED = ((128, 1), (512, 4), (2048, 16))

ADAM_LR, ADAM_B1, ADAM_B2, ADAM_EPS, ADAM_WD, ADAM_STEP = 0.001, 0.9, 0.999, 1e-08, 0.01, 10

PW = 4096
C_R, C_K, C_V = 0, 256, 512
C_AQ, C_AK, C_AV = 768, 1024, 1280
C_Z, C_XBC = 1536, 1792
C_HQ, C_HF, C_HI, C_HG = 2560, 2816, 3072, 3328
C_LORA, C_DT, C_VRES = 3584, 3712, 3840

RB = 256
VMEM_LIMIT = 56 * 1024 * 1024
PACK_W = 1024


def _cp(sem=None):
    return pltpu.CompilerParams(dimension_semantics=sem, vmem_limit_bytes=VMEM_LIMIT)


def _sds(shape, dt=F32):
    return jax.ShapeDtypeStruct(tuple(shape), dt)


def _rows(w, cb=0, rb=RB):
    return pl.BlockSpec((rb, w), lambda i: (i, cb))


def _full(shape):
    n = len(shape)
    return pl.BlockSpec(tuple(shape), lambda *_: (0,) * n)


def _sigmoid(x):
    return 1.0 / (1.0 + jnp.exp(-x))


def _silu(x):
    return x * _sigmoid(x)


def _softplus(x):
    return jnp.maximum(x, 0.0) + jnp.log(1.0 + jnp.exp(jnp.where(x > 0, -x, x)))


MID = lax.Precision.HIGH
NN, TN, NT = (((1,), (0,)), ((), ())), (((0,), (0,)), ((), ())), (((1,), (1,)), ((), ()))


def _dot(a, b):
    return lax.dot_general(a, b, NN, precision=MID, preferred_element_type=F32)


def _dot_tn(a, b):
    return lax.dot_general(a, b, TN, precision=MID, preferred_element_type=F32)


def _dot_nt(a, b):
    return lax.dot_general(a, b, NT, precision=MID, preferred_element_type=F32)


def _dotx(a, b):
    return lax.dot_general(a, b, NN, precision=HI, preferred_element_type=F32)


def _dotx_tn(a, b):
    return lax.dot_general(a, b, TN, precision=HI, preferred_element_type=F32)


def _seg_ones(n, seg):
    i = jnp.arange(n)
    return (i[:, None] // seg == i[None, :] // seg).astype(F32)


def _shift_down(x, s):
    row = lax.broadcasted_iota(jnp.int32, x.shape, 0)
    return jnp.where(row < s, 0.0, pltpu.roll(x, s, 0))


def _shift_up(x, s):
    n = x.shape[0]
    row = lax.broadcasted_iota(jnp.int32, x.shape, 0)
    return jnp.where(row >= n - s, 0.0, pltpu.roll(x, n - s, 0))


@functools.partial(jax.custom_vjp, nondiff_argnums=(1,))
def _tshift(x, s):
    return _shift_down(x, s)


def _tshift_fwd(x, s):
    return _shift_down(x, s), None


def _tshift_bwd(s, _, g):
    return (_shift_up(g, s),)


_tshift.defvjp(_tshift_fwd, _tshift_bwd)


def _map_fwd(name, fn, grid, ins, in_specs, out_shapes, out_specs):
    n_in = len(ins)

    def body(*refs):
        ys = fn(*[r[...] for r in refs[:n_in]])
        for r, y in zip(refs[n_in:], ys):
            r[...] = y

    return pl.pallas_call(body, grid=grid, in_specs=in_specs, out_specs=out_specs, out_shape=out_shapes,
                          name=name, compiler_params=_cp(("parallel",)))(*ins)


def _map_bwd(name, fn, grid, ins, in_specs, cts, ct_specs, want, acc=(), gout=None):
    n_in = len(ins)
    flat_cts = [c for group in cts for c in group]
    flat_specs = [s for group in ct_specs for s in group]
    n_ct = len(flat_cts)
    gout = gout or {}
    out_shapes = [gout[i][0] if i in gout else _sds(ins[i].shape) for i in want]
    out_specs = [gout[i][1] if i in gout else in_specs[i] for i in want]

    def body(*refs):
        xs = [r[...] for r in refs[:n_in]]
        cvals = [r[...] for r in refs[n_in:n_in + n_ct]]
        gouts = refs[n_in + n_ct:]
        cs, p = [], 0
        for group in cts:
            v = cvals[p]
            for q in range(1, len(group)):
                v = v + cvals[p + q]
            cs.append(v)
            p += len(group)

        def f(*wanted):
            full = list(xs)
            for i, w in zip(want, wanted):
                full[i] = w
            return tuple(fn(*full))

        _, vjp = jax.vjp(f, *[xs[i] for i in want])
        gs = vjp(tuple(cs))
        for o, i, g in zip(gouts, want, gs):
            if i in acc:
                @pl.when(pl.program_id(0) == 0)
                def _():
                    o[...] = jnp.zeros_like(o)

                o[...] += g
            else:
                o[...] = g

    sem = ("arbitrary",) if acc else ("parallel",)
    return pl.pallas_call(body, grid=grid, in_specs=list(in_specs) + flat_specs, out_specs=out_specs,
                          out_shape=out_shapes, name=name, compiler_params=_cp(sem))(*ins, *flat_cts)


def _addn(name, *arrs):
    n, c = arrs[0].shape

    def fn(*xs):
        r = xs[0]
        for x in xs[1:]:
            r = r + x
        return (r,)

    return _map_fwd(name, fn, (n // RB,), list(arrs), [_rows(c)] * len(arrs), [_sds((n, c))], [_rows(c)])[0]


def _mm(name, a, b, mode, tm, tn, tk, add=None, add_scale=1.0, epilogue=None):
    if mode == "nn":
        (m, k), n = a.shape, b.shape[1]
    elif mode == "nt":
        (m, k), n = a.shape, b.shape[0]
    else:
        (k, m), n = a.shape, b.shape[1]
    nk = k // tk
    dn = {"nn": (((1,), (0,)), ((), ())), "nt": (((1,), (1,)), ((), ())), "tn": (((0,), (0,)), ((), ()))}[mode]

    def body(*refs):
        if add is None:
            a_ref, b_ref, o_ref, acc = refs
        else:
            a_ref, b_ref, add_ref, o_ref, acc = refs
        kk = pl.program_id(2)

        @pl.when(kk == 0)
        def _():
            acc[...] = jnp.zeros_like(acc)

        acc[...] += lax.dot_general(a_ref[...].astype(BF16), b_ref[...].astype(BF16), dn, preferred_element_type=F32)

        @pl.when(kk == nk - 1)
        def _():
            r = acc[...]
            if epilogue == "relu2":
                r = jnp.maximum(r, 0.0)
                r = r * r
            elif epilogue == "relu2_bwd":
                r = r * (2.0 * jnp.sqrt(add_ref[...]))
            elif add is not None:
                r = r + add_scale * add_ref[...]
            o_ref[...] = r

    a_spec = pl.BlockSpec((tk, tm), lambda i, j, q: (q, i)) if mode == "tn" else pl.BlockSpec((tm, tk), lambda i, j, q: (i, q))
    b_spec = pl.BlockSpec((tn, tk), lambda i, j, q: (j, q)) if mode == "nt" else pl.BlockSpec((tk, tn), lambda i, j, q: (q, j))
    o_spec = pl.BlockSpec((tm, tn), lambda i, j, q: (i, j))
    ins, specs = [a, b], [a_spec, b_spec]
    if add is not None:
        ins.append(add)
        specs.append(o_spec)
    return pl.pallas_call(body, grid=(m // tm, n // tn, nk), in_specs=specs, out_specs=o_spec, out_shape=_sds((m, n)),
                          scratch_shapes=[pltpu.VMEM((tm, tn), F32)], name=name,
                          compiler_params=_cp(("parallel", "parallel", "arbitrary")))(*ins)


LERP_BLOCKS = (0, 1, 2, 3, 4, 5, C_LORA // 128, C_VRES // 128)


def _lerp_colmap(j):
    r = jnp.where(j < 6, j, jnp.where(j == 6, C_LORA // 128, C_VRES // 128))
    return (0, r)


def _lerp_fn(f, mu):
    return (f + (_tshift(f, 1) - f) * mu,)


def _lerp_specs():
    return [pl.BlockSpec((SEQ, 128), _lerp_colmap), pl.BlockSpec((1, 128), lambda j: (0, j))]


def lerp_fwd(l, proj, mu):
    return _map_fwd(f"lerp_fwd{l}", _lerp_fn, (8,), [proj, mu], _lerp_specs(), [_sds((SEQ, 1024))],
                    [pl.BlockSpec((SEQ, 128), lambda j: (0, j))])[0]


def lerp_bwd(l, proj, mu, dfl):
    n_in = 2

    def body(f_ref, mu_ref, g_ref, df_ref, dmu_ref):
        _, vjp = jax.vjp(_lerp_fn, f_ref[...], mu_ref[...])
        df, dmu = vjp((g_ref[...],))
        df_ref[...] = df
        dmu_ref[...] = dmu

    cspec = pl.BlockSpec((SEQ, 128), lambda j: (0, j))
    return pl.pallas_call(body, grid=(8,), in_specs=_lerp_specs() + [cspec],
                          out_specs=[cspec, pl.BlockSpec((1, 128), lambda j: (0, j))],
                          out_shape=[_sds((SEQ, 1024)), _sds((1, 1024))], name=f"lerp_bwd{l}",
                          compiler_params=_cp(("parallel",)))(proj, mu, dfl)


def _conv_fn(x, w, b):
    y = x * w[3:4, :] + _tshift(x, 1) * w[2:3, :] + _tshift(x, 2) * w[1:2, :] + _tshift(x, 3) * w[0:1, :] + b
    return (_silu(y),)


def _conv_specs():
    return [pl.BlockSpec((SEQ, 128), lambda j: (0, C_XBC // 128 + j)), pl.BlockSpec((4, 128), lambda j: (0, j)),
            pl.BlockSpec((1, 128), lambda j: (0, j))]


def conv_fwd(l, proj, w, b):
    return _map_fwd(f"conv_fwd{l}", _conv_fn, (6,), [proj, w, b], _conv_specs(), [_sds((SEQ, 768))],
                    [pl.BlockSpec((SEQ, 128), lambda j: (0, j))])[0]


def conv_bwd(l, proj, w, b, dxc):
    def body(x_ref, w_ref, b_ref, g_ref, dx_ref, dw_ref, db_ref):
        _, vjp = jax.vjp(_conv_fn, x_ref[...], w_ref[...], b_ref[...])
        dx, dw, db = vjp((g_ref[...],))
        dx_ref[...] = dx
        dw_ref[...] = dw
        db_ref[...] = db

    cspec = pl.BlockSpec((SEQ, 128), lambda j: (0, j))
    return pl.pallas_call(body, grid=(6,), in_specs=_conv_specs() + [cspec],
                          out_specs=[cspec, pl.BlockSpec((4, 128), lambda j: (0, j)), pl.BlockSpec((1, 128), lambda j: (0, j))],
                          out_shape=[_sds((SEQ, 768)), _sds((4, 768)), _sds((1, 768))], name=f"conv_bwd{l}",
                          compiler_params=_cp(("parallel",)))(proj, w, b, dxc)


def _rwkv_pre_fn(has_vres):
    def fn(fk, fv, flora, *rest):
        if has_vres:
            fvres, vfirst, w0, w2p, a0, a2p, g2p, k_k, k_a, v0, v2p, seg = rest
        else:
            w0, w2p, a0, a2p, g2p, k_k, k_a, seg = rest
        w_log = -_softplus(-(w0 + _dot(jnp.tanh(flora), w2p))) - 0.5
        w = jnp.exp(-jnp.exp(w_log))
        a = _sigmoid(a0 + _dot(flora, a2p))
        g = _dot(_sigmoid(flora), g2p)
        if has_vres:
            v2 = fv + (vfirst - fv) * _sigmoid(v0 + _dot(fvres, v2p))
        else:
            v2 = fv * 1.0
        kk = fk * k_k
        kk = kk / jnp.maximum(jnp.sqrt(_dot(kk * kk, seg)), 1e-12)
        k2 = fk * (1.0 + (a - 1.0) * k_a)
        return w, k2, v2, -kk, kk * a, g

    return fn


def _rwkv_pre_args(fl, vfirst, p, has_vres):
    ins = [fl, fl, fl]
    specs = [_rows(256, 1), _rows(256, 2), _rows(128, 6)]
    if has_vres:
        ins += [fl, vfirst]
        specs += [_rows(128, 7), _rows(256, 2)]
    names = ["w0", "w2p", "a0", "a2p", "g2p", "k_k", "k_a"] + (["v0", "v2p"] if has_vres else []) + ["seg64"]
    for nme in names:
        ins.append(p[nme])
        specs.append(_full(p[nme].shape))
    return ins, specs, names


def rwkv_pre_fwd(l, fl, vfirst, p):
    has_vres = l > 0
    ins, specs, _ = _rwkv_pre_args(fl, vfirst, p, has_vres)
    return _map_fwd(f"rwkv_pre_fwd{l}", _rwkv_pre_fn(has_vres), (SEQ // RB,), ins, specs,
                    [_sds((SEQ, DG))] * 6, [_rows(DG)] * 6)


def rwkv_pre_bwd(l, fl, vfirst, p, cts):
    has_vres = l > 0
    ins, specs, names = _rwkv_pre_args(fl, vfirst, p, has_vres)
    n_row = 5 if has_vres else 3
    want = list(range(n_row)) + [n_row + i for i, nme in enumerate(names) if nme != "seg64"]
    acc = tuple(w for w in want if w >= n_row)
    ct_specs = [[_rows(DG)] * len(g) for g in cts]
    gout = {0: (_sds((SEQ, DG)), _rows(DG)), 1: (_sds((SEQ, DG)), _rows(DG)), 2: (_sds((SEQ, 128)), _rows(128))}
    if has_vres:
        gout[3] = (_sds((SEQ, 128)), _rows(128))
        gout[4] = (_sds((SEQ, DG)), _rows(DG))
    gs = _map_bwd(f"rwkv_pre_bwd{l}", _rwkv_pre_fn(has_vres), (SEQ // RB,), ins, specs, cts, ct_specs, want, acc, gout)
    keys = ["fk", "fv", "flora"] + (["fvres", "vfirst"] if has_vres else []) + [nme for nme in names if nme != "seg64"]
    return dict(zip(keys, gs))


def _rwkv_post_fn(y, fr, k2, v2, g, lnx_w, lnx_b, r_k, seg):
    mu = _dot(y, seg) * (1.0 / HD)
    d = y - mu
    var = _dot(d * d, seg) * (1.0 / HD)
    yn = d * lax.rsqrt(var + GN_EPS) * lnx_w + lnx_b
    bonus = _dot(fr * k2 * r_k, seg) * v2
    return ((yn + bonus) * g,)


def _rwkv_post_args(y, fl, k2, v2, g, p):
    ins = [y, fl, k2, v2, g, p["lnx_w"], p["lnx_b"], p["r_k"], p["seg64"]]
    specs = [_rows(DG), _rows(DG, 0), _rows(DG), _rows(DG), _rows(DG)] + [_full(x.shape) for x in ins[5:]]
    return ins, specs


def rwkv_post_fwd(l, y, fl, k2, v2, g, p):
    ins, specs = _rwkv_post_args(y, fl, k2, v2, g, p)
    return _map_fwd(f"rwkv_post_fwd{l}", _rwkv_post_fn, (SEQ // RB,), ins, specs, [_sds((SEQ, DG))], [_rows(DG)])[0]


def rwkv_post_bwd(l, y, fl, k2, v2, g, p, dya):
    ins, specs = _rwkv_post_args(y, fl, k2, v2, g, p)
    gs = _map_bwd(f"rwkv_post_bwd{l}", _rwkv_post_fn, (SEQ // RB,), ins, specs, [[dya]], [[_rows(DG)]],
                  want=[0, 1, 2, 3, 4, 5, 6, 7], acc=(5, 6, 7), gout={1: (_sds((SEQ, DG)), _rows(DG))})
    return dict(zip(["y", "fr", "k2", "v2", "g", "lnx_w", "lnx_b", "r_k"], gs))


SCAN_TB = 64


def _coltile8(rows8, dmask, ones_stack, parts):
    pieces, rest = [], rows8
    for q in range(parts):
        piece = rest.astype(BF16).astype(F32)
        if q < parts - 1:
            rest = rest - piece
        pieces.append((piece[:, None, :] * dmask[None]).reshape(8 * HD, DG).astype(BF16))
    x = pieces[0] if parts == 1 else jnp.concatenate(pieces, axis=1)
    return jnp.dot(x, ones_stack, preferred_element_type=F32).reshape(8, HD, DG)


def _coltiles_bf16(rows_list, dmask, ones_bf16):
    x = jnp.concatenate([(r8[:, None, :] * dmask[None]).reshape(8 * HD, DG).astype(BF16) for r8 in rows_list], axis=0)
    t = jnp.dot(x, ones_bf16, preferred_element_type=F32)
    return [t[q * 8 * HD:(q + 1) * 8 * HD].reshape(8, HD, DG) for q in range(len(rows_list))]


def _segrows8(x8, dmask, ones_bf16):
    t = jnp.dot(x8.reshape(8 * HD, DG).astype(BF16), ones_bf16, preferred_element_type=F32).reshape(8, HD, DG)
    return jnp.sum(t * dmask[None], axis=1)


def rwkv_scan_fwd(l, fl, w, k2, v2, c, b, p):
    nblk = SEQ // SCAN_TB

    def body(r_ref, w_ref, k_ref, v_ref, c_ref, b_ref, ones_ref, dm_ref, y_ref, st_ref, s_sc):
        @pl.when(pl.program_id(0) == 0)
        def _():
            s_sc[...] = jnp.zeros_like(s_sc)

        ones3, ones = ones_ref[...], ones_ref[0:DG, :]
        dmask = dm_ref[...]

        def group(gi, carry):
            t0 = pl.multiple_of(gi * 8, 8)
            sl = pl.ds(t0, 8)
            v8 = v_ref[sl, :]
            wt = _coltile8(w_ref[sl, :], dmask, ones3, 3)
            ct, bt, kt, rt = _coltiles_bf16([c_ref[sl, :], b_ref[sl, :], k_ref[sl, :], r_ref[sl, :]], dmask, ones)
            t = s_sc[...]
            for j in range(8):
                sa = jnp.sum(t * ct[j], axis=0, keepdims=True)
                t = t * wt[j] + bt[j] * sa + kt[j] * v8[j:j + 1, :]
                st_ref[t0 + j] = t
            s_sc[...] = t
            y_ref[sl, :] = jnp.sum(st_ref[sl] * rt, axis=1)
            return carry

        lax.fori_loop(0, SCAN_TB // 8, group, 0)

    row = pl.BlockSpec((SCAN_TB, DG), lambda i: (i, 0))
    ins = [fl, w, k2, v2, c, b, p["seg64x3_bf16"], p["dmask"]]
    specs = [row] * 6 + [_full((3 * DG, DG)), _full((HD, DG))]
    return pl.pallas_call(body, grid=(nblk,), in_specs=specs,
                          out_specs=[row, pl.BlockSpec((SCAN_TB, HD, DG), lambda i: (i, 0, 0))],
                          out_shape=[_sds((SEQ, DG)), _sds((SEQ, HD, DG))],
                          scratch_shapes=[pltpu.VMEM((HD, DG), F32)], name=f"rwkv_scan_fwd{l}",
                          compiler_params=_cp(("arbitrary",)))(*ins)


def rwkv_scan_bwd(l, fl, w, k2, v2, c, b, states, dy, p):
    nblk = SEQ // SCAN_TB

    def body(r_ref, w_ref, k_ref, v_ref, c_ref, b_ref, dy_ref, st_ref, sp_ref, ones_ref, dm_ref,
             dr_ref, dw_ref, dk_ref, dv_ref, dc_ref, db_ref, g_sc, prev_sc, d8_sc, dsa_sc):
        i = pl.program_id(0)

        @pl.when(i == 0)
        def _():
            g_sc[...] = jnp.zeros_like(g_sc)

        ones3, ones = ones_ref[...], ones_ref[0:DG, :]
        dmask = dm_ref[...]
        first_block = i == nblk - 1

        def group(gr, carry):
            gi = SCAN_TB // 8 - 1 - gr
            t0 = pl.multiple_of(gi * 8, 8)
            sl = pl.ds(t0, 8)
            v8, dy8 = v_ref[sl, :], dy_ref[sl, :]
            t8 = st_ref[sl]
            @pl.when(gi > 0)
            def _():
                prev_sc[0] = st_ref[t0 - 1]

            @pl.when(gi == 0)
            def _():
                prev_sc[0] = jnp.where(first_block, 0.0, sp_ref[0])

            for j in range(1, 8):
                prev_sc[j] = t8[j - 1]
            tp8 = prev_sc[...]
            wt = _coltile8(w_ref[sl, :], dmask, ones3, 3)
            ct, bt, kt, rt = _coltiles_bf16([c_ref[sl, :], b_ref[sl, :], k_ref[sl, :], r_ref[sl, :]], dmask, ones)
            sa8 = jnp.sum(tp8 * ct, axis=1)
            g = g_sc[...]
            for j in range(7, -1, -1):
                g = g + rt[j] * dy8[j:j + 1, :]
                d8_sc[j] = g
                dsa = jnp.sum(g * bt[j], axis=0, keepdims=True)
                dsa_sc[j:j + 1, :] = dsa
                g = g * wt[j] + ct[j] * dsa
            g_sc[...] = g
            d8 = d8_sc[...]
            dsa8 = dsa_sc[...]
            dv_ref[sl, :] = jnp.sum(d8 * kt, axis=1)
            dr_ref[sl, :] = _segrows8(t8 * dy8[:, None, :], dmask, ones)
            dk_ref[sl, :] = _segrows8(d8 * v8[:, None, :], dmask, ones)
            dw_ref[sl, :] = _segrows8(tp8 * d8, dmask, ones)
            db_ref[sl, :] = _segrows8(d8 * sa8[:, None, :], dmask, ones)
            dc_ref[sl, :] = _segrows8(tp8 * dsa8[:, None, :], dmask, ones)
            return carry

        lax.fori_loop(0, SCAN_TB // 8, group, 0)

    row = pl.BlockSpec((SCAN_TB, DG), lambda i: (nblk - 1 - i, 0))
    st_spec = pl.BlockSpec((SCAN_TB, HD, DG), lambda i: (nblk - 1 - i, 0, 0))
    sp_spec = pl.BlockSpec((1, HD, DG), lambda i: (jnp.maximum((nblk - 1 - i) * SCAN_TB - 1, 0), 0, 0))
    ins = [fl, w, k2, v2, c, b, dy, states, states, p["seg64x3_bf16"], p["dmask"]]
    specs = [row] * 7 + [st_spec, sp_spec, _full((3 * DG, DG)), _full((HD, DG))]
    tile8 = pltpu.VMEM((8, HD, DG), F32)
    return pl.pallas_call(body, grid=(nblk,), in_specs=specs, out_specs=[row] * 6, out_shape=[_sds((SEQ, DG))] * 6,
                          scratch_shapes=[pltpu.VMEM((HD, DG), F32), tile8, tile8, pltpu.VMEM((8, DG), F32)],
                          name=f"rwkv_scan_bwd{l}", compiler_params=_cp(("arbitrary",)))(*ins)


HG_ROWS = 128


def _hgrn_chunk_fn(layer):
    def fn(hq, hf, hi, hg, sprev, lb0, lb1, norm_w, seg, bd, tri, causal, ones16):
        e0 = jnp.exp(lb0 - jnp.maximum(lb0, lb1))
        e1 = jnp.exp(lb1 - jnp.maximum(lb0, lb1))
        sm0, sm1 = e0 / (e0 + e1), e1 / (e0 + e1)
        lb = (sm0 - sm0) if layer == 0 else ((sm0 + sm1) - sm0)
        forget = lb + (1.0 - lb) * _sigmoid(hf)
        logf = jnp.log(forget)
        kk = 1.0 - forget
        q = _silu(hq)
        c = HGRN_CHUNK
        b = _dotx(tri, logf)
        bl = jnp.sum(logf, axis=0, keepdims=True)
        diff = (b[:, None, :] - b[None, :, :]).reshape(c * c, DG)
        dec = jnp.exp(jnp.where(causal > 0.5, diff, -1e30))
        qrep = jnp.broadcast_to(q[:, None, :], (c, c, DG)).reshape(c * c, DG)
        ktil = jnp.broadcast_to(kk[None, :, :], (c, c, DG)).reshape(c * c, DG)
        vtil = jnp.broadcast_to(hi[None, :, :], (c, c, DG)).reshape(c * c, DG)
        att = _dot(qrep * ktil * dec, seg)
        o_intra = jnp.sum((att * vtil).reshape(c, c, DG), axis=1)
        kdec = kk * jnp.exp(bl - b)
        u = _dot_tn(hi, kdec) * bd
        snext = sprev * jnp.exp(bl) + u
        o = o_intra + _dot_nt(q * jnp.exp(b), sprev)
        ms = _dot(o * o, seg) * (1.0 / HD)
        y = o * lax.rsqrt(ms + RMS_EPS) * norm_w * _silu(hg)
        return y, snext

    return fn


def _hgrn_consts(p):
    return [p["seg64"], p["seg64"], p["tri16"], p["causal16"], p["ones16"]]


def hgrn_fwd(l, proj, p):
    fn = _hgrn_chunk_fn(l)
    nch = HG_ROWS // HGRN_CHUNK

    def body(hq_ref, hf_ref, hi_ref, hg_ref, lb0_ref, lb1_ref, nw_ref, seg_ref, bd_ref, tri_ref, cau_ref, o16_ref,
             y_ref, st_ref, s_sc):
        @pl.when(pl.program_id(0) == 0)
        def _():
            s_sc[...] = jnp.zeros_like(s_sc)

        consts = (lb0_ref[...], lb1_ref[...], nw_ref[...], seg_ref[...], bd_ref[...], tri_ref[...], cau_ref[...],
                  o16_ref[...])

        def chunk(ci, carry):
            sl = pl.ds(pl.multiple_of(ci * HGRN_CHUNK, HGRN_CHUNK), HGRN_CHUNK)
            sprev = s_sc[...]
            st_ref[ci] = sprev
            y, snext = fn(hq_ref[sl, :], hf_ref[sl, :], hi_ref[sl, :], hg_ref[sl, :], sprev, *consts)
            y_ref[sl, :] = y
            s_sc[...] = snext
            return carry

        lax.fori_loop(0, nch, chunk, 0)

    rows = lambda cb: pl.BlockSpec((HG_ROWS, DG), lambda i: (i, cb))
    ins = [proj, proj, proj, proj, p["lb0"], p["lb1"], p["hgrn_norm_w"]] + _hgrn_consts(p)
    specs = [rows(C_HQ // DG), rows(C_HF // DG), rows(C_HI // DG), rows(C_HG // DG)] + [_full(x.shape) for x in ins[4:]]
    return pl.pallas_call(body, grid=(SEQ // HG_ROWS,), in_specs=specs,
                          out_specs=[rows(0), pl.BlockSpec((nch, DG, DG), lambda i: (i, 0, 0))],
                          out_shape=[_sds((SEQ, DG)), _sds((SEQ // HGRN_CHUNK, DG, DG))],
                          scratch_shapes=[pltpu.VMEM((DG, DG), F32)], name=f"hgrn_fwd{l}",
                          compiler_params=_cp(("arbitrary",)))(*ins)


def hgrn_bwd(l, proj, states, dy, p):
    fn = _hgrn_chunk_fn(l)
    nch = HG_ROWS // HGRN_CHUNK
    nblk = SEQ // HG_ROWS

    def body(hq_ref, hf_ref, hi_ref, hg_ref, st_ref, dy_ref, lb0_ref, lb1_ref, nw_ref, seg_ref, bd_ref, tri_ref,
             cau_ref, o16_ref, dp_ref, dlb0_ref, dlb1_ref, dnw_ref, ds_sc):
        @pl.when(pl.program_id(0) == 0)
        def _():
            ds_sc[...] = jnp.zeros_like(ds_sc)
            dlb0_ref[...] = jnp.zeros_like(dlb0_ref)
            dlb1_ref[...] = jnp.zeros_like(dlb1_ref)
            dnw_ref[...] = jnp.zeros_like(dnw_ref)

        consts = (seg_ref[...], bd_ref[...], tri_ref[...], cau_ref[...], o16_ref[...])

        def chunk(cr, carry):
            ci = nch - 1 - cr
            sl = pl.ds(pl.multiple_of(ci * HGRN_CHUNK, HGRN_CHUNK), HGRN_CHUNK)
            f = lambda hq, hf, hi, hg, sp, b0, b1, nw: fn(hq, hf, hi, hg, sp, b0, b1, nw, *consts)
            _, vjp = jax.vjp(f, hq_ref[sl, :], hf_ref[sl, :], hi_ref[sl, :], hg_ref[sl, :], st_ref[ci],
                             lb0_ref[...], lb1_ref[...], nw_ref[...])
            dhq, dhf, dhi, dhg, dsp, dlb0, dlb1, dnw = vjp((dy_ref[sl, :], ds_sc[...]))
            dp_ref[sl, 0:DG] = dhq
            dp_ref[sl, DG:2 * DG] = dhf
            dp_ref[sl, 2 * DG:3 * DG] = dhi
            dp_ref[sl, 3 * DG:4 * DG] = dhg
            ds_sc[...] = dsp
            dlb0_ref[...] += dlb0
            dlb1_ref[...] += dlb1
            dnw_ref[...] += dnw
            return carry

        lax.fori_loop(0, nch, chunk, 0)

    rows = lambda cb: pl.BlockSpec((HG_ROWS, DG), lambda i: (nblk - 1 - i, cb))
    ins = [proj, proj, proj, proj, states, dy, p["lb0"], p["lb1"], p["hgrn_norm_w"]] + _hgrn_consts(p)
    specs = [rows(C_HQ // DG), rows(C_HF // DG), rows(C_HI // DG), rows(C_HG // DG),
             pl.BlockSpec((nch, DG, DG), lambda i: (nblk - 1 - i, 0, 0)), rows(0)] + [_full(x.shape) for x in ins[6:]]
    return pl.pallas_call(body, grid=(nblk,), in_specs=specs,
                          out_specs=[pl.BlockSpec((HG_ROWS, 4 * DG), lambda i: (nblk - 1 - i, 0)), _full((1, DG)),
                                     _full((1, DG)), _full((1, DG))],
                          out_shape=[_sds((SEQ, 4 * DG)), _sds((1, DG)), _sds((1, DG)), _sds((1, DG))],
                          scratch_shapes=[pltpu.VMEM((DG, DG), F32)], name=f"hgrn_bwd{l}",
                          compiler_params=_cp(("arbitrary",)))(*ins)


def _ssd_chunk_fn(z, xs, bm, cm, dtr, sprev, dt_bias, a_log, d_par, norm_w, e128, tri, trit, seg128, ones128):
    lc = SSD_CHUNK
    dt = _softplus(dtr + dt_bias)
    a = -jnp.exp(a_log)
    da = dt * a * (lax.broadcasted_iota(jnp.int32, (1, 128), 1) < NH).astype(F32)
    cs = _dotx(tri, da)
    cst = _dotx_tn(da, trit)
    cs_b = _dotx(cs, e128)
    dt_b = _dotx(dt, e128)
    csl_b = _dotx(jnp.sum(da, axis=0, keepdims=True), e128)
    xdt = xs * dt_b
    lane = lax.broadcasted_iota(jnp.int32, (1, DG), 1)
    rowi = lax.broadcasted_iota(jnp.int32, (lc, lc), 0)
    coli = lax.broadcasted_iota(jnp.int32, (lc, lc), 1)
    y = jnp.zeros((lc, DG), F32)
    snew = jnp.zeros((DG, SSD_N), F32)
    d_b = jnp.zeros((1, DG), F32)
    wdec = xdt * jnp.exp(csl_b - cs_b)
    for g in range(2):
        bg = bm[:, g * SSD_N:(g + 1) * SSD_N]
        cg = cm[:, g * SSD_N:(g + 1) * SSD_N]
        gmat = _dot_nt(cg, bg)
        gmask = ((lane // 128) == g).astype(F32)
        snew = snew + _dot_tn(wdec * gmask, bg)
        y = y + _dot_nt(cg, sprev) * gmask * jnp.exp(cs_b)
        for hh in range(2):
            h = 2 * g + hh
            seg = jnp.where(rowi >= coli, cs[:, h:h + 1] - cst[h:h + 1, :], -1e30)
            hmask = ((lane // HD) == h).astype(F32)
            y = y + _dot(gmat * jnp.exp(seg), xdt * hmask)
            d_b = d_b + d_par[:, h:h + 1] * hmask
    cd = jnp.exp(_dotx_tn(_dotx(da, e128), ones128))
    snext = sprev * cd + snew
    y = y + xs * d_b
    y = y * _silu(z)
    ms = _dot(y * y, seg128) * (1.0 / 128.0)
    return y * lax.rsqrt(ms + RMS_EPS) * norm_w, snext


def ssd_fwd(l, proj, xc, p):
    nc = SEQ // SSD_CHUNK

    def body(z_ref, xs_ref, b_ref, c_ref, dt_ref, dtb_ref, al_ref, d_ref, nw_ref, e_ref, tri_ref, trit_ref, sg_ref,
             on_ref, y_ref, st_ref, s_sc):
        @pl.when(pl.program_id(0) == 0)
        def _():
            s_sc[...] = jnp.zeros_like(s_sc)

        sprev = s_sc[...]
        st_ref[0] = sprev
        y, snext = _ssd_chunk_fn(z_ref[...], xs_ref[...], b_ref[...], c_ref[...], dt_ref[...], sprev, dtb_ref[...],
                                 al_ref[...], d_ref[...], nw_ref[...], e_ref[...], tri_ref[...], trit_ref[...],
                                 sg_ref[...], on_ref[...])
        y_ref[...] = y
        s_sc[...] = snext

    rw = lambda w, cb: pl.BlockSpec((SSD_CHUNK, w), lambda i: (i, cb))
    ins = [proj, xc, xc, xc, proj, p["dt_bias"], p["a_log"], p["ssd_d"], p["ssd_norm_w"], p["e128"], p["tri128"],
           p["tri128t"], p["seg128"], p["ones128"]]
    specs = [rw(DG, C_Z // DG), rw(DG, 0), rw(DG, 1), rw(DG, 2), rw(128, C_DT // 128)] + [_full(x.shape) for x in ins[5:]]
    return pl.pallas_call(body, grid=(nc,), in_specs=specs,
                          out_specs=[rw(DG, 0), pl.BlockSpec((1, DG, SSD_N), lambda i: (i, 0, 0))],
                          out_shape=[_sds((SEQ, DG)), _sds((nc, DG, SSD_N))],
                          scratch_shapes=[pltpu.VMEM((DG, SSD_N), F32)], name=f"ssd_fwd{l}",
                          compiler_params=_cp(("arbitrary",)))(*ins)


def ssd_bwd(l, proj, xc, states, dy, p):
    nc = SEQ // SSD_CHUNK

    def body(z_ref, xs_ref, b_ref, c_ref, dt_ref, st_ref, dy_ref, dtb_ref, al_ref, d_ref, nw_ref, e_ref, tri_ref,
             trit_ref, sg_ref, on_ref, dz_ref, dxc_ref, ddt_ref, ddtb_ref, dal_ref, dd_ref, dnw_ref, ds_sc):
        @pl.when(pl.program_id(0) == 0)
        def _():
            ds_sc[...] = jnp.zeros_like(ds_sc)
            ddtb_ref[...] = jnp.zeros_like(ddtb_ref)
            dal_ref[...] = jnp.zeros_like(dal_ref)
            dd_ref[...] = jnp.zeros_like(dd_ref)
            dnw_ref[...] = jnp.zeros_like(dnw_ref)

        consts = (e_ref[...], tri_ref[...], trit_ref[...], sg_ref[...], on_ref[...])
        f = lambda *a: _ssd_chunk_fn(*a, *consts)
        _, vjp = jax.vjp(f, z_ref[...], xs_ref[...], b_ref[...], c_ref[...], dt_ref[...], st_ref[0], dtb_ref[...],
                         al_ref[...], d_ref[...], nw_ref[...])
        dz, dxs, db, dc, ddt, dsp, ddtb, dal, dd, dnw = vjp((dy_ref[...], ds_sc[...]))
        dz_ref[...] = dz
        dxc_ref[:, 0:DG] = dxs
        dxc_ref[:, DG:2 * DG] = db
        dxc_ref[:, 2 * DG:3 * DG] = dc
        ddt_ref[...] = ddt
        ds_sc[...] = dsp
        ddtb_ref[...] += ddtb
        dal_ref[...] += dal
        dd_ref[...] += dd
        dnw_ref[...] += dnw

    rw = lambda w, cb: pl.BlockSpec((SSD_CHUNK, w), lambda i: (nc - 1 - i, cb))
    ins = [proj, xc, xc, xc, proj, states, dy, p["dt_bias"], p["a_log"], p["ssd_d"], p["ssd_norm_w"], p["e128"],
           p["tri128"], p["tri128t"], p["seg128"], p["ones128"]]
    specs = [rw(DG, C_Z // DG), rw(DG, 0), rw(DG, 1), rw(DG, 2), rw(128, C_DT // 128),
             pl.BlockSpec((1, DG, SSD_N), lambda i: (nc - 1 - i, 0, 0)), rw(DG, 0)] + [_full(x.shape) for x in ins[7:]]
    return pl.pallas_call(body, grid=(nc,), in_specs=specs,
                          out_specs=[rw(DG, 0), rw(3 * DG, 0), rw(128, 0), _full((1, 128)), _full((1, 128)), _full((1, 128)),
                                     _full((1, DG))],
                          out_shape=[_sds((SEQ, DG)), _sds((SEQ, 3 * DG)), _sds((SEQ, 128)), _sds((1, 128)), _sds((1, 128)),
                                     _sds((1, 128)), _sds((1, DG))],
                          scratch_shapes=[pltpu.VMEM((DG, SSD_N), F32)], name=f"ssd_bwd{l}",
                          compiler_params=_cp(("arbitrary",)))(*ins)


ATT_BLK = 128


def _att_scores(qn, kc, kp, h, dil, has_prev):
    i = lax.broadcasted_iota(jnp.int32, (ATT_BLK, ATT_BLK), 0)
    j = lax.broadcasted_iota(jnp.int32, (ATT_BLK, ATT_BLK), 1)
    slope = 2.0 ** (-8.0 * (h + 1) / NH)
    scale = HD ** -0.5
    s_c = _dot_nt(qn, kc) * scale - slope * ((i - j) * dil).astype(F32)
    s_p = _dot_nt(qn, kp) * scale - slope * ((ATT_BLK + i - j) * dil).astype(F32)
    m_c = j <= i
    m_p = jnp.logical_and(j >= i, has_prev)
    return jnp.where(m_c, s_c, -1e30), jnp.where(m_p, s_p, -1e30), m_c, m_p


def _sub_spec(ln, width, col):
    return pl.BlockSpec((ln, DG), lambda z: (0, z * (width // DG) + col // DG))


QKV_W = 3 * DG


def attn_branch_fwd(l, bi, qkv, dil):
    ln = SEQ // dil
    nb = ln // ATT_BLK

    def body(q_ref, k_ref, v_ref, o_ref, l_ref):
        def blk(n, carry):
            r0 = pl.multiple_of(n * ATT_BLK, ATT_BLK)
            rp = pl.multiple_of(jnp.maximum(n - 1, 0) * ATT_BLK, ATT_BLK)
            cur, prv = pl.ds(r0, ATT_BLK), pl.ds(rp, ATT_BLK)
            for h in range(NH):
                hs = slice(h * HD, (h + 1) * HD)
                qn, kc, vc, kp, vp = q_ref[cur, hs], k_ref[cur, hs], v_ref[cur, hs], k_ref[prv, hs], v_ref[prv, hs]
                s_c, s_p, m_c, m_p = _att_scores(qn, kc, kp, h, dil, n > 0)
                m = jnp.maximum(jnp.max(s_c, axis=1, keepdims=True), jnp.max(s_p, axis=1, keepdims=True))
                p_c = jnp.where(m_c, jnp.exp(s_c - m), 0.0)
                p_p = jnp.where(m_p, jnp.exp(s_p - m), 0.0)
                den = jnp.sum(p_c, axis=1, keepdims=True) + jnp.sum(p_p, axis=1, keepdims=True)
                o_ref[cur, hs] = (_dot(p_c, vc) + _dot(p_p, vp)) / den
                l_ref[cur, hs] = jnp.broadcast_to(m + jnp.log(den), (ATT_BLK, HD))
            return carry

        lax.fori_loop(0, nb, blk, 0)

    pv = qkv.reshape(ln, dil * QKV_W)
    out = pl.BlockSpec((ln, DG), lambda z: (0, z))
    o, lse = pl.pallas_call(body, grid=(dil,), in_specs=[_sub_spec(ln, QKV_W, 0), _sub_spec(ln, QKV_W, DG), _sub_spec(ln, QKV_W, 2 * DG)],
                            out_specs=[out, out], out_shape=[_sds((ln, dil * DG))] * 2, name=f"attn_fwd{l}_{bi}",
                            compiler_params=_cp(("parallel",)))(pv, pv, pv)
    return o.reshape(SEQ, DG), lse.reshape(SEQ, DG)


def attn_branch_bwd(l, bi, qkv, dil, dyb, lse_all, delta):
    ln = SEQ // dil
    nb = ln // ATT_BLK
    scale = HD ** -0.5

    def body(q_ref, k_ref, v_ref, do_ref, l_ref, dl_ref, dq_ref, dk_ref, dv_ref):
        dk_ref[...] = jnp.zeros_like(dk_ref)
        dv_ref[...] = jnp.zeros_like(dv_ref)

        def blk(n, carry):
            r0 = pl.multiple_of(n * ATT_BLK, ATT_BLK)
            rp = pl.multiple_of(jnp.maximum(n - 1, 0) * ATT_BLK, ATT_BLK)
            cur, prv = pl.ds(r0, ATT_BLK), pl.ds(rp, ATT_BLK)
            for h in range(NH):
                hs = slice(h * HD, (h + 1) * HD)
                qn, don = q_ref[cur, hs], do_ref[cur, hs]
                lse, dlt = l_ref[cur, h * HD:h * HD + 1], dl_ref[cur, h * HD:h * HD + 1]
                kc, vc, kp, vp = k_ref[cur, hs], v_ref[cur, hs], k_ref[prv, hs], v_ref[prv, hs]
                s_c, s_p, m_c, m_p = _att_scores(qn, kc, kp, h, dil, n > 0)
                p_c = jnp.where(m_c, jnp.exp(s_c - lse), 0.0)
                p_p = jnp.where(m_p, jnp.exp(s_p - lse), 0.0)
                ds_c = p_c * (_dot_nt(don, vc) - dlt)
                ds_p = p_p * (_dot_nt(don, vp) - dlt)
                dq_ref[cur, hs] = (_dot(ds_c, kc) + _dot(ds_p, kp)) * scale
                dv_ref[prv, hs] += _dot_tn(p_p, don)
                dk_ref[prv, hs] += _dot_tn(ds_p, qn) * scale
                dv_ref[cur, hs] += _dot_tn(p_c, don)
                dk_ref[cur, hs] += _dot_tn(ds_c, qn) * scale
            return carry

        lax.fori_loop(0, nb, blk, 0)

    pv = qkv.reshape(ln, dil * QKV_W)
    sub = lambda t: t.reshape(ln, dil * DG)
    row = pl.BlockSpec((ln, DG), lambda z: (0, z))
    outs = pl.pallas_call(body, grid=(dil,),
                          in_specs=[_sub_spec(ln, QKV_W, 0), _sub_spec(ln, QKV_W, DG), _sub_spec(ln, QKV_W, 2 * DG), row, row, row],
                          out_specs=[row] * 3, out_shape=[_sds((ln, dil * DG))] * 3, name=f"attn_bwd{l}_{bi}",
                          compiler_params=_cp(("parallel",)))(pv, pv, pv, sub(dyb), sub(lse_all), sub(delta))
    return [t.reshape(SEQ, DG) for t in outs]


def _attn_merge_fn(o1, o2, o3, l1, l2, l3):
    m = jnp.maximum(jnp.maximum(l1, l2), l3)
    w1, w2, w3 = jnp.exp(l1 - m), jnp.exp(l2 - m), jnp.exp(l3 - m)
    den = w1 + w2 + w3
    return (w1 * o1 + w2 * o2 + w3 * o3) / den, m + jnp.log(den)


def attn_merge(l, os_, ls_):
    ins = list(os_) + list(ls_)
    return _map_fwd(f"attn_merge{l}", _attn_merge_fn, (SEQ // RB,), ins, [_rows(DG)] * 6, [_sds((SEQ, DG))] * 2,
                    [_rows(DG)] * 2)


def attn_delta(l, dyb, yb, seg):
    fn = lambda d, y, s: (_dot(d * y, s),)
    return _map_fwd(f"attn_delta{l}", fn, (SEQ // RB,), [dyb, yb, seg], [_rows(DG), _rows(DG), _full((DG, DG))],
                    [_sds((SEQ, DG))], [_rows(DG)])[0]


def _ln_fn(x, mix, w, b):
    h = ALPHA * x + mix
    mu = jnp.mean(h, axis=-1, keepdims=True)
    d = h - mu
    var = jnp.mean(d * d, axis=-1, keepdims=True)
    return (d * lax.rsqrt(var + LN_EPS) * w + b,)


def ln_fwd(name, x, mix, w, b):
    specs = [_rows(D_MODEL), _rows(D_MODEL), _full((1, D_MODEL)), _full((1, D_MODEL))]
    return _map_fwd(name, _ln_fn, (SEQ // RB,), [x, mix, w, b], specs, [_sds((SEQ, D_MODEL))], [_rows(D_MODEL)])[0]


def ln_bwd(name, x, mix, w, b, dy):
    specs = [_rows(D_MODEL), _rows(D_MODEL), _full((1, D_MODEL)), _full((1, D_MODEL))]
    return _map_bwd(name, _ln_fn, (SEQ // RB,), [x, mix, w, b], specs, [[dy]], [[_rows(D_MODEL)]], want=[1, 2, 3],
                    acc=(2, 3))


def loss_call(y, tgt):
    def fn(yy, tt):
        e = yy - tt
        part = 0.5 * jnp.sum(jnp.sum(e * e, axis=-1, keepdims=True) * (1.0 / D_MODEL), axis=0, keepdims=True)
        return e * (1.0 / D_MODEL), jnp.broadcast_to(part, (8, 128))

    return _map_fwd("loss", fn, (SEQ // RB,), [y, tgt], [_rows(D_MODEL)] * 2,
                    [_sds((SEQ, D_MODEL)), _sds((SEQ // RB * 8, 128))],
                    [_rows(D_MODEL), pl.BlockSpec((8, 128), lambda i: (i, 0))])


def layer_fwd(l, x, vfirst, wts, p):
    sv = {"x": x}
    proj = _mm(f"mm_in{l}", x, wts["w_in"], "nn", 512, 1024, 1024)
    fl = lerp_fwd(l, proj, p["mu"])
    xc = conv_fwd(l, proj, p["conv_w"], p["conv_b"])
    w, k2, v2, c, b, g = rwkv_pre_fwd(l, fl, vfirst, p)
    y_scan, states = rwkv_scan_fwd(l, fl, w, k2, v2, c, b, p)
    ya = rwkv_post_fwd(l, y_scan, fl, k2, v2, g, p)
    qkv = proj[:, C_AQ:C_AQ + 3 * DG]
    outs, lses = [], []
    for bi, (win, dil) in enumerate(DILATED):
        o, lse = attn_branch_fwd(l, bi, qkv, dil)
        outs.append(o)
        lses.append(lse)
    yb, lse_all = attn_merge(l, outs, lses)
    yc, ssd_states = ssd_fwd(l, proj, xc, p)
    yd, hg_states = hgrn_fwd(l, proj, p)
    ycat = jnp.concatenate([ya, yb, yc, yd], axis=1)
    mix = _mm(f"mm_out{l}", ycat, wts["w_out"], "nn", 512, 1024, 1024)
    x1 = ln_fwd(f"ln1_fwd{l}", x, mix, p["ln1_w"], p["ln1_b"])
    hh = _mm(f"mm_up{l}", x1, wts["w_up_t"], "nt", 512, 1024, 1024, epilogue="relu2")
    m2 = _mm(f"mm_down{l}", hh, wts["w_down"], "nn", 512, 1024, 1024)
    x2 = ln_fwd(f"ln2_fwd{l}", x1, m2, p["ln2_w"], p["ln2_b"])
    sv.update(proj=proj, fl=fl, xc=xc, w=w, k2=k2, v2=v2, c=c, b=b, g=g, y_scan=y_scan, states=states,
              yb=yb, lse_all=lse_all, ssd_states=ssd_states, hg_states=hg_states, ycat=ycat, mix=mix, x1=x1, hh=hh, qkv=qkv,
              m2=m2, vfirst=vfirst)
    return x2, sv


def layer_bwd(l, dx2, dvfirst_next, sv, wts, p):
    gr = {}
    x, x1, proj, fl = sv["x"], sv["x1"], sv["proj"], sv["fl"]
    dres2, gr["ln2_w"], gr["ln2_b"] = ln_bwd(f"ln2_bwd{l}", x1, sv["m2"], p["ln2_w"], p["ln2_b"], dx2)
    du = _mm(f"mm_down_dx{l}", dres2, wts["w_down"], "nt", 512, 1024, 1024, add=sv["hh"], epilogue="relu2_bwd")
    gr["w_down"] = _mm(f"mm_down_dw{l}", sv["hh"], dres2, "tn", 512, 1024, 512)
    dx1 = _mm(f"mm_up_dx{l}", du, wts["w_up_t"], "nn", 512, 1024, 1024, add=dres2, add_scale=ALPHA)
    gr["w_up_t"] = _mm(f"mm_up_dw{l}", du, x1, "tn", 512, 1024, 512)
    dres1, gr["ln1_w"], gr["ln1_b"] = ln_bwd(f"ln1_bwd{l}", x, sv["mix"], p["ln1_w"], p["ln1_b"], dx1)
    dycat = _mm(f"mm_out_dx{l}", dres1, wts["w_out"], "nt", 512, 1024, 1024)
    gr["w_out"] = _mm(f"mm_out_dw{l}", sv["ycat"], dres1, "tn", 512, 1024, 512)
    dya, dyb, dyc, dyd = (dycat[:, i * DG:(i + 1) * DG] for i in range(4))
    dhg4, gr["lb0"], gr["lb1"], gr["hgrn_norm_w"] = hgrn_bwd(l, proj, sv["hg_states"], dyd, p)
    dz, dxc, ddt, gr["dt_bias"], gr["a_log"], gr["ssd_d"], gr["ssd_norm_w"] = ssd_bwd(l, proj, sv["xc"], sv["ssd_states"], dyc, p)
    dxbc, gr["conv_w"], gr["conv_b"] = conv_bwd(l, proj, p["conv_w"], p["conv_b"], dxc)
    delta = attn_delta(l, dyb, sv["yb"], p["seg64"])
    dqs, dks, dvs = [], [], []
    for bi, (win, dil) in enumerate(DILATED):
        dq, dk, dv = attn_branch_bwd(l, bi, sv["qkv"], dil, dyb, sv["lse_all"], delta)
        dqs.append(dq)
        dks.append(dk)
        dvs.append(dv)
    dq_a, dk_a, dv_a = _addn(f"attn_dq{l}", *dqs), _addn(f"attn_dk{l}", *dks), _addn(f"attn_dv{l}", *dvs)
    pg = rwkv_post_bwd(l, sv["y_scan"], fl, sv["k2"], sv["v2"], sv["g"], p, dya)
    gr["lnx_w"], gr["lnx_b"], gr["r_k"] = pg["lnx_w"], pg["lnx_b"], pg["r_k"]
    dr, dw, dk, dv, dc, db = rwkv_scan_bwd(l, fl, sv["w"], sv["k2"], sv["v2"], sv["c"], sv["b"], sv["states"], pg["y"], p)
    v2_cts = [dv, pg["v2"]] + ([dvfirst_next] if dvfirst_next is not None else [])
    qg = rwkv_pre_bwd(l, fl, sv["vfirst"], p, [[dw], [dk, pg["k2"]], v2_cts, [dc], [db], [pg["g"]]])
    for nme in ("w0", "w2p", "a0", "a2p", "g2p", "k_k", "k_a", "v0", "v2p"):
        if nme in qg:
            gr[nme] = qg[nme]
    dfr = _addn(f"rwkv_dr{l}", dr, pg["fr"])
    dvres = qg["fvres"] if l > 0 else jnp.zeros((SEQ, 128), F32)
    dfl_out = jnp.concatenate([dfr, qg["fk"], qg["fv"], qg["flora"], dvres], axis=1)
    dfl_in, gr["mu"] = lerp_bwd(l, proj, p["mu"], dfl_out)
    dproj = jnp.concatenate([dfl_in[:, 0:768], dq_a, dk_a, dv_a, dz, dxbc, dhg4, dfl_in[:, 768:896], ddt,
                             dfl_in[:, 896:1024], jnp.zeros((SEQ, 128), F32)], axis=1)
    dx = _mm(f"mm_in_dx{l}", dproj, wts["w_in"], "nt", 512, 1024, 1024, add=dres1, add_scale=ALPHA)
    gr["w_in"] = _mm(f"mm_in_dw{l}", x, dproj, "tn", 512, 1024, 512)
    return dx, (qg["vfirst"] if l > 0 else None), gr


def _w_in_pad(w_in_l, w_vres):
    rows = w_in_l.shape[0]
    z = lambda n: jnp.zeros((rows, n), w_in_l.dtype)
    vres = z(128) if w_vres is None else jnp.concatenate([w_vres, z(96)], axis=1)
    return jnp.concatenate([w_in_l[:, 0:768], w_in_l[:, 896:1664], w_in_l[:, 1664:1920], w_in_l[:, 1920:2688],
                            w_in_l[:, 2692:3716], w_in_l[:, 768:896], w_in_l[:, 2688:2692], z(124), vres, z(128)], axis=1)


def _w_in_unpad(g):
    g_in = jnp.concatenate([g[:, 0:768], g[:, C_LORA:C_LORA + 128], g[:, 768:1536], g[:, C_Z:C_Z + 256],
                            g[:, C_XBC:C_XBC + 768], g[:, C_DT:C_DT + 4], g[:, C_HQ:C_HQ + 1024]], axis=1)
    return g_in, g[:, C_VRES:C_VRES + 32]


def _consts():
    i16 = jnp.arange(HGRN_CHUNK)
    pair = jnp.arange(HGRN_CHUNK * HGRN_CHUNK)
    i128 = jnp.arange(128)
    seg64 = _seg_ones(DG, HD)
    tri128 = (i128[:, None] >= i128[None, :]).astype(F32)
    return dict(
        seg64=seg64, seg64x3_bf16=jnp.concatenate([seg64, seg64, seg64], axis=0).astype(BF16),
        dmask=(jnp.arange(HD)[:, None] == (jnp.arange(DG)[None, :] % HD)).astype(F32),
        tri16=(i16[:, None] >= i16[None, :]).astype(F32),
        causal16=jnp.broadcast_to(((pair // HGRN_CHUNK) >= (pair % HGRN_CHUNK)).astype(F32)[:, None], (256, DG)),
        ones16=jnp.ones((HGRN_CHUNK, DG), F32),
        e128=((i128[:, None] == (jnp.arange(DG)[None, :] // HD)) & (i128[:, None] < NH)).astype(F32),
        tri128=tri128, tri128t=tri128.T, seg128=_seg_ones(DG, 128), ones128=jnp.ones((128, 128), F32))


def _pad_lanes(v, n):
    return jnp.concatenate([v, jnp.zeros((n - v.shape[0],), v.dtype)])[None, :]


def _layer_params(l, raw, consts):
    p = dict(consts)
    row = lambda name: raw[name][l][None, :]
    z = lambda r: jnp.zeros((r, DG), F32)
    mu_vres = raw["mu_vres"][l - 1] if l > 0 else jnp.zeros((32,), F32)
    p["mu"] = jnp.concatenate([raw["mu_shift"][l], mu_vres, jnp.zeros((96,), F32)])[None, :]
    p["conv_w"], p["conv_b"] = raw["ssd_conv_w"][l], row("ssd_conv_b")
    p["w0"], p["a0"], p["k_k"], p["k_a"] = row("rwkv_w0"), row("rwkv_a0"), row("rwkv_k_k"), row("rwkv_k_a")
    p["lnx_w"], p["lnx_b"] = row("rwkv_lnx_w"), row("rwkv_lnx_b")
    p["r_k"] = raw["rwkv_r_k"][l].reshape(1, DG)
    p["w2p"] = jnp.concatenate([raw["rwkv_w2"][l], z(96)], axis=0)
    p["a2p"] = jnp.concatenate([z(32), raw["rwkv_a2"][l], z(64)], axis=0)
    p["g2p"] = jnp.concatenate([z(64), raw["rwkv_g2"][l]], axis=0)
    if l > 0:
        p["v0"] = raw["rwkv_v0"][l - 1][None, :]
        p["v2p"] = jnp.concatenate([raw["rwkv_v2"][l - 1], z(96)], axis=0)
    p["lb0"], p["lb1"] = raw["lower_bounds"][0:1], raw["lower_bounds"][1:2]
    p["hgrn_norm_w"], p["ssd_norm_w"] = row("hgrn_norm_w"), row("ssd_norm_w")
    p["dt_bias"], p["a_log"], p["ssd_d"] = (_pad_lanes(raw[n][l], 128) for n in ("ssd_dt_bias", "ssd_A_log", "ssd_D"))
    for n in ("ln1_w", "ln1_b", "ln2_w", "ln2_b"):
        p[n] = row(n)
    return p


def _natural_grads(g0, g1):
    gs = (g0, g1)
    st = lambda key, f=lambda a: a[0]: jnp.stack([f(g[key]) for g in gs])
    out = {}
    out["lower_bounds"] = jnp.concatenate([g0["lb0"] + g1["lb0"], g0["lb1"] + g1["lb1"]], axis=0)
    out["mu_shift"] = st("mu", lambda a: a[0, :896])
    out["mu_vres"] = g1["mu"][:, 896:928]
    out["rwkv_w0"], out["rwkv_a0"], out["rwkv_k_k"], out["rwkv_k_a"] = st("w0"), st("a0"), st("k_k"), st("k_a")
    out["rwkv_w2"] = st("w2p", lambda a: a[0:32])
    out["rwkv_a2"] = st("a2p", lambda a: a[32:64])
    out["rwkv_g2"] = st("g2p", lambda a: a[64:128])
    out["rwkv_r_k"] = st("r_k", lambda a: a.reshape(NH, HD))
    out["rwkv_lnx_w"], out["rwkv_lnx_b"] = st("lnx_w"), st("lnx_b")
    out["rwkv_v0"] = g1["v0"]
    out["rwkv_v2"] = g1["v2p"][None, 0:32]
    out["ssd_conv_w"] = st("conv_w", lambda a: a)
    out["ssd_conv_b"] = st("conv_b")
    out["ssd_dt_bias"], out["ssd_A_log"], out["ssd_D"] = (st(k, lambda a: a[0, :NH]) for k in ("dt_bias", "a_log", "ssd_d"))
    out["ssd_norm_w"], out["hgrn_norm_w"] = st("ssd_norm_w"), st("hgrn_norm_w")
    for n in ("ln1_w", "ln1_b", "ln2_w", "ln2_b"):
        out[n] = st(n)
    return out


MESH_T = pl.DeviceIdType.MESH
ANY = pl.BlockSpec(memory_space=pl.ANY)


def _dev_index(px, py, pc):
    return 4 * px + 2 * py + pc


def all_gather(arrs):
    n = len(arrs)

    def body(*refs):
        ins, outs = refs[:n], refs[n:2 * n]
        send_sems, recv_sems, local_sems = refs[2 * n:]
        x, y, c = lax.axis_index("x"), lax.axis_index("y"), lax.axis_index("c")
        me, sibling = (x, y, c), (x, y, 1 - c)
        chips = [(1 - x, y), (x, 1 - y), (1 - x, 1 - y)]

        def copy(a, k, block, to, src=None):
            slot = outs[a].at[_dev_index(*block)]
            return pltpu.make_async_remote_copy(src_ref=slot if src is None else src, dst_ref=slot,
                                                send_sem=send_sems.at[a, k], recv_sem=recv_sems.at[a, k],
                                                device_id=to, device_id_type=MESH_T)

        mine = [pltpu.make_async_copy(ins[a], outs[a].at[_dev_index(*me)], local_sems.at[a]) for a in range(n)]
        for cp in mine:
            cp.start()
        first = []
        for a in range(n):
            first.append(copy(a, 0, me, sibling, src=ins[a]))
            first += [copy(a, 1 + j, me, (*chip, c), src=ins[a]) for j, chip in enumerate(chips)]
        for cp in first:
            cp.start()
        passed = []
        for j, chip in enumerate(chips):
            for a in range(n):
                copy(a, 1 + j, (*chip, c), me).wait_recv()
                fwd = copy(a, 4 + j, (*chip, c), sibling)
                fwd.start()
                passed.append(fwd)
        for a in range(n):
            copy(a, 0, sibling, me).wait_recv()
            for j, chip in enumerate(chips):
                copy(a, 4 + j, (*chip, 1 - c), me).wait_recv()
        for cp in first + passed:
            cp.wait_send()
        for cp in mine:
            cp.wait()

    return pl.pallas_call(
        body, in_specs=[ANY] * n, out_specs=[ANY] * n,
        out_shape=[_sds((N_DEV,) + a.shape, a.dtype) for a in arrs],
        scratch_shapes=[pltpu.SemaphoreType.DMA((n, 7)), pltpu.SemaphoreType.DMA((n, 7)), pltpu.SemaphoreType.DMA((n,))],
        name="all_gather")(*arrs)


def _chips(x, y):
    return [(x, y), (1 - x, y), (x, 1 - y), (1 - x, 1 - y)]


def exchange_siblings(arrs):
    n = len(arrs)

    def body(*refs):
        ins, sib = refs[:n], refs[n:2 * n]
        send_sems, recv_sems = refs[2 * n:]
        x, y, c = lax.axis_index("x"), lax.axis_index("y"), lax.axis_index("c")
        sibling = (x, y, 1 - c)
        sends = []
        for a in range(n):
            for k, (cx, cy) in enumerate(_chips(x, y)):
                sd = pltpu.make_async_remote_copy(src_ref=ins[a].at[_dev_index(cx, cy, 1 - c)], dst_ref=sib[a].at[k],
                                                  send_sem=send_sems.at[a, k], recv_sem=recv_sems.at[a, k],
                                                  device_id=sibling, device_id_type=MESH_T)
                sd.start()
                sends.append(sd)
        for sd in sends:
            sd.wait_recv()
        for sd in sends:
            sd.wait_send()

    sem = pltpu.SemaphoreType.DMA((n, 4))
    return pl.pallas_call(body, in_specs=[ANY] * n, out_specs=[ANY] * n,
                          out_shape=[_sds((4,) + a.shape[1:], a.dtype) for a in arrs],
                          scratch_shapes=[sem, sem], name="exchange_siblings")(*arrs)


def reduce_pair(name, send, slots, sib, wire_dtype):
    _, r, c = send.shape
    rb = min(r, 262144 // c)

    def body(slots_ref, m0, m1, m2, m3, s_ref, own_ref, part_ref):
        own_ref[...] = m0[...] + s_ref[0]
        for k, m_ref in enumerate((m1, m2, m3)):
            part_ref[k] = (m_ref[...] + s_ref[k + 1]).astype(wire_dtype)

    mine = [pl.BlockSpec((None, rb, c), lambda i, s, k=k: (s[k], i, 0)) for k in range(4)]
    grid_spec = pltpu.PrefetchScalarGridSpec(
        num_scalar_prefetch=1, grid=(r // rb,),
        in_specs=mine + [pl.BlockSpec((4, rb, c), lambda i, s: (0, i, 0))],
        out_specs=[pl.BlockSpec((rb, c), lambda i, s: (i, 0)), pl.BlockSpec((3, rb, c), lambda i, s: (0, i, 0))])
    return pl.pallas_call(body, grid_spec=grid_spec, out_shape=[_sds((r, c)), _sds((3, r, c), wire_dtype)], name=name,
                          compiler_params=_cp(("parallel",)))(slots, send, send, send, send, sib)


def exchange_chips(parts, rep):
    n = len(parts)

    def body(*refs):
        ins, rep_ref = refs[:n], refs[n]
        recv, rep_all = refs[n + 1:2 * n + 1], refs[2 * n + 1]
        send_sems, recv_sems, rsend_sems, rrecv_sems, local_sem = refs[2 * n + 2:]
        x, y, c = lax.axis_index("x"), lax.axis_index("y"), lax.axis_index("c")
        me = _dev_index(x, y, c)
        mine = pltpu.make_async_copy(rep_ref, rep_all.at[me], local_sem)
        mine.start()
        cps = []
        for a in range(n):
            for k, (cx, cy) in enumerate(_chips(x, y)[1:]):
                cp = pltpu.make_async_remote_copy(src_ref=ins[a].at[k], dst_ref=recv[a].at[k],
                                                  send_sem=send_sems.at[a, k], recv_sem=recv_sems.at[a, k],
                                                  device_id=(cx, cy, c), device_id_type=MESH_T)
                cp.start()
                cps.append(cp)
        rels = [(rx, ry, rc) for rx in (0, 1) for ry in (0, 1) for rc in (0, 1)][1:]
        peers = [(jnp.where(rx, 1 - x, x), jnp.where(ry, 1 - y, y), jnp.where(rc, 1 - c, c)) for rx, ry, rc in rels]
        rcps = []
        for k, peer in enumerate(peers):
            cp = pltpu.make_async_remote_copy(src_ref=rep_ref, dst_ref=rep_all.at[me], send_sem=rsend_sems.at[k],
                                              recv_sem=rrecv_sems.at[k], device_id=peer, device_id_type=MESH_T)
            cp.start()
            rcps.append(cp)
        for k, peer in enumerate(peers):
            pltpu.make_async_remote_copy(src_ref=rep_ref, dst_ref=rep_all.at[_dev_index(*peer)], send_sem=rsend_sems.at[k],
                                         recv_sem=rrecv_sems.at[k], device_id=peer, device_id_type=MESH_T).wait_recv()
        for cp in cps:
            cp.wait_recv()
        for cp in cps + rcps:
            cp.wait_send()
        mine.wait()

    outs = pl.pallas_call(
        body, in_specs=[ANY] * (n + 1), out_specs=[ANY] * (n + 1),
        out_shape=[_sds(a.shape, a.dtype) for a in parts] + [_sds((N_DEV,) + rep.shape, rep.dtype)],
        scratch_shapes=[pltpu.SemaphoreType.DMA((n, 3)), pltpu.SemaphoreType.DMA((n, 3)), pltpu.SemaphoreType.DMA((7,)),
                        pltpu.SemaphoreType.DMA((7,)), pltpu.SemaphoreType.DMA],
        name="exchange_chips")(*parts, rep)
    return outs[:n], outs[n]


def adamw(name, terms, w, m, v):
    r, c = w.shape
    rb = min(r, 262144 // c)
    c1 = 1.0 - ADAM_B1 ** ADAM_STEP
    c2 = 1.0 - ADAM_B2 ** ADAM_STEP
    nt = len(terms)

    def body(*refs):
        w_ref, m_ref, v_ref = refs[nt:nt + 3]
        g_ref, d_ref, nm_ref, nv_ref = refs[nt + 3:]
        g = refs[0][...].astype(F32)
        for t_ref in refs[1:nt]:
            g = g + t_ref[...].astype(F32)
        nm = ADAM_B1 * m_ref[...] + (1.0 - ADAM_B1) * g
        nv = ADAM_B2 * v_ref[...] + (1.0 - ADAM_B2) * (g * g)
        g_ref[...] = g
        nm_ref[...] = nm
        nv_ref[...] = nv
        d_ref[...] = -ADAM_LR * ((nm / c1) / (jnp.sqrt(nv / c2) + ADAM_EPS) + ADAM_WD * w_ref[...])

    blk = pl.BlockSpec((rb, c), lambda i: (i, 0))
    tspecs = [blk if k is None else pl.BlockSpec((None, rb, c), lambda i, k=k: (k, i, 0)) for _, k in terms]
    return pl.pallas_call(body, grid=(r // rb,), in_specs=tspecs + [blk] * 3, out_specs=[blk] * 4,
                          out_shape=[_sds((r, c))] * 4, name=name,
                          compiler_params=_cp(("parallel",)))(*[t for t, _ in terms], w, m, v)


SMS_ROWS = 16
REP_ROWS = 24
N_BIG = 8
SMALL_SHARDED = (("rwkv_w2", (2, 32, 32)), ("rwkv_a2", (2, 32, 32)), ("rwkv_g2", (2, 64, 32)), ("rwkv_v2", (1, 32, 32)),
                 ("ssd_conv_w", (2, 4, 96)))
REPLICATED = (("lower_bounds", (2, 256)), ("mu_shift", (2, 896)), ("mu_vres", (1, 32)), ("rwkv_w0", (2, 256)),
              ("rwkv_a0", (2, 256)), ("rwkv_k_k", (2, 256)), ("rwkv_k_a", (2, 256)), ("rwkv_r_k", (2, 4, 64)),
              ("rwkv_lnx_w", (2, 256)), ("rwkv_lnx_b", (2, 256)), ("rwkv_v0", (1, 256)), ("ssd_conv_b", (2, 768)),
              ("ssd_dt_bias", (2, 4)), ("ssd_A_log", (2, 4)), ("ssd_D", (2, 4)), ("ssd_norm_w", (2, 256)),
              ("hgrn_norm_w", (2, 256)), ("ln1_w", (2, 1024)), ("ln1_b", (2, 1024)), ("ln2_w", (2, 1024)),
              ("ln2_b", (2, 1024)))


def _flat_rows(parts, rows):
    flat = jnp.concatenate([a.reshape(-1) for a in parts])
    return jnp.concatenate([flat, jnp.zeros((rows * PACK_W - flat.shape[0],), flat.dtype)]).reshape(rows, PACK_W)


def _local_arrays(d):
    arrs = [_w_in_pad(d["w_in"][0], None), _w_in_pad(d["w_in"][1], d["w_in_vres"][0]), d["w_out"][0], d["w_out"][1],
            d["w_up"][0].T, d["w_up"][1].T, d["w_down"][0], d["w_down"][1],
            _flat_rows([d[n] for n, _ in SMALL_SHARDED], SMS_ROWS)]
    return arrs, _flat_rows([d[n] for n, _ in REPLICATED], REP_ROWS)


def _unflat(rows2d, table):
    flat, out, o = rows2d.reshape(-1), {}, 0
    for name, shape in table:
        n = 1
        for s in shape:
            n *= s
        out[name] = flat[o:o + n].reshape(shape)
        o += n
    return out


def _from_local_arrays(arrs, rep):
    d = {}
    g0, _ = _w_in_unpad(arrs[0])
    g1, gv = _w_in_unpad(arrs[1])
    d["w_in"], d["w_in_vres"] = jnp.stack([g0, g1]), gv[None]
    d["w_out"] = jnp.stack([arrs[2], arrs[3]])
    d["w_up"] = jnp.stack([arrs[4].T, arrs[5].T])
    d["w_down"] = jnp.stack([arrs[6], arrs[7]])
    d.update(_unflat(arrs[8], SMALL_SHARDED))
    d.update(_unflat(rep, REPLICATED))
    return d


def _gathered_weights(gathered):
    full = [g.reshape(N_DEV * g.shape[1], g.shape[2]) for g in gathered[:N_BIG]]
    wts = [dict(w_in=full[l], w_out=full[2 + l], w_up_t=full[4 + l], w_down=full[6 + l]) for l in range(DEPTH)]
    small, flat, o = {}, gathered[N_BIG].reshape(N_DEV, -1), 0
    for name, shape in SMALL_SHARDED:
        n = shape[0] * shape[1] * shape[2]
        blk = flat[:, o:o + n].reshape((N_DEV,) + shape)
        small[name] = blk.transpose(1, 2, 0, 3).reshape(shape[0], shape[1], N_DEV * shape[2])
        o += n
    return wts, small


def _send_arrays(big, small_grads):
    blocks = lambda g: g.reshape(N_DEV, g.shape[0] // N_DEV, g.shape[1])
    arrs = [blocks(big[l][k]) for k in ("w_in", "w_out", "w_up_t", "w_down") for l in range(DEPTH)]
    sms = []
    for name, shape in SMALL_SHARDED:
        g = small_grads[name].reshape(shape[0], shape[1], N_DEV, shape[2]).transpose(2, 0, 1, 3)
        sms.append(g.reshape(N_DEV, -1))
    sms = jnp.concatenate(sms, axis=1)
    sms = jnp.concatenate([sms, jnp.zeros((N_DEV, SMS_ROWS * PACK_W - sms.shape[1]), F32)], axis=1)
    arrs.append(sms.reshape(N_DEV, SMS_ROWS, PACK_W))
    return arrs, _flat_rows([small_grads[n] for n, _ in REPLICATED], REP_ROWS)


def _local_step(x, tgt, wts, raw):
    consts = _consts()
    ps = [_layer_params(l, raw, consts) for l in range(DEPTH)]
    x1, sv0 = layer_fwd(0, x, None, wts[0], ps[0])
    x2, sv1 = layer_fwd(1, x1, sv0["fl"], wts[1], ps[1])
    dy, lparts = loss_call(x2, tgt)
    loss = jnp.sum(lparts[::8, 0])
    dx1, dvfirst, g1 = layer_bwd(1, dy, None, sv1, wts[1], ps[1])
    dx0, _, g0 = layer_bwd(0, dx1, dvfirst, sv0, wts[0], ps[0])
    big = [{k: g[k] for k in ("w_in", "w_out", "w_up_t", "w_down")} for g in (g0, g1)]
    return loss, dx0, big, _natural_grads(g0, g1)


WEIGHT_NAMES = ("lower_bounds", "w_in", "w_in_vres", "mu_shift", "mu_vres", "rwkv_w0", "rwkv_w2", "rwkv_a0", "rwkv_a2",
                "rwkv_g2", "rwkv_k_k", "rwkv_k_a", "rwkv_r_k", "rwkv_lnx_w", "rwkv_lnx_b", "rwkv_v0", "rwkv_v2",
                "ssd_conv_w", "ssd_conv_b", "ssd_dt_bias", "ssd_A_log", "ssd_D", "ssd_norm_w", "hgrn_norm_w", "w_out",
                "ln1_w", "ln1_b", "w_up", "w_down", "ln2_w", "ln2_b")


def kernel(x, lower_bounds, w_in, w_in_vres, mu_shift, mu_vres, rwkv_w0, rwkv_w2, rwkv_a0, rwkv_a2, rwkv_g2, rwkv_k_k, rwkv_k_a, rwkv_r_k, rwkv_lnx_w, rwkv_lnx_b, rwkv_v0, rwkv_v2, ssd_conv_w, ssd_conv_b, ssd_dt_bias, ssd_A_log, ssd_D, ssd_norm_w, hgrn_norm_w, w_out, ln1_w, ln1_b, w_up, w_down, ln2_w, ln2_b, loss_target, m_lower_bounds, m_w_in, m_w_in_vres, m_mu_shift, m_mu_vres, m_rwkv_w0, m_rwkv_w2, m_rwkv_a0, m_rwkv_a2, m_rwkv_g2, m_rwkv_k_k, m_rwkv_k_a, m_rwkv_r_k, m_rwkv_lnx_w, m_rwkv_lnx_b, m_rwkv_v0, m_rwkv_v2, m_ssd_conv_w, m_ssd_conv_b, m_ssd_dt_bias, m_ssd_A_log, m_ssd_D, m_ssd_norm_w, m_hgrn_norm_w, m_w_out, m_ln1_w, m_ln1_b, m_w_up, m_w_down, m_ln2_w, m_ln2_b, v_lower_bounds, v_w_in, v_w_in_vres, v_mu_shift, v_mu_vres, v_rwkv_w0, v_rwkv_w2, v_rwkv_a0, v_rwkv_a2, v_rwkv_g2, v_rwkv_k_k, v_rwkv_k_a, v_rwkv_r_k, v_rwkv_lnx_w, v_rwkv_lnx_b, v_rwkv_v0, v_rwkv_v2, v_ssd_conv_w, v_ssd_conv_b, v_ssd_dt_bias, v_ssd_A_log, v_ssd_D, v_ssd_norm_w, v_hgrn_norm_w, v_w_out, v_ln1_w, v_ln1_b, v_w_up, v_w_down, v_ln2_w, v_ln2_b):
    given = dict(locals())
    w = {n: given[n] for n in WEIGHT_NAMES}
    w_arrs, w_rep = _local_arrays(w)
    m_arrs, m_rep = _local_arrays({n: given["m_" + n] for n in WEIGHT_NAMES})
    v_arrs, v_rep = _local_arrays({n: given["v_" + n] for n in WEIGHT_NAMES})
    gathered = all_gather([a.astype(BF16) for a in w_arrs[:N_BIG]] + [w_arrs[N_BIG]])
    wts, small_full = _gathered_weights(gathered)
    raw = {n: w[n] for n, _ in REPLICATED}
    raw.update(small_full)
    loss, dx, big, small_grads = _local_step(x[0], loss_target[0], wts, raw)
    send, rep = _send_arrays(big, small_grads)
    sib = exchange_siblings(send)
    mx, my, mc = lax.axis_index("x"), lax.axis_index("y"), lax.axis_index("c")
    slots = jnp.stack([_dev_index(cx, cy, mc) for cx, cy in _chips(mx, my)]).astype(jnp.int32)
    own, parts = [], []
    for a in range(N_BIG + 1):
        o, pt = reduce_pair(f"reduce_pair{a}", send[a], slots, sib[a], BF16 if a < N_BIG else F32)
        own.append(o)
        parts.append(pt)
    recv, rep_all = exchange_chips(parts, rep)
    results = [adamw(f"adamw{a}", [(own[a], None), (recv[a], 0), (recv[a], 1), (recv[a], 2)], w_arrs[a], m_arrs[a], v_arrs[a])
               for a in range(N_BIG + 1)]
    rep_res = adamw("adamw_rep", [(rep_all, q) for q in range(N_DEV)], w_rep, m_rep, v_rep)
    loss = lax.psum(loss, ("x", "y", "c"))
    outs = [loss, dx[None]]
    for q in range(4):
        d = _from_local_arrays([res[q] for res in results], rep_res[q])
        outs += [d[n] for n in WEIGHT_NAMES]
    return tuple(outs)
```

```python
import functools

import jax
import jax.numpy as jnp
from jax import lax
from jax.experimental import pallas as pl
from jax.experimental.pallas import tpu as pltpu

F32 = jnp.float32
BF16 = jnp.bfloat16
HI = lax.Precision.HIGHEST

N_DEV = 8
SEQ = 2048
D_MODEL = 1024
D_FF = 4096
DG = 256
NH = 4
HD = 64
DEPTH = 2
ALPHA = (2.0 * DEPTH) ** 0.25
LN_EPS = 1e-5
RMS_EPS = 1e-5
GN_EPS = HD * 1e-5
IN_COLS = 3716
SSD_N = 128
SSD_CHUNK = 128
HGRN_CHUNK = 16
DILATED = ((128, 1), (512, 4), (2048, 16))

ADAM_LR, ADAM_B1, ADAM_B2, ADAM_EPS, ADAM_WD, ADAM_STEP = 0.001, 0.9, 0.999, 1e-08, 0.01, 10

PW = 4096
C_R, C_K, C_V = 0, 256, 512
C_AQ, C_AK, C_AV = 768, 1024, 1280
C_Z, C_XBC = 1536, 1792
C_HQ, C_HF, C_HI, C_HG = 2560, 2816, 3072, 3328
C_LORA, C_DT, C_VRES = 3584, 3712, 3840

RB = 256
VMEM_LIMIT = 56 * 1024 * 1024
PACK_W = 1024


def _cp(sem=None):
    return pltpu.CompilerParams(dimension_semantics=sem, vmem_limit_bytes=VMEM_LIMIT)


def _sds(shape, dt=F32):
    return jax.ShapeDtypeStruct(tuple(shape), dt)


def _rows(w, cb=0, rb=RB):
    return pl.BlockSpec((rb, w), lambda i: (i, cb))


def _full(shape):
    n = len(shape)
    return pl.BlockSpec(tuple(shape), lambda *_: (0,) * n)


def _sigmoid(x):
    return 1.0 / (1.0 + jnp.exp(-x))


def _silu(x):
    return x * _sigmoid(x)


def _softplus(x):
    return jnp.maximum(x, 0.0) + jnp.log(1.0 + jnp.exp(jnp.where(x > 0, -x, x)))


MID = lax.Precision.HIGH
NN, TN, NT = (((1,), (0,)), ((), ())), (((0,), (0,)), ((), ())), (((1,), (1,)), ((), ()))


def _dot(a, b):
    return lax.dot_general(a, b, NN, precision=MID, preferred_element_type=F32)


def _dot_tn(a, b):
    return lax.dot_general(a, b, TN, precision=MID, preferred_element_type=F32)


def _dot_nt(a, b):
    return lax.dot_general(a, b, NT, precision=MID, preferred_element_type=F32)


def _dotx(a, b):
    return lax.dot_general(a, b, NN, precision=HI, preferred_element_type=F32)


def _dotx_tn(a, b):
    return lax.dot_general(a, b, TN, precision=HI, preferred_element_type=F32)


def _seg_ones(n, seg):
    i = jnp.arange(n)
    return (i[:, None] // seg == i[None, :] // seg).astype(F32)


def _shift_down(x, s):
    row = lax.broadcasted_iota(jnp.int32, x.shape, 0)
    return jnp.where(row < s, 0.0, pltpu.roll(x, s, 0))


def _shift_up(x, s):
    n = x.shape[0]
    row = lax.broadcasted_iota(jnp.int32, x.shape, 0)
    return jnp.where(row >= n - s, 0.0, pltpu.roll(x, n - s, 0))


@functools.partial(jax.custom_vjp, nondiff_argnums=(1,))
def _tshift(x, s):
    return _shift_down(x, s)


def _tshift_fwd(x, s):
    return _shift_down(x, s), None


def _tshift_bwd(s, _, g):
    return (_shift_up(g, s),)


_tshift.defvjp(_tshift_fwd, _tshift_bwd)


def _map_fwd(name, fn, grid, ins, in_specs, out_shapes, out_specs):
    n_in = len(ins)

    def body(*refs):
        ys = fn(*[r[...] for r in refs[:n_in]])
        for r, y in zip(refs[n_in:], ys):
            r[...] = y

    return pl.pallas_call(body, grid=grid, in_specs=in_specs, out_specs=out_specs, out_shape=out_shapes,
                          name=name, compiler_params=_cp(("parallel",)))(*ins)


def _map_bwd(name, fn, grid, ins, in_specs, cts, ct_specs, want, acc=(), gout=None):
    n_in = len(ins)
    flat_cts = [c for group in cts for c in group]
    flat_specs = [s for group in ct_specs for s in group]
    n_ct = len(flat_cts)
    gout = gout or {}
    out_shapes = [gout[i][0] if i in gout else _sds(ins[i].shape) for i in want]
    out_specs = [gout[i][1] if i in gout else in_specs[i] for i in want]

    def body(*refs):
        xs = [r[...] for r in refs[:n_in]]
        cvals = [r[...] for r in refs[n_in:n_in + n_ct]]
        gouts = refs[n_in + n_ct:]
        cs, p = [], 0
        for group in cts:
            v = cvals[p]
            for q in range(1, len(group)):
                v = v + cvals[p + q]
            cs.append(v)
            p += len(group)

        def f(*wanted):
            full = list(xs)
            for i, w in zip(want, wanted):
                full[i] = w
            return tuple(fn(*full))

        _, vjp = jax.vjp(f, *[xs[i] for i in want])
        gs = vjp(tuple(cs))
        for o, i, g in zip(gouts, want, gs):
            if i in acc:
                @pl.when(pl.program_id(0) == 0)
                def _():
                    o[...] = jnp.zeros_like(o)

                o[...] += g
            else:
                o[...] = g

    sem = ("arbitrary",) if acc else ("parallel",)
    return pl.pallas_call(body, grid=grid, in_specs=list(in_specs) + flat_specs, out_specs=out_specs,
                          out_shape=out_shapes, name=name, compiler_params=_cp(sem))(*ins, *flat_cts)


def _addn(name, *arrs):
    n, c = arrs[0].shape

    def fn(*xs):
        r = xs[0]
        for x in xs[1:]:
            r = r + x
        return (r,)

    return _map_fwd(name, fn, (n // RB,), list(arrs), [_rows(c)] * len(arrs), [_sds((n, c))], [_rows(c)])[0]


MM_TILES = {"k1024": (2048, 512, 1024), "k4096": (1024, 1024, 1024), "wgrad_tall": (2048, 1024, 512),
            "wgrad_wide": (1024, 2048, 512)}


def _mm(name, a, b, mode, tm, tn, tk, add=None, add_scale=1.0, epilogue=None):
    if mode == "nn":
        (m, k), n = a.shape, b.shape[1]
    elif mode == "nt":
        (m, k), n = a.shape, b.shape[0]
    else:
        (k, m), n = a.shape, b.shape[1]
    nk = k // tk
    dn = {"nn": (((1,), (0,)), ((), ())), "nt": (((1,), (1,)), ((), ())), "tn": (((0,), (0,)), ((), ()))}[mode]

    def body(*refs):
        a_ref, b_ref = refs[:2]
        add_ref = refs[2] if add is not None else None
        o_ref = refs[3] if add is not None else refs[2]
        prod = lax.dot_general(a_ref[...].astype(BF16), b_ref[...].astype(BF16), dn, preferred_element_type=F32)

        def finish(r):
            if epilogue == "relu2":
                r = jnp.maximum(r, 0.0)
                r = r * r
            elif epilogue == "relu2_bwd":
                r = r * (2.0 * jnp.sqrt(add_ref[...]))
            elif add is not None:
                r = r + add_scale * add_ref[...]
            o_ref[...] = r

        if nk == 1:
            finish(prod)
        else:
            acc = refs[-1]
            kk = pl.program_id(2)

            @pl.when(kk == 0)
            def _():
                acc[...] = prod

            @pl.when(kk > 0)
            def _():
                acc[...] += prod

            @pl.when(kk == nk - 1)
            def _():
                finish(acc[...])

    a_spec = pl.BlockSpec((tk, tm), lambda i, j, q: (q, i)) if mode == "tn" else pl.BlockSpec((tm, tk), lambda i, j, q: (i, q))
    b_spec = pl.BlockSpec((tn, tk), lambda i, j, q: (j, q)) if mode == "nt" else pl.BlockSpec((tk, tn), lambda i, j, q: (q, j))
    o_spec = pl.BlockSpec((tm, tn), lambda i, j, q: (i, j))
    ins, specs = [a, b], [a_spec, b_spec]
    if add is not None:
        ins.append(add)
        specs.append(o_spec)
    return pl.pallas_call(body, grid=(m // tm, n // tn, nk), in_specs=specs, out_specs=o_spec, out_shape=_sds((m, n)),
                          scratch_shapes=[pltpu.VMEM((tm, tn), F32)] if nk > 1 else [], name=name,
                          compiler_params=_cp(("parallel", "parallel", "arbitrary")))(*ins)


LERP_BLOCKS = (0, 1, 2, 3, 4, 5, C_LORA // 128, C_VRES // 128)


def _lerp_colmap(j):
    r = jnp.where(j < 6, j, jnp.where(j == 6, C_LORA // 128, C_VRES // 128))
    return (0, r)


def _lerp_fn(f, mu):
    return (f + (_tshift(f, 1) - f) * mu,)


def _lerp_specs():
    return [pl.BlockSpec((SEQ, 128), _lerp_colmap), pl.BlockSpec((1, 128), lambda j: (0, j))]


def lerp_fwd(l, proj, mu):
    return _map_fwd(f"lerp_fwd{l}", _lerp_fn, (8,), [proj, mu], _lerp_specs(), [_sds((SEQ, 1024))],
                    [pl.BlockSpec((SEQ, 128), lambda j: (0, j))])[0]


def lerp_bwd(l, proj, mu, dfl):
    n_in = 2

    def body(f_ref, mu_ref, g_ref, df_ref, dmu_ref):
        _, vjp = jax.vjp(_lerp_fn, f_ref[...], mu_ref[...])
        df, dmu = vjp((g_ref[...],))
        df_ref[...] = df
        dmu_ref[...] = dmu

    cspec = pl.BlockSpec((SEQ, 128), lambda j: (0, j))
    return pl.pallas_call(body, grid=(8,), in_specs=_lerp_specs() + [cspec],
                          out_specs=[cspec, pl.BlockSpec((1, 128), lambda j: (0, j))],
                          out_shape=[_sds((SEQ, 1024)), _sds((1, 1024))], name=f"lerp_bwd{l}",
                          compiler_params=_cp(("parallel",)))(proj, mu, dfl)


def _conv_fn(x, w, b):
    y = x * w[3:4, :] + _tshift(x, 1) * w[2:3, :] + _tshift(x, 2) * w[1:2, :] + _tshift(x, 3) * w[0:1, :] + b
    return (_silu(y),)


def _conv_specs():
    return [pl.BlockSpec((SEQ, 128), lambda j: (0, C_XBC // 128 + j)), pl.BlockSpec((4, 128), lambda j: (0, j)),
            pl.BlockSpec((1, 128), lambda j: (0, j))]


def conv_fwd(l, proj, w, b):
    return _map_fwd(f"conv_fwd{l}", _conv_fn, (6,), [proj, w, b], _conv_specs(), [_sds((SEQ, 768))],
                    [pl.BlockSpec((SEQ, 128), lambda j: (0, j))])[0]


def conv_bwd(l, proj, w, b, dxc):
    def body(x_ref, w_ref, b_ref, g_ref, dx_ref, dw_ref, db_ref):
        _, vjp = jax.vjp(_conv_fn, x_ref[...], w_ref[...], b_ref[...])
        dx, dw, db = vjp((g_ref[...],))
        dx_ref[...] = dx
        dw_ref[...] = dw
        db_ref[...] = db

    cspec = pl.BlockSpec((SEQ, 128), lambda j: (0, j))
    return pl.pallas_call(body, grid=(6,), in_specs=_conv_specs() + [cspec],
                          out_specs=[cspec, pl.BlockSpec((4, 128), lambda j: (0, j)), pl.BlockSpec((1, 128), lambda j: (0, j))],
                          out_shape=[_sds((SEQ, 768)), _sds((4, 768)), _sds((1, 768))], name=f"conv_bwd{l}",
                          compiler_params=_cp(("parallel",)))(proj, w, b, dxc)


def _rwkv_pre_fn(has_vres):
    def fn(fk, fv, flora, *rest):
        if has_vres:
            fvres, vfirst, w0, w2p, a0, a2p, g2p, k_k, k_a, v0, v2p, seg = rest
        else:
            w0, w2p, a0, a2p, g2p, k_k, k_a, seg = rest
        w_log = -_softplus(-(w0 + _dot(jnp.tanh(flora), w2p))) - 0.5
        w = jnp.exp(-jnp.exp(w_log))
        a = _sigmoid(a0 + _dot(flora, a2p))
        g = _dot(_sigmoid(flora), g2p)
        if has_vres:
            v2 = fv + (vfirst - fv) * _sigmoid(v0 + _dot(fvres, v2p))
        else:
            v2 = fv * 1.0
        kk = fk * k_k
        kk = kk / jnp.maximum(jnp.sqrt(_dot(kk * kk, seg)), 1e-12)
        k2 = fk * (1.0 + (a - 1.0) * k_a)
        return w, k2, v2, -kk, kk * a, g

    return fn


def _rwkv_pre_args(fl, vfirst, p, has_vres):
    ins = [fl, fl, fl]
    specs = [_rows(256, 1), _rows(256, 2), _rows(128, 6)]
    if has_vres:
        ins += [fl, vfirst]
        specs += [_rows(128, 7), _rows(256, 2)]
    names = ["w0", "w2p", "a0", "a2p", "g2p", "k_k", "k_a"] + (["v0", "v2p"] if has_vres else []) + ["seg64"]
    for nme in names:
        ins.append(p[nme])
        specs.append(_full(p[nme].shape))
    return ins, specs, names


def rwkv_pre_fwd(l, fl, vfirst, p):
    has_vres = l > 0
    ins, specs, _ = _rwkv_pre_args(fl, vfirst, p, has_vres)
    return _map_fwd(f"rwkv_pre_fwd{l}", _rwkv_pre_fn(has_vres), (SEQ // RB,), ins, specs,
                    [_sds((SEQ, DG))] * 6, [_rows(DG)] * 6)


def rwkv_pre_bwd(l, fl, vfirst, p, cts):
    has_vres = l > 0
    ins, specs, names = _rwkv_pre_args(fl, vfirst, p, has_vres)
    n_row = 5 if has_vres else 3
    want = list(range(n_row)) + [n_row + i for i, nme in enumerate(names) if nme != "seg64"]
    acc = tuple(w for w in want if w >= n_row)
    ct_specs = [[_rows(DG)] * len(g) for g in cts]
    gout = {0: (_sds((SEQ, DG)), _rows(DG)), 1: (_sds((SEQ, DG)), _rows(DG)), 2: (_sds((SEQ, 128)), _rows(128))}
    if has_vres:
        gout[3] = (_sds((SEQ, 128)), _rows(128))
        gout[4] = (_sds((SEQ, DG)), _rows(DG))
    gs = _map_bwd(f"rwkv_pre_bwd{l}", _rwkv_pre_fn(has_vres), (SEQ // RB,), ins, specs, cts, ct_specs, want, acc, gout)
    keys = ["fk", "fv", "flora"] + (["fvres", "vfirst"] if has_vres else []) + [nme for nme in names if nme != "seg64"]
    return dict(zip(keys, gs))


def _rwkv_post_fn(y, fr, k2, v2, g, lnx_w, lnx_b, r_k, seg):
    mu = _dot(y, seg) * (1.0 / HD)
    d = y - mu
    var = _dot(d * d, seg) * (1.0 / HD)
    yn = d * lax.rsqrt(var + GN_EPS) * lnx_w + lnx_b
    bonus = _dot(fr * k2 * r_k, seg) * v2
    return ((yn + bonus) * g,)


def _rwkv_post_args(y, fl, k2, v2, g, p):
    ins = [y, fl, k2, v2, g, p["lnx_w"], p["lnx_b"], p["r_k"], p["seg64"]]
    specs = [_rows(DG), _rows(DG, 0), _rows(DG), _rows(DG), _rows(DG)] + [_full(x.shape) for x in ins[5:]]
    return ins, specs


def rwkv_post_fwd(l, y, fl, k2, v2, g, p):
    ins, specs = _rwkv_post_args(y, fl, k2, v2, g, p)
    return _map_fwd(f"rwkv_post_fwd{l}", _rwkv_post_fn, (SEQ // RB,), ins, specs, [_sds((SEQ, DG))], [_rows(DG)])[0]


def rwkv_post_bwd(l, y, fl, k2, v2, g, p, dya):
    ins, specs = _rwkv_post_args(y, fl, k2, v2, g, p)
    gs = _map_bwd(f"rwkv_post_bwd{l}", _rwkv_post_fn, (SEQ // RB,), ins, specs, [[dya]], [[_rows(DG)]],
                  want=[0, 1, 2, 3, 4, 5, 6, 7], acc=(5, 6, 7), gout={1: (_sds((SEQ, DG)), _rows(DG))})
    return dict(zip(["y", "fr", "k2", "v2", "g", "lnx_w", "lnx_b", "r_k"], gs))


SCAN_TB = 64


def _coltile8(rows8, dmask, ones_stack, parts):
    pieces, rest = [], rows8
    for q in range(parts):
        piece = rest.astype(BF16).astype(F32)
        if q < parts - 1:
            rest = rest - piece
        pieces.append((piece[:, None, :] * dmask[None]).reshape(8 * HD, DG).astype(BF16))
    x = pieces[0] if parts == 1 else jnp.concatenate(pieces, axis=1)
    return jnp.dot(x, ones_stack, preferred_element_type=F32).reshape(8, HD, DG)


def _coltiles_bf16(rows_list, dmask, ones_bf16):
    x = jnp.concatenate([(r8[:, None, :] * dmask[None]).reshape(8 * HD, DG).astype(BF16) for r8 in rows_list], axis=0)
    t = jnp.dot(x, ones_bf16, preferred_element_type=F32)
    return [t[q * 8 * HD:(q + 1) * 8 * HD].reshape(8, HD, DG) for q in range(len(rows_list))]


def _segrows8(x8, dmask, ones_bf16):
    t = jnp.dot(x8.reshape(8 * HD, DG).astype(BF16), ones_bf16, preferred_element_type=F32).reshape(8, HD, DG)
    return jnp.sum(t * dmask[None], axis=1)


def rwkv_scan_fwd(l, fl, w, k2, v2, c, b, p):
    nblk = SEQ // SCAN_TB

    def body(r_ref, w_ref, k_ref, v_ref, c_ref, b_ref, ones_ref, dm_ref, y_ref, st_ref, s_sc):
        @pl.when(pl.program_id(0) == 0)
        def _():
            s_sc[...] = jnp.zeros_like(s_sc)

        ones3, ones = ones_ref[...], ones_ref[0:DG, :]
        dmask = dm_ref[...]

        def group(gi, carry):
            t0 = pl.multiple_of(gi * 8, 8)
            sl = pl.ds(t0, 8)
            v8 = v_ref[sl, :]
            wt = _coltile8(w_ref[sl, :], dmask, ones3, 3)
            ct, bt, kt, rt = _coltiles_bf16([c_ref[sl, :], b_ref[sl, :], k_ref[sl, :], r_ref[sl, :]], dmask, ones)
            t = s_sc[...]
            for j in range(8):
                sa = jnp.sum(t * ct[j], axis=0, keepdims=True)
                t = t * wt[j] + bt[j] * sa + kt[j] * v8[j:j + 1, :]
                st_ref[t0 + j] = t
            s_sc[...] = t
            y_ref[sl, :] = jnp.sum(st_ref[sl] * rt, axis=1)
            return carry

        lax.fori_loop(0, SCAN_TB // 8, group, 0)

    row = pl.BlockSpec((SCAN_TB, DG), lambda i: (i, 0))
    ins = [fl, w, k2, v2, c, b, p["seg64x3_bf16"], p["dmask"]]
    specs = [row] * 6 + [_full((3 * DG, DG)), _full((HD, DG))]
    return pl.pallas_call(body, grid=(nblk,), in_specs=specs,
                          out_specs=[row, pl.BlockSpec((SCAN_TB, HD, DG), lambda i: (i, 0, 0))],
                          out_shape=[_sds((SEQ, DG)), _sds((SEQ, HD, DG))],
                          scratch_shapes=[pltpu.VMEM((HD, DG), F32)], name=f"rwkv_scan_fwd{l}",
                          compiler_params=_cp(("arbitrary",)))(*ins)


def rwkv_scan_bwd(l, fl, w, k2, v2, c, b, states, dy, p):
    nblk = SEQ // SCAN_TB

    def body(r_ref, w_ref, k_ref, v_ref, c_ref, b_ref, dy_ref, st_ref, sp_ref, ones_ref, dm_ref,
             dr_ref, dw_ref, dk_ref, dv_ref, dc_ref, db_ref, g_sc, prev_sc, d8_sc, dsa_sc):
        i = pl.program_id(0)

        @pl.when(i == 0)
        def _():
            g_sc[...] = jnp.zeros_like(g_sc)

        ones3, ones = ones_ref[...], ones_ref[0:DG, :]
        dmask = dm_ref[...]
        first_block = i == nblk - 1

        def group(gr, carry):
            gi = SCAN_TB // 8 - 1 - gr
            t0 = pl.multiple_of(gi * 8, 8)
            sl = pl.ds(t0, 8)
            v8, dy8 = v_ref[sl, :], dy_ref[sl, :]
            t8 = st_ref[sl]
            @pl.when(gi > 0)
            def _():
                prev_sc[0] = st_ref[t0 - 1]

            @pl.when(gi == 0)
            def _():
                prev_sc[0] = jnp.where(first_block, 0.0, sp_ref[0])

            for j in range(1, 8):
                prev_sc[j] = t8[j - 1]
            tp8 = prev_sc[...]
            wt = _coltile8(w_ref[sl, :], dmask, ones3, 3)
            ct, bt, kt, rt = _coltiles_bf16([c_ref[sl, :], b_ref[sl, :], k_ref[sl, :], r_ref[sl, :]], dmask, ones)
            sa8 = jnp.sum(tp8 * ct, axis=1)
            g = g_sc[...]
            for j in range(7, -1, -1):
                g = g + rt[j] * dy8[j:j + 1, :]
                d8_sc[j] = g
                dsa = jnp.sum(g * bt[j], axis=0, keepdims=True)
                dsa_sc[j:j + 1, :] = dsa
                g = g * wt[j] + ct[j] * dsa
            g_sc[...] = g
            d8 = d8_sc[...]
            dsa8 = dsa_sc[...]
            dv_ref[sl, :] = jnp.sum(d8 * kt, axis=1)
            dr_ref[sl, :] = _segrows8(t8 * dy8[:, None, :], dmask, ones)
            dk_ref[sl, :] = _segrows8(d8 * v8[:, None, :], dmask, ones)
            dw_ref[sl, :] = _segrows8(tp8 * d8, dmask, ones)
            db_ref[sl, :] = _segrows8(d8 * sa8[:, None, :], dmask, ones)
            dc_ref[sl, :] = _segrows8(tp8 * dsa8[:, None, :], dmask, ones)
            return carry

        lax.fori_loop(0, SCAN_TB // 8, group, 0)

    row = pl.BlockSpec((SCAN_TB, DG), lambda i: (nblk - 1 - i, 0))
    st_spec = pl.BlockSpec((SCAN_TB, HD, DG), lambda i: (nblk - 1 - i, 0, 0))
    sp_spec = pl.BlockSpec((1, HD, DG), lambda i: (jnp.maximum((nblk - 1 - i) * SCAN_TB - 1, 0), 0, 0))
    ins = [fl, w, k2, v2, c, b, dy, states, states, p["seg64x3_bf16"], p["dmask"]]
    specs = [row] * 7 + [st_spec, sp_spec, _full((3 * DG, DG)), _full((HD, DG))]
    tile8 = pltpu.VMEM((8, HD, DG), F32)
    return pl.pallas_call(body, grid=(nblk,), in_specs=specs, out_specs=[row] * 6, out_shape=[_sds((SEQ, DG))] * 6,
                          scratch_shapes=[pltpu.VMEM((HD, DG), F32), tile8, tile8, pltpu.VMEM((8, DG), F32)],
                          name=f"rwkv_scan_bwd{l}", compiler_params=_cp(("arbitrary",)))(*ins)


HG_ROWS = 128


def _hgrn_chunk_fn(layer):
    def fn(hq, hf, hi, hg, sprev, lb0, lb1, norm_w, seg, bd, tri, causal, ones16):
        e0 = jnp.exp(lb0 - jnp.maximum(lb0, lb1))
        e1 = jnp.exp(lb1 - jnp.maximum(lb0, lb1))
        sm0, sm1 = e0 / (e0 + e1), e1 / (e0 + e1)
        lb = (sm0 - sm0) if layer == 0 else ((sm0 + sm1) - sm0)
        forget = lb + (1.0 - lb) * _sigmoid(hf)
        logf = jnp.log(forget)
        kk = 1.0 - forget
        q = _silu(hq)
        c = HGRN_CHUNK
        b = _dotx(tri, logf)
        bl = jnp.sum(logf, axis=0, keepdims=True)
        diff = (b[:, None, :] - b[None, :, :]).reshape(c * c, DG)
        dec = jnp.exp(jnp.where(causal > 0.5, diff, -1e30))
        qrep = jnp.broadcast_to(q[:, None, :], (c, c, DG)).reshape(c * c, DG)
        ktil = jnp.broadcast_to(kk[None, :, :], (c, c, DG)).reshape(c * c, DG)
        vtil = jnp.broadcast_to(hi[None, :, :], (c, c, DG)).reshape(c * c, DG)
        att = _dot(qrep * ktil * dec, seg)
        o_intra = jnp.sum((att * vtil).reshape(c, c, DG), axis=1)
        kdec = kk * jnp.exp(bl - b)
        u = _dot_tn(hi, kdec) * bd
        snext = sprev * jnp.exp(bl) + u
        o = o_intra + _dot_nt(q * jnp.exp(b), sprev)
        ms = _dot(o * o, seg) * (1.0 / HD)
        y = o * lax.rsqrt(ms + RMS_EPS) * norm_w * _silu(hg)
        return y, snext

    return fn


def _hgrn_consts(p):
    return [p["seg64"], p["seg64"], p["tri16"], p["causal16"], p["ones16"]]


def hgrn_fwd(l, proj, p):
    fn = _hgrn_chunk_fn(l)
    nch = HG_ROWS // HGRN_CHUNK

    def body(hq_ref, hf_ref, hi_ref, hg_ref, lb0_ref, lb1_ref, nw_ref, seg_ref, bd_ref, tri_ref, cau_ref, o16_ref,
             y_ref, st_ref, s_sc):
        @pl.when(pl.program_id(0) == 0)
        def _():
            s_sc[...] = jnp.zeros_like(s_sc)

        consts = (lb0_ref[...], lb1_ref[...], nw_ref[...], seg_ref[...], bd_ref[...], tri_ref[...], cau_ref[...],
                  o16_ref[...])

        def chunk(ci, carry):
            sl = pl.ds(pl.multiple_of(ci * HGRN_CHUNK, HGRN_CHUNK), HGRN_CHUNK)
            sprev = s_sc[...]
            st_ref[ci] = sprev
            y, snext = fn(hq_ref[sl, :], hf_ref[sl, :], hi_ref[sl, :], hg_ref[sl, :], sprev, *consts)
            y_ref[sl, :] = y
            s_sc[...] = snext
            return carry

        lax.fori_loop(0, nch, chunk, 0, unroll=2)

    rows = lambda cb: pl.BlockSpec((HG_ROWS, DG), lambda i: (i, cb))
    ins = [proj, proj, proj, proj, p["lb0"], p["lb1"], p["hgrn_norm_w"]] + _hgrn_consts(p)
    specs = [rows(C_HQ // DG), rows(C_HF // DG), rows(C_HI // DG), rows(C_HG // DG)] + [_full(x.shape) for x in ins[4:]]
    return pl.pallas_call(body, grid=(SEQ // HG_ROWS,), in_specs=specs,
                          out_specs=[rows(0), pl.BlockSpec((nch, DG, DG), lambda i: (i, 0, 0))],
                          out_shape=[_sds((SEQ, DG)), _sds((SEQ // HGRN_CHUNK, DG, DG))],
                          scratch_shapes=[pltpu.VMEM((DG, DG), F32)], name=f"hgrn_fwd{l}",
                          compiler_params=_cp(("arbitrary",)))(*ins)


def hgrn_bwd(l, proj, states, dy, p):
    fn = _hgrn_chunk_fn(l)
    nch = HG_ROWS // HGRN_CHUNK
    nblk = SEQ // HG_ROWS

    def body(hq_ref, hf_ref, hi_ref, hg_ref, st_ref, dy_ref, lb0_ref, lb1_ref, nw_ref, seg_ref, bd_ref, tri_ref,
             cau_ref, o16_ref, dp_ref, dlb0_ref, dlb1_ref, dnw_ref, ds_sc):
        @pl.when(pl.program_id(0) == 0)
        def _():
            ds_sc[...] = jnp.zeros_like(ds_sc)
            dlb0_ref[...] = jnp.zeros_like(dlb0_ref)
            dlb1_ref[...] = jnp.zeros_like(dlb1_ref)
            dnw_ref[...] = jnp.zeros_like(dnw_ref)

        consts = (seg_ref[...], bd_ref[...], tri_ref[...], cau_ref[...], o16_ref[...])

        def chunk(cr, carry):
            ci = nch - 1 - cr
            sl = pl.ds(pl.multiple_of(ci * HGRN_CHUNK, HGRN_CHUNK), HGRN_CHUNK)
            f = lambda hq, hf, hi, hg, sp, b0, b1, nw: fn(hq, hf, hi, hg, sp, b0, b1, nw, *consts)
            _, vjp = jax.vjp(f, hq_ref[sl, :], hf_ref[sl, :], hi_ref[sl, :], hg_ref[sl, :], st_ref[ci],
                             lb0_ref[...], lb1_ref[...], nw_ref[...])
            dhq, dhf, dhi, dhg, dsp, dlb0, dlb1, dnw = vjp((dy_ref[sl, :], ds_sc[...]))
            dp_ref[sl, 0:DG] = dhq
            dp_ref[sl, DG:2 * DG] = dhf
            dp_ref[sl, 2 * DG:3 * DG] = dhi
            dp_ref[sl, 3 * DG:4 * DG] = dhg
            ds_sc[...] = dsp
            dlb0_ref[...] += dlb0
            dlb1_ref[...] += dlb1
            dnw_ref[...] += dnw
            return carry

        lax.fori_loop(0, nch, chunk, 0, unroll=2)

    rows = lambda cb: pl.BlockSpec((HG_ROWS, DG), lambda i: (nblk - 1 - i, cb))
    ins = [proj, proj, proj, proj, states, dy, p["lb0"], p["lb1"], p["hgrn_norm_w"]] + _hgrn_consts(p)
    specs = [rows(C_HQ // DG), rows(C_HF // DG), rows(C_HI // DG), rows(C_HG // DG),
             pl.BlockSpec((nch, DG, DG), lambda i: (nblk - 1 - i, 0, 0)), rows(0)] + [_full(x.shape) for x in ins[6:]]
    return pl.pallas_call(body, grid=(nblk,), in_specs=specs,
                          out_specs=[pl.BlockSpec((HG_ROWS, 4 * DG), lambda i: (nblk - 1 - i, 0)), _full((1, DG)),
                                     _full((1, DG)), _full((1, DG))],
                          out_shape=[_sds((SEQ, 4 * DG)), _sds((1, DG)), _sds((1, DG)), _sds((1, DG))],
                          scratch_shapes=[pltpu.VMEM((DG, DG), F32)], name=f"hgrn_bwd{l}",
                          compiler_params=_cp(("arbitrary",)))(*ins)


def _ssd_chunk_fn(z, xs, bm, cm, dtr, sprev, dt_bias, a_log, d_par, norm_w, e128, tri, trit, seg128, ones128):
    lc = SSD_CHUNK
    dt = _softplus(dtr + dt_bias)
    a = -jnp.exp(a_log)
    da = dt * a * (lax.broadcasted_iota(jnp.int32, (1, 128), 1) < NH).astype(F32)
    cs = _dotx(tri, da)
    cst = _dotx_tn(da, trit)
    cs_b = _dotx(cs, e128)
    dt_b = _dotx(dt, e128)
    csl_b = _dotx(jnp.sum(da, axis=0, keepdims=True), e128)
    xdt = xs * dt_b
    lane = lax.broadcasted_iota(jnp.int32, (1, DG), 1)
    rowi = lax.broadcasted_iota(jnp.int32, (lc, lc), 0)
    coli = lax.broadcasted_iota(jnp.int32, (lc, lc), 1)
    y = jnp.zeros((lc, DG), F32)
    snew = jnp.zeros((DG, SSD_N), F32)
    d_b = jnp.zeros((1, DG), F32)
    wdec = xdt * jnp.exp(csl_b - cs_b)
    for g in range(2):
        bg = bm[:, g * SSD_N:(g + 1) * SSD_N]
        cg = cm[:, g * SSD_N:(g + 1) * SSD_N]
        gmat = _dot_nt(cg, bg)
        gmask = ((lane // 128) == g).astype(F32)
        snew = snew + _dot_tn(wdec * gmask, bg)
        y = y + _dot_nt(cg, sprev) * gmask * jnp.exp(cs_b)
        for hh in range(2):
            h = 2 * g + hh
            seg = jnp.where(rowi >= coli, cs[:, h:h + 1] - cst[h:h + 1, :], -1e30)
            hmask = ((lane // HD) == h).astype(F32)
            y = y + _dot(gmat * jnp.exp(seg), xdt * hmask)
            d_b = d_b + d_par[:, h:h + 1] * hmask
    cd = jnp.exp(_dotx_tn(_dotx(da, e128), ones128))
    snext = sprev * cd + snew
    y = y + xs * d_b
    y = y * _silu(z)
    ms = _dot(y * y, seg128) * (1.0 / 128.0)
    return y * lax.rsqrt(ms + RMS_EPS) * norm_w, snext


def ssd_fwd(l, proj, xc, p):
    nc = SEQ // SSD_CHUNK

    def body(z_ref, xs_ref, b_ref, c_ref, dt_ref, dtb_ref, al_ref, d_ref, nw_ref, e_ref, tri_ref, trit_ref, sg_ref,
             on_ref, y_ref, st_ref, s_sc):
        @pl.when(pl.program_id(0) == 0)
        def _():
            s_sc[...] = jnp.zeros_like(s_sc)

        sprev = s_sc[...]
        st_ref[0] = sprev
        y, snext = _ssd_chunk_fn(z_ref[...], xs_ref[...], b_ref[...], c_ref[...], dt_ref[...], sprev, dtb_ref[...],
                                 al_ref[...], d_ref[...], nw_ref[...], e_ref[...], tri_ref[...], trit_ref[...],
                                 sg_ref[...], on_ref[...])
        y_ref[...] = y
        s_sc[...] = snext

    rw = lambda w, cb: pl.BlockSpec((SSD_CHUNK, w), lambda i: (i, cb))
    ins = [proj, xc, xc, xc, proj, p["dt_bias"], p["a_log"], p["ssd_d"], p["ssd_norm_w"], p["e128"], p["tri128"],
           p["tri128t"], p["seg128"], p["ones128"]]
    specs = [rw(DG, C_Z // DG), rw(DG, 0), rw(DG, 1), rw(DG, 2), rw(128, C_DT // 128)] + [_full(x.shape) for x in ins[5:]]
    return pl.pallas_call(body, grid=(nc,), in_specs=specs,
                          out_specs=[rw(DG, 0), pl.BlockSpec((1, DG, SSD_N), lambda i: (i, 0, 0))],
                          out_shape=[_sds((SEQ, DG)), _sds((nc, DG, SSD_N))],
                          scratch_shapes=[pltpu.VMEM((DG, SSD_N), F32)], name=f"ssd_fwd{l}",
                          compiler_params=_cp(("arbitrary",)))(*ins)


def ssd_bwd(l, proj, xc, states, dy, p):
    nc = SEQ // SSD_CHUNK

    def body(z_ref, xs_ref, b_ref, c_ref, dt_ref, st_ref, dy_ref, dtb_ref, al_ref, d_ref, nw_ref, e_ref, tri_ref,
             trit_ref, sg_ref, on_ref, dz_ref, dxc_ref, ddt_ref, ddtb_ref, dal_ref, dd_ref, dnw_ref, ds_sc):
        @pl.when(pl.program_id(0) == 0)
        def _():
            ds_sc[...] = jnp.zeros_like(ds_sc)
            ddtb_ref[...] = jnp.zeros_like(ddtb_ref)
            dal_ref[...] = jnp.zeros_like(dal_ref)
            dd_ref[...] = jnp.zeros_like(dd_ref)
            dnw_ref[...] = jnp.zeros_like(dnw_ref)

        consts = (e_ref[...], tri_ref[...], trit_ref[...], sg_ref[...], on_ref[...])
        f = lambda *a: _ssd_chunk_fn(*a, *consts)
        _, vjp = jax.vjp(f, z_ref[...], xs_ref[...], b_ref[...], c_ref[...], dt_ref[...], st_ref[0], dtb_ref[...],
                         al_ref[...], d_ref[...], nw_ref[...])
        dz, dxs, db, dc, ddt, dsp, ddtb, dal, dd, dnw = vjp((dy_ref[...], ds_sc[...]))
        dz_ref[...] = dz
        dxc_ref[:, 0:DG] = dxs
        dxc_ref[:, DG:2 * DG] = db
        dxc_ref[:, 2 * DG:3 * DG] = dc
        ddt_ref[...] = ddt
        ds_sc[...] = dsp
        ddtb_ref[...] += ddtb
        dal_ref[...] += dal
        dd_ref[...] += dd
        dnw_ref[...] += dnw

    rw = lambda w, cb: pl.BlockSpec((SSD_CHUNK, w), lambda i: (nc - 1 - i, cb))
    ins = [proj, xc, xc, xc, proj, states, dy, p["dt_bias"], p["a_log"], p["ssd_d"], p["ssd_norm_w"], p["e128"],
           p["tri128"], p["tri128t"], p["seg128"], p["ones128"]]
    specs = [rw(DG, C_Z // DG), rw(DG, 0), rw(DG, 1), rw(DG, 2), rw(128, C_DT // 128),
             pl.BlockSpec((1, DG, SSD_N), lambda i: (nc - 1 - i, 0, 0)), rw(DG, 0)] + [_full(x.shape) for x in ins[7:]]
    return pl.pallas_call(body, grid=(nc,), in_specs=specs,
                          out_specs=[rw(DG, 0), rw(3 * DG, 0), rw(128, 0), _full((1, 128)), _full((1, 128)), _full((1, 128)),
                                     _full((1, DG))],
                          out_shape=[_sds((SEQ, DG)), _sds((SEQ, 3 * DG)), _sds((SEQ, 128)), _sds((1, 128)), _sds((1, 128)),
                                     _sds((1, 128)), _sds((1, DG))],
                          scratch_shapes=[pltpu.VMEM((DG, SSD_N), F32)], name=f"ssd_bwd{l}",
                          compiler_params=_cp(("arbitrary",)))(*ins)


ATT_BLK = 128


def _att_scores(qn, kc, kp, h, dil, has_prev):
    i = lax.broadcasted_iota(jnp.int32, (ATT_BLK, ATT_BLK), 0)
    j = lax.broadcasted_iota(jnp.int32, (ATT_BLK, ATT_BLK), 1)
    slope = 2.0 ** (-8.0 * (h + 1) / NH)
    scale = HD ** -0.5
    s_c = _dot_nt(qn, kc) * scale - slope * ((i - j) * dil).astype(F32)
    s_p = _dot_nt(qn, kp) * scale - slope * ((ATT_BLK + i - j) * dil).astype(F32)
    m_c = j <= i
    m_p = jnp.logical_and(j >= i, has_prev)
    return jnp.where(m_c, s_c, -1e30), jnp.where(m_p, s_p, -1e30), m_c, m_p


def _sub_spec(ln, width, col):
    return pl.BlockSpec((ln, DG), lambda z: (0, z * (width // DG) + col // DG))


QKV_W = 3 * DG


def attn_branch_fwd(l, bi, qkv, dil):
    ln = SEQ // dil
    nb = ln // ATT_BLK

    def body(q_ref, k_ref, v_ref, o_ref, l_ref):
        def blk(n, carry):
            r0 = pl.multiple_of(n * ATT_BLK, ATT_BLK)
            rp = pl.multiple_of(jnp.maximum(n - 1, 0) * ATT_BLK, ATT_BLK)
            cur, prv = pl.ds(r0, ATT_BLK), pl.ds(rp, ATT_BLK)
            for h in range(NH):
                hs = slice(h * HD, (h + 1) * HD)
                qn, kc, vc, kp, vp = q_ref[cur, hs], k_ref[cur, hs], v_ref[cur, hs], k_ref[prv, hs], v_ref[prv, hs]
                s_c, s_p, m_c, m_p = _att_scores(qn, kc, kp, h, dil, n > 0)
                m = jnp.maximum(jnp.max(s_c, axis=1, keepdims=True), jnp.max(s_p, axis=1, keepdims=True))
                p_c = jnp.where(m_c, jnp.exp(s_c - m), 0.0)
                p_p = jnp.where(m_p, jnp.exp(s_p - m), 0.0)
                den = jnp.sum(p_c, axis=1, keepdims=True) + jnp.sum(p_p, axis=1, keepdims=True)
                o_ref[cur, hs] = (_dot(p_c, vc) + _dot(p_p, vp)) / den
                l_ref[cur, hs] = jnp.broadcast_to(m + jnp.log(den), (ATT_BLK, HD))
            return carry

        lax.fori_loop(0, nb, blk, 0)

    pv = qkv.reshape(ln, dil * QKV_W)
    out = pl.BlockSpec((ln, DG), lambda z: (0, z))
    o, lse = pl.pallas_call(body, grid=(dil,), in_specs=[_sub_spec(ln, QKV_W, 0), _sub_spec(ln, QKV_W, DG), _sub_spec(ln, QKV_W, 2 * DG)],
                            out_specs=[out, out], out_shape=[_sds((ln, dil * DG))] * 2, name=f"attn_fwd{l}_{bi}",
                            compiler_params=_cp(("parallel",)))(pv, pv, pv)
    return o.reshape(SEQ, DG), lse.reshape(SEQ, DG)


def attn_branch_bwd(l, bi, qkv, dil, dyb, lse_all, delta):
    ln = SEQ // dil
    nb = ln // ATT_BLK
    scale = HD ** -0.5

    def body(q_ref, k_ref, v_ref, do_ref, l_ref, dl_ref, dq_ref, dk_ref, dv_ref):
        dk_ref[...] = jnp.zeros_like(dk_ref)
        dv_ref[...] = jnp.zeros_like(dv_ref)

        def blk(n, carry):
            r0 = pl.multiple_of(n * ATT_BLK, ATT_BLK)
            rp = pl.multiple_of(jnp.maximum(n - 1, 0) * ATT_BLK, ATT_BLK)
            cur, prv = pl.ds(r0, ATT_BLK), pl.ds(rp, ATT_BLK)
            for h in range(NH):
                hs = slice(h * HD, (h + 1) * HD)
                qn, don = q_ref[cur, hs], do_ref[cur, hs]
                lse, dlt = l_ref[cur, h * HD:h * HD + 1], dl_ref[cur, h * HD:h * HD + 1]
                kc, vc, kp, vp = k_ref[cur, hs], v_ref[cur, hs], k_ref[prv, hs], v_ref[prv, hs]
                s_c, s_p, m_c, m_p = _att_scores(qn, kc, kp, h, dil, n > 0)
                p_c = jnp.where(m_c, jnp.exp(s_c - lse), 0.0)
                p_p = jnp.where(m_p, jnp.exp(s_p - lse), 0.0)
                ds_c = p_c * (_dot_nt(don, vc) - dlt)
                ds_p = p_p * (_dot_nt(don, vp) - dlt)
                dq_ref[cur, hs] = (_dot(ds_c, kc) + _dot(ds_p, kp)) * scale
                dv_ref[prv, hs] += _dot_tn(p_p, don)
                dk_ref[prv, hs] += _dot_tn(ds_p, qn) * scale
                dv_ref[cur, hs] += _dot_tn(p_c, don)
                dk_ref[cur, hs] += _dot_tn(ds_c, qn) * scale
            return carry

        lax.fori_loop(0, nb, blk, 0)

    pv = qkv.reshape(ln, dil * QKV_W)
    sub = lambda t: t.reshape(ln, dil * DG)
    row = pl.BlockSpec((ln, DG), lambda z: (0, z))
    outs = pl.pallas_call(body, grid=(dil,),
                          in_specs=[_sub_spec(ln, QKV_W, 0), _sub_spec(ln, QKV_W, DG), _sub_spec(ln, QKV_W, 2 * DG), row, row, row],
                          out_specs=[row] * 3, out_shape=[_sds((ln, dil * DG))] * 3, name=f"attn_bwd{l}_{bi}",
                          compiler_params=_cp(("parallel",)))(pv, pv, pv, sub(dyb), sub(lse_all), sub(delta))
    return [t.reshape(SEQ, DG) for t in outs]


def _attn_merge_fn(o1, o2, o3, l1, l2, l3):
    m = jnp.maximum(jnp.maximum(l1, l2), l3)
    w1, w2, w3 = jnp.exp(l1 - m), jnp.exp(l2 - m), jnp.exp(l3 - m)
    den = w1 + w2 + w3
    return (w1 * o1 + w2 * o2 + w3 * o3) / den, m + jnp.log(den)


def attn_merge(l, os_, ls_):
    ins = list(os_) + list(ls_)
    return _map_fwd(f"attn_merge{l}", _attn_merge_fn, (SEQ // RB,), ins, [_rows(DG)] * 6, [_sds((SEQ, DG))] * 2,
                    [_rows(DG)] * 2)


def attn_delta(l, dyb, yb, seg):
    fn = lambda d, y, s: (_dot(d * y, s),)
    return _map_fwd(f"attn_delta{l}", fn, (SEQ // RB,), [dyb, yb, seg], [_rows(DG), _rows(DG), _full((DG, DG))],
                    [_sds((SEQ, DG))], [_rows(DG)])[0]


def _ln_fn(x, mix, w, b):
    h = ALPHA * x + mix
    mu = jnp.mean(h, axis=-1, keepdims=True)
    d = h - mu
    var = jnp.mean(d * d, axis=-1, keepdims=True)
    return (d * lax.rsqrt(var + LN_EPS) * w + b,)


def ln_fwd(name, x, mix, w, b):
    specs = [_rows(D_MODEL), _rows(D_MODEL), _full((1, D_MODEL)), _full((1, D_MODEL))]
    return _map_fwd(name, _ln_fn, (SEQ // RB,), [x, mix, w, b], specs, [_sds((SEQ, D_MODEL))], [_rows(D_MODEL)])[0]


def ln_bwd(name, x, mix, w, b, dy):
    specs = [_rows(D_MODEL), _rows(D_MODEL), _full((1, D_MODEL)), _full((1, D_MODEL))]
    return _map_bwd(name, _ln_fn, (SEQ // RB,), [x, mix, w, b], specs, [[dy]], [[_rows(D_MODEL)]], want=[1, 2, 3],
                    acc=(2, 3))


def loss_call(y, tgt):
    def fn(yy, tt):
        e = yy - tt
        part = 0.5 * jnp.sum(jnp.sum(e * e, axis=-1, keepdims=True) * (1.0 / D_MODEL), axis=0, keepdims=True)
        return e * (1.0 / D_MODEL), jnp.broadcast_to(part, (8, 128))

    return _map_fwd("loss", fn, (SEQ // RB,), [y, tgt], [_rows(D_MODEL)] * 2,
                    [_sds((SEQ, D_MODEL)), _sds((SEQ // RB * 8, 128))],
                    [_rows(D_MODEL), pl.BlockSpec((8, 128), lambda i: (i, 0))])


def layer_fwd(l, x, vfirst, wts, p):
    sv = {"x": x}
    proj = _mm(f"mm_in{l}", x, wts["w_in"], "nn", *MM_TILES["k1024"])
    fl = lerp_fwd(l, proj, p["mu"])
    xc = conv_fwd(l, proj, p["conv_w"], p["conv_b"])
    w, k2, v2, c, b, g = rwkv_pre_fwd(l, fl, vfirst, p)
    y_scan, states = rwkv_scan_fwd(l, fl, w, k2, v2, c, b, p)
    ya = rwkv_post_fwd(l, y_scan, fl, k2, v2, g, p)
    qkv = proj[:, C_AQ:C_AQ + 3 * DG]
    outs, lses = [], []
    for bi, (win, dil) in enumerate(DILATED):
        o, lse = attn_branch_fwd(l, bi, qkv, dil)
        outs.append(o)
        lses.append(lse)
    yb, lse_all = attn_merge(l, outs, lses)
    yc, ssd_states = ssd_fwd(l, proj, xc, p)
    yd, hg_states = hgrn_fwd(l, proj, p)
    ycat = jnp.concatenate([ya, yb, yc, yd], axis=1)
    mix = _mm(f"mm_out{l}", ycat, wts["w_out"], "nn", *MM_TILES["k1024"])
    x1 = ln_fwd(f"ln1_fwd{l}", x, mix, p["ln1_w"], p["ln1_b"])
    hh = _mm(f"mm_up{l}", x1, wts["w_up_t"], "nt", *MM_TILES["k1024"], epilogue="relu2")
    m2 = _mm(f"mm_down{l}", hh, wts["w_down"], "nn", *MM_TILES["k4096"])
    x2 = ln_fwd(f"ln2_fwd{l}", x1, m2, p["ln2_w"], p["ln2_b"])
    sv.update(proj=proj, fl=fl, xc=xc, w=w, k2=k2, v2=v2, c=c, b=b, g=g, y_scan=y_scan, states=states,
              yb=yb, lse_all=lse_all, ssd_states=ssd_states, hg_states=hg_states, ycat=ycat, mix=mix, x1=x1, hh=hh, qkv=qkv,
              m2=m2, vfirst=vfirst)
    return x2, sv


def layer_bwd(l, dx2, dvfirst_next, sv, wts, p):
    gr = {}
    x, x1, proj, fl = sv["x"], sv["x1"], sv["proj"], sv["fl"]
    dres2, gr["ln2_w"], gr["ln2_b"] = ln_bwd(f"ln2_bwd{l}", x1, sv["m2"], p["ln2_w"], p["ln2_b"], dx2)
    du = _mm(f"mm_down_dx{l}", dres2, wts["w_down"], "nt", *MM_TILES["k1024"], add=sv["hh"], epilogue="relu2_bwd")
    gr["w_down"] = _mm(f"mm_down_dw{l}", sv["hh"], dres2, "tn", *MM_TILES["wgrad_tall"])
    dx1 = _mm(f"mm_up_dx{l}", du, wts["w_up_t"], "nn", *MM_TILES["k4096"], add=dres2, add_scale=ALPHA)
    gr["w_up_t"] = _mm(f"mm_up_dw{l}", du, x1, "tn", *MM_TILES["wgrad_tall"])
    dres1, gr["ln1_w"], gr["ln1_b"] = ln_bwd(f"ln1_bwd{l}", x, sv["mix"], p["ln1_w"], p["ln1_b"], dx1)
    dycat = _mm(f"mm_out_dx{l}", dres1, wts["w_out"], "nt", *MM_TILES["k1024"])
    gr["w_out"] = _mm(f"mm_out_dw{l}", sv["ycat"], dres1, "tn", 1024, 1024, 512)
    dya, dyb, dyc, dyd = (dycat[:, i * DG:(i + 1) * DG] for i in range(4))
    dhg4, gr["lb0"], gr["lb1"], gr["hgrn_norm_w"] = hgrn_bwd(l, proj, sv["hg_states"], dyd, p)
    dz, dxc, ddt, gr["dt_bias"], gr["a_log"], gr["ssd_d"], gr["ssd_norm_w"] = ssd_bwd(l, proj, sv["xc"], sv["ssd_states"], dyc, p)
    dxbc, gr["conv_w"], gr["conv_b"] = conv_bwd(l, proj, p["conv_w"], p["conv_b"], dxc)
    delta = attn_delta(l, dyb, sv["yb"], p["seg64"])
    dqs, dks, dvs = [], [], []
    for bi, (win, dil) in enumerate(DILATED):
        dq, dk, dv = attn_branch_bwd(l, bi, sv["qkv"], dil, dyb, sv["lse_all"], delta)
        dqs.append(dq)
        dks.append(dk)
        dvs.append(dv)
    dq_a, dk_a, dv_a = _addn(f"attn_dq{l}", *dqs), _addn(f"attn_dk{l}", *dks), _addn(f"attn_dv{l}", *dvs)
    pg = rwkv_post_bwd(l, sv["y_scan"], fl, sv["k2"], sv["v2"], sv["g"], p, dya)
    gr["lnx_w"], gr["lnx_b"], gr["r_k"] = pg["lnx_w"], pg["lnx_b"], pg["r_k"]
    dr, dw, dk, dv, dc, db = rwkv_scan_bwd(l, fl, sv["w"], sv["k2"], sv["v2"], sv["c"], sv["b"], sv["states"], pg["y"], p)
    v2_cts = [dv, pg["v2"]] + ([dvfirst_next] if dvfirst_next is not None else [])
    qg = rwkv_pre_bwd(l, fl, sv["vfirst"], p, [[dw], [dk, pg["k2"]], v2_cts, [dc], [db], [pg["g"]]])
    for nme in ("w0", "w2p", "a0", "a2p", "g2p", "k_k", "k_a", "v0", "v2p"):
        if nme in qg:
            gr[nme] = qg[nme]
    dfr = _addn(f"rwkv_dr{l}", dr, pg["fr"])
    dvres = qg["fvres"] if l > 0 else jnp.zeros((SEQ, 128), F32)
    dfl_out = jnp.concatenate([dfr, qg["fk"], qg["fv"], qg["flora"], dvres], axis=1)
    dfl_in, gr["mu"] = lerp_bwd(l, proj, p["mu"], dfl_out)
    dproj = jnp.concatenate([dfl_in[:, 0:768], dq_a, dk_a, dv_a, dz, dxbc, dhg4, dfl_in[:, 768:896], ddt,
                             dfl_in[:, 896:1024], jnp.zeros((SEQ, 128), F32)], axis=1)
    dx = _mm(f"mm_in_dx{l}", dproj, wts["w_in"], "nt", *MM_TILES["k4096"], add=dres1, add_scale=ALPHA)
    gr["w_in"] = _mm(f"mm_in_dw{l}", x, dproj, "tn", *MM_TILES["wgrad_wide"])
    return dx, (qg["vfirst"] if l > 0 else None), gr


def _w_in_pad(w_in_l, w_vres):
    rows = w_in_l.shape[0]
    z = lambda n: jnp.zeros((rows, n), w_in_l.dtype)
    vres = z(128) if w_vres is None else jnp.concatenate([w_vres, z(96)], axis=1)
    return jnp.concatenate([w_in_l[:, 0:768], w_in_l[:, 896:1664], w_in_l[:, 1664:1920], w_in_l[:, 1920:2688],
                            w_in_l[:, 2692:3716], w_in_l[:, 768:896], w_in_l[:, 2688:2692], z(124), vres, z(128)], axis=1)


def _w_in_unpad(g):
    g_in = jnp.concatenate([g[:, 0:768], g[:, C_LORA:C_LORA + 128], g[:, 768:1536], g[:, C_Z:C_Z + 256],
                            g[:, C_XBC:C_XBC + 768], g[:, C_DT:C_DT + 4], g[:, C_HQ:C_HQ + 1024]], axis=1)
    return g_in, g[:, C_VRES:C_VRES + 32]


def _consts():
    i16 = jnp.arange(HGRN_CHUNK)
    pair = jnp.arange(HGRN_CHUNK * HGRN_CHUNK)
    i128 = jnp.arange(128)
    seg64 = _seg_ones(DG, HD)
    tri128 = (i128[:, None] >= i128[None, :]).astype(F32)
    return dict(
        seg64=seg64, seg64x3_bf16=jnp.concatenate([seg64, seg64, seg64], axis=0).astype(BF16),
        dmask=(jnp.arange(HD)[:, None] == (jnp.arange(DG)[None, :] % HD)).astype(F32),
        tri16=(i16[:, None] >= i16[None, :]).astype(F32),
        causal16=jnp.broadcast_to(((pair // HGRN_CHUNK) >= (pair % HGRN_CHUNK)).astype(F32)[:, None], (256, DG)),
        ones16=jnp.ones((HGRN_CHUNK, DG), F32),
        e128=((i128[:, None] == (jnp.arange(DG)[None, :] // HD)) & (i128[:, None] < NH)).astype(F32),
        tri128=tri128, tri128t=tri128.T, seg128=_seg_ones(DG, 128), ones128=jnp.ones((128, 128), F32))


def _pad_lanes(v, n):
    return jnp.concatenate([v, jnp.zeros((n - v.shape[0],), v.dtype)])[None, :]


def _layer_params(l, raw, consts):
    p = dict(consts)
    row = lambda name: raw[name][l][None, :]
    z = lambda r: jnp.zeros((r, DG), F32)
    mu_vres = raw["mu_vres"][l - 1] if l > 0 else jnp.zeros((32,), F32)
    p["mu"] = jnp.concatenate([raw["mu_shift"][l], mu_vres, jnp.zeros((96,), F32)])[None, :]
    p["conv_w"], p["conv_b"] = raw["ssd_conv_w"][l], row("ssd_conv_b")
    p["w0"], p["a0"], p["k_k"], p["k_a"] = row("rwkv_w0"), row("rwkv_a0"), row("rwkv_k_k"), row("rwkv_k_a")
    p["lnx_w"], p["lnx_b"] = row("rwkv_lnx_w"), row("rwkv_lnx_b")
    p["r_k"] = raw["rwkv_r_k"][l].reshape(1, DG)
    p["w2p"] = jnp.concatenate([raw["rwkv_w2"][l], z(96)], axis=0)
    p["a2p"] = jnp.concatenate([z(32), raw["rwkv_a2"][l], z(64)], axis=0)
    p["g2p"] = jnp.concatenate([z(64), raw["rwkv_g2"][l]], axis=0)
    if l > 0:
        p["v0"] = raw["rwkv_v0"][l - 1][None, :]
        p["v2p"] = jnp.concatenate([raw["rwkv_v2"][l - 1], z(96)], axis=0)
    p["lb0"], p["lb1"] = raw["lower_bounds"][0:1], raw["lower_bounds"][1:2]
    p["hgrn_norm_w"], p["ssd_norm_w"] = row("hgrn_norm_w"), row("ssd_norm_w")
    p["dt_bias"], p["a_log"], p["ssd_d"] = (_pad_lanes(raw[n][l], 128) for n in ("ssd_dt_bias", "ssd_A_log", "ssd_D"))
    for n in ("ln1_w", "ln1_b", "ln2_w", "ln2_b"):
        p[n] = row(n)
    return p


def _natural_grads(g0, g1):
    gs = (g0, g1)
    st = lambda key, f=lambda a: a[0]: jnp.stack([f(g[key]) for g in gs])
    out = {}
    out["lower_bounds"] = jnp.concatenate([g0["lb0"] + g1["lb0"], g0["lb1"] + g1["lb1"]], axis=0)
    out["mu_shift"] = st("mu", lambda a: a[0, :896])
    out["mu_vres"] = g1["mu"][:, 896:928]
    out["rwkv_w0"], out["rwkv_a0"], out["rwkv_k_k"], out["rwkv_k_a"] = st("w0"), st("a0"), st("k_k"), st("k_a")
    out["rwkv_w2"] = st("w2p", lambda a: a[0:32])
    out["rwkv_a2"] = st("a2p", lambda a: a[32:64])
    out["rwkv_g2"] = st("g2p", lambda a: a[64:128])
    out["rwkv_r_k"] = st("r_k", lambda a: a.reshape(NH, HD))
    out["rwkv_lnx_w"], out["rwkv_lnx_b"] = st("lnx_w"), st("lnx_b")
    out["rwkv_v0"] = g1["v0"]
    out["rwkv_v2"] = g1["v2p"][None, 0:32]
    out["ssd_conv_w"] = st("conv_w", lambda a: a)
    out["ssd_conv_b"] = st("conv_b")
    out["ssd_dt_bias"], out["ssd_A_log"], out["ssd_D"] = (st(k, lambda a: a[0, :NH]) for k in ("dt_bias", "a_log", "ssd_d"))
    out["ssd_norm_w"], out["hgrn_norm_w"] = st("ssd_norm_w"), st("hgrn_norm_w")
    for n in ("ln1_w", "ln1_b", "ln2_w", "ln2_b"):
        out[n] = st(n)
    return out


MESH_T = pl.DeviceIdType.MESH
ANY = pl.BlockSpec(memory_space=pl.ANY)


def _dev_index(px, py, pc):
    return 4 * px + 2 * py + pc


def all_gather(arrs):
    n = len(arrs)

    def body(*refs):
        ins, outs = refs[:n], refs[n:2 * n]
        send_sems, recv_sems, local_sems = refs[2 * n:]
        x, y, c = lax.axis_index("x"), lax.axis_index("y"), lax.axis_index("c")
        me, sibling = (x, y, c), (x, y, 1 - c)
        chips = [(1 - x, y), (x, 1 - y), (1 - x, 1 - y)]

        def copy(a, k, block, to, src=None):
            slot = outs[a].at[_dev_index(*block)]
            return pltpu.make_async_remote_copy(src_ref=slot if src is None else src, dst_ref=slot,
                                                send_sem=send_sems.at[a, k], recv_sem=recv_sems.at[a, k],
                                                device_id=to, device_id_type=MESH_T)

        mine = [pltpu.make_async_copy(ins[a], outs[a].at[_dev_index(*me)], local_sems.at[a]) for a in range(n)]
        for cp in mine:
            cp.start()
        first = []
        for a in range(n):
            first.append(copy(a, 0, me, sibling, src=ins[a]))
            first += [copy(a, 1 + j, me, (*chip, c), src=ins[a]) for j, chip in enumerate(chips)]
        for cp in first:
            cp.start()
        passed = []
        for j, chip in enumerate(chips):
            for a in range(n):
                copy(a, 1 + j, (*chip, c), me).wait_recv()
                fwd = copy(a, 4 + j, (*chip, c), sibling)
                fwd.start()
                passed.append(fwd)
        for a in range(n):
            copy(a, 0, sibling, me).wait_recv()
            for j, chip in enumerate(chips):
                copy(a, 4 + j, (*chip, 1 - c), me).wait_recv()
        for cp in first + passed:
            cp.wait_send()
        for cp in mine:
            cp.wait()

    return pl.pallas_call(
        body, in_specs=[ANY] * n, out_specs=[ANY] * n,
        out_shape=[_sds((N_DEV,) + a.shape, a.dtype) for a in arrs],
        scratch_shapes=[pltpu.SemaphoreType.DMA((n, 7)), pltpu.SemaphoreType.DMA((n, 7)), pltpu.SemaphoreType.DMA((n,))],
        name="all_gather")(*arrs)


def _chips(x, y):
    return [(x, y), (1 - x, y), (x, 1 - y), (1 - x, 1 - y)]


def exchange_siblings(arrs):
    n = len(arrs)

    def body(*refs):
        ins, sib = refs[:n], refs[n:2 * n]
        send_sems, recv_sems = refs[2 * n:]
        x, y, c = lax.axis_index("x"), lax.axis_index("y"), lax.axis_index("c")
        sibling = (x, y, 1 - c)
        sends = []
        for a in range(n):
            for k, (cx, cy) in enumerate(_chips(x, y)):
                sd = pltpu.make_async_remote_copy(src_ref=ins[a].at[_dev_index(cx, cy, 1 - c)], dst_ref=sib[a].at[k],
                                                  send_sem=send_sems.at[a, k], recv_sem=recv_sems.at[a, k],
                                                  device_id=sibling, device_id_type=MESH_T)
                sd.start()
                sends.append(sd)
        for sd in sends:
            sd.wait_recv()
        for sd in sends:
            sd.wait_send()

    sem = pltpu.SemaphoreType.DMA((n, 4))
    return pl.pallas_call(body, in_specs=[ANY] * n, out_specs=[ANY] * n,
                          out_shape=[_sds((4,) + a.shape[1:], a.dtype) for a in arrs],
                          scratch_shapes=[sem, sem], name="exchange_siblings")(*arrs)


def reduce_pair(name, send, slots, sib, wire_dtype):
    _, r, c = send.shape
    rb = min(r, 262144 // c)

    def body(slots_ref, m0, m1, m2, m3, s_ref, own_ref, part_ref):
        own_ref[...] = m0[...] + s_ref[0]
        for k, m_ref in enumerate((m1, m2, m3)):
            part_ref[k] = (m_ref[...] + s_ref[k + 1]).astype(wire_dtype)

    mine = [pl.BlockSpec((None, rb, c), lambda i, s, k=k: (s[k], i, 0)) for k in range(4)]
    grid_spec = pltpu.PrefetchScalarGridSpec(
        num_scalar_prefetch=1, grid=(r // rb,),
        in_specs=mine + [pl.BlockSpec((4, rb, c), lambda i, s: (0, i, 0))],
        out_specs=[pl.BlockSpec((rb, c), lambda i, s: (i, 0)), pl.BlockSpec((3, rb, c), lambda i, s: (0, i, 0))])
    return pl.pallas_call(body, grid_spec=grid_spec, out_shape=[_sds((r, c)), _sds((3, r, c), wire_dtype)], name=name,
                          compiler_params=_cp(("parallel",)))(slots, send, send, send, send, sib)


def exchange_chips(parts, rep):
    n = len(parts)

    def body(*refs):
        ins, rep_ref = refs[:n], refs[n]
        recv, rep_all = refs[n + 1:2 * n + 1], refs[2 * n + 1]
        send_sems, recv_sems, rsend_sems, rrecv_sems, local_sem = refs[2 * n + 2:]
        x, y, c = lax.axis_index("x"), lax.axis_index("y"), lax.axis_index("c")
        me = _dev_index(x, y, c)
        mine = pltpu.make_async_copy(rep_ref, rep_all.at[me], local_sem)
        mine.start()
        cps = []
        for a in range(n):
            for k, (cx, cy) in enumerate(_chips(x, y)[1:]):
                cp = pltpu.make_async_remote_copy(src_ref=ins[a].at[k], dst_ref=recv[a].at[k],
                                                  send_sem=send_sems.at[a, k], recv_sem=recv_sems.at[a, k],
                                                  device_id=(cx, cy, c), device_id_type=MESH_T)
                cp.start()
                cps.append(cp)
        rels = [(rx, ry, rc) for rx in (0, 1) for ry in (0, 1) for rc in (0, 1)][1:]
        peers = [(jnp.where(rx, 1 - x, x), jnp.where(ry, 1 - y, y), jnp.where(rc, 1 - c, c)) for rx, ry, rc in rels]
        rcps = []
        for k, peer in enumerate(peers):
            cp = pltpu.make_async_remote_copy(src_ref=rep_ref, dst_ref=rep_all.at[me], send_sem=rsend_sems.at[k],
                                              recv_sem=rrecv_sems.at[k], device_id=peer, device_id_type=MESH_T)
            cp.start()
            rcps.append(cp)
        for k, peer in enumerate(peers):
            pltpu.make_async_remote_copy(src_ref=rep_ref, dst_ref=rep_all.at[_dev_index(*peer)], send_sem=rsend_sems.at[k],
                                         recv_sem=rrecv_sems.at[k], device_id=peer, device_id_type=MESH_T).wait_recv()
        for cp in cps:
            cp.wait_recv()
        for cp in cps + rcps:
            cp.wait_send()
        mine.wait()

    outs = pl.pallas_call(
        body, in_specs=[ANY] * (n + 1), out_specs=[ANY] * (n + 1),
        out_shape=[_sds(a.shape, a.dtype) for a in parts] + [_sds((N_DEV,) + rep.shape, rep.dtype)],
        scratch_shapes=[pltpu.SemaphoreType.DMA((n, 3)), pltpu.SemaphoreType.DMA((n, 3)), pltpu.SemaphoreType.DMA((7,)),
                        pltpu.SemaphoreType.DMA((7,)), pltpu.SemaphoreType.DMA],
        name="exchange_chips")(*parts, rep)
    return outs[:n], outs[n]


def adamw(name, terms, w, m, v):
    r, c = w.shape
    rb = min(r, 262144 // c)
    c1 = 1.0 - ADAM_B1 ** ADAM_STEP
    c2 = 1.0 - ADAM_B2 ** ADAM_STEP
    nt = len(terms)

    def body(*refs):
        w_ref, m_ref, v_ref = refs[nt:nt + 3]
        g_ref, d_ref, nm_ref, nv_ref = refs[nt + 3:]
        g = refs[0][...].astype(F32)
        for t_ref in refs[1:nt]:
            g = g + t_ref[...].astype(F32)
        nm = ADAM_B1 * m_ref[...] + (1.0 - ADAM_B1) * g
        nv = ADAM_B2 * v_ref[...] + (1.0 - ADAM_B2) * (g * g)
        g_ref[...] = g
        nm_ref[...] = nm
        nv_ref[...] = nv
        d_ref[...] = -ADAM_LR * ((nm / c1) / (jnp.sqrt(nv / c2) + ADAM_EPS) + ADAM_WD * w_ref[...])

    blk = pl.BlockSpec((rb, c), lambda i: (i, 0))
    tspecs = [blk if k is None else pl.BlockSpec((None, rb, c), lambda i, k=k: (k, i, 0)) for _, k in terms]
    return pl.pallas_call(body, grid=(r // rb,), in_specs=tspecs + [blk] * 3, out_specs=[blk] * 4,
                          out_shape=[_sds((r, c))] * 4, name=name,
                          compiler_params=_cp(("parallel",)))(*[t for t, _ in terms], w, m, v)


SMS_ROWS = 16
REP_ROWS = 24
N_BIG = 8
SMALL_SHARDED = (("rwkv_w2", (2, 32, 32)), ("rwkv_a2", (2, 32, 32)), ("rwkv_g2", (2, 64, 32)), ("rwkv_v2", (1, 32, 32)),
                 ("ssd_conv_w", (2, 4, 96)))
REPLICATED = (("lower_bounds", (2, 256)), ("mu_shift", (2, 896)), ("mu_vres", (1, 32)), ("rwkv_w0", (2, 256)),
              ("rwkv_a0", (2, 256)), ("rwkv_k_k", (2, 256)), ("rwkv_k_a", (2, 256)), ("rwkv_r_k", (2, 4, 64)),
              ("rwkv_lnx_w", (2, 256)), ("rwkv_lnx_b", (2, 256)), ("rwkv_v0", (1, 256)), ("ssd_conv_b", (2, 768)),
              ("ssd_dt_bias", (2, 4)), ("ssd_A_log", (2, 4)), ("ssd_D", (2, 4)), ("ssd_norm_w", (2, 256)),
              ("hgrn_norm_w", (2, 256)), ("ln1_w", (2, 1024)), ("ln1_b", (2, 1024)), ("ln2_w", (2, 1024)),
              ("ln2_b", (2, 1024)))


def _flat_rows(parts, rows):
    flat = jnp.concatenate([a.reshape(-1) for a in parts])
    return jnp.concatenate([flat, jnp.zeros((rows * PACK_W - flat.shape[0],), flat.dtype)]).reshape(rows, PACK_W)


def _local_arrays(d):
    arrs = [_w_in_pad(d["w_in"][0], None), _w_in_pad(d["w_in"][1], d["w_in_vres"][0]), d["w_out"][0], d["w_out"][1],
            d["w_up"][0].T, d["w_up"][1].T, d["w_down"][0], d["w_down"][1],
            _flat_rows([d[n] for n, _ in SMALL_SHARDED], SMS_ROWS)]
    return arrs, _flat_rows([d[n] for n, _ in REPLICATED], REP_ROWS)


def _unflat(rows2d, table):
    flat, out, o = rows2d.reshape(-1), {}, 0
    for name, shape in table:
        n = 1
        for s in shape:
            n *= s
        out[name] = flat[o:o + n].reshape(shape)
        o += n
    return out


def _from_local_arrays(arrs, rep):
    d = {}
    g0, _ = _w_in_unpad(arrs[0])
    g1, gv = _w_in_unpad(arrs[1])
    d["w_in"], d["w_in_vres"] = jnp.stack([g0, g1]), gv[None]
    d["w_out"] = jnp.stack([arrs[2], arrs[3]])
    d["w_up"] = jnp.stack([arrs[4].T, arrs[5].T])
    d["w_down"] = jnp.stack([arrs[6], arrs[7]])
    d.update(_unflat(arrs[8], SMALL_SHARDED))
    d.update(_unflat(rep, REPLICATED))
    return d


def _gathered_weights(gathered):
    full = [g.reshape(N_DEV * g.shape[1], g.shape[2]) for g in gathered[:N_BIG]]
    wts = [dict(w_in=full[l], w_out=full[2 + l], w_up_t=full[4 + l], w_down=full[6 + l]) for l in range(DEPTH)]
    small, flat, o = {}, gathered[N_BIG].reshape(N_DEV, -1), 0
    for name, shape in SMALL_SHARDED:
        n = shape[0] * shape[1] * shape[2]
        blk = flat[:, o:o + n].reshape((N_DEV,) + shape)
        small[name] = blk.transpose(1, 2, 0, 3).reshape(shape[0], shape[1], N_DEV * shape[2])
        o += n
    return wts, small


def _send_arrays(big, small_grads):
    blocks = lambda g: g.reshape(N_DEV, g.shape[0] // N_DEV, g.shape[1])
    arrs = [blocks(big[l][k]) for k in ("w_in", "w_out", "w_up_t", "w_down") for l in range(DEPTH)]
    sms = []
    for name, shape in SMALL_SHARDED:
        g = small_grads[name].reshape(shape[0], shape[1], N_DEV, shape[2]).transpose(2, 0, 1, 3)
        sms.append(g.reshape(N_DEV, -1))
    sms = jnp.concatenate(sms, axis=1)
    sms = jnp.concatenate([sms, jnp.zeros((N_DEV, SMS_ROWS * PACK_W - sms.shape[1]), F32)], axis=1)
    arrs.append(sms.reshape(N_DEV, SMS_ROWS, PACK_W))
    return arrs, _flat_rows([small_grads[n] for n, _ in REPLICATED], REP_ROWS)


def _local_step(x, tgt, wts, raw):
    consts = _consts()
    ps = [_layer_params(l, raw, consts) for l in range(DEPTH)]
    x1, sv0 = layer_fwd(0, x, None, wts[0], ps[0])
    x2, sv1 = layer_fwd(1, x1, sv0["fl"], wts[1], ps[1])
    dy, lparts = loss_call(x2, tgt)
    loss = jnp.sum(lparts[::8, 0])
    dx1, dvfirst, g1 = layer_bwd(1, dy, None, sv1, wts[1], ps[1])
    dx0, _, g0 = layer_bwd(0, dx1, dvfirst, sv0, wts[0], ps[0])
    big = [{k: g[k] for k in ("w_in", "w_out", "w_up_t", "w_down")} for g in (g0, g1)]
    return loss, dx0, big, _natural_grads(g0, g1)


WEIGHT_NAMES = ("lower_bounds", "w_in", "w_in_vres", "mu_shift", "mu_vres", "rwkv_w0", "rwkv_w2", "rwkv_a0", "rwkv_a2",
                "rwkv_g2", "rwkv_k_k", "rwkv_k_a", "rwkv_r_k", "rwkv_lnx_w", "rwkv_lnx_b", "rwkv_v0", "rwkv_v2",
                "ssd_conv_w", "ssd_conv_b", "ssd_dt_bias", "ssd_A_log", "ssd_D", "ssd_norm_w", "hgrn_norm_w", "w_out",
                "ln1_w", "ln1_b", "w_up", "w_down", "ln2_w", "ln2_b")


def kernel(x, lower_bounds, w_in, w_in_vres, mu_shift, mu_vres, rwkv_w0, rwkv_w2, rwkv_a0, rwkv_a2, rwkv_g2, rwkv_k_k, rwkv_k_a, rwkv_r_k, rwkv_lnx_w, rwkv_lnx_b, rwkv_v0, rwkv_v2, ssd_conv_w, ssd_conv_b, ssd_dt_bias, ssd_A_log, ssd_D, ssd_norm_w, hgrn_norm_w, w_out, ln1_w, ln1_b, w_up, w_down, ln2_w, ln2_b, loss_target, m_lower_bounds, m_w_in, m_w_in_vres, m_mu_shift, m_mu_vres, m_rwkv_w0, m_rwkv_w2, m_rwkv_a0, m_rwkv_a2, m_rwkv_g2, m_rwkv_k_k, m_rwkv_k_a, m_rwkv_r_k, m_rwkv_lnx_w, m_rwkv_lnx_b, m_rwkv_v0, m_rwkv_v2, m_ssd_conv_w, m_ssd_conv_b, m_ssd_dt_bias, m_ssd_A_log, m_ssd_D, m_ssd_norm_w, m_hgrn_norm_w, m_w_out, m_ln1_w, m_ln1_b, m_w_up, m_w_down, m_ln2_w, m_ln2_b, v_lower_bounds, v_w_in, v_w_in_vres, v_mu_shift, v_mu_vres, v_rwkv_w0, v_rwkv_w2, v_rwkv_a0, v_rwkv_a2, v_rwkv_g2, v_rwkv_k_k, v_rwkv_k_a, v_rwkv_r_k, v_rwkv_lnx_w, v_rwkv_lnx_b, v_rwkv_v0, v_rwkv_v2, v_ssd_conv_w, v_ssd_conv_b, v_ssd_dt_bias, v_ssd_A_log, v_ssd_D, v_ssd_norm_w, v_hgrn_norm_w, v_w_out, v_ln1_w, v_ln1_b, v_w_up, v_w_down, v_ln2_w, v_ln2_b):
    given = dict(locals())
    w = {n: given[n] for n in WEIGHT_NAMES}
    w_arrs, w_rep = _local_arrays(w)
    m_arrs, m_rep = _local_arrays({n: given["m_" + n] for n in WEIGHT_NAMES})
    v_arrs, v_rep = _local_arrays({n: given["v_" + n] for n in WEIGHT_NAMES})
    gathered = all_gather([a.astype(BF16) for a in w_arrs[:N_BIG]] + [w_arrs[N_BIG]])
    wts, small_full = _gathered_weights(gathered)
    raw = {n: w[n] for n, _ in REPLICATED}
    raw.update(small_full)
    loss, dx, big, small_grads = _local_step(x[0], loss_target[0], wts, raw)
    send, rep = _send_arrays(big, small_grads)
    sib = exchange_siblings(send)
    mx, my, mc = lax.axis_index("x"), lax.axis_index("y"), lax.axis_index("c")
    slots = jnp.stack([_dev_index(cx, cy, mc) for cx, cy in _chips(mx, my)]).astype(jnp.int32)
    own, parts = [], []
    for a in range(N_BIG + 1):
        o, pt = reduce_pair(f"reduce_pair{a}", send[a], slots, sib[a], BF16 if a < N_BIG else F32)
        own.append(o)
        parts.append(pt)
    recv, rep_all = exchange_chips(parts, rep)
    results = [adamw(f"adamw{a}", [(own[a], None), (recv[a], 0), (recv[a], 1), (recv[a], 2)], w_arrs[a], m_arrs[a], v_arrs[a])
               for a in range(N_BIG + 1)]
    rep_res = adamw("adamw_rep", [(rep_all, q) for q in range(N_DEV)], w_rep, m_rep, v_rep)
    loss = lax.psum(loss, ("x", "y", "c"))
    outs = [loss, dx[None]]
    for q in range(4):
        d = _from_local_arrays([res[q] for res in results], rep_res[q])
        outs += [d[n] for n in WEIGHT_NAMES]
    return tuple(outs)
```

```python
import functools

import jax
import jax.numpy as jnp
from jax import lax
from jax.experimental import pallas as pl
from jax.experimental.pallas import tpu as pltpu

F32 = jnp.float32
BF16 = jnp.bfloat16
HI = lax.Precision.HIGHEST

N_DEV = 8
SEQ = 2048
D_MODEL = 1024
D_FF = 4096
DG = 256
NH = 4
HD = 64
DEPTH = 2
ALPHA = (2.0 * DEPTH) ** 0.25
LN_EPS = 1e-5
RMS_EPS = 1e-5
GN_EPS = HD * 1e-5
IN_COLS = 3716
SSD_N = 128
SSD_CHUNK = 128
HGRN_CHUNK = 16
DILATED = ((128, 1), (512, 4), (2048, 16))

ADAM_LR, ADAM_B1, ADAM_B2, ADAM_EPS, ADAM_WD, ADAM_STEP = 0.001, 0.9, 0.999, 1e-08, 0.01, 10

PW = 4096
C_R, C_K, C_V = 0, 256, 512
C_AQ, C_AK, C_AV = 768, 1024, 1280
C_Z, C_XBC = 1536, 1792
C_HQ, C_HF, C_HI, C_HG = 2560, 2816, 3072, 3328
C_LORA, C_DT, C_VRES = 3584, 3712, 3840

RB = 256
VMEM_LIMIT = 56 * 1024 * 1024
PACK_W = 1024


def _cp(sem=None):
    return pltpu.CompilerParams(dimension_semantics=sem, vmem_limit_bytes=VMEM_LIMIT)


def _sds(shape, dt=F32):
    return jax.ShapeDtypeStruct(tuple(shape), dt)


def _rows(w, cb=0, rb=RB):
    return pl.BlockSpec((rb, w), lambda i: (i, cb))


def _full(shape):
    n = len(shape)
    return pl.BlockSpec(tuple(shape), lambda *_: (0,) * n)


def _sigmoid(x):
    return 1.0 / (1.0 + jnp.exp(-x))


def _silu(x):
    return x * _sigmoid(x)


def _softplus(x):
    return jnp.maximum(x, 0.0) + jnp.log(1.0 + jnp.exp(jnp.where(x > 0, -x, x)))


MID = lax.Precision.HIGH
NN, TN, NT = (((1,), (0,)), ((), ())), (((0,), (0,)), ((), ())), (((1,), (1,)), ((), ()))


def _dot(a, b):
    return lax.dot_general(a, b, NN, precision=MID, preferred_element_type=F32)


def _dot_tn(a, b):
    return lax.dot_general(a, b, TN, precision=MID, preferred_element_type=F32)


def _dot_nt(a, b):
    return lax.dot_general(a, b, NT, precision=MID, preferred_element_type=F32)


def _dotx(a, b):
    return lax.dot_general(a, b, NN, precision=HI, preferred_element_type=F32)


def _dotx_tn(a, b):
    return lax.dot_general(a, b, TN, precision=HI, preferred_element_type=F32)


def _seg_ones(n, seg):
    i = jnp.arange(n)
    return (i[:, None] // seg == i[None, :] // seg).astype(F32)


def _shift_down(x, s):
    row = lax.broadcasted_iota(jnp.int32, x.shape, 0)
    return jnp.where(row < s, 0.0, pltpu.roll(x, s, 0))


def _shift_up(x, s):
    n = x.shape[0]
    row = lax.broadcasted_iota(jnp.int32, x.shape, 0)
    return jnp.where(row >= n - s, 0.0, pltpu.roll(x, n - s, 0))


@functools.partial(jax.custom_vjp, nondiff_argnums=(1,))
def _tshift(x, s):
    return _shift_down(x, s)


def _tshift_fwd(x, s):
    return _shift_down(x, s), None


def _tshift_bwd(s, _, g):
    return (_shift_up(g, s),)


_tshift.defvjp(_tshift_fwd, _tshift_bwd)


def _map_fwd(name, fn, grid, ins, in_specs, out_shapes, out_specs):
    n_in = len(ins)

    def body(*refs):
        ys = fn(*[r[...] for r in refs[:n_in]])
        for r, y in zip(refs[n_in:], ys):
            r[...] = y

    return pl.pallas_call(body, grid=grid, in_specs=in_specs, out_specs=out_specs, out_shape=out_shapes,
                          name=name, compiler_params=_cp(("parallel",)))(*ins)


def _map_bwd(name, fn, grid, ins, in_specs, cts, ct_specs, want, acc=(), gout=None):
    n_in = len(ins)
    flat_cts = [c for group in cts for c in group]
    flat_specs = [s for group in ct_specs for s in group]
    n_ct = len(flat_cts)
    gout = gout or {}
    out_shapes = [gout[i][0] if i in gout else _sds(ins[i].shape) for i in want]
    out_specs = [gout[i][1] if i in gout else in_specs[i] for i in want]

    def body(*refs):
        xs = [r[...] for r in refs[:n_in]]
        cvals = [r[...] for r in refs[n_in:n_in + n_ct]]
        gouts = refs[n_in + n_ct:]
        cs, p = [], 0
        for group in cts:
            v = cvals[p]
            for q in range(1, len(group)):
                v = v + cvals[p + q]
            cs.append(v)
            p += len(group)

        def f(*wanted):
            full = list(xs)
            for i, w in zip(want, wanted):
                full[i] = w
            return tuple(fn(*full))

        _, vjp = jax.vjp(f, *[xs[i] for i in want])
        gs = vjp(tuple(cs))
        for o, i, g in zip(gouts, want, gs):
            if i in acc:
                @pl.when(pl.program_id(0) == 0)
                def _():
                    o[...] = jnp.zeros_like(o)

                o[...] += g
            else:
                o[...] = g

    sem = ("arbitrary",) if acc else ("parallel",)
    return pl.pallas_call(body, grid=grid, in_specs=list(in_specs) + flat_specs, out_specs=out_specs,
                          out_shape=out_shapes, name=name, compiler_params=_cp(sem))(*ins, *flat_cts)


def _addn(name, *arrs):
    n, c = arrs[0].shape

    def fn(*xs):
        r = xs[0]
        for x in xs[1:]:
            r = r + x
        return (r,)

    return _map_fwd(name, fn, (n // RB,), list(arrs), [_rows(c)] * len(arrs), [_sds((n, c))], [_rows(c)])[0]


MM_TILES = {"k1024": (2048, 512, 1024), "k4096": (1024, 1024, 1024), "wgrad_tall": (2048, 1024, 512),
            "wgrad_wide": (1024, 2048, 512)}


def _mm(name, a, b, mode, tm, tn, tk, add=None, add_scale=1.0, epilogue=None):
    if mode == "nn":
        (m, k), n = a.shape, b.shape[1]
    elif mode == "nt":
        (m, k), n = a.shape, b.shape[0]
    else:
        (k, m), n = a.shape, b.shape[1]
    nk = k // tk
    dn = {"nn": (((1,), (0,)), ((), ())), "nt": (((1,), (1,)), ((), ())), "tn": (((0,), (0,)), ((), ()))}[mode]

    def body(*refs):
        a_ref, b_ref = refs[:2]
        add_ref = refs[2] if add is not None else None
        o_ref = refs[3] if add is not None else refs[2]
        prod = lax.dot_general(a_ref[...].astype(BF16), b_ref[...].astype(BF16), dn, preferred_element_type=F32)

        def finish(r):
            if epilogue == "relu2":
                r = jnp.maximum(r, 0.0)
                r = r * r
            elif epilogue == "relu2_bwd":
                r = r * (2.0 * jnp.sqrt(add_ref[...]))
            elif add is not None:
                r = r + add_scale * add_ref[...]
            o_ref[...] = r

        if nk == 1:
            finish(prod)
        else:
            acc = refs[-1]
            kk = pl.program_id(2)

            @pl.when(kk == 0)
            def _():
                acc[...] = prod

            @pl.when(kk > 0)
            def _():
                acc[...] += prod

            @pl.when(kk == nk - 1)
            def _():
                finish(acc[...])

    a_spec = pl.BlockSpec((tk, tm), lambda i, j, q: (q, i)) if mode == "tn" else pl.BlockSpec((tm, tk), lambda i, j, q: (i, q))
    b_spec = pl.BlockSpec((tn, tk), lambda i, j, q: (j, q)) if mode == "nt" else pl.BlockSpec((tk, tn), lambda i, j, q: (q, j))
    o_spec = pl.BlockSpec((tm, tn), lambda i, j, q: (i, j))
    ins, specs = [a, b], [a_spec, b_spec]
    if add is not None:
        ins.append(add)
        specs.append(o_spec)
    return pl.pallas_call(body, grid=(m // tm, n // tn, nk), in_specs=specs, out_specs=o_spec, out_shape=_sds((m, n)),
                          scratch_shapes=[pltpu.VMEM((tm, tn), F32)] if nk > 1 else [], name=name,
                          compiler_params=_cp(("parallel", "parallel", "arbitrary")))(*ins)


LERP_BLOCKS = (0, 1, 2, 3, 4, 5, C_LORA // 128, C_VRES // 128)


def _lerp_colmap(j):
    r = jnp.where(j < 6, j, jnp.where(j == 6, C_LORA // 128, C_VRES // 128))
    return (0, r)


def _lerp_fn(f, mu):
    return (f + (_tshift(f, 1) - f) * mu,)


def _lerp_specs():
    return [pl.BlockSpec((SEQ, 128), _lerp_colmap), pl.BlockSpec((1, 128), lambda j: (0, j))]


def lerp_fwd(l, proj, mu):
    return _map_fwd(f"lerp_fwd{l}", _lerp_fn, (8,), [proj, mu], _lerp_specs(), [_sds((SEQ, 1024))],
                    [pl.BlockSpec((SEQ, 128), lambda j: (0, j))])[0]


def lerp_bwd(l, proj, mu, dfl):
    n_in = 2

    def body(f_ref, mu_ref, g_ref, df_ref, dmu_ref):
        _, vjp = jax.vjp(_lerp_fn, f_ref[...], mu_ref[...])
        df, dmu = vjp((g_ref[...],))
        df_ref[...] = df
        dmu_ref[...] = dmu

    cspec = pl.BlockSpec((SEQ, 128), lambda j: (0, j))
    return pl.pallas_call(body, grid=(8,), in_specs=_lerp_specs() + [cspec],
                          out_specs=[cspec, pl.BlockSpec((1, 128), lambda j: (0, j))],
                          out_shape=[_sds((SEQ, 1024)), _sds((1, 1024))], name=f"lerp_bwd{l}",
                          compiler_params=_cp(("parallel",)))(proj, mu, dfl)


def _conv_fn(x, w, b):
    y = x * w[3:4, :] + _tshift(x, 1) * w[2:3, :] + _tshift(x, 2) * w[1:2, :] + _tshift(x, 3) * w[0:1, :] + b
    return (_silu(y),)


def _conv_specs():
    return [pl.BlockSpec((SEQ, 128), lambda j: (0, C_XBC // 128 + j)), pl.BlockSpec((4, 128), lambda j: (0, j)),
            pl.BlockSpec((1, 128), lambda j: (0, j))]


def conv_fwd(l, proj, w, b):
    return _map_fwd(f"conv_fwd{l}", _conv_fn, (6,), [proj, w, b], _conv_specs(), [_sds((SEQ, 768))],
                    [pl.BlockSpec((SEQ, 128), lambda j: (0, j))])[0]


def conv_bwd(l, proj, w, b, dxc):
    def body(x_ref, w_ref, b_ref, g_ref, dx_ref, dw_ref, db_ref):
        _, vjp = jax.vjp(_conv_fn, x_ref[...], w_ref[...], b_ref[...])
        dx, dw, db = vjp((g_ref[...],))
        dx_ref[...] = dx
        dw_ref[...] = dw
        db_ref[...] = db

    cspec = pl.BlockSpec((SEQ, 128), lambda j: (0, j))
    return pl.pallas_call(body, grid=(6,), in_specs=_conv_specs() + [cspec],
                          out_specs=[cspec, pl.BlockSpec((4, 128), lambda j: (0, j)), pl.BlockSpec((1, 128), lambda j: (0, j))],
                          out_shape=[_sds((SEQ, 768)), _sds((4, 768)), _sds((1, 768))], name=f"conv_bwd{l}",
                          compiler_params=_cp(("parallel",)))(proj, w, b, dxc)


def _rwkv_pre_fn(has_vres):
    def fn(fk, fv, flora, *rest):
        if has_vres:
            fvres, vfirst, w0, w2p, a0, a2p, g2p, k_k, k_a, v0, v2p, seg = rest
        else:
            w0, w2p, a0, a2p, g2p, k_k, k_a, seg = rest
        w_log = -_softplus(-(w0 + _dot(jnp.tanh(flora), w2p))) - 0.5
        w = jnp.exp(-jnp.exp(w_log))
        a = _sigmoid(a0 + _dot(flora, a2p))
        g = _dot(_sigmoid(flora), g2p)
        if has_vres:
            v2 = fv + (vfirst - fv) * _sigmoid(v0 + _dot(fvres, v2p))
        else:
            v2 = fv * 1.0
        kk = fk * k_k
        kk = kk / jnp.maximum(jnp.sqrt(_dot(kk * kk, seg)), 1e-12)
        k2 = fk * (1.0 + (a - 1.0) * k_a)
        return w, k2, v2, -kk, kk * a, g

    return fn


def _rwkv_pre_args(fl, vfirst, p, has_vres):
    ins = [fl, fl, fl]
    specs = [_rows(256, 1), _rows(256, 2), _rows(128, 6)]
    if has_vres:
        ins += [fl, vfirst]
        specs += [_rows(128, 7), _rows(256, 2)]
    names = ["w0", "w2p", "a0", "a2p", "g2p", "k_k", "k_a"] + (["v0", "v2p"] if has_vres else []) + ["seg64"]
    for nme in names:
        ins.append(p[nme])
        specs.append(_full(p[nme].shape))
    return ins, specs, names


def rwkv_pre_fwd(l, fl, vfirst, p):
    has_vres = l > 0
    ins, specs, _ = _rwkv_pre_args(fl, vfirst, p, has_vres)
    return _map_fwd(f"rwkv_pre_fwd{l}", _rwkv_pre_fn(has_vres), (SEQ // RB,), ins, specs,
                    [_sds((SEQ, DG))] * 6, [_rows(DG)] * 6)


def rwkv_pre_bwd(l, fl, vfirst, p, cts):
    has_vres = l > 0
    ins, specs, names = _rwkv_pre_args(fl, vfirst, p, has_vres)
    n_row = 5 if has_vres else 3
    want = list(range(n_row)) + [n_row + i for i, nme in enumerate(names) if nme != "seg64"]
    acc = tuple(w for w in want if w >= n_row)
    ct_specs = [[_rows(DG)] * len(g) for g in cts]
    gout = {0: (_sds((SEQ, DG)), _rows(DG)), 1: (_sds((SEQ, DG)), _rows(DG)), 2: (_sds((SEQ, 128)), _rows(128))}
    if has_vres:
        gout[3] = (_sds((SEQ, 128)), _rows(128))
        gout[4] = (_sds((SEQ, DG)), _rows(DG))
    gs = _map_bwd(f"rwkv_pre_bwd{l}", _rwkv_pre_fn(has_vres), (SEQ // RB,), ins, specs, cts, ct_specs, want, acc, gout)
    keys = ["fk", "fv", "flora"] + (["fvres", "vfirst"] if has_vres else []) + [nme for nme in names if nme != "seg64"]
    return dict(zip(keys, gs))


def _rwkv_post_fn(y, fr, k2, v2, g, lnx_w, lnx_b, r_k, seg):
    mu = _dot(y, seg) * (1.0 / HD)
    d = y - mu
    var = _dot(d * d, seg) * (1.0 / HD)
    yn = d * lax.rsqrt(var + GN_EPS) * lnx_w + lnx_b
    bonus = _dot(fr * k2 * r_k, seg) * v2
    return ((yn + bonus) * g,)


def _rwkv_post_args(y, fl, k2, v2, g, p):
    ins = [y, fl, k2, v2, g, p["lnx_w"], p["lnx_b"], p["r_k"], p["seg64"]]
    specs = [_rows(DG), _rows(DG, 0), _rows(DG), _rows(DG), _rows(DG)] + [_full(x.shape) for x in ins[5:]]
    return ins, specs


def rwkv_post_fwd(l, y, fl, k2, v2, g, p):
    ins, specs = _rwkv_post_args(y, fl, k2, v2, g, p)
    return _map_fwd(f"rwkv_post_fwd{l}", _rwkv_post_fn, (SEQ // RB,), ins, specs, [_sds((SEQ, DG))], [_rows(DG)])[0]


def rwkv_post_bwd(l, y, fl, k2, v2, g, p, dya):
    ins, specs = _rwkv_post_args(y, fl, k2, v2, g, p)
    gs = _map_bwd(f"rwkv_post_bwd{l}", _rwkv_post_fn, (SEQ // RB,), ins, specs, [[dya]], [[_rows(DG)]],
                  want=[0, 1, 2, 3, 4, 5, 6, 7], acc=(5, 6, 7), gout={1: (_sds((SEQ, DG)), _rows(DG))})
    return dict(zip(["y", "fr", "k2", "v2", "g", "lnx_w", "lnx_b", "r_k"], gs))


SCAN_TB = 64


def _coltile8(rows8, dmask, ones_stack, parts):
    pieces, rest = [], rows8
    for q in range(parts):
        piece = rest.astype(BF16).astype(F32)
        if q < parts - 1:
            rest = rest - piece
        pieces.append((piece[:, None, :] * dmask[None]).reshape(8 * HD, DG).astype(BF16))
    x = pieces[0] if parts == 1 else jnp.concatenate(pieces, axis=1)
    return jnp.dot(x, ones_stack, preferred_element_type=F32).reshape(8, HD, DG)


def _coltiles_bf16(rows_list, dmask, ones_bf16):
    x = jnp.concatenate([(r8[:, None, :] * dmask[None]).reshape(8 * HD, DG).astype(BF16) for r8 in rows_list], axis=0)
    t = jnp.dot(x, ones_bf16, preferred_element_type=F32)
    return [t[q * 8 * HD:(q + 1) * 8 * HD].reshape(8, HD, DG) for q in range(len(rows_list))]


def _segrows8(x8, dmask, ones_bf16):
    t = jnp.dot(x8.reshape(8 * HD, DG).astype(BF16), ones_bf16, preferred_element_type=F32).reshape(8, HD, DG)
    return jnp.sum(t * dmask[None], axis=1)


def rwkv_scan_fwd(l, fl, w, k2, v2, c, b, p, gather=()):
    nblk = SEQ // SCAN_TB
    ng = len(gather)

    def body(*refs):
        r_ref, w_ref, k_ref, v_ref, c_ref, b_ref, ones_ref, dm_ref = refs[:8]
        y_ref, st_ref = refs[8 + ng:10 + ng]
        s_sc = refs[10 + 2 * ng]
        if ng:
            begin, end = _gather_steps(refs[8:8 + ng], refs[10 + ng:10 + 2 * ng], *refs[11 + 2 * ng:])

            @pl.when(pl.program_id(0) == 0)
            def _():
                begin()

        @pl.when(pl.program_id(0) == 0)
        def _():
            s_sc[...] = jnp.zeros_like(s_sc)

        ones3, ones = ones_ref[...], ones_ref[0:DG, :]
        dmask = dm_ref[...]

        def group(gi, carry):
            t0 = pl.multiple_of(gi * 8, 8)
            sl = pl.ds(t0, 8)
            v8 = v_ref[sl, :]
            wt = _coltile8(w_ref[sl, :], dmask, ones3, 3)
            ct, bt, kt, rt = _coltiles_bf16([c_ref[sl, :], b_ref[sl, :], k_ref[sl, :], r_ref[sl, :]], dmask, ones)
            t = s_sc[...]
            for j in range(8):
                sa = jnp.sum(t * ct[j], axis=0, keepdims=True)
                t = t * wt[j] + bt[j] * sa + kt[j] * v8[j:j + 1, :]
                st_ref[t0 + j] = t
            s_sc[...] = t
            y_ref[sl, :] = jnp.sum(st_ref[sl] * rt, axis=1)
            return carry

        lax.fori_loop(0, SCAN_TB // 8, group, 0)

        if ng:
            @pl.when(pl.program_id(0) == nblk - 1)
            def _():
                end()

    row = pl.BlockSpec((SCAN_TB, DG), lambda i: (i, 0))
    ins = [fl, w, k2, v2, c, b, p["seg64x3_bf16"], p["dmask"]] + list(gather)
    specs = [row] * 6 + [_full((3 * DG, DG)), _full((HD, DG))] + [ANY] * ng
    outs = pl.pallas_call(body, grid=(nblk,), in_specs=specs,
                          out_specs=[row, pl.BlockSpec((SCAN_TB, HD, DG), lambda i: (i, 0, 0))] + [ANY] * ng,
                          out_shape=[_sds((SEQ, DG)), _sds((SEQ, HD, DG))] + _gather_shapes(gather),
                          scratch_shapes=[pltpu.VMEM((HD, DG), F32)] + (_gather_sems(ng) if ng else []),
                          name=f"rwkv_scan_fwd{l}", compiler_params=_cp(("arbitrary",)))(*ins)
    return outs[0], outs[1], list(outs[2:])


def rwkv_scan_bwd(l, fl, w, k2, v2, c, b, states, dy, p, exchange=()):
    nblk = SEQ // SCAN_TB
    nx = len(exchange)

    def body(*refs):
        r_ref, w_ref, k_ref, v_ref, c_ref, b_ref, dy_ref, st_ref, sp_ref, ones_ref, dm_ref = refs[:11]
        dr_ref, dw_ref, dk_ref, dv_ref, dc_ref, db_ref = refs[11 + nx:17 + nx]
        g_sc, prev_sc, d8_sc, dsa_sc = refs[17 + 2 * nx:21 + 2 * nx]
        i = pl.program_id(0)
        if nx:
            begin, end = _chip_exchange_steps(refs[11:11 + nx], refs[17 + nx:17 + 2 * nx], *refs[21 + 2 * nx:])

            @pl.when(i == 0)
            def _():
                begin()

        @pl.when(i == 0)
        def _():
            g_sc[...] = jnp.zeros_like(g_sc)

        ones3, ones = ones_ref[...], ones_ref[0:DG, :]
        dmask = dm_ref[...]
        first_block = i == nblk - 1

        def group(gr, carry):
            gi = SCAN_TB // 8 - 1 - gr
            t0 = pl.multiple_of(gi * 8, 8)
            sl = pl.ds(t0, 8)
            v8, dy8 = v_ref[sl, :], dy_ref[sl, :]
            t8 = st_ref[sl]
            @pl.when(gi > 0)
            def _():
                prev_sc[0] = st_ref[t0 - 1]

            @pl.when(gi == 0)
            def _():
                prev_sc[0] = jnp.where(first_block, 0.0, sp_ref[0])

            for j in range(1, 8):
                prev_sc[j] = t8[j - 1]
            tp8 = prev_sc[...]
            wt = _coltile8(w_ref[sl, :], dmask, ones3, 3)
            ct, bt, kt, rt = _coltiles_bf16([c_ref[sl, :], b_ref[sl, :], k_ref[sl, :], r_ref[sl, :]], dmask, ones)
            sa8 = jnp.sum(tp8 * ct, axis=1)
            g = g_sc[...]
            for j in range(7, -1, -1):
                g = g + rt[j] * dy8[j:j + 1, :]
                d8_sc[j] = g
                dsa = jnp.sum(g * bt[j], axis=0, keepdims=True)
                dsa_sc[j:j + 1, :] = dsa
                g = g * wt[j] + ct[j] * dsa
            g_sc[...] = g
            d8 = d8_sc[...]
            dsa8 = dsa_sc[...]
            dv_ref[sl, :] = jnp.sum(d8 * kt, axis=1)
            dr_ref[sl, :] = _segrows8(t8 * dy8[:, None, :], dmask, ones)
            dk_ref[sl, :] = _segrows8(d8 * v8[:, None, :], dmask, ones)
            dw_ref[sl, :] = _segrows8(tp8 * d8, dmask, ones)
            db_ref[sl, :] = _segrows8(d8 * sa8[:, None, :], dmask, ones)
            dc_ref[sl, :] = _segrows8(tp8 * dsa8[:, None, :], dmask, ones)
            return carry

        lax.fori_loop(0, SCAN_TB // 8, group, 0)

        if nx:
            @pl.when(i == nblk - 1)
            def _():
                end()

    row = pl.BlockSpec((SCAN_TB, DG), lambda i: (nblk - 1 - i, 0))
    st_spec = pl.BlockSpec((SCAN_TB, HD, DG), lambda i: (nblk - 1 - i, 0, 0))
    sp_spec = pl.BlockSpec((1, HD, DG), lambda i: (jnp.maximum((nblk - 1 - i) * SCAN_TB - 1, 0), 0, 0))
    ins = [fl, w, k2, v2, c, b, dy, states, states, p["seg64x3_bf16"], p["dmask"]] + list(exchange)
    specs = [row] * 7 + [st_spec, sp_spec, _full((3 * DG, DG)), _full((HD, DG))] + [ANY] * nx
    tile8 = pltpu.VMEM((8, HD, DG), F32)
    sems = [pltpu.SemaphoreType.DMA((nx, 3)), pltpu.SemaphoreType.DMA((nx, 3))] if nx else []
    outs = pl.pallas_call(body, grid=(nblk,), in_specs=specs, out_specs=[row] * 6 + [ANY] * nx,
                          out_shape=[_sds((SEQ, DG))] * 6 + [_sds(a.shape, a.dtype) for a in exchange],
                          scratch_shapes=[pltpu.VMEM((HD, DG), F32), tile8, tile8, pltpu.VMEM((8, DG), F32)] + sems,
                          name=f"rwkv_scan_bwd{l}", compiler_params=_cp(("arbitrary",)))(*ins)
    return outs[:6], list(outs[6:])


HG_ROWS = 128


HG_NC = HG_ROWS // HGRN_CHUNK


def _hgrn_block_fn(layer):
    def fn(hq, hf, hi, hg, sprev, lb0, lb1, norm_w, seg, bd, tri_bd, ones_bd, first_row, causal):
        e0 = jnp.exp(lb0 - jnp.maximum(lb0, lb1))
        e1 = jnp.exp(lb1 - jnp.maximum(lb0, lb1))
        sm0, sm1 = e0 / (e0 + e1), e1 / (e0 + e1)
        lb = (sm0 - sm0) if layer == 0 else ((sm0 + sm1) - sm0)
        forget = lb + (1.0 - lb) * _sigmoid(hf)
        logf = jnp.log(forget)
        kk = 1.0 - forget
        q = _silu(hq)
        c, nc = HGRN_CHUNK, HG_NC
        b = _dotx(tri_bd, logf)
        bl = _dotx(ones_bd, logf)
        split = lambda t: t.reshape(nc, c, DG)
        b4 = split(b)
        diff = (b4[:, :, None, :] - b4[:, None, :, :]).reshape(nc * c * c, DG)
        dec = jnp.exp(jnp.where(causal > 0.5, diff, -1e30))
        qrep = jnp.broadcast_to(split(q)[:, :, None, :], (nc, c, c, DG)).reshape(nc * c * c, DG)
        ktil = jnp.broadcast_to(split(kk)[:, None, :, :], (nc, c, c, DG)).reshape(nc * c * c, DG)
        vtil = jnp.broadcast_to(split(hi)[:, None, :, :], (nc, c, c, DG)).reshape(nc * c * c, DG)
        att = _dot(qrep * ktil * dec, seg)
        o_intra = jnp.sum((att * vtil).reshape(nc * c, c, DG), axis=1)
        kd4 = split(kk * jnp.exp(bl - b))
        qe4 = split(q * jnp.exp(b))
        v4 = split(hi)
        tot = jnp.exp(_dotx(first_row, bl))
        s, o_inter = sprev, []
        for ci in range(nc):
            o_inter.append(_dot_nt(qe4[ci], s))
            s = s * tot[ci:ci + 1, :] + _dot_tn(v4[ci], kd4[ci]) * bd
        o = o_intra + jnp.concatenate(o_inter, axis=0)
        ms = _dot(o * o, seg) * (1.0 / HD)
        y = o * lax.rsqrt(ms + RMS_EPS) * norm_w * _silu(hg)
        return y, s

    return fn


def _hgrn_consts(p):
    return [p["seg64"], p["seg64"], p["tri_bd128"], p["ones_bd128"], p["first_row"], p["causal_blk"]]


def hgrn_fwd(l, proj, p):
    fn = _hgrn_block_fn(l)

    def body(hq_ref, hf_ref, hi_ref, hg_ref, *rest):
        const_refs, (y_ref, st_ref, s_sc) = rest[:-3], rest[-3:]

        @pl.when(pl.program_id(0) == 0)
        def _():
            s_sc[...] = jnp.zeros_like(s_sc)

        sprev = s_sc[...]
        st_ref[0] = sprev
        y, snext = fn(hq_ref[...], hf_ref[...], hi_ref[...], hg_ref[...], sprev, *[r[...] for r in const_refs])
        y_ref[...] = y
        s_sc[...] = snext

    rows = lambda cb: pl.BlockSpec((HG_ROWS, DG), lambda i: (i, cb))
    ins = [proj, proj, proj, proj, p["lb0"], p["lb1"], p["hgrn_norm_w"]] + _hgrn_consts(p)
    specs = [rows(C_HQ // DG), rows(C_HF // DG), rows(C_HI // DG), rows(C_HG // DG)] + [_full(x.shape) for x in ins[4:]]
    return pl.pallas_call(body, grid=(SEQ // HG_ROWS,), in_specs=specs,
                          out_specs=[rows(0), pl.BlockSpec((1, DG, DG), lambda i: (i, 0, 0))],
                          out_shape=[_sds((SEQ, DG)), _sds((SEQ // HG_ROWS, DG, DG))],
                          scratch_shapes=[pltpu.VMEM((DG, DG), F32)], name=f"hgrn_fwd{l}",
                          compiler_params=_cp(("arbitrary",)))(*ins)


def hgrn_bwd(l, proj, states, dy, p):
    fn = _hgrn_block_fn(l)
    nblk = SEQ // HG_ROWS
    n_const = len(_hgrn_consts(p))

    def body(hq_ref, hf_ref, hi_ref, hg_ref, st_ref, dy_ref, lb0_ref, lb1_ref, nw_ref, *rest):
        const_refs, (dp_ref, dlb0_ref, dlb1_ref, dnw_ref, ds_sc) = rest[:n_const], rest[n_const:]

        @pl.when(pl.program_id(0) == 0)
        def _():
            ds_sc[...] = jnp.zeros_like(ds_sc)
            dlb0_ref[...] = jnp.zeros_like(dlb0_ref)
            dlb1_ref[...] = jnp.zeros_like(dlb1_ref)
            dnw_ref[...] = jnp.zeros_like(dnw_ref)

        consts = [r[...] for r in const_refs]
        f = lambda hq, hf, hi, hg, sp, b0, b1, nw: fn(hq, hf, hi, hg, sp, b0, b1, nw, *consts)
        _, vjp = jax.vjp(f, hq_ref[...], hf_ref[...], hi_ref[...], hg_ref[...], st_ref[0], lb0_ref[...], lb1_ref[...],
                         nw_ref[...])
        dhq, dhf, dhi, dhg, dsp, dlb0, dlb1, dnw = vjp((dy_ref[...], ds_sc[...]))
        dp_ref[:, 0:DG] = dhq
        dp_ref[:, DG:2 * DG] = dhf
        dp_ref[:, 2 * DG:3 * DG] = dhi
        dp_ref[:, 3 * DG:4 * DG] = dhg
        ds_sc[...] = dsp
        dlb0_ref[...] += dlb0
        dlb1_ref[...] += dlb1
        dnw_ref[...] += dnw

    rows = lambda cb: pl.BlockSpec((HG_ROWS, DG), lambda i: (nblk - 1 - i, cb))
    ins = [proj, proj, proj, proj, states, dy, p["lb0"], p["lb1"], p["hgrn_norm_w"]] + _hgrn_consts(p)
    specs = [rows(C_HQ // DG), rows(C_HF // DG), rows(C_HI // DG), rows(C_HG // DG),
             pl.BlockSpec((1, DG, DG), lambda i: (nblk - 1 - i, 0, 0)), rows(0)] + [_full(x.shape) for x in ins[6:]]
    return pl.pallas_call(body, grid=(nblk,), in_specs=specs,
                          out_specs=[pl.BlockSpec((HG_ROWS, 4 * DG), lambda i: (nblk - 1 - i, 0)), _full((1, DG)),
                                     _full((1, DG)), _full((1, DG))],
                          out_shape=[_sds((SEQ, 4 * DG)), _sds((1, DG)), _sds((1, DG)), _sds((1, DG))],
                          scratch_shapes=[pltpu.VMEM((DG, DG), F32)], name=f"hgrn_bwd{l}",
                          compiler_params=_cp(("arbitrary",)))(*ins)


def _ssd_chunk_fn(z, xs, bm, cm, dtr, sprev, dt_bias, a_log, d_par, norm_w, e128, tri, trit, seg128, ones128):
    lc = SSD_CHUNK
    dt = _softplus(dtr + dt_bias)
    a = -jnp.exp(a_log)
    da = dt * a * (lax.broadcasted_iota(jnp.int32, (1, 128), 1) < NH).astype(F32)
    cs = _dotx(tri, da)
    cst = _dotx_tn(da, trit)
    cs_b = _dotx(cs, e128)
    dt_b = _dotx(dt, e128)
    csl_b = _dotx(jnp.sum(da, axis=0, keepdims=True), e128)
    xdt = xs * dt_b
    lane = lax.broadcasted_iota(jnp.int32, (1, DG), 1)
    rowi = lax.broadcasted_iota(jnp.int32, (lc, lc), 0)
    coli = lax.broadcasted_iota(jnp.int32, (lc, lc), 1)
    y = jnp.zeros((lc, DG), F32)
    snew = jnp.zeros((DG, SSD_N), F32)
    d_b = jnp.zeros((1, DG), F32)
    wdec = xdt * jnp.exp(csl_b - cs_b)
    for g in range(2):
        bg = bm[:, g * SSD_N:(g + 1) * SSD_N]
        cg = cm[:, g * SSD_N:(g + 1) * SSD_N]
        gmat = _dot_nt(cg, bg)
        gmask = ((lane // 128) == g).astype(F32)
        snew = snew + _dot_tn(wdec * gmask, bg)
        y = y + _dot_nt(cg, sprev) * gmask * jnp.exp(cs_b)
        for hh in range(2):
            h = 2 * g + hh
            seg = jnp.where(rowi >= coli, cs[:, h:h + 1] - cst[h:h + 1, :], -1e30)
            hmask = ((lane // HD) == h).astype(F32)
            y = y + _dot(gmat * jnp.exp(seg), xdt * hmask)
            d_b = d_b + d_par[:, h:h + 1] * hmask
    cd = jnp.exp(_dotx_tn(_dotx(da, e128), ones128))
    snext = sprev * cd + snew
    y = y + xs * d_b
    y = y * _silu(z)
    ms = _dot(y * y, seg128) * (1.0 / 128.0)
    return y * lax.rsqrt(ms + RMS_EPS) * norm_w, snext


def ssd_fwd(l, proj, xc, p):
    nc = SEQ // SSD_CHUNK

    def body(z_ref, xs_ref, b_ref, c_ref, dt_ref, dtb_ref, al_ref, d_ref, nw_ref, e_ref, tri_ref, trit_ref, sg_ref,
             on_ref, y_ref, st_ref, s_sc):
        @pl.when(pl.program_id(0) == 0)
        def _():
            s_sc[...] = jnp.zeros_like(s_sc)

        sprev = s_sc[...]
        st_ref[0] = sprev
        y, snext = _ssd_chunk_fn(z_ref[...], xs_ref[...], b_ref[...], c_ref[...], dt_ref[...], sprev, dtb_ref[...],
                                 al_ref[...], d_ref[...], nw_ref[...], e_ref[...], tri_ref[...], trit_ref[...],
                                 sg_ref[...], on_ref[...])
        y_ref[...] = y
        s_sc[...] = snext

    rw = lambda w, cb: pl.BlockSpec((SSD_CHUNK, w), lambda i: (i, cb))
    ins = [proj, xc, xc, xc, proj, p["dt_bias"], p["a_log"], p["ssd_d"], p["ssd_norm_w"], p["e128"], p["tri128"],
           p["tri128t"], p["seg128"], p["ones128"]]
    specs = [rw(DG, C_Z // DG), rw(DG, 0), rw(DG, 1), rw(DG, 2), rw(128, C_DT // 128)] + [_full(x.shape) for x in ins[5:]]
    return pl.pallas_call(body, grid=(nc,), in_specs=specs,
                          out_specs=[rw(DG, 0), pl.BlockSpec((1, DG, SSD_N), lambda i: (i, 0, 0))],
                          out_shape=[_sds((SEQ, DG)), _sds((nc, DG, SSD_N))],
                          scratch_shapes=[pltpu.VMEM((DG, SSD_N), F32)], name=f"ssd_fwd{l}",
                          compiler_params=_cp(("arbitrary",)))(*ins)


def ssd_bwd(l, proj, xc, states, dy, p):
    nc = SEQ // SSD_CHUNK

    def body(z_ref, xs_ref, b_ref, c_ref, dt_ref, st_ref, dy_ref, dtb_ref, al_ref, d_ref, nw_ref, e_ref, tri_ref,
             trit_ref, sg_ref, on_ref, dz_ref, dxc_ref, ddt_ref, ddtb_ref, dal_ref, dd_ref, dnw_ref, ds_sc):
        @pl.when(pl.program_id(0) == 0)
        def _():
            ds_sc[...] = jnp.zeros_like(ds_sc)
            ddtb_ref[...] = jnp.zeros_like(ddtb_ref)
            dal_ref[...] = jnp.zeros_like(dal_ref)
            dd_ref[...] = jnp.zeros_like(dd_ref)
            dnw_ref[...] = jnp.zeros_like(dnw_ref)

        consts = (e_ref[...], tri_ref[...], trit_ref[...], sg_ref[...], on_ref[...])
        f = lambda *a: _ssd_chunk_fn(*a, *consts)
        _, vjp = jax.vjp(f, z_ref[...], xs_ref[...], b_ref[...], c_ref[...], dt_ref[...], st_ref[0], dtb_ref[...],
                         al_ref[...], d_ref[...], nw_ref[...])
        dz, dxs, db, dc, ddt, dsp, ddtb, dal, dd, dnw = vjp((dy_ref[...], ds_sc[...]))
        dz_ref[...] = dz
        dxc_ref[:, 0:DG] = dxs
        dxc_ref[:, DG:2 * DG] = db
        dxc_ref[:, 2 * DG:3 * DG] = dc
        ddt_ref[...] = ddt
        ds_sc[...] = dsp
        ddtb_ref[...] += ddtb
        dal_ref[...] += dal
        dd_ref[...] += dd
        dnw_ref[...] += dnw

    rw = lambda w, cb: pl.BlockSpec((SSD_CHUNK, w), lambda i: (nc - 1 - i, cb))
    ins = [proj, xc, xc, xc, proj, states, dy, p["dt_bias"], p["a_log"], p["ssd_d"], p["ssd_norm_w"], p["e128"],
           p["tri128"], p["tri128t"], p["seg128"], p["ones128"]]
    specs = [rw(DG, C_Z // DG), rw(DG, 0), rw(DG, 1), rw(DG, 2), rw(128, C_DT // 128),
             pl.BlockSpec((1, DG, SSD_N), lambda i: (nc - 1 - i, 0, 0)), rw(DG, 0)] + [_full(x.shape) for x in ins[7:]]
    return pl.pallas_call(body, grid=(nc,), in_specs=specs,
                          out_specs=[rw(DG, 0), rw(3 * DG, 0), rw(128, 0), _full((1, 128)), _full((1, 128)), _full((1, 128)),
                                     _full((1, DG))],
                          out_shape=[_sds((SEQ, DG)), _sds((SEQ, 3 * DG)), _sds((SEQ, 128)), _sds((1, 128)), _sds((1, 128)),
                                     _sds((1, 128)), _sds((1, DG))],
                          scratch_shapes=[pltpu.VMEM((DG, SSD_N), F32)], name=f"ssd_bwd{l}",
                          compiler_params=_cp(("arbitrary",)))(*ins)


ATT_BLK = 128


def _att_scores(qn, kc, kp, h, dil, has_prev):
    i = lax.broadcasted_iota(jnp.int32, (ATT_BLK, ATT_BLK), 0)
    j = lax.broadcasted_iota(jnp.int32, (ATT_BLK, ATT_BLK), 1)
    slope = 2.0 ** (-8.0 * (h + 1) / NH)
    scale = HD ** -0.5
    s_c = _dot_nt(qn, kc) * scale - slope * ((i - j) * dil).astype(F32)
    s_p = _dot_nt(qn, kp) * scale - slope * ((ATT_BLK + i - j) * dil).astype(F32)
    m_c = j <= i
    m_p = jnp.logical_and(j >= i, has_prev)
    return jnp.where(m_c, s_c, -1e30), jnp.where(m_p, s_p, -1e30), m_c, m_p


def _sub_spec(ln, width, col):
    return pl.BlockSpec((ln, DG), lambda z: (0, z * (width // DG) + col // DG))


QKV_W = 3 * DG


def attn_branch_fwd(l, bi, qkv, dil):
    ln = SEQ // dil
    nb = ln // ATT_BLK

    def body(q_ref, k_ref, v_ref, o_ref, l_ref):
        def blk(n, carry):
            r0 = pl.multiple_of(n * ATT_BLK, ATT_BLK)
            rp = pl.multiple_of(jnp.maximum(n - 1, 0) * ATT_BLK, ATT_BLK)
            cur, prv = pl.ds(r0, ATT_BLK), pl.ds(rp, ATT_BLK)
            for h in range(NH):
                hs = slice(h * HD, (h + 1) * HD)
                qn, kc, vc, kp, vp = q_ref[cur, hs], k_ref[cur, hs], v_ref[cur, hs], k_ref[prv, hs], v_ref[prv, hs]
                s_c, s_p, m_c, m_p = _att_scores(qn, kc, kp, h, dil, n > 0)
                m = jnp.maximum(jnp.max(s_c, axis=1, keepdims=True), jnp.max(s_p, axis=1, keepdims=True))
                p_c = jnp.where(m_c, jnp.exp(s_c - m), 0.0)
                p_p = jnp.where(m_p, jnp.exp(s_p - m), 0.0)
                den = jnp.sum(p_c, axis=1, keepdims=True) + jnp.sum(p_p, axis=1, keepdims=True)
                o_ref[cur, hs] = (_dot(p_c, vc) + _dot(p_p, vp)) / den
                l_ref[cur, hs] = jnp.broadcast_to(m + jnp.log(den), (ATT_BLK, HD))
            return carry

        lax.fori_loop(0, nb, blk, 0)

    pv = qkv.reshape(ln, dil * QKV_W)
    out = pl.BlockSpec((ln, DG), lambda z: (0, z))
    o, lse = pl.pallas_call(body, grid=(dil,), in_specs=[_sub_spec(ln, QKV_W, 0), _sub_spec(ln, QKV_W, DG), _sub_spec(ln, QKV_W, 2 * DG)],
                            out_specs=[out, out], out_shape=[_sds((ln, dil * DG))] * 2, name=f"attn_fwd{l}_{bi}",
                            compiler_params=_cp(("parallel",)))(pv, pv, pv)
    return o.reshape(SEQ, DG), lse.reshape(SEQ, DG)


def attn_branch_bwd(l, bi, qkv, dil, dyb, lse_all, delta):
    ln = SEQ // dil
    nb = ln // ATT_BLK
    scale = HD ** -0.5

    def body(q_ref, k_ref, v_ref, do_ref, l_ref, dl_ref, dq_ref, dk_ref, dv_ref):
        dk_ref[...] = jnp.zeros_like(dk_ref)
        dv_ref[...] = jnp.zeros_like(dv_ref)

        def blk(n, carry):
            r0 = pl.multiple_of(n * ATT_BLK, ATT_BLK)
            rp = pl.multiple_of(jnp.maximum(n - 1, 0) * ATT_BLK, ATT_BLK)
            cur, prv = pl.ds(r0, ATT_BLK), pl.ds(rp, ATT_BLK)
            for h in range(NH):
                hs = slice(h * HD, (h + 1) * HD)
                qn, don = q_ref[cur, hs], do_ref[cur, hs]
                lse, dlt = l_ref[cur, h * HD:h * HD + 1], dl_ref[cur, h * HD:h * HD + 1]
                kc, vc, kp, vp = k_ref[cur, hs], v_ref[cur, hs], k_ref[prv, hs], v_ref[prv, hs]
                s_c, s_p, m_c, m_p = _att_scores(qn, kc, kp, h, dil, n > 0)
                p_c = jnp.where(m_c, jnp.exp(s_c - lse), 0.0)
                p_p = jnp.where(m_p, jnp.exp(s_p - lse), 0.0)
                ds_c = p_c * (_dot_nt(don, vc) - dlt)
                ds_p = p_p * (_dot_nt(don, vp) - dlt)
                dq_ref[cur, hs] = (_dot(ds_c, kc) + _dot(ds_p, kp)) * scale
                dv_ref[prv, hs] += _dot_tn(p_p, don)
                dk_ref[prv, hs] += _dot_tn(ds_p, qn) * scale
                dv_ref[cur, hs] += _dot_tn(p_c, don)
                dk_ref[cur, hs] += _dot_tn(ds_c, qn) * scale
            return carry

        lax.fori_loop(0, nb, blk, 0)

    pv = qkv.reshape(ln, dil * QKV_W)
    sub = lambda t: t.reshape(ln, dil * DG)
    row = pl.BlockSpec((ln, DG), lambda z: (0, z))
    outs = pl.pallas_call(body, grid=(dil,),
                          in_specs=[_sub_spec(ln, QKV_W, 0), _sub_spec(ln, QKV_W, DG), _sub_spec(ln, QKV_W, 2 * DG), row, row, row],
                          out_specs=[row] * 3, out_shape=[_sds((ln, dil * DG))] * 3, name=f"attn_bwd{l}_{bi}",
                          compiler_params=_cp(("parallel",)))(pv, pv, pv, sub(dyb), sub(lse_all), sub(delta))
    return [t.reshape(SEQ, DG) for t in outs]


def _attn_merge_fn(o1, o2, o3, l1, l2, l3):
    m = jnp.maximum(jnp.maximum(l1, l2), l3)
    w1, w2, w3 = jnp.exp(l1 - m), jnp.exp(l2 - m), jnp.exp(l3 - m)
    den = w1 + w2 + w3
    return (w1 * o1 + w2 * o2 + w3 * o3) / den, m + jnp.log(den)


def attn_merge(l, os_, ls_):
    ins = list(os_) + list(ls_)
    return _map_fwd(f"attn_merge{l}", _attn_merge_fn, (SEQ // RB,), ins, [_rows(DG)] * 6, [_sds((SEQ, DG))] * 2,
                    [_rows(DG)] * 2)


def attn_delta(l, dyb, yb, seg):
    fn = lambda d, y, s: (_dot(d * y, s),)
    return _map_fwd(f"attn_delta{l}", fn, (SEQ // RB,), [dyb, yb, seg], [_rows(DG), _rows(DG), _full((DG, DG))],
                    [_sds((SEQ, DG))], [_rows(DG)])[0]


def _ln_fn(x, mix, w, b):
    h = ALPHA * x + mix
    mu = jnp.mean(h, axis=-1, keepdims=True)
    d = h - mu
    var = jnp.mean(d * d, axis=-1, keepdims=True)
    return (d * lax.rsqrt(var + LN_EPS) * w + b,)


def ln_fwd(name, x, mix, w, b):
    specs = [_rows(D_MODEL), _rows(D_MODEL), _full((1, D_MODEL)), _full((1, D_MODEL))]
    return _map_fwd(name, _ln_fn, (SEQ // RB,), [x, mix, w, b], specs, [_sds((SEQ, D_MODEL))], [_rows(D_MODEL)])[0]


def ln_bwd(name, x, mix, w, b, dy):
    specs = [_rows(D_MODEL), _rows(D_MODEL), _full((1, D_MODEL)), _full((1, D_MODEL))]
    return _map_bwd(name, _ln_fn, (SEQ // RB,), [x, mix, w, b], specs, [[dy]], [[_rows(D_MODEL)]], want=[1, 2, 3],
                    acc=(2, 3))


def loss_call(y, tgt):
    def fn(yy, tt):
        e = yy - tt
        part = 0.5 * jnp.sum(jnp.sum(e * e, axis=-1, keepdims=True) * (1.0 / D_MODEL), axis=0, keepdims=True)
        return e * (1.0 / D_MODEL), jnp.broadcast_to(part, (8, 128))

    return _map_fwd("loss", fn, (SEQ // RB,), [y, tgt], [_rows(D_MODEL)] * 2,
                    [_sds((SEQ, D_MODEL)), _sds((SEQ // RB * 8, 128))],
                    [_rows(D_MODEL), pl.BlockSpec((8, 128), lambda i: (i, 0))])


def layer_fwd(l, x, vfirst, wts, p, gather=()):
    sv = {"x": x}
    proj = _mm(f"mm_in{l}", x, wts["w_in"], "nn", *MM_TILES["k1024"])
    fl = lerp_fwd(l, proj, p["mu"])
    xc = conv_fwd(l, proj, p["conv_w"], p["conv_b"])
    w, k2, v2, c, b, g = rwkv_pre_fwd(l, fl, vfirst, p)
    y_scan, states, sv["gathered"] = rwkv_scan_fwd(l, fl, w, k2, v2, c, b, p, gather)
    ya = rwkv_post_fwd(l, y_scan, fl, k2, v2, g, p)
    qkv = proj[:, C_AQ:C_AQ + 3 * DG]
    outs, lses = [], []
    for bi, (win, dil) in enumerate(DILATED):
        o, lse = attn_branch_fwd(l, bi, qkv, dil)
        outs.append(o)
        lses.append(lse)
    yb, lse_all = attn_merge(l, outs, lses)
    yc, ssd_states = ssd_fwd(l, proj, xc, p)
    yd, hg_states = hgrn_fwd(l, proj, p)
    ycat = jnp.concatenate([ya, yb, yc, yd], axis=1)
    mix = _mm(f"mm_out{l}", ycat, wts["w_out"], "nn", *MM_TILES["k1024"])
    x1 = ln_fwd(f"ln1_fwd{l}", x, mix, p["ln1_w"], p["ln1_b"])
    hh = _mm(f"mm_up{l}", x1, wts["w_up_t"], "nt", *MM_TILES["k1024"], epilogue="relu2")
    m2 = _mm(f"mm_down{l}", hh, wts["w_down"], "nn", *MM_TILES["k4096"])
    x2 = ln_fwd(f"ln2_fwd{l}", x1, m2, p["ln2_w"], p["ln2_b"])
    sv.update(proj=proj, fl=fl, xc=xc, w=w, k2=k2, v2=v2, c=c, b=b, g=g, y_scan=y_scan, states=states,
              yb=yb, lse_all=lse_all, ssd_states=ssd_states, hg_states=hg_states, ycat=ycat, mix=mix, x1=x1, hh=hh, qkv=qkv,
              m2=m2, vfirst=vfirst)
    return x2, sv


def layer_bwd(l, dx2, dvfirst_next, sv, wts, p, exchange=()):
    gr = {}
    x, x1, proj, fl = sv["x"], sv["x1"], sv["proj"], sv["fl"]
    dres2, gr["ln2_w"], gr["ln2_b"] = ln_bwd(f"ln2_bwd{l}", x1, sv["m2"], p["ln2_w"], p["ln2_b"], dx2)
    du = _mm(f"mm_down_dx{l}", dres2, wts["w_down"], "nt", *MM_TILES["k1024"], add=sv["hh"], epilogue="relu2_bwd")
    gr["w_down"] = _mm(f"mm_down_dw{l}", sv["hh"], dres2, "tn", *MM_TILES["wgrad_tall"])
    dx1 = _mm(f"mm_up_dx{l}", du, wts["w_up_t"], "nn", *MM_TILES["k4096"], add=dres2, add_scale=ALPHA)
    gr["w_up_t"] = _mm(f"mm_up_dw{l}", du, x1, "tn", *MM_TILES["wgrad_tall"])
    dres1, gr["ln1_w"], gr["ln1_b"] = ln_bwd(f"ln1_bwd{l}", x, sv["mix"], p["ln1_w"], p["ln1_b"], dx1)
    dycat = _mm(f"mm_out_dx{l}", dres1, wts["w_out"], "nt", *MM_TILES["k1024"])
    gr["w_out"] = _mm(f"mm_out_dw{l}", sv["ycat"], dres1, "tn", 1024, 1024, 512)
    dya, dyb, dyc, dyd = (dycat[:, i * DG:(i + 1) * DG] for i in range(4))
    dhg4, gr["lb0"], gr["lb1"], gr["hgrn_norm_w"] = hgrn_bwd(l, proj, sv["hg_states"], dyd, p)
    dz, dxc, ddt, gr["dt_bias"], gr["a_log"], gr["ssd_d"], gr["ssd_norm_w"] = ssd_bwd(l, proj, sv["xc"], sv["ssd_states"], dyc, p)
    dxbc, gr["conv_w"], gr["conv_b"] = conv_bwd(l, proj, p["conv_w"], p["conv_b"], dxc)
    delta = attn_delta(l, dyb, sv["yb"], p["seg64"])
    dqs, dks, dvs = [], [], []
    for bi, (win, dil) in enumerate(DILATED):
        dq, dk, dv = attn_branch_bwd(l, bi, sv["qkv"], dil, dyb, sv["lse_all"], delta)
        dqs.append(dq)
        dks.append(dk)
        dvs.append(dv)
    dq_a, dk_a, dv_a = _addn(f"attn_dq{l}", *dqs), _addn(f"attn_dk{l}", *dks), _addn(f"attn_dv{l}", *dvs)
    pg = rwkv_post_bwd(l, sv["y_scan"], fl, sv["k2"], sv["v2"], sv["g"], p, dya)
    gr["lnx_w"], gr["lnx_b"], gr["r_k"] = pg["lnx_w"], pg["lnx_b"], pg["r_k"]
    (dr, dw, dk, dv, dc, db), gr["exchanged"] = rwkv_scan_bwd(l, fl, sv["w"], sv["k2"], sv["v2"], sv["c"], sv["b"],
                                                              sv["states"], pg["y"], p, exchange)
    v2_cts = [dv, pg["v2"]] + ([dvfirst_next] if dvfirst_next is not None else [])
    qg = rwkv_pre_bwd(l, fl, sv["vfirst"], p, [[dw], [dk, pg["k2"]], v2_cts, [dc], [db], [pg["g"]]])
    for nme in ("w0", "w2p", "a0", "a2p", "g2p", "k_k", "k_a", "v0", "v2p"):
        if nme in qg:
            gr[nme] = qg[nme]
    dfr = _addn(f"rwkv_dr{l}", dr, pg["fr"])
    dvres = qg["fvres"] if l > 0 else jnp.zeros((SEQ, 128), F32)
    dfl_out = jnp.concatenate([dfr, qg["fk"], qg["fv"], qg["flora"], dvres], axis=1)
    dfl_in, gr["mu"] = lerp_bwd(l, proj, p["mu"], dfl_out)
    dproj = jnp.concatenate([dfl_in[:, 0:768], dq_a, dk_a, dv_a, dz, dxbc, dhg4, dfl_in[:, 768:896], ddt,
                             dfl_in[:, 896:1024], jnp.zeros((SEQ, 128), F32)], axis=1)
    dx = _mm(f"mm_in_dx{l}", dproj, wts["w_in"], "nt", *MM_TILES["k4096"], add=dres1, add_scale=ALPHA)
    gr["w_in"] = _mm(f"mm_in_dw{l}", x, dproj, "tn", *MM_TILES["wgrad_wide"])
    return dx, (qg["vfirst"] if l > 0 else None), gr


def _w_in_pad(w_in_l, w_vres):
    rows = w_in_l.shape[0]
    z = lambda n: jnp.zeros((rows, n), w_in_l.dtype)
    vres = z(128) if w_vres is None else jnp.concatenate([w_vres, z(96)], axis=1)
    return jnp.concatenate([w_in_l[:, 0:768], w_in_l[:, 896:1664], w_in_l[:, 1664:1920], w_in_l[:, 1920:2688],
                            w_in_l[:, 2692:3716], w_in_l[:, 768:896], w_in_l[:, 2688:2692], z(124), vres, z(128)], axis=1)


def _w_in_unpad(g):
    g_in = jnp.concatenate([g[:, 0:768], g[:, C_LORA:C_LORA + 128], g[:, 768:1536], g[:, C_Z:C_Z + 256],
                            g[:, C_XBC:C_XBC + 768], g[:, C_DT:C_DT + 4], g[:, C_HQ:C_HQ + 1024]], axis=1)
    return g_in, g[:, C_VRES:C_VRES + 32]


def _consts():
    pair = jnp.arange(HG_NC * HGRN_CHUNK * HGRN_CHUNK)
    i128 = jnp.arange(128)
    same_chunk = (i128[:, None] // HGRN_CHUNK) == (i128[None, :] // HGRN_CHUNK)
    seg64 = _seg_ones(DG, HD)
    tri128 = (i128[:, None] >= i128[None, :]).astype(F32)
    return dict(
        seg64=seg64, seg64x3_bf16=jnp.concatenate([seg64, seg64, seg64], axis=0).astype(BF16),
        dmask=(jnp.arange(HD)[:, None] == (jnp.arange(DG)[None, :] % HD)).astype(F32),
        tri_bd128=(same_chunk & (i128[:, None] >= i128[None, :])).astype(F32), ones_bd128=same_chunk.astype(F32),
        first_row=(i128[None, :] == (jnp.arange(HG_NC) * HGRN_CHUNK)[:, None]).astype(F32),
        causal_blk=jnp.broadcast_to((((pair // HGRN_CHUNK) % HGRN_CHUNK) >= (pair % HGRN_CHUNK)).astype(F32)[:, None],
                                    (HG_NC * HGRN_CHUNK * HGRN_CHUNK, DG)),
        e128=((i128[:, None] == (jnp.arange(DG)[None, :] // HD)) & (i128[:, None] < NH)).astype(F32),
        tri128=tri128, tri128t=tri128.T, seg128=_seg_ones(DG, 128), ones128=jnp.ones((128, 128), F32))


def _pad_lanes(v, n):
    return jnp.concatenate([v, jnp.zeros((n - v.shape[0],), v.dtype)])[None, :]


def _layer_params(l, raw, consts):
    p = dict(consts)
    row = lambda name: raw[name][l][None, :]
    z = lambda r: jnp.zeros((r, DG), F32)
    mu_vres = raw["mu_vres"][l - 1] if l > 0 else jnp.zeros((32,), F32)
    p["mu"] = jnp.concatenate([raw["mu_shift"][l], mu_vres, jnp.zeros((96,), F32)])[None, :]
    p["conv_w"], p["conv_b"] = raw["ssd_conv_w"][l], row("ssd_conv_b")
    p["w0"], p["a0"], p["k_k"], p["k_a"] = row("rwkv_w0"), row("rwkv_a0"), row("rwkv_k_k"), row("rwkv_k_a")
    p["lnx_w"], p["lnx_b"] = row("rwkv_lnx_w"), row("rwkv_lnx_b")
    p["r_k"] = raw["rwkv_r_k"][l].reshape(1, DG)
    p["w2p"] = jnp.concatenate([raw["rwkv_w2"][l], z(96)], axis=0)
    p["a2p"] = jnp.concatenate([z(32), raw["rwkv_a2"][l], z(64)], axis=0)
    p["g2p"] = jnp.concatenate([z(64), raw["rwkv_g2"][l]], axis=0)
    if l > 0:
        p["v0"] = raw["rwkv_v0"][l - 1][None, :]
        p["v2p"] = jnp.concatenate([raw["rwkv_v2"][l - 1], z(96)], axis=0)
    p["lb0"], p["lb1"] = raw["lower_bounds"][0:1], raw["lower_bounds"][1:2]
    p["hgrn_norm_w"], p["ssd_norm_w"] = row("hgrn_norm_w"), row("ssd_norm_w")
    p["dt_bias"], p["a_log"], p["ssd_d"] = (_pad_lanes(raw[n][l], 128) for n in ("ssd_dt_bias", "ssd_A_log", "ssd_D"))
    for n in ("ln1_w", "ln1_b", "ln2_w", "ln2_b"):
        p[n] = row(n)
    return p


def _natural_grads(g0, g1):
    gs = (g0, g1)
    st = lambda key, f=lambda a: a[0]: jnp.stack([f(g[key]) for g in gs])
    out = {}
    out["lower_bounds"] = jnp.concatenate([g0["lb0"] + g1["lb0"], g0["lb1"] + g1["lb1"]], axis=0)
    out["mu_shift"] = st("mu", lambda a: a[0, :896])
    out["mu_vres"] = g1["mu"][:, 896:928]
    out["rwkv_w0"], out["rwkv_a0"], out["rwkv_k_k"], out["rwkv_k_a"] = st("w0"), st("a0"), st("k_k"), st("k_a")
    out["rwkv_w2"] = st("w2p", lambda a: a[0:32])
    out["rwkv_a2"] = st("a2p", lambda a: a[32:64])
    out["rwkv_g2"] = st("g2p", lambda a: a[64:128])
    out["rwkv_r_k"] = st("r_k", lambda a: a.reshape(NH, HD))
    out["rwkv_lnx_w"], out["rwkv_lnx_b"] = st("lnx_w"), st("lnx_b")
    out["rwkv_v0"] = g1["v0"]
    out["rwkv_v2"] = g1["v2p"][None, 0:32]
    out["ssd_conv_w"] = st("conv_w", lambda a: a)
    out["ssd_conv_b"] = st("conv_b")
    out["ssd_dt_bias"], out["ssd_A_log"], out["ssd_D"] = (st(k, lambda a: a[0, :NH]) for k in ("dt_bias", "a_log", "ssd_d"))
    out["ssd_norm_w"], out["hgrn_norm_w"] = st("ssd_norm_w"), st("hgrn_norm_w")
    for n in ("ln1_w", "ln1_b", "ln2_w", "ln2_b"):
        out[n] = st(n)
    return out


MESH_T = pl.DeviceIdType.MESH
ANY = pl.BlockSpec(memory_space=pl.ANY)


def _dev_index(px, py, pc):
    return 4 * px + 2 * py + pc


def all_gather(arrs):
    n = len(arrs)

    def body(*refs):
        begin, end = _gather_steps(refs[:n], refs[n:2 * n], *refs[2 * n:])
        begin()
        end()

    return pl.pallas_call(body, in_specs=[ANY] * n, out_specs=[ANY] * n, out_shape=_gather_shapes(arrs),
                          scratch_shapes=_gather_sems(n), name="all_gather")(*arrs)


def _gather_shapes(arrs):
    return [_sds((N_DEV,) + a.shape, a.dtype) for a in arrs]


def _gather_sems(n):
    return [pltpu.SemaphoreType.DMA((n, 7)), pltpu.SemaphoreType.DMA((n, 7)), pltpu.SemaphoreType.DMA((n,))]


def _gather_steps(ins, outs, send_sems, recv_sems, local_sems):
    n = len(ins)
    x, y, c = lax.axis_index("x"), lax.axis_index("y"), lax.axis_index("c")
    me, sibling = (x, y, c), (x, y, 1 - c)
    chips = [(1 - x, y), (x, 1 - y), (1 - x, 1 - y)]

    def copy(a, k, block, to, src=None):
        slot = outs[a].at[_dev_index(*block)]
        return pltpu.make_async_remote_copy(src_ref=slot if src is None else src, dst_ref=slot,
                                            send_sem=send_sems.at[a, k], recv_sem=recv_sems.at[a, k],
                                            device_id=to, device_id_type=MESH_T)

    def own_copies():
        mine = [pltpu.make_async_copy(ins[a], outs[a].at[_dev_index(*me)], local_sems.at[a]) for a in range(n)]
        first = []
        for a in range(n):
            first.append(copy(a, 0, me, sibling, src=ins[a]))
            first += [copy(a, 1 + j, me, (*chip, c), src=ins[a]) for j, chip in enumerate(chips)]
        return mine, first

    def begin():
        mine, first = own_copies()
        for cp in mine + first:
            cp.start()

    def end():
        mine, first = own_copies()
        passed = []
        for j, chip in enumerate(chips):
            for a in range(n):
                copy(a, 1 + j, (*chip, c), me).wait_recv()
                fwd = copy(a, 4 + j, (*chip, c), sibling)
                fwd.start()
                passed.append(fwd)
        for a in range(n):
            copy(a, 0, sibling, me).wait_recv()
            for j, chip in enumerate(chips):
                copy(a, 4 + j, (*chip, 1 - c), me).wait_recv()
        for cp in first + passed:
            cp.wait_send()
        for cp in mine:
            cp.wait()

    return begin, end


def _chips(x, y):
    return [(x, y), (1 - x, y), (x, 1 - y), (1 - x, 1 - y)]


def exchange_siblings(arrs, name):
    n = len(arrs)

    def body(*refs):
        ins, sib = refs[:n], refs[n:2 * n]
        send_sems, recv_sems = refs[2 * n:]
        x, y, c = lax.axis_index("x"), lax.axis_index("y"), lax.axis_index("c")
        sibling = (x, y, 1 - c)
        sends = []
        for a in range(n):
            for k, (cx, cy) in enumerate(_chips(x, y)):
                sd = pltpu.make_async_remote_copy(src_ref=ins[a].at[_dev_index(cx, cy, 1 - c)], dst_ref=sib[a].at[k],
                                                  send_sem=send_sems.at[a, k], recv_sem=recv_sems.at[a, k],
                                                  device_id=sibling, device_id_type=MESH_T)
                sd.start()
                sends.append(sd)
        for sd in sends:
            sd.wait_recv()
        for sd in sends:
            sd.wait_send()

    sem = pltpu.SemaphoreType.DMA((n, 4))
    return pl.pallas_call(body, in_specs=[ANY] * n, out_specs=[ANY] * n,
                          out_shape=[_sds((4,) + a.shape[1:], a.dtype) for a in arrs],
                          scratch_shapes=[sem, sem], name=name)(*arrs)


def reduce_pair(name, send, slots, sib, wire_dtype):
    _, r, c = send.shape
    rb = min(r, 262144 // c)

    def body(slots_ref, m0, m1, m2, m3, s_ref, own_ref, part_ref):
        own_ref[...] = m0[...] + s_ref[0]
        for k, m_ref in enumerate((m1, m2, m3)):
            part_ref[k] = (m_ref[...] + s_ref[k + 1]).astype(wire_dtype)

    mine = [pl.BlockSpec((None, rb, c), lambda i, s, k=k: (s[k], i, 0)) for k in range(4)]
    grid_spec = pltpu.PrefetchScalarGridSpec(
        num_scalar_prefetch=1, grid=(r // rb,),
        in_specs=mine + [pl.BlockSpec((4, rb, c), lambda i, s: (0, i, 0))],
        out_specs=[pl.BlockSpec((rb, c), lambda i, s: (i, 0)), pl.BlockSpec((3, rb, c), lambda i, s: (0, i, 0))])
    return pl.pallas_call(body, grid_spec=grid_spec, out_shape=[_sds((r, c)), _sds((3, r, c), wire_dtype)], name=name,
                          compiler_params=_cp(("parallel",)))(slots, send, send, send, send, sib)


def _chip_exchange_steps(ins, recv, send_sems, recv_sems):
    x, y, c = lax.axis_index("x"), lax.axis_index("y"), lax.axis_index("c")

    def copies():
        return [pltpu.make_async_remote_copy(src_ref=ins[a].at[k], dst_ref=recv[a].at[k], send_sem=send_sems.at[a, k],
                                             recv_sem=recv_sems.at[a, k], device_id=(cx, cy, c), device_id_type=MESH_T)
                for a in range(len(ins)) for k, (cx, cy) in enumerate(_chips(x, y)[1:])]

    def begin():
        for cp in copies():
            cp.start()

    def end():
        cps = copies()
        for cp in cps:
            cp.wait_recv()
        for cp in cps:
            cp.wait_send()

    return begin, end


def exchange_chips(parts, rep):
    n = len(parts)

    def body(*refs):
        ins, rep_ref = refs[:n], refs[n]
        recv, rep_all = refs[n + 1:2 * n + 1], refs[2 * n + 1]
        send_sems, recv_sems, rsend_sems, rrecv_sems, local_sem = refs[2 * n + 2:]
        x, y, c = lax.axis_index("x"), lax.axis_index("y"), lax.axis_index("c")
        me = _dev_index(x, y, c)
        mine = pltpu.make_async_copy(rep_ref, rep_all.at[me], local_sem)
        mine.start()
        begin, end = _chip_exchange_steps(ins, recv, send_sems, recv_sems)
        begin()
        rels = [(rx, ry, rc) for rx in (0, 1) for ry in (0, 1) for rc in (0, 1)][1:]
        peers = [(jnp.where(rx, 1 - x, x), jnp.where(ry, 1 - y, y), jnp.where(rc, 1 - c, c)) for rx, ry, rc in rels]
        rcps = []
        for k, peer in enumerate(peers):
            cp = pltpu.make_async_remote_copy(src_ref=rep_ref, dst_ref=rep_all.at[me], send_sem=rsend_sems.at[k],
                                              recv_sem=rrecv_sems.at[k], device_id=peer, device_id_type=MESH_T)
            cp.start()
            rcps.append(cp)
        for k, peer in enumerate(peers):
            pltpu.make_async_remote_copy(src_ref=rep_ref, dst_ref=rep_all.at[_dev_index(*peer)], send_sem=rsend_sems.at[k],
                                         recv_sem=rrecv_sems.at[k], device_id=peer, device_id_type=MESH_T).wait_recv()
        end()
        for cp in rcps:
            cp.wait_send()
        mine.wait()

    outs = pl.pallas_call(
        body, in_specs=[ANY] * (n + 1), out_specs=[ANY] * (n + 1),
        out_shape=[_sds(a.shape, a.dtype) for a in parts] + [_sds((N_DEV,) + rep.shape, rep.dtype)],
        scratch_shapes=[pltpu.SemaphoreType.DMA((n, 3)), pltpu.SemaphoreType.DMA((n, 3)), pltpu.SemaphoreType.DMA((7,)),
                        pltpu.SemaphoreType.DMA((7,)), pltpu.SemaphoreType.DMA],
        name="exchange_chips")(*parts, rep)
    return outs[:n], outs[n]


def adamw(name, terms, w, m, v):
    r, c = w.shape
    rb = min(r, 262144 // c)
    c1 = 1.0 - ADAM_B1 ** ADAM_STEP
    c2 = 1.0 - ADAM_B2 ** ADAM_STEP
    nt = len(terms)

    def body(*refs):
        w_ref, m_ref, v_ref = refs[nt:nt + 3]
        g_ref, d_ref, nm_ref, nv_ref = refs[nt + 3:]
        g = refs[0][...].astype(F32)
        for t_ref in refs[1:nt]:
            g = g + t_ref[...].astype(F32)
        nm = ADAM_B1 * m_ref[...] + (1.0 - ADAM_B1) * g
        nv = ADAM_B2 * v_ref[...] + (1.0 - ADAM_B2) * (g * g)
        g_ref[...] = g
        nm_ref[...] = nm
        nv_ref[...] = nv
        d_ref[...] = -ADAM_LR * ((nm / c1) / (jnp.sqrt(nv / c2) + ADAM_EPS) + ADAM_WD * w_ref[...])

    blk = pl.BlockSpec((rb, c), lambda i: (i, 0))
    tspecs = [blk if k is None else pl.BlockSpec((None, rb, c), lambda i, k=k: (k, i, 0)) for _, k in terms]
    return pl.pallas_call(body, grid=(r // rb,), in_specs=tspecs + [blk] * 3, out_specs=[blk] * 4,
                          out_shape=[_sds((r, c))] * 4, name=name,
                          compiler_params=_cp(("parallel",)))(*[t for t, _ in terms], w, m, v)


SMS_ROWS = 16
REP_ROWS = 24
N_BIG = 8
SMALL_SHARDED = (("rwkv_w2", (2, 32, 32)), ("rwkv_a2", (2, 32, 32)), ("rwkv_g2", (2, 64, 32)), ("rwkv_v2", (1, 32, 32)),
                 ("ssd_conv_w", (2, 4, 96)))
REPLICATED = (("lower_bounds", (2, 256)), ("mu_shift", (2, 896)), ("mu_vres", (1, 32)), ("rwkv_w0", (2, 256)),
              ("rwkv_a0", (2, 256)), ("rwkv_k_k", (2, 256)), ("rwkv_k_a", (2, 256)), ("rwkv_r_k", (2, 4, 64)),
              ("rwkv_lnx_w", (2, 256)), ("rwkv_lnx_b", (2, 256)), ("rwkv_v0", (1, 256)), ("ssd_conv_b", (2, 768)),
              ("ssd_dt_bias", (2, 4)), ("ssd_A_log", (2, 4)), ("ssd_D", (2, 4)), ("ssd_norm_w", (2, 256)),
              ("hgrn_norm_w", (2, 256)), ("ln1_w", (2, 1024)), ("ln1_b", (2, 1024)), ("ln2_w", (2, 1024)),
              ("ln2_b", (2, 1024)))


def _flat_rows(parts, rows):
    flat = jnp.concatenate([a.reshape(-1) for a in parts])
    return jnp.concatenate([flat, jnp.zeros((rows * PACK_W - flat.shape[0],), flat.dtype)]).reshape(rows, PACK_W)


def _local_arrays(d):
    arrs = [_w_in_pad(d["w_in"][0], None), _w_in_pad(d["w_in"][1], d["w_in_vres"][0]), d["w_out"][0], d["w_out"][1],
            d["w_up"][0].T, d["w_up"][1].T, d["w_down"][0], d["w_down"][1],
            _flat_rows([d[n] for n, _ in SMALL_SHARDED], SMS_ROWS)]
    return arrs, _flat_rows([d[n] for n, _ in REPLICATED], REP_ROWS)


def _unflat(rows2d, table):
    flat, out, o = rows2d.reshape(-1), {}, 0
    for name, shape in table:
        n = 1
        for s in shape:
            n *= s
        out[name] = flat[o:o + n].reshape(shape)
        o += n
    return out


def _from_local_arrays(arrs, rep):
    d = {}
    g0, _ = _w_in_unpad(arrs[0])
    g1, gv = _w_in_unpad(arrs[1])
    d["w_in"], d["w_in_vres"] = jnp.stack([g0, g1]), gv[None]
    d["w_out"] = jnp.stack([arrs[2], arrs[3]])
    d["w_up"] = jnp.stack([arrs[4].T, arrs[5].T])
    d["w_down"] = jnp.stack([arrs[6], arrs[7]])
    d.update(_unflat(arrs[8], SMALL_SHARDED))
    d.update(_unflat(rep, REPLICATED))
    return d


def _small_sharded_full(gs):
    small, flat, o = {}, gs.reshape(N_DEV, -1), 0
    for name, shape in SMALL_SHARDED:
        n = shape[0] * shape[1] * shape[2]
        blk = flat[:, o:o + n].reshape((N_DEV,) + shape)
        small[name] = blk.transpose(1, 2, 0, 3).reshape(shape[0], shape[1], N_DEV * shape[2])
        o += n
    return small


def _owner_blocks(g):
    return g.reshape(N_DEV, g.shape[0] // N_DEV, g.shape[1])


def _small_send_arrays(small_grads):
    sms = []
    for name, shape in SMALL_SHARDED:
        g = small_grads[name].reshape(shape[0], shape[1], N_DEV, shape[2]).transpose(2, 0, 1, 3)
        sms.append(g.reshape(N_DEV, -1))
    sms = jnp.concatenate(sms, axis=1)
    sms = jnp.concatenate([sms, jnp.zeros((N_DEV, SMS_ROWS * PACK_W - sms.shape[1]), F32)], axis=1)
    return sms.reshape(N_DEV, SMS_ROWS, PACK_W), _flat_rows([small_grads[n] for n, _ in REPLICATED], REP_ROWS)


BIG_KEYS = ("w_in", "w_out", "w_up_t", "w_down")


def _weights_of(full):
    return dict(zip(BIG_KEYS, full))


def _local_step(x, tgt, wts, raw, gather1=(), reduce1=None):
    consts = _consts()
    ps = [_layer_params(l, raw, consts) for l in range(DEPTH)]
    x1, sv0 = layer_fwd(0, x, None, wts[0], ps[0], gather1)
    wts1 = _weights_of([g.reshape(N_DEV * g.shape[1], g.shape[2]) for g in sv0["gathered"]]) if gather1 else wts[1]
    x2, sv1 = layer_fwd(1, x1, sv0["fl"], wts1, ps[1])
    dy, lparts = loss_call(x2, tgt)
    loss = jnp.sum(lparts[::8, 0])
    dx1, dvfirst, g1 = layer_bwd(1, dy, None, sv1, wts1, ps[1])
    big1 = {k: g1[k] for k in BIG_KEYS}
    own1, parts1 = reduce1(big1) if reduce1 else ((), ())
    dx0, _, g0 = layer_bwd(0, dx1, dvfirst, sv0, wts[0], ps[0], parts1)
    big = [{k: g0[k] for k in BIG_KEYS}, big1]
    return loss, dx0, big, _natural_grads(g0, g1), ((own1, g0["exchanged"]) if reduce1 else None)


WEIGHT_NAMES = ("lower_bounds", "w_in", "w_in_vres", "mu_shift", "mu_vres", "rwkv_w0", "rwkv_w2", "rwkv_a0", "rwkv_a2",
                "rwkv_g2", "rwkv_k_k", "rwkv_k_a", "rwkv_r_k", "rwkv_lnx_w", "rwkv_lnx_b", "rwkv_v0", "rwkv_v2",
                "ssd_conv_w", "ssd_conv_b", "ssd_dt_bias", "ssd_A_log", "ssd_D", "ssd_norm_w", "hgrn_norm_w", "w_out",
                "ln1_w", "ln1_b", "w_up", "w_down", "ln2_w", "ln2_b")


def kernel(x, lower_bounds, w_in, w_in_vres, mu_shift, mu_vres, rwkv_w0, rwkv_w2, rwkv_a0, rwkv_a2, rwkv_g2, rwkv_k_k, rwkv_k_a, rwkv_r_k, rwkv_lnx_w, rwkv_lnx_b, rwkv_v0, rwkv_v2, ssd_conv_w, ssd_conv_b, ssd_dt_bias, ssd_A_log, ssd_D, ssd_norm_w, hgrn_norm_w, w_out, ln1_w, ln1_b, w_up, w_down, ln2_w, ln2_b, loss_target, m_lower_bounds, m_w_in, m_w_in_vres, m_mu_shift, m_mu_vres, m_rwkv_w0, m_rwkv_w2, m_rwkv_a0, m_rwkv_a2, m_rwkv_g2, m_rwkv_k_k, m_rwkv_k_a, m_rwkv_r_k, m_rwkv_lnx_w, m_rwkv_lnx_b, m_rwkv_v0, m_rwkv_v2, m_ssd_conv_w, m_ssd_conv_b, m_ssd_dt_bias, m_ssd_A_log, m_ssd_D, m_ssd_norm_w, m_hgrn_norm_w, m_w_out, m_ln1_w, m_ln1_b, m_w_up, m_w_down, m_ln2_w, m_ln2_b, v_lower_bounds, v_w_in, v_w_in_vres, v_mu_shift, v_mu_vres, v_rwkv_w0, v_rwkv_w2, v_rwkv_a0, v_rwkv_a2, v_rwkv_g2, v_rwkv_k_k, v_rwkv_k_a, v_rwkv_r_k, v_rwkv_lnx_w, v_rwkv_lnx_b, v_rwkv_v0, v_rwkv_v2, v_ssd_conv_w, v_ssd_conv_b, v_ssd_dt_bias, v_ssd_A_log, v_ssd_D, v_ssd_norm_w, v_hgrn_norm_w, v_w_out, v_ln1_w, v_ln1_b, v_w_up, v_w_down, v_ln2_w, v_ln2_b):
    given = dict(locals())
    w = {n: given[n] for n in WEIGHT_NAMES}
    w_arrs, w_rep = _local_arrays(w)
    m_arrs, m_rep = _local_arrays({n: given["m_" + n] for n in WEIGHT_NAMES})
    v_arrs, v_rep = _local_arrays({n: given["v_" + n] for n in WEIGHT_NAMES})
    layer0, layer1 = [0, 2, 4, 6], [1, 3, 5, 7]
    gathered0 = all_gather([w_arrs[a].astype(BF16) for a in layer0] + [w_arrs[N_BIG]])
    wts0 = _weights_of([g.reshape(N_DEV * g.shape[1], g.shape[2]) for g in gathered0[:4]])
    raw = {n: w[n] for n, _ in REPLICATED}
    raw.update(_small_sharded_full(gathered0[4]))
    mx, my, mc = lax.axis_index("x"), lax.axis_index("y"), lax.axis_index("c")
    slots = jnp.stack([_dev_index(cx, cy, mc) for cx, cy in _chips(mx, my)]).astype(jnp.int32)

    def pair_sums(tag, send, wire):
        sib = exchange_siblings(send, f"exchange_siblings{tag}")
        res = [reduce_pair(f"reduce_pair{tag}_{i}", s, slots, sb, dt) for i, (s, sb, dt) in enumerate(zip(send, sib, wire))]
        return [o for o, _ in res], [pt for _, pt in res]

    reduce1 = lambda big1: pair_sums(1, [_owner_blocks(big1[k]) for k in BIG_KEYS], [BF16] * 4)
    loss, dx, big, small_grads, (own1, recv1) = _local_step(
        x[0], loss_target[0], [wts0, None], raw, [w_arrs[a].astype(BF16) for a in layer1], reduce1)
    sms_send, rep = _small_send_arrays(small_grads)
    own0, parts0 = pair_sums(0, [_owner_blocks(big[0][k]) for k in BIG_KEYS] + [sms_send], [BF16] * 4 + [F32])
    recv0, rep_all = exchange_chips(parts0, rep)
    own, recv = [None] * (N_BIG + 1), [None] * (N_BIG + 1)
    for i, a in enumerate(layer0 + [N_BIG]):
        own[a], recv[a] = own0[i], recv0[i]
    for i, a in enumerate(layer1):
        own[a], recv[a] = own1[i], recv1[i]
    results = [adamw(f"adamw{a}", [(own[a], None), (recv[a], 0), (recv[a], 1), (recv[a], 2)], w_arrs[a], m_arrs[a], v_arrs[a])
               for a in range(N_BIG + 1)]
    rep_res = adamw("adamw_rep", [(rep_all, q) for q in range(N_DEV)], w_rep, m_rep, v_rep)
    loss = lax.psum(loss, ("x", "y", "c"))
    outs = [loss, dx[None]]
    for q in range(4):
        d = _from_local_arrays([res[q] for res in results], rep_res[q])
        outs += [d[n] for n in WEIGHT_NAMES]
    return tuple(outs)
```

```python
import functools

import jax
import jax.numpy as jnp
from jax import lax
from jax.experimental import pallas as pl
from jax.experimental.pallas import tpu as pltpu

F32 = jnp.float32
BF16 = jnp.bfloat16
HI = lax.Precision.HIGHEST

N_DEV = 8
SEQ = 2048
D_MODEL = 1024
D_FF = 4096
DG = 256
NH = 4
HD = 64
DEPTH = 2
ALPHA = (2.0 * DEPTH) ** 0.25
LN_EPS = 1e-5
RMS_EPS = 1e-5
GN_EPS = HD * 1e-5
IN_COLS = 3716
SSD_N = 128
SSD_CHUNK = 128
HGRN_CHUNK = 16
DILATED = ((128, 1), (512, 4), (2048, 16))

ADAM_LR, ADAM_B1, ADAM_B2, ADAM_EPS, ADAM_WD, ADAM_STEP = 0.001, 0.9, 0.999, 1e-08, 0.01, 10

PW = 4096
C_R, C_K, C_V = 0, 256, 512
C_AQ, C_AK, C_AV = 768, 1024, 1280
C_Z, C_XBC = 1536, 1792
C_HQ, C_HF, C_HI, C_HG = 2560, 2816, 3072, 3328
C_LORA, C_DT, C_VRES = 3584, 3712, 3840

RB = 256
VMEM_LIMIT = 56 * 1024 * 1024
PACK_W = 1024


def _cp(sem=None):
    return pltpu.CompilerParams(dimension_semantics=sem, vmem_limit_bytes=VMEM_LIMIT)


def _sds(shape, dt=F32):
    return jax.ShapeDtypeStruct(tuple(shape), dt)


def _rows(w, cb=0, rb=RB):
    return pl.BlockSpec((rb, w), lambda i: (i, cb))


def _full(shape):
    n = len(shape)
    return pl.BlockSpec(tuple(shape), lambda *_: (0,) * n)


def _sigmoid(x):
    return 1.0 / (1.0 + jnp.exp(-x))


def _silu(x):
    return x * _sigmoid(x)


def _softplus(x):
    return jnp.maximum(x, 0.0) + jnp.log(1.0 + jnp.exp(jnp.where(x > 0, -x, x)))


MID = lax.Precision.HIGH
NN, TN, NT = (((1,), (0,)), ((), ())), (((0,), (0,)), ((), ())), (((1,), (1,)), ((), ()))


def _dot(a, b):
    return lax.dot_general(a, b, NN, precision=MID, preferred_element_type=F32)


def _dot_tn(a, b):
    return lax.dot_general(a, b, TN, precision=MID, preferred_element_type=F32)


def _dot_nt(a, b):
    return lax.dot_general(a, b, NT, precision=MID, preferred_element_type=F32)


def _dotx(a, b):
    return lax.dot_general(a, b, NN, precision=HI, preferred_element_type=F32)


def _dotx_tn(a, b):
    return lax.dot_general(a, b, TN, precision=HI, preferred_element_type=F32)


def _seg_ones(n, seg):
    i = jnp.arange(n)
    return (i[:, None] // seg == i[None, :] // seg).astype(F32)


def _shift_down(x, s):
    row = lax.broadcasted_iota(jnp.int32, x.shape, 0)
    return jnp.where(row < s, 0.0, pltpu.roll(x, s, 0))


def _shift_up(x, s):
    n = x.shape[0]
    row = lax.broadcasted_iota(jnp.int32, x.shape, 0)
    return jnp.where(row >= n - s, 0.0, pltpu.roll(x, n - s, 0))


@functools.partial(jax.custom_vjp, nondiff_argnums=(1,))
def _tshift(x, s):
    return _shift_down(x, s)


def _tshift_fwd(x, s):
    return _shift_down(x, s), None


def _tshift_bwd(s, _, g):
    return (_shift_up(g, s),)


_tshift.defvjp(_tshift_fwd, _tshift_bwd)


def _map_fwd(name, fn, grid, ins, in_specs, out_shapes, out_specs):
    n_in = len(ins)

    def body(*refs):
        ys = fn(*[r[...] for r in refs[:n_in]])
        for r, y in zip(refs[n_in:], ys):
            r[...] = y

    return pl.pallas_call(body, grid=grid, in_specs=in_specs, out_specs=out_specs, out_shape=out_shapes,
                          name=name, compiler_params=_cp(("parallel",)))(*ins)


def _map_bwd(name, fn, grid, ins, in_specs, cts, ct_specs, want, acc=(), gout=None):
    n_in = len(ins)
    flat_cts = [c for group in cts for c in group]
    flat_specs = [s for group in ct_specs for s in group]
    n_ct = len(flat_cts)
    gout = gout or {}
    out_shapes = [gout[i][0] if i in gout else _sds(ins[i].shape) for i in want]
    out_specs = [gout[i][1] if i in gout else in_specs[i] for i in want]

    def body(*refs):
        xs = [r[...] for r in refs[:n_in]]
        cvals = [r[...] for r in refs[n_in:n_in + n_ct]]
        gouts = refs[n_in + n_ct:]
        cs, p = [], 0
        for group in cts:
            v = cvals[p]
            for q in range(1, len(group)):
                v = v + cvals[p + q]
            cs.append(v)
            p += len(group)

        def f(*wanted):
            full = list(xs)
            for i, w in zip(want, wanted):
                full[i] = w
            return tuple(fn(*full))

        _, vjp = jax.vjp(f, *[xs[i] for i in want])
        gs = vjp(tuple(cs))
        for o, i, g in zip(gouts, want, gs):
            if i in acc:
                @pl.when(pl.program_id(0) == 0)
                def _():
                    o[...] = jnp.zeros_like(o)

                o[...] += g
            else:
                o[...] = g

    sem = ("arbitrary",) if acc else ("parallel",)
    return pl.pallas_call(body, grid=grid, in_specs=list(in_specs) + flat_specs, out_specs=out_specs,
                          out_shape=out_shapes, name=name, compiler_params=_cp(sem))(*ins, *flat_cts)


def _addn(name, *arrs):
    n, c = arrs[0].shape

    def fn(*xs):
        r = xs[0]
        for x in xs[1:]:
            r = r + x
        return (r,)

    return _map_fwd(name, fn, (n // RB,), list(arrs), [_rows(c)] * len(arrs), [_sds((n, c))], [_rows(c)])[0]


MM_TILES = {"k1024": (2048, 512, 1024), "k4096": (1024, 1024, 1024), "wgrad_tall": (2048, 1024, 512),
            "wgrad_wide": (1024, 2048, 512)}


def _mm(name, a, b, mode, tm, tn, tk, add=None, add_scale=1.0, epilogue=None):
    if mode == "nn":
        (m, k), n = a.shape, b.shape[1]
    elif mode == "nt":
        (m, k), n = a.shape, b.shape[0]
    else:
        (k, m), n = a.shape, b.shape[1]
    nk = k // tk
    dn = {"nn": (((1,), (0,)), ((), ())), "nt": (((1,), (1,)), ((), ())), "tn": (((0,), (0,)), ((), ()))}[mode]

    def body(*refs):
        a_ref, b_ref = refs[:2]
        add_ref = refs[2] if add is not None else None
        o_ref = refs[3] if add is not None else refs[2]
        prod = lax.dot_general(a_ref[...].astype(BF16), b_ref[...].astype(BF16), dn, preferred_element_type=F32)

        def finish(r):
            if epilogue == "relu2":
                r = jnp.maximum(r, 0.0)
                r = r * r
            elif epilogue == "relu2_bwd":
                r = r * (2.0 * jnp.sqrt(add_ref[...]))
            elif add is not None:
                r = r + add_scale * add_ref[...]
            o_ref[...] = r

        if nk == 1:
            finish(prod)
        else:
            acc = refs[-1]
            kk = pl.program_id(2)

            @pl.when(kk == 0)
            def _():
                acc[...] = prod

            @pl.when(kk > 0)
            def _():
                acc[...] += prod

            @pl.when(kk == nk - 1)
            def _():
                finish(acc[...])

    a_spec = pl.BlockSpec((tk, tm), lambda i, j, q: (q, i)) if mode == "tn" else pl.BlockSpec((tm, tk), lambda i, j, q: (i, q))
    b_spec = pl.BlockSpec((tn, tk), lambda i, j, q: (j, q)) if mode == "nt" else pl.BlockSpec((tk, tn), lambda i, j, q: (q, j))
    o_spec = pl.BlockSpec((tm, tn), lambda i, j, q: (i, j))
    ins, specs = [a, b], [a_spec, b_spec]
    if add is not None:
        ins.append(add)
        specs.append(o_spec)
    return pl.pallas_call(body, grid=(m // tm, n // tn, nk), in_specs=specs, out_specs=o_spec, out_shape=_sds((m, n)),
                          scratch_shapes=[pltpu.VMEM((tm, tn), F32)] if nk > 1 else [], name=name,
                          compiler_params=_cp(("parallel", "parallel", "arbitrary")))(*ins)


LERP_BLOCKS = (0, 1, 2, 3, 4, 5, C_LORA // 128, C_VRES // 128)


def _lerp_colmap(j):
    r = jnp.where(j < 6, j, jnp.where(j == 6, C_LORA // 128, C_VRES // 128))
    return (0, r)


def _lerp_fn(f, mu):
    return (f + (_tshift(f, 1) - f) * mu,)


def _lerp_specs():
    return [pl.BlockSpec((SEQ, 128), _lerp_colmap), pl.BlockSpec((1, 128), lambda j: (0, j))]


def lerp_fwd(l, proj, mu):
    return _map_fwd(f"lerp_fwd{l}", _lerp_fn, (8,), [proj, mu], _lerp_specs(), [_sds((SEQ, 1024))],
                    [pl.BlockSpec((SEQ, 128), lambda j: (0, j))])[0]


def lerp_bwd(l, proj, mu, dfl):
    n_in = 2

    def body(f_ref, mu_ref, g_ref, df_ref, dmu_ref):
        _, vjp = jax.vjp(_lerp_fn, f_ref[...], mu_ref[...])
        df, dmu = vjp((g_ref[...],))
        df_ref[...] = df
        dmu_ref[...] = dmu

    cspec = pl.BlockSpec((SEQ, 128), lambda j: (0, j))
    return pl.pallas_call(body, grid=(8,), in_specs=_lerp_specs() + [cspec],
                          out_specs=[cspec, pl.BlockSpec((1, 128), lambda j: (0, j))],
                          out_shape=[_sds((SEQ, 1024)), _sds((1, 1024))], name=f"lerp_bwd{l}",
                          compiler_params=_cp(("parallel",)))(proj, mu, dfl)


def _conv_fn(x, w, b):
    y = x * w[3:4, :] + _tshift(x, 1) * w[2:3, :] + _tshift(x, 2) * w[1:2, :] + _tshift(x, 3) * w[0:1, :] + b
    return (_silu(y),)


def _conv_specs():
    return [pl.BlockSpec((SEQ, 128), lambda j: (0, C_XBC // 128 + j)), pl.BlockSpec((4, 128), lambda j: (0, j)),
            pl.BlockSpec((1, 128), lambda j: (0, j))]


def conv_fwd(l, proj, w, b):
    return _map_fwd(f"conv_fwd{l}", _conv_fn, (6,), [proj, w, b], _conv_specs(), [_sds((SEQ, 768))],
                    [pl.BlockSpec((SEQ, 128), lambda j: (0, j))])[0]


def conv_bwd(l, proj, w, b, dxc):
    def body(x_ref, w_ref, b_ref, g_ref, dx_ref, dw_ref, db_ref):
        _, vjp = jax.vjp(_conv_fn, x_ref[...], w_ref[...], b_ref[...])
        dx, dw, db = vjp((g_ref[...],))
        dx_ref[...] = dx
        dw_ref[...] = dw
        db_ref[...] = db

    cspec = pl.BlockSpec((SEQ, 128), lambda j: (0, j))
    return pl.pallas_call(body, grid=(6,), in_specs=_conv_specs() + [cspec],
                          out_specs=[cspec, pl.BlockSpec((4, 128), lambda j: (0, j)), pl.BlockSpec((1, 128), lambda j: (0, j))],
                          out_shape=[_sds((SEQ, 768)), _sds((4, 768)), _sds((1, 768))], name=f"conv_bwd{l}",
                          compiler_params=_cp(("parallel",)))(proj, w, b, dxc)


def _rwkv_pre_fn(has_vres):
    def fn(fk, fv, flora, *rest):
        if has_vres:
            fvres, vfirst, w0, w2p, a0, a2p, g2p, k_k, k_a, v0, v2p, seg = rest
        else:
            w0, w2p, a0, a2p, g2p, k_k, k_a, seg = rest
        w_log = -_softplus(-(w0 + _dot(jnp.tanh(flora), w2p))) - 0.5
        w = jnp.exp(-jnp.exp(w_log))
        a = _sigmoid(a0 + _dot(flora, a2p))
        g = _dot(_sigmoid(flora), g2p)
        if has_vres:
            v2 = fv + (vfirst - fv) * _sigmoid(v0 + _dot(fvres, v2p))
        else:
            v2 = fv * 1.0
        kk = fk * k_k
        kk = kk / jnp.maximum(jnp.sqrt(_dot(kk * kk, seg)), 1e-12)
        k2 = fk * (1.0 + (a - 1.0) * k_a)
        return w, k2, v2, -kk, kk * a, g

    return fn


def _rwkv_pre_args(fl, vfirst, p, has_vres):
    ins = [fl, fl, fl]
    specs = [_rows(256, 1), _rows(256, 2), _rows(128, 6)]
    if has_vres:
        ins += [fl, vfirst]
        specs += [_rows(128, 7), _rows(256, 2)]
    names = ["w0", "w2p", "a0", "a2p", "g2p", "k_k", "k_a"] + (["v0", "v2p"] if has_vres else []) + ["seg64"]
    for nme in names:
        ins.append(p[nme])
        specs.append(_full(p[nme].shape))
    return ins, specs, names


def rwkv_pre_fwd(l, fl, vfirst, p):
    has_vres = l > 0
    ins, specs, _ = _rwkv_pre_args(fl, vfirst, p, has_vres)
    return _map_fwd(f"rwkv_pre_fwd{l}", _rwkv_pre_fn(has_vres), (SEQ // RB,), ins, specs,
                    [_sds((SEQ, DG))] * 6, [_rows(DG)] * 6)


def rwkv_pre_bwd(l, fl, vfirst, p, cts):
    has_vres = l > 0
    ins, specs, names = _rwkv_pre_args(fl, vfirst, p, has_vres)
    n_row = 5 if has_vres else 3
    want = list(range(n_row)) + [n_row + i for i, nme in enumerate(names) if nme != "seg64"]
    acc = tuple(w for w in want if w >= n_row)
    ct_specs = [[_rows(DG)] * len(g) for g in cts]
    gout = {0: (_sds((SEQ, DG)), _rows(DG)), 1: (_sds((SEQ, DG)), _rows(DG)), 2: (_sds((SEQ, 128)), _rows(128))}
    if has_vres:
        gout[3] = (_sds((SEQ, 128)), _rows(128))
        gout[4] = (_sds((SEQ, DG)), _rows(DG))
    gs = _map_bwd(f"rwkv_pre_bwd{l}", _rwkv_pre_fn(has_vres), (SEQ // RB,), ins, specs, cts, ct_specs, want, acc, gout)
    keys = ["fk", "fv", "flora"] + (["fvres", "vfirst"] if has_vres else []) + [nme for nme in names if nme != "seg64"]
    return dict(zip(keys, gs))


def _rwkv_post_fn(y, fr, k2, v2, g, lnx_w, lnx_b, r_k, seg):
    mu = _dot(y, seg) * (1.0 / HD)
    d = y - mu
    var = _dot(d * d, seg) * (1.0 / HD)
    yn = d * lax.rsqrt(var + GN_EPS) * lnx_w + lnx_b
    bonus = _dot(fr * k2 * r_k, seg) * v2
    return ((yn + bonus) * g,)


def _rwkv_post_args(y, fl, k2, v2, g, p):
    ins = [y, fl, k2, v2, g, p["lnx_w"], p["lnx_b"], p["r_k"], p["seg64"]]
    specs = [_rows(DG), _rows(DG, 0), _rows(DG), _rows(DG), _rows(DG)] + [_full(x.shape) for x in ins[5:]]
    return ins, specs


def rwkv_post_fwd(l, y, fl, k2, v2, g, p):
    ins, specs = _rwkv_post_args(y, fl, k2, v2, g, p)
    return _map_fwd(f"rwkv_post_fwd{l}", _rwkv_post_fn, (SEQ // RB,), ins, specs, [_sds((SEQ, DG))], [_rows(DG)])[0]


def rwkv_post_bwd(l, y, fl, k2, v2, g, p, dya):
    ins, specs = _rwkv_post_args(y, fl, k2, v2, g, p)
    gs = _map_bwd(f"rwkv_post_bwd{l}", _rwkv_post_fn, (SEQ // RB,), ins, specs, [[dya]], [[_rows(DG)]],
                  want=[0, 1, 2, 3, 4, 5, 6, 7], acc=(5, 6, 7), gout={1: (_sds((SEQ, DG)), _rows(DG))})
    return dict(zip(["y", "fr", "k2", "v2", "g", "lnx_w", "lnx_b", "r_k"], gs))


SCAN_TB = 64


def _coltile8(rows8, dmask, ones_stack, parts):
    pieces, rest = [], rows8
    for q in range(parts):
        piece = rest.astype(BF16).astype(F32)
        if q < parts - 1:
            rest = rest - piece
        pieces.append((piece[:, None, :] * dmask[None]).reshape(8 * HD, DG).astype(BF16))
    x = pieces[0] if parts == 1 else jnp.concatenate(pieces, axis=1)
    return jnp.dot(x, ones_stack, preferred_element_type=F32).reshape(8, HD, DG)


def _coltiles_bf16(rows_list, dmask, ones_bf16):
    x = jnp.concatenate([(r8[:, None, :] * dmask[None]).reshape(8 * HD, DG).astype(BF16) for r8 in rows_list], axis=0)
    t = jnp.dot(x, ones_bf16, preferred_element_type=F32)
    return [t[q * 8 * HD:(q + 1) * 8 * HD].reshape(8, HD, DG) for q in range(len(rows_list))]


def _segrows8(x8, dmask, ones_bf16):
    t = jnp.dot(x8.reshape(8 * HD, DG).astype(BF16), ones_bf16, preferred_element_type=F32).reshape(8, HD, DG)
    return jnp.sum(t * dmask[None], axis=1)


def rwkv_scan_fwd(l, fl, w, k2, v2, c, b, p, gather=()):
    nblk = SEQ // SCAN_TB
    ng = len(gather)

    def body(*refs):
        r_ref, w_ref, k_ref, v_ref, c_ref, b_ref, ones_ref, dm_ref = refs[:8]
        y_ref, st_ref = refs[8 + ng:10 + ng]
        s_sc = refs[10 + 2 * ng]
        if ng:
            begin, end = _gather_steps(refs[8:8 + ng], refs[10 + ng:10 + 2 * ng], *refs[11 + 2 * ng:])

            @pl.when(pl.program_id(0) == 0)
            def _():
                begin()

        @pl.when(pl.program_id(0) == 0)
        def _():
            s_sc[...] = jnp.zeros_like(s_sc)

        ones3, ones = ones_ref[...], ones_ref[0:DG, :]
        dmask = dm_ref[...]

        def group(gi, carry):
            t0 = pl.multiple_of(gi * 8, 8)
            sl = pl.ds(t0, 8)
            v8 = v_ref[sl, :]
            wt = _coltile8(w_ref[sl, :], dmask, ones3, 3)
            ct, bt, kt, rt = _coltiles_bf16([c_ref[sl, :], b_ref[sl, :], k_ref[sl, :], r_ref[sl, :]], dmask, ones)
            t = s_sc[...]
            for j in range(8):
                sa = jnp.sum(t * ct[j], axis=0, keepdims=True)
                t = t * wt[j] + bt[j] * sa + kt[j] * v8[j:j + 1, :]
                st_ref[t0 + j] = t
            s_sc[...] = t
            y_ref[sl, :] = jnp.sum(st_ref[sl] * rt, axis=1)
            return carry

        lax.fori_loop(0, SCAN_TB // 8, group, 0)

        if ng:
            @pl.when(pl.program_id(0) == nblk - 1)
            def _():
                end()

    row = pl.BlockSpec((SCAN_TB, DG), lambda i: (i, 0))
    ins = [fl, w, k2, v2, c, b, p["seg64x3_bf16"], p["dmask"]] + list(gather)
    specs = [row] * 6 + [_full((3 * DG, DG)), _full((HD, DG))] + [ANY] * ng
    outs = pl.pallas_call(body, grid=(nblk,), in_specs=specs,
                          out_specs=[row, pl.BlockSpec((SCAN_TB, HD, DG), lambda i: (i, 0, 0))] + [ANY] * ng,
                          out_shape=[_sds((SEQ, DG)), _sds((SEQ, HD, DG))] + _gather_shapes(gather),
                          scratch_shapes=[pltpu.VMEM((HD, DG), F32)] + (_gather_sems(ng) if ng else []),
                          name=f"rwkv_scan_fwd{l}", compiler_params=_cp(("arbitrary",)))(*ins)
    return outs[0], outs[1], list(outs[2:])


def rwkv_scan_bwd(l, fl, w, k2, v2, c, b, states, dy, p, exchange=()):
    nblk = SEQ // SCAN_TB
    nx = len(exchange)

    def body(*refs):
        r_ref, w_ref, k_ref, v_ref, c_ref, b_ref, dy_ref, st_ref, sp_ref, ones_ref, dm_ref = refs[:11]
        dr_ref, dw_ref, dk_ref, dv_ref, dc_ref, db_ref = refs[11 + nx:17 + nx]
        g_sc, prev_sc, d8_sc, dsa_sc = refs[17 + 2 * nx:21 + 2 * nx]
        i = pl.program_id(0)
        if nx:
            begin, end = _chip_exchange_steps(refs[11:11 + nx], refs[17 + nx:17 + 2 * nx], *refs[21 + 2 * nx:])

            @pl.when(i == 0)
            def _():
                begin()

        @pl.when(i == 0)
        def _():
            g_sc[...] = jnp.zeros_like(g_sc)

        ones3, ones = ones_ref[...], ones_ref[0:DG, :]
        dmask = dm_ref[...]
        first_block = i == nblk - 1

        def group(gr, carry):
            gi = SCAN_TB // 8 - 1 - gr
            t0 = pl.multiple_of(gi * 8, 8)
            sl = pl.ds(t0, 8)
            v8, dy8 = v_ref[sl, :], dy_ref[sl, :]
            t8 = st_ref[sl]
            @pl.when(gi > 0)
            def _():
                prev_sc[0] = st_ref[t0 - 1]

            @pl.when(gi == 0)
            def _():
                prev_sc[0] = jnp.where(first_block, 0.0, sp_ref[0])

            for j in range(1, 8):
                prev_sc[j] = t8[j - 1]
            tp8 = prev_sc[...]
            wt = _coltile8(w_ref[sl, :], dmask, ones3, 3)
            ct, bt, kt, rt = _coltiles_bf16([c_ref[sl, :], b_ref[sl, :], k_ref[sl, :], r_ref[sl, :]], dmask, ones)
            sa8 = jnp.sum(tp8 * ct, axis=1)
            g = g_sc[...]
            for j in range(7, -1, -1):
                g = g + rt[j] * dy8[j:j + 1, :]
                d8_sc[j] = g
                dsa = jnp.sum(g * bt[j], axis=0, keepdims=True)
                dsa_sc[j:j + 1, :] = dsa
                g = g * wt[j] + ct[j] * dsa
            g_sc[...] = g
            d8 = d8_sc[...]
            dsa8 = dsa_sc[...]
            dv_ref[sl, :] = jnp.sum(d8 * kt, axis=1)
            dr_ref[sl, :] = _segrows8(t8 * dy8[:, None, :], dmask, ones)
            dk_ref[sl, :] = _segrows8(d8 * v8[:, None, :], dmask, ones)
            dw_ref[sl, :] = _segrows8(tp8 * d8, dmask, ones)
            db_ref[sl, :] = _segrows8(d8 * sa8[:, None, :], dmask, ones)
            dc_ref[sl, :] = _segrows8(tp8 * dsa8[:, None, :], dmask, ones)
            return carry

        lax.fori_loop(0, SCAN_TB // 8, group, 0)

        if nx:
            @pl.when(i == nblk - 1)
            def _():
                end()

    row = pl.BlockSpec((SCAN_TB, DG), lambda i: (nblk - 1 - i, 0))
    st_spec = pl.BlockSpec((SCAN_TB, HD, DG), lambda i: (nblk - 1 - i, 0, 0))
    sp_spec = pl.BlockSpec((1, HD, DG), lambda i: (jnp.maximum((nblk - 1 - i) * SCAN_TB - 1, 0), 0, 0))
    ins = [fl, w, k2, v2, c, b, dy, states, states, p["seg64x3_bf16"], p["dmask"]] + list(exchange)
    specs = [row] * 7 + [st_spec, sp_spec, _full((3 * DG, DG)), _full((HD, DG))] + [ANY] * nx
    tile8 = pltpu.VMEM((8, HD, DG), F32)
    sems = [pltpu.SemaphoreType.DMA((nx, 3)), pltpu.SemaphoreType.DMA((nx, 3))] if nx else []
    outs = pl.pallas_call(body, grid=(nblk,), in_specs=specs, out_specs=[row] * 6 + [ANY] * nx,
                          out_shape=[_sds((SEQ, DG))] * 6 + [_sds(a.shape, a.dtype) for a in exchange],
                          scratch_shapes=[pltpu.VMEM((HD, DG), F32), tile8, tile8, pltpu.VMEM((8, DG), F32)] + sems,
                          name=f"rwkv_scan_bwd{l}", compiler_params=_cp(("arbitrary",)))(*ins)
    return outs[:6], list(outs[6:])


HG_ROWS = 128


HG_NC = HG_ROWS // HGRN_CHUNK


def _hgrn_block_fn(layer):
    def fn(hq, hf, hi, hg, sprev, lb0, lb1, norm_w, seg, bd, tri_bd, ones_bd, first_row, causal):
        e0 = jnp.exp(lb0 - jnp.maximum(lb0, lb1))
        e1 = jnp.exp(lb1 - jnp.maximum(lb0, lb1))
        sm0, sm1 = e0 / (e0 + e1), e1 / (e0 + e1)
        lb = (sm0 - sm0) if layer == 0 else ((sm0 + sm1) - sm0)
        forget = lb + (1.0 - lb) * _sigmoid(hf)
        logf = jnp.log(forget)
        kk = 1.0 - forget
        q = _silu(hq)
        c, nc = HGRN_CHUNK, HG_NC
        b = _dotx(tri_bd, logf)
        bl = _dotx(ones_bd, logf)
        split = lambda t: t.reshape(nc, c, DG)
        b4 = split(b)
        diff = (b4[:, :, None, :] - b4[:, None, :, :]).reshape(nc * c * c, DG)
        dec = jnp.exp(jnp.where(causal > 0.5, diff, -1e30))
        qrep = jnp.broadcast_to(split(q)[:, :, None, :], (nc, c, c, DG)).reshape(nc * c * c, DG)
        ktil = jnp.broadcast_to(split(kk)[:, None, :, :], (nc, c, c, DG)).reshape(nc * c * c, DG)
        vtil = jnp.broadcast_to(split(hi)[:, None, :, :], (nc, c, c, DG)).reshape(nc * c * c, DG)
        att = _dot(qrep * ktil * dec, seg)
        o_intra = jnp.sum((att * vtil).reshape(nc * c, c, DG), axis=1)
        kd4 = split(kk * jnp.exp(bl - b))
        qe4 = split(q * jnp.exp(b))
        v4 = split(hi)
        tot = jnp.exp(_dotx(first_row, bl))
        s, o_inter = sprev, []
        for ci in range(nc):
            o_inter.append(_dot_nt(qe4[ci], s))
            s = s * tot[ci:ci + 1, :] + _dot_tn(v4[ci], kd4[ci]) * bd
        o = o_intra + jnp.concatenate(o_inter, axis=0)
        ms = _dot(o * o, seg) * (1.0 / HD)
        y = o * lax.rsqrt(ms + RMS_EPS) * norm_w * _silu(hg)
        return y, s

    return fn


def _hgrn_consts(p):
    return [p["seg64"], p["seg64"], p["tri_bd128"], p["ones_bd128"], p["first_row"], p["causal_blk"]]


def hgrn_fwd(l, proj, p):
    fn = _hgrn_block_fn(l)

    def body(hq_ref, hf_ref, hi_ref, hg_ref, *rest):
        const_refs, (y_ref, st_ref, s_sc) = rest[:-3], rest[-3:]

        @pl.when(pl.program_id(0) == 0)
        def _():
            s_sc[...] = jnp.zeros_like(s_sc)

        sprev = s_sc[...]
        st_ref[0] = sprev
        y, snext = fn(hq_ref[...], hf_ref[...], hi_ref[...], hg_ref[...], sprev, *[r[...] for r in const_refs])
        y_ref[...] = y
        s_sc[...] = snext

    rows = lambda cb: pl.BlockSpec((HG_ROWS, DG), lambda i: (i, cb))
    ins = [proj, proj, proj, proj, p["lb0"], p["lb1"], p["hgrn_norm_w"]] + _hgrn_consts(p)
    specs = [rows(C_HQ // DG), rows(C_HF // DG), rows(C_HI // DG), rows(C_HG // DG)] + [_full(x.shape) for x in ins[4:]]
    return pl.pallas_call(body, grid=(SEQ // HG_ROWS,), in_specs=specs,
                          out_specs=[rows(0), pl.BlockSpec((1, DG, DG), lambda i: (i, 0, 0))],
                          out_shape=[_sds((SEQ, DG)), _sds((SEQ // HG_ROWS, DG, DG))],
                          scratch_shapes=[pltpu.VMEM((DG, DG), F32)], name=f"hgrn_fwd{l}",
                          compiler_params=_cp(("arbitrary",)))(*ins)


def hgrn_bwd(l, proj, states, dy, p):
    fn = _hgrn_block_fn(l)
    nblk = SEQ // HG_ROWS
    n_const = len(_hgrn_consts(p))

    def body(hq_ref, hf_ref, hi_ref, hg_ref, st_ref, dy_ref, lb0_ref, lb1_ref, nw_ref, *rest):
        const_refs, (dp_ref, dlb0_ref, dlb1_ref, dnw_ref, ds_sc) = rest[:n_const], rest[n_const:]

        @pl.when(pl.program_id(0) == 0)
        def _():
            ds_sc[...] = jnp.zeros_like(ds_sc)
            dlb0_ref[...] = jnp.zeros_like(dlb0_ref)
            dlb1_ref[...] = jnp.zeros_like(dlb1_ref)
            dnw_ref[...] = jnp.zeros_like(dnw_ref)

        consts = [r[...] for r in const_refs]
        f = lambda hq, hf, hi, hg, sp, b0, b1, nw: fn(hq, hf, hi, hg, sp, b0, b1, nw, *consts)
        _, vjp = jax.vjp(f, hq_ref[...], hf_ref[...], hi_ref[...], hg_ref[...], st_ref[0], lb0_ref[...], lb1_ref[...],
                         nw_ref[...])
        dhq, dhf, dhi, dhg, dsp, dlb0, dlb1, dnw = vjp((dy_ref[...], ds_sc[...]))
        dp_ref[:, 0:DG] = dhq
        dp_ref[:, DG:2 * DG] = dhf
        dp_ref[:, 2 * DG:3 * DG] = dhi
        dp_ref[:, 3 * DG:4 * DG] = dhg
        ds_sc[...] = dsp
        dlb0_ref[...] += dlb0
        dlb1_ref[...] += dlb1
        dnw_ref[...] += dnw

    rows = lambda cb: pl.BlockSpec((HG_ROWS, DG), lambda i: (nblk - 1 - i, cb))
    ins = [proj, proj, proj, proj, states, dy, p["lb0"], p["lb1"], p["hgrn_norm_w"]] + _hgrn_consts(p)
    specs = [rows(C_HQ // DG), rows(C_HF // DG), rows(C_HI // DG), rows(C_HG // DG),
             pl.BlockSpec((1, DG, DG), lambda i: (nblk - 1 - i, 0, 0)), rows(0)] + [_full(x.shape) for x in ins[6:]]
    return pl.pallas_call(body, grid=(nblk,), in_specs=specs,
                          out_specs=[pl.BlockSpec((HG_ROWS, 4 * DG), lambda i: (nblk - 1 - i, 0)), _full((1, DG)),
                                     _full((1, DG)), _full((1, DG))],
                          out_shape=[_sds((SEQ, 4 * DG)), _sds((1, DG)), _sds((1, DG)), _sds((1, DG))],
                          scratch_shapes=[pltpu.VMEM((DG, DG), F32)], name=f"hgrn_bwd{l}",
                          compiler_params=_cp(("arbitrary",)))(*ins)


def _ssd_chunk_fn(z, xs, bm, cm, dtr, sprev, dt_bias, a_log, d_par, norm_w, e128, tri, trit, seg128, ones128):
    lc = SSD_CHUNK
    dt = _softplus(dtr + dt_bias)
    a = -jnp.exp(a_log)
    da = dt * a * (lax.broadcasted_iota(jnp.int32, (1, 128), 1) < NH).astype(F32)
    cs = _dotx(tri, da)
    cst = _dotx_tn(da, trit)
    cs_b = _dotx(cs, e128)
    dt_b = _dotx(dt, e128)
    csl_b = _dotx(jnp.sum(da, axis=0, keepdims=True), e128)
    xdt = xs * dt_b
    lane = lax.broadcasted_iota(jnp.int32, (1, DG), 1)
    rowi = lax.broadcasted_iota(jnp.int32, (lc, lc), 0)
    coli = lax.broadcasted_iota(jnp.int32, (lc, lc), 1)
    y = jnp.zeros((lc, DG), F32)
    snew = jnp.zeros((DG, SSD_N), F32)
    d_b = jnp.zeros((1, DG), F32)
    wdec = xdt * jnp.exp(csl_b - cs_b)
    for g in range(2):
        bg = bm[:, g * SSD_N:(g + 1) * SSD_N]
        cg = cm[:, g * SSD_N:(g + 1) * SSD_N]
        gmat = _dot_nt(cg, bg)
        gmask = ((lane // 128) == g).astype(F32)
        snew = snew + _dot_tn(wdec * gmask, bg)
        y = y + _dot_nt(cg, sprev) * gmask * jnp.exp(cs_b)
        for hh in range(2):
            h = 2 * g + hh
            seg = jnp.where(rowi >= coli, cs[:, h:h + 1] - cst[h:h + 1, :], -1e30)
            hmask = ((lane // HD) == h).astype(F32)
            y = y + _dot(gmat * jnp.exp(seg), xdt * hmask)
            d_b = d_b + d_par[:, h:h + 1] * hmask
    cd = jnp.exp(_dotx_tn(_dotx(da, e128), ones128))
    snext = sprev * cd + snew
    y = y + xs * d_b
    y = y * _silu(z)
    ms = _dot(y * y, seg128) * (1.0 / 128.0)
    return y * lax.rsqrt(ms + RMS_EPS) * norm_w, snext


def ssd_fwd(l, proj, xc, p):
    nc = SEQ // SSD_CHUNK

    def body(z_ref, xs_ref, b_ref, c_ref, dt_ref, dtb_ref, al_ref, d_ref, nw_ref, e_ref, tri_ref, trit_ref, sg_ref,
             on_ref, y_ref, st_ref, s_sc):
        @pl.when(pl.program_id(0) == 0)
        def _():
            s_sc[...] = jnp.zeros_like(s_sc)

        sprev = s_sc[...]
        st_ref[0] = sprev
        y, snext = _ssd_chunk_fn(z_ref[...], xs_ref[...], b_ref[...], c_ref[...], dt_ref[...], sprev, dtb_ref[...],
                                 al_ref[...], d_ref[...], nw_ref[...], e_ref[...], tri_ref[...], trit_ref[...],
                                 sg_ref[...], on_ref[...])
        y_ref[...] = y
        s_sc[...] = snext

    rw = lambda w, cb: pl.BlockSpec((SSD_CHUNK, w), lambda i: (i, cb))
    ins = [proj, xc, xc, xc, proj, p["dt_bias"], p["a_log"], p["ssd_d"], p["ssd_norm_w"], p["e128"], p["tri128"],
           p["tri128t"], p["seg128"], p["ones128"]]
    specs = [rw(DG, C_Z // DG), rw(DG, 0), rw(DG, 1), rw(DG, 2), rw(128, C_DT // 128)] + [_full(x.shape) for x in ins[5:]]
    return pl.pallas_call(body, grid=(nc,), in_specs=specs,
                          out_specs=[rw(DG, 0), pl.BlockSpec((1, DG, SSD_N), lambda i: (i, 0, 0))],
                          out_shape=[_sds((SEQ, DG)), _sds((nc, DG, SSD_N))],
                          scratch_shapes=[pltpu.VMEM((DG, SSD_N), F32)], name=f"ssd_fwd{l}",
                          compiler_params=_cp(("arbitrary",)))(*ins)


def ssd_bwd(l, proj, xc, states, dy, p):
    nc = SEQ // SSD_CHUNK

    def body(z_ref, xs_ref, b_ref, c_ref, dt_ref, st_ref, dy_ref, dtb_ref, al_ref, d_ref, nw_ref, e_ref, tri_ref,
             trit_ref, sg_ref, on_ref, dz_ref, dxc_ref, ddt_ref, ddtb_ref, dal_ref, dd_ref, dnw_ref, ds_sc):
        @pl.when(pl.program_id(0) == 0)
        def _():
            ds_sc[...] = jnp.zeros_like(ds_sc)
            ddtb_ref[...] = jnp.zeros_like(ddtb_ref)
            dal_ref[...] = jnp.zeros_like(dal_ref)
            dd_ref[...] = jnp.zeros_like(dd_ref)
            dnw_ref[...] = jnp.zeros_like(dnw_ref)

        consts = (e_ref[...], tri_ref[...], trit_ref[...], sg_ref[...], on_ref[...])
        f = lambda *a: _ssd_chunk_fn(*a, *consts)
        _, vjp = jax.vjp(f, z_ref[...], xs_ref[...], b_ref[...], c_ref[...], dt_ref[...], st_ref[0], dtb_ref[...],
                         al_ref[...], d_ref[...], nw_ref[...])
        dz, dxs, db, dc, ddt, dsp, ddtb, dal, dd, dnw = vjp((dy_ref[...], ds_sc[...]))
        dz_ref[...] = dz
        dxc_ref[:, 0:DG] = dxs
        dxc_ref[:, DG:2 * DG] = db
        dxc_ref[:, 2 * DG:3 * DG] = dc
        ddt_ref[...] = ddt
        ds_sc[...] = dsp
        ddtb_ref[...] += ddtb
        dal_ref[...] += dal
        dd_ref[...] += dd
        dnw_ref[...] += dnw

    rw = lambda w, cb: pl.BlockSpec((SSD_CHUNK, w), lambda i: (nc - 1 - i, cb))
    ins = [proj, xc, xc, xc, proj, states, dy, p["dt_bias"], p["a_log"], p["ssd_d"], p["ssd_norm_w"], p["e128"],
           p["tri128"], p["tri128t"], p["seg128"], p["ones128"]]
    specs = [rw(DG, C_Z // DG), rw(DG, 0), rw(DG, 1), rw(DG, 2), rw(128, C_DT // 128),
             pl.BlockSpec((1, DG, SSD_N), lambda i: (nc - 1 - i, 0, 0)), rw(DG, 0)] + [_full(x.shape) for x in ins[7:]]
    return pl.pallas_call(body, grid=(nc,), in_specs=specs,
                          out_specs=[rw(DG, 0), rw(3 * DG, 0), rw(128, 0), _full((1, 128)), _full((1, 128)), _full((1, 128)),
                                     _full((1, DG))],
                          out_shape=[_sds((SEQ, DG)), _sds((SEQ, 3 * DG)), _sds((SEQ, 128)), _sds((1, 128)), _sds((1, 128)),
                                     _sds((1, 128)), _sds((1, DG))],
                          scratch_shapes=[pltpu.VMEM((DG, SSD_N), F32)], name=f"ssd_bwd{l}",
                          compiler_params=_cp(("arbitrary",)))(*ins)


ATT_BLK = 128


def _att_scores(qn, kc, kp, h, dil, has_prev):
    i = lax.broadcasted_iota(jnp.int32, (ATT_BLK, ATT_BLK), 0)
    j = lax.broadcasted_iota(jnp.int32, (ATT_BLK, ATT_BLK), 1)
    slope = 2.0 ** (-8.0 * (h + 1) / NH)
    scale = HD ** -0.5
    s_c = _dot_nt(qn, kc) * scale - slope * ((i - j) * dil).astype(F32)
    s_p = _dot_nt(qn, kp) * scale - slope * ((ATT_BLK + i - j) * dil).astype(F32)
    m_c = j <= i
    m_p = jnp.logical_and(j >= i, has_prev)
    return jnp.where(m_c, s_c, -1e30), jnp.where(m_p, s_p, -1e30), m_c, m_p


def _sub_spec(ln, width, col):
    return pl.BlockSpec((ln, DG), lambda z: (0, z * (width // DG) + col // DG))


QKV_W = 3 * DG


def attn_branch_fwd(l, bi, qkv, dil):
    ln = SEQ // dil
    nb = ln // ATT_BLK

    def body(q_ref, k_ref, v_ref, o_ref, l_ref):
        def blk(n, carry):
            r0 = pl.multiple_of(n * ATT_BLK, ATT_BLK)
            rp = pl.multiple_of(jnp.maximum(n - 1, 0) * ATT_BLK, ATT_BLK)
            cur, prv = pl.ds(r0, ATT_BLK), pl.ds(rp, ATT_BLK)
            for h in range(NH):
                hs = slice(h * HD, (h + 1) * HD)
                qn, kc, vc, kp, vp = q_ref[cur, hs], k_ref[cur, hs], v_ref[cur, hs], k_ref[prv, hs], v_ref[prv, hs]
                s_c, s_p, m_c, m_p = _att_scores(qn, kc, kp, h, dil, n > 0)
                m = jnp.maximum(jnp.max(s_c, axis=1, keepdims=True), jnp.max(s_p, axis=1, keepdims=True))
                p_c = jnp.where(m_c, jnp.exp(s_c - m), 0.0)
                p_p = jnp.where(m_p, jnp.exp(s_p - m), 0.0)
                den = jnp.sum(p_c, axis=1, keepdims=True) + jnp.sum(p_p, axis=1, keepdims=True)
                o_ref[cur, hs] = (_dot(p_c, vc) + _dot(p_p, vp)) / den
                l_ref[cur, hs] = jnp.broadcast_to(m + jnp.log(den), (ATT_BLK, HD))
            return carry

        lax.fori_loop(0, nb, blk, 0)

    pv = qkv.reshape(ln, dil * QKV_W)
    out = pl.BlockSpec((ln, DG), lambda z: (0, z))
    o, lse = pl.pallas_call(body, grid=(dil,), in_specs=[_sub_spec(ln, QKV_W, 0), _sub_spec(ln, QKV_W, DG), _sub_spec(ln, QKV_W, 2 * DG)],
                            out_specs=[out, out], out_shape=[_sds((ln, dil * DG))] * 2, name=f"attn_fwd{l}_{bi}",
                            compiler_params=_cp(("parallel",)))(pv, pv, pv)
    return o.reshape(SEQ, DG), lse.reshape(SEQ, DG)


def attn_branch_bwd(l, bi, qkv, dil, dyb, lse_all, delta):
    ln = SEQ // dil
    nb = ln // ATT_BLK
    scale = HD ** -0.5

    def body(q_ref, k_ref, v_ref, do_ref, l_ref, dl_ref, dq_ref, dk_ref, dv_ref):
        dk_ref[...] = jnp.zeros_like(dk_ref)
        dv_ref[...] = jnp.zeros_like(dv_ref)

        def blk(n, carry):
            r0 = pl.multiple_of(n * ATT_BLK, ATT_BLK)
            rp = pl.multiple_of(jnp.maximum(n - 1, 0) * ATT_BLK, ATT_BLK)
            cur, prv = pl.ds(r0, ATT_BLK), pl.ds(rp, ATT_BLK)
            for h in range(NH):
                hs = slice(h * HD, (h + 1) * HD)
                qn, don = q_ref[cur, hs], do_ref[cur, hs]
                lse, dlt = l_ref[cur, h * HD:h * HD + 1], dl_ref[cur, h * HD:h * HD + 1]
                kc, vc, kp, vp = k_ref[cur, hs], v_ref[cur, hs], k_ref[prv, hs], v_ref[prv, hs]
                s_c, s_p, m_c, m_p = _att_scores(qn, kc, kp, h, dil, n > 0)
                p_c = jnp.where(m_c, jnp.exp(s_c - lse), 0.0)
                p_p = jnp.where(m_p, jnp.exp(s_p - lse), 0.0)
                ds_c = p_c * (_dot_nt(don, vc) - dlt)
                ds_p = p_p * (_dot_nt(don, vp) - dlt)
                dq_ref[cur, hs] = (_dot(ds_c, kc) + _dot(ds_p, kp)) * scale
                dv_ref[prv, hs] += _dot_tn(p_p, don)
                dk_ref[prv, hs] += _dot_tn(ds_p, qn) * scale
                dv_ref[cur, hs] += _dot_tn(p_c, don)
                dk_ref[cur, hs] += _dot_tn(ds_c, qn) * scale
            return carry

        lax.fori_loop(0, nb, blk, 0)

    pv = qkv.reshape(ln, dil * QKV_W)
    sub = lambda t: t.reshape(ln, dil * DG)
    row = pl.BlockSpec((ln, DG), lambda z: (0, z))
    outs = pl.pallas_call(body, grid=(dil,),
                          in_specs=[_sub_spec(ln, QKV_W, 0), _sub_spec(ln, QKV_W, DG), _sub_spec(ln, QKV_W, 2 * DG), row, row, row],
                          out_specs=[row] * 3, out_shape=[_sds((ln, dil * DG))] * 3, name=f"attn_bwd{l}_{bi}",
                          compiler_params=_cp(("parallel",)))(pv, pv, pv, sub(dyb), sub(lse_all), sub(delta))
    return [t.reshape(SEQ, DG) for t in outs]


def _attn_merge_fn(o1, o2, o3, l1, l2, l3):
    m = jnp.maximum(jnp.maximum(l1, l2), l3)
    w1, w2, w3 = jnp.exp(l1 - m), jnp.exp(l2 - m), jnp.exp(l3 - m)
    den = w1 + w2 + w3
    return (w1 * o1 + w2 * o2 + w3 * o3) / den, m + jnp.log(den)


def attn_merge(l, os_, ls_):
    ins = list(os_) + list(ls_)
    return _map_fwd(f"attn_merge{l}", _attn_merge_fn, (SEQ // RB,), ins, [_rows(DG)] * 6, [_sds((SEQ, DG))] * 2,
                    [_rows(DG)] * 2)


def attn_delta(l, dyb, yb, seg):
    fn = lambda d, y, s: (_dot(d * y, s),)
    return _map_fwd(f"attn_delta{l}", fn, (SEQ // RB,), [dyb, yb, seg], [_rows(DG), _rows(DG), _full((DG, DG))],
                    [_sds((SEQ, DG))], [_rows(DG)])[0]


def _ln_fn(x, mix, w, b):
    h = ALPHA * x + mix
    mu = jnp.mean(h, axis=-1, keepdims=True)
    d = h - mu
    var = jnp.mean(d * d, axis=-1, keepdims=True)
    return (d * lax.rsqrt(var + LN_EPS) * w + b,)


def ln_fwd(name, x, mix, w, b):
    specs = [_rows(D_MODEL), _rows(D_MODEL), _full((1, D_MODEL)), _full((1, D_MODEL))]
    return _map_fwd(name, _ln_fn, (SEQ // RB,), [x, mix, w, b], specs, [_sds((SEQ, D_MODEL))], [_rows(D_MODEL)])[0]


def ln_bwd(name, x, mix, w, b, dy):
    specs = [_rows(D_MODEL), _rows(D_MODEL), _full((1, D_MODEL)), _full((1, D_MODEL))]
    return _map_bwd(name, _ln_fn, (SEQ // RB,), [x, mix, w, b], specs, [[dy]], [[_rows(D_MODEL)]], want=[1, 2, 3],
                    acc=(2, 3))


def loss_call(y, tgt):
    def fn(yy, tt):
        e = yy - tt
        part = 0.5 * jnp.sum(jnp.sum(e * e, axis=-1, keepdims=True) * (1.0 / D_MODEL), axis=0, keepdims=True)
        return e * (1.0 / D_MODEL), jnp.broadcast_to(part, (8, 128))

    return _map_fwd("loss", fn, (SEQ // RB,), [y, tgt], [_rows(D_MODEL)] * 2,
                    [_sds((SEQ, D_MODEL)), _sds((SEQ // RB * 8, 128))],
                    [_rows(D_MODEL), pl.BlockSpec((8, 128), lambda i: (i, 0))])


LATE_KEYS = ("w_out", "w_up_t", "w_down")


def _full_rows(g):
    return g.reshape(N_DEV * g.shape[1], g.shape[2])


def layer_fwd(l, x, vfirst, wts, p, gather=(), late=False):
    sv = {"x": x}
    proj = _mm(f"mm_in{l}", x, wts["w_in"], "nn", *MM_TILES["k1024"])
    fl = lerp_fwd(l, proj, p["mu"])
    xc = conv_fwd(l, proj, p["conv_w"], p["conv_b"])
    w, k2, v2, c, b, g = rwkv_pre_fwd(l, fl, vfirst, p)
    y_scan, states, sv["gathered"] = rwkv_scan_fwd(l, fl, w, k2, v2, c, b, p, gather)
    if late:
        wts = dict(wts, **dict(zip(LATE_KEYS, [_full_rows(g) for g in sv["gathered"][:3]])))
    sv["wts"] = wts
    ya = rwkv_post_fwd(l, y_scan, fl, k2, v2, g, p)
    qkv = proj[:, C_AQ:C_AQ + 3 * DG]
    outs, lses = [], []
    for bi, (win, dil) in enumerate(DILATED):
        o, lse = attn_branch_fwd(l, bi, qkv, dil)
        outs.append(o)
        lses.append(lse)
    yb, lse_all = attn_merge(l, outs, lses)
    yc, ssd_states = ssd_fwd(l, proj, xc, p)
    yd, hg_states = hgrn_fwd(l, proj, p)
    ycat = jnp.concatenate([ya, yb, yc, yd], axis=1)
    mix = _mm(f"mm_out{l}", ycat, wts["w_out"], "nn", *MM_TILES["k1024"])
    x1 = ln_fwd(f"ln1_fwd{l}", x, mix, p["ln1_w"], p["ln1_b"])
    hh = _mm(f"mm_up{l}", x1, wts["w_up_t"], "nt", *MM_TILES["k1024"], epilogue="relu2")
    m2 = _mm(f"mm_down{l}", hh, wts["w_down"], "nn", *MM_TILES["k4096"])
    x2 = ln_fwd(f"ln2_fwd{l}", x1, m2, p["ln2_w"], p["ln2_b"])
    sv.update(proj=proj, fl=fl, xc=xc, w=w, k2=k2, v2=v2, c=c, b=b, g=g, y_scan=y_scan, states=states,
              yb=yb, lse_all=lse_all, ssd_states=ssd_states, hg_states=hg_states, ycat=ycat, mix=mix, x1=x1, hh=hh, qkv=qkv,
              m2=m2, vfirst=vfirst)
    return x2, sv


def layer_bwd(l, dx2, dvfirst_next, sv, wts, p, exchange=(), early=None):
    gr = {}
    x, x1, proj, fl = sv["x"], sv["x1"], sv["proj"], sv["fl"]
    dres2, gr["ln2_w"], gr["ln2_b"] = ln_bwd(f"ln2_bwd{l}", x1, sv["m2"], p["ln2_w"], p["ln2_b"], dx2)
    du = _mm(f"mm_down_dx{l}", dres2, wts["w_down"], "nt", *MM_TILES["k1024"], add=sv["hh"], epilogue="relu2_bwd")
    gr["w_down"] = _mm(f"mm_down_dw{l}", sv["hh"], dres2, "tn", *MM_TILES["wgrad_tall"])
    dx1 = _mm(f"mm_up_dx{l}", du, wts["w_up_t"], "nn", *MM_TILES["k4096"], add=dres2, add_scale=ALPHA)
    gr["w_up_t"] = _mm(f"mm_up_dw{l}", du, x1, "tn", *MM_TILES["wgrad_tall"])
    dres1, gr["ln1_w"], gr["ln1_b"] = ln_bwd(f"ln1_bwd{l}", x, sv["mix"], p["ln1_w"], p["ln1_b"], dx1)
    dycat = _mm(f"mm_out_dx{l}", dres1, wts["w_out"], "nt", *MM_TILES["k1024"])
    gr["w_out"] = _mm(f"mm_out_dw{l}", sv["ycat"], dres1, "tn", 1024, 1024, 512)
    if early is not None:
        gr["early_own"], early_parts = early({k: gr[k] for k in LATE_KEYS})
        exchange = list(exchange) + list(early_parts)
    dya, dyb, dyc, dyd = (dycat[:, i * DG:(i + 1) * DG] for i in range(4))
    dhg4, gr["lb0"], gr["lb1"], gr["hgrn_norm_w"] = hgrn_bwd(l, proj, sv["hg_states"], dyd, p)
    dz, dxc, ddt, gr["dt_bias"], gr["a_log"], gr["ssd_d"], gr["ssd_norm_w"] = ssd_bwd(l, proj, sv["xc"], sv["ssd_states"], dyc, p)
    dxbc, gr["conv_w"], gr["conv_b"] = conv_bwd(l, proj, p["conv_w"], p["conv_b"], dxc)
    delta = attn_delta(l, dyb, sv["yb"], p["seg64"])
    dqs, dks, dvs = [], [], []
    for bi, (win, dil) in enumerate(DILATED):
        dq, dk, dv = attn_branch_bwd(l, bi, sv["qkv"], dil, dyb, sv["lse_all"], delta)
        dqs.append(dq)
        dks.append(dk)
        dvs.append(dv)
    dq_a, dk_a, dv_a = _addn(f"attn_dq{l}", *dqs), _addn(f"attn_dk{l}", *dks), _addn(f"attn_dv{l}", *dvs)
    pg = rwkv_post_bwd(l, sv["y_scan"], fl, sv["k2"], sv["v2"], sv["g"], p, dya)
    gr["lnx_w"], gr["lnx_b"], gr["r_k"] = pg["lnx_w"], pg["lnx_b"], pg["r_k"]
    (dr, dw, dk, dv, dc, db), gr["exchanged"] = rwkv_scan_bwd(l, fl, sv["w"], sv["k2"], sv["v2"], sv["c"], sv["b"],
                                                              sv["states"], pg["y"], p, exchange)
    v2_cts = [dv, pg["v2"]] + ([dvfirst_next] if dvfirst_next is not None else [])
    qg = rwkv_pre_bwd(l, fl, sv["vfirst"], p, [[dw], [dk, pg["k2"]], v2_cts, [dc], [db], [pg["g"]]])
    for nme in ("w0", "w2p", "a0", "a2p", "g2p", "k_k", "k_a", "v0", "v2p"):
        if nme in qg:
            gr[nme] = qg[nme]
    dfr = _addn(f"rwkv_dr{l}", dr, pg["fr"])
    dvres = qg["fvres"] if l > 0 else jnp.zeros((SEQ, 128), F32)
    dfl_out = jnp.concatenate([dfr, qg["fk"], qg["fv"], qg["flora"], dvres], axis=1)
    dfl_in, gr["mu"] = lerp_bwd(l, proj, p["mu"], dfl_out)
    dproj = jnp.concatenate([dfl_in[:, 0:768], dq_a, dk_a, dv_a, dz, dxbc, dhg4, dfl_in[:, 768:896], ddt,
                             dfl_in[:, 896:1024], jnp.zeros((SEQ, 128), F32)], axis=1)
    dx = _mm(f"mm_in_dx{l}", dproj, wts["w_in"], "nt", *MM_TILES["k4096"], add=dres1, add_scale=ALPHA)
    gr["w_in"] = _mm(f"mm_in_dw{l}", x, dproj, "tn", *MM_TILES["wgrad_wide"])
    return dx, (qg["vfirst"] if l > 0 else None), gr


def _w_in_pad(w_in_l, w_vres):
    rows = w_in_l.shape[0]
    z = lambda n: jnp.zeros((rows, n), w_in_l.dtype)
    vres = z(128) if w_vres is None else jnp.concatenate([w_vres, z(96)], axis=1)
    return jnp.concatenate([w_in_l[:, 0:768], w_in_l[:, 896:1664], w_in_l[:, 1664:1920], w_in_l[:, 1920:2688],
                            w_in_l[:, 2692:3716], w_in_l[:, 768:896], w_in_l[:, 2688:2692], z(124), vres, z(128)], axis=1)


def _w_in_unpad(g):
    g_in = jnp.concatenate([g[:, 0:768], g[:, C_LORA:C_LORA + 128], g[:, 768:1536], g[:, C_Z:C_Z + 256],
                            g[:, C_XBC:C_XBC + 768], g[:, C_DT:C_DT + 4], g[:, C_HQ:C_HQ + 1024]], axis=1)
    return g_in, g[:, C_VRES:C_VRES + 32]


def _consts():
    pair = jnp.arange(HG_NC * HGRN_CHUNK * HGRN_CHUNK)
    i128 = jnp.arange(128)
    same_chunk = (i128[:, None] // HGRN_CHUNK) == (i128[None, :] // HGRN_CHUNK)
    seg64 = _seg_ones(DG, HD)
    tri128 = (i128[:, None] >= i128[None, :]).astype(F32)
    return dict(
        seg64=seg64, seg64x3_bf16=jnp.concatenate([seg64, seg64, seg64], axis=0).astype(BF16),
        dmask=(jnp.arange(HD)[:, None] == (jnp.arange(DG)[None, :] % HD)).astype(F32),
        tri_bd128=(same_chunk & (i128[:, None] >= i128[None, :])).astype(F32), ones_bd128=same_chunk.astype(F32),
        first_row=(i128[None, :] == (jnp.arange(HG_NC) * HGRN_CHUNK)[:, None]).astype(F32),
        causal_blk=jnp.broadcast_to((((pair // HGRN_CHUNK) % HGRN_CHUNK) >= (pair % HGRN_CHUNK)).astype(F32)[:, None],
                                    (HG_NC * HGRN_CHUNK * HGRN_CHUNK, DG)),
        e128=((i128[:, None] == (jnp.arange(DG)[None, :] // HD)) & (i128[:, None] < NH)).astype(F32),
        tri128=tri128, tri128t=tri128.T, seg128=_seg_ones(DG, 128), ones128=jnp.ones((128, 128), F32))


def _pad_lanes(v, n):
    return jnp.concatenate([v, jnp.zeros((n - v.shape[0],), v.dtype)])[None, :]


def _layer_params(l, raw, consts):
    p = dict(consts)
    row = lambda name: raw[name][l][None, :]
    z = lambda r: jnp.zeros((r, DG), F32)
    mu_vres = raw["mu_vres"][l - 1] if l > 0 else jnp.zeros((32,), F32)
    p["mu"] = jnp.concatenate([raw["mu_shift"][l], mu_vres, jnp.zeros((96,), F32)])[None, :]
    p["conv_w"], p["conv_b"] = raw["ssd_conv_w"][l], row("ssd_conv_b")
    p["w0"], p["a0"], p["k_k"], p["k_a"] = row("rwkv_w0"), row("rwkv_a0"), row("rwkv_k_k"), row("rwkv_k_a")
    p["lnx_w"], p["lnx_b"] = row("rwkv_lnx_w"), row("rwkv_lnx_b")
    p["r_k"] = raw["rwkv_r_k"][l].reshape(1, DG)
    p["w2p"] = jnp.concatenate([raw["rwkv_w2"][l], z(96)], axis=0)
    p["a2p"] = jnp.concatenate([z(32), raw["rwkv_a2"][l], z(64)], axis=0)
    p["g2p"] = jnp.concatenate([z(64), raw["rwkv_g2"][l]], axis=0)
    if l > 0:
        p["v0"] = raw["rwkv_v0"][l - 1][None, :]
        p["v2p"] = jnp.concatenate([raw["rwkv_v2"][l - 1], z(96)], axis=0)
    p["lb0"], p["lb1"] = raw["lower_bounds"][0:1], raw["lower_bounds"][1:2]
    p["hgrn_norm_w"], p["ssd_norm_w"] = row("hgrn_norm_w"), row("ssd_norm_w")
    p["dt_bias"], p["a_log"], p["ssd_d"] = (_pad_lanes(raw[n][l], 128) for n in ("ssd_dt_bias", "ssd_A_log", "ssd_D"))
    for n in ("ln1_w", "ln1_b", "ln2_w", "ln2_b"):
        p[n] = row(n)
    return p


def _natural_grads(g0, g1):
    gs = (g0, g1)
    st = lambda key, f=lambda a: a[0]: jnp.stack([f(g[key]) for g in gs])
    out = {}
    out["lower_bounds"] = jnp.concatenate([g0["lb0"] + g1["lb0"], g0["lb1"] + g1["lb1"]], axis=0)
    out["mu_shift"] = st("mu", lambda a: a[0, :896])
    out["mu_vres"] = g1["mu"][:, 896:928]
    out["rwkv_w0"], out["rwkv_a0"], out["rwkv_k_k"], out["rwkv_k_a"] = st("w0"), st("a0"), st("k_k"), st("k_a")
    out["rwkv_w2"] = st("w2p", lambda a: a[0:32])
    out["rwkv_a2"] = st("a2p", lambda a: a[32:64])
    out["rwkv_g2"] = st("g2p", lambda a: a[64:128])
    out["rwkv_r_k"] = st("r_k", lambda a: a.reshape(NH, HD))
    out["rwkv_lnx_w"], out["rwkv_lnx_b"] = st("lnx_w"), st("lnx_b")
    out["rwkv_v0"] = g1["v0"]
    out["rwkv_v2"] = g1["v2p"][None, 0:32]
    out["ssd_conv_w"] = st("conv_w", lambda a: a)
    out["ssd_conv_b"] = st("conv_b")
    out["ssd_dt_bias"], out["ssd_A_log"], out["ssd_D"] = (st(k, lambda a: a[0, :NH]) for k in ("dt_bias", "a_log", "ssd_d"))
    out["ssd_norm_w"], out["hgrn_norm_w"] = st("ssd_norm_w"), st("hgrn_norm_w")
    for n in ("ln1_w", "ln1_b", "ln2_w", "ln2_b"):
        out[n] = st(n)
    return out


MESH_T = pl.DeviceIdType.MESH
ANY = pl.BlockSpec(memory_space=pl.ANY)


def _dev_index(px, py, pc):
    return 4 * px + 2 * py + pc


def all_gather(arrs):
    n = len(arrs)

    def body(*refs):
        begin, end = _gather_steps(refs[:n], refs[n:2 * n], *refs[2 * n:])
        begin()
        end()

    return pl.pallas_call(body, in_specs=[ANY] * n, out_specs=[ANY] * n, out_shape=_gather_shapes(arrs),
                          scratch_shapes=_gather_sems(n), name="all_gather")(*arrs)


def _gather_shapes(arrs):
    return [_sds((N_DEV,) + a.shape, a.dtype) for a in arrs]


def _gather_sems(n):
    return [pltpu.SemaphoreType.DMA((n, 7)), pltpu.SemaphoreType.DMA((n, 7)), pltpu.SemaphoreType.DMA((n,))]


def _gather_steps(ins, outs, send_sems, recv_sems, local_sems):
    n = len(ins)
    x, y, c = lax.axis_index("x"), lax.axis_index("y"), lax.axis_index("c")
    me, sibling = (x, y, c), (x, y, 1 - c)
    chips = [(1 - x, y), (x, 1 - y), (1 - x, 1 - y)]

    def copy(a, k, block, to, src=None):
        slot = outs[a].at[_dev_index(*block)]
        return pltpu.make_async_remote_copy(src_ref=slot if src is None else src, dst_ref=slot,
                                            send_sem=send_sems.at[a, k], recv_sem=recv_sems.at[a, k],
                                            device_id=to, device_id_type=MESH_T)

    def own_copies():
        mine = [pltpu.make_async_copy(ins[a], outs[a].at[_dev_index(*me)], local_sems.at[a]) for a in range(n)]
        first = []
        for a in range(n):
            first.append(copy(a, 0, me, sibling, src=ins[a]))
            first += [copy(a, 1 + j, me, (*chip, c), src=ins[a]) for j, chip in enumerate(chips)]
        return mine, first

    def begin():
        mine, first = own_copies()
        for cp in mine + first:
            cp.start()

    def end():
        mine, first = own_copies()
        passed = []
        for j, chip in enumerate(chips):
            for a in range(n):
                copy(a, 1 + j, (*chip, c), me).wait_recv()
                fwd = copy(a, 4 + j, (*chip, c), sibling)
                fwd.start()
                passed.append(fwd)
        for a in range(n):
            copy(a, 0, sibling, me).wait_recv()
            for j, chip in enumerate(chips):
                copy(a, 4 + j, (*chip, 1 - c), me).wait_recv()
        for cp in first + passed:
            cp.wait_send()
        for cp in mine:
            cp.wait()

    return begin, end


def _chips(x, y):
    return [(x, y), (1 - x, y), (x, 1 - y), (1 - x, 1 - y)]


def exchange_siblings(arrs, name):
    n = len(arrs)

    def body(*refs):
        ins, sib = refs[:n], refs[n:2 * n]
        send_sems, recv_sems = refs[2 * n:]
        x, y, c = lax.axis_index("x"), lax.axis_index("y"), lax.axis_index("c")
        sibling = (x, y, 1 - c)
        sends = []
        for a in range(n):
            for k, (cx, cy) in enumerate(_chips(x, y)):
                sd = pltpu.make_async_remote_copy(src_ref=ins[a].at[_dev_index(cx, cy, 1 - c)], dst_ref=sib[a].at[k],
                                                  send_sem=send_sems.at[a, k], recv_sem=recv_sems.at[a, k],
                                                  device_id=sibling, device_id_type=MESH_T)
                sd.start()
                sends.append(sd)
        for sd in sends:
            sd.wait_recv()
        for sd in sends:
            sd.wait_send()

    sem = pltpu.SemaphoreType.DMA((n, 4))
    return pl.pallas_call(body, in_specs=[ANY] * n, out_specs=[ANY] * n,
                          out_shape=[_sds((4,) + a.shape[1:], a.dtype) for a in arrs],
                          scratch_shapes=[sem, sem], name=name)(*arrs)


def reduce_pair(name, send, slots, sib, wire_dtype):
    _, r, c = send.shape
    rb = min(r, 262144 // c)

    def body(slots_ref, m0, m1, m2, m3, s_ref, own_ref, part_ref):
        own_ref[...] = m0[...] + s_ref[0]
        for k, m_ref in enumerate((m1, m2, m3)):
            part_ref[k] = (m_ref[...] + s_ref[k + 1]).astype(wire_dtype)

    mine = [pl.BlockSpec((None, rb, c), lambda i, s, k=k: (s[k], i, 0)) for k in range(4)]
    grid_spec = pltpu.PrefetchScalarGridSpec(
        num_scalar_prefetch=1, grid=(r // rb,),
        in_specs=mine + [pl.BlockSpec((4, rb, c), lambda i, s: (0, i, 0))],
        out_specs=[pl.BlockSpec((rb, c), lambda i, s: (i, 0)), pl.BlockSpec((3, rb, c), lambda i, s: (0, i, 0))])
    return pl.pallas_call(body, grid_spec=grid_spec, out_shape=[_sds((r, c)), _sds((3, r, c), wire_dtype)], name=name,
                          compiler_params=_cp(("parallel",)))(slots, send, send, send, send, sib)


def _chip_exchange_steps(ins, recv, send_sems, recv_sems):
    x, y, c = lax.axis_index("x"), lax.axis_index("y"), lax.axis_index("c")

    def copies():
        return [pltpu.make_async_remote_copy(src_ref=ins[a].at[k], dst_ref=recv[a].at[k], send_sem=send_sems.at[a, k],
                                             recv_sem=recv_sems.at[a, k], device_id=(cx, cy, c), device_id_type=MESH_T)
                for a in range(len(ins)) for k, (cx, cy) in enumerate(_chips(x, y)[1:])]

    def begin():
        for cp in copies():
            cp.start()

    def end():
        cps = copies()
        for cp in cps:
            cp.wait_recv()
        for cp in cps:
            cp.wait_send()

    return begin, end


def exchange_chips(parts, rep):
    n = len(parts)

    def body(*refs):
        ins, rep_ref = refs[:n], refs[n]
        recv, rep_all = refs[n + 1:2 * n + 1], refs[2 * n + 1]
        send_sems, recv_sems, rsend_sems, rrecv_sems, local_sem = refs[2 * n + 2:]
        x, y, c = lax.axis_index("x"), lax.axis_index("y"), lax.axis_index("c")
        me = _dev_index(x, y, c)
        mine = pltpu.make_async_copy(rep_ref, rep_all.at[me], local_sem)
        mine.start()
        begin, end = _chip_exchange_steps(ins, recv, send_sems, recv_sems)
        begin()
        rels = [(rx, ry, rc) for rx in (0, 1) for ry in (0, 1) for rc in (0, 1)][1:]
        peers = [(jnp.where(rx, 1 - x, x), jnp.where(ry, 1 - y, y), jnp.where(rc, 1 - c, c)) for rx, ry, rc in rels]
        rcps = []
        for k, peer in enumerate(peers):
            cp = pltpu.make_async_remote_copy(src_ref=rep_ref, dst_ref=rep_all.at[me], send_sem=rsend_sems.at[k],
                                              recv_sem=rrecv_sems.at[k], device_id=peer, device_id_type=MESH_T)
            cp.start()
            rcps.append(cp)
        for k, peer in enumerate(peers):
            pltpu.make_async_remote_copy(src_ref=rep_ref, dst_ref=rep_all.at[_dev_index(*peer)], send_sem=rsend_sems.at[k],
                                         recv_sem=rrecv_sems.at[k], device_id=peer, device_id_type=MESH_T).wait_recv()
        end()
        for cp in rcps:
            cp.wait_send()
        mine.wait()

    outs = pl.pallas_call(
        body, in_specs=[ANY] * (n + 1), out_specs=[ANY] * (n + 1),
        out_shape=[_sds(a.shape, a.dtype) for a in parts] + [_sds((N_DEV,) + rep.shape, rep.dtype)],
        scratch_shapes=[pltpu.SemaphoreType.DMA((n, 3)), pltpu.SemaphoreType.DMA((n, 3)), pltpu.SemaphoreType.DMA((7,)),
                        pltpu.SemaphoreType.DMA((7,)), pltpu.SemaphoreType.DMA],
        name="exchange_chips")(*parts, rep)
    return outs[:n], outs[n]


def adamw(name, terms, w, m, v):
    r, c = w.shape
    rb = min(r, 262144 // c)
    c1 = 1.0 - ADAM_B1 ** ADAM_STEP
    c2 = 1.0 - ADAM_B2 ** ADAM_STEP
    nt = len(terms)

    def body(*refs):
        w_ref, m_ref, v_ref = refs[nt:nt + 3]
        g_ref, d_ref, nm_ref, nv_ref = refs[nt + 3:]
        g = refs[0][...].astype(F32)
        for t_ref in refs[1:nt]:
            g = g + t_ref[...].astype(F32)
        nm = ADAM_B1 * m_ref[...] + (1.0 - ADAM_B1) * g
        nv = ADAM_B2 * v_ref[...] + (1.0 - ADAM_B2) * (g * g)
        g_ref[...] = g
        nm_ref[...] = nm
        nv_ref[...] = nv
        d_ref[...] = -ADAM_LR * ((nm / c1) / (jnp.sqrt(nv / c2) + ADAM_EPS) + ADAM_WD * w_ref[...])

    blk = pl.BlockSpec((rb, c), lambda i: (i, 0))
    tspecs = [blk if k is None else pl.BlockSpec((None, rb, c), lambda i, k=k: (k, i, 0)) for _, k in terms]
    return pl.pallas_call(body, grid=(r // rb,), in_specs=tspecs + [blk] * 3, out_specs=[blk] * 4,
                          out_shape=[_sds((r, c))] * 4, name=name,
                          compiler_params=_cp(("parallel",)))(*[t for t, _ in terms], w, m, v)


SMS_ROWS = 16
REP_ROWS = 24
N_BIG = 8
SMALL_SHARDED = (("rwkv_w2", (2, 32, 32)), ("rwkv_a2", (2, 32, 32)), ("rwkv_g2", (2, 64, 32)), ("rwkv_v2", (1, 32, 32)),
                 ("ssd_conv_w", (2, 4, 96)))
REPLICATED = (("lower_bounds", (2, 256)), ("mu_shift", (2, 896)), ("mu_vres", (1, 32)), ("rwkv_w0", (2, 256)),
              ("rwkv_a0", (2, 256)), ("rwkv_k_k", (2, 256)), ("rwkv_k_a", (2, 256)), ("rwkv_r_k", (2, 4, 64)),
              ("rwkv_lnx_w", (2, 256)), ("rwkv_lnx_b", (2, 256)), ("rwkv_v0", (1, 256)), ("ssd_conv_b", (2, 768)),
              ("ssd_dt_bias", (2, 4)), ("ssd_A_log", (2, 4)), ("ssd_D", (2, 4)), ("ssd_norm_w", (2, 256)),
              ("hgrn_norm_w", (2, 256)), ("ln1_w", (2, 1024)), ("ln1_b", (2, 1024)), ("ln2_w", (2, 1024)),
              ("ln2_b", (2, 1024)))


def _flat_rows(parts, rows):
    flat = jnp.concatenate([a.reshape(-1) for a in parts])
    return jnp.concatenate([flat, jnp.zeros((rows * PACK_W - flat.shape[0],), flat.dtype)]).reshape(rows, PACK_W)


def _local_arrays(d):
    arrs = [_w_in_pad(d["w_in"][0], None), _w_in_pad(d["w_in"][1], d["w_in_vres"][0]), d["w_out"][0], d["w_out"][1],
            d["w_up"][0].T, d["w_up"][1].T, d["w_down"][0], d["w_down"][1],
            _flat_rows([d[n] for n, _ in SMALL_SHARDED], SMS_ROWS)]
    return arrs, _flat_rows([d[n] for n, _ in REPLICATED], REP_ROWS)


def _unflat(rows2d, table):
    flat, out, o = rows2d.reshape(-1), {}, 0
    for name, shape in table:
        n = 1
        for s in shape:
            n *= s
        out[name] = flat[o:o + n].reshape(shape)
        o += n
    return out


def _from_local_arrays(arrs, rep):
    d = {}
    g0, _ = _w_in_unpad(arrs[0])
    g1, gv = _w_in_unpad(arrs[1])
    d["w_in"], d["w_in_vres"] = jnp.stack([g0, g1]), gv[None]
    d["w_out"] = jnp.stack([arrs[2], arrs[3]])
    d["w_up"] = jnp.stack([arrs[4].T, arrs[5].T])
    d["w_down"] = jnp.stack([arrs[6], arrs[7]])
    d.update(_unflat(arrs[8], SMALL_SHARDED))
    d.update(_unflat(rep, REPLICATED))
    return d


def _small_sharded_full(gs):
    small, flat, o = {}, gs.reshape(N_DEV, -1), 0
    for name, shape in SMALL_SHARDED:
        n = shape[0] * shape[1] * shape[2]
        blk = flat[:, o:o + n].reshape((N_DEV,) + shape)
        small[name] = blk.transpose(1, 2, 0, 3).reshape(shape[0], shape[1], N_DEV * shape[2])
        o += n
    return small


def _owner_blocks(g):
    return g.reshape(N_DEV, g.shape[0] // N_DEV, g.shape[1])


def _small_send_arrays(small_grads):
    sms = []
    for name, shape in SMALL_SHARDED:
        g = small_grads[name].reshape(shape[0], shape[1], N_DEV, shape[2]).transpose(2, 0, 1, 3)
        sms.append(g.reshape(N_DEV, -1))
    sms = jnp.concatenate(sms, axis=1)
    sms = jnp.concatenate([sms, jnp.zeros((N_DEV, SMS_ROWS * PACK_W - sms.shape[1]), F32)], axis=1)
    return sms.reshape(N_DEV, SMS_ROWS, PACK_W), _flat_rows([small_grads[n] for n, _ in REPLICATED], REP_ROWS)


BIG_KEYS = ("w_in", "w_out", "w_up_t", "w_down")


def _weights_of(full):
    return dict(zip(BIG_KEYS, full))


def _local_step(x, tgt, wts, raw, gather=(), pair_sums=None):
    consts = _consts()
    ps = [_layer_params(l, raw, consts) for l in range(DEPTH)]
    x1, sv0 = layer_fwd(0, x, None, wts[0], ps[0], gather, late=bool(gather))
    wts1 = _weights_of([_full_rows(g) for g in sv0["gathered"][3:]]) if gather else wts[1]
    x2, sv1 = layer_fwd(1, x1, sv0["fl"], wts1, ps[1])
    dy, lparts = loss_call(x2, tgt)
    loss = jnp.sum(lparts[::8, 0])
    dx1, dvfirst, g1 = layer_bwd(1, dy, None, sv1, wts1, ps[1])
    big1 = {k: g1[k] for k in BIG_KEYS}
    if pair_sums is None:
        dx0, _, g0 = layer_bwd(0, dx1, dvfirst, sv0, sv0["wts"], ps[0])
        early = None
    else:
        own1, parts1 = pair_sums("1", big1)
        dx0, _, g0 = layer_bwd(0, dx1, dvfirst, sv0, sv0["wts"], ps[0], parts1, lambda gs: pair_sums("0a", gs))
        early = (own1, g0["exchanged"][:4], g0["early_own"], g0["exchanged"][4:])
    big = [{k: g0[k] for k in BIG_KEYS}, big1]
    return loss, dx0, big, _natural_grads(g0, g1), early


WEIGHT_NAMES = ("lower_bounds", "w_in", "w_in_vres", "mu_shift", "mu_vres", "rwkv_w0", "rwkv_w2", "rwkv_a0", "rwkv_a2",
                "rwkv_g2", "rwkv_k_k", "rwkv_k_a", "rwkv_r_k", "rwkv_lnx_w", "rwkv_lnx_b", "rwkv_v0", "rwkv_v2",
                "ssd_conv_w", "ssd_conv_b", "ssd_dt_bias", "ssd_A_log", "ssd_D", "ssd_norm_w", "hgrn_norm_w", "w_out",
                "ln1_w", "ln1_b", "w_up", "w_down", "ln2_w", "ln2_b")


def kernel(x, lower_bounds, w_in, w_in_vres, mu_shift, mu_vres, rwkv_w0, rwkv_w2, rwkv_a0, rwkv_a2, rwkv_g2, rwkv_k_k, rwkv_k_a, rwkv_r_k, rwkv_lnx_w, rwkv_lnx_b, rwkv_v0, rwkv_v2, ssd_conv_w, ssd_conv_b, ssd_dt_bias, ssd_A_log, ssd_D, ssd_norm_w, hgrn_norm_w, w_out, ln1_w, ln1_b, w_up, w_down, ln2_w, ln2_b, loss_target, m_lower_bounds, m_w_in, m_w_in_vres, m_mu_shift, m_mu_vres, m_rwkv_w0, m_rwkv_w2, m_rwkv_a0, m_rwkv_a2, m_rwkv_g2, m_rwkv_k_k, m_rwkv_k_a, m_rwkv_r_k, m_rwkv_lnx_w, m_rwkv_lnx_b, m_rwkv_v0, m_rwkv_v2, m_ssd_conv_w, m_ssd_conv_b, m_ssd_dt_bias, m_ssd_A_log, m_ssd_D, m_ssd_norm_w, m_hgrn_norm_w, m_w_out, m_ln1_w, m_ln1_b, m_w_up, m_w_down, m_ln2_w, m_ln2_b, v_lower_bounds, v_w_in, v_w_in_vres, v_mu_shift, v_mu_vres, v_rwkv_w0, v_rwkv_w2, v_rwkv_a0, v_rwkv_a2, v_rwkv_g2, v_rwkv_k_k, v_rwkv_k_a, v_rwkv_r_k, v_rwkv_lnx_w, v_rwkv_lnx_b, v_rwkv_v0, v_rwkv_v2, v_ssd_conv_w, v_ssd_conv_b, v_ssd_dt_bias, v_ssd_A_log, v_ssd_D, v_ssd_norm_w, v_hgrn_norm_w, v_w_out, v_ln1_w, v_ln1_b, v_w_up, v_w_down, v_ln2_w, v_ln2_b):
    given = dict(locals())
    w = {n: given[n] for n in WEIGHT_NAMES}
    w_arrs, w_rep = _local_arrays(w)
    m_arrs, m_rep = _local_arrays({n: given["m_" + n] for n in WEIGHT_NAMES})
    v_arrs, v_rep = _local_arrays({n: given["v_" + n] for n in WEIGHT_NAMES})
    gathered0 = all_gather([w_arrs[0].astype(BF16), w_arrs[N_BIG]])
    raw = {n: w[n] for n, _ in REPLICATED}
    raw.update(_small_sharded_full(gathered0[1]))
    mx, my, mc = lax.axis_index("x"), lax.axis_index("y"), lax.axis_index("c")
    slots = jnp.stack([_dev_index(cx, cy, mc) for cx, cy in _chips(mx, my)]).astype(jnp.int32)

    def pair_sums(tag, grads, extra=()):
        send = [_owner_blocks(g) for g in grads.values()] + list(extra)
        wire = [BF16] * len(grads) + [F32] * len(extra)
        sib = exchange_siblings(send, f"exchange_siblings{tag}")
        res = [reduce_pair(f"reduce_pair{tag}_{i}", s, slots, sb, dt) for i, (s, sb, dt) in enumerate(zip(send, sib, wire))]
        return [o for o, _ in res], [pt for _, pt in res]

    behind_scan = [w_arrs[a].astype(BF16) for a in (2, 4, 6, 1, 3, 5, 7)]
    loss, dx, big, small_grads, (own1, recv1, own0a, recv0a) = _local_step(
        x[0], loss_target[0], [{"w_in": _full_rows(gathered0[0])}, None], raw, behind_scan, pair_sums)
    sms_send, rep = _small_send_arrays(small_grads)
    own0b, parts0b = pair_sums("0b", {"w_in": big[0]["w_in"]}, [sms_send])
    recv0b, rep_all = exchange_chips(parts0b, rep)
    own, recv = [None] * (N_BIG + 1), [None] * (N_BIG + 1)
    for a, o, r in zip((1, 3, 5, 7), own1, recv1):
        own[a], recv[a] = o, r
    for a, o, r in zip((2, 4, 6), own0a, recv0a):
        own[a], recv[a] = o, r
    for a, o, r in zip((0, N_BIG), own0b, recv0b):
        own[a], recv[a] = o, r
    results = [adamw(f"adamw{a}", [(own[a], None), (recv[a], 0), (recv[a], 1), (recv[a], 2)], w_arrs[a], m_arrs[a], v_arrs[a])
               for a in range(N_BIG + 1)]
    rep_res = adamw("adamw_rep", [(rep_all, q) for q in range(N_DEV)], w_rep, m_rep, v_rep)
    loss = lax.psum(loss, ("x", "y", "c"))
    outs = [loss, dx[None]]
    for q in range(4):
        d = _from_local_arrays([res[q] for res in results], rep_res[q])
        outs += [d[n] for n in WEIGHT_NAMES]
    return tuple(outs)
```

```python
import functools

import jax
import jax.numpy as jnp
from jax import lax
from jax.experimental import pallas as pl
from jax.experimental.pallas import tpu as pltpu

F32 = jnp.float32
BF16 = jnp.bfloat16
HI = lax.Precision.HIGHEST

N_DEV = 8
SEQ = 2048
D_MODEL = 1024
D_FF = 4096
DG = 256
NH = 4
HD = 64
DEPTH = 2
ALPHA = (2.0 * DEPTH) ** 0.25
LN_EPS = 1e-5
RMS_EPS = 1e-5
GN_EPS = HD * 1e-5
IN_COLS = 3716
SSD_N = 128
SSD_CHUNK = 128
HGRN_CHUNK = 16
DILATED = ((128, 1), (512, 4), (2048, 16))

ADAM_LR, ADAM_B1, ADAM_B2, ADAM_EPS, ADAM_WD, ADAM_STEP = 0.001, 0.9, 0.999, 1e-08, 0.01, 10

PW = 4096
C_R, C_K, C_V = 0, 256, 512
C_AQ, C_AK, C_AV = 768, 1024, 1280
C_Z, C_XBC = 1536, 1792
C_HQ, C_HF, C_HI, C_HG = 2560, 2816, 3072, 3328
C_LORA, C_DT, C_VRES = 3584, 3712, 3840

RB = 256
VMEM_LIMIT = 56 * 1024 * 1024
PACK_W = 1024


def _cp(sem=None):
    return pltpu.CompilerParams(dimension_semantics=sem, vmem_limit_bytes=VMEM_LIMIT)


def _sds(shape, dt=F32):
    return jax.ShapeDtypeStruct(tuple(shape), dt)


def _rows(w, cb=0, rb=RB):
    return pl.BlockSpec((rb, w), lambda i: (i, cb))


def _full(shape):
    n = len(shape)
    return pl.BlockSpec(tuple(shape), lambda *_: (0,) * n)


def _sigmoid(x):
    return 1.0 / (1.0 + jnp.exp(-x))


def _silu(x):
    return x * _sigmoid(x)


def _softplus(x):
    return jnp.maximum(x, 0.0) + jnp.log(1.0 + jnp.exp(jnp.where(x > 0, -x, x)))


MID = lax.Precision.HIGH
NN, TN, NT = (((1,), (0,)), ((), ())), (((0,), (0,)), ((), ())), (((1,), (1,)), ((), ()))


def _dot(a, b):
    return lax.dot_general(a, b, NN, precision=MID, preferred_element_type=F32)


def _dot_tn(a, b):
    return lax.dot_general(a, b, TN, precision=MID, preferred_element_type=F32)


def _dot_nt(a, b):
    return lax.dot_general(a, b, NT, precision=MID, preferred_element_type=F32)


def _dotx(a, b):
    return lax.dot_general(a, b, NN, precision=HI, preferred_element_type=F32)


def _dotx_tn(a, b):
    return lax.dot_general(a, b, TN, precision=HI, preferred_element_type=F32)


def _seg_ones(n, seg):
    i = jnp.arange(n)
    return (i[:, None] // seg == i[None, :] // seg).astype(F32)


def _shift_down(x, s):
    row = lax.broadcasted_iota(jnp.int32, x.shape, 0)
    return jnp.where(row < s, 0.0, pltpu.roll(x, s, 0))


def _shift_up(x, s):
    n = x.shape[0]
    row = lax.broadcasted_iota(jnp.int32, x.shape, 0)
    return jnp.where(row >= n - s, 0.0, pltpu.roll(x, n - s, 0))


@functools.partial(jax.custom_vjp, nondiff_argnums=(1,))
def _tshift(x, s):
    return _shift_down(x, s)


def _tshift_fwd(x, s):
    return _shift_down(x, s), None


def _tshift_bwd(s, _, g):
    return (_shift_up(g, s),)


_tshift.defvjp(_tshift_fwd, _tshift_bwd)


def _map_fwd(name, fn, grid, ins, in_specs, out_shapes, out_specs):
    n_in = len(ins)

    def body(*refs):
        ys = fn(*[r[...] for r in refs[:n_in]])
        for r, y in zip(refs[n_in:], ys):
            r[...] = y

    return pl.pallas_call(body, grid=grid, in_specs=in_specs, out_specs=out_specs, out_shape=out_shapes,
                          name=name, compiler_params=_cp(("parallel",)))(*ins)


def _map_bwd(name, fn, grid, ins, in_specs, cts, ct_specs, want, acc=(), gout=None):
    n_in = len(ins)
    flat_cts = [c for group in cts for c in group]
    flat_specs = [s for group in ct_specs for s in group]
    n_ct = len(flat_cts)
    gout = gout or {}
    out_shapes = [gout[i][0] if i in gout else _sds(ins[i].shape) for i in want]
    out_specs = [gout[i][1] if i in gout else in_specs[i] for i in want]

    def body(*refs):
        xs = [r[...] for r in refs[:n_in]]
        cvals = [r[...] for r in refs[n_in:n_in + n_ct]]
        gouts = refs[n_in + n_ct:]
        cs, p = [], 0
        for group in cts:
            v = cvals[p]
            for q in range(1, len(group)):
                v = v + cvals[p + q]
            cs.append(v)
            p += len(group)

        def f(*wanted):
            full = list(xs)
            for i, w in zip(want, wanted):
                full[i] = w
            return tuple(fn(*full))

        _, vjp = jax.vjp(f, *[xs[i] for i in want])
        gs = vjp(tuple(cs))
        for o, i, g in zip(gouts, want, gs):
            if i in acc:
                @pl.when(pl.program_id(0) == 0)
                def _():
                    o[...] = jnp.zeros_like(o)

                o[...] += g
            else:
                o[...] = g

    sem = ("arbitrary",) if acc else ("parallel",)
    return pl.pallas_call(body, grid=grid, in_specs=list(in_specs) + flat_specs, out_specs=out_specs,
                          out_shape=out_shapes, name=name, compiler_params=_cp(sem))(*ins, *flat_cts)


def _addn(name, *arrs):
    n, c = arrs[0].shape

    def fn(*xs):
        r = xs[0]
        for x in xs[1:]:
            r = r + x
        return (r,)

    return _map_fwd(name, fn, (n // RB,), list(arrs), [_rows(c)] * len(arrs), [_sds((n, c))], [_rows(c)])[0]


MM_TILES = {"k1024": (2048, 512, 1024), "k4096": (1024, 1024, 1024), "wgrad_tall": (2048, 1024, 512),
            "wgrad_wide": (1024, 2048, 512)}


def _mm(name, a, b, mode, tm, tn, tk, add=None, add_scale=1.0, epilogue=None):
    if mode == "nn":
        (m, k), n = a.shape, b.shape[1]
    elif mode == "nt":
        (m, k), n = a.shape, b.shape[0]
    else:
        (k, m), n = a.shape, b.shape[1]
    nk = k // tk
    dn = {"nn": (((1,), (0,)), ((), ())), "nt": (((1,), (1,)), ((), ())), "tn": (((0,), (0,)), ((), ()))}[mode]

    def body(*refs):
        a_ref, b_ref = refs[:2]
        add_ref = refs[2] if add is not None else None
        o_ref = refs[3] if add is not None else refs[2]
        prod = lax.dot_general(a_ref[...].astype(BF16), b_ref[...].astype(BF16), dn, preferred_element_type=F32)

        def finish(r):
            if epilogue == "relu2":
                r = jnp.maximum(r, 0.0)
                r = r * r
            elif epilogue == "relu2_bwd":
                r = r * (2.0 * jnp.sqrt(add_ref[...]))
            elif add is not None:
                r = r + add_scale * add_ref[...]
            o_ref[...] = r

        if nk == 1:
            finish(prod)
        else:
            acc = refs[-1]
            kk = pl.program_id(2)

            @pl.when(kk == 0)
            def _():
                acc[...] = prod

            @pl.when(kk > 0)
            def _():
                acc[...] += prod

            @pl.when(kk == nk - 1)
            def _():
                finish(acc[...])

    a_spec = pl.BlockSpec((tk, tm), lambda i, j, q: (q, i)) if mode == "tn" else pl.BlockSpec((tm, tk), lambda i, j, q: (i, q))
    b_spec = pl.BlockSpec((tn, tk), lambda i, j, q: (j, q)) if mode == "nt" else pl.BlockSpec((tk, tn), lambda i, j, q: (q, j))
    o_spec = pl.BlockSpec((tm, tn), lambda i, j, q: (i, j))
    ins, specs = [a, b], [a_spec, b_spec]
    if add is not None:
        ins.append(add)
        specs.append(o_spec)
    return pl.pallas_call(body, grid=(m // tm, n // tn, nk), in_specs=specs, out_specs=o_spec, out_shape=_sds((m, n)),
                          scratch_shapes=[pltpu.VMEM((tm, tn), F32)] if nk > 1 else [], name=name,
                          compiler_params=_cp(("parallel", "parallel", "arbitrary")))(*ins)


LERP_BLOCKS = (0, 1, 2, 3, 4, 5, C_LORA // 128, C_VRES // 128)


def _lerp_colmap(j):
    r = jnp.where(j < 6, j, jnp.where(j == 6, C_LORA // 128, C_VRES // 128))
    return (0, r)


def _lerp_fn(f, mu):
    return (f + (_tshift(f, 1) - f) * mu,)


def _lerp_specs():
    return [pl.BlockSpec((SEQ, 128), _lerp_colmap), pl.BlockSpec((1, 128), lambda j: (0, j))]


def lerp_fwd(l, proj, mu):
    return _map_fwd(f"lerp_fwd{l}", _lerp_fn, (8,), [proj, mu], _lerp_specs(), [_sds((SEQ, 1024))],
                    [pl.BlockSpec((SEQ, 128), lambda j: (0, j))])[0]


def lerp_bwd(l, proj, mu, dfl):
    n_in = 2

    def body(f_ref, mu_ref, g_ref, df_ref, dmu_ref):
        _, vjp = jax.vjp(_lerp_fn, f_ref[...], mu_ref[...])
        df, dmu = vjp((g_ref[...],))
        df_ref[...] = df
        dmu_ref[...] = dmu

    cspec = pl.BlockSpec((SEQ, 128), lambda j: (0, j))
    return pl.pallas_call(body, grid=(8,), in_specs=_lerp_specs() + [cspec],
                          out_specs=[cspec, pl.BlockSpec((1, 128), lambda j: (0, j))],
                          out_shape=[_sds((SEQ, 1024)), _sds((1, 1024))], name=f"lerp_bwd{l}",
                          compiler_params=_cp(("parallel",)))(proj, mu, dfl)


def _conv_fn(x, w, b):
    y = x * w[3:4, :] + _tshift(x, 1) * w[2:3, :] + _tshift(x, 2) * w[1:2, :] + _tshift(x, 3) * w[0:1, :] + b
    return (_silu(y),)


def _conv_specs():
    return [pl.BlockSpec((SEQ, 128), lambda j: (0, C_XBC // 128 + j)), pl.BlockSpec((4, 128), lambda j: (0, j)),
            pl.BlockSpec((1, 128), lambda j: (0, j))]


def conv_fwd(l, proj, w, b):
    return _map_fwd(f"conv_fwd{l}", _conv_fn, (6,), [proj, w, b], _conv_specs(), [_sds((SEQ, 768))],
                    [pl.BlockSpec((SEQ, 128), lambda j: (0, j))])[0]


def conv_bwd(l, proj, w, b, dxc):
    def body(x_ref, w_ref, b_ref, g_ref, dx_ref, dw_ref, db_ref):
        _, vjp = jax.vjp(_conv_fn, x_ref[...], w_ref[...], b_ref[...])
        dx, dw, db = vjp((g_ref[...],))
        dx_ref[...] = dx
        dw_ref[...] = dw
        db_ref[...] = db

    cspec = pl.BlockSpec((SEQ, 128), lambda j: (0, j))
    return pl.pallas_call(body, grid=(6,), in_specs=_conv_specs() + [cspec],
                          out_specs=[cspec, pl.BlockSpec((4, 128), lambda j: (0, j)), pl.BlockSpec((1, 128), lambda j: (0, j))],
                          out_shape=[_sds((SEQ, 768)), _sds((4, 768)), _sds((1, 768))], name=f"conv_bwd{l}",
                          compiler_params=_cp(("parallel",)))(proj, w, b, dxc)


def _rwkv_pre_fn(has_vres):
    def fn(fk, fv, flora, *rest):
        if has_vres:
            fvres, vfirst, w0, w2p, a0, a2p, g2p, k_k, k_a, v0, v2p, seg = rest
        else:
            w0, w2p, a0, a2p, g2p, k_k, k_a, seg = rest
        w_log = -_softplus(-(w0 + _dot(jnp.tanh(flora), w2p))) - 0.5
        w = jnp.exp(-jnp.exp(w_log))
        a = _sigmoid(a0 + _dot(flora, a2p))
        g = _dot(_sigmoid(flora), g2p)
        if has_vres:
            v2 = fv + (vfirst - fv) * _sigmoid(v0 + _dot(fvres, v2p))
        else:
            v2 = fv * 1.0
        kk = fk * k_k
        kk = kk / jnp.maximum(jnp.sqrt(_dot(kk * kk, seg)), 1e-12)
        k2 = fk * (1.0 + (a - 1.0) * k_a)
        return w, k2, v2, -kk, kk * a, g

    return fn


def _rwkv_pre_args(fl, vfirst, p, has_vres):
    ins = [fl, fl, fl]
    specs = [_rows(256, 1), _rows(256, 2), _rows(128, 6)]
    if has_vres:
        ins += [fl, vfirst]
        specs += [_rows(128, 7), _rows(256, 2)]
    names = ["w0", "w2p", "a0", "a2p", "g2p", "k_k", "k_a"] + (["v0", "v2p"] if has_vres else []) + ["seg64"]
    for nme in names:
        ins.append(p[nme])
        specs.append(_full(p[nme].shape))
    return ins, specs, names


def rwkv_pre_fwd(l, fl, vfirst, p):
    has_vres = l > 0
    ins, specs, _ = _rwkv_pre_args(fl, vfirst, p, has_vres)
    return _map_fwd(f"rwkv_pre_fwd{l}", _rwkv_pre_fn(has_vres), (SEQ // RB,), ins, specs,
                    [_sds((SEQ, DG))] * 6, [_rows(DG)] * 6)


def rwkv_pre_bwd(l, fl, vfirst, p, cts):
    has_vres = l > 0
    ins, specs, names = _rwkv_pre_args(fl, vfirst, p, has_vres)
    n_row = 5 if has_vres else 3
    want = list(range(n_row)) + [n_row + i for i, nme in enumerate(names) if nme != "seg64"]
    acc = tuple(w for w in want if w >= n_row)
    ct_specs = [[_rows(DG)] * len(g) for g in cts]
    gout = {0: (_sds((SEQ, DG)), _rows(DG)), 1: (_sds((SEQ, DG)), _rows(DG)), 2: (_sds((SEQ, 128)), _rows(128))}
    if has_vres:
        gout[3] = (_sds((SEQ, 128)), _rows(128))
        gout[4] = (_sds((SEQ, DG)), _rows(DG))
    gs = _map_bwd(f"rwkv_pre_bwd{l}", _rwkv_pre_fn(has_vres), (SEQ // RB,), ins, specs, cts, ct_specs, want, acc, gout)
    keys = ["fk", "fv", "flora"] + (["fvres", "vfirst"] if has_vres else []) + [nme for nme in names if nme != "seg64"]
    return dict(zip(keys, gs))


def _rwkv_post_fn(y, fr, k2, v2, g, lnx_w, lnx_b, r_k, seg):
    mu = _dot(y, seg) * (1.0 / HD)
    d = y - mu
    var = _dot(d * d, seg) * (1.0 / HD)
    yn = d * lax.rsqrt(var + GN_EPS) * lnx_w + lnx_b
    bonus = _dot(fr * k2 * r_k, seg) * v2
    return ((yn + bonus) * g,)


def _rwkv_post_args(y, fl, k2, v2, g, p):
    ins = [y, fl, k2, v2, g, p["lnx_w"], p["lnx_b"], p["r_k"], p["seg64"]]
    specs = [_rows(DG), _rows(DG, 0), _rows(DG), _rows(DG), _rows(DG)] + [_full(x.shape) for x in ins[5:]]
    return ins, specs


def rwkv_post_fwd(l, y, fl, k2, v2, g, p):
    ins, specs = _rwkv_post_args(y, fl, k2, v2, g, p)
    return _map_fwd(f"rwkv_post_fwd{l}", _rwkv_post_fn, (SEQ // RB,), ins, specs, [_sds((SEQ, DG))], [_rows(DG)])[0]


def rwkv_post_bwd(l, y, fl, k2, v2, g, p, dya):
    ins, specs = _rwkv_post_args(y, fl, k2, v2, g, p)
    gs = _map_bwd(f"rwkv_post_bwd{l}", _rwkv_post_fn, (SEQ // RB,), ins, specs, [[dya]], [[_rows(DG)]],
                  want=[0, 1, 2, 3, 4, 5, 6, 7], acc=(5, 6, 7), gout={1: (_sds((SEQ, DG)), _rows(DG))})
    return dict(zip(["y", "fr", "k2", "v2", "g", "lnx_w", "lnx_b", "r_k"], gs))


SCAN_TB = 64


def _coltile8(rows8, dmask, ones_stack, parts):
    pieces, rest = [], rows8
    for q in range(parts):
        piece = rest.astype(BF16).astype(F32)
        if q < parts - 1:
            rest = rest - piece
        pieces.append((piece[:, None, :] * dmask[None]).reshape(8 * HD, DG).astype(BF16))
    x = pieces[0] if parts == 1 else jnp.concatenate(pieces, axis=1)
    return jnp.dot(x, ones_stack, preferred_element_type=F32).reshape(8, HD, DG)


def _coltiles_bf16(rows_list, dmask, ones_bf16):
    x = jnp.concatenate([(r8[:, None, :] * dmask[None]).reshape(8 * HD, DG).astype(BF16) for r8 in rows_list], axis=0)
    t = jnp.dot(x, ones_bf16, preferred_element_type=F32)
    return [t[q * 8 * HD:(q + 1) * 8 * HD].reshape(8, HD, DG) for q in range(len(rows_list))]


def _segrows8(x8, dmask, ones_bf16):
    t = jnp.dot(x8.reshape(8 * HD, DG).astype(BF16), ones_bf16, preferred_element_type=F32).reshape(8, HD, DG)
    return jnp.sum(t * dmask[None], axis=1)


def rwkv_scan_fwd(l, fl, w, k2, v2, c, b, p, gather=()):
    nblk = SEQ // SCAN_TB
    ng = len(gather)

    def body(*refs):
        r_ref, w_ref, k_ref, v_ref, c_ref, b_ref, ones_ref, dm_ref = refs[:8]
        y_ref, st_ref = refs[8 + ng:10 + ng]
        s_sc = refs[10 + 2 * ng]
        if ng:
            begin, end = _gather_steps(refs[8:8 + ng], refs[10 + ng:10 + 2 * ng], *refs[11 + 2 * ng:])

            @pl.when(pl.program_id(0) == 0)
            def _():
                begin()

        @pl.when(pl.program_id(0) == 0)
        def _():
            s_sc[...] = jnp.zeros_like(s_sc)

        ones3, ones = ones_ref[...], ones_ref[0:DG, :]
        dmask = dm_ref[...]

        def group(gi, carry):
            t0 = pl.multiple_of(gi * 8, 8)
            sl = pl.ds(t0, 8)
            v8 = v_ref[sl, :]
            wt = _coltile8(w_ref[sl, :], dmask, ones3, 3)
            ct, bt, kt, rt = _coltiles_bf16([c_ref[sl, :], b_ref[sl, :], k_ref[sl, :], r_ref[sl, :]], dmask, ones)
            t = s_sc[...]
            for j in range(8):
                sa = jnp.sum(t * ct[j], axis=0, keepdims=True)
                t = t * wt[j] + bt[j] * sa + kt[j] * v8[j:j + 1, :]
                st_ref[t0 + j] = t
            s_sc[...] = t
            y_ref[sl, :] = jnp.sum(st_ref[sl] * rt, axis=1)
            return carry

        lax.fori_loop(0, SCAN_TB // 8, group, 0)

        if ng:
            @pl.when(pl.program_id(0) == nblk - 1)
            def _():
                end()

    row = pl.BlockSpec((SCAN_TB, DG), lambda i: (i, 0))
    ins = [fl, w, k2, v2, c, b, p["seg64x3_bf16"], p["dmask"]] + list(gather)
    specs = [row] * 6 + [_full((3 * DG, DG)), _full((HD, DG))] + [ANY] * ng
    outs = pl.pallas_call(body, grid=(nblk,), in_specs=specs,
                          out_specs=[row, pl.BlockSpec((SCAN_TB, HD, DG), lambda i: (i, 0, 0))] + [ANY] * ng,
                          out_shape=[_sds((SEQ, DG)), _sds((SEQ, HD, DG))] + _gather_shapes(gather),
                          scratch_shapes=[pltpu.VMEM((HD, DG), F32)] + (_gather_sems(ng) if ng else []),
                          name=f"rwkv_scan_fwd{l}", compiler_params=_cp(("arbitrary",)))(*ins)
    return outs[0], outs[1], list(outs[2:])


def rwkv_scan_bwd(l, fl, w, k2, v2, c, b, states, dy, p, exchange=()):
    nblk = SEQ // SCAN_TB
    nx = len(exchange)

    def body(*refs):
        r_ref, w_ref, k_ref, v_ref, c_ref, b_ref, dy_ref, st_ref, sp_ref, ones_ref, dm_ref = refs[:11]
        dr_ref, dw_ref, dk_ref, dv_ref, dc_ref, db_ref = refs[11 + nx:17 + nx]
        g_sc, prev_sc, d8_sc, dsa_sc = refs[17 + 2 * nx:21 + 2 * nx]
        i = pl.program_id(0)
        if nx:
            begin, end = _chip_exchange_steps(refs[11:11 + nx], refs[17 + nx:17 + 2 * nx], *refs[21 + 2 * nx:])

            @pl.when(i == 0)
            def _():
                begin()

        @pl.when(i == 0)
        def _():
            g_sc[...] = jnp.zeros_like(g_sc)

        ones3, ones = ones_ref[...], ones_ref[0:DG, :]
        dmask = dm_ref[...]
        first_block = i == nblk - 1

        def group(gr, carry):
            gi = SCAN_TB // 8 - 1 - gr
            t0 = pl.multiple_of(gi * 8, 8)
            sl = pl.ds(t0, 8)
            v8, dy8 = v_ref[sl, :], dy_ref[sl, :]
            t8 = st_ref[sl]
            @pl.when(gi > 0)
            def _():
                prev_sc[0] = st_ref[t0 - 1]

            @pl.when(gi == 0)
            def _():
                prev_sc[0] = jnp.where(first_block, 0.0, sp_ref[0])

            for j in range(1, 8):
                prev_sc[j] = t8[j - 1]
            tp8 = prev_sc[...]
            wt = _coltile8(w_ref[sl, :], dmask, ones3, 3)
            ct, bt, kt, rt = _coltiles_bf16([c_ref[sl, :], b_ref[sl, :], k_ref[sl, :], r_ref[sl, :]], dmask, ones)
            sa8 = jnp.sum(tp8 * ct, axis=1)
            g = g_sc[...]
            for j in range(7, -1, -1):
                g = g + rt[j] * dy8[j:j + 1, :]
                d8_sc[j] = g
                dsa = jnp.sum(g * bt[j], axis=0, keepdims=True)
                dsa_sc[j:j + 1, :] = dsa
                g = g * wt[j] + ct[j] * dsa
            g_sc[...] = g
            d8 = d8_sc[...]
            dsa8 = dsa_sc[...]
            dv_ref[sl, :] = jnp.sum(d8 * kt, axis=1)
            dr_ref[sl, :] = _segrows8(t8 * dy8[:, None, :], dmask, ones)
            dk_ref[sl, :] = _segrows8(d8 * v8[:, None, :], dmask, ones)
            dw_ref[sl, :] = _segrows8(tp8 * d8, dmask, ones)
            db_ref[sl, :] = _segrows8(d8 * sa8[:, None, :], dmask, ones)
            dc_ref[sl, :] = _segrows8(tp8 * dsa8[:, None, :], dmask, ones)
            return carry

        lax.fori_loop(0, SCAN_TB // 8, group, 0)

        if nx:
            @pl.when(i == nblk - 1)
            def _():
                end()

    row = pl.BlockSpec((SCAN_TB, DG), lambda i: (nblk - 1 - i, 0))
    st_spec = pl.BlockSpec((SCAN_TB, HD, DG), lambda i: (nblk - 1 - i, 0, 0))
    sp_spec = pl.BlockSpec((1, HD, DG), lambda i: (jnp.maximum((nblk - 1 - i) * SCAN_TB - 1, 0), 0, 0))
    ins = [fl, w, k2, v2, c, b, dy, states, states, p["seg64x3_bf16"], p["dmask"]] + list(exchange)
    specs = [row] * 7 + [st_spec, sp_spec, _full((3 * DG, DG)), _full((HD, DG))] + [ANY] * nx
    tile8 = pltpu.VMEM((8, HD, DG), F32)
    sems = [pltpu.SemaphoreType.DMA((nx, 3)), pltpu.SemaphoreType.DMA((nx, 3))] if nx else []
    outs = pl.pallas_call(body, grid=(nblk,), in_specs=specs, out_specs=[row] * 6 + [ANY] * nx,
                          out_shape=[_sds((SEQ, DG))] * 6 + [_sds(a.shape, a.dtype) for a in exchange],
                          scratch_shapes=[pltpu.VMEM((HD, DG), F32), tile8, tile8, pltpu.VMEM((8, DG), F32)] + sems,
                          name=f"rwkv_scan_bwd{l}", compiler_params=_cp(("arbitrary",)))(*ins)
    return outs[:6], list(outs[6:])


HG_ROWS = 128


HG_NC = HG_ROWS // HGRN_CHUNK


def _hgrn_block_fn(layer):
    def fn(hq, hf, hi, hg, sprev, lb0, lb1, norm_w, seg, bd, tri_bd, ones_bd, first_row, causal):
        e0 = jnp.exp(lb0 - jnp.maximum(lb0, lb1))
        e1 = jnp.exp(lb1 - jnp.maximum(lb0, lb1))
        sm0, sm1 = e0 / (e0 + e1), e1 / (e0 + e1)
        lb = (sm0 - sm0) if layer == 0 else ((sm0 + sm1) - sm0)
        forget = lb + (1.0 - lb) * _sigmoid(hf)
        logf = jnp.log(forget)
        kk = 1.0 - forget
        q = _silu(hq)
        c, nc = HGRN_CHUNK, HG_NC
        b = _dotx(tri_bd, logf)
        bl = _dotx(ones_bd, logf)
        split = lambda t: t.reshape(nc, c, DG)
        b4 = split(b)
        diff = (b4[:, :, None, :] - b4[:, None, :, :]).reshape(nc * c * c, DG)
        dec = jnp.exp(jnp.where(causal > 0.5, diff, -1e30))
        qrep = jnp.broadcast_to(split(q)[:, :, None, :], (nc, c, c, DG)).reshape(nc * c * c, DG)
        ktil = jnp.broadcast_to(split(kk)[:, None, :, :], (nc, c, c, DG)).reshape(nc * c * c, DG)
        vtil = jnp.broadcast_to(split(hi)[:, None, :, :], (nc, c, c, DG)).reshape(nc * c * c, DG)
        att = _dot(qrep * ktil * dec, seg)
        o_intra = jnp.sum((att * vtil).reshape(nc * c, c, DG), axis=1)
        kd4 = split(kk * jnp.exp(bl - b))
        qe4 = split(q * jnp.exp(b))
        v4 = split(hi)
        tot = jnp.exp(_dotx(first_row, bl))
        s, o_inter = sprev, []
        for ci in range(nc):
            o_inter.append(_dot_nt(qe4[ci], s))
            s = s * tot[ci:ci + 1, :] + _dot_tn(v4[ci], kd4[ci]) * bd
        o = o_intra + jnp.concatenate(o_inter, axis=0)
        ms = _dot(o * o, seg) * (1.0 / HD)
        y = o * lax.rsqrt(ms + RMS_EPS) * norm_w * _silu(hg)
        return y, s

    return fn


def _hgrn_consts(p):
    return [p["seg64"], p["seg64"], p["tri_bd128"], p["ones_bd128"], p["first_row"], p["causal_blk"]]


def hgrn_fwd(l, proj, p):
    fn = _hgrn_block_fn(l)

    def body(hq_ref, hf_ref, hi_ref, hg_ref, *rest):
        const_refs, (y_ref, st_ref, s_sc) = rest[:-3], rest[-3:]

        @pl.when(pl.program_id(0) == 0)
        def _():
            s_sc[...] = jnp.zeros_like(s_sc)

        sprev = s_sc[...]
        st_ref[0] = sprev
        y, snext = fn(hq_ref[...], hf_ref[...], hi_ref[...], hg_ref[...], sprev, *[r[...] for r in const_refs])
        y_ref[...] = y
        s_sc[...] = snext

    rows = lambda cb: pl.BlockSpec((HG_ROWS, DG), lambda i: (i, cb))
    ins = [proj, proj, proj, proj, p["lb0"], p["lb1"], p["hgrn_norm_w"]] + _hgrn_consts(p)
    specs = [rows(C_HQ // DG), rows(C_HF // DG), rows(C_HI // DG), rows(C_HG // DG)] + [_full(x.shape) for x in ins[4:]]
    return pl.pallas_call(body, grid=(SEQ // HG_ROWS,), in_specs=specs,
                          out_specs=[rows(0), pl.BlockSpec((1, DG, DG), lambda i: (i, 0, 0))],
                          out_shape=[_sds((SEQ, DG)), _sds((SEQ // HG_ROWS, DG, DG))],
                          scratch_shapes=[pltpu.VMEM((DG, DG), F32)], name=f"hgrn_fwd{l}",
                          compiler_params=_cp(("arbitrary",)))(*ins)


def hgrn_bwd(l, proj, states, dy, p):
    fn = _hgrn_block_fn(l)
    nblk = SEQ // HG_ROWS
    n_const = len(_hgrn_consts(p))

    def body(hq_ref, hf_ref, hi_ref, hg_ref, st_ref, dy_ref, lb0_ref, lb1_ref, nw_ref, *rest):
        const_refs, (dp_ref, dlb0_ref, dlb1_ref, dnw_ref, ds_sc) = rest[:n_const], rest[n_const:]

        @pl.when(pl.program_id(0) == 0)
        def _():
            ds_sc[...] = jnp.zeros_like(ds_sc)
            dlb0_ref[...] = jnp.zeros_like(dlb0_ref)
            dlb1_ref[...] = jnp.zeros_like(dlb1_ref)
            dnw_ref[...] = jnp.zeros_like(dnw_ref)

        consts = [r[...] for r in const_refs]
        f = lambda hq, hf, hi, hg, sp, b0, b1, nw: fn(hq, hf, hi, hg, sp, b0, b1, nw, *consts)
        _, vjp = jax.vjp(f, hq_ref[...], hf_ref[...], hi_ref[...], hg_ref[...], st_ref[0], lb0_ref[...], lb1_ref[...],
                         nw_ref[...])
        dhq, dhf, dhi, dhg, dsp, dlb0, dlb1, dnw = vjp((dy_ref[...], ds_sc[...]))
        dp_ref[:, 0:DG] = dhq
        dp_ref[:, DG:2 * DG] = dhf
        dp_ref[:, 2 * DG:3 * DG] = dhi
        dp_ref[:, 3 * DG:4 * DG] = dhg
        ds_sc[...] = dsp
        dlb0_ref[...] += dlb0
        dlb1_ref[...] += dlb1
        dnw_ref[...] += dnw

    rows = lambda cb: pl.BlockSpec((HG_ROWS, DG), lambda i: (nblk - 1 - i, cb))
    ins = [proj, proj, proj, proj, states, dy, p["lb0"], p["lb1"], p["hgrn_norm_w"]] + _hgrn_consts(p)
    specs = [rows(C_HQ // DG), rows(C_HF // DG), rows(C_HI // DG), rows(C_HG // DG),
             pl.BlockSpec((1, DG, DG), lambda i: (nblk - 1 - i, 0, 0)), rows(0)] + [_full(x.shape) for x in ins[6:]]
    return pl.pallas_call(body, grid=(nblk,), in_specs=specs,
                          out_specs=[pl.BlockSpec((HG_ROWS, 4 * DG), lambda i: (nblk - 1 - i, 0)), _full((1, DG)),
                                     _full((1, DG)), _full((1, DG))],
                          out_shape=[_sds((SEQ, 4 * DG)), _sds((1, DG)), _sds((1, DG)), _sds((1, DG))],
                          scratch_shapes=[pltpu.VMEM((DG, DG), F32)], name=f"hgrn_bwd{l}",
                          compiler_params=_cp(("arbitrary",)))(*ins)


def _ssd_chunk_fn(z, xs, bm, cm, dtr, sprev, dt_bias, a_log, d_par, norm_w, e128, tri, trit, seg128, ones128):
    lc = SSD_CHUNK
    dt = _softplus(dtr + dt_bias)
    a = -jnp.exp(a_log)
    da = dt * a * (lax.broadcasted_iota(jnp.int32, (1, 128), 1) < NH).astype(F32)
    cs = _dotx(tri, da)
    cst = _dotx_tn(da, trit)
    cs_b = _dotx(cs, e128)
    dt_b = _dotx(dt, e128)
    csl_b = _dotx(jnp.sum(da, axis=0, keepdims=True), e128)
    xdt = xs * dt_b
    lane = lax.broadcasted_iota(jnp.int32, (1, DG), 1)
    rowi = lax.broadcasted_iota(jnp.int32, (lc, lc), 0)
    coli = lax.broadcasted_iota(jnp.int32, (lc, lc), 1)
    y = jnp.zeros((lc, DG), F32)
    snew = jnp.zeros((DG, SSD_N), F32)
    d_b = jnp.zeros((1, DG), F32)
    wdec = xdt * jnp.exp(csl_b - cs_b)
    for g in range(2):
        bg = bm[:, g * SSD_N:(g + 1) * SSD_N]
        cg = cm[:, g * SSD_N:(g + 1) * SSD_N]
        gmat = _dot_nt(cg, bg)
        gmask = ((lane // 128) == g).astype(F32)
        snew = snew + _dot_tn(wdec * gmask, bg)
        y = y + _dot_nt(cg, sprev) * gmask * jnp.exp(cs_b)
        for hh in range(2):
            h = 2 * g + hh
            seg = jnp.where(rowi >= coli, cs[:, h:h + 1] - cst[h:h + 1, :], -1e30)
            hmask = ((lane // HD) == h).astype(F32)
            y = y + _dot(gmat * jnp.exp(seg), xdt * hmask)
            d_b = d_b + d_par[:, h:h + 1] * hmask
    cd = jnp.exp(_dotx_tn(_dotx(da, e128), ones128))
    snext = sprev * cd + snew
    y = y + xs * d_b
    y = y * _silu(z)
    ms = _dot(y * y, seg128) * (1.0 / 128.0)
    return y * lax.rsqrt(ms + RMS_EPS) * norm_w, snext


def ssd_fwd(l, proj, xc, p):
    nc = SEQ // SSD_CHUNK

    def body(z_ref, xs_ref, b_ref, c_ref, dt_ref, dtb_ref, al_ref, d_ref, nw_ref, e_ref, tri_ref, trit_ref, sg_ref,
             on_ref, y_ref, st_ref, s_sc):
        @pl.when(pl.program_id(0) == 0)
        def _():
            s_sc[...] = jnp.zeros_like(s_sc)

        sprev = s_sc[...]
        st_ref[0] = sprev
        y, snext = _ssd_chunk_fn(z_ref[...], xs_ref[...], b_ref[...], c_ref[...], dt_ref[...], sprev, dtb_ref[...],
                                 al_ref[...], d_ref[...], nw_ref[...], e_ref[...], tri_ref[...], trit_ref[...],
                                 sg_ref[...], on_ref[...])
        y_ref[...] = y
        s_sc[...] = snext

    rw = lambda w, cb: pl.BlockSpec((SSD_CHUNK, w), lambda i: (i, cb))
    ins = [proj, xc, xc, xc, proj, p["dt_bias"], p["a_log"], p["ssd_d"], p["ssd_norm_w"], p["e128"], p["tri128"],
           p["tri128t"], p["seg128"], p["ones128"]]
    specs = [rw(DG, C_Z // DG), rw(DG, 0), rw(DG, 1), rw(DG, 2), rw(128, C_DT // 128)] + [_full(x.shape) for x in ins[5:]]
    return pl.pallas_call(body, grid=(nc,), in_specs=specs,
                          out_specs=[rw(DG, 0), pl.BlockSpec((1, DG, SSD_N), lambda i: (i, 0, 0))],
                          out_shape=[_sds((SEQ, DG)), _sds((nc, DG, SSD_N))],
                          scratch_shapes=[pltpu.VMEM((DG, SSD_N), F32)], name=f"ssd_fwd{l}",
                          compiler_params=_cp(("arbitrary",)))(*ins)


def ssd_bwd(l, proj, xc, states, dy, p):
    nc = SEQ // SSD_CHUNK

    def body(z_ref, xs_ref, b_ref, c_ref, dt_ref, st_ref, dy_ref, dtb_ref, al_ref, d_ref, nw_ref, e_ref, tri_ref,
             trit_ref, sg_ref, on_ref, dz_ref, dxc_ref, ddt_ref, ddtb_ref, dal_ref, dd_ref, dnw_ref, ds_sc):
        @pl.when(pl.program_id(0) == 0)
        def _():
            ds_sc[...] = jnp.zeros_like(ds_sc)
            ddtb_ref[...] = jnp.zeros_like(ddtb_ref)
            dal_ref[...] = jnp.zeros_like(dal_ref)
            dd_ref[...] = jnp.zeros_like(dd_ref)
            dnw_ref[...] = jnp.zeros_like(dnw_ref)

        consts = (e_ref[...], tri_ref[...], trit_ref[...], sg_ref[...], on_ref[...])
        f = lambda *a: _ssd_chunk_fn(*a, *consts)
        _, vjp = jax.vjp(f, z_ref[...], xs_ref[...], b_ref[...], c_ref[...], dt_ref[...], st_ref[0], dtb_ref[...],
                         al_ref[...], d_ref[...], nw_ref[...])
        dz, dxs, db, dc, ddt, dsp, ddtb, dal, dd, dnw = vjp((dy_ref[...], ds_sc[...]))
        dz_ref[...] = dz
        dxc_ref[:, 0:DG] = dxs
        dxc_ref[:, DG:2 * DG] = db
        dxc_ref[:, 2 * DG:3 * DG] = dc
        ddt_ref[...] = ddt
        ds_sc[...] = dsp
        ddtb_ref[...] += ddtb
        dal_ref[...] += dal
        dd_ref[...] += dd
        dnw_ref[...] += dnw

    rw = lambda w, cb: pl.BlockSpec((SSD_CHUNK, w), lambda i: (nc - 1 - i, cb))
    ins = [proj, xc, xc, xc, proj, states, dy, p["dt_bias"], p["a_log"], p["ssd_d"], p["ssd_norm_w"], p["e128"],
           p["tri128"], p["tri128t"], p["seg128"], p["ones128"]]
    specs = [rw(DG, C_Z // DG), rw(DG, 0), rw(DG, 1), rw(DG, 2), rw(128, C_DT // 128),
             pl.BlockSpec((1, DG, SSD_N), lambda i: (nc - 1 - i, 0, 0)), rw(DG, 0)] + [_full(x.shape) for x in ins[7:]]
    return pl.pallas_call(body, grid=(nc,), in_specs=specs,
                          out_specs=[rw(DG, 0), rw(3 * DG, 0), rw(128, 0), _full((1, 128)), _full((1, 128)), _full((1, 128)),
                                     _full((1, DG))],
                          out_shape=[_sds((SEQ, DG)), _sds((SEQ, 3 * DG)), _sds((SEQ, 128)), _sds((1, 128)), _sds((1, 128)),
                                     _sds((1, 128)), _sds((1, DG))],
                          scratch_shapes=[pltpu.VMEM((DG, SSD_N), F32)], name=f"ssd_bwd{l}",
                          compiler_params=_cp(("arbitrary",)))(*ins)


ATT_BLK = 128


def _att_scores(qn, kc, kp, h, dil, has_prev):
    i = lax.broadcasted_iota(jnp.int32, (ATT_BLK, ATT_BLK), 0)
    j = lax.broadcasted_iota(jnp.int32, (ATT_BLK, ATT_BLK), 1)
    slope = 2.0 ** (-8.0 * (h + 1) / NH)
    scale = HD ** -0.5
    s_c = _dot_nt(qn, kc) * scale - slope * ((i - j) * dil).astype(F32)
    s_p = _dot_nt(qn, kp) * scale - slope * ((ATT_BLK + i - j) * dil).astype(F32)
    m_c = j <= i
    m_p = jnp.logical_and(j >= i, has_prev)
    return jnp.where(m_c, s_c, -1e30), jnp.where(m_p, s_p, -1e30), m_c, m_p


def _sub_spec(ln, width, col):
    return pl.BlockSpec((ln, DG), lambda z: (0, z * (width // DG) + col // DG))


QKV_W = 3 * DG


def attn_branch_fwd(l, bi, qkv, dil):
    ln = SEQ // dil
    nb = ln // ATT_BLK

    def body(q_ref, k_ref, v_ref, o_ref, l_ref):
        def blk(n, carry):
            r0 = pl.multiple_of(n * ATT_BLK, ATT_BLK)
            rp = pl.multiple_of(jnp.maximum(n - 1, 0) * ATT_BLK, ATT_BLK)
            cur, prv = pl.ds(r0, ATT_BLK), pl.ds(rp, ATT_BLK)
            for h in range(NH):
                hs = slice(h * HD, (h + 1) * HD)
                qn, kc, vc, kp, vp = q_ref[cur, hs], k_ref[cur, hs], v_ref[cur, hs], k_ref[prv, hs], v_ref[prv, hs]
                s_c, s_p, m_c, m_p = _att_scores(qn, kc, kp, h, dil, n > 0)
                m = jnp.maximum(jnp.max(s_c, axis=1, keepdims=True), jnp.max(s_p, axis=1, keepdims=True))
                p_c = jnp.where(m_c, jnp.exp(s_c - m), 0.0)
                p_p = jnp.where(m_p, jnp.exp(s_p - m), 0.0)
                den = jnp.sum(p_c, axis=1, keepdims=True) + jnp.sum(p_p, axis=1, keepdims=True)
                o_ref[cur, hs] = (_dot(p_c, vc) + _dot(p_p, vp)) / den
                l_ref[cur, hs] = jnp.broadcast_to(m + jnp.log(den), (ATT_BLK, HD))
            return carry

        lax.fori_loop(0, nb, blk, 0)

    pv = qkv.reshape(ln, dil * QKV_W)
    out = pl.BlockSpec((ln, DG), lambda z: (0, z))
    o, lse = pl.pallas_call(body, grid=(dil,), in_specs=[_sub_spec(ln, QKV_W, 0), _sub_spec(ln, QKV_W, DG), _sub_spec(ln, QKV_W, 2 * DG)],
                            out_specs=[out, out], out_shape=[_sds((ln, dil * DG))] * 2, name=f"attn_fwd{l}_{bi}",
                            compiler_params=_cp(("parallel",)))(pv, pv, pv)
    return o.reshape(SEQ, DG), lse.reshape(SEQ, DG)


def attn_branch_bwd(l, bi, qkv, dil, dyb, lse_all, delta):
    ln = SEQ // dil
    nb = ln // ATT_BLK
    scale = HD ** -0.5

    def body(q_ref, k_ref, v_ref, do_ref, l_ref, dl_ref, dq_ref, dk_ref, dv_ref):
        dk_ref[...] = jnp.zeros_like(dk_ref)
        dv_ref[...] = jnp.zeros_like(dv_ref)

        def blk(n, carry):
            r0 = pl.multiple_of(n * ATT_BLK, ATT_BLK)
            rp = pl.multiple_of(jnp.maximum(n - 1, 0) * ATT_BLK, ATT_BLK)
            cur, prv = pl.ds(r0, ATT_BLK), pl.ds(rp, ATT_BLK)
            for h in range(NH):
                hs = slice(h * HD, (h + 1) * HD)
                qn, don = q_ref[cur, hs], do_ref[cur, hs]
                lse, dlt = l_ref[cur, h * HD:h * HD + 1], dl_ref[cur, h * HD:h * HD + 1]
                kc, vc, kp, vp = k_ref[cur, hs], v_ref[cur, hs], k_ref[prv, hs], v_ref[prv, hs]
                s_c, s_p, m_c, m_p = _att_scores(qn, kc, kp, h, dil, n > 0)
                p_c = jnp.where(m_c, jnp.exp(s_c - lse), 0.0)
                p_p = jnp.where(m_p, jnp.exp(s_p - lse), 0.0)
                ds_c = p_c * (_dot_nt(don, vc) - dlt)
                ds_p = p_p * (_dot_nt(don, vp) - dlt)
                dq_ref[cur, hs] = (_dot(ds_c, kc) + _dot(ds_p, kp)) * scale
                dv_ref[prv, hs] += _dot_tn(p_p, don)
                dk_ref[prv, hs] += _dot_tn(ds_p, qn) * scale
                dv_ref[cur, hs] += _dot_tn(p_c, don)
                dk_ref[cur, hs] += _dot_tn(ds_c, qn) * scale
            return carry

        lax.fori_loop(0, nb, blk, 0)

    pv = qkv.reshape(ln, dil * QKV_W)
    sub = lambda t: t.reshape(ln, dil * DG)
    row = pl.BlockSpec((ln, DG), lambda z: (0, z))
    outs = pl.pallas_call(body, grid=(dil,),
                          in_specs=[_sub_spec(ln, QKV_W, 0), _sub_spec(ln, QKV_W, DG), _sub_spec(ln, QKV_W, 2 * DG), row, row, row],
                          out_specs=[row] * 3, out_shape=[_sds((ln, dil * DG))] * 3, name=f"attn_bwd{l}_{bi}",
                          compiler_params=_cp(("parallel",)))(pv, pv, pv, sub(dyb), sub(lse_all), sub(delta))
    return [t.reshape(SEQ, DG) for t in outs]


def _attn_merge_fn(o1, o2, o3, l1, l2, l3):
    m = jnp.maximum(jnp.maximum(l1, l2), l3)
    w1, w2, w3 = jnp.exp(l1 - m), jnp.exp(l2 - m), jnp.exp(l3 - m)
    den = w1 + w2 + w3
    return (w1 * o1 + w2 * o2 + w3 * o3) / den, m + jnp.log(den)


def attn_merge(l, os_, ls_):
    ins = list(os_) + list(ls_)
    return _map_fwd(f"attn_merge{l}", _attn_merge_fn, (SEQ // RB,), ins, [_rows(DG)] * 6, [_sds((SEQ, DG))] * 2,
                    [_rows(DG)] * 2)


def attn_delta(l, dyb, yb, seg):
    fn = lambda d, y, s: (_dot(d * y, s),)
    return _map_fwd(f"attn_delta{l}", fn, (SEQ // RB,), [dyb, yb, seg], [_rows(DG), _rows(DG), _full((DG, DG))],
                    [_sds((SEQ, DG))], [_rows(DG)])[0]


def _ln_fn(x, mix, w, b):
    h = ALPHA * x + mix
    mu = jnp.mean(h, axis=-1, keepdims=True)
    d = h - mu
    var = jnp.mean(d * d, axis=-1, keepdims=True)
    return (d * lax.rsqrt(var + LN_EPS) * w + b,)


def ln_fwd(name, x, mix, w, b):
    specs = [_rows(D_MODEL), _rows(D_MODEL), _full((1, D_MODEL)), _full((1, D_MODEL))]
    return _map_fwd(name, _ln_fn, (SEQ // RB,), [x, mix, w, b], specs, [_sds((SEQ, D_MODEL))], [_rows(D_MODEL)])[0]


def ln_bwd(name, x, mix, w, b, dy):
    specs = [_rows(D_MODEL), _rows(D_MODEL), _full((1, D_MODEL)), _full((1, D_MODEL))]
    return _map_bwd(name, _ln_fn, (SEQ // RB,), [x, mix, w, b], specs, [[dy]], [[_rows(D_MODEL)]], want=[1, 2, 3],
                    acc=(2, 3))


def loss_call(y, tgt):
    def fn(yy, tt):
        e = yy - tt
        part = 0.5 * jnp.sum(jnp.sum(e * e, axis=-1, keepdims=True) * (1.0 / D_MODEL), axis=0, keepdims=True)
        return e * (1.0 / D_MODEL), jnp.broadcast_to(part, (8, 128))

    return _map_fwd("loss", fn, (SEQ // RB,), [y, tgt], [_rows(D_MODEL)] * 2,
                    [_sds((SEQ, D_MODEL)), _sds((SEQ // RB * 8, 128))],
                    [_rows(D_MODEL), pl.BlockSpec((8, 128), lambda i: (i, 0))])


LATE_KEYS = ("w_out", "w_up_t", "w_down")


def _full_rows(g):
    return g.reshape(N_DEV * g.shape[1], g.shape[2])


def layer_fwd(l, x, vfirst, wts, p, gather=(), late=False):
    sv = {"x": x}
    proj = _mm(f"mm_in{l}", x, wts["w_in"], "nn", *MM_TILES["k1024"])
    fl = lerp_fwd(l, proj, p["mu"])
    xc = conv_fwd(l, proj, p["conv_w"], p["conv_b"])
    w, k2, v2, c, b, g = rwkv_pre_fwd(l, fl, vfirst, p)
    y_scan, states, sv["gathered"] = rwkv_scan_fwd(l, fl, w, k2, v2, c, b, p, gather)
    if late:
        wts = dict(wts, **dict(zip(LATE_KEYS, [_full_rows(g) for g in sv["gathered"][:3]])))
    sv["wts"] = wts
    ya = rwkv_post_fwd(l, y_scan, fl, k2, v2, g, p)
    qkv = proj[:, C_AQ:C_AQ + 3 * DG]
    outs, lses = [], []
    for bi, (win, dil) in enumerate(DILATED):
        o, lse = attn_branch_fwd(l, bi, qkv, dil)
        outs.append(o)
        lses.append(lse)
    yb, lse_all = attn_merge(l, outs, lses)
    yc, ssd_states = ssd_fwd(l, proj, xc, p)
    yd, hg_states = hgrn_fwd(l, proj, p)
    ycat = jnp.concatenate([ya, yb, yc, yd], axis=1)
    mix = _mm(f"mm_out{l}", ycat, wts["w_out"], "nn", *MM_TILES["k1024"])
    x1 = ln_fwd(f"ln1_fwd{l}", x, mix, p["ln1_w"], p["ln1_b"])
    hh = _mm(f"mm_up{l}", x1, wts["w_up_t"], "nt", *MM_TILES["k1024"], epilogue="relu2")
    m2 = _mm(f"mm_down{l}", hh, wts["w_down"], "nn", *MM_TILES["k4096"])
    x2 = ln_fwd(f"ln2_fwd{l}", x1, m2, p["ln2_w"], p["ln2_b"])
    sv.update(proj=proj, fl=fl, xc=xc, w=w, k2=k2, v2=v2, c=c, b=b, g=g, y_scan=y_scan, states=states,
              yb=yb, lse_all=lse_all, ssd_states=ssd_states, hg_states=hg_states, ycat=ycat, mix=mix, x1=x1, hh=hh, qkv=qkv,
              m2=m2, vfirst=vfirst)
    return x2, sv


def layer_bwd(l, dx2, dvfirst_next, sv, wts, p, exchange=(), early=None):
    gr = {}
    x, x1, proj, fl = sv["x"], sv["x1"], sv["proj"], sv["fl"]
    dres2, gr["ln2_w"], gr["ln2_b"] = ln_bwd(f"ln2_bwd{l}", x1, sv["m2"], p["ln2_w"], p["ln2_b"], dx2)
    du = _mm(f"mm_down_dx{l}", dres2, wts["w_down"], "nt", *MM_TILES["k1024"], add=sv["hh"], epilogue="relu2_bwd")
    gr["w_down"] = _mm(f"mm_down_dw{l}", sv["hh"], dres2, "tn", *MM_TILES["wgrad_tall"])
    dx1 = _mm(f"mm_up_dx{l}", du, wts["w_up_t"], "nn", *MM_TILES["k4096"], add=dres2, add_scale=ALPHA)
    gr["w_up_t"] = _mm(f"mm_up_dw{l}", du, x1, "tn", *MM_TILES["wgrad_tall"])
    dres1, gr["ln1_w"], gr["ln1_b"] = ln_bwd(f"ln1_bwd{l}", x, sv["mix"], p["ln1_w"], p["ln1_b"], dx1)
    dycat = _mm(f"mm_out_dx{l}", dres1, wts["w_out"], "nt", *MM_TILES["k1024"])
    gr["w_out"] = _mm(f"mm_out_dw{l}", sv["ycat"], dres1, "tn", 1024, 1024, 512)
    if early is not None:
        gr["early_own"], early_parts = early({k: gr[k] for k in LATE_KEYS})
        exchange = list(exchange) + list(early_parts)
    dya, dyb, dyc, dyd = (dycat[:, i * DG:(i + 1) * DG] for i in range(4))
    dhg4, gr["lb0"], gr["lb1"], gr["hgrn_norm_w"] = hgrn_bwd(l, proj, sv["hg_states"], dyd, p)
    dz, dxc, ddt, gr["dt_bias"], gr["a_log"], gr["ssd_d"], gr["ssd_norm_w"] = ssd_bwd(l, proj, sv["xc"], sv["ssd_states"], dyc, p)
    dxbc, gr["conv_w"], gr["conv_b"] = conv_bwd(l, proj, p["conv_w"], p["conv_b"], dxc)
    delta = attn_delta(l, dyb, sv["yb"], p["seg64"])
    dqs, dks, dvs = [], [], []
    for bi, (win, dil) in enumerate(DILATED):
        dq, dk, dv = attn_branch_bwd(l, bi, sv["qkv"], dil, dyb, sv["lse_all"], delta)
        dqs.append(dq)
        dks.append(dk)
        dvs.append(dv)
    dq_a, dk_a, dv_a = _addn(f"attn_dq{l}", *dqs), _addn(f"attn_dk{l}", *dks), _addn(f"attn_dv{l}", *dvs)
    pg = rwkv_post_bwd(l, sv["y_scan"], fl, sv["k2"], sv["v2"], sv["g"], p, dya)
    gr["lnx_w"], gr["lnx_b"], gr["r_k"] = pg["lnx_w"], pg["lnx_b"], pg["r_k"]
    (dr, dw, dk, dv, dc, db), gr["exchanged"] = rwkv_scan_bwd(l, fl, sv["w"], sv["k2"], sv["v2"], sv["c"], sv["b"],
                                                              sv["states"], pg["y"], p, exchange)
    v2_cts = [dv, pg["v2"]] + ([dvfirst_next] if dvfirst_next is not None else [])
    qg = rwkv_pre_bwd(l, fl, sv["vfirst"], p, [[dw], [dk, pg["k2"]], v2_cts, [dc], [db], [pg["g"]]])
    for nme in ("w0", "w2p", "a0", "a2p", "g2p", "k_k", "k_a", "v0", "v2p"):
        if nme in qg:
            gr[nme] = qg[nme]
    dfr = _addn(f"rwkv_dr{l}", dr, pg["fr"])
    dvres = qg["fvres"] if l > 0 else jnp.zeros((SEQ, 128), F32)
    dfl_out = jnp.concatenate([dfr, qg["fk"], qg["fv"], qg["flora"], dvres], axis=1)
    dfl_in, gr["mu"] = lerp_bwd(l, proj, p["mu"], dfl_out)
    dproj = jnp.concatenate([dfl_in[:, 0:768], dq_a, dk_a, dv_a, dz, dxbc, dhg4, dfl_in[:, 768:896], ddt,
                             dfl_in[:, 896:1024], jnp.zeros((SEQ, 128), F32)], axis=1)
    dx = _mm(f"mm_in_dx{l}", dproj, wts["w_in"], "nt", *MM_TILES["k4096"], add=dres1, add_scale=ALPHA)
    gr["w_in"] = _mm(f"mm_in_dw{l}", x, dproj, "tn", *MM_TILES["wgrad_wide"])
    return dx, (qg["vfirst"] if l > 0 else None), gr


def _w_in_pad(w_in_l, w_vres):
    rows = w_in_l.shape[0]
    z = lambda n: jnp.zeros((rows, n), w_in_l.dtype)
    vres = z(128) if w_vres is None else jnp.concatenate([w_vres, z(96)], axis=1)
    return jnp.concatenate([w_in_l[:, 0:768], w_in_l[:, 896:1664], w_in_l[:, 1664:1920], w_in_l[:, 1920:2688],
                            w_in_l[:, 2692:3716], w_in_l[:, 768:896], w_in_l[:, 2688:2692], z(124), vres, z(128)], axis=1)


def _w_in_unpad(g):
    g_in = jnp.concatenate([g[:, 0:768], g[:, C_LORA:C_LORA + 128], g[:, 768:1536], g[:, C_Z:C_Z + 256],
                            g[:, C_XBC:C_XBC + 768], g[:, C_DT:C_DT + 4], g[:, C_HQ:C_HQ + 1024]], axis=1)
    return g_in, g[:, C_VRES:C_VRES + 32]


def _consts():
    pair = jnp.arange(HG_NC * HGRN_CHUNK * HGRN_CHUNK)
    i128 = jnp.arange(128)
    same_chunk = (i128[:, None] // HGRN_CHUNK) == (i128[None, :] // HGRN_CHUNK)
    seg64 = _seg_ones(DG, HD)
    tri128 = (i128[:, None] >= i128[None, :]).astype(F32)
    return dict(
        seg64=seg64, seg64x3_bf16=jnp.concatenate([seg64, seg64, seg64], axis=0).astype(BF16),
        dmask=(jnp.arange(HD)[:, None] == (jnp.arange(DG)[None, :] % HD)).astype(F32),
        tri_bd128=(same_chunk & (i128[:, None] >= i128[None, :])).astype(F32), ones_bd128=same_chunk.astype(F32),
        first_row=(i128[None, :] == (jnp.arange(HG_NC) * HGRN_CHUNK)[:, None]).astype(F32),
        causal_blk=jnp.broadcast_to((((pair // HGRN_CHUNK) % HGRN_CHUNK) >= (pair % HGRN_CHUNK)).astype(F32)[:, None],
                                    (HG_NC * HGRN_CHUNK * HGRN_CHUNK, DG)),
        e128=((i128[:, None] == (jnp.arange(DG)[None, :] // HD)) & (i128[:, None] < NH)).astype(F32),
        tri128=tri128, tri128t=tri128.T, seg128=_seg_ones(DG, 128), ones128=jnp.ones((128, 128), F32))


def _pad_lanes(v, n):
    return jnp.concatenate([v, jnp.zeros((n - v.shape[0],), v.dtype)])[None, :]


def _layer_params(l, raw, consts):
    p = dict(consts)
    row = lambda name: raw[name][l][None, :]
    z = lambda r: jnp.zeros((r, DG), F32)
    mu_vres = raw["mu_vres"][l - 1] if l > 0 else jnp.zeros((32,), F32)
    p["mu"] = jnp.concatenate([raw["mu_shift"][l], mu_vres, jnp.zeros((96,), F32)])[None, :]
    p["conv_w"], p["conv_b"] = raw["ssd_conv_w"][l], row("ssd_conv_b")
    p["w0"], p["a0"], p["k_k"], p["k_a"] = row("rwkv_w0"), row("rwkv_a0"), row("rwkv_k_k"), row("rwkv_k_a")
    p["lnx_w"], p["lnx_b"] = row("rwkv_lnx_w"), row("rwkv_lnx_b")
    p["r_k"] = raw["rwkv_r_k"][l].reshape(1, DG)
    p["w2p"] = jnp.concatenate([raw["rwkv_w2"][l], z(96)], axis=0)
    p["a2p"] = jnp.concatenate([z(32), raw["rwkv_a2"][l], z(64)], axis=0)
    p["g2p"] = jnp.concatenate([z(64), raw["rwkv_g2"][l]], axis=0)
    if l > 0:
        p["v0"] = raw["rwkv_v0"][l - 1][None, :]
        p["v2p"] = jnp.concatenate([raw["rwkv_v2"][l - 1], z(96)], axis=0)
    p["lb0"], p["lb1"] = raw["lower_bounds"][0:1], raw["lower_bounds"][1:2]
    p["hgrn_norm_w"], p["ssd_norm_w"] = row("hgrn_norm_w"), row("ssd_norm_w")
    p["dt_bias"], p["a_log"], p["ssd_d"] = (_pad_lanes(raw[n][l], 128) for n in ("ssd_dt_bias", "ssd_A_log", "ssd_D"))
    for n in ("ln1_w", "ln1_b", "ln2_w", "ln2_b"):
        p[n] = row(n)
    return p


def _natural_grads(g0, g1):
    gs = (g0, g1)
    st = lambda key, f=lambda a: a[0]: jnp.stack([f(g[key]) for g in gs])
    out = {}
    out["lower_bounds"] = jnp.concatenate([g0["lb0"] + g1["lb0"], g0["lb1"] + g1["lb1"]], axis=0)
    out["mu_shift"] = st("mu", lambda a: a[0, :896])
    out["mu_vres"] = g1["mu"][:, 896:928]
    out["rwkv_w0"], out["rwkv_a0"], out["rwkv_k_k"], out["rwkv_k_a"] = st("w0"), st("a0"), st("k_k"), st("k_a")
    out["rwkv_w2"] = st("w2p", lambda a: a[0:32])
    out["rwkv_a2"] = st("a2p", lambda a: a[32:64])
    out["rwkv_g2"] = st("g2p", lambda a: a[64:128])
    out["rwkv_r_k"] = st("r_k", lambda a: a.reshape(NH, HD))
    out["rwkv_lnx_w"], out["rwkv_lnx_b"] = st("lnx_w"), st("lnx_b")
    out["rwkv_v0"] = g1["v0"]
    out["rwkv_v2"] = g1["v2p"][None, 0:32]
    out["ssd_conv_w"] = st("conv_w", lambda a: a)
    out["ssd_conv_b"] = st("conv_b")
    out["ssd_dt_bias"], out["ssd_A_log"], out["ssd_D"] = (st(k, lambda a: a[0, :NH]) for k in ("dt_bias", "a_log", "ssd_d"))
    out["ssd_norm_w"], out["hgrn_norm_w"] = st("ssd_norm_w"), st("hgrn_norm_w")
    for n in ("ln1_w", "ln1_b", "ln2_w", "ln2_b"):
        out[n] = st(n)
    return out


MESH_T = pl.DeviceIdType.MESH
ANY = pl.BlockSpec(memory_space=pl.ANY)


def _dev_index(px, py, pc):
    return 4 * px + 2 * py + pc


def all_gather(arrs):
    n = len(arrs)

    def body(*refs):
        begin, end = _gather_steps(refs[:n], refs[n:2 * n], *refs[2 * n:])
        begin()
        end()

    return pl.pallas_call(body, in_specs=[ANY] * n, out_specs=[ANY] * n, out_shape=_gather_shapes(arrs),
                          scratch_shapes=_gather_sems(n), name="all_gather")(*arrs)


def _gather_shapes(arrs):
    return [_sds((N_DEV,) + a.shape, a.dtype) for a in arrs]


def _gather_sems(n):
    return [pltpu.SemaphoreType.DMA((n, 7)), pltpu.SemaphoreType.DMA((n, 7)), pltpu.SemaphoreType.DMA((n,))]


def _gather_steps(ins, outs, send_sems, recv_sems, local_sems):
    n = len(ins)
    x, y, c = lax.axis_index("x"), lax.axis_index("y"), lax.axis_index("c")
    me, sibling = (x, y, c), (x, y, 1 - c)
    chips = [(1 - x, y), (x, 1 - y), (1 - x, 1 - y)]

    def copy(a, k, block, to, src=None):
        slot = outs[a].at[_dev_index(*block)]
        return pltpu.make_async_remote_copy(src_ref=slot if src is None else src, dst_ref=slot,
                                            send_sem=send_sems.at[a, k], recv_sem=recv_sems.at[a, k],
                                            device_id=to, device_id_type=MESH_T)

    def own_copies():
        mine = [pltpu.make_async_copy(ins[a], outs[a].at[_dev_index(*me)], local_sems.at[a]) for a in range(n)]
        first = []
        for a in range(n):
            first.append(copy(a, 0, me, sibling, src=ins[a]))
            first += [copy(a, 1 + j, me, (*chip, c), src=ins[a]) for j, chip in enumerate(chips)]
        return mine, first

    def begin():
        mine, first = own_copies()
        for cp in mine + first:
            cp.start()

    def end():
        mine, first = own_copies()
        passed = []
        for j, chip in enumerate(chips):
            for a in range(n):
                copy(a, 1 + j, (*chip, c), me).wait_recv()
                fwd = copy(a, 4 + j, (*chip, c), sibling)
                fwd.start()
                passed.append(fwd)
        for a in range(n):
            copy(a, 0, sibling, me).wait_recv()
            for j, chip in enumerate(chips):
                copy(a, 4 + j, (*chip, 1 - c), me).wait_recv()
        for cp in first + passed:
            cp.wait_send()
        for cp in mine:
            cp.wait()

    return begin, end


def _chips(x, y):
    return [(x, y), (1 - x, y), (x, 1 - y), (1 - x, 1 - y)]


def exchange_siblings(arrs, name):
    n = len(arrs)

    def body(*refs):
        ins, sib = refs[:n], refs[n:2 * n]
        send_sems, recv_sems = refs[2 * n:]
        x, y, c = lax.axis_index("x"), lax.axis_index("y"), lax.axis_index("c")
        sibling = (x, y, 1 - c)
        sends = []
        for a in range(n):
            for k, (cx, cy) in enumerate(_chips(x, y)):
                sd = pltpu.make_async_remote_copy(src_ref=ins[a].at[_dev_index(cx, cy, 1 - c)], dst_ref=sib[a].at[k],
                                                  send_sem=send_sems.at[a, k], recv_sem=recv_sems.at[a, k],
                                                  device_id=sibling, device_id_type=MESH_T)
                sd.start()
                sends.append(sd)
        for sd in sends:
            sd.wait_recv()
        for sd in sends:
            sd.wait_send()

    sem = pltpu.SemaphoreType.DMA((n, 4))
    return pl.pallas_call(body, in_specs=[ANY] * n, out_specs=[ANY] * n,
                          out_shape=[_sds((4,) + a.shape[1:], a.dtype) for a in arrs],
                          scratch_shapes=[sem, sem], name=name)(*arrs)


def reduce_pair(name, send, slots, sib, wire_dtype):
    _, r, c = send.shape
    rb = min(r, 262144 // c)

    def body(slots_ref, m0, m1, m2, m3, s_ref, own_ref, part_ref):
        own_ref[...] = m0[...] + s_ref[0]
        for k, m_ref in enumerate((m1, m2, m3)):
            part_ref[k] = (m_ref[...] + s_ref[k + 1]).astype(wire_dtype)

    mine = [pl.BlockSpec((None, rb, c), lambda i, s, k=k: (s[k], i, 0)) for k in range(4)]
    grid_spec = pltpu.PrefetchScalarGridSpec(
        num_scalar_prefetch=1, grid=(r // rb,),
        in_specs=mine + [pl.BlockSpec((4, rb, c), lambda i, s: (0, i, 0))],
        out_specs=[pl.BlockSpec((rb, c), lambda i, s: (i, 0)), pl.BlockSpec((3, rb, c), lambda i, s: (0, i, 0))])
    return pl.pallas_call(body, grid_spec=grid_spec, out_shape=[_sds((r, c)), _sds((3, r, c), wire_dtype)], name=name,
                          compiler_params=_cp(("parallel",)))(slots, send, send, send, send, sib)


def _chip_exchange_steps(ins, recv, send_sems, recv_sems):
    x, y, c = lax.axis_index("x"), lax.axis_index("y"), lax.axis_index("c")

    def copies():
        return [pltpu.make_async_remote_copy(src_ref=ins[a].at[k], dst_ref=recv[a].at[k], send_sem=send_sems.at[a, k],
                                             recv_sem=recv_sems.at[a, k], device_id=(cx, cy, c), device_id_type=MESH_T)
                for a in range(len(ins)) for k, (cx, cy) in enumerate(_chips(x, y)[1:])]

    def begin():
        for cp in copies():
            cp.start()

    def end():
        cps = copies()
        for cp in cps:
            cp.wait_recv()
        for cp in cps:
            cp.wait_send()

    return begin, end


def exchange_chips(parts, rep):
    n = len(parts)

    def body(*refs):
        ins, rep_ref = refs[:n], refs[n]
        recv, rep_all = refs[n + 1:2 * n + 1], refs[2 * n + 1]
        send_sems, recv_sems, rsend_sems, rrecv_sems, local_sem = refs[2 * n + 2:]
        x, y, c = lax.axis_index("x"), lax.axis_index("y"), lax.axis_index("c")
        me = _dev_index(x, y, c)
        mine = pltpu.make_async_copy(rep_ref, rep_all.at[me], local_sem)
        mine.start()
        begin, end = _chip_exchange_steps(ins, recv, send_sems, recv_sems)
        begin()
        rels = [(rx, ry, rc) for rx in (0, 1) for ry in (0, 1) for rc in (0, 1)][1:]
        peers = [(jnp.where(rx, 1 - x, x), jnp.where(ry, 1 - y, y), jnp.where(rc, 1 - c, c)) for rx, ry, rc in rels]
        rcps = []
        for k, peer in enumerate(peers):
            cp = pltpu.make_async_remote_copy(src_ref=rep_ref, dst_ref=rep_all.at[me], send_sem=rsend_sems.at[k],
                                              recv_sem=rrecv_sems.at[k], device_id=peer, device_id_type=MESH_T)
            cp.start()
            rcps.append(cp)
        for k, peer in enumerate(peers):
            pltpu.make_async_remote_copy(src_ref=rep_ref, dst_ref=rep_all.at[_dev_index(*peer)], send_sem=rsend_sems.at[k],
                                         recv_sem=rrecv_sems.at[k], device_id=peer, device_id_type=MESH_T).wait_recv()
        end()
        for cp in rcps:
            cp.wait_send()
        mine.wait()

    outs = pl.pallas_call(
        body, in_specs=[ANY] * (n + 1), out_specs=[ANY] * (n + 1),
        out_shape=[_sds(a.shape, a.dtype) for a in parts] + [_sds((N_DEV,) + rep.shape, rep.dtype)],
        scratch_shapes=[pltpu.SemaphoreType.DMA((n, 3)), pltpu.SemaphoreType.DMA((n, 3)), pltpu.SemaphoreType.DMA((7,)),
                        pltpu.SemaphoreType.DMA((7,)), pltpu.SemaphoreType.DMA],
        name="exchange_chips")(*parts, rep)
    return outs[:n], outs[n]


def adamw(name, terms, w, m, v, transposed=False):
    r, c = w.shape[::-1] if transposed else w.shape
    rb = r if transposed else min(r, 262144 // c)
    c1 = 1.0 - ADAM_B1 ** ADAM_STEP
    c2 = 1.0 - ADAM_B2 ** ADAM_STEP
    nt = len(terms)

    def body(*refs):
        w_ref, m_ref, v_ref = refs[nt:nt + 3]
        g_ref, d_ref, nm_ref, nv_ref = refs[nt + 3:]
        g = refs[0][...].astype(F32)
        for t_ref in refs[1:nt]:
            g = g + t_ref[...].astype(F32)
        if transposed:
            g = g.T
        nm = ADAM_B1 * m_ref[...] + (1.0 - ADAM_B1) * g
        nv = ADAM_B2 * v_ref[...] + (1.0 - ADAM_B2) * (g * g)
        g_ref[...] = g
        nm_ref[...] = nm
        nv_ref[...] = nv
        d_ref[...] = -ADAM_LR * ((nm / c1) / (jnp.sqrt(nv / c2) + ADAM_EPS) + ADAM_WD * w_ref[...])

    blk = pl.BlockSpec((rb, c), lambda i: (i, 0))
    wblk = pl.BlockSpec((c, r), lambda i: (0, 0)) if transposed else blk
    tspecs = [blk if k is None else pl.BlockSpec((None, rb, c), lambda i, k=k: (k, i, 0)) for _, k in terms]
    return pl.pallas_call(body, grid=(r // rb,), in_specs=tspecs + [wblk] * 3, out_specs=[wblk] * 4,
                          out_shape=[_sds(w.shape)] * 4, name=name,
                          compiler_params=_cp(("parallel",)))(*[t for t, _ in terms], w, m, v)


SMS_ROWS = 16
REP_ROWS = 24
N_BIG = 8
SMALL_SHARDED = (("rwkv_w2", (2, 32, 32)), ("rwkv_a2", (2, 32, 32)), ("rwkv_g2", (2, 64, 32)), ("rwkv_v2", (1, 32, 32)),
                 ("ssd_conv_w", (2, 4, 96)))
REPLICATED = (("lower_bounds", (2, 256)), ("mu_shift", (2, 896)), ("mu_vres", (1, 32)), ("rwkv_w0", (2, 256)),
              ("rwkv_a0", (2, 256)), ("rwkv_k_k", (2, 256)), ("rwkv_k_a", (2, 256)), ("rwkv_r_k", (2, 4, 64)),
              ("rwkv_lnx_w", (2, 256)), ("rwkv_lnx_b", (2, 256)), ("rwkv_v0", (1, 256)), ("ssd_conv_b", (2, 768)),
              ("ssd_dt_bias", (2, 4)), ("ssd_A_log", (2, 4)), ("ssd_D", (2, 4)), ("ssd_norm_w", (2, 256)),
              ("hgrn_norm_w", (2, 256)), ("ln1_w", (2, 1024)), ("ln1_b", (2, 1024)), ("ln2_w", (2, 1024)),
              ("ln2_b", (2, 1024)))


def _flat_rows(parts, rows):
    flat = jnp.concatenate([a.reshape(-1) for a in parts])
    return jnp.concatenate([flat, jnp.zeros((rows * PACK_W - flat.shape[0],), flat.dtype)]).reshape(rows, PACK_W)


def _local_arrays(d):
    arrs = [_w_in_pad(d["w_in"][0], None), _w_in_pad(d["w_in"][1], d["w_in_vres"][0]), d["w_out"][0], d["w_out"][1],
            d["w_up"][0], d["w_up"][1], d["w_down"][0], d["w_down"][1],
            _flat_rows([d[n] for n, _ in SMALL_SHARDED], SMS_ROWS)]
    return arrs, _flat_rows([d[n] for n, _ in REPLICATED], REP_ROWS)


def _unflat(rows2d, table):
    flat, out, o = rows2d.reshape(-1), {}, 0
    for name, shape in table:
        n = 1
        for s in shape:
            n *= s
        out[name] = flat[o:o + n].reshape(shape)
        o += n
    return out


def _from_local_arrays(arrs, rep):
    d = {}
    g0, _ = _w_in_unpad(arrs[0])
    g1, gv = _w_in_unpad(arrs[1])
    d["w_in"], d["w_in_vres"] = jnp.stack([g0, g1]), gv[None]
    d["w_out"] = jnp.stack([arrs[2], arrs[3]])
    d["w_up"] = jnp.stack([arrs[4], arrs[5]])
    d["w_down"] = jnp.stack([arrs[6], arrs[7]])
    d.update(_unflat(arrs[8], SMALL_SHARDED))
    d.update(_unflat(rep, REPLICATED))
    return d


def _small_sharded_full(gs):
    small, flat, o = {}, gs.reshape(N_DEV, -1), 0
    for name, shape in SMALL_SHARDED:
        n = shape[0] * shape[1] * shape[2]
        blk = flat[:, o:o + n].reshape((N_DEV,) + shape)
        small[name] = blk.transpose(1, 2, 0, 3).reshape(shape[0], shape[1], N_DEV * shape[2])
        o += n
    return small


def _owner_blocks(g):
    return g.reshape(N_DEV, g.shape[0] // N_DEV, g.shape[1])


def _small_send_arrays(small_grads):
    sms = []
    for name, shape in SMALL_SHARDED:
        g = small_grads[name].reshape(shape[0], shape[1], N_DEV, shape[2]).transpose(2, 0, 1, 3)
        sms.append(g.reshape(N_DEV, -1))
    sms = jnp.concatenate(sms, axis=1)
    sms = jnp.concatenate([sms, jnp.zeros((N_DEV, SMS_ROWS * PACK_W - sms.shape[1]), F32)], axis=1)
    return sms.reshape(N_DEV, SMS_ROWS, PACK_W), _flat_rows([small_grads[n] for n, _ in REPLICATED], REP_ROWS)


BIG_KEYS = ("w_in", "w_out", "w_up_t", "w_down")


def _weights_of(full):
    return dict(zip(BIG_KEYS, full))


def _local_step(x, tgt, wts, raw, gather=(), pair_sums=None):
    consts = _consts()
    ps = [_layer_params(l, raw, consts) for l in range(DEPTH)]
    x1, sv0 = layer_fwd(0, x, None, wts[0], ps[0], gather, late=bool(gather))
    wts1 = _weights_of([_full_rows(g) for g in sv0["gathered"][3:]]) if gather else wts[1]
    x2, sv1 = layer_fwd(1, x1, sv0["fl"], wts1, ps[1])
    dy, lparts = loss_call(x2, tgt)
    loss = jnp.sum(lparts[::8, 0])
    dx1, dvfirst, g1 = layer_bwd(1, dy, None, sv1, wts1, ps[1])
    big1 = {k: g1[k] for k in BIG_KEYS}
    if pair_sums is None:
        dx0, _, g0 = layer_bwd(0, dx1, dvfirst, sv0, sv0["wts"], ps[0])
        early = None
    else:
        own1, parts1 = pair_sums("1", big1)
        dx0, _, g0 = layer_bwd(0, dx1, dvfirst, sv0, sv0["wts"], ps[0], parts1, lambda gs: pair_sums("0a", gs))
        early = (own1, g0["exchanged"][:4], g0["early_own"], g0["exchanged"][4:])
    big = [{k: g0[k] for k in BIG_KEYS}, big1]
    return loss, dx0, big, _natural_grads(g0, g1), early


WEIGHT_NAMES = ("lower_bounds", "w_in", "w_in_vres", "mu_shift", "mu_vres", "rwkv_w0", "rwkv_w2", "rwkv_a0", "rwkv_a2",
                "rwkv_g2", "rwkv_k_k", "rwkv_k_a", "rwkv_r_k", "rwkv_lnx_w", "rwkv_lnx_b", "rwkv_v0", "rwkv_v2",
                "ssd_conv_w", "ssd_conv_b", "ssd_dt_bias", "ssd_A_log", "ssd_D", "ssd_norm_w", "hgrn_norm_w", "w_out",
                "ln1_w", "ln1_b", "w_up", "w_down", "ln2_w", "ln2_b")


def kernel(x, lower_bounds, w_in, w_in_vres, mu_shift, mu_vres, rwkv_w0, rwkv_w2, rwkv_a0, rwkv_a2, rwkv_g2, rwkv_k_k, rwkv_k_a, rwkv_r_k, rwkv_lnx_w, rwkv_lnx_b, rwkv_v0, rwkv_v2, ssd_conv_w, ssd_conv_b, ssd_dt_bias, ssd_A_log, ssd_D, ssd_norm_w, hgrn_norm_w, w_out, ln1_w, ln1_b, w_up, w_down, ln2_w, ln2_b, loss_target, m_lower_bounds, m_w_in, m_w_in_vres, m_mu_shift, m_mu_vres, m_rwkv_w0, m_rwkv_w2, m_rwkv_a0, m_rwkv_a2, m_rwkv_g2, m_rwkv_k_k, m_rwkv_k_a, m_rwkv_r_k, m_rwkv_lnx_w, m_rwkv_lnx_b, m_rwkv_v0, m_rwkv_v2, m_ssd_conv_w, m_ssd_conv_b, m_ssd_dt_bias, m_ssd_A_log, m_ssd_D, m_ssd_norm_w, m_hgrn_norm_w, m_w_out, m_ln1_w, m_ln1_b, m_w_up, m_w_down, m_ln2_w, m_ln2_b, v_lower_bounds, v_w_in, v_w_in_vres, v_mu_shift, v_mu_vres, v_rwkv_w0, v_rwkv_w2, v_rwkv_a0, v_rwkv_a2, v_rwkv_g2, v_rwkv_k_k, v_rwkv_k_a, v_rwkv_r_k, v_rwkv_lnx_w, v_rwkv_lnx_b, v_rwkv_v0, v_rwkv_v2, v_ssd_conv_w, v_ssd_conv_b, v_ssd_dt_bias, v_ssd_A_log, v_ssd_D, v_ssd_norm_w, v_hgrn_norm_w, v_w_out, v_ln1_w, v_ln1_b, v_w_up, v_w_down, v_ln2_w, v_ln2_b):
    given = dict(locals())
    w = {n: given[n] for n in WEIGHT_NAMES}
    w_arrs, w_rep = _local_arrays(w)
    m_arrs, m_rep = _local_arrays({n: given["m_" + n] for n in WEIGHT_NAMES})
    v_arrs, v_rep = _local_arrays({n: given["v_" + n] for n in WEIGHT_NAMES})
    wire = lambda a: (w_arrs[a].T if a in (4, 5) else w_arrs[a]).astype(BF16)
    gathered0 = all_gather([wire(0), w_arrs[N_BIG]])
    raw = {n: w[n] for n, _ in REPLICATED}
    raw.update(_small_sharded_full(gathered0[1]))
    mx, my, mc = lax.axis_index("x"), lax.axis_index("y"), lax.axis_index("c")
    slots = jnp.stack([_dev_index(cx, cy, mc) for cx, cy in _chips(mx, my)]).astype(jnp.int32)

    def pair_sums(tag, grads, extra=()):
        send = [_owner_blocks(g) for g in grads.values()] + list(extra)
        wire = [BF16] * len(grads) + [F32] * len(extra)
        sib = exchange_siblings(send, f"exchange_siblings{tag}")
        res = [reduce_pair(f"reduce_pair{tag}_{i}", s, slots, sb, dt) for i, (s, sb, dt) in enumerate(zip(send, sib, wire))]
        return [o for o, _ in res], [pt for _, pt in res]

    behind_scan = [wire(a) for a in (2, 4, 6, 1, 3, 5, 7)]
    loss, dx, big, small_grads, (own1, recv1, own0a, recv0a) = _local_step(
        x[0], loss_target[0], [{"w_in": _full_rows(gathered0[0])}, None], raw, behind_scan, pair_sums)
    sms_send, rep = _small_send_arrays(small_grads)
    own0b, parts0b = pair_sums("0b", {"w_in": big[0]["w_in"]}, [sms_send])
    recv0b, rep_all = exchange_chips(parts0b, rep)
    own, recv = [None] * (N_BIG + 1), [None] * (N_BIG + 1)
    for a, o, r in zip((1, 3, 5, 7), own1, recv1):
        own[a], recv[a] = o, r
    for a, o, r in zip((2, 4, 6), own0a, recv0a):
        own[a], recv[a] = o, r
    for a, o, r in zip((0, N_BIG), own0b, recv0b):
        own[a], recv[a] = o, r
    results = [adamw(f"adamw{a}", [(own[a], None), (recv[a], 0), (recv[a], 1), (recv[a], 2)], w_arrs[a], m_arrs[a], v_arrs[a],
                     transposed=a in (4, 5)) for a in range(N_BIG + 1)]
    rep_res = adamw("adamw_rep", [(rep_all, q) for q in range(N_DEV)], w_rep, m_rep, v_rep)
    loss = lax.psum(loss, ("x", "y", "c"))
    outs = [loss, dx[None]]
    for q in range(4):
        d = _from_local_arrays([res[q] for res in results], rep_res[q])
        outs += [d[n] for n in WEIGHT_NAMES]
    return tuple(outs)
```

```python
import functools

import jax
import jax.numpy as jnp
from jax import lax
from jax.experimental import pallas as pl
from jax.experimental.pallas import tpu as pltpu

F32 = jnp.float32
BF16 = jnp.bfloat16
HI = lax.Precision.HIGHEST

N_DEV = 8
SEQ = 2048
D_MODEL = 1024
D_FF = 4096
DG = 256
NH = 4
HD = 64
DEPTH = 2
ALPHA = (2.0 * DEPTH) ** 0.25
LN_EPS = 1e-5
RMS_EPS = 1e-5
GN_EPS = HD * 1e-5
IN_COLS = 3716
SSD_N = 128
SSD_CHUNK = 128
HGRN_CHUNK = 16
DILATED = ((128, 1), (512, 4), (2048, 16))

ADAM_LR, ADAM_B1, ADAM_B2, ADAM_EPS, ADAM_WD, ADAM_STEP = 0.001, 0.9, 0.999, 1e-08, 0.01, 10

PW = 4096
C_R, C_K, C_V = 0, 256, 512
C_AQ, C_AK, C_AV = 768, 1024, 1280
C_Z, C_XBC = 1536, 1792
C_HQ, C_HF, C_HI, C_HG = 2560, 2816, 3072, 3328
C_LORA, C_DT, C_VRES = 3584, 3712, 3840

RB = 256
VMEM_LIMIT = 56 * 1024 * 1024
PACK_W = 1024


def _cp(sem=None):
    return pltpu.CompilerParams(dimension_semantics=sem, vmem_limit_bytes=VMEM_LIMIT)


def _sds(shape, dt=F32):
    return jax.ShapeDtypeStruct(tuple(shape), dt)


def _rows(w, cb=0, rb=RB):
    return pl.BlockSpec((rb, w), lambda i: (i, cb))


def _full(shape):
    n = len(shape)
    return pl.BlockSpec(tuple(shape), lambda *_: (0,) * n)


def _sigmoid(x):
    return 1.0 / (1.0 + jnp.exp(-x))


def _silu(x):
    return x * _sigmoid(x)


def _softplus(x):
    return jnp.maximum(x, 0.0) + jnp.log(1.0 + jnp.exp(jnp.where(x > 0, -x, x)))


MID = lax.Precision.HIGH
NN, TN, NT = (((1,), (0,)), ((), ())), (((0,), (0,)), ((), ())), (((1,), (1,)), ((), ()))


def _dot(a, b):
    return lax.dot_general(a, b, NN, precision=MID, preferred_element_type=F32)


def _dot_tn(a, b):
    return lax.dot_general(a, b, TN, precision=MID, preferred_element_type=F32)


def _dot_nt(a, b):
    return lax.dot_general(a, b, NT, precision=MID, preferred_element_type=F32)


def _dotx(a, b):
    return lax.dot_general(a, b, NN, precision=HI, preferred_element_type=F32)


def _dotx_tn(a, b):
    return lax.dot_general(a, b, TN, precision=HI, preferred_element_type=F32)


def _seg_ones(n, seg):
    i = jnp.arange(n)
    return (i[:, None] // seg == i[None, :] // seg).astype(F32)


def _shift_down(x, s):
    row = lax.broadcasted_iota(jnp.int32, x.shape, 0)
    return jnp.where(row < s, 0.0, pltpu.roll(x, s, 0))


def _shift_up(x, s):
    n = x.shape[0]
    row = lax.broadcasted_iota(jnp.int32, x.shape, 0)
    return jnp.where(row >= n - s, 0.0, pltpu.roll(x, n - s, 0))


@functools.partial(jax.custom_vjp, nondiff_argnums=(1,))
def _tshift(x, s):
    return _shift_down(x, s)


def _tshift_fwd(x, s):
    return _shift_down(x, s), None


def _tshift_bwd(s, _, g):
    return (_shift_up(g, s),)


_tshift.defvjp(_tshift_fwd, _tshift_bwd)


def _map_fwd(name, fn, grid, ins, in_specs, out_shapes, out_specs):
    n_in = len(ins)

    def body(*refs):
        ys = fn(*[r[...] for r in refs[:n_in]])
        for r, y in zip(refs[n_in:], ys):
            r[...] = y

    return pl.pallas_call(body, grid=grid, in_specs=in_specs, out_specs=out_specs, out_shape=out_shapes,
                          name=name, compiler_params=_cp(("parallel",)))(*ins)


def _map_bwd(name, fn, grid, ins, in_specs, cts, ct_specs, want, acc=(), gout=None):
    n_in = len(ins)
    flat_cts = [c for group in cts for c in group]
    flat_specs = [s for group in ct_specs for s in group]
    n_ct = len(flat_cts)
    gout = gout or {}
    out_shapes = [gout[i][0] if i in gout else _sds(ins[i].shape) for i in want]
    out_specs = [gout[i][1] if i in gout else in_specs[i] for i in want]

    def body(*refs):
        xs = [r[...] for r in refs[:n_in]]
        cvals = [r[...] for r in refs[n_in:n_in + n_ct]]
        gouts = refs[n_in + n_ct:]
        cs, p = [], 0
        for group in cts:
            v = cvals[p]
            for q in range(1, len(group)):
                v = v + cvals[p + q]
            cs.append(v)
            p += len(group)

        def f(*wanted):
            full = list(xs)
            for i, w in zip(want, wanted):
                full[i] = w
            return tuple(fn(*full))

        _, vjp = jax.vjp(f, *[xs[i] for i in want])
        gs = vjp(tuple(cs))
        for o, i, g in zip(gouts, want, gs):
            if i in acc:
                @pl.when(pl.program_id(0) == 0)
                def _():
                    o[...] = jnp.zeros_like(o)

                o[...] += g
            else:
                o[...] = g

    sem = ("arbitrary",) if acc else ("parallel",)
    return pl.pallas_call(body, grid=grid, in_specs=list(in_specs) + flat_specs, out_specs=out_specs,
                          out_shape=out_shapes, name=name, compiler_params=_cp(sem))(*ins, *flat_cts)


def _addn(name, *arrs):
    n, c = arrs[0].shape

    def fn(*xs):
        r = xs[0]
        for x in xs[1:]:
            r = r + x
        return (r,)

    return _map_fwd(name, fn, (n // RB,), list(arrs), [_rows(c)] * len(arrs), [_sds((n, c))], [_rows(c)])[0]


MM_TILES = {"k1024": (2048, 512, 1024), "k4096": (1024, 1024, 1024), "wgrad_tall": (2048, 1024, 512),
            "wgrad_wide": (1024, 2048, 512)}


def _mm(name, a, b, mode, tm, tn, tk, add=None, add_scale=1.0, epilogue=None, out_dtype=F32):
    if mode == "nn":
        (m, k), n = a.shape, b.shape[1]
    elif mode == "nt":
        (m, k), n = a.shape, b.shape[0]
    else:
        (k, m), n = a.shape, b.shape[1]
    nk = k // tk
    dn = {"nn": (((1,), (0,)), ((), ())), "nt": (((1,), (1,)), ((), ())), "tn": (((0,), (0,)), ((), ()))}[mode]

    def body(*refs):
        a_ref, b_ref = refs[:2]
        add_ref = refs[2] if add is not None else None
        o_ref = refs[3] if add is not None else refs[2]
        prod = lax.dot_general(a_ref[...].astype(BF16), b_ref[...].astype(BF16), dn, preferred_element_type=F32)

        def finish(r):
            if epilogue == "relu2":
                r = jnp.maximum(r, 0.0)
                r = r * r
            elif epilogue == "relu2_bwd":
                r = r * (2.0 * jnp.sqrt(add_ref[...]))
            elif add is not None:
                r = r + add_scale * add_ref[...]
            o_ref[...] = r.astype(out_dtype)

        if nk == 1:
            finish(prod)
        else:
            acc = refs[-1]
            kk = pl.program_id(2)

            @pl.when(kk == 0)
            def _():
                acc[...] = prod

            @pl.when(kk > 0)
            def _():
                acc[...] += prod

            @pl.when(kk == nk - 1)
            def _():
                finish(acc[...])

    a_spec = pl.BlockSpec((tk, tm), lambda i, j, q: (q, i)) if mode == "tn" else pl.BlockSpec((tm, tk), lambda i, j, q: (i, q))
    b_spec = pl.BlockSpec((tn, tk), lambda i, j, q: (j, q)) if mode == "nt" else pl.BlockSpec((tk, tn), lambda i, j, q: (q, j))
    o_spec = pl.BlockSpec((tm, tn), lambda i, j, q: (i, j))
    ins, specs = [a, b], [a_spec, b_spec]
    if add is not None:
        ins.append(add)
        specs.append(o_spec)
    return pl.pallas_call(body, grid=(m // tm, n // tn, nk), in_specs=specs, out_specs=o_spec,
                          out_shape=_sds((m, n), out_dtype),
                          scratch_shapes=[pltpu.VMEM((tm, tn), F32)] if nk > 1 else [], name=name,
                          compiler_params=_cp(("parallel", "parallel", "arbitrary")))(*ins)


LERP_BLOCKS = (0, 1, 2, 3, 4, 5, C_LORA // 128, C_VRES // 128)


def _lerp_colmap(j):
    r = jnp.where(j < 6, j, jnp.where(j == 6, C_LORA // 128, C_VRES // 128))
    return (0, r)


def _lerp_fn(f, mu):
    return (f + (_tshift(f, 1) - f) * mu,)


def _lerp_specs():
    return [pl.BlockSpec((SEQ, 128), _lerp_colmap), pl.BlockSpec((1, 128), lambda j: (0, j))]


def lerp_fwd(l, proj, mu):
    return _map_fwd(f"lerp_fwd{l}", _lerp_fn, (8,), [proj, mu], _lerp_specs(), [_sds((SEQ, 1024))],
                    [pl.BlockSpec((SEQ, 128), lambda j: (0, j))])[0]


def lerp_bwd(l, proj, mu, dfl):
    n_in = 2

    def body(f_ref, mu_ref, g_ref, df_ref, dmu_ref):
        _, vjp = jax.vjp(_lerp_fn, f_ref[...], mu_ref[...])
        df, dmu = vjp((g_ref[...],))
        df_ref[...] = df
        dmu_ref[...] = dmu

    cspec = pl.BlockSpec((SEQ, 128), lambda j: (0, j))
    return pl.pallas_call(body, grid=(8,), in_specs=_lerp_specs() + [cspec],
                          out_specs=[cspec, pl.BlockSpec((1, 128), lambda j: (0, j))],
                          out_shape=[_sds((SEQ, 1024)), _sds((1, 1024))], name=f"lerp_bwd{l}",
                          compiler_params=_cp(("parallel",)))(proj, mu, dfl)


def _conv_fn(x, w, b):
    y = x * w[3:4, :] + _tshift(x, 1) * w[2:3, :] + _tshift(x, 2) * w[1:2, :] + _tshift(x, 3) * w[0:1, :] + b
    return (_silu(y),)


def _conv_specs():
    return [pl.BlockSpec((SEQ, 128), lambda j: (0, C_XBC // 128 + j)), pl.BlockSpec((4, 128), lambda j: (0, j)),
            pl.BlockSpec((1, 128), lambda j: (0, j))]


def conv_fwd(l, proj, w, b):
    return _map_fwd(f"conv_fwd{l}", _conv_fn, (6,), [proj, w, b], _conv_specs(), [_sds((SEQ, 768))],
                    [pl.BlockSpec((SEQ, 128), lambda j: (0, j))])[0]


def conv_bwd(l, proj, w, b, dxc):
    def body(x_ref, w_ref, b_ref, g_ref, dx_ref, dw_ref, db_ref):
        _, vjp = jax.vjp(_conv_fn, x_ref[...], w_ref[...], b_ref[...])
        dx, dw, db = vjp((g_ref[...],))
        dx_ref[...] = dx
        dw_ref[...] = dw
        db_ref[...] = db

    cspec = pl.BlockSpec((SEQ, 128), lambda j: (0, j))
    return pl.pallas_call(body, grid=(6,), in_specs=_conv_specs() + [cspec],
                          out_specs=[cspec, pl.BlockSpec((4, 128), lambda j: (0, j)), pl.BlockSpec((1, 128), lambda j: (0, j))],
                          out_shape=[_sds((SEQ, 768)), _sds((4, 768)), _sds((1, 768))], name=f"conv_bwd{l}",
                          compiler_params=_cp(("parallel",)))(proj, w, b, dxc)


def _rwkv_pre_fn(has_vres):
    def fn(fk, fv, flora, *rest):
        if has_vres:
            fvres, vfirst, w0, w2p, a0, a2p, g2p, k_k, k_a, v0, v2p, seg = rest
        else:
            w0, w2p, a0, a2p, g2p, k_k, k_a, seg = rest
        w_log = -_softplus(-(w0 + _dot(jnp.tanh(flora), w2p))) - 0.5
        w = jnp.exp(-jnp.exp(w_log))
        a = _sigmoid(a0 + _dot(flora, a2p))
        g = _dot(_sigmoid(flora), g2p)
        if has_vres:
            v2 = fv + (vfirst - fv) * _sigmoid(v0 + _dot(fvres, v2p))
        else:
            v2 = fv * 1.0
        kk = fk * k_k
        kk = kk / jnp.maximum(jnp.sqrt(_dot(kk * kk, seg)), 1e-12)
        k2 = fk * (1.0 + (a - 1.0) * k_a)
        return w, k2, v2, -kk, kk * a, g

    return fn


def _rwkv_pre_args(fl, vfirst, p, has_vres):
    ins = [fl, fl, fl]
    specs = [_rows(256, 1), _rows(256, 2), _rows(128, 6)]
    if has_vres:
        ins += [fl, vfirst]
        specs += [_rows(128, 7), _rows(256, 2)]
    names = ["w0", "w2p", "a0", "a2p", "g2p", "k_k", "k_a"] + (["v0", "v2p"] if has_vres else []) + ["seg64"]
    for nme in names:
        ins.append(p[nme])
        specs.append(_full(p[nme].shape))
    return ins, specs, names


def rwkv_pre_fwd(l, fl, vfirst, p):
    has_vres = l > 0
    ins, specs, _ = _rwkv_pre_args(fl, vfirst, p, has_vres)
    return _map_fwd(f"rwkv_pre_fwd{l}", _rwkv_pre_fn(has_vres), (SEQ // RB,), ins, specs,
                    [_sds((SEQ, DG))] * 6, [_rows(DG)] * 6)


def rwkv_pre_bwd(l, fl, vfirst, p, cts):
    has_vres = l > 0
    ins, specs, names = _rwkv_pre_args(fl, vfirst, p, has_vres)
    n_row = 5 if has_vres else 3
    want = list(range(n_row)) + [n_row + i for i, nme in enumerate(names) if nme != "seg64"]
    acc = tuple(w for w in want if w >= n_row)
    ct_specs = [[_rows(DG)] * len(g) for g in cts]
    gout = {0: (_sds((SEQ, DG)), _rows(DG)), 1: (_sds((SEQ, DG)), _rows(DG)), 2: (_sds((SEQ, 128)), _rows(128))}
    if has_vres:
        gout[3] = (_sds((SEQ, 128)), _rows(128))
        gout[4] = (_sds((SEQ, DG)), _rows(DG))
    gs = _map_bwd(f"rwkv_pre_bwd{l}", _rwkv_pre_fn(has_vres), (SEQ // RB,), ins, specs, cts, ct_specs, want, acc, gout)
    keys = ["fk", "fv", "flora"] + (["fvres", "vfirst"] if has_vres else []) + [nme for nme in names if nme != "seg64"]
    return dict(zip(keys, gs))


def _rwkv_post_fn(y, fr, k2, v2, g, lnx_w, lnx_b, r_k, seg):
    mu = _dot(y, seg) * (1.0 / HD)
    d = y - mu
    var = _dot(d * d, seg) * (1.0 / HD)
    yn = d * lax.rsqrt(var + GN_EPS) * lnx_w + lnx_b
    bonus = _dot(fr * k2 * r_k, seg) * v2
    return ((yn + bonus) * g,)


def _rwkv_post_args(y, fl, k2, v2, g, p):
    ins = [y, fl, k2, v2, g, p["lnx_w"], p["lnx_b"], p["r_k"], p["seg64"]]
    specs = [_rows(DG), _rows(DG, 0), _rows(DG), _rows(DG), _rows(DG)] + [_full(x.shape) for x in ins[5:]]
    return ins, specs


def rwkv_post_fwd(l, y, fl, k2, v2, g, p):
    ins, specs = _rwkv_post_args(y, fl, k2, v2, g, p)
    return _map_fwd(f"rwkv_post_fwd{l}", _rwkv_post_fn, (SEQ // RB,), ins, specs, [_sds((SEQ, DG))], [_rows(DG)])[0]


def rwkv_post_bwd(l, y, fl, k2, v2, g, p, dya):
    ins, specs = _rwkv_post_args(y, fl, k2, v2, g, p)
    gs = _map_bwd(f"rwkv_post_bwd{l}", _rwkv_post_fn, (SEQ // RB,), ins, specs, [[dya]], [[_rows(DG)]],
                  want=[0, 1, 2, 3, 4, 5, 6, 7], acc=(5, 6, 7), gout={1: (_sds((SEQ, DG)), _rows(DG))})
    return dict(zip(["y", "fr", "k2", "v2", "g", "lnx_w", "lnx_b", "r_k"], gs))


SCAN_TB = 64


def _coltile8(rows8, dmask, ones_stack, parts):
    pieces, rest = [], rows8
    for q in range(parts):
        piece = rest.astype(BF16).astype(F32)
        if q < parts - 1:
            rest = rest - piece
        pieces.append((piece[:, None, :] * dmask[None]).reshape(8 * HD, DG).astype(BF16))
    x = pieces[0] if parts == 1 else jnp.concatenate(pieces, axis=1)
    return jnp.dot(x, ones_stack, preferred_element_type=F32).reshape(8, HD, DG)


def _coltiles_bf16(rows_list, dmask, ones_bf16):
    x = jnp.concatenate([(r8[:, None, :] * dmask[None]).reshape(8 * HD, DG).astype(BF16) for r8 in rows_list], axis=0)
    t = jnp.dot(x, ones_bf16, preferred_element_type=F32)
    return [t[q * 8 * HD:(q + 1) * 8 * HD].reshape(8, HD, DG) for q in range(len(rows_list))]


def _segrows8(x8, dmask, ones_bf16):
    t = jnp.dot(x8.reshape(8 * HD, DG).astype(BF16), ones_bf16, preferred_element_type=F32).reshape(8, HD, DG)
    return jnp.sum(t * dmask[None], axis=1)


def rwkv_scan_fwd(l, fl, w, k2, v2, c, b, p, gather=()):
    nblk = SEQ // SCAN_TB
    ng = len(gather)

    def body(*refs):
        r_ref, w_ref, k_ref, v_ref, c_ref, b_ref, ones_ref, dm_ref = refs[:8]
        y_ref, st_ref = refs[8 + ng:10 + ng]
        s_sc = refs[10 + 2 * ng]
        if ng:
            begin, end = _gather_steps(refs[8:8 + ng], refs[10 + ng:10 + 2 * ng], *refs[11 + 2 * ng:])

            @pl.when(pl.program_id(0) == 0)
            def _():
                begin()

        @pl.when(pl.program_id(0) == 0)
        def _():
            s_sc[...] = jnp.zeros_like(s_sc)

        ones3, ones = ones_ref[...], ones_ref[0:DG, :]
        dmask = dm_ref[...]

        def group(gi, carry):
            t0 = pl.multiple_of(gi * 8, 8)
            sl = pl.ds(t0, 8)
            v8 = v_ref[sl, :]
            wt = _coltile8(w_ref[sl, :], dmask, ones3, 3)
            ct, bt, kt, rt = _coltiles_bf16([c_ref[sl, :], b_ref[sl, :], k_ref[sl, :], r_ref[sl, :]], dmask, ones)
            t = s_sc[...]
            for j in range(8):
                sa = jnp.sum(t * ct[j], axis=0, keepdims=True)
                t = t * wt[j] + bt[j] * sa + kt[j] * v8[j:j + 1, :]
                st_ref[t0 + j] = t
            s_sc[...] = t
            y_ref[sl, :] = jnp.sum(st_ref[sl] * rt, axis=1)
            return carry

        lax.fori_loop(0, SCAN_TB // 8, group, 0)

        if ng:
            @pl.when(pl.program_id(0) == nblk - 1)
            def _():
                end()

    row = pl.BlockSpec((SCAN_TB, DG), lambda i: (i, 0))
    ins = [fl, w, k2, v2, c, b, p["seg64x3_bf16"], p["dmask"]] + list(gather)
    specs = [row] * 6 + [_full((3 * DG, DG)), _full((HD, DG))] + [ANY] * ng
    outs = pl.pallas_call(body, grid=(nblk,), in_specs=specs,
                          out_specs=[row, pl.BlockSpec((SCAN_TB, HD, DG), lambda i: (i, 0, 0))] + [ANY] * ng,
                          out_shape=[_sds((SEQ, DG)), _sds((SEQ, HD, DG))] + _gather_shapes(gather),
                          scratch_shapes=[pltpu.VMEM((HD, DG), F32)] + (_gather_sems(ng) if ng else []),
                          name=f"rwkv_scan_fwd{l}", compiler_params=_cp(("arbitrary",)))(*ins)
    return outs[0], outs[1], list(outs[2:])


def rwkv_scan_bwd(l, fl, w, k2, v2, c, b, states, dy, p, exchange=()):
    nblk = SEQ // SCAN_TB
    nx = len(exchange)

    def body(*refs):
        r_ref, w_ref, k_ref, v_ref, c_ref, b_ref, dy_ref, st_ref, sp_ref, ones_ref, dm_ref = refs[:11]
        dr_ref, dw_ref, dk_ref, dv_ref, dc_ref, db_ref = refs[11 + nx:17 + nx]
        g_sc, prev_sc, d8_sc, dsa_sc = refs[17 + 2 * nx:21 + 2 * nx]
        i = pl.program_id(0)
        if nx:
            begin, end = _chip_exchange_steps(refs[11:11 + nx], refs[17 + nx:17 + 2 * nx], *refs[21 + 2 * nx:])

            @pl.when(i == 0)
            def _():
                begin()

        @pl.when(i == 0)
        def _():
            g_sc[...] = jnp.zeros_like(g_sc)

        ones3, ones = ones_ref[...], ones_ref[0:DG, :]
        dmask = dm_ref[...]
        first_block = i == nblk - 1

        def group(gr, carry):
            gi = SCAN_TB // 8 - 1 - gr
            t0 = pl.multiple_of(gi * 8, 8)
            sl = pl.ds(t0, 8)
            v8, dy8 = v_ref[sl, :], dy_ref[sl, :]
            t8 = st_ref[sl]
            @pl.when(gi > 0)
            def _():
                prev_sc[0] = st_ref[t0 - 1]

            @pl.when(gi == 0)
            def _():
                prev_sc[0] = jnp.where(first_block, 0.0, sp_ref[0])

            for j in range(1, 8):
                prev_sc[j] = t8[j - 1]
            tp8 = prev_sc[...]
            wt = _coltile8(w_ref[sl, :], dmask, ones3, 3)
            ct, bt, kt, rt = _coltiles_bf16([c_ref[sl, :], b_ref[sl, :], k_ref[sl, :], r_ref[sl, :]], dmask, ones)
            sa8 = jnp.sum(tp8 * ct, axis=1)
            g = g_sc[...]
            for j in range(7, -1, -1):
                g = g + rt[j] * dy8[j:j + 1, :]
                d8_sc[j] = g
                dsa = jnp.sum(g * bt[j], axis=0, keepdims=True)
                dsa_sc[j:j + 1, :] = dsa
                g = g * wt[j] + ct[j] * dsa
            g_sc[...] = g
            d8 = d8_sc[...]
            dsa8 = dsa_sc[...]
            dv_ref[sl, :] = jnp.sum(d8 * kt, axis=1)
            dr_ref[sl, :] = _segrows8(t8 * dy8[:, None, :], dmask, ones)
            dk_ref[sl, :] = _segrows8(d8 * v8[:, None, :], dmask, ones)
            dw_ref[sl, :] = _segrows8(tp8 * d8, dmask, ones)
            db_ref[sl, :] = _segrows8(d8 * sa8[:, None, :], dmask, ones)
            dc_ref[sl, :] = _segrows8(tp8 * dsa8[:, None, :], dmask, ones)
            return carry

        lax.fori_loop(0, SCAN_TB // 8, group, 0)

        if nx:
            @pl.when(i == nblk - 1)
            def _():
                end()

    row = pl.BlockSpec((SCAN_TB, DG), lambda i: (nblk - 1 - i, 0))
    st_spec = pl.BlockSpec((SCAN_TB, HD, DG), lambda i: (nblk - 1 - i, 0, 0))
    sp_spec = pl.BlockSpec((1, HD, DG), lambda i: (jnp.maximum((nblk - 1 - i) * SCAN_TB - 1, 0), 0, 0))
    ins = [fl, w, k2, v2, c, b, dy, states, states, p["seg64x3_bf16"], p["dmask"]] + list(exchange)
    specs = [row] * 7 + [st_spec, sp_spec, _full((3 * DG, DG)), _full((HD, DG))] + [ANY] * nx
    tile8 = pltpu.VMEM((8, HD, DG), F32)
    sems = [pltpu.SemaphoreType.DMA((nx, 3)), pltpu.SemaphoreType.DMA((nx, 3))] if nx else []
    outs = pl.pallas_call(body, grid=(nblk,), in_specs=specs, out_specs=[row] * 6 + [ANY] * nx,
                          out_shape=[_sds((SEQ, DG))] * 6 + [_sds(a.shape, a.dtype) for a in exchange],
                          scratch_shapes=[pltpu.VMEM((HD, DG), F32), tile8, tile8, pltpu.VMEM((8, DG), F32)] + sems,
                          name=f"rwkv_scan_bwd{l}", compiler_params=_cp(("arbitrary",)))(*ins)
    return outs[:6], list(outs[6:])


HG_ROWS = 128


HG_NC = HG_ROWS // HGRN_CHUNK


def _hgrn_block_fn(layer):
    def fn(hq, hf, hi, hg, sprev, lb0, lb1, norm_w, seg, bd, tri_bd, ones_bd, first_row, causal):
        e0 = jnp.exp(lb0 - jnp.maximum(lb0, lb1))
        e1 = jnp.exp(lb1 - jnp.maximum(lb0, lb1))
        sm0, sm1 = e0 / (e0 + e1), e1 / (e0 + e1)
        lb = (sm0 - sm0) if layer == 0 else ((sm0 + sm1) - sm0)
        forget = lb + (1.0 - lb) * _sigmoid(hf)
        logf = jnp.log(forget)
        kk = 1.0 - forget
        q = _silu(hq)
        c, nc = HGRN_CHUNK, HG_NC
        b = _dotx(tri_bd, logf)
        bl = _dotx(ones_bd, logf)
        split = lambda t: t.reshape(nc, c, DG)
        b4 = split(b)
        diff = (b4[:, :, None, :] - b4[:, None, :, :]).reshape(nc * c * c, DG)
        dec = jnp.exp(jnp.where(causal > 0.5, diff, -1e30))
        qrep = jnp.broadcast_to(split(q)[:, :, None, :], (nc, c, c, DG)).reshape(nc * c * c, DG)
        ktil = jnp.broadcast_to(split(kk)[:, None, :, :], (nc, c, c, DG)).reshape(nc * c * c, DG)
        vtil = jnp.broadcast_to(split(hi)[:, None, :, :], (nc, c, c, DG)).reshape(nc * c * c, DG)
        att = _dot(qrep * ktil * dec, seg)
        o_intra = jnp.sum((att * vtil).reshape(nc * c, c, DG), axis=1)
        kd4 = split(kk * jnp.exp(bl - b))
        qe4 = split(q * jnp.exp(b))
        v4 = split(hi)
        tot = jnp.exp(_dotx(first_row, bl))
        s, o_inter = sprev, []
        for ci in range(nc):
            o_inter.append(_dot_nt(qe4[ci], s))
            s = s * tot[ci:ci + 1, :] + _dot_tn(v4[ci], kd4[ci]) * bd
        o = o_intra + jnp.concatenate(o_inter, axis=0)
        ms = _dot(o * o, seg) * (1.0 / HD)
        y = o * lax.rsqrt(ms + RMS_EPS) * norm_w * _silu(hg)
        return y, s

    return fn


def _hgrn_consts(p):
    return [p["seg64"], p["seg64"], p["tri_bd128"], p["ones_bd128"], p["first_row"], p["causal_blk"]]


def hgrn_fwd(l, proj, p):
    fn = _hgrn_block_fn(l)

    def body(hq_ref, hf_ref, hi_ref, hg_ref, *rest):
        const_refs, (y_ref, st_ref, s_sc) = rest[:-3], rest[-3:]

        @pl.when(pl.program_id(0) == 0)
        def _():
            s_sc[...] = jnp.zeros_like(s_sc)

        sprev = s_sc[...]
        st_ref[0] = sprev
        y, snext = fn(hq_ref[...], hf_ref[...], hi_ref[...], hg_ref[...], sprev, *[r[...] for r in const_refs])
        y_ref[...] = y
        s_sc[...] = snext

    rows = lambda cb: pl.BlockSpec((HG_ROWS, DG), lambda i: (i, cb))
    ins = [proj, proj, proj, proj, p["lb0"], p["lb1"], p["hgrn_norm_w"]] + _hgrn_consts(p)
    specs = [rows(C_HQ // DG), rows(C_HF // DG), rows(C_HI // DG), rows(C_HG // DG)] + [_full(x.shape) for x in ins[4:]]
    return pl.pallas_call(body, grid=(SEQ // HG_ROWS,), in_specs=specs,
                          out_specs=[rows(0), pl.BlockSpec((1, DG, DG), lambda i: (i, 0, 0))],
                          out_shape=[_sds((SEQ, DG)), _sds((SEQ // HG_ROWS, DG, DG))],
                          scratch_shapes=[pltpu.VMEM((DG, DG), F32)], name=f"hgrn_fwd{l}",
                          compiler_params=_cp(("arbitrary",)))(*ins)


def hgrn_bwd(l, proj, states, dy, p):
    fn = _hgrn_block_fn(l)
    nblk = SEQ // HG_ROWS
    n_const = len(_hgrn_consts(p))

    def body(hq_ref, hf_ref, hi_ref, hg_ref, st_ref, dy_ref, lb0_ref, lb1_ref, nw_ref, *rest):
        const_refs, (dp_ref, dlb0_ref, dlb1_ref, dnw_ref, ds_sc) = rest[:n_const], rest[n_const:]

        @pl.when(pl.program_id(0) == 0)
        def _():
            ds_sc[...] = jnp.zeros_like(ds_sc)
            dlb0_ref[...] = jnp.zeros_like(dlb0_ref)
            dlb1_ref[...] = jnp.zeros_like(dlb1_ref)
            dnw_ref[...] = jnp.zeros_like(dnw_ref)

        consts = [r[...] for r in const_refs]
        f = lambda hq, hf, hi, hg, sp, b0, b1, nw: fn(hq, hf, hi, hg, sp, b0, b1, nw, *consts)
        _, vjp = jax.vjp(f, hq_ref[...], hf_ref[...], hi_ref[...], hg_ref[...], st_ref[0], lb0_ref[...], lb1_ref[...],
                         nw_ref[...])
        dhq, dhf, dhi, dhg, dsp, dlb0, dlb1, dnw = vjp((dy_ref[...], ds_sc[...]))
        dp_ref[:, 0:DG] = dhq
        dp_ref[:, DG:2 * DG] = dhf
        dp_ref[:, 2 * DG:3 * DG] = dhi
        dp_ref[:, 3 * DG:4 * DG] = dhg
        ds_sc[...] = dsp
        dlb0_ref[...] += dlb0
        dlb1_ref[...] += dlb1
        dnw_ref[...] += dnw

    rows = lambda cb: pl.BlockSpec((HG_ROWS, DG), lambda i: (nblk - 1 - i, cb))
    ins = [proj, proj, proj, proj, states, dy, p["lb0"], p["lb1"], p["hgrn_norm_w"]] + _hgrn_consts(p)
    specs = [rows(C_HQ // DG), rows(C_HF // DG), rows(C_HI // DG), rows(C_HG // DG),
             pl.BlockSpec((1, DG, DG), lambda i: (nblk - 1 - i, 0, 0)), rows(0)] + [_full(x.shape) for x in ins[6:]]
    return pl.pallas_call(body, grid=(nblk,), in_specs=specs,
                          out_specs=[pl.BlockSpec((HG_ROWS, 4 * DG), lambda i: (nblk - 1 - i, 0)), _full((1, DG)),
                                     _full((1, DG)), _full((1, DG))],
                          out_shape=[_sds((SEQ, 4 * DG)), _sds((1, DG)), _sds((1, DG)), _sds((1, DG))],
                          scratch_shapes=[pltpu.VMEM((DG, DG), F32)], name=f"hgrn_bwd{l}",
                          compiler_params=_cp(("arbitrary",)))(*ins)


def _ssd_chunk_fn(z, xs, bm, cm, dtr, sprev, dt_bias, a_log, d_par, norm_w, e128, tri, trit, seg128, ones128):
    lc = SSD_CHUNK
    dt = _softplus(dtr + dt_bias)
    a = -jnp.exp(a_log)
    da = dt * a * (lax.broadcasted_iota(jnp.int32, (1, 128), 1) < NH).astype(F32)
    cs = _dotx(tri, da)
    cst = _dotx_tn(da, trit)
    cs_b = _dotx(cs, e128)
    dt_b = _dotx(dt, e128)
    csl_b = _dotx(jnp.sum(da, axis=0, keepdims=True), e128)
    xdt = xs * dt_b
    lane = lax.broadcasted_iota(jnp.int32, (1, DG), 1)
    rowi = lax.broadcasted_iota(jnp.int32, (lc, lc), 0)
    coli = lax.broadcasted_iota(jnp.int32, (lc, lc), 1)
    y = jnp.zeros((lc, DG), F32)
    snew = jnp.zeros((DG, SSD_N), F32)
    d_b = jnp.zeros((1, DG), F32)
    wdec = xdt * jnp.exp(csl_b - cs_b)
    for g in range(2):
        bg = bm[:, g * SSD_N:(g + 1) * SSD_N]
        cg = cm[:, g * SSD_N:(g + 1) * SSD_N]
        gmat = _dot_nt(cg, bg)
        gmask = ((lane // 128) == g).astype(F32)
        snew = snew + _dot_tn(wdec * gmask, bg)
        y = y + _dot_nt(cg, sprev) * gmask * jnp.exp(cs_b)
        for hh in range(2):
            h = 2 * g + hh
            seg = jnp.where(rowi >= coli, cs[:, h:h + 1] - cst[h:h + 1, :], -1e30)
            hmask = ((lane // HD) == h).astype(F32)
            y = y + _dot(gmat * jnp.exp(seg), xdt * hmask)
            d_b = d_b + d_par[:, h:h + 1] * hmask
    cd = jnp.exp(_dotx_tn(_dotx(da, e128), ones128))
    snext = sprev * cd + snew
    y = y + xs * d_b
    y = y * _silu(z)
    ms = _dot(y * y, seg128) * (1.0 / 128.0)
    return y * lax.rsqrt(ms + RMS_EPS) * norm_w, snext


def ssd_fwd(l, proj, xc, p):
    nc = SEQ // SSD_CHUNK

    def body(z_ref, xs_ref, b_ref, c_ref, dt_ref, dtb_ref, al_ref, d_ref, nw_ref, e_ref, tri_ref, trit_ref, sg_ref,
             on_ref, y_ref, st_ref, s_sc):
        @pl.when(pl.program_id(0) == 0)
        def _():
            s_sc[...] = jnp.zeros_like(s_sc)

        sprev = s_sc[...]
        st_ref[0] = sprev
        y, snext = _ssd_chunk_fn(z_ref[...], xs_ref[...], b_ref[...], c_ref[...], dt_ref[...], sprev, dtb_ref[...],
                                 al_ref[...], d_ref[...], nw_ref[...], e_ref[...], tri_ref[...], trit_ref[...],
                                 sg_ref[...], on_ref[...])
        y_ref[...] = y
        s_sc[...] = snext

    rw = lambda w, cb: pl.BlockSpec((SSD_CHUNK, w), lambda i: (i, cb))
    ins = [proj, xc, xc, xc, proj, p["dt_bias"], p["a_log"], p["ssd_d"], p["ssd_norm_w"], p["e128"], p["tri128"],
           p["tri128t"], p["seg128"], p["ones128"]]
    specs = [rw(DG, C_Z // DG), rw(DG, 0), rw(DG, 1), rw(DG, 2), rw(128, C_DT // 128)] + [_full(x.shape) for x in ins[5:]]
    return pl.pallas_call(body, grid=(nc,), in_specs=specs,
                          out_specs=[rw(DG, 0), pl.BlockSpec((1, DG, SSD_N), lambda i: (i, 0, 0))],
                          out_shape=[_sds((SEQ, DG)), _sds((nc, DG, SSD_N))],
                          scratch_shapes=[pltpu.VMEM((DG, SSD_N), F32)], name=f"ssd_fwd{l}",
                          compiler_params=_cp(("arbitrary",)))(*ins)


def ssd_bwd(l, proj, xc, states, dy, p):
    nc = SEQ // SSD_CHUNK

    def body(z_ref, xs_ref, b_ref, c_ref, dt_ref, st_ref, dy_ref, dtb_ref, al_ref, d_ref, nw_ref, e_ref, tri_ref,
             trit_ref, sg_ref, on_ref, dz_ref, dxc_ref, ddt_ref, ddtb_ref, dal_ref, dd_ref, dnw_ref, ds_sc):
        @pl.when(pl.program_id(0) == 0)
        def _():
            ds_sc[...] = jnp.zeros_like(ds_sc)
            ddtb_ref[...] = jnp.zeros_like(ddtb_ref)
            dal_ref[...] = jnp.zeros_like(dal_ref)
            dd_ref[...] = jnp.zeros_like(dd_ref)
            dnw_ref[...] = jnp.zeros_like(dnw_ref)

        consts = (e_ref[...], tri_ref[...], trit_ref[...], sg_ref[...], on_ref[...])
        f = lambda *a: _ssd_chunk_fn(*a, *consts)
        _, vjp = jax.vjp(f, z_ref[...], xs_ref[...], b_ref[...], c_ref[...], dt_ref[...], st_ref[0], dtb_ref[...],
                         al_ref[...], d_ref[...], nw_ref[...])
        dz, dxs, db, dc, ddt, dsp, ddtb, dal, dd, dnw = vjp((dy_ref[...], ds_sc[...]))
        dz_ref[...] = dz
        dxc_ref[:, 0:DG] = dxs
        dxc_ref[:, DG:2 * DG] = db
        dxc_ref[:, 2 * DG:3 * DG] = dc
        ddt_ref[...] = ddt
        ds_sc[...] = dsp
        ddtb_ref[...] += ddtb
        dal_ref[...] += dal
        dd_ref[...] += dd
        dnw_ref[...] += dnw

    rw = lambda w, cb: pl.BlockSpec((SSD_CHUNK, w), lambda i: (nc - 1 - i, cb))
    ins = [proj, xc, xc, xc, proj, states, dy, p["dt_bias"], p["a_log"], p["ssd_d"], p["ssd_norm_w"], p["e128"],
           p["tri128"], p["tri128t"], p["seg128"], p["ones128"]]
    specs = [rw(DG, C_Z // DG), rw(DG, 0), rw(DG, 1), rw(DG, 2), rw(128, C_DT // 128),
             pl.BlockSpec((1, DG, SSD_N), lambda i: (nc - 1 - i, 0, 0)), rw(DG, 0)] + [_full(x.shape) for x in ins[7:]]
    return pl.pallas_call(body, grid=(nc,), in_specs=specs,
                          out_specs=[rw(DG, 0), rw(3 * DG, 0), rw(128, 0), _full((1, 128)), _full((1, 128)), _full((1, 128)),
                                     _full((1, DG))],
                          out_shape=[_sds((SEQ, DG)), _sds((SEQ, 3 * DG)), _sds((SEQ, 128)), _sds((1, 128)), _sds((1, 128)),
                                     _sds((1, 128)), _sds((1, DG))],
                          scratch_shapes=[pltpu.VMEM((DG, SSD_N), F32)], name=f"ssd_bwd{l}",
                          compiler_params=_cp(("arbitrary",)))(*ins)


ATT_BLK = 128


def _att_scores(qn, kc, kp, h, dil, has_prev):
    i = lax.broadcasted_iota(jnp.int32, (ATT_BLK, ATT_BLK), 0)
    j = lax.broadcasted_iota(jnp.int32, (ATT_BLK, ATT_BLK), 1)
    slope = 2.0 ** (-8.0 * (h + 1) / NH)
    scale = HD ** -0.5
    s_c = _dot_nt(qn, kc) * scale - slope * ((i - j) * dil).astype(F32)
    s_p = _dot_nt(qn, kp) * scale - slope * ((ATT_BLK + i - j) * dil).astype(F32)
    m_c = j <= i
    m_p = jnp.logical_and(j >= i, has_prev)
    return jnp.where(m_c, s_c, -1e30), jnp.where(m_p, s_p, -1e30), m_c, m_p


def _sub_spec(ln, width, col):
    return pl.BlockSpec((ln, DG), lambda z: (0, z * (width // DG) + col // DG))


QKV_W = 3 * DG


def attn_branch_fwd(l, bi, qkv, dil):
    ln = SEQ // dil
    nb = ln // ATT_BLK

    def body(q_ref, k_ref, v_ref, o_ref, l_ref):
        def blk(n, carry):
            r0 = pl.multiple_of(n * ATT_BLK, ATT_BLK)
            rp = pl.multiple_of(jnp.maximum(n - 1, 0) * ATT_BLK, ATT_BLK)
            cur, prv = pl.ds(r0, ATT_BLK), pl.ds(rp, ATT_BLK)
            for h in range(NH):
                hs = slice(h * HD, (h + 1) * HD)
                qn, kc, vc, kp, vp = q_ref[cur, hs], k_ref[cur, hs], v_ref[cur, hs], k_ref[prv, hs], v_ref[prv, hs]
                s_c, s_p, m_c, m_p = _att_scores(qn, kc, kp, h, dil, n > 0)
                m = jnp.maximum(jnp.max(s_c, axis=1, keepdims=True), jnp.max(s_p, axis=1, keepdims=True))
                p_c = jnp.where(m_c, jnp.exp(s_c - m), 0.0)
                p_p = jnp.where(m_p, jnp.exp(s_p - m), 0.0)
                den = jnp.sum(p_c, axis=1, keepdims=True) + jnp.sum(p_p, axis=1, keepdims=True)
                o_ref[cur, hs] = (_dot(p_c, vc) + _dot(p_p, vp)) / den
                l_ref[cur, hs] = jnp.broadcast_to(m + jnp.log(den), (ATT_BLK, HD))
            return carry

        lax.fori_loop(0, nb, blk, 0)

    pv = qkv.reshape(ln, dil * QKV_W)
    out = pl.BlockSpec((ln, DG), lambda z: (0, z))
    o, lse = pl.pallas_call(body, grid=(dil,), in_specs=[_sub_spec(ln, QKV_W, 0), _sub_spec(ln, QKV_W, DG), _sub_spec(ln, QKV_W, 2 * DG)],
                            out_specs=[out, out], out_shape=[_sds((ln, dil * DG))] * 2, name=f"attn_fwd{l}_{bi}",
                            compiler_params=_cp(("parallel",)))(pv, pv, pv)
    return o.reshape(SEQ, DG), lse.reshape(SEQ, DG)


def attn_branch_bwd(l, bi, qkv, dil, dyb, lse_all, delta):
    ln = SEQ // dil
    nb = ln // ATT_BLK
    scale = HD ** -0.5

    def body(q_ref, k_ref, v_ref, do_ref, l_ref, dl_ref, dq_ref, dk_ref, dv_ref):
        dk_ref[...] = jnp.zeros_like(dk_ref)
        dv_ref[...] = jnp.zeros_like(dv_ref)

        def blk(n, carry):
            r0 = pl.multiple_of(n * ATT_BLK, ATT_BLK)
            rp = pl.multiple_of(jnp.maximum(n - 1, 0) * ATT_BLK, ATT_BLK)
            cur, prv = pl.ds(r0, ATT_BLK), pl.ds(rp, ATT_BLK)
            for h in range(NH):
                hs = slice(h * HD, (h + 1) * HD)
                qn, don = q_ref[cur, hs], do_ref[cur, hs]
                lse, dlt = l_ref[cur, h * HD:h * HD + 1], dl_ref[cur, h * HD:h * HD + 1]
                kc, vc, kp, vp = k_ref[cur, hs], v_ref[cur, hs], k_ref[prv, hs], v_ref[prv, hs]
                s_c, s_p, m_c, m_p = _att_scores(qn, kc, kp, h, dil, n > 0)
                p_c = jnp.where(m_c, jnp.exp(s_c - lse), 0.0)
                p_p = jnp.where(m_p, jnp.exp(s_p - lse), 0.0)
                ds_c = p_c * (_dot_nt(don, vc) - dlt)
                ds_p = p_p * (_dot_nt(don, vp) - dlt)
                dq_ref[cur, hs] = (_dot(ds_c, kc) + _dot(ds_p, kp)) * scale
                dv_ref[prv, hs] += _dot_tn(p_p, don)
                dk_ref[prv, hs] += _dot_tn(ds_p, qn) * scale
                dv_ref[cur, hs] += _dot_tn(p_c, don)
                dk_ref[cur, hs] += _dot_tn(ds_c, qn) * scale
            return carry

        lax.fori_loop(0, nb, blk, 0)

    pv = qkv.reshape(ln, dil * QKV_W)
    sub = lambda t: t.reshape(ln, dil * DG)
    row = pl.BlockSpec((ln, DG), lambda z: (0, z))
    outs = pl.pallas_call(body, grid=(dil,),
                          in_specs=[_sub_spec(ln, QKV_W, 0), _sub_spec(ln, QKV_W, DG), _sub_spec(ln, QKV_W, 2 * DG), row, row, row],
                          out_specs=[row] * 3, out_shape=[_sds((ln, dil * DG))] * 3, name=f"attn_bwd{l}_{bi}",
                          compiler_params=_cp(("parallel",)))(pv, pv, pv, sub(dyb), sub(lse_all), sub(delta))
    return [t.reshape(SEQ, DG) for t in outs]


def _attn_merge_fn(o1, o2, o3, l1, l2, l3):
    m = jnp.maximum(jnp.maximum(l1, l2), l3)
    w1, w2, w3 = jnp.exp(l1 - m), jnp.exp(l2 - m), jnp.exp(l3 - m)
    den = w1 + w2 + w3
    return (w1 * o1 + w2 * o2 + w3 * o3) / den, m + jnp.log(den)


def attn_merge(l, os_, ls_):
    ins = list(os_) + list(ls_)
    return _map_fwd(f"attn_merge{l}", _attn_merge_fn, (SEQ // RB,), ins, [_rows(DG)] * 6, [_sds((SEQ, DG))] * 2,
                    [_rows(DG)] * 2)


def attn_delta(l, dyb, yb, seg):
    fn = lambda d, y, s: (_dot(d * y, s),)
    return _map_fwd(f"attn_delta{l}", fn, (SEQ // RB,), [dyb, yb, seg], [_rows(DG), _rows(DG), _full((DG, DG))],
                    [_sds((SEQ, DG))], [_rows(DG)])[0]


def _ln_fn(x, mix, w, b):
    h = ALPHA * x + mix
    mu = jnp.mean(h, axis=-1, keepdims=True)
    d = h - mu
    var = jnp.mean(d * d, axis=-1, keepdims=True)
    return (d * lax.rsqrt(var + LN_EPS) * w + b,)


def ln_fwd(name, x, mix, w, b):
    specs = [_rows(D_MODEL), _rows(D_MODEL), _full((1, D_MODEL)), _full((1, D_MODEL))]
    return _map_fwd(name, _ln_fn, (SEQ // RB,), [x, mix, w, b], specs, [_sds((SEQ, D_MODEL))], [_rows(D_MODEL)])[0]


def ln_bwd(name, x, mix, w, b, dy):
    specs = [_rows(D_MODEL), _rows(D_MODEL), _full((1, D_MODEL)), _full((1, D_MODEL))]
    return _map_bwd(name, _ln_fn, (SEQ // RB,), [x, mix, w, b], specs, [[dy]], [[_rows(D_MODEL)]], want=[1, 2, 3],
                    acc=(2, 3))


def loss_call(y, tgt):
    def fn(yy, tt):
        e = yy - tt
        part = 0.5 * jnp.sum(jnp.sum(e * e, axis=-1, keepdims=True) * (1.0 / D_MODEL), axis=0, keepdims=True)
        return e * (1.0 / D_MODEL), jnp.broadcast_to(part, (8, 128))

    return _map_fwd("loss", fn, (SEQ // RB,), [y, tgt], [_rows(D_MODEL)] * 2,
                    [_sds((SEQ, D_MODEL)), _sds((SEQ // RB * 8, 128))],
                    [_rows(D_MODEL), pl.BlockSpec((8, 128), lambda i: (i, 0))])


LATE_KEYS = ("w_out", "w_up_t", "w_down")


def _full_rows(g):
    return g.reshape(N_DEV * g.shape[1], g.shape[2])


def layer_fwd(l, x, vfirst, wts, p, gather=(), late=False):
    sv = {"x": x}
    proj = _mm(f"mm_in{l}", x, wts["w_in"], "nn", *MM_TILES["k1024"])
    fl = lerp_fwd(l, proj, p["mu"])
    xc = conv_fwd(l, proj, p["conv_w"], p["conv_b"])
    w, k2, v2, c, b, g = rwkv_pre_fwd(l, fl, vfirst, p)
    y_scan, states, sv["gathered"] = rwkv_scan_fwd(l, fl, w, k2, v2, c, b, p, gather)
    if late:
        wts = dict(wts, **dict(zip(LATE_KEYS, [_full_rows(g) for g in sv["gathered"][:3]])))
    sv["wts"] = wts
    ya = rwkv_post_fwd(l, y_scan, fl, k2, v2, g, p)
    qkv = proj[:, C_AQ:C_AQ + 3 * DG]
    outs, lses = [], []
    for bi, (win, dil) in enumerate(DILATED):
        o, lse = attn_branch_fwd(l, bi, qkv, dil)
        outs.append(o)
        lses.append(lse)
    yb, lse_all = attn_merge(l, outs, lses)
    yc, ssd_states = ssd_fwd(l, proj, xc, p)
    yd, hg_states = hgrn_fwd(l, proj, p)
    ycat = jnp.concatenate([ya, yb, yc, yd], axis=1).astype(BF16)
    mix = _mm(f"mm_out{l}", ycat, wts["w_out"], "nn", *MM_TILES["k1024"])
    x1 = ln_fwd(f"ln1_fwd{l}", x, mix, p["ln1_w"], p["ln1_b"])
    hh = _mm(f"mm_up{l}", x1, wts["w_up_t"], "nt", *MM_TILES["k1024"], epilogue="relu2")
    m2 = _mm(f"mm_down{l}", hh, wts["w_down"], "nn", *MM_TILES["k4096"])
    x2 = ln_fwd(f"ln2_fwd{l}", x1, m2, p["ln2_w"], p["ln2_b"])
    sv.update(proj=proj, fl=fl, xc=xc, w=w, k2=k2, v2=v2, c=c, b=b, g=g, y_scan=y_scan, states=states,
              yb=yb, lse_all=lse_all, ssd_states=ssd_states, hg_states=hg_states, ycat=ycat, mix=mix, x1=x1, hh=hh, qkv=qkv,
              m2=m2, vfirst=vfirst)
    return x2, sv


def layer_bwd(l, dx2, dvfirst_next, sv, wts, p, exchange=(), early=None):
    gr = {}
    x, x1, proj, fl = sv["x"], sv["x1"], sv["proj"], sv["fl"]
    dres2, gr["ln2_w"], gr["ln2_b"] = ln_bwd(f"ln2_bwd{l}", x1, sv["m2"], p["ln2_w"], p["ln2_b"], dx2)
    du = _mm(f"mm_down_dx{l}", dres2, wts["w_down"], "nt", *MM_TILES["k1024"], add=sv["hh"], epilogue="relu2_bwd",
             out_dtype=BF16)
    gr["w_down"] = _mm(f"mm_down_dw{l}", sv["hh"], dres2, "tn", *MM_TILES["wgrad_tall"])
    dx1 = _mm(f"mm_up_dx{l}", du, wts["w_up_t"], "nn", *MM_TILES["k4096"], add=dres2, add_scale=ALPHA)
    gr["w_up_t"] = _mm(f"mm_up_dw{l}", du, x1, "tn", *MM_TILES["wgrad_tall"])
    dres1, gr["ln1_w"], gr["ln1_b"] = ln_bwd(f"ln1_bwd{l}", x, sv["mix"], p["ln1_w"], p["ln1_b"], dx1)
    dycat = _mm(f"mm_out_dx{l}", dres1, wts["w_out"], "nt", *MM_TILES["k1024"])
    gr["w_out"] = _mm(f"mm_out_dw{l}", sv["ycat"], dres1, "tn", 1024, 1024, 512)
    if early is not None:
        gr["early_own"], early_parts = early({k: gr[k] for k in LATE_KEYS})
        exchange = list(exchange) + list(early_parts)
    dya, dyb, dyc, dyd = (dycat[:, i * DG:(i + 1) * DG] for i in range(4))
    dhg4, gr["lb0"], gr["lb1"], gr["hgrn_norm_w"] = hgrn_bwd(l, proj, sv["hg_states"], dyd, p)
    dz, dxc, ddt, gr["dt_bias"], gr["a_log"], gr["ssd_d"], gr["ssd_norm_w"] = ssd_bwd(l, proj, sv["xc"], sv["ssd_states"], dyc, p)
    dxbc, gr["conv_w"], gr["conv_b"] = conv_bwd(l, proj, p["conv_w"], p["conv_b"], dxc)
    delta = attn_delta(l, dyb, sv["yb"], p["seg64"])
    dqs, dks, dvs = [], [], []
    for bi, (win, dil) in enumerate(DILATED):
        dq, dk, dv = attn_branch_bwd(l, bi, sv["qkv"], dil, dyb, sv["lse_all"], delta)
        dqs.append(dq)
        dks.append(dk)
        dvs.append(dv)
    dq_a, dk_a, dv_a = _addn(f"attn_dq{l}", *dqs), _addn(f"attn_dk{l}", *dks), _addn(f"attn_dv{l}", *dvs)
    pg = rwkv_post_bwd(l, sv["y_scan"], fl, sv["k2"], sv["v2"], sv["g"], p, dya)
    gr["lnx_w"], gr["lnx_b"], gr["r_k"] = pg["lnx_w"], pg["lnx_b"], pg["r_k"]
    (dr, dw, dk, dv, dc, db), gr["exchanged"] = rwkv_scan_bwd(l, fl, sv["w"], sv["k2"], sv["v2"], sv["c"], sv["b"],
                                                              sv["states"], pg["y"], p, exchange)
    v2_cts = [dv, pg["v2"]] + ([dvfirst_next] if dvfirst_next is not None else [])
    qg = rwkv_pre_bwd(l, fl, sv["vfirst"], p, [[dw], [dk, pg["k2"]], v2_cts, [dc], [db], [pg["g"]]])
    for nme in ("w0", "w2p", "a0", "a2p", "g2p", "k_k", "k_a", "v0", "v2p"):
        if nme in qg:
            gr[nme] = qg[nme]
    dfr = _addn(f"rwkv_dr{l}", dr, pg["fr"])
    dvres = qg["fvres"] if l > 0 else jnp.zeros((SEQ, 128), F32)
    dfl_out = jnp.concatenate([dfr, qg["fk"], qg["fv"], qg["flora"], dvres], axis=1)
    dfl_in, gr["mu"] = lerp_bwd(l, proj, p["mu"], dfl_out)
    dproj = jnp.concatenate([dfl_in[:, 0:768], dq_a, dk_a, dv_a, dz, dxbc, dhg4, dfl_in[:, 768:896], ddt,
                             dfl_in[:, 896:1024], jnp.zeros((SEQ, 128), F32)], axis=1).astype(BF16)
    dx = _mm(f"mm_in_dx{l}", dproj, wts["w_in"], "nt", *MM_TILES["k4096"], add=dres1, add_scale=ALPHA)
    gr["w_in"] = _mm(f"mm_in_dw{l}", x, dproj, "tn", *MM_TILES["wgrad_wide"])
    return dx, (qg["vfirst"] if l > 0 else None), gr


def _w_in_pad(w_in_l, w_vres):
    rows = w_in_l.shape[0]
    z = lambda n: jnp.zeros((rows, n), w_in_l.dtype)
    vres = z(128) if w_vres is None else jnp.concatenate([w_vres, z(96)], axis=1)
    return jnp.concatenate([w_in_l[:, 0:768], w_in_l[:, 896:1664], w_in_l[:, 1664:1920], w_in_l[:, 1920:2688],
                            w_in_l[:, 2692:3716], w_in_l[:, 768:896], w_in_l[:, 2688:2692], z(124), vres, z(128)], axis=1)


def _w_in_unpad(g):
    g_in = jnp.concatenate([g[:, 0:768], g[:, C_LORA:C_LORA + 128], g[:, 768:1536], g[:, C_Z:C_Z + 256],
                            g[:, C_XBC:C_XBC + 768], g[:, C_DT:C_DT + 4], g[:, C_HQ:C_HQ + 1024]], axis=1)
    return g_in, g[:, C_VRES:C_VRES + 32]


def _consts():
    pair = jnp.arange(HG_NC * HGRN_CHUNK * HGRN_CHUNK)
    i128 = jnp.arange(128)
    same_chunk = (i128[:, None] // HGRN_CHUNK) == (i128[None, :] // HGRN_CHUNK)
    seg64 = _seg_ones(DG, HD)
    tri128 = (i128[:, None] >= i128[None, :]).astype(F32)
    return dict(
        seg64=seg64, seg64x3_bf16=jnp.concatenate([seg64, seg64, seg64], axis=0).astype(BF16),
        dmask=(jnp.arange(HD)[:, None] == (jnp.arange(DG)[None, :] % HD)).astype(F32),
        tri_bd128=(same_chunk & (i128[:, None] >= i128[None, :])).astype(F32), ones_bd128=same_chunk.astype(F32),
        first_row=(i128[None, :] == (jnp.arange(HG_NC) * HGRN_CHUNK)[:, None]).astype(F32),
        causal_blk=jnp.broadcast_to((((pair // HGRN_CHUNK) % HGRN_CHUNK) >= (pair % HGRN_CHUNK)).astype(F32)[:, None],
                                    (HG_NC * HGRN_CHUNK * HGRN_CHUNK, DG)),
        e128=((i128[:, None] == (jnp.arange(DG)[None, :] // HD)) & (i128[:, None] < NH)).astype(F32),
        tri128=tri128, tri128t=tri128.T, seg128=_seg_ones(DG, 128), ones128=jnp.ones((128, 128), F32))


def _pad_lanes(v, n):
    return jnp.concatenate([v, jnp.zeros((n - v.shape[0],), v.dtype)])[None, :]


def _layer_params(l, raw, consts):
    p = dict(consts)
    row = lambda name: raw[name][l][None, :]
    z = lambda r: jnp.zeros((r, DG), F32)
    mu_vres = raw["mu_vres"][l - 1] if l > 0 else jnp.zeros((32,), F32)
    p["mu"] = jnp.concatenate([raw["mu_shift"][l], mu_vres, jnp.zeros((96,), F32)])[None, :]
    p["conv_w"], p["conv_b"] = raw["ssd_conv_w"][l], row("ssd_conv_b")
    p["w0"], p["a0"], p["k_k"], p["k_a"] = row("rwkv_w0"), row("rwkv_a0"), row("rwkv_k_k"), row("rwkv_k_a")
    p["lnx_w"], p["lnx_b"] = row("rwkv_lnx_w"), row("rwkv_lnx_b")
    p["r_k"] = raw["rwkv_r_k"][l].reshape(1, DG)
    p["w2p"] = jnp.concatenate([raw["rwkv_w2"][l], z(96)], axis=0)
    p["a2p"] = jnp.concatenate([z(32), raw["rwkv_a2"][l], z(64)], axis=0)
    p["g2p"] = jnp.concatenate([z(64), raw["rwkv_g2"][l]], axis=0)
    if l > 0:
        p["v0"] = raw["rwkv_v0"][l - 1][None, :]
        p["v2p"] = jnp.concatenate([raw["rwkv_v2"][l - 1], z(96)], axis=0)
    p["lb0"], p["lb1"] = raw["lower_bounds"][0:1], raw["lower_bounds"][1:2]
    p["hgrn_norm_w"], p["ssd_norm_w"] = row("hgrn_norm_w"), row("ssd_norm_w")
    p["dt_bias"], p["a_log"], p["ssd_d"] = (_pad_lanes(raw[n][l], 128) for n in ("ssd_dt_bias", "ssd_A_log", "ssd_D"))
    for n in ("ln1_w", "ln1_b", "ln2_w", "ln2_b"):
        p[n] = row(n)
    return p


def _natural_grads(g0, g1):
    gs = (g0, g1)
    st = lambda key, f=lambda a: a[0]: jnp.stack([f(g[key]) for g in gs])
    out = {}
    out["lower_bounds"] = jnp.concatenate([g0["lb0"] + g1["lb0"], g0["lb1"] + g1["lb1"]], axis=0)
    out["mu_shift"] = st("mu", lambda a: a[0, :896])
    out["mu_vres"] = g1["mu"][:, 896:928]
    out["rwkv_w0"], out["rwkv_a0"], out["rwkv_k_k"], out["rwkv_k_a"] = st("w0"), st("a0"), st("k_k"), st("k_a")
    out["rwkv_w2"] = st("w2p", lambda a: a[0:32])
    out["rwkv_a2"] = st("a2p", lambda a: a[32:64])
    out["rwkv_g2"] = st("g2p", lambda a: a[64:128])
    out["rwkv_r_k"] = st("r_k", lambda a: a.reshape(NH, HD))
    out["rwkv_lnx_w"], out["rwkv_lnx_b"] = st("lnx_w"), st("lnx_b")
    out["rwkv_v0"] = g1["v0"]
    out["rwkv_v2"] = g1["v2p"][None, 0:32]
    out["ssd_conv_w"] = st("conv_w", lambda a: a)
    out["ssd_conv_b"] = st("conv_b")
    out["ssd_dt_bias"], out["ssd_A_log"], out["ssd_D"] = (st(k, lambda a: a[0, :NH]) for k in ("dt_bias", "a_log", "ssd_d"))
    out["ssd_norm_w"], out["hgrn_norm_w"] = st("ssd_norm_w"), st("hgrn_norm_w")
    for n in ("ln1_w", "ln1_b", "ln2_w", "ln2_b"):
        out[n] = st(n)
    return out


MESH_T = pl.DeviceIdType.MESH
ANY = pl.BlockSpec(memory_space=pl.ANY)


def _dev_index(px, py, pc):
    return 4 * px + 2 * py + pc


def all_gather(arrs):
    n = len(arrs)

    def body(*refs):
        begin, end = _gather_steps(refs[:n], refs[n:2 * n], *refs[2 * n:])
        begin()
        end()

    return pl.pallas_call(body, in_specs=[ANY] * n, out_specs=[ANY] * n, out_shape=_gather_shapes(arrs),
                          scratch_shapes=_gather_sems(n), name="all_gather")(*arrs)


def _gather_shapes(arrs):
    return [_sds((N_DEV,) + a.shape, a.dtype) for a in arrs]


def _gather_sems(n):
    return [pltpu.SemaphoreType.DMA((n, 7)), pltpu.SemaphoreType.DMA((n, 7)), pltpu.SemaphoreType.DMA((n,))]


def _gather_steps(ins, outs, send_sems, recv_sems, local_sems):
    n = len(ins)
    x, y, c = lax.axis_index("x"), lax.axis_index("y"), lax.axis_index("c")
    me, sibling = (x, y, c), (x, y, 1 - c)
    chips = [(1 - x, y), (x, 1 - y), (1 - x, 1 - y)]

    def copy(a, k, block, to, src=None):
        slot = outs[a].at[_dev_index(*block)]
        return pltpu.make_async_remote_copy(src_ref=slot if src is None else src, dst_ref=slot,
                                            send_sem=send_sems.at[a, k], recv_sem=recv_sems.at[a, k],
                                            device_id=to, device_id_type=MESH_T)

    def own_copies():
        mine = [pltpu.make_async_copy(ins[a], outs[a].at[_dev_index(*me)], local_sems.at[a]) for a in range(n)]
        first = []
        for a in range(n):
            first.append(copy(a, 0, me, sibling, src=ins[a]))
            first += [copy(a, 1 + j, me, (*chip, c), src=ins[a]) for j, chip in enumerate(chips)]
        return mine, first

    def begin():
        mine, first = own_copies()
        for cp in mine + first:
            cp.start()

    def end():
        mine, first = own_copies()
        passed = []
        for j, chip in enumerate(chips):
            for a in range(n):
                copy(a, 1 + j, (*chip, c), me).wait_recv()
                fwd = copy(a, 4 + j, (*chip, c), sibling)
                fwd.start()
                passed.append(fwd)
        for a in range(n):
            copy(a, 0, sibling, me).wait_recv()
            for j, chip in enumerate(chips):
                copy(a, 4 + j, (*chip, 1 - c), me).wait_recv()
        for cp in first + passed:
            cp.wait_send()
        for cp in mine:
            cp.wait()

    return begin, end


def _chips(x, y):
    return [(x, y), (1 - x, y), (x, 1 - y), (1 - x, 1 - y)]


def exchange_siblings(arrs, name):
    n = len(arrs)

    def body(*refs):
        ins, sib = refs[:n], refs[n:2 * n]
        send_sems, recv_sems = refs[2 * n:]
        x, y, c = lax.axis_index("x"), lax.axis_index("y"), lax.axis_index("c")
        sibling = (x, y, 1 - c)
        sends = []
        for a in range(n):
            for k, (cx, cy) in enumerate(_chips(x, y)):
                sd = pltpu.make_async_remote_copy(src_ref=ins[a].at[_dev_index(cx, cy, 1 - c)], dst_ref=sib[a].at[k],
                                                  send_sem=send_sems.at[a, k], recv_sem=recv_sems.at[a, k],
                                                  device_id=sibling, device_id_type=MESH_T)
                sd.start()
                sends.append(sd)
        for sd in sends:
            sd.wait_recv()
        for sd in sends:
            sd.wait_send()

    sem = pltpu.SemaphoreType.DMA((n, 4))
    return pl.pallas_call(body, in_specs=[ANY] * n, out_specs=[ANY] * n,
                          out_shape=[_sds((4,) + a.shape[1:], a.dtype) for a in arrs],
                          scratch_shapes=[sem, sem], name=name)(*arrs)


def reduce_pair(name, send, slots, sib, wire_dtype):
    _, r, c = send.shape
    rb = min(r, 262144 // c)

    def body(slots_ref, m0, m1, m2, m3, s_ref, own_ref, part_ref):
        own_ref[...] = m0[...] + s_ref[0]
        for k, m_ref in enumerate((m1, m2, m3)):
            part_ref[k] = (m_ref[...] + s_ref[k + 1]).astype(wire_dtype)

    mine = [pl.BlockSpec((None, rb, c), lambda i, s, k=k: (s[k], i, 0)) for k in range(4)]
    grid_spec = pltpu.PrefetchScalarGridSpec(
        num_scalar_prefetch=1, grid=(r // rb,),
        in_specs=mine + [pl.BlockSpec((4, rb, c), lambda i, s: (0, i, 0))],
        out_specs=[pl.BlockSpec((rb, c), lambda i, s: (i, 0)), pl.BlockSpec((3, rb, c), lambda i, s: (0, i, 0))])
    return pl.pallas_call(body, grid_spec=grid_spec, out_shape=[_sds((r, c)), _sds((3, r, c), wire_dtype)], name=name,
                          compiler_params=_cp(("parallel",)))(slots, send, send, send, send, sib)


def _chip_exchange_steps(ins, recv, send_sems, recv_sems):
    x, y, c = lax.axis_index("x"), lax.axis_index("y"), lax.axis_index("c")

    def copies():
        return [pltpu.make_async_remote_copy(src_ref=ins[a].at[k], dst_ref=recv[a].at[k], send_sem=send_sems.at[a, k],
                                             recv_sem=recv_sems.at[a, k], device_id=(cx, cy, c), device_id_type=MESH_T)
                for a in range(len(ins)) for k, (cx, cy) in enumerate(_chips(x, y)[1:])]

    def begin():
        for cp in copies():
            cp.start()

    def end():
        cps = copies()
        for cp in cps:
            cp.wait_recv()
        for cp in cps:
            cp.wait_send()

    return begin, end


def exchange_chips(parts, rep):
    n = len(parts)

    def body(*refs):
        ins, rep_ref = refs[:n], refs[n]
        recv, rep_all = refs[n + 1:2 * n + 1], refs[2 * n + 1]
        send_sems, recv_sems, rsend_sems, rrecv_sems, local_sem = refs[2 * n + 2:]
        x, y, c = lax.axis_index("x"), lax.axis_index("y"), lax.axis_index("c")
        me = _dev_index(x, y, c)
        mine = pltpu.make_async_copy(rep_ref, rep_all.at[me], local_sem)
        mine.start()
        begin, end = _chip_exchange_steps(ins, recv, send_sems, recv_sems)
        begin()
        rels = [(rx, ry, rc) for rx in (0, 1) for ry in (0, 1) for rc in (0, 1)][1:]
        peers = [(jnp.where(rx, 1 - x, x), jnp.where(ry, 1 - y, y), jnp.where(rc, 1 - c, c)) for rx, ry, rc in rels]
        rcps = []
        for k, peer in enumerate(peers):
            cp = pltpu.make_async_remote_copy(src_ref=rep_ref, dst_ref=rep_all.at[me], send_sem=rsend_sems.at[k],
                                              recv_sem=rrecv_sems.at[k], device_id=peer, device_id_type=MESH_T)
            cp.start()
            rcps.append(cp)
        for k, peer in enumerate(peers):
            pltpu.make_async_remote_copy(src_ref=rep_ref, dst_ref=rep_all.at[_dev_index(*peer)], send_sem=rsend_sems.at[k],
                                         recv_sem=rrecv_sems.at[k], device_id=peer, device_id_type=MESH_T).wait_recv()
        end()
        for cp in rcps:
            cp.wait_send()
        mine.wait()

    outs = pl.pallas_call(
        body, in_specs=[ANY] * (n + 1), out_specs=[ANY] * (n + 1),
        out_shape=[_sds(a.shape, a.dtype) for a in parts] + [_sds((N_DEV,) + rep.shape, rep.dtype)],
        scratch_shapes=[pltpu.SemaphoreType.DMA((n, 3)), pltpu.SemaphoreType.DMA((n, 3)), pltpu.SemaphoreType.DMA((7,)),
                        pltpu.SemaphoreType.DMA((7,)), pltpu.SemaphoreType.DMA],
        name="exchange_chips")(*parts, rep)
    return outs[:n], outs[n]


def adamw(name, terms, w, m, v, transposed=False):
    r, c = w.shape[::-1] if transposed else w.shape
    rb = r if transposed else min(r, 262144 // c)
    c1 = 1.0 - ADAM_B1 ** ADAM_STEP
    c2 = 1.0 - ADAM_B2 ** ADAM_STEP
    nt = len(terms)

    def body(*refs):
        w_ref, m_ref, v_ref = refs[nt:nt + 3]
        g_ref, d_ref, nm_ref, nv_ref = refs[nt + 3:]
        g = refs[0][...].astype(F32)
        for t_ref in refs[1:nt]:
            g = g + t_ref[...].astype(F32)
        if transposed:
            g = g.T
        nm = ADAM_B1 * m_ref[...] + (1.0 - ADAM_B1) * g
        nv = ADAM_B2 * v_ref[...] + (1.0 - ADAM_B2) * (g * g)
        g_ref[...] = g
        nm_ref[...] = nm
        nv_ref[...] = nv
        d_ref[...] = -ADAM_LR * ((nm / c1) / (jnp.sqrt(nv / c2) + ADAM_EPS) + ADAM_WD * w_ref[...])

    blk = pl.BlockSpec((rb, c), lambda i: (i, 0))
    wblk = pl.BlockSpec((c, r), lambda i: (0, 0)) if transposed else blk
    tspecs = [blk if k is None else pl.BlockSpec((None, rb, c), lambda i, k=k: (k, i, 0)) for _, k in terms]
    return pl.pallas_call(body, grid=(r // rb,), in_specs=tspecs + [wblk] * 3, out_specs=[wblk] * 4,
                          out_shape=[_sds(w.shape)] * 4, name=name,
                          compiler_params=_cp(("parallel",)))(*[t for t, _ in terms], w, m, v)


SMS_ROWS = 16
REP_ROWS = 24
N_BIG = 8
SMALL_SHARDED = (("rwkv_w2", (2, 32, 32)), ("rwkv_a2", (2, 32, 32)), ("rwkv_g2", (2, 64, 32)), ("rwkv_v2", (1, 32, 32)),
                 ("ssd_conv_w", (2, 4, 96)))
REPLICATED = (("lower_bounds", (2, 256)), ("mu_shift", (2, 896)), ("mu_vres", (1, 32)), ("rwkv_w0", (2, 256)),
              ("rwkv_a0", (2, 256)), ("rwkv_k_k", (2, 256)), ("rwkv_k_a", (2, 256)), ("rwkv_r_k", (2, 4, 64)),
              ("rwkv_lnx_w", (2, 256)), ("rwkv_lnx_b", (2, 256)), ("rwkv_v0", (1, 256)), ("ssd_conv_b", (2, 768)),
              ("ssd_dt_bias", (2, 4)), ("ssd_A_log", (2, 4)), ("ssd_D", (2, 4)), ("ssd_norm_w", (2, 256)),
              ("hgrn_norm_w", (2, 256)), ("ln1_w", (2, 1024)), ("ln1_b", (2, 1024)), ("ln2_w", (2, 1024)),
              ("ln2_b", (2, 1024)))


def _flat_rows(parts, rows):
    flat = jnp.concatenate([a.reshape(-1) for a in parts])
    return jnp.concatenate([flat, jnp.zeros((rows * PACK_W - flat.shape[0],), flat.dtype)]).reshape(rows, PACK_W)


def _local_arrays(d):
    arrs = [_w_in_pad(d["w_in"][0], None), _w_in_pad(d["w_in"][1], d["w_in_vres"][0]), d["w_out"][0], d["w_out"][1],
            d["w_up"][0], d["w_up"][1], d["w_down"][0], d["w_down"][1],
            _flat_rows([d[n] for n, _ in SMALL_SHARDED], SMS_ROWS)]
    return arrs, _flat_rows([d[n] for n, _ in REPLICATED], REP_ROWS)


def _unflat(rows2d, table):
    flat, out, o = rows2d.reshape(-1), {}, 0
    for name, shape in table:
        n = 1
        for s in shape:
            n *= s
        out[name] = flat[o:o + n].reshape(shape)
        o += n
    return out


def _from_local_arrays(arrs, rep):
    d = {}
    g0, _ = _w_in_unpad(arrs[0])
    g1, gv = _w_in_unpad(arrs[1])
    d["w_in"], d["w_in_vres"] = jnp.stack([g0, g1]), gv[None]
    d["w_out"] = jnp.stack([arrs[2], arrs[3]])
    d["w_up"] = jnp.stack([arrs[4], arrs[5]])
    d["w_down"] = jnp.stack([arrs[6], arrs[7]])
    d.update(_unflat(arrs[8], SMALL_SHARDED))
    d.update(_unflat(rep, REPLICATED))
    return d


def _small_sharded_full(gs):
    small, flat, o = {}, gs.reshape(N_DEV, -1), 0
    for name, shape in SMALL_SHARDED:
        n = shape[0] * shape[1] * shape[2]
        blk = flat[:, o:o + n].reshape((N_DEV,) + shape)
        small[name] = blk.transpose(1, 2, 0, 3).reshape(shape[0], shape[1], N_DEV * shape[2])
        o += n
    return small


def _owner_blocks(g):
    return g.reshape(N_DEV, g.shape[0] // N_DEV, g.shape[1])


def _small_send_arrays(small_grads):
    sms = []
    for name, shape in SMALL_SHARDED:
        g = small_grads[name].reshape(shape[0], shape[1], N_DEV, shape[2]).transpose(2, 0, 1, 3)
        sms.append(g.reshape(N_DEV, -1))
    sms = jnp.concatenate(sms, axis=1)
    sms = jnp.concatenate([sms, jnp.zeros((N_DEV, SMS_ROWS * PACK_W - sms.shape[1]), F32)], axis=1)
    return sms.reshape(N_DEV, SMS_ROWS, PACK_W), _flat_rows([small_grads[n] for n, _ in REPLICATED], REP_ROWS)


BIG_KEYS = ("w_in", "w_out", "w_up_t", "w_down")


def _weights_of(full):
    return dict(zip(BIG_KEYS, full))


def _local_step(x, tgt, wts, raw, gather=(), pair_sums=None):
    consts = _consts()
    ps = [_layer_params(l, raw, consts) for l in range(DEPTH)]
    x1, sv0 = layer_fwd(0, x, None, wts[0], ps[0], gather[:4], late=bool(gather))
    wts1 = {"w_in": _full_rows(sv0["gathered"][3])} if gather else wts[1]
    x2, sv1 = layer_fwd(1, x1, sv0["fl"], wts1, ps[1], gather[4:], late=bool(gather))
    dy, lparts = loss_call(x2, tgt)
    loss = jnp.sum(lparts[::8, 0])
    dx1, dvfirst, g1 = layer_bwd(1, dy, None, sv1, sv1["wts"], ps[1])
    big1 = {k: g1[k] for k in BIG_KEYS}
    if pair_sums is None:
        dx0, _, g0 = layer_bwd(0, dx1, dvfirst, sv0, sv0["wts"], ps[0])
        early = None
    else:
        own1, parts1 = pair_sums("1", big1)
        dx0, _, g0 = layer_bwd(0, dx1, dvfirst, sv0, sv0["wts"], ps[0], parts1, lambda gs: pair_sums("0a", gs))
        early = (own1, g0["exchanged"][:4], g0["early_own"], g0["exchanged"][4:])
    big = [{k: g0[k] for k in BIG_KEYS}, big1]
    return loss, dx0, big, _natural_grads(g0, g1), early


WEIGHT_NAMES = ("lower_bounds", "w_in", "w_in_vres", "mu_shift", "mu_vres", "rwkv_w0", "rwkv_w2", "rwkv_a0", "rwkv_a2",
                "rwkv_g2", "rwkv_k_k", "rwkv_k_a", "rwkv_r_k", "rwkv_lnx_w", "rwkv_lnx_b", "rwkv_v0", "rwkv_v2",
                "ssd_conv_w", "ssd_conv_b", "ssd_dt_bias", "ssd_A_log", "ssd_D", "ssd_norm_w", "hgrn_norm_w", "w_out",
                "ln1_w", "ln1_b", "w_up", "w_down", "ln2_w", "ln2_b")


def kernel(x, lower_bounds, w_in, w_in_vres, mu_shift, mu_vres, rwkv_w0, rwkv_w2, rwkv_a0, rwkv_a2, rwkv_g2, rwkv_k_k, rwkv_k_a, rwkv_r_k, rwkv_lnx_w, rwkv_lnx_b, rwkv_v0, rwkv_v2, ssd_conv_w, ssd_conv_b, ssd_dt_bias, ssd_A_log, ssd_D, ssd_norm_w, hgrn_norm_w, w_out, ln1_w, ln1_b, w_up, w_down, ln2_w, ln2_b, loss_target, m_lower_bounds, m_w_in, m_w_in_vres, m_mu_shift, m_mu_vres, m_rwkv_w0, m_rwkv_w2, m_rwkv_a0, m_rwkv_a2, m_rwkv_g2, m_rwkv_k_k, m_rwkv_k_a, m_rwkv_r_k, m_rwkv_lnx_w, m_rwkv_lnx_b, m_rwkv_v0, m_rwkv_v2, m_ssd_conv_w, m_ssd_conv_b, m_ssd_dt_bias, m_ssd_A_log, m_ssd_D, m_ssd_norm_w, m_hgrn_norm_w, m_w_out, m_ln1_w, m_ln1_b, m_w_up, m_w_down, m_ln2_w, m_ln2_b, v_lower_bounds, v_w_in, v_w_in_vres, v_mu_shift, v_mu_vres, v_rwkv_w0, v_rwkv_w2, v_rwkv_a0, v_rwkv_a2, v_rwkv_g2, v_rwkv_k_k, v_rwkv_k_a, v_rwkv_r_k, v_rwkv_lnx_w, v_rwkv_lnx_b, v_rwkv_v0, v_rwkv_v2, v_ssd_conv_w, v_ssd_conv_b, v_ssd_dt_bias, v_ssd_A_log, v_ssd_D, v_ssd_norm_w, v_hgrn_norm_w, v_w_out, v_ln1_w, v_ln1_b, v_w_up, v_w_down, v_ln2_w, v_ln2_b):
    given = dict(locals())
    w = {n: given[n] for n in WEIGHT_NAMES}
    w_arrs, w_rep = _local_arrays(w)
    m_arrs, m_rep = _local_arrays({n: given["m_" + n] for n in WEIGHT_NAMES})
    v_arrs, v_rep = _local_arrays({n: given["v_" + n] for n in WEIGHT_NAMES})
    wire = lambda a: (w_arrs[a].T if a in (4, 5) else w_arrs[a]).astype(BF16)
    gathered0 = all_gather([wire(0), w_arrs[N_BIG]])
    raw = {n: w[n] for n, _ in REPLICATED}
    raw.update(_small_sharded_full(gathered0[1]))
    mx, my, mc = lax.axis_index("x"), lax.axis_index("y"), lax.axis_index("c")
    slots = jnp.stack([_dev_index(cx, cy, mc) for cx, cy in _chips(mx, my)]).astype(jnp.int32)

    def pair_sums(tag, grads, extra=()):
        send = [_owner_blocks(g) for g in grads.values()] + list(extra)
        wire = [BF16] * len(grads) + [F32] * len(extra)
        sib = exchange_siblings(send, f"exchange_siblings{tag}")
        res = [reduce_pair(f"reduce_pair{tag}_{i}", s, slots, sb, dt) for i, (s, sb, dt) in enumerate(zip(send, sib, wire))]
        return [o for o, _ in res], [pt for _, pt in res]

    behind_scan = [wire(a) for a in (2, 4, 6, 1, 3, 5, 7)]
    loss, dx, big, small_grads, (own1, recv1, own0a, recv0a) = _local_step(
        x[0], loss_target[0], [{"w_in": _full_rows(gathered0[0])}, None], raw, behind_scan, pair_sums)
    sms_send, rep = _small_send_arrays(small_grads)
    own0b, parts0b = pair_sums("0b", {"w_in": big[0]["w_in"]}, [sms_send])
    recv0b, rep_all = exchange_chips(parts0b, rep)
    own, recv = [None] * (N_BIG + 1), [None] * (N_BIG + 1)
    for a, o, r in zip((1, 3, 5, 7), own1, recv1):
        own[a], recv[a] = o, r
    for a, o, r in zip((2, 4, 6), own0a, recv0a):
        own[a], recv[a] = o, r
    for a, o, r in zip((0, N_BIG), own0b, recv0b):
        own[a], recv[a] = o, r
    results = [adamw(f"adamw{a}", [(own[a], None), (recv[a], 0), (recv[a], 1), (recv[a], 2)], w_arrs[a], m_arrs[a], v_arrs[a],
                     transposed=a in (4, 5)) for a in range(N_BIG + 1)]
    rep_res = adamw("adamw_rep", [(rep_all, q) for q in range(N_DEV)], w_rep, m_rep, v_rep)
    loss = lax.psum(loss, ("x", "y", "c"))
    outs = [loss, dx[None]]
    for q in range(4):
        d = _from_local_arrays([res[q] for res in results], rep_res[q])
        outs += [d[n] for n in WEIGHT_NAMES]
    return tuple(outs)
```

```python
import functools

import jax
import jax.numpy as jnp
from jax import lax
from jax.experimental import pallas as pl
from jax.experimental.pallas import tpu as pltpu

F32 = jnp.float32
BF16 = jnp.bfloat16
HI = lax.Precision.HIGHEST

N_DEV = 8
SEQ = 2048
D_MODEL = 1024
D_FF = 4096
DG = 256
NH = 4
HD = 64
DEPTH = 2
ALPHA = (2.0 * DEPTH) ** 0.25
LN_EPS = 1e-5
RMS_EPS = 1e-5
GN_EPS = HD * 1e-5
IN_COLS = 3716
SSD_N = 128
SSD_CHUNK = 128
HGRN_CHUNK = 16
DILATED = ((128, 1), (512, 4), (2048, 16))

ADAM_LR, ADAM_B1, ADAM_B2, ADAM_EPS, ADAM_WD, ADAM_STEP = 0.001, 0.9, 0.999, 1e-08, 0.01, 10

PW = 4096
C_R, C_K, C_V = 0, 256, 512
C_AQ, C_AK, C_AV = 768, 1024, 1280
C_Z, C_XBC = 1536, 1792
C_HQ, C_HF, C_HI, C_HG = 2560, 2816, 3072, 3328
C_LORA, C_DT, C_VRES = 3584, 3712, 3840

RB = 256
VMEM_LIMIT = 56 * 1024 * 1024
PACK_W = 1024


def _cp(sem=None):
    return pltpu.CompilerParams(dimension_semantics=sem, vmem_limit_bytes=VMEM_LIMIT)


def _sds(shape, dt=F32):
    return jax.ShapeDtypeStruct(tuple(shape), dt)


def _rows(w, cb=0, rb=RB):
    return pl.BlockSpec((rb, w), lambda i: (i, cb))


def _full(shape):
    n = len(shape)
    return pl.BlockSpec(tuple(shape), lambda *_: (0,) * n)


def _sigmoid(x):
    return 1.0 / (1.0 + jnp.exp(-x))


def _silu(x):
    return x * _sigmoid(x)


def _softplus(x):
    return jnp.maximum(x, 0.0) + jnp.log(1.0 + jnp.exp(jnp.where(x > 0, -x, x)))


MID = lax.Precision.HIGH
NN, TN, NT = (((1,), (0,)), ((), ())), (((0,), (0,)), ((), ())), (((1,), (1,)), ((), ()))


def _dot(a, b):
    return lax.dot_general(a, b, NN, precision=MID, preferred_element_type=F32)


def _dot_tn(a, b):
    return lax.dot_general(a, b, TN, precision=MID, preferred_element_type=F32)


def _dot_nt(a, b):
    return lax.dot_general(a, b, NT, precision=MID, preferred_element_type=F32)


def _dotx(a, b):
    return lax.dot_general(a, b, NN, precision=HI, preferred_element_type=F32)


def _dotx_tn(a, b):
    return lax.dot_general(a, b, TN, precision=HI, preferred_element_type=F32)


def _seg_ones(n, seg):
    i = jnp.arange(n)
    return (i[:, None] // seg == i[None, :] // seg).astype(F32)


def _shift_down(x, s):
    row = lax.broadcasted_iota(jnp.int32, x.shape, 0)
    return jnp.where(row < s, 0.0, pltpu.roll(x, s, 0))


def _shift_up(x, s):
    n = x.shape[0]
    row = lax.broadcasted_iota(jnp.int32, x.shape, 0)
    return jnp.where(row >= n - s, 0.0, pltpu.roll(x, n - s, 0))


@functools.partial(jax.custom_vjp, nondiff_argnums=(1,))
def _tshift(x, s):
    return _shift_down(x, s)


def _tshift_fwd(x, s):
    return _shift_down(x, s), None


def _tshift_bwd(s, _, g):
    return (_shift_up(g, s),)


_tshift.defvjp(_tshift_fwd, _tshift_bwd)


def _map_fwd(name, fn, grid, ins, in_specs, out_shapes, out_specs):
    n_in = len(ins)

    def body(*refs):
        ys = fn(*[r[...] for r in refs[:n_in]])
        for r, y in zip(refs[n_in:], ys):
            r[...] = y

    return pl.pallas_call(body, grid=grid, in_specs=in_specs, out_specs=out_specs, out_shape=out_shapes,
                          name=name, compiler_params=_cp(("parallel",)))(*ins)


def _map_bwd(name, fn, grid, ins, in_specs, cts, ct_specs, want, acc=(), gout=None):
    n_in = len(ins)
    flat_cts = [c for group in cts for c in group]
    flat_specs = [s for group in ct_specs for s in group]
    n_ct = len(flat_cts)
    gout = gout or {}
    out_shapes = [gout[i][0] if i in gout else _sds(ins[i].shape) for i in want]
    out_specs = [gout[i][1] if i in gout else in_specs[i] for i in want]

    def body(*refs):
        xs = [r[...] for r in refs[:n_in]]
        cvals = [r[...] for r in refs[n_in:n_in + n_ct]]
        gouts = refs[n_in + n_ct:]
        cs, p = [], 0
        for group in cts:
            v = cvals[p]
            for q in range(1, len(group)):
                v = v + cvals[p + q]
            cs.append(v)
            p += len(group)

        def f(*wanted):
            full = list(xs)
            for i, w in zip(want, wanted):
                full[i] = w
            return tuple(fn(*full))

        _, vjp = jax.vjp(f, *[xs[i] for i in want])
        gs = vjp(tuple(cs))
        for o, i, g in zip(gouts, want, gs):
            if i in acc:
                @pl.when(pl.program_id(0) == 0)
                def _():
                    o[...] = jnp.zeros_like(o)

                o[...] += g
            else:
                o[...] = g

    sem = ("arbitrary",) if acc else ("parallel",)
    return pl.pallas_call(body, grid=grid, in_specs=list(in_specs) + flat_specs, out_specs=out_specs,
                          out_shape=out_shapes, name=name, compiler_params=_cp(sem))(*ins, *flat_cts)


def _addn(name, *arrs):
    n, c = arrs[0].shape

    def fn(*xs):
        r = xs[0]
        for x in xs[1:]:
            r = r + x
        return (r,)

    return _map_fwd(name, fn, (n // RB,), list(arrs), [_rows(c)] * len(arrs), [_sds((n, c))], [_rows(c)])[0]


MM_TILES = {"k1024": (2048, 512, 1024), "k4096": (1024, 1024, 1024), "wgrad_tall": (2048, 1024, 512),
            "wgrad_wide": (1024, 2048, 512)}


def _mm(name, a, b, mode, tm, tn, tk, add=None, add_scale=1.0, epilogue=None, out_dtype=F32):
    if mode == "nn":
        (m, k), n = a.shape, b.shape[1]
    elif mode == "nt":
        (m, k), n = a.shape, b.shape[0]
    else:
        (k, m), n = a.shape, b.shape[1]
    nk = k // tk
    dn = {"nn": (((1,), (0,)), ((), ())), "nt": (((1,), (1,)), ((), ())), "tn": (((0,), (0,)), ((), ()))}[mode]

    def body(*refs):
        a_ref, b_ref = refs[:2]
        add_ref = refs[2] if add is not None else None
        o_ref = refs[3] if add is not None else refs[2]
        prod = lax.dot_general(a_ref[...].astype(BF16), b_ref[...].astype(BF16), dn, preferred_element_type=F32)

        def finish(r):
            if epilogue == "relu2":
                r = jnp.maximum(r, 0.0)
                r = r * r
            elif epilogue == "relu2_bwd":
                r = r * (2.0 * jnp.sqrt(add_ref[...]))
            elif add is not None:
                r = r + add_scale * add_ref[...]
            o_ref[...] = r.astype(out_dtype)

        if nk == 1:
            finish(prod)
        else:
            acc = refs[-1]
            kk = pl.program_id(2)

            @pl.when(kk == 0)
            def _():
                acc[...] = prod

            @pl.when(kk > 0)
            def _():
                acc[...] += prod

            @pl.when(kk == nk - 1)
            def _():
                finish(acc[...])

    a_spec = pl.BlockSpec((tk, tm), lambda i, j, q: (q, i)) if mode == "tn" else pl.BlockSpec((tm, tk), lambda i, j, q: (i, q))
    b_spec = pl.BlockSpec((tn, tk), lambda i, j, q: (j, q)) if mode == "nt" else pl.BlockSpec((tk, tn), lambda i, j, q: (q, j))
    o_spec = pl.BlockSpec((tm, tn), lambda i, j, q: (i, j))
    ins, specs = [a, b], [a_spec, b_spec]
    if add is not None:
        ins.append(add)
        specs.append(o_spec)
    return pl.pallas_call(body, grid=(m // tm, n // tn, nk), in_specs=specs, out_specs=o_spec,
                          out_shape=_sds((m, n), out_dtype),
                          scratch_shapes=[pltpu.VMEM((tm, tn), F32)] if nk > 1 else [], name=name,
                          compiler_params=_cp(("parallel", "parallel", "arbitrary")))(*ins)


LERP_BLOCKS = (0, 1, 2, 3, 4, 5, C_LORA // 128, C_VRES // 128)


def _lerp_colmap(j):
    r = jnp.where(j < 6, j, jnp.where(j == 6, C_LORA // 128, C_VRES // 128))
    return (0, r)


def _lerp_fn(f, mu):
    return (f + (_tshift(f, 1) - f) * mu,)


def _lerp_specs():
    return [pl.BlockSpec((SEQ, 128), _lerp_colmap), pl.BlockSpec((1, 128), lambda j: (0, j))]


def lerp_fwd(l, proj, mu):
    return _map_fwd(f"lerp_fwd{l}", _lerp_fn, (8,), [proj, mu], _lerp_specs(), [_sds((SEQ, 1024))],
                    [pl.BlockSpec((SEQ, 128), lambda j: (0, j))])[0]


def lerp_bwd(l, proj, mu, dfl):
    n_in = 2

    def body(f_ref, mu_ref, g_ref, df_ref, dmu_ref):
        _, vjp = jax.vjp(_lerp_fn, f_ref[...], mu_ref[...])
        df, dmu = vjp((g_ref[...],))
        df_ref[...] = df
        dmu_ref[...] = dmu

    cspec = pl.BlockSpec((SEQ, 128), lambda j: (0, j))
    return pl.pallas_call(body, grid=(8,), in_specs=_lerp_specs() + [cspec],
                          out_specs=[cspec, pl.BlockSpec((1, 128), lambda j: (0, j))],
                          out_shape=[_sds((SEQ, 1024)), _sds((1, 1024))], name=f"lerp_bwd{l}",
                          compiler_params=_cp(("parallel",)))(proj, mu, dfl)


def _conv_fn(x, w, b):
    y = x * w[3:4, :] + _tshift(x, 1) * w[2:3, :] + _tshift(x, 2) * w[1:2, :] + _tshift(x, 3) * w[0:1, :] + b
    return (_silu(y),)


def _conv_specs():
    return [pl.BlockSpec((SEQ, 128), lambda j: (0, C_XBC // 128 + j)), pl.BlockSpec((4, 128), lambda j: (0, j)),
            pl.BlockSpec((1, 128), lambda j: (0, j))]


def conv_fwd(l, proj, w, b):
    return _map_fwd(f"conv_fwd{l}", _conv_fn, (6,), [proj, w, b], _conv_specs(), [_sds((SEQ, 768))],
                    [pl.BlockSpec((SEQ, 128), lambda j: (0, j))])[0]


def conv_bwd(l, proj, w, b, dxc):
    def body(x_ref, w_ref, b_ref, g_ref, dx_ref, dw_ref, db_ref):
        _, vjp = jax.vjp(_conv_fn, x_ref[...], w_ref[...], b_ref[...])
        dx, dw, db = vjp((g_ref[...],))
        dx_ref[...] = dx
        dw_ref[...] = dw
        db_ref[...] = db

    cspec = pl.BlockSpec((SEQ, 128), lambda j: (0, j))
    return pl.pallas_call(body, grid=(6,), in_specs=_conv_specs() + [cspec],
                          out_specs=[cspec, pl.BlockSpec((4, 128), lambda j: (0, j)), pl.BlockSpec((1, 128), lambda j: (0, j))],
                          out_shape=[_sds((SEQ, 768)), _sds((4, 768)), _sds((1, 768))], name=f"conv_bwd{l}",
                          compiler_params=_cp(("parallel",)))(proj, w, b, dxc)


def _rwkv_pre_fn(has_vres):
    def fn(fk, fv, flora, *rest):
        if has_vres:
            fvres, vfirst, w0, w2p, a0, a2p, g2p, k_k, k_a, v0, v2p, seg = rest
        else:
            w0, w2p, a0, a2p, g2p, k_k, k_a, seg = rest
        w_log = -_softplus(-(w0 + _dot(jnp.tanh(flora), w2p))) - 0.5
        w = jnp.exp(-jnp.exp(w_log))
        a = _sigmoid(a0 + _dot(flora, a2p))
        g = _dot(_sigmoid(flora), g2p)
        if has_vres:
            v2 = fv + (vfirst - fv) * _sigmoid(v0 + _dot(fvres, v2p))
        else:
            v2 = fv * 1.0
        kk = fk * k_k
        kk = kk / jnp.maximum(jnp.sqrt(_dot(kk * kk, seg)), 1e-12)
        k2 = fk * (1.0 + (a - 1.0) * k_a)
        return w, k2, v2, -kk, kk * a, g

    return fn


def _rwkv_pre_args(fl, vfirst, p, has_vres):
    ins = [fl, fl, fl]
    specs = [_rows(256, 1), _rows(256, 2), _rows(128, 6)]
    if has_vres:
        ins += [fl, vfirst]
        specs += [_rows(128, 7), _rows(256, 2)]
    names = ["w0", "w2p", "a0", "a2p", "g2p", "k_k", "k_a"] + (["v0", "v2p"] if has_vres else []) + ["seg64"]
    for nme in names:
        ins.append(p[nme])
        specs.append(_full(p[nme].shape))
    return ins, specs, names


def rwkv_pre_fwd(l, fl, vfirst, p):
    has_vres = l > 0
    ins, specs, _ = _rwkv_pre_args(fl, vfirst, p, has_vres)
    return _map_fwd(f"rwkv_pre_fwd{l}", _rwkv_pre_fn(has_vres), (SEQ // RB,), ins, specs,
                    [_sds((SEQ, DG))] * 6, [_rows(DG)] * 6)


def rwkv_pre_bwd(l, fl, vfirst, p, cts):
    has_vres = l > 0
    ins, specs, names = _rwkv_pre_args(fl, vfirst, p, has_vres)
    n_row = 5 if has_vres else 3
    want = list(range(n_row)) + [n_row + i for i, nme in enumerate(names) if nme != "seg64"]
    acc = tuple(w for w in want if w >= n_row)
    ct_specs = [[_rows(DG)] * len(g) for g in cts]
    gout = {0: (_sds((SEQ, DG)), _rows(DG)), 1: (_sds((SEQ, DG)), _rows(DG)), 2: (_sds((SEQ, 128)), _rows(128))}
    if has_vres:
        gout[3] = (_sds((SEQ, 128)), _rows(128))
        gout[4] = (_sds((SEQ, DG)), _rows(DG))
    gs = _map_bwd(f"rwkv_pre_bwd{l}", _rwkv_pre_fn(has_vres), (SEQ // RB,), ins, specs, cts, ct_specs, want, acc, gout)
    keys = ["fk", "fv", "flora"] + (["fvres", "vfirst"] if has_vres else []) + [nme for nme in names if nme != "seg64"]
    return dict(zip(keys, gs))


def _rwkv_post_fn(y, fr, k2, v2, g, lnx_w, lnx_b, r_k, seg):
    mu = _dot(y, seg) * (1.0 / HD)
    d = y - mu
    var = _dot(d * d, seg) * (1.0 / HD)
    yn = d * lax.rsqrt(var + GN_EPS) * lnx_w + lnx_b
    bonus = _dot(fr * k2 * r_k, seg) * v2
    return ((yn + bonus) * g,)


def _rwkv_post_args(y, fl, k2, v2, g, p):
    ins = [y, fl, k2, v2, g, p["lnx_w"], p["lnx_b"], p["r_k"], p["seg64"]]
    specs = [_rows(DG), _rows(DG, 0), _rows(DG), _rows(DG), _rows(DG)] + [_full(x.shape) for x in ins[5:]]
    return ins, specs


def rwkv_post_fwd(l, y, fl, k2, v2, g, p):
    ins, specs = _rwkv_post_args(y, fl, k2, v2, g, p)
    return _map_fwd(f"rwkv_post_fwd{l}", _rwkv_post_fn, (SEQ // RB,), ins, specs, [_sds((SEQ, DG))], [_rows(DG)])[0]


def rwkv_post_bwd(l, y, fl, k2, v2, g, p, dya):
    ins, specs = _rwkv_post_args(y, fl, k2, v2, g, p)
    gs = _map_bwd(f"rwkv_post_bwd{l}", _rwkv_post_fn, (SEQ // RB,), ins, specs, [[dya]], [[_rows(DG)]],
                  want=[0, 1, 2, 3, 4, 5, 6, 7], acc=(5, 6, 7), gout={1: (_sds((SEQ, DG)), _rows(DG))})
    return dict(zip(["y", "fr", "k2", "v2", "g", "lnx_w", "lnx_b", "r_k"], gs))


SCAN_TB = 64


def _coltile8(rows8, dmask, ones_stack, parts):
    pieces, rest = [], rows8
    for q in range(parts):
        piece = rest.astype(BF16).astype(F32)
        if q < parts - 1:
            rest = rest - piece
        pieces.append((piece[:, None, :] * dmask[None]).reshape(8 * HD, DG).astype(BF16))
    x = pieces[0] if parts == 1 else jnp.concatenate(pieces, axis=1)
    return jnp.dot(x, ones_stack, preferred_element_type=F32).reshape(8, HD, DG)


def _coltiles_bf16(rows_list, dmask, ones_bf16):
    x = jnp.concatenate([(r8[:, None, :] * dmask[None]).reshape(8 * HD, DG).astype(BF16) for r8 in rows_list], axis=0)
    t = jnp.dot(x, ones_bf16, preferred_element_type=F32)
    return [t[q * 8 * HD:(q + 1) * 8 * HD].reshape(8, HD, DG) for q in range(len(rows_list))]


def _segrows8(x8, dmask, ones_bf16):
    t = jnp.dot(x8.reshape(8 * HD, DG).astype(BF16), ones_bf16, preferred_element_type=F32).reshape(8, HD, DG)
    return jnp.sum(t * dmask[None], axis=1)


def rwkv_scan_fwd(l, fl, w, k2, v2, c, b, p, gather=()):
    nblk = SEQ // SCAN_TB
    ng = len(gather)

    def body(*refs):
        r_ref, w_ref, k_ref, v_ref, c_ref, b_ref, ones_ref, dm_ref = refs[:8]
        y_ref, st_ref = refs[8 + ng:10 + ng]
        s_sc = refs[10 + 2 * ng]
        if ng:
            begin, end = _gather_steps(refs[8:8 + ng], refs[10 + ng:10 + 2 * ng], *refs[11 + 2 * ng:])

            @pl.when(pl.program_id(0) == 0)
            def _():
                begin()

        @pl.when(pl.program_id(0) == 0)
        def _():
            s_sc[...] = jnp.zeros_like(s_sc)

        ones3, ones = ones_ref[...], ones_ref[0:DG, :]
        dmask = dm_ref[...]

        def group(gi, carry):
            t0 = pl.multiple_of(gi * 8, 8)
            sl = pl.ds(t0, 8)
            v8 = v_ref[sl, :]
            wt = _coltile8(w_ref[sl, :], dmask, ones3, 3)
            ct, bt, kt, rt = _coltiles_bf16([c_ref[sl, :], b_ref[sl, :], k_ref[sl, :], r_ref[sl, :]], dmask, ones)
            t = s_sc[...]
            for j in range(8):
                sa = jnp.sum(t * ct[j], axis=0, keepdims=True)
                t = t * wt[j] + bt[j] * sa + kt[j] * v8[j:j + 1, :]
                st_ref[t0 + j] = t
            s_sc[...] = t
            y_ref[sl, :] = jnp.sum(st_ref[sl] * rt, axis=1)
            return carry

        lax.fori_loop(0, SCAN_TB // 8, group, 0)

        if ng:
            @pl.when(pl.program_id(0) == nblk - 1)
            def _():
                end()

    row = pl.BlockSpec((SCAN_TB, DG), lambda i: (i, 0))
    ins = [fl, w, k2, v2, c, b, p["seg64x3_bf16"], p["dmask"]] + list(gather)
    specs = [row] * 6 + [_full((3 * DG, DG)), _full((HD, DG))] + [ANY] * ng
    outs = pl.pallas_call(body, grid=(nblk,), in_specs=specs,
                          out_specs=[row, pl.BlockSpec((SCAN_TB, HD, DG), lambda i: (i, 0, 0))] + [ANY] * ng,
                          out_shape=[_sds((SEQ, DG)), _sds((SEQ, HD, DG))] + _gather_shapes(gather),
                          scratch_shapes=[pltpu.VMEM((HD, DG), F32)] + (_gather_sems(ng) if ng else []),
                          name=f"rwkv_scan_fwd{l}", compiler_params=_cp(("arbitrary",)))(*ins)
    return outs[0], outs[1], list(outs[2:])


def rwkv_scan_bwd(l, fl, w, k2, v2, c, b, states, dy, p, exchange=()):
    nblk = SEQ // SCAN_TB
    nx = len(exchange)

    def body(*refs):
        r_ref, w_ref, k_ref, v_ref, c_ref, b_ref, dy_ref, st_ref, sp_ref, ones_ref, dm_ref = refs[:11]
        dr_ref, dw_ref, dk_ref, dv_ref, dc_ref, db_ref = refs[11 + nx:17 + nx]
        g_sc, prev_sc, d8_sc, dsa_sc = refs[17 + 2 * nx:21 + 2 * nx]
        i = pl.program_id(0)
        if nx:
            begin, end = _chip_exchange_steps(refs[11:11 + nx], refs[17 + nx:17 + 2 * nx], *refs[21 + 2 * nx:])

            @pl.when(i == 0)
            def _():
                begin()

        @pl.when(i == 0)
        def _():
            g_sc[...] = jnp.zeros_like(g_sc)

        ones3, ones = ones_ref[...], ones_ref[0:DG, :]
        dmask = dm_ref[...]
        first_block = i == nblk - 1

        def group(gr, carry):
            gi = SCAN_TB // 8 - 1 - gr
            t0 = pl.multiple_of(gi * 8, 8)
            sl = pl.ds(t0, 8)
            v8, dy8 = v_ref[sl, :], dy_ref[sl, :]
            t8 = st_ref[sl]
            @pl.when(gi > 0)
            def _():
                prev_sc[0] = st_ref[t0 - 1]

            @pl.when(gi == 0)
            def _():
                prev_sc[0] = jnp.where(first_block, 0.0, sp_ref[0])

            for j in range(1, 8):
                prev_sc[j] = t8[j - 1]
            tp8 = prev_sc[...]
            wt = _coltile8(w_ref[sl, :], dmask, ones3, 3)
            ct, bt, kt, rt = _coltiles_bf16([c_ref[sl, :], b_ref[sl, :], k_ref[sl, :], r_ref[sl, :]], dmask, ones)
            sa8 = jnp.sum(tp8 * ct, axis=1)
            g = g_sc[...]
            for j in range(7, -1, -1):
                g = g + rt[j] * dy8[j:j + 1, :]
                d8_sc[j] = g
                dsa = jnp.sum(g * bt[j], axis=0, keepdims=True)
                dsa_sc[j:j + 1, :] = dsa
                g = g * wt[j] + ct[j] * dsa
            g_sc[...] = g
            d8 = d8_sc[...]
            dsa8 = dsa_sc[...]
            dv_ref[sl, :] = jnp.sum(d8 * kt, axis=1)
            dr_ref[sl, :] = _segrows8(t8 * dy8[:, None, :], dmask, ones)
            dk_ref[sl, :] = _segrows8(d8 * v8[:, None, :], dmask, ones)
            dw_ref[sl, :] = _segrows8(tp8 * d8, dmask, ones)
            db_ref[sl, :] = _segrows8(d8 * sa8[:, None, :], dmask, ones)
            dc_ref[sl, :] = _segrows8(tp8 * dsa8[:, None, :], dmask, ones)
            return carry

        lax.fori_loop(0, SCAN_TB // 8, group, 0)

        if nx:
            @pl.when(i == nblk - 1)
            def _():
                end()

    row = pl.BlockSpec((SCAN_TB, DG), lambda i: (nblk - 1 - i, 0))
    st_spec = pl.BlockSpec((SCAN_TB, HD, DG), lambda i: (nblk - 1 - i, 0, 0))
    sp_spec = pl.BlockSpec((1, HD, DG), lambda i: (jnp.maximum((nblk - 1 - i) * SCAN_TB - 1, 0), 0, 0))
    ins = [fl, w, k2, v2, c, b, dy, states, states, p["seg64x3_bf16"], p["dmask"]] + list(exchange)
    specs = [row] * 7 + [st_spec, sp_spec, _full((3 * DG, DG)), _full((HD, DG))] + [ANY] * nx
    tile8 = pltpu.VMEM((8, HD, DG), F32)
    sems = [pltpu.SemaphoreType.DMA((nx, 3)), pltpu.SemaphoreType.DMA((nx, 3))] if nx else []
    outs = pl.pallas_call(body, grid=(nblk,), in_specs=specs, out_specs=[row] * 6 + [ANY] * nx,
                          out_shape=[_sds((SEQ, DG))] * 6 + [_sds(a.shape, a.dtype) for a in exchange],
                          scratch_shapes=[pltpu.VMEM((HD, DG), F32), tile8, tile8, pltpu.VMEM((8, DG), F32)] + sems,
                          name=f"rwkv_scan_bwd{l}", compiler_params=_cp(("arbitrary",)))(*ins)
    return outs[:6], list(outs[6:])


HG_ROWS = 128


HG_NC = HG_ROWS // HGRN_CHUNK


def _hgrn_block_fn(layer):
    def fn(hq, hf, hi, hg, sprev, lb0, lb1, norm_w, seg, bd, tri_bd, ones_bd, first_row, causal):
        e0 = jnp.exp(lb0 - jnp.maximum(lb0, lb1))
        e1 = jnp.exp(lb1 - jnp.maximum(lb0, lb1))
        sm0, sm1 = e0 / (e0 + e1), e1 / (e0 + e1)
        lb = (sm0 - sm0) if layer == 0 else ((sm0 + sm1) - sm0)
        forget = lb + (1.0 - lb) * _sigmoid(hf)
        logf = jnp.log(forget)
        kk = 1.0 - forget
        q = _silu(hq)
        c, nc = HGRN_CHUNK, HG_NC
        b = _dotx(tri_bd, logf)
        bl = _dotx(ones_bd, logf)
        split = lambda t: t.reshape(nc, c, DG)
        b4 = split(b)
        diff = (b4[:, :, None, :] - b4[:, None, :, :]).reshape(nc * c * c, DG)
        dec = jnp.exp(jnp.where(causal > 0.5, diff, -1e30))
        qrep = jnp.broadcast_to(split(q)[:, :, None, :], (nc, c, c, DG)).reshape(nc * c * c, DG)
        ktil = jnp.broadcast_to(split(kk)[:, None, :, :], (nc, c, c, DG)).reshape(nc * c * c, DG)
        vtil = jnp.broadcast_to(split(hi)[:, None, :, :], (nc, c, c, DG)).reshape(nc * c * c, DG)
        att = _dot(qrep * ktil * dec, seg)
        o_intra = jnp.sum((att * vtil).reshape(nc * c, c, DG), axis=1)
        kd4 = split(kk * jnp.exp(bl - b))
        qe4 = split(q * jnp.exp(b))
        v4 = split(hi)
        tot = jnp.exp(_dotx(first_row, bl))
        s, o_inter = sprev, []
        for ci in range(nc):
            o_inter.append(_dot_nt(qe4[ci], s))
            s = s * tot[ci:ci + 1, :] + _dot_tn(v4[ci], kd4[ci]) * bd
        o = o_intra + jnp.concatenate(o_inter, axis=0)
        ms = _dot(o * o, seg) * (1.0 / HD)
        y = o * lax.rsqrt(ms + RMS_EPS) * norm_w * _silu(hg)
        return y, s

    return fn


def _hgrn_consts(p):
    return [p["seg64"], p["seg64"], p["tri_bd128"], p["ones_bd128"], p["first_row"], p["causal_blk"]]


def hgrn_fwd(l, proj, p):
    fn = _hgrn_block_fn(l)

    def body(hq_ref, hf_ref, hi_ref, hg_ref, *rest):
        const_refs, (y_ref, st_ref, s_sc) = rest[:-3], rest[-3:]

        @pl.when(pl.program_id(0) == 0)
        def _():
            s_sc[...] = jnp.zeros_like(s_sc)

        sprev = s_sc[...]
        st_ref[0] = sprev
        y, snext = fn(hq_ref[...], hf_ref[...], hi_ref[...], hg_ref[...], sprev, *[r[...] for r in const_refs])
        y_ref[...] = y
        s_sc[...] = snext

    rows = lambda cb: pl.BlockSpec((HG_ROWS, DG), lambda i: (i, cb))
    ins = [proj, proj, proj, proj, p["lb0"], p["lb1"], p["hgrn_norm_w"]] + _hgrn_consts(p)
    specs = [rows(C_HQ // DG), rows(C_HF // DG), rows(C_HI // DG), rows(C_HG // DG)] + [_full(x.shape) for x in ins[4:]]
    return pl.pallas_call(body, grid=(SEQ // HG_ROWS,), in_specs=specs,
                          out_specs=[rows(0), pl.BlockSpec((1, DG, DG), lambda i: (i, 0, 0))],
                          out_shape=[_sds((SEQ, DG)), _sds((SEQ // HG_ROWS, DG, DG))],
                          scratch_shapes=[pltpu.VMEM((DG, DG), F32)], name=f"hgrn_fwd{l}",
                          compiler_params=_cp(("arbitrary",)))(*ins)


def hgrn_bwd(l, proj, states, dy, p, sibling=()):
    fn = _hgrn_block_fn(l)
    nblk = SEQ // HG_ROWS
    n_const = len(_hgrn_consts(p))
    ns = len(sibling)

    def body(hq_ref, hf_ref, hi_ref, hg_ref, st_ref, dy_ref, lb0_ref, lb1_ref, nw_ref, *rest):
        const_refs, rest = rest[:n_const], rest[n_const:]
        dp_ref, dlb0_ref, dlb1_ref, dnw_ref = rest[ns:ns + 4]
        ds_sc = rest[2 * ns + 4]
        if ns:
            begin, end = _sibling_steps(rest[:ns], rest[ns + 4:2 * ns + 4], *rest[2 * ns + 5:])

            @pl.when(pl.program_id(0) == 0)
            def _():
                begin()

        @pl.when(pl.program_id(0) == 0)
        def _():
            ds_sc[...] = jnp.zeros_like(ds_sc)
            dlb0_ref[...] = jnp.zeros_like(dlb0_ref)
            dlb1_ref[...] = jnp.zeros_like(dlb1_ref)
            dnw_ref[...] = jnp.zeros_like(dnw_ref)

        consts = [r[...] for r in const_refs]
        f = lambda hq, hf, hi, hg, sp, b0, b1, nw: fn(hq, hf, hi, hg, sp, b0, b1, nw, *consts)
        _, vjp = jax.vjp(f, hq_ref[...], hf_ref[...], hi_ref[...], hg_ref[...], st_ref[0], lb0_ref[...], lb1_ref[...],
                         nw_ref[...])
        dhq, dhf, dhi, dhg, dsp, dlb0, dlb1, dnw = vjp((dy_ref[...], ds_sc[...]))
        dp_ref[:, 0:DG] = dhq
        dp_ref[:, DG:2 * DG] = dhf
        dp_ref[:, 2 * DG:3 * DG] = dhi
        dp_ref[:, 3 * DG:4 * DG] = dhg
        ds_sc[...] = dsp
        dlb0_ref[...] += dlb0
        dlb1_ref[...] += dlb1
        dnw_ref[...] += dnw

        if ns:
            @pl.when(pl.program_id(0) == nblk - 1)
            def _():
                end()

    rows = lambda cb: pl.BlockSpec((HG_ROWS, DG), lambda i: (nblk - 1 - i, cb))
    ins = [proj, proj, proj, proj, states, dy, p["lb0"], p["lb1"], p["hgrn_norm_w"]] + _hgrn_consts(p)
    specs = [rows(C_HQ // DG), rows(C_HF // DG), rows(C_HI // DG), rows(C_HG // DG),
             pl.BlockSpec((1, DG, DG), lambda i: (nblk - 1 - i, 0, 0)), rows(0)] + [_full(x.shape) for x in ins[6:]]
    sem = pltpu.SemaphoreType.DMA((max(ns, 1), 4))
    outs = pl.pallas_call(body, grid=(nblk,), in_specs=specs + [ANY] * ns,
                          out_specs=[pl.BlockSpec((HG_ROWS, 4 * DG), lambda i: (nblk - 1 - i, 0)), _full((1, DG)),
                                     _full((1, DG)), _full((1, DG))] + [ANY] * ns,
                          out_shape=[_sds((SEQ, 4 * DG)), _sds((1, DG)), _sds((1, DG)), _sds((1, DG))]
                          + [_sds((4,) + a.shape[1:], a.dtype) for a in sibling],
                          scratch_shapes=[pltpu.VMEM((DG, DG), F32)] + ([sem, sem] if ns else []), name=f"hgrn_bwd{l}",
                          compiler_params=_cp(("arbitrary",)))(*ins, *sibling)
    return outs[:4], list(outs[4:])


def _ssd_chunk_fn(z, xs, bm, cm, dtr, sprev, dt_bias, a_log, d_par, norm_w, e128, tri, trit, seg128, ones128):
    lc = SSD_CHUNK
    dt = _softplus(dtr + dt_bias)
    a = -jnp.exp(a_log)
    da = dt * a * (lax.broadcasted_iota(jnp.int32, (1, 128), 1) < NH).astype(F32)
    cs = _dotx(tri, da)
    cst = _dotx_tn(da, trit)
    cs_b = _dotx(cs, e128)
    dt_b = _dotx(dt, e128)
    csl_b = _dotx(jnp.sum(da, axis=0, keepdims=True), e128)
    xdt = xs * dt_b
    lane = lax.broadcasted_iota(jnp.int32, (1, DG), 1)
    rowi = lax.broadcasted_iota(jnp.int32, (lc, lc), 0)
    coli = lax.broadcasted_iota(jnp.int32, (lc, lc), 1)
    y = jnp.zeros((lc, DG), F32)
    snew = jnp.zeros((DG, SSD_N), F32)
    d_b = jnp.zeros((1, DG), F32)
    wdec = xdt * jnp.exp(csl_b - cs_b)
    for g in range(2):
        bg = bm[:, g * SSD_N:(g + 1) * SSD_N]
        cg = cm[:, g * SSD_N:(g + 1) * SSD_N]
        gmat = _dot_nt(cg, bg)
        gmask = ((lane // 128) == g).astype(F32)
        snew = snew + _dot_tn(wdec * gmask, bg)
        y = y + _dot_nt(cg, sprev) * gmask * jnp.exp(cs_b)
        for hh in range(2):
            h = 2 * g + hh
            seg = jnp.where(rowi >= coli, cs[:, h:h + 1] - cst[h:h + 1, :], -1e30)
            hmask = ((lane // HD) == h).astype(F32)
            y = y + _dot(gmat * jnp.exp(seg), xdt * hmask)
            d_b = d_b + d_par[:, h:h + 1] * hmask
    cd = jnp.exp(_dotx_tn(_dotx(da, e128), ones128))
    snext = sprev * cd + snew
    y = y + xs * d_b
    y = y * _silu(z)
    ms = _dot(y * y, seg128) * (1.0 / 128.0)
    return y * lax.rsqrt(ms + RMS_EPS) * norm_w, snext


def ssd_fwd(l, proj, xc, p):
    nc = SEQ // SSD_CHUNK

    def body(z_ref, xs_ref, b_ref, c_ref, dt_ref, dtb_ref, al_ref, d_ref, nw_ref, e_ref, tri_ref, trit_ref, sg_ref,
             on_ref, y_ref, st_ref, s_sc):
        @pl.when(pl.program_id(0) == 0)
        def _():
            s_sc[...] = jnp.zeros_like(s_sc)

        sprev = s_sc[...]
        st_ref[0] = sprev
        y, snext = _ssd_chunk_fn(z_ref[...], xs_ref[...], b_ref[...], c_ref[...], dt_ref[...], sprev, dtb_ref[...],
                                 al_ref[...], d_ref[...], nw_ref[...], e_ref[...], tri_ref[...], trit_ref[...],
                                 sg_ref[...], on_ref[...])
        y_ref[...] = y
        s_sc[...] = snext

    rw = lambda w, cb: pl.BlockSpec((SSD_CHUNK, w), lambda i: (i, cb))
    ins = [proj, xc, xc, xc, proj, p["dt_bias"], p["a_log"], p["ssd_d"], p["ssd_norm_w"], p["e128"], p["tri128"],
           p["tri128t"], p["seg128"], p["ones128"]]
    specs = [rw(DG, C_Z // DG), rw(DG, 0), rw(DG, 1), rw(DG, 2), rw(128, C_DT // 128)] + [_full(x.shape) for x in ins[5:]]
    return pl.pallas_call(body, grid=(nc,), in_specs=specs,
                          out_specs=[rw(DG, 0), pl.BlockSpec((1, DG, SSD_N), lambda i: (i, 0, 0))],
                          out_shape=[_sds((SEQ, DG)), _sds((nc, DG, SSD_N))],
                          scratch_shapes=[pltpu.VMEM((DG, SSD_N), F32)], name=f"ssd_fwd{l}",
                          compiler_params=_cp(("arbitrary",)))(*ins)


def ssd_bwd(l, proj, xc, states, dy, p):
    nc = SEQ // SSD_CHUNK

    def body(z_ref, xs_ref, b_ref, c_ref, dt_ref, st_ref, dy_ref, dtb_ref, al_ref, d_ref, nw_ref, e_ref, tri_ref,
             trit_ref, sg_ref, on_ref, dz_ref, dxc_ref, ddt_ref, ddtb_ref, dal_ref, dd_ref, dnw_ref, ds_sc):
        @pl.when(pl.program_id(0) == 0)
        def _():
            ds_sc[...] = jnp.zeros_like(ds_sc)
            ddtb_ref[...] = jnp.zeros_like(ddtb_ref)
            dal_ref[...] = jnp.zeros_like(dal_ref)
            dd_ref[...] = jnp.zeros_like(dd_ref)
            dnw_ref[...] = jnp.zeros_like(dnw_ref)

        consts = (e_ref[...], tri_ref[...], trit_ref[...], sg_ref[...], on_ref[...])
        f = lambda *a: _ssd_chunk_fn(*a, *consts)
        _, vjp = jax.vjp(f, z_ref[...], xs_ref[...], b_ref[...], c_ref[...], dt_ref[...], st_ref[0], dtb_ref[...],
                         al_ref[...], d_ref[...], nw_ref[...])
        dz, dxs, db, dc, ddt, dsp, ddtb, dal, dd, dnw = vjp((dy_ref[...], ds_sc[...]))
        dz_ref[...] = dz
        dxc_ref[:, 0:DG] = dxs
        dxc_ref[:, DG:2 * DG] = db
        dxc_ref[:, 2 * DG:3 * DG] = dc
        ddt_ref[...] = ddt
        ds_sc[...] = dsp
        ddtb_ref[...] += ddtb
        dal_ref[...] += dal
        dd_ref[...] += dd
        dnw_ref[...] += dnw

    rw = lambda w, cb: pl.BlockSpec((SSD_CHUNK, w), lambda i: (nc - 1 - i, cb))
    ins = [proj, xc, xc, xc, proj, states, dy, p["dt_bias"], p["a_log"], p["ssd_d"], p["ssd_norm_w"], p["e128"],
           p["tri128"], p["tri128t"], p["seg128"], p["ones128"]]
    specs = [rw(DG, C_Z // DG), rw(DG, 0), rw(DG, 1), rw(DG, 2), rw(128, C_DT // 128),
             pl.BlockSpec((1, DG, SSD_N), lambda i: (nc - 1 - i, 0, 0)), rw(DG, 0)] + [_full(x.shape) for x in ins[7:]]
    return pl.pallas_call(body, grid=(nc,), in_specs=specs,
                          out_specs=[rw(DG, 0), rw(3 * DG, 0), rw(128, 0), _full((1, 128)), _full((1, 128)), _full((1, 128)),
                                     _full((1, DG))],
                          out_shape=[_sds((SEQ, DG)), _sds((SEQ, 3 * DG)), _sds((SEQ, 128)), _sds((1, 128)), _sds((1, 128)),
                                     _sds((1, 128)), _sds((1, DG))],
                          scratch_shapes=[pltpu.VMEM((DG, SSD_N), F32)], name=f"ssd_bwd{l}",
                          compiler_params=_cp(("arbitrary",)))(*ins)


ATT_BLK = 128


def _att_scores(qn, kc, kp, h, dil, has_prev):
    i = lax.broadcasted_iota(jnp.int32, (ATT_BLK, ATT_BLK), 0)
    j = lax.broadcasted_iota(jnp.int32, (ATT_BLK, ATT_BLK), 1)
    slope = 2.0 ** (-8.0 * (h + 1) / NH)
    scale = HD ** -0.5
    s_c = _dot_nt(qn, kc) * scale - slope * ((i - j) * dil).astype(F32)
    s_p = _dot_nt(qn, kp) * scale - slope * ((ATT_BLK + i - j) * dil).astype(F32)
    m_c = j <= i
    m_p = jnp.logical_and(j >= i, has_prev)
    return jnp.where(m_c, s_c, -1e30), jnp.where(m_p, s_p, -1e30), m_c, m_p


def _sub_spec(ln, width, col):
    return pl.BlockSpec((ln, DG), lambda z: (0, z * (width // DG) + col // DG))


QKV_W = 3 * DG


def attn_branch_fwd(l, bi, qkv, dil):
    ln = SEQ // dil
    nb = ln // ATT_BLK

    def body(q_ref, k_ref, v_ref, o_ref, l_ref):
        def blk(n, carry):
            r0 = pl.multiple_of(n * ATT_BLK, ATT_BLK)
            rp = pl.multiple_of(jnp.maximum(n - 1, 0) * ATT_BLK, ATT_BLK)
            cur, prv = pl.ds(r0, ATT_BLK), pl.ds(rp, ATT_BLK)
            for h in range(NH):
                hs = slice(h * HD, (h + 1) * HD)
                qn, kc, vc, kp, vp = q_ref[cur, hs], k_ref[cur, hs], v_ref[cur, hs], k_ref[prv, hs], v_ref[prv, hs]
                s_c, s_p, m_c, m_p = _att_scores(qn, kc, kp, h, dil, n > 0)
                m = jnp.maximum(jnp.max(s_c, axis=1, keepdims=True), jnp.max(s_p, axis=1, keepdims=True))
                p_c = jnp.where(m_c, jnp.exp(s_c - m), 0.0)
                p_p = jnp.where(m_p, jnp.exp(s_p - m), 0.0)
                den = jnp.sum(p_c, axis=1, keepdims=True) + jnp.sum(p_p, axis=1, keepdims=True)
                o_ref[cur, hs] = (_dot(p_c, vc) + _dot(p_p, vp)) / den
                l_ref[cur, hs] = jnp.broadcast_to(m + jnp.log(den), (ATT_BLK, HD))
            return carry

        lax.fori_loop(0, nb, blk, 0)

    pv = qkv.reshape(ln, dil * QKV_W)
    out = pl.BlockSpec((ln, DG), lambda z: (0, z))
    o, lse = pl.pallas_call(body, grid=(dil,), in_specs=[_sub_spec(ln, QKV_W, 0), _sub_spec(ln, QKV_W, DG), _sub_spec(ln, QKV_W, 2 * DG)],
                            out_specs=[out, out], out_shape=[_sds((ln, dil * DG))] * 2, name=f"attn_fwd{l}_{bi}",
                            compiler_params=_cp(("parallel",)))(pv, pv, pv)
    return o.reshape(SEQ, DG), lse.reshape(SEQ, DG)


def attn_branch_bwd(l, bi, qkv, dil, dyb, lse_all, delta):
    ln = SEQ // dil
    nb = ln // ATT_BLK
    scale = HD ** -0.5

    def body(q_ref, k_ref, v_ref, do_ref, l_ref, dl_ref, dq_ref, dk_ref, dv_ref):
        dk_ref[...] = jnp.zeros_like(dk_ref)
        dv_ref[...] = jnp.zeros_like(dv_ref)

        def blk(n, carry):
            r0 = pl.multiple_of(n * ATT_BLK, ATT_BLK)
            rp = pl.multiple_of(jnp.maximum(n - 1, 0) * ATT_BLK, ATT_BLK)
            cur, prv = pl.ds(r0, ATT_BLK), pl.ds(rp, ATT_BLK)
            for h in range(NH):
                hs = slice(h * HD, (h + 1) * HD)
                qn, don = q_ref[cur, hs], do_ref[cur, hs]
                lse, dlt = l_ref[cur, h * HD:h * HD + 1], dl_ref[cur, h * HD:h * HD + 1]
                kc, vc, kp, vp = k_ref[cur, hs], v_ref[cur, hs], k_ref[prv, hs], v_ref[prv, hs]
                s_c, s_p, m_c, m_p = _att_scores(qn, kc, kp, h, dil, n > 0)
                p_c = jnp.where(m_c, jnp.exp(s_c - lse), 0.0)
                p_p = jnp.where(m_p, jnp.exp(s_p - lse), 0.0)
                ds_c = p_c * (_dot_nt(don, vc) - dlt)
                ds_p = p_p * (_dot_nt(don, vp) - dlt)
                dq_ref[cur, hs] = (_dot(ds_c, kc) + _dot(ds_p, kp)) * scale
                dv_ref[prv, hs] += _dot_tn(p_p, don)
                dk_ref[prv, hs] += _dot_tn(ds_p, qn) * scale
                dv_ref[cur, hs] += _dot_tn(p_c, don)
                dk_ref[cur, hs] += _dot_tn(ds_c, qn) * scale
            return carry

        lax.fori_loop(0, nb, blk, 0)

    pv = qkv.reshape(ln, dil * QKV_W)
    sub = lambda t: t.reshape(ln, dil * DG)
    row = pl.BlockSpec((ln, DG), lambda z: (0, z))
    outs = pl.pallas_call(body, grid=(dil,),
                          in_specs=[_sub_spec(ln, QKV_W, 0), _sub_spec(ln, QKV_W, DG), _sub_spec(ln, QKV_W, 2 * DG), row, row, row],
                          out_specs=[row] * 3, out_shape=[_sds((ln, dil * DG))] * 3, name=f"attn_bwd{l}_{bi}",
                          compiler_params=_cp(("parallel",)))(pv, pv, pv, sub(dyb), sub(lse_all), sub(delta))
    return [t.reshape(SEQ, DG) for t in outs]


def _attn_merge_fn(o1, o2, o3, l1, l2, l3):
    m = jnp.maximum(jnp.maximum(l1, l2), l3)
    w1, w2, w3 = jnp.exp(l1 - m), jnp.exp(l2 - m), jnp.exp(l3 - m)
    den = w1 + w2 + w3
    return (w1 * o1 + w2 * o2 + w3 * o3) / den, m + jnp.log(den)


def attn_merge(l, os_, ls_):
    ins = list(os_) + list(ls_)
    return _map_fwd(f"attn_merge{l}", _attn_merge_fn, (SEQ // RB,), ins, [_rows(DG)] * 6, [_sds((SEQ, DG))] * 2,
                    [_rows(DG)] * 2)


def attn_delta(l, dyb, yb, seg):
    fn = lambda d, y, s: (_dot(d * y, s),)
    return _map_fwd(f"attn_delta{l}", fn, (SEQ // RB,), [dyb, yb, seg], [_rows(DG), _rows(DG), _full((DG, DG))],
                    [_sds((SEQ, DG))], [_rows(DG)])[0]


def _ln_fn(x, mix, w, b):
    h = ALPHA * x + mix
    mu = jnp.mean(h, axis=-1, keepdims=True)
    d = h - mu
    var = jnp.mean(d * d, axis=-1, keepdims=True)
    return (d * lax.rsqrt(var + LN_EPS) * w + b,)


def ln_fwd(name, x, mix, w, b):
    specs = [_rows(D_MODEL), _rows(D_MODEL), _full((1, D_MODEL)), _full((1, D_MODEL))]
    return _map_fwd(name, _ln_fn, (SEQ // RB,), [x, mix, w, b], specs, [_sds((SEQ, D_MODEL))], [_rows(D_MODEL)])[0]


def ln_bwd(name, x, mix, w, b, dy):
    specs = [_rows(D_MODEL), _rows(D_MODEL), _full((1, D_MODEL)), _full((1, D_MODEL))]
    return _map_bwd(name, _ln_fn, (SEQ // RB,), [x, mix, w, b], specs, [[dy]], [[_rows(D_MODEL)]], want=[1, 2, 3],
                    acc=(2, 3))


def loss_call(y, tgt):
    def fn(yy, tt):
        e = yy - tt
        part = 0.5 * jnp.sum(jnp.sum(e * e, axis=-1, keepdims=True) * (1.0 / D_MODEL), axis=0, keepdims=True)
        return e * (1.0 / D_MODEL), jnp.broadcast_to(part, (8, 128))

    return _map_fwd("loss", fn, (SEQ // RB,), [y, tgt], [_rows(D_MODEL)] * 2,
                    [_sds((SEQ, D_MODEL)), _sds((SEQ // RB * 8, 128))],
                    [_rows(D_MODEL), pl.BlockSpec((8, 128), lambda i: (i, 0))])


LATE_KEYS = ("w_out", "w_up_t", "w_down")


def _full_rows(g):
    return g.reshape(N_DEV * g.shape[1], g.shape[2])


def layer_fwd(l, x, vfirst, wts, p, gather=(), late=False):
    sv = {"x": x}
    proj = _mm(f"mm_in{l}", x, wts["w_in"], "nn", *MM_TILES["k1024"])
    fl = lerp_fwd(l, proj, p["mu"])
    xc = conv_fwd(l, proj, p["conv_w"], p["conv_b"])
    w, k2, v2, c, b, g = rwkv_pre_fwd(l, fl, vfirst, p)
    y_scan, states, sv["gathered"] = rwkv_scan_fwd(l, fl, w, k2, v2, c, b, p, gather)
    if late:
        wts = dict(wts, **dict(zip(LATE_KEYS, [_full_rows(g) for g in sv["gathered"][:3]])))
    sv["wts"] = wts
    ya = rwkv_post_fwd(l, y_scan, fl, k2, v2, g, p)
    qkv = proj[:, C_AQ:C_AQ + 3 * DG]
    outs, lses = [], []
    for bi, (win, dil) in enumerate(DILATED):
        o, lse = attn_branch_fwd(l, bi, qkv, dil)
        outs.append(o)
        lses.append(lse)
    yb, lse_all = attn_merge(l, outs, lses)
    yc, ssd_states = ssd_fwd(l, proj, xc, p)
    yd, hg_states = hgrn_fwd(l, proj, p)
    ycat = jnp.concatenate([ya, yb, yc, yd], axis=1).astype(BF16)
    mix = _mm(f"mm_out{l}", ycat, wts["w_out"], "nn", *MM_TILES["k1024"])
    x1 = ln_fwd(f"ln1_fwd{l}", x, mix, p["ln1_w"], p["ln1_b"])
    hh = _mm(f"mm_up{l}", x1, wts["w_up_t"], "nt", *MM_TILES["k1024"], epilogue="relu2")
    m2 = _mm(f"mm_down{l}", hh, wts["w_down"], "nn", *MM_TILES["k4096"])
    x2 = ln_fwd(f"ln2_fwd{l}", x1, m2, p["ln2_w"], p["ln2_b"])
    sv.update(proj=proj, fl=fl, xc=xc, w=w, k2=k2, v2=v2, c=c, b=b, g=g, y_scan=y_scan, states=states,
              yb=yb, lse_all=lse_all, ssd_states=ssd_states, hg_states=hg_states, ycat=ycat, mix=mix, x1=x1, hh=hh, qkv=qkv,
              m2=m2, vfirst=vfirst)
    return x2, sv


def layer_bwd(l, dx2, dvfirst_next, sv, wts, p, exchange=(), reducer=None):
    gr = {}
    x, x1, proj, fl = sv["x"], sv["x1"], sv["proj"], sv["fl"]
    dres2, gr["ln2_w"], gr["ln2_b"] = ln_bwd(f"ln2_bwd{l}", x1, sv["m2"], p["ln2_w"], p["ln2_b"], dx2)
    du = _mm(f"mm_down_dx{l}", dres2, wts["w_down"], "nt", *MM_TILES["k1024"], add=sv["hh"], epilogue="relu2_bwd",
             out_dtype=BF16)
    gr["w_down"] = _mm(f"mm_down_dw{l}", sv["hh"], dres2, "tn", *MM_TILES["wgrad_tall"])
    dx1 = _mm(f"mm_up_dx{l}", du, wts["w_up_t"], "nn", *MM_TILES["k4096"], add=dres2, add_scale=ALPHA)
    gr["w_up_t"] = _mm(f"mm_up_dw{l}", du, x1, "tn", *MM_TILES["wgrad_tall"])
    dres1, gr["ln1_w"], gr["ln1_b"] = ln_bwd(f"ln1_bwd{l}", x, sv["mix"], p["ln1_w"], p["ln1_b"], dx1)
    dycat = _mm(f"mm_out_dx{l}", dres1, wts["w_out"], "nt", *MM_TILES["k1024"])
    gr["w_out"] = _mm(f"mm_out_dw{l}", sv["ycat"], dres1, "tn", 1024, 1024, 512)
    dya, dyb, dyc, dyd = (dycat[:, i * DG:(i + 1) * DG] for i in range(4))
    send = [_owner_blocks(gr[k]) for k in LATE_KEYS] if reducer else []
    (dhg4, gr["lb0"], gr["lb1"], gr["hgrn_norm_w"]), sib = hgrn_bwd(l, proj, sv["hg_states"], dyd, p, send)
    if reducer:
        gr["early_own"], early_parts = reducer(f"{l}a", send, sib)
        exchange = list(exchange) + list(early_parts)
    dz, dxc, ddt, gr["dt_bias"], gr["a_log"], gr["ssd_d"], gr["ssd_norm_w"] = ssd_bwd(l, proj, sv["xc"], sv["ssd_states"], dyc, p)
    dxbc, gr["conv_w"], gr["conv_b"] = conv_bwd(l, proj, p["conv_w"], p["conv_b"], dxc)
    delta = attn_delta(l, dyb, sv["yb"], p["seg64"])
    dqs, dks, dvs = [], [], []
    for bi, (win, dil) in enumerate(DILATED):
        dq, dk, dv = attn_branch_bwd(l, bi, sv["qkv"], dil, dyb, sv["lse_all"], delta)
        dqs.append(dq)
        dks.append(dk)
        dvs.append(dv)
    dq_a, dk_a, dv_a = _addn(f"attn_dq{l}", *dqs), _addn(f"attn_dk{l}", *dks), _addn(f"attn_dv{l}", *dvs)
    pg = rwkv_post_bwd(l, sv["y_scan"], fl, sv["k2"], sv["v2"], sv["g"], p, dya)
    gr["lnx_w"], gr["lnx_b"], gr["r_k"] = pg["lnx_w"], pg["lnx_b"], pg["r_k"]
    (dr, dw, dk, dv, dc, db), gr["exchanged"] = rwkv_scan_bwd(l, fl, sv["w"], sv["k2"], sv["v2"], sv["c"], sv["b"],
                                                              sv["states"], pg["y"], p, exchange)
    v2_cts = [dv, pg["v2"]] + ([dvfirst_next] if dvfirst_next is not None else [])
    qg = rwkv_pre_bwd(l, fl, sv["vfirst"], p, [[dw], [dk, pg["k2"]], v2_cts, [dc], [db], [pg["g"]]])
    for nme in ("w0", "w2p", "a0", "a2p", "g2p", "k_k", "k_a", "v0", "v2p"):
        if nme in qg:
            gr[nme] = qg[nme]
    dfr = _addn(f"rwkv_dr{l}", dr, pg["fr"])
    dvres = qg["fvres"] if l > 0 else jnp.zeros((SEQ, 128), F32)
    dfl_out = jnp.concatenate([dfr, qg["fk"], qg["fv"], qg["flora"], dvres], axis=1)
    dfl_in, gr["mu"] = lerp_bwd(l, proj, p["mu"], dfl_out)
    dproj = jnp.concatenate([dfl_in[:, 0:768], dq_a, dk_a, dv_a, dz, dxbc, dhg4, dfl_in[:, 768:896], ddt,
                             dfl_in[:, 896:1024], jnp.zeros((SEQ, 128), F32)], axis=1).astype(BF16)
    dx = _mm(f"mm_in_dx{l}", dproj, wts["w_in"], "nt", *MM_TILES["k4096"], add=dres1, add_scale=ALPHA)
    gr["w_in"] = _mm(f"mm_in_dw{l}", x, dproj, "tn", *MM_TILES["wgrad_wide"])
    return dx, (qg["vfirst"] if l > 0 else None), gr


def _w_in_pad(w_in_l, w_vres):
    rows = w_in_l.shape[0]
    z = lambda n: jnp.zeros((rows, n), w_in_l.dtype)
    vres = z(128) if w_vres is None else jnp.concatenate([w_vres, z(96)], axis=1)
    return jnp.concatenate([w_in_l[:, 0:768], w_in_l[:, 896:1664], w_in_l[:, 1664:1920], w_in_l[:, 1920:2688],
                            w_in_l[:, 2692:3716], w_in_l[:, 768:896], w_in_l[:, 2688:2692], z(124), vres, z(128)], axis=1)


def _w_in_unpad(g):
    g_in = jnp.concatenate([g[:, 0:768], g[:, C_LORA:C_LORA + 128], g[:, 768:1536], g[:, C_Z:C_Z + 256],
                            g[:, C_XBC:C_XBC + 768], g[:, C_DT:C_DT + 4], g[:, C_HQ:C_HQ + 1024]], axis=1)
    return g_in, g[:, C_VRES:C_VRES + 32]


def _consts():
    pair = jnp.arange(HG_NC * HGRN_CHUNK * HGRN_CHUNK)
    i128 = jnp.arange(128)
    same_chunk = (i128[:, None] // HGRN_CHUNK) == (i128[None, :] // HGRN_CHUNK)
    seg64 = _seg_ones(DG, HD)
    tri128 = (i128[:, None] >= i128[None, :]).astype(F32)
    return dict(
        seg64=seg64, seg64x3_bf16=jnp.concatenate([seg64, seg64, seg64], axis=0).astype(BF16),
        dmask=(jnp.arange(HD)[:, None] == (jnp.arange(DG)[None, :] % HD)).astype(F32),
        tri_bd128=(same_chunk & (i128[:, None] >= i128[None, :])).astype(F32), ones_bd128=same_chunk.astype(F32),
        first_row=(i128[None, :] == (jnp.arange(HG_NC) * HGRN_CHUNK)[:, None]).astype(F32),
        causal_blk=jnp.broadcast_to((((pair // HGRN_CHUNK) % HGRN_CHUNK) >= (pair % HGRN_CHUNK)).astype(F32)[:, None],
                                    (HG_NC * HGRN_CHUNK * HGRN_CHUNK, DG)),
        e128=((i128[:, None] == (jnp.arange(DG)[None, :] // HD)) & (i128[:, None] < NH)).astype(F32),
        tri128=tri128, tri128t=tri128.T, seg128=_seg_ones(DG, 128), ones128=jnp.ones((128, 128), F32))


def _pad_lanes(v, n):
    return jnp.concatenate([v, jnp.zeros((n - v.shape[0],), v.dtype)])[None, :]


def _layer_params(l, raw, consts):
    p = dict(consts)
    row = lambda name: raw[name][l][None, :]
    z = lambda r: jnp.zeros((r, DG), F32)
    mu_vres = raw["mu_vres"][l - 1] if l > 0 else jnp.zeros((32,), F32)
    p["mu"] = jnp.concatenate([raw["mu_shift"][l], mu_vres, jnp.zeros((96,), F32)])[None, :]
    p["conv_w"], p["conv_b"] = raw["ssd_conv_w"][l], row("ssd_conv_b")
    p["w0"], p["a0"], p["k_k"], p["k_a"] = row("rwkv_w0"), row("rwkv_a0"), row("rwkv_k_k"), row("rwkv_k_a")
    p["lnx_w"], p["lnx_b"] = row("rwkv_lnx_w"), row("rwkv_lnx_b")
    p["r_k"] = raw["rwkv_r_k"][l].reshape(1, DG)
    p["w2p"] = jnp.concatenate([raw["rwkv_w2"][l], z(96)], axis=0)
    p["a2p"] = jnp.concatenate([z(32), raw["rwkv_a2"][l], z(64)], axis=0)
    p["g2p"] = jnp.concatenate([z(64), raw["rwkv_g2"][l]], axis=0)
    if l > 0:
        p["v0"] = raw["rwkv_v0"][l - 1][None, :]
        p["v2p"] = jnp.concatenate([raw["rwkv_v2"][l - 1], z(96)], axis=0)
    p["lb0"], p["lb1"] = raw["lower_bounds"][0:1], raw["lower_bounds"][1:2]
    p["hgrn_norm_w"], p["ssd_norm_w"] = row("hgrn_norm_w"), row("ssd_norm_w")
    p["dt_bias"], p["a_log"], p["ssd_d"] = (_pad_lanes(raw[n][l], 128) for n in ("ssd_dt_bias", "ssd_A_log", "ssd_D"))
    for n in ("ln1_w", "ln1_b", "ln2_w", "ln2_b"):
        p[n] = row(n)
    return p


def _natural_grads(g0, g1):
    gs = (g0, g1)
    st = lambda key, f=lambda a: a[0]: jnp.stack([f(g[key]) for g in gs])
    out = {}
    out["lower_bounds"] = jnp.concatenate([g0["lb0"] + g1["lb0"], g0["lb1"] + g1["lb1"]], axis=0)
    out["mu_shift"] = st("mu", lambda a: a[0, :896])
    out["mu_vres"] = g1["mu"][:, 896:928]
    out["rwkv_w0"], out["rwkv_a0"], out["rwkv_k_k"], out["rwkv_k_a"] = st("w0"), st("a0"), st("k_k"), st("k_a")
    out["rwkv_w2"] = st("w2p", lambda a: a[0:32])
    out["rwkv_a2"] = st("a2p", lambda a: a[32:64])
    out["rwkv_g2"] = st("g2p", lambda a: a[64:128])
    out["rwkv_r_k"] = st("r_k", lambda a: a.reshape(NH, HD))
    out["rwkv_lnx_w"], out["rwkv_lnx_b"] = st("lnx_w"), st("lnx_b")
    out["rwkv_v0"] = g1["v0"]
    out["rwkv_v2"] = g1["v2p"][None, 0:32]
    out["ssd_conv_w"] = st("conv_w", lambda a: a)
    out["ssd_conv_b"] = st("conv_b")
    out["ssd_dt_bias"], out["ssd_A_log"], out["ssd_D"] = (st(k, lambda a: a[0, :NH]) for k in ("dt_bias", "a_log", "ssd_d"))
    out["ssd_norm_w"], out["hgrn_norm_w"] = st("ssd_norm_w"), st("hgrn_norm_w")
    for n in ("ln1_w", "ln1_b", "ln2_w", "ln2_b"):
        out[n] = st(n)
    return out


MESH_T = pl.DeviceIdType.MESH
ANY = pl.BlockSpec(memory_space=pl.ANY)


def _dev_index(px, py, pc):
    return 4 * px + 2 * py + pc


def all_gather(arrs):
    n = len(arrs)

    def body(*refs):
        begin, end = _gather_steps(refs[:n], refs[n:2 * n], *refs[2 * n:])
        begin()
        end()

    return pl.pallas_call(body, in_specs=[ANY] * n, out_specs=[ANY] * n, out_shape=_gather_shapes(arrs),
                          scratch_shapes=_gather_sems(n), name="all_gather")(*arrs)


def _gather_shapes(arrs):
    return [_sds((N_DEV,) + a.shape, a.dtype) for a in arrs]


def _gather_sems(n):
    return [pltpu.SemaphoreType.DMA((n, 7)), pltpu.SemaphoreType.DMA((n, 7)), pltpu.SemaphoreType.DMA((n,))]


def _gather_steps(ins, outs, send_sems, recv_sems, local_sems):
    n = len(ins)
    x, y, c = lax.axis_index("x"), lax.axis_index("y"), lax.axis_index("c")
    me, sibling = (x, y, c), (x, y, 1 - c)
    chips = [(1 - x, y), (x, 1 - y), (1 - x, 1 - y)]

    def copy(a, k, block, to, src=None):
        slot = outs[a].at[_dev_index(*block)]
        return pltpu.make_async_remote_copy(src_ref=slot if src is None else src, dst_ref=slot,
                                            send_sem=send_sems.at[a, k], recv_sem=recv_sems.at[a, k],
                                            device_id=to, device_id_type=MESH_T)

    def own_copies():
        mine = [pltpu.make_async_copy(ins[a], outs[a].at[_dev_index(*me)], local_sems.at[a]) for a in range(n)]
        first = []
        for a in range(n):
            first.append(copy(a, 0, me, sibling, src=ins[a]))
            first += [copy(a, 1 + j, me, (*chip, c), src=ins[a]) for j, chip in enumerate(chips)]
        return mine, first

    def begin():
        mine, first = own_copies()
        for cp in mine + first:
            cp.start()

    def end():
        mine, first = own_copies()
        passed = []
        for j, chip in enumerate(chips):
            for a in range(n):
                copy(a, 1 + j, (*chip, c), me).wait_recv()
                fwd = copy(a, 4 + j, (*chip, c), sibling)
                fwd.start()
                passed.append(fwd)
        for a in range(n):
            copy(a, 0, sibling, me).wait_recv()
            for j, chip in enumerate(chips):
                copy(a, 4 + j, (*chip, 1 - c), me).wait_recv()
        for cp in first + passed:
            cp.wait_send()
        for cp in mine:
            cp.wait()

    return begin, end


def _chips(x, y):
    return [(x, y), (1 - x, y), (x, 1 - y), (1 - x, 1 - y)]


def _sibling_steps(ins, sib, send_sems, recv_sems):
    x, y, c = lax.axis_index("x"), lax.axis_index("y"), lax.axis_index("c")

    def copies():
        return [pltpu.make_async_remote_copy(src_ref=ins[a].at[_dev_index(cx, cy, 1 - c)], dst_ref=sib[a].at[k],
                                             send_sem=send_sems.at[a, k], recv_sem=recv_sems.at[a, k],
                                             device_id=(x, y, 1 - c), device_id_type=MESH_T)
                for a in range(len(ins)) for k, (cx, cy) in enumerate(_chips(x, y))]

    def begin():
        for cp in copies():
            cp.start()

    def end():
        cps = copies()
        for cp in cps:
            cp.wait_recv()
        for cp in cps:
            cp.wait_send()

    return begin, end


def exchange_siblings(arrs, name):
    n = len(arrs)

    def body(*refs):
        begin, end = _sibling_steps(refs[:n], refs[n:2 * n], *refs[2 * n:])
        begin()
        end()

    sem = pltpu.SemaphoreType.DMA((n, 4))
    return pl.pallas_call(body, in_specs=[ANY] * n, out_specs=[ANY] * n,
                          out_shape=[_sds((4,) + a.shape[1:], a.dtype) for a in arrs],
                          scratch_shapes=[sem, sem], name=name)(*arrs)


def reduce_pair(name, send, slots, sib, wire_dtype):
    _, r, c = send.shape
    rb = min(r, 262144 // c)

    def body(slots_ref, m0, m1, m2, m3, s_ref, own_ref, part_ref):
        own_ref[...] = m0[...] + s_ref[0]
        for k, m_ref in enumerate((m1, m2, m3)):
            part_ref[k] = (m_ref[...] + s_ref[k + 1]).astype(wire_dtype)

    mine = [pl.BlockSpec((None, rb, c), lambda i, s, k=k: (s[k], i, 0)) for k in range(4)]
    grid_spec = pltpu.PrefetchScalarGridSpec(
        num_scalar_prefetch=1, grid=(r // rb,),
        in_specs=mine + [pl.BlockSpec((4, rb, c), lambda i, s: (0, i, 0))],
        out_specs=[pl.BlockSpec((rb, c), lambda i, s: (i, 0)), pl.BlockSpec((3, rb, c), lambda i, s: (0, i, 0))])
    return pl.pallas_call(body, grid_spec=grid_spec, out_shape=[_sds((r, c)), _sds((3, r, c), wire_dtype)], name=name,
                          compiler_params=_cp(("parallel",)))(slots, send, send, send, send, sib)


def _chip_exchange_steps(ins, recv, send_sems, recv_sems):
    x, y, c = lax.axis_index("x"), lax.axis_index("y"), lax.axis_index("c")

    def copies():
        return [pltpu.make_async_remote_copy(src_ref=ins[a].at[k], dst_ref=recv[a].at[k], send_sem=send_sems.at[a, k],
                                             recv_sem=recv_sems.at[a, k], device_id=(cx, cy, c), device_id_type=MESH_T)
                for a in range(len(ins)) for k, (cx, cy) in enumerate(_chips(x, y)[1:])]

    def begin():
        for cp in copies():
            cp.start()

    def end():
        cps = copies()
        for cp in cps:
            cp.wait_recv()
        for cp in cps:
            cp.wait_send()

    return begin, end


def exchange_chips(parts, rep):
    n = len(parts)

    def body(*refs):
        ins, rep_ref = refs[:n], refs[n]
        recv, rep_all = refs[n + 1:2 * n + 1], refs[2 * n + 1]
        send_sems, recv_sems, rsend_sems, rrecv_sems, local_sem = refs[2 * n + 2:]
        x, y, c = lax.axis_index("x"), lax.axis_index("y"), lax.axis_index("c")
        me = _dev_index(x, y, c)
        mine = pltpu.make_async_copy(rep_ref, rep_all.at[me], local_sem)
        mine.start()
        begin, end = _chip_exchange_steps(ins, recv, send_sems, recv_sems)
        begin()
        rels = [(rx, ry, rc) for rx in (0, 1) for ry in (0, 1) for rc in (0, 1)][1:]
        peers = [(jnp.where(rx, 1 - x, x), jnp.where(ry, 1 - y, y), jnp.where(rc, 1 - c, c)) for rx, ry, rc in rels]
        rcps = []
        for k, peer in enumerate(peers):
            cp = pltpu.make_async_remote_copy(src_ref=rep_ref, dst_ref=rep_all.at[me], send_sem=rsend_sems.at[k],
                                              recv_sem=rrecv_sems.at[k], device_id=peer, device_id_type=MESH_T)
            cp.start()
            rcps.append(cp)
        for k, peer in enumerate(peers):
            pltpu.make_async_remote_copy(src_ref=rep_ref, dst_ref=rep_all.at[_dev_index(*peer)], send_sem=rsend_sems.at[k],
                                         recv_sem=rrecv_sems.at[k], device_id=peer, device_id_type=MESH_T).wait_recv()
        end()
        for cp in rcps:
            cp.wait_send()
        mine.wait()

    outs = pl.pallas_call(
        body, in_specs=[ANY] * (n + 1), out_specs=[ANY] * (n + 1),
        out_shape=[_sds(a.shape, a.dtype) for a in parts] + [_sds((N_DEV,) + rep.shape, rep.dtype)],
        scratch_shapes=[pltpu.SemaphoreType.DMA((n, 3)), pltpu.SemaphoreType.DMA((n, 3)), pltpu.SemaphoreType.DMA((7,)),
                        pltpu.SemaphoreType.DMA((7,)), pltpu.SemaphoreType.DMA],
        name="exchange_chips")(*parts, rep)
    return outs[:n], outs[n]


def adamw(name, terms, w, m, v, transposed=False):
    r, c = w.shape[::-1] if transposed else w.shape
    rb = r if transposed else min(r, 262144 // c)
    c1 = 1.0 - ADAM_B1 ** ADAM_STEP
    c2 = 1.0 - ADAM_B2 ** ADAM_STEP
    nt = len(terms)

    def body(*refs):
        w_ref, m_ref, v_ref = refs[nt:nt + 3]
        g_ref, d_ref, nm_ref, nv_ref = refs[nt + 3:]
        g = refs[0][...].astype(F32)
        for t_ref in refs[1:nt]:
            g = g + t_ref[...].astype(F32)
        if transposed:
            g = g.T
        nm = ADAM_B1 * m_ref[...] + (1.0 - ADAM_B1) * g
        nv = ADAM_B2 * v_ref[...] + (1.0 - ADAM_B2) * (g * g)
        g_ref[...] = g
        nm_ref[...] = nm
        nv_ref[...] = nv
        d_ref[...] = -ADAM_LR * ((nm / c1) / (jnp.sqrt(nv / c2) + ADAM_EPS) + ADAM_WD * w_ref[...])

    blk = pl.BlockSpec((rb, c), lambda i: (i, 0))
    wblk = pl.BlockSpec((c, r), lambda i: (0, 0)) if transposed else blk
    tspecs = [blk if k is None else pl.BlockSpec((None, rb, c), lambda i, k=k: (k, i, 0)) for _, k in terms]
    return pl.pallas_call(body, grid=(r // rb,), in_specs=tspecs + [wblk] * 3, out_specs=[wblk] * 4,
                          out_shape=[_sds(w.shape)] * 4, name=name,
                          compiler_params=_cp(("parallel",)))(*[t for t, _ in terms], w, m, v)


SMS_ROWS = 16
REP_ROWS = 24
N_BIG = 8
SMALL_SHARDED = (("rwkv_w2", (2, 32, 32)), ("rwkv_a2", (2, 32, 32)), ("rwkv_g2", (2, 64, 32)), ("rwkv_v2", (1, 32, 32)),
                 ("ssd_conv_w", (2, 4, 96)))
REPLICATED = (("lower_bounds", (2, 256)), ("mu_shift", (2, 896)), ("mu_vres", (1, 32)), ("rwkv_w0", (2, 256)),
              ("rwkv_a0", (2, 256)), ("rwkv_k_k", (2, 256)), ("rwkv_k_a", (2, 256)), ("rwkv_r_k", (2, 4, 64)),
              ("rwkv_lnx_w", (2, 256)), ("rwkv_lnx_b", (2, 256)), ("rwkv_v0", (1, 256)), ("ssd_conv_b", (2, 768)),
              ("ssd_dt_bias", (2, 4)), ("ssd_A_log", (2, 4)), ("ssd_D", (2, 4)), ("ssd_norm_w", (2, 256)),
              ("hgrn_norm_w", (2, 256)), ("ln1_w", (2, 1024)), ("ln1_b", (2, 1024)), ("ln2_w", (2, 1024)),
              ("ln2_b", (2, 1024)))


def _flat_rows(parts, rows):
    flat = jnp.concatenate([a.reshape(-1) for a in parts])
    return jnp.concatenate([flat, jnp.zeros((rows * PACK_W - flat.shape[0],), flat.dtype)]).reshape(rows, PACK_W)


def _local_arrays(d):
    arrs = [_w_in_pad(d["w_in"][0], None), _w_in_pad(d["w_in"][1], d["w_in_vres"][0]), d["w_out"][0], d["w_out"][1],
            d["w_up"][0], d["w_up"][1], d["w_down"][0], d["w_down"][1],
            _flat_rows([d[n] for n, _ in SMALL_SHARDED], SMS_ROWS)]
    return arrs, _flat_rows([d[n] for n, _ in REPLICATED], REP_ROWS)


def _unflat(rows2d, table):
    flat, out, o = rows2d.reshape(-1), {}, 0
    for name, shape in table:
        n = 1
        for s in shape:
            n *= s
        out[name] = flat[o:o + n].reshape(shape)
        o += n
    return out


def _from_local_arrays(arrs, rep):
    d = {}
    g0, _ = _w_in_unpad(arrs[0])
    g1, gv = _w_in_unpad(arrs[1])
    d["w_in"], d["w_in_vres"] = jnp.stack([g0, g1]), gv[None]
    d["w_out"] = jnp.stack([arrs[2], arrs[3]])
    d["w_up"] = jnp.stack([arrs[4], arrs[5]])
    d["w_down"] = jnp.stack([arrs[6], arrs[7]])
    d.update(_unflat(arrs[8], SMALL_SHARDED))
    d.update(_unflat(rep, REPLICATED))
    return d


def _small_sharded_full(gs):
    small, flat, o = {}, gs.reshape(N_DEV, -1), 0
    for name, shape in SMALL_SHARDED:
        n = shape[0] * shape[1] * shape[2]
        blk = flat[:, o:o + n].reshape((N_DEV,) + shape)
        small[name] = blk.transpose(1, 2, 0, 3).reshape(shape[0], shape[1], N_DEV * shape[2])
        o += n
    return small


def _owner_blocks(g):
    return g.reshape(N_DEV, g.shape[0] // N_DEV, g.shape[1])


def _small_send_arrays(small_grads):
    sms = []
    for name, shape in SMALL_SHARDED:
        g = small_grads[name].reshape(shape[0], shape[1], N_DEV, shape[2]).transpose(2, 0, 1, 3)
        sms.append(g.reshape(N_DEV, -1))
    sms = jnp.concatenate(sms, axis=1)
    sms = jnp.concatenate([sms, jnp.zeros((N_DEV, SMS_ROWS * PACK_W - sms.shape[1]), F32)], axis=1)
    return sms.reshape(N_DEV, SMS_ROWS, PACK_W), _flat_rows([small_grads[n] for n, _ in REPLICATED], REP_ROWS)


BIG_KEYS = ("w_in", "w_out", "w_up_t", "w_down")


def _weights_of(full):
    return dict(zip(BIG_KEYS, full))


def _local_step(x, tgt, wts, raw, gather=(), pair_sums=None, reducer=None):
    consts = _consts()
    ps = [_layer_params(l, raw, consts) for l in range(DEPTH)]
    x1, sv0 = layer_fwd(0, x, None, wts[0], ps[0], gather[:4], late=bool(gather))
    wts1 = {"w_in": _full_rows(sv0["gathered"][3])} if gather else wts[1]
    x2, sv1 = layer_fwd(1, x1, sv0["fl"], wts1, ps[1], gather[4:], late=bool(gather))
    dy, lparts = loss_call(x2, tgt)
    loss = jnp.sum(lparts[::8, 0])
    dx1, dvfirst, g1 = layer_bwd(1, dy, None, sv1, sv1["wts"], ps[1], (), reducer)
    big1 = {k: g1[k] for k in BIG_KEYS}
    if reducer is None:
        dx0, _, g0 = layer_bwd(0, dx1, dvfirst, sv0, sv0["wts"], ps[0])
        early = None
    else:
        own_in1, parts_in1 = pair_sums("1b", {"w_in": g1["w_in"]})
        dx0, _, g0 = layer_bwd(0, dx1, dvfirst, sv0, sv0["wts"], ps[0], parts_in1, reducer)
        own, recv = {(1, "w_in"): own_in1[0]}, {(1, "w_in"): g0["exchanged"][0]}
        for l, g, first in ((1, g1, 0), (0, g0, 1)):
            for i, k in enumerate(LATE_KEYS):
                own[(l, k)], recv[(l, k)] = g["early_own"][i], g["exchanged"][first + i]
        early = (own, recv)
    big = [{k: g0[k] for k in BIG_KEYS}, big1]
    return loss, dx0, big, _natural_grads(g0, g1), early


WEIGHT_NAMES = ("lower_bounds", "w_in", "w_in_vres", "mu_shift", "mu_vres", "rwkv_w0", "rwkv_w2", "rwkv_a0", "rwkv_a2",
                "rwkv_g2", "rwkv_k_k", "rwkv_k_a", "rwkv_r_k", "rwkv_lnx_w", "rwkv_lnx_b", "rwkv_v0", "rwkv_v2",
                "ssd_conv_w", "ssd_conv_b", "ssd_dt_bias", "ssd_A_log", "ssd_D", "ssd_norm_w", "hgrn_norm_w", "w_out",
                "ln1_w", "ln1_b", "w_up", "w_down", "ln2_w", "ln2_b")


def kernel(x, lower_bounds, w_in, w_in_vres, mu_shift, mu_vres, rwkv_w0, rwkv_w2, rwkv_a0, rwkv_a2, rwkv_g2, rwkv_k_k, rwkv_k_a, rwkv_r_k, rwkv_lnx_w, rwkv_lnx_b, rwkv_v0, rwkv_v2, ssd_conv_w, ssd_conv_b, ssd_dt_bias, ssd_A_log, ssd_D, ssd_norm_w, hgrn_norm_w, w_out, ln1_w, ln1_b, w_up, w_down, ln2_w, ln2_b, loss_target, m_lower_bounds, m_w_in, m_w_in_vres, m_mu_shift, m_mu_vres, m_rwkv_w0, m_rwkv_w2, m_rwkv_a0, m_rwkv_a2, m_rwkv_g2, m_rwkv_k_k, m_rwkv_k_a, m_rwkv_r_k, m_rwkv_lnx_w, m_rwkv_lnx_b, m_rwkv_v0, m_rwkv_v2, m_ssd_conv_w, m_ssd_conv_b, m_ssd_dt_bias, m_ssd_A_log, m_ssd_D, m_ssd_norm_w, m_hgrn_norm_w, m_w_out, m_ln1_w, m_ln1_b, m_w_up, m_w_down, m_ln2_w, m_ln2_b, v_lower_bounds, v_w_in, v_w_in_vres, v_mu_shift, v_mu_vres, v_rwkv_w0, v_rwkv_w2, v_rwkv_a0, v_rwkv_a2, v_rwkv_g2, v_rwkv_k_k, v_rwkv_k_a, v_rwkv_r_k, v_rwkv_lnx_w, v_rwkv_lnx_b, v_rwkv_v0, v_rwkv_v2, v_ssd_conv_w, v_ssd_conv_b, v_ssd_dt_bias, v_ssd_A_log, v_ssd_D, v_ssd_norm_w, v_hgrn_norm_w, v_w_out, v_ln1_w, v_ln1_b, v_w_up, v_w_down, v_ln2_w, v_ln2_b):
    given = dict(locals())
    w = {n: given[n] for n in WEIGHT_NAMES}
    w_arrs, w_rep = _local_arrays(w)
    m_arrs, m_rep = _local_arrays({n: given["m_" + n] for n in WEIGHT_NAMES})
    v_arrs, v_rep = _local_arrays({n: given["v_" + n] for n in WEIGHT_NAMES})
    wire = lambda a: (w_arrs[a].T if a in (4, 5) else w_arrs[a]).astype(BF16)
    gathered0 = all_gather([wire(0), w_arrs[N_BIG]])
    raw = {n: w[n] for n, _ in REPLICATED}
    raw.update(_small_sharded_full(gathered0[1]))
    mx, my, mc = lax.axis_index("x"), lax.axis_index("y"), lax.axis_index("c")
    slots = jnp.stack([_dev_index(cx, cy, mc) for cx, cy in _chips(mx, my)]).astype(jnp.int32)

    def reducer(tag, send, sib, n_f32=0):
        wire_dt = [BF16] * (len(send) - n_f32) + [F32] * n_f32
        res = [reduce_pair(f"reduce_pair{tag}_{i}", s, slots, sb, dt) for i, (s, sb, dt) in enumerate(zip(send, sib, wire_dt))]
        return [o for o, _ in res], [pt for _, pt in res]

    def pair_sums(tag, grads, extra=()):
        send = [_owner_blocks(g) for g in grads.values()] + list(extra)
        return reducer(tag, send, exchange_siblings(send, f"exchange_siblings{tag}"), len(extra))

    behind_scan = [wire(a) for a in (2, 4, 6, 1, 3, 5, 7)]
    loss, dx, big, small_grads, (own_by, recv_by) = _local_step(
        x[0], loss_target[0], [{"w_in": _full_rows(gathered0[0])}, None], raw, behind_scan, pair_sums, reducer)
    sms_send, rep = _small_send_arrays(small_grads)
    own0b, parts0b = pair_sums("0b", {"w_in": big[0]["w_in"]}, [sms_send])
    recv0b, rep_all = exchange_chips(parts0b, rep)
    own, recv = [None] * (N_BIG + 1), [None] * (N_BIG + 1)
    for (l, k), o in own_by.items():
        a = 2 * BIG_KEYS.index(k) + l
        own[a], recv[a] = o, recv_by[(l, k)]
    for a, o, r in zip((0, N_BIG), own0b, recv0b):
        own[a], recv[a] = o, r
    results = [adamw(f"adamw{a}", [(own[a], None), (recv[a], 0), (recv[a], 1), (recv[a], 2)], w_arrs[a], m_arrs[a], v_arrs[a],
                     transposed=a in (4, 5)) for a in range(N_BIG + 1)]
    rep_res = adamw("adamw_rep", [(rep_all, q) for q in range(N_DEV)], w_rep, m_rep, v_rep)
    loss = lax.psum(loss, ("x", "y", "c"))
    outs = [loss, dx[None]]
    for q in range(4):
        d = _from_local_arrays([res[q] for res in results], rep_res[q])
        outs += [d[n] for n in WEIGHT_NAMES]
    return tuple(outs)
```

```python
import functools

import jax
import jax.numpy as jnp
from jax import lax
from jax.experimental import pallas as pl
from jax.experimental.pallas import tpu as pltpu

F32 = jnp.float32
BF16 = jnp.bfloat16
HI = lax.Precision.HIGHEST

N_DEV = 8
SEQ = 2048
D_MODEL = 1024
D_FF = 4096
DG = 256
NH = 4
HD = 64
DEPTH = 2
ALPHA = (2.0 * DEPTH) ** 0.25
LN_EPS = 1e-5
RMS_EPS = 1e-5
GN_EPS = HD * 1e-5
IN_COLS = 3716
SSD_N = 128
SSD_CHUNK = 128
HGRN_CHUNK = 16
DILATED = ((128, 1), (512, 4), (2048, 16))

ADAM_LR, ADAM_B1, ADAM_B2, ADAM_EPS, ADAM_WD, ADAM_STEP = 0.001, 0.9, 0.999, 1e-08, 0.01, 10

PW = 4096
C_R, C_K, C_V = 0, 256, 512
C_AQ, C_AK, C_AV = 768, 1024, 1280
C_Z, C_XBC = 1536, 1792
C_HQ, C_HF, C_HI, C_HG = 2560, 2816, 3072, 3328
C_LORA, C_DT, C_VRES = 3584, 3712, 3840

RB = 256
VMEM_LIMIT = 56 * 1024 * 1024
PACK_W = 1024


def _cp(sem=None):
    return pltpu.CompilerParams(dimension_semantics=sem, vmem_limit_bytes=VMEM_LIMIT)


def _sds(shape, dt=F32):
    return jax.ShapeDtypeStruct(tuple(shape), dt)


def _rows(w, cb=0, rb=RB):
    return pl.BlockSpec((rb, w), lambda i: (i, cb))


def _full(shape):
    n = len(shape)
    return pl.BlockSpec(tuple(shape), lambda *_: (0,) * n)


def _sigmoid(x):
    return 1.0 / (1.0 + jnp.exp(-x))


def _silu(x):
    return x * _sigmoid(x)


def _softplus(x):
    return jnp.maximum(x, 0.0) + jnp.log(1.0 + jnp.exp(jnp.where(x > 0, -x, x)))


MID = lax.Precision.HIGH
NN, TN, NT = (((1,), (0,)), ((), ())), (((0,), (0,)), ((), ())), (((1,), (1,)), ((), ()))


def _dot(a, b):
    return lax.dot_general(a, b, NN, precision=MID, preferred_element_type=F32)


def _dot_tn(a, b):
    return lax.dot_general(a, b, TN, precision=MID, preferred_element_type=F32)


def _dot_nt(a, b):
    return lax.dot_general(a, b, NT, precision=MID, preferred_element_type=F32)


def _dotx(a, b):
    return lax.dot_general(a, b, NN, precision=HI, preferred_element_type=F32)


def _dotx_tn(a, b):
    return lax.dot_general(a, b, TN, precision=HI, preferred_element_type=F32)


def _seg_ones(n, seg):
    i = jnp.arange(n)
    return (i[:, None] // seg == i[None, :] // seg).astype(F32)


def _shift_down(x, s):
    row = lax.broadcasted_iota(jnp.int32, x.shape, 0)
    return jnp.where(row < s, 0.0, pltpu.roll(x, s, 0))


def _shift_up(x, s):
    n = x.shape[0]
    row = lax.broadcasted_iota(jnp.int32, x.shape, 0)
    return jnp.where(row >= n - s, 0.0, pltpu.roll(x, n - s, 0))


@functools.partial(jax.custom_vjp, nondiff_argnums=(1,))
def _tshift(x, s):
    return _shift_down(x, s)


def _tshift_fwd(x, s):
    return _shift_down(x, s), None


def _tshift_bwd(s, _, g):
    return (_shift_up(g, s),)


_tshift.defvjp(_tshift_fwd, _tshift_bwd)


def _map_fwd(name, fn, grid, ins, in_specs, out_shapes, out_specs):
    n_in = len(ins)

    def body(*refs):
        ys = fn(*[r[...] for r in refs[:n_in]])
        for r, y in zip(refs[n_in:], ys):
            r[...] = y

    return pl.pallas_call(body, grid=grid, in_specs=in_specs, out_specs=out_specs, out_shape=out_shapes,
                          name=name, compiler_params=_cp(("parallel",)))(*ins)


def _map_bwd(name, fn, grid, ins, in_specs, cts, ct_specs, want, acc=(), gout=None):
    n_in = len(ins)
    flat_cts = [c for group in cts for c in group]
    flat_specs = [s for group in ct_specs for s in group]
    n_ct = len(flat_cts)
    gout = gout or {}
    out_shapes = [gout[i][0] if i in gout else _sds(ins[i].shape) for i in want]
    out_specs = [gout[i][1] if i in gout else in_specs[i] for i in want]

    def body(*refs):
        xs = [r[...] for r in refs[:n_in]]
        cvals = [r[...] for r in refs[n_in:n_in + n_ct]]
        gouts = refs[n_in + n_ct:]
        cs, p = [], 0
        for group in cts:
            v = cvals[p]
            for q in range(1, len(group)):
                v = v + cvals[p + q]
            cs.append(v)
            p += len(group)

        def f(*wanted):
            full = list(xs)
            for i, w in zip(want, wanted):
                full[i] = w
            return tuple(fn(*full))

        _, vjp = jax.vjp(f, *[xs[i] for i in want])
        gs = vjp(tuple(cs))
        for o, i, g in zip(gouts, want, gs):
            if i in acc:
                @pl.when(pl.program_id(0) == 0)
                def _():
                    o[...] = jnp.zeros_like(o)

                o[...] += g
            else:
                o[...] = g

    sem = ("arbitrary",) if acc else ("parallel",)
    return pl.pallas_call(body, grid=grid, in_specs=list(in_specs) + flat_specs, out_specs=out_specs,
                          out_shape=out_shapes, name=name, compiler_params=_cp(sem))(*ins, *flat_cts)


def _addn(name, *arrs):
    n, c = arrs[0].shape

    def fn(*xs):
        r = xs[0]
        for x in xs[1:]:
            r = r + x
        return (r,)

    return _map_fwd(name, fn, (n // RB,), list(arrs), [_rows(c)] * len(arrs), [_sds((n, c))], [_rows(c)])[0]


MM_TILES = {"k1024": (2048, 512, 1024), "k4096": (1024, 1024, 1024), "wgrad_tall": (2048, 1024, 512),
            "wgrad_wide": (1024, 2048, 512)}


def _mm(name, a, b, mode, tm, tn, tk, add=None, add_scale=1.0, epilogue=None, out_dtype=F32):
    if mode == "nn":
        (m, k), n = a.shape, b.shape[1]
    elif mode == "nt":
        (m, k), n = a.shape, b.shape[0]
    else:
        (k, m), n = a.shape, b.shape[1]
    nk = k // tk
    dn = {"nn": (((1,), (0,)), ((), ())), "nt": (((1,), (1,)), ((), ())), "tn": (((0,), (0,)), ((), ()))}[mode]

    def body(*refs):
        a_ref, b_ref = refs[:2]
        add_ref = refs[2] if add is not None else None
        o_ref = refs[3] if add is not None else refs[2]
        prod = lax.dot_general(a_ref[...].astype(BF16), b_ref[...].astype(BF16), dn, preferred_element_type=F32)

        def finish(r):
            if epilogue == "relu2":
                r = jnp.maximum(r, 0.0)
                r = r * r
            elif epilogue == "relu2_bwd":
                r = r * (2.0 * jnp.sqrt(add_ref[...]))
            elif add is not None:
                r = r + add_scale * add_ref[...]
            o_ref[...] = r.astype(out_dtype)

        if nk == 1:
            finish(prod)
        else:
            acc = refs[-1]
            kk = pl.program_id(2)

            @pl.when(kk == 0)
            def _():
                acc[...] = prod

            @pl.when(kk > 0)
            def _():
                acc[...] += prod

            @pl.when(kk == nk - 1)
            def _():
                finish(acc[...])

    a_spec = pl.BlockSpec((tk, tm), lambda i, j, q: (q, i)) if mode == "tn" else pl.BlockSpec((tm, tk), lambda i, j, q: (i, q))
    b_spec = pl.BlockSpec((tn, tk), lambda i, j, q: (j, q)) if mode == "nt" else pl.BlockSpec((tk, tn), lambda i, j, q: (q, j))
    o_spec = pl.BlockSpec((tm, tn), lambda i, j, q: (i, j))
    ins, specs = [a, b], [a_spec, b_spec]
    if add is not None:
        ins.append(add)
        specs.append(o_spec)
    return pl.pallas_call(body, grid=(m // tm, n // tn, nk), in_specs=specs, out_specs=o_spec,
                          out_shape=_sds((m, n), out_dtype),
                          scratch_shapes=[pltpu.VMEM((tm, tn), F32)] if nk > 1 else [], name=name,
                          compiler_params=_cp(("parallel", "parallel", "arbitrary")))(*ins)


LERP_BLOCKS = (0, 1, 2, 3, 4, 5, C_LORA // 128, C_VRES // 128)


def _lerp_colmap(j):
    r = jnp.where(j < 6, j, jnp.where(j == 6, C_LORA // 128, C_VRES // 128))
    return (0, r)


def _lerp_fn(f, mu):
    return (f + (_tshift(f, 1) - f) * mu,)


def _lerp_specs():
    return [pl.BlockSpec((SEQ, 128), _lerp_colmap), pl.BlockSpec((1, 128), lambda j: (0, j))]


def lerp_fwd(l, proj, mu):
    return _map_fwd(f"lerp_fwd{l}", _lerp_fn, (8,), [proj, mu], _lerp_specs(), [_sds((SEQ, 1024))],
                    [pl.BlockSpec((SEQ, 128), lambda j: (0, j))])[0]


def lerp_bwd(l, proj, mu, dfl):
    n_in = 2

    def body(f_ref, mu_ref, g_ref, df_ref, dmu_ref):
        _, vjp = jax.vjp(_lerp_fn, f_ref[...], mu_ref[...])
        df, dmu = vjp((g_ref[...],))
        df_ref[...] = df
        dmu_ref[...] = dmu

    cspec = pl.BlockSpec((SEQ, 128), lambda j: (0, j))
    return pl.pallas_call(body, grid=(8,), in_specs=_lerp_specs() + [cspec],
                          out_specs=[cspec, pl.BlockSpec((1, 128), lambda j: (0, j))],
                          out_shape=[_sds((SEQ, 1024)), _sds((1, 1024))], name=f"lerp_bwd{l}",
                          compiler_params=_cp(("parallel",)))(proj, mu, dfl)


def _conv_fn(x, w, b):
    y = x * w[3:4, :] + _tshift(x, 1) * w[2:3, :] + _tshift(x, 2) * w[1:2, :] + _tshift(x, 3) * w[0:1, :] + b
    return (_silu(y),)


def _conv_specs():
    return [pl.BlockSpec((SEQ, 128), lambda j: (0, C_XBC // 128 + j)), pl.BlockSpec((4, 128), lambda j: (0, j)),
            pl.BlockSpec((1, 128), lambda j: (0, j))]


def conv_fwd(l, proj, w, b):
    return _map_fwd(f"conv_fwd{l}", _conv_fn, (6,), [proj, w, b], _conv_specs(), [_sds((SEQ, 768))],
                    [pl.BlockSpec((SEQ, 128), lambda j: (0, j))])[0]


def conv_bwd(l, proj, w, b, dxc):
    def body(x_ref, w_ref, b_ref, g_ref, dx_ref, dw_ref, db_ref):
        _, vjp = jax.vjp(_conv_fn, x_ref[...], w_ref[...], b_ref[...])
        dx, dw, db = vjp((g_ref[...],))
        dx_ref[...] = dx
        dw_ref[...] = dw
        db_ref[...] = db

    cspec = pl.BlockSpec((SEQ, 128), lambda j: (0, j))
    return pl.pallas_call(body, grid=(6,), in_specs=_conv_specs() + [cspec],
                          out_specs=[cspec, pl.BlockSpec((4, 128), lambda j: (0, j)), pl.BlockSpec((1, 128), lambda j: (0, j))],
                          out_shape=[_sds((SEQ, 768)), _sds((4, 768)), _sds((1, 768))], name=f"conv_bwd{l}",
                          compiler_params=_cp(("parallel",)))(proj, w, b, dxc)


def _rwkv_pre_fn(has_vres):
    def fn(fk, fv, flora, *rest):
        if has_vres:
            fvres, vfirst, w0, w2p, a0, a2p, g2p, k_k, k_a, v0, v2p, seg = rest
        else:
            w0, w2p, a0, a2p, g2p, k_k, k_a, seg = rest
        w_log = -_softplus(-(w0 + _dot(jnp.tanh(flora), w2p))) - 0.5
        w = jnp.exp(-jnp.exp(w_log))
        a = _sigmoid(a0 + _dot(flora, a2p))
        g = _dot(_sigmoid(flora), g2p)
        if has_vres:
            v2 = fv + (vfirst - fv) * _sigmoid(v0 + _dot(fvres, v2p))
        else:
            v2 = fv * 1.0
        kk = fk * k_k
        kk = kk / jnp.maximum(jnp.sqrt(_dot(kk * kk, seg)), 1e-12)
        k2 = fk * (1.0 + (a - 1.0) * k_a)
        return w, k2, v2, -kk, kk * a, g

    return fn


def _rwkv_pre_args(fl, vfirst, p, has_vres):
    ins = [fl, fl, fl]
    specs = [_rows(256, 1), _rows(256, 2), _rows(128, 6)]
    if has_vres:
        ins += [fl, vfirst]
        specs += [_rows(128, 7), _rows(256, 2)]
    names = ["w0", "w2p", "a0", "a2p", "g2p", "k_k", "k_a"] + (["v0", "v2p"] if has_vres else []) + ["seg64"]
    for nme in names:
        ins.append(p[nme])
        specs.append(_full(p[nme].shape))
    return ins, specs, names


def rwkv_pre_fwd(l, fl, vfirst, p):
    has_vres = l > 0
    ins, specs, _ = _rwkv_pre_args(fl, vfirst, p, has_vres)
    return _map_fwd(f"rwkv_pre_fwd{l}", _rwkv_pre_fn(has_vres), (SEQ // RB,), ins, specs,
                    [_sds((SEQ, DG))] * 6, [_rows(DG)] * 6)


def rwkv_pre_bwd(l, fl, vfirst, p, cts):
    has_vres = l > 0
    ins, specs, names = _rwkv_pre_args(fl, vfirst, p, has_vres)
    n_row = 5 if has_vres else 3
    want = list(range(n_row)) + [n_row + i for i, nme in enumerate(names) if nme != "seg64"]
    acc = tuple(w for w in want if w >= n_row)
    ct_specs = [[_rows(DG)] * len(g) for g in cts]
    gout = {0: (_sds((SEQ, DG)), _rows(DG)), 1: (_sds((SEQ, DG)), _rows(DG)), 2: (_sds((SEQ, 128)), _rows(128))}
    if has_vres:
        gout[3] = (_sds((SEQ, 128)), _rows(128))
        gout[4] = (_sds((SEQ, DG)), _rows(DG))
    gs = _map_bwd(f"rwkv_pre_bwd{l}", _rwkv_pre_fn(has_vres), (SEQ // RB,), ins, specs, cts, ct_specs, want, acc, gout)
    keys = ["fk", "fv", "flora"] + (["fvres", "vfirst"] if has_vres else []) + [nme for nme in names if nme != "seg64"]
    return dict(zip(keys, gs))


def _rwkv_post_fn(y, fr, k2, v2, g, lnx_w, lnx_b, r_k, seg):
    mu = _dot(y, seg) * (1.0 / HD)
    d = y - mu
    var = _dot(d * d, seg) * (1.0 / HD)
    yn = d * lax.rsqrt(var + GN_EPS) * lnx_w + lnx_b
    bonus = _dot(fr * k2 * r_k, seg) * v2
    return ((yn + bonus) * g,)


def _rwkv_post_args(y, fl, k2, v2, g, p):
    ins = [y, fl, k2, v2, g, p["lnx_w"], p["lnx_b"], p["r_k"], p["seg64"]]
    specs = [_rows(DG), _rows(DG, 0), _rows(DG), _rows(DG), _rows(DG)] + [_full(x.shape) for x in ins[5:]]
    return ins, specs


def rwkv_post_fwd(l, y, fl, k2, v2, g, p):
    ins, specs = _rwkv_post_args(y, fl, k2, v2, g, p)
    return _map_fwd(f"rwkv_post_fwd{l}", _rwkv_post_fn, (SEQ // RB,), ins, specs, [_sds((SEQ, DG))], [_rows(DG)])[0]


def rwkv_post_bwd(l, y, fl, k2, v2, g, p, dya):
    ins, specs = _rwkv_post_args(y, fl, k2, v2, g, p)
    gs = _map_bwd(f"rwkv_post_bwd{l}", _rwkv_post_fn, (SEQ // RB,), ins, specs, [[dya]], [[_rows(DG, 0)]],
                  want=[0, 1, 2, 3, 4, 5, 6, 7], acc=(5, 6, 7), gout={1: (_sds((SEQ, DG)), _rows(DG))})
    return dict(zip(["y", "fr", "k2", "v2", "g", "lnx_w", "lnx_b", "r_k"], gs))


SCAN_TB = 128


def _coltile8(rows8, dmask, ones_stack, parts):
    pieces, rest = [], rows8
    for q in range(parts):
        piece = rest.astype(BF16).astype(F32)
        if q < parts - 1:
            rest = rest - piece
        pieces.append((piece[:, None, :] * dmask[None]).reshape(8 * HD, DG).astype(BF16))
    x = pieces[0] if parts == 1 else jnp.concatenate(pieces, axis=1)
    return jnp.dot(x, ones_stack, preferred_element_type=F32).reshape(8, HD, DG)


def _coltiles_bf16(rows_list, dmask, ones_bf16):
    x = jnp.concatenate([(r8[:, None, :] * dmask[None]).reshape(8 * HD, DG).astype(BF16) for r8 in rows_list], axis=0)
    t = jnp.dot(x, ones_bf16, preferred_element_type=F32)
    return [t[q * 8 * HD:(q + 1) * 8 * HD].reshape(8, HD, DG) for q in range(len(rows_list))]


def _segrows8(x8, dmask, ones_bf16):
    t = jnp.dot(x8.reshape(8 * HD, DG).astype(BF16), ones_bf16, preferred_element_type=F32).reshape(8, HD, DG)
    return jnp.sum(t * dmask[None], axis=1)


def rwkv_scan_fwd(l, fl, w, k2, v2, c, b, p, gather=()):
    nblk = SEQ // SCAN_TB
    ng = len(gather)

    def body(*refs):
        r_ref, w_ref, k_ref, v_ref, c_ref, b_ref, ones_ref, dm_ref = refs[:8]
        y_ref, st_ref = refs[8 + ng:10 + ng]
        s_sc = refs[10 + 2 * ng]
        if ng:
            begin, end = _gather_steps(refs[8:8 + ng], refs[10 + ng:10 + 2 * ng], *refs[11 + 2 * ng:])

            @pl.when(pl.program_id(0) == 0)
            def _():
                begin()

        @pl.when(pl.program_id(0) == 0)
        def _():
            s_sc[...] = jnp.zeros_like(s_sc)

        ones3, ones = ones_ref[...], ones_ref[0:DG, :]
        dmask = dm_ref[...]

        def group(gi, carry):
            t0 = pl.multiple_of(gi * 8, 8)
            sl = pl.ds(t0, 8)
            v8 = v_ref[sl, :]
            wt = _coltile8(w_ref[sl, :], dmask, ones3, 3)
            ct, bt, kt, rt = _coltiles_bf16([c_ref[sl, :], b_ref[sl, :], k_ref[sl, :], r_ref[sl, :]], dmask, ones)
            t = s_sc[...]
            for j in range(8):
                sa = jnp.sum(t * ct[j], axis=0, keepdims=True)
                t = t * wt[j] + bt[j] * sa + kt[j] * v8[j:j + 1, :]
                st_ref[t0 + j] = t
            s_sc[...] = t
            y_ref[sl, :] = jnp.sum(st_ref[sl] * rt, axis=1)
            return carry

        lax.fori_loop(0, SCAN_TB // 8, group, 0)

        if ng:
            @pl.when(pl.program_id(0) == nblk - 1)
            def _():
                end()

    row = pl.BlockSpec((SCAN_TB, DG), lambda i: (i, 0))
    ins = [fl, w, k2, v2, c, b, p["seg64x3_bf16"], p["dmask"]] + list(gather)
    specs = [row] * 6 + [_full((3 * DG, DG)), _full((HD, DG))] + [ANY] * ng
    outs = pl.pallas_call(body, grid=(nblk,), in_specs=specs,
                          out_specs=[row, pl.BlockSpec((SCAN_TB, HD, DG), lambda i: (i, 0, 0))] + [ANY] * ng,
                          out_shape=[_sds((SEQ, DG)), _sds((SEQ, HD, DG))] + _gather_shapes(gather),
                          scratch_shapes=[pltpu.VMEM((HD, DG), F32)] + (_gather_sems(ng) if ng else []),
                          name=f"rwkv_scan_fwd{l}", compiler_params=_cp(("arbitrary",)))(*ins)
    return outs[0], outs[1], list(outs[2:])


def rwkv_scan_bwd(l, fl, w, k2, v2, c, b, states, dy, p, exchange=()):
    nblk = SEQ // SCAN_TB
    nx = len(exchange)

    def body(*refs):
        r_ref, w_ref, k_ref, v_ref, c_ref, b_ref, dy_ref, st_ref, sp_ref, ones_ref, dm_ref = refs[:11]
        dr_ref, dw_ref, dk_ref, dv_ref, dc_ref, db_ref = refs[11 + nx:17 + nx]
        g_sc, prev_sc, d8_sc, dsa_sc = refs[17 + 2 * nx:21 + 2 * nx]
        i = pl.program_id(0)
        if nx:
            begin, end = _chip_exchange_steps(refs[11:11 + nx], refs[17 + nx:17 + 2 * nx], *refs[21 + 2 * nx:])

            @pl.when(i == 0)
            def _():
                begin()

        @pl.when(i == 0)
        def _():
            g_sc[...] = jnp.zeros_like(g_sc)

        ones3, ones = ones_ref[...], ones_ref[0:DG, :]
        dmask = dm_ref[...]
        first_block = i == nblk - 1

        def group(gr, carry):
            gi = SCAN_TB // 8 - 1 - gr
            t0 = pl.multiple_of(gi * 8, 8)
            sl = pl.ds(t0, 8)
            v8, dy8 = v_ref[sl, :], dy_ref[sl, :]
            t8 = st_ref[sl]
            @pl.when(gi > 0)
            def _():
                prev_sc[0] = st_ref[t0 - 1]

            @pl.when(gi == 0)
            def _():
                prev_sc[0] = jnp.where(first_block, 0.0, sp_ref[0])

            for j in range(1, 8):
                prev_sc[j] = t8[j - 1]
            tp8 = prev_sc[...]
            wt = _coltile8(w_ref[sl, :], dmask, ones3, 3)
            ct, bt, kt, rt = _coltiles_bf16([c_ref[sl, :], b_ref[sl, :], k_ref[sl, :], r_ref[sl, :]], dmask, ones)
            sa8 = jnp.sum(tp8 * ct, axis=1)
            g = g_sc[...]
            for j in range(7, -1, -1):
                g = g + rt[j] * dy8[j:j + 1, :]
                d8_sc[j] = g
                dsa = jnp.sum(g * bt[j], axis=0, keepdims=True)
                dsa_sc[j:j + 1, :] = dsa
                g = g * wt[j] + ct[j] * dsa
            g_sc[...] = g
            d8 = d8_sc[...]
            dsa8 = dsa_sc[...]
            dv_ref[sl, :] = jnp.sum(d8 * kt, axis=1)
            dr_ref[sl, :] = _segrows8(t8 * dy8[:, None, :], dmask, ones)
            dk_ref[sl, :] = _segrows8(d8 * v8[:, None, :], dmask, ones)
            dw_ref[sl, :] = _segrows8(tp8 * d8, dmask, ones)
            db_ref[sl, :] = _segrows8(d8 * sa8[:, None, :], dmask, ones)
            dc_ref[sl, :] = _segrows8(tp8 * dsa8[:, None, :], dmask, ones)
            return carry

        lax.fori_loop(0, SCAN_TB // 8, group, 0)

        if nx:
            @pl.when(i == nblk - 1)
            def _():
                end()

    row = pl.BlockSpec((SCAN_TB, DG), lambda i: (nblk - 1 - i, 0))
    st_spec = pl.BlockSpec((SCAN_TB, HD, DG), lambda i: (nblk - 1 - i, 0, 0))
    sp_spec = pl.BlockSpec((1, HD, DG), lambda i: (jnp.maximum((nblk - 1 - i) * SCAN_TB - 1, 0), 0, 0))
    ins = [fl, w, k2, v2, c, b, dy, states, states, p["seg64x3_bf16"], p["dmask"]] + list(exchange)
    specs = [row] * 7 + [st_spec, sp_spec, _full((3 * DG, DG)), _full((HD, DG))] + [ANY] * nx
    tile8 = pltpu.VMEM((8, HD, DG), F32)
    sems = [pltpu.SemaphoreType.DMA((nx, 3)), pltpu.SemaphoreType.DMA((nx, 3))] if nx else []
    outs = pl.pallas_call(body, grid=(nblk,), in_specs=specs, out_specs=[row] * 6 + [ANY] * nx,
                          out_shape=[_sds((SEQ, DG))] * 6 + [_sds(a.shape, a.dtype) for a in exchange],
                          scratch_shapes=[pltpu.VMEM((HD, DG), F32), tile8, tile8, pltpu.VMEM((8, DG), F32)] + sems,
                          name=f"rwkv_scan_bwd{l}", compiler_params=_cp(("arbitrary",)))(*ins)
    return outs[:6], list(outs[6:])


HG_ROWS = 128


HG_NC = HG_ROWS // HGRN_CHUNK


def _hgrn_block_fn(layer):
    def fn(hq, hf, hi, hg, sprev, lb0, lb1, norm_w, seg, bd, tri_bd, ones_bd, first_row, causal):
        e0 = jnp.exp(lb0 - jnp.maximum(lb0, lb1))
        e1 = jnp.exp(lb1 - jnp.maximum(lb0, lb1))
        sm0, sm1 = e0 / (e0 + e1), e1 / (e0 + e1)
        lb = (sm0 - sm0) if layer == 0 else ((sm0 + sm1) - sm0)
        forget = lb + (1.0 - lb) * _sigmoid(hf)
        logf = jnp.log(forget)
        kk = 1.0 - forget
        q = _silu(hq)
        c, nc = HGRN_CHUNK, HG_NC
        b = _dotx(tri_bd, logf)
        bl = _dotx(ones_bd, logf)
        split = lambda t: t.reshape(nc, c, DG)
        b4 = split(b)
        diff = (b4[:, :, None, :] - b4[:, None, :, :]).reshape(nc * c * c, DG)
        dec = jnp.exp(jnp.where(causal > 0.5, diff, -1e30))
        qrep = jnp.broadcast_to(split(q)[:, :, None, :], (nc, c, c, DG)).reshape(nc * c * c, DG)
        ktil = jnp.broadcast_to(split(kk)[:, None, :, :], (nc, c, c, DG)).reshape(nc * c * c, DG)
        vtil = jnp.broadcast_to(split(hi)[:, None, :, :], (nc, c, c, DG)).reshape(nc * c * c, DG)
        att = _dot(qrep * ktil * dec, seg)
        o_intra = jnp.sum((att * vtil).reshape(nc * c, c, DG), axis=1)
        kd4 = split(kk * jnp.exp(bl - b))
        qe4 = split(q * jnp.exp(b))
        v4 = split(hi)
        tot = jnp.exp(_dotx(first_row, bl))
        s, o_inter = sprev, []
        for ci in range(nc):
            o_inter.append(_dot_nt(qe4[ci], s))
            s = s * tot[ci:ci + 1, :] + _dot_tn(v4[ci], kd4[ci]) * bd
        o = o_intra + jnp.concatenate(o_inter, axis=0)
        ms = _dot(o * o, seg) * (1.0 / HD)
        y = o * lax.rsqrt(ms + RMS_EPS) * norm_w * _silu(hg)
        return y, s

    return fn


def _hgrn_consts(p):
    return [p["seg64"], p["seg64"], p["tri_bd128"], p["ones_bd128"], p["first_row"], p["causal_blk"]]


def hgrn_fwd(l, proj, p):
    fn = _hgrn_block_fn(l)

    def body(hq_ref, hf_ref, hi_ref, hg_ref, *rest):
        const_refs, (y_ref, st_ref, s_sc) = rest[:-3], rest[-3:]

        @pl.when(pl.program_id(0) == 0)
        def _():
            s_sc[...] = jnp.zeros_like(s_sc)

        sprev = s_sc[...]
        st_ref[0] = sprev
        y, snext = fn(hq_ref[...], hf_ref[...], hi_ref[...], hg_ref[...], sprev, *[r[...] for r in const_refs])
        y_ref[...] = y
        s_sc[...] = snext

    rows = lambda cb: pl.BlockSpec((HG_ROWS, DG), lambda i: (i, cb))
    ins = [proj, proj, proj, proj, p["lb0"], p["lb1"], p["hgrn_norm_w"]] + _hgrn_consts(p)
    specs = [rows(C_HQ // DG), rows(C_HF // DG), rows(C_HI // DG), rows(C_HG // DG)] + [_full(x.shape) for x in ins[4:]]
    return pl.pallas_call(body, grid=(SEQ // HG_ROWS,), in_specs=specs,
                          out_specs=[rows(0), pl.BlockSpec((1, DG, DG), lambda i: (i, 0, 0))],
                          out_shape=[_sds((SEQ, DG)), _sds((SEQ // HG_ROWS, DG, DG))],
                          scratch_shapes=[pltpu.VMEM((DG, DG), F32)], name=f"hgrn_fwd{l}",
                          compiler_params=_cp(("arbitrary",)))(*ins)


def hgrn_bwd(l, proj, states, dy, p, sibling=(), dy_col=0):
    fn = _hgrn_block_fn(l)
    nblk = SEQ // HG_ROWS
    n_const = len(_hgrn_consts(p))
    ns = len(sibling)

    def body(hq_ref, hf_ref, hi_ref, hg_ref, st_ref, dy_ref, lb0_ref, lb1_ref, nw_ref, *rest):
        const_refs, rest = rest[:n_const], rest[n_const:]
        dp_ref, dlb0_ref, dlb1_ref, dnw_ref = rest[ns:ns + 4]
        ds_sc = rest[2 * ns + 4]
        if ns:
            begin, end = _sibling_steps(rest[:ns], rest[ns + 4:2 * ns + 4], *rest[2 * ns + 5:])

            @pl.when(pl.program_id(0) == 0)
            def _():
                begin()

        @pl.when(pl.program_id(0) == 0)
        def _():
            ds_sc[...] = jnp.zeros_like(ds_sc)
            dlb0_ref[...] = jnp.zeros_like(dlb0_ref)
            dlb1_ref[...] = jnp.zeros_like(dlb1_ref)
            dnw_ref[...] = jnp.zeros_like(dnw_ref)

        consts = [r[...] for r in const_refs]
        f = lambda hq, hf, hi, hg, sp, b0, b1, nw: fn(hq, hf, hi, hg, sp, b0, b1, nw, *consts)
        _, vjp = jax.vjp(f, hq_ref[...], hf_ref[...], hi_ref[...], hg_ref[...], st_ref[0], lb0_ref[...], lb1_ref[...],
                         nw_ref[...])
        dhq, dhf, dhi, dhg, dsp, dlb0, dlb1, dnw = vjp((dy_ref[...], ds_sc[...]))
        dp_ref[:, 0:DG] = dhq
        dp_ref[:, DG:2 * DG] = dhf
        dp_ref[:, 2 * DG:3 * DG] = dhi
        dp_ref[:, 3 * DG:4 * DG] = dhg
        ds_sc[...] = dsp
        dlb0_ref[...] += dlb0
        dlb1_ref[...] += dlb1
        dnw_ref[...] += dnw

        if ns:
            @pl.when(pl.program_id(0) == nblk - 1)
            def _():
                end()

    rows = lambda cb: pl.BlockSpec((HG_ROWS, DG), lambda i: (nblk - 1 - i, cb))
    ins = [proj, proj, proj, proj, states, dy, p["lb0"], p["lb1"], p["hgrn_norm_w"]] + _hgrn_consts(p)
    specs = [rows(C_HQ // DG), rows(C_HF // DG), rows(C_HI // DG), rows(C_HG // DG),
             pl.BlockSpec((1, DG, DG), lambda i: (nblk - 1 - i, 0, 0)), rows(dy_col)] + [_full(x.shape) for x in ins[6:]]
    sem = pltpu.SemaphoreType.DMA((max(ns, 1), 4))
    outs = pl.pallas_call(body, grid=(nblk,), in_specs=specs + [ANY] * ns,
                          out_specs=[pl.BlockSpec((HG_ROWS, 4 * DG), lambda i: (nblk - 1 - i, 0)), _full((1, DG)),
                                     _full((1, DG)), _full((1, DG))] + [ANY] * ns,
                          out_shape=[_sds((SEQ, 4 * DG)), _sds((1, DG)), _sds((1, DG)), _sds((1, DG))]
                          + [_sds((4,) + a.shape[1:], a.dtype) for a in sibling],
                          scratch_shapes=[pltpu.VMEM((DG, DG), F32)] + ([sem, sem] if ns else []), name=f"hgrn_bwd{l}",
                          compiler_params=_cp(("arbitrary",)))(*ins, *sibling)
    return outs[:4], list(outs[4:])


def _ssd_chunk_fn(z, xs, bm, cm, dtr, sprev, dt_bias, a_log, d_par, norm_w, e128, tri, trit, seg128, ones128):
    lc = SSD_CHUNK
    dt = _softplus(dtr + dt_bias)
    a = -jnp.exp(a_log)
    da = dt * a * (lax.broadcasted_iota(jnp.int32, (1, 128), 1) < NH).astype(F32)
    cs = _dotx(tri, da)
    cst = _dotx_tn(da, trit)
    cs_b = _dotx(cs, e128)
    dt_b = _dotx(dt, e128)
    csl_b = _dotx(jnp.sum(da, axis=0, keepdims=True), e128)
    xdt = xs * dt_b
    lane = lax.broadcasted_iota(jnp.int32, (1, DG), 1)
    rowi = lax.broadcasted_iota(jnp.int32, (lc, lc), 0)
    coli = lax.broadcasted_iota(jnp.int32, (lc, lc), 1)
    y = jnp.zeros((lc, DG), F32)
    snew = jnp.zeros((DG, SSD_N), F32)
    d_b = jnp.zeros((1, DG), F32)
    wdec = xdt * jnp.exp(csl_b - cs_b)
    for g in range(2):
        bg = bm[:, g * SSD_N:(g + 1) * SSD_N]
        cg = cm[:, g * SSD_N:(g + 1) * SSD_N]
        gmat = _dot_nt(cg, bg)
        gmask = ((lane // 128) == g).astype(F32)
        snew = snew + _dot_tn(wdec * gmask, bg)
        y = y + _dot_nt(cg, sprev) * gmask * jnp.exp(cs_b)
        for hh in range(2):
            h = 2 * g + hh
            seg = jnp.where(rowi >= coli, cs[:, h:h + 1] - cst[h:h + 1, :], -1e30)
            hmask = ((lane // HD) == h).astype(F32)
            y = y + _dot(gmat * jnp.exp(seg), xdt * hmask)
            d_b = d_b + d_par[:, h:h + 1] * hmask
    cd = jnp.exp(_dotx_tn(_dotx(da, e128), ones128))
    snext = sprev * cd + snew
    y = y + xs * d_b
    y = y * _silu(z)
    ms = _dot(y * y, seg128) * (1.0 / 128.0)
    return y * lax.rsqrt(ms + RMS_EPS) * norm_w, snext


def ssd_fwd(l, proj, xc, p):
    nc = SEQ // SSD_CHUNK

    def body(z_ref, xs_ref, b_ref, c_ref, dt_ref, dtb_ref, al_ref, d_ref, nw_ref, e_ref, tri_ref, trit_ref, sg_ref,
             on_ref, y_ref, st_ref, s_sc):
        @pl.when(pl.program_id(0) == 0)
        def _():
            s_sc[...] = jnp.zeros_like(s_sc)

        sprev = s_sc[...]
        st_ref[0] = sprev
        y, snext = _ssd_chunk_fn(z_ref[...], xs_ref[...], b_ref[...], c_ref[...], dt_ref[...], sprev, dtb_ref[...],
                                 al_ref[...], d_ref[...], nw_ref[...], e_ref[...], tri_ref[...], trit_ref[...],
                                 sg_ref[...], on_ref[...])
        y_ref[...] = y
        s_sc[...] = snext

    rw = lambda w, cb: pl.BlockSpec((SSD_CHUNK, w), lambda i: (i, cb))
    ins = [proj, xc, xc, xc, proj, p["dt_bias"], p["a_log"], p["ssd_d"], p["ssd_norm_w"], p["e128"], p["tri128"],
           p["tri128t"], p["seg128"], p["ones128"]]
    specs = [rw(DG, C_Z // DG), rw(DG, 0), rw(DG, 1), rw(DG, 2), rw(128, C_DT // 128)] + [_full(x.shape) for x in ins[5:]]
    return pl.pallas_call(body, grid=(nc,), in_specs=specs,
                          out_specs=[rw(DG, 0), pl.BlockSpec((1, DG, SSD_N), lambda i: (i, 0, 0))],
                          out_shape=[_sds((SEQ, DG)), _sds((nc, DG, SSD_N))],
                          scratch_shapes=[pltpu.VMEM((DG, SSD_N), F32)], name=f"ssd_fwd{l}",
                          compiler_params=_cp(("arbitrary",)))(*ins)


def ssd_bwd(l, proj, xc, states, dy, p, dy_col=0):
    nc = SEQ // SSD_CHUNK

    def body(z_ref, xs_ref, b_ref, c_ref, dt_ref, st_ref, dy_ref, dtb_ref, al_ref, d_ref, nw_ref, e_ref, tri_ref,
             trit_ref, sg_ref, on_ref, dz_ref, dxc_ref, ddt_ref, ddtb_ref, dal_ref, dd_ref, dnw_ref, ds_sc):
        @pl.when(pl.program_id(0) == 0)
        def _():
            ds_sc[...] = jnp.zeros_like(ds_sc)
            ddtb_ref[...] = jnp.zeros_like(ddtb_ref)
            dal_ref[...] = jnp.zeros_like(dal_ref)
            dd_ref[...] = jnp.zeros_like(dd_ref)
            dnw_ref[...] = jnp.zeros_like(dnw_ref)

        consts = (e_ref[...], tri_ref[...], trit_ref[...], sg_ref[...], on_ref[...])
        f = lambda *a: _ssd_chunk_fn(*a, *consts)
        _, vjp = jax.vjp(f, z_ref[...], xs_ref[...], b_ref[...], c_ref[...], dt_ref[...], st_ref[0], dtb_ref[...],
                         al_ref[...], d_ref[...], nw_ref[...])
        dz, dxs, db, dc, ddt, dsp, ddtb, dal, dd, dnw = vjp((dy_ref[...], ds_sc[...]))
        dz_ref[...] = dz
        dxc_ref[:, 0:DG] = dxs
        dxc_ref[:, DG:2 * DG] = db
        dxc_ref[:, 2 * DG:3 * DG] = dc
        ddt_ref[...] = ddt
        ds_sc[...] = dsp
        ddtb_ref[...] += ddtb
        dal_ref[...] += dal
        dd_ref[...] += dd
        dnw_ref[...] += dnw

    rw = lambda w, cb: pl.BlockSpec((SSD_CHUNK, w), lambda i: (nc - 1 - i, cb))
    ins = [proj, xc, xc, xc, proj, states, dy, p["dt_bias"], p["a_log"], p["ssd_d"], p["ssd_norm_w"], p["e128"],
           p["tri128"], p["tri128t"], p["seg128"], p["ones128"]]
    specs = [rw(DG, C_Z // DG), rw(DG, 0), rw(DG, 1), rw(DG, 2), rw(128, C_DT // 128),
             pl.BlockSpec((1, DG, SSD_N), lambda i: (nc - 1 - i, 0, 0)), rw(DG, dy_col)] + [_full(x.shape) for x in ins[7:]]
    return pl.pallas_call(body, grid=(nc,), in_specs=specs,
                          out_specs=[rw(DG, 0), rw(3 * DG, 0), rw(128, 0), _full((1, 128)), _full((1, 128)), _full((1, 128)),
                                     _full((1, DG))],
                          out_shape=[_sds((SEQ, DG)), _sds((SEQ, 3 * DG)), _sds((SEQ, 128)), _sds((1, 128)), _sds((1, 128)),
                                     _sds((1, 128)), _sds((1, DG))],
                          scratch_shapes=[pltpu.VMEM((DG, SSD_N), F32)], name=f"ssd_bwd{l}",
                          compiler_params=_cp(("arbitrary",)))(*ins)


ATT_BLK = 128


def _att_geometry(dil):
    i = lax.broadcasted_iota(jnp.int32, (ATT_BLK, ATT_BLK), 0)
    j = lax.broadcasted_iota(jnp.int32, (ATT_BLK, ATT_BLK), 1)
    return ((i - j) * dil).astype(F32), ((ATT_BLK + i - j) * dil).astype(F32), j <= i, j >= i


def _att_scores(qn, kc, kp, h, geom, has_prev):
    dist_c, dist_p, m_c, m_pj = geom
    slope = 2.0 ** (-8.0 * (h + 1) / NH)
    scale = HD ** -0.5
    s_c = _dot_nt(qn, kc) * scale - slope * dist_c
    s_p = _dot_nt(qn, kp) * scale - slope * dist_p
    m_p = jnp.logical_and(m_pj, has_prev)
    return jnp.where(m_c, s_c, -1e30), jnp.where(m_p, s_p, -1e30), m_c, m_p


def _sub_spec(ln, width, col):
    return pl.BlockSpec((ln, DG), lambda z: (0, z * (width // DG) + col // DG))


QKV_W = 3 * DG


def attn_branch_fwd(l, bi, qkv, dil):
    ln = SEQ // dil
    nb = ln // ATT_BLK

    def body(q_ref, k_ref, v_ref, o_ref, l_ref):
        geom = _att_geometry(dil)

        def blk(n, carry):
            r0 = pl.multiple_of(n * ATT_BLK, ATT_BLK)
            rp = pl.multiple_of(jnp.maximum(n - 1, 0) * ATT_BLK, ATT_BLK)
            cur, prv = pl.ds(r0, ATT_BLK), pl.ds(rp, ATT_BLK)
            for h in range(NH):
                hs = slice(h * HD, (h + 1) * HD)
                qn, kc, vc, kp, vp = q_ref[cur, hs], k_ref[cur, hs], v_ref[cur, hs], k_ref[prv, hs], v_ref[prv, hs]
                s_c, s_p, m_c, m_p = _att_scores(qn, kc, kp, h, geom, n > 0)
                m = jnp.maximum(jnp.max(s_c, axis=1, keepdims=True), jnp.max(s_p, axis=1, keepdims=True))
                p_c = jnp.where(m_c, jnp.exp(s_c - m), 0.0)
                p_p = jnp.where(m_p, jnp.exp(s_p - m), 0.0)
                den = jnp.sum(p_c, axis=1, keepdims=True) + jnp.sum(p_p, axis=1, keepdims=True)
                o_ref[cur, hs] = (_dot(p_c, vc) + _dot(p_p, vp)) / den
                l_ref[cur, hs] = jnp.broadcast_to(m + jnp.log(den), (ATT_BLK, HD))
            return carry

        lax.fori_loop(0, nb, blk, 0)

    pv = qkv.reshape(ln, dil * QKV_W)
    out = pl.BlockSpec((ln, DG), lambda z: (0, z))
    o, lse = pl.pallas_call(body, grid=(dil,), in_specs=[_sub_spec(ln, QKV_W, 0), _sub_spec(ln, QKV_W, DG), _sub_spec(ln, QKV_W, 2 * DG)],
                            out_specs=[out, out], out_shape=[_sds((ln, dil * DG))] * 2, name=f"attn_fwd{l}_{bi}",
                            compiler_params=_cp(("parallel",)))(pv, pv, pv)
    return o.reshape(SEQ, DG), lse.reshape(SEQ, DG)


def attn_branch_bwd(l, bi, qkv, dil, dyb, lse_all, delta):
    ln = SEQ // dil
    nb = ln // ATT_BLK
    scale = HD ** -0.5

    def body(q_ref, k_ref, v_ref, do_ref, l_ref, dl_ref, dq_ref, dk_ref, dv_ref):
        dk_ref[...] = jnp.zeros_like(dk_ref)
        dv_ref[...] = jnp.zeros_like(dv_ref)
        geom = _att_geometry(dil)

        def blk(n, carry):
            r0 = pl.multiple_of(n * ATT_BLK, ATT_BLK)
            rp = pl.multiple_of(jnp.maximum(n - 1, 0) * ATT_BLK, ATT_BLK)
            cur, prv = pl.ds(r0, ATT_BLK), pl.ds(rp, ATT_BLK)
            for h in range(NH):
                hs = slice(h * HD, (h + 1) * HD)
                qn, don = q_ref[cur, hs], do_ref[cur, hs]
                lse, dlt = l_ref[cur, h * HD:h * HD + 1], dl_ref[cur, h * HD:h * HD + 1]
                kc, vc, kp, vp = k_ref[cur, hs], v_ref[cur, hs], k_ref[prv, hs], v_ref[prv, hs]
                s_c, s_p, m_c, m_p = _att_scores(qn, kc, kp, h, geom, n > 0)
                p_c = jnp.where(m_c, jnp.exp(s_c - lse), 0.0)
                p_p = jnp.where(m_p, jnp.exp(s_p - lse), 0.0)
                ds_c = p_c * (_dot_nt(don, vc) - dlt)
                ds_p = p_p * (_dot_nt(don, vp) - dlt)
                dq_ref[cur, hs] = (_dot(ds_c, kc) + _dot(ds_p, kp)) * scale
                dv_ref[prv, hs] += _dot_tn(p_p, don)
                dk_ref[prv, hs] += _dot_tn(ds_p, qn) * scale
                dv_ref[cur, hs] += _dot_tn(p_c, don)
                dk_ref[cur, hs] += _dot_tn(ds_c, qn) * scale
            return carry

        lax.fori_loop(0, nb, blk, 0)

    pv = qkv.reshape(ln, dil * QKV_W)
    sub = lambda t: t.reshape(ln, dil * DG)
    row = pl.BlockSpec((ln, DG), lambda z: (0, z))
    outs = pl.pallas_call(body, grid=(dil,),
                          in_specs=[_sub_spec(ln, QKV_W, 0), _sub_spec(ln, QKV_W, DG), _sub_spec(ln, QKV_W, 2 * DG), row, row, row],
                          out_specs=[row] * 3, out_shape=[_sds((ln, dil * DG))] * 3, name=f"attn_bwd{l}_{bi}",
                          compiler_params=_cp(("parallel",)))(pv, pv, pv, sub(dyb), sub(lse_all), sub(delta))
    return [t.reshape(SEQ, DG) for t in outs]


def _attn_merge_fn(o1, o2, o3, l1, l2, l3):
    m = jnp.maximum(jnp.maximum(l1, l2), l3)
    w1, w2, w3 = jnp.exp(l1 - m), jnp.exp(l2 - m), jnp.exp(l3 - m)
    den = w1 + w2 + w3
    return (w1 * o1 + w2 * o2 + w3 * o3) / den, m + jnp.log(den)


def attn_merge(l, os_, ls_):
    ins = list(os_) + list(ls_)
    return _map_fwd(f"attn_merge{l}", _attn_merge_fn, (SEQ // RB,), ins, [_rows(DG)] * 6, [_sds((SEQ, DG))] * 2,
                    [_rows(DG)] * 2)


def attn_delta(l, dyb, yb, seg):
    fn = lambda d, y, s: (_dot(d * y, s),)
    return _map_fwd(f"attn_delta{l}", fn, (SEQ // RB,), [dyb, yb, seg], [_rows(DG), _rows(DG), _full((DG, DG))],
                    [_sds((SEQ, DG))], [_rows(DG)])[0]


def _ln_fn(x, mix, w, b):
    h = ALPHA * x + mix
    mu = jnp.mean(h, axis=-1, keepdims=True)
    d = h - mu
    var = jnp.mean(d * d, axis=-1, keepdims=True)
    return (d * lax.rsqrt(var + LN_EPS) * w + b,)


def ln_fwd(name, x, mix, w, b):
    specs = [_rows(D_MODEL), _rows(D_MODEL), _full((1, D_MODEL)), _full((1, D_MODEL))]
    return _map_fwd(name, _ln_fn, (SEQ // RB,), [x, mix, w, b], specs, [_sds((SEQ, D_MODEL))], [_rows(D_MODEL)])[0]


def ln_bwd(name, x, mix, w, b, dy):
    specs = [_rows(D_MODEL), _rows(D_MODEL), _full((1, D_MODEL)), _full((1, D_MODEL))]
    return _map_bwd(name, _ln_fn, (SEQ // RB,), [x, mix, w, b], specs, [[dy]], [[_rows(D_MODEL)]], want=[1, 2, 3],
                    acc=(2, 3))


def loss_call(y, tgt):
    def fn(yy, tt):
        e = yy - tt
        part = 0.5 * jnp.sum(jnp.sum(e * e, axis=-1, keepdims=True) * (1.0 / D_MODEL), axis=0, keepdims=True)
        return e * (1.0 / D_MODEL), jnp.broadcast_to(part, (8, 128))

    return _map_fwd("loss", fn, (SEQ // RB,), [y, tgt], [_rows(D_MODEL)] * 2,
                    [_sds((SEQ, D_MODEL)), _sds((SEQ // RB * 8, 128))],
                    [_rows(D_MODEL), pl.BlockSpec((8, 128), lambda i: (i, 0))])


LATE_KEYS = ("w_out", "w_up_t", "w_down")


def _full_rows(g):
    return g.reshape(N_DEV * g.shape[1], g.shape[2])


def layer_fwd(l, x, vfirst, wts, p, gather=(), late=False):
    sv = {"x": x}
    proj = _mm(f"mm_in{l}", x, wts["w_in"], "nn", *MM_TILES["k1024"])
    fl = lerp_fwd(l, proj, p["mu"])
    xc = conv_fwd(l, proj, p["conv_w"], p["conv_b"])
    w, k2, v2, c, b, g = rwkv_pre_fwd(l, fl, vfirst, p)
    y_scan, states, sv["gathered"] = rwkv_scan_fwd(l, fl, w, k2, v2, c, b, p, gather)
    if late:
        wts = dict(wts, **dict(zip(LATE_KEYS, [_full_rows(g) for g in sv["gathered"][:3]])))
    sv["wts"] = wts
    ya = rwkv_post_fwd(l, y_scan, fl, k2, v2, g, p)
    qkv = proj[:, C_AQ:C_AQ + 3 * DG]
    outs, lses = [], []
    for bi, (win, dil) in enumerate(DILATED):
        o, lse = attn_branch_fwd(l, bi, qkv, dil)
        outs.append(o)
        lses.append(lse)
    yb, lse_all = attn_merge(l, outs, lses)
    yc, ssd_states = ssd_fwd(l, proj, xc, p)
    yd, hg_states = hgrn_fwd(l, proj, p)
    ycat = jnp.concatenate([ya, yb, yc, yd], axis=1).astype(BF16)
    mix = _mm(f"mm_out{l}", ycat, wts["w_out"], "nn", *MM_TILES["k1024"])
    x1 = ln_fwd(f"ln1_fwd{l}", x, mix, p["ln1_w"], p["ln1_b"])
    hh = _mm(f"mm_up{l}", x1, wts["w_up_t"], "nt", *MM_TILES["k1024"], epilogue="relu2")
    m2 = _mm(f"mm_down{l}", hh, wts["w_down"], "nn", *MM_TILES["k4096"])
    x2 = ln_fwd(f"ln2_fwd{l}", x1, m2, p["ln2_w"], p["ln2_b"])
    sv.update(proj=proj, fl=fl, xc=xc, w=w, k2=k2, v2=v2, c=c, b=b, g=g, y_scan=y_scan, states=states,
              yb=yb, lse_all=lse_all, ssd_states=ssd_states, hg_states=hg_states, ycat=ycat, mix=mix, x1=x1, hh=hh, qkv=qkv,
              m2=m2, vfirst=vfirst)
    return x2, sv


def layer_bwd(l, dx2, dvfirst_next, sv, wts, p, exchange=(), reducer=None):
    gr = {}
    x, x1, proj, fl = sv["x"], sv["x1"], sv["proj"], sv["fl"]
    dres2, gr["ln2_w"], gr["ln2_b"] = ln_bwd(f"ln2_bwd{l}", x1, sv["m2"], p["ln2_w"], p["ln2_b"], dx2)
    du = _mm(f"mm_down_dx{l}", dres2, wts["w_down"], "nt", *MM_TILES["k1024"], add=sv["hh"], epilogue="relu2_bwd",
             out_dtype=BF16)
    gr["w_down"] = _mm(f"mm_down_dw{l}", sv["hh"], dres2, "tn", *MM_TILES["wgrad_tall"])
    dx1 = _mm(f"mm_up_dx{l}", du, wts["w_up_t"], "nn", *MM_TILES["k4096"], add=dres2, add_scale=ALPHA)
    gr["w_up_t"] = _mm(f"mm_up_dw{l}", du, x1, "tn", *MM_TILES["wgrad_tall"])
    dres1, gr["ln1_w"], gr["ln1_b"] = ln_bwd(f"ln1_bwd{l}", x, sv["mix"], p["ln1_w"], p["ln1_b"], dx1)
    dycat = _mm(f"mm_out_dx{l}", dres1, wts["w_out"], "nt", *MM_TILES["k1024"])
    gr["w_out"] = _mm(f"mm_out_dw{l}", sv["ycat"], dres1, "tn", 1024, 1024, 512)
    dyb = dycat[:, DG:2 * DG]
    send = [_owner_blocks(gr[k]) for k in LATE_KEYS] if reducer else []
    (dhg4, gr["lb0"], gr["lb1"], gr["hgrn_norm_w"]), sib = hgrn_bwd(l, proj, sv["hg_states"], dycat, p, send, dy_col=3)
    if reducer:
        gr["early_own"], early_parts = reducer(f"{l}a", send, sib)
        exchange = list(exchange) + list(early_parts)
    dz, dxc, ddt, gr["dt_bias"], gr["a_log"], gr["ssd_d"], gr["ssd_norm_w"] = ssd_bwd(l, proj, sv["xc"], sv["ssd_states"], dycat, p, dy_col=2)
    dxbc, gr["conv_w"], gr["conv_b"] = conv_bwd(l, proj, p["conv_w"], p["conv_b"], dxc)
    delta = attn_delta(l, dyb, sv["yb"], p["seg64"])
    dqs, dks, dvs = [], [], []
    for bi, (win, dil) in enumerate(DILATED):
        dq, dk, dv = attn_branch_bwd(l, bi, sv["qkv"], dil, dyb, sv["lse_all"], delta)
        dqs.append(dq)
        dks.append(dk)
        dvs.append(dv)
    dq_a, dk_a, dv_a = _addn(f"attn_dq{l}", *dqs), _addn(f"attn_dk{l}", *dks), _addn(f"attn_dv{l}", *dvs)
    pg = rwkv_post_bwd(l, sv["y_scan"], fl, sv["k2"], sv["v2"], sv["g"], p, dycat)
    gr["lnx_w"], gr["lnx_b"], gr["r_k"] = pg["lnx_w"], pg["lnx_b"], pg["r_k"]
    (dr, dw, dk, dv, dc, db), gr["exchanged"] = rwkv_scan_bwd(l, fl, sv["w"], sv["k2"], sv["v2"], sv["c"], sv["b"],
                                                              sv["states"], pg["y"], p, exchange)
    v2_cts = [dv, pg["v2"]] + ([dvfirst_next] if dvfirst_next is not None else [])
    qg = rwkv_pre_bwd(l, fl, sv["vfirst"], p, [[dw], [dk, pg["k2"]], v2_cts, [dc], [db], [pg["g"]]])
    for nme in ("w0", "w2p", "a0", "a2p", "g2p", "k_k", "k_a", "v0", "v2p"):
        if nme in qg:
            gr[nme] = qg[nme]
    dfr = _addn(f"rwkv_dr{l}", dr, pg["fr"])
    dvres = qg["fvres"] if l > 0 else jnp.zeros((SEQ, 128), F32)
    dfl_out = jnp.concatenate([dfr, qg["fk"], qg["fv"], qg["flora"], dvres], axis=1)
    dfl_in, gr["mu"] = lerp_bwd(l, proj, p["mu"], dfl_out)
    dproj = jnp.concatenate([dfl_in[:, 0:768], dq_a, dk_a, dv_a, dz, dxbc, dhg4, dfl_in[:, 768:896], ddt,
                             dfl_in[:, 896:1024], jnp.zeros((SEQ, 128), F32)], axis=1).astype(BF16)
    dx = _mm(f"mm_in_dx{l}", dproj, wts["w_in"], "nt", *MM_TILES["k4096"], add=dres1, add_scale=ALPHA)
    gr["w_in"] = _mm(f"mm_in_dw{l}", x, dproj, "tn", *MM_TILES["wgrad_wide"])
    return dx, (qg["vfirst"] if l > 0 else None), gr


def _w_in_pad(w_in_l, w_vres):
    rows = w_in_l.shape[0]
    z = lambda n: jnp.zeros((rows, n), w_in_l.dtype)
    vres = z(128) if w_vres is None else jnp.concatenate([w_vres, z(96)], axis=1)
    return jnp.concatenate([w_in_l[:, 0:768], w_in_l[:, 896:1664], w_in_l[:, 1664:1920], w_in_l[:, 1920:2688],
                            w_in_l[:, 2692:3716], w_in_l[:, 768:896], w_in_l[:, 2688:2692], z(124), vres, z(128)], axis=1)


def _w_in_unpad(g):
    g_in = jnp.concatenate([g[:, 0:768], g[:, C_LORA:C_LORA + 128], g[:, 768:1536], g[:, C_Z:C_Z + 256],
                            g[:, C_XBC:C_XBC + 768], g[:, C_DT:C_DT + 4], g[:, C_HQ:C_HQ + 1024]], axis=1)
    return g_in, g[:, C_VRES:C_VRES + 32]


def _consts():
    pair = jnp.arange(HG_NC * HGRN_CHUNK * HGRN_CHUNK)
    i128 = jnp.arange(128)
    same_chunk = (i128[:, None] // HGRN_CHUNK) == (i128[None, :] // HGRN_CHUNK)
    seg64 = _seg_ones(DG, HD)
    tri128 = (i128[:, None] >= i128[None, :]).astype(F32)
    return dict(
        seg64=seg64, seg64x3_bf16=jnp.concatenate([seg64, seg64, seg64], axis=0).astype(BF16),
        dmask=(jnp.arange(HD)[:, None] == (jnp.arange(DG)[None, :] % HD)).astype(F32),
        tri_bd128=(same_chunk & (i128[:, None] >= i128[None, :])).astype(F32), ones_bd128=same_chunk.astype(F32),
        first_row=(i128[None, :] == (jnp.arange(HG_NC) * HGRN_CHUNK)[:, None]).astype(F32),
        causal_blk=jnp.broadcast_to((((pair // HGRN_CHUNK) % HGRN_CHUNK) >= (pair % HGRN_CHUNK)).astype(F32)[:, None],
                                    (HG_NC * HGRN_CHUNK * HGRN_CHUNK, DG)),
        e128=((i128[:, None] == (jnp.arange(DG)[None, :] // HD)) & (i128[:, None] < NH)).astype(F32),
        tri128=tri128, tri128t=tri128.T, seg128=_seg_ones(DG, 128), ones128=jnp.ones((128, 128), F32))


def _pad_lanes(v, n):
    return jnp.concatenate([v, jnp.zeros((n - v.shape[0],), v.dtype)])[None, :]


def _layer_params(l, raw, consts):
    p = dict(consts)
    row = lambda name: raw[name][l][None, :]
    z = lambda r: jnp.zeros((r, DG), F32)
    mu_vres = raw["mu_vres"][l - 1] if l > 0 else jnp.zeros((32,), F32)
    p["mu"] = jnp.concatenate([raw["mu_shift"][l], mu_vres, jnp.zeros((96,), F32)])[None, :]
    p["conv_w"], p["conv_b"] = raw["ssd_conv_w"][l], row("ssd_conv_b")
    p["w0"], p["a0"], p["k_k"], p["k_a"] = row("rwkv_w0"), row("rwkv_a0"), row("rwkv_k_k"), row("rwkv_k_a")
    p["lnx_w"], p["lnx_b"] = row("rwkv_lnx_w"), row("rwkv_lnx_b")
    p["r_k"] = raw["rwkv_r_k"][l].reshape(1, DG)
    p["w2p"] = jnp.concatenate([raw["rwkv_w2"][l], z(96)], axis=0)
    p["a2p"] = jnp.concatenate([z(32), raw["rwkv_a2"][l], z(64)], axis=0)
    p["g2p"] = jnp.concatenate([z(64), raw["rwkv_g2"][l]], axis=0)
    if l > 0:
        p["v0"] = raw["rwkv_v0"][l - 1][None, :]
        p["v2p"] = jnp.concatenate([raw["rwkv_v2"][l - 1], z(96)], axis=0)
    p["lb0"], p["lb1"] = raw["lower_bounds"][0:1], raw["lower_bounds"][1:2]
    p["hgrn_norm_w"], p["ssd_norm_w"] = row("hgrn_norm_w"), row("ssd_norm_w")
    p["dt_bias"], p["a_log"], p["ssd_d"] = (_pad_lanes(raw[n][l], 128) for n in ("ssd_dt_bias", "ssd_A_log", "ssd_D"))
    for n in ("ln1_w", "ln1_b", "ln2_w", "ln2_b"):
        p[n] = row(n)
    return p


def _natural_grads(g0, g1):
    gs = (g0, g1)
    st = lambda key, f=lambda a: a[0]: jnp.stack([f(g[key]) for g in gs])
    out = {}
    out["lower_bounds"] = jnp.concatenate([g0["lb0"] + g1["lb0"], g0["lb1"] + g1["lb1"]], axis=0)
    out["mu_shift"] = st("mu", lambda a: a[0, :896])
    out["mu_vres"] = g1["mu"][:, 896:928]
    out["rwkv_w0"], out["rwkv_a0"], out["rwkv_k_k"], out["rwkv_k_a"] = st("w0"), st("a0"), st("k_k"), st("k_a")
    out["rwkv_w2"] = st("w2p", lambda a: a[0:32])
    out["rwkv_a2"] = st("a2p", lambda a: a[32:64])
    out["rwkv_g2"] = st("g2p", lambda a: a[64:128])
    out["rwkv_r_k"] = st("r_k", lambda a: a.reshape(NH, HD))
    out["rwkv_lnx_w"], out["rwkv_lnx_b"] = st("lnx_w"), st("lnx_b")
    out["rwkv_v0"] = g1["v0"]
    out["rwkv_v2"] = g1["v2p"][None, 0:32]
    out["ssd_conv_w"] = st("conv_w", lambda a: a)
    out["ssd_conv_b"] = st("conv_b")
    out["ssd_dt_bias"], out["ssd_A_log"], out["ssd_D"] = (st(k, lambda a: a[0, :NH]) for k in ("dt_bias", "a_log", "ssd_d"))
    out["ssd_norm_w"], out["hgrn_norm_w"] = st("ssd_norm_w"), st("hgrn_norm_w")
    for n in ("ln1_w", "ln1_b", "ln2_w", "ln2_b"):
        out[n] = st(n)
    return out


MESH_T = pl.DeviceIdType.MESH
ANY = pl.BlockSpec(memory_space=pl.ANY)


def _dev_index(px, py, pc):
    return 4 * px + 2 * py + pc


def all_gather(arrs):
    n = len(arrs)

    def body(*refs):
        begin, end = _gather_steps(refs[:n], refs[n:2 * n], *refs[2 * n:])
        begin()
        end()

    return pl.pallas_call(body, in_specs=[ANY] * n, out_specs=[ANY] * n, out_shape=_gather_shapes(arrs),
                          scratch_shapes=_gather_sems(n), name="all_gather")(*arrs)


def _gather_shapes(arrs):
    return [_sds((N_DEV,) + a.shape, a.dtype) for a in arrs]


def _gather_sems(n):
    return [pltpu.SemaphoreType.DMA((n, 7)), pltpu.SemaphoreType.DMA((n, 7)), pltpu.SemaphoreType.DMA((n,))]


def _gather_steps(ins, outs, send_sems, recv_sems, local_sems):
    n = len(ins)
    x, y, c = lax.axis_index("x"), lax.axis_index("y"), lax.axis_index("c")
    me, sibling = (x, y, c), (x, y, 1 - c)
    chips = [(1 - x, y), (x, 1 - y), (1 - x, 1 - y)]

    def copy(a, k, block, to, src=None):
        slot = outs[a].at[_dev_index(*block)]
        return pltpu.make_async_remote_copy(src_ref=slot if src is None else src, dst_ref=slot,
                                            send_sem=send_sems.at[a, k], recv_sem=recv_sems.at[a, k],
                                            device_id=to, device_id_type=MESH_T)

    def own_copies():
        mine = [pltpu.make_async_copy(ins[a], outs[a].at[_dev_index(*me)], local_sems.at[a]) for a in range(n)]
        first = []
        for a in range(n):
            first.append(copy(a, 0, me, sibling, src=ins[a]))
            first += [copy(a, 1 + j, me, (*chip, c), src=ins[a]) for j, chip in enumerate(chips)]
        return mine, first

    def begin():
        mine, first = own_copies()
        for cp in mine + first:
            cp.start()

    def end():
        mine, first = own_copies()
        passed = []
        for j, chip in enumerate(chips):
            for a in range(n):
                copy(a, 1 + j, (*chip, c), me).wait_recv()
                fwd = copy(a, 4 + j, (*chip, c), sibling)
                fwd.start()
                passed.append(fwd)
        for a in range(n):
            copy(a, 0, sibling, me).wait_recv()
            for j, chip in enumerate(chips):
                copy(a, 4 + j, (*chip, 1 - c), me).wait_recv()
        for cp in first + passed:
            cp.wait_send()
        for cp in mine:
            cp.wait()

    return begin, end


def _chips(x, y):
    return [(x, y), (1 - x, y), (x, 1 - y), (1 - x, 1 - y)]


def _sibling_steps(ins, sib, send_sems, recv_sems):
    x, y, c = lax.axis_index("x"), lax.axis_index("y"), lax.axis_index("c")

    def copies():
        return [pltpu.make_async_remote_copy(src_ref=ins[a].at[_dev_index(cx, cy, 1 - c)], dst_ref=sib[a].at[k],
                                             send_sem=send_sems.at[a, k], recv_sem=recv_sems.at[a, k],
                                             device_id=(x, y, 1 - c), device_id_type=MESH_T)
                for a in range(len(ins)) for k, (cx, cy) in enumerate(_chips(x, y))]

    def begin():
        for cp in copies():
            cp.start()

    def end():
        cps = copies()
        for cp in cps:
            cp.wait_recv()
        for cp in cps:
            cp.wait_send()

    return begin, end


def exchange_siblings(arrs, name):
    n = len(arrs)

    def body(*refs):
        begin, end = _sibling_steps(refs[:n], refs[n:2 * n], *refs[2 * n:])
        begin()
        end()

    sem = pltpu.SemaphoreType.DMA((n, 4))
    return pl.pallas_call(body, in_specs=[ANY] * n, out_specs=[ANY] * n,
                          out_shape=[_sds((4,) + a.shape[1:], a.dtype) for a in arrs],
                          scratch_shapes=[sem, sem], name=name)(*arrs)


def reduce_pair(name, send, slots, sib, wire_dtype):
    _, r, c = send.shape
    rb = min(r, 262144 // c)

    def body(slots_ref, m0, m1, m2, m3, s_ref, own_ref, part_ref):
        own_ref[...] = m0[...] + s_ref[0]
        for k, m_ref in enumerate((m1, m2, m3)):
            part_ref[k] = (m_ref[...] + s_ref[k + 1]).astype(wire_dtype)

    mine = [pl.BlockSpec((None, rb, c), lambda i, s, k=k: (s[k], i, 0)) for k in range(4)]
    grid_spec = pltpu.PrefetchScalarGridSpec(
        num_scalar_prefetch=1, grid=(r // rb,),
        in_specs=mine + [pl.BlockSpec((4, rb, c), lambda i, s: (0, i, 0))],
        out_specs=[pl.BlockSpec((rb, c), lambda i, s: (i, 0)), pl.BlockSpec((3, rb, c), lambda i, s: (0, i, 0))])
    return pl.pallas_call(body, grid_spec=grid_spec, out_shape=[_sds((r, c)), _sds((3, r, c), wire_dtype)], name=name,
                          compiler_params=_cp(("parallel",)))(slots, send, send, send, send, sib)


def _chip_exchange_steps(ins, recv, send_sems, recv_sems):
    x, y, c = lax.axis_index("x"), lax.axis_index("y"), lax.axis_index("c")

    def copies():
        return [pltpu.make_async_remote_copy(src_ref=ins[a].at[k], dst_ref=recv[a].at[k], send_sem=send_sems.at[a, k],
                                             recv_sem=recv_sems.at[a, k], device_id=(cx, cy, c), device_id_type=MESH_T)
                for a in range(len(ins)) for k, (cx, cy) in enumerate(_chips(x, y)[1:])]

    def begin():
        for cp in copies():
            cp.start()

    def end():
        cps = copies()
        for cp in cps:
            cp.wait_recv()
        for cp in cps:
            cp.wait_send()

    return begin, end


def exchange_chips(parts, rep):
    n = len(parts)

    def body(*refs):
        ins, rep_ref = refs[:n], refs[n]
        recv, rep_all = refs[n + 1:2 * n + 1], refs[2 * n + 1]
        send_sems, recv_sems, rsend_sems, rrecv_sems, local_sem = refs[2 * n + 2:]
        x, y, c = lax.axis_index("x"), lax.axis_index("y"), lax.axis_index("c")
        me = _dev_index(x, y, c)
        mine = pltpu.make_async_copy(rep_ref, rep_all.at[me], local_sem)
        mine.start()
        begin, end = _chip_exchange_steps(ins, recv, send_sems, recv_sems)
        begin()
        rels = [(rx, ry, rc) for rx in (0, 1) for ry in (0, 1) for rc in (0, 1)][1:]
        peers = [(jnp.where(rx, 1 - x, x), jnp.where(ry, 1 - y, y), jnp.where(rc, 1 - c, c)) for rx, ry, rc in rels]
        rcps = []
        for k, peer in enumerate(peers):
            cp = pltpu.make_async_remote_copy(src_ref=rep_ref, dst_ref=rep_all.at[me], send_sem=rsend_sems.at[k],
                                              recv_sem=rrecv_sems.at[k], device_id=peer, device_id_type=MESH_T)
            cp.start()
            rcps.append(cp)
        for k, peer in enumerate(peers):
            pltpu.make_async_remote_copy(src_ref=rep_ref, dst_ref=rep_all.at[_dev_index(*peer)], send_sem=rsend_sems.at[k],
                                         recv_sem=rrecv_sems.at[k], device_id=peer, device_id_type=MESH_T).wait_recv()
        end()
        for cp in rcps:
            cp.wait_send()
        mine.wait()

    outs = pl.pallas_call(
        body, in_specs=[ANY] * (n + 1), out_specs=[ANY] * (n + 1),
        out_shape=[_sds(a.shape, a.dtype) for a in parts] + [_sds((N_DEV,) + rep.shape, rep.dtype)],
        scratch_shapes=[pltpu.SemaphoreType.DMA((n, 3)), pltpu.SemaphoreType.DMA((n, 3)), pltpu.SemaphoreType.DMA((7,)),
                        pltpu.SemaphoreType.DMA((7,)), pltpu.SemaphoreType.DMA],
        name="exchange_chips")(*parts, rep)
    return outs[:n], outs[n]


def adamw(name, terms, w, m, v, transposed=False):
    r, c = w.shape[::-1] if transposed else w.shape
    rb = r if transposed else min(r, 262144 // c)
    c1 = 1.0 - ADAM_B1 ** ADAM_STEP
    c2 = 1.0 - ADAM_B2 ** ADAM_STEP
    nt = len(terms)

    def body(*refs):
        w_ref, m_ref, v_ref = refs[nt:nt + 3]
        g_ref, d_ref, nm_ref, nv_ref = refs[nt + 3:]
        g = refs[0][...].astype(F32)
        for t_ref in refs[1:nt]:
            g = g + t_ref[...].astype(F32)
        if transposed:
            g = g.T
        nm = ADAM_B1 * m_ref[...] + (1.0 - ADAM_B1) * g
        nv = ADAM_B2 * v_ref[...] + (1.0 - ADAM_B2) * (g * g)
        g_ref[...] = g
        nm_ref[...] = nm
        nv_ref[...] = nv
        d_ref[...] = -ADAM_LR * ((nm / c1) / (jnp.sqrt(nv / c2) + ADAM_EPS) + ADAM_WD * w_ref[...])

    blk = pl.BlockSpec((rb, c), lambda i: (i, 0))
    wblk = pl.BlockSpec((c, r), lambda i: (0, 0)) if transposed else blk
    tspecs = [blk if k is None else pl.BlockSpec((None, rb, c), lambda i, k=k: (k, i, 0)) for _, k in terms]
    return pl.pallas_call(body, grid=(r // rb,), in_specs=tspecs + [wblk] * 3, out_specs=[wblk] * 4,
                          out_shape=[_sds(w.shape)] * 4, name=name,
                          compiler_params=_cp(("parallel",)))(*[t for t, _ in terms], w, m, v)


SMS_ROWS = 16
REP_ROWS = 24
N_BIG = 8
SMALL_SHARDED = (("rwkv_w2", (2, 32, 32)), ("rwkv_a2", (2, 32, 32)), ("rwkv_g2", (2, 64, 32)), ("rwkv_v2", (1, 32, 32)),
                 ("ssd_conv_w", (2, 4, 96)))
REPLICATED = (("lower_bounds", (2, 256)), ("mu_shift", (2, 896)), ("mu_vres", (1, 32)), ("rwkv_w0", (2, 256)),
              ("rwkv_a0", (2, 256)), ("rwkv_k_k", (2, 256)), ("rwkv_k_a", (2, 256)), ("rwkv_r_k", (2, 4, 64)),
              ("rwkv_lnx_w", (2, 256)), ("rwkv_lnx_b", (2, 256)), ("rwkv_v0", (1, 256)), ("ssd_conv_b", (2, 768)),
              ("ssd_dt_bias", (2, 4)), ("ssd_A_log", (2, 4)), ("ssd_D", (2, 4)), ("ssd_norm_w", (2, 256)),
              ("hgrn_norm_w", (2, 256)), ("ln1_w", (2, 1024)), ("ln1_b", (2, 1024)), ("ln2_w", (2, 1024)),
              ("ln2_b", (2, 1024)))


def _flat_rows(parts, rows):
    flat = jnp.concatenate([a.reshape(-1) for a in parts])
    return jnp.concatenate([flat, jnp.zeros((rows * PACK_W - flat.shape[0],), flat.dtype)]).reshape(rows, PACK_W)


def _local_arrays(d):
    arrs = [_w_in_pad(d["w_in"][0], None), _w_in_pad(d["w_in"][1], d["w_in_vres"][0]), d["w_out"][0], d["w_out"][1],
            d["w_up"][0], d["w_up"][1], d["w_down"][0], d["w_down"][1],
            _flat_rows([d[n] for n, _ in SMALL_SHARDED], SMS_ROWS)]
    return arrs, _flat_rows([d[n] for n, _ in REPLICATED], REP_ROWS)


def _unflat(rows2d, table):
    flat, out, o = rows2d.reshape(-1), {}, 0
    for name, shape in table:
        n = 1
        for s in shape:
            n *= s
        out[name] = flat[o:o + n].reshape(shape)
        o += n
    return out


def _from_local_arrays(arrs, rep):
    d = {}
    g0, _ = _w_in_unpad(arrs[0])
    g1, gv = _w_in_unpad(arrs[1])
    d["w_in"], d["w_in_vres"] = jnp.stack([g0, g1]), gv[None]
    d["w_out"] = jnp.stack([arrs[2], arrs[3]])
    d["w_up"] = jnp.stack([arrs[4], arrs[5]])
    d["w_down"] = jnp.stack([arrs[6], arrs[7]])
    d.update(_unflat(arrs[8], SMALL_SHARDED))
    d.update(_unflat(rep, REPLICATED))
    return d


def _small_sharded_full(gs):
    small, flat, o = {}, gs.reshape(N_DEV, -1), 0
    for name, shape in SMALL_SHARDED:
        n = shape[0] * shape[1] * shape[2]
        blk = flat[:, o:o + n].reshape((N_DEV,) + shape)
        small[name] = blk.transpose(1, 2, 0, 3).reshape(shape[0], shape[1], N_DEV * shape[2])
        o += n
    return small


def _owner_blocks(g):
    return g.reshape(N_DEV, g.shape[0] // N_DEV, g.shape[1])


def _small_send_arrays(small_grads):
    sms = []
    for name, shape in SMALL_SHARDED:
        g = small_grads[name].reshape(shape[0], shape[1], N_DEV, shape[2]).transpose(2, 0, 1, 3)
        sms.append(g.reshape(N_DEV, -1))
    sms = jnp.concatenate(sms, axis=1)
    sms = jnp.concatenate([sms, jnp.zeros((N_DEV, SMS_ROWS * PACK_W - sms.shape[1]), F32)], axis=1)
    return sms.reshape(N_DEV, SMS_ROWS, PACK_W), _flat_rows([small_grads[n] for n, _ in REPLICATED], REP_ROWS)


BIG_KEYS = ("w_in", "w_out", "w_up_t", "w_down")


def _weights_of(full):
    return dict(zip(BIG_KEYS, full))


def _local_step(x, tgt, wts, raw, gather=(), pair_sums=None, reducer=None):
    consts = _consts()
    ps = [_layer_params(l, raw, consts) for l in range(DEPTH)]
    x1, sv0 = layer_fwd(0, x, None, wts[0], ps[0], gather[:4], late=bool(gather))
    wts1 = {"w_in": _full_rows(sv0["gathered"][3])} if gather else wts[1]
    x2, sv1 = layer_fwd(1, x1, sv0["fl"], wts1, ps[1], gather[4:], late=bool(gather))
    dy, lparts = loss_call(x2, tgt)
    loss = jnp.sum(lparts[::8, 0])
    dx1, dvfirst, g1 = layer_bwd(1, dy, None, sv1, sv1["wts"], ps[1], (), reducer)
    big1 = {k: g1[k] for k in BIG_KEYS}
    if reducer is None:
        dx0, _, g0 = layer_bwd(0, dx1, dvfirst, sv0, sv0["wts"], ps[0])
        early = None
    else:
        own_in1, parts_in1 = pair_sums("1b", {"w_in": g1["w_in"]})
        dx0, _, g0 = layer_bwd(0, dx1, dvfirst, sv0, sv0["wts"], ps[0], parts_in1, reducer)
        own, recv = {(1, "w_in"): own_in1[0]}, {(1, "w_in"): g0["exchanged"][0]}
        for l, g, first in ((1, g1, 0), (0, g0, 1)):
            for i, k in enumerate(LATE_KEYS):
                own[(l, k)], recv[(l, k)] = g["early_own"][i], g["exchanged"][first + i]
        early = (own, recv)
    big = [{k: g0[k] for k in BIG_KEYS}, big1]
    return loss, dx0, big, _natural_grads(g0, g1), early


WEIGHT_NAMES = ("lower_bounds", "w_in", "w_in_vres", "mu_shift", "mu_vres", "rwkv_w0", "rwkv_w2", "rwkv_a0", "rwkv_a2",
                "rwkv_g2", "rwkv_k_k", "rwkv_k_a", "rwkv_r_k", "rwkv_lnx_w", "rwkv_lnx_b", "rwkv_v0", "rwkv_v2",
                "ssd_conv_w", "ssd_conv_b", "ssd_dt_bias", "ssd_A_log", "ssd_D", "ssd_norm_w", "hgrn_norm_w", "w_out",
                "ln1_w", "ln1_b", "w_up", "w_down", "ln2_w", "ln2_b")


def kernel(x, lower_bounds, w_in, w_in_vres, mu_shift, mu_vres, rwkv_w0, rwkv_w2, rwkv_a0, rwkv_a2, rwkv_g2, rwkv_k_k, rwkv_k_a, rwkv_r_k, rwkv_lnx_w, rwkv_lnx_b, rwkv_v0, rwkv_v2, ssd_conv_w, ssd_conv_b, ssd_dt_bias, ssd_A_log, ssd_D, ssd_norm_w, hgrn_norm_w, w_out, ln1_w, ln1_b, w_up, w_down, ln2_w, ln2_b, loss_target, m_lower_bounds, m_w_in, m_w_in_vres, m_mu_shift, m_mu_vres, m_rwkv_w0, m_rwkv_w2, m_rwkv_a0, m_rwkv_a2, m_rwkv_g2, m_rwkv_k_k, m_rwkv_k_a, m_rwkv_r_k, m_rwkv_lnx_w, m_rwkv_lnx_b, m_rwkv_v0, m_rwkv_v2, m_ssd_conv_w, m_ssd_conv_b, m_ssd_dt_bias, m_ssd_A_log, m_ssd_D, m_ssd_norm_w, m_hgrn_norm_w, m_w_out, m_ln1_w, m_ln1_b, m_w_up, m_w_down, m_ln2_w, m_ln2_b, v_lower_bounds, v_w_in, v_w_in_vres, v_mu_shift, v_mu_vres, v_rwkv_w0, v_rwkv_w2, v_rwkv_a0, v_rwkv_a2, v_rwkv_g2, v_rwkv_k_k, v_rwkv_k_a, v_rwkv_r_k, v_rwkv_lnx_w, v_rwkv_lnx_b, v_rwkv_v0, v_rwkv_v2, v_ssd_conv_w, v_ssd_conv_b, v_ssd_dt_bias, v_ssd_A_log, v_ssd_D, v_ssd_norm_w, v_hgrn_norm_w, v_w_out, v_ln1_w, v_ln1_b, v_w_up, v_w_down, v_ln2_w, v_ln2_b):
    given = dict(locals())
    w = {n: given[n] for n in WEIGHT_NAMES}
    w_arrs, w_rep = _local_arrays(w)
    m_arrs, m_rep = _local_arrays({n: given["m_" + n] for n in WEIGHT_NAMES})
    v_arrs, v_rep = _local_arrays({n: given["v_" + n] for n in WEIGHT_NAMES})
    wire = lambda a: (w_arrs[a].T if a in (4, 5) else w_arrs[a]).astype(BF16)
    gathered0 = all_gather([wire(0), w_arrs[N_BIG]])
    raw = {n: w[n] for n, _ in REPLICATED}
    raw.update(_small_sharded_full(gathered0[1]))
    mx, my, mc = lax.axis_index("x"), lax.axis_index("y"), lax.axis_index("c")
    slots = jnp.stack([_dev_index(cx, cy, mc) for cx, cy in _chips(mx, my)]).astype(jnp.int32)

    def reducer(tag, send, sib, n_f32=0):
        wire_dt = [BF16] * (len(send) - n_f32) + [F32] * n_f32
        res = [reduce_pair(f"reduce_pair{tag}_{i}", s, slots, sb, dt) for i, (s, sb, dt) in enumerate(zip(send, sib, wire_dt))]
        return [o for o, _ in res], [pt for _, pt in res]

    def pair_sums(tag, grads, extra=()):
        send = [_owner_blocks(g) for g in grads.values()] + list(extra)
        return reducer(tag, send, exchange_siblings(send, f"exchange_siblings{tag}"), len(extra))

    behind_scan = [wire(a) for a in (2, 4, 6, 1, 3, 5, 7)]
    loss, dx, big, small_grads, (own_by, recv_by) = _local_step(
        x[0], loss_target[0], [{"w_in": _full_rows(gathered0[0])}, None], raw, behind_scan, pair_sums, reducer)
    sms_send, rep = _small_send_arrays(small_grads)
    own0b, parts0b = pair_sums("0b", {"w_in": big[0]["w_in"]}, [sms_send])
    recv0b, rep_all = exchange_chips(parts0b, rep)
    own, recv = [None] * (N_BIG + 1), [None] * (N_BIG + 1)
    for (l, k), o in own_by.items():
        a = 2 * BIG_KEYS.index(k) + l
        own[a], recv[a] = o, recv_by[(l, k)]
    for a, o, r in zip((0, N_BIG), own0b, recv0b):
        own[a], recv[a] = o, r
    results = [adamw(f"adamw{a}", [(own[a], None), (recv[a], 0), (recv[a], 1), (recv[a], 2)], w_arrs[a], m_arrs[a], v_arrs[a],
                     transposed=a in (4, 5)) for a in range(N_BIG + 1)]
    rep_res = adamw("adamw_rep", [(rep_all, q) for q in range(N_DEV)], w_rep, m_rep, v_rep)
    loss = lax.psum(loss, ("x", "y", "c"))
    outs = [loss, dx[None]]
    for q in range(4):
        d = _from_local_arrays([res[q] for res in results], rep_res[q])
        outs += [d[n] for n in WEIGHT_NAMES]
    return tuple(outs)
```

```python
import functools

import jax
import jax.numpy as jnp
from jax import lax
from jax.experimental import pallas as pl
from jax.experimental.pallas import tpu as pltpu

F32 = jnp.float32
BF16 = jnp.bfloat16
HI = lax.Precision.HIGHEST

N_DEV = 8
SEQ = 2048
D_MODEL = 1024
D_FF = 4096
DG = 256
NH = 4
HD = 64
DEPTH = 2
ALPHA = (2.0 * DEPTH) ** 0.25
LN_EPS = 1e-5
RMS_EPS = 1e-5
GN_EPS = HD * 1e-5
SSD_N = 128
SSD_CHUNK = 128
HGRN_CHUNK = 16
DILATED = ((128, 1), (512, 4), (2048, 16))

ADAM_LR, ADAM_B1, ADAM_B2, ADAM_EPS, ADAM_WD, ADAM_STEP = 0.001, 0.9, 0.999, 1e-08, 0.01, 10

PW = 4096
C_R, C_K, C_V = 0, 256, 512
C_AQ, C_AK, C_AV = 768, 1024, 1280
C_Z, C_XBC = 1536, 1792
C_HQ, C_HF, C_HI, C_HG = 2560, 2816, 3072, 3328
C_LORA, C_DT, C_VRES = 3584, 3712, 3840

RB = 256
VMEM_LIMIT = 56 * 1024 * 1024
PACK_W = 1024


def _cp(sem=None):
    return pltpu.CompilerParams(dimension_semantics=sem, vmem_limit_bytes=VMEM_LIMIT)


def _sds(shape, dt=F32):
    return jax.ShapeDtypeStruct(tuple(shape), dt)


def _rows(w, cb=0, rb=RB):
    return pl.BlockSpec((rb, w), lambda i: (i, cb))


def _full(shape):
    n = len(shape)
    return pl.BlockSpec(tuple(shape), lambda *_: (0,) * n)


def _sigmoid(x):
    return 1.0 / (1.0 + jnp.exp(-x))


def _silu(x):
    return x * _sigmoid(x)


def _softplus(x):
    return jnp.maximum(x, 0.0) + jnp.log(1.0 + jnp.exp(jnp.where(x > 0, -x, x)))


MID = lax.Precision.HIGH
NN, TN, NT = (((1,), (0,)), ((), ())), (((0,), (0,)), ((), ())), (((1,), (1,)), ((), ()))


def _dot(a, b):
    return lax.dot_general(a, b, NN, precision=MID, preferred_element_type=F32)


def _dot_tn(a, b):
    return lax.dot_general(a, b, TN, precision=MID, preferred_element_type=F32)


def _dot_nt(a, b):
    return lax.dot_general(a, b, NT, precision=MID, preferred_element_type=F32)


def _dotx(a, b):
    return lax.dot_general(a, b, NN, precision=HI, preferred_element_type=F32)


def _dotx_tn(a, b):
    return lax.dot_general(a, b, TN, precision=HI, preferred_element_type=F32)


def _seg_ones(n, seg):
    i = jnp.arange(n)
    return (i[:, None] // seg == i[None, :] // seg).astype(F32)


def _shift_down(x, s):
    row = lax.broadcasted_iota(jnp.int32, x.shape, 0)
    return jnp.where(row < s, 0.0, pltpu.roll(x, s, 0))


def _shift_up(x, s):
    n = x.shape[0]
    row = lax.broadcasted_iota(jnp.int32, x.shape, 0)
    return jnp.where(row >= n - s, 0.0, pltpu.roll(x, n - s, 0))


@functools.partial(jax.custom_vjp, nondiff_argnums=(1,))
def _tshift(x, s):
    return _shift_down(x, s)


def _tshift_fwd(x, s):
    return _shift_down(x, s), None


def _tshift_bwd(s, _, g):
    return (_shift_up(g, s),)


_tshift.defvjp(_tshift_fwd, _tshift_bwd)


def _map_fwd(name, fn, grid, ins, in_specs, out_shapes, out_specs):
    n_in = len(ins)

    def body(*refs):
        ys = fn(*[r[...] for r in refs[:n_in]])
        for r, y in zip(refs[n_in:], ys):
            r[...] = y

    return pl.pallas_call(body, grid=grid, in_specs=in_specs, out_specs=out_specs, out_shape=out_shapes,
                          name=name, compiler_params=_cp(("parallel",)))(*ins)


def _map_bwd(name, fn, grid, ins, in_specs, cts, ct_specs, want, acc=(), gout=None):
    n_in = len(ins)
    flat_cts = [c for group in cts for c in group]
    flat_specs = [s for group in ct_specs for s in group]
    n_ct = len(flat_cts)
    gout = gout or {}
    out_shapes = [gout[i][0] if i in gout else _sds(ins[i].shape) for i in want]
    out_specs = [gout[i][1] if i in gout else in_specs[i] for i in want]

    def body(*refs):
        xs = [r[...] for r in refs[:n_in]]
        cvals = [r[...] for r in refs[n_in:n_in + n_ct]]
        gouts = refs[n_in + n_ct:]
        cs, p = [], 0
        for group in cts:
            v = cvals[p]
            for q in range(1, len(group)):
                v = v + cvals[p + q]
            cs.append(v)
            p += len(group)

        def f(*wanted):
            full = list(xs)
            for i, w in zip(want, wanted):
                full[i] = w
            return tuple(fn(*full))

        _, vjp = jax.vjp(f, *[xs[i] for i in want])
        gs = vjp(tuple(cs))
        for o, i, g in zip(gouts, want, gs):
            if i in acc:
                @pl.when(pl.program_id(0) == 0)
                def _():
                    o[...] = jnp.zeros_like(o)

                o[...] += g
            else:
                o[...] = g

    sem = ("arbitrary",) if acc else ("parallel",)
    return pl.pallas_call(body, grid=grid, in_specs=list(in_specs) + flat_specs, out_specs=out_specs,
                          out_shape=out_shapes, name=name, compiler_params=_cp(sem))(*ins, *flat_cts)


def _addn(name, *arrs):
    n, c = arrs[0].shape

    def fn(*xs):
        r = xs[0]
        for x in xs[1:]:
            r = r + x
        return (r,)

    return _map_fwd(name, fn, (n // RB,), list(arrs), [_rows(c)] * len(arrs), [_sds((n, c))], [_rows(c)])[0]


MM_TILES = {"k1024": (2048, 512, 1024), "k4096": (1024, 1024, 1024), "wgrad_tall": (2048, 1024, 512),
            "wgrad_wide": (1024, 2048, 512)}


def _mm(name, a, b, mode, tm, tn, tk, add=None, add_scale=1.0, epilogue=None, out_dtype=F32):
    if mode == "nn":
        (m, k), n = a.shape, b.shape[1]
    elif mode == "nt":
        (m, k), n = a.shape, b.shape[0]
    else:
        (k, m), n = a.shape, b.shape[1]
    nk = k // tk
    dn = {"nn": (((1,), (0,)), ((), ())), "nt": (((1,), (1,)), ((), ())), "tn": (((0,), (0,)), ((), ()))}[mode]

    def body(*refs):
        a_ref, b_ref = refs[:2]
        add_ref = refs[2] if add is not None else None
        o_ref = refs[3] if add is not None else refs[2]
        prod = lax.dot_general(a_ref[...].astype(BF16), b_ref[...].astype(BF16), dn, preferred_element_type=F32)

        def finish(r):
            if epilogue == "relu2":
                r = jnp.maximum(r, 0.0)
                r = r * r
            elif epilogue == "relu2_bwd":
                r = r * (2.0 * jnp.sqrt(add_ref[...]))
            elif add is not None:
                r = r + add_scale * add_ref[...]
            o_ref[...] = r.astype(out_dtype)

        if nk == 1:
            finish(prod)
        else:
            acc = refs[-1]
            kk = pl.program_id(2)

            @pl.when(kk == 0)
            def _():
                acc[...] = prod

            @pl.when(kk > 0)
            def _():
                acc[...] += prod

            @pl.when(kk == nk - 1)
            def _():
                finish(acc[...])

    a_spec = pl.BlockSpec((tk, tm), lambda i, j, q: (q, i)) if mode == "tn" else pl.BlockSpec((tm, tk), lambda i, j, q: (i, q))
    b_spec = pl.BlockSpec((tn, tk), lambda i, j, q: (j, q)) if mode == "nt" else pl.BlockSpec((tk, tn), lambda i, j, q: (q, j))
    o_spec = pl.BlockSpec((tm, tn), lambda i, j, q: (i, j))
    ins, specs = [a, b], [a_spec, b_spec]
    if add is not None:
        ins.append(add)
        specs.append(o_spec)
    return pl.pallas_call(body, grid=(m // tm, n // tn, nk), in_specs=specs, out_specs=o_spec,
                          out_shape=_sds((m, n), out_dtype),
                          scratch_shapes=[pltpu.VMEM((tm, tn), F32)] if nk > 1 else [], name=name,
                          compiler_params=_cp(("parallel", "parallel", "arbitrary")))(*ins)


def _lerp_colmap(j):
    r = jnp.where(j < 6, j, jnp.where(j == 6, C_LORA // 128, C_VRES // 128))
    return (0, r)


def _lerp_fn(f, mu):
    return (f + (_tshift(f, 1) - f) * mu,)


def _lerp_specs():
    return [pl.BlockSpec((SEQ, 128), _lerp_colmap), pl.BlockSpec((1, 128), lambda j: (0, j))]


def lerp_fwd(l, proj, mu):
    return _map_fwd(f"lerp_fwd{l}", _lerp_fn, (8,), [proj, mu], _lerp_specs(), [_sds((SEQ, 1024))],
                    [pl.BlockSpec((SEQ, 128), lambda j: (0, j))])[0]


def lerp_bwd(l, proj, mu, dfl):
    n_in = 2

    def body(f_ref, mu_ref, g_ref, df_ref, dmu_ref):
        _, vjp = jax.vjp(_lerp_fn, f_ref[...], mu_ref[...])
        df, dmu = vjp((g_ref[...],))
        df_ref[...] = df
        dmu_ref[...] = dmu

    cspec = pl.BlockSpec((SEQ, 128), lambda j: (0, j))
    return pl.pallas_call(body, grid=(8,), in_specs=_lerp_specs() + [cspec],
                          out_specs=[cspec, pl.BlockSpec((1, 128), lambda j: (0, j))],
                          out_shape=[_sds((SEQ, 1024)), _sds((1, 1024))], name=f"lerp_bwd{l}",
                          compiler_params=_cp(("parallel",)))(proj, mu, dfl)


def _conv_fn(x, w, b):
    y = x * w[3:4, :] + _tshift(x, 1) * w[2:3, :] + _tshift(x, 2) * w[1:2, :] + _tshift(x, 3) * w[0:1, :] + b
    return (_silu(y),)


def _conv_specs():
    return [pl.BlockSpec((SEQ, 128), lambda j: (0, C_XBC // 128 + j)), pl.BlockSpec((4, 128), lambda j: (0, j)),
            pl.BlockSpec((1, 128), lambda j: (0, j))]


def conv_fwd(l, proj, w, b):
    return _map_fwd(f"conv_fwd{l}", _conv_fn, (6,), [proj, w, b], _conv_specs(), [_sds((SEQ, 768))],
                    [pl.BlockSpec((SEQ, 128), lambda j: (0, j))])[0]


def conv_bwd(l, proj, w, b, dxc):
    def body(x_ref, w_ref, b_ref, g_ref, dx_ref, dw_ref, db_ref):
        _, vjp = jax.vjp(_conv_fn, x_ref[...], w_ref[...], b_ref[...])
        dx, dw, db = vjp((g_ref[...],))
        dx_ref[...] = dx
        dw_ref[...] = dw
        db_ref[...] = db

    cspec = pl.BlockSpec((SEQ, 128), lambda j: (0, j))
    return pl.pallas_call(body, grid=(6,), in_specs=_conv_specs() + [cspec],
                          out_specs=[cspec, pl.BlockSpec((4, 128), lambda j: (0, j)), pl.BlockSpec((1, 128), lambda j: (0, j))],
                          out_shape=[_sds((SEQ, 768)), _sds((4, 768)), _sds((1, 768))], name=f"conv_bwd{l}",
                          compiler_params=_cp(("parallel",)))(proj, w, b, dxc)


def _rwkv_pre_fn(has_vres):
    def fn(fk, fv, flora, *rest):
        if has_vres:
            fvres, vfirst, w0, w2p, a0, a2p, g2p, k_k, k_a, v0, v2p, seg = rest
        else:
            w0, w2p, a0, a2p, g2p, k_k, k_a, seg = rest
        w_log = -_softplus(-(w0 + _dot(jnp.tanh(flora), w2p))) - 0.5
        w = jnp.exp(-jnp.exp(w_log))
        a = _sigmoid(a0 + _dot(flora, a2p))
        g = _dot(_sigmoid(flora), g2p)
        if has_vres:
            v2 = fv + (vfirst - fv) * _sigmoid(v0 + _dot(fvres, v2p))
        else:
            v2 = fv * 1.0
        kk = fk * k_k
        kk = kk / jnp.maximum(jnp.sqrt(_dot(kk * kk, seg)), 1e-12)
        k2 = fk * (1.0 + (a - 1.0) * k_a)
        return w, k2, v2, -kk, kk * a, g

    return fn


def _rwkv_pre_args(fl, vfirst, p, has_vres):
    ins = [fl, fl, fl]
    specs = [_rows(256, 1), _rows(256, 2), _rows(128, 6)]
    if has_vres:
        ins += [fl, vfirst]
        specs += [_rows(128, 7), _rows(256, 2)]
    names = ["w0", "w2p", "a0", "a2p", "g2p", "k_k", "k_a"] + (["v0", "v2p"] if has_vres else []) + ["seg64"]
    for nme in names:
        ins.append(p[nme])
        specs.append(_full(p[nme].shape))
    return ins, specs, names


def rwkv_pre_fwd(l, fl, vfirst, p):
    has_vres = l > 0
    ins, specs, _ = _rwkv_pre_args(fl, vfirst, p, has_vres)
    return _map_fwd(f"rwkv_pre_fwd{l}", _rwkv_pre_fn(has_vres), (SEQ // RB,), ins, specs,
                    [_sds((SEQ, DG))] * 6, [_rows(DG)] * 6)


def rwkv_pre_bwd(l, fl, vfirst, p, cts):
    has_vres = l > 0
    ins, specs, names = _rwkv_pre_args(fl, vfirst, p, has_vres)
    n_row = 5 if has_vres else 3
    want = list(range(n_row)) + [n_row + i for i, nme in enumerate(names) if nme != "seg64"]
    acc = tuple(w for w in want if w >= n_row)
    ct_specs = [[_rows(DG)] * len(g) for g in cts]
    gout = {0: (_sds((SEQ, DG)), _rows(DG)), 1: (_sds((SEQ, DG)), _rows(DG)), 2: (_sds((SEQ, 128)), _rows(128))}
    if has_vres:
        gout[3] = (_sds((SEQ, 128)), _rows(128))
        gout[4] = (_sds((SEQ, DG)), _rows(DG))
    gs = _map_bwd(f"rwkv_pre_bwd{l}", _rwkv_pre_fn(has_vres), (SEQ // RB,), ins, specs, cts, ct_specs, want, acc, gout)
    keys = ["fk", "fv", "flora"] + (["fvres", "vfirst"] if has_vres else []) + [nme for nme in names if nme != "seg64"]
    return dict(zip(keys, gs))


def _rwkv_post_fn(y, fr, k2, v2, g, lnx_w, lnx_b, r_k, seg):
    mu = _dot(y, seg) * (1.0 / HD)
    d = y - mu
    var = _dot(d * d, seg) * (1.0 / HD)
    yn = d * lax.rsqrt(var + GN_EPS) * lnx_w + lnx_b
    bonus = _dot(fr * k2 * r_k, seg) * v2
    return ((yn + bonus) * g,)


def _rwkv_post_args(y, fl, k2, v2, g, p):
    ins = [y, fl, k2, v2, g, p["lnx_w"], p["lnx_b"], p["r_k"], p["seg64"]]
    specs = [_rows(DG), _rows(DG, 0), _rows(DG), _rows(DG), _rows(DG)] + [_full(x.shape) for x in ins[5:]]
    return ins, specs


def rwkv_post_fwd(l, y, fl, k2, v2, g, p):
    ins, specs = _rwkv_post_args(y, fl, k2, v2, g, p)
    return _map_fwd(f"rwkv_post_fwd{l}", _rwkv_post_fn, (SEQ // RB,), ins, specs, [_sds((SEQ, DG))], [_rows(DG)])[0]


def rwkv_post_bwd(l, y, fl, k2, v2, g, p, dya):
    ins, specs = _rwkv_post_args(y, fl, k2, v2, g, p)
    gs = _map_bwd(f"rwkv_post_bwd{l}", _rwkv_post_fn, (SEQ // RB,), ins, specs, [[dya]], [[_rows(DG, 0)]],
                  want=[0, 1, 2, 3, 4, 5, 6, 7], acc=(5, 6, 7), gout={1: (_sds((SEQ, DG)), _rows(DG))})
    return dict(zip(["y", "fr", "k2", "v2", "g", "lnx_w", "lnx_b", "r_k"], gs))


SCAN_TB = 128


def _coltile8(rows8, dmask, ones_stack, parts):
    pieces, rest = [], rows8
    for q in range(parts):
        piece = rest.astype(BF16).astype(F32)
        if q < parts - 1:
            rest = rest - piece
        pieces.append((piece[:, None, :] * dmask[None]).reshape(8 * HD, DG).astype(BF16))
    x = pieces[0] if parts == 1 else jnp.concatenate(pieces, axis=1)
    return jnp.dot(x, ones_stack, preferred_element_type=F32).reshape(8, HD, DG)


def _coltiles_bf16(rows_list, dmask, ones_bf16):
    x = jnp.concatenate([(r8[:, None, :] * dmask[None]).reshape(8 * HD, DG).astype(BF16) for r8 in rows_list], axis=0)
    t = jnp.dot(x, ones_bf16, preferred_element_type=F32)
    return [t[q * 8 * HD:(q + 1) * 8 * HD].reshape(8, HD, DG) for q in range(len(rows_list))]


def _segrows8(x8, dmask, ones_bf16):
    t = jnp.dot(x8.reshape(8 * HD, DG).astype(BF16), ones_bf16, preferred_element_type=F32).reshape(8, HD, DG)
    return jnp.sum(t * dmask[None], axis=1)


def rwkv_scan_fwd(l, fl, w, k2, v2, c, b, p, gather=()):
    nblk = SEQ // SCAN_TB
    ng = len(gather)

    def body(*refs):
        r_ref, w_ref, k_ref, v_ref, c_ref, b_ref, ones_ref, dm_ref = refs[:8]
        y_ref, st_ref = refs[8 + ng:10 + ng]
        s_sc = refs[10 + 2 * ng]
        if ng:
            begin, middle, end = _gather_steps(refs[8:8 + ng], refs[10 + ng:10 + 2 * ng], *refs[11 + 2 * ng:])

            @pl.when(pl.program_id(0) == 0)
            def _():
                begin()

            @pl.when(pl.program_id(0) == (3 * nblk) // 4)
            def _():
                middle()

        @pl.when(pl.program_id(0) == 0)
        def _():
            s_sc[...] = jnp.zeros_like(s_sc)

        ones3, ones = ones_ref[...], ones_ref[0:DG, :]
        dmask = dm_ref[...]

        def group(gi, carry):
            t0 = pl.multiple_of(gi * 8, 8)
            sl = pl.ds(t0, 8)
            v8 = v_ref[sl, :]
            wt = _coltile8(w_ref[sl, :], dmask, ones3, 3)
            ct, bt, kt, rt = _coltiles_bf16([c_ref[sl, :], b_ref[sl, :], k_ref[sl, :], r_ref[sl, :]], dmask, ones)
            t = s_sc[...]
            for j in range(8):
                sa = jnp.sum(t * ct[j], axis=0, keepdims=True)
                t = t * wt[j] + bt[j] * sa + kt[j] * v8[j:j + 1, :]
                st_ref[t0 + j] = t
            s_sc[...] = t
            y_ref[sl, :] = jnp.sum(st_ref[sl] * rt, axis=1)
            return carry

        lax.fori_loop(0, SCAN_TB // 8, group, 0)

        if ng:
            @pl.when(pl.program_id(0) == nblk - 1)
            def _():
                end()

    row = pl.BlockSpec((SCAN_TB, DG), lambda i: (i, 0))
    ins = [fl, w, k2, v2, c, b, p["seg64x3_bf16"], p["dmask"]] + list(gather)
    specs = [row] * 6 + [_full((3 * DG, DG)), _full((HD, DG))] + [ANY] * ng
    outs = pl.pallas_call(body, grid=(nblk,), in_specs=specs,
                          out_specs=[row, pl.BlockSpec((SCAN_TB, HD, DG), lambda i: (i, 0, 0))] + [ANY] * ng,
                          out_shape=[_sds((SEQ, DG)), _sds((SEQ, HD, DG))] + _gather_shapes(gather),
                          scratch_shapes=[pltpu.VMEM((HD, DG), F32)] + (_gather_sems(ng) if ng else []),
                          name=f"rwkv_scan_fwd{l}", compiler_params=_cp(("arbitrary",)))(*ins)
    return outs[0], outs[1], list(outs[2:])


def rwkv_scan_bwd(l, fl, w, k2, v2, c, b, states, dy, p, exchange=()):
    nblk = SEQ // SCAN_TB
    nx = len(exchange)

    def body(*refs):
        r_ref, w_ref, k_ref, v_ref, c_ref, b_ref, dy_ref, st_ref, sp_ref, ones_ref, dm_ref = refs[:11]
        dr_ref, dw_ref, dk_ref, dv_ref, dc_ref, db_ref = refs[11 + nx:17 + nx]
        g_sc, prev_sc, d8_sc, dsa_sc = refs[17 + 2 * nx:21 + 2 * nx]
        i = pl.program_id(0)
        if nx:
            begin, end = _chip_exchange_steps(refs[11:11 + nx], refs[17 + nx:17 + 2 * nx], *refs[21 + 2 * nx:])

            @pl.when(i == 0)
            def _():
                begin()

        @pl.when(i == 0)
        def _():
            g_sc[...] = jnp.zeros_like(g_sc)

        ones3, ones = ones_ref[...], ones_ref[0:DG, :]
        dmask = dm_ref[...]
        first_block = i == nblk - 1

        def group(gr, carry):
            gi = SCAN_TB // 8 - 1 - gr
            t0 = pl.multiple_of(gi * 8, 8)
            sl = pl.ds(t0, 8)
            v8, dy8 = v_ref[sl, :], dy_ref[sl, :]
            t8 = st_ref[sl]
            @pl.when(gi > 0)
            def _():
                prev_sc[0] = st_ref[t0 - 1]

            @pl.when(gi == 0)
            def _():
                prev_sc[0] = jnp.where(first_block, 0.0, sp_ref[0])

            for j in range(1, 8):
                prev_sc[j] = t8[j - 1]
            tp8 = prev_sc[...]
            wt = _coltile8(w_ref[sl, :], dmask, ones3, 3)
            ct, bt, kt, rt = _coltiles_bf16([c_ref[sl, :], b_ref[sl, :], k_ref[sl, :], r_ref[sl, :]], dmask, ones)
            sa8 = jnp.sum(tp8 * ct, axis=1)
            g = g_sc[...]
            for j in range(7, -1, -1):
                g = g + rt[j] * dy8[j:j + 1, :]
                d8_sc[j] = g
                dsa = jnp.sum(g * bt[j], axis=0, keepdims=True)
                dsa_sc[j:j + 1, :] = dsa
                g = g * wt[j] + ct[j] * dsa
            g_sc[...] = g
            d8 = d8_sc[...]
            dsa8 = dsa_sc[...]
            dv_ref[sl, :] = jnp.sum(d8 * kt, axis=1)
            dr_ref[sl, :] = _segrows8(t8 * dy8[:, None, :], dmask, ones)
            dk_ref[sl, :] = _segrows8(d8 * v8[:, None, :], dmask, ones)
            dw_ref[sl, :] = _segrows8(tp8 * d8, dmask, ones)
            db_ref[sl, :] = _segrows8(d8 * sa8[:, None, :], dmask, ones)
            dc_ref[sl, :] = _segrows8(tp8 * dsa8[:, None, :], dmask, ones)
            return carry

        lax.fori_loop(0, SCAN_TB // 8, group, 0)

        if nx:
            @pl.when(i == nblk - 1)
            def _():
                end()

    row = pl.BlockSpec((SCAN_TB, DG), lambda i: (nblk - 1 - i, 0))
    st_spec = pl.BlockSpec((SCAN_TB, HD, DG), lambda i: (nblk - 1 - i, 0, 0))
    sp_spec = pl.BlockSpec((1, HD, DG), lambda i: (jnp.maximum((nblk - 1 - i) * SCAN_TB - 1, 0), 0, 0))
    ins = [fl, w, k2, v2, c, b, dy, states, states, p["seg64x3_bf16"], p["dmask"]] + list(exchange)
    specs = [row] * 7 + [st_spec, sp_spec, _full((3 * DG, DG)), _full((HD, DG))] + [ANY] * nx
    tile8 = pltpu.VMEM((8, HD, DG), F32)
    sems = [pltpu.SemaphoreType.DMA((nx, 3)), pltpu.SemaphoreType.DMA((nx, 3))] if nx else []
    outs = pl.pallas_call(body, grid=(nblk,), in_specs=specs, out_specs=[row] * 6 + [ANY] * nx,
                          out_shape=[_sds((SEQ, DG))] * 6 + [_sds(a.shape, a.dtype) for a in exchange],
                          scratch_shapes=[pltpu.VMEM((HD, DG), F32), tile8, tile8, pltpu.VMEM((8, DG), F32)] + sems,
                          name=f"rwkv_scan_bwd{l}", compiler_params=_cp(("arbitrary",)))(*ins)
    return outs[:6], list(outs[6:])


HG_ROWS = 128


HG_NC = HG_ROWS // HGRN_CHUNK


def _hgrn_block_fn(layer):
    def fn(hq, hf, hi, hg, sprev, lb0, lb1, norm_w, seg, bd, tri_bd, ones_bd, first_row, causal):
        e0 = jnp.exp(lb0 - jnp.maximum(lb0, lb1))
        e1 = jnp.exp(lb1 - jnp.maximum(lb0, lb1))
        sm0, sm1 = e0 / (e0 + e1), e1 / (e0 + e1)
        lb = (sm0 - sm0) if layer == 0 else ((sm0 + sm1) - sm0)
        forget = lb + (1.0 - lb) * _sigmoid(hf)
        logf = jnp.log(forget)
        kk = 1.0 - forget
        q = _silu(hq)
        c, nc = HGRN_CHUNK, HG_NC
        b = _dotx(tri_bd, logf)
        bl = _dotx(ones_bd, logf)
        split = lambda t: t.reshape(nc, c, DG)
        b4 = split(b)
        diff = (b4[:, :, None, :] - b4[:, None, :, :]).reshape(nc * c * c, DG)
        dec = jnp.exp(jnp.where(causal > 0.5, diff, -1e30))
        qrep = jnp.broadcast_to(split(q)[:, :, None, :], (nc, c, c, DG)).reshape(nc * c * c, DG)
        ktil = jnp.broadcast_to(split(kk)[:, None, :, :], (nc, c, c, DG)).reshape(nc * c * c, DG)
        vtil = jnp.broadcast_to(split(hi)[:, None, :, :], (nc, c, c, DG)).reshape(nc * c * c, DG)
        att = _dot(qrep * ktil * dec, seg)
        o_intra = jnp.sum((att * vtil).reshape(nc * c, c, DG), axis=1)
        kd4 = split(kk * jnp.exp(bl - b))
        qe4 = split(q * jnp.exp(b))
        v4 = split(hi)
        tot = jnp.exp(_dotx(first_row, bl))
        s, o_inter = sprev, []
        for ci in range(nc):
            o_inter.append(_dot_nt(qe4[ci], s))
            s = s * tot[ci:ci + 1, :] + _dot_tn(v4[ci], kd4[ci]) * bd
        o = o_intra + jnp.concatenate(o_inter, axis=0)
        ms = _dot(o * o, seg) * (1.0 / HD)
        y = o * lax.rsqrt(ms + RMS_EPS) * norm_w * _silu(hg)
        return y, s

    return fn


def _hgrn_consts(p):
    return [p["seg64"], p["seg64"], p["tri_bd128"], p["ones_bd128"], p["first_row"], p["causal_blk"]]


def hgrn_fwd(l, proj, p):
    fn = _hgrn_block_fn(l)

    def body(hq_ref, hf_ref, hi_ref, hg_ref, *rest):
        const_refs, (y_ref, st_ref, s_sc) = rest[:-3], rest[-3:]

        @pl.when(pl.program_id(0) == 0)
        def _():
            s_sc[...] = jnp.zeros_like(s_sc)

        sprev = s_sc[...]
        st_ref[0] = sprev
        y, snext = fn(hq_ref[...], hf_ref[...], hi_ref[...], hg_ref[...], sprev, *[r[...] for r in const_refs])
        y_ref[...] = y
        s_sc[...] = snext

    rows = lambda cb: pl.BlockSpec((HG_ROWS, DG), lambda i: (i, cb))
    ins = [proj, proj, proj, proj, p["lb0"], p["lb1"], p["hgrn_norm_w"]] + _hgrn_consts(p)
    specs = [rows(C_HQ // DG), rows(C_HF // DG), rows(C_HI // DG), rows(C_HG // DG)] + [_full(x.shape) for x in ins[4:]]
    return pl.pallas_call(body, grid=(SEQ // HG_ROWS,), in_specs=specs,
                          out_specs=[rows(0), pl.BlockSpec((1, DG, DG), lambda i: (i, 0, 0))],
                          out_shape=[_sds((SEQ, DG)), _sds((SEQ // HG_ROWS, DG, DG))],
                          scratch_shapes=[pltpu.VMEM((DG, DG), F32)], name=f"hgrn_fwd{l}",
                          compiler_params=_cp(("arbitrary",)))(*ins)


def hgrn_bwd(l, proj, states, dy, p, sibling=(), dy_col=0):
    fn = _hgrn_block_fn(l)
    nblk = SEQ // HG_ROWS
    n_const = len(_hgrn_consts(p))
    ns = len(sibling)

    def body(hq_ref, hf_ref, hi_ref, hg_ref, st_ref, dy_ref, lb0_ref, lb1_ref, nw_ref, *rest):
        const_refs, rest = rest[:n_const], rest[n_const:]
        dp_ref, dlb0_ref, dlb1_ref, dnw_ref = rest[ns:ns + 4]
        ds_sc = rest[2 * ns + 4]
        if ns:
            begin, end = _sibling_steps(rest[:ns], rest[ns + 4:2 * ns + 4], *rest[2 * ns + 5:])

            @pl.when(pl.program_id(0) == 0)
            def _():
                begin()

        @pl.when(pl.program_id(0) == 0)
        def _():
            ds_sc[...] = jnp.zeros_like(ds_sc)
            dlb0_ref[...] = jnp.zeros_like(dlb0_ref)
            dlb1_ref[...] = jnp.zeros_like(dlb1_ref)
            dnw_ref[...] = jnp.zeros_like(dnw_ref)

        consts = [r[...] for r in const_refs]
        f = lambda hq, hf, hi, hg, sp, b0, b1, nw: fn(hq, hf, hi, hg, sp, b0, b1, nw, *consts)
        _, vjp = jax.vjp(f, hq_ref[...], hf_ref[...], hi_ref[...], hg_ref[...], st_ref[0], lb0_ref[...], lb1_ref[...],
                         nw_ref[...])
        dhq, dhf, dhi, dhg, dsp, dlb0, dlb1, dnw = vjp((dy_ref[...], ds_sc[...]))
        dp_ref[:, 0:DG] = dhq
        dp_ref[:, DG:2 * DG] = dhf
        dp_ref[:, 2 * DG:3 * DG] = dhi
        dp_ref[:, 3 * DG:4 * DG] = dhg
        ds_sc[...] = dsp
        dlb0_ref[...] += dlb0
        dlb1_ref[...] += dlb1
        dnw_ref[...] += dnw

        if ns:
            @pl.when(pl.program_id(0) == nblk - 1)
            def _():
                end()

    rows = lambda cb: pl.BlockSpec((HG_ROWS, DG), lambda i: (nblk - 1 - i, cb))
    ins = [proj, proj, proj, proj, states, dy, p["lb0"], p["lb1"], p["hgrn_norm_w"]] + _hgrn_consts(p)
    specs = [rows(C_HQ // DG), rows(C_HF // DG), rows(C_HI // DG), rows(C_HG // DG),
             pl.BlockSpec((1, DG, DG), lambda i: (nblk - 1 - i, 0, 0)), rows(dy_col)] + [_full(x.shape) for x in ins[6:]]
    sem = pltpu.SemaphoreType.DMA((max(ns, 1), 4))
    outs = pl.pallas_call(body, grid=(nblk,), in_specs=specs + [ANY] * ns,
                          out_specs=[pl.BlockSpec((HG_ROWS, 4 * DG), lambda i: (nblk - 1 - i, 0)), _full((1, DG)),
                                     _full((1, DG)), _full((1, DG))] + [ANY] * ns,
                          out_shape=[_sds((SEQ, 4 * DG)), _sds((1, DG)), _sds((1, DG)), _sds((1, DG))]
                          + [_sds((4,) + a.shape[1:], a.dtype) for a in sibling],
                          scratch_shapes=[pltpu.VMEM((DG, DG), F32)] + ([sem, sem] if ns else []), name=f"hgrn_bwd{l}",
                          compiler_params=_cp(("arbitrary",)))(*ins, *sibling)
    return outs[:4], list(outs[4:])


def _ssd_chunk_fn(z, xs, bm, cm, dtr, sprev, dt_bias, a_log, d_par, norm_w, e128, tri, trit, seg128, ones128):
    lc = SSD_CHUNK
    dt = _softplus(dtr + dt_bias)
    a = -jnp.exp(a_log)
    da = dt * a * (lax.broadcasted_iota(jnp.int32, (1, 128), 1) < NH).astype(F32)
    cs = _dotx(tri, da)
    cst = _dotx_tn(da, trit)
    cs_b = _dotx(cs, e128)
    dt_b = _dotx(dt, e128)
    csl_b = _dotx(jnp.sum(da, axis=0, keepdims=True), e128)
    xdt = xs * dt_b
    lane = lax.broadcasted_iota(jnp.int32, (1, DG), 1)
    rowi = lax.broadcasted_iota(jnp.int32, (lc, lc), 0)
    coli = lax.broadcasted_iota(jnp.int32, (lc, lc), 1)
    y = jnp.zeros((lc, DG), F32)
    snew = jnp.zeros((DG, SSD_N), F32)
    d_b = jnp.zeros((1, DG), F32)
    wdec = xdt * jnp.exp(csl_b - cs_b)
    for g in range(2):
        bg = bm[:, g * SSD_N:(g + 1) * SSD_N]
        cg = cm[:, g * SSD_N:(g + 1) * SSD_N]
        gmat = _dot_nt(cg, bg)
        gmask = ((lane // 128) == g).astype(F32)
        snew = snew + _dot_tn(wdec * gmask, bg)
        y = y + _dot_nt(cg, sprev) * gmask * jnp.exp(cs_b)
        for hh in range(2):
            h = 2 * g + hh
            seg = jnp.where(rowi >= coli, cs[:, h:h + 1] - cst[h:h + 1, :], -1e30)
            hmask = ((lane // HD) == h).astype(F32)
            y = y + _dot(gmat * jnp.exp(seg), xdt * hmask)
            d_b = d_b + d_par[:, h:h + 1] * hmask
    cd = jnp.exp(_dotx_tn(_dotx(da, e128), ones128))
    snext = sprev * cd + snew
    y = y + xs * d_b
    y = y * _silu(z)
    ms = _dot(y * y, seg128) * (1.0 / 128.0)
    return y * lax.rsqrt(ms + RMS_EPS) * norm_w, snext


def ssd_fwd(l, proj, xc, p):
    nc = SEQ // SSD_CHUNK

    def body(z_ref, xs_ref, b_ref, c_ref, dt_ref, dtb_ref, al_ref, d_ref, nw_ref, e_ref, tri_ref, trit_ref, sg_ref,
             on_ref, y_ref, st_ref, s_sc):
        @pl.when(pl.program_id(0) == 0)
        def _():
            s_sc[...] = jnp.zeros_like(s_sc)

        sprev = s_sc[...]
        st_ref[0] = sprev
        y, snext = _ssd_chunk_fn(z_ref[...], xs_ref[...], b_ref[...], c_ref[...], dt_ref[...], sprev, dtb_ref[...],
                                 al_ref[...], d_ref[...], nw_ref[...], e_ref[...], tri_ref[...], trit_ref[...],
                                 sg_ref[...], on_ref[...])
        y_ref[...] = y
        s_sc[...] = snext

    rw = lambda w, cb: pl.BlockSpec((SSD_CHUNK, w), lambda i: (i, cb))
    ins = [proj, xc, xc, xc, proj, p["dt_bias"], p["a_log"], p["ssd_d"], p["ssd_norm_w"], p["e128"], p["tri128"],
           p["tri128t"], p["seg128"], p["ones128"]]
    specs = [rw(DG, C_Z // DG), rw(DG, 0), rw(DG, 1), rw(DG, 2), rw(128, C_DT // 128)] + [_full(x.shape) for x in ins[5:]]
    return pl.pallas_call(body, grid=(nc,), in_specs=specs,
                          out_specs=[rw(DG, 0), pl.BlockSpec((1, DG, SSD_N), lambda i: (i, 0, 0))],
                          out_shape=[_sds((SEQ, DG)), _sds((nc, DG, SSD_N))],
                          scratch_shapes=[pltpu.VMEM((DG, SSD_N), F32)], name=f"ssd_fwd{l}",
                          compiler_params=_cp(("arbitrary",)))(*ins)


def ssd_bwd(l, proj, xc, states, dy, p, dy_col=0):
    nc = SEQ // SSD_CHUNK

    def body(z_ref, xs_ref, b_ref, c_ref, dt_ref, st_ref, dy_ref, dtb_ref, al_ref, d_ref, nw_ref, e_ref, tri_ref,
             trit_ref, sg_ref, on_ref, dz_ref, dxc_ref, ddt_ref, ddtb_ref, dal_ref, dd_ref, dnw_ref, ds_sc):
        @pl.when(pl.program_id(0) == 0)
        def _():
            ds_sc[...] = jnp.zeros_like(ds_sc)
            ddtb_ref[...] = jnp.zeros_like(ddtb_ref)
            dal_ref[...] = jnp.zeros_like(dal_ref)
            dd_ref[...] = jnp.zeros_like(dd_ref)
            dnw_ref[...] = jnp.zeros_like(dnw_ref)

        consts = (e_ref[...], tri_ref[...], trit_ref[...], sg_ref[...], on_ref[...])
        f = lambda *a: _ssd_chunk_fn(*a, *consts)
        _, vjp = jax.vjp(f, z_ref[...], xs_ref[...], b_ref[...], c_ref[...], dt_ref[...], st_ref[0], dtb_ref[...],
                         al_ref[...], d_ref[...], nw_ref[...])
        dz, dxs, db, dc, ddt, dsp, ddtb, dal, dd, dnw = vjp((dy_ref[...], ds_sc[...]))
        dz_ref[...] = dz
        dxc_ref[:, 0:DG] = dxs
        dxc_ref[:, DG:2 * DG] = db
        dxc_ref[:, 2 * DG:3 * DG] = dc
        ddt_ref[...] = ddt
        ds_sc[...] = dsp
        ddtb_ref[...] += ddtb
        dal_ref[...] += dal
        dd_ref[...] += dd
        dnw_ref[...] += dnw

    rw = lambda w, cb: pl.BlockSpec((SSD_CHUNK, w), lambda i: (nc - 1 - i, cb))
    ins = [proj, xc, xc, xc, proj, states, dy, p["dt_bias"], p["a_log"], p["ssd_d"], p["ssd_norm_w"], p["e128"],
           p["tri128"], p["tri128t"], p["seg128"], p["ones128"]]
    specs = [rw(DG, C_Z // DG), rw(DG, 0), rw(DG, 1), rw(DG, 2), rw(128, C_DT // 128),
             pl.BlockSpec((1, DG, SSD_N), lambda i: (nc - 1 - i, 0, 0)), rw(DG, dy_col)] + [_full(x.shape) for x in ins[7:]]
    return pl.pallas_call(body, grid=(nc,), in_specs=specs,
                          out_specs=[rw(DG, 0), rw(3 * DG, 0), rw(128, 0), _full((1, 128)), _full((1, 128)), _full((1, 128)),
                                     _full((1, DG))],
                          out_shape=[_sds((SEQ, DG)), _sds((SEQ, 3 * DG)), _sds((SEQ, 128)), _sds((1, 128)), _sds((1, 128)),
                                     _sds((1, 128)), _sds((1, DG))],
                          scratch_shapes=[pltpu.VMEM((DG, SSD_N), F32)], name=f"ssd_bwd{l}",
                          compiler_params=_cp(("arbitrary",)))(*ins)


ATT_BLK = 128


def _att_geometry(dil):
    i = lax.broadcasted_iota(jnp.int32, (ATT_BLK, ATT_BLK), 0)
    j = lax.broadcasted_iota(jnp.int32, (ATT_BLK, ATT_BLK), 1)
    return ((i - j) * dil).astype(F32), ((ATT_BLK + i - j) * dil).astype(F32), j <= i, j >= i


def _att_scores(qn, kc, kp, h, geom, has_prev):
    dist_c, dist_p, m_c, m_pj = geom
    slope = 2.0 ** (-8.0 * (h + 1) / NH)
    scale = HD ** -0.5
    s_c = _dot_nt(qn, kc) * scale - slope * dist_c
    s_p = _dot_nt(qn, kp) * scale - slope * dist_p
    m_p = jnp.logical_and(m_pj, has_prev)
    return jnp.where(m_c, s_c, -1e30), jnp.where(m_p, s_p, -1e30), m_c, m_p


def _sub_spec(ln, width, col):
    return pl.BlockSpec((ln, DG), lambda z: (0, z * (width // DG) + col // DG))


QKV_W = 3 * DG


def attn_branch_fwd(l, bi, qkv, dil):
    ln = SEQ // dil
    nb = ln // ATT_BLK

    def body(q_ref, k_ref, v_ref, o_ref, l_ref):
        geom = _att_geometry(dil)

        def blk(n, carry):
            r0 = pl.multiple_of(n * ATT_BLK, ATT_BLK)
            rp = pl.multiple_of(jnp.maximum(n - 1, 0) * ATT_BLK, ATT_BLK)
            cur, prv = pl.ds(r0, ATT_BLK), pl.ds(rp, ATT_BLK)
            for h in range(NH):
                hs = slice(h * HD, (h + 1) * HD)
                qn, kc, vc, kp, vp = q_ref[cur, hs], k_ref[cur, hs], v_ref[cur, hs], k_ref[prv, hs], v_ref[prv, hs]
                s_c, s_p, m_c, m_p = _att_scores(qn, kc, kp, h, geom, n > 0)
                m = jnp.maximum(jnp.max(s_c, axis=1, keepdims=True), jnp.max(s_p, axis=1, keepdims=True))
                p_c = jnp.where(m_c, jnp.exp(s_c - m), 0.0)
                p_p = jnp.where(m_p, jnp.exp(s_p - m), 0.0)
                den = jnp.sum(p_c, axis=1, keepdims=True) + jnp.sum(p_p, axis=1, keepdims=True)
                o_ref[cur, hs] = (_dot(p_c, vc) + _dot(p_p, vp)) / den
                l_ref[cur, hs] = jnp.broadcast_to(m + jnp.log(den), (ATT_BLK, HD))
            return carry

        lax.fori_loop(0, nb, blk, 0)

    pv = qkv.reshape(ln, dil * QKV_W)
    out = pl.BlockSpec((ln, DG), lambda z: (0, z))
    o, lse = pl.pallas_call(body, grid=(dil,), in_specs=[_sub_spec(ln, QKV_W, 0), _sub_spec(ln, QKV_W, DG), _sub_spec(ln, QKV_W, 2 * DG)],
                            out_specs=[out, out], out_shape=[_sds((ln, dil * DG))] * 2, name=f"attn_fwd{l}_{bi}",
                            compiler_params=_cp(("parallel",)))(pv, pv, pv)
    return o.reshape(SEQ, DG), lse.reshape(SEQ, DG)


def attn_branch_bwd(l, bi, qkv, dil, dyb, lse_all, delta):
    ln = SEQ // dil
    nb = ln // ATT_BLK
    scale = HD ** -0.5

    def body(q_ref, k_ref, v_ref, do_ref, l_ref, dl_ref, dq_ref, dk_ref, dv_ref):
        dk_ref[...] = jnp.zeros_like(dk_ref)
        dv_ref[...] = jnp.zeros_like(dv_ref)
        geom = _att_geometry(dil)

        def blk(n, carry):
            r0 = pl.multiple_of(n * ATT_BLK, ATT_BLK)
            rp = pl.multiple_of(jnp.maximum(n - 1, 0) * ATT_BLK, ATT_BLK)
            cur, prv = pl.ds(r0, ATT_BLK), pl.ds(rp, ATT_BLK)
            for h in range(NH):
                hs = slice(h * HD, (h + 1) * HD)
                qn, don = q_ref[cur, hs], do_ref[cur, hs]
                lse, dlt = l_ref[cur, h * HD:h * HD + 1], dl_ref[cur, h * HD:h * HD + 1]
                kc, vc, kp, vp = k_ref[cur, hs], v_ref[cur, hs], k_ref[prv, hs], v_ref[prv, hs]
                s_c, s_p, m_c, m_p = _att_scores(qn, kc, kp, h, geom, n > 0)
                p_c = jnp.where(m_c, jnp.exp(s_c - lse), 0.0)
                p_p = jnp.where(m_p, jnp.exp(s_p - lse), 0.0)
                ds_c = p_c * (_dot_nt(don, vc) - dlt)
                ds_p = p_p * (_dot_nt(don, vp) - dlt)
                dq_ref[cur, hs] = (_dot(ds_c, kc) + _dot(ds_p, kp)) * scale
                dv_ref[prv, hs] += _dot_tn(p_p, don)
                dk_ref[prv, hs] += _dot_tn(ds_p, qn) * scale
                dv_ref[cur, hs] += _dot_tn(p_c, don)
                dk_ref[cur, hs] += _dot_tn(ds_c, qn) * scale
            return carry

        lax.fori_loop(0, nb, blk, 0)

    pv = qkv.reshape(ln, dil * QKV_W)
    sub = lambda t: t.reshape(ln, dil * DG)
    row = pl.BlockSpec((ln, DG), lambda z: (0, z))
    outs = pl.pallas_call(body, grid=(dil,),
                          in_specs=[_sub_spec(ln, QKV_W, 0), _sub_spec(ln, QKV_W, DG), _sub_spec(ln, QKV_W, 2 * DG), row, row, row],
                          out_specs=[row] * 3, out_shape=[_sds((ln, dil * DG))] * 3, name=f"attn_bwd{l}_{bi}",
                          compiler_params=_cp(("parallel",)))(pv, pv, pv, sub(dyb), sub(lse_all), sub(delta))
    return [t.reshape(SEQ, DG) for t in outs]


def _attn_merge_fn(o1, o2, o3, l1, l2, l3):
    m = jnp.maximum(jnp.maximum(l1, l2), l3)
    w1, w2, w3 = jnp.exp(l1 - m), jnp.exp(l2 - m), jnp.exp(l3 - m)
    den = w1 + w2 + w3
    return (w1 * o1 + w2 * o2 + w3 * o3) / den, m + jnp.log(den)


def attn_merge(l, os_, ls_):
    ins = list(os_) + list(ls_)
    return _map_fwd(f"attn_merge{l}", _attn_merge_fn, (SEQ // RB,), ins, [_rows(DG)] * 6, [_sds((SEQ, DG))] * 2,
                    [_rows(DG)] * 2)


def attn_delta(l, dyb, yb, seg):
    fn = lambda d, y, s: (_dot(d * y, s),)
    return _map_fwd(f"attn_delta{l}", fn, (SEQ // RB,), [dyb, yb, seg], [_rows(DG), _rows(DG), _full((DG, DG))],
                    [_sds((SEQ, DG))], [_rows(DG)])[0]


def _ln_fn(x, mix, w, b):
    h = ALPHA * x + mix
    mu = jnp.mean(h, axis=-1, keepdims=True)
    d = h - mu
    var = jnp.mean(d * d, axis=-1, keepdims=True)
    return (d * lax.rsqrt(var + LN_EPS) * w + b,)


def ln_fwd(name, x, mix, w, b):
    specs = [_rows(D_MODEL), _rows(D_MODEL), _full((1, D_MODEL)), _full((1, D_MODEL))]
    return _map_fwd(name, _ln_fn, (SEQ // RB,), [x, mix, w, b], specs, [_sds((SEQ, D_MODEL))], [_rows(D_MODEL)])[0]


def ln_bwd(name, x, mix, w, b, dy):
    specs = [_rows(D_MODEL), _rows(D_MODEL), _full((1, D_MODEL)), _full((1, D_MODEL))]
    return _map_bwd(name, _ln_fn, (SEQ // RB,), [x, mix, w, b], specs, [[dy]], [[_rows(D_MODEL)]], want=[1, 2, 3],
                    acc=(2, 3))


def loss_call(y, tgt):
    def fn(yy, tt):
        e = yy - tt
        part = 0.5 * jnp.sum(jnp.sum(e * e, axis=-1, keepdims=True) * (1.0 / D_MODEL), axis=0, keepdims=True)
        return e * (1.0 / D_MODEL), jnp.broadcast_to(part, (8, 128))

    return _map_fwd("loss", fn, (SEQ // RB,), [y, tgt], [_rows(D_MODEL)] * 2,
                    [_sds((SEQ, D_MODEL)), _sds((SEQ // RB * 8, 128))],
                    [_rows(D_MODEL), pl.BlockSpec((8, 128), lambda i: (i, 0))])


LATE_KEYS = ("w_out", "w_up_t", "w_down")


def _full_rows(g):
    return g.reshape(N_DEV * g.shape[1], g.shape[2])


def layer_fwd(l, x, vfirst, wts, p, gather=(), late=False):
    sv = {"x": x}
    proj = _mm(f"mm_in{l}", x, wts["w_in"], "nn", *MM_TILES["k1024"])
    fl = lerp_fwd(l, proj, p["mu"])
    xc = conv_fwd(l, proj, p["conv_w"], p["conv_b"])
    w, k2, v2, c, b, g = rwkv_pre_fwd(l, fl, vfirst, p)
    y_scan, states, sv["gathered"] = rwkv_scan_fwd(l, fl, w, k2, v2, c, b, p, gather)
    if late:
        wts = dict(wts, **dict(zip(LATE_KEYS, [_full_rows(g) for g in sv["gathered"][:3]])))
    sv["wts"] = wts
    ya = rwkv_post_fwd(l, y_scan, fl, k2, v2, g, p)
    qkv = proj[:, C_AQ:C_AQ + 3 * DG]
    outs, lses = [], []
    for bi, (win, dil) in enumerate(DILATED):
        o, lse = attn_branch_fwd(l, bi, qkv, dil)
        outs.append(o)
        lses.append(lse)
    yb, lse_all = attn_merge(l, outs, lses)
    yc, ssd_states = ssd_fwd(l, proj, xc, p)
    yd, hg_states = hgrn_fwd(l, proj, p)
    ycat = jnp.concatenate([ya, yb, yc, yd], axis=1).astype(BF16)
    mix = _mm(f"mm_out{l}", ycat, wts["w_out"], "nn", *MM_TILES["k1024"])
    x1 = ln_fwd(f"ln1_fwd{l}", x, mix, p["ln1_w"], p["ln1_b"])
    hh = _mm(f"mm_up{l}", x1, wts["w_up_t"], "nt", *MM_TILES["k1024"], epilogue="relu2")
    m2 = _mm(f"mm_down{l}", hh, wts["w_down"], "nn", *MM_TILES["k4096"])
    x2 = ln_fwd(f"ln2_fwd{l}", x1, m2, p["ln2_w"], p["ln2_b"])
    sv.update(proj=proj, fl=fl, xc=xc, w=w, k2=k2, v2=v2, c=c, b=b, g=g, y_scan=y_scan, states=states,
              yb=yb, lse_all=lse_all, ssd_states=ssd_states, hg_states=hg_states, ycat=ycat, mix=mix, x1=x1, hh=hh, qkv=qkv,
              m2=m2, vfirst=vfirst)
    return x2, sv


def layer_bwd(l, dx2, dvfirst_next, sv, wts, p, exchange=(), reducer=None):
    gr = {}
    x, x1, proj, fl = sv["x"], sv["x1"], sv["proj"], sv["fl"]
    dres2, gr["ln2_w"], gr["ln2_b"] = ln_bwd(f"ln2_bwd{l}", x1, sv["m2"], p["ln2_w"], p["ln2_b"], dx2)
    du = _mm(f"mm_down_dx{l}", dres2, wts["w_down"], "nt", *MM_TILES["k1024"], add=sv["hh"], epilogue="relu2_bwd",
             out_dtype=BF16)
    gr["w_down"] = _mm(f"mm_down_dw{l}", sv["hh"], dres2, "tn", *MM_TILES["wgrad_tall"])
    dx1 = _mm(f"mm_up_dx{l}", du, wts["w_up_t"], "nn", *MM_TILES["k4096"], add=dres2, add_scale=ALPHA)
    gr["w_up_t"] = _mm(f"mm_up_dw{l}", du, x1, "tn", *MM_TILES["wgrad_tall"])
    dres1, gr["ln1_w"], gr["ln1_b"] = ln_bwd(f"ln1_bwd{l}", x, sv["mix"], p["ln1_w"], p["ln1_b"], dx1)
    dycat = _mm(f"mm_out_dx{l}", dres1, wts["w_out"], "nt", *MM_TILES["k1024"])
    gr["w_out"] = _mm(f"mm_out_dw{l}", sv["ycat"], dres1, "tn", 1024, 1024, 512)
    dyb = dycat[:, DG:2 * DG]
    send = [_owner_blocks(gr[k]) for k in LATE_KEYS] if reducer else []
    (dhg4, gr["lb0"], gr["lb1"], gr["hgrn_norm_w"]), sib = hgrn_bwd(l, proj, sv["hg_states"], dycat, p, send, dy_col=3)
    if reducer:
        gr["early_own"], early_parts = reducer(f"{l}a", send, sib)
        exchange = list(exchange) + list(early_parts)
    dz, dxc, ddt, gr["dt_bias"], gr["a_log"], gr["ssd_d"], gr["ssd_norm_w"] = ssd_bwd(l, proj, sv["xc"], sv["ssd_states"], dycat, p, dy_col=2)
    dxbc, gr["conv_w"], gr["conv_b"] = conv_bwd(l, proj, p["conv_w"], p["conv_b"], dxc)
    delta = attn_delta(l, dyb, sv["yb"], p["seg64"])
    dqs, dks, dvs = [], [], []
    for bi, (win, dil) in enumerate(DILATED):
        dq, dk, dv = attn_branch_bwd(l, bi, sv["qkv"], dil, dyb, sv["lse_all"], delta)
        dqs.append(dq)
        dks.append(dk)
        dvs.append(dv)
    dq_a, dk_a, dv_a = _addn(f"attn_dq{l}", *dqs), _addn(f"attn_dk{l}", *dks), _addn(f"attn_dv{l}", *dvs)
    pg = rwkv_post_bwd(l, sv["y_scan"], fl, sv["k2"], sv["v2"], sv["g"], p, dycat)
    gr["lnx_w"], gr["lnx_b"], gr["r_k"] = pg["lnx_w"], pg["lnx_b"], pg["r_k"]
    (dr, dw, dk, dv, dc, db), gr["exchanged"] = rwkv_scan_bwd(l, fl, sv["w"], sv["k2"], sv["v2"], sv["c"], sv["b"],
                                                              sv["states"], pg["y"], p, exchange)
    v2_cts = [dv, pg["v2"]] + ([dvfirst_next] if dvfirst_next is not None else [])
    qg = rwkv_pre_bwd(l, fl, sv["vfirst"], p, [[dw], [dk, pg["k2"]], v2_cts, [dc], [db], [pg["g"]]])
    for nme in ("w0", "w2p", "a0", "a2p", "g2p", "k_k", "k_a", "v0", "v2p"):
        if nme in qg:
            gr[nme] = qg[nme]
    dfr = _addn(f"rwkv_dr{l}", dr, pg["fr"])
    dvres = qg["fvres"] if l > 0 else jnp.zeros((SEQ, 128), F32)
    dfl_out = jnp.concatenate([dfr, qg["fk"], qg["fv"], qg["flora"], dvres], axis=1)
    dfl_in, gr["mu"] = lerp_bwd(l, proj, p["mu"], dfl_out)
    dproj = jnp.concatenate([dfl_in[:, 0:768], dq_a, dk_a, dv_a, dz, dxbc, dhg4, dfl_in[:, 768:896], ddt,
                             dfl_in[:, 896:1024], jnp.zeros((SEQ, 128), F32)], axis=1).astype(BF16)
    dx = _mm(f"mm_in_dx{l}", dproj, wts["w_in"], "nt", *MM_TILES["k4096"], add=dres1, add_scale=ALPHA)
    gr["w_in"] = _mm(f"mm_in_dw{l}", x, dproj, "tn", *MM_TILES["wgrad_wide"])
    return dx, (qg["vfirst"] if l > 0 else None), gr


def _w_in_pad(w_in_l, w_vres):
    rows = w_in_l.shape[0]
    z = lambda n: jnp.zeros((rows, n), w_in_l.dtype)
    vres = z(128) if w_vres is None else jnp.concatenate([w_vres, z(96)], axis=1)
    return jnp.concatenate([w_in_l[:, 0:768], w_in_l[:, 896:1664], w_in_l[:, 1664:1920], w_in_l[:, 1920:2688],
                            w_in_l[:, 2692:3716], w_in_l[:, 768:896], w_in_l[:, 2688:2692], z(124), vres, z(128)], axis=1)


def _w_in_unpad(g):
    g_in = jnp.concatenate([g[:, 0:768], g[:, C_LORA:C_LORA + 128], g[:, 768:1536], g[:, C_Z:C_Z + 256],
                            g[:, C_XBC:C_XBC + 768], g[:, C_DT:C_DT + 4], g[:, C_HQ:C_HQ + 1024]], axis=1)
    return g_in, g[:, C_VRES:C_VRES + 32]


def _consts():
    pair = jnp.arange(HG_NC * HGRN_CHUNK * HGRN_CHUNK)
    i128 = jnp.arange(128)
    same_chunk = (i128[:, None] // HGRN_CHUNK) == (i128[None, :] // HGRN_CHUNK)
    seg64 = _seg_ones(DG, HD)
    tri128 = (i128[:, None] >= i128[None, :]).astype(F32)
    return dict(
        seg64=seg64, seg64x3_bf16=jnp.concatenate([seg64, seg64, seg64], axis=0).astype(BF16),
        dmask=(jnp.arange(HD)[:, None] == (jnp.arange(DG)[None, :] % HD)).astype(F32),
        tri_bd128=(same_chunk & (i128[:, None] >= i128[None, :])).astype(F32), ones_bd128=same_chunk.astype(F32),
        first_row=(i128[None, :] == (jnp.arange(HG_NC) * HGRN_CHUNK)[:, None]).astype(F32),
        causal_blk=jnp.broadcast_to((((pair // HGRN_CHUNK) % HGRN_CHUNK) >= (pair % HGRN_CHUNK)).astype(F32)[:, None],
                                    (HG_NC * HGRN_CHUNK * HGRN_CHUNK, DG)),
        e128=((i128[:, None] == (jnp.arange(DG)[None, :] // HD)) & (i128[:, None] < NH)).astype(F32),
        tri128=tri128, tri128t=tri128.T, seg128=_seg_ones(DG, 128), ones128=jnp.ones((128, 128), F32))


def _pad_lanes(v, n):
    return jnp.concatenate([v, jnp.zeros((n - v.shape[0],), v.dtype)])[None, :]


def _layer_params(l, raw, consts):
    p = dict(consts)
    row = lambda name: raw[name][l][None, :]
    z = lambda r: jnp.zeros((r, DG), F32)
    mu_vres = raw["mu_vres"][l - 1] if l > 0 else jnp.zeros((32,), F32)
    p["mu"] = jnp.concatenate([raw["mu_shift"][l], mu_vres, jnp.zeros((96,), F32)])[None, :]
    p["conv_w"], p["conv_b"] = raw["ssd_conv_w"][l], row("ssd_conv_b")
    p["w0"], p["a0"], p["k_k"], p["k_a"] = row("rwkv_w0"), row("rwkv_a0"), row("rwkv_k_k"), row("rwkv_k_a")
    p["lnx_w"], p["lnx_b"] = row("rwkv_lnx_w"), row("rwkv_lnx_b")
    p["r_k"] = raw["rwkv_r_k"][l].reshape(1, DG)
    p["w2p"] = jnp.concatenate([raw["rwkv_w2"][l], z(96)], axis=0)
    p["a2p"] = jnp.concatenate([z(32), raw["rwkv_a2"][l], z(64)], axis=0)
    p["g2p"] = jnp.concatenate([z(64), raw["rwkv_g2"][l]], axis=0)
    if l > 0:
        p["v0"] = raw["rwkv_v0"][l - 1][None, :]
        p["v2p"] = jnp.concatenate([raw["rwkv_v2"][l - 1], z(96)], axis=0)
    p["lb0"], p["lb1"] = raw["lower_bounds"][0:1], raw["lower_bounds"][1:2]
    p["hgrn_norm_w"], p["ssd_norm_w"] = row("hgrn_norm_w"), row("ssd_norm_w")
    p["dt_bias"], p["a_log"], p["ssd_d"] = (_pad_lanes(raw[n][l], 128) for n in ("ssd_dt_bias", "ssd_A_log", "ssd_D"))
    for n in ("ln1_w", "ln1_b", "ln2_w", "ln2_b"):
        p[n] = row(n)
    return p


def _natural_grads(g0, g1):
    gs = (g0, g1)
    st = lambda key, f=lambda a: a[0]: jnp.stack([f(g[key]) for g in gs])
    out = {}
    out["lower_bounds"] = jnp.concatenate([g0["lb0"] + g1["lb0"], g0["lb1"] + g1["lb1"]], axis=0)
    out["mu_shift"] = st("mu", lambda a: a[0, :896])
    out["mu_vres"] = g1["mu"][:, 896:928]
    out["rwkv_w0"], out["rwkv_a0"], out["rwkv_k_k"], out["rwkv_k_a"] = st("w0"), st("a0"), st("k_k"), st("k_a")
    out["rwkv_w2"] = st("w2p", lambda a: a[0:32])
    out["rwkv_a2"] = st("a2p", lambda a: a[32:64])
    out["rwkv_g2"] = st("g2p", lambda a: a[64:128])
    out["rwkv_r_k"] = st("r_k", lambda a: a.reshape(NH, HD))
    out["rwkv_lnx_w"], out["rwkv_lnx_b"] = st("lnx_w"), st("lnx_b")
    out["rwkv_v0"] = g1["v0"]
    out["rwkv_v2"] = g1["v2p"][None, 0:32]
    out["ssd_conv_w"] = st("conv_w", lambda a: a)
    out["ssd_conv_b"] = st("conv_b")
    out["ssd_dt_bias"], out["ssd_A_log"], out["ssd_D"] = (st(k, lambda a: a[0, :NH]) for k in ("dt_bias", "a_log", "ssd_d"))
    out["ssd_norm_w"], out["hgrn_norm_w"] = st("ssd_norm_w"), st("hgrn_norm_w")
    for n in ("ln1_w", "ln1_b", "ln2_w", "ln2_b"):
        out[n] = st(n)
    return out


MESH_T = pl.DeviceIdType.MESH
ANY = pl.BlockSpec(memory_space=pl.ANY)


def _dev_index(px, py, pc):
    return 4 * px + 2 * py + pc


def all_gather(arrs):
    n = len(arrs)

    def body(*refs):
        begin, middle, end = _gather_steps(refs[:n], refs[n:2 * n], *refs[2 * n:])
        begin()
        middle()
        end()

    return pl.pallas_call(body, in_specs=[ANY] * n, out_specs=[ANY] * n, out_shape=_gather_shapes(arrs),
                          scratch_shapes=_gather_sems(n), name="all_gather")(*arrs)


def _gather_shapes(arrs):
    return [_sds((N_DEV,) + a.shape, a.dtype) for a in arrs]


def _gather_sems(n):
    return [pltpu.SemaphoreType.DMA((n, 7)), pltpu.SemaphoreType.DMA((n, 7)), pltpu.SemaphoreType.DMA((n,))]


def _gather_steps(ins, outs, send_sems, recv_sems, local_sems):
    n = len(ins)
    x, y, c = lax.axis_index("x"), lax.axis_index("y"), lax.axis_index("c")
    me, sibling = (x, y, c), (x, y, 1 - c)
    chips = [(1 - x, y), (x, 1 - y), (1 - x, 1 - y)]

    def copy(a, k, block, to, src=None):
        slot = outs[a].at[_dev_index(*block)]
        return pltpu.make_async_remote_copy(src_ref=slot if src is None else src, dst_ref=slot,
                                            send_sem=send_sems.at[a, k], recv_sem=recv_sems.at[a, k],
                                            device_id=to, device_id_type=MESH_T)

    def own_copies():
        mine = [pltpu.make_async_copy(ins[a], outs[a].at[_dev_index(*me)], local_sems.at[a]) for a in range(n)]
        first = []
        for a in range(n):
            first.append(copy(a, 0, me, sibling, src=ins[a]))
            first += [copy(a, 1 + j, me, (*chip, c), src=ins[a]) for j, chip in enumerate(chips)]
        return mine, first

    def begin():
        mine, first = own_copies()
        for cp in mine + first:
            cp.start()

    def passed_on():
        return [copy(a, 4 + j, (*chip, c), sibling) for j, chip in enumerate(chips) for a in range(n)]

    def middle():
        for j, chip in enumerate(chips):
            for a in range(n):
                copy(a, 1 + j, (*chip, c), me).wait_recv()
        for cp in passed_on():
            cp.start()

    def end():
        mine, first = own_copies()
        for a in range(n):
            copy(a, 0, sibling, me).wait_recv()
            for j, chip in enumerate(chips):
                copy(a, 4 + j, (*chip, 1 - c), me).wait_recv()
        for cp in first + passed_on():
            cp.wait_send()
        for cp in mine:
            cp.wait()

    return begin, middle, end


def _chips(x, y):
    return [(x, y), (1 - x, y), (x, 1 - y), (1 - x, 1 - y)]


def _sibling_steps(ins, sib, send_sems, recv_sems):
    x, y, c = lax.axis_index("x"), lax.axis_index("y"), lax.axis_index("c")

    def copies():
        return [pltpu.make_async_remote_copy(src_ref=ins[a].at[_dev_index(cx, cy, 1 - c)], dst_ref=sib[a].at[k],
                                             send_sem=send_sems.at[a, k], recv_sem=recv_sems.at[a, k],
                                             device_id=(x, y, 1 - c), device_id_type=MESH_T)
                for a in range(len(ins)) for k, (cx, cy) in enumerate(_chips(x, y))]

    def begin():
        for cp in copies():
            cp.start()

    def end():
        cps = copies()
        for cp in cps:
            cp.wait_recv()
        for cp in cps:
            cp.wait_send()

    return begin, end


def exchange_siblings(arrs, name):
    n = len(arrs)

    def body(*refs):
        begin, end = _sibling_steps(refs[:n], refs[n:2 * n], *refs[2 * n:])
        begin()
        end()

    sem = pltpu.SemaphoreType.DMA((n, 4))
    return pl.pallas_call(body, in_specs=[ANY] * n, out_specs=[ANY] * n,
                          out_shape=[_sds((4,) + a.shape[1:], a.dtype) for a in arrs],
                          scratch_shapes=[sem, sem], name=name)(*arrs)


def reduce_pair(name, send, slots, sib, wire_dtype):
    _, r, c = send.shape
    rb = min(r, 262144 // c)

    def body(slots_ref, m0, m1, m2, m3, s_ref, own_ref, part_ref):
        own_ref[...] = m0[...] + s_ref[0]
        for k, m_ref in enumerate((m1, m2, m3)):
            part_ref[k] = (m_ref[...] + s_ref[k + 1]).astype(wire_dtype)

    mine = [pl.BlockSpec((None, rb, c), lambda i, s, k=k: (s[k], i, 0)) for k in range(4)]
    grid_spec = pltpu.PrefetchScalarGridSpec(
        num_scalar_prefetch=1, grid=(r // rb,),
        in_specs=mine + [pl.BlockSpec((4, rb, c), lambda i, s: (0, i, 0))],
        out_specs=[pl.BlockSpec((rb, c), lambda i, s: (i, 0)), pl.BlockSpec((3, rb, c), lambda i, s: (0, i, 0))])
    return pl.pallas_call(body, grid_spec=grid_spec, out_shape=[_sds((r, c)), _sds((3, r, c), wire_dtype)], name=name,
                          compiler_params=_cp(("parallel",)))(slots, send, send, send, send, sib)


def _chip_exchange_steps(ins, recv, send_sems, recv_sems):
    x, y, c = lax.axis_index("x"), lax.axis_index("y"), lax.axis_index("c")

    def copies():
        return [pltpu.make_async_remote_copy(src_ref=ins[a].at[k], dst_ref=recv[a].at[k], send_sem=send_sems.at[a, k],
                                             recv_sem=recv_sems.at[a, k], device_id=(cx, cy, c), device_id_type=MESH_T)
                for a in range(len(ins)) for k, (cx, cy) in enumerate(_chips(x, y)[1:])]

    def begin():
        for cp in copies():
            cp.start()

    def end():
        cps = copies()
        for cp in cps:
            cp.wait_recv()
        for cp in cps:
            cp.wait_send()

    return begin, end


def exchange_chips(parts, rep):
    n = len(parts)

    def body(*refs):
        ins, rep_ref = refs[:n], refs[n]
        recv, rep_all = refs[n + 1:2 * n + 1], refs[2 * n + 1]
        send_sems, recv_sems, rsend_sems, rrecv_sems, local_sem = refs[2 * n + 2:]
        x, y, c = lax.axis_index("x"), lax.axis_index("y"), lax.axis_index("c")
        me = _dev_index(x, y, c)
        mine = pltpu.make_async_copy(rep_ref, rep_all.at[me], local_sem)
        mine.start()
        begin, end = _chip_exchange_steps(ins, recv, send_sems, recv_sems)
        begin()
        rels = [(rx, ry, rc) for rx in (0, 1) for ry in (0, 1) for rc in (0, 1)][1:]
        peers = [(jnp.where(rx, 1 - x, x), jnp.where(ry, 1 - y, y), jnp.where(rc, 1 - c, c)) for rx, ry, rc in rels]
        rcps = []
        for k, peer in enumerate(peers):
            cp = pltpu.make_async_remote_copy(src_ref=rep_ref, dst_ref=rep_all.at[me], send_sem=rsend_sems.at[k],
                                              recv_sem=rrecv_sems.at[k], device_id=peer, device_id_type=MESH_T)
            cp.start()
            rcps.append(cp)
        for k, peer in enumerate(peers):
            pltpu.make_async_remote_copy(src_ref=rep_ref, dst_ref=rep_all.at[_dev_index(*peer)], send_sem=rsend_sems.at[k],
                                         recv_sem=rrecv_sems.at[k], device_id=peer, device_id_type=MESH_T).wait_recv()
        end()
        for cp in rcps:
            cp.wait_send()
        mine.wait()

    outs = pl.pallas_call(
        body, in_specs=[ANY] * (n + 1), out_specs=[ANY] * (n + 1),
        out_shape=[_sds(a.shape, a.dtype) for a in parts] + [_sds((N_DEV,) + rep.shape, rep.dtype)],
        scratch_shapes=[pltpu.SemaphoreType.DMA((n, 3)), pltpu.SemaphoreType.DMA((n, 3)), pltpu.SemaphoreType.DMA((7,)),
                        pltpu.SemaphoreType.DMA((7,)), pltpu.SemaphoreType.DMA],
        name="exchange_chips")(*parts, rep)
    return outs[:n], outs[n]


def adamw(name, terms, w, m, v, transposed=False):
    r, c = w.shape[::-1] if transposed else w.shape
    rb = r if transposed else min(r, 262144 // c)
    c1 = 1.0 - ADAM_B1 ** ADAM_STEP
    c2 = 1.0 - ADAM_B2 ** ADAM_STEP
    nt = len(terms)

    def body(*refs):
        w_ref, m_ref, v_ref = refs[nt:nt + 3]
        g_ref, d_ref, nm_ref, nv_ref = refs[nt + 3:]
        g = refs[0][...].astype(F32)
        for t_ref in refs[1:nt]:
            g = g + t_ref[...].astype(F32)
        if transposed:
            g = g.T
        nm = ADAM_B1 * m_ref[...] + (1.0 - ADAM_B1) * g
        nv = ADAM_B2 * v_ref[...] + (1.0 - ADAM_B2) * (g * g)
        g_ref[...] = g
        nm_ref[...] = nm
        nv_ref[...] = nv
        d_ref[...] = -ADAM_LR * ((nm / c1) / (jnp.sqrt(nv / c2) + ADAM_EPS) + ADAM_WD * w_ref[...])

    blk = pl.BlockSpec((rb, c), lambda i: (i, 0))
    wblk = pl.BlockSpec((c, r), lambda i: (0, 0)) if transposed else blk
    tspecs = [blk if k is None else pl.BlockSpec((None, rb, c), lambda i, k=k: (k, i, 0)) for _, k in terms]
    return pl.pallas_call(body, grid=(r // rb,), in_specs=tspecs + [wblk] * 3, out_specs=[wblk] * 4,
                          out_shape=[_sds(w.shape)] * 4, name=name,
                          compiler_params=_cp(("parallel",)))(*[t for t, _ in terms], w, m, v)


SMS_ROWS = 16
REP_ROWS = 24
N_BIG = 8
SMALL_SHARDED = (("rwkv_w2", (2, 32, 32)), ("rwkv_a2", (2, 32, 32)), ("rwkv_g2", (2, 64, 32)), ("rwkv_v2", (1, 32, 32)),
                 ("ssd_conv_w", (2, 4, 96)))
REPLICATED = (("lower_bounds", (2, 256)), ("mu_shift", (2, 896)), ("mu_vres", (1, 32)), ("rwkv_w0", (2, 256)),
              ("rwkv_a0", (2, 256)), ("rwkv_k_k", (2, 256)), ("rwkv_k_a", (2, 256)), ("rwkv_r_k", (2, 4, 64)),
              ("rwkv_lnx_w", (2, 256)), ("rwkv_lnx_b", (2, 256)), ("rwkv_v0", (1, 256)), ("ssd_conv_b", (2, 768)),
              ("ssd_dt_bias", (2, 4)), ("ssd_A_log", (2, 4)), ("ssd_D", (2, 4)), ("ssd_norm_w", (2, 256)),
              ("hgrn_norm_w", (2, 256)), ("ln1_w", (2, 1024)), ("ln1_b", (2, 1024)), ("ln2_w", (2, 1024)),
              ("ln2_b", (2, 1024)))


def _flat_rows(parts, rows):
    flat = jnp.concatenate([a.reshape(-1) for a in parts])
    return jnp.concatenate([flat, jnp.zeros((rows * PACK_W - flat.shape[0],), flat.dtype)]).reshape(rows, PACK_W)


def _local_arrays(d):
    arrs = [_w_in_pad(d["w_in"][0], None), _w_in_pad(d["w_in"][1], d["w_in_vres"][0]), d["w_out"][0], d["w_out"][1],
            d["w_up"][0], d["w_up"][1], d["w_down"][0], d["w_down"][1],
            _flat_rows([d[n] for n, _ in SMALL_SHARDED], SMS_ROWS)]
    return arrs, _flat_rows([d[n] for n, _ in REPLICATED], REP_ROWS)


def _unflat(rows2d, table):
    flat, out, o = rows2d.reshape(-1), {}, 0
    for name, shape in table:
        n = 1
        for s in shape:
            n *= s
        out[name] = flat[o:o + n].reshape(shape)
        o += n
    return out


def _from_local_arrays(arrs, rep):
    d = {}
    g0, _ = _w_in_unpad(arrs[0])
    g1, gv = _w_in_unpad(arrs[1])
    d["w_in"], d["w_in_vres"] = jnp.stack([g0, g1]), gv[None]
    d["w_out"] = jnp.stack([arrs[2], arrs[3]])
    d["w_up"] = jnp.stack([arrs[4], arrs[5]])
    d["w_down"] = jnp.stack([arrs[6], arrs[7]])
    d.update(_unflat(arrs[8], SMALL_SHARDED))
    d.update(_unflat(rep, REPLICATED))
    return d


def _small_sharded_full(gs):
    small, flat, o = {}, gs.reshape(N_DEV, -1), 0
    for name, shape in SMALL_SHARDED:
        n = shape[0] * shape[1] * shape[2]
        blk = flat[:, o:o + n].reshape((N_DEV,) + shape)
        small[name] = blk.transpose(1, 2, 0, 3).reshape(shape[0], shape[1], N_DEV * shape[2])
        o += n
    return small


def _owner_blocks(g):
    return g.reshape(N_DEV, g.shape[0] // N_DEV, g.shape[1])


def _small_send_arrays(small_grads):
    sms = []
    for name, shape in SMALL_SHARDED:
        g = small_grads[name].reshape(shape[0], shape[1], N_DEV, shape[2]).transpose(2, 0, 1, 3)
        sms.append(g.reshape(N_DEV, -1))
    sms = jnp.concatenate(sms, axis=1)
    sms = jnp.concatenate([sms, jnp.zeros((N_DEV, SMS_ROWS * PACK_W - sms.shape[1]), F32)], axis=1)
    return sms.reshape(N_DEV, SMS_ROWS, PACK_W), _flat_rows([small_grads[n] for n, _ in REPLICATED], REP_ROWS)


BIG_KEYS = ("w_in", "w_out", "w_up_t", "w_down")


def _local_step(x, tgt, wts, raw, gather=(), pair_sums=None, reducer=None):
    consts = _consts()
    ps = [_layer_params(l, raw, consts) for l in range(DEPTH)]
    x1, sv0 = layer_fwd(0, x, None, wts[0], ps[0], gather[:4], late=bool(gather))
    wts1 = {"w_in": _full_rows(sv0["gathered"][3])} if gather else wts[1]
    x2, sv1 = layer_fwd(1, x1, sv0["fl"], wts1, ps[1], gather[4:], late=bool(gather))
    dy, lparts = loss_call(x2, tgt)
    loss = jnp.sum(lparts[::8, 0])
    dx1, dvfirst, g1 = layer_bwd(1, dy, None, sv1, sv1["wts"], ps[1], (), reducer)
    big1 = {k: g1[k] for k in BIG_KEYS}
    if reducer is None:
        dx0, _, g0 = layer_bwd(0, dx1, dvfirst, sv0, sv0["wts"], ps[0])
        early = None
    else:
        own_in1, parts_in1 = pair_sums("1b", {"w_in": g1["w_in"]})
        dx0, _, g0 = layer_bwd(0, dx1, dvfirst, sv0, sv0["wts"], ps[0], parts_in1, reducer)
        own, recv = {(1, "w_in"): own_in1[0]}, {(1, "w_in"): g0["exchanged"][0]}
        for l, g, first in ((1, g1, 0), (0, g0, 1)):
            for i, k in enumerate(LATE_KEYS):
                own[(l, k)], recv[(l, k)] = g["early_own"][i], g["exchanged"][first + i]
        early = (own, recv)
    big = [{k: g0[k] for k in BIG_KEYS}, big1]
    return loss, dx0, big, _natural_grads(g0, g1), early


WEIGHT_NAMES = ("lower_bounds", "w_in", "w_in_vres", "mu_shift", "mu_vres", "rwkv_w0", "rwkv_w2", "rwkv_a0", "rwkv_a2",
                "rwkv_g2", "rwkv_k_k", "rwkv_k_a", "rwkv_r_k", "rwkv_lnx_w", "rwkv_lnx_b", "rwkv_v0", "rwkv_v2",
                "ssd_conv_w", "ssd_conv_b", "ssd_dt_bias", "ssd_A_log", "ssd_D", "ssd_norm_w", "hgrn_norm_w", "w_out",
                "ln1_w", "ln1_b", "w_up", "w_down", "ln2_w", "ln2_b")


def kernel(x, lower_bounds, w_in, w_in_vres, mu_shift, mu_vres, rwkv_w0, rwkv_w2, rwkv_a0, rwkv_a2, rwkv_g2, rwkv_k_k, rwkv_k_a, rwkv_r_k, rwkv_lnx_w, rwkv_lnx_b, rwkv_v0, rwkv_v2, ssd_conv_w, ssd_conv_b, ssd_dt_bias, ssd_A_log, ssd_D, ssd_norm_w, hgrn_norm_w, w_out, ln1_w, ln1_b, w_up, w_down, ln2_w, ln2_b, loss_target, m_lower_bounds, m_w_in, m_w_in_vres, m_mu_shift, m_mu_vres, m_rwkv_w0, m_rwkv_w2, m_rwkv_a0, m_rwkv_a2, m_rwkv_g2, m_rwkv_k_k, m_rwkv_k_a, m_rwkv_r_k, m_rwkv_lnx_w, m_rwkv_lnx_b, m_rwkv_v0, m_rwkv_v2, m_ssd_conv_w, m_ssd_conv_b, m_ssd_dt_bias, m_ssd_A_log, m_ssd_D, m_ssd_norm_w, m_hgrn_norm_w, m_w_out, m_ln1_w, m_ln1_b, m_w_up, m_w_down, m_ln2_w, m_ln2_b, v_lower_bounds, v_w_in, v_w_in_vres, v_mu_shift, v_mu_vres, v_rwkv_w0, v_rwkv_w2, v_rwkv_a0, v_rwkv_a2, v_rwkv_g2, v_rwkv_k_k, v_rwkv_k_a, v_rwkv_r_k, v_rwkv_lnx_w, v_rwkv_lnx_b, v_rwkv_v0, v_rwkv_v2, v_ssd_conv_w, v_ssd_conv_b, v_ssd_dt_bias, v_ssd_A_log, v_ssd_D, v_ssd_norm_w, v_hgrn_norm_w, v_w_out, v_ln1_w, v_ln1_b, v_w_up, v_w_down, v_ln2_w, v_ln2_b):
    given = dict(locals())
    w = {n: given[n] for n in WEIGHT_NAMES}
    w_arrs, w_rep = _local_arrays(w)
    m_arrs, m_rep = _local_arrays({n: given["m_" + n] for n in WEIGHT_NAMES})
    v_arrs, v_rep = _local_arrays({n: given["v_" + n] for n in WEIGHT_NAMES})
    wire = lambda a: (w_arrs[a].T if a in (4, 5) else w_arrs[a]).astype(BF16)
    gathered0 = all_gather([wire(0), w_arrs[N_BIG]])
    raw = {n: w[n] for n, _ in REPLICATED}
    raw.update(_small_sharded_full(gathered0[1]))
    mx, my, mc = lax.axis_index("x"), lax.axis_index("y"), lax.axis_index("c")
    slots = jnp.stack([_dev_index(cx, cy, mc) for cx, cy in _chips(mx, my)]).astype(jnp.int32)

    def reducer(tag, send, sib, n_f32=0):
        wire_dt = [BF16] * (len(send) - n_f32) + [F32] * n_f32
        res = [reduce_pair(f"reduce_pair{tag}_{i}", s, slots, sb, dt) for i, (s, sb, dt) in enumerate(zip(send, sib, wire_dt))]
        return [o for o, _ in res], [pt for _, pt in res]

    def pair_sums(tag, grads, extra=()):
        send = [_owner_blocks(g) for g in grads.values()] + list(extra)
        return reducer(tag, send, exchange_siblings(send, f"exchange_siblings{tag}"), len(extra))

    behind_scan = [wire(a) for a in (2, 4, 6, 1, 3, 5, 7)]
    loss, dx, big, small_grads, (own_by, recv_by) = _local_step(
        x[0], loss_target[0], [{"w_in": _full_rows(gathered0[0])}, None], raw, behind_scan, pair_sums, reducer)
    sms_send, rep = _small_send_arrays(small_grads)
    own0b, parts0b = pair_sums("0b", {"w_in": big[0]["w_in"]}, [sms_send])
    recv0b, rep_all = exchange_chips(parts0b, rep)
    own, recv = [None] * (N_BIG + 1), [None] * (N_BIG + 1)
    for (l, k), o in own_by.items():
        a = 2 * BIG_KEYS.index(k) + l
        own[a], recv[a] = o, recv_by[(l, k)]
    for a, o, r in zip((0, N_BIG), own0b, recv0b):
        own[a], recv[a] = o, r
    results = [adamw(f"adamw{a}", [(own[a], None), (recv[a], 0), (recv[a], 1), (recv[a], 2)], w_arrs[a], m_arrs[a], v_arrs[a],
                     transposed=a in (4, 5)) for a in range(N_BIG + 1)]
    rep_res = adamw("adamw_rep", [(rep_all, q) for q in range(N_DEV)], w_rep, m_rep, v_rep)
    loss = lax.psum(loss, ("x", "y", "c"))
    outs = [loss, dx[None]]
    for q in range(4):
        d = _from_local_arrays([res[q] for res in results], rep_res[q])
        outs += [d[n] for n in WEIGHT_NAMES]
    return tuple(outs)
```

```python
import functools

import jax
import jax.numpy as jnp
from jax import lax
from jax.experimental import pallas as pl
from jax.experimental.pallas import tpu as pltpu

F32 = jnp.float32
BF16 = jnp.bfloat16
HI = lax.Precision.HIGHEST

N_DEV = 8
SEQ = 2048
D_MODEL = 1024
D_FF = 4096
DG = 256
NH = 4
HD = 64
DEPTH = 2
ALPHA = (2.0 * DEPTH) ** 0.25
LN_EPS = 1e-5
RMS_EPS = 1e-5
GN_EPS = HD * 1e-5
SSD_N = 128
SSD_CHUNK = 128
HGRN_CHUNK = 16
DILATED = ((128, 1), (512, 4), (2048, 16))

ADAM_LR, ADAM_B1, ADAM_B2, ADAM_EPS, ADAM_WD, ADAM_STEP = 0.001, 0.9, 0.999, 1e-08, 0.01, 10

PW = 4096
C_R, C_K, C_V = 0, 256, 512
C_AQ, C_AK, C_AV = 768, 1024, 1280
C_Z, C_XBC = 1536, 1792
C_HQ, C_HF, C_HI, C_HG = 2560, 2816, 3072, 3328
C_LORA, C_DT, C_VRES = 3584, 3712, 3840

RB = 256
VMEM_LIMIT = 56 * 1024 * 1024
PACK_W = 1024


def _cp(sem=None):
    return pltpu.CompilerParams(dimension_semantics=sem, vmem_limit_bytes=VMEM_LIMIT)


def _sds(shape, dt=F32):
    return jax.ShapeDtypeStruct(tuple(shape), dt)


def _rows(w, cb=0, rb=RB):
    return pl.BlockSpec((rb, w), lambda i: (i, cb))


def _full(shape):
    n = len(shape)
    return pl.BlockSpec(tuple(shape), lambda *_: (0,) * n)


def _sigmoid(x):
    return 1.0 / (1.0 + jnp.exp(-x))


def _silu(x):
    return x * _sigmoid(x)


def _softplus(x):
    return jnp.maximum(x, 0.0) + jnp.log(1.0 + jnp.exp(jnp.where(x > 0, -x, x)))


MID = lax.Precision.HIGH
NN, TN, NT = (((1,), (0,)), ((), ())), (((0,), (0,)), ((), ())), (((1,), (1,)), ((), ()))


def _dot(a, b):
    return lax.dot_general(a, b, NN, precision=MID, preferred_element_type=F32)


def _dot_tn(a, b):
    return lax.dot_general(a, b, TN, precision=MID, preferred_element_type=F32)


def _dot_nt(a, b):
    return lax.dot_general(a, b, NT, precision=MID, preferred_element_type=F32)


def _dotx(a, b):
    return lax.dot_general(a, b, NN, precision=HI, preferred_element_type=F32)


def _dotx_tn(a, b):
    return lax.dot_general(a, b, TN, precision=HI, preferred_element_type=F32)


def _seg_ones(n, seg):
    i = jnp.arange(n)
    return (i[:, None] // seg == i[None, :] // seg).astype(F32)


def _shift_down(x, s):
    row = lax.broadcasted_iota(jnp.int32, x.shape, 0)
    return jnp.where(row < s, 0.0, pltpu.roll(x, s, 0))


def _shift_up(x, s):
    n = x.shape[0]
    row = lax.broadcasted_iota(jnp.int32, x.shape, 0)
    return jnp.where(row >= n - s, 0.0, pltpu.roll(x, n - s, 0))


@functools.partial(jax.custom_vjp, nondiff_argnums=(1,))
def _tshift(x, s):
    return _shift_down(x, s)


def _tshift_fwd(x, s):
    return _shift_down(x, s), None


def _tshift_bwd(s, _, g):
    return (_shift_up(g, s),)


_tshift.defvjp(_tshift_fwd, _tshift_bwd)


def _map_fwd(name, fn, grid, ins, in_specs, out_shapes, out_specs):
    n_in = len(ins)

    def body(*refs):
        ys = fn(*[r[...] for r in refs[:n_in]])
        for r, y in zip(refs[n_in:], ys):
            r[...] = y

    return pl.pallas_call(body, grid=grid, in_specs=in_specs, out_specs=out_specs, out_shape=out_shapes,
                          name=name, compiler_params=_cp(("parallel",)))(*ins)


def _map_bwd(name, fn, grid, ins, in_specs, cts, ct_specs, want, acc=(), gout=None):
    n_in = len(ins)
    flat_cts = [c for group in cts for c in group]
    flat_specs = [s for group in ct_specs for s in group]
    n_ct = len(flat_cts)
    gout = gout or {}
    out_shapes = [gout[i][0] if i in gout else _sds(ins[i].shape) for i in want]
    out_specs = [gout[i][1] if i in gout else in_specs[i] for i in want]

    def body(*refs):
        xs = [r[...] for r in refs[:n_in]]
        cvals = [r[...] for r in refs[n_in:n_in + n_ct]]
        gouts = refs[n_in + n_ct:]
        cs, p = [], 0
        for group in cts:
            v = cvals[p]
            for q in range(1, len(group)):
                v = v + cvals[p + q]
            cs.append(v)
            p += len(group)

        def f(*wanted):
            full = list(xs)
            for i, w in zip(want, wanted):
                full[i] = w
            return tuple(fn(*full))

        _, vjp = jax.vjp(f, *[xs[i] for i in want])
        gs = vjp(tuple(cs))
        for o, i, g in zip(gouts, want, gs):
            if i in acc:
                @pl.when(pl.program_id(0) == 0)
                def _():
                    o[...] = jnp.zeros_like(o)

                o[...] += g
            else:
                o[...] = g

    sem = ("arbitrary",) if acc else ("parallel",)
    return pl.pallas_call(body, grid=grid, in_specs=list(in_specs) + flat_specs, out_specs=out_specs,
                          out_shape=out_shapes, name=name, compiler_params=_cp(sem))(*ins, *flat_cts)


def _addn(name, *arrs):
    n, c = arrs[0].shape

    def fn(*xs):
        r = xs[0]
        for x in xs[1:]:
            r = r + x
        return (r,)

    return _map_fwd(name, fn, (n // RB,), list(arrs), [_rows(c)] * len(arrs), [_sds((n, c))], [_rows(c)])[0]


MM_TILES = {"k1024": (2048, 512, 1024), "k4096": (1024, 1024, 1024), "wgrad_tall": (2048, 1024, 512),
            "wgrad_wide": (1024, 2048, 512)}


def _mm(name, a, b, mode, tm, tn, tk, add=None, add_scale=1.0, epilogue=None, out_dtype=F32):
    if mode == "nn":
        (m, k), n = a.shape, b.shape[1]
    elif mode == "nt":
        (m, k), n = a.shape, b.shape[0]
    else:
        (k, m), n = a.shape, b.shape[1]
    nk = k // tk
    dn = {"nn": (((1,), (0,)), ((), ())), "nt": (((1,), (1,)), ((), ())), "tn": (((0,), (0,)), ((), ()))}[mode]

    def body(*refs):
        a_ref, b_ref = refs[:2]
        add_ref = refs[2] if add is not None else None
        o_ref = refs[3] if add is not None else refs[2]
        prod = lax.dot_general(a_ref[...].astype(BF16), b_ref[...].astype(BF16), dn, preferred_element_type=F32)

        def finish(r):
            if epilogue == "relu2":
                r = jnp.maximum(r, 0.0)
                r = r * r
            elif epilogue == "relu2_bwd":
                r = r * (2.0 * jnp.sqrt(add_ref[...]))
            elif add is not None:
                r = r + add_scale * add_ref[...]
            o_ref[...] = r.astype(out_dtype)

        if nk == 1:
            finish(prod)
        else:
            acc = refs[-1]
            kk = pl.program_id(2)

            @pl.when(kk == 0)
            def _():
                acc[...] = prod

            @pl.when(kk > 0)
            def _():
                acc[...] += prod

            @pl.when(kk == nk - 1)
            def _():
                finish(acc[...])

    a_spec = pl.BlockSpec((tk, tm), lambda i, j, q: (q, i)) if mode == "tn" else pl.BlockSpec((tm, tk), lambda i, j, q: (i, q))
    b_spec = pl.BlockSpec((tn, tk), lambda i, j, q: (j, q)) if mode == "nt" else pl.BlockSpec((tk, tn), lambda i, j, q: (q, j))
    o_spec = pl.BlockSpec((tm, tn), lambda i, j, q: (i, j))
    ins, specs = [a, b], [a_spec, b_spec]
    if add is not None:
        ins.append(add)
        specs.append(o_spec)
    return pl.pallas_call(body, grid=(m // tm, n // tn, nk), in_specs=specs, out_specs=o_spec,
                          out_shape=_sds((m, n), out_dtype),
                          scratch_shapes=[pltpu.VMEM((tm, tn), F32)] if nk > 1 else [], name=name,
                          compiler_params=_cp(("parallel", "parallel", "arbitrary")))(*ins)


def _lerp_colmap(j):
    r = jnp.where(j < 6, j, jnp.where(j == 6, C_LORA // 128, C_VRES // 128))
    return (0, r)


def _lerp_fn(f, mu):
    return (f + (_tshift(f, 1) - f) * mu,)


def _lerp_specs():
    return [pl.BlockSpec((SEQ, 128), _lerp_colmap), pl.BlockSpec((1, 128), lambda j: (0, j))]


def lerp_fwd(l, proj, mu):
    return _map_fwd(f"lerp_fwd{l}", _lerp_fn, (8,), [proj, mu], _lerp_specs(), [_sds((SEQ, 1024))],
                    [pl.BlockSpec((SEQ, 128), lambda j: (0, j))])[0]


def lerp_bwd(l, proj, mu, dfl):
    n_in = 2

    def body(f_ref, mu_ref, g_ref, df_ref, dmu_ref):
        _, vjp = jax.vjp(_lerp_fn, f_ref[...], mu_ref[...])
        df, dmu = vjp((g_ref[...],))
        df_ref[...] = df
        dmu_ref[...] = dmu

    cspec = pl.BlockSpec((SEQ, 128), lambda j: (0, j))
    return pl.pallas_call(body, grid=(8,), in_specs=_lerp_specs() + [cspec],
                          out_specs=[cspec, pl.BlockSpec((1, 128), lambda j: (0, j))],
                          out_shape=[_sds((SEQ, 1024)), _sds((1, 1024))], name=f"lerp_bwd{l}",
                          compiler_params=_cp(("parallel",)))(proj, mu, dfl)


def _conv_fn(x, w, b):
    y = x * w[3:4, :] + _tshift(x, 1) * w[2:3, :] + _tshift(x, 2) * w[1:2, :] + _tshift(x, 3) * w[0:1, :] + b
    return (_silu(y),)


def _conv_specs():
    return [pl.BlockSpec((SEQ, 128), lambda j: (0, C_XBC // 128 + j)), pl.BlockSpec((4, 128), lambda j: (0, j)),
            pl.BlockSpec((1, 128), lambda j: (0, j))]


def conv_fwd(l, proj, w, b):
    return _map_fwd(f"conv_fwd{l}", _conv_fn, (6,), [proj, w, b], _conv_specs(), [_sds((SEQ, 768))],
                    [pl.BlockSpec((SEQ, 128), lambda j: (0, j))])[0]


def conv_bwd(l, proj, w, b, dxc):
    def body(x_ref, w_ref, b_ref, g_ref, dx_ref, dw_ref, db_ref):
        _, vjp = jax.vjp(_conv_fn, x_ref[...], w_ref[...], b_ref[...])
        dx, dw, db = vjp((g_ref[...],))
        dx_ref[...] = dx
        dw_ref[...] = dw
        db_ref[...] = db

    cspec = pl.BlockSpec((SEQ, 128), lambda j: (0, j))
    return pl.pallas_call(body, grid=(6,), in_specs=_conv_specs() + [cspec],
                          out_specs=[cspec, pl.BlockSpec((4, 128), lambda j: (0, j)), pl.BlockSpec((1, 128), lambda j: (0, j))],
                          out_shape=[_sds((SEQ, 768)), _sds((4, 768)), _sds((1, 768))], name=f"conv_bwd{l}",
                          compiler_params=_cp(("parallel",)))(proj, w, b, dxc)


def _rwkv_pre_fn(has_vres):
    def fn(fk, fv, flora, *rest):
        if has_vres:
            fvres, vfirst, w0, w2p, a0, a2p, g2p, k_k, k_a, v0, v2p, seg = rest
        else:
            w0, w2p, a0, a2p, g2p, k_k, k_a, seg = rest
        w_log = -_softplus(-(w0 + _dot(jnp.tanh(flora), w2p))) - 0.5
        w = jnp.exp(-jnp.exp(w_log))
        a = _sigmoid(a0 + _dot(flora, a2p))
        g = _dot(_sigmoid(flora), g2p)
        if has_vres:
            v2 = fv + (vfirst - fv) * _sigmoid(v0 + _dot(fvres, v2p))
        else:
            v2 = fv * 1.0
        kk = fk * k_k
        kk = kk / jnp.maximum(jnp.sqrt(_dot(kk * kk, seg)), 1e-12)
        k2 = fk * (1.0 + (a - 1.0) * k_a)
        return w, k2, v2, -kk, kk * a, g

    return fn


def _rwkv_pre_args(fl, vfirst, p, has_vres):
    ins = [fl, fl, fl]
    specs = [_rows(256, 1), _rows(256, 2), _rows(128, 6)]
    if has_vres:
        ins += [fl, vfirst]
        specs += [_rows(128, 7), _rows(256, 2)]
    names = ["w0", "w2p", "a0", "a2p", "g2p", "k_k", "k_a"] + (["v0", "v2p"] if has_vres else []) + ["seg64"]
    for nme in names:
        ins.append(p[nme])
        specs.append(_full(p[nme].shape))
    return ins, specs, names


def rwkv_pre_fwd(l, fl, vfirst, p):
    has_vres = l > 0
    ins, specs, _ = _rwkv_pre_args(fl, vfirst, p, has_vres)
    return _map_fwd(f"rwkv_pre_fwd{l}", _rwkv_pre_fn(has_vres), (SEQ // RB,), ins, specs,
                    [_sds((SEQ, DG))] * 6, [_rows(DG)] * 6)


def rwkv_pre_bwd(l, fl, vfirst, p, cts):
    has_vres = l > 0
    ins, specs, names = _rwkv_pre_args(fl, vfirst, p, has_vres)
    n_row = 5 if has_vres else 3
    want = list(range(n_row)) + [n_row + i for i, nme in enumerate(names) if nme != "seg64"]
    acc = tuple(w for w in want if w >= n_row)
    ct_specs = [[_rows(DG)] * len(g) for g in cts]
    gout = {0: (_sds((SEQ, DG)), _rows(DG)), 1: (_sds((SEQ, DG)), _rows(DG)), 2: (_sds((SEQ, 128)), _rows(128))}
    if has_vres:
        gout[3] = (_sds((SEQ, 128)), _rows(128))
        gout[4] = (_sds((SEQ, DG)), _rows(DG))
    gs = _map_bwd(f"rwkv_pre_bwd{l}", _rwkv_pre_fn(has_vres), (SEQ // RB,), ins, specs, cts, ct_specs, want, acc, gout)
    keys = ["fk", "fv", "flora"] + (["fvres", "vfirst"] if has_vres else []) + [nme for nme in names if nme != "seg64"]
    return dict(zip(keys, gs))


def _rwkv_post_fn(y, fr, k2, v2, g, lnx_w, lnx_b, r_k, seg):
    mu = _dot(y, seg) * (1.0 / HD)
    d = y - mu
    var = _dot(d * d, seg) * (1.0 / HD)
    yn = d * lax.rsqrt(var + GN_EPS) * lnx_w + lnx_b
    bonus = _dot(fr * k2 * r_k, seg) * v2
    return ((yn + bonus) * g,)


def _rwkv_post_args(y, fl, k2, v2, g, p):
    ins = [y, fl, k2, v2, g, p["lnx_w"], p["lnx_b"], p["r_k"], p["seg64"]]
    specs = [_rows(DG), _rows(DG, 0), _rows(DG), _rows(DG), _rows(DG)] + [_full(x.shape) for x in ins[5:]]
    return ins, specs


def rwkv_post_fwd(l, y, fl, k2, v2, g, p):
    ins, specs = _rwkv_post_args(y, fl, k2, v2, g, p)
    return _map_fwd(f"rwkv_post_fwd{l}", _rwkv_post_fn, (SEQ // RB,), ins, specs, [_sds((SEQ, DG))], [_rows(DG)])[0]


def rwkv_post_bwd(l, y, fl, k2, v2, g, p, dya):
    ins, specs = _rwkv_post_args(y, fl, k2, v2, g, p)
    gs = _map_bwd(f"rwkv_post_bwd{l}", _rwkv_post_fn, (SEQ // RB,), ins, specs, [[dya]], [[_rows(DG, 0)]],
                  want=[0, 1, 2, 3, 4, 5, 6, 7], acc=(5, 6, 7), gout={1: (_sds((SEQ, DG)), _rows(DG))})
    return dict(zip(["y", "fr", "k2", "v2", "g", "lnx_w", "lnx_b", "r_k"], gs))


SCAN_TB = 128


def _coltile8(rows8, dmask, ones_stack, parts):
    pieces, rest = [], rows8
    for q in range(parts):
        piece = rest.astype(BF16).astype(F32)
        if q < parts - 1:
            rest = rest - piece
        pieces.append((piece[:, None, :] * dmask[None]).reshape(8 * HD, DG).astype(BF16))
    x = pieces[0] if parts == 1 else jnp.concatenate(pieces, axis=1)
    return jnp.dot(x, ones_stack, preferred_element_type=F32).reshape(8, HD, DG)


def _coltiles_bf16(rows_list, dmask, ones_bf16):
    x = jnp.concatenate([(r8[:, None, :] * dmask[None]).reshape(8 * HD, DG).astype(BF16) for r8 in rows_list], axis=0)
    t = jnp.dot(x, ones_bf16, preferred_element_type=F32)
    return [t[q * 8 * HD:(q + 1) * 8 * HD].reshape(8, HD, DG) for q in range(len(rows_list))]


def _segrows8(x8, dmask, ones_bf16):
    t = jnp.dot(x8.reshape(8 * HD, DG).astype(BF16), ones_bf16, preferred_element_type=F32).reshape(8, HD, DG)
    return jnp.sum(t * dmask[None], axis=1)


def rwkv_scan_fwd(l, fl, w, k2, v2, c, b, p, gather=()):
    nblk = SEQ // SCAN_TB
    ng = len(gather)

    def body(*refs):
        r_ref, w_ref, k_ref, v_ref, c_ref, b_ref, ones_ref, dm_ref = refs[:8]
        y_ref, st_ref = refs[8 + ng:10 + ng]
        s_sc = refs[10 + 2 * ng]
        if ng:
            begin, middle, end = _gather_steps(refs[8:8 + ng], refs[10 + ng:10 + 2 * ng], *refs[11 + 2 * ng:])

            @pl.when(pl.program_id(0) == 0)
            def _():
                begin()

            @pl.when(pl.program_id(0) == (3 * nblk) // 4)
            def _():
                middle()

        @pl.when(pl.program_id(0) == 0)
        def _():
            s_sc[...] = jnp.zeros_like(s_sc)

        ones3, ones = ones_ref[...], ones_ref[0:DG, :]
        dmask = dm_ref[...]

        def group(gi, carry):
            t0 = pl.multiple_of(gi * 8, 8)
            sl = pl.ds(t0, 8)
            v8 = v_ref[sl, :]
            wt = _coltile8(w_ref[sl, :], dmask, ones3, 3)
            ct, bt, kt, rt = _coltiles_bf16([c_ref[sl, :], b_ref[sl, :], k_ref[sl, :], r_ref[sl, :]], dmask, ones)
            t = s_sc[...]
            for j in range(8):
                sa = jnp.sum(t * ct[j], axis=0, keepdims=True)
                t = t * wt[j] + bt[j] * sa + kt[j] * v8[j:j + 1, :]
                st_ref[t0 + j] = t
            s_sc[...] = t
            y_ref[sl, :] = jnp.sum(st_ref[sl] * rt, axis=1)
            return carry

        lax.fori_loop(0, SCAN_TB // 8, group, 0)

        if ng:
            @pl.when(pl.program_id(0) == nblk - 1)
            def _():
                end()

    row = pl.BlockSpec((SCAN_TB, DG), lambda i: (i, 0))
    ins = [fl, w, k2, v2, c, b, p["seg64x3_bf16"], p["dmask"]] + list(gather)
    specs = [row] * 6 + [_full((3 * DG, DG)), _full((HD, DG))] + [ANY] * ng
    outs = pl.pallas_call(body, grid=(nblk,), in_specs=specs,
                          out_specs=[row, pl.BlockSpec((SCAN_TB, HD, DG), lambda i: (i, 0, 0))] + [ANY] * ng,
                          out_shape=[_sds((SEQ, DG)), _sds((SEQ, HD, DG))] + _gather_shapes(gather),
                          scratch_shapes=[pltpu.VMEM((HD, DG), F32)] + (_gather_sems(ng) if ng else []),
                          name=f"rwkv_scan_fwd{l}", compiler_params=_cp(("arbitrary",)))(*ins)
    return outs[0], outs[1], list(outs[2:])


def rwkv_scan_bwd(l, fl, w, k2, v2, c, b, states, dy, p, exchange=()):
    nblk = SEQ // SCAN_TB
    nx = len(exchange)

    def body(*refs):
        r_ref, w_ref, k_ref, v_ref, c_ref, b_ref, dy_ref, st_ref, sp_ref, ones_ref, dm_ref = refs[:11]
        dr_ref, dw_ref, dk_ref, dv_ref, dc_ref, db_ref = refs[11 + nx:17 + nx]
        g_sc, prev_sc, d8_sc, dsa_sc = refs[17 + 2 * nx:21 + 2 * nx]
        i = pl.program_id(0)
        if nx:
            begin, end = _chip_exchange_steps(refs[11:11 + nx], refs[17 + nx:17 + 2 * nx], *refs[21 + 2 * nx:])

            @pl.when(i == 0)
            def _():
                begin()

        @pl.when(i == 0)
        def _():
            g_sc[...] = jnp.zeros_like(g_sc)

        ones3, ones = ones_ref[...], ones_ref[0:DG, :]
        dmask = dm_ref[...]
        first_block = i == nblk - 1

        def group(gr, carry):
            gi = SCAN_TB // 8 - 1 - gr
            t0 = pl.multiple_of(gi * 8, 8)
            sl = pl.ds(t0, 8)
            v8, dy8 = v_ref[sl, :], dy_ref[sl, :]
            t8 = st_ref[sl]
            @pl.when(gi > 0)
            def _():
                prev_sc[0] = st_ref[t0 - 1]

            @pl.when(gi == 0)
            def _():
                prev_sc[0] = jnp.where(first_block, 0.0, sp_ref[0])

            for j in range(1, 8):
                prev_sc[j] = t8[j - 1]
            tp8 = prev_sc[...]
            wt = _coltile8(w_ref[sl, :], dmask, ones3, 3)
            ct, bt, kt, rt = _coltiles_bf16([c_ref[sl, :], b_ref[sl, :], k_ref[sl, :], r_ref[sl, :]], dmask, ones)
            sa8 = jnp.sum(tp8 * ct, axis=1)
            g = g_sc[...]
            for j in range(7, -1, -1):
                g = g + rt[j] * dy8[j:j + 1, :]
                d8_sc[j] = g
                dsa = jnp.sum(g * bt[j], axis=0, keepdims=True)
                dsa_sc[j:j + 1, :] = dsa
                g = g * wt[j] + ct[j] * dsa
            g_sc[...] = g
            d8 = d8_sc[...]
            dsa8 = dsa_sc[...]
            dv_ref[sl, :] = jnp.sum(d8 * kt, axis=1)
            dr_ref[sl, :] = _segrows8(t8 * dy8[:, None, :], dmask, ones)
            dk_ref[sl, :] = _segrows8(d8 * v8[:, None, :], dmask, ones)
            dw_ref[sl, :] = _segrows8(tp8 * d8, dmask, ones)
            db_ref[sl, :] = _segrows8(d8 * sa8[:, None, :], dmask, ones)
            dc_ref[sl, :] = _segrows8(tp8 * dsa8[:, None, :], dmask, ones)
            return carry

        lax.fori_loop(0, SCAN_TB // 8, group, 0)

        if nx:
            @pl.when(i == nblk - 1)
            def _():
                end()

    row = pl.BlockSpec((SCAN_TB, DG), lambda i: (nblk - 1 - i, 0))
    st_spec = pl.BlockSpec((SCAN_TB, HD, DG), lambda i: (nblk - 1 - i, 0, 0))
    sp_spec = pl.BlockSpec((1, HD, DG), lambda i: (jnp.maximum((nblk - 1 - i) * SCAN_TB - 1, 0), 0, 0))
    ins = [fl, w, k2, v2, c, b, dy, states, states, p["seg64x3_bf16"], p["dmask"]] + list(exchange)
    specs = [row] * 7 + [st_spec, sp_spec, _full((3 * DG, DG)), _full((HD, DG))] + [ANY] * nx
    tile8 = pltpu.VMEM((8, HD, DG), F32)
    sems = [pltpu.SemaphoreType.DMA((nx, 3)), pltpu.SemaphoreType.DMA((nx, 3))] if nx else []
    outs = pl.pallas_call(body, grid=(nblk,), in_specs=specs, out_specs=[row] * 6 + [ANY] * nx,
                          out_shape=[_sds((SEQ, DG))] * 6 + [_sds(a.shape, a.dtype) for a in exchange],
                          scratch_shapes=[pltpu.VMEM((HD, DG), F32), tile8, tile8, pltpu.VMEM((8, DG), F32)] + sems,
                          name=f"rwkv_scan_bwd{l}", compiler_params=_cp(("arbitrary",)))(*ins)
    return outs[:6], list(outs[6:])


HG_ROWS = 128


HG_NC = HG_ROWS // HGRN_CHUNK


def _hgrn_block_fn(layer):
    def fn(hq, hf, hi, hg, sprev, lb0, lb1, norm_w, seg, bd, tri_bd, ones_bd, first_row, causal):
        e0 = jnp.exp(lb0 - jnp.maximum(lb0, lb1))
        e1 = jnp.exp(lb1 - jnp.maximum(lb0, lb1))
        sm0, sm1 = e0 / (e0 + e1), e1 / (e0 + e1)
        lb = (sm0 - sm0) if layer == 0 else ((sm0 + sm1) - sm0)
        forget = lb + (1.0 - lb) * _sigmoid(hf)
        logf = jnp.log(forget)
        kk = 1.0 - forget
        q = _silu(hq)
        c, nc = HGRN_CHUNK, HG_NC
        b = _dotx(tri_bd, logf)
        bl = _dotx(ones_bd, logf)
        split = lambda t: t.reshape(nc, c, DG)
        b4 = split(b)
        diff = (b4[:, :, None, :] - b4[:, None, :, :]).reshape(nc * c * c, DG)
        dec = jnp.exp(jnp.where(causal > 0.5, diff, -1e30))
        qrep = jnp.broadcast_to(split(q)[:, :, None, :], (nc, c, c, DG)).reshape(nc * c * c, DG)
        ktil = jnp.broadcast_to(split(kk)[:, None, :, :], (nc, c, c, DG)).reshape(nc * c * c, DG)
        vtil = jnp.broadcast_to(split(hi)[:, None, :, :], (nc, c, c, DG)).reshape(nc * c * c, DG)
        att = _dot(qrep * ktil * dec, seg)
        o_intra = jnp.sum((att * vtil).reshape(nc * c, c, DG), axis=1)
        kd4 = split(kk * jnp.exp(bl - b))
        qe4 = split(q * jnp.exp(b))
        v4 = split(hi)
        tot = jnp.exp(_dotx(first_row, bl))
        s, o_inter = sprev, []
        for ci in range(nc):
            o_inter.append(_dot_nt(qe4[ci], s))
            s = s * tot[ci:ci + 1, :] + _dot_tn(v4[ci], kd4[ci]) * bd
        o = o_intra + jnp.concatenate(o_inter, axis=0)
        ms = _dot(o * o, seg) * (1.0 / HD)
        y = o * lax.rsqrt(ms + RMS_EPS) * norm_w * _silu(hg)
        return y, s

    return fn


def _hgrn_consts(p):
    return [p["seg64"], p["seg64"], p["tri_bd128"], p["ones_bd128"], p["first_row"], p["causal_blk"]]


def hgrn_fwd(l, proj, p):
    fn = _hgrn_block_fn(l)

    def body(hq_ref, hf_ref, hi_ref, hg_ref, *rest):
        const_refs, (y_ref, st_ref, s_sc) = rest[:-3], rest[-3:]

        @pl.when(pl.program_id(0) == 0)
        def _():
            s_sc[...] = jnp.zeros_like(s_sc)

        sprev = s_sc[...]
        st_ref[0] = sprev
        y, snext = fn(hq_ref[...], hf_ref[...], hi_ref[...], hg_ref[...], sprev, *[r[...] for r in const_refs])
        y_ref[...] = y
        s_sc[...] = snext

    rows = lambda cb: pl.BlockSpec((HG_ROWS, DG), lambda i: (i, cb))
    ins = [proj, proj, proj, proj, p["lb0"], p["lb1"], p["hgrn_norm_w"]] + _hgrn_consts(p)
    specs = [rows(C_HQ // DG), rows(C_HF // DG), rows(C_HI // DG), rows(C_HG // DG)] + [_full(x.shape) for x in ins[4:]]
    return pl.pallas_call(body, grid=(SEQ // HG_ROWS,), in_specs=specs,
                          out_specs=[rows(0), pl.BlockSpec((1, DG, DG), lambda i: (i, 0, 0))],
                          out_shape=[_sds((SEQ, DG)), _sds((SEQ // HG_ROWS, DG, DG))],
                          scratch_shapes=[pltpu.VMEM((DG, DG), F32)], name=f"hgrn_fwd{l}",
                          compiler_params=_cp(("arbitrary",)))(*ins)


def hgrn_bwd(l, proj, states, dy, p, sibling=(), dy_col=0):
    fn = _hgrn_block_fn(l)
    nblk = SEQ // HG_ROWS
    n_const = len(_hgrn_consts(p))
    ns = len(sibling)

    def body(hq_ref, hf_ref, hi_ref, hg_ref, st_ref, dy_ref, lb0_ref, lb1_ref, nw_ref, *rest):
        const_refs, rest = rest[:n_const], rest[n_const:]
        dp_ref, dlb0_ref, dlb1_ref, dnw_ref = rest[ns:ns + 4]
        ds_sc = rest[2 * ns + 4]
        if ns:
            begin, end = _sibling_steps(rest[:ns], rest[ns + 4:2 * ns + 4], *rest[2 * ns + 5:])

            @pl.when(pl.program_id(0) == 0)
            def _():
                begin()

        @pl.when(pl.program_id(0) == 0)
        def _():
            ds_sc[...] = jnp.zeros_like(ds_sc)
            dlb0_ref[...] = jnp.zeros_like(dlb0_ref)
            dlb1_ref[...] = jnp.zeros_like(dlb1_ref)
            dnw_ref[...] = jnp.zeros_like(dnw_ref)

        consts = [r[...] for r in const_refs]
        f = lambda hq, hf, hi, hg, sp, b0, b1, nw: fn(hq, hf, hi, hg, sp, b0, b1, nw, *consts)
        _, vjp = jax.vjp(f, hq_ref[...], hf_ref[...], hi_ref[...], hg_ref[...], st_ref[0], lb0_ref[...], lb1_ref[...],
                         nw_ref[...])
        dhq, dhf, dhi, dhg, dsp, dlb0, dlb1, dnw = vjp((dy_ref[...], ds_sc[...]))
        dp_ref[:, 0:DG] = dhq
        dp_ref[:, DG:2 * DG] = dhf
        dp_ref[:, 2 * DG:3 * DG] = dhi
        dp_ref[:, 3 * DG:4 * DG] = dhg
        ds_sc[...] = dsp
        dlb0_ref[...] += dlb0
        dlb1_ref[...] += dlb1
        dnw_ref[...] += dnw

        if ns:
            @pl.when(pl.program_id(0) == nblk - 1)
            def _():
                end()

    rows = lambda cb: pl.BlockSpec((HG_ROWS, DG), lambda i: (nblk - 1 - i, cb))
    ins = [proj, proj, proj, proj, states, dy, p["lb0"], p["lb1"], p["hgrn_norm_w"]] + _hgrn_consts(p)
    specs = [rows(C_HQ // DG), rows(C_HF // DG), rows(C_HI // DG), rows(C_HG // DG),
             pl.BlockSpec((1, DG, DG), lambda i: (nblk - 1 - i, 0, 0)), rows(dy_col)] + [_full(x.shape) for x in ins[6:]]
    sem = pltpu.SemaphoreType.DMA((max(ns, 1), 4))
    outs = pl.pallas_call(body, grid=(nblk,), in_specs=specs + [ANY] * ns,
                          out_specs=[pl.BlockSpec((HG_ROWS, 4 * DG), lambda i: (nblk - 1 - i, 0)), _full((1, DG)),
                                     _full((1, DG)), _full((1, DG))] + [ANY] * ns,
                          out_shape=[_sds((SEQ, 4 * DG)), _sds((1, DG)), _sds((1, DG)), _sds((1, DG))]
                          + [_sds((4,) + a.shape[1:], a.dtype) for a in sibling],
                          scratch_shapes=[pltpu.VMEM((DG, DG), F32)] + ([sem, sem] if ns else []), name=f"hgrn_bwd{l}",
                          compiler_params=_cp(("arbitrary",)))(*ins, *sibling)
    return outs[:4], list(outs[4:])


def _ssd_chunk_fn(z, xs, bm, cm, dtr, sprev, dt_bias, a_log, d_par, norm_w, e128, tri, trit, seg128, ones128):
    lc = SSD_CHUNK
    dt = _softplus(dtr + dt_bias)
    a = -jnp.exp(a_log)
    da = dt * a * (lax.broadcasted_iota(jnp.int32, (1, 128), 1) < NH).astype(F32)
    cs = _dotx(tri, da)
    cst = _dotx_tn(da, trit)
    cs_b = _dotx(cs, e128)
    dt_b = _dotx(dt, e128)
    csl_b = _dotx(jnp.sum(da, axis=0, keepdims=True), e128)
    xdt = xs * dt_b
    lane = lax.broadcasted_iota(jnp.int32, (1, DG), 1)
    rowi = lax.broadcasted_iota(jnp.int32, (lc, lc), 0)
    coli = lax.broadcasted_iota(jnp.int32, (lc, lc), 1)
    y = jnp.zeros((lc, DG), F32)
    snew = jnp.zeros((DG, SSD_N), F32)
    d_b = jnp.zeros((1, DG), F32)
    wdec = xdt * jnp.exp(csl_b - cs_b)
    for g in range(2):
        bg = bm[:, g * SSD_N:(g + 1) * SSD_N]
        cg = cm[:, g * SSD_N:(g + 1) * SSD_N]
        gmat = _dot_nt(cg, bg)
        gmask = ((lane // 128) == g).astype(F32)
        snew = snew + _dot_tn(wdec * gmask, bg)
        y = y + _dot_nt(cg, sprev) * gmask * jnp.exp(cs_b)
        for hh in range(2):
            h = 2 * g + hh
            seg = jnp.where(rowi >= coli, cs[:, h:h + 1] - cst[h:h + 1, :], -1e30)
            hmask = ((lane // HD) == h).astype(F32)
            y = y + _dot(gmat * jnp.exp(seg), xdt * hmask)
            d_b = d_b + d_par[:, h:h + 1] * hmask
    cd = jnp.exp(_dotx_tn(_dotx(da, e128), ones128))
    snext = sprev * cd + snew
    y = y + xs * d_b
    y = y * _silu(z)
    ms = _dot(y * y, seg128) * (1.0 / 128.0)
    return y * lax.rsqrt(ms + RMS_EPS) * norm_w, snext


def ssd_fwd(l, proj, xc, p):
    nc = SEQ // SSD_CHUNK

    def body(z_ref, xs_ref, b_ref, c_ref, dt_ref, dtb_ref, al_ref, d_ref, nw_ref, e_ref, tri_ref, trit_ref, sg_ref,
             on_ref, y_ref, st_ref, s_sc):
        @pl.when(pl.program_id(0) == 0)
        def _():
            s_sc[...] = jnp.zeros_like(s_sc)

        sprev = s_sc[...]
        st_ref[0] = sprev
        y, snext = _ssd_chunk_fn(z_ref[...], xs_ref[...], b_ref[...], c_ref[...], dt_ref[...], sprev, dtb_ref[...],
                                 al_ref[...], d_ref[...], nw_ref[...], e_ref[...], tri_ref[...], trit_ref[...],
                                 sg_ref[...], on_ref[...])
        y_ref[...] = y
        s_sc[...] = snext

    rw = lambda w, cb: pl.BlockSpec((SSD_CHUNK, w), lambda i: (i, cb))
    ins = [proj, xc, xc, xc, proj, p["dt_bias"], p["a_log"], p["ssd_d"], p["ssd_norm_w"], p["e128"], p["tri128"],
           p["tri128t"], p["seg128"], p["ones128"]]
    specs = [rw(DG, C_Z // DG), rw(DG, 0), rw(DG, 1), rw(DG, 2), rw(128, C_DT // 128)] + [_full(x.shape) for x in ins[5:]]
    return pl.pallas_call(body, grid=(nc,), in_specs=specs,
                          out_specs=[rw(DG, 0), pl.BlockSpec((1, DG, SSD_N), lambda i: (i, 0, 0))],
                          out_shape=[_sds((SEQ, DG)), _sds((nc, DG, SSD_N))],
                          scratch_shapes=[pltpu.VMEM((DG, SSD_N), F32)], name=f"ssd_fwd{l}",
                          compiler_params=_cp(("arbitrary",)))(*ins)


def ssd_bwd(l, proj, xc, states, dy, p, dy_col=0):
    nc = SEQ // SSD_CHUNK

    def body(z_ref, xs_ref, b_ref, c_ref, dt_ref, st_ref, dy_ref, dtb_ref, al_ref, d_ref, nw_ref, e_ref, tri_ref,
             trit_ref, sg_ref, on_ref, dz_ref, dxc_ref, ddt_ref, ddtb_ref, dal_ref, dd_ref, dnw_ref, ds_sc):
        @pl.when(pl.program_id(0) == 0)
        def _():
            ds_sc[...] = jnp.zeros_like(ds_sc)
            ddtb_ref[...] = jnp.zeros_like(ddtb_ref)
            dal_ref[...] = jnp.zeros_like(dal_ref)
            dd_ref[...] = jnp.zeros_like(dd_ref)
            dnw_ref[...] = jnp.zeros_like(dnw_ref)

        consts = (e_ref[...], tri_ref[...], trit_ref[...], sg_ref[...], on_ref[...])
        f = lambda *a: _ssd_chunk_fn(*a, *consts)
        _, vjp = jax.vjp(f, z_ref[...], xs_ref[...], b_ref[...], c_ref[...], dt_ref[...], st_ref[0], dtb_ref[...],
                         al_ref[...], d_ref[...], nw_ref[...])
        dz, dxs, db, dc, ddt, dsp, ddtb, dal, dd, dnw = vjp((dy_ref[...], ds_sc[...]))
        dz_ref[...] = dz
        dxc_ref[:, 0:DG] = dxs
        dxc_ref[:, DG:2 * DG] = db
        dxc_ref[:, 2 * DG:3 * DG] = dc
        ddt_ref[...] = ddt
        ds_sc[...] = dsp
        ddtb_ref[...] += ddtb
        dal_ref[...] += dal
        dd_ref[...] += dd
        dnw_ref[...] += dnw

    rw = lambda w, cb: pl.BlockSpec((SSD_CHUNK, w), lambda i: (nc - 1 - i, cb))
    ins = [proj, xc, xc, xc, proj, states, dy, p["dt_bias"], p["a_log"], p["ssd_d"], p["ssd_norm_w"], p["e128"],
           p["tri128"], p["tri128t"], p["seg128"], p["ones128"]]
    specs = [rw(DG, C_Z // DG), rw(DG, 0), rw(DG, 1), rw(DG, 2), rw(128, C_DT // 128),
             pl.BlockSpec((1, DG, SSD_N), lambda i: (nc - 1 - i, 0, 0)), rw(DG, dy_col)] + [_full(x.shape) for x in ins[7:]]
    return pl.pallas_call(body, grid=(nc,), in_specs=specs,
                          out_specs=[rw(DG, 0), rw(3 * DG, 0), rw(128, 0), _full((1, 128)), _full((1, 128)), _full((1, 128)),
                                     _full((1, DG))],
                          out_shape=[_sds((SEQ, DG)), _sds((SEQ, 3 * DG)), _sds((SEQ, 128)), _sds((1, 128)), _sds((1, 128)),
                                     _sds((1, 128)), _sds((1, DG))],
                          scratch_shapes=[pltpu.VMEM((DG, SSD_N), F32)], name=f"ssd_bwd{l}",
                          compiler_params=_cp(("arbitrary",)))(*ins)


ATT_BLK = 128


def _att_geometry(dil):
    i = lax.broadcasted_iota(jnp.int32, (ATT_BLK, ATT_BLK), 0)
    j = lax.broadcasted_iota(jnp.int32, (ATT_BLK, ATT_BLK), 1)
    return ((i - j) * dil).astype(F32), ((ATT_BLK + i - j) * dil).astype(F32), j <= i, j >= i


def _att_scores(qn, kc, kp, h, geom, has_prev):
    dist_c, dist_p, m_c, m_pj = geom
    slope = 2.0 ** (-8.0 * (h + 1) / NH)
    scale = HD ** -0.5
    s_c = _dot_nt(qn, kc) * scale - slope * dist_c
    s_p = _dot_nt(qn, kp) * scale - slope * dist_p
    m_p = jnp.logical_and(m_pj, has_prev)
    return jnp.where(m_c, s_c, -1e30), jnp.where(m_p, s_p, -1e30), m_c, m_p


def _sub_spec(ln, width, col):
    return pl.BlockSpec((ln, DG), lambda z: (0, z * (width // DG) + col // DG))


QKV_W = 3 * DG


def attn_branch_fwd(l, bi, qkv, dil):
    ln = SEQ // dil
    nb = ln // ATT_BLK

    def body(q_ref, k_ref, v_ref, o_ref, l_ref):
        geom = _att_geometry(dil)

        def blk(n, carry):
            r0 = pl.multiple_of(n * ATT_BLK, ATT_BLK)
            rp = pl.multiple_of(jnp.maximum(n - 1, 0) * ATT_BLK, ATT_BLK)
            cur, prv = pl.ds(r0, ATT_BLK), pl.ds(rp, ATT_BLK)
            for h in range(NH):
                hs = slice(h * HD, (h + 1) * HD)
                qn, kc, vc, kp, vp = q_ref[cur, hs], k_ref[cur, hs], v_ref[cur, hs], k_ref[prv, hs], v_ref[prv, hs]
                s_c, s_p, m_c, m_p = _att_scores(qn, kc, kp, h, geom, n > 0)
                m = jnp.maximum(jnp.max(s_c, axis=1, keepdims=True), jnp.max(s_p, axis=1, keepdims=True))
                p_c = jnp.where(m_c, jnp.exp(s_c - m), 0.0)
                p_p = jnp.where(m_p, jnp.exp(s_p - m), 0.0)
                den = jnp.sum(p_c, axis=1, keepdims=True) + jnp.sum(p_p, axis=1, keepdims=True)
                o_ref[cur, hs] = (_dot(p_c, vc) + _dot(p_p, vp)) / den
                l_ref[cur, hs] = jnp.broadcast_to(m + jnp.log(den), (ATT_BLK, HD))
            return carry

        lax.fori_loop(0, nb, blk, 0)

    pv = qkv.reshape(ln, dil * QKV_W)
    out = pl.BlockSpec((ln, DG), lambda z: (0, z))
    o, lse = pl.pallas_call(body, grid=(dil,), in_specs=[_sub_spec(ln, QKV_W, 0), _sub_spec(ln, QKV_W, DG), _sub_spec(ln, QKV_W, 2 * DG)],
                            out_specs=[out, out], out_shape=[_sds((ln, dil * DG))] * 2, name=f"attn_fwd{l}_{bi}",
                            compiler_params=_cp(("parallel",)))(pv, pv, pv)
    return o.reshape(SEQ, DG), lse.reshape(SEQ, DG)


def attn_branch_bwd(l, bi, qkv, dil, dyb, lse_all, delta):
    ln = SEQ // dil
    nb = ln // ATT_BLK
    scale = HD ** -0.5

    def body(q_ref, k_ref, v_ref, do_ref, l_ref, dl_ref, dq_ref, dk_ref, dv_ref):
        dk_ref[...] = jnp.zeros_like(dk_ref)
        dv_ref[...] = jnp.zeros_like(dv_ref)
        geom = _att_geometry(dil)

        def blk(n, carry):
            r0 = pl.multiple_of(n * ATT_BLK, ATT_BLK)
            rp = pl.multiple_of(jnp.maximum(n - 1, 0) * ATT_BLK, ATT_BLK)
            cur, prv = pl.ds(r0, ATT_BLK), pl.ds(rp, ATT_BLK)
            for h in range(NH):
                hs = slice(h * HD, (h + 1) * HD)
                qn, don = q_ref[cur, hs], do_ref[cur, hs]
                lse, dlt = l_ref[cur, h * HD:h * HD + 1], dl_ref[cur, h * HD:h * HD + 1]
                kc, vc, kp, vp = k_ref[cur, hs], v_ref[cur, hs], k_ref[prv, hs], v_ref[prv, hs]
                s_c, s_p, m_c, m_p = _att_scores(qn, kc, kp, h, geom, n > 0)
                p_c = jnp.where(m_c, jnp.exp(s_c - lse), 0.0)
                p_p = jnp.where(m_p, jnp.exp(s_p - lse), 0.0)
                ds_c = p_c * (_dot_nt(don, vc) - dlt)
                ds_p = p_p * (_dot_nt(don, vp) - dlt)
                dq_ref[cur, hs] = (_dot(ds_c, kc) + _dot(ds_p, kp)) * scale
                dv_ref[prv, hs] += _dot_tn(p_p, don)
                dk_ref[prv, hs] += _dot_tn(ds_p, qn) * scale
                dv_ref[cur, hs] += _dot_tn(p_c, don)
                dk_ref[cur, hs] += _dot_tn(ds_c, qn) * scale
            return carry

        lax.fori_loop(0, nb, blk, 0)

    pv = qkv.reshape(ln, dil * QKV_W)
    sub = lambda t: t.reshape(ln, dil * DG)
    row = pl.BlockSpec((ln, DG), lambda z: (0, z))
    outs = pl.pallas_call(body, grid=(dil,),
                          in_specs=[_sub_spec(ln, QKV_W, 0), _sub_spec(ln, QKV_W, DG), _sub_spec(ln, QKV_W, 2 * DG), row, row, row],
                          out_specs=[row] * 3, out_shape=[_sds((ln, dil * DG))] * 3, name=f"attn_bwd{l}_{bi}",
                          compiler_params=_cp(("parallel",)))(pv, pv, pv, sub(dyb), sub(lse_all), sub(delta))
    return [t.reshape(SEQ, DG) for t in outs]


def _attn_merge_fn(o1, o2, o3, l1, l2, l3):
    m = jnp.maximum(jnp.maximum(l1, l2), l3)
    w1, w2, w3 = jnp.exp(l1 - m), jnp.exp(l2 - m), jnp.exp(l3 - m)
    den = w1 + w2 + w3
    return (w1 * o1 + w2 * o2 + w3 * o3) / den, m + jnp.log(den)


def attn_merge(l, os_, ls_):
    ins = list(os_) + list(ls_)
    return _map_fwd(f"attn_merge{l}", _attn_merge_fn, (SEQ // RB,), ins, [_rows(DG)] * 6, [_sds((SEQ, DG))] * 2,
                    [_rows(DG)] * 2)


def attn_delta(l, dyb, yb, seg):
    fn = lambda d, y, s: (_dot(d * y, s),)
    return _map_fwd(f"attn_delta{l}", fn, (SEQ // RB,), [dyb, yb, seg], [_rows(DG), _rows(DG), _full((DG, DG))],
                    [_sds((SEQ, DG))], [_rows(DG)])[0]


def _ln_fn(x, mix, w, b):
    h = ALPHA * x + mix
    mu = jnp.mean(h, axis=-1, keepdims=True)
    d = h - mu
    var = jnp.mean(d * d, axis=-1, keepdims=True)
    return (d * lax.rsqrt(var + LN_EPS) * w + b,)


def ln_fwd(name, x, mix, w, b):
    specs = [_rows(D_MODEL), _rows(D_MODEL), _full((1, D_MODEL)), _full((1, D_MODEL))]
    return _map_fwd(name, _ln_fn, (SEQ // RB,), [x, mix, w, b], specs, [_sds((SEQ, D_MODEL))], [_rows(D_MODEL)])[0]


def ln_bwd(name, x, mix, w, b, dy):
    specs = [_rows(D_MODEL), _rows(D_MODEL), _full((1, D_MODEL)), _full((1, D_MODEL))]
    return _map_bwd(name, _ln_fn, (SEQ // RB,), [x, mix, w, b], specs, [[dy]], [[_rows(D_MODEL)]], want=[1, 2, 3],
                    acc=(2, 3))


def loss_call(y, tgt):
    def fn(yy, tt):
        e = yy - tt
        part = 0.5 * jnp.sum(jnp.sum(e * e, axis=-1, keepdims=True) * (1.0 / D_MODEL), axis=0, keepdims=True)
        return e * (1.0 / D_MODEL), jnp.broadcast_to(part, (8, 128))

    return _map_fwd("loss", fn, (SEQ // RB,), [y, tgt], [_rows(D_MODEL)] * 2,
                    [_sds((SEQ, D_MODEL)), _sds((SEQ // RB * 8, 128))],
                    [_rows(D_MODEL), pl.BlockSpec((8, 128), lambda i: (i, 0))])


LATE_KEYS = ("w_out", "w_up_t", "w_down")


def _full_rows(g):
    return g.reshape(N_DEV * g.shape[1], g.shape[2])


def layer_fwd(l, x, vfirst, wts, p, gather=(), late=False):
    sv = {"x": x}
    proj = _mm(f"mm_in{l}", x, wts["w_in"], "nn", *MM_TILES["k1024"])
    fl = lerp_fwd(l, proj, p["mu"])
    xc = conv_fwd(l, proj, p["conv_w"], p["conv_b"])
    w, k2, v2, c, b, g = rwkv_pre_fwd(l, fl, vfirst, p)
    y_scan, states, sv["gathered"] = rwkv_scan_fwd(l, fl, w, k2, v2, c, b, p, gather)
    if late:
        wts = dict(wts, **dict(zip(LATE_KEYS, [_full_rows(g) for g in sv["gathered"][:3]])))
    sv["wts"] = wts
    ya = rwkv_post_fwd(l, y_scan, fl, k2, v2, g, p)
    qkv = proj[:, C_AQ:C_AQ + 3 * DG]
    outs, lses = [], []
    for bi, (win, dil) in enumerate(DILATED):
        o, lse = attn_branch_fwd(l, bi, qkv, dil)
        outs.append(o)
        lses.append(lse)
    yb, lse_all = attn_merge(l, outs, lses)
    yc, ssd_states = ssd_fwd(l, proj, xc, p)
    yd, hg_states = hgrn_fwd(l, proj, p)
    ycat = jnp.concatenate([ya, yb, yc, yd], axis=1).astype(BF16)
    mix = _mm(f"mm_out{l}", ycat, wts["w_out"], "nn", *MM_TILES["k1024"])
    x1 = ln_fwd(f"ln1_fwd{l}", x, mix, p["ln1_w"], p["ln1_b"])
    hh = _mm(f"mm_up{l}", x1, wts["w_up_t"], "nt", *MM_TILES["k1024"], epilogue="relu2")
    m2 = _mm(f"mm_down{l}", hh, wts["w_down"], "nn", *MM_TILES["k4096"])
    x2 = ln_fwd(f"ln2_fwd{l}", x1, m2, p["ln2_w"], p["ln2_b"])
    sv.update(proj=proj, fl=fl, xc=xc, w=w, k2=k2, v2=v2, c=c, b=b, g=g, y_scan=y_scan, states=states,
              yb=yb, lse_all=lse_all, ssd_states=ssd_states, hg_states=hg_states, ycat=ycat, mix=mix, x1=x1, hh=hh, qkv=qkv,
              m2=m2, vfirst=vfirst)
    return x2, sv


def layer_bwd(l, dx2, dvfirst_next, sv, wts, p, exchange=(), reducer=None):
    gr = {}
    x, x1, proj, fl = sv["x"], sv["x1"], sv["proj"], sv["fl"]
    dres2, gr["ln2_w"], gr["ln2_b"] = ln_bwd(f"ln2_bwd{l}", x1, sv["m2"], p["ln2_w"], p["ln2_b"], dx2)
    du = _mm(f"mm_down_dx{l}", dres2, wts["w_down"], "nt", *MM_TILES["k1024"], add=sv["hh"], epilogue="relu2_bwd",
             out_dtype=BF16)
    gr["w_down"] = _mm(f"mm_down_dw{l}", sv["hh"], dres2, "tn", *MM_TILES["wgrad_tall"])
    dx1 = _mm(f"mm_up_dx{l}", du, wts["w_up_t"], "nn", *MM_TILES["k4096"], add=dres2, add_scale=ALPHA)
    gr["w_up_t"] = _mm(f"mm_up_dw{l}", du, x1, "tn", *MM_TILES["wgrad_tall"])
    dres1, gr["ln1_w"], gr["ln1_b"] = ln_bwd(f"ln1_bwd{l}", x, sv["mix"], p["ln1_w"], p["ln1_b"], dx1)
    dycat = _mm(f"mm_out_dx{l}", dres1, wts["w_out"], "nt", *MM_TILES["k1024"])
    gr["w_out"] = _mm(f"mm_out_dw{l}", sv["ycat"], dres1, "tn", 1024, 1024, 512)
    dyb = dycat[:, DG:2 * DG]
    send = [_owner_blocks(gr[k]) for k in LATE_KEYS] if reducer else []
    (dhg4, gr["lb0"], gr["lb1"], gr["hgrn_norm_w"]), sib = hgrn_bwd(l, proj, sv["hg_states"], dycat, p, send, dy_col=3)
    if reducer:
        gr["early_own"], early_parts = reducer(f"{l}a", send, sib)
        exchange = list(exchange) + list(early_parts)
    dz, dxc, ddt, gr["dt_bias"], gr["a_log"], gr["ssd_d"], gr["ssd_norm_w"] = ssd_bwd(l, proj, sv["xc"], sv["ssd_states"], dycat, p, dy_col=2)
    dxbc, gr["conv_w"], gr["conv_b"] = conv_bwd(l, proj, p["conv_w"], p["conv_b"], dxc)
    delta = attn_delta(l, dyb, sv["yb"], p["seg64"])
    dqs, dks, dvs = [], [], []
    for bi, (win, dil) in enumerate(DILATED):
        dq, dk, dv = attn_branch_bwd(l, bi, sv["qkv"], dil, dyb, sv["lse_all"], delta)
        dqs.append(dq)
        dks.append(dk)
        dvs.append(dv)
    dq_a, dk_a, dv_a = _addn(f"attn_dq{l}", *dqs), _addn(f"attn_dk{l}", *dks), _addn(f"attn_dv{l}", *dvs)
    pg = rwkv_post_bwd(l, sv["y_scan"], fl, sv["k2"], sv["v2"], sv["g"], p, dycat)
    gr["lnx_w"], gr["lnx_b"], gr["r_k"] = pg["lnx_w"], pg["lnx_b"], pg["r_k"]
    (dr, dw, dk, dv, dc, db), gr["exchanged"] = rwkv_scan_bwd(l, fl, sv["w"], sv["k2"], sv["v2"], sv["c"], sv["b"],
                                                              sv["states"], pg["y"], p, exchange)
    v2_cts = [dv, pg["v2"]] + ([dvfirst_next] if dvfirst_next is not None else [])
    qg = rwkv_pre_bwd(l, fl, sv["vfirst"], p, [[dw], [dk, pg["k2"]], v2_cts, [dc], [db], [pg["g"]]])
    for nme in ("w0", "w2p", "a0", "a2p", "g2p", "k_k", "k_a", "v0", "v2p"):
        if nme in qg:
            gr[nme] = qg[nme]
    dfr = _addn(f"rwkv_dr{l}", dr, pg["fr"])
    dvres = qg["fvres"] if l > 0 else jnp.zeros((SEQ, 128), F32)
    dfl_out = jnp.concatenate([dfr, qg["fk"], qg["fv"], qg["flora"], dvres], axis=1)
    dfl_in, gr["mu"] = lerp_bwd(l, proj, p["mu"], dfl_out)
    dproj = jnp.concatenate([dfl_in[:, 0:768], dq_a, dk_a, dv_a, dz, dxbc, dhg4, dfl_in[:, 768:896], ddt,
                             dfl_in[:, 896:1024], jnp.zeros((SEQ, 128), F32)], axis=1).astype(BF16)
    dx = _mm(f"mm_in_dx{l}", dproj, wts["w_in"], "nt", *MM_TILES["k4096"], add=dres1, add_scale=ALPHA)
    gr["w_in"] = _mm(f"mm_in_dw{l}", x, dproj, "tn", *MM_TILES["wgrad_wide"])
    return dx, (qg["vfirst"] if l > 0 else None), gr


def _w_in_pad(w_in_l, w_vres):
    rows = w_in_l.shape[0]
    z = lambda n: jnp.zeros((rows, n), w_in_l.dtype)
    vres = z(128) if w_vres is None else jnp.concatenate([w_vres, z(96)], axis=1)
    return jnp.concatenate([w_in_l[:, 0:768], w_in_l[:, 896:1664], w_in_l[:, 1664:1920], w_in_l[:, 1920:2688],
                            w_in_l[:, 2692:3716], w_in_l[:, 768:896], w_in_l[:, 2688:2692], z(124), vres, z(128)], axis=1)


def _w_in_unpad(g):
    g_in = jnp.concatenate([g[:, 0:768], g[:, C_LORA:C_LORA + 128], g[:, 768:1536], g[:, C_Z:C_Z + 256],
                            g[:, C_XBC:C_XBC + 768], g[:, C_DT:C_DT + 4], g[:, C_HQ:C_HQ + 1024]], axis=1)
    return g_in, g[:, C_VRES:C_VRES + 32]


def _consts():
    pair = jnp.arange(HG_NC * HGRN_CHUNK * HGRN_CHUNK)
    i128 = jnp.arange(128)
    same_chunk = (i128[:, None] // HGRN_CHUNK) == (i128[None, :] // HGRN_CHUNK)
    seg64 = _seg_ones(DG, HD)
    tri128 = (i128[:, None] >= i128[None, :]).astype(F32)
    return dict(
        seg64=seg64, seg64x3_bf16=jnp.concatenate([seg64, seg64, seg64], axis=0).astype(BF16),
        dmask=(jnp.arange(HD)[:, None] == (jnp.arange(DG)[None, :] % HD)).astype(F32),
        tri_bd128=(same_chunk & (i128[:, None] >= i128[None, :])).astype(F32), ones_bd128=same_chunk.astype(F32),
        first_row=(i128[None, :] == (jnp.arange(HG_NC) * HGRN_CHUNK)[:, None]).astype(F32),
        causal_blk=jnp.broadcast_to((((pair // HGRN_CHUNK) % HGRN_CHUNK) >= (pair % HGRN_CHUNK)).astype(F32)[:, None],
                                    (HG_NC * HGRN_CHUNK * HGRN_CHUNK, DG)),
        e128=((i128[:, None] == (jnp.arange(DG)[None, :] // HD)) & (i128[:, None] < NH)).astype(F32),
        tri128=tri128, tri128t=tri128.T, seg128=_seg_ones(DG, 128), ones128=jnp.ones((128, 128), F32))


def _pad_lanes(v, n):
    return jnp.concatenate([v, jnp.zeros((n - v.shape[0],), v.dtype)])[None, :]


def _layer_params(l, raw, consts):
    p = dict(consts)
    row = lambda name: raw[name][l][None, :]
    z = lambda r: jnp.zeros((r, DG), F32)
    mu_vres = raw["mu_vres"][l - 1] if l > 0 else jnp.zeros((32,), F32)
    p["mu"] = jnp.concatenate([raw["mu_shift"][l], mu_vres, jnp.zeros((96,), F32)])[None, :]
    p["conv_w"], p["conv_b"] = raw["ssd_conv_w"][l], row("ssd_conv_b")
    p["w0"], p["a0"], p["k_k"], p["k_a"] = row("rwkv_w0"), row("rwkv_a0"), row("rwkv_k_k"), row("rwkv_k_a")
    p["lnx_w"], p["lnx_b"] = row("rwkv_lnx_w"), row("rwkv_lnx_b")
    p["r_k"] = raw["rwkv_r_k"][l].reshape(1, DG)
    p["w2p"] = jnp.concatenate([raw["rwkv_w2"][l], z(96)], axis=0)
    p["a2p"] = jnp.concatenate([z(32), raw["rwkv_a2"][l], z(64)], axis=0)
    p["g2p"] = jnp.concatenate([z(64), raw["rwkv_g2"][l]], axis=0)
    if l > 0:
        p["v0"] = raw["rwkv_v0"][l - 1][None, :]
        p["v2p"] = jnp.concatenate([raw["rwkv_v2"][l - 1], z(96)], axis=0)
    p["lb0"], p["lb1"] = raw["lower_bounds"][0:1], raw["lower_bounds"][1:2]
    p["hgrn_norm_w"], p["ssd_norm_w"] = row("hgrn_norm_w"), row("ssd_norm_w")
    p["dt_bias"], p["a_log"], p["ssd_d"] = (_pad_lanes(raw[n][l], 128) for n in ("ssd_dt_bias", "ssd_A_log", "ssd_D"))
    for n in ("ln1_w", "ln1_b", "ln2_w", "ln2_b"):
        p[n] = row(n)
    return p


def _natural_grads(g0, g1):
    gs = (g0, g1)
    st = lambda key, f=lambda a: a[0]: jnp.stack([f(g[key]) for g in gs])
    out = {}
    out["lower_bounds"] = jnp.concatenate([g0["lb0"] + g1["lb0"], g0["lb1"] + g1["lb1"]], axis=0)
    out["mu_shift"] = st("mu", lambda a: a[0, :896])
    out["mu_vres"] = g1["mu"][:, 896:928]
    out["rwkv_w0"], out["rwkv_a0"], out["rwkv_k_k"], out["rwkv_k_a"] = st("w0"), st("a0"), st("k_k"), st("k_a")
    out["rwkv_w2"] = st("w2p", lambda a: a[0:32])
    out["rwkv_a2"] = st("a2p", lambda a: a[32:64])
    out["rwkv_g2"] = st("g2p", lambda a: a[64:128])
    out["rwkv_r_k"] = st("r_k", lambda a: a.reshape(NH, HD))
    out["rwkv_lnx_w"], out["rwkv_lnx_b"] = st("lnx_w"), st("lnx_b")
    out["rwkv_v0"] = g1["v0"]
    out["rwkv_v2"] = g1["v2p"][None, 0:32]
    out["ssd_conv_w"] = st("conv_w", lambda a: a)
    out["ssd_conv_b"] = st("conv_b")
    out["ssd_dt_bias"], out["ssd_A_log"], out["ssd_D"] = (st(k, lambda a: a[0, :NH]) for k in ("dt_bias", "a_log", "ssd_d"))
    out["ssd_norm_w"], out["hgrn_norm_w"] = st("ssd_norm_w"), st("hgrn_norm_w")
    for n in ("ln1_w", "ln1_b", "ln2_w", "ln2_b"):
        out[n] = st(n)
    return out


MESH_T = pl.DeviceIdType.MESH
ANY = pl.BlockSpec(memory_space=pl.ANY)


def _dev_index(px, py, pc):
    return 4 * px + 2 * py + pc


def all_gather(arrs):
    n = len(arrs)

    def body(*refs):
        begin, middle, end = _gather_steps(refs[:n], refs[n:2 * n], *refs[2 * n:])
        begin()
        middle()
        end()

    return pl.pallas_call(body, in_specs=[ANY] * n, out_specs=[ANY] * n, out_shape=_gather_shapes(arrs),
                          scratch_shapes=_gather_sems(n), name="all_gather")(*arrs)


def _gather_shapes(arrs):
    return [_sds((N_DEV,) + a.shape, a.dtype) for a in arrs]


def _gather_sems(n):
    return [pltpu.SemaphoreType.DMA((n, 7)), pltpu.SemaphoreType.DMA((n, 7)), pltpu.SemaphoreType.DMA((n,))]


def _gather_steps(ins, outs, send_sems, recv_sems, local_sems):
    n = len(ins)
    x, y, c = lax.axis_index("x"), lax.axis_index("y"), lax.axis_index("c")
    me, sibling = (x, y, c), (x, y, 1 - c)
    chips = [(1 - x, y), (x, 1 - y), (1 - x, 1 - y)]

    def copy(a, k, block, to, src=None):
        slot = outs[a].at[_dev_index(*block)]
        return pltpu.make_async_remote_copy(src_ref=slot if src is None else src, dst_ref=slot,
                                            send_sem=send_sems.at[a, k], recv_sem=recv_sems.at[a, k],
                                            device_id=to, device_id_type=MESH_T)

    def own_copies():
        mine = [pltpu.make_async_copy(ins[a], outs[a].at[_dev_index(*me)], local_sems.at[a]) for a in range(n)]
        first = []
        for a in range(n):
            first.append(copy(a, 0, me, sibling, src=ins[a]))
            first += [copy(a, 1 + j, me, (*chip, c), src=ins[a]) for j, chip in enumerate(chips)]
        return mine, first

    def begin():
        mine, first = own_copies()
        for cp in mine + first:
            cp.start()

    def passed_on():
        return [copy(a, 4 + j, (*chip, c), sibling) for j, chip in enumerate(chips) for a in range(n)]

    def middle():
        for j, chip in enumerate(chips):
            for a in range(n):
                copy(a, 1 + j, (*chip, c), me).wait_recv()
        for cp in passed_on():
            cp.start()

    def end():
        mine, first = own_copies()
        for a in range(n):
            copy(a, 0, sibling, me).wait_recv()
            for j, chip in enumerate(chips):
                copy(a, 4 + j, (*chip, 1 - c), me).wait_recv()
        for cp in first + passed_on():
            cp.wait_send()
        for cp in mine:
            cp.wait()

    return begin, middle, end


def _chips(x, y):
    return [(x, y), (1 - x, y), (x, 1 - y), (1 - x, 1 - y)]


def _sibling_steps(ins, sib, send_sems, recv_sems):
    x, y, c = lax.axis_index("x"), lax.axis_index("y"), lax.axis_index("c")

    def copies():
        return [pltpu.make_async_remote_copy(src_ref=ins[a].at[_dev_index(cx, cy, 1 - c)], dst_ref=sib[a].at[k],
                                             send_sem=send_sems.at[a, k], recv_sem=recv_sems.at[a, k],
                                             device_id=(x, y, 1 - c), device_id_type=MESH_T)
                for a in range(len(ins)) for k, (cx, cy) in enumerate(_chips(x, y))]

    def begin():
        for cp in copies():
            cp.start()

    def end():
        cps = copies()
        for cp in cps:
            cp.wait_recv()
        for cp in cps:
            cp.wait_send()

    return begin, end


def exchange_siblings(arrs, name):
    n = len(arrs)

    def body(*refs):
        begin, end = _sibling_steps(refs[:n], refs[n:2 * n], *refs[2 * n:])
        begin()
        end()

    sem = pltpu.SemaphoreType.DMA((n, 4))
    return pl.pallas_call(body, in_specs=[ANY] * n, out_specs=[ANY] * n,
                          out_shape=[_sds((4,) + a.shape[1:], a.dtype) for a in arrs],
                          scratch_shapes=[sem, sem], name=name)(*arrs)


def reduce_pair(name, send, slots, sib, wire_dtype):
    _, r, c = send.shape
    rb = min(r, 262144 // c)

    def body(slots_ref, m0, m1, m2, m3, s_ref, own_ref, part_ref):
        own_ref[...] = m0[...] + s_ref[0]
        for k, m_ref in enumerate((m1, m2, m3)):
            part_ref[k] = (m_ref[...] + s_ref[k + 1]).astype(wire_dtype)

    mine = [pl.BlockSpec((None, rb, c), lambda i, s, k=k: (s[k], i, 0)) for k in range(4)]
    grid_spec = pltpu.PrefetchScalarGridSpec(
        num_scalar_prefetch=1, grid=(r // rb,),
        in_specs=mine + [pl.BlockSpec((4, rb, c), lambda i, s: (0, i, 0))],
        out_specs=[pl.BlockSpec((rb, c), lambda i, s: (i, 0)), pl.BlockSpec((3, rb, c), lambda i, s: (0, i, 0))])
    return pl.pallas_call(body, grid_spec=grid_spec, out_shape=[_sds((r, c)), _sds((3, r, c), wire_dtype)], name=name,
                          compiler_params=_cp(("parallel",)))(slots, send, send, send, send, sib)


def _chip_exchange_steps(ins, recv, send_sems, recv_sems):
    x, y, c = lax.axis_index("x"), lax.axis_index("y"), lax.axis_index("c")

    def copies():
        return [pltpu.make_async_remote_copy(src_ref=ins[a].at[k], dst_ref=recv[a].at[k], send_sem=send_sems.at[a, k],
                                             recv_sem=recv_sems.at[a, k], device_id=(cx, cy, c), device_id_type=MESH_T)
                for a in range(len(ins)) for k, (cx, cy) in enumerate(_chips(x, y)[1:])]

    def begin():
        for cp in copies():
            cp.start()

    def end():
        cps = copies()
        for cp in cps:
            cp.wait_recv()
        for cp in cps:
            cp.wait_send()

    return begin, end


def exchange_chips(parts, rep):
    n = len(parts)

    def body(*refs):
        ins, rep_ref = refs[:n], refs[n]
        recv, rep_all = refs[n + 1:2 * n + 1], refs[2 * n + 1]
        send_sems, recv_sems, rsend_sems, rrecv_sems, local_sem = refs[2 * n + 2:]
        x, y, c = lax.axis_index("x"), lax.axis_index("y"), lax.axis_index("c")
        me = _dev_index(x, y, c)
        mine = pltpu.make_async_copy(rep_ref, rep_all.at[me], local_sem)
        mine.start()
        begin, end = _chip_exchange_steps(ins, recv, send_sems, recv_sems)
        begin()
        rels = [(rx, ry, rc) for rx in (0, 1) for ry in (0, 1) for rc in (0, 1)][1:]
        peers = [(jnp.where(rx, 1 - x, x), jnp.where(ry, 1 - y, y), jnp.where(rc, 1 - c, c)) for rx, ry, rc in rels]
        rcps = []
        for k, peer in enumerate(peers):
            cp = pltpu.make_async_remote_copy(src_ref=rep_ref, dst_ref=rep_all.at[me], send_sem=rsend_sems.at[k],
                                              recv_sem=rrecv_sems.at[k], device_id=peer, device_id_type=MESH_T)
            cp.start()
            rcps.append(cp)
        for k, peer in enumerate(peers):
            pltpu.make_async_remote_copy(src_ref=rep_ref, dst_ref=rep_all.at[_dev_index(*peer)], send_sem=rsend_sems.at[k],
                                         recv_sem=rrecv_sems.at[k], device_id=peer, device_id_type=MESH_T).wait_recv()
        end()
        for cp in rcps:
            cp.wait_send()
        mine.wait()

    outs = pl.pallas_call(
        body, in_specs=[ANY] * (n + 1), out_specs=[ANY] * (n + 1),
        out_shape=[_sds(a.shape, a.dtype) for a in parts] + [_sds((N_DEV,) + rep.shape, rep.dtype)],
        scratch_shapes=[pltpu.SemaphoreType.DMA((n, 3)), pltpu.SemaphoreType.DMA((n, 3)), pltpu.SemaphoreType.DMA((7,)),
                        pltpu.SemaphoreType.DMA((7,)), pltpu.SemaphoreType.DMA],
        name="exchange_chips")(*parts, rep)
    return outs[:n], outs[n]


def adamw(name, terms, w, m, v, transposed=False):
    r, c = w.shape[::-1] if transposed else w.shape
    rb = r if transposed else min(r, 262144 // c)
    c1 = 1.0 - ADAM_B1 ** ADAM_STEP
    c2 = 1.0 - ADAM_B2 ** ADAM_STEP
    nt = len(terms)

    def body(*refs):
        w_ref, m_ref, v_ref = refs[nt:nt + 3]
        g_ref, d_ref, nm_ref, nv_ref = refs[nt + 3:]
        g = refs[0][...].astype(F32)
        for t_ref in refs[1:nt]:
            g = g + t_ref[...].astype(F32)
        if transposed:
            g = g.T
        nm = ADAM_B1 * m_ref[...] + (1.0 - ADAM_B1) * g
        nv = ADAM_B2 * v_ref[...] + (1.0 - ADAM_B2) * (g * g)
        g_ref[...] = g
        nm_ref[...] = nm
        nv_ref[...] = nv
        d_ref[...] = -ADAM_LR * ((nm / c1) / (jnp.sqrt(nv / c2) + ADAM_EPS) + ADAM_WD * w_ref[...])

    blk = pl.BlockSpec((rb, c), lambda i: (i, 0))
    wblk = pl.BlockSpec((c, r), lambda i: (0, 0)) if transposed else blk
    tspecs = [blk if k is None else pl.BlockSpec((None, rb, c), lambda i, k=k: (k, i, 0)) for _, k in terms]
    return pl.pallas_call(body, grid=(r // rb,), in_specs=tspecs + [wblk] * 3, out_specs=[wblk] * 4,
                          out_shape=[_sds(w.shape)] * 4, name=name,
                          compiler_params=_cp(("parallel",)))(*[t for t, _ in terms], w, m, v)


W_IN_PIECES = ((0, 768, 0), (768, 896, C_LORA), (896, 1664, 768), (1664, 1920, C_Z), (1920, 2688, C_XBC),
               (2688, 2692, C_DT), (2692, 3716, C_HQ))
VRES_W = 32


def adamw_w_in(name, terms, w, m, v, vres=None):
    nt, nv = len(terms), 3 if vres else 0
    c1 = 1.0 - ADAM_B1 ** ADAM_STEP
    c2 = 1.0 - ADAM_B2 ** ADAM_STEP

    def body(*refs):
        w_ref, m_ref, v_ref = refs[nt:nt + 3]
        vres_refs = refs[nt + 3:nt + 3 + nv]
        outs = refs[nt + 3 + nv:nt + 7 + nv]
        vres_outs = refs[nt + 7 + nv:]
        g_all = refs[0][...].astype(F32)
        for t_ref in refs[1:nt]:
            g_all = g_all + t_ref[...].astype(F32)

        def update(g, wmv, out_refs, cols):
            nm = ADAM_B1 * wmv[1][:, cols] + (1.0 - ADAM_B1) * g
            nv_ = ADAM_B2 * wmv[2][:, cols] + (1.0 - ADAM_B2) * (g * g)
            out_refs[0][:, cols] = g
            out_refs[1][:, cols] = -ADAM_LR * ((nm / c1) / (jnp.sqrt(nv_ / c2) + ADAM_EPS) + ADAM_WD * wmv[0][:, cols])
            out_refs[2][:, cols] = nm
            out_refs[3][:, cols] = nv_

        for lo, hi, src in W_IN_PIECES:
            update(g_all[:, src:src + hi - lo], (w_ref, m_ref, v_ref), outs, slice(lo, hi))
        if vres:
            update(g_all[:, C_VRES:C_VRES + VRES_W], vres_refs, vres_outs, slice(0, VRES_W))

    r, c = terms[0][0].shape[-2:]
    tspecs = [_full((r, c)) if k is None else pl.BlockSpec((None, r, c), lambda i, k=k: (k, 0, 0)) for _, k in terms]
    wspec, vspec = _full(w.shape), _full((w.shape[0], VRES_W))
    outs = pl.pallas_call(body, grid=(1,), in_specs=tspecs + [wspec] * 3 + [vspec] * nv,
                          out_specs=[wspec] * 4 + [vspec] * (4 if vres else 0),
                          out_shape=[_sds(w.shape)] * 4 + [_sds((w.shape[0], VRES_W))] * (4 if vres else 0), name=name,
                          compiler_params=_cp(("arbitrary",)))(*[t for t, _ in terms], w, m, v, *(vres or ()))
    return list(outs[:4]), list(outs[4:])


SMS_ROWS = 16
REP_ROWS = 24
N_BIG = 8
SMALL_SHARDED = (("rwkv_w2", (2, 32, 32)), ("rwkv_a2", (2, 32, 32)), ("rwkv_g2", (2, 64, 32)), ("rwkv_v2", (1, 32, 32)),
                 ("ssd_conv_w", (2, 4, 96)))
REPLICATED = (("lower_bounds", (2, 256)), ("mu_shift", (2, 896)), ("mu_vres", (1, 32)), ("rwkv_w0", (2, 256)),
              ("rwkv_a0", (2, 256)), ("rwkv_k_k", (2, 256)), ("rwkv_k_a", (2, 256)), ("rwkv_r_k", (2, 4, 64)),
              ("rwkv_lnx_w", (2, 256)), ("rwkv_lnx_b", (2, 256)), ("rwkv_v0", (1, 256)), ("ssd_conv_b", (2, 768)),
              ("ssd_dt_bias", (2, 4)), ("ssd_A_log", (2, 4)), ("ssd_D", (2, 4)), ("ssd_norm_w", (2, 256)),
              ("hgrn_norm_w", (2, 256)), ("ln1_w", (2, 1024)), ("ln1_b", (2, 1024)), ("ln2_w", (2, 1024)),
              ("ln2_b", (2, 1024)))


def _flat_rows(parts, rows):
    flat = jnp.concatenate([a.reshape(-1) for a in parts])
    return jnp.concatenate([flat, jnp.zeros((rows * PACK_W - flat.shape[0],), flat.dtype)]).reshape(rows, PACK_W)


def _local_arrays(d):
    arrs = [_w_in_pad(d["w_in"][0], None), _w_in_pad(d["w_in"][1], d["w_in_vres"][0]), d["w_out"][0], d["w_out"][1],
            d["w_up"][0], d["w_up"][1], d["w_down"][0], d["w_down"][1],
            _flat_rows([d[n] for n, _ in SMALL_SHARDED], SMS_ROWS)]
    return arrs, _flat_rows([d[n] for n, _ in REPLICATED], REP_ROWS)


def _unflat(rows2d, table):
    flat, out, o = rows2d.reshape(-1), {}, 0
    for name, shape in table:
        n = 1
        for s in shape:
            n *= s
        out[name] = flat[o:o + n].reshape(shape)
        o += n
    return out


def _from_local_arrays(arrs, rep, w_in_vres):
    d = {}
    d["w_in"], d["w_in_vres"] = jnp.stack([arrs[0], arrs[1]]), w_in_vres[None]
    d["w_out"] = jnp.stack([arrs[2], arrs[3]])
    d["w_up"] = jnp.stack([arrs[4], arrs[5]])
    d["w_down"] = jnp.stack([arrs[6], arrs[7]])
    d.update(_unflat(arrs[8], SMALL_SHARDED))
    d.update(_unflat(rep, REPLICATED))
    return d


def _small_sharded_full(gs):
    small, flat, o = {}, gs.reshape(N_DEV, -1), 0
    for name, shape in SMALL_SHARDED:
        n = shape[0] * shape[1] * shape[2]
        blk = flat[:, o:o + n].reshape((N_DEV,) + shape)
        small[name] = blk.transpose(1, 2, 0, 3).reshape(shape[0], shape[1], N_DEV * shape[2])
        o += n
    return small


def _owner_blocks(g):
    return g.reshape(N_DEV, g.shape[0] // N_DEV, g.shape[1])


def _small_send_arrays(small_grads):
    sms = []
    for name, shape in SMALL_SHARDED:
        g = small_grads[name].reshape(shape[0], shape[1], N_DEV, shape[2]).transpose(2, 0, 1, 3)
        sms.append(g.reshape(N_DEV, -1))
    sms = jnp.concatenate(sms, axis=1)
    sms = jnp.concatenate([sms, jnp.zeros((N_DEV, SMS_ROWS * PACK_W - sms.shape[1]), F32)], axis=1)
    return sms.reshape(N_DEV, SMS_ROWS, PACK_W), _flat_rows([small_grads[n] for n, _ in REPLICATED], REP_ROWS)


BIG_KEYS = ("w_in", "w_out", "w_up_t", "w_down")


def _local_step(x, tgt, wts, raw, gather=(), pair_sums=None, reducer=None):
    consts = _consts()
    ps = [_layer_params(l, raw, consts) for l in range(DEPTH)]
    x1, sv0 = layer_fwd(0, x, None, wts[0], ps[0], gather[:4], late=bool(gather))
    wts1 = {"w_in": _full_rows(sv0["gathered"][3])} if gather else wts[1]
    x2, sv1 = layer_fwd(1, x1, sv0["fl"], wts1, ps[1], gather[4:], late=bool(gather))
    dy, lparts = loss_call(x2, tgt)
    loss = jnp.sum(lparts[::8, 0])
    dx1, dvfirst, g1 = layer_bwd(1, dy, None, sv1, sv1["wts"], ps[1], (), reducer)
    big1 = {k: g1[k] for k in BIG_KEYS}
    if reducer is None:
        dx0, _, g0 = layer_bwd(0, dx1, dvfirst, sv0, sv0["wts"], ps[0])
        early = None
    else:
        own_in1, parts_in1 = pair_sums("1b", {"w_in": g1["w_in"]})
        dx0, _, g0 = layer_bwd(0, dx1, dvfirst, sv0, sv0["wts"], ps[0], parts_in1, reducer)
        own, recv = {(1, "w_in"): own_in1[0]}, {(1, "w_in"): g0["exchanged"][0]}
        for l, g, first in ((1, g1, 0), (0, g0, 1)):
            for i, k in enumerate(LATE_KEYS):
                own[(l, k)], recv[(l, k)] = g["early_own"][i], g["exchanged"][first + i]
        early = (own, recv)
    big = [{k: g0[k] for k in BIG_KEYS}, big1]
    return loss, dx0, big, _natural_grads(g0, g1), early


WEIGHT_NAMES = ("lower_bounds", "w_in", "w_in_vres", "mu_shift", "mu_vres", "rwkv_w0", "rwkv_w2", "rwkv_a0", "rwkv_a2",
                "rwkv_g2", "rwkv_k_k", "rwkv_k_a", "rwkv_r_k", "rwkv_lnx_w", "rwkv_lnx_b", "rwkv_v0", "rwkv_v2",
                "ssd_conv_w", "ssd_conv_b", "ssd_dt_bias", "ssd_A_log", "ssd_D", "ssd_norm_w", "hgrn_norm_w", "w_out",
                "ln1_w", "ln1_b", "w_up", "w_down", "ln2_w", "ln2_b")


def kernel(x, lower_bounds, w_in, w_in_vres, mu_shift, mu_vres, rwkv_w0, rwkv_w2, rwkv_a0, rwkv_a2, rwkv_g2, rwkv_k_k, rwkv_k_a, rwkv_r_k, rwkv_lnx_w, rwkv_lnx_b, rwkv_v0, rwkv_v2, ssd_conv_w, ssd_conv_b, ssd_dt_bias, ssd_A_log, ssd_D, ssd_norm_w, hgrn_norm_w, w_out, ln1_w, ln1_b, w_up, w_down, ln2_w, ln2_b, loss_target, m_lower_bounds, m_w_in, m_w_in_vres, m_mu_shift, m_mu_vres, m_rwkv_w0, m_rwkv_w2, m_rwkv_a0, m_rwkv_a2, m_rwkv_g2, m_rwkv_k_k, m_rwkv_k_a, m_rwkv_r_k, m_rwkv_lnx_w, m_rwkv_lnx_b, m_rwkv_v0, m_rwkv_v2, m_ssd_conv_w, m_ssd_conv_b, m_ssd_dt_bias, m_ssd_A_log, m_ssd_D, m_ssd_norm_w, m_hgrn_norm_w, m_w_out, m_ln1_w, m_ln1_b, m_w_up, m_w_down, m_ln2_w, m_ln2_b, v_lower_bounds, v_w_in, v_w_in_vres, v_mu_shift, v_mu_vres, v_rwkv_w0, v_rwkv_w2, v_rwkv_a0, v_rwkv_a2, v_rwkv_g2, v_rwkv_k_k, v_rwkv_k_a, v_rwkv_r_k, v_rwkv_lnx_w, v_rwkv_lnx_b, v_rwkv_v0, v_rwkv_v2, v_ssd_conv_w, v_ssd_conv_b, v_ssd_dt_bias, v_ssd_A_log, v_ssd_D, v_ssd_norm_w, v_hgrn_norm_w, v_w_out, v_ln1_w, v_ln1_b, v_w_up, v_w_down, v_ln2_w, v_ln2_b):
    given = dict(locals())
    w = {n: given[n] for n in WEIGHT_NAMES}
    w_arrs, w_rep = _local_arrays(w)
    m_arrs, m_rep = _local_arrays({n: given["m_" + n] for n in WEIGHT_NAMES})
    v_arrs, v_rep = _local_arrays({n: given["v_" + n] for n in WEIGHT_NAMES})
    wire = lambda a: (w_arrs[a].T if a in (4, 5) else w_arrs[a]).astype(BF16)
    gathered0 = all_gather([wire(0), w_arrs[N_BIG]])
    raw = {n: w[n] for n, _ in REPLICATED}
    raw.update(_small_sharded_full(gathered0[1]))
    mx, my, mc = lax.axis_index("x"), lax.axis_index("y"), lax.axis_index("c")
    slots = jnp.stack([_dev_index(cx, cy, mc) for cx, cy in _chips(mx, my)]).astype(jnp.int32)

    def reducer(tag, send, sib, n_f32=0):
        wire_dt = [BF16] * (len(send) - n_f32) + [F32] * n_f32
        res = [reduce_pair(f"reduce_pair{tag}_{i}", s, slots, sb, dt) for i, (s, sb, dt) in enumerate(zip(send, sib, wire_dt))]
        return [o for o, _ in res], [pt for _, pt in res]

    def pair_sums(tag, grads, extra=()):
        send = [_owner_blocks(g) for g in grads.values()] + list(extra)
        return reducer(tag, send, exchange_siblings(send, f"exchange_siblings{tag}"), len(extra))

    behind_scan = [wire(a) for a in (2, 4, 6, 1, 3, 5, 7)]
    loss, dx, big, small_grads, (own_by, recv_by) = _local_step(
        x[0], loss_target[0], [{"w_in": _full_rows(gathered0[0])}, None], raw, behind_scan, pair_sums, reducer)
    sms_send, rep = _small_send_arrays(small_grads)
    own0b, parts0b = pair_sums("0b", {"w_in": big[0]["w_in"]}, [sms_send])
    recv0b, rep_all = exchange_chips(parts0b, rep)
    own, recv = [None] * (N_BIG + 1), [None] * (N_BIG + 1)
    for (l, k), o in own_by.items():
        a = 2 * BIG_KEYS.index(k) + l
        own[a], recv[a] = o, recv_by[(l, k)]
    for a, o, r in zip((0, N_BIG), own0b, recv0b):
        own[a], recv[a] = o, r
    terms = lambda a: [(own[a], None), (recv[a], 0), (recv[a], 1), (recv[a], 2)]
    moments = [{n: given[pre + n] for n in ("w_in", "w_in_vres")} for pre in ("", "m_", "v_")]
    in0, _ = adamw_w_in("adamw0", terms(0), *[d["w_in"][0] for d in moments])
    in1, vres = adamw_w_in("adamw1", terms(1), *[d["w_in"][1] for d in moments], vres=[d["w_in_vres"][0] for d in moments])
    results = [in0, in1] + [adamw(f"adamw{a}", terms(a), w_arrs[a], m_arrs[a], v_arrs[a], transposed=a in (4, 5))
                            for a in range(2, N_BIG + 1)]
    rep_res = adamw("adamw_rep", [(rep_all, q) for q in range(N_DEV)], w_rep, m_rep, v_rep)
    loss = lax.psum(loss, ("x", "y", "c"))
    outs = [loss, dx[None]]
    for q in range(4):
        d = _from_local_arrays([res[q] for res in results], rep_res[q], vres[q])
        outs += [d[n] for n in WEIGHT_NAMES]
    return tuple(outs)
```

```python
import functools

import jax
import jax.numpy as jnp
from jax import lax
from jax.experimental import pallas as pl
from jax.experimental.pallas import tpu as pltpu

F32 = jnp.float32
BF16 = jnp.bfloat16
HI = lax.Precision.HIGHEST

N_DEV = 8
SEQ = 2048
D_MODEL = 1024
D_FF = 4096
DG = 256
NH = 4
HD = 64
DEPTH = 2
ALPHA = (2.0 * DEPTH) ** 0.25
LN_EPS = 1e-5
RMS_EPS = 1e-5
GN_EPS = HD * 1e-5
SSD_N = 128
SSD_CHUNK = 128
HGRN_CHUNK = 16
DILATED = ((128, 1), (512, 4), (2048, 16))

ADAM_LR, ADAM_B1, ADAM_B2, ADAM_EPS, ADAM_WD, ADAM_STEP = 0.001, 0.9, 0.999, 1e-08, 0.01, 10

PW = 4096
C_R, C_K, C_V = 0, 256, 512
C_AQ, C_AK, C_AV = 768, 1024, 1280
C_Z, C_XBC = 1536, 1792
C_HQ, C_HF, C_HI, C_HG = 2560, 2816, 3072, 3328
C_LORA, C_DT, C_VRES = 3584, 3712, 3840

RB = 256
VMEM_LIMIT = 56 * 1024 * 1024
PACK_W = 1024


def _cp(sem=None):
    return pltpu.CompilerParams(dimension_semantics=sem, vmem_limit_bytes=VMEM_LIMIT)


def _sds(shape, dt=F32):
    return jax.ShapeDtypeStruct(tuple(shape), dt)


def _rows(w, cb=0, rb=RB):
    return pl.BlockSpec((rb, w), lambda i: (i, cb))


def _full(shape):
    n = len(shape)
    return pl.BlockSpec(tuple(shape), lambda *_: (0,) * n)


def _sigmoid(x):
    return 1.0 / (1.0 + jnp.exp(-x))


def _silu(x):
    return x * _sigmoid(x)


def _softplus(x):
    return jnp.maximum(x, 0.0) + jnp.log(1.0 + jnp.exp(jnp.where(x > 0, -x, x)))


MID = lax.Precision.HIGH
NN, TN, NT = (((1,), (0,)), ((), ())), (((0,), (0,)), ((), ())), (((1,), (1,)), ((), ()))


def _dot(a, b):
    return lax.dot_general(a, b, NN, precision=MID, preferred_element_type=F32)


def _dot_tn(a, b):
    return lax.dot_general(a, b, TN, precision=MID, preferred_element_type=F32)


def _dot_nt(a, b):
    return lax.dot_general(a, b, NT, precision=MID, preferred_element_type=F32)


def _dotx(a, b):
    return lax.dot_general(a, b, NN, precision=HI, preferred_element_type=F32)


def _dotx_tn(a, b):
    return lax.dot_general(a, b, TN, precision=HI, preferred_element_type=F32)


def _seg_ones(n, seg):
    i = jnp.arange(n)
    return (i[:, None] // seg == i[None, :] // seg).astype(F32)


def _shift_down(x, s):
    row = lax.broadcasted_iota(jnp.int32, x.shape, 0)
    return jnp.where(row < s, 0.0, pltpu.roll(x, s, 0))


def _shift_up(x, s):
    n = x.shape[0]
    row = lax.broadcasted_iota(jnp.int32, x.shape, 0)
    return jnp.where(row >= n - s, 0.0, pltpu.roll(x, n - s, 0))


@functools.partial(jax.custom_vjp, nondiff_argnums=(1,))
def _tshift(x, s):
    return _shift_down(x, s)


def _tshift_fwd(x, s):
    return _shift_down(x, s), None


def _tshift_bwd(s, _, g):
    return (_shift_up(g, s),)


_tshift.defvjp(_tshift_fwd, _tshift_bwd)


def _map_fwd(name, fn, grid, ins, in_specs, out_shapes, out_specs):
    n_in = len(ins)

    def body(*refs):
        ys = fn(*[r[...] for r in refs[:n_in]])
        for r, y in zip(refs[n_in:], ys):
            r[...] = y

    return pl.pallas_call(body, grid=grid, in_specs=in_specs, out_specs=out_specs, out_shape=out_shapes,
                          name=name, compiler_params=_cp(("parallel",)))(*ins)


def _map_bwd(name, fn, grid, ins, in_specs, cts, ct_specs, want, acc=(), gout=None):
    n_in = len(ins)
    flat_cts = [c for group in cts for c in group]
    flat_specs = [s for group in ct_specs for s in group]
    n_ct = len(flat_cts)
    gout = gout or {}
    out_shapes = [gout[i][0] if i in gout else _sds(ins[i].shape) for i in want]
    out_specs = [gout[i][1] if i in gout else in_specs[i] for i in want]

    def body(*refs):
        xs = [r[...] for r in refs[:n_in]]
        cvals = [r[...] for r in refs[n_in:n_in + n_ct]]
        gouts = refs[n_in + n_ct:]
        cs, p = [], 0
        for group in cts:
            v = cvals[p]
            for q in range(1, len(group)):
                v = v + cvals[p + q]
            cs.append(v)
            p += len(group)

        def f(*wanted):
            full = list(xs)
            for i, w in zip(want, wanted):
                full[i] = w
            return tuple(fn(*full))

        _, vjp = jax.vjp(f, *[xs[i] for i in want])
        gs = vjp(tuple(cs))
        for o, i, g in zip(gouts, want, gs):
            if i in acc:
                @pl.when(pl.program_id(0) == 0)
                def _():
                    o[...] = jnp.zeros_like(o)

                o[...] += g
            else:
                o[...] = g

    sem = ("arbitrary",) if acc else ("parallel",)
    return pl.pallas_call(body, grid=grid, in_specs=list(in_specs) + flat_specs, out_specs=out_specs,
                          out_shape=out_shapes, name=name, compiler_params=_cp(sem))(*ins, *flat_cts)


def _addn(name, *arrs):
    n, c = arrs[0].shape

    def fn(*xs):
        r = xs[0]
        for x in xs[1:]:
            r = r + x
        return (r,)

    return _map_fwd(name, fn, (n // RB,), list(arrs), [_rows(c)] * len(arrs), [_sds((n, c))], [_rows(c)])[0]


MM_TILES = {"k1024": (2048, 512, 1024), "k4096": (1024, 1024, 1024), "wgrad_tall": (2048, 1024, 512),
            "wgrad_wide": (1024, 2048, 512)}


def _mm(name, a, b, mode, tm, tn, tk, add=None, add_scale=1.0, epilogue=None, out_dtype=F32):
    if mode == "nn":
        (m, k), n = a.shape, b.shape[1]
    elif mode == "nt":
        (m, k), n = a.shape, b.shape[0]
    else:
        (k, m), n = a.shape, b.shape[1]
    nk = k // tk
    dn = {"nn": (((1,), (0,)), ((), ())), "nt": (((1,), (1,)), ((), ())), "tn": (((0,), (0,)), ((), ()))}[mode]

    def body(*refs):
        a_ref, b_ref = refs[:2]
        add_ref = refs[2] if add is not None else None
        o_ref = refs[3] if add is not None else refs[2]
        prod = lax.dot_general(a_ref[...].astype(BF16), b_ref[...].astype(BF16), dn, preferred_element_type=F32)

        def finish(r):
            if epilogue == "relu2":
                r = jnp.maximum(r, 0.0)
                r = r * r
            elif epilogue == "relu2_bwd":
                r = r * (2.0 * jnp.sqrt(add_ref[...]))
            elif add is not None:
                r = r + add_scale * add_ref[...]
            o_ref[...] = r.astype(out_dtype)

        if nk == 1:
            finish(prod)
        else:
            acc = refs[-1]
            kk = pl.program_id(2)

            @pl.when(kk == 0)
            def _():
                acc[...] = prod

            @pl.when(kk > 0)
            def _():
                acc[...] += prod

            @pl.when(kk == nk - 1)
            def _():
                finish(acc[...])

    a_spec = pl.BlockSpec((tk, tm), lambda i, j, q: (q, i)) if mode == "tn" else pl.BlockSpec((tm, tk), lambda i, j, q: (i, q))
    b_spec = pl.BlockSpec((tn, tk), lambda i, j, q: (j, q)) if mode == "nt" else pl.BlockSpec((tk, tn), lambda i, j, q: (q, j))
    o_spec = pl.BlockSpec((tm, tn), lambda i, j, q: (i, j))
    ins, specs = [a, b], [a_spec, b_spec]
    if add is not None:
        ins.append(add)
        specs.append(o_spec)
    return pl.pallas_call(body, grid=(m // tm, n // tn, nk), in_specs=specs, out_specs=o_spec,
                          out_shape=_sds((m, n), out_dtype),
                          scratch_shapes=[pltpu.VMEM((tm, tn), F32)] if nk > 1 else [], name=name,
                          compiler_params=_cp(("parallel", "parallel", "arbitrary")))(*ins)


def _lerp_colmap(j):
    r = jnp.where(j < 6, j, jnp.where(j == 6, C_LORA // 128, C_VRES // 128))
    return (0, r)


def _lerp_fn(f, mu):
    return (f + (_tshift(f, 1) - f) * mu,)


def _lerp_specs():
    return [pl.BlockSpec((SEQ, 128), _lerp_colmap), pl.BlockSpec((1, 128), lambda j: (0, j))]


def lerp_fwd(l, proj, mu):
    return _map_fwd(f"lerp_fwd{l}", _lerp_fn, (8,), [proj, mu], _lerp_specs(), [_sds((SEQ, 1024))],
                    [pl.BlockSpec((SEQ, 128), lambda j: (0, j))])[0]


def lerp_bwd(l, proj, mu, dfl):
    n_in = 2

    def body(f_ref, mu_ref, g_ref, df_ref, dmu_ref):
        _, vjp = jax.vjp(_lerp_fn, f_ref[...], mu_ref[...])
        df, dmu = vjp((g_ref[...],))
        df_ref[...] = df
        dmu_ref[...] = dmu

    cspec = pl.BlockSpec((SEQ, 128), lambda j: (0, j))
    return pl.pallas_call(body, grid=(8,), in_specs=_lerp_specs() + [cspec],
                          out_specs=[cspec, pl.BlockSpec((1, 128), lambda j: (0, j))],
                          out_shape=[_sds((SEQ, 1024)), _sds((1, 1024))], name=f"lerp_bwd{l}",
                          compiler_params=_cp(("parallel",)))(proj, mu, dfl)


def _conv_fn(x, w, b):
    y = x * w[3:4, :] + _tshift(x, 1) * w[2:3, :] + _tshift(x, 2) * w[1:2, :] + _tshift(x, 3) * w[0:1, :] + b
    return (_silu(y),)


def _conv_specs():
    return [pl.BlockSpec((SEQ, 128), lambda j: (0, C_XBC // 128 + j)), pl.BlockSpec((4, 128), lambda j: (0, j)),
            pl.BlockSpec((1, 128), lambda j: (0, j))]


def conv_fwd(l, proj, w, b):
    return _map_fwd(f"conv_fwd{l}", _conv_fn, (6,), [proj, w, b], _conv_specs(), [_sds((SEQ, 768))],
                    [pl.BlockSpec((SEQ, 128), lambda j: (0, j))])[0]


def conv_bwd(l, proj, w, b, dxc):
    def body(x_ref, w_ref, b_ref, g_ref, dx_ref, dw_ref, db_ref):
        _, vjp = jax.vjp(_conv_fn, x_ref[...], w_ref[...], b_ref[...])
        dx, dw, db = vjp((g_ref[...],))
        dx_ref[...] = dx
        dw_ref[...] = dw
        db_ref[...] = db

    cspec = pl.BlockSpec((SEQ, 128), lambda j: (0, j))
    return pl.pallas_call(body, grid=(6,), in_specs=_conv_specs() + [cspec],
                          out_specs=[cspec, pl.BlockSpec((4, 128), lambda j: (0, j)), pl.BlockSpec((1, 128), lambda j: (0, j))],
                          out_shape=[_sds((SEQ, 768)), _sds((4, 768)), _sds((1, 768))], name=f"conv_bwd{l}",
                          compiler_params=_cp(("parallel",)))(proj, w, b, dxc)


def _rwkv_pre_fn(has_vres):
    def fn(fk, fv, flora, *rest):
        if has_vres:
            fvres, vfirst, w0, w2p, a0, a2p, g2p, k_k, k_a, v0, v2p, seg = rest
        else:
            w0, w2p, a0, a2p, g2p, k_k, k_a, seg = rest
        w_log = -_softplus(-(w0 + _dot(jnp.tanh(flora), w2p))) - 0.5
        w = jnp.exp(-jnp.exp(w_log))
        a = _sigmoid(a0 + _dot(flora, a2p))
        g = _dot(_sigmoid(flora), g2p)
        if has_vres:
            v2 = fv + (vfirst - fv) * _sigmoid(v0 + _dot(fvres, v2p))
        else:
            v2 = fv * 1.0
        kk = fk * k_k
        kk = kk / jnp.maximum(jnp.sqrt(_dot(kk * kk, seg)), 1e-12)
        k2 = fk * (1.0 + (a - 1.0) * k_a)
        return w, k2, v2, -kk, kk * a, g

    return fn


def _rwkv_pre_args(fl, vfirst, p, has_vres):
    ins = [fl, fl, fl]
    specs = [_rows(256, 1), _rows(256, 2), _rows(128, 6)]
    if has_vres:
        ins += [fl, vfirst]
        specs += [_rows(128, 7), _rows(256, 2)]
    names = ["w0", "w2p", "a0", "a2p", "g2p", "k_k", "k_a"] + (["v0", "v2p"] if has_vres else []) + ["seg64"]
    for nme in names:
        ins.append(p[nme])
        specs.append(_full(p[nme].shape))
    return ins, specs, names


def rwkv_pre_fwd(l, fl, vfirst, p):
    has_vres = l > 0
    ins, specs, _ = _rwkv_pre_args(fl, vfirst, p, has_vres)
    return _map_fwd(f"rwkv_pre_fwd{l}", _rwkv_pre_fn(has_vres), (SEQ // RB,), ins, specs,
                    [_sds((SEQ, DG))] * 6, [_rows(DG)] * 6)


def rwkv_pre_bwd(l, fl, vfirst, p, cts):
    has_vres = l > 0
    ins, specs, names = _rwkv_pre_args(fl, vfirst, p, has_vres)
    n_row = 5 if has_vres else 3
    want = list(range(n_row)) + [n_row + i for i, nme in enumerate(names) if nme != "seg64"]
    acc = tuple(w for w in want if w >= n_row)
    ct_specs = [[_rows(DG)] * len(g) for g in cts]
    gout = {0: (_sds((SEQ, DG)), _rows(DG)), 1: (_sds((SEQ, DG)), _rows(DG)), 2: (_sds((SEQ, 128)), _rows(128))}
    if has_vres:
        gout[3] = (_sds((SEQ, 128)), _rows(128))
        gout[4] = (_sds((SEQ, DG)), _rows(DG))
    gs = _map_bwd(f"rwkv_pre_bwd{l}", _rwkv_pre_fn(has_vres), (SEQ // RB,), ins, specs, cts, ct_specs, want, acc, gout)
    keys = ["fk", "fv", "flora"] + (["fvres", "vfirst"] if has_vres else []) + [nme for nme in names if nme != "seg64"]
    return dict(zip(keys, gs))


def _rwkv_post_fn(y, fr, k2, v2, g, lnx_w, lnx_b, r_k, seg):
    mu = _dot(y, seg) * (1.0 / HD)
    d = y - mu
    var = _dot(d * d, seg) * (1.0 / HD)
    yn = d * lax.rsqrt(var + GN_EPS) * lnx_w + lnx_b
    bonus = _dot(fr * k2 * r_k, seg) * v2
    return ((yn + bonus) * g,)


def _rwkv_post_args(y, fl, k2, v2, g, p):
    ins = [y, fl, k2, v2, g, p["lnx_w"], p["lnx_b"], p["r_k"], p["seg64"]]
    specs = [_rows(DG), _rows(DG, 0), _rows(DG), _rows(DG), _rows(DG)] + [_full(x.shape) for x in ins[5:]]
    return ins, specs


def rwkv_post_fwd(l, y, fl, k2, v2, g, p):
    ins, specs = _rwkv_post_args(y, fl, k2, v2, g, p)
    return _map_fwd(f"rwkv_post_fwd{l}", _rwkv_post_fn, (SEQ // RB,), ins, specs, [_sds((SEQ, DG))], [_rows(DG)])[0]


def rwkv_post_bwd(l, y, fl, k2, v2, g, p, dya):
    ins, specs = _rwkv_post_args(y, fl, k2, v2, g, p)
    gs = _map_bwd(f"rwkv_post_bwd{l}", _rwkv_post_fn, (SEQ // RB,), ins, specs, [[dya]], [[_rows(DG, 0)]],
                  want=[0, 1, 2, 3, 4, 5, 6, 7], acc=(5, 6, 7), gout={1: (_sds((SEQ, DG)), _rows(DG))})
    return dict(zip(["y", "fr", "k2", "v2", "g", "lnx_w", "lnx_b", "r_k"], gs))


SCAN_TB = 128


def _coltile8(rows8, dmask, ones_stack, parts):
    pieces, rest = [], rows8
    for q in range(parts):
        piece = rest.astype(BF16).astype(F32)
        if q < parts - 1:
            rest = rest - piece
        pieces.append((piece[:, None, :] * dmask[None]).reshape(8 * HD, DG).astype(BF16))
    x = pieces[0] if parts == 1 else jnp.concatenate(pieces, axis=1)
    return jnp.dot(x, ones_stack, preferred_element_type=F32).reshape(8, HD, DG)


def _coltiles_bf16(rows_list, dmask, ones_bf16):
    x = jnp.concatenate([(r8[:, None, :] * dmask[None]).reshape(8 * HD, DG).astype(BF16) for r8 in rows_list], axis=0)
    t = jnp.dot(x, ones_bf16, preferred_element_type=F32)
    return [t[q * 8 * HD:(q + 1) * 8 * HD].reshape(8, HD, DG) for q in range(len(rows_list))]


def _segrows8(x8, dmask, ones_bf16):
    t = jnp.dot(x8.reshape(8 * HD, DG).astype(BF16), ones_bf16, preferred_element_type=F32).reshape(8, HD, DG)
    return jnp.sum(t * dmask[None], axis=1)


def rwkv_scan_fwd(l, fl, w, k2, v2, c, b, p, gather=()):
    nblk = SEQ // SCAN_TB
    ng = len(gather)

    def body(*refs):
        r_ref, w_ref, k_ref, v_ref, c_ref, b_ref, ones_ref, dm_ref = refs[:8]
        y_ref, st_ref = refs[8 + ng:10 + ng]
        s_sc = refs[10 + 2 * ng]
        if ng:
            begin, middle, end = _gather_steps(refs[8:8 + ng], refs[10 + ng:10 + 2 * ng], *refs[11 + 2 * ng:])

            @pl.when(pl.program_id(0) == 0)
            def _():
                begin()

            @pl.when(pl.program_id(0) == (3 * nblk) // 4)
            def _():
                middle()

        @pl.when(pl.program_id(0) == 0)
        def _():
            s_sc[...] = jnp.zeros_like(s_sc)

        ones3, ones = ones_ref[...], ones_ref[0:DG, :]
        dmask = dm_ref[...]

        def group(gi, carry):
            t0 = pl.multiple_of(gi * 8, 8)
            sl = pl.ds(t0, 8)
            v8 = v_ref[sl, :]
            wt = _coltile8(w_ref[sl, :], dmask, ones3, 3)
            ct, bt, kt, rt = _coltiles_bf16([c_ref[sl, :], b_ref[sl, :], k_ref[sl, :], r_ref[sl, :]], dmask, ones)
            t = s_sc[...]
            for j in range(8):
                sa = jnp.sum(t * ct[j], axis=0, keepdims=True)
                t = t * wt[j] + bt[j] * sa + kt[j] * v8[j:j + 1, :]
                st_ref[t0 + j] = t
            s_sc[...] = t
            y_ref[sl, :] = jnp.sum(st_ref[sl] * rt, axis=1)
            return carry

        lax.fori_loop(0, SCAN_TB // 8, group, 0)

        if ng:
            @pl.when(pl.program_id(0) == nblk - 1)
            def _():
                end()

    row = pl.BlockSpec((SCAN_TB, DG), lambda i: (i, 0))
    ins = [fl, w, k2, v2, c, b, p["seg64x3_bf16"], p["dmask"]] + list(gather)
    specs = [row] * 6 + [_full((3 * DG, DG)), _full((HD, DG))] + [ANY] * ng
    outs = pl.pallas_call(body, grid=(nblk,), in_specs=specs,
                          out_specs=[row, pl.BlockSpec((SCAN_TB, HD, DG), lambda i: (i, 0, 0))] + [ANY] * ng,
                          out_shape=[_sds((SEQ, DG)), _sds((SEQ, HD, DG))] + _gather_shapes(gather),
                          scratch_shapes=[pltpu.VMEM((HD, DG), F32)] + (_gather_sems(ng) if ng else []),
                          name=f"rwkv_scan_fwd{l}", compiler_params=_cp(("arbitrary",)))(*ins)
    return outs[0], outs[1], list(outs[2:])


def rwkv_scan_bwd(l, fl, w, k2, v2, c, b, states, dy, p, exchange=()):
    nblk = SEQ // SCAN_TB
    nx = len(exchange)

    def body(*refs):
        r_ref, w_ref, k_ref, v_ref, c_ref, b_ref, dy_ref, st_ref, sp_ref, ones_ref, dm_ref = refs[:11]
        dr_ref, dw_ref, dk_ref, dv_ref, dc_ref, db_ref = refs[11 + nx:17 + nx]
        g_sc, prev_sc, d8_sc, dsa_sc = refs[17 + 2 * nx:21 + 2 * nx]
        i = pl.program_id(0)
        if nx:
            begin, end = _chip_exchange_steps(refs[11:11 + nx], refs[17 + nx:17 + 2 * nx], *refs[21 + 2 * nx:])

            @pl.when(i == 0)
            def _():
                begin()

        @pl.when(i == 0)
        def _():
            g_sc[...] = jnp.zeros_like(g_sc)

        ones3, ones = ones_ref[...], ones_ref[0:DG, :]
        dmask = dm_ref[...]
        first_block = i == nblk - 1

        def group(gr, carry):
            gi = SCAN_TB // 8 - 1 - gr
            t0 = pl.multiple_of(gi * 8, 8)
            sl = pl.ds(t0, 8)
            v8, dy8 = v_ref[sl, :], dy_ref[sl, :]
            t8 = st_ref[sl]
            @pl.when(gi > 0)
            def _():
                prev_sc[0] = st_ref[t0 - 1]

            @pl.when(gi == 0)
            def _():
                prev_sc[0] = jnp.where(first_block, 0.0, sp_ref[0])

            for j in range(1, 8):
                prev_sc[j] = t8[j - 1]
            tp8 = prev_sc[...]
            wt = _coltile8(w_ref[sl, :], dmask, ones3, 3)
            ct, bt, kt, rt = _coltiles_bf16([c_ref[sl, :], b_ref[sl, :], k_ref[sl, :], r_ref[sl, :]], dmask, ones)
            sa8 = jnp.sum(tp8 * ct, axis=1)
            g = g_sc[...]
            for j in range(7, -1, -1):
                g = g + rt[j] * dy8[j:j + 1, :]
                d8_sc[j] = g
                dsa = jnp.sum(g * bt[j], axis=0, keepdims=True)
                dsa_sc[j:j + 1, :] = dsa
                g = g * wt[j] + ct[j] * dsa
            g_sc[...] = g
            d8 = d8_sc[...]
            dsa8 = dsa_sc[...]
            dv_ref[sl, :] = jnp.sum(d8 * kt, axis=1)
            dr_ref[sl, :] = _segrows8(t8 * dy8[:, None, :], dmask, ones)
            dk_ref[sl, :] = _segrows8(d8 * v8[:, None, :], dmask, ones)
            dw_ref[sl, :] = _segrows8(tp8 * d8, dmask, ones)
            db_ref[sl, :] = _segrows8(d8 * sa8[:, None, :], dmask, ones)
            dc_ref[sl, :] = _segrows8(tp8 * dsa8[:, None, :], dmask, ones)
            return carry

        lax.fori_loop(0, SCAN_TB // 8, group, 0)

        if nx:
            @pl.when(i == nblk - 1)
            def _():
                end()

    row = pl.BlockSpec((SCAN_TB, DG), lambda i: (nblk - 1 - i, 0))
    st_spec = pl.BlockSpec((SCAN_TB, HD, DG), lambda i: (nblk - 1 - i, 0, 0))
    sp_spec = pl.BlockSpec((1, HD, DG), lambda i: (jnp.maximum((nblk - 1 - i) * SCAN_TB - 1, 0), 0, 0))
    ins = [fl, w, k2, v2, c, b, dy, states, states, p["seg64x3_bf16"], p["dmask"]] + list(exchange)
    specs = [row] * 7 + [st_spec, sp_spec, _full((3 * DG, DG)), _full((HD, DG))] + [ANY] * nx
    tile8 = pltpu.VMEM((8, HD, DG), F32)
    sems = [pltpu.SemaphoreType.DMA((nx, 3)), pltpu.SemaphoreType.DMA((nx, 3))] if nx else []
    outs = pl.pallas_call(body, grid=(nblk,), in_specs=specs, out_specs=[row] * 6 + [ANY] * nx,
                          out_shape=[_sds((SEQ, DG))] * 6 + [_sds(a.shape, a.dtype) for a in exchange],
                          scratch_shapes=[pltpu.VMEM((HD, DG), F32), tile8, tile8, pltpu.VMEM((8, DG), F32)] + sems,
                          name=f"rwkv_scan_bwd{l}", compiler_params=_cp(("arbitrary",)))(*ins)
    return outs[:6], list(outs[6:])


HG_ROWS = 128


HG_NC = HG_ROWS // HGRN_CHUNK


def _hgrn_block_fn(layer):
    def fn(hq, hf, hi, hg, sprev, lb0, lb1, norm_w, seg, bd, tri_bd, ones_bd, first_row, causal):
        e0 = jnp.exp(lb0 - jnp.maximum(lb0, lb1))
        e1 = jnp.exp(lb1 - jnp.maximum(lb0, lb1))
        sm0, sm1 = e0 / (e0 + e1), e1 / (e0 + e1)
        lb = (sm0 - sm0) if layer == 0 else ((sm0 + sm1) - sm0)
        forget = lb + (1.0 - lb) * _sigmoid(hf)
        logf = jnp.log(forget)
        kk = 1.0 - forget
        q = _silu(hq)
        c, nc = HGRN_CHUNK, HG_NC
        b = _dotx(tri_bd, logf)
        bl = _dotx(ones_bd, logf)
        split = lambda t: t.reshape(nc, c, DG)
        b4 = split(b)
        diff = (b4[:, :, None, :] - b4[:, None, :, :]).reshape(nc * c * c, DG)
        dec = jnp.exp(jnp.where(causal > 0.5, diff, -1e30))
        qrep = jnp.broadcast_to(split(q)[:, :, None, :], (nc, c, c, DG)).reshape(nc * c * c, DG)
        ktil = jnp.broadcast_to(split(kk)[:, None, :, :], (nc, c, c, DG)).reshape(nc * c * c, DG)
        vtil = jnp.broadcast_to(split(hi)[:, None, :, :], (nc, c, c, DG)).reshape(nc * c * c, DG)
        att = _dot(qrep * ktil * dec, seg)
        o_intra = jnp.sum((att * vtil).reshape(nc * c, c, DG), axis=1)
        kd4 = split(kk * jnp.exp(bl - b))
        qe4 = split(q * jnp.exp(b))
        v4 = split(hi)
        tot = jnp.exp(_dotx(first_row, bl))
        s, o_inter = sprev, []
        for ci in range(nc):
            o_inter.append(_dot_nt(qe4[ci], s))
            s = s * tot[ci:ci + 1, :] + _dot_tn(v4[ci], kd4[ci]) * bd
        o = o_intra + jnp.concatenate(o_inter, axis=0)
        ms = _dot(o * o, seg) * (1.0 / HD)
        y = o * lax.rsqrt(ms + RMS_EPS) * norm_w * _silu(hg)
        return y, s

    return fn


def _hgrn_consts(p):
    return [p["seg64"], p["seg64"], p["tri_bd128"], p["ones_bd128"], p["first_row"], p["causal_blk"]]


def hgrn_fwd(l, proj, p):
    fn = _hgrn_block_fn(l)

    def body(hq_ref, hf_ref, hi_ref, hg_ref, *rest):
        const_refs, (y_ref, st_ref, s_sc) = rest[:-3], rest[-3:]

        @pl.when(pl.program_id(0) == 0)
        def _():
            s_sc[...] = jnp.zeros_like(s_sc)

        sprev = s_sc[...]
        st_ref[0] = sprev
        y, snext = fn(hq_ref[...], hf_ref[...], hi_ref[...], hg_ref[...], sprev, *[r[...] for r in const_refs])
        y_ref[...] = y
        s_sc[...] = snext

    rows = lambda cb: pl.BlockSpec((HG_ROWS, DG), lambda i: (i, cb))
    ins = [proj, proj, proj, proj, p["lb0"], p["lb1"], p["hgrn_norm_w"]] + _hgrn_consts(p)
    specs = [rows(C_HQ // DG), rows(C_HF // DG), rows(C_HI // DG), rows(C_HG // DG)] + [_full(x.shape) for x in ins[4:]]
    return pl.pallas_call(body, grid=(SEQ // HG_ROWS,), in_specs=specs,
                          out_specs=[rows(0), pl.BlockSpec((1, DG, DG), lambda i: (i, 0, 0))],
                          out_shape=[_sds((SEQ, DG)), _sds((SEQ // HG_ROWS, DG, DG))],
                          scratch_shapes=[pltpu.VMEM((DG, DG), F32)], name=f"hgrn_fwd{l}",
                          compiler_params=_cp(("arbitrary",)))(*ins)


def hgrn_bwd(l, proj, states, dy, p, sibling=(), dy_col=0):
    fn = _hgrn_block_fn(l)
    nblk = SEQ // HG_ROWS
    n_const = len(_hgrn_consts(p))
    ns = len(sibling)

    def body(hq_ref, hf_ref, hi_ref, hg_ref, st_ref, dy_ref, lb0_ref, lb1_ref, nw_ref, *rest):
        const_refs, rest = rest[:n_const], rest[n_const:]
        dp_ref, dlb0_ref, dlb1_ref, dnw_ref = rest[ns:ns + 4]
        ds_sc = rest[2 * ns + 4]
        if ns:
            begin, end = _sibling_steps(rest[:ns], rest[ns + 4:2 * ns + 4], *rest[2 * ns + 5:])

            @pl.when(pl.program_id(0) == 0)
            def _():
                begin()

        @pl.when(pl.program_id(0) == 0)
        def _():
            ds_sc[...] = jnp.zeros_like(ds_sc)
            dlb0_ref[...] = jnp.zeros_like(dlb0_ref)
            dlb1_ref[...] = jnp.zeros_like(dlb1_ref)
            dnw_ref[...] = jnp.zeros_like(dnw_ref)

        consts = [r[...] for r in const_refs]
        f = lambda hq, hf, hi, hg, sp, b0, b1, nw: fn(hq, hf, hi, hg, sp, b0, b1, nw, *consts)
        _, vjp = jax.vjp(f, hq_ref[...], hf_ref[...], hi_ref[...], hg_ref[...], st_ref[0], lb0_ref[...], lb1_ref[...],
                         nw_ref[...])
        dhq, dhf, dhi, dhg, dsp, dlb0, dlb1, dnw = vjp((dy_ref[...], ds_sc[...]))
        dp_ref[:, 0:DG] = dhq
        dp_ref[:, DG:2 * DG] = dhf
        dp_ref[:, 2 * DG:3 * DG] = dhi
        dp_ref[:, 3 * DG:4 * DG] = dhg
        ds_sc[...] = dsp
        dlb0_ref[...] += dlb0
        dlb1_ref[...] += dlb1
        dnw_ref[...] += dnw

        if ns:
            @pl.when(pl.program_id(0) == nblk - 1)
            def _():
                end()

    rows = lambda cb: pl.BlockSpec((HG_ROWS, DG), lambda i: (nblk - 1 - i, cb))
    ins = [proj, proj, proj, proj, states, dy, p["lb0"], p["lb1"], p["hgrn_norm_w"]] + _hgrn_consts(p)
    specs = [rows(C_HQ // DG), rows(C_HF // DG), rows(C_HI // DG), rows(C_HG // DG),
             pl.BlockSpec((1, DG, DG), lambda i: (nblk - 1 - i, 0, 0)), rows(dy_col)] + [_full(x.shape) for x in ins[6:]]
    sem = pltpu.SemaphoreType.DMA((max(ns, 1), 4))
    outs = pl.pallas_call(body, grid=(nblk,), in_specs=specs + [ANY] * ns,
                          out_specs=[pl.BlockSpec((HG_ROWS, 4 * DG), lambda i: (nblk - 1 - i, 0)), _full((1, DG)),
                                     _full((1, DG)), _full((1, DG))] + [ANY] * ns,
                          out_shape=[_sds((SEQ, 4 * DG)), _sds((1, DG)), _sds((1, DG)), _sds((1, DG))]
                          + [_sds((4,) + a.shape[1:], a.dtype) for a in sibling],
                          scratch_shapes=[pltpu.VMEM((DG, DG), F32)] + ([sem, sem] if ns else []), name=f"hgrn_bwd{l}",
                          compiler_params=_cp(("arbitrary",)))(*ins, *sibling)
    return outs[:4], list(outs[4:])


def _ssd_chunk_fn(z, xs, bm, cm, dtr, sprev, dt_bias, a_log, d_par, norm_w, e128, tri, trit, seg128, ones128):
    lc = SSD_CHUNK
    dt = _softplus(dtr + dt_bias)
    a = -jnp.exp(a_log)
    da = dt * a * (lax.broadcasted_iota(jnp.int32, (1, 128), 1) < NH).astype(F32)
    cs = _dotx(tri, da)
    cst = _dotx_tn(da, trit)
    cs_b = _dotx(cs, e128)
    dt_b = _dotx(dt, e128)
    csl_b = _dotx(jnp.sum(da, axis=0, keepdims=True), e128)
    xdt = xs * dt_b
    lane = lax.broadcasted_iota(jnp.int32, (1, DG), 1)
    rowi = lax.broadcasted_iota(jnp.int32, (lc, lc), 0)
    coli = lax.broadcasted_iota(jnp.int32, (lc, lc), 1)
    y = jnp.zeros((lc, DG), F32)
    snew = jnp.zeros((DG, SSD_N), F32)
    d_b = jnp.zeros((1, DG), F32)
    wdec = xdt * jnp.exp(csl_b - cs_b)
    for g in range(2):
        bg = bm[:, g * SSD_N:(g + 1) * SSD_N]
        cg = cm[:, g * SSD_N:(g + 1) * SSD_N]
        gmat = _dot_nt(cg, bg)
        gmask = ((lane // 128) == g).astype(F32)
        snew = snew + _dot_tn(wdec * gmask, bg)
        y = y + _dot_nt(cg, sprev) * gmask * jnp.exp(cs_b)
        for hh in range(2):
            h = 2 * g + hh
            seg = jnp.where(rowi >= coli, cs[:, h:h + 1] - cst[h:h + 1, :], -1e30)
            hmask = ((lane // HD) == h).astype(F32)
            y = y + _dot(gmat * jnp.exp(seg), xdt * hmask)
            d_b = d_b + d_par[:, h:h + 1] * hmask
    cd = jnp.exp(_dotx_tn(_dotx(da, e128), ones128))
    snext = sprev * cd + snew
    y = y + xs * d_b
    y = y * _silu(z)
    ms = _dot(y * y, seg128) * (1.0 / 128.0)
    return y * lax.rsqrt(ms + RMS_EPS) * norm_w, snext


def ssd_fwd(l, proj, xc, p):
    nc = SEQ // SSD_CHUNK

    def body(z_ref, xs_ref, b_ref, c_ref, dt_ref, dtb_ref, al_ref, d_ref, nw_ref, e_ref, tri_ref, trit_ref, sg_ref,
             on_ref, y_ref, st_ref, s_sc):
        @pl.when(pl.program_id(0) == 0)
        def _():
            s_sc[...] = jnp.zeros_like(s_sc)

        sprev = s_sc[...]
        st_ref[0] = sprev
        y, snext = _ssd_chunk_fn(z_ref[...], xs_ref[...], b_ref[...], c_ref[...], dt_ref[...], sprev, dtb_ref[...],
                                 al_ref[...], d_ref[...], nw_ref[...], e_ref[...], tri_ref[...], trit_ref[...],
                                 sg_ref[...], on_ref[...])
        y_ref[...] = y
        s_sc[...] = snext

    rw = lambda w, cb: pl.BlockSpec((SSD_CHUNK, w), lambda i: (i, cb))
    ins = [proj, xc, xc, xc, proj, p["dt_bias"], p["a_log"], p["ssd_d"], p["ssd_norm_w"], p["e128"], p["tri128"],
           p["tri128t"], p["seg128"], p["ones128"]]
    specs = [rw(DG, C_Z // DG), rw(DG, 0), rw(DG, 1), rw(DG, 2), rw(128, C_DT // 128)] + [_full(x.shape) for x in ins[5:]]
    return pl.pallas_call(body, grid=(nc,), in_specs=specs,
                          out_specs=[rw(DG, 0), pl.BlockSpec((1, DG, SSD_N), lambda i: (i, 0, 0))],
                          out_shape=[_sds((SEQ, DG)), _sds((nc, DG, SSD_N))],
                          scratch_shapes=[pltpu.VMEM((DG, SSD_N), F32)], name=f"ssd_fwd{l}",
                          compiler_params=_cp(("arbitrary",)))(*ins)


def ssd_bwd(l, proj, xc, states, dy, p, dy_col=0):
    nc = SEQ // SSD_CHUNK

    def body(z_ref, xs_ref, b_ref, c_ref, dt_ref, st_ref, dy_ref, dtb_ref, al_ref, d_ref, nw_ref, e_ref, tri_ref,
             trit_ref, sg_ref, on_ref, dz_ref, dxc_ref, ddt_ref, ddtb_ref, dal_ref, dd_ref, dnw_ref, ds_sc):
        @pl.when(pl.program_id(0) == 0)
        def _():
            ds_sc[...] = jnp.zeros_like(ds_sc)
            ddtb_ref[...] = jnp.zeros_like(ddtb_ref)
            dal_ref[...] = jnp.zeros_like(dal_ref)
            dd_ref[...] = jnp.zeros_like(dd_ref)
            dnw_ref[...] = jnp.zeros_like(dnw_ref)

        consts = (e_ref[...], tri_ref[...], trit_ref[...], sg_ref[...], on_ref[...])
        f = lambda *a: _ssd_chunk_fn(*a, *consts)
        _, vjp = jax.vjp(f, z_ref[...], xs_ref[...], b_ref[...], c_ref[...], dt_ref[...], st_ref[0], dtb_ref[...],
                         al_ref[...], d_ref[...], nw_ref[...])
        dz, dxs, db, dc, ddt, dsp, ddtb, dal, dd, dnw = vjp((dy_ref[...], ds_sc[...]))
        dz_ref[...] = dz
        dxc_ref[:, 0:DG] = dxs
        dxc_ref[:, DG:2 * DG] = db
        dxc_ref[:, 2 * DG:3 * DG] = dc
        ddt_ref[...] = ddt
        ds_sc[...] = dsp
        ddtb_ref[...] += ddtb
        dal_ref[...] += dal
        dd_ref[...] += dd
        dnw_ref[...] += dnw

    rw = lambda w, cb: pl.BlockSpec((SSD_CHUNK, w), lambda i: (nc - 1 - i, cb))
    ins = [proj, xc, xc, xc, proj, states, dy, p["dt_bias"], p["a_log"], p["ssd_d"], p["ssd_norm_w"], p["e128"],
           p["tri128"], p["tri128t"], p["seg128"], p["ones128"]]
    specs = [rw(DG, C_Z // DG), rw(DG, 0), rw(DG, 1), rw(DG, 2), rw(128, C_DT // 128),
             pl.BlockSpec((1, DG, SSD_N), lambda i: (nc - 1 - i, 0, 0)), rw(DG, dy_col)] + [_full(x.shape) for x in ins[7:]]
    return pl.pallas_call(body, grid=(nc,), in_specs=specs,
                          out_specs=[rw(DG, 0), rw(3 * DG, 0), rw(128, 0), _full((1, 128)), _full((1, 128)), _full((1, 128)),
                                     _full((1, DG))],
                          out_shape=[_sds((SEQ, DG)), _sds((SEQ, 3 * DG)), _sds((SEQ, 128)), _sds((1, 128)), _sds((1, 128)),
                                     _sds((1, 128)), _sds((1, DG))],
                          scratch_shapes=[pltpu.VMEM((DG, SSD_N), F32)], name=f"ssd_bwd{l}",
                          compiler_params=_cp(("arbitrary",)))(*ins)


ATT_BLK = 128


def _att_geometry(dil):
    i = lax.broadcasted_iota(jnp.int32, (ATT_BLK, ATT_BLK), 0)
    j = lax.broadcasted_iota(jnp.int32, (ATT_BLK, ATT_BLK), 1)
    return ((i - j) * dil).astype(F32), ((ATT_BLK + i - j) * dil).astype(F32), j <= i, j >= i


def _att_scores(qn, kc, kp, h, geom, has_prev):
    dist_c, dist_p, m_c, m_pj = geom
    slope = 2.0 ** (-8.0 * (h + 1) / NH)
    scale = HD ** -0.5
    s_c = _dot_nt(qn, kc) * scale - slope * dist_c
    s_p = _dot_nt(qn, kp) * scale - slope * dist_p
    m_p = jnp.logical_and(m_pj, has_prev)
    return jnp.where(m_c, s_c, -1e30), jnp.where(m_p, s_p, -1e30), m_c, m_p


def _sub_spec(ln, width, col):
    return pl.BlockSpec((ln, DG), lambda z: (0, z * (width // DG) + col // DG))


QKV_W = 3 * DG


def attn_branch_fwd(l, bi, qkv, dil):
    ln = SEQ // dil
    nb = ln // ATT_BLK

    def body(q_ref, k_ref, v_ref, o_ref, l_ref):
        geom = _att_geometry(dil)

        def blk(n, carry):
            r0 = pl.multiple_of(n * ATT_BLK, ATT_BLK)
            rp = pl.multiple_of(jnp.maximum(n - 1, 0) * ATT_BLK, ATT_BLK)
            cur, prv = pl.ds(r0, ATT_BLK), pl.ds(rp, ATT_BLK)
            for h in range(NH):
                hs = slice(h * HD, (h + 1) * HD)
                qn, kc, vc, kp, vp = q_ref[cur, hs], k_ref[cur, hs], v_ref[cur, hs], k_ref[prv, hs], v_ref[prv, hs]
                s_c, s_p, m_c, m_p = _att_scores(qn, kc, kp, h, geom, n > 0)
                m = jnp.maximum(jnp.max(s_c, axis=1, keepdims=True), jnp.max(s_p, axis=1, keepdims=True))
                p_c = jnp.where(m_c, jnp.exp(s_c - m), 0.0)
                p_p = jnp.where(m_p, jnp.exp(s_p - m), 0.0)
                den = jnp.sum(p_c, axis=1, keepdims=True) + jnp.sum(p_p, axis=1, keepdims=True)
                o_ref[cur, hs] = (_dot(p_c, vc) + _dot(p_p, vp)) / den
                l_ref[cur, hs] = jnp.broadcast_to(m + jnp.log(den), (ATT_BLK, HD))
            return carry

        lax.fori_loop(0, nb, blk, 0)

    pv = qkv.reshape(ln, dil * QKV_W)
    out = pl.BlockSpec((ln, DG), lambda z: (0, z))
    o, lse = pl.pallas_call(body, grid=(dil,), in_specs=[_sub_spec(ln, QKV_W, 0), _sub_spec(ln, QKV_W, DG), _sub_spec(ln, QKV_W, 2 * DG)],
                            out_specs=[out, out], out_shape=[_sds((ln, dil * DG))] * 2, name=f"attn_fwd{l}_{bi}",
                            compiler_params=_cp(("parallel",)))(pv, pv, pv)
    return o.reshape(SEQ, DG), lse.reshape(SEQ, DG)


def attn_branch_bwd(l, bi, qkv, dil, dyb, lse_all, delta):
    ln = SEQ // dil
    nb = ln // ATT_BLK
    scale = HD ** -0.5

    def body(q_ref, k_ref, v_ref, do_ref, l_ref, dl_ref, dq_ref, dk_ref, dv_ref):
        dk_ref[...] = jnp.zeros_like(dk_ref)
        dv_ref[...] = jnp.zeros_like(dv_ref)
        geom = _att_geometry(dil)

        def blk(n, carry):
            r0 = pl.multiple_of(n * ATT_BLK, ATT_BLK)
            rp = pl.multiple_of(jnp.maximum(n - 1, 0) * ATT_BLK, ATT_BLK)
            cur, prv = pl.ds(r0, ATT_BLK), pl.ds(rp, ATT_BLK)
            for h in range(NH):
                hs = slice(h * HD, (h + 1) * HD)
                qn, don = q_ref[cur, hs], do_ref[cur, hs]
                lse, dlt = l_ref[cur, h * HD:h * HD + 1], dl_ref[cur, h * HD:h * HD + 1]
                kc, vc, kp, vp = k_ref[cur, hs], v_ref[cur, hs], k_ref[prv, hs], v_ref[prv, hs]
                s_c, s_p, m_c, m_p = _att_scores(qn, kc, kp, h, geom, n > 0)
                p_c = jnp.where(m_c, jnp.exp(s_c - lse), 0.0)
                p_p = jnp.where(m_p, jnp.exp(s_p - lse), 0.0)
                ds_c = p_c * (_dot_nt(don, vc) - dlt)
                ds_p = p_p * (_dot_nt(don, vp) - dlt)
                dq_ref[cur, hs] = (_dot(ds_c, kc) + _dot(ds_p, kp)) * scale
                dv_ref[prv, hs] += _dot_tn(p_p, don)
                dk_ref[prv, hs] += _dot_tn(ds_p, qn) * scale
                dv_ref[cur, hs] += _dot_tn(p_c, don)
                dk_ref[cur, hs] += _dot_tn(ds_c, qn) * scale
            return carry

        lax.fori_loop(0, nb, blk, 0)

    pv = qkv.reshape(ln, dil * QKV_W)
    sub = lambda t: t.reshape(ln, dil * DG)
    row = pl.BlockSpec((ln, DG), lambda z: (0, z))
    outs = pl.pallas_call(body, grid=(dil,),
                          in_specs=[_sub_spec(ln, QKV_W, 0), _sub_spec(ln, QKV_W, DG), _sub_spec(ln, QKV_W, 2 * DG), row, row, row],
                          out_specs=[row] * 3, out_shape=[_sds((ln, dil * DG))] * 3, name=f"attn_bwd{l}_{bi}",
                          compiler_params=_cp(("parallel",)))(pv, pv, pv, sub(dyb), sub(lse_all), sub(delta))
    return [t.reshape(SEQ, DG) for t in outs]


def _attn_merge_fn(o1, o2, o3, l1, l2, l3):
    m = jnp.maximum(jnp.maximum(l1, l2), l3)
    w1, w2, w3 = jnp.exp(l1 - m), jnp.exp(l2 - m), jnp.exp(l3 - m)
    den = w1 + w2 + w3
    return (w1 * o1 + w2 * o2 + w3 * o3) / den, m + jnp.log(den)


def attn_merge(l, os_, ls_):
    ins = list(os_) + list(ls_)
    return _map_fwd(f"attn_merge{l}", _attn_merge_fn, (SEQ // RB,), ins, [_rows(DG)] * 6, [_sds((SEQ, DG))] * 2,
                    [_rows(DG)] * 2)


def attn_delta(l, dyb, yb, seg):
    fn = lambda d, y, s: (_dot(d * y, s),)
    return _map_fwd(f"attn_delta{l}", fn, (SEQ // RB,), [dyb, yb, seg], [_rows(DG), _rows(DG), _full((DG, DG))],
                    [_sds((SEQ, DG))], [_rows(DG)])[0]


def _ln_fn(x, mix, w, b):
    h = ALPHA * x + mix
    mu = jnp.mean(h, axis=-1, keepdims=True)
    d = h - mu
    var = jnp.mean(d * d, axis=-1, keepdims=True)
    return (d * lax.rsqrt(var + LN_EPS) * w + b,)


def ln_fwd(name, x, mix, w, b):
    specs = [_rows(D_MODEL), _rows(D_MODEL), _full((1, D_MODEL)), _full((1, D_MODEL))]
    return _map_fwd(name, _ln_fn, (SEQ // RB,), [x, mix, w, b], specs, [_sds((SEQ, D_MODEL))], [_rows(D_MODEL)])[0]


def ln_bwd(name, x, mix, w, b, dy):
    specs = [_rows(D_MODEL), _rows(D_MODEL), _full((1, D_MODEL)), _full((1, D_MODEL))]
    return _map_bwd(name, _ln_fn, (SEQ // RB,), [x, mix, w, b], specs, [[dy]], [[_rows(D_MODEL)]], want=[1, 2, 3],
                    acc=(2, 3))


def loss_call(y, tgt):
    def fn(yy, tt):
        e = yy - tt
        part = 0.5 * jnp.sum(jnp.sum(e * e, axis=-1, keepdims=True) * (1.0 / D_MODEL), axis=0, keepdims=True)
        return e * (1.0 / D_MODEL), jnp.broadcast_to(part, (8, 128))

    return _map_fwd("loss", fn, (SEQ // RB,), [y, tgt], [_rows(D_MODEL)] * 2,
                    [_sds((SEQ, D_MODEL)), _sds((SEQ // RB * 8, 128))],
                    [_rows(D_MODEL), pl.BlockSpec((8, 128), lambda i: (i, 0))])


LATE_KEYS = ("w_out", "w_up_t", "w_down")


def _full_rows(g):
    return g.reshape(N_DEV * g.shape[1], g.shape[2])


def layer_fwd(l, x, vfirst, wts, p, gather=(), late=False):
    sv = {"x": x}
    proj = _mm(f"mm_in{l}", x, wts["w_in"], "nn", *MM_TILES["k1024"])
    fl = lerp_fwd(l, proj, p["mu"])
    xc = conv_fwd(l, proj, p["conv_w"], p["conv_b"])
    w, k2, v2, c, b, g = rwkv_pre_fwd(l, fl, vfirst, p)
    y_scan, states, sv["gathered"] = rwkv_scan_fwd(l, fl, w, k2, v2, c, b, p, gather)
    if late:
        wts = dict(wts, **dict(zip(LATE_KEYS, [_full_rows(g) for g in sv["gathered"][:3]])))
    sv["wts"] = wts
    ya = rwkv_post_fwd(l, y_scan, fl, k2, v2, g, p)
    qkv = proj[:, C_AQ:C_AQ + 3 * DG]
    outs, lses = [], []
    for bi, (win, dil) in enumerate(DILATED):
        o, lse = attn_branch_fwd(l, bi, qkv, dil)
        outs.append(o)
        lses.append(lse)
    yb, lse_all = attn_merge(l, outs, lses)
    yc, ssd_states = ssd_fwd(l, proj, xc, p)
    yd, hg_states = hgrn_fwd(l, proj, p)
    ycat = jnp.concatenate([ya, yb, yc, yd], axis=1).astype(BF16)
    mix = _mm(f"mm_out{l}", ycat, wts["w_out"], "nn", *MM_TILES["k1024"])
    x1 = ln_fwd(f"ln1_fwd{l}", x, mix, p["ln1_w"], p["ln1_b"])
    hh = _mm(f"mm_up{l}", x1, wts["w_up_t"], "nt", *MM_TILES["k1024"], epilogue="relu2")
    m2 = _mm(f"mm_down{l}", hh, wts["w_down"], "nn", *MM_TILES["k4096"])
    x2 = ln_fwd(f"ln2_fwd{l}", x1, m2, p["ln2_w"], p["ln2_b"])
    sv.update(proj=proj, fl=fl, xc=xc, w=w, k2=k2, v2=v2, c=c, b=b, g=g, y_scan=y_scan, states=states,
              yb=yb, lse_all=lse_all, ssd_states=ssd_states, hg_states=hg_states, ycat=ycat, mix=mix, x1=x1, hh=hh, qkv=qkv,
              m2=m2, vfirst=vfirst)
    return x2, sv


def layer_bwd(l, dx2, dvfirst_next, sv, wts, p, exchange=(), reducer=None):
    gr = {}
    x, x1, proj, fl = sv["x"], sv["x1"], sv["proj"], sv["fl"]
    dres2, gr["ln2_w"], gr["ln2_b"] = ln_bwd(f"ln2_bwd{l}", x1, sv["m2"], p["ln2_w"], p["ln2_b"], dx2)
    du = _mm(f"mm_down_dx{l}", dres2, wts["w_down"], "nt", *MM_TILES["k1024"], add=sv["hh"], epilogue="relu2_bwd",
             out_dtype=BF16)
    gr["w_down"] = _mm(f"mm_down_dw{l}", sv["hh"], dres2, "tn", *MM_TILES["wgrad_tall"])
    dx1 = _mm(f"mm_up_dx{l}", du, wts["w_up_t"], "nn", *MM_TILES["k4096"], add=dres2, add_scale=ALPHA)
    gr["w_up_t"] = _mm(f"mm_up_dw{l}", du, x1, "tn", *MM_TILES["wgrad_tall"])
    dres1, gr["ln1_w"], gr["ln1_b"] = ln_bwd(f"ln1_bwd{l}", x, sv["mix"], p["ln1_w"], p["ln1_b"], dx1)
    dycat = _mm(f"mm_out_dx{l}", dres1, wts["w_out"], "nt", *MM_TILES["k1024"])
    gr["w_out"] = _mm(f"mm_out_dw{l}", sv["ycat"], dres1, "tn", 1024, 1024, 512)
    dyb = dycat[:, DG:2 * DG]
    send = [_owner_blocks(gr[k]) for k in LATE_KEYS] if reducer else []
    (dhg4, gr["lb0"], gr["lb1"], gr["hgrn_norm_w"]), sib = hgrn_bwd(l, proj, sv["hg_states"], dycat, p, send, dy_col=3)
    if reducer:
        gr["early_own"], early_parts = reducer(f"{l}a", send, sib)
        exchange = list(exchange) + list(early_parts)
    dz, dxc, ddt, gr["dt_bias"], gr["a_log"], gr["ssd_d"], gr["ssd_norm_w"] = ssd_bwd(l, proj, sv["xc"], sv["ssd_states"], dycat, p, dy_col=2)
    dxbc, gr["conv_w"], gr["conv_b"] = conv_bwd(l, proj, p["conv_w"], p["conv_b"], dxc)
    delta = attn_delta(l, dyb, sv["yb"], p["seg64"])
    dqs, dks, dvs = [], [], []
    for bi, (win, dil) in enumerate(DILATED):
        dq, dk, dv = attn_branch_bwd(l, bi, sv["qkv"], dil, dyb, sv["lse_all"], delta)
        dqs.append(dq)
        dks.append(dk)
        dvs.append(dv)
    dq_a, dk_a, dv_a = _addn(f"attn_dq{l}", *dqs), _addn(f"attn_dk{l}", *dks), _addn(f"attn_dv{l}", *dvs)
    pg = rwkv_post_bwd(l, sv["y_scan"], fl, sv["k2"], sv["v2"], sv["g"], p, dycat)
    gr["lnx_w"], gr["lnx_b"], gr["r_k"] = pg["lnx_w"], pg["lnx_b"], pg["r_k"]
    (dr, dw, dk, dv, dc, db), gr["exchanged"] = rwkv_scan_bwd(l, fl, sv["w"], sv["k2"], sv["v2"], sv["c"], sv["b"],
                                                              sv["states"], pg["y"], p, exchange)
    v2_cts = [dv, pg["v2"]] + ([dvfirst_next] if dvfirst_next is not None else [])
    qg = rwkv_pre_bwd(l, fl, sv["vfirst"], p, [[dw], [dk, pg["k2"]], v2_cts, [dc], [db], [pg["g"]]])
    for nme in ("w0", "w2p", "a0", "a2p", "g2p", "k_k", "k_a", "v0", "v2p"):
        if nme in qg:
            gr[nme] = qg[nme]
    dfr = _addn(f"rwkv_dr{l}", dr, pg["fr"])
    dvres = qg["fvres"] if l > 0 else jnp.zeros((SEQ, 128), F32)
    dfl_out = jnp.concatenate([dfr, qg["fk"], qg["fv"], qg["flora"], dvres], axis=1)
    dfl_in, gr["mu"] = lerp_bwd(l, proj, p["mu"], dfl_out)
    dproj = jnp.concatenate([dfl_in[:, 0:768], dq_a, dk_a, dv_a, dz, dxbc, dhg4, dfl_in[:, 768:896], ddt,
                             dfl_in[:, 896:1024], jnp.zeros((SEQ, 128), F32)], axis=1).astype(BF16)
    dx = _mm(f"mm_in_dx{l}", dproj, wts["w_in"], "nt", *MM_TILES["k4096"], add=dres1, add_scale=ALPHA)
    gr["w_in"] = _mm(f"mm_in_dw{l}", x, dproj, "tn", *MM_TILES["wgrad_wide"])
    return dx, (qg["vfirst"] if l > 0 else None), gr


def _w_in_pad(w_in_l, w_vres):
    rows = w_in_l.shape[0]
    z = lambda n: jnp.zeros((rows, n), w_in_l.dtype)
    vres = z(128) if w_vres is None else jnp.concatenate([w_vres, z(96)], axis=1)
    return jnp.concatenate([w_in_l[:, 0:768], w_in_l[:, 896:1664], w_in_l[:, 1664:1920], w_in_l[:, 1920:2688],
                            w_in_l[:, 2692:3716], w_in_l[:, 768:896], w_in_l[:, 2688:2692], z(124), vres, z(128)], axis=1)


def _w_in_unpad(g):
    g_in = jnp.concatenate([g[:, 0:768], g[:, C_LORA:C_LORA + 128], g[:, 768:1536], g[:, C_Z:C_Z + 256],
                            g[:, C_XBC:C_XBC + 768], g[:, C_DT:C_DT + 4], g[:, C_HQ:C_HQ + 1024]], axis=1)
    return g_in, g[:, C_VRES:C_VRES + 32]


def _consts():
    pair = jnp.arange(HG_NC * HGRN_CHUNK * HGRN_CHUNK)
    i128 = jnp.arange(128)
    same_chunk = (i128[:, None] // HGRN_CHUNK) == (i128[None, :] // HGRN_CHUNK)
    seg64 = _seg_ones(DG, HD)
    tri128 = (i128[:, None] >= i128[None, :]).astype(F32)
    return dict(
        seg64=seg64, seg64x3_bf16=jnp.concatenate([seg64, seg64, seg64], axis=0).astype(BF16),
        dmask=(jnp.arange(HD)[:, None] == (jnp.arange(DG)[None, :] % HD)).astype(F32),
        tri_bd128=(same_chunk & (i128[:, None] >= i128[None, :])).astype(F32), ones_bd128=same_chunk.astype(F32),
        first_row=(i128[None, :] == (jnp.arange(HG_NC) * HGRN_CHUNK)[:, None]).astype(F32),
        causal_blk=jnp.broadcast_to((((pair // HGRN_CHUNK) % HGRN_CHUNK) >= (pair % HGRN_CHUNK)).astype(F32)[:, None],
                                    (HG_NC * HGRN_CHUNK * HGRN_CHUNK, DG)),
        e128=((i128[:, None] == (jnp.arange(DG)[None, :] // HD)) & (i128[:, None] < NH)).astype(F32),
        tri128=tri128, tri128t=tri128.T, seg128=_seg_ones(DG, 128), ones128=jnp.ones((128, 128), F32))


def _pad_lanes(v, n):
    return jnp.concatenate([v, jnp.zeros((n - v.shape[0],), v.dtype)])[None, :]


def _layer_params(l, raw, consts):
    p = dict(consts)
    row = lambda name: raw[name][l][None, :]
    z = lambda r: jnp.zeros((r, DG), F32)
    mu_vres = raw["mu_vres"][l - 1] if l > 0 else jnp.zeros((32,), F32)
    p["mu"] = jnp.concatenate([raw["mu_shift"][l], mu_vres, jnp.zeros((96,), F32)])[None, :]
    p["conv_w"], p["conv_b"] = raw["ssd_conv_w"][l], row("ssd_conv_b")
    p["w0"], p["a0"], p["k_k"], p["k_a"] = row("rwkv_w0"), row("rwkv_a0"), row("rwkv_k_k"), row("rwkv_k_a")
    p["lnx_w"], p["lnx_b"] = row("rwkv_lnx_w"), row("rwkv_lnx_b")
    p["r_k"] = raw["rwkv_r_k"][l].reshape(1, DG)
    p["w2p"] = jnp.concatenate([raw["rwkv_w2"][l], z(96)], axis=0)
    p["a2p"] = jnp.concatenate([z(32), raw["rwkv_a2"][l], z(64)], axis=0)
    p["g2p"] = jnp.concatenate([z(64), raw["rwkv_g2"][l]], axis=0)
    if l > 0:
        p["v0"] = raw["rwkv_v0"][l - 1][None, :]
        p["v2p"] = jnp.concatenate([raw["rwkv_v2"][l - 1], z(96)], axis=0)
    p["lb0"], p["lb1"] = raw["lower_bounds"][0:1], raw["lower_bounds"][1:2]
    p["hgrn_norm_w"], p["ssd_norm_w"] = row("hgrn_norm_w"), row("ssd_norm_w")
    p["dt_bias"], p["a_log"], p["ssd_d"] = (_pad_lanes(raw[n][l], 128) for n in ("ssd_dt_bias", "ssd_A_log", "ssd_D"))
    for n in ("ln1_w", "ln1_b", "ln2_w", "ln2_b"):
        p[n] = row(n)
    return p


def _natural_grads(g0, g1):
    gs = (g0, g1)
    st = lambda key, f=lambda a: a[0]: jnp.stack([f(g[key]) for g in gs])
    out = {}
    out["lower_bounds"] = jnp.concatenate([g0["lb0"] + g1["lb0"], g0["lb1"] + g1["lb1"]], axis=0)
    out["mu_shift"] = st("mu", lambda a: a[0, :896])
    out["mu_vres"] = g1["mu"][:, 896:928]
    out["rwkv_w0"], out["rwkv_a0"], out["rwkv_k_k"], out["rwkv_k_a"] = st("w0"), st("a0"), st("k_k"), st("k_a")
    out["rwkv_w2"] = st("w2p", lambda a: a[0:32])
    out["rwkv_a2"] = st("a2p", lambda a: a[32:64])
    out["rwkv_g2"] = st("g2p", lambda a: a[64:128])
    out["rwkv_r_k"] = st("r_k", lambda a: a.reshape(NH, HD))
    out["rwkv_lnx_w"], out["rwkv_lnx_b"] = st("lnx_w"), st("lnx_b")
    out["rwkv_v0"] = g1["v0"]
    out["rwkv_v2"] = g1["v2p"][None, 0:32]
    out["ssd_conv_w"] = st("conv_w", lambda a: a)
    out["ssd_conv_b"] = st("conv_b")
    out["ssd_dt_bias"], out["ssd_A_log"], out["ssd_D"] = (st(k, lambda a: a[0, :NH]) for k in ("dt_bias", "a_log", "ssd_d"))
    out["ssd_norm_w"], out["hgrn_norm_w"] = st("ssd_norm_w"), st("hgrn_norm_w")
    for n in ("ln1_w", "ln1_b", "ln2_w", "ln2_b"):
        out[n] = st(n)
    return out


MESH_T = pl.DeviceIdType.MESH
ANY = pl.BlockSpec(memory_space=pl.ANY)


def _dev_index(px, py, pc):
    return 4 * px + 2 * py + pc


def all_gather(arrs):
    n = len(arrs)

    def body(*refs):
        begin, middle, end = _gather_steps(refs[:n], refs[n:2 * n], *refs[2 * n:])
        begin()
        middle()
        end()

    return pl.pallas_call(body, in_specs=[ANY] * n, out_specs=[ANY] * n, out_shape=_gather_shapes(arrs),
                          scratch_shapes=_gather_sems(n), name="all_gather")(*arrs)


def _gather_shapes(arrs):
    return [_sds((N_DEV,) + a.shape, a.dtype) for a in arrs]


def _gather_sems(n):
    return [pltpu.SemaphoreType.DMA((n, 7)), pltpu.SemaphoreType.DMA((n, 7)), pltpu.SemaphoreType.DMA((n,))]


def _gather_steps(ins, outs, send_sems, recv_sems, local_sems):
    n = len(ins)
    x, y, c = lax.axis_index("x"), lax.axis_index("y"), lax.axis_index("c")
    me, sibling = (x, y, c), (x, y, 1 - c)
    chips = [(1 - x, y), (x, 1 - y), (1 - x, 1 - y)]

    def copy(a, k, block, to, src=None):
        slot = outs[a].at[_dev_index(*block)]
        return pltpu.make_async_remote_copy(src_ref=slot if src is None else src, dst_ref=slot,
                                            send_sem=send_sems.at[a, k], recv_sem=recv_sems.at[a, k],
                                            device_id=to, device_id_type=MESH_T)

    def own_copies():
        mine = [pltpu.make_async_copy(ins[a], outs[a].at[_dev_index(*me)], local_sems.at[a]) for a in range(n)]
        first = []
        for a in range(n):
            first.append(copy(a, 0, me, sibling, src=ins[a]))
            first += [copy(a, 1 + j, me, (*chip, c), src=ins[a]) for j, chip in enumerate(chips)]
        return mine, first

    def begin():
        mine, first = own_copies()
        for cp in mine + first:
            cp.start()

    def passed_on():
        return [copy(a, 4 + j, (*chip, c), sibling) for j, chip in enumerate(chips) for a in range(n)]

    def middle():
        for j, chip in enumerate(chips):
            for a in range(n):
                copy(a, 1 + j, (*chip, c), me).wait_recv()
        for cp in passed_on():
            cp.start()

    def end():
        mine, first = own_copies()
        for a in range(n):
            copy(a, 0, sibling, me).wait_recv()
            for j, chip in enumerate(chips):
                copy(a, 4 + j, (*chip, 1 - c), me).wait_recv()
        for cp in first + passed_on():
            cp.wait_send()
        for cp in mine:
            cp.wait()

    return begin, middle, end


def _chips(x, y):
    return [(x, y), (1 - x, y), (x, 1 - y), (1 - x, 1 - y)]


def _sibling_steps(ins, sib, send_sems, recv_sems):
    x, y, c = lax.axis_index("x"), lax.axis_index("y"), lax.axis_index("c")

    def copies():
        return [pltpu.make_async_remote_copy(src_ref=ins[a].at[_dev_index(cx, cy, 1 - c)], dst_ref=sib[a].at[k],
                                             send_sem=send_sems.at[a, k], recv_sem=recv_sems.at[a, k],
                                             device_id=(x, y, 1 - c), device_id_type=MESH_T)
                for a in range(len(ins)) for k, (cx, cy) in enumerate(_chips(x, y))]

    def begin():
        for cp in copies():
            cp.start()

    def end():
        cps = copies()
        for cp in cps:
            cp.wait_recv()
        for cp in cps:
            cp.wait_send()

    return begin, end


def exchange_siblings(arrs, name):
    n = len(arrs)

    def body(*refs):
        begin, end = _sibling_steps(refs[:n], refs[n:2 * n], *refs[2 * n:])
        begin()
        end()

    sem = pltpu.SemaphoreType.DMA((n, 4))
    return pl.pallas_call(body, in_specs=[ANY] * n, out_specs=[ANY] * n,
                          out_shape=[_sds((4,) + a.shape[1:], a.dtype) for a in arrs],
                          scratch_shapes=[sem, sem], name=name)(*arrs)


def reduce_pair(name, send, slots, sib, wire_dtype):
    _, r, c = send.shape
    rb = min(r, 262144 // c)

    def body(slots_ref, m0, m1, m2, m3, s_ref, own_ref, part_ref):
        own_ref[...] = m0[...] + s_ref[0]
        for k, m_ref in enumerate((m1, m2, m3)):
            part_ref[k] = (m_ref[...] + s_ref[k + 1]).astype(wire_dtype)

    mine = [pl.BlockSpec((None, rb, c), lambda i, s, k=k: (s[k], i, 0)) for k in range(4)]
    grid_spec = pltpu.PrefetchScalarGridSpec(
        num_scalar_prefetch=1, grid=(r // rb,),
        in_specs=mine + [pl.BlockSpec((4, rb, c), lambda i, s: (0, i, 0))],
        out_specs=[pl.BlockSpec((rb, c), lambda i, s: (i, 0)), pl.BlockSpec((3, rb, c), lambda i, s: (0, i, 0))])
    return pl.pallas_call(body, grid_spec=grid_spec, out_shape=[_sds((r, c)), _sds((3, r, c), wire_dtype)], name=name,
                          compiler_params=_cp(("parallel",)))(slots, send, send, send, send, sib)


def _chip_exchange_steps(ins, recv, send_sems, recv_sems):
    x, y, c = lax.axis_index("x"), lax.axis_index("y"), lax.axis_index("c")

    def copies():
        return [pltpu.make_async_remote_copy(src_ref=ins[a].at[k], dst_ref=recv[a].at[k], send_sem=send_sems.at[a, k],
                                             recv_sem=recv_sems.at[a, k], device_id=(cx, cy, c), device_id_type=MESH_T)
                for a in range(len(ins)) for k, (cx, cy) in enumerate(_chips(x, y)[1:])]

    def begin():
        for cp in copies():
            cp.start()

    def end():
        cps = copies()
        for cp in cps:
            cp.wait_recv()
        for cp in cps:
            cp.wait_send()

    return begin, end


def exchange_chips(parts, rep):
    n = len(parts)

    def body(*refs):
        ins, rep_ref = refs[:n], refs[n]
        recv, rep_all = refs[n + 1:2 * n + 1], refs[2 * n + 1]
        send_sems, recv_sems, rsend_sems, rrecv_sems, local_sem = refs[2 * n + 2:]
        x, y, c = lax.axis_index("x"), lax.axis_index("y"), lax.axis_index("c")
        me = _dev_index(x, y, c)
        mine = pltpu.make_async_copy(rep_ref, rep_all.at[me], local_sem)
        mine.start()
        begin, end = _chip_exchange_steps(ins, recv, send_sems, recv_sems)
        begin()
        rels = [(rx, ry, rc) for rx in (0, 1) for ry in (0, 1) for rc in (0, 1)][1:]
        peers = [(jnp.where(rx, 1 - x, x), jnp.where(ry, 1 - y, y), jnp.where(rc, 1 - c, c)) for rx, ry, rc in rels]
        rcps = []
        for k, peer in enumerate(peers):
            cp = pltpu.make_async_remote_copy(src_ref=rep_ref, dst_ref=rep_all.at[me], send_sem=rsend_sems.at[k],
                                              recv_sem=rrecv_sems.at[k], device_id=peer, device_id_type=MESH_T)
            cp.start()
            rcps.append(cp)
        for k, peer in enumerate(peers):
            pltpu.make_async_remote_copy(src_ref=rep_ref, dst_ref=rep_all.at[_dev_index(*peer)], send_sem=rsend_sems.at[k],
                                         recv_sem=rrecv_sems.at[k], device_id=peer, device_id_type=MESH_T).wait_recv()
        end()
        for cp in rcps:
            cp.wait_send()
        mine.wait()

    outs = pl.pallas_call(
        body, in_specs=[ANY] * (n + 1), out_specs=[ANY] * (n + 1),
        out_shape=[_sds(a.shape, a.dtype) for a in parts] + [_sds((N_DEV,) + rep.shape, rep.dtype)],
        scratch_shapes=[pltpu.SemaphoreType.DMA((n, 3)), pltpu.SemaphoreType.DMA((n, 3)), pltpu.SemaphoreType.DMA((7,)),
                        pltpu.SemaphoreType.DMA((7,)), pltpu.SemaphoreType.DMA],
        name="exchange_chips")(*parts, rep)
    return outs[:n], outs[n]


def adamw(name, terms, w, m, v, transposed=False):
    r, c = w.shape[::-1] if transposed else w.shape
    rb = r if transposed else min(r, 262144 // c)
    c1 = 1.0 - ADAM_B1 ** ADAM_STEP
    c2 = 1.0 - ADAM_B2 ** ADAM_STEP
    nt = len(terms)

    def body(*refs):
        w_ref, m_ref, v_ref = refs[nt:nt + 3]
        g_ref, d_ref, nm_ref, nv_ref = refs[nt + 3:]
        g = refs[0][...].astype(F32)
        for t_ref in refs[1:nt]:
            g = g + t_ref[...].astype(F32)
        if transposed:
            g = g.T
        nm = ADAM_B1 * m_ref[...] + (1.0 - ADAM_B1) * g
        nv = ADAM_B2 * v_ref[...] + (1.0 - ADAM_B2) * (g * g)
        g_ref[...] = g
        nm_ref[...] = nm
        nv_ref[...] = nv
        d_ref[...] = -ADAM_LR * ((nm / c1) / (jnp.sqrt(nv / c2) + ADAM_EPS) + ADAM_WD * w_ref[...])

    blk = pl.BlockSpec((rb, c), lambda i: (i, 0))
    wblk = pl.BlockSpec((c, r), lambda i: (0, 0)) if transposed else blk
    tspecs = [blk if k is None else pl.BlockSpec((None, rb, c), lambda i, k=k: (k, i, 0)) for _, k in terms]
    return pl.pallas_call(body, grid=(r // rb,), in_specs=tspecs + [wblk] * 3, out_specs=[wblk] * 4,
                          out_shape=[_sds(w.shape)] * 4, name=name,
                          compiler_params=_cp(("parallel",)))(*[t for t, _ in terms], w, m, v)


W_IN_PIECES = ((0, 768, 0), (768, 896, C_LORA), (896, 1664, 768), (1664, 1920, C_Z), (1920, 2688, C_XBC),
               (2688, 2692, C_DT), (2692, 3716, C_HQ))
VRES_W = 32


def adamw_w_in(name, terms, w, m, v, vres=None):
    nt, nv = len(terms), 3 if vres else 0
    c1 = 1.0 - ADAM_B1 ** ADAM_STEP
    c2 = 1.0 - ADAM_B2 ** ADAM_STEP

    def body(*refs):
        w_ref, m_ref, v_ref = refs[nt:nt + 3]
        vres_refs = refs[nt + 3:nt + 3 + nv]
        outs = refs[nt + 3 + nv:nt + 7 + nv]
        vres_outs = refs[nt + 7 + nv:]
        g_all = refs[0][...].astype(F32)
        for t_ref in refs[1:nt]:
            g_all = g_all + t_ref[...].astype(F32)

        def update(g, wmv, out_refs, cols):
            nm = ADAM_B1 * wmv[1][:, cols] + (1.0 - ADAM_B1) * g
            nv_ = ADAM_B2 * wmv[2][:, cols] + (1.0 - ADAM_B2) * (g * g)
            out_refs[0][:, cols] = g
            out_refs[1][:, cols] = -ADAM_LR * ((nm / c1) / (jnp.sqrt(nv_ / c2) + ADAM_EPS) + ADAM_WD * wmv[0][:, cols])
            out_refs[2][:, cols] = nm
            out_refs[3][:, cols] = nv_

        for lo, hi, src in W_IN_PIECES:
            update(g_all[:, src:src + hi - lo], (w_ref, m_ref, v_ref), outs, slice(lo, hi))
        if vres:
            update(g_all[:, C_VRES:C_VRES + VRES_W], vres_refs, vres_outs, slice(0, VRES_W))

    r, c = terms[0][0].shape[-2:]
    tspecs = [_full((r, c)) if k is None else pl.BlockSpec((None, r, c), lambda i, k=k: (k, 0, 0)) for _, k in terms]
    wspec, vspec = _full(w.shape), _full((w.shape[0], VRES_W))
    outs = pl.pallas_call(body, grid=(1,), in_specs=tspecs + [wspec] * 3 + [vspec] * nv,
                          out_specs=[wspec] * 4 + [vspec] * (4 if vres else 0),
                          out_shape=[_sds(w.shape)] * 4 + [_sds((w.shape[0], VRES_W))] * (4 if vres else 0), name=name,
                          compiler_params=_cp(("arbitrary",)))(*[t for t, _ in terms], w, m, v, *(vres or ()))
    return list(outs[:4]), list(outs[4:])


SMS_ROWS = 16
N_BIG = 8
SMALL_SHARDED = (("rwkv_w2", (2, 32, 32)), ("rwkv_a2", (2, 32, 32)), ("rwkv_g2", (2, 64, 32)), ("rwkv_v2", (1, 32, 32)),
                 ("ssd_conv_w", (2, 4, 96)))
REPLICATED = (("lower_bounds", (2, 256)), ("mu_shift", (2, 896)), ("mu_vres", (1, 32)), ("rwkv_w0", (2, 256)),
              ("rwkv_a0", (2, 256)), ("rwkv_k_k", (2, 256)), ("rwkv_k_a", (2, 256)), ("rwkv_r_k", (2, 4, 64)),
              ("rwkv_lnx_w", (2, 256)), ("rwkv_lnx_b", (2, 256)), ("rwkv_v0", (1, 256)), ("ssd_conv_b", (2, 768)),
              ("ssd_dt_bias", (2, 4)), ("ssd_A_log", (2, 4)), ("ssd_D", (2, 4)), ("ssd_norm_w", (2, 256)),
              ("hgrn_norm_w", (2, 256)), ("ln1_w", (2, 1024)), ("ln1_b", (2, 1024)), ("ln2_w", (2, 1024)),
              ("ln2_b", (2, 1024)))


def _flat_rows(parts, rows):
    flat = jnp.concatenate([a.reshape(-1) for a in parts])
    return jnp.concatenate([flat, jnp.zeros((rows * PACK_W - flat.shape[0],), flat.dtype)]).reshape(rows, PACK_W)


def _local_arrays(d):
    return [_w_in_pad(d["w_in"][0], None), _w_in_pad(d["w_in"][1], d["w_in_vres"][0]), d["w_out"][0], d["w_out"][1],
            d["w_up"][0], d["w_up"][1], d["w_down"][0], d["w_down"][1],
            _flat_rows([d[n] for n, _ in SMALL_SHARDED], SMS_ROWS)]


def _unflat(rows2d, table):
    flat, out, o = rows2d.reshape(-1), {}, 0
    for name, shape in table:
        n = 1
        for s in shape:
            n *= s
        out[name] = flat[o:o + n].reshape(shape)
        o += n
    return out


def _from_local_arrays(arrs, rep, w_in_vres):
    d = dict(rep)
    d["w_in"], d["w_in_vres"] = jnp.stack([arrs[0], arrs[1]]), w_in_vres[None]
    d["w_out"] = jnp.stack([arrs[2], arrs[3]])
    d["w_up"] = jnp.stack([arrs[4], arrs[5]])
    d["w_down"] = jnp.stack([arrs[6], arrs[7]])
    d.update(_unflat(arrs[8], SMALL_SHARDED))
    return d


def _small_sharded_full(gs):
    small, flat, o = {}, gs.reshape(N_DEV, -1), 0
    for name, shape in SMALL_SHARDED:
        n = shape[0] * shape[1] * shape[2]
        blk = flat[:, o:o + n].reshape((N_DEV,) + shape)
        small[name] = blk.transpose(1, 2, 0, 3).reshape(shape[0], shape[1], N_DEV * shape[2])
        o += n
    return small


def _owner_blocks(g):
    return g.reshape(N_DEV, g.shape[0] // N_DEV, g.shape[1])


def _as_rows(shape):
    width = 1
    for s in shape[1:]:
        width *= s
    return shape[0], width


REP_2D = tuple((name, _as_rows(shape)) for name, shape in REPLICATED)
REP_ROW0 = tuple(sum(a for _, (a, _) in REP_2D[:i]) for i in range(len(REP_2D)))
REP_ROWS = sum(a for _, (a, _) in REP_2D)


def _rep_rows(d):
    rows = []
    for name, (a, b) in REP_2D:
        v = d[name].reshape(a, b)
        rows.append(v if b == PACK_W else jnp.concatenate([v, jnp.zeros((a, PACK_W - b), F32)], axis=1))
    return jnp.concatenate(rows, axis=0)


def adamw_replicated(rep_all, w, m, v):
    names = [name for name, _ in REP_2D]
    n = len(names)
    c1 = 1.0 - ADAM_B1 ** ADAM_STEP
    c2 = 1.0 - ADAM_B2 ** ADAM_STEP

    def body(*refs):
        rep_ref, w_refs, m_refs, v_refs, outs = refs[0], refs[1:1 + n], refs[1 + n:1 + 2 * n], refs[1 + 2 * n:1 + 3 * n], refs[1 + 3 * n:]
        for i, (_, (a, b)) in enumerate(REP_2D):
            r0 = REP_ROW0[i]
            g = rep_ref[0, r0:r0 + a, 0:b]
            for q in range(1, N_DEV):
                g = g + rep_ref[q, r0:r0 + a, 0:b]
            nm = ADAM_B1 * m_refs[i][...] + (1.0 - ADAM_B1) * g
            nv = ADAM_B2 * v_refs[i][...] + (1.0 - ADAM_B2) * (g * g)
            outs[i][...] = g
            outs[n + i][...] = -ADAM_LR * ((nm / c1) / (jnp.sqrt(nv / c2) + ADAM_EPS) + ADAM_WD * w_refs[i][...])
            outs[2 * n + i][...] = nm
            outs[3 * n + i][...] = nv

    flat = lambda d: [d[name].reshape(ab) for name, ab in REP_2D]
    pspecs = [_full(ab) for _, ab in REP_2D]
    res = pl.pallas_call(body, grid=(1,), in_specs=[_full(rep_all.shape)] + pspecs * 3, out_specs=pspecs * 4,
                         out_shape=[_sds(ab) for _, ab in REP_2D] * 4, name="adamw_replicated",
                         compiler_params=_cp(("arbitrary",)))(rep_all, *flat(w), *flat(m), *flat(v))
    shapes = dict(REPLICATED)
    return [{name: res[k * n + i].reshape(shapes[name]) for i, name in enumerate(names)} for k in range(4)]


def _small_send_arrays(small_grads):
    sms = []
    for name, shape in SMALL_SHARDED:
        g = small_grads[name].reshape(shape[0], shape[1], N_DEV, shape[2]).transpose(2, 0, 1, 3)
        sms.append(g.reshape(N_DEV, -1))
    sms = jnp.concatenate(sms, axis=1)
    sms = jnp.concatenate([sms, jnp.zeros((N_DEV, SMS_ROWS * PACK_W - sms.shape[1]), F32)], axis=1)
    return sms.reshape(N_DEV, SMS_ROWS, PACK_W), _rep_rows(small_grads)


BIG_KEYS = ("w_in", "w_out", "w_up_t", "w_down")


def _local_step(x, tgt, wts, raw, gather=(), pair_sums=None, reducer=None):
    consts = _consts()
    ps = [_layer_params(l, raw, consts) for l in range(DEPTH)]
    x1, sv0 = layer_fwd(0, x, None, wts[0], ps[0], gather[:4], late=bool(gather))
    wts1 = {"w_in": _full_rows(sv0["gathered"][3])} if gather else wts[1]
    x2, sv1 = layer_fwd(1, x1, sv0["fl"], wts1, ps[1], gather[4:], late=bool(gather))
    dy, lparts = loss_call(x2, tgt)
    loss = jnp.sum(lparts[::8, 0])
    dx1, dvfirst, g1 = layer_bwd(1, dy, None, sv1, sv1["wts"], ps[1], (), reducer)
    big1 = {k: g1[k] for k in BIG_KEYS}
    if reducer is None:
        dx0, _, g0 = layer_bwd(0, dx1, dvfirst, sv0, sv0["wts"], ps[0])
        early = None
    else:
        own_in1, parts_in1 = pair_sums("1b", {"w_in": g1["w_in"]})
        dx0, _, g0 = layer_bwd(0, dx1, dvfirst, sv0, sv0["wts"], ps[0], parts_in1, reducer)
        own, recv = {(1, "w_in"): own_in1[0]}, {(1, "w_in"): g0["exchanged"][0]}
        for l, g, first in ((1, g1, 0), (0, g0, 1)):
            for i, k in enumerate(LATE_KEYS):
                own[(l, k)], recv[(l, k)] = g["early_own"][i], g["exchanged"][first + i]
        early = (own, recv)
    big = [{k: g0[k] for k in BIG_KEYS}, big1]
    return loss, dx0, big, _natural_grads(g0, g1), early


WEIGHT_NAMES = ("lower_bounds", "w_in", "w_in_vres", "mu_shift", "mu_vres", "rwkv_w0", "rwkv_w2", "rwkv_a0", "rwkv_a2",
                "rwkv_g2", "rwkv_k_k", "rwkv_k_a", "rwkv_r_k", "rwkv_lnx_w", "rwkv_lnx_b", "rwkv_v0", "rwkv_v2",
                "ssd_conv_w", "ssd_conv_b", "ssd_dt_bias", "ssd_A_log", "ssd_D", "ssd_norm_w", "hgrn_norm_w", "w_out",
                "ln1_w", "ln1_b", "w_up", "w_down", "ln2_w", "ln2_b")


def kernel(x, lower_bounds, w_in, w_in_vres, mu_shift, mu_vres, rwkv_w0, rwkv_w2, rwkv_a0, rwkv_a2, rwkv_g2, rwkv_k_k, rwkv_k_a, rwkv_r_k, rwkv_lnx_w, rwkv_lnx_b, rwkv_v0, rwkv_v2, ssd_conv_w, ssd_conv_b, ssd_dt_bias, ssd_A_log, ssd_D, ssd_norm_w, hgrn_norm_w, w_out, ln1_w, ln1_b, w_up, w_down, ln2_w, ln2_b, loss_target, m_lower_bounds, m_w_in, m_w_in_vres, m_mu_shift, m_mu_vres, m_rwkv_w0, m_rwkv_w2, m_rwkv_a0, m_rwkv_a2, m_rwkv_g2, m_rwkv_k_k, m_rwkv_k_a, m_rwkv_r_k, m_rwkv_lnx_w, m_rwkv_lnx_b, m_rwkv_v0, m_rwkv_v2, m_ssd_conv_w, m_ssd_conv_b, m_ssd_dt_bias, m_ssd_A_log, m_ssd_D, m_ssd_norm_w, m_hgrn_norm_w, m_w_out, m_ln1_w, m_ln1_b, m_w_up, m_w_down, m_ln2_w, m_ln2_b, v_lower_bounds, v_w_in, v_w_in_vres, v_mu_shift, v_mu_vres, v_rwkv_w0, v_rwkv_w2, v_rwkv_a0, v_rwkv_a2, v_rwkv_g2, v_rwkv_k_k, v_rwkv_k_a, v_rwkv_r_k, v_rwkv_lnx_w, v_rwkv_lnx_b, v_rwkv_v0, v_rwkv_v2, v_ssd_conv_w, v_ssd_conv_b, v_ssd_dt_bias, v_ssd_A_log, v_ssd_D, v_ssd_norm_w, v_hgrn_norm_w, v_w_out, v_ln1_w, v_ln1_b, v_w_up, v_w_down, v_ln2_w, v_ln2_b):
    given = dict(locals())
    w = {n: given[n] for n in WEIGHT_NAMES}
    m_all, v_all = ({n: given[pre + n] for n in WEIGHT_NAMES} for pre in ("m_", "v_"))
    w_arrs, m_arrs, v_arrs = _local_arrays(w), _local_arrays(m_all), _local_arrays(v_all)
    wire = lambda a: (w_arrs[a].T if a in (4, 5) else w_arrs[a]).astype(BF16)
    gathered0 = all_gather([wire(0), w_arrs[N_BIG]])
    raw = {n: w[n] for n, _ in REPLICATED}
    raw.update(_small_sharded_full(gathered0[1]))
    mx, my, mc = lax.axis_index("x"), lax.axis_index("y"), lax.axis_index("c")
    slots = jnp.stack([_dev_index(cx, cy, mc) for cx, cy in _chips(mx, my)]).astype(jnp.int32)

    def reducer(tag, send, sib, n_f32=0):
        wire_dt = [BF16] * (len(send) - n_f32) + [F32] * n_f32
        res = [reduce_pair(f"reduce_pair{tag}_{i}", s, slots, sb, dt) for i, (s, sb, dt) in enumerate(zip(send, sib, wire_dt))]
        return [o for o, _ in res], [pt for _, pt in res]

    def pair_sums(tag, grads, extra=()):
        send = [_owner_blocks(g) for g in grads.values()] + list(extra)
        return reducer(tag, send, exchange_siblings(send, f"exchange_siblings{tag}"), len(extra))

    behind_scan = [wire(a) for a in (2, 4, 6, 1, 3, 5, 7)]
    loss, dx, big, small_grads, (own_by, recv_by) = _local_step(
        x[0], loss_target[0], [{"w_in": _full_rows(gathered0[0])}, None], raw, behind_scan, pair_sums, reducer)
    sms_send, rep = _small_send_arrays(small_grads)
    own0b, parts0b = pair_sums("0b", {"w_in": big[0]["w_in"]}, [sms_send])
    recv0b, rep_all = exchange_chips(parts0b, rep)
    own, recv = [None] * (N_BIG + 1), [None] * (N_BIG + 1)
    for (l, k), o in own_by.items():
        a = 2 * BIG_KEYS.index(k) + l
        own[a], recv[a] = o, recv_by[(l, k)]
    for a, o, r in zip((0, N_BIG), own0b, recv0b):
        own[a], recv[a] = o, r
    terms = lambda a: [(own[a], None), (recv[a], 0), (recv[a], 1), (recv[a], 2)]
    moments = [{n: given[pre + n] for n in ("w_in", "w_in_vres")} for pre in ("", "m_", "v_")]
    in0, _ = adamw_w_in("adamw0", terms(0), *[d["w_in"][0] for d in moments])
    in1, vres = adamw_w_in("adamw1", terms(1), *[d["w_in"][1] for d in moments], vres=[d["w_in_vres"][0] for d in moments])
    results = [in0, in1] + [adamw(f"adamw{a}", terms(a), w_arrs[a], m_arrs[a], v_arrs[a], transposed=a in (4, 5))
                            for a in range(2, N_BIG + 1)]
    rep_res = adamw_replicated(rep_all, w, m_all, v_all)
    loss = lax.psum(loss, ("x", "y", "c"))
    outs = [loss, dx[None]]
    for q in range(4):
        d = _from_local_arrays([res[q] for res in results], rep_res[q], vres[q])
        outs += [d[n] for n in WEIGHT_NAMES]
    return tuple(outs)
```

```python
import functools

import jax
import jax.numpy as jnp
from jax import lax
from jax.experimental import pallas as pl
from jax.experimental.pallas import tpu as pltpu

F32 = jnp.float32
BF16 = jnp.bfloat16
HI = lax.Precision.HIGHEST

N_DEV = 8
SEQ = 2048
D_MODEL = 1024
D_FF = 4096
DG = 256
NH = 4
HD = 64
DEPTH = 2
ALPHA = (2.0 * DEPTH) ** 0.25
LN_EPS = 1e-5
RMS_EPS = 1e-5
GN_EPS = HD * 1e-5
SSD_N = 128
SSD_CHUNK = 128
HGRN_CHUNK = 16
DILATED = ((128, 1), (512, 4), (2048, 16))

ADAM_LR, ADAM_B1, ADAM_B2, ADAM_EPS, ADAM_WD, ADAM_STEP = 0.001, 0.9, 0.999, 1e-08, 0.01, 10

PW = 4096
C_R, C_K, C_V = 0, 256, 512
C_AQ, C_AK, C_AV = 768, 1024, 1280
C_Z, C_XBC = 1536, 1792
C_HQ, C_HF, C_HI, C_HG = 2560, 2816, 3072, 3328
C_LORA, C_DT, C_VRES = 3584, 3712, 3840

RB = 256
VMEM_LIMIT = 56 * 1024 * 1024
PACK_W = 1024


def _cp(sem=None):
    return pltpu.CompilerParams(dimension_semantics=sem, vmem_limit_bytes=VMEM_LIMIT)


def _sds(shape, dt=F32):
    return jax.ShapeDtypeStruct(tuple(shape), dt)


def _rows(w, cb=0, rb=RB):
    return pl.BlockSpec((rb, w), lambda i: (i, cb))


def _full(shape):
    n = len(shape)
    return pl.BlockSpec(tuple(shape), lambda *_: (0,) * n)


def _sigmoid(x):
    return 1.0 / (1.0 + jnp.exp(-x))


def _silu(x):
    return x * _sigmoid(x)


def _softplus(x):
    return jnp.maximum(x, 0.0) + jnp.log(1.0 + jnp.exp(jnp.where(x > 0, -x, x)))


MID = lax.Precision.HIGH
NN, TN, NT = (((1,), (0,)), ((), ())), (((0,), (0,)), ((), ())), (((1,), (1,)), ((), ()))


def _dot(a, b):
    return lax.dot_general(a, b, NN, precision=MID, preferred_element_type=F32)


def _dot_tn(a, b):
    return lax.dot_general(a, b, TN, precision=MID, preferred_element_type=F32)


def _dot_nt(a, b):
    return lax.dot_general(a, b, NT, precision=MID, preferred_element_type=F32)


def _dotx(a, b):
    return lax.dot_general(a, b, NN, precision=HI, preferred_element_type=F32)


def _dotx_tn(a, b):
    return lax.dot_general(a, b, TN, precision=HI, preferred_element_type=F32)


def _seg_ones(n, seg):
    i = jnp.arange(n)
    return (i[:, None] // seg == i[None, :] // seg).astype(F32)


def _shift_down(x, s):
    row = lax.broadcasted_iota(jnp.int32, x.shape, 0)
    return jnp.where(row < s, 0.0, pltpu.roll(x, s, 0))


def _shift_up(x, s):
    n = x.shape[0]
    row = lax.broadcasted_iota(jnp.int32, x.shape, 0)
    return jnp.where(row >= n - s, 0.0, pltpu.roll(x, n - s, 0))


@functools.partial(jax.custom_vjp, nondiff_argnums=(1,))
def _tshift(x, s):
    return _shift_down(x, s)


def _tshift_fwd(x, s):
    return _shift_down(x, s), None


def _tshift_bwd(s, _, g):
    return (_shift_up(g, s),)


_tshift.defvjp(_tshift_fwd, _tshift_bwd)


def _map_fwd(name, fn, grid, ins, in_specs, out_shapes, out_specs):
    n_in = len(ins)

    def body(*refs):
        ys = fn(*[r[...] for r in refs[:n_in]])
        for r, y in zip(refs[n_in:], ys):
            r[...] = y

    return pl.pallas_call(body, grid=grid, in_specs=in_specs, out_specs=out_specs, out_shape=out_shapes,
                          name=name, compiler_params=_cp(("parallel",)))(*ins)


def _map_bwd(name, fn, grid, ins, in_specs, cts, ct_specs, want, acc=(), gout=None):
    n_in = len(ins)
    flat_cts = [c for group in cts for c in group]
    flat_specs = [s for group in ct_specs for s in group]
    n_ct = len(flat_cts)
    gout = gout or {}
    out_shapes = [gout[i][0] if i in gout else _sds(ins[i].shape) for i in want]
    out_specs = [gout[i][1] if i in gout else in_specs[i] for i in want]

    def body(*refs):
        xs = [r[...] for r in refs[:n_in]]
        cvals = [r[...] for r in refs[n_in:n_in + n_ct]]
        gouts = refs[n_in + n_ct:]
        cs, p = [], 0
        for group in cts:
            v = cvals[p]
            for q in range(1, len(group)):
                v = v + cvals[p + q]
            cs.append(v)
            p += len(group)

        def f(*wanted):
            full = list(xs)
            for i, w in zip(want, wanted):
                full[i] = w
            return tuple(fn(*full))

        _, vjp = jax.vjp(f, *[xs[i] for i in want])
        gs = vjp(tuple(cs))
        for o, i, g in zip(gouts, want, gs):
            if i in acc:
                @pl.when(pl.program_id(0) == 0)
                def _():
                    o[...] = jnp.zeros_like(o)

                o[...] += g
            else:
                o[...] = g

    sem = ("arbitrary",) if acc else ("parallel",)
    return pl.pallas_call(body, grid=grid, in_specs=list(in_specs) + flat_specs, out_specs=out_specs,
                          out_shape=out_shapes, name=name, compiler_params=_cp(sem))(*ins, *flat_cts)


def _addn(name, *arrs):
    n, c = arrs[0].shape

    def fn(*xs):
        r = xs[0]
        for x in xs[1:]:
            r = r + x
        return (r,)

    return _map_fwd(name, fn, (n // RB,), list(arrs), [_rows(c)] * len(arrs), [_sds((n, c))], [_rows(c)])[0]


MM_TILES = {"k1024": (2048, 512, 1024), "k4096": (1024, 1024, 1024), "wgrad_tall": (2048, 1024, 512),
            "wgrad_wide": (1024, 2048, 512)}


def _mm(name, a, b, mode, tm, tn, tk, add=None, add_scale=1.0, epilogue=None, out_dtype=F32):
    if mode == "nn":
        (m, k), n = a.shape, b.shape[1]
    elif mode == "nt":
        (m, k), n = a.shape, b.shape[0]
    else:
        (k, m), n = a.shape, b.shape[1]
    nk = k // tk
    dn = {"nn": (((1,), (0,)), ((), ())), "nt": (((1,), (1,)), ((), ())), "tn": (((0,), (0,)), ((), ()))}[mode]

    def body(*refs):
        a_ref, b_ref = refs[:2]
        add_ref = refs[2] if add is not None else None
        o_ref = refs[3] if add is not None else refs[2]
        prod = lax.dot_general(a_ref[...].astype(BF16), b_ref[...].astype(BF16), dn, preferred_element_type=F32)

        def finish(r):
            if epilogue == "relu2":
                r = jnp.maximum(r, 0.0)
                r = r * r
            elif epilogue == "relu2_bwd":
                r = r * (2.0 * jnp.sqrt(add_ref[...]))
            elif add is not None:
                r = r + add_scale * add_ref[...]
            o_ref[...] = r.astype(out_dtype)

        if nk == 1:
            finish(prod)
        else:
            acc = refs[-1]
            kk = pl.program_id(2)

            @pl.when(kk == 0)
            def _():
                acc[...] = prod

            @pl.when(kk > 0)
            def _():
                acc[...] += prod

            @pl.when(kk == nk - 1)
            def _():
                finish(acc[...])

    a_spec = pl.BlockSpec((tk, tm), lambda i, j, q: (q, i)) if mode == "tn" else pl.BlockSpec((tm, tk), lambda i, j, q: (i, q))
    b_spec = pl.BlockSpec((tn, tk), lambda i, j, q: (j, q)) if mode == "nt" else pl.BlockSpec((tk, tn), lambda i, j, q: (q, j))
    o_spec = pl.BlockSpec((tm, tn), lambda i, j, q: (i, j))
    ins, specs = [a, b], [a_spec, b_spec]
    if add is not None:
        ins.append(add)
        specs.append(o_spec)
    return pl.pallas_call(body, grid=(m // tm, n // tn, nk), in_specs=specs, out_specs=o_spec,
                          out_shape=_sds((m, n), out_dtype),
                          scratch_shapes=[pltpu.VMEM((tm, tn), F32)] if nk > 1 else [], name=name,
                          compiler_params=_cp(("parallel", "parallel", "arbitrary")))(*ins)


def _lerp_colmap(j):
    r = jnp.where(j < 6, j, jnp.where(j == 6, C_LORA // 128, C_VRES // 128))
    return (0, r)


def _lerp_fn(f, mu):
    return (f + (_tshift(f, 1) - f) * mu,)


def _lerp_specs():
    return [pl.BlockSpec((SEQ, 128), _lerp_colmap), pl.BlockSpec((1, 128), lambda j: (0, j))]


def lerp_fwd(l, proj, mu):
    return _map_fwd(f"lerp_fwd{l}", _lerp_fn, (8,), [proj, mu], _lerp_specs(), [_sds((SEQ, 1024))],
                    [pl.BlockSpec((SEQ, 128), lambda j: (0, j))])[0]


def lerp_bwd(l, proj, mu, dfl):
    n_in = 2

    def body(f_ref, mu_ref, g_ref, df_ref, dmu_ref):
        _, vjp = jax.vjp(_lerp_fn, f_ref[...], mu_ref[...])
        df, dmu = vjp((g_ref[...],))
        df_ref[...] = df
        dmu_ref[...] = dmu

    cspec = pl.BlockSpec((SEQ, 128), lambda j: (0, j))
    return pl.pallas_call(body, grid=(8,), in_specs=_lerp_specs() + [cspec],
                          out_specs=[cspec, pl.BlockSpec((1, 128), lambda j: (0, j))],
                          out_shape=[_sds((SEQ, 1024)), _sds((1, 1024))], name=f"lerp_bwd{l}",
                          compiler_params=_cp(("parallel",)))(proj, mu, dfl)


def _conv_fn(x, w, b):
    y = x * w[3:4, :] + _tshift(x, 1) * w[2:3, :] + _tshift(x, 2) * w[1:2, :] + _tshift(x, 3) * w[0:1, :] + b
    return (_silu(y),)


def _conv_specs():
    return [pl.BlockSpec((SEQ, 128), lambda j: (0, C_XBC // 128 + j)), pl.BlockSpec((4, 128), lambda j: (0, j)),
            pl.BlockSpec((1, 128), lambda j: (0, j))]


def conv_fwd(l, proj, w, b):
    return _map_fwd(f"conv_fwd{l}", _conv_fn, (6,), [proj, w, b], _conv_specs(), [_sds((SEQ, 768))],
                    [pl.BlockSpec((SEQ, 128), lambda j: (0, j))])[0]


def conv_bwd(l, proj, w, b, dxc):
    def body(x_ref, w_ref, b_ref, g_ref, dx_ref, dw_ref, db_ref):
        _, vjp = jax.vjp(_conv_fn, x_ref[...], w_ref[...], b_ref[...])
        dx, dw, db = vjp((g_ref[...],))
        dx_ref[...] = dx
        dw_ref[...] = dw
        db_ref[...] = db

    cspec = pl.BlockSpec((SEQ, 128), lambda j: (0, j))
    return pl.pallas_call(body, grid=(6,), in_specs=_conv_specs() + [cspec],
                          out_specs=[cspec, pl.BlockSpec((4, 128), lambda j: (0, j)), pl.BlockSpec((1, 128), lambda j: (0, j))],
                          out_shape=[_sds((SEQ, 768)), _sds((4, 768)), _sds((1, 768))], name=f"conv_bwd{l}",
                          compiler_params=_cp(("parallel",)))(proj, w, b, dxc)


def _rwkv_pre_fn(has_vres):
    def fn(fk, fv, flora, *rest):
        if has_vres:
            fvres, vfirst, w0, w2p, a0, a2p, g2p, k_k, k_a, v0, v2p, seg = rest
        else:
            w0, w2p, a0, a2p, g2p, k_k, k_a, seg = rest
        w_log = -_softplus(-(w0 + _dot(jnp.tanh(flora), w2p))) - 0.5
        w = jnp.exp(-jnp.exp(w_log))
        a = _sigmoid(a0 + _dot(flora, a2p))
        g = _dot(_sigmoid(flora), g2p)
        if has_vres:
            v2 = fv + (vfirst - fv) * _sigmoid(v0 + _dot(fvres, v2p))
        else:
            v2 = fv * 1.0
        kk = fk * k_k
        kk = kk / jnp.maximum(jnp.sqrt(_dot(kk * kk, seg)), 1e-12)
        k2 = fk * (1.0 + (a - 1.0) * k_a)
        return w, k2, v2, -kk, kk * a, g

    return fn


def _rwkv_pre_args(fl, vfirst, p, has_vres):
    ins = [fl, fl, fl]
    specs = [_rows(256, 1), _rows(256, 2), _rows(128, 6)]
    if has_vres:
        ins += [fl, vfirst]
        specs += [_rows(128, 7), _rows(256, 2)]
    names = ["w0", "w2p", "a0", "a2p", "g2p", "k_k", "k_a"] + (["v0", "v2p"] if has_vres else []) + ["seg64"]
    for nme in names:
        ins.append(p[nme])
        specs.append(_full(p[nme].shape))
    return ins, specs, names


def rwkv_pre_fwd(l, fl, vfirst, p):
    has_vres = l > 0
    ins, specs, _ = _rwkv_pre_args(fl, vfirst, p, has_vres)
    return _map_fwd(f"rwkv_pre_fwd{l}", _rwkv_pre_fn(has_vres), (SEQ // RB,), ins, specs,
                    [_sds((SEQ, DG))] * 6, [_rows(DG)] * 6)


def rwkv_pre_bwd(l, fl, vfirst, p, cts):
    has_vres = l > 0
    ins, specs, names = _rwkv_pre_args(fl, vfirst, p, has_vres)
    n_row = 5 if has_vres else 3
    want = list(range(n_row)) + [n_row + i for i, nme in enumerate(names) if nme != "seg64"]
    acc = tuple(w for w in want if w >= n_row)
    ct_specs = [[_rows(DG)] * len(g) for g in cts]
    gout = {0: (_sds((SEQ, DG)), _rows(DG)), 1: (_sds((SEQ, DG)), _rows(DG)), 2: (_sds((SEQ, 128)), _rows(128))}
    if has_vres:
        gout[3] = (_sds((SEQ, 128)), _rows(128))
        gout[4] = (_sds((SEQ, DG)), _rows(DG))
    gs = _map_bwd(f"rwkv_pre_bwd{l}", _rwkv_pre_fn(has_vres), (SEQ // RB,), ins, specs, cts, ct_specs, want, acc, gout)
    keys = ["fk", "fv", "flora"] + (["fvres", "vfirst"] if has_vres else []) + [nme for nme in names if nme != "seg64"]
    return dict(zip(keys, gs))


def _rwkv_post_fn(y, fr, k2, v2, g, lnx_w, lnx_b, r_k, seg):
    mu = _dot(y, seg) * (1.0 / HD)
    d = y - mu
    var = _dot(d * d, seg) * (1.0 / HD)
    yn = d * lax.rsqrt(var + GN_EPS) * lnx_w + lnx_b
    bonus = _dot(fr * k2 * r_k, seg) * v2
    return ((yn + bonus) * g,)


def _rwkv_post_args(y, fl, k2, v2, g, p):
    ins = [y, fl, k2, v2, g, p["lnx_w"], p["lnx_b"], p["r_k"], p["seg64"]]
    specs = [_rows(DG), _rows(DG, 0), _rows(DG), _rows(DG), _rows(DG)] + [_full(x.shape) for x in ins[5:]]
    return ins, specs


def rwkv_post_fwd(l, y, fl, k2, v2, g, p):
    ins, specs = _rwkv_post_args(y, fl, k2, v2, g, p)
    return _map_fwd(f"rwkv_post_fwd{l}", _rwkv_post_fn, (SEQ // RB,), ins, specs, [_sds((SEQ, DG))], [_rows(DG)])[0]


def rwkv_post_bwd(l, y, fl, k2, v2, g, p, dya):
    ins, specs = _rwkv_post_args(y, fl, k2, v2, g, p)
    gs = _map_bwd(f"rwkv_post_bwd{l}", _rwkv_post_fn, (SEQ // RB,), ins, specs, [[dya]], [[_rows(DG, 0)]],
                  want=[0, 1, 2, 3, 4, 5, 6, 7], acc=(5, 6, 7), gout={1: (_sds((SEQ, DG)), _rows(DG))})
    return dict(zip(["y", "fr", "k2", "v2", "g", "lnx_w", "lnx_b", "r_k"], gs))


SCAN_TB = 128


def _coltile8(rows8, dmask, ones_stack, parts):
    pieces, rest = [], rows8
    for q in range(parts):
        piece = rest.astype(BF16).astype(F32)
        if q < parts - 1:
            rest = rest - piece
        pieces.append((piece[:, None, :] * dmask[None]).reshape(8 * HD, DG).astype(BF16))
    x = pieces[0] if parts == 1 else jnp.concatenate(pieces, axis=1)
    return jnp.dot(x, ones_stack, preferred_element_type=F32).reshape(8, HD, DG)


def _coltiles_bf16(rows_list, dmask, ones_bf16):
    x = jnp.concatenate([(r8[:, None, :] * dmask[None]).reshape(8 * HD, DG).astype(BF16) for r8 in rows_list], axis=0)
    t = jnp.dot(x, ones_bf16, preferred_element_type=F32)
    return [t[q * 8 * HD:(q + 1) * 8 * HD].reshape(8, HD, DG) for q in range(len(rows_list))]


def _segrows8(x8, dmask, ones_bf16):
    t = jnp.dot(x8.reshape(8 * HD, DG).astype(BF16), ones_bf16, preferred_element_type=F32).reshape(8, HD, DG)
    return jnp.sum(t * dmask[None], axis=1)


def rwkv_scan_fwd(l, fl, w, k2, v2, c, b, p, gather=()):
    nblk = SEQ // SCAN_TB
    ng = len(gather)

    def body(*refs):
        r_ref, w_ref, k_ref, v_ref, c_ref, b_ref, ones_ref, dm_ref = refs[:8]
        y_ref, st_ref = refs[8 + ng:10 + ng]
        s_sc = refs[10 + 2 * ng]
        if ng:
            begin, middle, end = _gather_steps(refs[8:8 + ng], refs[10 + ng:10 + 2 * ng], *refs[11 + 2 * ng:])

            @pl.when(pl.program_id(0) == 0)
            def _():
                begin()

            @pl.when(pl.program_id(0) == (3 * nblk) // 4)
            def _():
                middle()

        @pl.when(pl.program_id(0) == 0)
        def _():
            s_sc[...] = jnp.zeros_like(s_sc)

        ones3, ones = ones_ref[...], ones_ref[0:DG, :]
        dmask = dm_ref[...]

        def group(gi, carry):
            t0 = pl.multiple_of(gi * 8, 8)
            sl = pl.ds(t0, 8)
            v8 = v_ref[sl, :]
            wt = _coltile8(w_ref[sl, :], dmask, ones3, 3)
            ct, bt, kt, rt = _coltiles_bf16([c_ref[sl, :], b_ref[sl, :], k_ref[sl, :], r_ref[sl, :]], dmask, ones)
            t = s_sc[...]
            for j in range(8):
                sa = jnp.sum(t * ct[j], axis=0, keepdims=True)
                t = t * wt[j] + bt[j] * sa + kt[j] * v8[j:j + 1, :]
                st_ref[t0 + j] = t
            s_sc[...] = t
            y_ref[sl, :] = jnp.sum(st_ref[sl] * rt, axis=1)
            return carry

        lax.fori_loop(0, SCAN_TB // 8, group, 0)

        if ng:
            @pl.when(pl.program_id(0) == nblk - 1)
            def _():
                end()

    row = pl.BlockSpec((SCAN_TB, DG), lambda i: (i, 0))
    ins = [fl, w, k2, v2, c, b, p["seg64x3_bf16"], p["dmask"]] + list(gather)
    specs = [row] * 6 + [_full((3 * DG, DG)), _full((HD, DG))] + [ANY] * ng
    outs = pl.pallas_call(body, grid=(nblk,), in_specs=specs,
                          out_specs=[row, pl.BlockSpec((SCAN_TB, HD, DG), lambda i: (i, 0, 0))] + [ANY] * ng,
                          out_shape=[_sds((SEQ, DG)), _sds((SEQ, HD, DG))] + _gather_shapes(gather),
                          scratch_shapes=[pltpu.VMEM((HD, DG), F32)] + (_gather_sems(ng) if ng else []),
                          name=f"rwkv_scan_fwd{l}", compiler_params=_cp(("arbitrary",)))(*ins)
    return outs[0], outs[1], list(outs[2:])


def rwkv_scan_bwd(l, fl, w, k2, v2, c, b, states, dy, p, exchange=()):
    nblk = SEQ // SCAN_TB
    nx = len(exchange)

    def body(*refs):
        r_ref, w_ref, k_ref, v_ref, c_ref, b_ref, dy_ref, st_ref, sp_ref, ones_ref, dm_ref = refs[:11]
        dr_ref, dw_ref, dk_ref, dv_ref, dc_ref, db_ref = refs[11 + nx:17 + nx]
        g_sc, prev_sc, d8_sc, dsa_sc = refs[17 + 2 * nx:21 + 2 * nx]
        i = pl.program_id(0)
        if nx:
            begin, end = _chip_exchange_steps(refs[11:11 + nx], refs[17 + nx:17 + 2 * nx], *refs[21 + 2 * nx:])

            @pl.when(i == 0)
            def _():
                begin()

        @pl.when(i == 0)
        def _():
            g_sc[...] = jnp.zeros_like(g_sc)

        ones3, ones = ones_ref[...], ones_ref[0:DG, :]
        dmask = dm_ref[...]
        first_block = i == nblk - 1

        def group(gr, carry):
            gi = SCAN_TB // 8 - 1 - gr
            t0 = pl.multiple_of(gi * 8, 8)
            sl = pl.ds(t0, 8)
            v8, dy8 = v_ref[sl, :], dy_ref[sl, :]
            t8 = st_ref[sl]
            @pl.when(gi > 0)
            def _():
                prev_sc[0] = st_ref[t0 - 1]

            @pl.when(gi == 0)
            def _():
                prev_sc[0] = jnp.where(first_block, 0.0, sp_ref[0])

            for j in range(1, 8):
                prev_sc[j] = t8[j - 1]
            tp8 = prev_sc[...]
            wt = _coltile8(w_ref[sl, :], dmask, ones3, 3)
            ct, bt, kt, rt = _coltiles_bf16([c_ref[sl, :], b_ref[sl, :], k_ref[sl, :], r_ref[sl, :]], dmask, ones)
            sa8 = jnp.sum(tp8 * ct, axis=1)
            g = g_sc[...]
            for j in range(7, -1, -1):
                g = g + rt[j] * dy8[j:j + 1, :]
                d8_sc[j] = g
                dsa = jnp.sum(g * bt[j], axis=0, keepdims=True)
                dsa_sc[j:j + 1, :] = dsa
                g = g * wt[j] + ct[j] * dsa
            g_sc[...] = g
            d8 = d8_sc[...]
            dsa8 = dsa_sc[...]
            dv_ref[sl, :] = jnp.sum(d8 * kt, axis=1)
            dr_ref[sl, :] = _segrows8(t8 * dy8[:, None, :], dmask, ones)
            dk_ref[sl, :] = _segrows8(d8 * v8[:, None, :], dmask, ones)
            dw_ref[sl, :] = _segrows8(tp8 * d8, dmask, ones)
            db_ref[sl, :] = _segrows8(d8 * sa8[:, None, :], dmask, ones)
            dc_ref[sl, :] = _segrows8(tp8 * dsa8[:, None, :], dmask, ones)
            return carry

        lax.fori_loop(0, SCAN_TB // 8, group, 0)

        if nx:
            @pl.when(i == nblk - 1)
            def _():
                end()

    row = pl.BlockSpec((SCAN_TB, DG), lambda i: (nblk - 1 - i, 0))
    st_spec = pl.BlockSpec((SCAN_TB, HD, DG), lambda i: (nblk - 1 - i, 0, 0))
    sp_spec = pl.BlockSpec((1, HD, DG), lambda i: (jnp.maximum((nblk - 1 - i) * SCAN_TB - 1, 0), 0, 0))
    ins = [fl, w, k2, v2, c, b, dy, states, states, p["seg64x3_bf16"], p["dmask"]] + list(exchange)
    specs = [row] * 7 + [st_spec, sp_spec, _full((3 * DG, DG)), _full((HD, DG))] + [ANY] * nx
    tile8 = pltpu.VMEM((8, HD, DG), F32)
    sems = [pltpu.SemaphoreType.DMA((nx, 3)), pltpu.SemaphoreType.DMA((nx, 3))] if nx else []
    outs = pl.pallas_call(body, grid=(nblk,), in_specs=specs, out_specs=[row] * 6 + [ANY] * nx,
                          out_shape=[_sds((SEQ, DG))] * 6 + [_sds(a.shape, a.dtype) for a in exchange],
                          scratch_shapes=[pltpu.VMEM((HD, DG), F32), tile8, tile8, pltpu.VMEM((8, DG), F32)] + sems,
                          name=f"rwkv_scan_bwd{l}", compiler_params=_cp(("arbitrary",)))(*ins)
    return outs[:6], list(outs[6:])


HG_ROWS = 256


HG_NC = HG_ROWS // HGRN_CHUNK


def _hgrn_block_fn(layer):
    def fn(hq, hf, hi, hg, sprev, lb0, lb1, norm_w, seg, bd, tri_bd, ones_bd, first_row, causal):
        e0 = jnp.exp(lb0 - jnp.maximum(lb0, lb1))
        e1 = jnp.exp(lb1 - jnp.maximum(lb0, lb1))
        sm0, sm1 = e0 / (e0 + e1), e1 / (e0 + e1)
        lb = (sm0 - sm0) if layer == 0 else ((sm0 + sm1) - sm0)
        forget = lb + (1.0 - lb) * _sigmoid(hf)
        logf = jnp.log(forget)
        kk = 1.0 - forget
        q = _silu(hq)
        c, nc = HGRN_CHUNK, HG_NC
        b = _dotx(tri_bd, logf)
        bl = _dotx(ones_bd, logf)
        split = lambda t: t.reshape(nc, c, DG)
        b4 = split(b)
        diff = (b4[:, :, None, :] - b4[:, None, :, :]).reshape(nc * c * c, DG)
        dec = jnp.exp(jnp.where(causal > 0.5, diff, -1e30))
        qrep = jnp.broadcast_to(split(q)[:, :, None, :], (nc, c, c, DG)).reshape(nc * c * c, DG)
        ktil = jnp.broadcast_to(split(kk)[:, None, :, :], (nc, c, c, DG)).reshape(nc * c * c, DG)
        vtil = jnp.broadcast_to(split(hi)[:, None, :, :], (nc, c, c, DG)).reshape(nc * c * c, DG)
        att = _dot(qrep * ktil * dec, seg)
        o_intra = jnp.sum((att * vtil).reshape(nc * c, c, DG), axis=1)
        kd4 = split(kk * jnp.exp(bl - b))
        qe4 = split(q * jnp.exp(b))
        v4 = split(hi)
        tot = jnp.exp(_dotx(first_row, bl))
        s, o_inter = sprev, []
        for ci in range(nc):
            o_inter.append(_dot_nt(qe4[ci], s))
            s = s * tot[ci:ci + 1, :] + _dot_tn(v4[ci], kd4[ci]) * bd
        o = o_intra + jnp.concatenate(o_inter, axis=0)
        ms = _dot(o * o, seg) * (1.0 / HD)
        y = o * lax.rsqrt(ms + RMS_EPS) * norm_w * _silu(hg)
        return y, s

    return fn


def _hgrn_consts(p):
    return [p["seg64"], p["seg64"], p["tri_chunks"], p["ones_chunks"], p["first_row"], p["causal_blk"]]


def hgrn_fwd(l, proj, p):
    fn = _hgrn_block_fn(l)

    def body(hq_ref, hf_ref, hi_ref, hg_ref, *rest):
        const_refs, (y_ref, st_ref, s_sc) = rest[:-3], rest[-3:]

        @pl.when(pl.program_id(0) == 0)
        def _():
            s_sc[...] = jnp.zeros_like(s_sc)

        sprev = s_sc[...]
        st_ref[0] = sprev
        y, snext = fn(hq_ref[...], hf_ref[...], hi_ref[...], hg_ref[...], sprev, *[r[...] for r in const_refs])
        y_ref[...] = y
        s_sc[...] = snext

    rows = lambda cb: pl.BlockSpec((HG_ROWS, DG), lambda i: (i, cb))
    ins = [proj, proj, proj, proj, p["lb0"], p["lb1"], p["hgrn_norm_w"]] + _hgrn_consts(p)
    specs = [rows(C_HQ // DG), rows(C_HF // DG), rows(C_HI // DG), rows(C_HG // DG)] + [_full(x.shape) for x in ins[4:]]
    return pl.pallas_call(body, grid=(SEQ // HG_ROWS,), in_specs=specs,
                          out_specs=[rows(0), pl.BlockSpec((1, DG, DG), lambda i: (i, 0, 0))],
                          out_shape=[_sds((SEQ, DG)), _sds((SEQ // HG_ROWS, DG, DG))],
                          scratch_shapes=[pltpu.VMEM((DG, DG), F32)], name=f"hgrn_fwd{l}",
                          compiler_params=_cp(("arbitrary",)))(*ins)


def hgrn_bwd(l, proj, states, dy, p, sibling=(), dy_col=0):
    fn = _hgrn_block_fn(l)
    nblk = SEQ // HG_ROWS
    n_const = len(_hgrn_consts(p))
    ns = len(sibling)

    def body(hq_ref, hf_ref, hi_ref, hg_ref, st_ref, dy_ref, lb0_ref, lb1_ref, nw_ref, *rest):
        const_refs, rest = rest[:n_const], rest[n_const:]
        dp_ref, dlb0_ref, dlb1_ref, dnw_ref = rest[ns:ns + 4]
        ds_sc = rest[2 * ns + 4]
        if ns:
            begin, end = _sibling_steps(rest[:ns], rest[ns + 4:2 * ns + 4], *rest[2 * ns + 5:])

            @pl.when(pl.program_id(0) == 0)
            def _():
                begin()

        @pl.when(pl.program_id(0) == 0)
        def _():
            ds_sc[...] = jnp.zeros_like(ds_sc)
            dlb0_ref[...] = jnp.zeros_like(dlb0_ref)
            dlb1_ref[...] = jnp.zeros_like(dlb1_ref)
            dnw_ref[...] = jnp.zeros_like(dnw_ref)

        consts = [r[...] for r in const_refs]
        f = lambda hq, hf, hi, hg, sp, b0, b1, nw: fn(hq, hf, hi, hg, sp, b0, b1, nw, *consts)
        _, vjp = jax.vjp(f, hq_ref[...], hf_ref[...], hi_ref[...], hg_ref[...], st_ref[0], lb0_ref[...], lb1_ref[...],
                         nw_ref[...])
        dhq, dhf, dhi, dhg, dsp, dlb0, dlb1, dnw = vjp((dy_ref[...], ds_sc[...]))
        dp_ref[:, 0:DG] = dhq
        dp_ref[:, DG:2 * DG] = dhf
        dp_ref[:, 2 * DG:3 * DG] = dhi
        dp_ref[:, 3 * DG:4 * DG] = dhg
        ds_sc[...] = dsp
        dlb0_ref[...] += dlb0
        dlb1_ref[...] += dlb1
        dnw_ref[...] += dnw

        if ns:
            @pl.when(pl.program_id(0) == nblk - 1)
            def _():
                end()

    rows = lambda cb: pl.BlockSpec((HG_ROWS, DG), lambda i: (nblk - 1 - i, cb))
    ins = [proj, proj, proj, proj, states, dy, p["lb0"], p["lb1"], p["hgrn_norm_w"]] + _hgrn_consts(p)
    specs = [rows(C_HQ // DG), rows(C_HF // DG), rows(C_HI // DG), rows(C_HG // DG),
             pl.BlockSpec((1, DG, DG), lambda i: (nblk - 1 - i, 0, 0)), rows(dy_col)] + [_full(x.shape) for x in ins[6:]]
    sem = pltpu.SemaphoreType.DMA((max(ns, 1), 4))
    outs = pl.pallas_call(body, grid=(nblk,), in_specs=specs + [ANY] * ns,
                          out_specs=[pl.BlockSpec((HG_ROWS, 4 * DG), lambda i: (nblk - 1 - i, 0)), _full((1, DG)),
                                     _full((1, DG)), _full((1, DG))] + [ANY] * ns,
                          out_shape=[_sds((SEQ, 4 * DG)), _sds((1, DG)), _sds((1, DG)), _sds((1, DG))]
                          + [_sds((4,) + a.shape[1:], a.dtype) for a in sibling],
                          scratch_shapes=[pltpu.VMEM((DG, DG), F32)] + ([sem, sem] if ns else []), name=f"hgrn_bwd{l}",
                          compiler_params=_cp(("arbitrary",)))(*ins, *sibling)
    return outs[:4], list(outs[4:])


def _ssd_chunk_fn(z, xs, bm, cm, dtr, sprev, dt_bias, a_log, d_par, norm_w, e128, tri, trit, seg128, ones128):
    lc = SSD_CHUNK
    dt = _softplus(dtr + dt_bias)
    a = -jnp.exp(a_log)
    da = dt * a * (lax.broadcasted_iota(jnp.int32, (1, 128), 1) < NH).astype(F32)
    cs = _dotx(tri, da)
    cst = _dotx_tn(da, trit)
    cs_b = _dotx(cs, e128)
    dt_b = _dotx(dt, e128)
    csl_b = _dotx(jnp.sum(da, axis=0, keepdims=True), e128)
    xdt = xs * dt_b
    lane = lax.broadcasted_iota(jnp.int32, (1, DG), 1)
    rowi = lax.broadcasted_iota(jnp.int32, (lc, lc), 0)
    coli = lax.broadcasted_iota(jnp.int32, (lc, lc), 1)
    y = jnp.zeros((lc, DG), F32)
    snew = jnp.zeros((DG, SSD_N), F32)
    d_b = jnp.zeros((1, DG), F32)
    wdec = xdt * jnp.exp(csl_b - cs_b)
    for g in range(2):
        bg = bm[:, g * SSD_N:(g + 1) * SSD_N]
        cg = cm[:, g * SSD_N:(g + 1) * SSD_N]
        gmat = _dot_nt(cg, bg)
        gmask = ((lane // 128) == g).astype(F32)
        snew = snew + _dot_tn(wdec * gmask, bg)
        y = y + _dot_nt(cg, sprev) * gmask * jnp.exp(cs_b)
        for hh in range(2):
            h = 2 * g + hh
            seg = jnp.where(rowi >= coli, cs[:, h:h + 1] - cst[h:h + 1, :], -1e30)
            hmask = ((lane // HD) == h).astype(F32)
            y = y + _dot(gmat * jnp.exp(seg), xdt * hmask)
            d_b = d_b + d_par[:, h:h + 1] * hmask
    cd = jnp.exp(_dotx_tn(_dotx(da, e128), ones128))
    snext = sprev * cd + snew
    y = y + xs * d_b
    y = y * _silu(z)
    ms = _dot(y * y, seg128) * (1.0 / 128.0)
    return y * lax.rsqrt(ms + RMS_EPS) * norm_w, snext


def ssd_fwd(l, proj, xc, p):
    nc = SEQ // SSD_CHUNK

    def body(z_ref, xs_ref, b_ref, c_ref, dt_ref, dtb_ref, al_ref, d_ref, nw_ref, e_ref, tri_ref, trit_ref, sg_ref,
             on_ref, y_ref, st_ref, s_sc):
        @pl.when(pl.program_id(0) == 0)
        def _():
            s_sc[...] = jnp.zeros_like(s_sc)

        sprev = s_sc[...]
        st_ref[0] = sprev
        y, snext = _ssd_chunk_fn(z_ref[...], xs_ref[...], b_ref[...], c_ref[...], dt_ref[...], sprev, dtb_ref[...],
                                 al_ref[...], d_ref[...], nw_ref[...], e_ref[...], tri_ref[...], trit_ref[...],
                                 sg_ref[...], on_ref[...])
        y_ref[...] = y
        s_sc[...] = snext

    rw = lambda w, cb: pl.BlockSpec((SSD_CHUNK, w), lambda i: (i, cb))
    ins = [proj, xc, xc, xc, proj, p["dt_bias"], p["a_log"], p["ssd_d"], p["ssd_norm_w"], p["e128"], p["tri128"],
           p["tri128t"], p["seg128"], p["ones128"]]
    specs = [rw(DG, C_Z // DG), rw(DG, 0), rw(DG, 1), rw(DG, 2), rw(128, C_DT // 128)] + [_full(x.shape) for x in ins[5:]]
    return pl.pallas_call(body, grid=(nc,), in_specs=specs,
                          out_specs=[rw(DG, 0), pl.BlockSpec((1, DG, SSD_N), lambda i: (i, 0, 0))],
                          out_shape=[_sds((SEQ, DG)), _sds((nc, DG, SSD_N))],
                          scratch_shapes=[pltpu.VMEM((DG, SSD_N), F32)], name=f"ssd_fwd{l}",
                          compiler_params=_cp(("arbitrary",)))(*ins)


def ssd_bwd(l, proj, xc, states, dy, p, dy_col=0):
    nc = SEQ // SSD_CHUNK

    def body(z_ref, xs_ref, b_ref, c_ref, dt_ref, st_ref, dy_ref, dtb_ref, al_ref, d_ref, nw_ref, e_ref, tri_ref,
             trit_ref, sg_ref, on_ref, dz_ref, dxc_ref, ddt_ref, ddtb_ref, dal_ref, dd_ref, dnw_ref, ds_sc):
        @pl.when(pl.program_id(0) == 0)
        def _():
            ds_sc[...] = jnp.zeros_like(ds_sc)
            ddtb_ref[...] = jnp.zeros_like(ddtb_ref)
            dal_ref[...] = jnp.zeros_like(dal_ref)
            dd_ref[...] = jnp.zeros_like(dd_ref)
            dnw_ref[...] = jnp.zeros_like(dnw_ref)

        consts = (e_ref[...], tri_ref[...], trit_ref[...], sg_ref[...], on_ref[...])
        f = lambda *a: _ssd_chunk_fn(*a, *consts)
        _, vjp = jax.vjp(f, z_ref[...], xs_ref[...], b_ref[...], c_ref[...], dt_ref[...], st_ref[0], dtb_ref[...],
                         al_ref[...], d_ref[...], nw_ref[...])
        dz, dxs, db, dc, ddt, dsp, ddtb, dal, dd, dnw = vjp((dy_ref[...], ds_sc[...]))
        dz_ref[...] = dz
        dxc_ref[:, 0:DG] = dxs
        dxc_ref[:, DG:2 * DG] = db
        dxc_ref[:, 2 * DG:3 * DG] = dc
        ddt_ref[...] = ddt
        ds_sc[...] = dsp
        ddtb_ref[...] += ddtb
        dal_ref[...] += dal
        dd_ref[...] += dd
        dnw_ref[...] += dnw

    rw = lambda w, cb: pl.BlockSpec((SSD_CHUNK, w), lambda i: (nc - 1 - i, cb))
    ins = [proj, xc, xc, xc, proj, states, dy, p["dt_bias"], p["a_log"], p["ssd_d"], p["ssd_norm_w"], p["e128"],
           p["tri128"], p["tri128t"], p["seg128"], p["ones128"]]
    specs = [rw(DG, C_Z // DG), rw(DG, 0), rw(DG, 1), rw(DG, 2), rw(128, C_DT // 128),
             pl.BlockSpec((1, DG, SSD_N), lambda i: (nc - 1 - i, 0, 0)), rw(DG, dy_col)] + [_full(x.shape) for x in ins[7:]]
    return pl.pallas_call(body, grid=(nc,), in_specs=specs,
                          out_specs=[rw(DG, 0), rw(3 * DG, 0), rw(128, 0), _full((1, 128)), _full((1, 128)), _full((1, 128)),
                                     _full((1, DG))],
                          out_shape=[_sds((SEQ, DG)), _sds((SEQ, 3 * DG)), _sds((SEQ, 128)), _sds((1, 128)), _sds((1, 128)),
                                     _sds((1, 128)), _sds((1, DG))],
                          scratch_shapes=[pltpu.VMEM((DG, SSD_N), F32)], name=f"ssd_bwd{l}",
                          compiler_params=_cp(("arbitrary",)))(*ins)


ATT_BLK = 128


def _att_geometry(dil):
    i = lax.broadcasted_iota(jnp.int32, (ATT_BLK, ATT_BLK), 0)
    j = lax.broadcasted_iota(jnp.int32, (ATT_BLK, ATT_BLK), 1)
    return ((i - j) * dil).astype(F32), ((ATT_BLK + i - j) * dil).astype(F32), j <= i, j >= i


def _att_scores(qn, kc, kp, h, geom, has_prev):
    dist_c, dist_p, m_c, m_pj = geom
    slope = 2.0 ** (-8.0 * (h + 1) / NH)
    scale = HD ** -0.5
    s_c = _dot_nt(qn, kc) * scale - slope * dist_c
    s_p = _dot_nt(qn, kp) * scale - slope * dist_p
    m_p = jnp.logical_and(m_pj, has_prev)
    return jnp.where(m_c, s_c, -1e30), jnp.where(m_p, s_p, -1e30), m_c, m_p


def _sub_spec(ln, width, col):
    return pl.BlockSpec((ln, DG), lambda z: (0, z * (width // DG) + col // DG))


QKV_W = 3 * DG


def attn_branch_fwd(l, bi, qkv, dil):
    ln = SEQ // dil
    nb = ln // ATT_BLK

    def body(q_ref, k_ref, v_ref, o_ref, l_ref):
        geom = _att_geometry(dil)

        def blk(n, carry):
            r0 = pl.multiple_of(n * ATT_BLK, ATT_BLK)
            rp = pl.multiple_of(jnp.maximum(n - 1, 0) * ATT_BLK, ATT_BLK)
            cur, prv = pl.ds(r0, ATT_BLK), pl.ds(rp, ATT_BLK)
            for h in range(NH):
                hs = slice(h * HD, (h + 1) * HD)
                qn, kc, vc, kp, vp = q_ref[cur, hs], k_ref[cur, hs], v_ref[cur, hs], k_ref[prv, hs], v_ref[prv, hs]
                s_c, s_p, m_c, m_p = _att_scores(qn, kc, kp, h, geom, n > 0)
                m = jnp.maximum(jnp.max(s_c, axis=1, keepdims=True), jnp.max(s_p, axis=1, keepdims=True))
                p_c = jnp.where(m_c, jnp.exp(s_c - m), 0.0)
                p_p = jnp.where(m_p, jnp.exp(s_p - m), 0.0)
                den = jnp.sum(p_c, axis=1, keepdims=True) + jnp.sum(p_p, axis=1, keepdims=True)
                o_ref[cur, hs] = (_dot(p_c, vc) + _dot(p_p, vp)) / den
                l_ref[cur, hs] = jnp.broadcast_to(m + jnp.log(den), (ATT_BLK, HD))
            return carry

        lax.fori_loop(0, nb, blk, 0)

    pv = qkv.reshape(ln, dil * QKV_W)
    out = pl.BlockSpec((ln, DG), lambda z: (0, z))
    o, lse = pl.pallas_call(body, grid=(dil,), in_specs=[_sub_spec(ln, QKV_W, 0), _sub_spec(ln, QKV_W, DG), _sub_spec(ln, QKV_W, 2 * DG)],
                            out_specs=[out, out], out_shape=[_sds((ln, dil * DG))] * 2, name=f"attn_fwd{l}_{bi}",
                            compiler_params=_cp(("parallel",)))(pv, pv, pv)
    return o.reshape(SEQ, DG), lse.reshape(SEQ, DG)


def attn_branch_bwd(l, bi, qkv, dil, dyb, lse_all, delta):
    ln = SEQ // dil
    nb = ln // ATT_BLK
    scale = HD ** -0.5

    def body(q_ref, k_ref, v_ref, do_ref, l_ref, dl_ref, dq_ref, dk_ref, dv_ref):
        dk_ref[...] = jnp.zeros_like(dk_ref)
        dv_ref[...] = jnp.zeros_like(dv_ref)
        geom = _att_geometry(dil)

        def blk(n, carry):
            r0 = pl.multiple_of(n * ATT_BLK, ATT_BLK)
            rp = pl.multiple_of(jnp.maximum(n - 1, 0) * ATT_BLK, ATT_BLK)
            cur, prv = pl.ds(r0, ATT_BLK), pl.ds(rp, ATT_BLK)
            for h in range(NH):
                hs = slice(h * HD, (h + 1) * HD)
                qn, don = q_ref[cur, hs], do_ref[cur, hs]
                lse, dlt = l_ref[cur, h * HD:h * HD + 1], dl_ref[cur, h * HD:h * HD + 1]
                kc, vc, kp, vp = k_ref[cur, hs], v_ref[cur, hs], k_ref[prv, hs], v_ref[prv, hs]
                s_c, s_p, m_c, m_p = _att_scores(qn, kc, kp, h, geom, n > 0)
                p_c = jnp.where(m_c, jnp.exp(s_c - lse), 0.0)
                p_p = jnp.where(m_p, jnp.exp(s_p - lse), 0.0)
                ds_c = p_c * (_dot_nt(don, vc) - dlt)
                ds_p = p_p * (_dot_nt(don, vp) - dlt)
                dq_ref[cur, hs] = (_dot(ds_c, kc) + _dot(ds_p, kp)) * scale
                dv_ref[prv, hs] += _dot_tn(p_p, don)
                dk_ref[prv, hs] += _dot_tn(ds_p, qn) * scale
                dv_ref[cur, hs] += _dot_tn(p_c, don)
                dk_ref[cur, hs] += _dot_tn(ds_c, qn) * scale
            return carry

        lax.fori_loop(0, nb, blk, 0)

    pv = qkv.reshape(ln, dil * QKV_W)
    sub = lambda t: t.reshape(ln, dil * DG)
    row = pl.BlockSpec((ln, DG), lambda z: (0, z))
    outs = pl.pallas_call(body, grid=(dil,),
                          in_specs=[_sub_spec(ln, QKV_W, 0), _sub_spec(ln, QKV_W, DG), _sub_spec(ln, QKV_W, 2 * DG), row, row, row],
                          out_specs=[row] * 3, out_shape=[_sds((ln, dil * DG))] * 3, name=f"attn_bwd{l}_{bi}",
                          compiler_params=_cp(("parallel",)))(pv, pv, pv, sub(dyb), sub(lse_all), sub(delta))
    return [t.reshape(SEQ, DG) for t in outs]


def _attn_merge_fn(o1, o2, o3, l1, l2, l3):
    m = jnp.maximum(jnp.maximum(l1, l2), l3)
    w1, w2, w3 = jnp.exp(l1 - m), jnp.exp(l2 - m), jnp.exp(l3 - m)
    den = w1 + w2 + w3
    return (w1 * o1 + w2 * o2 + w3 * o3) / den, m + jnp.log(den)


def attn_merge(l, os_, ls_):
    ins = list(os_) + list(ls_)
    return _map_fwd(f"attn_merge{l}", _attn_merge_fn, (SEQ // RB,), ins, [_rows(DG)] * 6, [_sds((SEQ, DG))] * 2,
                    [_rows(DG)] * 2)


def attn_delta(l, dyb, yb, seg):
    fn = lambda d, y, s: (_dot(d * y, s),)
    return _map_fwd(f"attn_delta{l}", fn, (SEQ // RB,), [dyb, yb, seg], [_rows(DG), _rows(DG), _full((DG, DG))],
                    [_sds((SEQ, DG))], [_rows(DG)])[0]


def _ln_fn(x, mix, w, b):
    h = ALPHA * x + mix
    mu = jnp.mean(h, axis=-1, keepdims=True)
    d = h - mu
    var = jnp.mean(d * d, axis=-1, keepdims=True)
    return (d * lax.rsqrt(var + LN_EPS) * w + b,)


def ln_fwd(name, x, mix, w, b):
    specs = [_rows(D_MODEL), _rows(D_MODEL), _full((1, D_MODEL)), _full((1, D_MODEL))]
    return _map_fwd(name, _ln_fn, (SEQ // RB,), [x, mix, w, b], specs, [_sds((SEQ, D_MODEL))], [_rows(D_MODEL)])[0]


def ln_bwd(name, x, mix, w, b, dy):
    specs = [_rows(D_MODEL), _rows(D_MODEL), _full((1, D_MODEL)), _full((1, D_MODEL))]
    return _map_bwd(name, _ln_fn, (SEQ // RB,), [x, mix, w, b], specs, [[dy]], [[_rows(D_MODEL)]], want=[1, 2, 3],
                    acc=(2, 3))


def loss_call(y, tgt):
    def fn(yy, tt):
        e = yy - tt
        part = 0.5 * jnp.sum(jnp.sum(e * e, axis=-1, keepdims=True) * (1.0 / D_MODEL), axis=0, keepdims=True)
        return e * (1.0 / D_MODEL), jnp.broadcast_to(part, (8, 128))

    return _map_fwd("loss", fn, (SEQ // RB,), [y, tgt], [_rows(D_MODEL)] * 2,
                    [_sds((SEQ, D_MODEL)), _sds((SEQ // RB * 8, 128))],
                    [_rows(D_MODEL), pl.BlockSpec((8, 128), lambda i: (i, 0))])


LATE_KEYS = ("w_out", "w_up_t", "w_down")


def _full_rows(g):
    return g.reshape(N_DEV * g.shape[1], g.shape[2])


def layer_fwd(l, x, vfirst, wts, p, gather=(), late=False):
    sv = {"x": x}
    proj = _mm(f"mm_in{l}", x, wts["w_in"], "nn", *MM_TILES["k1024"])
    fl = lerp_fwd(l, proj, p["mu"])
    xc = conv_fwd(l, proj, p["conv_w"], p["conv_b"])
    w, k2, v2, c, b, g = rwkv_pre_fwd(l, fl, vfirst, p)
    y_scan, states, sv["gathered"] = rwkv_scan_fwd(l, fl, w, k2, v2, c, b, p, gather)
    if late:
        wts = dict(wts, **dict(zip(LATE_KEYS, [_full_rows(g) for g in sv["gathered"][:3]])))
    sv["wts"] = wts
    ya = rwkv_post_fwd(l, y_scan, fl, k2, v2, g, p)
    qkv = proj[:, C_AQ:C_AQ + 3 * DG]
    outs, lses = [], []
    for bi, (win, dil) in enumerate(DILATED):
        o, lse = attn_branch_fwd(l, bi, qkv, dil)
        outs.append(o)
        lses.append(lse)
    yb, lse_all = attn_merge(l, outs, lses)
    yc, ssd_states = ssd_fwd(l, proj, xc, p)
    yd, hg_states = hgrn_fwd(l, proj, p)
    ycat = jnp.concatenate([ya, yb, yc, yd], axis=1).astype(BF16)
    mix = _mm(f"mm_out{l}", ycat, wts["w_out"], "nn", *MM_TILES["k1024"])
    x1 = ln_fwd(f"ln1_fwd{l}", x, mix, p["ln1_w"], p["ln1_b"])
    hh = _mm(f"mm_up{l}", x1, wts["w_up_t"], "nt", *MM_TILES["k1024"], epilogue="relu2")
    m2 = _mm(f"mm_down{l}", hh, wts["w_down"], "nn", *MM_TILES["k4096"])
    x2 = ln_fwd(f"ln2_fwd{l}", x1, m2, p["ln2_w"], p["ln2_b"])
    sv.update(proj=proj, fl=fl, xc=xc, w=w, k2=k2, v2=v2, c=c, b=b, g=g, y_scan=y_scan, states=states,
              yb=yb, lse_all=lse_all, ssd_states=ssd_states, hg_states=hg_states, ycat=ycat, mix=mix, x1=x1, hh=hh, qkv=qkv,
              m2=m2, vfirst=vfirst)
    return x2, sv


def layer_bwd(l, dx2, dvfirst_next, sv, wts, p, exchange=(), reducer=None):
    gr = {}
    x, x1, proj, fl = sv["x"], sv["x1"], sv["proj"], sv["fl"]
    dres2, gr["ln2_w"], gr["ln2_b"] = ln_bwd(f"ln2_bwd{l}", x1, sv["m2"], p["ln2_w"], p["ln2_b"], dx2)
    du = _mm(f"mm_down_dx{l}", dres2, wts["w_down"], "nt", *MM_TILES["k1024"], add=sv["hh"], epilogue="relu2_bwd",
             out_dtype=BF16)
    gr["w_down"] = _mm(f"mm_down_dw{l}", sv["hh"], dres2, "tn", *MM_TILES["wgrad_tall"])
    dx1 = _mm(f"mm_up_dx{l}", du, wts["w_up_t"], "nn", *MM_TILES["k4096"], add=dres2, add_scale=ALPHA)
    gr["w_up_t"] = _mm(f"mm_up_dw{l}", du, x1, "tn", *MM_TILES["wgrad_tall"])
    dres1, gr["ln1_w"], gr["ln1_b"] = ln_bwd(f"ln1_bwd{l}", x, sv["mix"], p["ln1_w"], p["ln1_b"], dx1)
    dycat = _mm(f"mm_out_dx{l}", dres1, wts["w_out"], "nt", *MM_TILES["k1024"])
    gr["w_out"] = _mm(f"mm_out_dw{l}", sv["ycat"], dres1, "tn", 1024, 1024, 512)
    dyb = dycat[:, DG:2 * DG]
    send = [_owner_blocks(gr[k]) for k in LATE_KEYS] if reducer else []
    (dhg4, gr["lb0"], gr["lb1"], gr["hgrn_norm_w"]), sib = hgrn_bwd(l, proj, sv["hg_states"], dycat, p, send, dy_col=3)
    if reducer:
        gr["early_own"], early_parts = reducer(f"{l}a", send, sib)
        exchange = list(exchange) + list(early_parts)
    dz, dxc, ddt, gr["dt_bias"], gr["a_log"], gr["ssd_d"], gr["ssd_norm_w"] = ssd_bwd(l, proj, sv["xc"], sv["ssd_states"], dycat, p, dy_col=2)
    dxbc, gr["conv_w"], gr["conv_b"] = conv_bwd(l, proj, p["conv_w"], p["conv_b"], dxc)
    delta = attn_delta(l, dyb, sv["yb"], p["seg64"])
    dqs, dks, dvs = [], [], []
    for bi, (win, dil) in enumerate(DILATED):
        dq, dk, dv = attn_branch_bwd(l, bi, sv["qkv"], dil, dyb, sv["lse_all"], delta)
        dqs.append(dq)
        dks.append(dk)
        dvs.append(dv)
    dq_a, dk_a, dv_a = _addn(f"attn_dq{l}", *dqs), _addn(f"attn_dk{l}", *dks), _addn(f"attn_dv{l}", *dvs)
    pg = rwkv_post_bwd(l, sv["y_scan"], fl, sv["k2"], sv["v2"], sv["g"], p, dycat)
    gr["lnx_w"], gr["lnx_b"], gr["r_k"] = pg["lnx_w"], pg["lnx_b"], pg["r_k"]
    (dr, dw, dk, dv, dc, db), gr["exchanged"] = rwkv_scan_bwd(l, fl, sv["w"], sv["k2"], sv["v2"], sv["c"], sv["b"],
                                                              sv["states"], pg["y"], p, exchange)
    v2_cts = [dv, pg["v2"]] + ([dvfirst_next] if dvfirst_next is not None else [])
    qg = rwkv_pre_bwd(l, fl, sv["vfirst"], p, [[dw], [dk, pg["k2"]], v2_cts, [dc], [db], [pg["g"]]])
    for nme in ("w0", "w2p", "a0", "a2p", "g2p", "k_k", "k_a", "v0", "v2p"):
        if nme in qg:
            gr[nme] = qg[nme]
    dfr = _addn(f"rwkv_dr{l}", dr, pg["fr"])
    dvres = qg["fvres"] if l > 0 else jnp.zeros((SEQ, 128), F32)
    dfl_out = jnp.concatenate([dfr, qg["fk"], qg["fv"], qg["flora"], dvres], axis=1)
    dfl_in, gr["mu"] = lerp_bwd(l, proj, p["mu"], dfl_out)
    dproj = jnp.concatenate([dfl_in[:, 0:768], dq_a, dk_a, dv_a, dz, dxbc, dhg4, dfl_in[:, 768:896], ddt,
                             dfl_in[:, 896:1024], jnp.zeros((SEQ, 128), F32)], axis=1).astype(BF16)
    dx = _mm(f"mm_in_dx{l}", dproj, wts["w_in"], "nt", *MM_TILES["k4096"], add=dres1, add_scale=ALPHA)
    gr["w_in"] = _mm(f"mm_in_dw{l}", x, dproj, "tn", *MM_TILES["wgrad_wide"])
    return dx, (qg["vfirst"] if l > 0 else None), gr


def _w_in_pad(w_in_l, w_vres):
    rows = w_in_l.shape[0]
    z = lambda n: jnp.zeros((rows, n), w_in_l.dtype)
    vres = z(128) if w_vres is None else jnp.concatenate([w_vres, z(96)], axis=1)
    return jnp.concatenate([w_in_l[:, 0:768], w_in_l[:, 896:1664], w_in_l[:, 1664:1920], w_in_l[:, 1920:2688],
                            w_in_l[:, 2692:3716], w_in_l[:, 768:896], w_in_l[:, 2688:2692], z(124), vres, z(128)], axis=1)


def _w_in_unpad(g):
    g_in = jnp.concatenate([g[:, 0:768], g[:, C_LORA:C_LORA + 128], g[:, 768:1536], g[:, C_Z:C_Z + 256],
                            g[:, C_XBC:C_XBC + 768], g[:, C_DT:C_DT + 4], g[:, C_HQ:C_HQ + 1024]], axis=1)
    return g_in, g[:, C_VRES:C_VRES + 32]


def _consts():
    pair = jnp.arange(HG_NC * HGRN_CHUNK * HGRN_CHUNK)
    i128 = jnp.arange(128)
    ihg = jnp.arange(HG_ROWS)
    same_chunk = (ihg[:, None] // HGRN_CHUNK) == (ihg[None, :] // HGRN_CHUNK)
    seg64 = _seg_ones(DG, HD)
    tri128 = (i128[:, None] >= i128[None, :]).astype(F32)
    return dict(
        seg64=seg64, seg64x3_bf16=jnp.concatenate([seg64, seg64, seg64], axis=0).astype(BF16),
        dmask=(jnp.arange(HD)[:, None] == (jnp.arange(DG)[None, :] % HD)).astype(F32),
        tri_chunks=(same_chunk & (ihg[:, None] >= ihg[None, :])).astype(F32), ones_chunks=same_chunk.astype(F32),
        first_row=(ihg[None, :] == (jnp.arange(HG_NC) * HGRN_CHUNK)[:, None]).astype(F32),
        causal_blk=jnp.broadcast_to((((pair // HGRN_CHUNK) % HGRN_CHUNK) >= (pair % HGRN_CHUNK)).astype(F32)[:, None],
                                    (HG_NC * HGRN_CHUNK * HGRN_CHUNK, DG)),
        e128=((i128[:, None] == (jnp.arange(DG)[None, :] // HD)) & (i128[:, None] < NH)).astype(F32),
        tri128=tri128, tri128t=tri128.T, seg128=_seg_ones(DG, 128), ones128=jnp.ones((128, 128), F32))


def _pad_lanes(v, n):
    return jnp.concatenate([v, jnp.zeros((n - v.shape[0],), v.dtype)])[None, :]


def _layer_params(l, raw, consts):
    p = dict(consts)
    row = lambda name: raw[name][l][None, :]
    z = lambda r: jnp.zeros((r, DG), F32)
    mu_vres = raw["mu_vres"][l - 1] if l > 0 else jnp.zeros((32,), F32)
    p["mu"] = jnp.concatenate([raw["mu_shift"][l], mu_vres, jnp.zeros((96,), F32)])[None, :]
    p["conv_w"], p["conv_b"] = raw["ssd_conv_w"][l], row("ssd_conv_b")
    p["w0"], p["a0"], p["k_k"], p["k_a"] = row("rwkv_w0"), row("rwkv_a0"), row("rwkv_k_k"), row("rwkv_k_a")
    p["lnx_w"], p["lnx_b"] = row("rwkv_lnx_w"), row("rwkv_lnx_b")
    p["r_k"] = raw["rwkv_r_k"][l].reshape(1, DG)
    p["w2p"] = jnp.concatenate([raw["rwkv_w2"][l], z(96)], axis=0)
    p["a2p"] = jnp.concatenate([z(32), raw["rwkv_a2"][l], z(64)], axis=0)
    p["g2p"] = jnp.concatenate([z(64), raw["rwkv_g2"][l]], axis=0)
    if l > 0:
        p["v0"] = raw["rwkv_v0"][l - 1][None, :]
        p["v2p"] = jnp.concatenate([raw["rwkv_v2"][l - 1], z(96)], axis=0)
    p["lb0"], p["lb1"] = raw["lower_bounds"][0:1], raw["lower_bounds"][1:2]
    p["hgrn_norm_w"], p["ssd_norm_w"] = row("hgrn_norm_w"), row("ssd_norm_w")
    p["dt_bias"], p["a_log"], p["ssd_d"] = (_pad_lanes(raw[n][l], 128) for n in ("ssd_dt_bias", "ssd_A_log", "ssd_D"))
    for n in ("ln1_w", "ln1_b", "ln2_w", "ln2_b"):
        p[n] = row(n)
    return p


def _natural_grads(g0, g1):
    gs = (g0, g1)
    st = lambda key, f=lambda a: a[0]: jnp.stack([f(g[key]) for g in gs])
    out = {}
    out["lower_bounds"] = jnp.concatenate([g0["lb0"] + g1["lb0"], g0["lb1"] + g1["lb1"]], axis=0)
    out["mu_shift"] = st("mu", lambda a: a[0, :896])
    out["mu_vres"] = g1["mu"][:, 896:928]
    out["rwkv_w0"], out["rwkv_a0"], out["rwkv_k_k"], out["rwkv_k_a"] = st("w0"), st("a0"), st("k_k"), st("k_a")
    out["rwkv_w2"] = st("w2p", lambda a: a[0:32])
    out["rwkv_a2"] = st("a2p", lambda a: a[32:64])
    out["rwkv_g2"] = st("g2p", lambda a: a[64:128])
    out["rwkv_r_k"] = st("r_k", lambda a: a.reshape(NH, HD))
    out["rwkv_lnx_w"], out["rwkv_lnx_b"] = st("lnx_w"), st("lnx_b")
    out["rwkv_v0"] = g1["v0"]
    out["rwkv_v2"] = g1["v2p"][None, 0:32]
    out["ssd_conv_w"] = st("conv_w", lambda a: a)
    out["ssd_conv_b"] = st("conv_b")
    out["ssd_dt_bias"], out["ssd_A_log"], out["ssd_D"] = (st(k, lambda a: a[0, :NH]) for k in ("dt_bias", "a_log", "ssd_d"))
    out["ssd_norm_w"], out["hgrn_norm_w"] = st("ssd_norm_w"), st("hgrn_norm_w")
    for n in ("ln1_w", "ln1_b", "ln2_w", "ln2_b"):
        out[n] = st(n)
    return out


MESH_T = pl.DeviceIdType.MESH
ANY = pl.BlockSpec(memory_space=pl.ANY)


def _dev_index(px, py, pc):
    return 4 * px + 2 * py + pc


def all_gather(arrs):
    n = len(arrs)

    def body(*refs):
        begin, middle, end = _gather_steps(refs[:n], refs[n:2 * n], *refs[2 * n:])
        begin()
        middle()
        end()

    return pl.pallas_call(body, in_specs=[ANY] * n, out_specs=[ANY] * n, out_shape=_gather_shapes(arrs),
                          scratch_shapes=_gather_sems(n), name="all_gather")(*arrs)


def _gather_shapes(arrs):
    return [_sds((N_DEV,) + a.shape, a.dtype) for a in arrs]


def _gather_sems(n):
    return [pltpu.SemaphoreType.DMA((n, 7)), pltpu.SemaphoreType.DMA((n, 7)), pltpu.SemaphoreType.DMA((n,))]


def _gather_steps(ins, outs, send_sems, recv_sems, local_sems):
    n = len(ins)
    x, y, c = lax.axis_index("x"), lax.axis_index("y"), lax.axis_index("c")
    me, sibling = (x, y, c), (x, y, 1 - c)
    chips = [(1 - x, y), (x, 1 - y), (1 - x, 1 - y)]

    def copy(a, k, block, to, src=None):
        slot = outs[a].at[_dev_index(*block)]
        return pltpu.make_async_remote_copy(src_ref=slot if src is None else src, dst_ref=slot,
                                            send_sem=send_sems.at[a, k], recv_sem=recv_sems.at[a, k],
                                            device_id=to, device_id_type=MESH_T)

    def own_copies():
        mine = [pltpu.make_async_copy(ins[a], outs[a].at[_dev_index(*me)], local_sems.at[a]) for a in range(n)]
        first = []
        for a in range(n):
            first.append(copy(a, 0, me, sibling, src=ins[a]))
            first += [copy(a, 1 + j, me, (*chip, c), src=ins[a]) for j, chip in enumerate(chips)]
        return mine, first

    def begin():
        mine, first = own_copies()
        for cp in mine + first:
            cp.start()

    def passed_on():
        return [copy(a, 4 + j, (*chip, c), sibling) for j, chip in enumerate(chips) for a in range(n)]

    def middle():
        for j, chip in enumerate(chips):
            for a in range(n):
                copy(a, 1 + j, (*chip, c), me).wait_recv()
        for cp in passed_on():
            cp.start()

    def end():
        mine, first = own_copies()
        for a in range(n):
            copy(a, 0, sibling, me).wait_recv()
            for j, chip in enumerate(chips):
                copy(a, 4 + j, (*chip, 1 - c), me).wait_recv()
        for cp in first + passed_on():
            cp.wait_send()
        for cp in mine:
            cp.wait()

    return begin, middle, end


def _chips(x, y):
    return [(x, y), (1 - x, y), (x, 1 - y), (1 - x, 1 - y)]


def _sibling_steps(ins, sib, send_sems, recv_sems):
    x, y, c = lax.axis_index("x"), lax.axis_index("y"), lax.axis_index("c")

    def copies():
        return [pltpu.make_async_remote_copy(src_ref=ins[a].at[_dev_index(cx, cy, 1 - c)], dst_ref=sib[a].at[k],
                                             send_sem=send_sems.at[a, k], recv_sem=recv_sems.at[a, k],
                                             device_id=(x, y, 1 - c), device_id_type=MESH_T)
                for a in range(len(ins)) for k, (cx, cy) in enumerate(_chips(x, y))]

    def begin():
        for cp in copies():
            cp.start()

    def end():
        cps = copies()
        for cp in cps:
            cp.wait_recv()
        for cp in cps:
            cp.wait_send()

    return begin, end


def exchange_siblings(arrs, name):
    n = len(arrs)

    def body(*refs):
        begin, end = _sibling_steps(refs[:n], refs[n:2 * n], *refs[2 * n:])
        begin()
        end()

    sem = pltpu.SemaphoreType.DMA((n, 4))
    return pl.pallas_call(body, in_specs=[ANY] * n, out_specs=[ANY] * n,
                          out_shape=[_sds((4,) + a.shape[1:], a.dtype) for a in arrs],
                          scratch_shapes=[sem, sem], name=name)(*arrs)


def reduce_pair(name, send, slots, sib, wire_dtype):
    _, r, c = send.shape
    rb = min(r, 262144 // c)

    def body(slots_ref, m0, m1, m2, m3, s_ref, own_ref, part_ref):
        own_ref[...] = m0[...] + s_ref[0]
        for k, m_ref in enumerate((m1, m2, m3)):
            part_ref[k] = (m_ref[...] + s_ref[k + 1]).astype(wire_dtype)

    mine = [pl.BlockSpec((None, rb, c), lambda i, s, k=k: (s[k], i, 0)) for k in range(4)]
    grid_spec = pltpu.PrefetchScalarGridSpec(
        num_scalar_prefetch=1, grid=(r // rb,),
        in_specs=mine + [pl.BlockSpec((4, rb, c), lambda i, s: (0, i, 0))],
        out_specs=[pl.BlockSpec((rb, c), lambda i, s: (i, 0)), pl.BlockSpec((3, rb, c), lambda i, s: (0, i, 0))])
    return pl.pallas_call(body, grid_spec=grid_spec, out_shape=[_sds((r, c)), _sds((3, r, c), wire_dtype)], name=name,
                          compiler_params=_cp(("parallel",)))(slots, send, send, send, send, sib)


def _chip_exchange_steps(ins, recv, send_sems, recv_sems):
    x, y, c = lax.axis_index("x"), lax.axis_index("y"), lax.axis_index("c")

    def copies():
        return [pltpu.make_async_remote_copy(src_ref=ins[a].at[k], dst_ref=recv[a].at[k], send_sem=send_sems.at[a, k],
                                             recv_sem=recv_sems.at[a, k], device_id=(cx, cy, c), device_id_type=MESH_T)
                for a in range(len(ins)) for k, (cx, cy) in enumerate(_chips(x, y)[1:])]

    def begin():
        for cp in copies():
            cp.start()

    def end():
        cps = copies()
        for cp in cps:
            cp.wait_recv()
        for cp in cps:
            cp.wait_send()

    return begin, end


def exchange_chips(parts, rep):
    n = len(parts)

    def body(*refs):
        ins, rep_ref = refs[:n], refs[n]
        recv, rep_all = refs[n + 1:2 * n + 1], refs[2 * n + 1]
        send_sems, recv_sems, rsend_sems, rrecv_sems, local_sem = refs[2 * n + 2:]
        x, y, c = lax.axis_index("x"), lax.axis_index("y"), lax.axis_index("c")
        me = _dev_index(x, y, c)
        mine = pltpu.make_async_copy(rep_ref, rep_all.at[me], local_sem)
        mine.start()
        begin, end = _chip_exchange_steps(ins, recv, send_sems, recv_sems)
        begin()
        rels = [(rx, ry, rc) for rx in (0, 1) for ry in (0, 1) for rc in (0, 1)][1:]
        peers = [(jnp.where(rx, 1 - x, x), jnp.where(ry, 1 - y, y), jnp.where(rc, 1 - c, c)) for rx, ry, rc in rels]
        rcps = []
        for k, peer in enumerate(peers):
            cp = pltpu.make_async_remote_copy(src_ref=rep_ref, dst_ref=rep_all.at[me], send_sem=rsend_sems.at[k],
                                              recv_sem=rrecv_sems.at[k], device_id=peer, device_id_type=MESH_T)
            cp.start()
            rcps.append(cp)
        for k, peer in enumerate(peers):
            pltpu.make_async_remote_copy(src_ref=rep_ref, dst_ref=rep_all.at[_dev_index(*peer)], send_sem=rsend_sems.at[k],
                                         recv_sem=rrecv_sems.at[k], device_id=peer, device_id_type=MESH_T).wait_recv()
        end()
        for cp in rcps:
            cp.wait_send()
        mine.wait()

    outs = pl.pallas_call(
        body, in_specs=[ANY] * (n + 1), out_specs=[ANY] * (n + 1),
        out_shape=[_sds(a.shape, a.dtype) for a in parts] + [_sds((N_DEV,) + rep.shape, rep.dtype)],
        scratch_shapes=[pltpu.SemaphoreType.DMA((n, 3)), pltpu.SemaphoreType.DMA((n, 3)), pltpu.SemaphoreType.DMA((7,)),
                        pltpu.SemaphoreType.DMA((7,)), pltpu.SemaphoreType.DMA],
        name="exchange_chips")(*parts, rep)
    return outs[:n], outs[n]


def adamw(name, terms, w, m, v, transposed=False):
    r, c = w.shape[::-1] if transposed else w.shape
    rb = r if transposed else min(r, 262144 // c)
    c1 = 1.0 - ADAM_B1 ** ADAM_STEP
    c2 = 1.0 - ADAM_B2 ** ADAM_STEP
    nt = len(terms)

    def body(*refs):
        w_ref, m_ref, v_ref = refs[nt:nt + 3]
        g_ref, d_ref, nm_ref, nv_ref = refs[nt + 3:]
        g = refs[0][...].astype(F32)
        for t_ref in refs[1:nt]:
            g = g + t_ref[...].astype(F32)
        if transposed:
            g = g.T
        nm = ADAM_B1 * m_ref[...] + (1.0 - ADAM_B1) * g
        nv = ADAM_B2 * v_ref[...] + (1.0 - ADAM_B2) * (g * g)
        g_ref[...] = g
        nm_ref[...] = nm
        nv_ref[...] = nv
        d_ref[...] = -ADAM_LR * ((nm / c1) / (jnp.sqrt(nv / c2) + ADAM_EPS) + ADAM_WD * w_ref[...])

    blk = pl.BlockSpec((rb, c), lambda i: (i, 0))
    wblk = pl.BlockSpec((c, r), lambda i: (0, 0)) if transposed else blk
    tspecs = [blk if k is None else pl.BlockSpec((None, rb, c), lambda i, k=k: (k, i, 0)) for _, k in terms]
    return pl.pallas_call(body, grid=(r // rb,), in_specs=tspecs + [wblk] * 3, out_specs=[wblk] * 4,
                          out_shape=[_sds(w.shape)] * 4, name=name,
                          compiler_params=_cp(("parallel",)))(*[t for t, _ in terms], w, m, v)


W_IN_PIECES = ((0, 768, 0), (768, 896, C_LORA), (896, 1664, 768), (1664, 1920, C_Z), (1920, 2688, C_XBC),
               (2688, 2692, C_DT), (2692, 3716, C_HQ))
VRES_W = 32


def adamw_w_in(name, terms, w, m, v, vres=None):
    nt, nv = len(terms), 3 if vres else 0
    c1 = 1.0 - ADAM_B1 ** ADAM_STEP
    c2 = 1.0 - ADAM_B2 ** ADAM_STEP

    def body(*refs):
        w_ref, m_ref, v_ref = refs[nt:nt + 3]
        vres_refs = refs[nt + 3:nt + 3 + nv]
        outs = refs[nt + 3 + nv:nt + 7 + nv]
        vres_outs = refs[nt + 7 + nv:]
        g_all = refs[0][...].astype(F32)
        for t_ref in refs[1:nt]:
            g_all = g_all + t_ref[...].astype(F32)

        def update(g, wmv, out_refs, cols):
            nm = ADAM_B1 * wmv[1][:, cols] + (1.0 - ADAM_B1) * g
            nv_ = ADAM_B2 * wmv[2][:, cols] + (1.0 - ADAM_B2) * (g * g)
            out_refs[0][:, cols] = g
            out_refs[1][:, cols] = -ADAM_LR * ((nm / c1) / (jnp.sqrt(nv_ / c2) + ADAM_EPS) + ADAM_WD * wmv[0][:, cols])
            out_refs[2][:, cols] = nm
            out_refs[3][:, cols] = nv_

        for lo, hi, src in W_IN_PIECES:
            update(g_all[:, src:src + hi - lo], (w_ref, m_ref, v_ref), outs, slice(lo, hi))
        if vres:
            update(g_all[:, C_VRES:C_VRES + VRES_W], vres_refs, vres_outs, slice(0, VRES_W))

    r, c = terms[0][0].shape[-2:]
    tspecs = [_full((r, c)) if k is None else pl.BlockSpec((None, r, c), lambda i, k=k: (k, 0, 0)) for _, k in terms]
    wspec, vspec = _full(w.shape), _full((w.shape[0], VRES_W))
    outs = pl.pallas_call(body, grid=(1,), in_specs=tspecs + [wspec] * 3 + [vspec] * nv,
                          out_specs=[wspec] * 4 + [vspec] * (4 if vres else 0),
                          out_shape=[_sds(w.shape)] * 4 + [_sds((w.shape[0], VRES_W))] * (4 if vres else 0), name=name,
                          compiler_params=_cp(("arbitrary",)))(*[t for t, _ in terms], w, m, v, *(vres or ()))
    return list(outs[:4]), list(outs[4:])


SMS_ROWS = 16
N_BIG = 8
SMALL_SHARDED = (("rwkv_w2", (2, 32, 32)), ("rwkv_a2", (2, 32, 32)), ("rwkv_g2", (2, 64, 32)), ("rwkv_v2", (1, 32, 32)),
                 ("ssd_conv_w", (2, 4, 96)))
REPLICATED = (("lower_bounds", (2, 256)), ("mu_shift", (2, 896)), ("mu_vres", (1, 32)), ("rwkv_w0", (2, 256)),
              ("rwkv_a0", (2, 256)), ("rwkv_k_k", (2, 256)), ("rwkv_k_a", (2, 256)), ("rwkv_r_k", (2, 4, 64)),
              ("rwkv_lnx_w", (2, 256)), ("rwkv_lnx_b", (2, 256)), ("rwkv_v0", (1, 256)), ("ssd_conv_b", (2, 768)),
              ("ssd_dt_bias", (2, 4)), ("ssd_A_log", (2, 4)), ("ssd_D", (2, 4)), ("ssd_norm_w", (2, 256)),
              ("hgrn_norm_w", (2, 256)), ("ln1_w", (2, 1024)), ("ln1_b", (2, 1024)), ("ln2_w", (2, 1024)),
              ("ln2_b", (2, 1024)))


def _flat_rows(parts, rows):
    flat = jnp.concatenate([a.reshape(-1) for a in parts])
    return jnp.concatenate([flat, jnp.zeros((rows * PACK_W - flat.shape[0],), flat.dtype)]).reshape(rows, PACK_W)


def _local_arrays(d):
    return [_w_in_pad(d["w_in"][0], None), _w_in_pad(d["w_in"][1], d["w_in_vres"][0]), d["w_out"][0], d["w_out"][1],
            d["w_up"][0], d["w_up"][1], d["w_down"][0], d["w_down"][1],
            _flat_rows([d[n] for n, _ in SMALL_SHARDED], SMS_ROWS)]


def _unflat(rows2d, table):
    flat, out, o = rows2d.reshape(-1), {}, 0
    for name, shape in table:
        n = 1
        for s in shape:
            n *= s
        out[name] = flat[o:o + n].reshape(shape)
        o += n
    return out


def _from_local_arrays(arrs, rep, w_in_vres):
    d = dict(rep)
    d["w_in"], d["w_in_vres"] = jnp.stack([arrs[0], arrs[1]]), w_in_vres[None]
    d["w_out"] = jnp.stack([arrs[2], arrs[3]])
    d["w_up"] = jnp.stack([arrs[4], arrs[5]])
    d["w_down"] = jnp.stack([arrs[6], arrs[7]])
    d.update(_unflat(arrs[8], SMALL_SHARDED))
    return d


def _small_sharded_full(gs):
    small, flat, o = {}, gs.reshape(N_DEV, -1), 0
    for name, shape in SMALL_SHARDED:
        n = shape[0] * shape[1] * shape[2]
        blk = flat[:, o:o + n].reshape((N_DEV,) + shape)
        small[name] = blk.transpose(1, 2, 0, 3).reshape(shape[0], shape[1], N_DEV * shape[2])
        o += n
    return small


def _owner_blocks(g):
    return g.reshape(N_DEV, g.shape[0] // N_DEV, g.shape[1])


def _as_rows(shape):
    width = 1
    for s in shape[1:]:
        width *= s
    return shape[0], width


REP_2D = tuple((name, _as_rows(shape)) for name, shape in REPLICATED)
REP_ROW0 = tuple(sum(a for _, (a, _) in REP_2D[:i]) for i in range(len(REP_2D)))
REP_ROWS = sum(a for _, (a, _) in REP_2D)


def _rep_rows(d):
    rows = []
    for name, (a, b) in REP_2D:
        v = d[name].reshape(a, b)
        rows.append(v if b == PACK_W else jnp.concatenate([v, jnp.zeros((a, PACK_W - b), F32)], axis=1))
    return jnp.concatenate(rows, axis=0)


def adamw_replicated(rep_all, w, m, v):
    names = [name for name, _ in REP_2D]
    n = len(names)
    c1 = 1.0 - ADAM_B1 ** ADAM_STEP
    c2 = 1.0 - ADAM_B2 ** ADAM_STEP

    def body(*refs):
        rep_ref, w_refs, m_refs, v_refs, outs = refs[0], refs[1:1 + n], refs[1 + n:1 + 2 * n], refs[1 + 2 * n:1 + 3 * n], refs[1 + 3 * n:]
        for i, (_, (a, b)) in enumerate(REP_2D):
            r0 = REP_ROW0[i]
            g = rep_ref[0, r0:r0 + a, 0:b]
            for q in range(1, N_DEV):
                g = g + rep_ref[q, r0:r0 + a, 0:b]
            nm = ADAM_B1 * m_refs[i][...] + (1.0 - ADAM_B1) * g
            nv = ADAM_B2 * v_refs[i][...] + (1.0 - ADAM_B2) * (g * g)
            outs[i][...] = g
            outs[n + i][...] = -ADAM_LR * ((nm / c1) / (jnp.sqrt(nv / c2) + ADAM_EPS) + ADAM_WD * w_refs[i][...])
            outs[2 * n + i][...] = nm
            outs[3 * n + i][...] = nv

    flat = lambda d: [d[name].reshape(ab) for name, ab in REP_2D]
    pspecs = [_full(ab) for _, ab in REP_2D]
    res = pl.pallas_call(body, grid=(1,), in_specs=[_full(rep_all.shape)] + pspecs * 3, out_specs=pspecs * 4,
                         out_shape=[_sds(ab) for _, ab in REP_2D] * 4, name="adamw_replicated",
                         compiler_params=_cp(("arbitrary",)))(rep_all, *flat(w), *flat(m), *flat(v))
    shapes = dict(REPLICATED)
    return [{name: res[k * n + i].reshape(shapes[name]) for i, name in enumerate(names)} for k in range(4)]


def _small_send_arrays(small_grads):
    sms = []
    for name, shape in SMALL_SHARDED:
        g = small_grads[name].reshape(shape[0], shape[1], N_DEV, shape[2]).transpose(2, 0, 1, 3)
        sms.append(g.reshape(N_DEV, -1))
    sms = jnp.concatenate(sms, axis=1)
    sms = jnp.concatenate([sms, jnp.zeros((N_DEV, SMS_ROWS * PACK_W - sms.shape[1]), F32)], axis=1)
    return sms.reshape(N_DEV, SMS_ROWS, PACK_W), _rep_rows(small_grads)


BIG_KEYS = ("w_in", "w_out", "w_up_t", "w_down")


def _local_step(x, tgt, wts, raw, gather=(), pair_sums=None, reducer=None):
    consts = _consts()
    ps = [_layer_params(l, raw, consts) for l in range(DEPTH)]
    x1, sv0 = layer_fwd(0, x, None, wts[0], ps[0], gather[:4], late=bool(gather))
    wts1 = {"w_in": _full_rows(sv0["gathered"][3])} if gather else wts[1]
    x2, sv1 = layer_fwd(1, x1, sv0["fl"], wts1, ps[1], gather[4:], late=bool(gather))
    dy, lparts = loss_call(x2, tgt)
    loss = jnp.sum(lparts[::8, 0])
    dx1, dvfirst, g1 = layer_bwd(1, dy, None, sv1, sv1["wts"], ps[1], (), reducer)
    big1 = {k: g1[k] for k in BIG_KEYS}
    if reducer is None:
        dx0, _, g0 = layer_bwd(0, dx1, dvfirst, sv0, sv0["wts"], ps[0])
        early = None
    else:
        own_in1, parts_in1 = pair_sums("1b", {"w_in": g1["w_in"]})
        dx0, _, g0 = layer_bwd(0, dx1, dvfirst, sv0, sv0["wts"], ps[0], parts_in1, reducer)
        own, recv = {(1, "w_in"): own_in1[0]}, {(1, "w_in"): g0["exchanged"][0]}
        for l, g, first in ((1, g1, 0), (0, g0, 1)):
            for i, k in enumerate(LATE_KEYS):
                own[(l, k)], recv[(l, k)] = g["early_own"][i], g["exchanged"][first + i]
        early = (own, recv)
    big = [{k: g0[k] for k in BIG_KEYS}, big1]
    return loss, dx0, big, _natural_grads(g0, g1), early


WEIGHT_NAMES = ("lower_bounds", "w_in", "w_in_vres", "mu_shift", "mu_vres", "rwkv_w0", "rwkv_w2", "rwkv_a0", "rwkv_a2",
                "rwkv_g2", "rwkv_k_k", "rwkv_k_a", "rwkv_r_k", "rwkv_lnx_w", "rwkv_lnx_b", "rwkv_v0", "rwkv_v2",
                "ssd_conv_w", "ssd_conv_b", "ssd_dt_bias", "ssd_A_log", "ssd_D", "ssd_norm_w", "hgrn_norm_w", "w_out",
                "ln1_w", "ln1_b", "w_up", "w_down", "ln2_w", "ln2_b")


def kernel(x, lower_bounds, w_in, w_in_vres, mu_shift, mu_vres, rwkv_w0, rwkv_w2, rwkv_a0, rwkv_a2, rwkv_g2, rwkv_k_k, rwkv_k_a, rwkv_r_k, rwkv_lnx_w, rwkv_lnx_b, rwkv_v0, rwkv_v2, ssd_conv_w, ssd_conv_b, ssd_dt_bias, ssd_A_log, ssd_D, ssd_norm_w, hgrn_norm_w, w_out, ln1_w, ln1_b, w_up, w_down, ln2_w, ln2_b, loss_target, m_lower_bounds, m_w_in, m_w_in_vres, m_mu_shift, m_mu_vres, m_rwkv_w0, m_rwkv_w2, m_rwkv_a0, m_rwkv_a2, m_rwkv_g2, m_rwkv_k_k, m_rwkv_k_a, m_rwkv_r_k, m_rwkv_lnx_w, m_rwkv_lnx_b, m_rwkv_v0, m_rwkv_v2, m_ssd_conv_w, m_ssd_conv_b, m_ssd_dt_bias, m_ssd_A_log, m_ssd_D, m_ssd_norm_w, m_hgrn_norm_w, m_w_out, m_ln1_w, m_ln1_b, m_w_up, m_w_down, m_ln2_w, m_ln2_b, v_lower_bounds, v_w_in, v_w_in_vres, v_mu_shift, v_mu_vres, v_rwkv_w0, v_rwkv_w2, v_rwkv_a0, v_rwkv_a2, v_rwkv_g2, v_rwkv_k_k, v_rwkv_k_a, v_rwkv_r_k, v_rwkv_lnx_w, v_rwkv_lnx_b, v_rwkv_v0, v_rwkv_v2, v_ssd_conv_w, v_ssd_conv_b, v_ssd_dt_bias, v_ssd_A_log, v_ssd_D, v_ssd_norm_w, v_hgrn_norm_w, v_w_out, v_ln1_w, v_ln1_b, v_w_up, v_w_down, v_ln2_w, v_ln2_b):
    given = dict(locals())
    w = {n: given[n] for n in WEIGHT_NAMES}
    m_all, v_all = ({n: given[pre + n] for n in WEIGHT_NAMES} for pre in ("m_", "v_"))
    w_arrs, m_arrs, v_arrs = _local_arrays(w), _local_arrays(m_all), _local_arrays(v_all)
    wire = lambda a: (w_arrs[a].T if a in (4, 5) else w_arrs[a]).astype(BF16)
    gathered0 = all_gather([wire(0), w_arrs[N_BIG]])
    raw = {n: w[n] for n, _ in REPLICATED}
    raw.update(_small_sharded_full(gathered0[1]))
    mx, my, mc = lax.axis_index("x"), lax.axis_index("y"), lax.axis_index("c")
    slots = jnp.stack([_dev_index(cx, cy, mc) for cx, cy in _chips(mx, my)]).astype(jnp.int32)

    def reducer(tag, send, sib, n_f32=0):
        wire_dt = [BF16] * (len(send) - n_f32) + [F32] * n_f32
        res = [reduce_pair(f"reduce_pair{tag}_{i}", s, slots, sb, dt) for i, (s, sb, dt) in enumerate(zip(send, sib, wire_dt))]
        return [o for o, _ in res], [pt for _, pt in res]

    def pair_sums(tag, grads, extra=()):
        send = [_owner_blocks(g) for g in grads.values()] + list(extra)
        return reducer(tag, send, exchange_siblings(send, f"exchange_siblings{tag}"), len(extra))

    behind_scan = [wire(a) for a in (2, 4, 6, 1, 3, 5, 7)]
    loss, dx, big, small_grads, (own_by, recv_by) = _local_step(
        x[0], loss_target[0], [{"w_in": _full_rows(gathered0[0])}, None], raw, behind_scan, pair_sums, reducer)
    sms_send, rep = _small_send_arrays(small_grads)
    own0b, parts0b = pair_sums("0b", {"w_in": big[0]["w_in"]}, [sms_send])
    recv0b, rep_all = exchange_chips(parts0b, rep)
    own, recv = [None] * (N_BIG + 1), [None] * (N_BIG + 1)
    for (l, k), o in own_by.items():
        a = 2 * BIG_KEYS.index(k) + l
        own[a], recv[a] = o, recv_by[(l, k)]
    for a, o, r in zip((0, N_BIG), own0b, recv0b):
        own[a], recv[a] = o, r
    terms = lambda a: [(own[a], None), (recv[a], 0), (recv[a], 1), (recv[a], 2)]
    moments = [{n: given[pre + n] for n in ("w_in", "w_in_vres")} for pre in ("", "m_", "v_")]
    in0, _ = adamw_w_in("adamw0", terms(0), *[d["w_in"][0] for d in moments])
    in1, vres = adamw_w_in("adamw1", terms(1), *[d["w_in"][1] for d in moments], vres=[d["w_in_vres"][0] for d in moments])
    results = [in0, in1] + [adamw(f"adamw{a}", terms(a), w_arrs[a], m_arrs[a], v_arrs[a], transposed=a in (4, 5))
                            for a in range(2, N_BIG + 1)]
    rep_res = adamw_replicated(rep_all, w, m_all, v_all)
    loss = lax.psum(loss, ("x", "y", "c"))
    outs = [loss, dx[None]]
    for q in range(4):
        d = _from_local_arrays([res[q] for res in results], rep_res[q], vres[q])
        outs += [d[n] for n in WEIGHT_NAMES]
    return tuple(outs)
```

```python
import functools

import jax
import jax.numpy as jnp
from jax import lax
from jax.experimental import pallas as pl
from jax.experimental.pallas import tpu as pltpu

F32 = jnp.float32
BF16 = jnp.bfloat16
HI = lax.Precision.HIGHEST

N_DEV = 8
SEQ = 2048
D_MODEL = 1024
D_FF = 4096
DG = 256
NH = 4
HD = 64
DEPTH = 2
ALPHA = (2.0 * DEPTH) ** 0.25
LN_EPS = 1e-5
RMS_EPS = 1e-5
GN_EPS = HD * 1e-5
SSD_N = 128
SSD_CHUNK = 128
HGRN_CHUNK = 16
DILATED = ((128, 1), (512, 4), (2048, 16))

ADAM_LR, ADAM_B1, ADAM_B2, ADAM_EPS, ADAM_WD, ADAM_STEP = 0.001, 0.9, 0.999, 1e-08, 0.01, 10

PW = 4096
C_R, C_K, C_V = 0, 256, 512
C_AQ, C_AK, C_AV = 768, 1024, 1280
C_Z, C_XBC = 1536, 1792
C_HQ, C_HF, C_HI, C_HG = 2560, 2816, 3072, 3328
C_LORA, C_DT, C_VRES = 3584, 3712, 3840

RB = 256
VMEM_LIMIT = 56 * 1024 * 1024
PACK_W = 1024


def _cp(sem=None):
    return pltpu.CompilerParams(dimension_semantics=sem, vmem_limit_bytes=VMEM_LIMIT)


def _sds(shape, dt=F32):
    return jax.ShapeDtypeStruct(tuple(shape), dt)


def _rows(w, cb=0, rb=RB):
    return pl.BlockSpec((rb, w), lambda i: (i, cb))


def _full(shape):
    n = len(shape)
    return pl.BlockSpec(tuple(shape), lambda *_: (0,) * n)


def _sigmoid(x):
    return 1.0 / (1.0 + jnp.exp(-x))


def _silu(x):
    return x * _sigmoid(x)


def _softplus(x):
    return jnp.maximum(x, 0.0) + jnp.log(1.0 + jnp.exp(jnp.where(x > 0, -x, x)))


MID = lax.Precision.HIGH
NN, TN, NT = (((1,), (0,)), ((), ())), (((0,), (0,)), ((), ())), (((1,), (1,)), ((), ()))


def _dot(a, b):
    return lax.dot_general(a, b, NN, precision=MID, preferred_element_type=F32)


def _dot_tn(a, b):
    return lax.dot_general(a, b, TN, precision=MID, preferred_element_type=F32)


def _dot_nt(a, b):
    return lax.dot_general(a, b, NT, precision=MID, preferred_element_type=F32)


def _dotx(a, b):
    return lax.dot_general(a, b, NN, precision=HI, preferred_element_type=F32)


def _dotx_tn(a, b):
    return lax.dot_general(a, b, TN, precision=HI, preferred_element_type=F32)


def _seg_ones(n, seg):
    i = jnp.arange(n)
    return (i[:, None] // seg == i[None, :] // seg).astype(F32)


def _shift_down(x, s):
    row = lax.broadcasted_iota(jnp.int32, x.shape, 0)
    return jnp.where(row < s, 0.0, pltpu.roll(x, s, 0))


def _shift_up(x, s):
    n = x.shape[0]
    row = lax.broadcasted_iota(jnp.int32, x.shape, 0)
    return jnp.where(row >= n - s, 0.0, pltpu.roll(x, n - s, 0))


@functools.partial(jax.custom_vjp, nondiff_argnums=(1,))
def _tshift(x, s):
    return _shift_down(x, s)


def _tshift_fwd(x, s):
    return _shift_down(x, s), None


def _tshift_bwd(s, _, g):
    return (_shift_up(g, s),)


_tshift.defvjp(_tshift_fwd, _tshift_bwd)


def _map_fwd(name, fn, grid, ins, in_specs, out_shapes, out_specs):
    n_in = len(ins)

    def body(*refs):
        ys = fn(*[r[...] for r in refs[:n_in]])
        for r, y in zip(refs[n_in:], ys):
            r[...] = y

    return pl.pallas_call(body, grid=grid, in_specs=in_specs, out_specs=out_specs, out_shape=out_shapes,
                          name=name, compiler_params=_cp(("parallel",)))(*ins)


def _map_bwd(name, fn, grid, ins, in_specs, cts, ct_specs, want, acc=(), gout=None):
    n_in = len(ins)
    flat_cts = [c for group in cts for c in group]
    flat_specs = [s for group in ct_specs for s in group]
    n_ct = len(flat_cts)
    gout = gout or {}
    out_shapes = [gout[i][0] if i in gout else _sds(ins[i].shape) for i in want]
    out_specs = [gout[i][1] if i in gout else in_specs[i] for i in want]

    def body(*refs):
        xs = [r[...] for r in refs[:n_in]]
        cvals = [r[...] for r in refs[n_in:n_in + n_ct]]
        gouts = refs[n_in + n_ct:]
        cs, p = [], 0
        for group in cts:
            v = cvals[p]
            for q in range(1, len(group)):
                v = v + cvals[p + q]
            cs.append(v)
            p += len(group)

        def f(*wanted):
            full = list(xs)
            for i, w in zip(want, wanted):
                full[i] = w
            return tuple(fn(*full))

        _, vjp = jax.vjp(f, *[xs[i] for i in want])
        gs = vjp(tuple(cs))
        for o, i, g in zip(gouts, want, gs):
            if i in acc:
                @pl.when(pl.program_id(0) == 0)
                def _():
                    o[...] = jnp.zeros_like(o)

                o[...] += g
            else:
                o[...] = g

    sem = ("arbitrary",) if acc else ("parallel",)
    return pl.pallas_call(body, grid=grid, in_specs=list(in_specs) + flat_specs, out_specs=out_specs,
                          out_shape=out_shapes, name=name, compiler_params=_cp(sem))(*ins, *flat_cts)


def _addn(name, *arrs):
    n, c = arrs[0].shape

    def fn(*xs):
        r = xs[0]
        for x in xs[1:]:
            r = r + x
        return (r,)

    return _map_fwd(name, fn, (n // RB,), list(arrs), [_rows(c)] * len(arrs), [_sds((n, c))], [_rows(c)])[0]


MM_TILES = {"k1024": (2048, 512, 1024), "k4096": (1024, 1024, 1024), "wgrad_tall": (2048, 1024, 512),
            "wgrad_wide": (1024, 2048, 512)}


def _mm(name, a, b, mode, tm, tn, tk, add=None, add_scale=1.0, epilogue=None, out_dtype=F32):
    if mode == "nn":
        (m, k), n = a.shape, b.shape[1]
    elif mode == "nt":
        (m, k), n = a.shape, b.shape[0]
    else:
        (k, m), n = a.shape, b.shape[1]
    nk = k // tk
    dn = {"nn": (((1,), (0,)), ((), ())), "nt": (((1,), (1,)), ((), ())), "tn": (((0,), (0,)), ((), ()))}[mode]

    def body(*refs):
        a_ref, b_ref = refs[:2]
        add_ref = refs[2] if add is not None else None
        o_ref = refs[3] if add is not None else refs[2]
        prod = lax.dot_general(a_ref[...].astype(BF16), b_ref[...].astype(BF16), dn, preferred_element_type=F32)

        def finish(r):
            if epilogue == "relu2":
                r = jnp.maximum(r, 0.0)
                r = r * r
            elif epilogue == "relu2_bwd":
                r = r * (2.0 * jnp.sqrt(add_ref[...]))
            elif add is not None:
                r = r + add_scale * add_ref[...]
            o_ref[...] = r.astype(out_dtype)

        if nk == 1:
            finish(prod)
        else:
            acc = refs[-1]
            kk = pl.program_id(2)

            @pl.when(kk == 0)
            def _():
                acc[...] = prod

            @pl.when(kk > 0)
            def _():
                acc[...] += prod

            @pl.when(kk == nk - 1)
            def _():
                finish(acc[...])

    a_spec = pl.BlockSpec((tk, tm), lambda i, j, q: (q, i)) if mode == "tn" else pl.BlockSpec((tm, tk), lambda i, j, q: (i, q))
    b_spec = pl.BlockSpec((tn, tk), lambda i, j, q: (j, q)) if mode == "nt" else pl.BlockSpec((tk, tn), lambda i, j, q: (q, j))
    o_spec = pl.BlockSpec((tm, tn), lambda i, j, q: (i, j))
    ins, specs = [a, b], [a_spec, b_spec]
    if add is not None:
        ins.append(add)
        specs.append(o_spec)
    return pl.pallas_call(body, grid=(m // tm, n // tn, nk), in_specs=specs, out_specs=o_spec,
                          out_shape=_sds((m, n), out_dtype),
                          scratch_shapes=[pltpu.VMEM((tm, tn), F32)] if nk > 1 else [], name=name,
                          compiler_params=_cp(("parallel", "parallel", "arbitrary")))(*ins)


def _lerp_colmap(j):
    r = jnp.where(j < 6, j, jnp.where(j == 6, C_LORA // 128, C_VRES // 128))
    return (0, r)


def _lerp_fn(f, mu):
    return (f + (_tshift(f, 1) - f) * mu,)


def _lerp_specs():
    return [pl.BlockSpec((SEQ, 128), _lerp_colmap), pl.BlockSpec((1, 128), lambda j: (0, j))]


def lerp_fwd(l, proj, mu):
    return _map_fwd(f"lerp_fwd{l}", _lerp_fn, (8,), [proj, mu], _lerp_specs(), [_sds((SEQ, 1024))],
                    [pl.BlockSpec((SEQ, 128), lambda j: (0, j))])[0]


def lerp_bwd(l, proj, mu, dfl):
    n_in = 2

    def body(f_ref, mu_ref, g_ref, df_ref, dmu_ref):
        _, vjp = jax.vjp(_lerp_fn, f_ref[...], mu_ref[...])
        df, dmu = vjp((g_ref[...],))
        df_ref[...] = df
        dmu_ref[...] = dmu

    cspec = pl.BlockSpec((SEQ, 128), lambda j: (0, j))
    return pl.pallas_call(body, grid=(8,), in_specs=_lerp_specs() + [cspec],
                          out_specs=[cspec, pl.BlockSpec((1, 128), lambda j: (0, j))],
                          out_shape=[_sds((SEQ, 1024)), _sds((1, 1024))], name=f"lerp_bwd{l}",
                          compiler_params=_cp(("parallel",)))(proj, mu, dfl)


def _conv_fn(x, w, b):
    y = x * w[3:4, :] + _tshift(x, 1) * w[2:3, :] + _tshift(x, 2) * w[1:2, :] + _tshift(x, 3) * w[0:1, :] + b
    return (_silu(y),)


def _conv_specs():
    return [pl.BlockSpec((SEQ, 128), lambda j: (0, C_XBC // 128 + j)), pl.BlockSpec((4, 128), lambda j: (0, j)),
            pl.BlockSpec((1, 128), lambda j: (0, j))]


def conv_fwd(l, proj, w, b):
    return _map_fwd(f"conv_fwd{l}", _conv_fn, (6,), [proj, w, b], _conv_specs(), [_sds((SEQ, 768))],
                    [pl.BlockSpec((SEQ, 128), lambda j: (0, j))])[0]


def conv_bwd(l, proj, w, b, dxc):
    def body(x_ref, w_ref, b_ref, g_ref, dx_ref, dw_ref, db_ref):
        _, vjp = jax.vjp(_conv_fn, x_ref[...], w_ref[...], b_ref[...])
        dx, dw, db = vjp((g_ref[...],))
        dx_ref[...] = dx
        dw_ref[...] = dw
        db_ref[...] = db

    cspec = pl.BlockSpec((SEQ, 128), lambda j: (0, j))
    return pl.pallas_call(body, grid=(6,), in_specs=_conv_specs() + [cspec],
                          out_specs=[cspec, pl.BlockSpec((4, 128), lambda j: (0, j)), pl.BlockSpec((1, 128), lambda j: (0, j))],
                          out_shape=[_sds((SEQ, 768)), _sds((4, 768)), _sds((1, 768))], name=f"conv_bwd{l}",
                          compiler_params=_cp(("parallel",)))(proj, w, b, dxc)


def _rwkv_pre_fn(has_vres):
    def fn(fk, fv, flora, *rest):
        if has_vres:
            fvres, vfirst, w0, w2p, a0, a2p, g2p, k_k, k_a, v0, v2p, seg = rest
        else:
            w0, w2p, a0, a2p, g2p, k_k, k_a, seg = rest
        w_log = -_softplus(-(w0 + _dot(jnp.tanh(flora), w2p))) - 0.5
        w = jnp.exp(-jnp.exp(w_log))
        a = _sigmoid(a0 + _dot(flora, a2p))
        g = _dot(_sigmoid(flora), g2p)
        if has_vres:
            v2 = fv + (vfirst - fv) * _sigmoid(v0 + _dot(fvres, v2p))
        else:
            v2 = fv * 1.0
        kk = fk * k_k
        kk = kk / jnp.maximum(jnp.sqrt(_dot(kk * kk, seg)), 1e-12)
        k2 = fk * (1.0 + (a - 1.0) * k_a)
        return w, k2, v2, -kk, kk * a, g

    return fn


def _rwkv_pre_args(fl, vfirst, p, has_vres):
    ins = [fl, fl, fl]
    specs = [_rows(256, 1), _rows(256, 2), _rows(128, 6)]
    if has_vres:
        ins += [fl, vfirst]
        specs += [_rows(128, 7), _rows(256, 2)]
    names = ["w0", "w2p", "a0", "a2p", "g2p", "k_k", "k_a"] + (["v0", "v2p"] if has_vres else []) + ["seg64"]
    for nme in names:
        ins.append(p[nme])
        specs.append(_full(p[nme].shape))
    return ins, specs, names


def rwkv_pre_fwd(l, fl, vfirst, p):
    has_vres = l > 0
    ins, specs, _ = _rwkv_pre_args(fl, vfirst, p, has_vres)
    return _map_fwd(f"rwkv_pre_fwd{l}", _rwkv_pre_fn(has_vres), (SEQ // RB,), ins, specs,
                    [_sds((SEQ, DG))] * 6, [_rows(DG)] * 6)


def rwkv_pre_bwd(l, fl, vfirst, p, cts):
    has_vres = l > 0
    ins, specs, names = _rwkv_pre_args(fl, vfirst, p, has_vres)
    n_row = 5 if has_vres else 3
    want = list(range(n_row)) + [n_row + i for i, nme in enumerate(names) if nme != "seg64"]
    acc = tuple(w for w in want if w >= n_row)
    ct_specs = [[_rows(DG)] * len(g) for g in cts]
    gout = {0: (_sds((SEQ, DG)), _rows(DG)), 1: (_sds((SEQ, DG)), _rows(DG)), 2: (_sds((SEQ, 128)), _rows(128))}
    if has_vres:
        gout[3] = (_sds((SEQ, 128)), _rows(128))
        gout[4] = (_sds((SEQ, DG)), _rows(DG))
    gs = _map_bwd(f"rwkv_pre_bwd{l}", _rwkv_pre_fn(has_vres), (SEQ // RB,), ins, specs, cts, ct_specs, want, acc, gout)
    keys = ["fk", "fv", "flora"] + (["fvres", "vfirst"] if has_vres else []) + [nme for nme in names if nme != "seg64"]
    return dict(zip(keys, gs))


def _rwkv_post_fn(y, fr, k2, v2, g, lnx_w, lnx_b, r_k, seg):
    mu = _dot(y, seg) * (1.0 / HD)
    d = y - mu
    var = _dot(d * d, seg) * (1.0 / HD)
    yn = d * lax.rsqrt(var + GN_EPS) * lnx_w + lnx_b
    bonus = _dot(fr * k2 * r_k, seg) * v2
    return ((yn + bonus) * g,)


def _rwkv_post_args(y, fl, k2, v2, g, p):
    ins = [y, fl, k2, v2, g, p["lnx_w"], p["lnx_b"], p["r_k"], p["seg64"]]
    specs = [_rows(DG), _rows(DG, 0), _rows(DG), _rows(DG), _rows(DG)] + [_full(x.shape) for x in ins[5:]]
    return ins, specs


def rwkv_post_fwd(l, y, fl, k2, v2, g, p):
    ins, specs = _rwkv_post_args(y, fl, k2, v2, g, p)
    return _map_fwd(f"rwkv_post_fwd{l}", _rwkv_post_fn, (SEQ // RB,), ins, specs, [_sds((SEQ, DG))], [_rows(DG)])[0]


def rwkv_post_bwd(l, y, fl, k2, v2, g, p, dya):
    ins, specs = _rwkv_post_args(y, fl, k2, v2, g, p)
    gs = _map_bwd(f"rwkv_post_bwd{l}", _rwkv_post_fn, (SEQ // RB,), ins, specs, [[dya]], [[_rows(DG, 0)]],
                  want=[0, 1, 2, 3, 4, 5, 6, 7], acc=(5, 6, 7), gout={1: (_sds((SEQ, DG)), _rows(DG))})
    return dict(zip(["y", "fr", "k2", "v2", "g", "lnx_w", "lnx_b", "r_k"], gs))


SCAN_TB = 128


def _coltile8(rows8, dmask, ones_stack, parts):
    pieces, rest = [], rows8
    for q in range(parts):
        piece = rest.astype(BF16).astype(F32)
        if q < parts - 1:
            rest = rest - piece
        pieces.append((piece[:, None, :] * dmask[None]).reshape(8 * HD, DG).astype(BF16))
    x = pieces[0] if parts == 1 else jnp.concatenate(pieces, axis=1)
    return jnp.dot(x, ones_stack, preferred_element_type=F32).reshape(8, HD, DG)


def _coltiles_bf16(rows_list, dmask, ones_bf16):
    x = jnp.concatenate([(r8[:, None, :] * dmask[None]).reshape(8 * HD, DG).astype(BF16) for r8 in rows_list], axis=0)
    t = jnp.dot(x, ones_bf16, preferred_element_type=F32)
    return [t[q * 8 * HD:(q + 1) * 8 * HD].reshape(8, HD, DG) for q in range(len(rows_list))]


def _segrows8(x8, dmask, ones_bf16):
    t = jnp.dot(x8.reshape(8 * HD, DG).astype(BF16), ones_bf16, preferred_element_type=F32).reshape(8, HD, DG)
    return jnp.sum(t * dmask[None], axis=1)


def rwkv_scan_fwd(l, fl, w, k2, v2, c, b, p, gather=()):
    nblk = SEQ // SCAN_TB
    ng = len(gather)

    def body(*refs):
        r_ref, w_ref, k_ref, v_ref, c_ref, b_ref, ones_ref, dm_ref = refs[:8]
        y_ref, st_ref = refs[8 + ng:10 + ng]
        s_sc = refs[10 + 2 * ng]
        if ng:
            begin, middle, end = _gather_steps(refs[8:8 + ng], refs[10 + ng:10 + 2 * ng], *refs[11 + 2 * ng:])

            @pl.when(pl.program_id(0) == 0)
            def _():
                begin()

            @pl.when(pl.program_id(0) == (3 * nblk) // 4)
            def _():
                middle()

        @pl.when(pl.program_id(0) == 0)
        def _():
            s_sc[...] = jnp.zeros_like(s_sc)

        ones3, ones = ones_ref[...], ones_ref[0:DG, :]
        dmask = dm_ref[...]

        def group(gi, carry):
            t0 = pl.multiple_of(gi * 8, 8)
            sl = pl.ds(t0, 8)
            v8 = v_ref[sl, :]
            wt = _coltile8(w_ref[sl, :], dmask, ones3, 3)
            ct, bt, kt, rt = _coltiles_bf16([c_ref[sl, :], b_ref[sl, :], k_ref[sl, :], r_ref[sl, :]], dmask, ones)
            t = s_sc[...]
            for j in range(8):
                sa = jnp.sum(t * ct[j], axis=0, keepdims=True)
                t = t * wt[j] + bt[j] * sa + kt[j] * v8[j:j + 1, :]
                st_ref[t0 + j] = t
            s_sc[...] = t
            y_ref[sl, :] = jnp.sum(st_ref[sl] * rt, axis=1)
            return carry

        lax.fori_loop(0, SCAN_TB // 8, group, 0, unroll=2)

        if ng:
            @pl.when(pl.program_id(0) == nblk - 1)
            def _():
                end()

    row = pl.BlockSpec((SCAN_TB, DG), lambda i: (i, 0))
    ins = [fl, w, k2, v2, c, b, p["seg64x3_bf16"], p["dmask"]] + list(gather)
    specs = [row] * 6 + [_full((3 * DG, DG)), _full((HD, DG))] + [ANY] * ng
    outs = pl.pallas_call(body, grid=(nblk,), in_specs=specs,
                          out_specs=[row, pl.BlockSpec((SCAN_TB, HD, DG), lambda i: (i, 0, 0))] + [ANY] * ng,
                          out_shape=[_sds((SEQ, DG)), _sds((SEQ, HD, DG))] + _gather_shapes(gather),
                          scratch_shapes=[pltpu.VMEM((HD, DG), F32)] + (_gather_sems(ng) if ng else []),
                          name=f"rwkv_scan_fwd{l}", compiler_params=_cp(("arbitrary",)))(*ins)
    return outs[0], outs[1], list(outs[2:])


def rwkv_scan_bwd(l, fl, w, k2, v2, c, b, states, dy, p, exchange=()):
    nblk = SEQ // SCAN_TB
    nx = len(exchange)

    def body(*refs):
        r_ref, w_ref, k_ref, v_ref, c_ref, b_ref, dy_ref, st_ref, sp_ref, ones_ref, dm_ref = refs[:11]
        dr_ref, dw_ref, dk_ref, dv_ref, dc_ref, db_ref = refs[11 + nx:17 + nx]
        g_sc, prev_sc, d8_sc, dsa_sc = refs[17 + 2 * nx:21 + 2 * nx]
        i = pl.program_id(0)
        if nx:
            begin, end = _chip_exchange_steps(refs[11:11 + nx], refs[17 + nx:17 + 2 * nx], *refs[21 + 2 * nx:])

            @pl.when(i == 0)
            def _():
                begin()

        @pl.when(i == 0)
        def _():
            g_sc[...] = jnp.zeros_like(g_sc)

        ones3, ones = ones_ref[...], ones_ref[0:DG, :]
        dmask = dm_ref[...]
        first_block = i == nblk - 1

        def group(gr, carry):
            gi = SCAN_TB // 8 - 1 - gr
            t0 = pl.multiple_of(gi * 8, 8)
            sl = pl.ds(t0, 8)
            v8, dy8 = v_ref[sl, :], dy_ref[sl, :]
            t8 = st_ref[sl]
            @pl.when(gi > 0)
            def _():
                prev_sc[0] = st_ref[t0 - 1]

            @pl.when(gi == 0)
            def _():
                prev_sc[0] = jnp.where(first_block, 0.0, sp_ref[0])

            for j in range(1, 8):
                prev_sc[j] = t8[j - 1]
            tp8 = prev_sc[...]
            wt = _coltile8(w_ref[sl, :], dmask, ones3, 3)
            ct, bt, kt, rt = _coltiles_bf16([c_ref[sl, :], b_ref[sl, :], k_ref[sl, :], r_ref[sl, :]], dmask, ones)
            sa8 = jnp.sum(tp8 * ct, axis=1)
            g = g_sc[...]
            for j in range(7, -1, -1):
                g = g + rt[j] * dy8[j:j + 1, :]
                d8_sc[j] = g
                dsa = jnp.sum(g * bt[j], axis=0, keepdims=True)
                dsa_sc[j:j + 1, :] = dsa
                g = g * wt[j] + ct[j] * dsa
            g_sc[...] = g
            d8 = d8_sc[...]
            dsa8 = dsa_sc[...]
            dv_ref[sl, :] = jnp.sum(d8 * kt, axis=1)
            dr_ref[sl, :] = _segrows8(t8 * dy8[:, None, :], dmask, ones)
            dk_ref[sl, :] = _segrows8(d8 * v8[:, None, :], dmask, ones)
            dw_ref[sl, :] = _segrows8(tp8 * d8, dmask, ones)
            db_ref[sl, :] = _segrows8(d8 * sa8[:, None, :], dmask, ones)
            dc_ref[sl, :] = _segrows8(tp8 * dsa8[:, None, :], dmask, ones)
            return carry

        lax.fori_loop(0, SCAN_TB // 8, group, 0, unroll=2)

        if nx:
            @pl.when(i == nblk - 1)
            def _():
                end()

    row = pl.BlockSpec((SCAN_TB, DG), lambda i: (nblk - 1 - i, 0))
    st_spec = pl.BlockSpec((SCAN_TB, HD, DG), lambda i: (nblk - 1 - i, 0, 0))
    sp_spec = pl.BlockSpec((1, HD, DG), lambda i: (jnp.maximum((nblk - 1 - i) * SCAN_TB - 1, 0), 0, 0))
    ins = [fl, w, k2, v2, c, b, dy, states, states, p["seg64x3_bf16"], p["dmask"]] + list(exchange)
    specs = [row] * 7 + [st_spec, sp_spec, _full((3 * DG, DG)), _full((HD, DG))] + [ANY] * nx
    tile8 = pltpu.VMEM((8, HD, DG), F32)
    sems = [pltpu.SemaphoreType.DMA((nx, 3)), pltpu.SemaphoreType.DMA((nx, 3))] if nx else []
    outs = pl.pallas_call(body, grid=(nblk,), in_specs=specs, out_specs=[row] * 6 + [ANY] * nx,
                          out_shape=[_sds((SEQ, DG))] * 6 + [_sds(a.shape, a.dtype) for a in exchange],
                          scratch_shapes=[pltpu.VMEM((HD, DG), F32), tile8, tile8, pltpu.VMEM((8, DG), F32)] + sems,
                          name=f"rwkv_scan_bwd{l}", compiler_params=_cp(("arbitrary",)))(*ins)
    return outs[:6], list(outs[6:])


HG_ROWS = 256


HG_NC = HG_ROWS // HGRN_CHUNK


def _hgrn_block_fn(layer):
    def fn(hq, hf, hi, hg, sprev, lb0, lb1, norm_w, seg, bd, tri_bd, ones_bd, first_row, causal):
        e0 = jnp.exp(lb0 - jnp.maximum(lb0, lb1))
        e1 = jnp.exp(lb1 - jnp.maximum(lb0, lb1))
        sm0, sm1 = e0 / (e0 + e1), e1 / (e0 + e1)
        lb = (sm0 - sm0) if layer == 0 else ((sm0 + sm1) - sm0)
        forget = lb + (1.0 - lb) * _sigmoid(hf)
        logf = jnp.log(forget)
        kk = 1.0 - forget
        q = _silu(hq)
        c, nc = HGRN_CHUNK, HG_NC
        b = _dotx(tri_bd, logf)
        bl = _dotx(ones_bd, logf)
        split = lambda t: t.reshape(nc, c, DG)
        b4 = split(b)
        diff = (b4[:, :, None, :] - b4[:, None, :, :]).reshape(nc * c * c, DG)
        dec = jnp.exp(jnp.where(causal > 0.5, diff, -1e30))
        qrep = jnp.broadcast_to(split(q)[:, :, None, :], (nc, c, c, DG)).reshape(nc * c * c, DG)
        ktil = jnp.broadcast_to(split(kk)[:, None, :, :], (nc, c, c, DG)).reshape(nc * c * c, DG)
        vtil = jnp.broadcast_to(split(hi)[:, None, :, :], (nc, c, c, DG)).reshape(nc * c * c, DG)
        att = _dot(qrep * ktil * dec, seg)
        o_intra = jnp.sum((att * vtil).reshape(nc * c, c, DG), axis=1)
        kd4 = split(kk * jnp.exp(bl - b))
        qe4 = split(q * jnp.exp(b))
        v4 = split(hi)
        tot = jnp.exp(_dotx(first_row, bl))
        s, o_inter = sprev, []
        for ci in range(nc):
            o_inter.append(_dot_nt(qe4[ci], s))
            s = s * tot[ci:ci + 1, :] + _dot_tn(v4[ci], kd4[ci]) * bd
        o = o_intra + jnp.concatenate(o_inter, axis=0)
        ms = _dot(o * o, seg) * (1.0 / HD)
        y = o * lax.rsqrt(ms + RMS_EPS) * norm_w * _silu(hg)
        return y, s

    return fn


def _hgrn_consts(p):
    return [p["seg64"], p["seg64"], p["tri_chunks"], p["ones_chunks"], p["first_row"], p["causal_blk"]]


def hgrn_fwd(l, proj, p):
    fn = _hgrn_block_fn(l)

    def body(hq_ref, hf_ref, hi_ref, hg_ref, *rest):
        const_refs, (y_ref, st_ref, s_sc) = rest[:-3], rest[-3:]

        @pl.when(pl.program_id(0) == 0)
        def _():
            s_sc[...] = jnp.zeros_like(s_sc)

        sprev = s_sc[...]
        st_ref[0] = sprev
        y, snext = fn(hq_ref[...], hf_ref[...], hi_ref[...], hg_ref[...], sprev, *[r[...] for r in const_refs])
        y_ref[...] = y
        s_sc[...] = snext

    rows = lambda cb: pl.BlockSpec((HG_ROWS, DG), lambda i: (i, cb))
    ins = [proj, proj, proj, proj, p["lb0"], p["lb1"], p["hgrn_norm_w"]] + _hgrn_consts(p)
    specs = [rows(C_HQ // DG), rows(C_HF // DG), rows(C_HI // DG), rows(C_HG // DG)] + [_full(x.shape) for x in ins[4:]]
    return pl.pallas_call(body, grid=(SEQ // HG_ROWS,), in_specs=specs,
                          out_specs=[rows(0), pl.BlockSpec((1, DG, DG), lambda i: (i, 0, 0))],
                          out_shape=[_sds((SEQ, DG)), _sds((SEQ // HG_ROWS, DG, DG))],
                          scratch_shapes=[pltpu.VMEM((DG, DG), F32)], name=f"hgrn_fwd{l}",
                          compiler_params=_cp(("arbitrary",)))(*ins)


def hgrn_bwd(l, proj, states, dy, p, sibling=(), dy_col=0):
    fn = _hgrn_block_fn(l)
    nblk = SEQ // HG_ROWS
    n_const = len(_hgrn_consts(p))
    ns = len(sibling)

    def body(hq_ref, hf_ref, hi_ref, hg_ref, st_ref, dy_ref, lb0_ref, lb1_ref, nw_ref, *rest):
        const_refs, rest = rest[:n_const], rest[n_const:]
        dp_ref, dlb0_ref, dlb1_ref, dnw_ref = rest[ns:ns + 4]
        ds_sc = rest[2 * ns + 4]
        if ns:
            begin, end = _sibling_steps(rest[:ns], rest[ns + 4:2 * ns + 4], *rest[2 * ns + 5:])

            @pl.when(pl.program_id(0) == 0)
            def _():
                begin()

        @pl.when(pl.program_id(0) == 0)
        def _():
            ds_sc[...] = jnp.zeros_like(ds_sc)
            dlb0_ref[...] = jnp.zeros_like(dlb0_ref)
            dlb1_ref[...] = jnp.zeros_like(dlb1_ref)
            dnw_ref[...] = jnp.zeros_like(dnw_ref)

        consts = [r[...] for r in const_refs]
        f = lambda hq, hf, hi, hg, sp, b0, b1, nw: fn(hq, hf, hi, hg, sp, b0, b1, nw, *consts)
        _, vjp = jax.vjp(f, hq_ref[...], hf_ref[...], hi_ref[...], hg_ref[...], st_ref[0], lb0_ref[...], lb1_ref[...],
                         nw_ref[...])
        dhq, dhf, dhi, dhg, dsp, dlb0, dlb1, dnw = vjp((dy_ref[...], ds_sc[...]))
        dp_ref[:, 0:DG] = dhq
        dp_ref[:, DG:2 * DG] = dhf
        dp_ref[:, 2 * DG:3 * DG] = dhi
        dp_ref[:, 3 * DG:4 * DG] = dhg
        ds_sc[...] = dsp
        dlb0_ref[...] += dlb0
        dlb1_ref[...] += dlb1
        dnw_ref[...] += dnw

        if ns:
            @pl.when(pl.program_id(0) == nblk - 1)
            def _():
                end()

    rows = lambda cb: pl.BlockSpec((HG_ROWS, DG), lambda i: (nblk - 1 - i, cb))
    ins = [proj, proj, proj, proj, states, dy, p["lb0"], p["lb1"], p["hgrn_norm_w"]] + _hgrn_consts(p)
    specs = [rows(C_HQ // DG), rows(C_HF // DG), rows(C_HI // DG), rows(C_HG // DG),
             pl.BlockSpec((1, DG, DG), lambda i: (nblk - 1 - i, 0, 0)), rows(dy_col)] + [_full(x.shape) for x in ins[6:]]
    sem = pltpu.SemaphoreType.DMA((max(ns, 1), 4))
    outs = pl.pallas_call(body, grid=(nblk,), in_specs=specs + [ANY] * ns,
                          out_specs=[pl.BlockSpec((HG_ROWS, 4 * DG), lambda i: (nblk - 1 - i, 0)), _full((1, DG)),
                                     _full((1, DG)), _full((1, DG))] + [ANY] * ns,
                          out_shape=[_sds((SEQ, 4 * DG)), _sds((1, DG)), _sds((1, DG)), _sds((1, DG))]
                          + [_sds((4,) + a.shape[1:], a.dtype) for a in sibling],
                          scratch_shapes=[pltpu.VMEM((DG, DG), F32)] + ([sem, sem] if ns else []), name=f"hgrn_bwd{l}",
                          compiler_params=_cp(("arbitrary",)))(*ins, *sibling)
    return outs[:4], list(outs[4:])


def _ssd_chunk_fn(z, xs, bm, cm, dtr, sprev, dt_bias, a_log, d_par, norm_w, e128, tri, trit, seg128, ones128):
    lc = SSD_CHUNK
    dt = _softplus(dtr + dt_bias)
    a = -jnp.exp(a_log)
    da = dt * a * (lax.broadcasted_iota(jnp.int32, (1, 128), 1) < NH).astype(F32)
    cs = _dotx(tri, da)
    cst = _dotx_tn(da, trit)
    cs_b = _dotx(cs, e128)
    dt_b = _dotx(dt, e128)
    csl_b = _dotx(jnp.sum(da, axis=0, keepdims=True), e128)
    xdt = xs * dt_b
    lane = lax.broadcasted_iota(jnp.int32, (1, DG), 1)
    rowi = lax.broadcasted_iota(jnp.int32, (lc, lc), 0)
    coli = lax.broadcasted_iota(jnp.int32, (lc, lc), 1)
    y = jnp.zeros((lc, DG), F32)
    snew = jnp.zeros((DG, SSD_N), F32)
    d_b = jnp.zeros((1, DG), F32)
    wdec = xdt * jnp.exp(csl_b - cs_b)
    for g in range(2):
        bg = bm[:, g * SSD_N:(g + 1) * SSD_N]
        cg = cm[:, g * SSD_N:(g + 1) * SSD_N]
        gmat = _dot_nt(cg, bg)
        gmask = ((lane // 128) == g).astype(F32)
        snew = snew + _dot_tn(wdec * gmask, bg)
        y = y + _dot_nt(cg, sprev) * gmask * jnp.exp(cs_b)
        for hh in range(2):
            h = 2 * g + hh
            seg = jnp.where(rowi >= coli, cs[:, h:h + 1] - cst[h:h + 1, :], -1e30)
            hmask = ((lane // HD) == h).astype(F32)
            y = y + _dot(gmat * jnp.exp(seg), xdt * hmask)
            d_b = d_b + d_par[:, h:h + 1] * hmask
    cd = jnp.exp(_dotx_tn(_dotx(da, e128), ones128))
    snext = sprev * cd + snew
    y = y + xs * d_b
    y = y * _silu(z)
    ms = _dot(y * y, seg128) * (1.0 / 128.0)
    return y * lax.rsqrt(ms + RMS_EPS) * norm_w, snext


def ssd_fwd(l, proj, xc, p):
    nc = SEQ // SSD_CHUNK

    def body(z_ref, xs_ref, b_ref, c_ref, dt_ref, dtb_ref, al_ref, d_ref, nw_ref, e_ref, tri_ref, trit_ref, sg_ref,
             on_ref, y_ref, st_ref, s_sc):
        @pl.when(pl.program_id(0) == 0)
        def _():
            s_sc[...] = jnp.zeros_like(s_sc)

        sprev = s_sc[...]
        st_ref[0] = sprev
        y, snext = _ssd_chunk_fn(z_ref[...], xs_ref[...], b_ref[...], c_ref[...], dt_ref[...], sprev, dtb_ref[...],
                                 al_ref[...], d_ref[...], nw_ref[...], e_ref[...], tri_ref[...], trit_ref[...],
                                 sg_ref[...], on_ref[...])
        y_ref[...] = y
        s_sc[...] = snext

    rw = lambda w, cb: pl.BlockSpec((SSD_CHUNK, w), lambda i: (i, cb))
    ins = [proj, xc, xc, xc, proj, p["dt_bias"], p["a_log"], p["ssd_d"], p["ssd_norm_w"], p["e128"], p["tri128"],
           p["tri128t"], p["seg128"], p["ones128"]]
    specs = [rw(DG, C_Z // DG), rw(DG, 0), rw(DG, 1), rw(DG, 2), rw(128, C_DT // 128)] + [_full(x.shape) for x in ins[5:]]
    return pl.pallas_call(body, grid=(nc,), in_specs=specs,
                          out_specs=[rw(DG, 0), pl.BlockSpec((1, DG, SSD_N), lambda i: (i, 0, 0))],
                          out_shape=[_sds((SEQ, DG)), _sds((nc, DG, SSD_N))],
                          scratch_shapes=[pltpu.VMEM((DG, SSD_N), F32)], name=f"ssd_fwd{l}",
                          compiler_params=_cp(("arbitrary",)))(*ins)


def ssd_bwd(l, proj, xc, states, dy, p, dy_col=0):
    nc = SEQ // SSD_CHUNK

    def body(z_ref, xs_ref, b_ref, c_ref, dt_ref, st_ref, dy_ref, dtb_ref, al_ref, d_ref, nw_ref, e_ref, tri_ref,
             trit_ref, sg_ref, on_ref, dz_ref, dxc_ref, ddt_ref, ddtb_ref, dal_ref, dd_ref, dnw_ref, ds_sc):
        @pl.when(pl.program_id(0) == 0)
        def _():
            ds_sc[...] = jnp.zeros_like(ds_sc)
            ddtb_ref[...] = jnp.zeros_like(ddtb_ref)
            dal_ref[...] = jnp.zeros_like(dal_ref)
            dd_ref[...] = jnp.zeros_like(dd_ref)
            dnw_ref[...] = jnp.zeros_like(dnw_ref)

        consts = (e_ref[...], tri_ref[...], trit_ref[...], sg_ref[...], on_ref[...])
        f = lambda *a: _ssd_chunk_fn(*a, *consts)
        _, vjp = jax.vjp(f, z_ref[...], xs_ref[...], b_ref[...], c_ref[...], dt_ref[...], st_ref[0], dtb_ref[...],
                         al_ref[...], d_ref[...], nw_ref[...])
        dz, dxs, db, dc, ddt, dsp, ddtb, dal, dd, dnw = vjp((dy_ref[...], ds_sc[...]))
        dz_ref[...] = dz
        dxc_ref[:, 0:DG] = dxs
        dxc_ref[:, DG:2 * DG] = db
        dxc_ref[:, 2 * DG:3 * DG] = dc
        ddt_ref[...] = ddt
        ds_sc[...] = dsp
        ddtb_ref[...] += ddtb
        dal_ref[...] += dal
        dd_ref[...] += dd
        dnw_ref[...] += dnw

    rw = lambda w, cb: pl.BlockSpec((SSD_CHUNK, w), lambda i: (nc - 1 - i, cb))
    ins = [proj, xc, xc, xc, proj, states, dy, p["dt_bias"], p["a_log"], p["ssd_d"], p["ssd_norm_w"], p["e128"],
           p["tri128"], p["tri128t"], p["seg128"], p["ones128"]]
    specs = [rw(DG, C_Z // DG), rw(DG, 0), rw(DG, 1), rw(DG, 2), rw(128, C_DT // 128),
             pl.BlockSpec((1, DG, SSD_N), lambda i: (nc - 1 - i, 0, 0)), rw(DG, dy_col)] + [_full(x.shape) for x in ins[7:]]
    return pl.pallas_call(body, grid=(nc,), in_specs=specs,
                          out_specs=[rw(DG, 0), rw(3 * DG, 0), rw(128, 0), _full((1, 128)), _full((1, 128)), _full((1, 128)),
                                     _full((1, DG))],
                          out_shape=[_sds((SEQ, DG)), _sds((SEQ, 3 * DG)), _sds((SEQ, 128)), _sds((1, 128)), _sds((1, 128)),
                                     _sds((1, 128)), _sds((1, DG))],
                          scratch_shapes=[pltpu.VMEM((DG, SSD_N), F32)], name=f"ssd_bwd{l}",
                          compiler_params=_cp(("arbitrary",)))(*ins)


ATT_BLK = 128


def _att_geometry(dil):
    i = lax.broadcasted_iota(jnp.int32, (ATT_BLK, ATT_BLK), 0)
    j = lax.broadcasted_iota(jnp.int32, (ATT_BLK, ATT_BLK), 1)
    return ((i - j) * dil).astype(F32), ((ATT_BLK + i - j) * dil).astype(F32), j <= i, j >= i


def _att_scores(qn, kc, kp, h, geom, has_prev):
    dist_c, dist_p, m_c, m_pj = geom
    slope = 2.0 ** (-8.0 * (h + 1) / NH)
    scale = HD ** -0.5
    s_c = _dot_nt(qn, kc) * scale - slope * dist_c
    s_p = _dot_nt(qn, kp) * scale - slope * dist_p
    m_p = jnp.logical_and(m_pj, has_prev)
    return jnp.where(m_c, s_c, -1e30), jnp.where(m_p, s_p, -1e30), m_c, m_p


def _sub_spec(ln, width, col):
    return pl.BlockSpec((ln, DG), lambda z: (0, z * (width // DG) + col // DG))


QKV_W = 3 * DG


def attn_branch_fwd(l, bi, qkv, dil):
    ln = SEQ // dil
    nb = ln // ATT_BLK

    def body(q_ref, k_ref, v_ref, o_ref, l_ref):
        geom = _att_geometry(dil)

        def blk(n, carry):
            r0 = pl.multiple_of(n * ATT_BLK, ATT_BLK)
            rp = pl.multiple_of(jnp.maximum(n - 1, 0) * ATT_BLK, ATT_BLK)
            cur, prv = pl.ds(r0, ATT_BLK), pl.ds(rp, ATT_BLK)
            for h in range(NH):
                hs = slice(h * HD, (h + 1) * HD)
                qn, kc, vc, kp, vp = q_ref[cur, hs], k_ref[cur, hs], v_ref[cur, hs], k_ref[prv, hs], v_ref[prv, hs]
                s_c, s_p, m_c, m_p = _att_scores(qn, kc, kp, h, geom, n > 0)
                m = jnp.maximum(jnp.max(s_c, axis=1, keepdims=True), jnp.max(s_p, axis=1, keepdims=True))
                p_c = jnp.where(m_c, jnp.exp(s_c - m), 0.0)
                p_p = jnp.where(m_p, jnp.exp(s_p - m), 0.0)
                den = jnp.sum(p_c, axis=1, keepdims=True) + jnp.sum(p_p, axis=1, keepdims=True)
                o_ref[cur, hs] = (_dot(p_c, vc) + _dot(p_p, vp)) / den
                l_ref[cur, hs] = jnp.broadcast_to(m + jnp.log(den), (ATT_BLK, HD))
            return carry

        lax.fori_loop(0, nb, blk, 0)

    pv = qkv.reshape(ln, dil * QKV_W)
    out = pl.BlockSpec((ln, DG), lambda z: (0, z))
    o, lse = pl.pallas_call(body, grid=(dil,), in_specs=[_sub_spec(ln, QKV_W, 0), _sub_spec(ln, QKV_W, DG), _sub_spec(ln, QKV_W, 2 * DG)],
                            out_specs=[out, out], out_shape=[_sds((ln, dil * DG))] * 2, name=f"attn_fwd{l}_{bi}",
                            compiler_params=_cp(("parallel",)))(pv, pv, pv)
    return o.reshape(SEQ, DG), lse.reshape(SEQ, DG)


def attn_branch_bwd(l, bi, qkv, dil, dyb, lse_all, delta):
    ln = SEQ // dil
    nb = ln // ATT_BLK
    scale = HD ** -0.5

    def body(q_ref, k_ref, v_ref, do_ref, l_ref, dl_ref, dq_ref, dk_ref, dv_ref):
        dk_ref[...] = jnp.zeros_like(dk_ref)
        dv_ref[...] = jnp.zeros_like(dv_ref)
        geom = _att_geometry(dil)

        def blk(n, carry):
            r0 = pl.multiple_of(n * ATT_BLK, ATT_BLK)
            rp = pl.multiple_of(jnp.maximum(n - 1, 0) * ATT_BLK, ATT_BLK)
            cur, prv = pl.ds(r0, ATT_BLK), pl.ds(rp, ATT_BLK)
            for h in range(NH):
                hs = slice(h * HD, (h + 1) * HD)
                qn, don = q_ref[cur, hs], do_ref[cur, hs]
                lse, dlt = l_ref[cur, h * HD:h * HD + 1], dl_ref[cur, h * HD:h * HD + 1]
                kc, vc, kp, vp = k_ref[cur, hs], v_ref[cur, hs], k_ref[prv, hs], v_ref[prv, hs]
                s_c, s_p, m_c, m_p = _att_scores(qn, kc, kp, h, geom, n > 0)
                p_c = jnp.where(m_c, jnp.exp(s_c - lse), 0.0)
                p_p = jnp.where(m_p, jnp.exp(s_p - lse), 0.0)
                ds_c = p_c * (_dot_nt(don, vc) - dlt)
                ds_p = p_p * (_dot_nt(don, vp) - dlt)
                dq_ref[cur, hs] = (_dot(ds_c, kc) + _dot(ds_p, kp)) * scale
                dv_ref[prv, hs] += _dot_tn(p_p, don)
                dk_ref[prv, hs] += _dot_tn(ds_p, qn) * scale
                dv_ref[cur, hs] += _dot_tn(p_c, don)
                dk_ref[cur, hs] += _dot_tn(ds_c, qn) * scale
            return carry

        lax.fori_loop(0, nb, blk, 0)

    pv = qkv.reshape(ln, dil * QKV_W)
    sub = lambda t: t.reshape(ln, dil * DG)
    row = pl.BlockSpec((ln, DG), lambda z: (0, z))
    outs = pl.pallas_call(body, grid=(dil,),
                          in_specs=[_sub_spec(ln, QKV_W, 0), _sub_spec(ln, QKV_W, DG), _sub_spec(ln, QKV_W, 2 * DG), row, row, row],
                          out_specs=[row] * 3, out_shape=[_sds((ln, dil * DG))] * 3, name=f"attn_bwd{l}_{bi}",
                          compiler_params=_cp(("parallel",)))(pv, pv, pv, sub(dyb), sub(lse_all), sub(delta))
    return [t.reshape(SEQ, DG) for t in outs]


def _attn_merge_fn(o1, o2, o3, l1, l2, l3):
    m = jnp.maximum(jnp.maximum(l1, l2), l3)
    w1, w2, w3 = jnp.exp(l1 - m), jnp.exp(l2 - m), jnp.exp(l3 - m)
    den = w1 + w2 + w3
    return (w1 * o1 + w2 * o2 + w3 * o3) / den, m + jnp.log(den)


def attn_merge(l, os_, ls_):
    ins = list(os_) + list(ls_)
    return _map_fwd(f"attn_merge{l}", _attn_merge_fn, (SEQ // RB,), ins, [_rows(DG)] * 6, [_sds((SEQ, DG))] * 2,
                    [_rows(DG)] * 2)


def attn_delta(l, dyb, yb, seg):
    fn = lambda d, y, s: (_dot(d * y, s),)
    return _map_fwd(f"attn_delta{l}", fn, (SEQ // RB,), [dyb, yb, seg], [_rows(DG), _rows(DG), _full((DG, DG))],
                    [_sds((SEQ, DG))], [_rows(DG)])[0]


def _ln_fn(x, mix, w, b):
    h = ALPHA * x + mix
    mu = jnp.mean(h, axis=-1, keepdims=True)
    d = h - mu
    var = jnp.mean(d * d, axis=-1, keepdims=True)
    return (d * lax.rsqrt(var + LN_EPS) * w + b,)


def ln_fwd(name, x, mix, w, b):
    specs = [_rows(D_MODEL), _rows(D_MODEL), _full((1, D_MODEL)), _full((1, D_MODEL))]
    return _map_fwd(name, _ln_fn, (SEQ // RB,), [x, mix, w, b], specs, [_sds((SEQ, D_MODEL))], [_rows(D_MODEL)])[0]


def ln_bwd(name, x, mix, w, b, dy):
    specs = [_rows(D_MODEL), _rows(D_MODEL), _full((1, D_MODEL)), _full((1, D_MODEL))]
    return _map_bwd(name, _ln_fn, (SEQ // RB,), [x, mix, w, b], specs, [[dy]], [[_rows(D_MODEL)]], want=[1, 2, 3],
                    acc=(2, 3))


def loss_call(y, tgt):
    def fn(yy, tt):
        e = yy - tt
        part = 0.5 * jnp.sum(jnp.sum(e * e, axis=-1, keepdims=True) * (1.0 / D_MODEL), axis=0, keepdims=True)
        return e * (1.0 / D_MODEL), jnp.broadcast_to(part, (8, 128))

    return _map_fwd("loss", fn, (SEQ // RB,), [y, tgt], [_rows(D_MODEL)] * 2,
                    [_sds((SEQ, D_MODEL)), _sds((SEQ // RB * 8, 128))],
                    [_rows(D_MODEL), pl.BlockSpec((8, 128), lambda i: (i, 0))])


LATE_KEYS = ("w_out", "w_up_t", "w_down")


def _full_rows(g):
    return g.reshape(N_DEV * g.shape[1], g.shape[2])


def layer_fwd(l, x, vfirst, wts, p, gather=(), late=False):
    sv = {"x": x}
    proj = _mm(f"mm_in{l}", x, wts["w_in"], "nn", *MM_TILES["k1024"])
    fl = lerp_fwd(l, proj, p["mu"])
    xc = conv_fwd(l, proj, p["conv_w"], p["conv_b"])
    w, k2, v2, c, b, g = rwkv_pre_fwd(l, fl, vfirst, p)
    y_scan, states, sv["gathered"] = rwkv_scan_fwd(l, fl, w, k2, v2, c, b, p, gather)
    if late:
        wts = dict(wts, **dict(zip(LATE_KEYS, [_full_rows(g) for g in sv["gathered"][:3]])))
    sv["wts"] = wts
    ya = rwkv_post_fwd(l, y_scan, fl, k2, v2, g, p)
    qkv = proj[:, C_AQ:C_AQ + 3 * DG]
    outs, lses = [], []
    for bi, (win, dil) in enumerate(DILATED):
        o, lse = attn_branch_fwd(l, bi, qkv, dil)
        outs.append(o)
        lses.append(lse)
    yb, lse_all = attn_merge(l, outs, lses)
    yc, ssd_states = ssd_fwd(l, proj, xc, p)
    yd, hg_states = hgrn_fwd(l, proj, p)
    ycat = jnp.concatenate([ya, yb, yc, yd], axis=1).astype(BF16)
    mix = _mm(f"mm_out{l}", ycat, wts["w_out"], "nn", *MM_TILES["k1024"])
    x1 = ln_fwd(f"ln1_fwd{l}", x, mix, p["ln1_w"], p["ln1_b"])
    hh = _mm(f"mm_up{l}", x1, wts["w_up_t"], "nt", *MM_TILES["k1024"], epilogue="relu2")
    m2 = _mm(f"mm_down{l}", hh, wts["w_down"], "nn", *MM_TILES["k4096"])
    x2 = ln_fwd(f"ln2_fwd{l}", x1, m2, p["ln2_w"], p["ln2_b"])
    sv.update(proj=proj, fl=fl, xc=xc, w=w, k2=k2, v2=v2, c=c, b=b, g=g, y_scan=y_scan, states=states,
              yb=yb, lse_all=lse_all, ssd_states=ssd_states, hg_states=hg_states, ycat=ycat, mix=mix, x1=x1, hh=hh, qkv=qkv,
              m2=m2, vfirst=vfirst)
    return x2, sv


def layer_bwd(l, dx2, dvfirst_next, sv, wts, p, exchange=(), reducer=None):
    gr = {}
    x, x1, proj, fl = sv["x"], sv["x1"], sv["proj"], sv["fl"]
    dres2, gr["ln2_w"], gr["ln2_b"] = ln_bwd(f"ln2_bwd{l}", x1, sv["m2"], p["ln2_w"], p["ln2_b"], dx2)
    du = _mm(f"mm_down_dx{l}", dres2, wts["w_down"], "nt", *MM_TILES["k1024"], add=sv["hh"], epilogue="relu2_bwd",
             out_dtype=BF16)
    gr["w_down"] = _mm(f"mm_down_dw{l}", sv["hh"], dres2, "tn", *MM_TILES["wgrad_tall"])
    dx1 = _mm(f"mm_up_dx{l}", du, wts["w_up_t"], "nn", *MM_TILES["k4096"], add=dres2, add_scale=ALPHA)
    gr["w_up_t"] = _mm(f"mm_up_dw{l}", du, x1, "tn", *MM_TILES["wgrad_tall"])
    dres1, gr["ln1_w"], gr["ln1_b"] = ln_bwd(f"ln1_bwd{l}", x, sv["mix"], p["ln1_w"], p["ln1_b"], dx1)
    dycat = _mm(f"mm_out_dx{l}", dres1, wts["w_out"], "nt", *MM_TILES["k1024"])
    gr["w_out"] = _mm(f"mm_out_dw{l}", sv["ycat"], dres1, "tn", 1024, 1024, 512)
    dyb = dycat[:, DG:2 * DG]
    send = [_owner_blocks(gr[k]) for k in LATE_KEYS] if reducer else []
    (dhg4, gr["lb0"], gr["lb1"], gr["hgrn_norm_w"]), sib = hgrn_bwd(l, proj, sv["hg_states"], dycat, p, send, dy_col=3)
    if reducer:
        gr["early_own"], early_parts = reducer(f"{l}a", send, sib)
        exchange = list(exchange) + list(early_parts)
    dz, dxc, ddt, gr["dt_bias"], gr["a_log"], gr["ssd_d"], gr["ssd_norm_w"] = ssd_bwd(l, proj, sv["xc"], sv["ssd_states"], dycat, p, dy_col=2)
    dxbc, gr["conv_w"], gr["conv_b"] = conv_bwd(l, proj, p["conv_w"], p["conv_b"], dxc)
    delta = attn_delta(l, dyb, sv["yb"], p["seg64"])
    dqs, dks, dvs = [], [], []
    for bi, (win, dil) in enumerate(DILATED):
        dq, dk, dv = attn_branch_bwd(l, bi, sv["qkv"], dil, dyb, sv["lse_all"], delta)
        dqs.append(dq)
        dks.append(dk)
        dvs.append(dv)
    dq_a, dk_a, dv_a = _addn(f"attn_dq{l}", *dqs), _addn(f"attn_dk{l}", *dks), _addn(f"attn_dv{l}", *dvs)
    pg = rwkv_post_bwd(l, sv["y_scan"], fl, sv["k2"], sv["v2"], sv["g"], p, dycat)
    gr["lnx_w"], gr["lnx_b"], gr["r_k"] = pg["lnx_w"], pg["lnx_b"], pg["r_k"]
    (dr, dw, dk, dv, dc, db), gr["exchanged"] = rwkv_scan_bwd(l, fl, sv["w"], sv["k2"], sv["v2"], sv["c"], sv["b"],
                                                              sv["states"], pg["y"], p, exchange)
    v2_cts = [dv, pg["v2"]] + ([dvfirst_next] if dvfirst_next is not None else [])
    qg = rwkv_pre_bwd(l, fl, sv["vfirst"], p, [[dw], [dk, pg["k2"]], v2_cts, [dc], [db], [pg["g"]]])
    for nme in ("w0", "w2p", "a0", "a2p", "g2p", "k_k", "k_a", "v0", "v2p"):
        if nme in qg:
            gr[nme] = qg[nme]
    dfr = _addn(f"rwkv_dr{l}", dr, pg["fr"])
    dvres = qg["fvres"] if l > 0 else jnp.zeros((SEQ, 128), F32)
    dfl_out = jnp.concatenate([dfr, qg["fk"], qg["fv"], qg["flora"], dvres], axis=1)
    dfl_in, gr["mu"] = lerp_bwd(l, proj, p["mu"], dfl_out)
    dproj = jnp.concatenate([dfl_in[:, 0:768], dq_a, dk_a, dv_a, dz, dxbc, dhg4, dfl_in[:, 768:896], ddt,
                             dfl_in[:, 896:1024], jnp.zeros((SEQ, 128), F32)], axis=1).astype(BF16)
    dx = _mm(f"mm_in_dx{l}", dproj, wts["w_in"], "nt", *MM_TILES["k4096"], add=dres1, add_scale=ALPHA)
    gr["w_in"] = _mm(f"mm_in_dw{l}", x, dproj, "tn", *MM_TILES["wgrad_wide"])
    return dx, (qg["vfirst"] if l > 0 else None), gr


def _w_in_pad(w_in_l, w_vres):
    rows = w_in_l.shape[0]
    z = lambda n: jnp.zeros((rows, n), w_in_l.dtype)
    vres = z(128) if w_vres is None else jnp.concatenate([w_vres, z(96)], axis=1)
    return jnp.concatenate([w_in_l[:, 0:768], w_in_l[:, 896:1664], w_in_l[:, 1664:1920], w_in_l[:, 1920:2688],
                            w_in_l[:, 2692:3716], w_in_l[:, 768:896], w_in_l[:, 2688:2692], z(124), vres, z(128)], axis=1)


def _w_in_unpad(g):
    g_in = jnp.concatenate([g[:, 0:768], g[:, C_LORA:C_LORA + 128], g[:, 768:1536], g[:, C_Z:C_Z + 256],
                            g[:, C_XBC:C_XBC + 768], g[:, C_DT:C_DT + 4], g[:, C_HQ:C_HQ + 1024]], axis=1)
    return g_in, g[:, C_VRES:C_VRES + 32]


def _consts():
    pair = jnp.arange(HG_NC * HGRN_CHUNK * HGRN_CHUNK)
    i128 = jnp.arange(128)
    ihg = jnp.arange(HG_ROWS)
    same_chunk = (ihg[:, None] // HGRN_CHUNK) == (ihg[None, :] // HGRN_CHUNK)
    seg64 = _seg_ones(DG, HD)
    tri128 = (i128[:, None] >= i128[None, :]).astype(F32)
    return dict(
        seg64=seg64, seg64x3_bf16=jnp.concatenate([seg64, seg64, seg64], axis=0).astype(BF16),
        dmask=(jnp.arange(HD)[:, None] == (jnp.arange(DG)[None, :] % HD)).astype(F32),
        tri_chunks=(same_chunk & (ihg[:, None] >= ihg[None, :])).astype(F32), ones_chunks=same_chunk.astype(F32),
        first_row=(ihg[None, :] == (jnp.arange(HG_NC) * HGRN_CHUNK)[:, None]).astype(F32),
        causal_blk=jnp.broadcast_to((((pair // HGRN_CHUNK) % HGRN_CHUNK) >= (pair % HGRN_CHUNK)).astype(F32)[:, None],
                                    (HG_NC * HGRN_CHUNK * HGRN_CHUNK, DG)),
        e128=((i128[:, None] == (jnp.arange(DG)[None, :] // HD)) & (i128[:, None] < NH)).astype(F32),
        tri128=tri128, tri128t=tri128.T, seg128=_seg_ones(DG, 128), ones128=jnp.ones((128, 128), F32))


def _pad_lanes(v, n):
    return jnp.concatenate([v, jnp.zeros((n - v.shape[0],), v.dtype)])[None, :]


def _layer_params(l, raw, consts):
    p = dict(consts)
    row = lambda name: raw[name][l][None, :]
    z = lambda r: jnp.zeros((r, DG), F32)
    mu_vres = raw["mu_vres"][l - 1] if l > 0 else jnp.zeros((32,), F32)
    p["mu"] = jnp.concatenate([raw["mu_shift"][l], mu_vres, jnp.zeros((96,), F32)])[None, :]
    p["conv_w"], p["conv_b"] = raw["ssd_conv_w"][l], row("ssd_conv_b")
    p["w0"], p["a0"], p["k_k"], p["k_a"] = row("rwkv_w0"), row("rwkv_a0"), row("rwkv_k_k"), row("rwkv_k_a")
    p["lnx_w"], p["lnx_b"] = row("rwkv_lnx_w"), row("rwkv_lnx_b")
    p["r_k"] = raw["rwkv_r_k"][l].reshape(1, DG)
    p["w2p"] = jnp.concatenate([raw["rwkv_w2"][l], z(96)], axis=0)
    p["a2p"] = jnp.concatenate([z(32), raw["rwkv_a2"][l], z(64)], axis=0)
    p["g2p"] = jnp.concatenate([z(64), raw["rwkv_g2"][l]], axis=0)
    if l > 0:
        p["v0"] = raw["rwkv_v0"][l - 1][None, :]
        p["v2p"] = jnp.concatenate([raw["rwkv_v2"][l - 1], z(96)], axis=0)
    p["lb0"], p["lb1"] = raw["lower_bounds"][0:1], raw["lower_bounds"][1:2]
    p["hgrn_norm_w"], p["ssd_norm_w"] = row("hgrn_norm_w"), row("ssd_norm_w")
    p["dt_bias"], p["a_log"], p["ssd_d"] = (_pad_lanes(raw[n][l], 128) for n in ("ssd_dt_bias", "ssd_A_log", "ssd_D"))
    for n in ("ln1_w", "ln1_b", "ln2_w", "ln2_b"):
        p[n] = row(n)
    return p


def _natural_grads(g0, g1):
    gs = (g0, g1)
    st = lambda key, f=lambda a: a[0]: jnp.stack([f(g[key]) for g in gs])
    out = {}
    out["lower_bounds"] = jnp.concatenate([g0["lb0"] + g1["lb0"], g0["lb1"] + g1["lb1"]], axis=0)
    out["mu_shift"] = st("mu", lambda a: a[0, :896])
    out["mu_vres"] = g1["mu"][:, 896:928]
    out["rwkv_w0"], out["rwkv_a0"], out["rwkv_k_k"], out["rwkv_k_a"] = st("w0"), st("a0"), st("k_k"), st("k_a")
    out["rwkv_w2"] = st("w2p", lambda a: a[0:32])
    out["rwkv_a2"] = st("a2p", lambda a: a[32:64])
    out["rwkv_g2"] = st("g2p", lambda a: a[64:128])
    out["rwkv_r_k"] = st("r_k", lambda a: a.reshape(NH, HD))
    out["rwkv_lnx_w"], out["rwkv_lnx_b"] = st("lnx_w"), st("lnx_b")
    out["rwkv_v0"] = g1["v0"]
    out["rwkv_v2"] = g1["v2p"][None, 0:32]
    out["ssd_conv_w"] = st("conv_w", lambda a: a)
    out["ssd_conv_b"] = st("conv_b")
    out["ssd_dt_bias"], out["ssd_A_log"], out["ssd_D"] = (st(k, lambda a: a[0, :NH]) for k in ("dt_bias", "a_log", "ssd_d"))
    out["ssd_norm_w"], out["hgrn_norm_w"] = st("ssd_norm_w"), st("hgrn_norm_w")
    for n in ("ln1_w", "ln1_b", "ln2_w", "ln2_b"):
        out[n] = st(n)
    return out


MESH_T = pl.DeviceIdType.MESH
ANY = pl.BlockSpec(memory_space=pl.ANY)


def _dev_index(px, py, pc):
    return 4 * px + 2 * py + pc


def all_gather(arrs):
    n = len(arrs)

    def body(*refs):
        begin, middle, end = _gather_steps(refs[:n], refs[n:2 * n], *refs[2 * n:])
        begin()
        middle()
        end()

    return pl.pallas_call(body, in_specs=[ANY] * n, out_specs=[ANY] * n, out_shape=_gather_shapes(arrs),
                          scratch_shapes=_gather_sems(n), name="all_gather")(*arrs)


def _gather_shapes(arrs):
    return [_sds((N_DEV,) + a.shape, a.dtype) for a in arrs]


def _gather_sems(n):
    return [pltpu.SemaphoreType.DMA((n, 7)), pltpu.SemaphoreType.DMA((n, 7)), pltpu.SemaphoreType.DMA((n,))]


def _gather_steps(ins, outs, send_sems, recv_sems, local_sems):
    n = len(ins)
    x, y, c = lax.axis_index("x"), lax.axis_index("y"), lax.axis_index("c")
    me, sibling = (x, y, c), (x, y, 1 - c)
    chips = [(1 - x, y), (x, 1 - y), (1 - x, 1 - y)]

    def copy(a, k, block, to, src=None):
        slot = outs[a].at[_dev_index(*block)]
        return pltpu.make_async_remote_copy(src_ref=slot if src is None else src, dst_ref=slot,
                                            send_sem=send_sems.at[a, k], recv_sem=recv_sems.at[a, k],
                                            device_id=to, device_id_type=MESH_T)

    def own_copies():
        mine = [pltpu.make_async_copy(ins[a], outs[a].at[_dev_index(*me)], local_sems.at[a]) for a in range(n)]
        first = []
        for a in range(n):
            first.append(copy(a, 0, me, sibling, src=ins[a]))
            first += [copy(a, 1 + j, me, (*chip, c), src=ins[a]) for j, chip in enumerate(chips)]
        return mine, first

    def begin():
        mine, first = own_copies()
        for cp in mine + first:
            cp.start()

    def passed_on():
        return [copy(a, 4 + j, (*chip, c), sibling) for j, chip in enumerate(chips) for a in range(n)]

    def middle():
        for j, chip in enumerate(chips):
            for a in range(n):
                copy(a, 1 + j, (*chip, c), me).wait_recv()
        for cp in passed_on():
            cp.start()

    def end():
        mine, first = own_copies()
        for a in range(n):
            copy(a, 0, sibling, me).wait_recv()
            for j, chip in enumerate(chips):
                copy(a, 4 + j, (*chip, 1 - c), me).wait_recv()
        for cp in first + passed_on():
            cp.wait_send()
        for cp in mine:
            cp.wait()

    return begin, middle, end


def _chips(x, y):
    return [(x, y), (1 - x, y), (x, 1 - y), (1 - x, 1 - y)]


def _sibling_steps(ins, sib, send_sems, recv_sems):
    x, y, c = lax.axis_index("x"), lax.axis_index("y"), lax.axis_index("c")

    def copies():
        return [pltpu.make_async_remote_copy(src_ref=ins[a].at[_dev_index(cx, cy, 1 - c)], dst_ref=sib[a].at[k],
                                             send_sem=send_sems.at[a, k], recv_sem=recv_sems.at[a, k],
                                             device_id=(x, y, 1 - c), device_id_type=MESH_T)
                for a in range(len(ins)) for k, (cx, cy) in enumerate(_chips(x, y))]

    def begin():
        for cp in copies():
            cp.start()

    def end():
        cps = copies()
        for cp in cps:
            cp.wait_recv()
        for cp in cps:
            cp.wait_send()

    return begin, end


def exchange_siblings(arrs, name):
    n = len(arrs)

    def body(*refs):
        begin, end = _sibling_steps(refs[:n], refs[n:2 * n], *refs[2 * n:])
        begin()
        end()

    sem = pltpu.SemaphoreType.DMA((n, 4))
    return pl.pallas_call(body, in_specs=[ANY] * n, out_specs=[ANY] * n,
                          out_shape=[_sds((4,) + a.shape[1:], a.dtype) for a in arrs],
                          scratch_shapes=[sem, sem], name=name)(*arrs)


def reduce_pair(name, send, slots, sib, wire_dtype):
    _, r, c = send.shape
    rb = min(r, 262144 // c)

    def body(slots_ref, m0, m1, m2, m3, s_ref, own_ref, part_ref):
        own_ref[...] = m0[...] + s_ref[0]
        for k, m_ref in enumerate((m1, m2, m3)):
            part_ref[k] = (m_ref[...] + s_ref[k + 1]).astype(wire_dtype)

    mine = [pl.BlockSpec((None, rb, c), lambda i, s, k=k: (s[k], i, 0)) for k in range(4)]
    grid_spec = pltpu.PrefetchScalarGridSpec(
        num_scalar_prefetch=1, grid=(r // rb,),
        in_specs=mine + [pl.BlockSpec((4, rb, c), lambda i, s: (0, i, 0))],
        out_specs=[pl.BlockSpec((rb, c), lambda i, s: (i, 0)), pl.BlockSpec((3, rb, c), lambda i, s: (0, i, 0))])
    return pl.pallas_call(body, grid_spec=grid_spec, out_shape=[_sds((r, c)), _sds((3, r, c), wire_dtype)], name=name,
                          compiler_params=_cp(("parallel",)))(slots, send, send, send, send, sib)


def _chip_exchange_steps(ins, recv, send_sems, recv_sems):
    x, y, c = lax.axis_index("x"), lax.axis_index("y"), lax.axis_index("c")

    def copies():
        return [pltpu.make_async_remote_copy(src_ref=ins[a].at[k], dst_ref=recv[a].at[k], send_sem=send_sems.at[a, k],
                                             recv_sem=recv_sems.at[a, k], device_id=(cx, cy, c), device_id_type=MESH_T)
                for a in range(len(ins)) for k, (cx, cy) in enumerate(_chips(x, y)[1:])]

    def begin():
        for cp in copies():
            cp.start()

    def end():
        cps = copies()
        for cp in cps:
            cp.wait_recv()
        for cp in cps:
            cp.wait_send()

    return begin, end


def exchange_chips(parts, rep):
    n = len(parts)

    def body(*refs):
        ins, rep_ref = refs[:n], refs[n]
        recv, rep_all = refs[n + 1:2 * n + 1], refs[2 * n + 1]
        send_sems, recv_sems, rsend_sems, rrecv_sems, local_sem = refs[2 * n + 2:]
        x, y, c = lax.axis_index("x"), lax.axis_index("y"), lax.axis_index("c")
        me = _dev_index(x, y, c)
        mine = pltpu.make_async_copy(rep_ref, rep_all.at[me], local_sem)
        mine.start()
        begin, end = _chip_exchange_steps(ins, recv, send_sems, recv_sems)
        begin()
        rels = [(rx, ry, rc) for rx in (0, 1) for ry in (0, 1) for rc in (0, 1)][1:]
        peers = [(jnp.where(rx, 1 - x, x), jnp.where(ry, 1 - y, y), jnp.where(rc, 1 - c, c)) for rx, ry, rc in rels]
        rcps = []
        for k, peer in enumerate(peers):
            cp = pltpu.make_async_remote_copy(src_ref=rep_ref, dst_ref=rep_all.at[me], send_sem=rsend_sems.at[k],
                                              recv_sem=rrecv_sems.at[k], device_id=peer, device_id_type=MESH_T)
            cp.start()
            rcps.append(cp)
        for k, peer in enumerate(peers):
            pltpu.make_async_remote_copy(src_ref=rep_ref, dst_ref=rep_all.at[_dev_index(*peer)], send_sem=rsend_sems.at[k],
                                         recv_sem=rrecv_sems.at[k], device_id=peer, device_id_type=MESH_T).wait_recv()
        end()
        for cp in rcps:
            cp.wait_send()
        mine.wait()

    outs = pl.pallas_call(
        body, in_specs=[ANY] * (n + 1), out_specs=[ANY] * (n + 1),
        out_shape=[_sds(a.shape, a.dtype) for a in parts] + [_sds((N_DEV,) + rep.shape, rep.dtype)],
        scratch_shapes=[pltpu.SemaphoreType.DMA((n, 3)), pltpu.SemaphoreType.DMA((n, 3)), pltpu.SemaphoreType.DMA((7,)),
                        pltpu.SemaphoreType.DMA((7,)), pltpu.SemaphoreType.DMA],
        name="exchange_chips")(*parts, rep)
    return outs[:n], outs[n]


def adamw(name, terms, w, m, v, transposed=False):
    r, c = w.shape[::-1] if transposed else w.shape
    rb = r if transposed else min(r, 262144 // c)
    c1 = 1.0 - ADAM_B1 ** ADAM_STEP
    c2 = 1.0 - ADAM_B2 ** ADAM_STEP
    nt = len(terms)

    def body(*refs):
        w_ref, m_ref, v_ref = refs[nt:nt + 3]
        g_ref, d_ref, nm_ref, nv_ref = refs[nt + 3:]
        g = refs[0][...].astype(F32)
        for t_ref in refs[1:nt]:
            g = g + t_ref[...].astype(F32)
        if transposed:
            g = g.T
        nm = ADAM_B1 * m_ref[...] + (1.0 - ADAM_B1) * g
        nv = ADAM_B2 * v_ref[...] + (1.0 - ADAM_B2) * (g * g)
        g_ref[...] = g
        nm_ref[...] = nm
        nv_ref[...] = nv
        d_ref[...] = -ADAM_LR * ((nm / c1) / (jnp.sqrt(nv / c2) + ADAM_EPS) + ADAM_WD * w_ref[...])

    blk = pl.BlockSpec((rb, c), lambda i: (i, 0))
    wblk = pl.BlockSpec((c, r), lambda i: (0, 0)) if transposed else blk
    tspecs = [blk if k is None else pl.BlockSpec((None, rb, c), lambda i, k=k: (k, i, 0)) for _, k in terms]
    return pl.pallas_call(body, grid=(r // rb,), in_specs=tspecs + [wblk] * 3, out_specs=[wblk] * 4,
                          out_shape=[_sds(w.shape)] * 4, name=name,
                          compiler_params=_cp(("parallel",)))(*[t for t, _ in terms], w, m, v)


W_IN_PIECES = ((0, 768, 0), (768, 896, C_LORA), (896, 1664, 768), (1664, 1920, C_Z), (1920, 2688, C_XBC),
               (2688, 2692, C_DT), (2692, 3716, C_HQ))
VRES_W = 32


def adamw_w_in(name, terms, w, m, v, vres=None):
    nt, nv = len(terms), 3 if vres else 0
    c1 = 1.0 - ADAM_B1 ** ADAM_STEP
    c2 = 1.0 - ADAM_B2 ** ADAM_STEP

    def body(*refs):
        w_ref, m_ref, v_ref = refs[nt:nt + 3]
        vres_refs = refs[nt + 3:nt + 3 + nv]
        outs = refs[nt + 3 + nv:nt + 7 + nv]
        vres_outs = refs[nt + 7 + nv:]
        g_all = refs[0][...].astype(F32)
        for t_ref in refs[1:nt]:
            g_all = g_all + t_ref[...].astype(F32)

        def update(g, wmv, out_refs, cols):
            nm = ADAM_B1 * wmv[1][:, cols] + (1.0 - ADAM_B1) * g
            nv_ = ADAM_B2 * wmv[2][:, cols] + (1.0 - ADAM_B2) * (g * g)
            out_refs[0][:, cols] = g
            out_refs[1][:, cols] = -ADAM_LR * ((nm / c1) / (jnp.sqrt(nv_ / c2) + ADAM_EPS) + ADAM_WD * wmv[0][:, cols])
            out_refs[2][:, cols] = nm
            out_refs[3][:, cols] = nv_

        for lo, hi, src in W_IN_PIECES:
            update(g_all[:, src:src + hi - lo], (w_ref, m_ref, v_ref), outs, slice(lo, hi))
        if vres:
            update(g_all[:, C_VRES:C_VRES + VRES_W], vres_refs, vres_outs, slice(0, VRES_W))

    r, c = terms[0][0].shape[-2:]
    tspecs = [_full((r, c)) if k is None else pl.BlockSpec((None, r, c), lambda i, k=k: (k, 0, 0)) for _, k in terms]
    wspec, vspec = _full(w.shape), _full((w.shape[0], VRES_W))
    outs = pl.pallas_call(body, grid=(1,), in_specs=tspecs + [wspec] * 3 + [vspec] * nv,
                          out_specs=[wspec] * 4 + [vspec] * (4 if vres else 0),
                          out_shape=[_sds(w.shape)] * 4 + [_sds((w.shape[0], VRES_W))] * (4 if vres else 0), name=name,
                          compiler_params=_cp(("arbitrary",)))(*[t for t, _ in terms], w, m, v, *(vres or ()))
    return list(outs[:4]), list(outs[4:])


SMS_ROWS = 16
N_BIG = 8
SMALL_SHARDED = (("rwkv_w2", (2, 32, 32)), ("rwkv_a2", (2, 32, 32)), ("rwkv_g2", (2, 64, 32)), ("rwkv_v2", (1, 32, 32)),
                 ("ssd_conv_w", (2, 4, 96)))
REPLICATED = (("lower_bounds", (2, 256)), ("mu_shift", (2, 896)), ("mu_vres", (1, 32)), ("rwkv_w0", (2, 256)),
              ("rwkv_a0", (2, 256)), ("rwkv_k_k", (2, 256)), ("rwkv_k_a", (2, 256)), ("rwkv_r_k", (2, 4, 64)),
              ("rwkv_lnx_w", (2, 256)), ("rwkv_lnx_b", (2, 256)), ("rwkv_v0", (1, 256)), ("ssd_conv_b", (2, 768)),
              ("ssd_dt_bias", (2, 4)), ("ssd_A_log", (2, 4)), ("ssd_D", (2, 4)), ("ssd_norm_w", (2, 256)),
              ("hgrn_norm_w", (2, 256)), ("ln1_w", (2, 1024)), ("ln1_b", (2, 1024)), ("ln2_w", (2, 1024)),
              ("ln2_b", (2, 1024)))


def _flat_rows(parts, rows):
    flat = jnp.concatenate([a.reshape(-1) for a in parts])
    return jnp.concatenate([flat, jnp.zeros((rows * PACK_W - flat.shape[0],), flat.dtype)]).reshape(rows, PACK_W)


def _local_arrays(d):
    return [_w_in_pad(d["w_in"][0], None), _w_in_pad(d["w_in"][1], d["w_in_vres"][0]), d["w_out"][0], d["w_out"][1],
            d["w_up"][0], d["w_up"][1], d["w_down"][0], d["w_down"][1],
            _flat_rows([d[n] for n, _ in SMALL_SHARDED], SMS_ROWS)]


def _unflat(rows2d, table):
    flat, out, o = rows2d.reshape(-1), {}, 0
    for name, shape in table:
        n = 1
        for s in shape:
            n *= s
        out[name] = flat[o:o + n].reshape(shape)
        o += n
    return out


def _from_local_arrays(arrs, rep, w_in_vres):
    d = dict(rep)
    d["w_in"], d["w_in_vres"] = jnp.stack([arrs[0], arrs[1]]), w_in_vres[None]
    d["w_out"] = jnp.stack([arrs[2], arrs[3]])
    d["w_up"] = jnp.stack([arrs[4], arrs[5]])
    d["w_down"] = jnp.stack([arrs[6], arrs[7]])
    d.update(_unflat(arrs[8], SMALL_SHARDED))
    return d


def _small_sharded_full(gs):
    small, flat, o = {}, gs.reshape(N_DEV, -1), 0
    for name, shape in SMALL_SHARDED:
        n = shape[0] * shape[1] * shape[2]
        blk = flat[:, o:o + n].reshape((N_DEV,) + shape)
        small[name] = blk.transpose(1, 2, 0, 3).reshape(shape[0], shape[1], N_DEV * shape[2])
        o += n
    return small


def _owner_blocks(g):
    return g.reshape(N_DEV, g.shape[0] // N_DEV, g.shape[1])


def _as_rows(shape):
    width = 1
    for s in shape[1:]:
        width *= s
    return shape[0], width


REP_2D = tuple((name, _as_rows(shape)) for name, shape in REPLICATED)
REP_ROW0 = tuple(sum(a for _, (a, _) in REP_2D[:i]) for i in range(len(REP_2D)))
REP_ROWS = sum(a for _, (a, _) in REP_2D)


def _rep_rows(d):
    rows = []
    for name, (a, b) in REP_2D:
        v = d[name].reshape(a, b)
        rows.append(v if b == PACK_W else jnp.concatenate([v, jnp.zeros((a, PACK_W - b), F32)], axis=1))
    return jnp.concatenate(rows, axis=0)


def adamw_replicated(rep_all, w, m, v):
    names = [name for name, _ in REP_2D]
    n = len(names)
    c1 = 1.0 - ADAM_B1 ** ADAM_STEP
    c2 = 1.0 - ADAM_B2 ** ADAM_STEP

    def body(*refs):
        rep_ref, w_refs, m_refs, v_refs, outs = refs[0], refs[1:1 + n], refs[1 + n:1 + 2 * n], refs[1 + 2 * n:1 + 3 * n], refs[1 + 3 * n:]
        for i, (_, (a, b)) in enumerate(REP_2D):
            r0 = REP_ROW0[i]
            g = rep_ref[0, r0:r0 + a, 0:b]
            for q in range(1, N_DEV):
                g = g + rep_ref[q, r0:r0 + a, 0:b]
            nm = ADAM_B1 * m_refs[i][...] + (1.0 - ADAM_B1) * g
            nv = ADAM_B2 * v_refs[i][...] + (1.0 - ADAM_B2) * (g * g)
            outs[i][...] = g
            outs[n + i][...] = -ADAM_LR * ((nm / c1) / (jnp.sqrt(nv / c2) + ADAM_EPS) + ADAM_WD * w_refs[i][...])
            outs[2 * n + i][...] = nm
            outs[3 * n + i][...] = nv

    flat = lambda d: [d[name].reshape(ab) for name, ab in REP_2D]
    pspecs = [_full(ab) for _, ab in REP_2D]
    res = pl.pallas_call(body, grid=(1,), in_specs=[_full(rep_all.shape)] + pspecs * 3, out_specs=pspecs * 4,
                         out_shape=[_sds(ab) for _, ab in REP_2D] * 4, name="adamw_replicated",
                         compiler_params=_cp(("arbitrary",)))(rep_all, *flat(w), *flat(m), *flat(v))
    shapes = dict(REPLICATED)
    return [{name: res[k * n + i].reshape(shapes[name]) for i, name in enumerate(names)} for k in range(4)]


def _small_send_arrays(small_grads):
    sms = []
    for name, shape in SMALL_SHARDED:
        g = small_grads[name].reshape(shape[0], shape[1], N_DEV, shape[2]).transpose(2, 0, 1, 3)
        sms.append(g.reshape(N_DEV, -1))
    sms = jnp.concatenate(sms, axis=1)
    sms = jnp.concatenate([sms, jnp.zeros((N_DEV, SMS_ROWS * PACK_W - sms.shape[1]), F32)], axis=1)
    return sms.reshape(N_DEV, SMS_ROWS, PACK_W), _rep_rows(small_grads)


BIG_KEYS = ("w_in", "w_out", "w_up_t", "w_down")


def _local_step(x, tgt, wts, raw, gather=(), pair_sums=None, reducer=None):
    consts = _consts()
    ps = [_layer_params(l, raw, consts) for l in range(DEPTH)]
    x1, sv0 = layer_fwd(0, x, None, wts[0], ps[0], gather[:4], late=bool(gather))
    wts1 = {"w_in": _full_rows(sv0["gathered"][3])} if gather else wts[1]
    x2, sv1 = layer_fwd(1, x1, sv0["fl"], wts1, ps[1], gather[4:], late=bool(gather))
    dy, lparts = loss_call(x2, tgt)
    loss = jnp.sum(lparts[::8, 0])
    dx1, dvfirst, g1 = layer_bwd(1, dy, None, sv1, sv1["wts"], ps[1], (), reducer)
    big1 = {k: g1[k] for k in BIG_KEYS}
    if reducer is None:
        dx0, _, g0 = layer_bwd(0, dx1, dvfirst, sv0, sv0["wts"], ps[0])
        early = None
    else:
        own_in1, parts_in1 = pair_sums("1b", {"w_in": g1["w_in"]})
        dx0, _, g0 = layer_bwd(0, dx1, dvfirst, sv0, sv0["wts"], ps[0], parts_in1, reducer)
        own, recv = {(1, "w_in"): own_in1[0]}, {(1, "w_in"): g0["exchanged"][0]}
        for l, g, first in ((1, g1, 0), (0, g0, 1)):
            for i, k in enumerate(LATE_KEYS):
                own[(l, k)], recv[(l, k)] = g["early_own"][i], g["exchanged"][first + i]
        early = (own, recv)
    big = [{k: g0[k] for k in BIG_KEYS}, big1]
    return loss, dx0, big, _natural_grads(g0, g1), early


WEIGHT_NAMES = ("lower_bounds", "w_in", "w_in_vres", "mu_shift", "mu_vres", "rwkv_w0", "rwkv_w2", "rwkv_a0", "rwkv_a2",
                "rwkv_g2", "rwkv_k_k", "rwkv_k_a", "rwkv_r_k", "rwkv_lnx_w", "rwkv_lnx_b", "rwkv_v0", "rwkv_v2",
                "ssd_conv_w", "ssd_conv_b", "ssd_dt_bias", "ssd_A_log", "ssd_D", "ssd_norm_w", "hgrn_norm_w", "w_out",
                "ln1_w", "ln1_b", "w_up", "w_down", "ln2_w", "ln2_b")


def kernel(x, lower_bounds, w_in, w_in_vres, mu_shift, mu_vres, rwkv_w0, rwkv_w2, rwkv_a0, rwkv_a2, rwkv_g2, rwkv_k_k, rwkv_k_a, rwkv_r_k, rwkv_lnx_w, rwkv_lnx_b, rwkv_v0, rwkv_v2, ssd_conv_w, ssd_conv_b, ssd_dt_bias, ssd_A_log, ssd_D, ssd_norm_w, hgrn_norm_w, w_out, ln1_w, ln1_b, w_up, w_down, ln2_w, ln2_b, loss_target, m_lower_bounds, m_w_in, m_w_in_vres, m_mu_shift, m_mu_vres, m_rwkv_w0, m_rwkv_w2, m_rwkv_a0, m_rwkv_a2, m_rwkv_g2, m_rwkv_k_k, m_rwkv_k_a, m_rwkv_r_k, m_rwkv_lnx_w, m_rwkv_lnx_b, m_rwkv_v0, m_rwkv_v2, m_ssd_conv_w, m_ssd_conv_b, m_ssd_dt_bias, m_ssd_A_log, m_ssd_D, m_ssd_norm_w, m_hgrn_norm_w, m_w_out, m_ln1_w, m_ln1_b, m_w_up, m_w_down, m_ln2_w, m_ln2_b, v_lower_bounds, v_w_in, v_w_in_vres, v_mu_shift, v_mu_vres, v_rwkv_w0, v_rwkv_w2, v_rwkv_a0, v_rwkv_a2, v_rwkv_g2, v_rwkv_k_k, v_rwkv_k_a, v_rwkv_r_k, v_rwkv_lnx_w, v_rwkv_lnx_b, v_rwkv_v0, v_rwkv_v2, v_ssd_conv_w, v_ssd_conv_b, v_ssd_dt_bias, v_ssd_A_log, v_ssd_D, v_ssd_norm_w, v_hgrn_norm_w, v_w_out, v_ln1_w, v_ln1_b, v_w_up, v_w_down, v_ln2_w, v_ln2_b):
    given = dict(locals())
    w = {n: given[n] for n in WEIGHT_NAMES}
    m_all, v_all = ({n: given[pre + n] for n in WEIGHT_NAMES} for pre in ("m_", "v_"))
    w_arrs, m_arrs, v_arrs = _local_arrays(w), _local_arrays(m_all), _local_arrays(v_all)
    wire = lambda a: (w_arrs[a].T if a in (4, 5) else w_arrs[a]).astype(BF16)
    gathered0 = all_gather([wire(0), w_arrs[N_BIG]])
    raw = {n: w[n] for n, _ in REPLICATED}
    raw.update(_small_sharded_full(gathered0[1]))
    mx, my, mc = lax.axis_index("x"), lax.axis_index("y"), lax.axis_index("c")
    slots = jnp.stack([_dev_index(cx, cy, mc) for cx, cy in _chips(mx, my)]).astype(jnp.int32)

    def reducer(tag, send, sib, n_f32=0):
        wire_dt = [BF16] * (len(send) - n_f32) + [F32] * n_f32
        res = [reduce_pair(f"reduce_pair{tag}_{i}", s, slots, sb, dt) for i, (s, sb, dt) in enumerate(zip(send, sib, wire_dt))]
        return [o for o, _ in res], [pt for _, pt in res]

    def pair_sums(tag, grads, extra=()):
        send = [_owner_blocks(g) for g in grads.values()] + list(extra)
        return reducer(tag, send, exchange_siblings(send, f"exchange_siblings{tag}"), len(extra))

    behind_scan = [wire(a) for a in (2, 4, 6, 1, 3, 5, 7)]
    loss, dx, big, small_grads, (own_by, recv_by) = _local_step(
        x[0], loss_target[0], [{"w_in": _full_rows(gathered0[0])}, None], raw, behind_scan, pair_sums, reducer)
    sms_send, rep = _small_send_arrays(small_grads)
    own0b, parts0b = pair_sums("0b", {"w_in": big[0]["w_in"]}, [sms_send])
    recv0b, rep_all = exchange_chips(parts0b, rep)
    own, recv = [None] * (N_BIG + 1), [None] * (N_BIG + 1)
    for (l, k), o in own_by.items():
        a = 2 * BIG_KEYS.index(k) + l
        own[a], recv[a] = o, recv_by[(l, k)]
    for a, o, r in zip((0, N_BIG), own0b, recv0b):
        own[a], recv[a] = o, r
    terms = lambda a: [(own[a], None), (recv[a], 0), (recv[a], 1), (recv[a], 2)]
    moments = [{n: given[pre + n] for n in ("w_in", "w_in_vres")} for pre in ("", "m_", "v_")]
    in0, _ = adamw_w_in("adamw0", terms(0), *[d["w_in"][0] for d in moments])
    in1, vres = adamw_w_in("adamw1", terms(1), *[d["w_in"][1] for d in moments], vres=[d["w_in_vres"][0] for d in moments])
    results = [in0, in1] + [adamw(f"adamw{a}", terms(a), w_arrs[a], m_arrs[a], v_arrs[a], transposed=a in (4, 5))
                            for a in range(2, N_BIG + 1)]
    rep_res = adamw_replicated(rep_all, w, m_all, v_all)
    loss = lax.psum(loss, ("x", "y", "c"))
    outs = [loss, dx[None]]
    for q in range(4):
        d = _from_local_arrays([res[q] for res in results], rep_res[q], vres[q])
        outs += [d[n] for n in WEIGHT_NAMES]
    return tuple(outs)
```

```python
import functools

import jax
import jax.numpy as jnp
from jax import lax
from jax.experimental import pallas as pl
from jax.experimental.pallas import tpu as pltpu

F32 = jnp.float32
BF16 = jnp.bfloat16
HI = lax.Precision.HIGHEST

N_DEV = 8
SEQ = 2048
D_MODEL = 1024
D_FF = 4096
DG = 256
NH = 4
HD = 64
DEPTH = 2
ALPHA = (2.0 * DEPTH) ** 0.25
LN_EPS = 1e-5
RMS_EPS = 1e-5
GN_EPS = HD * 1e-5
SSD_N = 128
SSD_CHUNK = 128
HGRN_CHUNK = 16
DILATED = ((128, 1), (512, 4), (2048, 16))

ADAM_LR, ADAM_B1, ADAM_B2, ADAM_EPS, ADAM_WD, ADAM_STEP = 0.001, 0.9, 0.999, 1e-08, 0.01, 10

PW = 4096
C_R, C_K, C_V = 0, 256, 512
C_AQ, C_AK, C_AV = 768, 1024, 1280
C_Z, C_XBC = 1536, 1792
C_HQ, C_HF, C_HI, C_HG = 2560, 2816, 3072, 3328
C_LORA, C_DT, C_VRES = 3584, 3712, 3840

RB = 256
VMEM_LIMIT = 56 * 1024 * 1024
PACK_W = 1024


def _cp(sem=None):
    return pltpu.CompilerParams(dimension_semantics=sem, vmem_limit_bytes=VMEM_LIMIT)


def _sds(shape, dt=F32):
    return jax.ShapeDtypeStruct(tuple(shape), dt)


def _rows(w, cb=0, rb=RB):
    return pl.BlockSpec((rb, w), lambda i: (i, cb))


def _full(shape):
    n = len(shape)
    return pl.BlockSpec(tuple(shape), lambda *_: (0,) * n)


def _sigmoid(x):
    return 1.0 / (1.0 + jnp.exp(-x))


def _silu(x):
    return x * _sigmoid(x)


def _softplus(x):
    return jnp.maximum(x, 0.0) + jnp.log(1.0 + jnp.exp(jnp.where(x > 0, -x, x)))


MID = lax.Precision.HIGH
NN, TN, NT = (((1,), (0,)), ((), ())), (((0,), (0,)), ((), ())), (((1,), (1,)), ((), ()))


def _dot(a, b):
    return lax.dot_general(a, b, NN, precision=MID, preferred_element_type=F32)


def _dot_tn(a, b):
    return lax.dot_general(a, b, TN, precision=MID, preferred_element_type=F32)


def _dot_nt(a, b):
    return lax.dot_general(a, b, NT, precision=MID, preferred_element_type=F32)


def _dotx(a, b):
    return lax.dot_general(a, b, NN, precision=HI, preferred_element_type=F32)


def _dotx_tn(a, b):
    return lax.dot_general(a, b, TN, precision=HI, preferred_element_type=F32)


def _seg_ones(n, seg):
    i = jnp.arange(n)
    return (i[:, None] // seg == i[None, :] // seg).astype(F32)


def _shift_down(x, s):
    row = lax.broadcasted_iota(jnp.int32, x.shape, 0)
    return jnp.where(row < s, 0.0, pltpu.roll(x, s, 0))


def _shift_up(x, s):
    n = x.shape[0]
    row = lax.broadcasted_iota(jnp.int32, x.shape, 0)
    return jnp.where(row >= n - s, 0.0, pltpu.roll(x, n - s, 0))


@functools.partial(jax.custom_vjp, nondiff_argnums=(1,))
def _tshift(x, s):
    return _shift_down(x, s)


def _tshift_fwd(x, s):
    return _shift_down(x, s), None


def _tshift_bwd(s, _, g):
    return (_shift_up(g, s),)


_tshift.defvjp(_tshift_fwd, _tshift_bwd)


def _map_fwd(name, fn, grid, ins, in_specs, out_shapes, out_specs):
    n_in = len(ins)

    def body(*refs):
        ys = fn(*[r[...] for r in refs[:n_in]])
        for r, y in zip(refs[n_in:], ys):
            r[...] = y

    return pl.pallas_call(body, grid=grid, in_specs=in_specs, out_specs=out_specs, out_shape=out_shapes,
                          name=name, compiler_params=_cp(("parallel",)))(*ins)


def _map_bwd(name, fn, grid, ins, in_specs, cts, ct_specs, want, acc=(), gout=None):
    n_in = len(ins)
    flat_cts = [c for group in cts for c in group]
    flat_specs = [s for group in ct_specs for s in group]
    n_ct = len(flat_cts)
    gout = gout or {}
    out_shapes = [gout[i][0] if i in gout else _sds(ins[i].shape) for i in want]
    out_specs = [gout[i][1] if i in gout else in_specs[i] for i in want]

    def body(*refs):
        xs = [r[...] for r in refs[:n_in]]
        cvals = [r[...] for r in refs[n_in:n_in + n_ct]]
        gouts = refs[n_in + n_ct:]
        cs, p = [], 0
        for group in cts:
            v = cvals[p]
            for q in range(1, len(group)):
                v = v + cvals[p + q]
            cs.append(v)
            p += len(group)

        def f(*wanted):
            full = list(xs)
            for i, w in zip(want, wanted):
                full[i] = w
            return tuple(fn(*full))

        _, vjp = jax.vjp(f, *[xs[i] for i in want])
        gs = vjp(tuple(cs))
        for o, i, g in zip(gouts, want, gs):
            if i in acc:
                @pl.when(pl.program_id(0) == 0)
                def _():
                    o[...] = jnp.zeros_like(o)

                o[...] += g
            else:
                o[...] = g

    sem = ("arbitrary",) if acc else ("parallel",)
    return pl.pallas_call(body, grid=grid, in_specs=list(in_specs) + flat_specs, out_specs=out_specs,
                          out_shape=out_shapes, name=name, compiler_params=_cp(sem))(*ins, *flat_cts)


def _addn(name, *arrs):
    n, c = arrs[0].shape

    def fn(*xs):
        r = xs[0]
        for x in xs[1:]:
            r = r + x
        return (r,)

    return _map_fwd(name, fn, (n // RB,), list(arrs), [_rows(c)] * len(arrs), [_sds((n, c))], [_rows(c)])[0]


MM_TILES = {"k1024": (2048, 512, 1024), "k4096": (1024, 1024, 1024), "wgrad_tall": (2048, 1024, 512),
            "wgrad_wide": (1024, 2048, 512)}


def _mm(name, a, b, mode, tm, tn, tk, add=None, add_scale=1.0, epilogue=None, out_dtype=F32):
    if mode == "nn":
        (m, k), n = a.shape, b.shape[1]
    elif mode == "nt":
        (m, k), n = a.shape, b.shape[0]
    else:
        (k, m), n = a.shape, b.shape[1]
    nk = k // tk
    dn = {"nn": (((1,), (0,)), ((), ())), "nt": (((1,), (1,)), ((), ())), "tn": (((0,), (0,)), ((), ()))}[mode]

    def body(*refs):
        a_ref, b_ref = refs[:2]
        add_ref = refs[2] if add is not None else None
        o_ref = refs[3] if add is not None else refs[2]
        prod = lax.dot_general(a_ref[...].astype(BF16), b_ref[...].astype(BF16), dn, preferred_element_type=F32)

        def finish(r):
            if epilogue == "relu2":
                r = jnp.maximum(r, 0.0)
                r = r * r
            elif epilogue == "relu2_bwd":
                r = r * (2.0 * jnp.sqrt(add_ref[...]))
            elif add is not None:
                r = r + add_scale * add_ref[...]
            o_ref[...] = r.astype(out_dtype)

        if nk == 1:
            finish(prod)
        else:
            acc = refs[-1]
            kk = pl.program_id(2)

            @pl.when(kk == 0)
            def _():
                acc[...] = prod

            @pl.when(kk > 0)
            def _():
                acc[...] += prod

            @pl.when(kk == nk - 1)
            def _():
                finish(acc[...])

    a_spec = pl.BlockSpec((tk, tm), lambda i, j, q: (q, i)) if mode == "tn" else pl.BlockSpec((tm, tk), lambda i, j, q: (i, q))
    b_spec = pl.BlockSpec((tn, tk), lambda i, j, q: (j, q)) if mode == "nt" else pl.BlockSpec((tk, tn), lambda i, j, q: (q, j))
    o_spec = pl.BlockSpec((tm, tn), lambda i, j, q: (i, j))
    ins, specs = [a, b], [a_spec, b_spec]
    if add is not None:
        ins.append(add)
        specs.append(o_spec)
    return pl.pallas_call(body, grid=(m // tm, n // tn, nk), in_specs=specs, out_specs=o_spec,
                          out_shape=_sds((m, n), out_dtype),
                          scratch_shapes=[pltpu.VMEM((tm, tn), F32)] if nk > 1 else [], name=name,
                          compiler_params=_cp(("parallel", "parallel", "arbitrary")))(*ins)


def _lerp_colmap(j):
    r = jnp.where(j < 6, j, jnp.where(j == 6, C_LORA // 128, C_VRES // 128))
    return (0, r)


def _lerp_fn(f, mu):
    return (f + (_tshift(f, 1) - f) * mu,)


def _lerp_specs():
    return [pl.BlockSpec((SEQ, 128), _lerp_colmap), pl.BlockSpec((1, 128), lambda j: (0, j))]


def lerp_fwd(l, proj, mu):
    return _map_fwd(f"lerp_fwd{l}", _lerp_fn, (8,), [proj, mu], _lerp_specs(), [_sds((SEQ, 1024))],
                    [pl.BlockSpec((SEQ, 128), lambda j: (0, j))])[0]


def lerp_bwd(l, proj, mu, dfl):
    n_in = 2

    def body(f_ref, mu_ref, g_ref, df_ref, dmu_ref):
        _, vjp = jax.vjp(_lerp_fn, f_ref[...], mu_ref[...])
        df, dmu = vjp((g_ref[...],))
        df_ref[...] = df
        dmu_ref[...] = dmu

    cspec = pl.BlockSpec((SEQ, 128), lambda j: (0, j))
    return pl.pallas_call(body, grid=(8,), in_specs=_lerp_specs() + [cspec],
                          out_specs=[cspec, pl.BlockSpec((1, 128), lambda j: (0, j))],
                          out_shape=[_sds((SEQ, 1024)), _sds((1, 1024))], name=f"lerp_bwd{l}",
                          compiler_params=_cp(("parallel",)))(proj, mu, dfl)


def _conv_fn(x, w, b):
    y = x * w[3:4, :] + _tshift(x, 1) * w[2:3, :] + _tshift(x, 2) * w[1:2, :] + _tshift(x, 3) * w[0:1, :] + b
    return (_silu(y),)


def _conv_specs():
    return [pl.BlockSpec((SEQ, 128), lambda j: (0, C_XBC // 128 + j)), pl.BlockSpec((4, 128), lambda j: (0, j)),
            pl.BlockSpec((1, 128), lambda j: (0, j))]


def conv_fwd(l, proj, w, b):
    return _map_fwd(f"conv_fwd{l}", _conv_fn, (6,), [proj, w, b], _conv_specs(), [_sds((SEQ, 768))],
                    [pl.BlockSpec((SEQ, 128), lambda j: (0, j))])[0]


def conv_bwd(l, proj, w, b, dxc):
    def body(x_ref, w_ref, b_ref, g_ref, dx_ref, dw_ref, db_ref):
        _, vjp = jax.vjp(_conv_fn, x_ref[...], w_ref[...], b_ref[...])
        dx, dw, db = vjp((g_ref[...],))
        dx_ref[...] = dx
        dw_ref[...] = dw
        db_ref[...] = db

    cspec = pl.BlockSpec((SEQ, 128), lambda j: (0, j))
    return pl.pallas_call(body, grid=(6,), in_specs=_conv_specs() + [cspec],
                          out_specs=[cspec, pl.BlockSpec((4, 128), lambda j: (0, j)), pl.BlockSpec((1, 128), lambda j: (0, j))],
                          out_shape=[_sds((SEQ, 768)), _sds((4, 768)), _sds((1, 768))], name=f"conv_bwd{l}",
                          compiler_params=_cp(("parallel",)))(proj, w, b, dxc)


def _rwkv_pre_fn(has_vres):
    def fn(fk, fv, flora, *rest):
        if has_vres:
            fvres, vfirst, w0, w2p, a0, a2p, g2p, k_k, k_a, v0, v2p, seg = rest
        else:
            w0, w2p, a0, a2p, g2p, k_k, k_a, seg = rest
        w_log = -_softplus(-(w0 + _dot(jnp.tanh(flora), w2p))) - 0.5
        w = jnp.exp(-jnp.exp(w_log))
        a = _sigmoid(a0 + _dot(flora, a2p))
        g = _dot(_sigmoid(flora), g2p)
        if has_vres:
            v2 = fv + (vfirst - fv) * _sigmoid(v0 + _dot(fvres, v2p))
        else:
            v2 = fv * 1.0
        kk = fk * k_k
        kk = kk / jnp.maximum(jnp.sqrt(_dot(kk * kk, seg)), 1e-12)
        k2 = fk * (1.0 + (a - 1.0) * k_a)
        return w, k2, v2, -kk, kk * a, g

    return fn


def _rwkv_pre_args(fl, vfirst, p, has_vres):
    ins = [fl, fl, fl]
    specs = [_rows(256, 1), _rows(256, 2), _rows(128, 6)]
    if has_vres:
        ins += [fl, vfirst]
        specs += [_rows(128, 7), _rows(256, 2)]
    names = ["w0", "w2p", "a0", "a2p", "g2p", "k_k", "k_a"] + (["v0", "v2p"] if has_vres else []) + ["seg64"]
    for nme in names:
        ins.append(p[nme])
        specs.append(_full(p[nme].shape))
    return ins, specs, names


def rwkv_pre_fwd(l, fl, vfirst, p):
    has_vres = l > 0
    ins, specs, _ = _rwkv_pre_args(fl, vfirst, p, has_vres)
    return _map_fwd(f"rwkv_pre_fwd{l}", _rwkv_pre_fn(has_vres), (SEQ // RB,), ins, specs,
                    [_sds((SEQ, DG))] * 6, [_rows(DG)] * 6)


def rwkv_pre_bwd(l, fl, vfirst, p, cts):
    has_vres = l > 0
    ins, specs, names = _rwkv_pre_args(fl, vfirst, p, has_vres)
    n_row = 5 if has_vres else 3
    want = list(range(n_row)) + [n_row + i for i, nme in enumerate(names) if nme != "seg64"]
    acc = tuple(w for w in want if w >= n_row)
    ct_specs = [[_rows(DG)] * len(g) for g in cts]
    gout = {0: (_sds((SEQ, DG)), _rows(DG)), 1: (_sds((SEQ, DG)), _rows(DG)), 2: (_sds((SEQ, 128)), _rows(128))}
    if has_vres:
        gout[3] = (_sds((SEQ, 128)), _rows(128))
        gout[4] = (_sds((SEQ, DG)), _rows(DG))
    gs = _map_bwd(f"rwkv_pre_bwd{l}", _rwkv_pre_fn(has_vres), (SEQ // RB,), ins, specs, cts, ct_specs, want, acc, gout)
    keys = ["fk", "fv", "flora"] + (["fvres", "vfirst"] if has_vres else []) + [nme for nme in names if nme != "seg64"]
    return dict(zip(keys, gs))


def _rwkv_post_fn(y, fr, k2, v2, g, lnx_w, lnx_b, r_k, seg):
    mu = _dot(y, seg) * (1.0 / HD)
    d = y - mu
    var = _dot(d * d, seg) * (1.0 / HD)
    yn = d * lax.rsqrt(var + GN_EPS) * lnx_w + lnx_b
    bonus = _dot(fr * k2 * r_k, seg) * v2
    return ((yn + bonus) * g,)


def _rwkv_post_args(y, fl, k2, v2, g, p):
    ins = [y, fl, k2, v2, g, p["lnx_w"], p["lnx_b"], p["r_k"], p["seg64"]]
    specs = [_rows(DG), _rows(DG, 0), _rows(DG), _rows(DG), _rows(DG)] + [_full(x.shape) for x in ins[5:]]
    return ins, specs


def rwkv_post_fwd(l, y, fl, k2, v2, g, p):
    ins, specs = _rwkv_post_args(y, fl, k2, v2, g, p)
    return _map_fwd(f"rwkv_post_fwd{l}", _rwkv_post_fn, (SEQ // RB,), ins, specs, [_sds((SEQ, DG))], [_rows(DG)])[0]


def rwkv_post_bwd(l, y, fl, k2, v2, g, p, dya):
    ins, specs = _rwkv_post_args(y, fl, k2, v2, g, p)
    gs = _map_bwd(f"rwkv_post_bwd{l}", _rwkv_post_fn, (SEQ // RB,), ins, specs, [[dya]], [[_rows(DG, 0)]],
                  want=[0, 1, 2, 3, 4, 5, 6, 7], acc=(5, 6, 7), gout={1: (_sds((SEQ, DG)), _rows(DG))})
    return dict(zip(["y", "fr", "k2", "v2", "g", "lnx_w", "lnx_b", "r_k"], gs))


SCAN_TB = 128


def _coltile8(rows8, dmask, ones_stack, parts):
    pieces, rest = [], rows8
    for q in range(parts):
        piece = rest.astype(BF16).astype(F32)
        if q < parts - 1:
            rest = rest - piece
        pieces.append((piece[:, None, :] * dmask[None]).reshape(8 * HD, DG).astype(BF16))
    x = pieces[0] if parts == 1 else jnp.concatenate(pieces, axis=1)
    return jnp.dot(x, ones_stack, preferred_element_type=F32).reshape(8, HD, DG)


def _coltiles_bf16(rows_list, dmask, ones_bf16):
    x = jnp.concatenate([(r8[:, None, :] * dmask[None]).reshape(8 * HD, DG).astype(BF16) for r8 in rows_list], axis=0)
    t = jnp.dot(x, ones_bf16, preferred_element_type=F32)
    return [t[q * 8 * HD:(q + 1) * 8 * HD].reshape(8, HD, DG) for q in range(len(rows_list))]


def _segrows8(x8, dmask, ones_bf16):
    t = jnp.dot(x8.reshape(8 * HD, DG).astype(BF16), ones_bf16, preferred_element_type=F32).reshape(8, HD, DG)
    return jnp.sum(t * dmask[None], axis=1)


def rwkv_scan_fwd(l, fl, w, k2, v2, c, b, p, gather=()):
    nblk = SEQ // SCAN_TB
    ng = len(gather)

    def body(*refs):
        r_ref, w_ref, k_ref, v_ref, c_ref, b_ref, ones_ref, dm_ref = refs[:8]
        y_ref, st_ref = refs[8 + ng:10 + ng]
        s_sc = refs[10 + 2 * ng]
        if ng:
            begin, middle, end = _gather_steps(refs[8:8 + ng], refs[10 + ng:10 + 2 * ng], *refs[11 + 2 * ng:])

            @pl.when(pl.program_id(0) == 0)
            def _():
                begin()

            @pl.when(pl.program_id(0) == (3 * nblk) // 4)
            def _():
                middle()

        @pl.when(pl.program_id(0) == 0)
        def _():
            s_sc[...] = jnp.zeros_like(s_sc)

        ones3, ones = ones_ref[...], ones_ref[0:DG, :]
        dmask = dm_ref[...]

        def group(gi, carry):
            t0 = pl.multiple_of(gi * 8, 8)
            sl = pl.ds(t0, 8)
            v8 = v_ref[sl, :]
            wt = _coltile8(w_ref[sl, :], dmask, ones3, 3)
            ct, bt, kt, rt = _coltiles_bf16([c_ref[sl, :], b_ref[sl, :], k_ref[sl, :], r_ref[sl, :]], dmask, ones)
            t = s_sc[...]
            for j in range(8):
                sa = jnp.sum(t * ct[j], axis=0, keepdims=True)
                t = t * wt[j] + bt[j] * sa + kt[j] * v8[j:j + 1, :]
                st_ref[t0 + j] = t
            s_sc[...] = t
            y_ref[sl, :] = jnp.sum(st_ref[sl] * rt, axis=1)
            return carry

        lax.fori_loop(0, SCAN_TB // 8, group, 0, unroll=4)

        if ng:
            @pl.when(pl.program_id(0) == nblk - 1)
            def _():
                end()

    row = pl.BlockSpec((SCAN_TB, DG), lambda i: (i, 0))
    ins = [fl, w, k2, v2, c, b, p["seg64x3_bf16"], p["dmask"]] + list(gather)
    specs = [row] * 6 + [_full((3 * DG, DG)), _full((HD, DG))] + [ANY] * ng
    outs = pl.pallas_call(body, grid=(nblk,), in_specs=specs,
                          out_specs=[row, pl.BlockSpec((SCAN_TB, HD, DG), lambda i: (i, 0, 0))] + [ANY] * ng,
                          out_shape=[_sds((SEQ, DG)), _sds((SEQ, HD, DG))] + _gather_shapes(gather),
                          scratch_shapes=[pltpu.VMEM((HD, DG), F32)] + (_gather_sems(ng) if ng else []),
                          name=f"rwkv_scan_fwd{l}", compiler_params=_cp(("arbitrary",)))(*ins)
    return outs[0], outs[1], list(outs[2:])


def rwkv_scan_bwd(l, fl, w, k2, v2, c, b, states, dy, p, exchange=()):
    nblk = SEQ // SCAN_TB
    nx = len(exchange)

    def body(*refs):
        r_ref, w_ref, k_ref, v_ref, c_ref, b_ref, dy_ref, st_ref, sp_ref, ones_ref, dm_ref = refs[:11]
        dr_ref, dw_ref, dk_ref, dv_ref, dc_ref, db_ref = refs[11 + nx:17 + nx]
        g_sc, prev_sc, d8_sc, dsa_sc = refs[17 + 2 * nx:21 + 2 * nx]
        i = pl.program_id(0)
        if nx:
            begin, end = _chip_exchange_steps(refs[11:11 + nx], refs[17 + nx:17 + 2 * nx], *refs[21 + 2 * nx:])

            @pl.when(i == 0)
            def _():
                begin()

        @pl.when(i == 0)
        def _():
            g_sc[...] = jnp.zeros_like(g_sc)

        ones3, ones = ones_ref[...], ones_ref[0:DG, :]
        dmask = dm_ref[...]
        first_block = i == nblk - 1

        def group(gr, carry):
            gi = SCAN_TB // 8 - 1 - gr
            t0 = pl.multiple_of(gi * 8, 8)
            sl = pl.ds(t0, 8)
            v8, dy8 = v_ref[sl, :], dy_ref[sl, :]
            t8 = st_ref[sl]
            @pl.when(gi > 0)
            def _():
                prev_sc[0] = st_ref[t0 - 1]

            @pl.when(gi == 0)
            def _():
                prev_sc[0] = jnp.where(first_block, 0.0, sp_ref[0])

            for j in range(1, 8):
                prev_sc[j] = t8[j - 1]
            tp8 = prev_sc[...]
            wt = _coltile8(w_ref[sl, :], dmask, ones3, 3)
            ct, bt, kt, rt = _coltiles_bf16([c_ref[sl, :], b_ref[sl, :], k_ref[sl, :], r_ref[sl, :]], dmask, ones)
            sa8 = jnp.sum(tp8 * ct, axis=1)
            g = g_sc[...]
            for j in range(7, -1, -1):
                g = g + rt[j] * dy8[j:j + 1, :]
                d8_sc[j] = g
                dsa = jnp.sum(g * bt[j], axis=0, keepdims=True)
                dsa_sc[j:j + 1, :] = dsa
                g = g * wt[j] + ct[j] * dsa
            g_sc[...] = g
            d8 = d8_sc[...]
            dsa8 = dsa_sc[...]
            dv_ref[sl, :] = jnp.sum(d8 * kt, axis=1)
            dr_ref[sl, :] = _segrows8(t8 * dy8[:, None, :], dmask, ones)
            dk_ref[sl, :] = _segrows8(d8 * v8[:, None, :], dmask, ones)
            dw_ref[sl, :] = _segrows8(tp8 * d8, dmask, ones)
            db_ref[sl, :] = _segrows8(d8 * sa8[:, None, :], dmask, ones)
            dc_ref[sl, :] = _segrows8(tp8 * dsa8[:, None, :], dmask, ones)
            return carry

        lax.fori_loop(0, SCAN_TB // 8, group, 0, unroll=4)

        if nx:
            @pl.when(i == nblk - 1)
            def _():
                end()

    row = pl.BlockSpec((SCAN_TB, DG), lambda i: (nblk - 1 - i, 0))
    st_spec = pl.BlockSpec((SCAN_TB, HD, DG), lambda i: (nblk - 1 - i, 0, 0))
    sp_spec = pl.BlockSpec((1, HD, DG), lambda i: (jnp.maximum((nblk - 1 - i) * SCAN_TB - 1, 0), 0, 0))
    ins = [fl, w, k2, v2, c, b, dy, states, states, p["seg64x3_bf16"], p["dmask"]] + list(exchange)
    specs = [row] * 7 + [st_spec, sp_spec, _full((3 * DG, DG)), _full((HD, DG))] + [ANY] * nx
    tile8 = pltpu.VMEM((8, HD, DG), F32)
    sems = [pltpu.SemaphoreType.DMA((nx, 3)), pltpu.SemaphoreType.DMA((nx, 3))] if nx else []
    outs = pl.pallas_call(body, grid=(nblk,), in_specs=specs, out_specs=[row] * 6 + [ANY] * nx,
                          out_shape=[_sds((SEQ, DG))] * 6 + [_sds(a.shape, a.dtype) for a in exchange],
                          scratch_shapes=[pltpu.VMEM((HD, DG), F32), tile8, tile8, pltpu.VMEM((8, DG), F32)] + sems,
                          name=f"rwkv_scan_bwd{l}", compiler_params=_cp(("arbitrary",)))(*ins)
    return outs[:6], list(outs[6:])


HG_ROWS = 256


HG_NC = HG_ROWS // HGRN_CHUNK


def _hgrn_block_fn(layer):
    def fn(hq, hf, hi, hg, sprev, lb0, lb1, norm_w, seg, bd, tri_bd, ones_bd, first_row, causal):
        e0 = jnp.exp(lb0 - jnp.maximum(lb0, lb1))
        e1 = jnp.exp(lb1 - jnp.maximum(lb0, lb1))
        sm0, sm1 = e0 / (e0 + e1), e1 / (e0 + e1)
        lb = (sm0 - sm0) if layer == 0 else ((sm0 + sm1) - sm0)
        forget = lb + (1.0 - lb) * _sigmoid(hf)
        logf = jnp.log(forget)
        kk = 1.0 - forget
        q = _silu(hq)
        c, nc = HGRN_CHUNK, HG_NC
        b = _dotx(tri_bd, logf)
        bl = _dotx(ones_bd, logf)
        split = lambda t: t.reshape(nc, c, DG)
        b4 = split(b)
        diff = (b4[:, :, None, :] - b4[:, None, :, :]).reshape(nc * c * c, DG)
        dec = jnp.exp(jnp.where(causal > 0.5, diff, -1e30))
        qrep = jnp.broadcast_to(split(q)[:, :, None, :], (nc, c, c, DG)).reshape(nc * c * c, DG)
        ktil = jnp.broadcast_to(split(kk)[:, None, :, :], (nc, c, c, DG)).reshape(nc * c * c, DG)
        vtil = jnp.broadcast_to(split(hi)[:, None, :, :], (nc, c, c, DG)).reshape(nc * c * c, DG)
        att = _dot(qrep * ktil * dec, seg)
        o_intra = jnp.sum((att * vtil).reshape(nc * c, c, DG), axis=1)
        kd4 = split(kk * jnp.exp(bl - b))
        qe4 = split(q * jnp.exp(b))
        v4 = split(hi)
        tot = jnp.exp(_dotx(first_row, bl))
        s, o_inter = sprev, []
        for ci in range(nc):
            o_inter.append(_dot_nt(qe4[ci], s))
            s = s * tot[ci:ci + 1, :] + _dot_tn(v4[ci], kd4[ci]) * bd
        o = o_intra + jnp.concatenate(o_inter, axis=0)
        ms = _dot(o * o, seg) * (1.0 / HD)
        y = o * lax.rsqrt(ms + RMS_EPS) * norm_w * _silu(hg)
        return y, s

    return fn


def _hgrn_consts(p):
    return [p["seg64"], p["seg64"], p["tri_chunks"], p["ones_chunks"], p["first_row"], p["causal_blk"]]


def hgrn_fwd(l, proj, p):
    fn = _hgrn_block_fn(l)

    def body(hq_ref, hf_ref, hi_ref, hg_ref, *rest):
        const_refs, (y_ref, st_ref, s_sc) = rest[:-3], rest[-3:]

        @pl.when(pl.program_id(0) == 0)
        def _():
            s_sc[...] = jnp.zeros_like(s_sc)

        sprev = s_sc[...]
        st_ref[0] = sprev
        y, snext = fn(hq_ref[...], hf_ref[...], hi_ref[...], hg_ref[...], sprev, *[r[...] for r in const_refs])
        y_ref[...] = y
        s_sc[...] = snext

    rows = lambda cb: pl.BlockSpec((HG_ROWS, DG), lambda i: (i, cb))
    ins = [proj, proj, proj, proj, p["lb0"], p["lb1"], p["hgrn_norm_w"]] + _hgrn_consts(p)
    specs = [rows(C_HQ // DG), rows(C_HF // DG), rows(C_HI // DG), rows(C_HG // DG)] + [_full(x.shape) for x in ins[4:]]
    return pl.pallas_call(body, grid=(SEQ // HG_ROWS,), in_specs=specs,
                          out_specs=[rows(0), pl.BlockSpec((1, DG, DG), lambda i: (i, 0, 0))],
                          out_shape=[_sds((SEQ, DG)), _sds((SEQ // HG_ROWS, DG, DG))],
                          scratch_shapes=[pltpu.VMEM((DG, DG), F32)], name=f"hgrn_fwd{l}",
                          compiler_params=_cp(("arbitrary",)))(*ins)


def hgrn_bwd(l, proj, states, dy, p, sibling=(), dy_col=0):
    fn = _hgrn_block_fn(l)
    nblk = SEQ // HG_ROWS
    n_const = len(_hgrn_consts(p))
    ns = len(sibling)

    def body(hq_ref, hf_ref, hi_ref, hg_ref, st_ref, dy_ref, lb0_ref, lb1_ref, nw_ref, *rest):
        const_refs, rest = rest[:n_const], rest[n_const:]
        dp_ref, dlb0_ref, dlb1_ref, dnw_ref = rest[ns:ns + 4]
        ds_sc = rest[2 * ns + 4]
        if ns:
            begin, end = _sibling_steps(rest[:ns], rest[ns + 4:2 * ns + 4], *rest[2 * ns + 5:])

            @pl.when(pl.program_id(0) == 0)
            def _():
                begin()

        @pl.when(pl.program_id(0) == 0)
        def _():
            ds_sc[...] = jnp.zeros_like(ds_sc)
            dlb0_ref[...] = jnp.zeros_like(dlb0_ref)
            dlb1_ref[...] = jnp.zeros_like(dlb1_ref)
            dnw_ref[...] = jnp.zeros_like(dnw_ref)

        consts = [r[...] for r in const_refs]
        f = lambda hq, hf, hi, hg, sp, b0, b1, nw: fn(hq, hf, hi, hg, sp, b0, b1, nw, *consts)
        _, vjp = jax.vjp(f, hq_ref[...], hf_ref[...], hi_ref[...], hg_ref[...], st_ref[0], lb0_ref[...], lb1_ref[...],
                         nw_ref[...])
        dhq, dhf, dhi, dhg, dsp, dlb0, dlb1, dnw = vjp((dy_ref[...], ds_sc[...]))
        dp_ref[:, 0:DG] = dhq
        dp_ref[:, DG:2 * DG] = dhf
        dp_ref[:, 2 * DG:3 * DG] = dhi
        dp_ref[:, 3 * DG:4 * DG] = dhg
        ds_sc[...] = dsp
        dlb0_ref[...] += dlb0
        dlb1_ref[...] += dlb1
        dnw_ref[...] += dnw

        if ns:
            @pl.when(pl.program_id(0) == nblk - 1)
            def _():
                end()

    rows = lambda cb: pl.BlockSpec((HG_ROWS, DG), lambda i: (nblk - 1 - i, cb))
    ins = [proj, proj, proj, proj, states, dy, p["lb0"], p["lb1"], p["hgrn_norm_w"]] + _hgrn_consts(p)
    specs = [rows(C_HQ // DG), rows(C_HF // DG), rows(C_HI // DG), rows(C_HG // DG),
             pl.BlockSpec((1, DG, DG), lambda i: (nblk - 1 - i, 0, 0)), rows(dy_col)] + [_full(x.shape) for x in ins[6:]]
    sem = pltpu.SemaphoreType.DMA((max(ns, 1), 4))
    outs = pl.pallas_call(body, grid=(nblk,), in_specs=specs + [ANY] * ns,
                          out_specs=[pl.BlockSpec((HG_ROWS, 4 * DG), lambda i: (nblk - 1 - i, 0)), _full((1, DG)),
                                     _full((1, DG)), _full((1, DG))] + [ANY] * ns,
                          out_shape=[_sds((SEQ, 4 * DG)), _sds((1, DG)), _sds((1, DG)), _sds((1, DG))]
                          + [_sds((4,) + a.shape[1:], a.dtype) for a in sibling],
                          scratch_shapes=[pltpu.VMEM((DG, DG), F32)] + ([sem, sem] if ns else []), name=f"hgrn_bwd{l}",
                          compiler_params=_cp(("arbitrary",)))(*ins, *sibling)
    return outs[:4], list(outs[4:])


def _ssd_chunk_fn(z, xs, bm, cm, dtr, sprev, dt_bias, a_log, d_par, norm_w, e128, tri, trit, seg128, ones128):
    lc = SSD_CHUNK
    dt = _softplus(dtr + dt_bias)
    a = -jnp.exp(a_log)
    da = dt * a * (lax.broadcasted_iota(jnp.int32, (1, 128), 1) < NH).astype(F32)
    cs = _dotx(tri, da)
    cst = _dotx_tn(da, trit)
    cs_b = _dotx(cs, e128)
    dt_b = _dotx(dt, e128)
    csl_b = _dotx(jnp.sum(da, axis=0, keepdims=True), e128)
    xdt = xs * dt_b
    lane = lax.broadcasted_iota(jnp.int32, (1, DG), 1)
    rowi = lax.broadcasted_iota(jnp.int32, (lc, lc), 0)
    coli = lax.broadcasted_iota(jnp.int32, (lc, lc), 1)
    y = jnp.zeros((lc, DG), F32)
    snew = jnp.zeros((DG, SSD_N), F32)
    d_b = jnp.zeros((1, DG), F32)
    wdec = xdt * jnp.exp(csl_b - cs_b)
    for g in range(2):
        bg = bm[:, g * SSD_N:(g + 1) * SSD_N]
        cg = cm[:, g * SSD_N:(g + 1) * SSD_N]
        gmat = _dot_nt(cg, bg)
        gmask = ((lane // 128) == g).astype(F32)
        snew = snew + _dot_tn(wdec * gmask, bg)
        y = y + _dot_nt(cg, sprev) * gmask * jnp.exp(cs_b)
        for hh in range(2):
            h = 2 * g + hh
            seg = jnp.where(rowi >= coli, cs[:, h:h + 1] - cst[h:h + 1, :], -1e30)
            hmask = ((lane // HD) == h).astype(F32)
            y = y + _dot(gmat * jnp.exp(seg), xdt * hmask)
            d_b = d_b + d_par[:, h:h + 1] * hmask
    cd = jnp.exp(_dotx_tn(_dotx(da, e128), ones128))
    snext = sprev * cd + snew
    y = y + xs * d_b
    y = y * _silu(z)
    ms = _dot(y * y, seg128) * (1.0 / 128.0)
    return y * lax.rsqrt(ms + RMS_EPS) * norm_w, snext


def ssd_fwd(l, proj, xc, p):
    nc = SEQ // SSD_CHUNK

    def body(z_ref, xs_ref, b_ref, c_ref, dt_ref, dtb_ref, al_ref, d_ref, nw_ref, e_ref, tri_ref, trit_ref, sg_ref,
             on_ref, y_ref, st_ref, s_sc):
        @pl.when(pl.program_id(0) == 0)
        def _():
            s_sc[...] = jnp.zeros_like(s_sc)

        sprev = s_sc[...]
        st_ref[0] = sprev
        y, snext = _ssd_chunk_fn(z_ref[...], xs_ref[...], b_ref[...], c_ref[...], dt_ref[...], sprev, dtb_ref[...],
                                 al_ref[...], d_ref[...], nw_ref[...], e_ref[...], tri_ref[...], trit_ref[...],
                                 sg_ref[...], on_ref[...])
        y_ref[...] = y
        s_sc[...] = snext

    rw = lambda w, cb: pl.BlockSpec((SSD_CHUNK, w), lambda i: (i, cb))
    ins = [proj, xc, xc, xc, proj, p["dt_bias"], p["a_log"], p["ssd_d"], p["ssd_norm_w"], p["e128"], p["tri128"],
           p["tri128t"], p["seg128"], p["ones128"]]
    specs = [rw(DG, C_Z // DG), rw(DG, 0), rw(DG, 1), rw(DG, 2), rw(128, C_DT // 128)] + [_full(x.shape) for x in ins[5:]]
    return pl.pallas_call(body, grid=(nc,), in_specs=specs,
                          out_specs=[rw(DG, 0), pl.BlockSpec((1, DG, SSD_N), lambda i: (i, 0, 0))],
                          out_shape=[_sds((SEQ, DG)), _sds((nc, DG, SSD_N))],
                          scratch_shapes=[pltpu.VMEM((DG, SSD_N), F32)], name=f"ssd_fwd{l}",
                          compiler_params=_cp(("arbitrary",)))(*ins)


def ssd_bwd(l, proj, xc, states, dy, p, dy_col=0):
    nc = SEQ // SSD_CHUNK

    def body(z_ref, xs_ref, b_ref, c_ref, dt_ref, st_ref, dy_ref, dtb_ref, al_ref, d_ref, nw_ref, e_ref, tri_ref,
             trit_ref, sg_ref, on_ref, dz_ref, dxc_ref, ddt_ref, ddtb_ref, dal_ref, dd_ref, dnw_ref, ds_sc):
        @pl.when(pl.program_id(0) == 0)
        def _():
            ds_sc[...] = jnp.zeros_like(ds_sc)
            ddtb_ref[...] = jnp.zeros_like(ddtb_ref)
            dal_ref[...] = jnp.zeros_like(dal_ref)
            dd_ref[...] = jnp.zeros_like(dd_ref)
            dnw_ref[...] = jnp.zeros_like(dnw_ref)

        consts = (e_ref[...], tri_ref[...], trit_ref[...], sg_ref[...], on_ref[...])
        f = lambda *a: _ssd_chunk_fn(*a, *consts)
        _, vjp = jax.vjp(f, z_ref[...], xs_ref[...], b_ref[...], c_ref[...], dt_ref[...], st_ref[0], dtb_ref[...],
                         al_ref[...], d_ref[...], nw_ref[...])
        dz, dxs, db, dc, ddt, dsp, ddtb, dal, dd, dnw = vjp((dy_ref[...], ds_sc[...]))
        dz_ref[...] = dz
        dxc_ref[:, 0:DG] = dxs
        dxc_ref[:, DG:2 * DG] = db
        dxc_ref[:, 2 * DG:3 * DG] = dc
        ddt_ref[...] = ddt
        ds_sc[...] = dsp
        ddtb_ref[...] += ddtb
        dal_ref[...] += dal
        dd_ref[...] += dd
        dnw_ref[...] += dnw

    rw = lambda w, cb: pl.BlockSpec((SSD_CHUNK, w), lambda i: (nc - 1 - i, cb))
    ins = [proj, xc, xc, xc, proj, states, dy, p["dt_bias"], p["a_log"], p["ssd_d"], p["ssd_norm_w"], p["e128"],
           p["tri128"], p["tri128t"], p["seg128"], p["ones128"]]
    specs = [rw(DG, C_Z // DG), rw(DG, 0), rw(DG, 1), rw(DG, 2), rw(128, C_DT // 128),
             pl.BlockSpec((1, DG, SSD_N), lambda i: (nc - 1 - i, 0, 0)), rw(DG, dy_col)] + [_full(x.shape) for x in ins[7:]]
    return pl.pallas_call(body, grid=(nc,), in_specs=specs,
                          out_specs=[rw(DG, 0), rw(3 * DG, 0), rw(128, 0), _full((1, 128)), _full((1, 128)), _full((1, 128)),
                                     _full((1, DG))],
                          out_shape=[_sds((SEQ, DG)), _sds((SEQ, 3 * DG)), _sds((SEQ, 128)), _sds((1, 128)), _sds((1, 128)),
                                     _sds((1, 128)), _sds((1, DG))],
                          scratch_shapes=[pltpu.VMEM((DG, SSD_N), F32)], name=f"ssd_bwd{l}",
                          compiler_params=_cp(("arbitrary",)))(*ins)


ATT_BLK = 128


def _att_geometry(dil):
    i = lax.broadcasted_iota(jnp.int32, (ATT_BLK, ATT_BLK), 0)
    j = lax.broadcasted_iota(jnp.int32, (ATT_BLK, ATT_BLK), 1)
    return ((i - j) * dil).astype(F32), ((ATT_BLK + i - j) * dil).astype(F32), j <= i, j >= i


def _att_scores(qn, kc, kp, h, geom, has_prev):
    dist_c, dist_p, m_c, m_pj = geom
    slope = 2.0 ** (-8.0 * (h + 1) / NH)
    scale = HD ** -0.5
    s_c = _dot_nt(qn, kc) * scale - slope * dist_c
    s_p = _dot_nt(qn, kp) * scale - slope * dist_p
    m_p = jnp.logical_and(m_pj, has_prev)
    return jnp.where(m_c, s_c, -1e30), jnp.where(m_p, s_p, -1e30), m_c, m_p


def _sub_spec(ln, width, col):
    return pl.BlockSpec((ln, DG), lambda z: (0, z * (width // DG) + col // DG))


QKV_W = 3 * DG


def attn_branch_fwd(l, bi, qkv, dil):
    ln = SEQ // dil
    nb = ln // ATT_BLK

    def body(q_ref, k_ref, v_ref, o_ref, l_ref):
        geom = _att_geometry(dil)

        def blk(n, carry):
            r0 = pl.multiple_of(n * ATT_BLK, ATT_BLK)
            rp = pl.multiple_of(jnp.maximum(n - 1, 0) * ATT_BLK, ATT_BLK)
            cur, prv = pl.ds(r0, ATT_BLK), pl.ds(rp, ATT_BLK)
            for h in range(NH):
                hs = slice(h * HD, (h + 1) * HD)
                qn, kc, vc, kp, vp = q_ref[cur, hs], k_ref[cur, hs], v_ref[cur, hs], k_ref[prv, hs], v_ref[prv, hs]
                s_c, s_p, m_c, m_p = _att_scores(qn, kc, kp, h, geom, n > 0)
                m = jnp.maximum(jnp.max(s_c, axis=1, keepdims=True), jnp.max(s_p, axis=1, keepdims=True))
                p_c = jnp.where(m_c, jnp.exp(s_c - m), 0.0)
                p_p = jnp.where(m_p, jnp.exp(s_p - m), 0.0)
                den = jnp.sum(p_c, axis=1, keepdims=True) + jnp.sum(p_p, axis=1, keepdims=True)
                o_ref[cur, hs] = (_dot(p_c, vc) + _dot(p_p, vp)) / den
                l_ref[cur, hs] = jnp.broadcast_to(m + jnp.log(den), (ATT_BLK, HD))
            return carry

        lax.fori_loop(0, nb, blk, 0)

    pv = qkv.reshape(ln, dil * QKV_W)
    out = pl.BlockSpec((ln, DG), lambda z: (0, z))
    o, lse = pl.pallas_call(body, grid=(dil,), in_specs=[_sub_spec(ln, QKV_W, 0), _sub_spec(ln, QKV_W, DG), _sub_spec(ln, QKV_W, 2 * DG)],
                            out_specs=[out, out], out_shape=[_sds((ln, dil * DG))] * 2, name=f"attn_fwd{l}_{bi}",
                            compiler_params=_cp(("parallel",)))(pv, pv, pv)
    return o.reshape(SEQ, DG), lse.reshape(SEQ, DG)


def attn_branch_bwd(l, bi, qkv, dil, dyb, lse_all, delta):
    ln = SEQ // dil
    nb = ln // ATT_BLK
    scale = HD ** -0.5

    def body(q_ref, k_ref, v_ref, do_ref, l_ref, dl_ref, dq_ref, dk_ref, dv_ref):
        dk_ref[...] = jnp.zeros_like(dk_ref)
        dv_ref[...] = jnp.zeros_like(dv_ref)
        geom = _att_geometry(dil)

        def blk(n, carry):
            r0 = pl.multiple_of(n * ATT_BLK, ATT_BLK)
            rp = pl.multiple_of(jnp.maximum(n - 1, 0) * ATT_BLK, ATT_BLK)
            cur, prv = pl.ds(r0, ATT_BLK), pl.ds(rp, ATT_BLK)
            for h in range(NH):
                hs = slice(h * HD, (h + 1) * HD)
                qn, don = q_ref[cur, hs], do_ref[cur, hs]
                lse, dlt = l_ref[cur, h * HD:h * HD + 1], dl_ref[cur, h * HD:h * HD + 1]
                kc, vc, kp, vp = k_ref[cur, hs], v_ref[cur, hs], k_ref[prv, hs], v_ref[prv, hs]
                s_c, s_p, m_c, m_p = _att_scores(qn, kc, kp, h, geom, n > 0)
                p_c = jnp.where(m_c, jnp.exp(s_c - lse), 0.0)
                p_p = jnp.where(m_p, jnp.exp(s_p - lse), 0.0)
                ds_c = p_c * (_dot_nt(don, vc) - dlt)
                ds_p = p_p * (_dot_nt(don, vp) - dlt)
                dq_ref[cur, hs] = (_dot(ds_c, kc) + _dot(ds_p, kp)) * scale
                dv_ref[prv, hs] += _dot_tn(p_p, don)
                dk_ref[prv, hs] += _dot_tn(ds_p, qn) * scale
                dv_ref[cur, hs] += _dot_tn(p_c, don)
                dk_ref[cur, hs] += _dot_tn(ds_c, qn) * scale
            return carry

        lax.fori_loop(0, nb, blk, 0)

    pv = qkv.reshape(ln, dil * QKV_W)
    sub = lambda t: t.reshape(ln, dil * DG)
    row = pl.BlockSpec((ln, DG), lambda z: (0, z))
    outs = pl.pallas_call(body, grid=(dil,),
                          in_specs=[_sub_spec(ln, QKV_W, 0), _sub_spec(ln, QKV_W, DG), _sub_spec(ln, QKV_W, 2 * DG), row, row, row],
                          out_specs=[row] * 3, out_shape=[_sds((ln, dil * DG))] * 3, name=f"attn_bwd{l}_{bi}",
                          compiler_params=_cp(("parallel",)))(pv, pv, pv, sub(dyb), sub(lse_all), sub(delta))
    return [t.reshape(SEQ, DG) for t in outs]


def _attn_merge_fn(o1, o2, o3, l1, l2, l3):
    m = jnp.maximum(jnp.maximum(l1, l2), l3)
    w1, w2, w3 = jnp.exp(l1 - m), jnp.exp(l2 - m), jnp.exp(l3 - m)
    den = w1 + w2 + w3
    return (w1 * o1 + w2 * o2 + w3 * o3) / den, m + jnp.log(den)


def attn_merge(l, os_, ls_):
    ins = list(os_) + list(ls_)
    return _map_fwd(f"attn_merge{l}", _attn_merge_fn, (SEQ // RB,), ins, [_rows(DG)] * 6, [_sds((SEQ, DG))] * 2,
                    [_rows(DG)] * 2)


def attn_delta(l, dyb, yb, seg):
    fn = lambda d, y, s: (_dot(d * y, s),)
    return _map_fwd(f"attn_delta{l}", fn, (SEQ // RB,), [dyb, yb, seg], [_rows(DG), _rows(DG), _full((DG, DG))],
                    [_sds((SEQ, DG))], [_rows(DG)])[0]


def _ln_fn(x, mix, w, b):
    h = ALPHA * x + mix
    mu = jnp.mean(h, axis=-1, keepdims=True)
    d = h - mu
    var = jnp.mean(d * d, axis=-1, keepdims=True)
    return (d * lax.rsqrt(var + LN_EPS) * w + b,)


def ln_fwd(name, x, mix, w, b):
    specs = [_rows(D_MODEL), _rows(D_MODEL), _full((1, D_MODEL)), _full((1, D_MODEL))]
    return _map_fwd(name, _ln_fn, (SEQ // RB,), [x, mix, w, b], specs, [_sds((SEQ, D_MODEL))], [_rows(D_MODEL)])[0]


def ln_bwd(name, x, mix, w, b, dy):
    specs = [_rows(D_MODEL), _rows(D_MODEL), _full((1, D_MODEL)), _full((1, D_MODEL))]
    return _map_bwd(name, _ln_fn, (SEQ // RB,), [x, mix, w, b], specs, [[dy]], [[_rows(D_MODEL)]], want=[1, 2, 3],
                    acc=(2, 3))


def loss_call(y, tgt):
    def fn(yy, tt):
        e = yy - tt
        part = 0.5 * jnp.sum(jnp.sum(e * e, axis=-1, keepdims=True) * (1.0 / D_MODEL), axis=0, keepdims=True)
        return e * (1.0 / D_MODEL), jnp.broadcast_to(part, (8, 128))

    return _map_fwd("loss", fn, (SEQ // RB,), [y, tgt], [_rows(D_MODEL)] * 2,
                    [_sds((SEQ, D_MODEL)), _sds((SEQ // RB * 8, 128))],
                    [_rows(D_MODEL), pl.BlockSpec((8, 128), lambda i: (i, 0))])


LATE_KEYS = ("w_out", "w_up_t", "w_down")


def _full_rows(g):
    return g.reshape(N_DEV * g.shape[1], g.shape[2])


def layer_fwd(l, x, vfirst, wts, p, gather=(), late=False):
    sv = {"x": x}
    proj = _mm(f"mm_in{l}", x, wts["w_in"], "nn", *MM_TILES["k1024"])
    fl = lerp_fwd(l, proj, p["mu"])
    xc = conv_fwd(l, proj, p["conv_w"], p["conv_b"])
    w, k2, v2, c, b, g = rwkv_pre_fwd(l, fl, vfirst, p)
    y_scan, states, sv["gathered"] = rwkv_scan_fwd(l, fl, w, k2, v2, c, b, p, gather)
    if late:
        wts = dict(wts, **dict(zip(LATE_KEYS, [_full_rows(g) for g in sv["gathered"][:3]])))
    sv["wts"] = wts
    ya = rwkv_post_fwd(l, y_scan, fl, k2, v2, g, p)
    qkv = proj[:, C_AQ:C_AQ + 3 * DG]
    outs, lses = [], []
    for bi, (win, dil) in enumerate(DILATED):
        o, lse = attn_branch_fwd(l, bi, qkv, dil)
        outs.append(o)
        lses.append(lse)
    yb, lse_all = attn_merge(l, outs, lses)
    yc, ssd_states = ssd_fwd(l, proj, xc, p)
    yd, hg_states = hgrn_fwd(l, proj, p)
    ycat = jnp.concatenate([ya, yb, yc, yd], axis=1).astype(BF16)
    mix = _mm(f"mm_out{l}", ycat, wts["w_out"], "nn", *MM_TILES["k1024"])
    x1 = ln_fwd(f"ln1_fwd{l}", x, mix, p["ln1_w"], p["ln1_b"])
    hh = _mm(f"mm_up{l}", x1, wts["w_up_t"], "nt", *MM_TILES["k1024"], epilogue="relu2")
    m2 = _mm(f"mm_down{l}", hh, wts["w_down"], "nn", *MM_TILES["k4096"])
    x2 = ln_fwd(f"ln2_fwd{l}", x1, m2, p["ln2_w"], p["ln2_b"])
    sv.update(proj=proj, fl=fl, xc=xc, w=w, k2=k2, v2=v2, c=c, b=b, g=g, y_scan=y_scan, states=states,
              yb=yb, lse_all=lse_all, ssd_states=ssd_states, hg_states=hg_states, ycat=ycat, mix=mix, x1=x1, hh=hh, qkv=qkv,
              m2=m2, vfirst=vfirst)
    return x2, sv


def layer_bwd(l, dx2, dvfirst_next, sv, wts, p, exchange=(), reducer=None):
    gr = {}
    x, x1, proj, fl = sv["x"], sv["x1"], sv["proj"], sv["fl"]
    dres2, gr["ln2_w"], gr["ln2_b"] = ln_bwd(f"ln2_bwd{l}", x1, sv["m2"], p["ln2_w"], p["ln2_b"], dx2)
    du = _mm(f"mm_down_dx{l}", dres2, wts["w_down"], "nt", *MM_TILES["k1024"], add=sv["hh"], epilogue="relu2_bwd",
             out_dtype=BF16)
    gr["w_down"] = _mm(f"mm_down_dw{l}", sv["hh"], dres2, "tn", *MM_TILES["wgrad_tall"])
    dx1 = _mm(f"mm_up_dx{l}", du, wts["w_up_t"], "nn", *MM_TILES["k4096"], add=dres2, add_scale=ALPHA)
    gr["w_up_t"] = _mm(f"mm_up_dw{l}", du, x1, "tn", *MM_TILES["wgrad_tall"])
    dres1, gr["ln1_w"], gr["ln1_b"] = ln_bwd(f"ln1_bwd{l}", x, sv["mix"], p["ln1_w"], p["ln1_b"], dx1)
    dycat = _mm(f"mm_out_dx{l}", dres1, wts["w_out"], "nt", *MM_TILES["k1024"])
    gr["w_out"] = _mm(f"mm_out_dw{l}", sv["ycat"], dres1, "tn", 1024, 1024, 512)
    dyb = dycat[:, DG:2 * DG]
    send = [_owner_blocks(gr[k]) for k in LATE_KEYS] if reducer else []
    (dhg4, gr["lb0"], gr["lb1"], gr["hgrn_norm_w"]), sib = hgrn_bwd(l, proj, sv["hg_states"], dycat, p, send, dy_col=3)
    if reducer:
        gr["early_own"], early_parts = reducer(f"{l}a", send, sib)
        exchange = list(exchange) + list(early_parts)
    dz, dxc, ddt, gr["dt_bias"], gr["a_log"], gr["ssd_d"], gr["ssd_norm_w"] = ssd_bwd(l, proj, sv["xc"], sv["ssd_states"], dycat, p, dy_col=2)
    dxbc, gr["conv_w"], gr["conv_b"] = conv_bwd(l, proj, p["conv_w"], p["conv_b"], dxc)
    delta = attn_delta(l, dyb, sv["yb"], p["seg64"])
    dqs, dks, dvs = [], [], []
    for bi, (win, dil) in enumerate(DILATED):
        dq, dk, dv = attn_branch_bwd(l, bi, sv["qkv"], dil, dyb, sv["lse_all"], delta)
        dqs.append(dq)
        dks.append(dk)
        dvs.append(dv)
    dq_a, dk_a, dv_a = _addn(f"attn_dq{l}", *dqs), _addn(f"attn_dk{l}", *dks), _addn(f"attn_dv{l}", *dvs)
    pg = rwkv_post_bwd(l, sv["y_scan"], fl, sv["k2"], sv["v2"], sv["g"], p, dycat)
    gr["lnx_w"], gr["lnx_b"], gr["r_k"] = pg["lnx_w"], pg["lnx_b"], pg["r_k"]
    (dr, dw, dk, dv, dc, db), gr["exchanged"] = rwkv_scan_bwd(l, fl, sv["w"], sv["k2"], sv["v2"], sv["c"], sv["b"],
                                                              sv["states"], pg["y"], p, exchange)
    v2_cts = [dv, pg["v2"]] + ([dvfirst_next] if dvfirst_next is not None else [])
    qg = rwkv_pre_bwd(l, fl, sv["vfirst"], p, [[dw], [dk, pg["k2"]], v2_cts, [dc], [db], [pg["g"]]])
    for nme in ("w0", "w2p", "a0", "a2p", "g2p", "k_k", "k_a", "v0", "v2p"):
        if nme in qg:
            gr[nme] = qg[nme]
    dfr = _addn(f"rwkv_dr{l}", dr, pg["fr"])
    dvres = qg["fvres"] if l > 0 else jnp.zeros((SEQ, 128), F32)
    dfl_out = jnp.concatenate([dfr, qg["fk"], qg["fv"], qg["flora"], dvres], axis=1)
    dfl_in, gr["mu"] = lerp_bwd(l, proj, p["mu"], dfl_out)
    dproj = jnp.concatenate([dfl_in[:, 0:768], dq_a, dk_a, dv_a, dz, dxbc, dhg4, dfl_in[:, 768:896], ddt,
                             dfl_in[:, 896:1024], jnp.zeros((SEQ, 128), F32)], axis=1).astype(BF16)
    dx = _mm(f"mm_in_dx{l}", dproj, wts["w_in"], "nt", *MM_TILES["k4096"], add=dres1, add_scale=ALPHA)
    gr["w_in"] = _mm(f"mm_in_dw{l}", x, dproj, "tn", *MM_TILES["wgrad_wide"])
    return dx, (qg["vfirst"] if l > 0 else None), gr


def _w_in_pad(w_in_l, w_vres):
    rows = w_in_l.shape[0]
    z = lambda n: jnp.zeros((rows, n), w_in_l.dtype)
    vres = z(128) if w_vres is None else jnp.concatenate([w_vres, z(96)], axis=1)
    return jnp.concatenate([w_in_l[:, 0:768], w_in_l[:, 896:1664], w_in_l[:, 1664:1920], w_in_l[:, 1920:2688],
                            w_in_l[:, 2692:3716], w_in_l[:, 768:896], w_in_l[:, 2688:2692], z(124), vres, z(128)], axis=1)


def _w_in_unpad(g):
    g_in = jnp.concatenate([g[:, 0:768], g[:, C_LORA:C_LORA + 128], g[:, 768:1536], g[:, C_Z:C_Z + 256],
                            g[:, C_XBC:C_XBC + 768], g[:, C_DT:C_DT + 4], g[:, C_HQ:C_HQ + 1024]], axis=1)
    return g_in, g[:, C_VRES:C_VRES + 32]


def _consts():
    pair = jnp.arange(HG_NC * HGRN_CHUNK * HGRN_CHUNK)
    i128 = jnp.arange(128)
    ihg = jnp.arange(HG_ROWS)
    same_chunk = (ihg[:, None] // HGRN_CHUNK) == (ihg[None, :] // HGRN_CHUNK)
    seg64 = _seg_ones(DG, HD)
    tri128 = (i128[:, None] >= i128[None, :]).astype(F32)
    return dict(
        seg64=seg64, seg64x3_bf16=jnp.concatenate([seg64, seg64, seg64], axis=0).astype(BF16),
        dmask=(jnp.arange(HD)[:, None] == (jnp.arange(DG)[None, :] % HD)).astype(F32),
        tri_chunks=(same_chunk & (ihg[:, None] >= ihg[None, :])).astype(F32), ones_chunks=same_chunk.astype(F32),
        first_row=(ihg[None, :] == (jnp.arange(HG_NC) * HGRN_CHUNK)[:, None]).astype(F32),
        causal_blk=jnp.broadcast_to((((pair // HGRN_CHUNK) % HGRN_CHUNK) >= (pair % HGRN_CHUNK)).astype(F32)[:, None],
                                    (HG_NC * HGRN_CHUNK * HGRN_CHUNK, DG)),
        e128=((i128[:, None] == (jnp.arange(DG)[None, :] // HD)) & (i128[:, None] < NH)).astype(F32),
        tri128=tri128, tri128t=tri128.T, seg128=_seg_ones(DG, 128), ones128=jnp.ones((128, 128), F32))


def _pad_lanes(v, n):
    return jnp.concatenate([v, jnp.zeros((n - v.shape[0],), v.dtype)])[None, :]


def _layer_params(l, raw, consts):
    p = dict(consts)
    row = lambda name: raw[name][l][None, :]
    z = lambda r: jnp.zeros((r, DG), F32)
    mu_vres = raw["mu_vres"][l - 1] if l > 0 else jnp.zeros((32,), F32)
    p["mu"] = jnp.concatenate([raw["mu_shift"][l], mu_vres, jnp.zeros((96,), F32)])[None, :]
    p["conv_w"], p["conv_b"] = raw["ssd_conv_w"][l], row("ssd_conv_b")
    p["w0"], p["a0"], p["k_k"], p["k_a"] = row("rwkv_w0"), row("rwkv_a0"), row("rwkv_k_k"), row("rwkv_k_a")
    p["lnx_w"], p["lnx_b"] = row("rwkv_lnx_w"), row("rwkv_lnx_b")
    p["r_k"] = raw["rwkv_r_k"][l].reshape(1, DG)
    p["w2p"] = jnp.concatenate([raw["rwkv_w2"][l], z(96)], axis=0)
    p["a2p"] = jnp.concatenate([z(32), raw["rwkv_a2"][l], z(64)], axis=0)
    p["g2p"] = jnp.concatenate([z(64), raw["rwkv_g2"][l]], axis=0)
    if l > 0:
        p["v0"] = raw["rwkv_v0"][l - 1][None, :]
        p["v2p"] = jnp.concatenate([raw["rwkv_v2"][l - 1], z(96)], axis=0)
    p["lb0"], p["lb1"] = raw["lower_bounds"][0:1], raw["lower_bounds"][1:2]
    p["hgrn_norm_w"], p["ssd_norm_w"] = row("hgrn_norm_w"), row("ssd_norm_w")
    p["dt_bias"], p["a_log"], p["ssd_d"] = (_pad_lanes(raw[n][l], 128) for n in ("ssd_dt_bias", "ssd_A_log", "ssd_D"))
    for n in ("ln1_w", "ln1_b", "ln2_w", "ln2_b"):
        p[n] = row(n)
    return p


def _natural_grads(g0, g1):
    gs = (g0, g1)
    st = lambda key, f=lambda a: a[0]: jnp.stack([f(g[key]) for g in gs])
    out = {}
    out["lower_bounds"] = jnp.concatenate([g0["lb0"] + g1["lb0"], g0["lb1"] + g1["lb1"]], axis=0)
    out["mu_shift"] = st("mu", lambda a: a[0, :896])
    out["mu_vres"] = g1["mu"][:, 896:928]
    out["rwkv_w0"], out["rwkv_a0"], out["rwkv_k_k"], out["rwkv_k_a"] = st("w0"), st("a0"), st("k_k"), st("k_a")
    out["rwkv_w2"] = st("w2p", lambda a: a[0:32])
    out["rwkv_a2"] = st("a2p", lambda a: a[32:64])
    out["rwkv_g2"] = st("g2p", lambda a: a[64:128])
    out["rwkv_r_k"] = st("r_k", lambda a: a.reshape(NH, HD))
    out["rwkv_lnx_w"], out["rwkv_lnx_b"] = st("lnx_w"), st("lnx_b")
    out["rwkv_v0"] = g1["v0"]
    out["rwkv_v2"] = g1["v2p"][None, 0:32]
    out["ssd_conv_w"] = st("conv_w", lambda a: a)
    out["ssd_conv_b"] = st("conv_b")
    out["ssd_dt_bias"], out["ssd_A_log"], out["ssd_D"] = (st(k, lambda a: a[0, :NH]) for k in ("dt_bias", "a_log", "ssd_d"))
    out["ssd_norm_w"], out["hgrn_norm_w"] = st("ssd_norm_w"), st("hgrn_norm_w")
    for n in ("ln1_w", "ln1_b", "ln2_w", "ln2_b"):
        out[n] = st(n)
    return out


MESH_T = pl.DeviceIdType.MESH
ANY = pl.BlockSpec(memory_space=pl.ANY)


def _dev_index(px, py, pc):
    return 4 * px + 2 * py + pc


def all_gather(arrs):
    n = len(arrs)

    def body(*refs):
        begin, middle, end = _gather_steps(refs[:n], refs[n:2 * n], *refs[2 * n:])
        begin()
        middle()
        end()

    return pl.pallas_call(body, in_specs=[ANY] * n, out_specs=[ANY] * n, out_shape=_gather_shapes(arrs),
                          scratch_shapes=_gather_sems(n), name="all_gather")(*arrs)


def _gather_shapes(arrs):
    return [_sds((N_DEV,) + a.shape, a.dtype) for a in arrs]


def _gather_sems(n):
    return [pltpu.SemaphoreType.DMA((n, 7)), pltpu.SemaphoreType.DMA((n, 7)), pltpu.SemaphoreType.DMA((n,))]


def _gather_steps(ins, outs, send_sems, recv_sems, local_sems):
    n = len(ins)
    x, y, c = lax.axis_index("x"), lax.axis_index("y"), lax.axis_index("c")
    me, sibling = (x, y, c), (x, y, 1 - c)
    chips = [(1 - x, y), (x, 1 - y), (1 - x, 1 - y)]

    def copy(a, k, block, to, src=None):
        slot = outs[a].at[_dev_index(*block)]
        return pltpu.make_async_remote_copy(src_ref=slot if src is None else src, dst_ref=slot,
                                            send_sem=send_sems.at[a, k], recv_sem=recv_sems.at[a, k],
                                            device_id=to, device_id_type=MESH_T)

    def own_copies():
        mine = [pltpu.make_async_copy(ins[a], outs[a].at[_dev_index(*me)], local_sems.at[a]) for a in range(n)]
        first = []
        for a in range(n):
            first.append(copy(a, 0, me, sibling, src=ins[a]))
            first += [copy(a, 1 + j, me, (*chip, c), src=ins[a]) for j, chip in enumerate(chips)]
        return mine, first

    def begin():
        mine, first = own_copies()
        for cp in mine + first:
            cp.start()

    def passed_on():
        return [copy(a, 4 + j, (*chip, c), sibling) for j, chip in enumerate(chips) for a in range(n)]

    def middle():
        for j, chip in enumerate(chips):
            for a in range(n):
                copy(a, 1 + j, (*chip, c), me).wait_recv()
        for cp in passed_on():
            cp.start()

    def end():
        mine, first = own_copies()
        for a in range(n):
            copy(a, 0, sibling, me).wait_recv()
            for j, chip in enumerate(chips):
                copy(a, 4 + j, (*chip, 1 - c), me).wait_recv()
        for cp in first + passed_on():
            cp.wait_send()
        for cp in mine:
            cp.wait()

    return begin, middle, end


def _chips(x, y):
    return [(x, y), (1 - x, y), (x, 1 - y), (1 - x, 1 - y)]


def _sibling_steps(ins, sib, send_sems, recv_sems):
    x, y, c = lax.axis_index("x"), lax.axis_index("y"), lax.axis_index("c")

    def copies():
        return [pltpu.make_async_remote_copy(src_ref=ins[a].at[_dev_index(cx, cy, 1 - c)], dst_ref=sib[a].at[k],
                                             send_sem=send_sems.at[a, k], recv_sem=recv_sems.at[a, k],
                                             device_id=(x, y, 1 - c), device_id_type=MESH_T)
                for a in range(len(ins)) for k, (cx, cy) in enumerate(_chips(x, y))]

    def begin():
        for cp in copies():
            cp.start()

    def end():
        cps = copies()
        for cp in cps:
            cp.wait_recv()
        for cp in cps:
            cp.wait_send()

    return begin, end


def exchange_siblings(arrs, name):
    n = len(arrs)

    def body(*refs):
        begin, end = _sibling_steps(refs[:n], refs[n:2 * n], *refs[2 * n:])
        begin()
        end()

    sem = pltpu.SemaphoreType.DMA((n, 4))
    return pl.pallas_call(body, in_specs=[ANY] * n, out_specs=[ANY] * n,
                          out_shape=[_sds((4,) + a.shape[1:], a.dtype) for a in arrs],
                          scratch_shapes=[sem, sem], name=name)(*arrs)


def reduce_pair(name, send, slots, sib, wire_dtype):
    _, r, c = send.shape
    rb = min(r, 262144 // c)

    def body(slots_ref, m0, m1, m2, m3, s_ref, own_ref, part_ref):
        own_ref[...] = m0[...] + s_ref[0]
        for k, m_ref in enumerate((m1, m2, m3)):
            part_ref[k] = (m_ref[...] + s_ref[k + 1]).astype(wire_dtype)

    mine = [pl.BlockSpec((None, rb, c), lambda i, s, k=k: (s[k], i, 0)) for k in range(4)]
    grid_spec = pltpu.PrefetchScalarGridSpec(
        num_scalar_prefetch=1, grid=(r // rb,),
        in_specs=mine + [pl.BlockSpec((4, rb, c), lambda i, s: (0, i, 0))],
        out_specs=[pl.BlockSpec((rb, c), lambda i, s: (i, 0)), pl.BlockSpec((3, rb, c), lambda i, s: (0, i, 0))])
    return pl.pallas_call(body, grid_spec=grid_spec, out_shape=[_sds((r, c)), _sds((3, r, c), wire_dtype)], name=name,
                          compiler_params=_cp(("parallel",)))(slots, send, send, send, send, sib)


def _chip_exchange_steps(ins, recv, send_sems, recv_sems):
    x, y, c = lax.axis_index("x"), lax.axis_index("y"), lax.axis_index("c")

    def copies():
        return [pltpu.make_async_remote_copy(src_ref=ins[a].at[k], dst_ref=recv[a].at[k], send_sem=send_sems.at[a, k],
                                             recv_sem=recv_sems.at[a, k], device_id=(cx, cy, c), device_id_type=MESH_T)
                for a in range(len(ins)) for k, (cx, cy) in enumerate(_chips(x, y)[1:])]

    def begin():
        for cp in copies():
            cp.start()

    def end():
        cps = copies()
        for cp in cps:
            cp.wait_recv()
        for cp in cps:
            cp.wait_send()

    return begin, end


def exchange_chips(parts, rep):
    n = len(parts)

    def body(*refs):
        ins, rep_ref = refs[:n], refs[n]
        recv, rep_all = refs[n + 1:2 * n + 1], refs[2 * n + 1]
        send_sems, recv_sems, rsend_sems, rrecv_sems, local_sem = refs[2 * n + 2:]
        x, y, c = lax.axis_index("x"), lax.axis_index("y"), lax.axis_index("c")
        me = _dev_index(x, y, c)
        mine = pltpu.make_async_copy(rep_ref, rep_all.at[me], local_sem)
        mine.start()
        begin, end = _chip_exchange_steps(ins, recv, send_sems, recv_sems)
        begin()
        rels = [(rx, ry, rc) for rx in (0, 1) for ry in (0, 1) for rc in (0, 1)][1:]
        peers = [(jnp.where(rx, 1 - x, x), jnp.where(ry, 1 - y, y), jnp.where(rc, 1 - c, c)) for rx, ry, rc in rels]
        rcps = []
        for k, peer in enumerate(peers):
            cp = pltpu.make_async_remote_copy(src_ref=rep_ref, dst_ref=rep_all.at[me], send_sem=rsend_sems.at[k],
                                              recv_sem=rrecv_sems.at[k], device_id=peer, device_id_type=MESH_T)
            cp.start()
            rcps.append(cp)
        for k, peer in enumerate(peers):
            pltpu.make_async_remote_copy(src_ref=rep_ref, dst_ref=rep_all.at[_dev_index(*peer)], send_sem=rsend_sems.at[k],
                                         recv_sem=rrecv_sems.at[k], device_id=peer, device_id_type=MESH_T).wait_recv()
        end()
        for cp in rcps:
            cp.wait_send()
        mine.wait()

    outs = pl.pallas_call(
        body, in_specs=[ANY] * (n + 1), out_specs=[ANY] * (n + 1),
        out_shape=[_sds(a.shape, a.dtype) for a in parts] + [_sds((N_DEV,) + rep.shape, rep.dtype)],
        scratch_shapes=[pltpu.SemaphoreType.DMA((n, 3)), pltpu.SemaphoreType.DMA((n, 3)), pltpu.SemaphoreType.DMA((7,)),
                        pltpu.SemaphoreType.DMA((7,)), pltpu.SemaphoreType.DMA],
        name="exchange_chips")(*parts, rep)
    return outs[:n], outs[n]


def adamw(name, terms, w, m, v, transposed=False):
    r, c = w.shape[::-1] if transposed else w.shape
    rb = r if transposed else min(r, 262144 // c)
    c1 = 1.0 - ADAM_B1 ** ADAM_STEP
    c2 = 1.0 - ADAM_B2 ** ADAM_STEP
    nt = len(terms)

    def body(*refs):
        w_ref, m_ref, v_ref = refs[nt:nt + 3]
        g_ref, d_ref, nm_ref, nv_ref = refs[nt + 3:]
        g = refs[0][...].astype(F32)
        for t_ref in refs[1:nt]:
            g = g + t_ref[...].astype(F32)
        if transposed:
            g = g.T
        nm = ADAM_B1 * m_ref[...] + (1.0 - ADAM_B1) * g
        nv = ADAM_B2 * v_ref[...] + (1.0 - ADAM_B2) * (g * g)
        g_ref[...] = g
        nm_ref[...] = nm
        nv_ref[...] = nv
        d_ref[...] = -ADAM_LR * ((nm / c1) / (jnp.sqrt(nv / c2) + ADAM_EPS) + ADAM_WD * w_ref[...])

    blk = pl.BlockSpec((rb, c), lambda i: (i, 0))
    wblk = pl.BlockSpec((c, r), lambda i: (0, 0)) if transposed else blk
    tspecs = [blk if k is None else pl.BlockSpec((None, rb, c), lambda i, k=k: (k, i, 0)) for _, k in terms]
    return pl.pallas_call(body, grid=(r // rb,), in_specs=tspecs + [wblk] * 3, out_specs=[wblk] * 4,
                          out_shape=[_sds(w.shape)] * 4, name=name,
                          compiler_params=_cp(("parallel",)))(*[t for t, _ in terms], w, m, v)


W_IN_PIECES = ((0, 768, 0), (768, 896, C_LORA), (896, 1664, 768), (1664, 1920, C_Z), (1920, 2688, C_XBC),
               (2688, 2692, C_DT), (2692, 3716, C_HQ))
VRES_W = 32


def adamw_w_in(name, terms, w, m, v, vres=None):
    nt, nv = len(terms), 3 if vres else 0
    c1 = 1.0 - ADAM_B1 ** ADAM_STEP
    c2 = 1.0 - ADAM_B2 ** ADAM_STEP

    def body(*refs):
        w_ref, m_ref, v_ref = refs[nt:nt + 3]
        vres_refs = refs[nt + 3:nt + 3 + nv]
        outs = refs[nt + 3 + nv:nt + 7 + nv]
        vres_outs = refs[nt + 7 + nv:]
        g_all = refs[0][...].astype(F32)
        for t_ref in refs[1:nt]:
            g_all = g_all + t_ref[...].astype(F32)

        def update(g, wmv, out_refs, cols):
            nm = ADAM_B1 * wmv[1][:, cols] + (1.0 - ADAM_B1) * g
            nv_ = ADAM_B2 * wmv[2][:, cols] + (1.0 - ADAM_B2) * (g * g)
            out_refs[0][:, cols] = g
            out_refs[1][:, cols] = -ADAM_LR * ((nm / c1) / (jnp.sqrt(nv_ / c2) + ADAM_EPS) + ADAM_WD * wmv[0][:, cols])
            out_refs[2][:, cols] = nm
            out_refs[3][:, cols] = nv_

        for lo, hi, src in W_IN_PIECES:
            update(g_all[:, src:src + hi - lo], (w_ref, m_ref, v_ref), outs, slice(lo, hi))
        if vres:
            update(g_all[:, C_VRES:C_VRES + VRES_W], vres_refs, vres_outs, slice(0, VRES_W))

    r, c = terms[0][0].shape[-2:]
    tspecs = [_full((r, c)) if k is None else pl.BlockSpec((None, r, c), lambda i, k=k: (k, 0, 0)) for _, k in terms]
    wspec, vspec = _full(w.shape), _full((w.shape[0], VRES_W))
    outs = pl.pallas_call(body, grid=(1,), in_specs=tspecs + [wspec] * 3 + [vspec] * nv,
                          out_specs=[wspec] * 4 + [vspec] * (4 if vres else 0),
                          out_shape=[_sds(w.shape)] * 4 + [_sds((w.shape[0], VRES_W))] * (4 if vres else 0), name=name,
                          compiler_params=_cp(("arbitrary",)))(*[t for t, _ in terms], w, m, v, *(vres or ()))
    return list(outs[:4]), list(outs[4:])


SMS_ROWS = 16
N_BIG = 8
SMALL_SHARDED = (("rwkv_w2", (2, 32, 32)), ("rwkv_a2", (2, 32, 32)), ("rwkv_g2", (2, 64, 32)), ("rwkv_v2", (1, 32, 32)),
                 ("ssd_conv_w", (2, 4, 96)))
REPLICATED = (("lower_bounds", (2, 256)), ("mu_shift", (2, 896)), ("mu_vres", (1, 32)), ("rwkv_w0", (2, 256)),
              ("rwkv_a0", (2, 256)), ("rwkv_k_k", (2, 256)), ("rwkv_k_a", (2, 256)), ("rwkv_r_k", (2, 4, 64)),
              ("rwkv_lnx_w", (2, 256)), ("rwkv_lnx_b", (2, 256)), ("rwkv_v0", (1, 256)), ("ssd_conv_b", (2, 768)),
              ("ssd_dt_bias", (2, 4)), ("ssd_A_log", (2, 4)), ("ssd_D", (2, 4)), ("ssd_norm_w", (2, 256)),
              ("hgrn_norm_w", (2, 256)), ("ln1_w", (2, 1024)), ("ln1_b", (2, 1024)), ("ln2_w", (2, 1024)),
              ("ln2_b", (2, 1024)))


def _flat_rows(parts, rows):
    flat = jnp.concatenate([a.reshape(-1) for a in parts])
    return jnp.concatenate([flat, jnp.zeros((rows * PACK_W - flat.shape[0],), flat.dtype)]).reshape(rows, PACK_W)


def _local_arrays(d):
    return [_w_in_pad(d["w_in"][0], None), _w_in_pad(d["w_in"][1], d["w_in_vres"][0]), d["w_out"][0], d["w_out"][1],
            d["w_up"][0], d["w_up"][1], d["w_down"][0], d["w_down"][1],
            _flat_rows([d[n] for n, _ in SMALL_SHARDED], SMS_ROWS)]


def _unflat(rows2d, table):
    flat, out, o = rows2d.reshape(-1), {}, 0
    for name, shape in table:
        n = 1
        for s in shape:
            n *= s
        out[name] = flat[o:o + n].reshape(shape)
        o += n
    return out


def _from_local_arrays(arrs, rep, w_in_vres):
    d = dict(rep)
    d["w_in"], d["w_in_vres"] = jnp.stack([arrs[0], arrs[1]]), w_in_vres[None]
    d["w_out"] = jnp.stack([arrs[2], arrs[3]])
    d["w_up"] = jnp.stack([arrs[4], arrs[5]])
    d["w_down"] = jnp.stack([arrs[6], arrs[7]])
    d.update(_unflat(arrs[8], SMALL_SHARDED))
    return d


def _small_sharded_full(gs):
    small, flat, o = {}, gs.reshape(N_DEV, -1), 0
    for name, shape in SMALL_SHARDED:
        n = shape[0] * shape[1] * shape[2]
        blk = flat[:, o:o + n].reshape((N_DEV,) + shape)
        small[name] = blk.transpose(1, 2, 0, 3).reshape(shape[0], shape[1], N_DEV * shape[2])
        o += n
    return small


def _owner_blocks(g):
    return g.reshape(N_DEV, g.shape[0] // N_DEV, g.shape[1])


def _as_rows(shape):
    width = 1
    for s in shape[1:]:
        width *= s
    return shape[0], width


REP_2D = tuple((name, _as_rows(shape)) for name, shape in REPLICATED)
REP_ROW0 = tuple(sum(a for _, (a, _) in REP_2D[:i]) for i in range(len(REP_2D)))
REP_ROWS = sum(a for _, (a, _) in REP_2D)


def _rep_rows(d):
    rows = []
    for name, (a, b) in REP_2D:
        v = d[name].reshape(a, b)
        rows.append(v if b == PACK_W else jnp.concatenate([v, jnp.zeros((a, PACK_W - b), F32)], axis=1))
    return jnp.concatenate(rows, axis=0)


def adamw_replicated(rep_all, w, m, v):
    names = [name for name, _ in REP_2D]
    n = len(names)
    c1 = 1.0 - ADAM_B1 ** ADAM_STEP
    c2 = 1.0 - ADAM_B2 ** ADAM_STEP

    def body(*refs):
        rep_ref, w_refs, m_refs, v_refs, outs = refs[0], refs[1:1 + n], refs[1 + n:1 + 2 * n], refs[1 + 2 * n:1 + 3 * n], refs[1 + 3 * n:]
        for i, (_, (a, b)) in enumerate(REP_2D):
            r0 = REP_ROW0[i]
            g = rep_ref[0, r0:r0 + a, 0:b]
            for q in range(1, N_DEV):
                g = g + rep_ref[q, r0:r0 + a, 0:b]
            nm = ADAM_B1 * m_refs[i][...] + (1.0 - ADAM_B1) * g
            nv = ADAM_B2 * v_refs[i][...] + (1.0 - ADAM_B2) * (g * g)
            outs[i][...] = g
            outs[n + i][...] = -ADAM_LR * ((nm / c1) / (jnp.sqrt(nv / c2) + ADAM_EPS) + ADAM_WD * w_refs[i][...])
            outs[2 * n + i][...] = nm
            outs[3 * n + i][...] = nv

    flat = lambda d: [d[name].reshape(ab) for name, ab in REP_2D]
    pspecs = [_full(ab) for _, ab in REP_2D]
    res = pl.pallas_call(body, grid=(1,), in_specs=[_full(rep_all.shape)] + pspecs * 3, out_specs=pspecs * 4,
                         out_shape=[_sds(ab) for _, ab in REP_2D] * 4, name="adamw_replicated",
                         compiler_params=_cp(("arbitrary",)))(rep_all, *flat(w), *flat(m), *flat(v))
    shapes = dict(REPLICATED)
    return [{name: res[k * n + i].reshape(shapes[name]) for i, name in enumerate(names)} for k in range(4)]


def _small_send_arrays(small_grads):
    sms = []
    for name, shape in SMALL_SHARDED:
        g = small_grads[name].reshape(shape[0], shape[1], N_DEV, shape[2]).transpose(2, 0, 1, 3)
        sms.append(g.reshape(N_DEV, -1))
    sms = jnp.concatenate(sms, axis=1)
    sms = jnp.concatenate([sms, jnp.zeros((N_DEV, SMS_ROWS * PACK_W - sms.shape[1]), F32)], axis=1)
    return sms.reshape(N_DEV, SMS_ROWS, PACK_W), _rep_rows(small_grads)


BIG_KEYS = ("w_in", "w_out", "w_up_t", "w_down")


def _local_step(x, tgt, wts, raw, gather=(), pair_sums=None, reducer=None):
    consts = _consts()
    ps = [_layer_params(l, raw, consts) for l in range(DEPTH)]
    x1, sv0 = layer_fwd(0, x, None, wts[0], ps[0], gather[:4], late=bool(gather))
    wts1 = {"w_in": _full_rows(sv0["gathered"][3])} if gather else wts[1]
    x2, sv1 = layer_fwd(1, x1, sv0["fl"], wts1, ps[1], gather[4:], late=bool(gather))
    dy, lparts = loss_call(x2, tgt)
    loss = jnp.sum(lparts[::8, 0])
    dx1, dvfirst, g1 = layer_bwd(1, dy, None, sv1, sv1["wts"], ps[1], (), reducer)
    big1 = {k: g1[k] for k in BIG_KEYS}
    if reducer is None:
        dx0, _, g0 = layer_bwd(0, dx1, dvfirst, sv0, sv0["wts"], ps[0])
        early = None
    else:
        own_in1, parts_in1 = pair_sums("1b", {"w_in": g1["w_in"]})
        dx0, _, g0 = layer_bwd(0, dx1, dvfirst, sv0, sv0["wts"], ps[0], parts_in1, reducer)
        own, recv = {(1, "w_in"): own_in1[0]}, {(1, "w_in"): g0["exchanged"][0]}
        for l, g, first in ((1, g1, 0), (0, g0, 1)):
            for i, k in enumerate(LATE_KEYS):
                own[(l, k)], recv[(l, k)] = g["early_own"][i], g["exchanged"][first + i]
        early = (own, recv)
    big = [{k: g0[k] for k in BIG_KEYS}, big1]
    return loss, dx0, big, _natural_grads(g0, g1), early


WEIGHT_NAMES = ("lower_bounds", "w_in", "w_in_vres", "mu_shift", "mu_vres", "rwkv_w0", "rwkv_w2", "rwkv_a0", "rwkv_a2",
                "rwkv_g2", "rwkv_k_k", "rwkv_k_a", "rwkv_r_k", "rwkv_lnx_w", "rwkv_lnx_b", "rwkv_v0", "rwkv_v2",
                "ssd_conv_w", "ssd_conv_b", "ssd_dt_bias", "ssd_A_log", "ssd_D", "ssd_norm_w", "hgrn_norm_w", "w_out",
                "ln1_w", "ln1_b", "w_up", "w_down", "ln2_w", "ln2_b")


def kernel(x, lower_bounds, w_in, w_in_vres, mu_shift, mu_vres, rwkv_w0, rwkv_w2, rwkv_a0, rwkv_a2, rwkv_g2, rwkv_k_k, rwkv_k_a, rwkv_r_k, rwkv_lnx_w, rwkv_lnx_b, rwkv_v0, rwkv_v2, ssd_conv_w, ssd_conv_b, ssd_dt_bias, ssd_A_log, ssd_D, ssd_norm_w, hgrn_norm_w, w_out, ln1_w, ln1_b, w_up, w_down, ln2_w, ln2_b, loss_target, m_lower_bounds, m_w_in, m_w_in_vres, m_mu_shift, m_mu_vres, m_rwkv_w0, m_rwkv_w2, m_rwkv_a0, m_rwkv_a2, m_rwkv_g2, m_rwkv_k_k, m_rwkv_k_a, m_rwkv_r_k, m_rwkv_lnx_w, m_rwkv_lnx_b, m_rwkv_v0, m_rwkv_v2, m_ssd_conv_w, m_ssd_conv_b, m_ssd_dt_bias, m_ssd_A_log, m_ssd_D, m_ssd_norm_w, m_hgrn_norm_w, m_w_out, m_ln1_w, m_ln1_b, m_w_up, m_w_down, m_ln2_w, m_ln2_b, v_lower_bounds, v_w_in, v_w_in_vres, v_mu_shift, v_mu_vres, v_rwkv_w0, v_rwkv_w2, v_rwkv_a0, v_rwkv_a2, v_rwkv_g2, v_rwkv_k_k, v_rwkv_k_a, v_rwkv_r_k, v_rwkv_lnx_w, v_rwkv_lnx_b, v_rwkv_v0, v_rwkv_v2, v_ssd_conv_w, v_ssd_conv_b, v_ssd_dt_bias, v_ssd_A_log, v_ssd_D, v_ssd_norm_w, v_hgrn_norm_w, v_w_out, v_ln1_w, v_ln1_b, v_w_up, v_w_down, v_ln2_w, v_ln2_b):
    given = dict(locals())
    w = {n: given[n] for n in WEIGHT_NAMES}
    m_all, v_all = ({n: given[pre + n] for n in WEIGHT_NAMES} for pre in ("m_", "v_"))
    w_arrs, m_arrs, v_arrs = _local_arrays(w), _local_arrays(m_all), _local_arrays(v_all)
    wire = lambda a: (w_arrs[a].T if a in (4, 5) else w_arrs[a]).astype(BF16)
    gathered0 = all_gather([wire(0), w_arrs[N_BIG]])
    raw = {n: w[n] for n, _ in REPLICATED}
    raw.update(_small_sharded_full(gathered0[1]))
    mx, my, mc = lax.axis_index("x"), lax.axis_index("y"), lax.axis_index("c")
    slots = jnp.stack([_dev_index(cx, cy, mc) for cx, cy in _chips(mx, my)]).astype(jnp.int32)

    def reducer(tag, send, sib, n_f32=0):
        wire_dt = [BF16] * (len(send) - n_f32) + [F32] * n_f32
        res = [reduce_pair(f"reduce_pair{tag}_{i}", s, slots, sb, dt) for i, (s, sb, dt) in enumerate(zip(send, sib, wire_dt))]
        return [o for o, _ in res], [pt for _, pt in res]

    def pair_sums(tag, grads, extra=()):
        send = [_owner_blocks(g) for g in grads.values()] + list(extra)
        return reducer(tag, send, exchange_siblings(send, f"exchange_siblings{tag}"), len(extra))

    behind_scan = [wire(a) for a in (2, 4, 6, 1, 3, 5, 7)]
    loss, dx, big, small_grads, (own_by, recv_by) = _local_step(
        x[0], loss_target[0], [{"w_in": _full_rows(gathered0[0])}, None], raw, behind_scan, pair_sums, reducer)
    sms_send, rep = _small_send_arrays(small_grads)
    own0b, parts0b = pair_sums("0b", {"w_in": big[0]["w_in"]}, [sms_send])
    recv0b, rep_all = exchange_chips(parts0b, rep)
    own, recv = [None] * (N_BIG + 1), [None] * (N_BIG + 1)
    for (l, k), o in own_by.items():
        a = 2 * BIG_KEYS.index(k) + l
        own[a], recv[a] = o, recv_by[(l, k)]
    for a, o, r in zip((0, N_BIG), own0b, recv0b):
        own[a], recv[a] = o, r
    terms = lambda a: [(own[a], None), (recv[a], 0), (recv[a], 1), (recv[a], 2)]
    moments = [{n: given[pre + n] for n in ("w_in", "w_in_vres")} for pre in ("", "m_", "v_")]
    in0, _ = adamw_w_in("adamw0", terms(0), *[d["w_in"][0] for d in moments])
    in1, vres = adamw_w_in("adamw1", terms(1), *[d["w_in"][1] for d in moments], vres=[d["w_in_vres"][0] for d in moments])
    results = [in0, in1] + [adamw(f"adamw{a}", terms(a), w_arrs[a], m_arrs[a], v_arrs[a], transposed=a in (4, 5))
                            for a in range(2, N_BIG + 1)]
    rep_res = adamw_replicated(rep_all, w, m_all, v_all)
    loss = lax.psum(loss, ("x", "y", "c"))
    outs = [loss, dx[None]]
    for q in range(4):
        d = _from_local_arrays([res[q] for res in results], rep_res[q], vres[q])
        outs += [d[n] for n in WEIGHT_NAMES]
    return tuple(outs)
```

```python
import functools

import jax
import jax.numpy as jnp
from jax import lax
from jax.experimental import pallas as pl
from jax.experimental.pallas import tpu as pltpu

F32 = jnp.float32
BF16 = jnp.bfloat16
HI = lax.Precision.HIGHEST

N_DEV = 8
SEQ = 2048
D_MODEL = 1024
D_FF = 4096
DG = 256
NH = 4
HD = 64
DEPTH = 2
ALPHA = (2.0 * DEPTH) ** 0.25
LN_EPS = 1e-5
RMS_EPS = 1e-5
GN_EPS = HD * 1e-5
SSD_N = 128
SSD_CHUNK = 128
HGRN_CHUNK = 16
DILATED = ((128, 1), (512, 4), (2048, 16))

ADAM_LR, ADAM_B1, ADAM_B2, ADAM_EPS, ADAM_WD, ADAM_STEP = 0.001, 0.9, 0.999, 1e-08, 0.01, 10

PW = 4096
C_R, C_K, C_V = 0, 256, 512
C_AQ, C_AK, C_AV = 768, 1024, 1280
C_Z, C_XBC = 1536, 1792
C_HQ, C_HF, C_HI, C_HG = 2560, 2816, 3072, 3328
C_LORA, C_DT, C_VRES = 3584, 3712, 3840

RB = 256
VMEM_LIMIT = 56 * 1024 * 1024
PACK_W = 1024


def _cp(sem=None):
    return pltpu.CompilerParams(dimension_semantics=sem, vmem_limit_bytes=VMEM_LIMIT)


def _sds(shape, dt=F32):
    return jax.ShapeDtypeStruct(tuple(shape), dt)


def _rows(w, cb=0, rb=RB):
    return pl.BlockSpec((rb, w), lambda i: (i, cb))


def _full(shape):
    n = len(shape)
    return pl.BlockSpec(tuple(shape), lambda *_: (0,) * n)


def _sigmoid(x):
    return 1.0 / (1.0 + jnp.exp(-x))


def _silu(x):
    return x * _sigmoid(x)


def _softplus(x):
    return jnp.maximum(x, 0.0) + jnp.log(1.0 + jnp.exp(jnp.where(x > 0, -x, x)))


MID = lax.Precision.HIGH
NN, TN, NT = (((1,), (0,)), ((), ())), (((0,), (0,)), ((), ())), (((1,), (1,)), ((), ()))


def _dot(a, b):
    return lax.dot_general(a, b, NN, precision=MID, preferred_element_type=F32)


def _dot_tn(a, b):
    return lax.dot_general(a, b, TN, precision=MID, preferred_element_type=F32)


def _dot_nt(a, b):
    return lax.dot_general(a, b, NT, precision=MID, preferred_element_type=F32)


def _dotx(a, b):
    return lax.dot_general(a, b, NN, precision=HI, preferred_element_type=F32)


def _dotx_tn(a, b):
    return lax.dot_general(a, b, TN, precision=HI, preferred_element_type=F32)


def _seg_ones(n, seg):
    i = jnp.arange(n)
    return (i[:, None] // seg == i[None, :] // seg).astype(F32)


def _shift_down(x, s):
    row = lax.broadcasted_iota(jnp.int32, x.shape, 0)
    return jnp.where(row < s, 0.0, pltpu.roll(x, s, 0))


def _shift_up(x, s):
    n = x.shape[0]
    row = lax.broadcasted_iota(jnp.int32, x.shape, 0)
    return jnp.where(row >= n - s, 0.0, pltpu.roll(x, n - s, 0))


@functools.partial(jax.custom_vjp, nondiff_argnums=(1,))
def _tshift(x, s):
    return _shift_down(x, s)


def _tshift_fwd(x, s):
    return _shift_down(x, s), None


def _tshift_bwd(s, _, g):
    return (_shift_up(g, s),)


_tshift.defvjp(_tshift_fwd, _tshift_bwd)


def _map_fwd(name, fn, grid, ins, in_specs, out_shapes, out_specs):
    n_in = len(ins)

    def body(*refs):
        ys = fn(*[r[...] for r in refs[:n_in]])
        for r, y in zip(refs[n_in:], ys):
            r[...] = y

    return pl.pallas_call(body, grid=grid, in_specs=in_specs, out_specs=out_specs, out_shape=out_shapes,
                          name=name, compiler_params=_cp(("parallel",)))(*ins)


def _map_bwd(name, fn, grid, ins, in_specs, cts, ct_specs, want, acc=(), gout=None):
    n_in = len(ins)
    flat_cts = [c for group in cts for c in group]
    flat_specs = [s for group in ct_specs for s in group]
    n_ct = len(flat_cts)
    gout = gout or {}
    out_shapes = [gout[i][0] if i in gout else _sds(ins[i].shape) for i in want]
    out_specs = [gout[i][1] if i in gout else in_specs[i] for i in want]

    def body(*refs):
        xs = [r[...] for r in refs[:n_in]]
        cvals = [r[...] for r in refs[n_in:n_in + n_ct]]
        gouts = refs[n_in + n_ct:]
        cs, p = [], 0
        for group in cts:
            v = cvals[p]
            for q in range(1, len(group)):
                v = v + cvals[p + q]
            cs.append(v)
            p += len(group)

        def f(*wanted):
            full = list(xs)
            for i, w in zip(want, wanted):
                full[i] = w
            return tuple(fn(*full))

        _, vjp = jax.vjp(f, *[xs[i] for i in want])
        gs = vjp(tuple(cs))
        for o, i, g in zip(gouts, want, gs):
            if i in acc:
                @pl.when(pl.program_id(0) == 0)
                def _():
                    o[...] = jnp.zeros_like(o)

                o[...] += g
            else:
                o[...] = g

    sem = ("arbitrary",) if acc else ("parallel",)
    return pl.pallas_call(body, grid=grid, in_specs=list(in_specs) + flat_specs, out_specs=out_specs,
                          out_shape=out_shapes, name=name, compiler_params=_cp(sem))(*ins, *flat_cts)


def _addn(name, *arrs):
    n, c = arrs[0].shape

    def fn(*xs):
        r = xs[0]
        for x in xs[1:]:
            r = r + x
        return (r,)

    return _map_fwd(name, fn, (n // RB,), list(arrs), [_rows(c)] * len(arrs), [_sds((n, c))], [_rows(c)])[0]


MM_TILES = {"k1024": (2048, 512, 1024), "k4096": (1024, 1024, 1024), "wgrad_tall": (2048, 1024, 512),
            "wgrad_wide": (1024, 2048, 512)}


def _mm(name, a, b, mode, tm, tn, tk, add=None, add_scale=1.0, epilogue=None, out_dtype=F32):
    if mode == "nn":
        (m, k), n = a.shape, b.shape[1]
    elif mode == "nt":
        (m, k), n = a.shape, b.shape[0]
    else:
        (k, m), n = a.shape, b.shape[1]
    nk = k // tk
    dn = {"nn": (((1,), (0,)), ((), ())), "nt": (((1,), (1,)), ((), ())), "tn": (((0,), (0,)), ((), ()))}[mode]

    def body(*refs):
        a_ref, b_ref = refs[:2]
        add_ref = refs[2] if add is not None else None
        o_ref = refs[3] if add is not None else refs[2]
        prod = lax.dot_general(a_ref[...].astype(BF16), b_ref[...].astype(BF16), dn, preferred_element_type=F32)

        def finish(r):
            if epilogue == "relu2":
                r = jnp.maximum(r, 0.0)
                r = r * r
            elif epilogue == "relu2_bwd":
                r = r * (2.0 * jnp.sqrt(add_ref[...]))
            elif add is not None:
                r = r + add_scale * add_ref[...]
            o_ref[...] = r.astype(out_dtype)

        if nk == 1:
            finish(prod)
        else:
            acc = refs[-1]
            kk = pl.program_id(2)

            @pl.when(kk == 0)
            def _():
                acc[...] = prod

            @pl.when(kk > 0)
            def _():
                acc[...] += prod

            @pl.when(kk == nk - 1)
            def _():
                finish(acc[...])

    a_spec = pl.BlockSpec((tk, tm), lambda i, j, q: (q, i)) if mode == "tn" else pl.BlockSpec((tm, tk), lambda i, j, q: (i, q))
    b_spec = pl.BlockSpec((tn, tk), lambda i, j, q: (j, q)) if mode == "nt" else pl.BlockSpec((tk, tn), lambda i, j, q: (q, j))
    o_spec = pl.BlockSpec((tm, tn), lambda i, j, q: (i, j))
    ins, specs = [a, b], [a_spec, b_spec]
    if add is not None:
        ins.append(add)
        specs.append(o_spec)
    return pl.pallas_call(body, grid=(m // tm, n // tn, nk), in_specs=specs, out_specs=o_spec,
                          out_shape=_sds((m, n), out_dtype),
                          scratch_shapes=[pltpu.VMEM((tm, tn), F32)] if nk > 1 else [], name=name,
                          compiler_params=_cp(("parallel", "parallel", "arbitrary")))(*ins)


def _lerp_colmap(j):
    r = jnp.where(j < 6, j, jnp.where(j == 6, C_LORA // 128, C_VRES // 128))
    return (0, r)


def _lerp_fn(f, mu):
    return (f + (_tshift(f, 1) - f) * mu,)


def _lerp_specs():
    return [pl.BlockSpec((SEQ, 128), _lerp_colmap), pl.BlockSpec((1, 128), lambda j: (0, j))]


def lerp_fwd(l, proj, mu):
    return _map_fwd(f"lerp_fwd{l}", _lerp_fn, (8,), [proj, mu], _lerp_specs(), [_sds((SEQ, 1024))],
                    [pl.BlockSpec((SEQ, 128), lambda j: (0, j))])[0]


def lerp_bwd(l, proj, mu, dfl):
    n_in = 2

    def body(f_ref, mu_ref, g_ref, df_ref, dmu_ref):
        _, vjp = jax.vjp(_lerp_fn, f_ref[...], mu_ref[...])
        df, dmu = vjp((g_ref[...],))
        df_ref[...] = df
        dmu_ref[...] = dmu

    cspec = pl.BlockSpec((SEQ, 128), lambda j: (0, j))
    return pl.pallas_call(body, grid=(8,), in_specs=_lerp_specs() + [cspec],
                          out_specs=[cspec, pl.BlockSpec((1, 128), lambda j: (0, j))],
                          out_shape=[_sds((SEQ, 1024)), _sds((1, 1024))], name=f"lerp_bwd{l}",
                          compiler_params=_cp(("parallel",)))(proj, mu, dfl)


def _conv_fn(x, w, b):
    y = x * w[3:4, :] + _tshift(x, 1) * w[2:3, :] + _tshift(x, 2) * w[1:2, :] + _tshift(x, 3) * w[0:1, :] + b
    return (_silu(y),)


def _conv_specs():
    return [pl.BlockSpec((SEQ, 128), lambda j: (0, C_XBC // 128 + j)), pl.BlockSpec((4, 128), lambda j: (0, j)),
            pl.BlockSpec((1, 128), lambda j: (0, j))]


def conv_fwd(l, proj, w, b):
    return _map_fwd(f"conv_fwd{l}", _conv_fn, (6,), [proj, w, b], _conv_specs(), [_sds((SEQ, 768))],
                    [pl.BlockSpec((SEQ, 128), lambda j: (0, j))])[0]


def conv_bwd(l, proj, w, b, dxc):
    def body(x_ref, w_ref, b_ref, g_ref, dx_ref, dw_ref, db_ref):
        _, vjp = jax.vjp(_conv_fn, x_ref[...], w_ref[...], b_ref[...])
        dx, dw, db = vjp((g_ref[...],))
        dx_ref[...] = dx
        dw_ref[...] = dw
        db_ref[...] = db

    cspec = pl.BlockSpec((SEQ, 128), lambda j: (0, j))
    return pl.pallas_call(body, grid=(6,), in_specs=_conv_specs() + [cspec],
                          out_specs=[cspec, pl.BlockSpec((4, 128), lambda j: (0, j)), pl.BlockSpec((1, 128), lambda j: (0, j))],
                          out_shape=[_sds((SEQ, 768)), _sds((4, 768)), _sds((1, 768))], name=f"conv_bwd{l}",
                          compiler_params=_cp(("parallel",)))(proj, w, b, dxc)


def _rwkv_pre_fn(has_vres):
    def fn(fk, fv, flora, *rest):
        if has_vres:
            fvres, vfirst, w0, w2p, a0, a2p, g2p, k_k, k_a, v0, v2p, seg = rest
        else:
            w0, w2p, a0, a2p, g2p, k_k, k_a, seg = rest
        w_log = -_softplus(-(w0 + _dot(jnp.tanh(flora), w2p))) - 0.5
        w = jnp.exp(-jnp.exp(w_log))
        a = _sigmoid(a0 + _dot(flora, a2p))
        g = _dot(_sigmoid(flora), g2p)
        if has_vres:
            v2 = fv + (vfirst - fv) * _sigmoid(v0 + _dot(fvres, v2p))
        else:
            v2 = fv * 1.0
        kk = fk * k_k
        kk = kk / jnp.maximum(jnp.sqrt(_dot(kk * kk, seg)), 1e-12)
        k2 = fk * (1.0 + (a - 1.0) * k_a)
        return w, k2, v2, -kk, kk * a, g

    return fn


def _rwkv_pre_args(fl, vfirst, p, has_vres):
    ins = [fl, fl, fl]
    specs = [_rows(256, 1), _rows(256, 2), _rows(128, 6)]
    if has_vres:
        ins += [fl, vfirst]
        specs += [_rows(128, 7), _rows(256, 2)]
    names = ["w0", "w2p", "a0", "a2p", "g2p", "k_k", "k_a"] + (["v0", "v2p"] if has_vres else []) + ["seg64"]
    for nme in names:
        ins.append(p[nme])
        specs.append(_full(p[nme].shape))
    return ins, specs, names


def rwkv_pre_fwd(l, fl, vfirst, p):
    has_vres = l > 0
    ins, specs, _ = _rwkv_pre_args(fl, vfirst, p, has_vres)
    return _map_fwd(f"rwkv_pre_fwd{l}", _rwkv_pre_fn(has_vres), (SEQ // RB,), ins, specs,
                    [_sds((SEQ, DG))] * 6, [_rows(DG)] * 6)


def rwkv_pre_bwd(l, fl, vfirst, p, cts):
    has_vres = l > 0
    ins, specs, names = _rwkv_pre_args(fl, vfirst, p, has_vres)
    n_row = 5 if has_vres else 3
    want = list(range(n_row)) + [n_row + i for i, nme in enumerate(names) if nme != "seg64"]
    acc = tuple(w for w in want if w >= n_row)
    ct_specs = [[_rows(DG)] * len(g) for g in cts]
    gout = {0: (_sds((SEQ, DG)), _rows(DG)), 1: (_sds((SEQ, DG)), _rows(DG)), 2: (_sds((SEQ, 128)), _rows(128))}
    if has_vres:
        gout[3] = (_sds((SEQ, 128)), _rows(128))
        gout[4] = (_sds((SEQ, DG)), _rows(DG))
    gs = _map_bwd(f"rwkv_pre_bwd{l}", _rwkv_pre_fn(has_vres), (SEQ // RB,), ins, specs, cts, ct_specs, want, acc, gout)
    keys = ["fk", "fv", "flora"] + (["fvres", "vfirst"] if has_vres else []) + [nme for nme in names if nme != "seg64"]
    return dict(zip(keys, gs))


def _rwkv_post_fn(y, fr, k2, v2, g, lnx_w, lnx_b, r_k, seg):
    mu = _dot(y, seg) * (1.0 / HD)
    d = y - mu
    var = _dot(d * d, seg) * (1.0 / HD)
    yn = d * lax.rsqrt(var + GN_EPS) * lnx_w + lnx_b
    bonus = _dot(fr * k2 * r_k, seg) * v2
    return ((yn + bonus) * g,)


def _rwkv_post_args(y, fl, k2, v2, g, p):
    ins = [y, fl, k2, v2, g, p["lnx_w"], p["lnx_b"], p["r_k"], p["seg64"]]
    specs = [_rows(DG), _rows(DG, 0), _rows(DG), _rows(DG), _rows(DG)] + [_full(x.shape) for x in ins[5:]]
    return ins, specs


def rwkv_post_fwd(l, y, fl, k2, v2, g, p):
    ins, specs = _rwkv_post_args(y, fl, k2, v2, g, p)
    return _map_fwd(f"rwkv_post_fwd{l}", _rwkv_post_fn, (SEQ // RB,), ins, specs, [_sds((SEQ, DG))], [_rows(DG)])[0]


def rwkv_post_bwd(l, y, fl, k2, v2, g, p, dya):
    ins, specs = _rwkv_post_args(y, fl, k2, v2, g, p)
    gs = _map_bwd(f"rwkv_post_bwd{l}", _rwkv_post_fn, (SEQ // RB,), ins, specs, [[dya]], [[_rows(DG, 0)]],
                  want=[0, 1, 2, 3, 4, 5, 6, 7], acc=(5, 6, 7), gout={1: (_sds((SEQ, DG)), _rows(DG))})
    return dict(zip(["y", "fr", "k2", "v2", "g", "lnx_w", "lnx_b", "r_k"], gs))


SCAN_TB = 128


def _coltile8(rows8, dmask, ones_stack, parts):
    pieces, rest = [], rows8
    for q in range(parts):
        piece = rest.astype(BF16).astype(F32)
        if q < parts - 1:
            rest = rest - piece
        pieces.append((piece[:, None, :] * dmask[None]).reshape(8 * HD, DG).astype(BF16))
    x = pieces[0] if parts == 1 else jnp.concatenate(pieces, axis=1)
    return jnp.dot(x, ones_stack, preferred_element_type=F32).reshape(8, HD, DG)


def _coltiles_bf16(rows_list, dmask, ones_bf16):
    x = jnp.concatenate([(r8[:, None, :] * dmask[None]).reshape(8 * HD, DG).astype(BF16) for r8 in rows_list], axis=0)
    t = jnp.dot(x, ones_bf16, preferred_element_type=F32)
    return [t[q * 8 * HD:(q + 1) * 8 * HD].reshape(8, HD, DG) for q in range(len(rows_list))]


def _segrows8(x8, dmask, ones_bf16):
    t = jnp.dot(x8.reshape(8 * HD, DG).astype(BF16), ones_bf16, preferred_element_type=F32).reshape(8, HD, DG)
    return jnp.sum(t * dmask[None], axis=1)


def rwkv_scan_fwd(l, fl, w, k2, v2, c, b, p, gather=()):
    nblk = SEQ // SCAN_TB
    ng = len(gather)

    def body(*refs):
        r_ref, w_ref, k_ref, v_ref, c_ref, b_ref, ones_ref, dm_ref = refs[:8]
        y_ref, st_ref = refs[8 + ng:10 + ng]
        s_sc = refs[10 + 2 * ng]
        if ng:
            begin, middle, end = _gather_steps(refs[8:8 + ng], refs[10 + ng:10 + 2 * ng], *refs[11 + 2 * ng:])

            @pl.when(pl.program_id(0) == 0)
            def _():
                begin()

            @pl.when(pl.program_id(0) == (3 * nblk) // 4)
            def _():
                middle()

        @pl.when(pl.program_id(0) == 0)
        def _():
            s_sc[...] = jnp.zeros_like(s_sc)

        ones3, ones = ones_ref[...], ones_ref[0:DG, :]
        dmask = dm_ref[...]

        def group(gi, carry):
            t0 = pl.multiple_of(gi * 8, 8)
            sl = pl.ds(t0, 8)
            v8 = v_ref[sl, :]
            wt = _coltile8(w_ref[sl, :], dmask, ones3, 3)
            ct, bt, kt, rt = _coltiles_bf16([c_ref[sl, :], b_ref[sl, :], k_ref[sl, :], r_ref[sl, :]], dmask, ones)
            t = s_sc[...]
            for j in range(8):
                sa = jnp.sum(t * ct[j], axis=0, keepdims=True)
                t = t * wt[j] + bt[j] * sa + kt[j] * v8[j:j + 1, :]
                st_ref[t0 + j] = t
            s_sc[...] = t
            y_ref[sl, :] = jnp.sum(st_ref[sl] * rt, axis=1)
            return carry

        lax.fori_loop(0, SCAN_TB // 8, group, 0, unroll=4)

        if ng:
            @pl.when(pl.program_id(0) == nblk - 1)
            def _():
                end()

    row = pl.BlockSpec((SCAN_TB, DG), lambda i: (i, 0))
    ins = [fl, w, k2, v2, c, b, p["seg64x3_bf16"], p["dmask"]] + list(gather)
    specs = [row] * 6 + [_full((3 * DG, DG)), _full((HD, DG))] + [ANY] * ng
    outs = pl.pallas_call(body, grid=(nblk,), in_specs=specs,
                          out_specs=[row, pl.BlockSpec((SCAN_TB, HD, DG), lambda i: (i, 0, 0))] + [ANY] * ng,
                          out_shape=[_sds((SEQ, DG)), _sds((SEQ, HD, DG))] + _gather_shapes(gather),
                          scratch_shapes=[pltpu.VMEM((HD, DG), F32)] + (_gather_sems(ng) if ng else []),
                          name=f"rwkv_scan_fwd{l}", compiler_params=_cp(("arbitrary",)))(*ins)
    return outs[0], outs[1], list(outs[2:])


def rwkv_scan_bwd(l, fl, w, k2, v2, c, b, states, dy, p, exchange=()):
    nblk = SEQ // SCAN_TB
    nx = len(exchange)

    def body(*refs):
        r_ref, w_ref, k_ref, v_ref, c_ref, b_ref, dy_ref, st_ref, sp_ref, ones_ref, dm_ref = refs[:11]
        dr_ref, dw_ref, dk_ref, dv_ref, dc_ref, db_ref = refs[11 + nx:17 + nx]
        g_sc, prev_sc, d8_sc, dsa_sc = refs[17 + 2 * nx:21 + 2 * nx]
        i = pl.program_id(0)
        if nx:
            begin, end = _chip_exchange_steps(refs[11:11 + nx], refs[17 + nx:17 + 2 * nx], *refs[21 + 2 * nx:])

            @pl.when(i == 0)
            def _():
                begin()

        @pl.when(i == 0)
        def _():
            g_sc[...] = jnp.zeros_like(g_sc)

        ones3, ones = ones_ref[...], ones_ref[0:DG, :]
        dmask = dm_ref[...]
        first_block = i == nblk - 1

        def group(gr, carry):
            gi = SCAN_TB // 8 - 1 - gr
            t0 = pl.multiple_of(gi * 8, 8)
            sl = pl.ds(t0, 8)
            v8, dy8 = v_ref[sl, :], dy_ref[sl, :]
            t8 = st_ref[sl]
            @pl.when(gi > 0)
            def _():
                prev_sc[0] = st_ref[t0 - 1]

            @pl.when(gi == 0)
            def _():
                prev_sc[0] = jnp.where(first_block, 0.0, sp_ref[0])

            for j in range(1, 8):
                prev_sc[j] = t8[j - 1]
            tp8 = prev_sc[...]
            wt = _coltile8(w_ref[sl, :], dmask, ones3, 3)
            ct, bt, kt, rt = _coltiles_bf16([c_ref[sl, :], b_ref[sl, :], k_ref[sl, :], r_ref[sl, :]], dmask, ones)
            sa8 = jnp.sum(tp8 * ct, axis=1)
            g = g_sc[...]
            for j in range(7, -1, -1):
                g = g + rt[j] * dy8[j:j + 1, :]
                d8_sc[j] = g
                dsa = jnp.sum(g * bt[j], axis=0, keepdims=True)
                dsa_sc[j:j + 1, :] = dsa
                g = g * wt[j] + ct[j] * dsa
            g_sc[...] = g
            d8 = d8_sc[...]
            dsa8 = dsa_sc[...]
            dv_ref[sl, :] = jnp.sum(d8 * kt, axis=1)
            dr_ref[sl, :] = _segrows8(t8 * dy8[:, None, :], dmask, ones)
            dk_ref[sl, :] = _segrows8(d8 * v8[:, None, :], dmask, ones)
            dw_ref[sl, :] = _segrows8(tp8 * d8, dmask, ones)
            db_ref[sl, :] = _segrows8(d8 * sa8[:, None, :], dmask, ones)
            dc_ref[sl, :] = _segrows8(tp8 * dsa8[:, None, :], dmask, ones)
            return carry

        lax.fori_loop(0, SCAN_TB // 8, group, 0, unroll=4)

        if nx:
            @pl.when(i == nblk - 1)
            def _():
                end()

    row = pl.BlockSpec((SCAN_TB, DG), lambda i: (nblk - 1 - i, 0))
    st_spec = pl.BlockSpec((SCAN_TB, HD, DG), lambda i: (nblk - 1 - i, 0, 0))
    sp_spec = pl.BlockSpec((1, HD, DG), lambda i: (jnp.maximum((nblk - 1 - i) * SCAN_TB - 1, 0), 0, 0))
    ins = [fl, w, k2, v2, c, b, dy, states, states, p["seg64x3_bf16"], p["dmask"]] + list(exchange)
    specs = [row] * 7 + [st_spec, sp_spec, _full((3 * DG, DG)), _full((HD, DG))] + [ANY] * nx
    tile8 = pltpu.VMEM((8, HD, DG), F32)
    sems = [pltpu.SemaphoreType.DMA((nx, 3)), pltpu.SemaphoreType.DMA((nx, 3))] if nx else []
    outs = pl.pallas_call(body, grid=(nblk,), in_specs=specs, out_specs=[row] * 6 + [ANY] * nx,
                          out_shape=[_sds((SEQ, DG))] * 6 + [_sds(a.shape, a.dtype) for a in exchange],
                          scratch_shapes=[pltpu.VMEM((HD, DG), F32), tile8, tile8, pltpu.VMEM((8, DG), F32)] + sems,
                          name=f"rwkv_scan_bwd{l}", compiler_params=_cp(("arbitrary",)))(*ins)
    return outs[:6], list(outs[6:])


HG_ROWS = 256


HG_NC = HG_ROWS // HGRN_CHUNK


def _hgrn_block_fn(layer):
    def fn(hq, hf, hi, hg, sprev, lb0, lb1, norm_w, seg, bd, tri_bd, ones_bd, first_row, causal):
        e0 = jnp.exp(lb0 - jnp.maximum(lb0, lb1))
        e1 = jnp.exp(lb1 - jnp.maximum(lb0, lb1))
        sm0, sm1 = e0 / (e0 + e1), e1 / (e0 + e1)
        lb = (sm0 - sm0) if layer == 0 else ((sm0 + sm1) - sm0)
        forget = lb + (1.0 - lb) * _sigmoid(hf)
        logf = jnp.log(forget)
        kk = 1.0 - forget
        q = _silu(hq)
        c, nc = HGRN_CHUNK, HG_NC
        b = _dotx(tri_bd, logf)
        bl = _dotx(ones_bd, logf)
        split = lambda t: t.reshape(nc, c, DG)
        b4 = split(b)
        diff = (b4[:, :, None, :] - b4[:, None, :, :]).reshape(nc * c * c, DG)
        dec = jnp.exp(jnp.where(causal > 0.5, diff, -1e30))
        qrep = jnp.broadcast_to(split(q)[:, :, None, :], (nc, c, c, DG)).reshape(nc * c * c, DG)
        ktil = jnp.broadcast_to(split(kk)[:, None, :, :], (nc, c, c, DG)).reshape(nc * c * c, DG)
        vtil = jnp.broadcast_to(split(hi)[:, None, :, :], (nc, c, c, DG)).reshape(nc * c * c, DG)
        att = _dot(qrep * ktil * dec, seg)
        o_intra = jnp.sum((att * vtil).reshape(nc * c, c, DG), axis=1)
        kd4 = split(kk * jnp.exp(bl - b))
        qe4 = split(q * jnp.exp(b))
        v4 = split(hi)
        tot = jnp.exp(_dotx(first_row, bl))
        s, o_inter = sprev, []
        for ci in range(nc):
            o_inter.append(_dot_nt(qe4[ci], s))
            s = s * tot[ci:ci + 1, :] + _dot_tn(v4[ci], kd4[ci]) * bd
        o = o_intra + jnp.concatenate(o_inter, axis=0)
        ms = _dot(o * o, seg) * (1.0 / HD)
        y = o * lax.rsqrt(ms + RMS_EPS) * norm_w * _silu(hg)
        return y, s

    return fn


def _hgrn_consts(p):
    return [p["seg64"], p["seg64"], p["tri_chunks"], p["ones_chunks"], p["first_row"], p["causal_blk"]]


def hgrn_fwd(l, proj, p):
    fn = _hgrn_block_fn(l)

    def body(hq_ref, hf_ref, hi_ref, hg_ref, *rest):
        const_refs, (y_ref, st_ref, s_sc) = rest[:-3], rest[-3:]

        @pl.when(pl.program_id(0) == 0)
        def _():
            s_sc[...] = jnp.zeros_like(s_sc)

        sprev = s_sc[...]
        st_ref[0] = sprev
        y, snext = fn(hq_ref[...], hf_ref[...], hi_ref[...], hg_ref[...], sprev, *[r[...] for r in const_refs])
        y_ref[...] = y
        s_sc[...] = snext

    rows = lambda cb: pl.BlockSpec((HG_ROWS, DG), lambda i: (i, cb))
    ins = [proj, proj, proj, proj, p["lb0"], p["lb1"], p["hgrn_norm_w"]] + _hgrn_consts(p)
    specs = [rows(C_HQ // DG), rows(C_HF // DG), rows(C_HI // DG), rows(C_HG // DG)] + [_full(x.shape) for x in ins[4:]]
    return pl.pallas_call(body, grid=(SEQ // HG_ROWS,), in_specs=specs,
                          out_specs=[rows(0), pl.BlockSpec((1, DG, DG), lambda i: (i, 0, 0))],
                          out_shape=[_sds((SEQ, DG)), _sds((SEQ // HG_ROWS, DG, DG))],
                          scratch_shapes=[pltpu.VMEM((DG, DG), F32)], name=f"hgrn_fwd{l}",
                          compiler_params=_cp(("arbitrary",)))(*ins)


def hgrn_bwd(l, proj, states, dy, p, sibling=(), dy_col=0):
    fn = _hgrn_block_fn(l)
    nblk = SEQ // HG_ROWS
    n_const = len(_hgrn_consts(p))
    ns = len(sibling)

    def body(hq_ref, hf_ref, hi_ref, hg_ref, st_ref, dy_ref, lb0_ref, lb1_ref, nw_ref, *rest):
        const_refs, rest = rest[:n_const], rest[n_const:]
        dp_ref, dlb0_ref, dlb1_ref, dnw_ref = rest[ns:ns + 4]
        ds_sc = rest[2 * ns + 4]
        if ns:
            begin, end = _sibling_steps(rest[:ns], rest[ns + 4:2 * ns + 4], *rest[2 * ns + 5:])

            @pl.when(pl.program_id(0) == 0)
            def _():
                begin()

        @pl.when(pl.program_id(0) == 0)
        def _():
            ds_sc[...] = jnp.zeros_like(ds_sc)
            dlb0_ref[...] = jnp.zeros_like(dlb0_ref)
            dlb1_ref[...] = jnp.zeros_like(dlb1_ref)
            dnw_ref[...] = jnp.zeros_like(dnw_ref)

        consts = [r[...] for r in const_refs]
        f = lambda hq, hf, hi, hg, sp, b0, b1, nw: fn(hq, hf, hi, hg, sp, b0, b1, nw, *consts)
        _, vjp = jax.vjp(f, hq_ref[...], hf_ref[...], hi_ref[...], hg_ref[...], st_ref[0], lb0_ref[...], lb1_ref[...],
                         nw_ref[...])
        dhq, dhf, dhi, dhg, dsp, dlb0, dlb1, dnw = vjp((dy_ref[...], ds_sc[...]))
        dp_ref[:, 0:DG] = dhq
        dp_ref[:, DG:2 * DG] = dhf
        dp_ref[:, 2 * DG:3 * DG] = dhi
        dp_ref[:, 3 * DG:4 * DG] = dhg
        ds_sc[...] = dsp
        dlb0_ref[...] += dlb0
        dlb1_ref[...] += dlb1
        dnw_ref[...] += dnw

        if ns:
            @pl.when(pl.program_id(0) == nblk - 1)
            def _():
                end()

    rows = lambda cb: pl.BlockSpec((HG_ROWS, DG), lambda i: (nblk - 1 - i, cb))
    ins = [proj, proj, proj, proj, states, dy, p["lb0"], p["lb1"], p["hgrn_norm_w"]] + _hgrn_consts(p)
    specs = [rows(C_HQ // DG), rows(C_HF // DG), rows(C_HI // DG), rows(C_HG // DG),
             pl.BlockSpec((1, DG, DG), lambda i: (nblk - 1 - i, 0, 0)), rows(dy_col)] + [_full(x.shape) for x in ins[6:]]
    sem = pltpu.SemaphoreType.DMA((max(ns, 1), 4))
    outs = pl.pallas_call(body, grid=(nblk,), in_specs=specs + [ANY] * ns,
                          out_specs=[pl.BlockSpec((HG_ROWS, 4 * DG), lambda i: (nblk - 1 - i, 0)), _full((1, DG)),
                                     _full((1, DG)), _full((1, DG))] + [ANY] * ns,
                          out_shape=[_sds((SEQ, 4 * DG)), _sds((1, DG)), _sds((1, DG)), _sds((1, DG))]
                          + [_sds((4,) + a.shape[1:], a.dtype) for a in sibling],
                          scratch_shapes=[pltpu.VMEM((DG, DG), F32)] + ([sem, sem] if ns else []), name=f"hgrn_bwd{l}",
                          compiler_params=_cp(("arbitrary",)))(*ins, *sibling)
    return outs[:4], list(outs[4:])


def _ssd_chunk_fn(z, xs, bm, cm, dtr, sprev, dt_bias, a_log, d_par, norm_w, e128, tri, trit, seg128, ones128):
    lc = SSD_CHUNK
    dt = _softplus(dtr + dt_bias)
    a = -jnp.exp(a_log)
    da = dt * a * (lax.broadcasted_iota(jnp.int32, (1, 128), 1) < NH).astype(F32)
    cs = _dotx(tri, da)
    cst = _dotx_tn(da, trit)
    cs_b = _dotx(cs, e128)
    dt_b = _dotx(dt, e128)
    csl_b = _dotx(jnp.sum(da, axis=0, keepdims=True), e128)
    xdt = xs * dt_b
    lane = lax.broadcasted_iota(jnp.int32, (1, DG), 1)
    rowi = lax.broadcasted_iota(jnp.int32, (lc, lc), 0)
    coli = lax.broadcasted_iota(jnp.int32, (lc, lc), 1)
    y = jnp.zeros((lc, DG), F32)
    snew = jnp.zeros((DG, SSD_N), F32)
    d_b = jnp.zeros((1, DG), F32)
    wdec = xdt * jnp.exp(csl_b - cs_b)
    for g in range(2):
        bg = bm[:, g * SSD_N:(g + 1) * SSD_N]
        cg = cm[:, g * SSD_N:(g + 1) * SSD_N]
        gmat = _dot_nt(cg, bg)
        gmask = ((lane // 128) == g).astype(F32)
        snew = snew + _dot_tn(wdec * gmask, bg)
        y = y + _dot_nt(cg, sprev) * gmask * jnp.exp(cs_b)
        for hh in range(2):
            h = 2 * g + hh
            seg = jnp.where(rowi >= coli, cs[:, h:h + 1] - cst[h:h + 1, :], -1e30)
            hmask = ((lane // HD) == h).astype(F32)
            y = y + _dot(gmat * jnp.exp(seg), xdt * hmask)
            d_b = d_b + d_par[:, h:h + 1] * hmask
    cd = jnp.exp(_dotx_tn(_dotx(da, e128), ones128))
    snext = sprev * cd + snew
    y = y + xs * d_b
    y = y * _silu(z)
    ms = _dot(y * y, seg128) * (1.0 / 128.0)
    return y * lax.rsqrt(ms + RMS_EPS) * norm_w, snext


def ssd_fwd(l, proj, xc, p):
    nc = SEQ // SSD_CHUNK

    def body(z_ref, xs_ref, b_ref, c_ref, dt_ref, dtb_ref, al_ref, d_ref, nw_ref, e_ref, tri_ref, trit_ref, sg_ref,
             on_ref, y_ref, st_ref, s_sc):
        @pl.when(pl.program_id(0) == 0)
        def _():
            s_sc[...] = jnp.zeros_like(s_sc)

        sprev = s_sc[...]
        st_ref[0] = sprev
        y, snext = _ssd_chunk_fn(z_ref[...], xs_ref[...], b_ref[...], c_ref[...], dt_ref[...], sprev, dtb_ref[...],
                                 al_ref[...], d_ref[...], nw_ref[...], e_ref[...], tri_ref[...], trit_ref[...],
                                 sg_ref[...], on_ref[...])
        y_ref[...] = y
        s_sc[...] = snext

    rw = lambda w, cb: pl.BlockSpec((SSD_CHUNK, w), lambda i: (i, cb))
    ins = [proj, xc, xc, xc, proj, p["dt_bias"], p["a_log"], p["ssd_d"], p["ssd_norm_w"], p["e128"], p["tri128"],
           p["tri128t"], p["seg128"], p["ones128"]]
    specs = [rw(DG, C_Z // DG), rw(DG, 0), rw(DG, 1), rw(DG, 2), rw(128, C_DT // 128)] + [_full(x.shape) for x in ins[5:]]
    return pl.pallas_call(body, grid=(nc,), in_specs=specs,
                          out_specs=[rw(DG, 0), pl.BlockSpec((1, DG, SSD_N), lambda i: (i, 0, 0))],
                          out_shape=[_sds((SEQ, DG)), _sds((nc, DG, SSD_N))],
                          scratch_shapes=[pltpu.VMEM((DG, SSD_N), F32)], name=f"ssd_fwd{l}",
                          compiler_params=_cp(("arbitrary",)))(*ins)


def ssd_bwd(l, proj, xc, states, dy, p, dy_col=0):
    nc = SEQ // SSD_CHUNK

    def body(z_ref, xs_ref, b_ref, c_ref, dt_ref, st_ref, dy_ref, dtb_ref, al_ref, d_ref, nw_ref, e_ref, tri_ref,
             trit_ref, sg_ref, on_ref, dz_ref, dxc_ref, ddt_ref, ddtb_ref, dal_ref, dd_ref, dnw_ref, ds_sc):
        @pl.when(pl.program_id(0) == 0)
        def _():
            ds_sc[...] = jnp.zeros_like(ds_sc)
            ddtb_ref[...] = jnp.zeros_like(ddtb_ref)
            dal_ref[...] = jnp.zeros_like(dal_ref)
            dd_ref[...] = jnp.zeros_like(dd_ref)
            dnw_ref[...] = jnp.zeros_like(dnw_ref)

        consts = (e_ref[...], tri_ref[...], trit_ref[...], sg_ref[...], on_ref[...])
        f = lambda *a: _ssd_chunk_fn(*a, *consts)
        _, vjp = jax.vjp(f, z_ref[...], xs_ref[...], b_ref[...], c_ref[...], dt_ref[...], st_ref[0], dtb_ref[...],
                         al_ref[...], d_ref[...], nw_ref[...])
        dz, dxs, db, dc, ddt, dsp, ddtb, dal, dd, dnw = vjp((dy_ref[...], ds_sc[...]))
        dz_ref[...] = dz
        dxc_ref[:, 0:DG] = dxs
        dxc_ref[:, DG:2 * DG] = db
        dxc_ref[:, 2 * DG:3 * DG] = dc
        ddt_ref[...] = ddt
        ds_sc[...] = dsp
        ddtb_ref[...] += ddtb
        dal_ref[...] += dal
        dd_ref[...] += dd
        dnw_ref[...] += dnw

    rw = lambda w, cb: pl.BlockSpec((SSD_CHUNK, w), lambda i: (nc - 1 - i, cb))
    ins = [proj, xc, xc, xc, proj, states, dy, p["dt_bias"], p["a_log"], p["ssd_d"], p["ssd_norm_w"], p["e128"],
           p["tri128"], p["tri128t"], p["seg128"], p["ones128"]]
    specs = [rw(DG, C_Z // DG), rw(DG, 0), rw(DG, 1), rw(DG, 2), rw(128, C_DT // 128),
             pl.BlockSpec((1, DG, SSD_N), lambda i: (nc - 1 - i, 0, 0)), rw(DG, dy_col)] + [_full(x.shape) for x in ins[7:]]
    return pl.pallas_call(body, grid=(nc,), in_specs=specs,
                          out_specs=[rw(DG, 0), rw(3 * DG, 0), rw(128, 0), _full((1, 128)), _full((1, 128)), _full((1, 128)),
                                     _full((1, DG))],
                          out_shape=[_sds((SEQ, DG)), _sds((SEQ, 3 * DG)), _sds((SEQ, 128)), _sds((1, 128)), _sds((1, 128)),
                                     _sds((1, 128)), _sds((1, DG))],
                          scratch_shapes=[pltpu.VMEM((DG, SSD_N), F32)], name=f"ssd_bwd{l}",
                          compiler_params=_cp(("arbitrary",)))(*ins)


ATT_BLK = 128


def _att_geometry(dil):
    i = lax.broadcasted_iota(jnp.int32, (ATT_BLK, ATT_BLK), 0)
    j = lax.broadcasted_iota(jnp.int32, (ATT_BLK, ATT_BLK), 1)
    return ((i - j) * dil).astype(F32), ((ATT_BLK + i - j) * dil).astype(F32), j <= i, j >= i


def _att_scores(qn, kc, kp, h, geom, has_prev):
    dist_c, dist_p, m_c, m_pj = geom
    slope = 2.0 ** (-8.0 * (h + 1) / NH)
    scale = HD ** -0.5
    s_c = _dot_nt(qn, kc) * scale - slope * dist_c
    s_p = _dot_nt(qn, kp) * scale - slope * dist_p
    m_p = jnp.logical_and(m_pj, has_prev)
    return jnp.where(m_c, s_c, -1e30), jnp.where(m_p, s_p, -1e30), m_c, m_p


def _sub_spec(ln, width, col):
    return pl.BlockSpec((ln, DG), lambda z: (0, z * (width // DG) + col // DG))


QKV_W = 3 * DG


def attn_branch_fwd(l, bi, qkv, dil):
    ln = SEQ // dil
    nb = ln // ATT_BLK

    def body(q_ref, k_ref, v_ref, o_ref, l_ref):
        geom = _att_geometry(dil)

        def blk(n, carry):
            r0 = pl.multiple_of(n * ATT_BLK, ATT_BLK)
            rp = pl.multiple_of(jnp.maximum(n - 1, 0) * ATT_BLK, ATT_BLK)
            cur, prv = pl.ds(r0, ATT_BLK), pl.ds(rp, ATT_BLK)
            for h in range(NH):
                hs = slice(h * HD, (h + 1) * HD)
                qn, kc, vc, kp, vp = q_ref[cur, hs], k_ref[cur, hs], v_ref[cur, hs], k_ref[prv, hs], v_ref[prv, hs]
                s_c, s_p, m_c, m_p = _att_scores(qn, kc, kp, h, geom, n > 0)
                m = jnp.maximum(jnp.max(s_c, axis=1, keepdims=True), jnp.max(s_p, axis=1, keepdims=True))
                p_c = jnp.where(m_c, jnp.exp(s_c - m), 0.0)
                p_p = jnp.where(m_p, jnp.exp(s_p - m), 0.0)
                den = jnp.sum(p_c, axis=1, keepdims=True) + jnp.sum(p_p, axis=1, keepdims=True)
                o_ref[cur, hs] = (_dot(p_c, vc) + _dot(p_p, vp)) / den
                l_ref[cur, hs] = jnp.broadcast_to(m + jnp.log(den), (ATT_BLK, HD))
            return carry

        lax.fori_loop(0, nb, blk, 0, unroll=2 if nb > 1 else 1)

    pv = qkv.reshape(ln, dil * QKV_W)
    out = pl.BlockSpec((ln, DG), lambda z: (0, z))
    o, lse = pl.pallas_call(body, grid=(dil,), in_specs=[_sub_spec(ln, QKV_W, 0), _sub_spec(ln, QKV_W, DG), _sub_spec(ln, QKV_W, 2 * DG)],
                            out_specs=[out, out], out_shape=[_sds((ln, dil * DG))] * 2, name=f"attn_fwd{l}_{bi}",
                            compiler_params=_cp(("parallel",)))(pv, pv, pv)
    return o.reshape(SEQ, DG), lse.reshape(SEQ, DG)


def attn_branch_bwd(l, bi, qkv, dil, dyb, lse_all, delta):
    ln = SEQ // dil
    nb = ln // ATT_BLK
    scale = HD ** -0.5

    def body(q_ref, k_ref, v_ref, do_ref, l_ref, dl_ref, dq_ref, dk_ref, dv_ref):
        dk_ref[...] = jnp.zeros_like(dk_ref)
        dv_ref[...] = jnp.zeros_like(dv_ref)
        geom = _att_geometry(dil)

        def blk(n, carry):
            r0 = pl.multiple_of(n * ATT_BLK, ATT_BLK)
            rp = pl.multiple_of(jnp.maximum(n - 1, 0) * ATT_BLK, ATT_BLK)
            cur, prv = pl.ds(r0, ATT_BLK), pl.ds(rp, ATT_BLK)
            for h in range(NH):
                hs = slice(h * HD, (h + 1) * HD)
                qn, don = q_ref[cur, hs], do_ref[cur, hs]
                lse, dlt = l_ref[cur, h * HD:h * HD + 1], dl_ref[cur, h * HD:h * HD + 1]
                kc, vc, kp, vp = k_ref[cur, hs], v_ref[cur, hs], k_ref[prv, hs], v_ref[prv, hs]
                s_c, s_p, m_c, m_p = _att_scores(qn, kc, kp, h, geom, n > 0)
                p_c = jnp.where(m_c, jnp.exp(s_c - lse), 0.0)
                p_p = jnp.where(m_p, jnp.exp(s_p - lse), 0.0)
                ds_c = p_c * (_dot_nt(don, vc) - dlt)
                ds_p = p_p * (_dot_nt(don, vp) - dlt)
                dq_ref[cur, hs] = (_dot(ds_c, kc) + _dot(ds_p, kp)) * scale
                dv_ref[prv, hs] += _dot_tn(p_p, don)
                dk_ref[prv, hs] += _dot_tn(ds_p, qn) * scale
                dv_ref[cur, hs] += _dot_tn(p_c, don)
                dk_ref[cur, hs] += _dot_tn(ds_c, qn) * scale
            return carry

        lax.fori_loop(0, nb, blk, 0, unroll=2 if nb > 1 else 1)

    pv = qkv.reshape(ln, dil * QKV_W)
    sub = lambda t: t.reshape(ln, dil * DG)
    row = pl.BlockSpec((ln, DG), lambda z: (0, z))
    outs = pl.pallas_call(body, grid=(dil,),
                          in_specs=[_sub_spec(ln, QKV_W, 0), _sub_spec(ln, QKV_W, DG), _sub_spec(ln, QKV_W, 2 * DG), row, row, row],
                          out_specs=[row] * 3, out_shape=[_sds((ln, dil * DG))] * 3, name=f"attn_bwd{l}_{bi}",
                          compiler_params=_cp(("parallel",)))(pv, pv, pv, sub(dyb), sub(lse_all), sub(delta))
    return [t.reshape(SEQ, DG) for t in outs]


def _attn_merge_fn(o1, o2, o3, l1, l2, l3):
    m = jnp.maximum(jnp.maximum(l1, l2), l3)
    w1, w2, w3 = jnp.exp(l1 - m), jnp.exp(l2 - m), jnp.exp(l3 - m)
    den = w1 + w2 + w3
    return (w1 * o1 + w2 * o2 + w3 * o3) / den, m + jnp.log(den)


def attn_merge(l, os_, ls_):
    ins = list(os_) + list(ls_)
    return _map_fwd(f"attn_merge{l}", _attn_merge_fn, (SEQ // RB,), ins, [_rows(DG)] * 6, [_sds((SEQ, DG))] * 2,
                    [_rows(DG)] * 2)


def attn_delta(l, dyb, yb, seg):
    fn = lambda d, y, s: (_dot(d * y, s),)
    return _map_fwd(f"attn_delta{l}", fn, (SEQ // RB,), [dyb, yb, seg], [_rows(DG), _rows(DG), _full((DG, DG))],
                    [_sds((SEQ, DG))], [_rows(DG)])[0]


def _ln_fn(x, mix, w, b):
    h = ALPHA * x + mix
    mu = jnp.mean(h, axis=-1, keepdims=True)
    d = h - mu
    var = jnp.mean(d * d, axis=-1, keepdims=True)
    return (d * lax.rsqrt(var + LN_EPS) * w + b,)


def ln_fwd(name, x, mix, w, b):
    specs = [_rows(D_MODEL), _rows(D_MODEL), _full((1, D_MODEL)), _full((1, D_MODEL))]
    return _map_fwd(name, _ln_fn, (SEQ // RB,), [x, mix, w, b], specs, [_sds((SEQ, D_MODEL))], [_rows(D_MODEL)])[0]


def ln_bwd(name, x, mix, w, b, dy):
    specs = [_rows(D_MODEL), _rows(D_MODEL), _full((1, D_MODEL)), _full((1, D_MODEL))]
    return _map_bwd(name, _ln_fn, (SEQ // RB,), [x, mix, w, b], specs, [[dy]], [[_rows(D_MODEL)]], want=[1, 2, 3],
                    acc=(2, 3))


def loss_call(y, tgt):
    def fn(yy, tt):
        e = yy - tt
        part = 0.5 * jnp.sum(jnp.sum(e * e, axis=-1, keepdims=True) * (1.0 / D_MODEL), axis=0, keepdims=True)
        return e * (1.0 / D_MODEL), jnp.broadcast_to(part, (8, 128))

    return _map_fwd("loss", fn, (SEQ // RB,), [y, tgt], [_rows(D_MODEL)] * 2,
                    [_sds((SEQ, D_MODEL)), _sds((SEQ // RB * 8, 128))],
                    [_rows(D_MODEL), pl.BlockSpec((8, 128), lambda i: (i, 0))])


LATE_KEYS = ("w_out", "w_up_t", "w_down")


def _full_rows(g):
    return g.reshape(N_DEV * g.shape[1], g.shape[2])


def layer_fwd(l, x, vfirst, wts, p, gather=(), late=False):
    sv = {"x": x}
    proj = _mm(f"mm_in{l}", x, wts["w_in"], "nn", *MM_TILES["k1024"])
    fl = lerp_fwd(l, proj, p["mu"])
    xc = conv_fwd(l, proj, p["conv_w"], p["conv_b"])
    w, k2, v2, c, b, g = rwkv_pre_fwd(l, fl, vfirst, p)
    y_scan, states, sv["gathered"] = rwkv_scan_fwd(l, fl, w, k2, v2, c, b, p, gather)
    if late:
        wts = dict(wts, **dict(zip(LATE_KEYS, [_full_rows(g) for g in sv["gathered"][:3]])))
    sv["wts"] = wts
    ya = rwkv_post_fwd(l, y_scan, fl, k2, v2, g, p)
    qkv = proj[:, C_AQ:C_AQ + 3 * DG]
    outs, lses = [], []
    for bi, (win, dil) in enumerate(DILATED):
        o, lse = attn_branch_fwd(l, bi, qkv, dil)
        outs.append(o)
        lses.append(lse)
    yb, lse_all = attn_merge(l, outs, lses)
    yc, ssd_states = ssd_fwd(l, proj, xc, p)
    yd, hg_states = hgrn_fwd(l, proj, p)
    ycat = jnp.concatenate([ya, yb, yc, yd], axis=1).astype(BF16)
    mix = _mm(f"mm_out{l}", ycat, wts["w_out"], "nn", *MM_TILES["k1024"])
    x1 = ln_fwd(f"ln1_fwd{l}", x, mix, p["ln1_w"], p["ln1_b"])
    hh = _mm(f"mm_up{l}", x1, wts["w_up_t"], "nt", *MM_TILES["k1024"], epilogue="relu2")
    m2 = _mm(f"mm_down{l}", hh, wts["w_down"], "nn", *MM_TILES["k4096"])
    x2 = ln_fwd(f"ln2_fwd{l}", x1, m2, p["ln2_w"], p["ln2_b"])
    sv.update(proj=proj, fl=fl, xc=xc, w=w, k2=k2, v2=v2, c=c, b=b, g=g, y_scan=y_scan, states=states,
              yb=yb, lse_all=lse_all, ssd_states=ssd_states, hg_states=hg_states, ycat=ycat, mix=mix, x1=x1, hh=hh, qkv=qkv,
              m2=m2, vfirst=vfirst)
    return x2, sv


def layer_bwd(l, dx2, dvfirst_next, sv, wts, p, exchange=(), reducer=None):
    gr = {}
    x, x1, proj, fl = sv["x"], sv["x1"], sv["proj"], sv["fl"]
    dres2, gr["ln2_w"], gr["ln2_b"] = ln_bwd(f"ln2_bwd{l}", x1, sv["m2"], p["ln2_w"], p["ln2_b"], dx2)
    du = _mm(f"mm_down_dx{l}", dres2, wts["w_down"], "nt", *MM_TILES["k1024"], add=sv["hh"], epilogue="relu2_bwd",
             out_dtype=BF16)
    gr["w_down"] = _mm(f"mm_down_dw{l}", sv["hh"], dres2, "tn", *MM_TILES["wgrad_tall"])
    dx1 = _mm(f"mm_up_dx{l}", du, wts["w_up_t"], "nn", *MM_TILES["k4096"], add=dres2, add_scale=ALPHA)
    gr["w_up_t"] = _mm(f"mm_up_dw{l}", du, x1, "tn", *MM_TILES["wgrad_tall"])
    dres1, gr["ln1_w"], gr["ln1_b"] = ln_bwd(f"ln1_bwd{l}", x, sv["mix"], p["ln1_w"], p["ln1_b"], dx1)
    dycat = _mm(f"mm_out_dx{l}", dres1, wts["w_out"], "nt", *MM_TILES["k1024"])
    gr["w_out"] = _mm(f"mm_out_dw{l}", sv["ycat"], dres1, "tn", 1024, 1024, 512)
    dyb = dycat[:, DG:2 * DG]
    send = [_owner_blocks(gr[k]) for k in LATE_KEYS] if reducer else []
    (dhg4, gr["lb0"], gr["lb1"], gr["hgrn_norm_w"]), sib = hgrn_bwd(l, proj, sv["hg_states"], dycat, p, send, dy_col=3)
    if reducer:
        gr["early_own"], early_parts = reducer(f"{l}a", send, sib)
        exchange = list(exchange) + list(early_parts)
    dz, dxc, ddt, gr["dt_bias"], gr["a_log"], gr["ssd_d"], gr["ssd_norm_w"] = ssd_bwd(l, proj, sv["xc"], sv["ssd_states"], dycat, p, dy_col=2)
    dxbc, gr["conv_w"], gr["conv_b"] = conv_bwd(l, proj, p["conv_w"], p["conv_b"], dxc)
    delta = attn_delta(l, dyb, sv["yb"], p["seg64"])
    dqs, dks, dvs = [], [], []
    for bi, (win, dil) in enumerate(DILATED):
        dq, dk, dv = attn_branch_bwd(l, bi, sv["qkv"], dil, dyb, sv["lse_all"], delta)
        dqs.append(dq)
        dks.append(dk)
        dvs.append(dv)
    dq_a, dk_a, dv_a = _addn(f"attn_dq{l}", *dqs), _addn(f"attn_dk{l}", *dks), _addn(f"attn_dv{l}", *dvs)
    pg = rwkv_post_bwd(l, sv["y_scan"], fl, sv["k2"], sv["v2"], sv["g"], p, dycat)
    gr["lnx_w"], gr["lnx_b"], gr["r_k"] = pg["lnx_w"], pg["lnx_b"], pg["r_k"]
    (dr, dw, dk, dv, dc, db), gr["exchanged"] = rwkv_scan_bwd(l, fl, sv["w"], sv["k2"], sv["v2"], sv["c"], sv["b"],
                                                              sv["states"], pg["y"], p, exchange)
    v2_cts = [dv, pg["v2"]] + ([dvfirst_next] if dvfirst_next is not None else [])
    qg = rwkv_pre_bwd(l, fl, sv["vfirst"], p, [[dw], [dk, pg["k2"]], v2_cts, [dc], [db], [pg["g"]]])
    for nme in ("w0", "w2p", "a0", "a2p", "g2p", "k_k", "k_a", "v0", "v2p"):
        if nme in qg:
            gr[nme] = qg[nme]
    dfr = _addn(f"rwkv_dr{l}", dr, pg["fr"])
    dvres = qg["fvres"] if l > 0 else jnp.zeros((SEQ, 128), F32)
    dfl_out = jnp.concatenate([dfr, qg["fk"], qg["fv"], qg["flora"], dvres], axis=1)
    dfl_in, gr["mu"] = lerp_bwd(l, proj, p["mu"], dfl_out)
    dproj = jnp.concatenate([dfl_in[:, 0:768], dq_a, dk_a, dv_a, dz, dxbc, dhg4, dfl_in[:, 768:896], ddt,
                             dfl_in[:, 896:1024], jnp.zeros((SEQ, 128), F32)], axis=1).astype(BF16)
    dx = _mm(f"mm_in_dx{l}", dproj, wts["w_in"], "nt", *MM_TILES["k4096"], add=dres1, add_scale=ALPHA)
    gr["w_in"] = _mm(f"mm_in_dw{l}", x, dproj, "tn", *MM_TILES["wgrad_wide"])
    return dx, (qg["vfirst"] if l > 0 else None), gr


def _w_in_pad(w_in_l, w_vres):
    rows = w_in_l.shape[0]
    z = lambda n: jnp.zeros((rows, n), w_in_l.dtype)
    vres = z(128) if w_vres is None else jnp.concatenate([w_vres, z(96)], axis=1)
    return jnp.concatenate([w_in_l[:, 0:768], w_in_l[:, 896:1664], w_in_l[:, 1664:1920], w_in_l[:, 1920:2688],
                            w_in_l[:, 2692:3716], w_in_l[:, 768:896], w_in_l[:, 2688:2692], z(124), vres, z(128)], axis=1)


def _w_in_unpad(g):
    g_in = jnp.concatenate([g[:, 0:768], g[:, C_LORA:C_LORA + 128], g[:, 768:1536], g[:, C_Z:C_Z + 256],
                            g[:, C_XBC:C_XBC + 768], g[:, C_DT:C_DT + 4], g[:, C_HQ:C_HQ + 1024]], axis=1)
    return g_in, g[:, C_VRES:C_VRES + 32]


def _consts():
    pair = jnp.arange(HG_NC * HGRN_CHUNK * HGRN_CHUNK)
    i128 = jnp.arange(128)
    ihg = jnp.arange(HG_ROWS)
    same_chunk = (ihg[:, None] // HGRN_CHUNK) == (ihg[None, :] // HGRN_CHUNK)
    seg64 = _seg_ones(DG, HD)
    tri128 = (i128[:, None] >= i128[None, :]).astype(F32)
    return dict(
        seg64=seg64, seg64x3_bf16=jnp.concatenate([seg64, seg64, seg64], axis=0).astype(BF16),
        dmask=(jnp.arange(HD)[:, None] == (jnp.arange(DG)[None, :] % HD)).astype(F32),
        tri_chunks=(same_chunk & (ihg[:, None] >= ihg[None, :])).astype(F32), ones_chunks=same_chunk.astype(F32),
        first_row=(ihg[None, :] == (jnp.arange(HG_NC) * HGRN_CHUNK)[:, None]).astype(F32),
        causal_blk=jnp.broadcast_to((((pair // HGRN_CHUNK) % HGRN_CHUNK) >= (pair % HGRN_CHUNK)).astype(F32)[:, None],
                                    (HG_NC * HGRN_CHUNK * HGRN_CHUNK, DG)),
        e128=((i128[:, None] == (jnp.arange(DG)[None, :] // HD)) & (i128[:, None] < NH)).astype(F32),
        tri128=tri128, tri128t=tri128.T, seg128=_seg_ones(DG, 128), ones128=jnp.ones((128, 128), F32))


def _pad_lanes(v, n):
    return jnp.concatenate([v, jnp.zeros((n - v.shape[0],), v.dtype)])[None, :]


def _layer_params(l, raw, consts):
    p = dict(consts)
    row = lambda name: raw[name][l][None, :]
    z = lambda r: jnp.zeros((r, DG), F32)
    mu_vres = raw["mu_vres"][l - 1] if l > 0 else jnp.zeros((32,), F32)
    p["mu"] = jnp.concatenate([raw["mu_shift"][l], mu_vres, jnp.zeros((96,), F32)])[None, :]
    p["conv_w"], p["conv_b"] = raw["ssd_conv_w"][l], row("ssd_conv_b")
    p["w0"], p["a0"], p["k_k"], p["k_a"] = row("rwkv_w0"), row("rwkv_a0"), row("rwkv_k_k"), row("rwkv_k_a")
    p["lnx_w"], p["lnx_b"] = row("rwkv_lnx_w"), row("rwkv_lnx_b")
    p["r_k"] = raw["rwkv_r_k"][l].reshape(1, DG)
    p["w2p"] = jnp.concatenate([raw["rwkv_w2"][l], z(96)], axis=0)
    p["a2p"] = jnp.concatenate([z(32), raw["rwkv_a2"][l], z(64)], axis=0)
    p["g2p"] = jnp.concatenate([z(64), raw["rwkv_g2"][l]], axis=0)
    if l > 0:
        p["v0"] = raw["rwkv_v0"][l - 1][None, :]
        p["v2p"] = jnp.concatenate([raw["rwkv_v2"][l - 1], z(96)], axis=0)
    p["lb0"], p["lb1"] = raw["lower_bounds"][0:1], raw["lower_bounds"][1:2]
    p["hgrn_norm_w"], p["ssd_norm_w"] = row("hgrn_norm_w"), row("ssd_norm_w")
    p["dt_bias"], p["a_log"], p["ssd_d"] = (_pad_lanes(raw[n][l], 128) for n in ("ssd_dt_bias", "ssd_A_log", "ssd_D"))
    for n in ("ln1_w", "ln1_b", "ln2_w", "ln2_b"):
        p[n] = row(n)
    return p


def _natural_grads(g0, g1):
    gs = (g0, g1)
    st = lambda key, f=lambda a: a[0]: jnp.stack([f(g[key]) for g in gs])
    out = {}
    out["lower_bounds"] = jnp.concatenate([g0["lb0"] + g1["lb0"], g0["lb1"] + g1["lb1"]], axis=0)
    out["mu_shift"] = st("mu", lambda a: a[0, :896])
    out["mu_vres"] = g1["mu"][:, 896:928]
    out["rwkv_w0"], out["rwkv_a0"], out["rwkv_k_k"], out["rwkv_k_a"] = st("w0"), st("a0"), st("k_k"), st("k_a")
    out["rwkv_w2"] = st("w2p", lambda a: a[0:32])
    out["rwkv_a2"] = st("a2p", lambda a: a[32:64])
    out["rwkv_g2"] = st("g2p", lambda a: a[64:128])
    out["rwkv_r_k"] = st("r_k", lambda a: a.reshape(NH, HD))
    out["rwkv_lnx_w"], out["rwkv_lnx_b"] = st("lnx_w"), st("lnx_b")
    out["rwkv_v0"] = g1["v0"]
    out["rwkv_v2"] = g1["v2p"][None, 0:32]
    out["ssd_conv_w"] = st("conv_w", lambda a: a)
    out["ssd_conv_b"] = st("conv_b")
    out["ssd_dt_bias"], out["ssd_A_log"], out["ssd_D"] = (st(k, lambda a: a[0, :NH]) for k in ("dt_bias", "a_log", "ssd_d"))
    out["ssd_norm_w"], out["hgrn_norm_w"] = st("ssd_norm_w"), st("hgrn_norm_w")
    for n in ("ln1_w", "ln1_b", "ln2_w", "ln2_b"):
        out[n] = st(n)
    return out


MESH_T = pl.DeviceIdType.MESH
ANY = pl.BlockSpec(memory_space=pl.ANY)


def _dev_index(px, py, pc):
    return 4 * px + 2 * py + pc


def all_gather(arrs):
    n = len(arrs)

    def body(*refs):
        begin, middle, end = _gather_steps(refs[:n], refs[n:2 * n], *refs[2 * n:])
        begin()
        middle()
        end()

    return pl.pallas_call(body, in_specs=[ANY] * n, out_specs=[ANY] * n, out_shape=_gather_shapes(arrs),
                          scratch_shapes=_gather_sems(n), name="all_gather")(*arrs)


def _gather_shapes(arrs):
    return [_sds((N_DEV,) + a.shape, a.dtype) for a in arrs]


def _gather_sems(n):
    return [pltpu.SemaphoreType.DMA((n, 7)), pltpu.SemaphoreType.DMA((n, 7)), pltpu.SemaphoreType.DMA((n,))]


def _gather_steps(ins, outs, send_sems, recv_sems, local_sems):
    n = len(ins)
    x, y, c = lax.axis_index("x"), lax.axis_index("y"), lax.axis_index("c")
    me, sibling = (x, y, c), (x, y, 1 - c)
    chips = [(1 - x, y), (x, 1 - y), (1 - x, 1 - y)]

    def copy(a, k, block, to, src=None):
        slot = outs[a].at[_dev_index(*block)]
        return pltpu.make_async_remote_copy(src_ref=slot if src is None else src, dst_ref=slot,
                                            send_sem=send_sems.at[a, k], recv_sem=recv_sems.at[a, k],
                                            device_id=to, device_id_type=MESH_T)

    def own_copies():
        mine = [pltpu.make_async_copy(ins[a], outs[a].at[_dev_index(*me)], local_sems.at[a]) for a in range(n)]
        first = []
        for a in range(n):
            first.append(copy(a, 0, me, sibling, src=ins[a]))
            first += [copy(a, 1 + j, me, (*chip, c), src=ins[a]) for j, chip in enumerate(chips)]
        return mine, first

    def begin():
        mine, first = own_copies()
        for cp in mine + first:
            cp.start()

    def passed_on():
        return [copy(a, 4 + j, (*chip, c), sibling) for j, chip in enumerate(chips) for a in range(n)]

    def middle():
        for j, chip in enumerate(chips):
            for a in range(n):
                copy(a, 1 + j, (*chip, c), me).wait_recv()
        for cp in passed_on():
            cp.start()

    def end():
        mine, first = own_copies()
        for a in range(n):
            copy(a, 0, sibling, me).wait_recv()
            for j, chip in enumerate(chips):
                copy(a, 4 + j, (*chip, 1 - c), me).wait_recv()
        for cp in first + passed_on():
            cp.wait_send()
        for cp in mine:
            cp.wait()

    return begin, middle, end


def _chips(x, y):
    return [(x, y), (1 - x, y), (x, 1 - y), (1 - x, 1 - y)]


def _sibling_steps(ins, sib, send_sems, recv_sems):
    x, y, c = lax.axis_index("x"), lax.axis_index("y"), lax.axis_index("c")

    def copies():
        return [pltpu.make_async_remote_copy(src_ref=ins[a].at[_dev_index(cx, cy, 1 - c)], dst_ref=sib[a].at[k],
                                             send_sem=send_sems.at[a, k], recv_sem=recv_sems.at[a, k],
                                             device_id=(x, y, 1 - c), device_id_type=MESH_T)
                for a in range(len(ins)) for k, (cx, cy) in enumerate(_chips(x, y))]

    def begin():
        for cp in copies():
            cp.start()

    def end():
        cps = copies()
        for cp in cps:
            cp.wait_recv()
        for cp in cps:
            cp.wait_send()

    return begin, end


def exchange_siblings(arrs, name):
    n = len(arrs)

    def body(*refs):
        begin, end = _sibling_steps(refs[:n], refs[n:2 * n], *refs[2 * n:])
        begin()
        end()

    sem = pltpu.SemaphoreType.DMA((n, 4))
    return pl.pallas_call(body, in_specs=[ANY] * n, out_specs=[ANY] * n,
                          out_shape=[_sds((4,) + a.shape[1:], a.dtype) for a in arrs],
                          scratch_shapes=[sem, sem], name=name)(*arrs)


def reduce_pair(name, send, slots, sib, wire_dtype):
    _, r, c = send.shape
    rb = min(r, 262144 // c)

    def body(slots_ref, m0, m1, m2, m3, s_ref, own_ref, part_ref):
        own_ref[...] = m0[...] + s_ref[0]
        for k, m_ref in enumerate((m1, m2, m3)):
            part_ref[k] = (m_ref[...] + s_ref[k + 1]).astype(wire_dtype)

    mine = [pl.BlockSpec((None, rb, c), lambda i, s, k=k: (s[k], i, 0)) for k in range(4)]
    grid_spec = pltpu.PrefetchScalarGridSpec(
        num_scalar_prefetch=1, grid=(r // rb,),
        in_specs=mine + [pl.BlockSpec((4, rb, c), lambda i, s: (0, i, 0))],
        out_specs=[pl.BlockSpec((rb, c), lambda i, s: (i, 0)), pl.BlockSpec((3, rb, c), lambda i, s: (0, i, 0))])
    return pl.pallas_call(body, grid_spec=grid_spec, out_shape=[_sds((r, c)), _sds((3, r, c), wire_dtype)], name=name,
                          compiler_params=_cp(("parallel",)))(slots, send, send, send, send, sib)


def _chip_exchange_steps(ins, recv, send_sems, recv_sems):
    x, y, c = lax.axis_index("x"), lax.axis_index("y"), lax.axis_index("c")

    def copies():
        return [pltpu.make_async_remote_copy(src_ref=ins[a].at[k], dst_ref=recv[a].at[k], send_sem=send_sems.at[a, k],
                                             recv_sem=recv_sems.at[a, k], device_id=(cx, cy, c), device_id_type=MESH_T)
                for a in range(len(ins)) for k, (cx, cy) in enumerate(_chips(x, y)[1:])]

    def begin():
        for cp in copies():
            cp.start()

    def end():
        cps = copies()
        for cp in cps:
            cp.wait_recv()
        for cp in cps:
            cp.wait_send()

    return begin, end


def exchange_chips(parts, rep):
    n = len(parts)

    def body(*refs):
        ins, rep_ref = refs[:n], refs[n]
        recv, rep_all = refs[n + 1:2 * n + 1], refs[2 * n + 1]
        send_sems, recv_sems, rsend_sems, rrecv_sems, local_sem = refs[2 * n + 2:]
        x, y, c = lax.axis_index("x"), lax.axis_index("y"), lax.axis_index("c")
        me = _dev_index(x, y, c)
        mine = pltpu.make_async_copy(rep_ref, rep_all.at[me], local_sem)
        mine.start()
        begin, end = _chip_exchange_steps(ins, recv, send_sems, recv_sems)
        begin()
        rels = [(rx, ry, rc) for rx in (0, 1) for ry in (0, 1) for rc in (0, 1)][1:]
        peers = [(jnp.where(rx, 1 - x, x), jnp.where(ry, 1 - y, y), jnp.where(rc, 1 - c, c)) for rx, ry, rc in rels]
        rcps = []
        for k, peer in enumerate(peers):
            cp = pltpu.make_async_remote_copy(src_ref=rep_ref, dst_ref=rep_all.at[me], send_sem=rsend_sems.at[k],
                                              recv_sem=rrecv_sems.at[k], device_id=peer, device_id_type=MESH_T)
            cp.start()
            rcps.append(cp)
        for k, peer in enumerate(peers):
            pltpu.make_async_remote_copy(src_ref=rep_ref, dst_ref=rep_all.at[_dev_index(*peer)], send_sem=rsend_sems.at[k],
                                         recv_sem=rrecv_sems.at[k], device_id=peer, device_id_type=MESH_T).wait_recv()
        end()
        for cp in rcps:
            cp.wait_send()
        mine.wait()

    outs = pl.pallas_call(
        body, in_specs=[ANY] * (n + 1), out_specs=[ANY] * (n + 1),
        out_shape=[_sds(a.shape, a.dtype) for a in parts] + [_sds((N_DEV,) + rep.shape, rep.dtype)],
        scratch_shapes=[pltpu.SemaphoreType.DMA((n, 3)), pltpu.SemaphoreType.DMA((n, 3)), pltpu.SemaphoreType.DMA((7,)),
                        pltpu.SemaphoreType.DMA((7,)), pltpu.SemaphoreType.DMA],
        name="exchange_chips")(*parts, rep)
    return outs[:n], outs[n]


def adamw(name, terms, w, m, v, transposed=False):
    r, c = w.shape[::-1] if transposed else w.shape
    rb = r if transposed else min(r, 262144 // c)
    c1 = 1.0 - ADAM_B1 ** ADAM_STEP
    c2 = 1.0 - ADAM_B2 ** ADAM_STEP
    nt = len(terms)

    def body(*refs):
        w_ref, m_ref, v_ref = refs[nt:nt + 3]
        g_ref, d_ref, nm_ref, nv_ref = refs[nt + 3:]
        g = refs[0][...].astype(F32)
        for t_ref in refs[1:nt]:
            g = g + t_ref[...].astype(F32)
        if transposed:
            g = g.T
        nm = ADAM_B1 * m_ref[...] + (1.0 - ADAM_B1) * g
        nv = ADAM_B2 * v_ref[...] + (1.0 - ADAM_B2) * (g * g)
        g_ref[...] = g
        nm_ref[...] = nm
        nv_ref[...] = nv
        d_ref[...] = -ADAM_LR * ((nm / c1) / (jnp.sqrt(nv / c2) + ADAM_EPS) + ADAM_WD * w_ref[...])

    blk = pl.BlockSpec((rb, c), lambda i: (i, 0))
    wblk = pl.BlockSpec((c, r), lambda i: (0, 0)) if transposed else blk
    tspecs = [blk if k is None else pl.BlockSpec((None, rb, c), lambda i, k=k: (k, i, 0)) for _, k in terms]
    return pl.pallas_call(body, grid=(r // rb,), in_specs=tspecs + [wblk] * 3, out_specs=[wblk] * 4,
                          out_shape=[_sds(w.shape)] * 4, name=name,
                          compiler_params=_cp(("parallel",)))(*[t for t, _ in terms], w, m, v)


W_IN_PIECES = ((0, 768, 0), (768, 896, C_LORA), (896, 1664, 768), (1664, 1920, C_Z), (1920, 2688, C_XBC),
               (2688, 2692, C_DT), (2692, 3716, C_HQ))
VRES_W = 32


def adamw_w_in(name, terms, w, m, v, vres=None):
    nt, nv = len(terms), 3 if vres else 0
    c1 = 1.0 - ADAM_B1 ** ADAM_STEP
    c2 = 1.0 - ADAM_B2 ** ADAM_STEP

    def body(*refs):
        w_ref, m_ref, v_ref = refs[nt:nt + 3]
        vres_refs = refs[nt + 3:nt + 3 + nv]
        outs = refs[nt + 3 + nv:nt + 7 + nv]
        vres_outs = refs[nt + 7 + nv:]
        g_all = refs[0][...].astype(F32)
        for t_ref in refs[1:nt]:
            g_all = g_all + t_ref[...].astype(F32)

        def update(g, wmv, out_refs, cols):
            nm = ADAM_B1 * wmv[1][:, cols] + (1.0 - ADAM_B1) * g
            nv_ = ADAM_B2 * wmv[2][:, cols] + (1.0 - ADAM_B2) * (g * g)
            out_refs[0][:, cols] = g
            out_refs[1][:, cols] = -ADAM_LR * ((nm / c1) / (jnp.sqrt(nv_ / c2) + ADAM_EPS) + ADAM_WD * wmv[0][:, cols])
            out_refs[2][:, cols] = nm
            out_refs[3][:, cols] = nv_

        for lo, hi, src in W_IN_PIECES:
            update(g_all[:, src:src + hi - lo], (w_ref, m_ref, v_ref), outs, slice(lo, hi))
        if vres:
            update(g_all[:, C_VRES:C_VRES + VRES_W], vres_refs, vres_outs, slice(0, VRES_W))

    r, c = terms[0][0].shape[-2:]
    tspecs = [_full((r, c)) if k is None else pl.BlockSpec((None, r, c), lambda i, k=k: (k, 0, 0)) for _, k in terms]
    wspec, vspec = _full(w.shape), _full((w.shape[0], VRES_W))
    outs = pl.pallas_call(body, grid=(1,), in_specs=tspecs + [wspec] * 3 + [vspec] * nv,
                          out_specs=[wspec] * 4 + [vspec] * (4 if vres else 0),
                          out_shape=[_sds(w.shape)] * 4 + [_sds((w.shape[0], VRES_W))] * (4 if vres else 0), name=name,
                          compiler_params=_cp(("arbitrary",)))(*[t for t, _ in terms], w, m, v, *(vres or ()))
    return list(outs[:4]), list(outs[4:])


SMS_ROWS = 16
N_BIG = 8
SMALL_SHARDED = (("rwkv_w2", (2, 32, 32)), ("rwkv_a2", (2, 32, 32)), ("rwkv_g2", (2, 64, 32)), ("rwkv_v2", (1, 32, 32)),
                 ("ssd_conv_w", (2, 4, 96)))
REPLICATED = (("lower_bounds", (2, 256)), ("mu_shift", (2, 896)), ("mu_vres", (1, 32)), ("rwkv_w0", (2, 256)),
              ("rwkv_a0", (2, 256)), ("rwkv_k_k", (2, 256)), ("rwkv_k_a", (2, 256)), ("rwkv_r_k", (2, 4, 64)),
              ("rwkv_lnx_w", (2, 256)), ("rwkv_lnx_b", (2, 256)), ("rwkv_v0", (1, 256)), ("ssd_conv_b", (2, 768)),
              ("ssd_dt_bias", (2, 4)), ("ssd_A_log", (2, 4)), ("ssd_D", (2, 4)), ("ssd_norm_w", (2, 256)),
              ("hgrn_norm_w", (2, 256)), ("ln1_w", (2, 1024)), ("ln1_b", (2, 1024)), ("ln2_w", (2, 1024)),
              ("ln2_b", (2, 1024)))


def _flat_rows(parts, rows):
    flat = jnp.concatenate([a.reshape(-1) for a in parts])
    return jnp.concatenate([flat, jnp.zeros((rows * PACK_W - flat.shape[0],), flat.dtype)]).reshape(rows, PACK_W)


def _local_arrays(d):
    return [_w_in_pad(d["w_in"][0], None), _w_in_pad(d["w_in"][1], d["w_in_vres"][0]), d["w_out"][0], d["w_out"][1],
            d["w_up"][0], d["w_up"][1], d["w_down"][0], d["w_down"][1],
            _flat_rows([d[n] for n, _ in SMALL_SHARDED], SMS_ROWS)]


def _unflat(rows2d, table):
    flat, out, o = rows2d.reshape(-1), {}, 0
    for name, shape in table:
        n = 1
        for s in shape:
            n *= s
        out[name] = flat[o:o + n].reshape(shape)
        o += n
    return out


def _from_local_arrays(arrs, rep, w_in_vres):
    d = dict(rep)
    d["w_in"], d["w_in_vres"] = jnp.stack([arrs[0], arrs[1]]), w_in_vres[None]
    d["w_out"] = jnp.stack([arrs[2], arrs[3]])
    d["w_up"] = jnp.stack([arrs[4], arrs[5]])
    d["w_down"] = jnp.stack([arrs[6], arrs[7]])
    d.update(_unflat(arrs[8], SMALL_SHARDED))
    return d


def _small_sharded_full(gs):
    small, flat, o = {}, gs.reshape(N_DEV, -1), 0
    for name, shape in SMALL_SHARDED:
        n = shape[0] * shape[1] * shape[2]
        blk = flat[:, o:o + n].reshape((N_DEV,) + shape)
        small[name] = blk.transpose(1, 2, 0, 3).reshape(shape[0], shape[1], N_DEV * shape[2])
        o += n
    return small


def _owner_blocks(g):
    return g.reshape(N_DEV, g.shape[0] // N_DEV, g.shape[1])


def _as_rows(shape):
    width = 1
    for s in shape[1:]:
        width *= s
    return shape[0], width


REP_2D = tuple((name, _as_rows(shape)) for name, shape in REPLICATED)
REP_ROW0 = tuple(sum(a for _, (a, _) in REP_2D[:i]) for i in range(len(REP_2D)))
REP_ROWS = sum(a for _, (a, _) in REP_2D)


def _rep_rows(d):
    rows = []
    for name, (a, b) in REP_2D:
        v = d[name].reshape(a, b)
        rows.append(v if b == PACK_W else jnp.concatenate([v, jnp.zeros((a, PACK_W - b), F32)], axis=1))
    return jnp.concatenate(rows, axis=0)


def adamw_replicated(rep_all, w, m, v):
    names = [name for name, _ in REP_2D]
    n = len(names)
    c1 = 1.0 - ADAM_B1 ** ADAM_STEP
    c2 = 1.0 - ADAM_B2 ** ADAM_STEP

    def body(*refs):
        rep_ref, w_refs, m_refs, v_refs, outs = refs[0], refs[1:1 + n], refs[1 + n:1 + 2 * n], refs[1 + 2 * n:1 + 3 * n], refs[1 + 3 * n:]
        for i, (_, (a, b)) in enumerate(REP_2D):
            r0 = REP_ROW0[i]
            g = rep_ref[0, r0:r0 + a, 0:b]
            for q in range(1, N_DEV):
                g = g + rep_ref[q, r0:r0 + a, 0:b]
            nm = ADAM_B1 * m_refs[i][...] + (1.0 - ADAM_B1) * g
            nv = ADAM_B2 * v_refs[i][...] + (1.0 - ADAM_B2) * (g * g)
            outs[i][...] = g
            outs[n + i][...] = -ADAM_LR * ((nm / c1) / (jnp.sqrt(nv / c2) + ADAM_EPS) + ADAM_WD * w_refs[i][...])
            outs[2 * n + i][...] = nm
            outs[3 * n + i][...] = nv

    flat = lambda d: [d[name].reshape(ab) for name, ab in REP_2D]
    pspecs = [_full(ab) for _, ab in REP_2D]
    res = pl.pallas_call(body, grid=(1,), in_specs=[_full(rep_all.shape)] + pspecs * 3, out_specs=pspecs * 4,
                         out_shape=[_sds(ab) for _, ab in REP_2D] * 4, name="adamw_replicated",
                         compiler_params=_cp(("arbitrary",)))(rep_all, *flat(w), *flat(m), *flat(v))
    shapes = dict(REPLICATED)
    return [{name: res[k * n + i].reshape(shapes[name]) for i, name in enumerate(names)} for k in range(4)]


def _small_send_arrays(small_grads):
    sms = []
    for name, shape in SMALL_SHARDED:
        g = small_grads[name].reshape(shape[0], shape[1], N_DEV, shape[2]).transpose(2, 0, 1, 3)
        sms.append(g.reshape(N_DEV, -1))
    sms = jnp.concatenate(sms, axis=1)
    sms = jnp.concatenate([sms, jnp.zeros((N_DEV, SMS_ROWS * PACK_W - sms.shape[1]), F32)], axis=1)
    return sms.reshape(N_DEV, SMS_ROWS, PACK_W), _rep_rows(small_grads)


BIG_KEYS = ("w_in", "w_out", "w_up_t", "w_down")


def _local_step(x, tgt, wts, raw, gather=(), pair_sums=None, reducer=None):
    consts = _consts()
    ps = [_layer_params(l, raw, consts) for l in range(DEPTH)]
    x1, sv0 = layer_fwd(0, x, None, wts[0], ps[0], gather[:4], late=bool(gather))
    wts1 = {"w_in": _full_rows(sv0["gathered"][3])} if gather else wts[1]
    x2, sv1 = layer_fwd(1, x1, sv0["fl"], wts1, ps[1], gather[4:], late=bool(gather))
    dy, lparts = loss_call(x2, tgt)
    loss = jnp.sum(lparts[::8, 0])
    dx1, dvfirst, g1 = layer_bwd(1, dy, None, sv1, sv1["wts"], ps[1], (), reducer)
    big1 = {k: g1[k] for k in BIG_KEYS}
    if reducer is None:
        dx0, _, g0 = layer_bwd(0, dx1, dvfirst, sv0, sv0["wts"], ps[0])
        early = None
    else:
        own_in1, parts_in1 = pair_sums("1b", {"w_in": g1["w_in"]})
        dx0, _, g0 = layer_bwd(0, dx1, dvfirst, sv0, sv0["wts"], ps[0], parts_in1, reducer)
        own, recv = {(1, "w_in"): own_in1[0]}, {(1, "w_in"): g0["exchanged"][0]}
        for l, g, first in ((1, g1, 0), (0, g0, 1)):
            for i, k in enumerate(LATE_KEYS):
                own[(l, k)], recv[(l, k)] = g["early_own"][i], g["exchanged"][first + i]
        early = (own, recv)
    big = [{k: g0[k] for k in BIG_KEYS}, big1]
    return loss, dx0, big, _natural_grads(g0, g1), early


WEIGHT_NAMES = ("lower_bounds", "w_in", "w_in_vres", "mu_shift", "mu_vres", "rwkv_w0", "rwkv_w2", "rwkv_a0", "rwkv_a2",
                "rwkv_g2", "rwkv_k_k", "rwkv_k_a", "rwkv_r_k", "rwkv_lnx_w", "rwkv_lnx_b", "rwkv_v0", "rwkv_v2",
                "ssd_conv_w", "ssd_conv_b", "ssd_dt_bias", "ssd_A_log", "ssd_D", "ssd_norm_w", "hgrn_norm_w", "w_out",
                "ln1_w", "ln1_b", "w_up", "w_down", "ln2_w", "ln2_b")


def kernel(x, lower_bounds, w_in, w_in_vres, mu_shift, mu_vres, rwkv_w0, rwkv_w2, rwkv_a0, rwkv_a2, rwkv_g2, rwkv_k_k, rwkv_k_a, rwkv_r_k, rwkv_lnx_w, rwkv_lnx_b, rwkv_v0, rwkv_v2, ssd_conv_w, ssd_conv_b, ssd_dt_bias, ssd_A_log, ssd_D, ssd_norm_w, hgrn_norm_w, w_out, ln1_w, ln1_b, w_up, w_down, ln2_w, ln2_b, loss_target, m_lower_bounds, m_w_in, m_w_in_vres, m_mu_shift, m_mu_vres, m_rwkv_w0, m_rwkv_w2, m_rwkv_a0, m_rwkv_a2, m_rwkv_g2, m_rwkv_k_k, m_rwkv_k_a, m_rwkv_r_k, m_rwkv_lnx_w, m_rwkv_lnx_b, m_rwkv_v0, m_rwkv_v2, m_ssd_conv_w, m_ssd_conv_b, m_ssd_dt_bias, m_ssd_A_log, m_ssd_D, m_ssd_norm_w, m_hgrn_norm_w, m_w_out, m_ln1_w, m_ln1_b, m_w_up, m_w_down, m_ln2_w, m_ln2_b, v_lower_bounds, v_w_in, v_w_in_vres, v_mu_shift, v_mu_vres, v_rwkv_w0, v_rwkv_w2, v_rwkv_a0, v_rwkv_a2, v_rwkv_g2, v_rwkv_k_k, v_rwkv_k_a, v_rwkv_r_k, v_rwkv_lnx_w, v_rwkv_lnx_b, v_rwkv_v0, v_rwkv_v2, v_ssd_conv_w, v_ssd_conv_b, v_ssd_dt_bias, v_ssd_A_log, v_ssd_D, v_ssd_norm_w, v_hgrn_norm_w, v_w_out, v_ln1_w, v_ln1_b, v_w_up, v_w_down, v_ln2_w, v_ln2_b):
    given = dict(locals())
    w = {n: given[n] for n in WEIGHT_NAMES}
    m_all, v_all = ({n: given[pre + n] for n in WEIGHT_NAMES} for pre in ("m_", "v_"))
    w_arrs, m_arrs, v_arrs = _local_arrays(w), _local_arrays(m_all), _local_arrays(v_all)
    wire = lambda a: (w_arrs[a].T if a in (4, 5) else w_arrs[a]).astype(BF16)
    gathered0 = all_gather([wire(0), w_arrs[N_BIG]])
    raw = {n: w[n] for n, _ in REPLICATED}
    raw.update(_small_sharded_full(gathered0[1]))
    mx, my, mc = lax.axis_index("x"), lax.axis_index("y"), lax.axis_index("c")
    slots = jnp.stack([_dev_index(cx, cy, mc) for cx, cy in _chips(mx, my)]).astype(jnp.int32)

    def reducer(tag, send, sib, n_f32=0):
        wire_dt = [BF16] * (len(send) - n_f32) + [F32] * n_f32
        res = [reduce_pair(f"reduce_pair{tag}_{i}", s, slots, sb, dt) for i, (s, sb, dt) in enumerate(zip(send, sib, wire_dt))]
        return [o for o, _ in res], [pt for _, pt in res]

    def pair_sums(tag, grads, extra=()):
        send = [_owner_blocks(g) for g in grads.values()] + list(extra)
        return reducer(tag, send, exchange_siblings(send, f"exchange_siblings{tag}"), len(extra))

    behind_scan = [wire(a) for a in (2, 4, 6, 1, 3, 5, 7)]
    loss, dx, big, small_grads, (own_by, recv_by) = _local_step(
        x[0], loss_target[0], [{"w_in": _full_rows(gathered0[0])}, None], raw, behind_scan, pair_sums, reducer)
    sms_send, rep = _small_send_arrays(small_grads)
    own0b, parts0b = pair_sums("0b", {"w_in": big[0]["w_in"]}, [sms_send])
    recv0b, rep_all = exchange_chips(parts0b, rep)
    own, recv = [None] * (N_BIG + 1), [None] * (N_BIG + 1)
    for (l, k), o in own_by.items():
        a = 2 * BIG_KEYS.index(k) + l
        own[a], recv[a] = o, recv_by[(l, k)]
    for a, o, r in zip((0, N_BIG), own0b, recv0b):
        own[a], recv[a] = o, r
    terms = lambda a: [(own[a], None), (recv[a], 0), (recv[a], 1), (recv[a], 2)]
    moments = [{n: given[pre + n] for n in ("w_in", "w_in_vres")} for pre in ("", "m_", "v_")]
    in0, _ = adamw_w_in("adamw0", terms(0), *[d["w_in"][0] for d in moments])
    in1, vres = adamw_w_in("adamw1", terms(1), *[d["w_in"][1] for d in moments], vres=[d["w_in_vres"][0] for d in moments])
    results = [in0, in1] + [adamw(f"adamw{a}", terms(a), w_arrs[a], m_arrs[a], v_arrs[a], transposed=a in (4, 5))
                            for a in range(2, N_BIG + 1)]
    rep_res = adamw_replicated(rep_all, w, m_all, v_all)
    loss = lax.psum(loss, ("x", "y", "c"))
    outs = [loss, dx[None]]
    for q in range(4):
        d = _from_local_arrays([res[q] for res in results], rep_res[q], vres[q])
        outs += [d[n] for n in WEIGHT_NAMES]
    return tuple(outs)
```
